```python
import jax, jax.numpy as jnp
from jax import lax
import numpy as np

D_MODEL = 1024
BATCH = 8
SEQ = 4096
DEPTH = 1

LRU_WIDTH = D_MODEL
LRU_HEADS = 4
LRU_HEAD_DIM = LRU_WIDTH // LRU_HEADS
CONV_WIDTH = 4
LRU_C = 8.0
SGU_WIDTH = D_MODEL
SGU_GROUPS = 4
SGU_GROUP_DIM = SGU_WIDTH // SGU_GROUPS
CHUNK = 128
D_FF = 4 * D_MODEL
NORM_EPS = 1e-6
LN_EPS = 1e-5
OFF_XA = 0
OFF_GA = OFF_XA + LRU_WIDTH
OFF_U = OFF_GA + LRU_WIDTH
OFF_V = OFF_U + SGU_WIDTH
OFF_MA = OFF_V + SGU_WIDTH
OFF_MB = OFF_MA + D_MODEL
D_IN = OFF_MB + D_MODEL

kernel_name = "hybrid_rglru_sgu_gated_block"


def rms_norm(x, g):
    xf = x.astype(jnp.float32)
    y = xf * lax.rsqrt(jnp.mean(xf * xf, axis=-1, keepdims=True) + NORM_EPS)
    return (y * g.astype(jnp.float32)).astype(x.dtype)


def layer_norm(x, g, b):
    xf = x.astype(jnp.float32)
    mu = jnp.mean(xf, axis=-1, keepdims=True)
    var = jnp.mean(jnp.square(xf - mu), axis=-1, keepdims=True)
    y = (xf - mu) * lax.rsqrt(var + LN_EPS)
    return (y * g.astype(jnp.float32) + b.astype(jnp.float32)).astype(x.dtype)


def causal_depthwise_conv(x, w, b):
    k_w = w.shape[0]
    s = x.shape[1]
    xp = jnp.pad(x, ((0, 0), (k_w - 1, 0), (0, 0)))
    out = b
    for k in range(k_w):
        out = out + xp[:, k_w - 1 - k:k_w - 1 - k + s] * w[k]
    return out


def rg_lru(x, w_r, b_r, w_i, b_i, lam):
    bsz, s, c = x.shape
    xh = x.reshape(bsz, s, LRU_HEADS, LRU_HEAD_DIM)
    r = jax.nn.sigmoid(jnp.einsum('bshi,hij->bshj', xh, w_r) + b_r).reshape(bsz, s, c)
    i = jax.nn.sigmoid(jnp.einsum('bshi,hij->bshj', xh, w_i) + b_i).reshape(bsz, s, c)
    log_a = -LRU_C * r.astype(jnp.float32) * jax.nn.softplus(-lam.astype(jnp.float32))
    a = jnp.exp(log_a)
    mult = jnp.sqrt(-jnp.expm1(2.0 * log_a))
    bx = x.astype(jnp.float32) * i.astype(jnp.float32) * mult

    def combine(left, right):
        a_l, b_l = left
        a_r, b_r2 = right
        return a_l * a_r, a_r * b_l + b_r2

    _, h = lax.associative_scan(combine, (a, bx), axis=1)
    return h.astype(x.dtype)


def chunked_spatial_gating(u, v, ln_g, ln_b, w_s, b_s):
    bsz, s, c = v.shape
    n_chunks = s // CHUNK
    v = layer_norm(v, ln_g, ln_b)
    vc = v.reshape(bsz, n_chunks, CHUNK, SGU_GROUPS, SGU_GROUP_DIM)
    mask = jnp.tril(jnp.ones((CHUNK, CHUNK), dtype=w_s.dtype))
    sp = jnp.einsum('gts,bnsgc->bntgc', w_s * mask, vc) + jnp.transpose(b_s)[:, :, None]
    return u * sp.reshape(bsz, s, c)


def _fwd_setup_inputs(seed: int = 0) -> dict:
    key = jax.random.key(seed)
    ks = jax.random.split(key, 24)
    f32 = jnp.float32
    nrm = lambda k, shape, scale: jax.random.normal(k, shape, f32) * scale
    x = jax.random.normal(ks[0], (BATCH, SEQ, D_MODEL), f32)
    norm_mix_g = 1.0 + nrm(ks[1], (DEPTH, D_MODEL), 0.02)
    w_in = nrm(ks[2], (DEPTH, D_MODEL, D_IN), D_MODEL ** -0.5)
    conv_w = nrm(ks[3], (DEPTH, CONV_WIDTH, LRU_WIDTH), CONV_WIDTH ** -0.5)
    conv_b = nrm(ks[4], (DEPTH, LRU_WIDTH), 0.01)
    w_rgate = nrm(ks[5], (DEPTH, LRU_HEADS, LRU_HEAD_DIM, LRU_HEAD_DIM), LRU_HEAD_DIM ** -0.5)
    b_rgate = nrm(ks[6], (DEPTH, LRU_HEADS, LRU_HEAD_DIM), 0.01)
    w_igate = nrm(ks[7], (DEPTH, LRU_HEADS, LRU_HEAD_DIM, LRU_HEAD_DIM), LRU_HEAD_DIM ** -0.5)
    b_igate = nrm(ks[8], (DEPTH, LRU_HEADS, LRU_HEAD_DIM), 0.01)
    a_c = jax.random.uniform(ks[9], (DEPTH, LRU_WIDTH), f32, 0.9, 0.999)
    sig = a_c ** (1.0 / LRU_C)
    lru_lambda = jnp.log(sig) - jnp.log1p(-sig)
    w_out_a = nrm(ks[10], (DEPTH, LRU_WIDTH, D_MODEL), LRU_WIDTH ** -0.5)
    sgu_ln_g = 1.0 + nrm(ks[11], (DEPTH, SGU_WIDTH), 0.02)
    sgu_ln_b = nrm(ks[12], (DEPTH, SGU_WIDTH), 0.01)
    sgu_w_s = nrm(ks[13], (DEPTH, SGU_GROUPS, CHUNK, CHUNK), CHUNK ** -0.5)
    sgu_b_s = 1.0 + nrm(ks[14], (DEPTH, SGU_GROUPS, CHUNK), 0.01)
    w_out_b = nrm(ks[15], (DEPTH, SGU_WIDTH, D_MODEL), SGU_WIDTH ** -0.5)
    w_out = nrm(ks[16], (DEPTH, D_MODEL, D_MODEL), D_MODEL ** -0.5)
    norm_mlp_g = 1.0 + nrm(ks[17], (DEPTH, D_MODEL), 0.02)
    w_up = nrm(ks[18], (DEPTH, D_MODEL, D_FF), D_MODEL ** -0.5)
    w_down = nrm(ks[19], (DEPTH, D_FF, D_MODEL), D_FF ** -0.5)
    norm_final_g = 1.0 + nrm(ks[20], (D_MODEL,), 0.02)
    return {"x": x, "norm_mix_g": norm_mix_g, "w_in": w_in, "conv_w": conv_w, "conv_b": conv_b,
            "w_rgate": w_rgate, "b_rgate": b_rgate, "w_igate": w_igate, "b_igate": b_igate,
            "lru_lambda": lru_lambda, "w_out_a": w_out_a, "sgu_ln_g": sgu_ln_g, "sgu_ln_b": sgu_ln_b,
            "sgu_w_s": sgu_w_s, "sgu_b_s": sgu_b_s, "w_out_b": w_out_b, "w_out": w_out,
            "norm_mlp_g": norm_mlp_g, "w_up": w_up, "w_down": w_down, "norm_final_g": norm_final_g}


def _fwd_reference(x, norm_mix_g, w_in, conv_w, conv_b, w_rgate, b_rgate, w_igate, b_igate,
              lru_lambda, w_out_a, sgu_ln_g, sgu_ln_b, sgu_w_s, sgu_b_s, w_out_b, w_out,
              norm_mlp_g, w_up, w_down, norm_final_g):
    h = x
    for l in range(DEPTH):
        n = rms_norm(h, norm_mix_g[l])
        z = n @ w_in[l]
        xa = z[..., OFF_XA:OFF_GA]
        ga = z[..., OFF_GA:OFF_U]
        ub = z[..., OFF_U:OFF_V]
        vb = z[..., OFF_V:OFF_MA]
        ma = z[..., OFF_MA:OFF_MB]
        mb = z[..., OFF_MB:D_IN]
        xa = causal_depthwise_conv(xa, conv_w[l], conv_b[l])
        ya = rg_lru(xa, w_rgate[l], b_rgate[l], w_igate[l], b_igate[l], lru_lambda[l]) * jax.nn.gelu(ga)
        yb = chunked_spatial_gating(jax.nn.gelu(ub), jax.nn.gelu(vb), sgu_ln_g[l], sgu_ln_b[l],
                                    sgu_w_s[l], sgu_b_s[l])
        merged = jax.nn.sigmoid(ma) * (ya @ w_out_a[l]) + jax.nn.sigmoid(mb) * (yb @ w_out_b[l])
        h = h + merged @ w_out[l]
        n2 = rms_norm(h, norm_mlp_g[l])
        h = h + jnp.square(jax.nn.relu(n2 @ w_up[l])) @ w_down[l]
    return rms_norm(h, norm_final_g)


import jax as _jax
import jax.numpy as _jnp

TWIN_FORMAT = 'train_step'
FWD_PARAMS = ['x', 'norm_mix_g', 'w_in', 'conv_w', 'conv_b', 'w_rgate', 'b_rgate', 'w_igate', 'b_igate', 'lru_lambda', 'w_out_a', 'sgu_ln_g', 'sgu_ln_b', 'sgu_w_s', 'sgu_b_s', 'w_out_b', 'w_out', 'norm_mlp_g', 'w_up', 'w_down', 'norm_final_g']
TWIN_WEIGHTS = ['norm_mix_g', 'w_in', 'conv_w', 'conv_b', 'w_rgate', 'b_rgate', 'w_igate', 'b_igate', 'lru_lambda', 'w_out_a', 'sgu_ln_g', 'sgu_ln_b', 'sgu_w_s', 'sgu_b_s', 'w_out_b', 'w_out', 'norm_mlp_g', 'w_up', 'w_down', 'norm_final_g']
TWIN_DIFF_INPUT = 'x'
TWIN_INPUTS = ['x', 'norm_mix_g', 'w_in', 'conv_w', 'conv_b', 'w_rgate', 'b_rgate', 'w_igate', 'b_igate', 'lru_lambda', 'w_out_a', 'sgu_ln_g', 'sgu_ln_b', 'sgu_w_s', 'sgu_b_s', 'w_out_b', 'w_out', 'norm_mlp_g', 'w_up', 'w_down', 'norm_final_g', 'loss_target', 'm_norm_mix_g', 'm_w_in', 'm_conv_w', 'm_conv_b', 'm_w_rgate', 'm_b_rgate', 'm_w_igate', 'm_b_igate', 'm_lru_lambda', 'm_w_out_a', 'm_sgu_ln_g', 'm_sgu_ln_b', 'm_sgu_w_s', 'm_sgu_b_s', 'm_w_out_b', 'm_w_out', 'm_norm_mlp_g', 'm_w_up', 'm_w_down', 'm_norm_final_g', 'v_norm_mix_g', 'v_w_in', 'v_conv_w', 'v_conv_b', 'v_w_rgate', 'v_b_rgate', 'v_w_igate', 'v_b_igate', 'v_lru_lambda', 'v_w_out_a', 'v_sgu_ln_g', 'v_sgu_ln_b', 'v_sgu_w_s', 'v_sgu_b_s', 'v_w_out_b', 'v_w_out', 'v_norm_mlp_g', 'v_w_up', 'v_w_down', 'v_norm_final_g']
TWIN_OUTPUTS = ['loss', 'grad_x', 'grad_norm_mix_g', 'grad_w_in', 'grad_conv_w', 'grad_conv_b', 'grad_w_rgate', 'grad_b_rgate', 'grad_w_igate', 'grad_b_igate', 'grad_lru_lambda', 'grad_w_out_a', 'grad_sgu_ln_g', 'grad_sgu_ln_b', 'grad_sgu_w_s', 'grad_sgu_b_s', 'grad_w_out_b', 'grad_w_out', 'grad_norm_mlp_g', 'grad_w_up', 'grad_w_down', 'grad_norm_final_g', 'delta_norm_mix_g', 'delta_w_in', 'delta_conv_w', 'delta_conv_b', 'delta_w_rgate', 'delta_b_rgate', 'delta_w_igate', 'delta_b_igate', 'delta_lru_lambda', 'delta_w_out_a', 'delta_sgu_ln_g', 'delta_sgu_ln_b', 'delta_sgu_w_s', 'delta_sgu_b_s', 'delta_w_out_b', 'delta_w_out', 'delta_norm_mlp_g', 'delta_w_up', 'delta_w_down', 'delta_norm_final_g', 'new_m_norm_mix_g', 'new_m_w_in', 'new_m_conv_w', 'new_m_conv_b', 'new_m_w_rgate', 'new_m_b_rgate', 'new_m_w_igate', 'new_m_b_igate', 'new_m_lru_lambda', 'new_m_w_out_a', 'new_m_sgu_ln_g', 'new_m_sgu_ln_b', 'new_m_sgu_w_s', 'new_m_sgu_b_s', 'new_m_w_out_b', 'new_m_w_out', 'new_m_norm_mlp_g', 'new_m_w_up', 'new_m_w_down', 'new_m_norm_final_g', 'new_v_norm_mix_g', 'new_v_w_in', 'new_v_conv_w', 'new_v_conv_b', 'new_v_w_rgate', 'new_v_b_rgate', 'new_v_w_igate', 'new_v_b_igate', 'new_v_lru_lambda', 'new_v_w_out_a', 'new_v_sgu_ln_g', 'new_v_sgu_ln_b', 'new_v_sgu_w_s', 'new_v_sgu_b_s', 'new_v_w_out_b', 'new_v_w_out', 'new_v_norm_mlp_g', 'new_v_w_up', 'new_v_w_down', 'new_v_norm_final_g']
TWIN_LEAF_KINDS = {'loss': 'loss', 'grad_x': 'grad_x', 'grad_norm_mix_g': 'grad_w', 'grad_w_in': 'grad_w', 'grad_conv_w': 'grad_w', 'grad_conv_b': 'grad_w', 'grad_w_rgate': 'grad_w', 'grad_b_rgate': 'grad_w', 'grad_w_igate': 'grad_w', 'grad_b_igate': 'grad_w', 'grad_lru_lambda': 'grad_w', 'grad_w_out_a': 'grad_w', 'grad_sgu_ln_g': 'grad_w', 'grad_sgu_ln_b': 'grad_w', 'grad_sgu_w_s': 'grad_w', 'grad_sgu_b_s': 'grad_w', 'grad_w_out_b': 'grad_w', 'grad_w_out': 'grad_w', 'grad_norm_mlp_g': 'grad_w', 'grad_w_up': 'grad_w', 'grad_w_down': 'grad_w', 'grad_norm_final_g': 'grad_w', 'delta_norm_mix_g': 'delta_w', 'delta_w_in': 'delta_w', 'delta_conv_w': 'delta_w', 'delta_conv_b': 'delta_w', 'delta_w_rgate': 'delta_w', 'delta_b_rgate': 'delta_w', 'delta_w_igate': 'delta_w', 'delta_b_igate': 'delta_w', 'delta_lru_lambda': 'delta_w', 'delta_w_out_a': 'delta_w', 'delta_sgu_ln_g': 'delta_w', 'delta_sgu_ln_b': 'delta_w', 'delta_sgu_w_s': 'delta_w', 'delta_sgu_b_s': 'delta_w', 'delta_w_out_b': 'delta_w', 'delta_w_out': 'delta_w', 'delta_norm_mlp_g': 'delta_w', 'delta_w_up': 'delta_w', 'delta_w_down': 'delta_w', 'delta_norm_final_g': 'delta_w', 'new_m_norm_mix_g': 'new_m', 'new_m_w_in': 'new_m', 'new_m_conv_w': 'new_m', 'new_m_conv_b': 'new_m', 'new_m_w_rgate': 'new_m', 'new_m_b_rgate': 'new_m', 'new_m_w_igate': 'new_m', 'new_m_b_igate': 'new_m', 'new_m_lru_lambda': 'new_m', 'new_m_w_out_a': 'new_m', 'new_m_sgu_ln_g': 'new_m', 'new_m_sgu_ln_b': 'new_m', 'new_m_sgu_w_s': 'new_m', 'new_m_sgu_b_s': 'new_m', 'new_m_w_out_b': 'new_m', 'new_m_w_out': 'new_m', 'new_m_norm_mlp_g': 'new_m', 'new_m_w_up': 'new_m', 'new_m_w_down': 'new_m', 'new_m_norm_final_g': 'new_m', 'new_v_norm_mix_g': 'new_v', 'new_v_w_in': 'new_v', 'new_v_conv_w': 'new_v', 'new_v_conv_b': 'new_v', 'new_v_w_rgate': 'new_v', 'new_v_b_rgate': 'new_v', 'new_v_w_igate': 'new_v', 'new_v_b_igate': 'new_v', 'new_v_lru_lambda': 'new_v', 'new_v_w_out_a': 'new_v', 'new_v_sgu_ln_g': 'new_v', 'new_v_sgu_ln_b': 'new_v', 'new_v_sgu_w_s': 'new_v', 'new_v_sgu_b_s': 'new_v', 'new_v_w_out_b': 'new_v', 'new_v_w_out': 'new_v', 'new_v_norm_mlp_g': 'new_v', 'new_v_w_up': 'new_v', 'new_v_w_down': 'new_v', 'new_v_norm_final_g': 'new_v'}


def _forward(args):
    return _fwd_reference(*[args[k] for k in FWD_PARAMS])


def _output_shape():
    def fwd():
        inp = _fwd_setup_inputs(0)
        return _fwd_reference(*[inp[k] for k in FWD_PARAMS])
    out = _jax.eval_shape(fwd)
    return out.shape, out.dtype

N_MICROBATCH = 1
ADAM_LR = 0.001
ADAM_B1 = 0.9
ADAM_B2 = 0.999
ADAM_EPS = 1e-08
ADAM_WD = 0.01
ADAM_STEP = 10
PER_EXAMPLE_BATCH_AXIS = {'x': 0, 'loss_target': 0}
SHARED_INPUTS = []
_WEIGHT_DTYPES = {'norm_mix_g': _jnp.float32, 'w_in': _jnp.float32, 'conv_w': _jnp.float32, 'conv_b': _jnp.float32, 'w_rgate': _jnp.float32, 'b_rgate': _jnp.float32, 'w_igate': _jnp.float32, 'b_igate': _jnp.float32, 'lru_lambda': _jnp.float32, 'w_out_a': _jnp.float32, 'sgu_ln_g': _jnp.float32, 'sgu_ln_b': _jnp.float32, 'sgu_w_s': _jnp.float32, 'sgu_b_s': _jnp.float32, 'w_out_b': _jnp.float32, 'w_out': _jnp.float32, 'norm_mlp_g': _jnp.float32, 'w_up': _jnp.float32, 'w_down': _jnp.float32, 'norm_final_g': _jnp.float32}
MOMENT_SCALE = {'norm_mix_g': 1.103029e-01, 'w_in': 4.565345e-02, 'conv_w': 4.205269e-02, 'conv_b': 4.758943e-01, 'w_rgate': 1.103055e-02, 'b_rgate': 9.194932e-03, 'w_igate': 1.962319e-02, 'b_igate': 1.458177e-02, 'lru_lambda': 1.912384e-02, 'w_out_a': 3.764454e-02, 'sgu_ln_g': 4.489299e-02, 'sgu_ln_b': 4.273609e-02, 'sgu_w_s': 6.195116e-02, 'sgu_b_s': 8.632839e-02, 'w_out_b': 8.115249e-02, 'w_out': 8.966606e-02, 'norm_mlp_g': 1.596707e-01, 'w_up': 7.670577e-02, 'w_down': 1.591219e-01, 'norm_final_g': 3.239905e+01}


def _to_microbatches(a, axis):
    t = _jnp.moveaxis(a, axis, 0)
    t = t.reshape((N_MICROBATCH, t.shape[0] // N_MICROBATCH) + t.shape[1:])
    return _jnp.moveaxis(t, 1, axis + 1)


def setup_inputs(seed: int = 0) -> dict:
    inp = _fwd_setup_inputs(seed)
    key = _jax.random.fold_in(_jax.random.key(seed), 7919)
    shape, _ = _output_shape()
    out = dict(inp)
    out["loss_target"] = _jax.random.normal(_jax.random.fold_in(key, 0), shape, _jnp.float32)
    for i, name in enumerate(TWIN_WEIGHTS):
        w = inp[name].astype(_jnp.float32)
        if MOMENT_SCALE is None:
            s = _jnp.sqrt(_jnp.mean(_jnp.square(w)) + 1e-30)
        else:
            s = MOMENT_SCALE[name]
        km, kv = _jax.random.split(_jax.random.fold_in(key, i + 1))
        out[name] = w
        out["m_" + name] = s * _jax.random.normal(km, w.shape, _jnp.float32)
        out["v_" + name] = (s * s) * _jax.random.uniform(kv, w.shape, _jnp.float32, 0.5, 1.5)
    if N_MICROBATCH > 1:
        for name, axis in PER_EXAMPLE_BATCH_AXIS.items():
            out[name] = _to_microbatches(out[name], axis)
    return {'x': out['x'], 'norm_mix_g': out['norm_mix_g'], 'w_in': out['w_in'], 'conv_w': out['conv_w'], 'conv_b': out['conv_b'], 'w_rgate': out['w_rgate'], 'b_rgate': out['b_rgate'], 'w_igate': out['w_igate'], 'b_igate': out['b_igate'], 'lru_lambda': out['lru_lambda'], 'w_out_a': out['w_out_a'], 'sgu_ln_g': out['sgu_ln_g'], 'sgu_ln_b': out['sgu_ln_b'], 'sgu_w_s': out['sgu_w_s'], 'sgu_b_s': out['sgu_b_s'], 'w_out_b': out['w_out_b'], 'w_out': out['w_out'], 'norm_mlp_g': out['norm_mlp_g'], 'w_up': out['w_up'], 'w_down': out['w_down'], 'norm_final_g': out['norm_final_g'], 'loss_target': out['loss_target'], 'm_norm_mix_g': out['m_norm_mix_g'], 'm_w_in': out['m_w_in'], 'm_conv_w': out['m_conv_w'], 'm_conv_b': out['m_conv_b'], 'm_w_rgate': out['m_w_rgate'], 'm_b_rgate': out['m_b_rgate'], 'm_w_igate': out['m_w_igate'], 'm_b_igate': out['m_b_igate'], 'm_lru_lambda': out['m_lru_lambda'], 'm_w_out_a': out['m_w_out_a'], 'm_sgu_ln_g': out['m_sgu_ln_g'], 'm_sgu_ln_b': out['m_sgu_ln_b'], 'm_sgu_w_s': out['m_sgu_w_s'], 'm_sgu_b_s': out['m_sgu_b_s'], 'm_w_out_b': out['m_w_out_b'], 'm_w_out': out['m_w_out'], 'm_norm_mlp_g': out['m_norm_mlp_g'], 'm_w_up': out['m_w_up'], 'm_w_down': out['m_w_down'], 'm_norm_final_g': out['m_norm_final_g'], 'v_norm_mix_g': out['v_norm_mix_g'], 'v_w_in': out['v_w_in'], 'v_conv_w': out['v_conv_w'], 'v_conv_b': out['v_conv_b'], 'v_w_rgate': out['v_w_rgate'], 'v_b_rgate': out['v_b_rgate'], 'v_w_igate': out['v_w_igate'], 'v_b_igate': out['v_b_igate'], 'v_lru_lambda': out['v_lru_lambda'], 'v_w_out_a': out['v_w_out_a'], 'v_sgu_ln_g': out['v_sgu_ln_g'], 'v_sgu_ln_b': out['v_sgu_ln_b'], 'v_sgu_w_s': out['v_sgu_w_s'], 'v_sgu_b_s': out['v_sgu_b_s'], 'v_w_out_b': out['v_w_out_b'], 'v_w_out': out['v_w_out'], 'v_norm_mlp_g': out['v_norm_mlp_g'], 'v_w_up': out['v_w_up'], 'v_w_down': out['v_w_down'], 'v_norm_final_g': out['v_norm_final_g']}


def _loss(weights, diff, rest, loss_target):
    with _jax.named_scope("forward"):
        args = {**rest, TWIN_DIFF_INPUT: diff, **{k: w.astype(_WEIGHT_DTYPES[k]) for k, w in weights.items()}}
        y = _forward(args)
    with _jax.named_scope("loss_head"):
        err = _jnp.square(y.astype(_jnp.float32) - loss_target)
        return 0.5 * _jnp.sum(_jnp.mean(err, axis=-1)) if err.ndim else 0.5 * err


def _adamw(w, g, m, v):
    m = ADAM_B1 * m + (1.0 - ADAM_B1) * g
    v = ADAM_B2 * v + (1.0 - ADAM_B2) * _jnp.square(g)
    m_hat = m / (1.0 - ADAM_B1 ** ADAM_STEP)
    v_hat = v / (1.0 - ADAM_B2 ** ADAM_STEP)
    delta = -ADAM_LR * (m_hat / (_jnp.sqrt(v_hat) + ADAM_EPS) + ADAM_WD * w)
    return delta, m, v


def reference(x, norm_mix_g, w_in, conv_w, conv_b, w_rgate, b_rgate, w_igate, b_igate, lru_lambda, w_out_a, sgu_ln_g, sgu_ln_b, sgu_w_s, sgu_b_s, w_out_b, w_out, norm_mlp_g, w_up, w_down, norm_final_g, loss_target, m_norm_mix_g, m_w_in, m_conv_w, m_conv_b, m_w_rgate, m_b_rgate, m_w_igate, m_b_igate, m_lru_lambda, m_w_out_a, m_sgu_ln_g, m_sgu_ln_b, m_sgu_w_s, m_sgu_b_s, m_w_out_b, m_w_out, m_norm_mlp_g, m_w_up, m_w_down, m_norm_final_g, v_norm_mix_g, v_w_in, v_conv_w, v_conv_b, v_w_rgate, v_b_rgate, v_w_igate, v_b_igate, v_lru_lambda, v_w_out_a, v_sgu_ln_g, v_sgu_ln_b, v_sgu_w_s, v_sgu_b_s, v_w_out_b, v_w_out, v_norm_mlp_g, v_w_up, v_w_down, v_norm_final_g):
    given = dict(x=x, norm_mix_g=norm_mix_g, w_in=w_in, conv_w=conv_w, conv_b=conv_b, w_rgate=w_rgate, b_rgate=b_rgate, w_igate=w_igate, b_igate=b_igate, lru_lambda=lru_lambda, w_out_a=w_out_a, sgu_ln_g=sgu_ln_g, sgu_ln_b=sgu_ln_b, sgu_w_s=sgu_w_s, sgu_b_s=sgu_b_s, w_out_b=w_out_b, w_out=w_out, norm_mlp_g=norm_mlp_g, w_up=w_up, w_down=w_down, norm_final_g=norm_final_g, loss_target=loss_target, m_norm_mix_g=m_norm_mix_g, m_w_in=m_w_in, m_conv_w=m_conv_w, m_conv_b=m_conv_b, m_w_rgate=m_w_rgate, m_b_rgate=m_b_rgate, m_w_igate=m_w_igate, m_b_igate=m_b_igate, m_lru_lambda=m_lru_lambda, m_w_out_a=m_w_out_a, m_sgu_ln_g=m_sgu_ln_g, m_sgu_ln_b=m_sgu_ln_b, m_sgu_w_s=m_sgu_w_s, m_sgu_b_s=m_sgu_b_s, m_w_out_b=m_w_out_b, m_w_out=m_w_out, m_norm_mlp_g=m_norm_mlp_g, m_w_up=m_w_up, m_w_down=m_w_down, m_norm_final_g=m_norm_final_g, v_norm_mix_g=v_norm_mix_g, v_w_in=v_w_in, v_conv_w=v_conv_w, v_conv_b=v_conv_b, v_w_rgate=v_w_rgate, v_b_rgate=v_b_rgate, v_w_igate=v_w_igate, v_b_igate=v_b_igate, v_lru_lambda=v_lru_lambda, v_w_out_a=v_w_out_a, v_sgu_ln_g=v_sgu_ln_g, v_sgu_ln_b=v_sgu_ln_b, v_sgu_w_s=v_sgu_w_s, v_sgu_b_s=v_sgu_b_s, v_w_out_b=v_w_out_b, v_w_out=v_w_out, v_norm_mlp_g=v_norm_mlp_g, v_w_up=v_w_up, v_w_down=v_w_down, v_norm_final_g=v_norm_final_g)
    weights = {n: given[n] for n in TWIN_WEIGHTS}
    shared = {n: given[n] for n in SHARED_INPUTS}
    per_example = {n: given[n] for n in ['x']}
    grad_fn = _jax.value_and_grad(_loss, argnums=(0, 1))

    def one_microbatch(ex, loss_target):
        ex = dict(ex)
        diff = ex.pop(TWIN_DIFF_INPUT)
        return grad_fn(weights, diff, {**shared, **ex}, loss_target)

    if N_MICROBATCH == 1:
        loss, (grad_w, grad_x) = one_microbatch(per_example, given["loss_target"])
    else:
        def body(carry, xs):
            loss_sum, grad_sum = carry
            l_k, (gw_k, gx_k) = one_microbatch(xs[0], xs[1])
            with _jax.named_scope("update"):
                return (loss_sum + l_k, _jax.tree.map(_jnp.add, grad_sum, gw_k)), gx_k

        init = (_jnp.zeros((), _jnp.float32), _jax.tree.map(_jnp.zeros_like, weights))
        (loss, grad_w), grad_x = _jax.lax.scan(body, init, (per_example, given["loss_target"]))
    with _jax.named_scope("update"):
        delta_w, new_m, new_v = {}, {}, {}
        for n in TWIN_WEIGHTS:
            delta_w[n], new_m[n], new_v[n] = _adamw(weights[n], grad_w[n], given["m_" + n], given["v_" + n])
    return (loss, grad_x, *[grad_w[n] for n in TWIN_WEIGHTS], *[delta_w[n] for n in TWIN_WEIGHTS],
            *[new_m[n] for n in TWIN_WEIGHTS], *[new_v[n] for n in TWIN_WEIGHTS])
```

```python
import jax
import jax.numpy as jnp
from jax import lax
from jax.experimental import pallas as pl
from jax.experimental.pallas import tpu as pltpu

F32 = jnp.float32
BF16 = jnp.bfloat16
SDS = jax.ShapeDtypeStruct
MESH = pl.DeviceIdType.MESH
ANY = pl.BlockSpec(memory_space=pl.ANY)

D = 1024
N_SLOT = 8
W_IN_COLS = 768
FF_COLS = 512
HEADS, HEAD_DIM = 4, 256
GROUPS, GROUP_DIM = 4, 256
CHUNK = 128
CONV_K = 4
NORM_EPS = 1e-6
LN_EPS = 1e-5
LRU_C = 8.0
ADAM_LR, ADAM_B1, ADAM_B2, ADAM_EPS, ADAM_WD, ADAM_STEP = 0.001, 0.9, 0.999, 1e-08, 0.01, 10

TM_ROWS = 1024
TM_MERGE = 512
T_BRANCH = 256
MiB = 1024 * 1024

_GELU_C = 0.7978845608028654
_GELU_A = 0.044715


def _cparams(sem, vmem_mib):
    return pltpu.CompilerParams(dimension_semantics=sem, vmem_limit_bytes=vmem_mib * MiB)


def _gelu(x):
    t = jnp.tanh(_GELU_C * (x + _GELU_A * x * x * x))
    return 0.5 * x * (1.0 + t)


def _gelu_and_grad(x):
    x2 = x * x
    t = jnp.tanh(_GELU_C * x * (1.0 + _GELU_A * x2))
    g = 0.5 * x * (1.0 + t)
    dg = 0.5 * (1.0 + t) + 0.5 * x * (1.0 - t * t) * _GELU_C * (1.0 + 3.0 * _GELU_A * x2)
    return g, dg


def _softplus(x):
    return jnp.maximum(x, 0.0) + jnp.log1p(jnp.exp(-jnp.abs(x)))


def _dot(a, b):
    return jnp.dot(a, b, preferred_element_type=F32)


def _dot_nt(a, b):
    return lax.dot_general(a, b, (((1,), (1,)), ((), ())), preferred_element_type=F32)


def _dot_tn(a, b):
    return lax.dot_general(a, b, (((0,), (0,)), ((), ())), preferred_element_type=F32)


def _rows_shifted(prev8, cur, k):
    ext = jnp.concatenate([prev8, cur], axis=0)
    return pltpu.roll(ext, k, 0)[8:]


def _rows_advanced(cur, next8, k):
    t = cur.shape[0]
    ext = jnp.concatenate([cur, next8], axis=0)
    return pltpu.roll(ext, t + 8 - k, 0)[:t]


def _in_proj(x, g_mix, w_in_g):
    s = x.shape[0]
    tm = min(TM_ROWS, s)

    def body(x_ref, g_ref, w_ref, z_ref, n_ref, n_s):
        @pl.when(pl.program_id(1) == 0)
        def _():
            xv = x_ref[...]
            rstd = lax.rsqrt(jnp.mean(xv * xv, axis=-1, keepdims=True) + NORM_EPS)
            nb = (xv * rstd * g_ref[...]).astype(BF16)
            n_s[...] = nb
            n_ref[...] = nb

        z_ref[...] = _dot(n_s[...], w_ref[...]).astype(BF16)

    return pl.pallas_call(
        body, name="in_proj", grid=(s // tm, N_SLOT),
        in_specs=[pl.BlockSpec((tm, D), lambda i, j: (i, 0)),
                  pl.BlockSpec((1, D), lambda i, j: (0, 0)),
                  pl.BlockSpec((None, D, W_IN_COLS), lambda i, j: (j, 0, 0))],
        out_specs=[pl.BlockSpec((tm, W_IN_COLS), lambda i, j: (i, j)),
                   pl.BlockSpec((tm, D), lambda i, j: (i, 0))],
        out_shape=[SDS((s, N_SLOT * W_IN_COLS), BF16), SDS((s, D), BF16)],
        scratch_shapes=[pltpu.VMEM((tm, D), BF16)],
        compiler_params=_cparams(("arbitrary", "arbitrary"), 40),
    )(x, g_mix, w_in_g)


def _lru_gates(xc, xcb, wr_ref, br, wi_ref, bi, sp_lam, a_s, b_s, r_s=None, i_s=None, m_s=None):
    for h in range(HEADS):
        sl = slice(h * HEAD_DIM, (h + 1) * HEAD_DIM)
        r = jax.nn.sigmoid(_dot(xcb[:, sl], wr_ref[h]) + br[:, sl])
        ig = jax.nn.sigmoid(_dot(xcb[:, sl], wi_ref[h]) + bi[:, sl])
        log_a = (-LRU_C) * r * sp_lam[:, sl]
        a = jnp.exp(log_a)
        mult = jnp.sqrt(-jnp.tanh(log_a) * (a * a + 1.0))
        a_s[:, sl] = a
        b_s[:, sl] = xc[:, sl] * ig * mult
        if r_s is not None:
            r_s[:, sl] = r
            i_s[:, sl] = ig
            m_s[:, sl] = mult


def _conv_fwd(xa, prev8, cw, cb):
    xc = cb + cw[0:1, :] * xa
    for k in range(1, CONV_K):
        xc = xc + cw[k:k + 1, :] * _rows_shifted(prev8, xa, k)
    return xc


def _branch_a_fwd(z, conv_w, conv_b, w_r, b_r, w_i, b_i, lam):
    s = z.shape[0]
    ta = min(T_BRANCH, s)
    per16 = ta // 16

    def body(xa_ref, xp_ref, ga_ref, cw_ref, cb_ref, wr_ref, br_ref, wi_ref, bi_ref, lam_ref,
             ya_ref, hs_ref, a_s, b_s, h_s, carry_s):
        i = pl.program_id(0)

        @pl.when(i == 0)
        def _():
            carry_s[...] = jnp.zeros_like(carry_s)

        xa = xa_ref[...].astype(F32)
        prev8 = jnp.where(i > 0, xp_ref[...].astype(F32)[8:16], 0.0)
        xc = _conv_fwd(xa, prev8, cw_ref[...], cb_ref[...])
        sp_lam = _softplus(-lam_ref[...])
        _lru_gates(xc, xc.astype(BF16), wr_ref, br_ref[...], wi_ref, bi_ref[...], sp_lam, a_s, b_s)

        row = lax.broadcasted_iota(jnp.int32, (8, D), 0)

        def group(g, carry):
            off = pl.multiple_of(g * 8, 8)
            a8 = a_s[pl.ds(off, 8), :]
            b8 = b_s[pl.ds(off, 8), :]
            for d in (1, 2, 4):
                a_sh = jnp.where(row >= d, pltpu.roll(a8, d, 0), 1.0)
                b_sh = jnp.where(row >= d, pltpu.roll(b8, d, 0), 0.0)
                b8 = a8 * b_sh + b8
                a8 = a8 * a_sh
            h8 = b8 + a8 * carry
            h_s[pl.ds(off, 8), :] = h8
            return jnp.broadcast_to(h8[7:8, :], (8, D))

        carry_s[...] = lax.fori_loop(0, ta // 8, group, carry_s[...])
        hs = h_s[...]
        hs_ref[...] = hs.astype(BF16)
        ya_ref[...] = (hs * _gelu(ga_ref[...].astype(F32))).astype(BF16)

    vec = pl.BlockSpec((1, D), lambda i: (0, 0))
    gate = pl.BlockSpec((HEADS, HEAD_DIM, HEAD_DIM), lambda i: (0, 0, 0))
    return pl.pallas_call(
        body, name="branch_a_fwd", grid=(s // ta,),
        in_specs=[pl.BlockSpec((ta, D), lambda i: (i, 0)),
                  pl.BlockSpec((16, D), lambda i: (jnp.maximum(i * per16 - 1, 0), 0)),
                  pl.BlockSpec((ta, D), lambda i: (i, 1)),
                  pl.BlockSpec((CONV_K, D), lambda i: (0, 0)), vec, gate, vec, gate, vec, vec],
        out_specs=[pl.BlockSpec((ta, D), lambda i: (i, 0)), pl.BlockSpec((ta, D), lambda i: (i, 0))],
        out_shape=[SDS((s, D), BF16), SDS((s, D), BF16)],
        scratch_shapes=[pltpu.VMEM((ta, D), F32), pltpu.VMEM((ta, D), F32), pltpu.VMEM((ta, D), F32),
                        pltpu.VMEM((8, D), F32)],
        compiler_params=_cparams(("arbitrary",), 40),
    )(z, z, z, conv_w, conv_b, w_r, b_r, w_i, b_i, lam)


def _sgu_common(ub, vb, lg, lb, with_grad):
    if with_grad:
        u, du = _gelu_and_grad(ub)
        v, dv = _gelu_and_grad(vb)
    else:
        u, v, du, dv = _gelu(ub), _gelu(vb), None, None
    mu = jnp.mean(v, axis=-1, keepdims=True)
    vc = v - mu
    rstd = lax.rsqrt(jnp.mean(vc * vc, axis=-1, keepdims=True) + LN_EPS)
    vhat = vc * rstd
    vln = vhat * lg + lb
    return u, du, dv, rstd, vhat, vln


def _masked_ws(ws_ref):
    t = lax.broadcasted_iota(jnp.int32, (CHUNK, CHUNK), 0)
    c = lax.broadcasted_iota(jnp.int32, (CHUNK, CHUNK), 1)
    keep = c <= t
    return [jnp.where(keep, ws_ref[g], 0.0).astype(BF16) for g in range(GROUPS)]


def _branch_b_fwd(z, ln_g, ln_b, w_s, b_s_t):
    s = z.shape[0]
    tb = min(T_BRANCH, s)

    def body(ub_ref, vb_ref, lg_ref, lb_ref, ws_ref, bs_ref, yb_ref):
        u, _, _, _, _, vln = _sgu_common(ub_ref[...].astype(F32), vb_ref[...].astype(F32),
                                         lg_ref[...], lb_ref[...], False)
        vlnb = vln.astype(BF16)
        wm = _masked_ws(ws_ref)
        bs = bs_ref[...]
        for c in range(tb // CHUNK):
            rs = slice(c * CHUNK, (c + 1) * CHUNK)
            for g in range(GROUPS):
                cs = slice(g * GROUP_DIM, (g + 1) * GROUP_DIM)
                sp = _dot(wm[g], vlnb[rs, cs]) + bs[:, g:g + 1]
                yb_ref[rs, cs] = (u[rs, cs] * sp).astype(BF16)

    vec = pl.BlockSpec((1, D), lambda i: (0, 0))
    return pl.pallas_call(
        body, name="branch_b_fwd", grid=(s // tb,),
        in_specs=[pl.BlockSpec((tb, D), lambda i: (i, 2)), pl.BlockSpec((tb, D), lambda i: (i, 3)), vec, vec,
                  pl.BlockSpec((GROUPS, CHUNK, CHUNK), lambda i: (0, 0, 0)),
                  pl.BlockSpec((CHUNK, GROUPS), lambda i: (0, 0))],
        out_specs=pl.BlockSpec((tb, D), lambda i: (i, 0)),
        out_shape=SDS((s, D), BF16),
        compiler_params=_cparams(("arbitrary",), 40),
    )(z, z, ln_g, ln_b, w_s, b_s_t)


def _merge_out(ya, yb, z, x, w_oa, w_ob, w_out):
    s = x.shape[0]
    tm = min(TM_MERGE, s)

    def body(ya_ref, yb_ref, ma_ref, mb_ref, x_ref, woa_ref, wob_ref, wo_ref, pa_ref, pb_ref, mg_ref, h1_ref):
        pa = _dot(ya_ref[...], woa_ref[...])
        pb = _dot(yb_ref[...], wob_ref[...])
        merged = (jax.nn.sigmoid(ma_ref[...].astype(F32)) * pa
                  + jax.nn.sigmoid(mb_ref[...].astype(F32)) * pb).astype(BF16)
        pa_ref[...] = pa.astype(BF16)
        pb_ref[...] = pb.astype(BF16)
        mg_ref[...] = merged
        h1_ref[...] = x_ref[...] + _dot(merged, wo_ref[...])

    row = pl.BlockSpec((tm, D), lambda i: (i, 0))
    wsp = pl.BlockSpec((D, D), lambda i: (0, 0))
    return pl.pallas_call(
        body, name="merge_out", grid=(s // tm,),
        in_specs=[row, row, pl.BlockSpec((tm, D), lambda i: (i, 4)), pl.BlockSpec((tm, D), lambda i: (i, 5)),
                  row, wsp, wsp, wsp],
        out_specs=[row, row, row, row],
        out_shape=[SDS((s, D), BF16), SDS((s, D), BF16), SDS((s, D), BF16), SDS((s, D), F32)],
        compiler_params=_cparams(("arbitrary",), 48),
    )(ya, yb, z, z, x, w_oa, w_ob, w_out)


def _mlp_fwd(h1, g_mlp, w_up_g, w_down, g_fin, tgt):
    s = h1.shape[0]
    tm = min(TM_ROWS, s)
    nj = N_SLOT

    def body(h1_ref, gm_ref, wu_ref, wd_ref, gf_ref, t_ref, r_ref, n2_ref, dh2_ref, loss_ref, dgf_ref, n2_s, acc_s):
        i, j = pl.program_id(0), pl.program_id(1)

        @pl.when(j == 0)
        def _():
            hv = h1_ref[...]
            rstd = lax.rsqrt(jnp.mean(hv * hv, axis=-1, keepdims=True) + NORM_EPS)
            nb = (hv * rstd * gm_ref[...]).astype(BF16)
            n2_s[...] = nb
            n2_ref[...] = nb
            acc_s[...] = jnp.zeros_like(acc_s)

        @pl.when((i == 0) & (j == 0))
        def _():
            loss_ref[...] = jnp.zeros_like(loss_ref)
            dgf_ref[...] = jnp.zeros_like(dgf_ref)

        r = jnp.maximum(_dot(n2_s[...], wu_ref[...]), 0.0)
        r_ref[...] = r.astype(BF16)
        acc_s[...] += _dot((r * r).astype(BF16), wd_ref[...])

        @pl.when(j == nj - 1)
        def _():
            h2 = h1_ref[...] + acc_s[...]
            rstd = lax.rsqrt(jnp.mean(h2 * h2, axis=-1, keepdims=True) + NORM_EPS)
            hh = h2 * rstd
            gf = gf_ref[...]
            e = hh * gf - t_ref[...]
            loss_ref[...] += jnp.sum(e * e) * (0.5 / D)
            dy = e * (1.0 / D)
            dgf_ref[...] += jnp.sum(dy * hh, axis=0, keepdims=True)
            dhh = dy * gf
            dh2_ref[...] = rstd * (dhh - hh * jnp.mean(dhh * hh, axis=-1, keepdims=True))

    row = pl.BlockSpec((tm, D), lambda i, j: (i, 0))
    vec = pl.BlockSpec((1, D), lambda i, j: (0, 0))
    return pl.pallas_call(
        body, name="mlp_fwd", grid=(s // tm, nj),
        in_specs=[row, vec, pl.BlockSpec((None, D, FF_COLS), lambda i, j: (j, 0, 0)),
                  pl.BlockSpec((FF_COLS, D), lambda i, j: (j, 0)), vec, row],
        out_specs=[pl.BlockSpec((tm, FF_COLS), lambda i, j: (i, j)), row, row,
                   pl.BlockSpec((8, 128), lambda i, j: (0, 0)), vec],
        out_shape=[SDS((s, nj * FF_COLS), BF16), SDS((s, D), BF16), SDS((s, D), F32),
                   SDS((8, 128), F32), SDS((1, D), F32)],
        scratch_shapes=[pltpu.VMEM((tm, D), BF16), pltpu.VMEM((tm, D), F32)],
        compiler_params=_cparams(("arbitrary", "arbitrary"), 52),
    )(h1, g_mlp, w_up_g, w_down, g_fin, tgt)


def _mlp_bwd(dh2, r, w_down, w_up_g, h1, g_mlp):
    s = h1.shape[0]
    tm = min(TM_ROWS, s)
    nj = N_SLOT

    def body(dh2_ref, r_ref, wd_ref, wu_ref, h1_ref, gm_ref, df_ref, dh1_ref, dgm_ref, dh2b_s, acc_s):
        i, j = pl.program_id(0), pl.program_id(1)

        @pl.when(j == 0)
        def _():
            dh2b_s[...] = dh2_ref[...].astype(BF16)
            acc_s[...] = jnp.zeros_like(acc_s)

        @pl.when((i == 0) & (j == 0))
        def _():
            dgm_ref[...] = jnp.zeros_like(dgm_ref)

        d_act = _dot_nt(dh2b_s[...], wd_ref[...])
        df = (d_act * (2.0 * r_ref[...].astype(F32))).astype(BF16)
        df_ref[...] = df
        acc_s[...] += _dot_nt(df, wu_ref[...])

        @pl.when(j == nj - 1)
        def _():
            hv = h1_ref[...]
            rstd = lax.rsqrt(jnp.mean(hv * hv, axis=-1, keepdims=True) + NORM_EPS)
            hh = hv * rstd
            dn2 = acc_s[...]
            dgm_ref[...] += jnp.sum(dn2 * hh, axis=0, keepdims=True)
            dhat = dn2 * gm_ref[...]
            dh1_ref[...] = dh2_ref[...] + rstd * (dhat - hh * jnp.mean(dhat * hh, axis=-1, keepdims=True))

    row = pl.BlockSpec((tm, D), lambda i, j: (i, 0))
    vec = pl.BlockSpec((1, D), lambda i, j: (0, 0))
    ffb = pl.BlockSpec((tm, FF_COLS), lambda i, j: (i, j))
    return pl.pallas_call(
        body, name="mlp_bwd", grid=(s // tm, nj),
        in_specs=[row, ffb, pl.BlockSpec((FF_COLS, D), lambda i, j: (j, 0)),
                  pl.BlockSpec((None, D, FF_COLS), lambda i, j: (j, 0, 0)), row, vec],
        out_specs=[ffb, row, vec],
        out_shape=[SDS((s, nj * FF_COLS), BF16), SDS((s, D), F32), SDS((1, D), F32)],
        scratch_shapes=[pltpu.VMEM((tm, D), BF16), pltpu.VMEM((tm, D), F32)],
        compiler_params=_cparams(("arbitrary", "arbitrary"), 52),
    )(dh2, r, w_down, w_up_g, h1, g_mlp)


def _merge_bwd(dh1, z, pa, pb, w_out, w_oa, w_ob):
    s = dh1.shape[0]
    tm = min(TM_MERGE, s)

    def body(dh1_ref, ma_ref, mb_ref, pa_ref, pb_ref, wo_ref, woa_ref, wob_ref,
             dz_ref, dpa_ref, dpb_ref, dya_ref, dyb_ref):
        dm = _dot_nt(dh1_ref[...].astype(BF16), wo_ref[...])
        sa = jax.nn.sigmoid(ma_ref[...].astype(F32))
        sb = jax.nn.sigmoid(mb_ref[...].astype(F32))
        dpa = (dm * sa).astype(BF16)
        dpb = (dm * sb).astype(BF16)
        dz_ref[:, 0:D] = (dm * pa_ref[...].astype(F32) * sa * (1.0 - sa)).astype(BF16)
        dz_ref[:, D:2 * D] = (dm * pb_ref[...].astype(F32) * sb * (1.0 - sb)).astype(BF16)
        dpa_ref[...] = dpa
        dpb_ref[...] = dpb
        dya_ref[...] = _dot_nt(dpa, woa_ref[...]).astype(BF16)
        dyb_ref[...] = _dot_nt(dpb, wob_ref[...]).astype(BF16)

    row = pl.BlockSpec((tm, D), lambda i: (i, 0))
    wsp = pl.BlockSpec((D, D), lambda i: (0, 0))
    return pl.pallas_call(
        body, name="merge_bwd", grid=(s // tm,),
        in_specs=[row, pl.BlockSpec((tm, D), lambda i: (i, 4)), pl.BlockSpec((tm, D), lambda i: (i, 5)),
                  row, row, wsp, wsp, wsp],
        out_specs=[pl.BlockSpec((tm, 2 * D), lambda i: (i, 2)), row, row, row, row],
        out_shape=[SDS((s, 6 * D), BF16)] + [SDS((s, D), BF16)] * 4,
        compiler_params=_cparams(("arbitrary",), 48),
    )(dh1, z, z, pa, pb, w_out, w_oa, w_ob)


def _branch_b_bwd(dz, dyb, z, ln_g, ln_b, w_s, b_s_t):
    s = z.shape[0]
    tb = min(T_BRANCH, s)

    def body(dz_in, dyb_ref, ub_ref, vb_ref, lg_ref, lb_ref, ws_ref, bs_ref,
             dz_ref, dws_ref, dbs_ref, dln_ref, du_s, dvln_s):
        del dz_in

        @pl.when(pl.program_id(0) == 0)
        def _():
            dws_ref[...] = jnp.zeros_like(dws_ref)
            dbs_ref[...] = jnp.zeros_like(dbs_ref)
            dln_ref[...] = jnp.zeros_like(dln_ref)

        lg = lg_ref[...]
        u, du, dv, rstd, vhat, vln = _sgu_common(ub_ref[...].astype(F32), vb_ref[...].astype(F32),
                                                 lg, lb_ref[...], True)
        vlnb = vln.astype(BF16)
        dyb_v = dyb_ref[...].astype(F32)
        wm = _masked_ws(ws_ref)
        bs = bs_ref[...]
        for c in range(tb // CHUNK):
            rs = slice(c * CHUNK, (c + 1) * CHUNK)
            for g in range(GROUPS):
                cs = slice(g * GROUP_DIM, (g + 1) * GROUP_DIM)
                v_blk = vlnb[rs, cs]
                sp = _dot(wm[g], v_blk) + bs[:, g:g + 1]
                d_sp = dyb_v[rs, cs] * u[rs, cs]
                d_spb = d_sp.astype(BF16)
                du_s[rs, cs] = dyb_v[rs, cs] * sp
                dvln_s[rs, cs] = _dot_tn(wm[g], d_spb)
                dws_ref[g] += _dot_nt(d_spb, v_blk)
                dbs_ref[g] += jnp.broadcast_to(jnp.sum(d_sp, axis=-1, keepdims=True), (CHUNK, CHUNK))
        dvln = dvln_s[...]
        dln_ref[0:1, :] += jnp.sum(dvln * vhat, axis=0, keepdims=True)
        dln_ref[1:2, :] += jnp.sum(dvln, axis=0, keepdims=True)
        dvh = dvln * lg
        d_v = rstd * (dvh - jnp.mean(dvh, axis=-1, keepdims=True)
                      - vhat * jnp.mean(dvh * vhat, axis=-1, keepdims=True))
        dz_ref[:, 0:D] = (du_s[...] * du).astype(BF16)
        dz_ref[:, D:2 * D] = (d_v * dv).astype(BF16)

    vec = pl.BlockSpec((1, D), lambda i: (0, 0))
    sq = pl.BlockSpec((GROUPS, CHUNK, CHUNK), lambda i: (0, 0, 0))
    return pl.pallas_call(
        body, name="branch_b_bwd", grid=(s // tb,),
        in_specs=[ANY, pl.BlockSpec((tb, D), lambda i: (i, 0)),
                  pl.BlockSpec((tb, D), lambda i: (i, 2)), pl.BlockSpec((tb, D), lambda i: (i, 3)), vec, vec, sq,
                  pl.BlockSpec((CHUNK, GROUPS), lambda i: (0, 0))],
        out_specs=[pl.BlockSpec((tb, 2 * D), lambda i: (i, 1)), sq, sq, pl.BlockSpec((8, D), lambda i: (0, 0))],
        out_shape=[SDS(dz.shape, BF16), SDS((GROUPS, CHUNK, CHUNK), F32), SDS((GROUPS, CHUNK, CHUNK), F32),
                   SDS((8, D), F32)],
        scratch_shapes=[pltpu.VMEM((tb, D), F32), pltpu.VMEM((tb, D), F32)],
        input_output_aliases={0: 0},
        compiler_params=_cparams(("arbitrary",), 40),
    )(dz, dyb, z, z, ln_g, ln_b, w_s, b_s_t)


def _branch_a_bwd(dz, dya, z, hs, conv_w, conv_b, w_r, b_r, w_i, b_i, lam):
    s = z.shape[0]
    ta = min(T_BRANCH, s)
    nb = s // ta
    per16 = ta // 16

    def body(dz_in, dya_ref, xa_ref, xp_ref, ga_ref, hs_ref, hp_ref, cw_ref, cb_ref, wr_ref, br_ref, wi_ref,
             bi_ref, lam_ref, dz_ref, vec_ref, dwr_ref, dwi_ref,
             a_s, b_s, h_s, r_s, i_s, m_s, dcar_s, acar_s, dxc_s):
        del dz_in
        i = pl.program_id(0)
        blk = nb - 1 - i

        @pl.when(i == 0)
        def _():
            dcar_s[...] = jnp.zeros_like(dcar_s)
            acar_s[...] = jnp.zeros_like(acar_s)
            dxc_s[...] = jnp.zeros_like(dxc_s)
            vec_ref[...] = jnp.zeros_like(vec_ref)
            dwr_ref[...] = jnp.zeros_like(dwr_ref)
            dwi_ref[...] = jnp.zeros_like(dwi_ref)

        cw = cw_ref[...]
        lam_v = lam_ref[...]
        xa = xa_ref[...].astype(F32)
        prev8 = jnp.where(blk > 0, xp_ref[...].astype(F32)[8:16], 0.0)
        xc = _conv_fwd(xa, prev8, cw, cb_ref[...])
        xcb = xc.astype(BF16)
        sp_lam = _softplus(-lam_v)
        _lru_gates(xc, xcb, wr_ref, br_ref[...], wi_ref, bi_ref[...], sp_lam, a_s, b_s, r_s, i_s, m_s)

        hs_v = hs_ref[...].astype(F32)
        hprev8 = jnp.where(blk > 0, hp_ref[...].astype(F32)[8:16], 0.0)
        h_m1 = _rows_shifted(hprev8, hs_v, 1)
        gg, dgg = _gelu_and_grad(ga_ref[...].astype(F32))
        dya_v = dya_ref[...].astype(F32)
        dz_ref[:, D:2 * D] = (dya_v * hs_v * dgg).astype(BF16)

        a_v = a_s[...]
        a_s[...] = _rows_advanced(a_v, acar_s[...], 1)
        b_s[...] = dya_v * gg

        row = lax.broadcasted_iota(jnp.int32, (8, D), 0)
        ng = ta // 8

        def group(gi, carry):
            off = pl.multiple_of((ng - 1 - gi) * 8, 8)
            c8 = a_s[pl.ds(off, 8), :]
            d8 = b_s[pl.ds(off, 8), :]
            for d in (1, 2, 4):
                c_sh = jnp.where(row < 8 - d, pltpu.roll(c8, 8 - d, 0), 1.0)
                d_sh = jnp.where(row < 8 - d, pltpu.roll(d8, 8 - d, 0), 0.0)
                d8 = c8 * d_sh + d8
                c8 = c8 * c_sh
            dh8 = d8 + c8 * carry
            h_s[pl.ds(off, 8), :] = dh8
            return jnp.broadcast_to(dh8[0:1, :], (8, D))

        dcar_s[...] = lax.fori_loop(0, ng, group, dcar_s[...])
        acar_s[...] = jnp.broadcast_to(a_v[0:1, :], (8, D))

        dbx = h_s[...]
        r_v, i_v, m_v = r_s[...], i_s[...], m_s[...]
        d_mult = dbx * xc * i_v
        d_loga = dbx * h_m1 * a_v - d_mult * (a_v * a_v) / m_v
        d_pr = d_loga * ((-LRU_C) * sp_lam) * r_v * (1.0 - r_v)
        d_pi = dbx * xc * m_v * i_v * (1.0 - i_v)
        vec_ref[7:8, :] += jnp.sum(d_loga * r_v, axis=0, keepdims=True) * (LRU_C * jax.nn.sigmoid(-lam_v))
        vec_ref[5:6, :] += jnp.sum(d_pr, axis=0, keepdims=True)
        vec_ref[6:7, :] += jnp.sum(d_pi, axis=0, keepdims=True)
        d_prb = d_pr.astype(BF16)
        d_pib = d_pi.astype(BF16)
        h_s[...] = dbx * i_v * m_v
        for h in range(HEADS):
            sl = slice(h * HEAD_DIM, (h + 1) * HEAD_DIM)
            h_s[:, sl] += _dot_nt(d_prb[:, sl], wr_ref[h]) + _dot_nt(d_pib[:, sl], wi_ref[h])
            dwr_ref[h] += _dot_tn(xcb[:, sl], d_prb[:, sl])
            dwi_ref[h] += _dot_tn(xcb[:, sl], d_pib[:, sl])
        d_xc = h_s[...]
        vec_ref[4:5, :] += jnp.sum(d_xc, axis=0, keepdims=True)
        vec_ref[0:1, :] += jnp.sum(d_xc * xa, axis=0, keepdims=True)
        d_xa = cw[0:1, :] * d_xc
        nxt = dxc_s[...]
        for k in range(1, CONV_K):
            vec_ref[k:k + 1, :] += jnp.sum(d_xc * _rows_shifted(prev8, xa, k), axis=0, keepdims=True)
            d_xa = d_xa + cw[k:k + 1, :] * _rows_advanced(d_xc, nxt, k)
        dz_ref[:, 0:D] = d_xa.astype(BF16)
        dxc_s[...] = d_xc[0:8, :]

    vec = pl.BlockSpec((1, D), lambda i: (0, 0))
    gate = pl.BlockSpec((HEADS, HEAD_DIM, HEAD_DIM), lambda i: (0, 0, 0))
    cur = lambda c: pl.BlockSpec((ta, D), lambda i: (nb - 1 - i, c))
    before = lambda c: pl.BlockSpec((16, D), lambda i: (jnp.maximum((nb - 1 - i) * per16 - 1, 0), c))
    return pl.pallas_call(
        body, name="branch_a_bwd", grid=(nb,),
        in_specs=[ANY, cur(0), cur(0), before(0), cur(1), cur(0), before(0),
                  pl.BlockSpec((CONV_K, D), lambda i: (0, 0)), vec, gate, vec, gate, vec, vec],
        out_specs=[pl.BlockSpec((ta, 2 * D), lambda i: (nb - 1 - i, 0)), pl.BlockSpec((8, D), lambda i: (0, 0)),
                   gate, gate],
        out_shape=[SDS(dz.shape, BF16), SDS((8, D), F32), SDS((HEADS, HEAD_DIM, HEAD_DIM), F32),
                   SDS((HEADS, HEAD_DIM, HEAD_DIM), F32)],
        scratch_shapes=[pltpu.VMEM((ta, D), F32)] * 6 + [pltpu.VMEM((8, D), F32)] * 3,
        input_output_aliases={0: 0},
        compiler_params=_cparams(("arbitrary",), 48),
    )(dz, dya, z, z, z, hs, hs, conv_w, conv_b, w_r, b_r, w_i, b_i, lam)


def _in_bwd(dz, w_in_g, x, dh1, g_mix):
    s = x.shape[0]
    tm = min(TM_ROWS, s)
    nj = N_SLOT

    def body(dz_ref, w_ref, x_ref, dh1_ref, g_ref, dx_ref, dg_ref, acc_s):
        i, j = pl.program_id(0), pl.program_id(1)

        @pl.when(j == 0)
        def _():
            acc_s[...] = jnp.zeros_like(acc_s)

        @pl.when((i == 0) & (j == 0))
        def _():
            dg_ref[...] = jnp.zeros_like(dg_ref)

        acc_s[...] += _dot_nt(dz_ref[...], w_ref[...])

        @pl.when(j == nj - 1)
        def _():
            xv = x_ref[...]
            rstd = lax.rsqrt(jnp.mean(xv * xv, axis=-1, keepdims=True) + NORM_EPS)
            xh = xv * rstd
            dn = acc_s[...]
            dg_ref[...] += jnp.sum(dn * xh, axis=0, keepdims=True)
            dhat = dn * g_ref[...]
            dx_ref[...] = dh1_ref[...] + rstd * (dhat - xh * jnp.mean(dhat * xh, axis=-1, keepdims=True))

    row = pl.BlockSpec((tm, D), lambda i, j: (i, 0))
    vec = pl.BlockSpec((1, D), lambda i, j: (0, 0))
    return pl.pallas_call(
        body, name="in_bwd", grid=(s // tm, nj),
        in_specs=[pl.BlockSpec((tm, W_IN_COLS), lambda i, j: (i, j)),
                  pl.BlockSpec((None, D, W_IN_COLS), lambda i, j: (j, 0, 0)), row, row, vec],
        out_specs=[row, vec],
        out_shape=[SDS((s, D), F32), SDS((1, D), F32)],
        scratch_shapes=[pltpu.VMEM((tm, D), F32)],
        compiler_params=_cparams(("arbitrary", "arbitrary"), 48),
    )(dz, w_in_g, x, dh1, g_mix)


def _wgrad(name, a, b, nblk, a_split, b_split, square_a=False):
    s = a.shape[0]
    ts = min(TM_ROWS, s)
    a_w = a.shape[1] // nblk if a_split else a.shape[1]
    b_w = b.shape[1] // nblk if b_split else b.shape[1]

    def body(a_ref, b_ref, o_ref, acc_s):
        t = pl.program_id(1)

        @pl.when(t == 0)
        def _():
            acc_s[...] = jnp.zeros_like(acc_s)

        av = a_ref[...]
        if square_a:
            av = av * av
        acc_s[...] += _dot_tn(av.astype(BF16), b_ref[...].astype(BF16))

        @pl.when(t == pl.num_programs(1) - 1)
        def _():
            o_ref[...] = acc_s[...].astype(BF16)

    return pl.pallas_call(
        body, name=name, grid=(nblk, s // ts),
        in_specs=[pl.BlockSpec((ts, a_w), (lambda k, t: (t, k)) if a_split else (lambda k, t: (t, 0))),
                  pl.BlockSpec((ts, b_w), (lambda k, t: (t, k)) if b_split else (lambda k, t: (t, 0)))],
        out_specs=pl.BlockSpec((None, a_w, b_w), lambda k, t: (k, 0, 0)),
        out_shape=SDS((nblk, a_w, b_w), BF16),
        scratch_shapes=[pltpu.VMEM((a_w, b_w), F32)],
        compiler_params=_cparams(("arbitrary", "arbitrary"), 48),
    )(a, b)


def _place():
    x, y, c = lax.axis_index("x"), lax.axis_index("y"), lax.axis_index("c")
    return x, y, c


def _other_chips(x, y):
    return [(x, 1 - y, 2 * x + 1 - y), (1 - x, y, 2 * (1 - x) + y), (1 - x, 1 - y, 2 * (1 - x) + 1 - y)]


def _all_gather(shards):
    n = len(shards)

    def body(*refs):
        ins, outs = refs[:n], refs[n:2 * n]
        send_sems, recv_sems, local_sems = refs[2 * n:]
        x, y, c = _place()
        chip = 2 * x + y
        me = 2 * chip + c
        sib = (x, y, 1 - c)
        chips = _other_chips(x, y)

        def rc(k, t, src, blk, to):
            return pltpu.make_async_remote_copy(
                src_ref=src, dst_ref=outs[t].at[blk], send_sem=send_sems.at[k * n + t],
                recv_sem=recv_sems.at[k * n + t], device_id=to, device_id_type=MESH)

        local = [pltpu.make_async_copy(ins[t], outs[t].at[me], local_sems.at[t]) for t in range(n)]
        for cp in local:
            cp.start()
        sends = [rc(0, t, ins[t], me, sib) for t in range(n)]
        for j, (px, py, _) in enumerate(chips):
            sends += [rc(1 + j, t, ins[t], me, (px, py, c)) for t in range(n)]
        for cp in sends:
            cp.start()
        passed = []
        for j, (px, py, pc) in enumerate(chips):
            blk = 2 * pc + c
            for t in range(n):
                rc(1 + j, t, ins[t], blk, sib).wait_recv()
            for t in range(n):
                cp = rc(4 + j, t, outs[t].at[blk], blk, sib)
                cp.start()
                passed.append(cp)
        for t in range(n):
            rc(0, t, ins[t], 2 * chip + 1 - c, sib).wait_recv()
        for j, (px, py, pc) in enumerate(chips):
            for t in range(n):
                rc(4 + j, t, ins[t], 2 * pc + 1 - c, sib).wait_recv()
        for cp in sends + passed:
            cp.wait_send()
        for cp in local:
            cp.wait()

    return pl.pallas_call(
        body, name="all_gather",
        in_specs=[ANY] * n, out_specs=[ANY] * n,
        out_shape=[SDS((N_SLOT,) + tuple(a.shape), a.dtype) for a in shards],
        scratch_shapes=[pltpu.SemaphoreType.DMA((7 * n,)), pltpu.SemaphoreType.DMA((7 * n,)),
                        pltpu.SemaphoreType.DMA((n,))],
        compiler_params=pltpu.CompilerParams(has_side_effects=True),
    )(*shards)


def _to_sibling(grads, small):
    n = len(grads)

    def body(*refs):
        ins, small_in = refs[:n], refs[n]
        outs, small_out = refs[n + 1:2 * n + 1], refs[2 * n + 1]
        send_sems, recv_sems = refs[2 * n + 2:]
        x, y, c = _place()
        sib = (x, y, 1 - c)

        def rc(t, src, dst):
            return pltpu.make_async_remote_copy(src_ref=src, dst_ref=dst, send_sem=send_sems.at[t],
                                                recv_sem=recv_sems.at[t], device_id=sib, device_id_type=MESH)

        for t in range(n):
            for j in range(4):
                rc(t, ins[t].at[2 * j + 1 - c], outs[t].at[j]).start()
        rc(n, small_in, small_out).start()
        for t in range(n):
            rc(t, ins[t].at[pl.ds(0, 4)], outs[t]).wait()
        rc(n, small_in, small_out).wait()

    return pl.pallas_call(
        body, name="rs_to_sibling",
        in_specs=[ANY] * (n + 1), out_specs=[ANY] * (n + 1),
        out_shape=[SDS((4,) + tuple(g.shape[1:]), g.dtype) for g in grads] + [SDS(small.shape, small.dtype)],
        scratch_shapes=[pltpu.SemaphoreType.DMA((n + 1,)), pltpu.SemaphoreType.DMA((n + 1,))],
        compiler_params=pltpu.CompilerParams(has_side_effects=True),
    )(*grads, small)


def _to_chips(parts, small):
    n = len(parts)

    def body(*refs):
        ins, small_in = refs[:n], refs[n]
        outs, small_out = refs[n + 1:2 * n + 1], refs[2 * n + 1]
        send_sems, recv_sems, local_sems = refs[2 * n + 2:]
        x, y, c = _place()
        chip = 2 * x + y

        local = [pltpu.make_async_copy(ins[t].at[chip], outs[t].at[chip], local_sems.at[t]) for t in range(n)]
        local.append(pltpu.make_async_copy(small_in, small_out.at[chip], local_sems.at[n]))
        for cp in local:
            cp.start()
        for px, py, pc in _other_chips(x, y):
            for t in range(n):
                pltpu.make_async_remote_copy(src_ref=ins[t].at[pc], dst_ref=outs[t].at[chip],
                                             send_sem=send_sems.at[t], recv_sem=recv_sems.at[t],
                                             device_id=(px, py, c), device_id_type=MESH).start()
            pltpu.make_async_remote_copy(src_ref=small_in, dst_ref=small_out.at[chip],
                                         send_sem=send_sems.at[n], recv_sem=recv_sems.at[n],
                                         device_id=(px, py, c), device_id_type=MESH).start()
        for t in range(n + 1):
            three = (outs[t] if t < n else small_out).at[pl.ds(0, 3)]
            pltpu.make_async_remote_copy(src_ref=three, dst_ref=three, send_sem=send_sems.at[t],
                                         recv_sem=recv_sems.at[t], device_id=(x, y, c), device_id_type=MESH).wait()
        for cp in local:
            cp.wait()

    return pl.pallas_call(
        body, name="rs_to_chips",
        in_specs=[ANY] * (n + 1), out_specs=[ANY] * (n + 1),
        out_shape=[SDS(p.shape, p.dtype) for p in parts] + [SDS((4,) + tuple(small.shape), small.dtype)],
        scratch_shapes=[pltpu.SemaphoreType.DMA((n + 1,)), pltpu.SemaphoreType.DMA((n + 1,)),
                        pltpu.SemaphoreType.DMA((n + 1,))],
        compiler_params=pltpu.CompilerParams(has_side_effects=True),
    )(*parts, small)


def _row_tile(rows):
    for t in (512, 256, 128, 64, 32, 16, 8):
        if rows % t == 0:
            return t
    return rows


def _pair_sum(name, g8, recv4, core):
    _, rows, cols = recv4.shape
    tr = _row_tile(rows)
    g42 = g8.reshape(4, 2, rows, cols)

    def body(c_ref, g_ref, r_ref, o_ref):
        del c_ref
        o_ref[...] = (g_ref[...].astype(F32) + r_ref[...].astype(F32)).astype(o_ref.dtype)

    return pl.pallas_call(
        body, name=name,
        grid_spec=pltpu.PrefetchScalarGridSpec(
            num_scalar_prefetch=1, grid=(4, rows // tr),
            in_specs=[pl.BlockSpec((None, None, tr, cols), lambda j, i, c_ref: (j, c_ref[0], i, 0)),
                      pl.BlockSpec((None, tr, cols), lambda j, i, c_ref: (j, i, 0))],
            out_specs=pl.BlockSpec((None, tr, cols), lambda j, i, c_ref: (j, i, 0))),
        out_shape=SDS(recv4.shape, g8.dtype),
        compiler_params=_cparams(("arbitrary", "arbitrary"), 32),
    )(core, g42, recv4)


def _add2(name, a, b):
    rows, cols = a.shape
    tr = _row_tile(rows)

    def body(a_ref, b_ref, o_ref):
        o_ref[...] = a_ref[...] + b_ref[...]

    blk = pl.BlockSpec((tr, cols), lambda i: (i, 0))
    return pl.pallas_call(body, name=name, grid=(rows // tr,), in_specs=[blk, blk], out_specs=blk,
                          out_shape=SDS(a.shape, a.dtype),
                          compiler_params=_cparams(("arbitrary",), 32))(a, b)


def _sum4(name, r4):
    _, rows, cols = r4.shape
    tr = _row_tile(rows)

    def body(r_ref, o_ref):
        o_ref[...] = ((r_ref[0] + r_ref[1]) + r_ref[2]) + r_ref[3]

    return pl.pallas_call(body, name=name, grid=(rows // tr,),
                          in_specs=[pl.BlockSpec((4, tr, cols), lambda i: (0, i, 0))],
                          out_specs=pl.BlockSpec((tr, cols), lambda i: (i, 0)),
                          out_shape=SDS((rows, cols), r4.dtype),
                          compiler_params=_cparams(("arbitrary",), 32))(r4)


def _adamw(name, terms, w, m, v):
    k, rows, cols = terms.shape
    tr = _row_tile(rows)
    c1 = 1.0 / (1.0 - ADAM_B1 ** ADAM_STEP)
    c2 = 1.0 / (1.0 - ADAM_B2 ** ADAM_STEP)

    def body(t_ref, w_ref, m_ref, v_ref, g_ref, d_ref, mo_ref, vo_ref):
        g = t_ref[0].astype(F32)
        for q in range(1, k):
            g = g + t_ref[q].astype(F32)
        mn = ADAM_B1 * m_ref[...] + (1.0 - ADAM_B1) * g
        vn = ADAM_B2 * v_ref[...] + (1.0 - ADAM_B2) * (g * g)
        g_ref[...] = g
        mo_ref[...] = mn
        vo_ref[...] = vn
        d_ref[...] = (-ADAM_LR) * ((mn * c1) / (jnp.sqrt(vn * c2) + ADAM_EPS) + ADAM_WD * w_ref[...])

    blk = pl.BlockSpec((tr, cols), lambda i: (i, 0))
    return pl.pallas_call(body, name=name, grid=(rows // tr,),
                          in_specs=[pl.BlockSpec((k, tr, cols), lambda i: (0, i, 0)), blk, blk, blk],
                          out_specs=[blk] * 4, out_shape=[SDS((rows, cols), F32)] * 4,
                          compiler_params=_cparams(("arbitrary",), 40))(terms, w, m, v)


_SMALL_ROWS = 80
_FLAT_ROWS = 584


def kernel(x, norm_mix_g, w_in, conv_w, conv_b, w_rgate, b_rgate, w_igate, b_igate, lru_lambda, w_out_a, sgu_ln_g, sgu_ln_b, sgu_w_s, sgu_b_s, w_out_b, w_out, norm_mlp_g, w_up, w_down, norm_final_g, loss_target, m_norm_mix_g, m_w_in, m_conv_w, m_conv_b, m_w_rgate, m_b_rgate, m_w_igate, m_b_igate, m_lru_lambda, m_w_out_a, m_sgu_ln_g, m_sgu_ln_b, m_sgu_w_s, m_sgu_b_s, m_w_out_b, m_w_out, m_norm_mlp_g, m_w_up, m_w_down, m_norm_final_g, v_norm_mix_g, v_w_in, v_conv_w, v_conv_b, v_w_rgate, v_b_rgate, v_w_igate, v_b_igate, v_lru_lambda, v_w_out_a, v_sgu_ln_g, v_sgu_ln_b, v_sgu_w_s, v_sgu_b_s, v_w_out_b, v_w_out, v_norm_mlp_g, v_w_up, v_w_down, v_norm_final_g):
    cx, cy, cc = _place()
    me = 4 * cx + 2 * cy + cc
    core = jnp.reshape(cc, (1,)).astype(jnp.int32)
    xs = x[0]
    tgt = loss_target[0]
    s = xs.shape[0]

    gate_shard = jnp.stack([w_rgate[0], w_igate[0]]).astype(BF16).reshape(2 * HEADS * 32, HEAD_DIM)
    vec_shard = jnp.concatenate([conv_w[0], b_rgate[0], b_igate[0]], axis=1)
    vec_shard = jnp.pad(vec_shard, ((0, 4), (0, 256 - vec_shard.shape[1])))
    shards = [w_in[0].astype(BF16), w_out_a[0].astype(BF16), w_out_b[0].astype(BF16), w_out[0].astype(BF16),
              w_up[0].astype(BF16), w_down[0].astype(BF16), gate_shard, vec_shard]
    w_in_g, w_oa_g, w_ob_g, w_out_g, w_up_g, w_down_g, gate_g, vec_g = _all_gather(shards)
    w_oa_f = w_oa_g.reshape(D, D)
    w_ob_f = w_ob_g.reshape(D, D)
    w_out_f = w_out_g.reshape(D, D)
    w_down_f = w_down_g.reshape(N_SLOT * FF_COLS, D)
    gates = gate_g.reshape(N_SLOT, 2, HEADS, 32, HEAD_DIM).transpose(1, 2, 0, 3, 4).reshape(2, HEADS, HEAD_DIM, HEAD_DIM)
    w_r_f, w_i_f = gates[0], gates[1]
    conv_w_f = vec_g[:, 0:4, 0:128].transpose(1, 0, 2).reshape(CONV_K, D)
    b_r_f = vec_g[:, 0:4, 128:160].transpose(1, 0, 2).reshape(1, D)
    b_i_f = vec_g[:, 0:4, 160:192].transpose(1, 0, 2).reshape(1, D)
    b_s_t = jnp.transpose(sgu_b_s[0])

    z, n1 = _in_proj(xs, norm_mix_g, w_in_g)
    ya, hs = _branch_a_fwd(z, conv_w_f, conv_b, w_r_f, b_r_f, w_i_f, b_i_f, lru_lambda)
    yb = _branch_b_fwd(z, sgu_ln_g, sgu_ln_b, sgu_w_s[0], b_s_t)
    pa, pb, merged, h1 = _merge_out(ya, yb, z, xs, w_oa_f, w_ob_f, w_out_f)
    gf2 = norm_final_g.reshape(1, D)
    r_act, n2, dh2, loss_acc, d_gfin = _mlp_fwd(h1, norm_mlp_g, w_up_g, w_down_f, gf2, tgt)

    df, dh1, d_gmlp = _mlp_bwd(dh2, r_act, w_down_f, w_up_g, h1, norm_mlp_g)
    dz, dpa, dpb, dya, dyb = _merge_bwd(dh1, z, pa, pb, w_out_f, w_oa_f, w_ob_f)
    dz, d_ws, d_bs, d_ln = _branch_b_bwd(dz, dyb, z, sgu_ln_g, sgu_ln_b, sgu_w_s[0], b_s_t)
    dz, d_vec, d_wr, d_wi = _branch_a_bwd(dz, dya, z, hs, conv_w_f, conv_b, w_r_f, b_r_f, w_i_f, b_i_f, lru_lambda)
    dx, d_gmix = _in_bwd(dz, w_in_g, xs, dh1, norm_mix_g)

    g_down = _wgrad("wgrad_down", r_act, dh2, N_SLOT, True, False, square_a=True)
    g_up = _wgrad("wgrad_up", n2, df, N_SLOT, False, True)
    g_out = _wgrad("wgrad_out", merged, dh1, 1, False, False).reshape(N_SLOT, D // N_SLOT, D)
    g_oa = _wgrad("wgrad_out_a", ya, dpa, 1, False, False).reshape(N_SLOT, D // N_SLOT, D)
    g_ob = _wgrad("wgrad_out_b", yb, dpb, 1, False, False).reshape(N_SLOT, D // N_SLOT, D)
    g_in = _wgrad("wgrad_in", n1, dz, N_SLOT, False, True)
    g_gate = jnp.stack([d_wr, d_wi]).reshape(2, HEADS, N_SLOT, 32, HEAD_DIM).transpose(2, 0, 1, 3, 4)
    g_gate = g_gate.reshape(N_SLOT, 2 * HEADS * 32, HEAD_DIM).astype(BF16)
    ws_mask = jnp.tril(jnp.ones((CHUNK, CHUNK), F32))
    small = jnp.concatenate([
        d_gmix, d_vec[0:4], d_vec[4:5], d_vec[5:6], d_vec[6:7], d_vec[7:8], d_ln[0:1], d_ln[1:2], d_gmlp, d_gfin,
        jnp.pad(d_bs[:, :, 0].reshape(1, GROUPS * CHUNK), ((0, 0), (0, D - GROUPS * CHUNK))),
        (d_ws * ws_mask).reshape(64, D), jnp.zeros((_SMALL_ROWS - 78, D), F32)], axis=0)

    grads = [g_in, g_oa, g_ob, g_out, g_up, g_down, g_gate]
    recv = _to_sibling(grads, small)
    names = ["in", "out_a", "out_b", "out", "up", "down", "gate"]
    parts = [_pair_sum("pair_sum_" + nm, g, r, core) for nm, g, r in zip(names, grads, recv[:-1])]
    small_chip = _add2("pair_sum_small", small, recv[-1])
    got = _to_chips(parts, small_chip)
    small_sum = _sum4("sum_small", got[-1])

    def step(nm, terms, w, m, v, rows, cols):
        g, d, mn, vn = _adamw("adamw_" + nm, terms.reshape(4, rows, cols), w.reshape(rows, cols),
                              m.reshape(rows, cols), v.reshape(rows, cols))
        return [a.reshape(w.shape) for a in (g, d, mn, vn)]

    o_in = step("in", got[0], w_in, m_w_in, v_w_in, D, W_IN_COLS)
    o_oa = step("out_a", got[1], w_out_a, m_w_out_a, v_w_out_a, D // N_SLOT, D)
    o_ob = step("out_b", got[2], w_out_b, m_w_out_b, v_w_out_b, D // N_SLOT, D)
    o_out = step("out", got[3], w_out, m_w_out, v_w_out, D // N_SLOT, D)
    o_up = step("up", got[4], w_up, m_w_up, v_w_up, D, FF_COLS)
    o_down = step("down", got[5], w_down, m_w_down, v_w_down, FF_COLS, D)
    gate_w = jnp.stack([w_rgate[0], w_igate[0]]).reshape(2 * HEADS * 32, HEAD_DIM)
    gate_m = jnp.stack([m_w_rgate[0], m_w_igate[0]]).reshape(2 * HEADS * 32, HEAD_DIM)
    gate_v = jnp.stack([v_w_rgate[0], v_w_igate[0]]).reshape(2 * HEADS * 32, HEAD_DIM)
    o_gate = _adamw("adamw_gate", got[6], gate_w, gate_m, gate_v)
    o_gate = [a.reshape(2, 1, HEADS, 32, HEAD_DIM) for a in o_gate]
    o_wr = [a[0] for a in o_gate]
    o_wi = [a[1] for a in o_gate]

    def own(full, width):
        return lax.dynamic_slice_in_dim(full, me * width, width, axis=1)

    small_g = {
        "norm_mix_g": small_sum[0:1], "conv_w": own(small_sum[1:5], 128)[None], "conv_b": small_sum[5:6],
        "b_rgate": own(small_sum[6:7].reshape(HEADS, HEAD_DIM), 32)[None],
        "b_igate": own(small_sum[7:8].reshape(HEADS, HEAD_DIM), 32)[None],
        "lru_lambda": small_sum[8:9], "sgu_ln_g": small_sum[9:10], "sgu_ln_b": small_sum[10:11],
        "norm_mlp_g": small_sum[11:12], "norm_final_g": small_sum[12],
        "sgu_b_s": small_sum[13, 0:GROUPS * CHUNK].reshape(1, GROUPS, CHUNK),
        "sgu_w_s": small_sum[14:78].reshape(1, GROUPS, CHUNK, CHUNK),
    }
    small_w = {"norm_mix_g": (norm_mix_g, m_norm_mix_g, v_norm_mix_g), "conv_w": (conv_w, m_conv_w, v_conv_w),
               "conv_b": (conv_b, m_conv_b, v_conv_b), "b_rgate": (b_rgate, m_b_rgate, v_b_rgate),
               "b_igate": (b_igate, m_b_igate, v_b_igate), "lru_lambda": (lru_lambda, m_lru_lambda, v_lru_lambda),
               "sgu_ln_g": (sgu_ln_g, m_sgu_ln_g, v_sgu_ln_g), "sgu_ln_b": (sgu_ln_b, m_sgu_ln_b, v_sgu_ln_b),
               "norm_mlp_g": (norm_mlp_g, m_norm_mlp_g, v_norm_mlp_g),
               "norm_final_g": (norm_final_g, m_norm_final_g, v_norm_final_g),
               "sgu_b_s": (sgu_b_s, m_sgu_b_s, v_sgu_b_s), "sgu_w_s": (sgu_w_s, m_sgu_w_s, v_sgu_w_s)}
    order = list(small_g)

    def flat(arrs):
        f = jnp.concatenate([a.reshape(-1) for a in arrs])
        return jnp.pad(f, (0, _FLAT_ROWS * 128 - f.shape[0])).reshape(_FLAT_ROWS, 128)

    fg = flat([small_g[k] for k in order])
    fw, fm, fv = (flat([small_w[k][q] for k in order]) for q in range(3))
    flat_out = _adamw("adamw_small", fg.reshape(1, _FLAT_ROWS, 128), fw, fm, fv)
    o_small = {}
    off = 0
    for k in order:
        shape = small_w[k][0].shape
        size = 1
        for dim in shape:
            size *= dim
        o_small[k] = [a.reshape(-1)[off:off + size].reshape(shape) for a in flat_out]
        off += size

    loss = lax.psum(loss_acc[0, 0], ("x", "y", "c"))

    per_weight = {"norm_mix_g": o_small["norm_mix_g"], "w_in": o_in, "conv_w": o_small["conv_w"],
                  "conv_b": o_small["conv_b"], "w_rgate": o_wr, "b_rgate": o_small["b_rgate"], "w_igate": o_wi,
                  "b_igate": o_small["b_igate"], "lru_lambda": o_small["lru_lambda"], "w_out_a": o_oa,
                  "sgu_ln_g": o_small["sgu_ln_g"], "sgu_ln_b": o_small["sgu_ln_b"], "sgu_w_s": o_small["sgu_w_s"],
                  "sgu_b_s": o_small["sgu_b_s"], "w_out_b": o_ob, "w_out": o_out, "norm_mlp_g": o_small["norm_mlp_g"],
                  "w_up": o_up, "w_down": o_down, "norm_final_g": o_small["norm_final_g"]}
    names_w = list(per_weight)
    return (loss, dx[None], *[per_weight[k][0] for k in names_w], *[per_weight[k][1] for k in names_w],
            *[per_weight[k][2] for k in names_w], *[per_weight[k][3] for k in names_w])
```

```python
import jax
import jax.numpy as jnp
from jax import lax
from jax.experimental import pallas as pl
from jax.experimental.pallas import tpu as pltpu

F32 = jnp.float32
BF16 = jnp.bfloat16
SDS = jax.ShapeDtypeStruct
MESH = pl.DeviceIdType.MESH
ANY = pl.BlockSpec(memory_space=pl.ANY)

D = 1024
N_SLOT = 8
W_IN_COLS = 768
FF_COLS = 512
HEADS, HEAD_DIM = 4, 256
GROUPS, GROUP_DIM = 4, 256
CHUNK = 128
CONV_K = 4
NORM_EPS = 1e-6
LN_EPS = 1e-5
LRU_C = 8.0
ADAM_LR, ADAM_B1, ADAM_B2, ADAM_EPS, ADAM_WD, ADAM_STEP = 0.001, 0.9, 0.999, 1e-08, 0.01, 10

TM_ROWS = 1024
TM_MERGE = 512
T_BRANCH = 256
MiB = 1024 * 1024

_GELU_C = 0.7978845608028654
_GELU_A = 0.044715


def _cparams(sem, vmem_mib):
    return pltpu.CompilerParams(dimension_semantics=sem, vmem_limit_bytes=vmem_mib * MiB)


def _gelu(x):
    t = jnp.tanh(_GELU_C * (x + _GELU_A * x * x * x))
    return 0.5 * x * (1.0 + t)


def _gelu_and_grad(x):
    x2 = x * x
    t = jnp.tanh(_GELU_C * x * (1.0 + _GELU_A * x2))
    g = 0.5 * x * (1.0 + t)
    dg = 0.5 * (1.0 + t) + 0.5 * x * (1.0 - t * t) * _GELU_C * (1.0 + 3.0 * _GELU_A * x2)
    return g, dg


def _softplus(x):
    return jnp.maximum(x, 0.0) + jnp.log1p(jnp.exp(-jnp.abs(x)))


def _dot(a, b):
    return jnp.dot(a, b, preferred_element_type=F32)


def _dot_nt(a, b):
    return lax.dot_general(a, b, (((1,), (1,)), ((), ())), preferred_element_type=F32)


def _dot_tn(a, b):
    return lax.dot_general(a, b, (((0,), (0,)), ((), ())), preferred_element_type=F32)


def _rows_shifted(prev8, cur, k):
    ext = jnp.concatenate([prev8, cur], axis=0)
    return pltpu.roll(ext, k, 0)[8:]


def _rows_advanced(cur, next8, k):
    t = cur.shape[0]
    ext = jnp.concatenate([cur, next8], axis=0)
    return pltpu.roll(ext, t + 8 - k, 0)[:t]


def _in_proj(x, g_mix, w_in_g, comm=None):
    s = x.shape[0]
    tm = min(TM_ROWS, s)

    def body(x_ref, g_ref, w_ref, z_ref, n_ref, n_s):
        @pl.when(pl.program_id(1) == 0)
        def _():
            xv = x_ref[...]
            rstd = lax.rsqrt(jnp.mean(xv * xv, axis=-1, keepdims=True) + NORM_EPS)
            nb = (xv * rstd * g_ref[...]).astype(BF16)
            n_s[...] = nb
            n_ref[...] = nb

        z_ref[...] = _dot(n_s[...], w_ref[...]).astype(BF16)

    return _call(
        body, name="in_proj", grid=(s // tm, N_SLOT),
        in_specs=[pl.BlockSpec((tm, D), lambda i, j: (i, 0)),
                  pl.BlockSpec((1, D), lambda i, j: (0, 0)),
                  pl.BlockSpec((None, D, W_IN_COLS), lambda i, j: (j, 0, 0))],
        out_specs=[pl.BlockSpec((tm, W_IN_COLS), lambda i, j: (i, j)),
                   pl.BlockSpec((tm, D), lambda i, j: (i, 0))],
        out_shape=[SDS((s, N_SLOT * W_IN_COLS), BF16), SDS((s, D), BF16)],
        scratch_shapes=[pltpu.VMEM((tm, D), BF16)],
        params=_cparams(("arbitrary", "arbitrary"), 40), args=(x, g_mix, w_in_g), comm=comm)


def _lru_gates(xc, xcb, wr_ref, br, wi_ref, bi, sp_lam, a_s, b_s, r_s=None, i_s=None, m_s=None):
    for h in range(HEADS):
        sl = slice(h * HEAD_DIM, (h + 1) * HEAD_DIM)
        r = jax.nn.sigmoid(_dot(xcb[:, sl], wr_ref[h]) + br[:, sl])
        ig = jax.nn.sigmoid(_dot(xcb[:, sl], wi_ref[h]) + bi[:, sl])
        log_a = (-LRU_C) * r * sp_lam[:, sl]
        a = jnp.exp(log_a)
        mult = jnp.sqrt(-jnp.tanh(log_a) * (a * a + 1.0))
        a_s[:, sl] = a
        b_s[:, sl] = xc[:, sl] * ig * mult
        if r_s is not None:
            r_s[:, sl] = r
            i_s[:, sl] = ig
            m_s[:, sl] = mult


def _conv_fwd(xa, prev8, cw, cb):
    xc = cb + cw[0:1, :] * xa
    for k in range(1, CONV_K):
        xc = xc + cw[k:k + 1, :] * _rows_shifted(prev8, xa, k)
    return xc


def _branch_a_fwd(z, conv_w, conv_b, w_r, b_r, w_i, b_i, lam, comm=None):
    s = z.shape[0]
    ta = min(T_BRANCH, s)
    per16 = ta // 16

    def body(xa_ref, xp_ref, ga_ref, cw_ref, cb_ref, wr_ref, br_ref, wi_ref, bi_ref, lam_ref,
             ya_ref, hs_ref, a_s, b_s, h_s, carry_s):
        i = pl.program_id(0)

        @pl.when(i == 0)
        def _():
            carry_s[...] = jnp.zeros_like(carry_s)

        xa = xa_ref[...].astype(F32)
        prev8 = jnp.where(i > 0, xp_ref[...].astype(F32)[8:16], 0.0)
        xc = _conv_fwd(xa, prev8, cw_ref[...], cb_ref[...])
        sp_lam = _softplus(-lam_ref[...])
        _lru_gates(xc, xc.astype(BF16), wr_ref, br_ref[...], wi_ref, bi_ref[...], sp_lam, a_s, b_s)

        row = lax.broadcasted_iota(jnp.int32, (8, D), 0)

        def group(g, carry):
            off = pl.multiple_of(g * 8, 8)
            a8 = a_s[pl.ds(off, 8), :]
            b8 = b_s[pl.ds(off, 8), :]
            for d in (1, 2, 4):
                a_sh = jnp.where(row >= d, pltpu.roll(a8, d, 0), 1.0)
                b_sh = jnp.where(row >= d, pltpu.roll(b8, d, 0), 0.0)
                b8 = a8 * b_sh + b8
                a8 = a8 * a_sh
            h8 = b8 + a8 * carry
            h_s[pl.ds(off, 8), :] = h8
            return jnp.broadcast_to(h8[7:8, :], (8, D))

        carry_s[...] = lax.fori_loop(0, ta // 8, group, carry_s[...])
        hs = h_s[...]
        hs_ref[...] = hs.astype(BF16)
        ya_ref[...] = (hs * _gelu(ga_ref[...].astype(F32))).astype(BF16)

    vec = pl.BlockSpec((1, D), lambda i: (0, 0))
    gate = pl.BlockSpec((HEADS, HEAD_DIM, HEAD_DIM), lambda i: (0, 0, 0))
    return _call(
        body, name="branch_a_fwd", grid=(s // ta,),
        in_specs=[pl.BlockSpec((ta, D), lambda i: (i, 0)),
                  pl.BlockSpec((16, D), lambda i: (jnp.maximum(i * per16 - 1, 0), 0)),
                  pl.BlockSpec((ta, D), lambda i: (i, 1)),
                  pl.BlockSpec((CONV_K, D), lambda i: (0, 0)), vec, gate, vec, gate, vec, vec],
        out_specs=[pl.BlockSpec((ta, D), lambda i: (i, 0)), pl.BlockSpec((ta, D), lambda i: (i, 0))],
        out_shape=[SDS((s, D), BF16), SDS((s, D), BF16)],
        scratch_shapes=[pltpu.VMEM((ta, D), F32), pltpu.VMEM((ta, D), F32), pltpu.VMEM((ta, D), F32),
                        pltpu.VMEM((8, D), F32)],
        params=_cparams(("arbitrary",), 40), args=(z, z, z, conv_w, conv_b, w_r, b_r, w_i, b_i, lam), comm=comm)


def _sgu_common(ub, vb, lg, lb, with_grad):
    if with_grad:
        u, du = _gelu_and_grad(ub)
        v, dv = _gelu_and_grad(vb)
    else:
        u, v, du, dv = _gelu(ub), _gelu(vb), None, None
    mu = jnp.mean(v, axis=-1, keepdims=True)
    vc = v - mu
    rstd = lax.rsqrt(jnp.mean(vc * vc, axis=-1, keepdims=True) + LN_EPS)
    vhat = vc * rstd
    vln = vhat * lg + lb
    return u, du, dv, rstd, vhat, vln


def _masked_ws(ws_ref):
    t = lax.broadcasted_iota(jnp.int32, (CHUNK, CHUNK), 0)
    c = lax.broadcasted_iota(jnp.int32, (CHUNK, CHUNK), 1)
    keep = c <= t
    return [jnp.where(keep, ws_ref[g], 0.0).astype(BF16) for g in range(GROUPS)]


def _branch_b_fwd(z, ln_g, ln_b, w_s, b_s_t):
    s = z.shape[0]
    tb = min(T_BRANCH, s)

    def body(ub_ref, vb_ref, lg_ref, lb_ref, ws_ref, bs_ref, yb_ref):
        u, _, _, _, _, vln = _sgu_common(ub_ref[...].astype(F32), vb_ref[...].astype(F32),
                                         lg_ref[...], lb_ref[...], False)
        vlnb = vln.astype(BF16)
        wm = _masked_ws(ws_ref)
        bs = bs_ref[...]
        for c in range(tb // CHUNK):
            rs = slice(c * CHUNK, (c + 1) * CHUNK)
            for g in range(GROUPS):
                cs = slice(g * GROUP_DIM, (g + 1) * GROUP_DIM)
                sp = _dot(wm[g], vlnb[rs, cs]) + bs[:, g:g + 1]
                yb_ref[rs, cs] = (u[rs, cs] * sp).astype(BF16)

    vec = pl.BlockSpec((1, D), lambda i: (0, 0))
    return pl.pallas_call(
        body, name="branch_b_fwd", grid=(s // tb,),
        in_specs=[pl.BlockSpec((tb, D), lambda i: (i, 2)), pl.BlockSpec((tb, D), lambda i: (i, 3)), vec, vec,
                  pl.BlockSpec((GROUPS, CHUNK, CHUNK), lambda i: (0, 0, 0)),
                  pl.BlockSpec((CHUNK, GROUPS), lambda i: (0, 0))],
        out_specs=pl.BlockSpec((tb, D), lambda i: (i, 0)),
        out_shape=SDS((s, D), BF16),
        compiler_params=_cparams(("arbitrary",), 40),
    )(z, z, ln_g, ln_b, w_s, b_s_t)


def _merge_out(ya, yb, z, x, w_oa, w_ob, w_out):
    s = x.shape[0]
    tm = min(TM_MERGE, s)

    def body(ya_ref, yb_ref, ma_ref, mb_ref, x_ref, woa_ref, wob_ref, wo_ref, pa_ref, pb_ref, mg_ref, h1_ref):
        pa = _dot(ya_ref[...], woa_ref[...])
        pb = _dot(yb_ref[...], wob_ref[...])
        merged = (jax.nn.sigmoid(ma_ref[...].astype(F32)) * pa
                  + jax.nn.sigmoid(mb_ref[...].astype(F32)) * pb).astype(BF16)
        pa_ref[...] = pa.astype(BF16)
        pb_ref[...] = pb.astype(BF16)
        mg_ref[...] = merged
        h1_ref[...] = x_ref[...] + _dot(merged, wo_ref[...])

    row = pl.BlockSpec((tm, D), lambda i: (i, 0))
    wsp = pl.BlockSpec((D, D), lambda i: (0, 0))
    return pl.pallas_call(
        body, name="merge_out", grid=(s // tm,),
        in_specs=[row, row, pl.BlockSpec((tm, D), lambda i: (i, 4)), pl.BlockSpec((tm, D), lambda i: (i, 5)),
                  row, wsp, wsp, wsp],
        out_specs=[row, row, row, row],
        out_shape=[SDS((s, D), BF16), SDS((s, D), BF16), SDS((s, D), BF16), SDS((s, D), F32)],
        compiler_params=_cparams(("arbitrary",), 48),
    )(ya, yb, z, z, x, w_oa, w_ob, w_out)


def _mlp_fwd(h1, g_mlp, w_up_g, w_down, g_fin, tgt):
    s = h1.shape[0]
    tm = min(TM_ROWS, s)
    nj = N_SLOT

    def body(h1_ref, gm_ref, wu_ref, wd_ref, gf_ref, t_ref, r_ref, n2_ref, dh2_ref, loss_ref, dgf_ref, n2_s, acc_s):
        i, j = pl.program_id(0), pl.program_id(1)

        @pl.when(j == 0)
        def _():
            hv = h1_ref[...]
            rstd = lax.rsqrt(jnp.mean(hv * hv, axis=-1, keepdims=True) + NORM_EPS)
            nb = (hv * rstd * gm_ref[...]).astype(BF16)
            n2_s[...] = nb
            n2_ref[...] = nb
            acc_s[...] = jnp.zeros_like(acc_s)

        @pl.when((i == 0) & (j == 0))
        def _():
            loss_ref[...] = jnp.zeros_like(loss_ref)
            dgf_ref[...] = jnp.zeros_like(dgf_ref)

        r = jnp.maximum(_dot(n2_s[...], wu_ref[...]), 0.0)
        r_ref[...] = r.astype(BF16)
        acc_s[...] += _dot((r * r).astype(BF16), wd_ref[...])

        @pl.when(j == nj - 1)
        def _():
            h2 = h1_ref[...] + acc_s[...]
            rstd = lax.rsqrt(jnp.mean(h2 * h2, axis=-1, keepdims=True) + NORM_EPS)
            hh = h2 * rstd
            gf = gf_ref[...]
            e = hh * gf - t_ref[...]
            loss_ref[...] += jnp.sum(e * e) * (0.5 / D)
            dy = e * (1.0 / D)
            dgf_ref[...] += jnp.sum(dy * hh, axis=0, keepdims=True)
            dhh = dy * gf
            dh2_ref[...] = rstd * (dhh - hh * jnp.mean(dhh * hh, axis=-1, keepdims=True))

    row = pl.BlockSpec((tm, D), lambda i, j: (i, 0))
    vec = pl.BlockSpec((1, D), lambda i, j: (0, 0))
    return pl.pallas_call(
        body, name="mlp_fwd", grid=(s // tm, nj),
        in_specs=[row, vec, pl.BlockSpec((None, D, FF_COLS), lambda i, j: (j, 0, 0)),
                  pl.BlockSpec((FF_COLS, D), lambda i, j: (j, 0)), vec, row],
        out_specs=[pl.BlockSpec((tm, FF_COLS), lambda i, j: (i, j)), row, row,
                   pl.BlockSpec((8, 128), lambda i, j: (0, 0)), vec],
        out_shape=[SDS((s, nj * FF_COLS), BF16), SDS((s, D), BF16), SDS((s, D), F32),
                   SDS((8, 128), F32), SDS((1, D), F32)],
        scratch_shapes=[pltpu.VMEM((tm, D), BF16), pltpu.VMEM((tm, D), F32)],
        compiler_params=_cparams(("arbitrary", "arbitrary"), 52),
    )(h1, g_mlp, w_up_g, w_down, g_fin, tgt)


def _mlp_bwd(dh2, r, w_down, w_up_g, h1, g_mlp, comm=None):
    s = h1.shape[0]
    tm = min(TM_ROWS, s)
    nj = N_SLOT

    def body(dh2_ref, r_ref, wd_ref, wu_ref, h1_ref, gm_ref, df_ref, dh1_ref, dgm_ref, dh2b_s, acc_s):
        i, j = pl.program_id(0), pl.program_id(1)

        @pl.when(j == 0)
        def _():
            dh2b_s[...] = dh2_ref[...].astype(BF16)
            acc_s[...] = jnp.zeros_like(acc_s)

        @pl.when((i == 0) & (j == 0))
        def _():
            dgm_ref[...] = jnp.zeros_like(dgm_ref)

        d_act = _dot_nt(dh2b_s[...], wd_ref[...])
        df = (d_act * (2.0 * r_ref[...].astype(F32))).astype(BF16)
        df_ref[...] = df
        acc_s[...] += _dot_nt(df, wu_ref[...])

        @pl.when(j == nj - 1)
        def _():
            hv = h1_ref[...]
            rstd = lax.rsqrt(jnp.mean(hv * hv, axis=-1, keepdims=True) + NORM_EPS)
            hh = hv * rstd
            dn2 = acc_s[...]
            dgm_ref[...] += jnp.sum(dn2 * hh, axis=0, keepdims=True)
            dhat = dn2 * gm_ref[...]
            dh1_ref[...] = dh2_ref[...] + rstd * (dhat - hh * jnp.mean(dhat * hh, axis=-1, keepdims=True))

    row = pl.BlockSpec((tm, D), lambda i, j: (i, 0))
    vec = pl.BlockSpec((1, D), lambda i, j: (0, 0))
    ffb = pl.BlockSpec((tm, FF_COLS), lambda i, j: (i, j))
    return _call(
        body, name="mlp_bwd", grid=(s // tm, nj),
        in_specs=[row, ffb, pl.BlockSpec((FF_COLS, D), lambda i, j: (j, 0)),
                  pl.BlockSpec((None, D, FF_COLS), lambda i, j: (j, 0, 0)), row, vec],
        out_specs=[ffb, row, vec],
        out_shape=[SDS((s, nj * FF_COLS), BF16), SDS((s, D), F32), SDS((1, D), F32)],
        scratch_shapes=[pltpu.VMEM((tm, D), BF16), pltpu.VMEM((tm, D), F32)],
        params=_cparams(("arbitrary", "arbitrary"), 52), args=(dh2, r, w_down, w_up_g, h1, g_mlp), comm=comm)


def _merge_bwd(dh1, z, pa, pb, w_out, w_oa, w_ob, comm=None):
    s = dh1.shape[0]
    tm = min(TM_MERGE, s)

    def body(dh1_ref, ma_ref, mb_ref, pa_ref, pb_ref, wo_ref, woa_ref, wob_ref,
             dz_ref, dpa_ref, dpb_ref, dya_ref, dyb_ref):
        dm = _dot_nt(dh1_ref[...].astype(BF16), wo_ref[...])
        sa = jax.nn.sigmoid(ma_ref[...].astype(F32))
        sb = jax.nn.sigmoid(mb_ref[...].astype(F32))
        dpa = (dm * sa).astype(BF16)
        dpb = (dm * sb).astype(BF16)
        dz_ref[:, 0:D] = (dm * pa_ref[...].astype(F32) * sa * (1.0 - sa)).astype(BF16)
        dz_ref[:, D:2 * D] = (dm * pb_ref[...].astype(F32) * sb * (1.0 - sb)).astype(BF16)
        dpa_ref[...] = dpa
        dpb_ref[...] = dpb
        dya_ref[...] = _dot_nt(dpa, woa_ref[...]).astype(BF16)
        dyb_ref[...] = _dot_nt(dpb, wob_ref[...]).astype(BF16)

    row = pl.BlockSpec((tm, D), lambda i: (i, 0))
    wsp = pl.BlockSpec((D, D), lambda i: (0, 0))
    return _call(
        body, name="merge_bwd", grid=(s // tm,),
        in_specs=[row, pl.BlockSpec((tm, D), lambda i: (i, 4)), pl.BlockSpec((tm, D), lambda i: (i, 5)),
                  row, row, wsp, wsp, wsp],
        out_specs=[pl.BlockSpec((tm, 2 * D), lambda i: (i, 2)), row, row, row, row],
        out_shape=[SDS((s, 6 * D), BF16)] + [SDS((s, D), BF16)] * 4, scratch_shapes=[],
        params=_cparams(("arbitrary",), 48), args=(dh1, z, z, pa, pb, w_out, w_oa, w_ob), comm=comm)


def _branch_b_bwd(dz, dyb, z, ln_g, ln_b, w_s, b_s_t, comm=None):
    s = z.shape[0]
    tb = min(T_BRANCH, s)

    def body(dz_in, dyb_ref, ub_ref, vb_ref, lg_ref, lb_ref, ws_ref, bs_ref,
             dz_ref, dws_ref, dbs_ref, dln_ref, du_s, dvln_s):
        del dz_in

        @pl.when(pl.program_id(0) == 0)
        def _():
            dws_ref[...] = jnp.zeros_like(dws_ref)
            dbs_ref[...] = jnp.zeros_like(dbs_ref)
            dln_ref[...] = jnp.zeros_like(dln_ref)

        lg = lg_ref[...]
        u, du, dv, rstd, vhat, vln = _sgu_common(ub_ref[...].astype(F32), vb_ref[...].astype(F32),
                                                 lg, lb_ref[...], True)
        vlnb = vln.astype(BF16)
        dyb_v = dyb_ref[...].astype(F32)
        wm = _masked_ws(ws_ref)
        bs = bs_ref[...]
        for c in range(tb // CHUNK):
            rs = slice(c * CHUNK, (c + 1) * CHUNK)
            for g in range(GROUPS):
                cs = slice(g * GROUP_DIM, (g + 1) * GROUP_DIM)
                v_blk = vlnb[rs, cs]
                sp = _dot(wm[g], v_blk) + bs[:, g:g + 1]
                d_sp = dyb_v[rs, cs] * u[rs, cs]
                d_spb = d_sp.astype(BF16)
                du_s[rs, cs] = dyb_v[rs, cs] * sp
                dvln_s[rs, cs] = _dot_tn(wm[g], d_spb)
                dws_ref[g] += _dot_nt(d_spb, v_blk)
                dbs_ref[g] += jnp.broadcast_to(jnp.sum(d_sp, axis=-1, keepdims=True), (CHUNK, CHUNK))
        dvln = dvln_s[...]
        dln_ref[0:1, :] += jnp.sum(dvln * vhat, axis=0, keepdims=True)
        dln_ref[1:2, :] += jnp.sum(dvln, axis=0, keepdims=True)
        dvh = dvln * lg
        d_v = rstd * (dvh - jnp.mean(dvh, axis=-1, keepdims=True)
                      - vhat * jnp.mean(dvh * vhat, axis=-1, keepdims=True))
        dz_ref[:, 0:D] = (du_s[...] * du).astype(BF16)
        dz_ref[:, D:2 * D] = (d_v * dv).astype(BF16)

    vec = pl.BlockSpec((1, D), lambda i: (0, 0))
    sq = pl.BlockSpec((GROUPS, CHUNK, CHUNK), lambda i: (0, 0, 0))
    return _call(
        body, name="branch_b_bwd", grid=(s // tb,),
        in_specs=[ANY, pl.BlockSpec((tb, D), lambda i: (i, 0)),
                  pl.BlockSpec((tb, D), lambda i: (i, 2)), pl.BlockSpec((tb, D), lambda i: (i, 3)), vec, vec, sq,
                  pl.BlockSpec((CHUNK, GROUPS), lambda i: (0, 0))],
        out_specs=[pl.BlockSpec((tb, 2 * D), lambda i: (i, 1)), sq, sq, pl.BlockSpec((8, D), lambda i: (0, 0))],
        out_shape=[SDS(dz.shape, BF16), SDS((GROUPS, CHUNK, CHUNK), F32), SDS((GROUPS, CHUNK, CHUNK), F32),
                   SDS((8, D), F32)],
        scratch_shapes=[pltpu.VMEM((tb, D), F32), pltpu.VMEM((tb, D), F32)], aliases={0: 0},
        params=_cparams(("arbitrary",), 40), args=(dz, dyb, z, z, ln_g, ln_b, w_s, b_s_t), comm=comm)


def _branch_a_bwd(dz, dya, z, hs, conv_w, conv_b, w_r, b_r, w_i, b_i, lam, comm=None):
    s = z.shape[0]
    ta = min(T_BRANCH, s)
    nb = s // ta
    per16 = ta // 16

    def body(dz_in, dya_ref, xa_ref, xp_ref, ga_ref, hs_ref, hp_ref, cw_ref, cb_ref, wr_ref, br_ref, wi_ref,
             bi_ref, lam_ref, dz_ref, vec_ref, dwr_ref, dwi_ref,
             a_s, b_s, h_s, r_s, i_s, m_s, dcar_s, acar_s, dxc_s):
        del dz_in
        i = pl.program_id(0)
        blk = nb - 1 - i

        @pl.when(i == 0)
        def _():
            dcar_s[...] = jnp.zeros_like(dcar_s)
            acar_s[...] = jnp.zeros_like(acar_s)
            dxc_s[...] = jnp.zeros_like(dxc_s)
            vec_ref[...] = jnp.zeros_like(vec_ref)
            dwr_ref[...] = jnp.zeros_like(dwr_ref)
            dwi_ref[...] = jnp.zeros_like(dwi_ref)

        cw = cw_ref[...]
        lam_v = lam_ref[...]
        xa = xa_ref[...].astype(F32)
        prev8 = jnp.where(blk > 0, xp_ref[...].astype(F32)[8:16], 0.0)
        xc = _conv_fwd(xa, prev8, cw, cb_ref[...])
        xcb = xc.astype(BF16)
        sp_lam = _softplus(-lam_v)
        _lru_gates(xc, xcb, wr_ref, br_ref[...], wi_ref, bi_ref[...], sp_lam, a_s, b_s, r_s, i_s, m_s)

        hs_v = hs_ref[...].astype(F32)
        hprev8 = jnp.where(blk > 0, hp_ref[...].astype(F32)[8:16], 0.0)
        h_m1 = _rows_shifted(hprev8, hs_v, 1)
        gg, dgg = _gelu_and_grad(ga_ref[...].astype(F32))
        dya_v = dya_ref[...].astype(F32)
        dz_ref[:, D:2 * D] = (dya_v * hs_v * dgg).astype(BF16)

        a_v = a_s[...]
        a_s[...] = _rows_advanced(a_v, acar_s[...], 1)
        b_s[...] = dya_v * gg

        row = lax.broadcasted_iota(jnp.int32, (8, D), 0)
        ng = ta // 8

        def group(gi, carry):
            off = pl.multiple_of((ng - 1 - gi) * 8, 8)
            c8 = a_s[pl.ds(off, 8), :]
            d8 = b_s[pl.ds(off, 8), :]
            for d in (1, 2, 4):
                c_sh = jnp.where(row < 8 - d, pltpu.roll(c8, 8 - d, 0), 1.0)
                d_sh = jnp.where(row < 8 - d, pltpu.roll(d8, 8 - d, 0), 0.0)
                d8 = c8 * d_sh + d8
                c8 = c8 * c_sh
            dh8 = d8 + c8 * carry
            h_s[pl.ds(off, 8), :] = dh8
            return jnp.broadcast_to(dh8[0:1, :], (8, D))

        dcar_s[...] = lax.fori_loop(0, ng, group, dcar_s[...])
        acar_s[...] = jnp.broadcast_to(a_v[0:1, :], (8, D))

        dbx = h_s[...]
        r_v, i_v, m_v = r_s[...], i_s[...], m_s[...]
        d_mult = dbx * xc * i_v
        d_loga = dbx * h_m1 * a_v - d_mult * (a_v * a_v) / m_v
        d_pr = d_loga * ((-LRU_C) * sp_lam) * r_v * (1.0 - r_v)
        d_pi = dbx * xc * m_v * i_v * (1.0 - i_v)
        vec_ref[7:8, :] += jnp.sum(d_loga * r_v, axis=0, keepdims=True) * (LRU_C * jax.nn.sigmoid(-lam_v))
        vec_ref[5:6, :] += jnp.sum(d_pr, axis=0, keepdims=True)
        vec_ref[6:7, :] += jnp.sum(d_pi, axis=0, keepdims=True)
        d_prb = d_pr.astype(BF16)
        d_pib = d_pi.astype(BF16)
        h_s[...] = dbx * i_v * m_v
        for h in range(HEADS):
            sl = slice(h * HEAD_DIM, (h + 1) * HEAD_DIM)
            h_s[:, sl] += _dot_nt(d_prb[:, sl], wr_ref[h]) + _dot_nt(d_pib[:, sl], wi_ref[h])
            dwr_ref[h] += _dot_tn(xcb[:, sl], d_prb[:, sl])
            dwi_ref[h] += _dot_tn(xcb[:, sl], d_pib[:, sl])
        d_xc = h_s[...]
        vec_ref[4:5, :] += jnp.sum(d_xc, axis=0, keepdims=True)
        vec_ref[0:1, :] += jnp.sum(d_xc * xa, axis=0, keepdims=True)
        d_xa = cw[0:1, :] * d_xc
        nxt = dxc_s[...]
        for k in range(1, CONV_K):
            vec_ref[k:k + 1, :] += jnp.sum(d_xc * _rows_shifted(prev8, xa, k), axis=0, keepdims=True)
            d_xa = d_xa + cw[k:k + 1, :] * _rows_advanced(d_xc, nxt, k)
        dz_ref[:, 0:D] = d_xa.astype(BF16)
        dxc_s[...] = d_xc[0:8, :]

    vec = pl.BlockSpec((1, D), lambda i: (0, 0))
    gate = pl.BlockSpec((HEADS, HEAD_DIM, HEAD_DIM), lambda i: (0, 0, 0))
    cur = lambda c: pl.BlockSpec((ta, D), lambda i: (nb - 1 - i, c))
    before = lambda c: pl.BlockSpec((16, D), lambda i: (jnp.maximum((nb - 1 - i) * per16 - 1, 0), c))
    return _call(
        body, name="branch_a_bwd", grid=(nb,),
        in_specs=[ANY, cur(0), cur(0), before(0), cur(1), cur(0), before(0),
                  pl.BlockSpec((CONV_K, D), lambda i: (0, 0)), vec, gate, vec, gate, vec, vec],
        out_specs=[pl.BlockSpec((ta, 2 * D), lambda i: (nb - 1 - i, 0)), pl.BlockSpec((8, D), lambda i: (0, 0)),
                   gate, gate],
        out_shape=[SDS(dz.shape, BF16), SDS((8, D), F32), SDS((HEADS, HEAD_DIM, HEAD_DIM), F32),
                   SDS((HEADS, HEAD_DIM, HEAD_DIM), F32)],
        scratch_shapes=[pltpu.VMEM((ta, D), F32)] * 6 + [pltpu.VMEM((8, D), F32)] * 3, aliases={0: 0},
        params=_cparams(("arbitrary",), 48),
        args=(dz, dya, z, z, z, hs, hs, conv_w, conv_b, w_r, b_r, w_i, b_i, lam), comm=comm)


def _in_bwd(dz, w_in_g, x, dh1, g_mix, comm=None):
    s = x.shape[0]
    tm = min(TM_ROWS, s)
    nj = N_SLOT

    def body(dz_ref, w_ref, x_ref, dh1_ref, g_ref, dx_ref, dg_ref, acc_s):
        i, j = pl.program_id(0), pl.program_id(1)

        @pl.when(j == 0)
        def _():
            acc_s[...] = jnp.zeros_like(acc_s)

        @pl.when((i == 0) & (j == 0))
        def _():
            dg_ref[...] = jnp.zeros_like(dg_ref)

        acc_s[...] += _dot_nt(dz_ref[...], w_ref[...])

        @pl.when(j == nj - 1)
        def _():
            xv = x_ref[...]
            rstd = lax.rsqrt(jnp.mean(xv * xv, axis=-1, keepdims=True) + NORM_EPS)
            xh = xv * rstd
            dn = acc_s[...]
            dg_ref[...] += jnp.sum(dn * xh, axis=0, keepdims=True)
            dhat = dn * g_ref[...]
            dx_ref[...] = dh1_ref[...] + rstd * (dhat - xh * jnp.mean(dhat * xh, axis=-1, keepdims=True))

    row = pl.BlockSpec((tm, D), lambda i, j: (i, 0))
    vec = pl.BlockSpec((1, D), lambda i, j: (0, 0))
    return _call(
        body, name="in_bwd", grid=(s // tm, nj),
        in_specs=[pl.BlockSpec((tm, W_IN_COLS), lambda i, j: (i, j)),
                  pl.BlockSpec((None, D, W_IN_COLS), lambda i, j: (j, 0, 0)), row, row, vec],
        out_specs=[row, vec],
        out_shape=[SDS((s, D), F32), SDS((1, D), F32)],
        scratch_shapes=[pltpu.VMEM((tm, D), F32)],
        params=_cparams(("arbitrary", "arbitrary"), 48), args=(dz, w_in_g, x, dh1, g_mix), comm=comm)


def _wgrad(name, a, b, nblk, a_split, b_split, square_a=False):
    s = a.shape[0]
    ts = min(TM_ROWS, s)
    a_w = a.shape[1] // nblk if a_split else a.shape[1]
    b_w = b.shape[1] // nblk if b_split else b.shape[1]

    def body(a_ref, b_ref, o_ref, acc_s):
        t = pl.program_id(1)

        @pl.when(t == 0)
        def _():
            acc_s[...] = jnp.zeros_like(acc_s)

        av = a_ref[...]
        if square_a:
            av = av * av
        acc_s[...] += _dot_tn(av.astype(BF16), b_ref[...].astype(BF16))

        @pl.when(t == pl.num_programs(1) - 1)
        def _():
            o_ref[...] = acc_s[...].astype(BF16)

    return pl.pallas_call(
        body, name=name, grid=(nblk, s // ts),
        in_specs=[pl.BlockSpec((ts, a_w), (lambda k, t: (t, k)) if a_split else (lambda k, t: (t, 0))),
                  pl.BlockSpec((ts, b_w), (lambda k, t: (t, k)) if b_split else (lambda k, t: (t, 0)))],
        out_specs=pl.BlockSpec((None, a_w, b_w), lambda k, t: (k, 0, 0)),
        out_shape=SDS((nblk, a_w, b_w), BF16),
        scratch_shapes=[pltpu.VMEM((a_w, b_w), F32)],
        compiler_params=_cparams(("arbitrary", "arbitrary"), 48),
    )(a, b)


def _place():
    x, y, c = lax.axis_index("x"), lax.axis_index("y"), lax.axis_index("c")
    return x, y, c


def _other_chips(x, y):
    return [(x, 1 - y, 2 * x + 1 - y), (1 - x, y, 2 * (1 - x) + y), (1 - x, 1 - y, 2 * (1 - x) + 1 - y)]


class _Plan:
    def __init__(self, arrays, out_shape, sems, start, finish):
        self.arrays, self.out_shape, self.sems, self.start, self.finish = arrays, out_shape, sems, start, finish


def _gather_plan(shards):
    n = len(shards)

    def copies(ins, outs, sems):
        send_sems, recv_sems, local_sems = sems
        x, y, c = _place()
        chip = 2 * x + y
        me = 2 * chip + c
        sib = (x, y, 1 - c)
        chips = _other_chips(x, y)

        def rc(k, t, src, blk, to):
            return pltpu.make_async_remote_copy(
                src_ref=src, dst_ref=outs[t].at[blk], send_sem=send_sems.at[k * n + t],
                recv_sem=recv_sems.at[k * n + t], device_id=to, device_id_type=MESH)

        local = [pltpu.make_async_copy(ins[t], outs[t].at[me], local_sems.at[t]) for t in range(n)]
        sends = [rc(0, t, ins[t], me, sib) for t in range(n)]
        for j, (px, py, _) in enumerate(chips):
            sends += [rc(1 + j, t, ins[t], me, (px, py, c)) for t in range(n)]
        return rc, local, sends, chips, chip, c, sib

    def start(ins, outs, sems):
        _, local, sends, _, _, _, _ = copies(ins, outs, sems)
        for cp in local + sends:
            cp.start()

    def finish(ins, outs, sems):
        rc, local, sends, chips, chip, c, sib = copies(ins, outs, sems)
        passed = []
        for j, (px, py, pc) in enumerate(chips):
            blk = 2 * pc + c
            for t in range(n):
                rc(1 + j, t, ins[t], blk, sib).wait_recv()
            for t in range(n):
                cp = rc(4 + j, t, outs[t].at[blk], blk, sib)
                cp.start()
                passed.append(cp)
        for t in range(n):
            rc(0, t, ins[t], 2 * chip + 1 - c, sib).wait_recv()
        for j, (px, py, pc) in enumerate(chips):
            for t in range(n):
                rc(4 + j, t, ins[t], 2 * pc + 1 - c, sib).wait_recv()
        for cp in sends + passed:
            cp.wait_send()
        for cp in local:
            cp.wait()

    return _Plan(list(shards), [SDS((N_SLOT,) + tuple(a.shape), a.dtype) for a in shards],
                 [pltpu.SemaphoreType.DMA((7 * n,)), pltpu.SemaphoreType.DMA((7 * n,)),
                  pltpu.SemaphoreType.DMA((n,))], start, finish)


def _sibling_plan(grads, whole=()):
    n, m = len(grads), len(whole)

    def copies(ins, outs, sems):
        send_sems, recv_sems = sems
        x, y, c = _place()
        sib = (x, y, 1 - c)

        def rc(t, src, dst):
            return pltpu.make_async_remote_copy(src_ref=src, dst_ref=dst, send_sem=send_sems.at[t],
                                                recv_sem=recv_sems.at[t], device_id=sib, device_id_type=MESH)
        return rc, c

    def start(ins, outs, sems):
        rc, c = copies(ins, outs, sems)
        for t in range(n):
            for j in range(4):
                rc(t, ins[t].at[2 * j + 1 - c], outs[t].at[j]).start()
        for t in range(n, n + m):
            rc(t, ins[t], outs[t]).start()

    def finish(ins, outs, sems):
        rc, _ = copies(ins, outs, sems)
        for t in range(n):
            rc(t, ins[t].at[pl.ds(0, 4)], outs[t]).wait()
        for t in range(n, n + m):
            rc(t, ins[t], outs[t]).wait()

    return _Plan(list(grads) + list(whole),
                 [SDS((4,) + tuple(g.shape[1:]), g.dtype) for g in grads] + [SDS(a.shape, a.dtype) for a in whole],
                 [pltpu.SemaphoreType.DMA((n + m,)), pltpu.SemaphoreType.DMA((n + m,))], start, finish)


def _chips_plan(parts, whole=()):
    n, m = len(parts), len(whole)

    def src_of(ins, t, pc):
        return ins[t].at[pc] if t < n else ins[t]

    def local_copies(ins, outs, sems, chip):
        return [pltpu.make_async_copy(src_of(ins, t, chip), outs[t].at[chip], sems[2].at[t]) for t in range(n + m)]

    def start(ins, outs, sems):
        send_sems, recv_sems, _ = sems
        x, y, c = _place()
        chip = 2 * x + y
        for cp in local_copies(ins, outs, sems, chip):
            cp.start()
        for px, py, pc in _other_chips(x, y):
            for t in range(n + m):
                pltpu.make_async_remote_copy(src_ref=src_of(ins, t, pc), dst_ref=outs[t].at[chip],
                                             send_sem=send_sems.at[t], recv_sem=recv_sems.at[t],
                                             device_id=(px, py, c), device_id_type=MESH).start()

    def finish(ins, outs, sems):
        send_sems, recv_sems, _ = sems
        x, y, c = _place()
        for t in range(n + m):
            three = outs[t].at[pl.ds(0, 3)]
            pltpu.make_async_remote_copy(src_ref=three, dst_ref=three, send_sem=send_sems.at[t],
                                         recv_sem=recv_sems.at[t], device_id=(x, y, c), device_id_type=MESH).wait()
        for cp in local_copies(ins, outs, sems, 2 * x + y):
            cp.wait()

    return _Plan(list(parts) + list(whole),
                 [SDS(p.shape, p.dtype) for p in parts] + [SDS((4,) + tuple(a.shape), a.dtype) for a in whole],
                 [pltpu.SemaphoreType.DMA((n + m,)), pltpu.SemaphoreType.DMA((n + m,)),
                  pltpu.SemaphoreType.DMA((n + m,))], start, finish)


def _run_plan(name, plan):
    k = len(plan.arrays)

    def body(*refs):
        ins, outs, sems = refs[:k], refs[k:2 * k], refs[2 * k:]
        plan.start(ins, outs, sems)
        plan.finish(ins, outs, sems)

    return pl.pallas_call(
        body, name=name, in_specs=[ANY] * k, out_specs=[ANY] * k, out_shape=plan.out_shape,
        scratch_shapes=plan.sems, compiler_params=pltpu.CompilerParams(has_side_effects=True),
    )(*plan.arrays)


def _call(body, *, name, grid, in_specs, out_specs, out_shape, scratch_shapes, params, args, comm=None,
          aliases=None):
    aliases = aliases or {}
    if comm is None:
        res = pl.pallas_call(body, name=name, grid=grid, in_specs=in_specs, out_specs=out_specs,
                             out_shape=out_shape, scratch_shapes=scratch_shapes, compiler_params=params,
                             input_output_aliases=aliases)(*args)
        return list(res), []
    n_in, n_out, n_scr, k = len(in_specs), len(out_specs), len(scratch_shapes), len(comm.arrays)

    def wrapped(*refs):
        ins = refs[:n_in]
        c_in = refs[n_in:n_in + k]
        outs = refs[n_in + k:n_in + k + n_out]
        c_out = refs[n_in + k + n_out:n_in + 2 * k + n_out]
        scr = refs[n_in + 2 * k + n_out:n_in + 2 * k + n_out + n_scr]
        sems = refs[n_in + 2 * k + n_out + n_scr:]
        ids = [pl.program_id(d) for d in range(len(grid))]
        first = ids[0] == 0
        last = ids[0] == grid[0] - 1
        for d in range(1, len(grid)):
            first = first & (ids[d] == 0)
            last = last & (ids[d] == grid[d] - 1)

        @pl.when(first)
        def _():
            comm.start(c_in, c_out, sems)

        body(*ins, *outs, *scr)

        @pl.when(last)
        def _():
            comm.finish(c_in, c_out, sems)

    res = pl.pallas_call(
        wrapped, name=name, grid=grid, in_specs=list(in_specs) + [ANY] * k, out_specs=list(out_specs) + [ANY] * k,
        out_shape=list(out_shape) + list(comm.out_shape), scratch_shapes=list(scratch_shapes) + list(comm.sems),
        compiler_params=params, input_output_aliases=aliases)(*args, *comm.arrays)
    return list(res[:n_out]), list(res[n_out:])


def _row_tile(rows):
    for t in (512, 256, 128, 64, 32, 16, 8):
        if rows % t == 0:
            return t
    return rows


def _pair_sum(name, g8, recv4, core):
    _, rows, cols = recv4.shape
    tr = _row_tile(rows)
    g42 = g8.reshape(4, 2, rows, cols)

    def body(c_ref, g_ref, r_ref, o_ref):
        del c_ref
        o_ref[...] = (g_ref[...].astype(F32) + r_ref[...].astype(F32)).astype(o_ref.dtype)

    return pl.pallas_call(
        body, name=name,
        grid_spec=pltpu.PrefetchScalarGridSpec(
            num_scalar_prefetch=1, grid=(4, rows // tr),
            in_specs=[pl.BlockSpec((None, None, tr, cols), lambda j, i, c_ref: (j, c_ref[0], i, 0)),
                      pl.BlockSpec((None, tr, cols), lambda j, i, c_ref: (j, i, 0))],
            out_specs=pl.BlockSpec((None, tr, cols), lambda j, i, c_ref: (j, i, 0))),
        out_shape=SDS(recv4.shape, g8.dtype),
        compiler_params=_cparams(("arbitrary", "arbitrary"), 32),
    )(core, g42, recv4)


def _add2(name, a, b):
    rows, cols = a.shape
    tr = _row_tile(rows)

    def body(a_ref, b_ref, o_ref):
        o_ref[...] = a_ref[...] + b_ref[...]

    blk = pl.BlockSpec((tr, cols), lambda i: (i, 0))
    return pl.pallas_call(body, name=name, grid=(rows // tr,), in_specs=[blk, blk], out_specs=blk,
                          out_shape=SDS(a.shape, a.dtype),
                          compiler_params=_cparams(("arbitrary",), 32))(a, b)


def _sum4(name, r4):
    _, rows, cols = r4.shape
    tr = _row_tile(rows)

    def body(r_ref, o_ref):
        o_ref[...] = ((r_ref[0] + r_ref[1]) + r_ref[2]) + r_ref[3]

    return pl.pallas_call(body, name=name, grid=(rows // tr,),
                          in_specs=[pl.BlockSpec((4, tr, cols), lambda i: (0, i, 0))],
                          out_specs=pl.BlockSpec((tr, cols), lambda i: (i, 0)),
                          out_shape=SDS((rows, cols), r4.dtype),
                          compiler_params=_cparams(("arbitrary",), 32))(r4)


def _adamw(name, terms, w, m, v):
    k, rows, cols = terms.shape
    tr = _row_tile(rows)
    c1 = 1.0 / (1.0 - ADAM_B1 ** ADAM_STEP)
    c2 = 1.0 / (1.0 - ADAM_B2 ** ADAM_STEP)

    def body(t_ref, w_ref, m_ref, v_ref, g_ref, d_ref, mo_ref, vo_ref):
        g = t_ref[0].astype(F32)
        for q in range(1, k):
            g = g + t_ref[q].astype(F32)
        mn = ADAM_B1 * m_ref[...] + (1.0 - ADAM_B1) * g
        vn = ADAM_B2 * v_ref[...] + (1.0 - ADAM_B2) * (g * g)
        g_ref[...] = g
        mo_ref[...] = mn
        vo_ref[...] = vn
        d_ref[...] = (-ADAM_LR) * ((mn * c1) / (jnp.sqrt(vn * c2) + ADAM_EPS) + ADAM_WD * w_ref[...])

    blk = pl.BlockSpec((tr, cols), lambda i: (i, 0))
    return pl.pallas_call(body, name=name, grid=(rows // tr,),
                          in_specs=[pl.BlockSpec((k, tr, cols), lambda i: (0, i, 0)), blk, blk, blk],
                          out_specs=[blk] * 4, out_shape=[SDS((rows, cols), F32)] * 4,
                          compiler_params=_cparams(("arbitrary",), 40))(terms, w, m, v)


_SMALL_ROWS = 80
_FLAT_ROWS = 640


def kernel(x, norm_mix_g, w_in, conv_w, conv_b, w_rgate, b_rgate, w_igate, b_igate, lru_lambda, w_out_a, sgu_ln_g, sgu_ln_b, sgu_w_s, sgu_b_s, w_out_b, w_out, norm_mlp_g, w_up, w_down, norm_final_g, loss_target, m_norm_mix_g, m_w_in, m_conv_w, m_conv_b, m_w_rgate, m_b_rgate, m_w_igate, m_b_igate, m_lru_lambda, m_w_out_a, m_sgu_ln_g, m_sgu_ln_b, m_sgu_w_s, m_sgu_b_s, m_w_out_b, m_w_out, m_norm_mlp_g, m_w_up, m_w_down, m_norm_final_g, v_norm_mix_g, v_w_in, v_conv_w, v_conv_b, v_w_rgate, v_b_rgate, v_w_igate, v_b_igate, v_lru_lambda, v_w_out_a, v_sgu_ln_g, v_sgu_ln_b, v_sgu_w_s, v_sgu_b_s, v_w_out_b, v_w_out, v_norm_mlp_g, v_w_up, v_w_down, v_norm_final_g):
    cx, cy, cc = _place()
    me = 4 * cx + 2 * cy + cc
    core = jnp.reshape(cc, (1,)).astype(jnp.int32)
    xs = x[0]
    tgt = loss_target[0]
    s = xs.shape[0]

    gate_shard = jnp.stack([w_rgate[0], w_igate[0]]).astype(BF16).reshape(2 * HEADS * 32, HEAD_DIM)
    vec_shard = jnp.concatenate([conv_w[0], b_rgate[0], b_igate[0]], axis=1)
    vec_shard = jnp.pad(vec_shard, ((0, 4), (0, 256 - vec_shard.shape[1])))
    shards = [w_in[0].astype(BF16), w_out_a[0].astype(BF16), w_out_b[0].astype(BF16), w_out[0].astype(BF16),
              w_up[0].astype(BF16), w_down[0].astype(BF16), gate_shard, vec_shard]
    w_in_g, gate_g, vec_g = _run_plan("gather_first", _gather_plan([shards[0], shards[6], shards[7]]))
    gates = gate_g.reshape(N_SLOT, 2, HEADS, 32, HEAD_DIM).transpose(1, 2, 0, 3, 4).reshape(2, HEADS, HEAD_DIM, HEAD_DIM)
    w_r_f, w_i_f = gates[0], gates[1]
    conv_w_f = vec_g[:, 0:4, 0:128].transpose(1, 0, 2).reshape(CONV_K, D)
    b_r_f = vec_g[:, 0:4, 128:160].transpose(1, 0, 2).reshape(1, D)
    b_i_f = vec_g[:, 0:4, 160:192].transpose(1, 0, 2).reshape(1, D)
    b_s_t = jnp.transpose(sgu_b_s[0])

    (z, n1), (w_oa_g, w_ob_g, w_out_g, w_up_g) = _in_proj(xs, norm_mix_g, w_in_g, comm=_gather_plan(shards[1:5]))
    (ya, hs), (w_down_g,) = _branch_a_fwd(z, conv_w_f, conv_b, w_r_f, b_r_f, w_i_f, b_i_f, lru_lambda,
                                          comm=_gather_plan(shards[5:6]))
    w_oa_f = w_oa_g.reshape(D, D)
    w_ob_f = w_ob_g.reshape(D, D)
    w_out_f = w_out_g.reshape(D, D)
    w_down_f = w_down_g.reshape(N_SLOT * FF_COLS, D)
    yb = _branch_b_fwd(z, sgu_ln_g, sgu_ln_b, sgu_w_s[0], b_s_t)
    pa, pb, merged, h1 = _merge_out(ya, yb, z, xs, w_oa_f, w_ob_f, w_out_f)
    gf2 = norm_final_g.reshape(1, D)
    r_act, n2, dh2, loss_acc, d_gfin = _mlp_fwd(h1, norm_mlp_g, w_up_g, w_down_f, gf2, tgt)

    def level1(names, grads, whole=()):
        recv = _run_plan("rs_sibling_" + names[0], _sibling_plan(grads, whole))
        parts = [_pair_sum("pair_sum_" + nm, g, r, core) for nm, g, r in zip(names, grads, recv)]
        return parts, recv[len(grads):]

    g_down = _wgrad("wgrad_down", r_act, dh2, N_SLOT, True, False, square_a=True)
    p_down, _ = level1(["down"], [g_down])
    (df, dh1, d_gmlp), (got_down,) = _mlp_bwd(dh2, r_act, w_down_f, w_up_g, h1, norm_mlp_g, comm=_chips_plan(p_down))
    g_up = _wgrad("wgrad_up", n2, df, N_SLOT, False, True)
    g_out = _wgrad("wgrad_out", merged, dh1, 1, False, False).reshape(N_SLOT, D // N_SLOT, D)
    (p_up, p_out), _ = level1(["up", "out"], [g_up, g_out])
    (dz, dpa, dpb, dya, dyb), (got_up,) = _merge_bwd(dh1, z, pa, pb, w_out_f, w_oa_f, w_ob_f,
                                                     comm=_chips_plan([p_up]))
    g_oa = _wgrad("wgrad_out_a", ya, dpa, 1, False, False).reshape(N_SLOT, D // N_SLOT, D)
    g_ob = _wgrad("wgrad_out_b", yb, dpb, 1, False, False).reshape(N_SLOT, D // N_SLOT, D)
    (p_oa, p_ob), _ = level1(["out_a", "out_b"], [g_oa, g_ob])
    (dz, d_ws, d_bs, d_ln), (got_out, got_oa, got_ob) = _branch_b_bwd(
        dz, dyb, z, sgu_ln_g, sgu_ln_b, sgu_w_s[0], b_s_t, comm=_chips_plan([p_out, p_oa, p_ob]))
    (dz, d_vec, d_wr, d_wi), _ = _branch_a_bwd(dz, dya, z, hs, conv_w_f, conv_b, w_r_f, b_r_f, w_i_f, b_i_f,
                                               lru_lambda)
    g_in = _wgrad("wgrad_in", n1, dz, N_SLOT, False, True)
    g_gate = jnp.stack([d_wr, d_wi]).reshape(2, HEADS, N_SLOT, 32, HEAD_DIM).transpose(2, 0, 1, 3, 4)
    g_gate = g_gate.reshape(N_SLOT, 2 * HEADS * 32, HEAD_DIM).astype(BF16)
    (p_in, p_gate), _ = level1(["in", "gate"], [g_in, g_gate])
    (dx, d_gmix), (got_in, got_gate) = _in_bwd(dz, w_in_g, xs, dh1, norm_mix_g, comm=_chips_plan([p_in, p_gate]))

    ws_mask = jnp.tril(jnp.ones((CHUNK, CHUNK), F32))
    small = jnp.concatenate([
        d_gmix, d_vec[0:4], d_vec[4:5], d_vec[5:6], d_vec[6:7], d_vec[7:8], d_ln[0:1], d_ln[1:2], d_gmlp, d_gfin,
        jnp.pad(d_bs[:, :, 0].reshape(1, GROUPS * CHUNK), ((0, 0), (0, D - GROUPS * CHUNK))),
        (d_ws * ws_mask).reshape(64, D), jnp.zeros((_SMALL_ROWS - 78, D), F32)], axis=0)
    (small_recv,) = _run_plan("rs_sibling_small", _sibling_plan([], [small]))
    small_chip = _add2("pair_sum_small", small, small_recv)
    (small_got,) = _run_plan("rs_chips_small", _chips_plan([], [small_chip]))
    small_sum = _sum4("sum_small", small_got)
    got = [got_in, got_oa, got_ob, got_out, got_up, got_down, got_gate]

    def step(nm, terms, w, m, v, rows, cols):
        g, d, mn, vn = _adamw("adamw_" + nm, terms.reshape(4, rows, cols), w.reshape(rows, cols),
                              m.reshape(rows, cols), v.reshape(rows, cols))
        return [a.reshape(w.shape) for a in (g, d, mn, vn)]

    o_in = step("in", got[0], w_in, m_w_in, v_w_in, D, W_IN_COLS)
    o_oa = step("out_a", got[1], w_out_a, m_w_out_a, v_w_out_a, D // N_SLOT, D)
    o_ob = step("out_b", got[2], w_out_b, m_w_out_b, v_w_out_b, D // N_SLOT, D)
    o_out = step("out", got[3], w_out, m_w_out, v_w_out, D // N_SLOT, D)
    o_up = step("up", got[4], w_up, m_w_up, v_w_up, D, FF_COLS)
    o_down = step("down", got[5], w_down, m_w_down, v_w_down, FF_COLS, D)
    gate_w = jnp.stack([w_rgate[0], w_igate[0]]).reshape(2 * HEADS * 32, HEAD_DIM)
    gate_m = jnp.stack([m_w_rgate[0], m_w_igate[0]]).reshape(2 * HEADS * 32, HEAD_DIM)
    gate_v = jnp.stack([v_w_rgate[0], v_w_igate[0]]).reshape(2 * HEADS * 32, HEAD_DIM)
    o_gate = _adamw("adamw_gate", got[6], gate_w, gate_m, gate_v)
    o_gate = [a.reshape(2, 1, HEADS, 32, HEAD_DIM) for a in o_gate]
    o_wr = [a[0] for a in o_gate]
    o_wi = [a[1] for a in o_gate]

    def own(full, width):
        return lax.dynamic_slice_in_dim(full, me * width, width, axis=1)

    small_g = {
        "norm_mix_g": small_sum[0:1], "conv_w": own(small_sum[1:5], 128)[None], "conv_b": small_sum[5:6],
        "b_rgate": own(small_sum[6:7].reshape(HEADS, HEAD_DIM), 32)[None],
        "b_igate": own(small_sum[7:8].reshape(HEADS, HEAD_DIM), 32)[None],
        "lru_lambda": small_sum[8:9], "sgu_ln_g": small_sum[9:10], "sgu_ln_b": small_sum[10:11],
        "norm_mlp_g": small_sum[11:12], "norm_final_g": small_sum[12],
        "sgu_b_s": small_sum[13, 0:GROUPS * CHUNK].reshape(1, GROUPS, CHUNK),
        "sgu_w_s": small_sum[14:78].reshape(1, GROUPS, CHUNK, CHUNK),
    }
    small_w = {"norm_mix_g": (norm_mix_g, m_norm_mix_g, v_norm_mix_g), "conv_w": (conv_w, m_conv_w, v_conv_w),
               "conv_b": (conv_b, m_conv_b, v_conv_b), "b_rgate": (b_rgate, m_b_rgate, v_b_rgate),
               "b_igate": (b_igate, m_b_igate, v_b_igate), "lru_lambda": (lru_lambda, m_lru_lambda, v_lru_lambda),
               "sgu_ln_g": (sgu_ln_g, m_sgu_ln_g, v_sgu_ln_g), "sgu_ln_b": (sgu_ln_b, m_sgu_ln_b, v_sgu_ln_b),
               "norm_mlp_g": (norm_mlp_g, m_norm_mlp_g, v_norm_mlp_g),
               "norm_final_g": (norm_final_g, m_norm_final_g, v_norm_final_g),
               "sgu_b_s": (sgu_b_s, m_sgu_b_s, v_sgu_b_s), "sgu_w_s": (sgu_w_s, m_sgu_w_s, v_sgu_w_s)}
    order = list(small_g)

    def flat(arrs):
        f = jnp.concatenate([a.reshape(-1) for a in arrs])
        return jnp.pad(f, (0, _FLAT_ROWS * 128 - f.shape[0])).reshape(_FLAT_ROWS, 128)

    fg = flat([small_g[k] for k in order])
    fw, fm, fv = (flat([small_w[k][q] for k in order]) for q in range(3))
    flat_out = _adamw("adamw_small", fg.reshape(1, _FLAT_ROWS, 128), fw, fm, fv)
    o_small = {}
    off = 0
    for k in order:
        shape = small_w[k][0].shape
        size = 1
        for dim in shape:
            size *= dim
        o_small[k] = [a.reshape(-1)[off:off + size].reshape(shape) for a in flat_out]
        off += size

    loss = lax.psum(loss_acc[0, 0], ("x", "y", "c"))

    per_weight = {"norm_mix_g": o_small["norm_mix_g"], "w_in": o_in, "conv_w": o_small["conv_w"],
                  "conv_b": o_small["conv_b"], "w_rgate": o_wr, "b_rgate": o_small["b_rgate"], "w_igate": o_wi,
                  "b_igate": o_small["b_igate"], "lru_lambda": o_small["lru_lambda"], "w_out_a": o_oa,
                  "sgu_ln_g": o_small["sgu_ln_g"], "sgu_ln_b": o_small["sgu_ln_b"], "sgu_w_s": o_small["sgu_w_s"],
                  "sgu_b_s": o_small["sgu_b_s"], "w_out_b": o_ob, "w_out": o_out, "norm_mlp_g": o_small["norm_mlp_g"],
                  "w_up": o_up, "w_down": o_down, "norm_final_g": o_small["norm_final_g"]}
    names_w = list(per_weight)
    return (loss, dx[None], *[per_weight[k][0] for k in names_w], *[per_weight[k][1] for k in names_w],
            *[per_weight[k][2] for k in names_w], *[per_weight[k][3] for k in names_w])
```

```python
import jax
import jax.numpy as jnp
from jax import lax
from jax.experimental import pallas as pl
from jax.experimental.pallas import tpu as pltpu

F32 = jnp.float32
BF16 = jnp.bfloat16
SDS = jax.ShapeDtypeStruct
MESH = pl.DeviceIdType.MESH
ANY = pl.BlockSpec(memory_space=pl.ANY)

D = 1024
N_SLOT = 8
W_IN_COLS = 768
FF_COLS = 512
HEADS, HEAD_DIM = 4, 256
GROUPS, GROUP_DIM = 4, 256
CHUNK = 128
CONV_K = 4
NORM_EPS = 1e-6
LN_EPS = 1e-5
LRU_C = 8.0
ADAM_LR, ADAM_B1, ADAM_B2, ADAM_EPS, ADAM_WD, ADAM_STEP = 0.001, 0.9, 0.999, 1e-08, 0.01, 10

TM_ROWS = 1024
TM_MERGE = 512
T_BRANCH = 256
MiB = 1024 * 1024

_GELU_C = 0.7978845608028654
_GELU_A = 0.044715


def _cparams(sem, vmem_mib):
    return pltpu.CompilerParams(dimension_semantics=sem, vmem_limit_bytes=vmem_mib * MiB)


def _gelu(x):
    t = jnp.tanh(_GELU_C * (x + _GELU_A * x * x * x))
    return 0.5 * x * (1.0 + t)


def _gelu_and_grad(x):
    x2 = x * x
    t = jnp.tanh(_GELU_C * x * (1.0 + _GELU_A * x2))
    g = 0.5 * x * (1.0 + t)
    dg = 0.5 * (1.0 + t) + 0.5 * x * (1.0 - t * t) * _GELU_C * (1.0 + 3.0 * _GELU_A * x2)
    return g, dg


def _softplus(x):
    return jnp.maximum(x, 0.0) + jnp.log1p(jnp.exp(-jnp.abs(x)))


def _dot(a, b):
    return jnp.dot(a, b, preferred_element_type=F32)


def _dot_nt(a, b):
    return lax.dot_general(a, b, (((1,), (1,)), ((), ())), preferred_element_type=F32)


def _dot_tn(a, b):
    return lax.dot_general(a, b, (((0,), (0,)), ((), ())), preferred_element_type=F32)


def _rows_shifted(prev8, cur, k):
    ext = jnp.concatenate([prev8, cur], axis=0)
    return pltpu.roll(ext, k, 0)[8:]


def _rows_advanced(cur, next8, k):
    t = cur.shape[0]
    ext = jnp.concatenate([cur, next8], axis=0)
    return pltpu.roll(ext, t + 8 - k, 0)[:t]


def _in_proj(x, g_mix, w_in_g, comm=None):
    s = x.shape[0]
    tm = min(TM_ROWS, s)

    def body(x_ref, g_ref, w_ref, z_ref, n_ref, n_s):
        @pl.when(pl.program_id(1) == 0)
        def _():
            xv = x_ref[...]
            rstd = lax.rsqrt(jnp.mean(xv * xv, axis=-1, keepdims=True) + NORM_EPS)
            nb = (xv * rstd * g_ref[...]).astype(BF16)
            n_s[...] = nb
            n_ref[...] = nb

        z_ref[...] = _dot(n_s[...], w_ref[...]).astype(BF16)

    return _call(
        body, name="in_proj", grid=(s // tm, N_SLOT),
        in_specs=[pl.BlockSpec((tm, D), lambda i, j: (i, 0)),
                  pl.BlockSpec((1, D), lambda i, j: (0, 0)),
                  pl.BlockSpec((None, D, W_IN_COLS), lambda i, j: (j, 0, 0))],
        out_specs=[pl.BlockSpec((tm, W_IN_COLS), lambda i, j: (i, j)),
                   pl.BlockSpec((tm, D), lambda i, j: (i, 0))],
        out_shape=[SDS((s, N_SLOT * W_IN_COLS), BF16), SDS((s, D), BF16)],
        scratch_shapes=[pltpu.VMEM((tm, D), BF16)],
        params=_cparams(("arbitrary", "arbitrary"), 40), args=(x, g_mix, w_in_g), comm=comm)


def _lru_gates(xc, xcb, wr_ref, br, wi_ref, bi, sp_lam, a_s, b_s, r_s=None, i_s=None, m_s=None):
    for h in range(HEADS):
        sl = slice(h * HEAD_DIM, (h + 1) * HEAD_DIM)
        r = jax.nn.sigmoid(_dot(xcb[:, sl], wr_ref[h]) + br[:, sl])
        ig = jax.nn.sigmoid(_dot(xcb[:, sl], wi_ref[h]) + bi[:, sl])
        log_a = (-LRU_C) * r * sp_lam[:, sl]
        a = jnp.exp(log_a)
        mult = jnp.sqrt(-jnp.tanh(log_a) * (a * a + 1.0))
        a_s[:, sl] = a
        b_s[:, sl] = xc[:, sl] * ig * mult
        if r_s is not None:
            r_s[:, sl] = r
            i_s[:, sl] = ig
            m_s[:, sl] = mult


def _conv_fwd(xa, prev8, cw, cb):
    xc = cb + cw[0:1, :] * xa
    for k in range(1, CONV_K):
        xc = xc + cw[k:k + 1, :] * _rows_shifted(prev8, xa, k)
    return xc


def _branch_a_fwd(z, conv_w, conv_b, w_r, b_r, w_i, b_i, lam, comm=None):
    s = z.shape[0]
    ta = min(T_BRANCH, s)
    per16 = ta // 16

    def body(xa_ref, xp_ref, ga_ref, cw_ref, cb_ref, wr_ref, br_ref, wi_ref, bi_ref, lam_ref,
             ya_ref, hs_ref, a_s, b_s, h_s, carry_s):
        i = pl.program_id(0)

        @pl.when(i == 0)
        def _():
            carry_s[...] = jnp.zeros_like(carry_s)

        xa = xa_ref[...].astype(F32)
        prev8 = jnp.where(i > 0, xp_ref[...].astype(F32)[8:16], 0.0)
        xc = _conv_fwd(xa, prev8, cw_ref[...], cb_ref[...])
        sp_lam = _softplus(-lam_ref[...])
        _lru_gates(xc, xc.astype(BF16), wr_ref, br_ref[...], wi_ref, bi_ref[...], sp_lam, a_s, b_s)

        row = lax.broadcasted_iota(jnp.int32, (8, D), 0)

        def group(g, carry):
            off = pl.multiple_of(g * 8, 8)
            a8 = a_s[pl.ds(off, 8), :]
            b8 = b_s[pl.ds(off, 8), :]
            for d in (1, 2, 4):
                a_sh = jnp.where(row >= d, pltpu.roll(a8, d, 0), 1.0)
                b_sh = jnp.where(row >= d, pltpu.roll(b8, d, 0), 0.0)
                b8 = a8 * b_sh + b8
                a8 = a8 * a_sh
            h8 = b8 + a8 * carry
            h_s[pl.ds(off, 8), :] = h8
            return jnp.broadcast_to(h8[7:8, :], (8, D))

        carry_s[...] = lax.fori_loop(0, ta // 8, group, carry_s[...])
        hs = h_s[...]
        hs_ref[...] = hs.astype(BF16)
        ya_ref[...] = (hs * _gelu(ga_ref[...].astype(F32))).astype(BF16)

    vec = pl.BlockSpec((1, D), lambda i: (0, 0))
    gate = pl.BlockSpec((HEADS, HEAD_DIM, HEAD_DIM), lambda i: (0, 0, 0))
    return _call(
        body, name="branch_a_fwd", grid=(s // ta,),
        in_specs=[pl.BlockSpec((ta, D), lambda i: (i, 0)),
                  pl.BlockSpec((16, D), lambda i: (jnp.maximum(i * per16 - 1, 0), 0)),
                  pl.BlockSpec((ta, D), lambda i: (i, 1)),
                  pl.BlockSpec((CONV_K, D), lambda i: (0, 0)), vec, gate, vec, gate, vec, vec],
        out_specs=[pl.BlockSpec((ta, D), lambda i: (i, 0)), pl.BlockSpec((ta, D), lambda i: (i, 0))],
        out_shape=[SDS((s, D), BF16), SDS((s, D), BF16)],
        scratch_shapes=[pltpu.VMEM((ta, D), F32), pltpu.VMEM((ta, D), F32), pltpu.VMEM((ta, D), F32),
                        pltpu.VMEM((8, D), F32)],
        params=_cparams(("arbitrary",), 40), args=(z, z, z, conv_w, conv_b, w_r, b_r, w_i, b_i, lam), comm=comm)


def _sgu_common(ub, vb, lg, lb, with_grad):
    if with_grad:
        u, du = _gelu_and_grad(ub)
        v, dv = _gelu_and_grad(vb)
    else:
        u, v, du, dv = _gelu(ub), _gelu(vb), None, None
    mu = jnp.mean(v, axis=-1, keepdims=True)
    vc = v - mu
    rstd = lax.rsqrt(jnp.mean(vc * vc, axis=-1, keepdims=True) + LN_EPS)
    vhat = vc * rstd
    vln = vhat * lg + lb
    return u, du, dv, rstd, vhat, vln


def _masked_ws(ws_ref):
    t = lax.broadcasted_iota(jnp.int32, (CHUNK, CHUNK), 0)
    c = lax.broadcasted_iota(jnp.int32, (CHUNK, CHUNK), 1)
    keep = c <= t
    return [jnp.where(keep, ws_ref[g], 0.0).astype(BF16) for g in range(GROUPS)]


def _branch_b_fwd(z, ln_g, ln_b, w_s, b_s_t):
    s = z.shape[0]
    tb = min(T_BRANCH, s)

    def body(ub_ref, vb_ref, lg_ref, lb_ref, ws_ref, bs_ref, yb_ref):
        u, _, _, _, _, vln = _sgu_common(ub_ref[...].astype(F32), vb_ref[...].astype(F32),
                                         lg_ref[...], lb_ref[...], False)
        vlnb = vln.astype(BF16)
        wm = _masked_ws(ws_ref)
        bs = bs_ref[...]
        for c in range(tb // CHUNK):
            rs = slice(c * CHUNK, (c + 1) * CHUNK)
            for g in range(GROUPS):
                cs = slice(g * GROUP_DIM, (g + 1) * GROUP_DIM)
                sp = _dot(wm[g], vlnb[rs, cs]) + bs[:, g:g + 1]
                yb_ref[rs, cs] = (u[rs, cs] * sp).astype(BF16)

    vec = pl.BlockSpec((1, D), lambda i: (0, 0))
    return pl.pallas_call(
        body, name="branch_b_fwd", grid=(s // tb,),
        in_specs=[pl.BlockSpec((tb, D), lambda i: (i, 2)), pl.BlockSpec((tb, D), lambda i: (i, 3)), vec, vec,
                  pl.BlockSpec((GROUPS, CHUNK, CHUNK), lambda i: (0, 0, 0)),
                  pl.BlockSpec((CHUNK, GROUPS), lambda i: (0, 0))],
        out_specs=pl.BlockSpec((tb, D), lambda i: (i, 0)),
        out_shape=SDS((s, D), BF16),
        compiler_params=_cparams(("arbitrary",), 40),
    )(z, z, ln_g, ln_b, w_s, b_s_t)


def _merge_out(ya, yb, z, x, w_oa, w_ob, w_out):
    s = x.shape[0]
    tm = min(TM_MERGE, s)

    def body(ya_ref, yb_ref, ma_ref, mb_ref, x_ref, woa_ref, wob_ref, wo_ref, pa_ref, pb_ref, mg_ref, h1_ref):
        pa = _dot(ya_ref[...], woa_ref[...])
        pb = _dot(yb_ref[...], wob_ref[...])
        merged = (jax.nn.sigmoid(ma_ref[...].astype(F32)) * pa
                  + jax.nn.sigmoid(mb_ref[...].astype(F32)) * pb).astype(BF16)
        pa_ref[...] = pa.astype(BF16)
        pb_ref[...] = pb.astype(BF16)
        mg_ref[...] = merged
        h1_ref[...] = x_ref[...] + _dot(merged, wo_ref[...])

    row = pl.BlockSpec((tm, D), lambda i: (i, 0))
    wsp = pl.BlockSpec((D, D), lambda i: (0, 0))
    return pl.pallas_call(
        body, name="merge_out", grid=(s // tm,),
        in_specs=[row, row, pl.BlockSpec((tm, D), lambda i: (i, 4)), pl.BlockSpec((tm, D), lambda i: (i, 5)),
                  row, wsp, wsp, wsp],
        out_specs=[row, row, row, row],
        out_shape=[SDS((s, D), BF16), SDS((s, D), BF16), SDS((s, D), BF16), SDS((s, D), F32)],
        compiler_params=_cparams(("arbitrary",), 48),
    )(ya, yb, z, z, x, w_oa, w_ob, w_out)


def _mlp_fwd(h1, g_mlp, w_up_g, w_down, g_fin, tgt):
    s = h1.shape[0]
    tm = min(TM_ROWS, s)
    nj = N_SLOT

    def body(h1_ref, gm_ref, wu_ref, wd_ref, gf_ref, t_ref, r_ref, n2_ref, dh2_ref, loss_ref, dgf_ref, n2_s, acc_s):
        i, j = pl.program_id(0), pl.program_id(1)

        @pl.when(j == 0)
        def _():
            hv = h1_ref[...]
            rstd = lax.rsqrt(jnp.mean(hv * hv, axis=-1, keepdims=True) + NORM_EPS)
            nb = (hv * rstd * gm_ref[...]).astype(BF16)
            n2_s[...] = nb
            n2_ref[...] = nb
            acc_s[...] = jnp.zeros_like(acc_s)

        @pl.when((i == 0) & (j == 0))
        def _():
            loss_ref[...] = jnp.zeros_like(loss_ref)
            dgf_ref[...] = jnp.zeros_like(dgf_ref)

        r = jnp.maximum(_dot(n2_s[...], wu_ref[...]), 0.0)
        r_ref[...] = r.astype(BF16)
        acc_s[...] += _dot((r * r).astype(BF16), wd_ref[...])

        @pl.when(j == nj - 1)
        def _():
            h2 = h1_ref[...] + acc_s[...]
            rstd = lax.rsqrt(jnp.mean(h2 * h2, axis=-1, keepdims=True) + NORM_EPS)
            hh = h2 * rstd
            gf = gf_ref[...]
            e = hh * gf - t_ref[...]
            loss_ref[...] += jnp.sum(e * e) * (0.5 / D)
            dy = e * (1.0 / D)
            dgf_ref[...] += jnp.sum(dy * hh, axis=0, keepdims=True)
            dhh = dy * gf
            dh2_ref[...] = rstd * (dhh - hh * jnp.mean(dhh * hh, axis=-1, keepdims=True))

    row = pl.BlockSpec((tm, D), lambda i, j: (i, 0))
    vec = pl.BlockSpec((1, D), lambda i, j: (0, 0))
    return pl.pallas_call(
        body, name="mlp_fwd", grid=(s // tm, nj),
        in_specs=[row, vec, pl.BlockSpec((None, D, FF_COLS), lambda i, j: (j, 0, 0)),
                  pl.BlockSpec((FF_COLS, D), lambda i, j: (j, 0)), vec, row],
        out_specs=[pl.BlockSpec((tm, FF_COLS), lambda i, j: (i, j)), row, row,
                   pl.BlockSpec((8, 128), lambda i, j: (0, 0)), vec],
        out_shape=[SDS((s, nj * FF_COLS), BF16), SDS((s, D), BF16), SDS((s, D), F32),
                   SDS((8, 128), F32), SDS((1, D), F32)],
        scratch_shapes=[pltpu.VMEM((tm, D), BF16), pltpu.VMEM((tm, D), F32)],
        compiler_params=_cparams(("arbitrary", "arbitrary"), 52),
    )(h1, g_mlp, w_up_g, w_down, g_fin, tgt)


def _mlp_bwd(dh2, r, w_down, w_up_g, h1, g_mlp, comm=None):
    s = h1.shape[0]
    tm = min(TM_ROWS, s)
    nj = N_SLOT

    def body(dh2_ref, r_ref, wd_ref, wu_ref, h1_ref, gm_ref, df_ref, dh1_ref, dgm_ref, dh2b_s, acc_s):
        i, j = pl.program_id(0), pl.program_id(1)

        @pl.when(j == 0)
        def _():
            dh2b_s[...] = dh2_ref[...].astype(BF16)
            acc_s[...] = jnp.zeros_like(acc_s)

        @pl.when((i == 0) & (j == 0))
        def _():
            dgm_ref[...] = jnp.zeros_like(dgm_ref)

        d_act = _dot_nt(dh2b_s[...], wd_ref[...])
        df = (d_act * (2.0 * r_ref[...].astype(F32))).astype(BF16)
        df_ref[...] = df
        acc_s[...] += _dot_nt(df, wu_ref[...])

        @pl.when(j == nj - 1)
        def _():
            hv = h1_ref[...]
            rstd = lax.rsqrt(jnp.mean(hv * hv, axis=-1, keepdims=True) + NORM_EPS)
            hh = hv * rstd
            dn2 = acc_s[...]
            dgm_ref[...] += jnp.sum(dn2 * hh, axis=0, keepdims=True)
            dhat = dn2 * gm_ref[...]
            dh1_ref[...] = dh2_ref[...] + rstd * (dhat - hh * jnp.mean(dhat * hh, axis=-1, keepdims=True))

    row = pl.BlockSpec((tm, D), lambda i, j: (i, 0))
    vec = pl.BlockSpec((1, D), lambda i, j: (0, 0))
    ffb = pl.BlockSpec((tm, FF_COLS), lambda i, j: (i, j))
    return _call(
        body, name="mlp_bwd", grid=(s // tm, nj),
        in_specs=[row, ffb, pl.BlockSpec((FF_COLS, D), lambda i, j: (j, 0)),
                  pl.BlockSpec((None, D, FF_COLS), lambda i, j: (j, 0, 0)), row, vec],
        out_specs=[ffb, row, vec],
        out_shape=[SDS((s, nj * FF_COLS), BF16), SDS((s, D), F32), SDS((1, D), F32)],
        scratch_shapes=[pltpu.VMEM((tm, D), BF16), pltpu.VMEM((tm, D), F32)],
        params=_cparams(("arbitrary", "arbitrary"), 52), args=(dh2, r, w_down, w_up_g, h1, g_mlp), comm=comm)


def _merge_bwd(dh1, z, pa, pb, w_out, w_oa, w_ob, comm=None):
    s = dh1.shape[0]
    tm = min(TM_MERGE, s)

    def body(dh1_ref, ma_ref, mb_ref, pa_ref, pb_ref, wo_ref, woa_ref, wob_ref,
             dz_ref, dpa_ref, dpb_ref, dya_ref, dyb_ref):
        dm = _dot_nt(dh1_ref[...].astype(BF16), wo_ref[...])
        sa = jax.nn.sigmoid(ma_ref[...].astype(F32))
        sb = jax.nn.sigmoid(mb_ref[...].astype(F32))
        dpa = (dm * sa).astype(BF16)
        dpb = (dm * sb).astype(BF16)
        dz_ref[:, 0:D] = (dm * pa_ref[...].astype(F32) * sa * (1.0 - sa)).astype(BF16)
        dz_ref[:, D:2 * D] = (dm * pb_ref[...].astype(F32) * sb * (1.0 - sb)).astype(BF16)
        dpa_ref[...] = dpa
        dpb_ref[...] = dpb
        dya_ref[...] = _dot_nt(dpa, woa_ref[...]).astype(BF16)
        dyb_ref[...] = _dot_nt(dpb, wob_ref[...]).astype(BF16)

    row = pl.BlockSpec((tm, D), lambda i: (i, 0))
    wsp = pl.BlockSpec((D, D), lambda i: (0, 0))
    return _call(
        body, name="merge_bwd", grid=(s // tm,),
        in_specs=[row, pl.BlockSpec((tm, D), lambda i: (i, 4)), pl.BlockSpec((tm, D), lambda i: (i, 5)),
                  row, row, wsp, wsp, wsp],
        out_specs=[pl.BlockSpec((tm, 2 * D), lambda i: (i, 2)), row, row, row, row],
        out_shape=[SDS((s, 6 * D), BF16)] + [SDS((s, D), BF16)] * 4, scratch_shapes=[],
        params=_cparams(("arbitrary",), 48), args=(dh1, z, z, pa, pb, w_out, w_oa, w_ob), comm=comm)


def _branch_b_bwd(dz, dyb, z, ln_g, ln_b, w_s, b_s_t, comm=None):
    s = z.shape[0]
    tb = min(T_BRANCH, s)

    def body(dz_in, dyb_ref, ub_ref, vb_ref, lg_ref, lb_ref, ws_ref, bs_ref,
             dz_ref, dws_ref, dbs_ref, dln_ref, du_s, dvln_s):
        del dz_in

        @pl.when(pl.program_id(0) == 0)
        def _():
            dws_ref[...] = jnp.zeros_like(dws_ref)
            dbs_ref[...] = jnp.zeros_like(dbs_ref)
            dln_ref[...] = jnp.zeros_like(dln_ref)

        lg = lg_ref[...]
        u, du, dv, rstd, vhat, vln = _sgu_common(ub_ref[...].astype(F32), vb_ref[...].astype(F32),
                                                 lg, lb_ref[...], True)
        vlnb = vln.astype(BF16)
        dyb_v = dyb_ref[...].astype(F32)
        wm = _masked_ws(ws_ref)
        keep = (lax.broadcasted_iota(jnp.int32, (CHUNK, CHUNK), 1)
                <= lax.broadcasted_iota(jnp.int32, (CHUNK, CHUNK), 0))
        bs = bs_ref[...]
        for c in range(tb // CHUNK):
            rs = slice(c * CHUNK, (c + 1) * CHUNK)
            for g in range(GROUPS):
                cs = slice(g * GROUP_DIM, (g + 1) * GROUP_DIM)
                v_blk = vlnb[rs, cs]
                sp = _dot(wm[g], v_blk) + bs[:, g:g + 1]
                d_sp = dyb_v[rs, cs] * u[rs, cs]
                d_spb = d_sp.astype(BF16)
                du_s[rs, cs] = dyb_v[rs, cs] * sp
                dvln_s[rs, cs] = _dot_tn(wm[g], d_spb)
                dws_ref[g] += jnp.where(keep, _dot_nt(d_spb, v_blk), 0.0)
                dbs_ref[g] += jnp.broadcast_to(jnp.sum(d_sp, axis=-1, keepdims=True), (CHUNK, CHUNK))
        dvln = dvln_s[...]
        dln_ref[0:1, :] += jnp.sum(dvln * vhat, axis=0, keepdims=True)
        dln_ref[1:2, :] += jnp.sum(dvln, axis=0, keepdims=True)
        dvh = dvln * lg
        d_v = rstd * (dvh - jnp.mean(dvh, axis=-1, keepdims=True)
                      - vhat * jnp.mean(dvh * vhat, axis=-1, keepdims=True))
        dz_ref[:, 0:D] = (du_s[...] * du).astype(BF16)
        dz_ref[:, D:2 * D] = (d_v * dv).astype(BF16)

    vec = pl.BlockSpec((1, D), lambda i: (0, 0))
    sq = pl.BlockSpec((GROUPS, CHUNK, CHUNK), lambda i: (0, 0, 0))
    return _call(
        body, name="branch_b_bwd", grid=(s // tb,),
        in_specs=[ANY, pl.BlockSpec((tb, D), lambda i: (i, 0)),
                  pl.BlockSpec((tb, D), lambda i: (i, 2)), pl.BlockSpec((tb, D), lambda i: (i, 3)), vec, vec, sq,
                  pl.BlockSpec((CHUNK, GROUPS), lambda i: (0, 0))],
        out_specs=[pl.BlockSpec((tb, 2 * D), lambda i: (i, 1)), sq, sq, pl.BlockSpec((8, D), lambda i: (0, 0))],
        out_shape=[SDS(dz.shape, BF16), SDS((GROUPS, CHUNK, CHUNK), F32), SDS((GROUPS, CHUNK, CHUNK), F32),
                   SDS((8, D), F32)],
        scratch_shapes=[pltpu.VMEM((tb, D), F32), pltpu.VMEM((tb, D), F32)], aliases={0: 0},
        params=_cparams(("arbitrary",), 40), args=(dz, dyb, z, z, ln_g, ln_b, w_s, b_s_t), comm=comm)


def _branch_a_bwd(dz, dya, z, hs, conv_w, conv_b, w_r, b_r, w_i, b_i, lam, comm=None):
    s = z.shape[0]
    ta = min(T_BRANCH, s)
    nb = s // ta
    per16 = ta // 16

    def body(dz_in, dya_ref, xa_ref, xp_ref, ga_ref, hs_ref, hp_ref, cw_ref, cb_ref, wr_ref, br_ref, wi_ref,
             bi_ref, lam_ref, dz_ref, vec_ref, dwr_ref, dwi_ref,
             a_s, b_s, h_s, r_s, i_s, m_s, dcar_s, acar_s, dxc_s):
        del dz_in
        i = pl.program_id(0)
        blk = nb - 1 - i

        @pl.when(i == 0)
        def _():
            dcar_s[...] = jnp.zeros_like(dcar_s)
            acar_s[...] = jnp.zeros_like(acar_s)
            dxc_s[...] = jnp.zeros_like(dxc_s)
            vec_ref[...] = jnp.zeros_like(vec_ref)
            dwr_ref[...] = jnp.zeros_like(dwr_ref)
            dwi_ref[...] = jnp.zeros_like(dwi_ref)

        cw = cw_ref[...]
        lam_v = lam_ref[...]
        xa = xa_ref[...].astype(F32)
        prev8 = jnp.where(blk > 0, xp_ref[...].astype(F32)[8:16], 0.0)
        xc = _conv_fwd(xa, prev8, cw, cb_ref[...])
        xcb = xc.astype(BF16)
        sp_lam = _softplus(-lam_v)
        _lru_gates(xc, xcb, wr_ref, br_ref[...], wi_ref, bi_ref[...], sp_lam, a_s, b_s, r_s, i_s, m_s)

        hs_v = hs_ref[...].astype(F32)
        hprev8 = jnp.where(blk > 0, hp_ref[...].astype(F32)[8:16], 0.0)
        h_m1 = _rows_shifted(hprev8, hs_v, 1)
        gg, dgg = _gelu_and_grad(ga_ref[...].astype(F32))
        dya_v = dya_ref[...].astype(F32)
        dz_ref[:, D:2 * D] = (dya_v * hs_v * dgg).astype(BF16)

        a_v = a_s[...]
        a_s[...] = _rows_advanced(a_v, acar_s[...], 1)
        b_s[...] = dya_v * gg

        row = lax.broadcasted_iota(jnp.int32, (8, D), 0)
        ng = ta // 8

        def group(gi, carry):
            off = pl.multiple_of((ng - 1 - gi) * 8, 8)
            c8 = a_s[pl.ds(off, 8), :]
            d8 = b_s[pl.ds(off, 8), :]
            for d in (1, 2, 4):
                c_sh = jnp.where(row < 8 - d, pltpu.roll(c8, 8 - d, 0), 1.0)
                d_sh = jnp.where(row < 8 - d, pltpu.roll(d8, 8 - d, 0), 0.0)
                d8 = c8 * d_sh + d8
                c8 = c8 * c_sh
            dh8 = d8 + c8 * carry
            h_s[pl.ds(off, 8), :] = dh8
            return jnp.broadcast_to(dh8[0:1, :], (8, D))

        dcar_s[...] = lax.fori_loop(0, ng, group, dcar_s[...])
        acar_s[...] = jnp.broadcast_to(a_v[0:1, :], (8, D))

        dbx = h_s[...]
        r_v, i_v, m_v = r_s[...], i_s[...], m_s[...]
        d_mult = dbx * xc * i_v
        d_loga = dbx * h_m1 * a_v - d_mult * (a_v * a_v) / m_v
        d_pr = d_loga * ((-LRU_C) * sp_lam) * r_v * (1.0 - r_v)
        d_pi = dbx * xc * m_v * i_v * (1.0 - i_v)
        vec_ref[7:8, :] += jnp.sum(d_loga * r_v, axis=0, keepdims=True) * (LRU_C * jax.nn.sigmoid(-lam_v))
        vec_ref[5:6, :] += jnp.sum(d_pr, axis=0, keepdims=True)
        vec_ref[6:7, :] += jnp.sum(d_pi, axis=0, keepdims=True)
        d_prb = d_pr.astype(BF16)
        d_pib = d_pi.astype(BF16)
        h_s[...] = dbx * i_v * m_v
        for h in range(HEADS):
            sl = slice(h * HEAD_DIM, (h + 1) * HEAD_DIM)
            h_s[:, sl] += _dot_nt(d_prb[:, sl], wr_ref[h]) + _dot_nt(d_pib[:, sl], wi_ref[h])
            dwr_ref[h] += _dot_tn(xcb[:, sl], d_prb[:, sl])
            dwi_ref[h] += _dot_tn(xcb[:, sl], d_pib[:, sl])
        d_xc = h_s[...]
        vec_ref[4:5, :] += jnp.sum(d_xc, axis=0, keepdims=True)
        vec_ref[0:1, :] += jnp.sum(d_xc * xa, axis=0, keepdims=True)
        d_xa = cw[0:1, :] * d_xc
        nxt = dxc_s[...]
        for k in range(1, CONV_K):
            vec_ref[k:k + 1, :] += jnp.sum(d_xc * _rows_shifted(prev8, xa, k), axis=0, keepdims=True)
            d_xa = d_xa + cw[k:k + 1, :] * _rows_advanced(d_xc, nxt, k)
        dz_ref[:, 0:D] = d_xa.astype(BF16)
        dxc_s[...] = d_xc[0:8, :]

    vec = pl.BlockSpec((1, D), lambda i: (0, 0))
    gate = pl.BlockSpec((HEADS, HEAD_DIM, HEAD_DIM), lambda i: (0, 0, 0))
    cur = lambda c: pl.BlockSpec((ta, D), lambda i: (nb - 1 - i, c))
    before = lambda c: pl.BlockSpec((16, D), lambda i: (jnp.maximum((nb - 1 - i) * per16 - 1, 0), c))
    return _call(
        body, name="branch_a_bwd", grid=(nb,),
        in_specs=[ANY, cur(0), cur(0), before(0), cur(1), cur(0), before(0),
                  pl.BlockSpec((CONV_K, D), lambda i: (0, 0)), vec, gate, vec, gate, vec, vec],
        out_specs=[pl.BlockSpec((ta, 2 * D), lambda i: (nb - 1 - i, 0)), pl.BlockSpec((8, D), lambda i: (0, 0)),
                   gate, gate],
        out_shape=[SDS(dz.shape, BF16), SDS((8, D), F32), SDS((HEADS, HEAD_DIM, HEAD_DIM), F32),
                   SDS((HEADS, HEAD_DIM, HEAD_DIM), F32)],
        scratch_shapes=[pltpu.VMEM((ta, D), F32)] * 6 + [pltpu.VMEM((8, D), F32)] * 3, aliases={0: 0},
        params=_cparams(("arbitrary",), 48),
        args=(dz, dya, z, z, z, hs, hs, conv_w, conv_b, w_r, b_r, w_i, b_i, lam), comm=comm)


def _in_bwd(dz, w_in_g, x, dh1, g_mix, comm=None):
    s = x.shape[0]
    tm = min(TM_ROWS, s)
    nj = N_SLOT

    def body(dz_ref, w_ref, x_ref, dh1_ref, g_ref, dx_ref, dg_ref, acc_s):
        i, j = pl.program_id(0), pl.program_id(1)

        @pl.when(j == 0)
        def _():
            acc_s[...] = jnp.zeros_like(acc_s)

        @pl.when((i == 0) & (j == 0))
        def _():
            dg_ref[...] = jnp.zeros_like(dg_ref)

        acc_s[...] += _dot_nt(dz_ref[...], w_ref[...])

        @pl.when(j == nj - 1)
        def _():
            xv = x_ref[...]
            rstd = lax.rsqrt(jnp.mean(xv * xv, axis=-1, keepdims=True) + NORM_EPS)
            xh = xv * rstd
            dn = acc_s[...]
            dg_ref[...] += jnp.sum(dn * xh, axis=0, keepdims=True)
            dhat = dn * g_ref[...]
            dx_ref[...] = dh1_ref[...] + rstd * (dhat - xh * jnp.mean(dhat * xh, axis=-1, keepdims=True))

    row = pl.BlockSpec((tm, D), lambda i, j: (i, 0))
    vec = pl.BlockSpec((1, D), lambda i, j: (0, 0))
    return _call(
        body, name="in_bwd", grid=(s // tm, nj),
        in_specs=[pl.BlockSpec((tm, W_IN_COLS), lambda i, j: (i, j)),
                  pl.BlockSpec((None, D, W_IN_COLS), lambda i, j: (j, 0, 0)), row, row, vec],
        out_specs=[row, vec],
        out_shape=[SDS((s, D), F32), SDS((1, D), F32)],
        scratch_shapes=[pltpu.VMEM((tm, D), F32)],
        params=_cparams(("arbitrary", "arbitrary"), 48), args=(dz, w_in_g, x, dh1, g_mix), comm=comm)


def _wgrad(name, a, b, nblk, a_split, b_split, square_a=False):
    s = a.shape[0]
    ts = min(TM_ROWS, s)
    a_w = a.shape[1] // nblk if a_split else a.shape[1]
    b_w = b.shape[1] // nblk if b_split else b.shape[1]

    def body(a_ref, b_ref, o_ref, acc_s):
        t = pl.program_id(1)

        @pl.when(t == 0)
        def _():
            acc_s[...] = jnp.zeros_like(acc_s)

        av = a_ref[...]
        if square_a:
            av = av * av
        acc_s[...] += _dot_tn(av.astype(BF16), b_ref[...].astype(BF16))

        @pl.when(t == pl.num_programs(1) - 1)
        def _():
            o_ref[...] = acc_s[...].astype(BF16)

    return pl.pallas_call(
        body, name=name, grid=(nblk, s // ts),
        in_specs=[pl.BlockSpec((ts, a_w), (lambda k, t: (t, k)) if a_split else (lambda k, t: (t, 0))),
                  pl.BlockSpec((ts, b_w), (lambda k, t: (t, k)) if b_split else (lambda k, t: (t, 0)))],
        out_specs=pl.BlockSpec((None, a_w, b_w), lambda k, t: (k, 0, 0)),
        out_shape=SDS((nblk, a_w, b_w), BF16),
        scratch_shapes=[pltpu.VMEM((a_w, b_w), F32)],
        compiler_params=_cparams(("arbitrary", "arbitrary"), 48),
    )(a, b)


def _place():
    x, y, c = lax.axis_index("x"), lax.axis_index("y"), lax.axis_index("c")
    return x, y, c


def _other_chips(x, y):
    return [(x, 1 - y, 2 * x + 1 - y), (1 - x, y, 2 * (1 - x) + y), (1 - x, 1 - y, 2 * (1 - x) + 1 - y)]


class _Plan:
    def __init__(self, arrays, out_shape, sems, start, finish):
        self.arrays, self.out_shape, self.sems, self.start, self.finish = arrays, out_shape, sems, start, finish


def _gather_plan(shards):
    n = len(shards)

    def copies(ins, outs, sems):
        send_sems, recv_sems, local_sems = sems
        x, y, c = _place()
        chip = 2 * x + y
        me = 2 * chip + c
        sib = (x, y, 1 - c)
        chips = _other_chips(x, y)

        def rc(k, t, src, blk, to):
            return pltpu.make_async_remote_copy(
                src_ref=src, dst_ref=outs[t].at[blk], send_sem=send_sems.at[k * n + t],
                recv_sem=recv_sems.at[k * n + t], device_id=to, device_id_type=MESH)

        local = [pltpu.make_async_copy(ins[t], outs[t].at[me], local_sems.at[t]) for t in range(n)]
        sends = [rc(0, t, ins[t], me, sib) for t in range(n)]
        for j, (px, py, _) in enumerate(chips):
            sends += [rc(1 + j, t, ins[t], me, (px, py, c)) for t in range(n)]
        return rc, local, sends, chips, chip, c, sib

    def start(ins, outs, sems):
        _, local, sends, _, _, _, _ = copies(ins, outs, sems)
        for cp in local + sends:
            cp.start()

    def finish(ins, outs, sems):
        rc, local, sends, chips, chip, c, sib = copies(ins, outs, sems)
        passed = []
        for j, (px, py, pc) in enumerate(chips):
            blk = 2 * pc + c
            for t in range(n):
                rc(1 + j, t, ins[t], blk, sib).wait_recv()
            for t in range(n):
                cp = rc(4 + j, t, outs[t].at[blk], blk, sib)
                cp.start()
                passed.append(cp)
        for t in range(n):
            rc(0, t, ins[t], 2 * chip + 1 - c, sib).wait_recv()
        for j, (px, py, pc) in enumerate(chips):
            for t in range(n):
                rc(4 + j, t, ins[t], 2 * pc + 1 - c, sib).wait_recv()
        for cp in sends + passed:
            cp.wait_send()
        for cp in local:
            cp.wait()

    return _Plan(list(shards), [SDS((N_SLOT,) + tuple(a.shape), a.dtype) for a in shards],
                 [pltpu.SemaphoreType.DMA((7 * n,)), pltpu.SemaphoreType.DMA((7 * n,)),
                  pltpu.SemaphoreType.DMA((n,))], start, finish)


def _sibling_plan(grads, whole=()):
    n, m = len(grads), len(whole)

    def copies(ins, outs, sems):
        send_sems, recv_sems = sems
        x, y, c = _place()
        sib = (x, y, 1 - c)

        def rc(t, src, dst):
            return pltpu.make_async_remote_copy(src_ref=src, dst_ref=dst, send_sem=send_sems.at[t],
                                                recv_sem=recv_sems.at[t], device_id=sib, device_id_type=MESH)
        return rc, c

    def start(ins, outs, sems):
        rc, c = copies(ins, outs, sems)
        for t in range(n):
            for j in range(4):
                rc(t, ins[t].at[2 * j + 1 - c], outs[t].at[j]).start()
        for t in range(n, n + m):
            rc(t, ins[t], outs[t]).start()

    def finish(ins, outs, sems):
        rc, _ = copies(ins, outs, sems)
        for t in range(n):
            rc(t, ins[t].at[pl.ds(0, 4)], outs[t]).wait()
        for t in range(n, n + m):
            rc(t, ins[t], outs[t]).wait()

    return _Plan(list(grads) + list(whole),
                 [SDS((4,) + tuple(g.shape[1:]), g.dtype) for g in grads] + [SDS(a.shape, a.dtype) for a in whole],
                 [pltpu.SemaphoreType.DMA((n + m,)), pltpu.SemaphoreType.DMA((n + m,))], start, finish)


def _chips_plan(parts, whole=()):
    n, m = len(parts), len(whole)

    def src_of(ins, t, pc):
        return ins[t].at[pc] if t < n else ins[t]

    def local_copies(ins, outs, sems, chip):
        return [pltpu.make_async_copy(src_of(ins, t, chip), outs[t].at[chip], sems[2].at[t]) for t in range(n + m)]

    def start(ins, outs, sems):
        send_sems, recv_sems, _ = sems
        x, y, c = _place()
        chip = 2 * x + y
        for cp in local_copies(ins, outs, sems, chip):
            cp.start()
        for px, py, pc in _other_chips(x, y):
            for t in range(n + m):
                pltpu.make_async_remote_copy(src_ref=src_of(ins, t, pc), dst_ref=outs[t].at[chip],
                                             send_sem=send_sems.at[t], recv_sem=recv_sems.at[t],
                                             device_id=(px, py, c), device_id_type=MESH).start()

    def finish(ins, outs, sems):
        send_sems, recv_sems, _ = sems
        x, y, c = _place()
        for t in range(n + m):
            three = outs[t].at[pl.ds(0, 3)]
            pltpu.make_async_remote_copy(src_ref=three, dst_ref=three, send_sem=send_sems.at[t],
                                         recv_sem=recv_sems.at[t], device_id=(x, y, c), device_id_type=MESH).wait()
        for cp in local_copies(ins, outs, sems, 2 * x + y):
            cp.wait()

    return _Plan(list(parts) + list(whole),
                 [SDS(p.shape, p.dtype) for p in parts] + [SDS((4,) + tuple(a.shape), a.dtype) for a in whole],
                 [pltpu.SemaphoreType.DMA((n + m,)), pltpu.SemaphoreType.DMA((n + m,)),
                  pltpu.SemaphoreType.DMA((n + m,))], start, finish)


def _exchange_plan(arr):
    def peers(x, y, c):
        flip = lambda v, f: 1 - v if f else v
        return [(flip(x, fx), flip(y, fy), flip(c, fc))
                for fx in (0, 1) for fy in (0, 1) for fc in (0, 1) if fx or fy or fc]

    def start(ins, outs, sems):
        x, y, c = _place()
        me = 4 * x + 2 * y + c
        pltpu.make_async_copy(ins[0], outs[0].at[me], sems[2].at[0]).start()
        for to in peers(x, y, c):
            pltpu.make_async_remote_copy(src_ref=ins[0], dst_ref=outs[0].at[me], send_sem=sems[0].at[0],
                                         recv_sem=sems[1].at[0], device_id=to, device_id_type=MESH).start()

    def finish(ins, outs, sems):
        x, y, c = _place()
        seven = outs[0].at[pl.ds(0, 7)]
        pltpu.make_async_remote_copy(src_ref=seven, dst_ref=seven, send_sem=sems[0].at[0], recv_sem=sems[1].at[0],
                                     device_id=(x, y, c), device_id_type=MESH).wait()
        pltpu.make_async_copy(ins[0], outs[0].at[4 * x + 2 * y + c], sems[2].at[0]).wait()

    return _Plan([arr], [SDS((N_SLOT,) + tuple(arr.shape), arr.dtype)],
                 [pltpu.SemaphoreType.DMA((1,)), pltpu.SemaphoreType.DMA((1,)), pltpu.SemaphoreType.DMA((1,))],
                 start, finish)


def _join(*plans):
    def cut(seq, sizes):
        out, at = [], 0
        for k in sizes:
            out.append(seq[at:at + k])
            at += k
        return out

    n_arr = [len(p.arrays) for p in plans]
    n_sem = [len(p.sems) for p in plans]

    def start(ins, outs, sems):
        for p, i, o, s in zip(plans, cut(ins, n_arr), cut(outs, n_arr), cut(sems, n_sem)):
            p.start(i, o, s)

    def finish(ins, outs, sems):
        for p, i, o, s in zip(plans, cut(ins, n_arr), cut(outs, n_arr), cut(sems, n_sem)):
            p.finish(i, o, s)

    return _Plan([a for p in plans for a in p.arrays], [o for p in plans for o in p.out_shape],
                 [s for p in plans for s in p.sems], start, finish)


def _run_plan(name, plan):
    k = len(plan.arrays)

    def body(*refs):
        ins, outs, sems = refs[:k], refs[k:2 * k], refs[2 * k:]
        plan.start(ins, outs, sems)
        plan.finish(ins, outs, sems)

    return pl.pallas_call(
        body, name=name, in_specs=[ANY] * k, out_specs=[ANY] * k, out_shape=plan.out_shape,
        scratch_shapes=plan.sems, compiler_params=pltpu.CompilerParams(has_side_effects=True),
    )(*plan.arrays)


def _call(body, *, name, grid, in_specs, out_specs, out_shape, scratch_shapes, params, args, comm=None,
          aliases=None):
    aliases = aliases or {}
    if comm is None:
        res = pl.pallas_call(body, name=name, grid=grid, in_specs=in_specs, out_specs=out_specs,
                             out_shape=out_shape, scratch_shapes=scratch_shapes, compiler_params=params,
                             input_output_aliases=aliases)(*args)
        return list(res), []
    n_in, n_out, n_scr, k = len(in_specs), len(out_specs), len(scratch_shapes), len(comm.arrays)

    def wrapped(*refs):
        ins = refs[:n_in]
        c_in = refs[n_in:n_in + k]
        outs = refs[n_in + k:n_in + k + n_out]
        c_out = refs[n_in + k + n_out:n_in + 2 * k + n_out]
        scr = refs[n_in + 2 * k + n_out:n_in + 2 * k + n_out + n_scr]
        sems = refs[n_in + 2 * k + n_out + n_scr:]
        ids = [pl.program_id(d) for d in range(len(grid))]
        first = ids[0] == 0
        last = ids[0] == grid[0] - 1
        for d in range(1, len(grid)):
            first = first & (ids[d] == 0)
            last = last & (ids[d] == grid[d] - 1)

        @pl.when(first)
        def _():
            comm.start(c_in, c_out, sems)

        body(*ins, *outs, *scr)

        @pl.when(last)
        def _():
            comm.finish(c_in, c_out, sems)

    res = pl.pallas_call(
        wrapped, name=name, grid=grid, in_specs=list(in_specs) + [ANY] * k, out_specs=list(out_specs) + [ANY] * k,
        out_shape=list(out_shape) + list(comm.out_shape), scratch_shapes=list(scratch_shapes) + list(comm.sems),
        compiler_params=params, input_output_aliases=aliases)(*args, *comm.arrays)
    return list(res[:n_out]), list(res[n_out:])


def _row_tile(rows):
    for t in (512, 256, 128, 64, 32, 16, 8):
        if rows % t == 0:
            return t
    return rows


def _pair_sum(name, g8, recv4, core):
    _, rows, cols = recv4.shape
    tr = _row_tile(rows)
    g42 = g8.reshape(4, 2, rows, cols)

    def body(c_ref, g_ref, r_ref, o_ref):
        del c_ref
        o_ref[...] = (g_ref[...].astype(F32) + r_ref[...].astype(F32)).astype(o_ref.dtype)

    return pl.pallas_call(
        body, name=name,
        grid_spec=pltpu.PrefetchScalarGridSpec(
            num_scalar_prefetch=1, grid=(4, rows // tr),
            in_specs=[pl.BlockSpec((None, None, tr, cols), lambda j, i, c_ref: (j, c_ref[0], i, 0)),
                      pl.BlockSpec((None, tr, cols), lambda j, i, c_ref: (j, i, 0))],
            out_specs=pl.BlockSpec((None, tr, cols), lambda j, i, c_ref: (j, i, 0))),
        out_shape=SDS(recv4.shape, g8.dtype),
        compiler_params=_cparams(("arbitrary", "arbitrary"), 32),
    )(core, g42, recv4)


def _add2(name, a, b):
    rows, cols = a.shape
    tr = _row_tile(rows)

    def body(a_ref, b_ref, o_ref):
        o_ref[...] = a_ref[...] + b_ref[...]

    blk = pl.BlockSpec((tr, cols), lambda i: (i, 0))
    return pl.pallas_call(body, name=name, grid=(rows // tr,), in_specs=[blk, blk], out_specs=blk,
                          out_shape=SDS(a.shape, a.dtype),
                          compiler_params=_cparams(("arbitrary",), 32))(a, b)


def _sum_terms(name, terms):
    k, rows, cols = terms.shape
    tr = _row_tile(rows)

    def body(r_ref, o_ref):
        acc = r_ref[0]
        for q in range(1, k):
            acc = acc + r_ref[q]
        o_ref[...] = acc

    return pl.pallas_call(body, name=name, grid=(rows // tr,),
                          in_specs=[pl.BlockSpec((k, tr, cols), lambda i: (0, i, 0))],
                          out_specs=pl.BlockSpec((tr, cols), lambda i: (i, 0)),
                          out_shape=SDS((rows, cols), terms.dtype),
                          compiler_params=_cparams(("arbitrary",), 32))(terms)


def _adam_update(g, w, m, v):
    c1 = 1.0 / (1.0 - ADAM_B1 ** ADAM_STEP)
    c2 = 1.0 / (1.0 - ADAM_B2 ** ADAM_STEP)
    mn = ADAM_B1 * m + (1.0 - ADAM_B1) * g
    vn = ADAM_B2 * v + (1.0 - ADAM_B2) * (g * g)
    delta = (-ADAM_LR) * ((mn * c1) / (jnp.sqrt(vn * c2) + ADAM_EPS) + ADAM_WD * w)
    return delta, mn, vn


def _adamw_many(name, gs, ws, ms, vs):
    n = len(gs)

    def body(*refs):
        for p in range(n):
            g, w, m, v = (refs[q * n + p][...] for q in range(4))
            d, mn, vn = _adam_update(g, w, m, v)
            refs[4 * n + p][...] = d
            refs[5 * n + p][...] = mn
            refs[6 * n + p][...] = vn

    full = [pl.BlockSpec(memory_space=pltpu.VMEM)] * n
    shapes = [SDS(w.shape, F32) for w in ws]
    res = pl.pallas_call(body, name=name, in_specs=full * 4, out_specs=full * 3, out_shape=shapes * 3,
                         compiler_params=pltpu.CompilerParams(vmem_limit_bytes=32 * MiB))(*gs, *ws, *ms, *vs)
    return [(res[p], res[n + p], res[2 * n + p]) for p in range(n)]


def _adamw(name, terms, w, m, v):
    k, rows, cols = terms.shape
    tr = _row_tile(rows)

    def body(t_ref, w_ref, m_ref, v_ref, g_ref, d_ref, mo_ref, vo_ref):
        g = t_ref[0].astype(F32)
        for q in range(1, k):
            g = g + t_ref[q].astype(F32)
        g_ref[...] = g
        d_ref[...], mo_ref[...], vo_ref[...] = _adam_update(g, w_ref[...], m_ref[...], v_ref[...])

    blk = pl.BlockSpec((tr, cols), lambda i: (i, 0))
    return pl.pallas_call(body, name=name, grid=(rows // tr,),
                          in_specs=[pl.BlockSpec((k, tr, cols), lambda i: (0, i, 0)), blk, blk, blk],
                          out_specs=[blk] * 4, out_shape=[SDS((rows, cols), F32)] * 4,
                          compiler_params=_cparams(("arbitrary",), 40))(terms, w, m, v)


def kernel(x, norm_mix_g, w_in, conv_w, conv_b, w_rgate, b_rgate, w_igate, b_igate, lru_lambda, w_out_a, sgu_ln_g, sgu_ln_b, sgu_w_s, sgu_b_s, w_out_b, w_out, norm_mlp_g, w_up, w_down, norm_final_g, loss_target, m_norm_mix_g, m_w_in, m_conv_w, m_conv_b, m_w_rgate, m_b_rgate, m_w_igate, m_b_igate, m_lru_lambda, m_w_out_a, m_sgu_ln_g, m_sgu_ln_b, m_sgu_w_s, m_sgu_b_s, m_w_out_b, m_w_out, m_norm_mlp_g, m_w_up, m_w_down, m_norm_final_g, v_norm_mix_g, v_w_in, v_conv_w, v_conv_b, v_w_rgate, v_b_rgate, v_w_igate, v_b_igate, v_lru_lambda, v_w_out_a, v_sgu_ln_g, v_sgu_ln_b, v_sgu_w_s, v_sgu_b_s, v_w_out_b, v_w_out, v_norm_mlp_g, v_w_up, v_w_down, v_norm_final_g):
    cx, cy, cc = _place()
    me = 4 * cx + 2 * cy + cc
    core = jnp.reshape(cc, (1,)).astype(jnp.int32)
    xs = x[0]
    tgt = loss_target[0]
    s = xs.shape[0]

    gate_shard = jnp.stack([w_rgate[0], w_igate[0]]).astype(BF16).reshape(2 * HEADS * 32, HEAD_DIM)
    vec_shard = jnp.concatenate([conv_w[0], b_rgate[0], b_igate[0]], axis=1)
    vec_shard = jnp.pad(vec_shard, ((0, 4), (0, 256 - vec_shard.shape[1])))
    shards = [w_in[0].astype(BF16), w_out_a[0].astype(BF16), w_out_b[0].astype(BF16), w_out[0].astype(BF16),
              w_up[0].astype(BF16), w_down[0].astype(BF16), gate_shard, vec_shard]
    w_in_g, gate_g, vec_g = _run_plan("gather_first", _gather_plan([shards[0], shards[6], shards[7]]))
    gates = gate_g.reshape(N_SLOT, 2, HEADS, 32, HEAD_DIM).transpose(1, 2, 0, 3, 4).reshape(2, HEADS, HEAD_DIM, HEAD_DIM)
    w_r_f, w_i_f = gates[0], gates[1]
    conv_w_f = vec_g[:, 0:4, 0:128].transpose(1, 0, 2).reshape(CONV_K, D)
    b_r_f = vec_g[:, 0:4, 128:160].transpose(1, 0, 2).reshape(1, D)
    b_i_f = vec_g[:, 0:4, 160:192].transpose(1, 0, 2).reshape(1, D)
    b_s_t = jnp.transpose(sgu_b_s[0])

    (z, n1), (w_oa_g, w_ob_g, w_out_g, w_up_g) = _in_proj(xs, norm_mix_g, w_in_g, comm=_gather_plan(shards[1:5]))
    (ya, hs), (w_down_g,) = _branch_a_fwd(z, conv_w_f, conv_b, w_r_f, b_r_f, w_i_f, b_i_f, lru_lambda,
                                          comm=_gather_plan(shards[5:6]))
    w_oa_f = w_oa_g.reshape(D, D)
    w_ob_f = w_ob_g.reshape(D, D)
    w_out_f = w_out_g.reshape(D, D)
    w_down_f = w_down_g.reshape(N_SLOT * FF_COLS, D)
    yb = _branch_b_fwd(z, sgu_ln_g, sgu_ln_b, sgu_w_s[0], b_s_t)
    pa, pb, merged, h1 = _merge_out(ya, yb, z, xs, w_oa_f, w_ob_f, w_out_f)
    gf2 = norm_final_g.reshape(1, D)
    r_act, n2, dh2, loss_acc, d_gfin = _mlp_fwd(h1, norm_mlp_g, w_up_g, w_down_f, gf2, tgt)

    def pair(names, grads, recv):
        return [_pair_sum("pair_sum_" + nm, g, r, core) for nm, g, r in zip(names, grads, recv)]

    g_down = _wgrad("wgrad_down", r_act, dh2, N_SLOT, True, False, square_a=True)
    (df, dh1, d_gmlp), (r_down,) = _mlp_bwd(dh2, r_act, w_down_f, w_up_g, h1, norm_mlp_g,
                                            comm=_sibling_plan([g_down]))
    (p_down,) = pair(["down"], [g_down], [r_down])
    g_up = _wgrad("wgrad_up", n2, df, N_SLOT, False, True)
    g_out = _wgrad("wgrad_out", merged, dh1, 1, False, False).reshape(N_SLOT, D // N_SLOT, D)
    (dz, dpa, dpb, dya, dyb), (got_down, r_up, r_out) = _merge_bwd(
        dh1, z, pa, pb, w_out_f, w_oa_f, w_ob_f, comm=_join(_chips_plan([p_down]), _sibling_plan([g_up, g_out])))
    p_up, p_out = pair(["up", "out"], [g_up, g_out], [r_up, r_out])
    g_oa = _wgrad("wgrad_out_a", ya, dpa, 1, False, False).reshape(N_SLOT, D // N_SLOT, D)
    g_ob = _wgrad("wgrad_out_b", yb, dpb, 1, False, False).reshape(N_SLOT, D // N_SLOT, D)
    (dz, d_ws, d_bs, d_ln), (got_up, r_oa, r_ob) = _branch_b_bwd(
        dz, dyb, z, sgu_ln_g, sgu_ln_b, sgu_w_s[0], b_s_t,
        comm=_join(_chips_plan([p_up]), _sibling_plan([g_oa, g_ob])))
    p_oa, p_ob = pair(["out_a", "out_b"], [g_oa, g_ob], [r_oa, r_ob])
    (dz, d_vec, d_wr, d_wi), (got_out, got_oa, got_ob) = _branch_a_bwd(
        dz, dya, z, hs, conv_w_f, conv_b, w_r_f, b_r_f, w_i_f, b_i_f, lru_lambda,
        comm=_chips_plan([p_out, p_oa, p_ob]))
    g_in = _wgrad("wgrad_in", n1, dz, N_SLOT, False, True)
    g_gate = jnp.stack([d_wr, d_wi]).reshape(2, HEADS, N_SLOT, 32, HEAD_DIM).transpose(2, 0, 1, 3, 4)
    g_gate = g_gate.reshape(N_SLOT, 2 * HEADS * 32, HEAD_DIM).astype(BF16)

    d_bs_row = jnp.pad(d_bs[:, :, 0].reshape(1, GROUPS * CHUNK), ((0, 0), (0, D - GROUPS * CHUNK)))
    vecs = jnp.concatenate([d_vec, jnp.concatenate([d_ln[0:2], d_gmlp, d_gfin, d_bs_row, jnp.zeros((3, D), F32)])])
    d_ws2 = d_ws.reshape(GROUPS * CHUNK, CHUNK)
    r_in, r_gate, r_vecs, r_ws = _run_plan("rs_sibling_in", _sibling_plan([g_in, g_gate], [vecs, d_ws2]))
    p_in, p_gate = pair(["in", "gate"], [g_in, g_gate], [r_in, r_gate])
    vecs_chip = _add2("pair_sum_vecs", vecs, r_vecs)
    ws_chip = _add2("pair_sum_ws", d_ws2, r_ws)
    (dx, d_gmix), (got_in, got_gate, got_vecs, got_ws) = _in_bwd(
        dz, w_in_g, xs, dh1, norm_mix_g, comm=_chips_plan([p_in, p_gate], [vecs_chip, ws_chip]))
    vecs_sum = _sum_terms("sum_vecs", got_vecs)
    last = jnp.concatenate([d_gmix, jnp.pad(loss_acc[0:1], ((0, 0), (0, D - 128))), jnp.zeros((6, D), F32)])
    (last_all,) = _run_plan("exchange_last", _exchange_plan(last))
    last_sum = _sum_terms("sum_last", last_all)
    loss = last_sum[1, 0]
    got = [got_in, got_oa, got_ob, got_out, got_up, got_down, got_gate]

    def step(nm, terms, w, m, v, rows, cols):
        g, d, mn, vn = _adamw("adamw_" + nm, terms.reshape(4, rows, cols), w.reshape(rows, cols),
                              m.reshape(rows, cols), v.reshape(rows, cols))
        return [a.reshape(w.shape) for a in (g, d, mn, vn)]

    o_in = step("in", got[0], w_in, m_w_in, v_w_in, D, W_IN_COLS)
    o_oa = step("out_a", got[1], w_out_a, m_w_out_a, v_w_out_a, D // N_SLOT, D)
    o_ob = step("out_b", got[2], w_out_b, m_w_out_b, v_w_out_b, D // N_SLOT, D)
    o_out = step("out", got[3], w_out, m_w_out, v_w_out, D // N_SLOT, D)
    o_up = step("up", got[4], w_up, m_w_up, v_w_up, D, FF_COLS)
    o_down = step("down", got[5], w_down, m_w_down, v_w_down, FF_COLS, D)
    gate_w = jnp.stack([w_rgate[0], w_igate[0]]).reshape(2 * HEADS * 32, HEAD_DIM)
    gate_m = jnp.stack([m_w_rgate[0], m_w_igate[0]]).reshape(2 * HEADS * 32, HEAD_DIM)
    gate_v = jnp.stack([v_w_rgate[0], v_w_igate[0]]).reshape(2 * HEADS * 32, HEAD_DIM)
    o_gate = _adamw("adamw_gate", got[6], gate_w, gate_m, gate_v)
    o_gate = [a.reshape(2, 1, HEADS, 32, HEAD_DIM) for a in o_gate]
    o_wr = [a[0] for a in o_gate]
    o_wi = [a[1] for a in o_gate]

    def own(full, width):
        return lax.dynamic_slice_in_dim(full, me * width, width, axis=1)

    small_g = {
        "norm_mix_g": last_sum[0:1], "conv_w": own(vecs_sum[0:4], 128), "conv_b": vecs_sum[4:5],
        "b_rgate": own(vecs_sum[5:6].reshape(HEADS, HEAD_DIM), 32),
        "b_igate": own(vecs_sum[6:7].reshape(HEADS, HEAD_DIM), 32),
        "lru_lambda": vecs_sum[7:8], "sgu_ln_g": vecs_sum[8:9], "sgu_ln_b": vecs_sum[9:10],
        "norm_mlp_g": vecs_sum[10:11], "norm_final_g": vecs_sum[11:12],
        "sgu_b_s": vecs_sum[12, 0:GROUPS * CHUNK].reshape(GROUPS, CHUNK),
    }
    small_w = {"norm_mix_g": (norm_mix_g, m_norm_mix_g, v_norm_mix_g), "conv_w": (conv_w, m_conv_w, v_conv_w),
               "conv_b": (conv_b, m_conv_b, v_conv_b), "b_rgate": (b_rgate, m_b_rgate, v_b_rgate),
               "b_igate": (b_igate, m_b_igate, v_b_igate), "lru_lambda": (lru_lambda, m_lru_lambda, v_lru_lambda),
               "sgu_ln_g": (sgu_ln_g, m_sgu_ln_g, v_sgu_ln_g), "sgu_ln_b": (sgu_ln_b, m_sgu_ln_b, v_sgu_ln_b),
               "norm_mlp_g": (norm_mlp_g, m_norm_mlp_g, v_norm_mlp_g),
               "norm_final_g": (norm_final_g, m_norm_final_g, v_norm_final_g),
               "sgu_b_s": (sgu_b_s, m_sgu_b_s, v_sgu_b_s), "sgu_w_s": (sgu_w_s, m_sgu_w_s, v_sgu_w_s)}
    order = list(small_g)
    as2d = lambda k, a: a.reshape(small_g[k].shape)
    upd = _adamw_many("adamw_small", [small_g[k] for k in order], *[[as2d(k, small_w[k][q]) for k in order]
                                                                     for q in range(3)])
    o_small = {k: [a.reshape(small_w[k][0].shape) for a in (small_g[k],) + u] for k, u in zip(order, upd)}
    ws3 = [a[0].reshape(GROUPS * CHUNK, CHUNK) for a in small_w.pop("sgu_w_s")]
    o_small["sgu_w_s"] = [a.reshape(sgu_w_s.shape) for a in _adamw("adamw_ws", got_ws, *ws3)]

    per_weight = {"norm_mix_g": o_small["norm_mix_g"], "w_in": o_in, "conv_w": o_small["conv_w"],
                  "conv_b": o_small["conv_b"], "w_rgate": o_wr, "b_rgate": o_small["b_rgate"], "w_igate": o_wi,
                  "b_igate": o_small["b_igate"], "lru_lambda": o_small["lru_lambda"], "w_out_a": o_oa,
                  "sgu_ln_g": o_small["sgu_ln_g"], "sgu_ln_b": o_small["sgu_ln_b"], "sgu_w_s": o_small["sgu_w_s"],
                  "sgu_b_s": o_small["sgu_b_s"], "w_out_b": o_ob, "w_out": o_out, "norm_mlp_g": o_small["norm_mlp_g"],
                  "w_up": o_up, "w_down": o_down, "norm_final_g": o_small["norm_final_g"]}
    names_w = list(per_weight)
    return (loss, dx[None], *[per_weight[k][0] for k in names_w], *[per_weight[k][1] for k in names_w],
            *[per_weight[k][2] for k in names_w], *[per_weight[k][3] for k in names_w])
```

```python
import jax
import jax.numpy as jnp
from jax import lax
from jax.experimental import pallas as pl
from jax.experimental.pallas import tpu as pltpu

F32 = jnp.float32
BF16 = jnp.bfloat16
SDS = jax.ShapeDtypeStruct
MESH = pl.DeviceIdType.MESH
ANY = pl.BlockSpec(memory_space=pl.ANY)

D = 1024
N_SLOT = 8
W_IN_COLS = 768
FF_COLS = 512
HEADS, HEAD_DIM = 4, 256
GROUPS, GROUP_DIM = 4, 256
CHUNK = 128
CONV_K = 4
NORM_EPS = 1e-6
LN_EPS = 1e-5
LRU_C = 8.0
ADAM_LR, ADAM_B1, ADAM_B2, ADAM_EPS, ADAM_WD, ADAM_STEP = 0.001, 0.9, 0.999, 1e-08, 0.01, 10

TM_ROWS = 1024
TM_MERGE = 512
T_BRANCH = 256
MiB = 1024 * 1024

_GELU_C = 0.7978845608028654
_GELU_A = 0.044715


def _cparams(sem, vmem_mib):
    return pltpu.CompilerParams(dimension_semantics=sem, vmem_limit_bytes=vmem_mib * MiB)


def _gelu(x):
    t = jnp.tanh(_GELU_C * (x + _GELU_A * x * x * x))
    return 0.5 * x * (1.0 + t)


def _gelu_and_grad(x):
    x2 = x * x
    t = jnp.tanh(_GELU_C * x * (1.0 + _GELU_A * x2))
    g = 0.5 * x * (1.0 + t)
    dg = 0.5 * (1.0 + t) + 0.5 * x * (1.0 - t * t) * _GELU_C * (1.0 + 3.0 * _GELU_A * x2)
    return g, dg


def _softplus(x):
    return jnp.maximum(x, 0.0) + jnp.log1p(jnp.exp(-jnp.abs(x)))


def _dot(a, b):
    return jnp.dot(a, b, preferred_element_type=F32)


def _dot_nt(a, b):
    return lax.dot_general(a, b, (((1,), (1,)), ((), ())), preferred_element_type=F32)


def _dot_tn(a, b):
    return lax.dot_general(a, b, (((0,), (0,)), ((), ())), preferred_element_type=F32)


def _rows_shifted(prev8, cur, k):
    ext = jnp.concatenate([prev8, cur], axis=0)
    return pltpu.roll(ext, k, 0)[8:]


def _rows_advanced(cur, next8, k):
    t = cur.shape[0]
    ext = jnp.concatenate([cur, next8], axis=0)
    return pltpu.roll(ext, t + 8 - k, 0)[:t]


def _slot_order(x, y, c):
    chip = 2 * x + y
    order = [2 * chip + c, 2 * chip + 1 - c]
    for _, _, pc in _other_chips(x, y):
        order += [2 * pc + c, 2 * pc + 1 - c]
    return jnp.stack(order).astype(jnp.int32)


def _in_proj(x, g_mix, w_in_own, order, comm=None):
    s = x.shape[0]
    tm = min(TM_ROWS, s)
    ni = s // tm

    def body(order_ref, x_ref, g_ref, own_ref, z_ref, n_ref, wg_ref, n_s, w_s, send_sems, recv_sems, local_sems):
        i, j = pl.program_id(0), pl.program_id(1)
        px, py, c = _place()
        chip = 2 * px + py
        me = 2 * chip + c
        sib = (px, py, 1 - c)
        chips = _other_chips(px, py)

        def rc(k, src, blk, to):
            return pltpu.make_async_remote_copy(src_ref=src, dst_ref=w_s.at[blk], send_sem=send_sems.at[k],
                                                recv_sem=recv_sems.at[k], device_id=to, device_id_type=MESH)

        own_in = pltpu.make_async_copy(own_ref, w_s.at[me], local_sems.at[0])
        sends = [rc(0, own_ref, me, sib)] + [rc(1 + q, own_ref, me, (qx, qy, c)) for q, (qx, qy, _) in enumerate(chips)]
        passed = [rc(4 + q, w_s.at[2 * pc + c], 2 * pc + c, sib) for q, (_, _, pc) in enumerate(chips)]
        keep = pltpu.make_async_copy(w_s, wg_ref, local_sems.at[1])

        @pl.when((i == 0) & (j == 0))
        def _():
            own_in.start()
            for cp in sends:
                cp.start()
            own_in.wait()

        @pl.when((i == 0) & (j == 1))
        def _():
            rc(0, own_ref, 2 * chip + 1 - c, sib).wait_recv()

        for q, (_, _, pc) in enumerate(chips):
            @pl.when((i == 0) & (j == 2 + 2 * q))
            def _():
                rc(1 + q, own_ref, 2 * pc + c, sib).wait_recv()
                passed[q].start()

            @pl.when((i == 0) & (j == 3 + 2 * q))
            def _():
                rc(4 + q, own_ref, 2 * pc + 1 - c, sib).wait_recv()

        @pl.when(j == 0)
        def _():
            xv = x_ref[...]
            rstd = lax.rsqrt(jnp.mean(xv * xv, axis=-1, keepdims=True) + NORM_EPS)
            nb = (xv * rstd * g_ref[...]).astype(BF16)
            n_s[...] = nb
            n_ref[...] = nb

        z_ref[...] = _dot(n_s[...], w_s[order_ref[j]]).astype(BF16)

        @pl.when((i == 0) & (j == N_SLOT - 1))
        def _():
            keep.start()

        @pl.when((i == ni - 1) & (j == N_SLOT - 1))
        def _():
            for cp in sends + passed:
                cp.wait_send()
            keep.wait()

    (z, n1, w_in_g), extra = _call(
        body, name="in_proj", grid=(ni, N_SLOT), prefetch=(order,),
        in_specs=[pl.BlockSpec((tm, D), lambda i, j, o: (i, 0)),
                  pl.BlockSpec((1, D), lambda i, j, o: (0, 0)), ANY],
        out_specs=[pl.BlockSpec((tm, W_IN_COLS), lambda i, j, o: (i, o[j])),
                   pl.BlockSpec((tm, D), lambda i, j, o: (i, 0)), ANY],
        out_shape=[SDS((s, N_SLOT * W_IN_COLS), BF16), SDS((s, D), BF16), SDS((N_SLOT, D, W_IN_COLS), BF16)],
        scratch_shapes=[pltpu.VMEM((tm, D), BF16), pltpu.VMEM((N_SLOT, D, W_IN_COLS), BF16),
                        pltpu.SemaphoreType.DMA((7,)), pltpu.SemaphoreType.DMA((7,)), pltpu.SemaphoreType.DMA((2,))],
        params=_cparams(("arbitrary", "arbitrary"), 56), args=(x, g_mix, w_in_own), comm=comm)
    return (z, n1, w_in_g), extra


def _lru_gates(xc, xcb, wr_ref, br, wi_ref, bi, sp_lam, a_s, b_s, r_s=None, i_s=None, m_s=None):
    for h in range(HEADS):
        sl = slice(h * HEAD_DIM, (h + 1) * HEAD_DIM)
        r = jax.nn.sigmoid(_dot(xcb[:, sl], wr_ref[h]) + br[:, sl])
        ig = jax.nn.sigmoid(_dot(xcb[:, sl], wi_ref[h]) + bi[:, sl])
        log_a = (-LRU_C) * r * sp_lam[:, sl]
        a = jnp.exp(log_a)
        mult = jnp.sqrt(-jnp.tanh(log_a) * (a * a + 1.0))
        a_s[:, sl] = a
        b_s[:, sl] = xc[:, sl] * ig * mult
        if r_s is not None:
            r_s[:, sl] = r
            i_s[:, sl] = ig
            m_s[:, sl] = mult


def _conv_fwd(xa, prev8, cw, cb):
    xc = cb + cw[0:1, :] * xa
    for k in range(1, CONV_K):
        xc = xc + cw[k:k + 1, :] * _rows_shifted(prev8, xa, k)
    return xc


def _branch_a_fwd(z, conv_w, conv_b, w_r, b_r, w_i, b_i, lam, comm=None):
    s = z.shape[0]
    ta = min(T_BRANCH, s)
    per16 = ta // 16

    def body(xa_ref, xp_ref, ga_ref, cw_ref, cb_ref, wr_ref, br_ref, wi_ref, bi_ref, lam_ref,
             ya_ref, hs_ref, a_s, b_s, h_s, carry_s):
        i = pl.program_id(0)

        @pl.when(i == 0)
        def _():
            carry_s[...] = jnp.zeros_like(carry_s)

        xa = xa_ref[...].astype(F32)
        prev8 = jnp.where(i > 0, xp_ref[...].astype(F32)[8:16], 0.0)
        xc = _conv_fwd(xa, prev8, cw_ref[...], cb_ref[...])
        sp_lam = _softplus(-lam_ref[...])
        _lru_gates(xc, xc.astype(BF16), wr_ref, br_ref[...], wi_ref, bi_ref[...], sp_lam, a_s, b_s)

        row = lax.broadcasted_iota(jnp.int32, (8, D), 0)

        def group(g, carry):
            off = pl.multiple_of(g * 8, 8)
            a8 = a_s[pl.ds(off, 8), :]
            b8 = b_s[pl.ds(off, 8), :]
            for d in (1, 2, 4):
                a_sh = jnp.where(row >= d, pltpu.roll(a8, d, 0), 1.0)
                b_sh = jnp.where(row >= d, pltpu.roll(b8, d, 0), 0.0)
                b8 = a8 * b_sh + b8
                a8 = a8 * a_sh
            h8 = b8 + a8 * carry
            h_s[pl.ds(off, 8), :] = h8
            return jnp.broadcast_to(h8[7:8, :], (8, D))

        carry_s[...] = lax.fori_loop(0, ta // 8, group, carry_s[...])
        hs = h_s[...]
        hs_ref[...] = hs.astype(BF16)
        ya_ref[...] = (hs * _gelu(ga_ref[...].astype(F32))).astype(BF16)

    vec = pl.BlockSpec((1, D), lambda i: (0, 0))
    gate = pl.BlockSpec((HEADS, HEAD_DIM, HEAD_DIM), lambda i: (0, 0, 0))
    return _call(
        body, name="branch_a_fwd", grid=(s // ta,),
        in_specs=[pl.BlockSpec((ta, D), lambda i: (i, 0)),
                  pl.BlockSpec((16, D), lambda i: (jnp.maximum(i * per16 - 1, 0), 0)),
                  pl.BlockSpec((ta, D), lambda i: (i, 1)),
                  pl.BlockSpec((CONV_K, D), lambda i: (0, 0)), vec, gate, vec, gate, vec, vec],
        out_specs=[pl.BlockSpec((ta, D), lambda i: (i, 0)), pl.BlockSpec((ta, D), lambda i: (i, 0))],
        out_shape=[SDS((s, D), BF16), SDS((s, D), BF16)],
        scratch_shapes=[pltpu.VMEM((ta, D), F32), pltpu.VMEM((ta, D), F32), pltpu.VMEM((ta, D), F32),
                        pltpu.VMEM((8, D), F32)],
        params=_cparams(("arbitrary",), 40), args=(z, z, z, conv_w, conv_b, w_r, b_r, w_i, b_i, lam), comm=comm)


def _sgu_common(ub, vb, lg, lb, with_grad):
    if with_grad:
        u, du = _gelu_and_grad(ub)
        v, dv = _gelu_and_grad(vb)
    else:
        u, v, du, dv = _gelu(ub), _gelu(vb), None, None
    mu = jnp.mean(v, axis=-1, keepdims=True)
    vc = v - mu
    rstd = lax.rsqrt(jnp.mean(vc * vc, axis=-1, keepdims=True) + LN_EPS)
    vhat = vc * rstd
    vln = vhat * lg + lb
    return u, du, dv, rstd, vhat, vln


def _masked_ws(ws_ref):
    t = lax.broadcasted_iota(jnp.int32, (CHUNK, CHUNK), 0)
    c = lax.broadcasted_iota(jnp.int32, (CHUNK, CHUNK), 1)
    keep = c <= t
    return [jnp.where(keep, ws_ref[g], 0.0).astype(BF16) for g in range(GROUPS)]


def _branch_b_fwd(z, ln_g, ln_b, w_s, b_s_t):
    s = z.shape[0]
    tb = min(T_BRANCH, s)

    def body(ub_ref, vb_ref, lg_ref, lb_ref, ws_ref, bs_ref, yb_ref):
        u, _, _, _, _, vln = _sgu_common(ub_ref[...].astype(F32), vb_ref[...].astype(F32),
                                         lg_ref[...], lb_ref[...], False)
        vlnb = vln.astype(BF16)
        wm = _masked_ws(ws_ref)
        bs = bs_ref[...]
        for c in range(tb // CHUNK):
            rs = slice(c * CHUNK, (c + 1) * CHUNK)
            for g in range(GROUPS):
                cs = slice(g * GROUP_DIM, (g + 1) * GROUP_DIM)
                sp = _dot(wm[g], vlnb[rs, cs]) + bs[:, g:g + 1]
                yb_ref[rs, cs] = (u[rs, cs] * sp).astype(BF16)

    vec = pl.BlockSpec((1, D), lambda i: (0, 0))
    return pl.pallas_call(
        body, name="branch_b_fwd", grid=(s // tb,),
        in_specs=[pl.BlockSpec((tb, D), lambda i: (i, 2)), pl.BlockSpec((tb, D), lambda i: (i, 3)), vec, vec,
                  pl.BlockSpec((GROUPS, CHUNK, CHUNK), lambda i: (0, 0, 0)),
                  pl.BlockSpec((CHUNK, GROUPS), lambda i: (0, 0))],
        out_specs=pl.BlockSpec((tb, D), lambda i: (i, 0)),
        out_shape=SDS((s, D), BF16),
        compiler_params=_cparams(("arbitrary",), 40),
    )(z, z, ln_g, ln_b, w_s, b_s_t)


def _merge_out(ya, yb, z, x, w_oa, w_ob, w_out, comm=None):
    s = x.shape[0]
    tm = min(TM_MERGE, s)

    def body(ya_ref, yb_ref, ma_ref, mb_ref, x_ref, woa_ref, wob_ref, wo_ref, pa_ref, pb_ref, mg_ref, h1_ref):
        pa = _dot(ya_ref[...], woa_ref[...])
        pb = _dot(yb_ref[...], wob_ref[...])
        merged = (jax.nn.sigmoid(ma_ref[...].astype(F32)) * pa
                  + jax.nn.sigmoid(mb_ref[...].astype(F32)) * pb).astype(BF16)
        pa_ref[...] = pa.astype(BF16)
        pb_ref[...] = pb.astype(BF16)
        mg_ref[...] = merged
        h1_ref[...] = x_ref[...] + _dot(merged, wo_ref[...])

    row = pl.BlockSpec((tm, D), lambda i: (i, 0))
    wsp = pl.BlockSpec((D, D), lambda i: (0, 0))
    return _call(
        body, name="merge_out", grid=(s // tm,),
        in_specs=[row, row, pl.BlockSpec((tm, D), lambda i: (i, 4)), pl.BlockSpec((tm, D), lambda i: (i, 5)),
                  row, wsp, wsp, wsp],
        out_specs=[row, row, row, row],
        out_shape=[SDS((s, D), BF16), SDS((s, D), BF16), SDS((s, D), BF16), SDS((s, D), F32)], scratch_shapes=[],
        params=_cparams(("arbitrary",), 48), args=(ya, yb, z, z, x, w_oa, w_ob, w_out), comm=comm)


def _mlp_fwd(h1, g_mlp, w_up_g, w_down, g_fin, tgt):
    s = h1.shape[0]
    tm = min(TM_ROWS, s)
    nj = N_SLOT

    def body(h1_ref, gm_ref, wu_ref, wd_ref, gf_ref, t_ref, r_ref, n2_ref, dh2_ref, loss_ref, dgf_ref, n2_s, acc_s):
        i, j = pl.program_id(0), pl.program_id(1)

        @pl.when(j == 0)
        def _():
            hv = h1_ref[...]
            rstd = lax.rsqrt(jnp.mean(hv * hv, axis=-1, keepdims=True) + NORM_EPS)
            nb = (hv * rstd * gm_ref[...]).astype(BF16)
            n2_s[...] = nb
            n2_ref[...] = nb
            acc_s[...] = jnp.zeros_like(acc_s)

        @pl.when((i == 0) & (j == 0))
        def _():
            loss_ref[...] = jnp.zeros_like(loss_ref)
            dgf_ref[...] = jnp.zeros_like(dgf_ref)

        r = jnp.maximum(_dot(n2_s[...], wu_ref[...]), 0.0)
        r_ref[...] = r.astype(BF16)
        acc_s[...] += _dot((r * r).astype(BF16), wd_ref[...])

        @pl.when(j == nj - 1)
        def _():
            h2 = h1_ref[...] + acc_s[...]
            rstd = lax.rsqrt(jnp.mean(h2 * h2, axis=-1, keepdims=True) + NORM_EPS)
            hh = h2 * rstd
            gf = gf_ref[...]
            e = hh * gf - t_ref[...]
            loss_ref[...] += jnp.sum(e * e) * (0.5 / D)
            dy = e * (1.0 / D)
            dgf_ref[...] += jnp.sum(dy * hh, axis=0, keepdims=True)
            dhh = dy * gf
            dh2_ref[...] = rstd * (dhh - hh * jnp.mean(dhh * hh, axis=-1, keepdims=True))

    row = pl.BlockSpec((tm, D), lambda i, j: (i, 0))
    vec = pl.BlockSpec((1, D), lambda i, j: (0, 0))
    return pl.pallas_call(
        body, name="mlp_fwd", grid=(s // tm, nj),
        in_specs=[row, vec, pl.BlockSpec((None, D, FF_COLS), lambda i, j: (j, 0, 0)),
                  pl.BlockSpec((FF_COLS, D), lambda i, j: (j, 0)), vec, row],
        out_specs=[pl.BlockSpec((tm, FF_COLS), lambda i, j: (i, j)), row, row,
                   pl.BlockSpec((8, 128), lambda i, j: (0, 0)), vec],
        out_shape=[SDS((s, nj * FF_COLS), BF16), SDS((s, D), BF16), SDS((s, D), F32),
                   SDS((8, 128), F32), SDS((1, D), F32)],
        scratch_shapes=[pltpu.VMEM((tm, D), BF16), pltpu.VMEM((tm, D), F32)],
        compiler_params=_cparams(("arbitrary", "arbitrary"), 52),
    )(h1, g_mlp, w_up_g, w_down, g_fin, tgt)


def _mlp_bwd(dh2, r, w_down, w_up_g, h1, g_mlp, comm=None):
    s = h1.shape[0]
    tm = min(TM_ROWS, s)
    nj = N_SLOT

    def body(dh2_ref, r_ref, wd_ref, wu_ref, h1_ref, gm_ref, df_ref, dh1_ref, dgm_ref, dh2b_s, acc_s):
        i, j = pl.program_id(0), pl.program_id(1)

        @pl.when(j == 0)
        def _():
            dh2b_s[...] = dh2_ref[...].astype(BF16)
            acc_s[...] = jnp.zeros_like(acc_s)

        @pl.when((i == 0) & (j == 0))
        def _():
            dgm_ref[...] = jnp.zeros_like(dgm_ref)

        d_act = _dot_nt(dh2b_s[...], wd_ref[...])
        df = (d_act * (2.0 * r_ref[...].astype(F32))).astype(BF16)
        df_ref[...] = df
        acc_s[...] += _dot_nt(df, wu_ref[...])

        @pl.when(j == nj - 1)
        def _():
            hv = h1_ref[...]
            rstd = lax.rsqrt(jnp.mean(hv * hv, axis=-1, keepdims=True) + NORM_EPS)
            hh = hv * rstd
            dn2 = acc_s[...]
            dgm_ref[...] += jnp.sum(dn2 * hh, axis=0, keepdims=True)
            dhat = dn2 * gm_ref[...]
            dh1_ref[...] = dh2_ref[...] + rstd * (dhat - hh * jnp.mean(dhat * hh, axis=-1, keepdims=True))

    row = pl.BlockSpec((tm, D), lambda i, j: (i, 0))
    vec = pl.BlockSpec((1, D), lambda i, j: (0, 0))
    ffb = pl.BlockSpec((tm, FF_COLS), lambda i, j: (i, j))
    return _call(
        body, name="mlp_bwd", grid=(s // tm, nj),
        in_specs=[row, ffb, pl.BlockSpec((FF_COLS, D), lambda i, j: (j, 0)),
                  pl.BlockSpec((None, D, FF_COLS), lambda i, j: (j, 0, 0)), row, vec],
        out_specs=[ffb, row, vec],
        out_shape=[SDS((s, nj * FF_COLS), BF16), SDS((s, D), F32), SDS((1, D), F32)],
        scratch_shapes=[pltpu.VMEM((tm, D), BF16), pltpu.VMEM((tm, D), F32)],
        params=_cparams(("arbitrary", "arbitrary"), 52), args=(dh2, r, w_down, w_up_g, h1, g_mlp), comm=comm)


def _merge_bwd(dh1, z, pa, pb, w_out, w_oa, w_ob, comm=None):
    s = dh1.shape[0]
    tm = min(TM_MERGE, s)

    def body(dh1_ref, ma_ref, mb_ref, pa_ref, pb_ref, wo_ref, woa_ref, wob_ref,
             dz_ref, dpa_ref, dpb_ref, dya_ref, dyb_ref):
        dm = _dot_nt(dh1_ref[...].astype(BF16), wo_ref[...])
        sa = jax.nn.sigmoid(ma_ref[...].astype(F32))
        sb = jax.nn.sigmoid(mb_ref[...].astype(F32))
        dpa = (dm * sa).astype(BF16)
        dpb = (dm * sb).astype(BF16)
        dz_ref[:, 0:D] = (dm * pa_ref[...].astype(F32) * sa * (1.0 - sa)).astype(BF16)
        dz_ref[:, D:2 * D] = (dm * pb_ref[...].astype(F32) * sb * (1.0 - sb)).astype(BF16)
        dpa_ref[...] = dpa
        dpb_ref[...] = dpb
        dya_ref[...] = _dot_nt(dpa, woa_ref[...]).astype(BF16)
        dyb_ref[...] = _dot_nt(dpb, wob_ref[...]).astype(BF16)

    row = pl.BlockSpec((tm, D), lambda i: (i, 0))
    wsp = pl.BlockSpec((D, D), lambda i: (0, 0))
    return _call(
        body, name="merge_bwd", grid=(s // tm,),
        in_specs=[row, pl.BlockSpec((tm, D), lambda i: (i, 4)), pl.BlockSpec((tm, D), lambda i: (i, 5)),
                  row, row, wsp, wsp, wsp],
        out_specs=[pl.BlockSpec((tm, 2 * D), lambda i: (i, 2)), row, row, row, row],
        out_shape=[SDS((s, 6 * D), BF16)] + [SDS((s, D), BF16)] * 4, scratch_shapes=[],
        params=_cparams(("arbitrary",), 48), args=(dh1, z, z, pa, pb, w_out, w_oa, w_ob), comm=comm)


def _branch_b_bwd(dz, dyb, z, ln_g, ln_b, w_s, b_s_t, comm=None):
    s = z.shape[0]
    tb = min(T_BRANCH, s)

    def body(dz_in, dyb_ref, ub_ref, vb_ref, lg_ref, lb_ref, ws_ref, bs_ref,
             dz_ref, dws_ref, dbs_ref, dln_ref, du_s, dvln_s):
        del dz_in

        @pl.when(pl.program_id(0) == 0)
        def _():
            dws_ref[...] = jnp.zeros_like(dws_ref)
            dbs_ref[...] = jnp.zeros_like(dbs_ref)
            dln_ref[...] = jnp.zeros_like(dln_ref)

        lg = lg_ref[...]
        u, du, dv, rstd, vhat, vln = _sgu_common(ub_ref[...].astype(F32), vb_ref[...].astype(F32),
                                                 lg, lb_ref[...], True)
        vlnb = vln.astype(BF16)
        dyb_v = dyb_ref[...].astype(F32)
        wm = _masked_ws(ws_ref)
        keep = (lax.broadcasted_iota(jnp.int32, (CHUNK, CHUNK), 1)
                <= lax.broadcasted_iota(jnp.int32, (CHUNK, CHUNK), 0))
        bs = bs_ref[...]
        for c in range(tb // CHUNK):
            rs = slice(c * CHUNK, (c + 1) * CHUNK)
            for g in range(GROUPS):
                cs = slice(g * GROUP_DIM, (g + 1) * GROUP_DIM)
                v_blk = vlnb[rs, cs]
                sp = _dot(wm[g], v_blk) + bs[:, g:g + 1]
                d_sp = dyb_v[rs, cs] * u[rs, cs]
                d_spb = d_sp.astype(BF16)
                du_s[rs, cs] = dyb_v[rs, cs] * sp
                dvln_s[rs, cs] = _dot_tn(wm[g], d_spb)
                dws_ref[g] += jnp.where(keep, _dot_nt(d_spb, v_blk), 0.0)
                dbs_ref[g] += jnp.broadcast_to(jnp.sum(d_sp, axis=-1, keepdims=True), (CHUNK, CHUNK))
        dvln = dvln_s[...]
        dln_ref[0:1, :] += jnp.sum(dvln * vhat, axis=0, keepdims=True)
        dln_ref[1:2, :] += jnp.sum(dvln, axis=0, keepdims=True)
        dvh = dvln * lg
        d_v = rstd * (dvh - jnp.mean(dvh, axis=-1, keepdims=True)
                      - vhat * jnp.mean(dvh * vhat, axis=-1, keepdims=True))
        dz_ref[:, 0:D] = (du_s[...] * du).astype(BF16)
        dz_ref[:, D:2 * D] = (d_v * dv).astype(BF16)

    vec = pl.BlockSpec((1, D), lambda i: (0, 0))
    sq = pl.BlockSpec((GROUPS, CHUNK, CHUNK), lambda i: (0, 0, 0))
    return _call(
        body, name="branch_b_bwd", grid=(s // tb,),
        in_specs=[ANY, pl.BlockSpec((tb, D), lambda i: (i, 0)),
                  pl.BlockSpec((tb, D), lambda i: (i, 2)), pl.BlockSpec((tb, D), lambda i: (i, 3)), vec, vec, sq,
                  pl.BlockSpec((CHUNK, GROUPS), lambda i: (0, 0))],
        out_specs=[pl.BlockSpec((tb, 2 * D), lambda i: (i, 1)), sq, sq, pl.BlockSpec((8, D), lambda i: (0, 0))],
        out_shape=[SDS(dz.shape, BF16), SDS((GROUPS, CHUNK, CHUNK), F32), SDS((GROUPS, CHUNK, CHUNK), F32),
                   SDS((8, D), F32)],
        scratch_shapes=[pltpu.VMEM((tb, D), F32), pltpu.VMEM((tb, D), F32)], aliases={0: 0},
        params=_cparams(("arbitrary",), 40), args=(dz, dyb, z, z, ln_g, ln_b, w_s, b_s_t), comm=comm)


def _branch_a_bwd(dz, dya, z, hs, conv_w, conv_b, w_r, b_r, w_i, b_i, lam, comm=None):
    s = z.shape[0]
    ta = min(T_BRANCH, s)
    nb = s // ta
    per16 = ta // 16

    def body(dz_in, dya_ref, xa_ref, xp_ref, ga_ref, hs_ref, hp_ref, cw_ref, cb_ref, wr_ref, br_ref, wi_ref,
             bi_ref, lam_ref, dz_ref, vec_ref, dwr_ref, dwi_ref,
             a_s, b_s, h_s, r_s, i_s, m_s, dcar_s, acar_s, dxc_s):
        del dz_in
        i = pl.program_id(0)
        blk = nb - 1 - i

        @pl.when(i == 0)
        def _():
            dcar_s[...] = jnp.zeros_like(dcar_s)
            acar_s[...] = jnp.zeros_like(acar_s)
            dxc_s[...] = jnp.zeros_like(dxc_s)
            vec_ref[...] = jnp.zeros_like(vec_ref)
            dwr_ref[...] = jnp.zeros_like(dwr_ref)
            dwi_ref[...] = jnp.zeros_like(dwi_ref)

        cw = cw_ref[...]
        lam_v = lam_ref[...]
        xa = xa_ref[...].astype(F32)
        prev8 = jnp.where(blk > 0, xp_ref[...].astype(F32)[8:16], 0.0)
        xc = _conv_fwd(xa, prev8, cw, cb_ref[...])
        xcb = xc.astype(BF16)
        sp_lam = _softplus(-lam_v)
        _lru_gates(xc, xcb, wr_ref, br_ref[...], wi_ref, bi_ref[...], sp_lam, a_s, b_s, r_s, i_s, m_s)

        hs_v = hs_ref[...].astype(F32)
        hprev8 = jnp.where(blk > 0, hp_ref[...].astype(F32)[8:16], 0.0)
        h_m1 = _rows_shifted(hprev8, hs_v, 1)
        gg, dgg = _gelu_and_grad(ga_ref[...].astype(F32))
        dya_v = dya_ref[...].astype(F32)
        dz_ref[:, D:2 * D] = (dya_v * hs_v * dgg).astype(BF16)

        a_v = a_s[...]
        a_s[...] = _rows_advanced(a_v, acar_s[...], 1)
        b_s[...] = dya_v * gg

        row = lax.broadcasted_iota(jnp.int32, (8, D), 0)
        ng = ta // 8

        def group(gi, carry):
            off = pl.multiple_of((ng - 1 - gi) * 8, 8)
            c8 = a_s[pl.ds(off, 8), :]
            d8 = b_s[pl.ds(off, 8), :]
            for d in (1, 2, 4):
                c_sh = jnp.where(row < 8 - d, pltpu.roll(c8, 8 - d, 0), 1.0)
                d_sh = jnp.where(row < 8 - d, pltpu.roll(d8, 8 - d, 0), 0.0)
                d8 = c8 * d_sh + d8
                c8 = c8 * c_sh
            dh8 = d8 + c8 * carry
            h_s[pl.ds(off, 8), :] = dh8
            return jnp.broadcast_to(dh8[0:1, :], (8, D))

        dcar_s[...] = lax.fori_loop(0, ng, group, dcar_s[...])
        acar_s[...] = jnp.broadcast_to(a_v[0:1, :], (8, D))

        dbx = h_s[...]
        r_v, i_v, m_v = r_s[...], i_s[...], m_s[...]
        d_mult = dbx * xc * i_v
        d_loga = dbx * h_m1 * a_v - d_mult * (a_v * a_v) / m_v
        d_pr = d_loga * ((-LRU_C) * sp_lam) * r_v * (1.0 - r_v)
        d_pi = dbx * xc * m_v * i_v * (1.0 - i_v)
        vec_ref[7:8, :] += jnp.sum(d_loga * r_v, axis=0, keepdims=True) * (LRU_C * jax.nn.sigmoid(-lam_v))
        vec_ref[5:6, :] += jnp.sum(d_pr, axis=0, keepdims=True)
        vec_ref[6:7, :] += jnp.sum(d_pi, axis=0, keepdims=True)
        d_prb = d_pr.astype(BF16)
        d_pib = d_pi.astype(BF16)
        h_s[...] = dbx * i_v * m_v
        for h in range(HEADS):
            sl = slice(h * HEAD_DIM, (h + 1) * HEAD_DIM)
            h_s[:, sl] += _dot_nt(d_prb[:, sl], wr_ref[h]) + _dot_nt(d_pib[:, sl], wi_ref[h])
            dwr_ref[h] += _dot_tn(xcb[:, sl], d_prb[:, sl])
            dwi_ref[h] += _dot_tn(xcb[:, sl], d_pib[:, sl])
        d_xc = h_s[...]
        vec_ref[4:5, :] += jnp.sum(d_xc, axis=0, keepdims=True)
        vec_ref[0:1, :] += jnp.sum(d_xc * xa, axis=0, keepdims=True)
        d_xa = cw[0:1, :] * d_xc
        nxt = dxc_s[...]
        for k in range(1, CONV_K):
            vec_ref[k:k + 1, :] += jnp.sum(d_xc * _rows_shifted(prev8, xa, k), axis=0, keepdims=True)
            d_xa = d_xa + cw[k:k + 1, :] * _rows_advanced(d_xc, nxt, k)
        dz_ref[:, 0:D] = d_xa.astype(BF16)
        dxc_s[...] = d_xc[0:8, :]

    vec = pl.BlockSpec((1, D), lambda i: (0, 0))
    gate = pl.BlockSpec((HEADS, HEAD_DIM, HEAD_DIM), lambda i: (0, 0, 0))
    cur = lambda c: pl.BlockSpec((ta, D), lambda i: (nb - 1 - i, c))
    before = lambda c: pl.BlockSpec((16, D), lambda i: (jnp.maximum((nb - 1 - i) * per16 - 1, 0), c))
    return _call(
        body, name="branch_a_bwd", grid=(nb,),
        in_specs=[ANY, cur(0), cur(0), before(0), cur(1), cur(0), before(0),
                  pl.BlockSpec((CONV_K, D), lambda i: (0, 0)), vec, gate, vec, gate, vec, vec],
        out_specs=[pl.BlockSpec((ta, 2 * D), lambda i: (nb - 1 - i, 0)), pl.BlockSpec((8, D), lambda i: (0, 0)),
                   gate, gate],
        out_shape=[SDS(dz.shape, BF16), SDS((8, D), F32), SDS((HEADS, HEAD_DIM, HEAD_DIM), F32),
                   SDS((HEADS, HEAD_DIM, HEAD_DIM), F32)],
        scratch_shapes=[pltpu.VMEM((ta, D), F32)] * 6 + [pltpu.VMEM((8, D), F32)] * 3, aliases={0: 0},
        params=_cparams(("arbitrary",), 48),
        args=(dz, dya, z, z, z, hs, hs, conv_w, conv_b, w_r, b_r, w_i, b_i, lam), comm=comm)


def _in_bwd(dz, w_in_g, x, dh1, g_mix, comm=None):
    s = x.shape[0]
    tm = min(TM_ROWS, s)
    nj = N_SLOT

    def body(dz_ref, w_ref, x_ref, dh1_ref, g_ref, dx_ref, dg_ref, acc_s):
        i, j = pl.program_id(0), pl.program_id(1)

        @pl.when(j == 0)
        def _():
            acc_s[...] = jnp.zeros_like(acc_s)

        @pl.when((i == 0) & (j == 0))
        def _():
            dg_ref[...] = jnp.zeros_like(dg_ref)

        acc_s[...] += _dot_nt(dz_ref[...], w_ref[...])

        @pl.when(j == nj - 1)
        def _():
            xv = x_ref[...]
            rstd = lax.rsqrt(jnp.mean(xv * xv, axis=-1, keepdims=True) + NORM_EPS)
            xh = xv * rstd
            dn = acc_s[...]
            dg_ref[...] += jnp.sum(dn * xh, axis=0, keepdims=True)
            dhat = dn * g_ref[...]
            dx_ref[...] = dh1_ref[...] + rstd * (dhat - xh * jnp.mean(dhat * xh, axis=-1, keepdims=True))

    row = pl.BlockSpec((tm, D), lambda i, j: (i, 0))
    vec = pl.BlockSpec((1, D), lambda i, j: (0, 0))
    return _call(
        body, name="in_bwd", grid=(s // tm, nj),
        in_specs=[pl.BlockSpec((tm, W_IN_COLS), lambda i, j: (i, j)),
                  pl.BlockSpec((None, D, W_IN_COLS), lambda i, j: (j, 0, 0)), row, row, vec],
        out_specs=[row, vec],
        out_shape=[SDS((s, D), F32), SDS((1, D), F32)],
        scratch_shapes=[pltpu.VMEM((tm, D), F32)],
        params=_cparams(("arbitrary", "arbitrary"), 48), args=(dz, w_in_g, x, dh1, g_mix), comm=comm)


def _wgrad(name, a, b, nblk, a_split, b_split, square_a=False):
    s = a.shape[0]
    ts = min(TM_ROWS, s)
    a_w = a.shape[1] // nblk if a_split else a.shape[1]
    b_w = b.shape[1] // nblk if b_split else b.shape[1]

    def body(a_ref, b_ref, o_ref, acc_s):
        t = pl.program_id(1)

        @pl.when(t == 0)
        def _():
            acc_s[...] = jnp.zeros_like(acc_s)

        av = a_ref[...]
        if square_a:
            av = av * av
        acc_s[...] += _dot_tn(av.astype(BF16), b_ref[...].astype(BF16))

        @pl.when(t == pl.num_programs(1) - 1)
        def _():
            o_ref[...] = acc_s[...].astype(BF16)

    return pl.pallas_call(
        body, name=name, grid=(nblk, s // ts),
        in_specs=[pl.BlockSpec((ts, a_w), (lambda k, t: (t, k)) if a_split else (lambda k, t: (t, 0))),
                  pl.BlockSpec((ts, b_w), (lambda k, t: (t, k)) if b_split else (lambda k, t: (t, 0)))],
        out_specs=pl.BlockSpec((None, a_w, b_w), lambda k, t: (k, 0, 0)),
        out_shape=SDS((nblk, a_w, b_w), BF16),
        scratch_shapes=[pltpu.VMEM((a_w, b_w), F32)],
        compiler_params=_cparams(("arbitrary", "arbitrary"), 48),
    )(a, b)


def _place():
    x, y, c = lax.axis_index("x"), lax.axis_index("y"), lax.axis_index("c")
    return x, y, c


def _other_chips(x, y):
    return [(x, 1 - y, 2 * x + 1 - y), (1 - x, y, 2 * (1 - x) + y), (1 - x, 1 - y, 2 * (1 - x) + 1 - y)]


class _Plan:
    def __init__(self, arrays, out_shape, sems, start, finish):
        self.arrays, self.out_shape, self.sems, self.start, self.finish = arrays, out_shape, sems, start, finish


def _gather_plan(shards):
    n = len(shards)

    def copies(ins, outs, sems):
        send_sems, recv_sems, local_sems = sems
        x, y, c = _place()
        chip = 2 * x + y
        me = 2 * chip + c
        sib = (x, y, 1 - c)
        chips = _other_chips(x, y)

        def rc(k, t, src, blk, to):
            return pltpu.make_async_remote_copy(
                src_ref=src, dst_ref=outs[t].at[blk], send_sem=send_sems.at[k * n + t],
                recv_sem=recv_sems.at[k * n + t], device_id=to, device_id_type=MESH)

        local = [pltpu.make_async_copy(ins[t], outs[t].at[me], local_sems.at[t]) for t in range(n)]
        sends = [rc(0, t, ins[t], me, sib) for t in range(n)]
        for j, (px, py, _) in enumerate(chips):
            sends += [rc(1 + j, t, ins[t], me, (px, py, c)) for t in range(n)]
        return rc, local, sends, chips, chip, c, sib

    def start(ins, outs, sems):
        _, local, sends, _, _, _, _ = copies(ins, outs, sems)
        for cp in local + sends:
            cp.start()

    def finish(ins, outs, sems):
        rc, local, sends, chips, chip, c, sib = copies(ins, outs, sems)
        passed = []
        for j, (px, py, pc) in enumerate(chips):
            blk = 2 * pc + c
            for t in range(n):
                rc(1 + j, t, ins[t], blk, sib).wait_recv()
            for t in range(n):
                cp = rc(4 + j, t, outs[t].at[blk], blk, sib)
                cp.start()
                passed.append(cp)
        for t in range(n):
            rc(0, t, ins[t], 2 * chip + 1 - c, sib).wait_recv()
        for j, (px, py, pc) in enumerate(chips):
            for t in range(n):
                rc(4 + j, t, ins[t], 2 * pc + 1 - c, sib).wait_recv()
        for cp in sends + passed:
            cp.wait_send()
        for cp in local:
            cp.wait()

    return _Plan(list(shards), [SDS((N_SLOT,) + tuple(a.shape), a.dtype) for a in shards],
                 [pltpu.SemaphoreType.DMA((7 * n,)), pltpu.SemaphoreType.DMA((7 * n,)),
                  pltpu.SemaphoreType.DMA((n,))], start, finish)


def _sibling_plan(grads, whole=()):
    n, m = len(grads), len(whole)

    def copies(ins, outs, sems):
        send_sems, recv_sems = sems
        x, y, c = _place()
        sib = (x, y, 1 - c)

        def rc(t, src, dst):
            return pltpu.make_async_remote_copy(src_ref=src, dst_ref=dst, send_sem=send_sems.at[t],
                                                recv_sem=recv_sems.at[t], device_id=sib, device_id_type=MESH)
        return rc, c

    def start(ins, outs, sems):
        rc, c = copies(ins, outs, sems)
        for t in range(n):
            for j in range(4):
                rc(t, ins[t].at[2 * j + 1 - c], outs[t].at[j]).start()
        for t in range(n, n + m):
            rc(t, ins[t], outs[t]).start()

    def finish(ins, outs, sems):
        rc, _ = copies(ins, outs, sems)
        for t in range(n):
            rc(t, ins[t].at[pl.ds(0, 4)], outs[t]).wait()
        for t in range(n, n + m):
            rc(t, ins[t], outs[t]).wait()

    return _Plan(list(grads) + list(whole),
                 [SDS((4,) + tuple(g.shape[1:]), g.dtype) for g in grads] + [SDS(a.shape, a.dtype) for a in whole],
                 [pltpu.SemaphoreType.DMA((n + m,)), pltpu.SemaphoreType.DMA((n + m,))], start, finish)


def _chips_plan(parts, whole=()):
    n, m = len(parts), len(whole)

    def src_of(ins, t, pc):
        return ins[t].at[pc] if t < n else ins[t]

    def local_copies(ins, outs, sems, chip):
        return [pltpu.make_async_copy(src_of(ins, t, chip), outs[t].at[chip], sems[2].at[t]) for t in range(n + m)]

    def start(ins, outs, sems):
        send_sems, recv_sems, _ = sems
        x, y, c = _place()
        chip = 2 * x + y
        for cp in local_copies(ins, outs, sems, chip):
            cp.start()
        for px, py, pc in _other_chips(x, y):
            for t in range(n + m):
                pltpu.make_async_remote_copy(src_ref=src_of(ins, t, pc), dst_ref=outs[t].at[chip],
                                             send_sem=send_sems.at[t], recv_sem=recv_sems.at[t],
                                             device_id=(px, py, c), device_id_type=MESH).start()

    def finish(ins, outs, sems):
        send_sems, recv_sems, _ = sems
        x, y, c = _place()
        for t in range(n + m):
            three = outs[t].at[pl.ds(0, 3)]
            pltpu.make_async_remote_copy(src_ref=three, dst_ref=three, send_sem=send_sems.at[t],
                                         recv_sem=recv_sems.at[t], device_id=(x, y, c), device_id_type=MESH).wait()
        for cp in local_copies(ins, outs, sems, 2 * x + y):
            cp.wait()

    return _Plan(list(parts) + list(whole),
                 [SDS(p.shape, p.dtype) for p in parts] + [SDS((4,) + tuple(a.shape), a.dtype) for a in whole],
                 [pltpu.SemaphoreType.DMA((n + m,)), pltpu.SemaphoreType.DMA((n + m,)),
                  pltpu.SemaphoreType.DMA((n + m,))], start, finish)


def _exchange_plan(arr):
    def peers(x, y, c):
        flip = lambda v, f: 1 - v if f else v
        return [(flip(x, fx), flip(y, fy), flip(c, fc))
                for fx in (0, 1) for fy in (0, 1) for fc in (0, 1) if fx or fy or fc]

    def start(ins, outs, sems):
        x, y, c = _place()
        me = 4 * x + 2 * y + c
        pltpu.make_async_copy(ins[0], outs[0].at[me], sems[2].at[0]).start()
        for to in peers(x, y, c):
            pltpu.make_async_remote_copy(src_ref=ins[0], dst_ref=outs[0].at[me], send_sem=sems[0].at[0],
                                         recv_sem=sems[1].at[0], device_id=to, device_id_type=MESH).start()

    def finish(ins, outs, sems):
        x, y, c = _place()
        seven = outs[0].at[pl.ds(0, 7)]
        pltpu.make_async_remote_copy(src_ref=seven, dst_ref=seven, send_sem=sems[0].at[0], recv_sem=sems[1].at[0],
                                     device_id=(x, y, c), device_id_type=MESH).wait()
        pltpu.make_async_copy(ins[0], outs[0].at[4 * x + 2 * y + c], sems[2].at[0]).wait()

    return _Plan([arr], [SDS((N_SLOT,) + tuple(arr.shape), arr.dtype)],
                 [pltpu.SemaphoreType.DMA((1,)), pltpu.SemaphoreType.DMA((1,)), pltpu.SemaphoreType.DMA((1,))],
                 start, finish)


def _join(*plans):
    def cut(seq, sizes):
        out, at = [], 0
        for k in sizes:
            out.append(seq[at:at + k])
            at += k
        return out

    n_arr = [len(p.arrays) for p in plans]
    n_sem = [len(p.sems) for p in plans]

    def start(ins, outs, sems):
        for p, i, o, s in zip(plans, cut(ins, n_arr), cut(outs, n_arr), cut(sems, n_sem)):
            p.start(i, o, s)

    def finish(ins, outs, sems):
        for p, i, o, s in zip(plans, cut(ins, n_arr), cut(outs, n_arr), cut(sems, n_sem)):
            p.finish(i, o, s)

    return _Plan([a for p in plans for a in p.arrays], [o for p in plans for o in p.out_shape],
                 [s for p in plans for s in p.sems], start, finish)


def _run_plan(name, plan):
    k = len(plan.arrays)

    def body(*refs):
        ins, outs, sems = refs[:k], refs[k:2 * k], refs[2 * k:]
        plan.start(ins, outs, sems)
        plan.finish(ins, outs, sems)

    return pl.pallas_call(
        body, name=name, in_specs=[ANY] * k, out_specs=[ANY] * k, out_shape=plan.out_shape,
        scratch_shapes=plan.sems, compiler_params=pltpu.CompilerParams(has_side_effects=True),
    )(*plan.arrays)


def _call(body, *, name, grid, in_specs, out_specs, out_shape, scratch_shapes, params, args, comm=None,
          aliases=None, prefetch=()):
    aliases = aliases or {}
    n_pre = len(prefetch)

    def launch(fn, ins_specs, outs_specs, outs_shape, scratch, operands):
        spec = pltpu.PrefetchScalarGridSpec(num_scalar_prefetch=n_pre, grid=grid, in_specs=ins_specs,
                                            out_specs=outs_specs, scratch_shapes=scratch)
        return pl.pallas_call(fn, name=name, grid_spec=spec, out_shape=outs_shape, compiler_params=params,
                              input_output_aliases=aliases)(*prefetch, *operands)

    if comm is None:
        return list(launch(body, in_specs, out_specs, out_shape, scratch_shapes, args)), []
    n_in, n_out, n_scr, k = len(in_specs), len(out_specs), len(scratch_shapes), len(comm.arrays)

    def wrapped(*refs):
        pre, refs = refs[:n_pre], refs[n_pre:]
        ins = refs[:n_in]
        c_in = refs[n_in:n_in + k]
        outs = refs[n_in + k:n_in + k + n_out]
        c_out = refs[n_in + k + n_out:n_in + 2 * k + n_out]
        scr = refs[n_in + 2 * k + n_out:n_in + 2 * k + n_out + n_scr]
        sems = refs[n_in + 2 * k + n_out + n_scr:]
        ids = [pl.program_id(d) for d in range(len(grid))]
        first = ids[0] == 0
        last = ids[0] == grid[0] - 1
        for d in range(1, len(grid)):
            first = first & (ids[d] == 0)
            last = last & (ids[d] == grid[d] - 1)

        @pl.when(first)
        def _():
            comm.start(c_in, c_out, sems)

        body(*pre, *ins, *outs, *scr)

        @pl.when(last)
        def _():
            comm.finish(c_in, c_out, sems)

    res = launch(wrapped, list(in_specs) + [ANY] * k, list(out_specs) + [ANY] * k,
                 list(out_shape) + list(comm.out_shape), list(scratch_shapes) + list(comm.sems),
                 tuple(args) + tuple(comm.arrays))
    return list(res[:n_out]), list(res[n_out:])


def _row_tile(rows):
    for t in (512, 256, 128, 64, 32, 16, 8):
        if rows % t == 0:
            return t
    return rows


def _pair_sum(name, g8, recv4, core):
    _, rows, cols = recv4.shape
    tr = _row_tile(rows)
    g42 = g8.reshape(4, 2, rows, cols)

    def body(c_ref, g_ref, r_ref, o_ref):
        del c_ref
        o_ref[...] = (g_ref[...].astype(F32) + r_ref[...].astype(F32)).astype(o_ref.dtype)

    return pl.pallas_call(
        body, name=name,
        grid_spec=pltpu.PrefetchScalarGridSpec(
            num_scalar_prefetch=1, grid=(4, rows // tr),
            in_specs=[pl.BlockSpec((None, None, tr, cols), lambda j, i, c_ref: (j, c_ref[0], i, 0)),
                      pl.BlockSpec((None, tr, cols), lambda j, i, c_ref: (j, i, 0))],
            out_specs=pl.BlockSpec((None, tr, cols), lambda j, i, c_ref: (j, i, 0))),
        out_shape=SDS(recv4.shape, g8.dtype),
        compiler_params=_cparams(("arbitrary", "arbitrary"), 32),
    )(core, g42, recv4)


def _add2(name, a, b):
    rows, cols = a.shape
    tr = _row_tile(rows)

    def body(a_ref, b_ref, o_ref):
        o_ref[...] = a_ref[...] + b_ref[...]

    blk = pl.BlockSpec((tr, cols), lambda i: (i, 0))
    return pl.pallas_call(body, name=name, grid=(rows // tr,), in_specs=[blk, blk], out_specs=blk,
                          out_shape=SDS(a.shape, a.dtype),
                          compiler_params=_cparams(("arbitrary",), 32))(a, b)


def _sum_terms(name, terms):
    k, rows, cols = terms.shape
    tr = _row_tile(rows)

    def body(r_ref, o_ref):
        acc = r_ref[0]
        for q in range(1, k):
            acc = acc + r_ref[q]
        o_ref[...] = acc

    return pl.pallas_call(body, name=name, grid=(rows // tr,),
                          in_specs=[pl.BlockSpec((k, tr, cols), lambda i: (0, i, 0))],
                          out_specs=pl.BlockSpec((tr, cols), lambda i: (i, 0)),
                          out_shape=SDS((rows, cols), terms.dtype),
                          compiler_params=_cparams(("arbitrary",), 32))(terms)


def _adam_update(g, w, m, v):
    c1 = 1.0 / (1.0 - ADAM_B1 ** ADAM_STEP)
    c2 = 1.0 / (1.0 - ADAM_B2 ** ADAM_STEP)
    mn = ADAM_B1 * m + (1.0 - ADAM_B1) * g
    vn = ADAM_B2 * v + (1.0 - ADAM_B2) * (g * g)
    delta = (-ADAM_LR) * ((mn * c1) / (jnp.sqrt(vn * c2) + ADAM_EPS) + ADAM_WD * w)
    return delta, mn, vn


def _adamw_many(name, gs, ws, ms, vs):
    n = len(gs)

    def body(*refs):
        for p in range(n):
            g, w, m, v = (refs[q * n + p][...] for q in range(4))
            d, mn, vn = _adam_update(g, w, m, v)
            refs[4 * n + p][...] = d
            refs[5 * n + p][...] = mn
            refs[6 * n + p][...] = vn

    full = [pl.BlockSpec(memory_space=pltpu.VMEM)] * n
    shapes = [SDS(w.shape, F32) for w in ws]
    res = pl.pallas_call(body, name=name, in_specs=full * 4, out_specs=full * 3, out_shape=shapes * 3,
                         compiler_params=pltpu.CompilerParams(vmem_limit_bytes=32 * MiB))(*gs, *ws, *ms, *vs)
    return [(res[p], res[n + p], res[2 * n + p]) for p in range(n)]


def _adamw(name, terms, w, m, v):
    k, rows, cols = terms.shape
    tr = _row_tile(rows)

    def body(t_ref, w_ref, m_ref, v_ref, g_ref, d_ref, mo_ref, vo_ref):
        g = t_ref[0].astype(F32)
        for q in range(1, k):
            g = g + t_ref[q].astype(F32)
        g_ref[...] = g
        d_ref[...], mo_ref[...], vo_ref[...] = _adam_update(g, w_ref[...], m_ref[...], v_ref[...])

    blk = pl.BlockSpec((tr, cols), lambda i: (i, 0))
    return pl.pallas_call(body, name=name, grid=(rows // tr,),
                          in_specs=[pl.BlockSpec((k, tr, cols), lambda i: (0, i, 0)), blk, blk, blk],
                          out_specs=[blk] * 4, out_shape=[SDS((rows, cols), F32)] * 4,
                          compiler_params=_cparams(("arbitrary",), 40))(terms, w, m, v)


def kernel(x, norm_mix_g, w_in, conv_w, conv_b, w_rgate, b_rgate, w_igate, b_igate, lru_lambda, w_out_a, sgu_ln_g, sgu_ln_b, sgu_w_s, sgu_b_s, w_out_b, w_out, norm_mlp_g, w_up, w_down, norm_final_g, loss_target, m_norm_mix_g, m_w_in, m_conv_w, m_conv_b, m_w_rgate, m_b_rgate, m_w_igate, m_b_igate, m_lru_lambda, m_w_out_a, m_sgu_ln_g, m_sgu_ln_b, m_sgu_w_s, m_sgu_b_s, m_w_out_b, m_w_out, m_norm_mlp_g, m_w_up, m_w_down, m_norm_final_g, v_norm_mix_g, v_w_in, v_conv_w, v_conv_b, v_w_rgate, v_b_rgate, v_w_igate, v_b_igate, v_lru_lambda, v_w_out_a, v_sgu_ln_g, v_sgu_ln_b, v_sgu_w_s, v_sgu_b_s, v_w_out_b, v_w_out, v_norm_mlp_g, v_w_up, v_w_down, v_norm_final_g):
    cx, cy, cc = _place()
    me = 4 * cx + 2 * cy + cc
    core = jnp.reshape(cc, (1,)).astype(jnp.int32)
    xs = x[0]
    tgt = loss_target[0]
    s = xs.shape[0]

    gate_shard = jnp.stack([w_rgate[0], w_igate[0]]).astype(BF16).reshape(2 * HEADS * 32, HEAD_DIM)
    vec_shard = jnp.concatenate([conv_w[0], b_rgate[0], b_igate[0]], axis=1)
    vec_shard = jnp.pad(vec_shard, ((0, 4), (0, 256 - vec_shard.shape[1])))
    shards = [w_in[0].astype(BF16), w_out_a[0].astype(BF16), w_out_b[0].astype(BF16), w_out[0].astype(BF16),
              w_up[0].astype(BF16), w_down[0].astype(BF16), gate_shard, vec_shard]
    (z, n1, w_in_g), (gate_g, vec_g) = _in_proj(xs, norm_mix_g, shards[0], _slot_order(cx, cy, cc),
                                                comm=_gather_plan(shards[6:8]))
    gates = gate_g.reshape(N_SLOT, 2, HEADS, 32, HEAD_DIM).transpose(1, 2, 0, 3, 4).reshape(2, HEADS, HEAD_DIM, HEAD_DIM)
    w_r_f, w_i_f = gates[0], gates[1]
    conv_w_f = vec_g[:, 0:4, 0:128].transpose(1, 0, 2).reshape(CONV_K, D)
    b_r_f = vec_g[:, 0:4, 128:160].transpose(1, 0, 2).reshape(1, D)
    b_i_f = vec_g[:, 0:4, 160:192].transpose(1, 0, 2).reshape(1, D)
    b_s_t = jnp.transpose(sgu_b_s[0])

    (ya, hs), (w_oa_g, w_ob_g, w_out_g, w_up_g) = _branch_a_fwd(
        z, conv_w_f, conv_b, w_r_f, b_r_f, w_i_f, b_i_f, lru_lambda, comm=_gather_plan(shards[1:5]))
    w_oa_f = w_oa_g.reshape(D, D)
    w_ob_f = w_ob_g.reshape(D, D)
    w_out_f = w_out_g.reshape(D, D)
    yb = _branch_b_fwd(z, sgu_ln_g, sgu_ln_b, sgu_w_s[0], b_s_t)
    (pa, pb, merged, h1), (w_down_g,) = _merge_out(ya, yb, z, xs, w_oa_f, w_ob_f, w_out_f,
                                                   comm=_gather_plan(shards[5:6]))
    w_down_f = w_down_g.reshape(N_SLOT * FF_COLS, D)
    gf2 = norm_final_g.reshape(1, D)
    r_act, n2, dh2, loss_acc, d_gfin = _mlp_fwd(h1, norm_mlp_g, w_up_g, w_down_f, gf2, tgt)

    def pair(names, grads, recv):
        return [_pair_sum("pair_sum_" + nm, g, r, core) for nm, g, r in zip(names, grads, recv)]

    g_down = _wgrad("wgrad_down", r_act, dh2, N_SLOT, True, False, square_a=True)
    (df, dh1, d_gmlp), (r_down,) = _mlp_bwd(dh2, r_act, w_down_f, w_up_g, h1, norm_mlp_g,
                                            comm=_sibling_plan([g_down]))
    (p_down,) = pair(["down"], [g_down], [r_down])
    g_up = _wgrad("wgrad_up", n2, df, N_SLOT, False, True)
    g_out = _wgrad("wgrad_out", merged, dh1, 1, False, False).reshape(N_SLOT, D // N_SLOT, D)
    (dz, dpa, dpb, dya, dyb), (got_down, r_up, r_out) = _merge_bwd(
        dh1, z, pa, pb, w_out_f, w_oa_f, w_ob_f, comm=_join(_chips_plan([p_down]), _sibling_plan([g_up, g_out])))
    p_up, p_out = pair(["up", "out"], [g_up, g_out], [r_up, r_out])
    g_oa = _wgrad("wgrad_out_a", ya, dpa, 1, False, False).reshape(N_SLOT, D // N_SLOT, D)
    g_ob = _wgrad("wgrad_out_b", yb, dpb, 1, False, False).reshape(N_SLOT, D // N_SLOT, D)
    (dz, d_ws, d_bs, d_ln), (got_up, r_oa, r_ob) = _branch_b_bwd(
        dz, dyb, z, sgu_ln_g, sgu_ln_b, sgu_w_s[0], b_s_t,
        comm=_join(_chips_plan([p_up]), _sibling_plan([g_oa, g_ob])))
    p_oa, p_ob = pair(["out_a", "out_b"], [g_oa, g_ob], [r_oa, r_ob])
    (dz, d_vec, d_wr, d_wi), (got_out, got_oa, got_ob) = _branch_a_bwd(
        dz, dya, z, hs, conv_w_f, conv_b, w_r_f, b_r_f, w_i_f, b_i_f, lru_lambda,
        comm=_chips_plan([p_out, p_oa, p_ob]))
    g_in = _wgrad("wgrad_in", n1, dz, N_SLOT, False, True)
    g_gate = jnp.stack([d_wr, d_wi]).reshape(2, HEADS, N_SLOT, 32, HEAD_DIM).transpose(2, 0, 1, 3, 4)
    g_gate = g_gate.reshape(N_SLOT, 2 * HEADS * 32, HEAD_DIM).astype(BF16)

    d_bs_row = jnp.pad(d_bs[:, :, 0].reshape(1, GROUPS * CHUNK), ((0, 0), (0, D - GROUPS * CHUNK)))
    vecs = jnp.concatenate([d_vec, jnp.concatenate([d_ln[0:2], d_gmlp, d_gfin, d_bs_row, jnp.zeros((3, D), F32)])])
    d_ws2 = d_ws.reshape(GROUPS * CHUNK, CHUNK)
    r_in, r_gate, r_vecs, r_ws = _run_plan("rs_sibling_in", _sibling_plan([g_in, g_gate], [vecs, d_ws2]))
    p_in, p_gate = pair(["in", "gate"], [g_in, g_gate], [r_in, r_gate])
    vecs_chip = _add2("pair_sum_vecs", vecs, r_vecs)
    ws_chip = _add2("pair_sum_ws", d_ws2, r_ws)
    (dx, d_gmix), (got_in, got_gate, got_vecs, got_ws) = _in_bwd(
        dz, w_in_g, xs, dh1, norm_mix_g, comm=_chips_plan([p_in, p_gate], [vecs_chip, ws_chip]))
    vecs_sum = _sum_terms("sum_vecs", got_vecs)
    last = jnp.concatenate([d_gmix, jnp.pad(loss_acc[0:1], ((0, 0), (0, D - 128))), jnp.zeros((6, D), F32)])
    (last_all,) = _run_plan("exchange_last", _exchange_plan(last))
    last_sum = _sum_terms("sum_last", last_all)
    loss = last_sum[1, 0]
    got = [got_in, got_oa, got_ob, got_out, got_up, got_down, got_gate]

    def step(nm, terms, w, m, v, rows, cols):
        g, d, mn, vn = _adamw("adamw_" + nm, terms.reshape(4, rows, cols), w.reshape(rows, cols),
                              m.reshape(rows, cols), v.reshape(rows, cols))
        return [a.reshape(w.shape) for a in (g, d, mn, vn)]

    o_in = step("in", got[0], w_in, m_w_in, v_w_in, D, W_IN_COLS)
    o_oa = step("out_a", got[1], w_out_a, m_w_out_a, v_w_out_a, D // N_SLOT, D)
    o_ob = step("out_b", got[2], w_out_b, m_w_out_b, v_w_out_b, D // N_SLOT, D)
    o_out = step("out", got[3], w_out, m_w_out, v_w_out, D // N_SLOT, D)
    o_up = step("up", got[4], w_up, m_w_up, v_w_up, D, FF_COLS)
    o_down = step("down", got[5], w_down, m_w_down, v_w_down, FF_COLS, D)
    gate_w = jnp.stack([w_rgate[0], w_igate[0]]).reshape(2 * HEADS * 32, HEAD_DIM)
    gate_m = jnp.stack([m_w_rgate[0], m_w_igate[0]]).reshape(2 * HEADS * 32, HEAD_DIM)
    gate_v = jnp.stack([v_w_rgate[0], v_w_igate[0]]).reshape(2 * HEADS * 32, HEAD_DIM)
    o_gate = _adamw("adamw_gate", got[6], gate_w, gate_m, gate_v)
    o_gate = [a.reshape(2, 1, HEADS, 32, HEAD_DIM) for a in o_gate]
    o_wr = [a[0] for a in o_gate]
    o_wi = [a[1] for a in o_gate]

    def own(full, width):
        return lax.dynamic_slice_in_dim(full, me * width, width, axis=1)

    small_g = {
        "norm_mix_g": last_sum[0:1], "conv_w": own(vecs_sum[0:4], 128), "conv_b": vecs_sum[4:5],
        "b_rgate": own(vecs_sum[5:6].reshape(HEADS, HEAD_DIM), 32),
        "b_igate": own(vecs_sum[6:7].reshape(HEADS, HEAD_DIM), 32),
        "lru_lambda": vecs_sum[7:8], "sgu_ln_g": vecs_sum[8:9], "sgu_ln_b": vecs_sum[9:10],
        "norm_mlp_g": vecs_sum[10:11], "norm_final_g": vecs_sum[11:12],
        "sgu_b_s": vecs_sum[12, 0:GROUPS * CHUNK].reshape(GROUPS, CHUNK),
    }
    small_w = {"norm_mix_g": (norm_mix_g, m_norm_mix_g, v_norm_mix_g), "conv_w": (conv_w, m_conv_w, v_conv_w),
               "conv_b": (conv_b, m_conv_b, v_conv_b), "b_rgate": (b_rgate, m_b_rgate, v_b_rgate),
               "b_igate": (b_igate, m_b_igate, v_b_igate), "lru_lambda": (lru_lambda, m_lru_lambda, v_lru_lambda),
               "sgu_ln_g": (sgu_ln_g, m_sgu_ln_g, v_sgu_ln_g), "sgu_ln_b": (sgu_ln_b, m_sgu_ln_b, v_sgu_ln_b),
               "norm_mlp_g": (norm_mlp_g, m_norm_mlp_g, v_norm_mlp_g),
               "norm_final_g": (norm_final_g, m_norm_final_g, v_norm_final_g),
               "sgu_b_s": (sgu_b_s, m_sgu_b_s, v_sgu_b_s), "sgu_w_s": (sgu_w_s, m_sgu_w_s, v_sgu_w_s)}
    order = list(small_g)
    as2d = lambda k, a: a.reshape(small_g[k].shape)
    upd = _adamw_many("adamw_small", [small_g[k] for k in order], *[[as2d(k, small_w[k][q]) for k in order]
                                                                     for q in range(3)])
    o_small = {k: [a.reshape(small_w[k][0].shape) for a in (small_g[k],) + u] for k, u in zip(order, upd)}
    ws3 = [a[0].reshape(GROUPS * CHUNK, CHUNK) for a in small_w.pop("sgu_w_s")]
    o_small["sgu_w_s"] = [a.reshape(sgu_w_s.shape) for a in _adamw("adamw_ws", got_ws, *ws3)]

    per_weight = {"norm_mix_g": o_small["norm_mix_g"], "w_in": o_in, "conv_w": o_small["conv_w"],
                  "conv_b": o_small["conv_b"], "w_rgate": o_wr, "b_rgate": o_small["b_rgate"], "w_igate": o_wi,
                  "b_igate": o_small["b_igate"], "lru_lambda": o_small["lru_lambda"], "w_out_a": o_oa,
                  "sgu_ln_g": o_small["sgu_ln_g"], "sgu_ln_b": o_small["sgu_ln_b"], "sgu_w_s": o_small["sgu_w_s"],
                  "sgu_b_s": o_small["sgu_b_s"], "w_out_b": o_ob, "w_out": o_out, "norm_mlp_g": o_small["norm_mlp_g"],
                  "w_up": o_up, "w_down": o_down, "norm_final_g": o_small["norm_final_g"]}
    names_w = list(per_weight)
    return (loss, dx[None], *[per_weight[k][0] for k in names_w], *[per_weight[k][1] for k in names_w],
            *[per_weight[k][2] for k in names_w], *[per_weight[k][3] for k in names_w])
```

```python
import jax
import jax.numpy as jnp
from jax import lax
from jax.experimental import pallas as pl
from jax.experimental.pallas import tpu as pltpu

F32 = jnp.float32
BF16 = jnp.bfloat16
SDS = jax.ShapeDtypeStruct
MESH = pl.DeviceIdType.MESH
ANY = pl.BlockSpec(memory_space=pl.ANY)

D = 1024
N_SLOT = 8
W_IN_COLS = 768
FF_COLS = 512
HEADS, HEAD_DIM = 4, 256
GROUPS, GROUP_DIM = 4, 256
CHUNK = 128
CONV_K = 4
NORM_EPS = 1e-6
LN_EPS = 1e-5
LRU_C = 8.0
ADAM_LR, ADAM_B1, ADAM_B2, ADAM_EPS, ADAM_WD, ADAM_STEP = 0.001, 0.9, 0.999, 1e-08, 0.01, 10

TM_ROWS = 1024
TM_MERGE = 512
T_BRANCH = 256
MiB = 1024 * 1024

_GELU_C = 0.7978845608028654
_GELU_A = 0.044715


def _cparams(sem, vmem_mib):
    return pltpu.CompilerParams(dimension_semantics=sem, vmem_limit_bytes=vmem_mib * MiB)


def _gelu(x):
    t = jnp.tanh(_GELU_C * (x + _GELU_A * x * x * x))
    return 0.5 * x * (1.0 + t)


def _gelu_and_grad(x):
    x2 = x * x
    t = jnp.tanh(_GELU_C * x * (1.0 + _GELU_A * x2))
    g = 0.5 * x * (1.0 + t)
    dg = 0.5 * (1.0 + t) + 0.5 * x * (1.0 - t * t) * _GELU_C * (1.0 + 3.0 * _GELU_A * x2)
    return g, dg


def _softplus(x):
    return jnp.maximum(x, 0.0) + jnp.log1p(jnp.exp(-jnp.abs(x)))


def _dot(a, b):
    return jnp.dot(a, b, preferred_element_type=F32)


def _dot_nt(a, b):
    return lax.dot_general(a, b, (((1,), (1,)), ((), ())), preferred_element_type=F32)


def _dot_tn(a, b):
    return lax.dot_general(a, b, (((0,), (0,)), ((), ())), preferred_element_type=F32)


def _rows_shifted(prev8, cur, k):
    ext = jnp.concatenate([prev8, cur], axis=0)
    return pltpu.roll(ext, k, 0)[8:]


def _rows_advanced(cur, next8, k):
    t = cur.shape[0]
    ext = jnp.concatenate([cur, next8], axis=0)
    return pltpu.roll(ext, t + 8 - k, 0)[:t]


def _slot_order(x, y, c):
    chip = 2 * x + y
    order = [2 * chip + c, 2 * chip + 1 - c]
    for _, _, pc in _other_chips(x, y):
        order += [2 * pc + c, 2 * pc + 1 - c]
    return jnp.stack(order).astype(jnp.int32)


def _in_proj(x, g_mix, w_in_own, order, comm=None):
    s = x.shape[0]
    tm = min(TM_ROWS, s)
    ni = s // tm

    def body(order_ref, x_ref, g_ref, own_ref, z_ref, n_ref, wg_ref, n_s, w_s, send_sems, recv_sems, local_sems):
        j, i = pl.program_id(0), pl.program_id(1)
        px, py, c = _place()
        chip = 2 * px + py
        me = 2 * chip + c
        sib = (px, py, 1 - c)
        chips = _other_chips(px, py)

        def rc(k, src, blk, to):
            return pltpu.make_async_remote_copy(src_ref=src, dst_ref=w_s.at[blk], send_sem=send_sems.at[k],
                                                recv_sem=recv_sems.at[k], device_id=to, device_id_type=MESH)

        own_in = pltpu.make_async_copy(own_ref, w_s.at[me], local_sems.at[0])
        sends = [rc(0, own_ref, me, sib)] + [rc(1 + q, own_ref, me, (qx, qy, c)) for q, (qx, qy, _) in enumerate(chips)]
        passed = [rc(4 + q, w_s.at[2 * pc + c], 2 * pc + c, sib) for q, (_, _, pc) in enumerate(chips)]
        keep = pltpu.make_async_copy(w_s, wg_ref, local_sems.at[1])

        @pl.when((i == 0) & (j == 0))
        def _():
            own_in.start()
            for cp in sends:
                cp.start()
            own_in.wait()

        @pl.when((i == 0) & (j == 1))
        def _():
            rc(0, own_ref, 2 * chip + 1 - c, sib).wait_recv()

        for q, (_, _, pc) in enumerate(chips):
            @pl.when((i == 0) & (j == 2 + 2 * q))
            def _():
                rc(1 + q, own_ref, 2 * pc + c, sib).wait_recv()
                passed[q].start()

            @pl.when((i == 0) & (j == 3 + 2 * q))
            def _():
                rc(4 + q, own_ref, 2 * pc + 1 - c, sib).wait_recv()

        rows = pl.ds(pl.multiple_of(i * tm, tm), tm)

        @pl.when(j == 0)
        def _():
            xv = x_ref[...]
            rstd = lax.rsqrt(jnp.mean(xv * xv, axis=-1, keepdims=True) + NORM_EPS)
            nb = (xv * rstd * g_ref[...]).astype(BF16)
            n_s[rows, :] = nb
            n_ref[...] = nb

        z_ref[...] = _dot(n_s[rows, :], w_s[order_ref[j]]).astype(BF16)

        @pl.when((i == 0) & (j == N_SLOT - 1))
        def _():
            keep.start()

        @pl.when((i == ni - 1) & (j == N_SLOT - 1))
        def _():
            for cp in sends + passed:
                cp.wait_send()
            keep.wait()

    first_pass = lambda j, i, o: (jnp.where(j == 0, i, ni - 1), 0)
    (z, n1, w_in_g), extra = _call(
        body, name="in_proj", grid=(N_SLOT, ni), prefetch=(order,),
        in_specs=[pl.BlockSpec((tm, D), first_pass),
                  pl.BlockSpec((1, D), lambda j, i, o: (0, 0)), ANY],
        out_specs=[pl.BlockSpec((tm, W_IN_COLS), lambda j, i, o: (i, o[j])),
                   pl.BlockSpec((tm, D), first_pass), ANY],
        out_shape=[SDS((s, N_SLOT * W_IN_COLS), BF16), SDS((s, D), BF16), SDS((N_SLOT, D, W_IN_COLS), BF16)],
        scratch_shapes=[pltpu.VMEM((s, D), BF16), pltpu.VMEM((N_SLOT, D, W_IN_COLS), BF16),
                        pltpu.SemaphoreType.DMA((7,)), pltpu.SemaphoreType.DMA((7,)), pltpu.SemaphoreType.DMA((2,))],
        params=_cparams(("arbitrary", "arbitrary"), 56), args=(x, g_mix, w_in_own), comm=comm)
    return (z, n1, w_in_g), extra


def _lru_gates(xc, xcb, wr_ref, br, wi_ref, bi, sp_lam, a_s, b_s, r_s=None, i_s=None, m_s=None):
    for h in range(HEADS):
        sl = slice(h * HEAD_DIM, (h + 1) * HEAD_DIM)
        r = jax.nn.sigmoid(_dot(xcb[:, sl], wr_ref[h]) + br[:, sl])
        ig = jax.nn.sigmoid(_dot(xcb[:, sl], wi_ref[h]) + bi[:, sl])
        log_a = (-LRU_C) * r * sp_lam[:, sl]
        a = jnp.exp(log_a)
        mult = jnp.sqrt(-jnp.tanh(log_a) * (a * a + 1.0))
        a_s[:, sl] = a
        b_s[:, sl] = xc[:, sl] * ig * mult
        if r_s is not None:
            r_s[:, sl] = r
            i_s[:, sl] = ig
            m_s[:, sl] = mult


def _conv_fwd(xa, prev8, cw, cb):
    xc = cb + cw[0:1, :] * xa
    for k in range(1, CONV_K):
        xc = xc + cw[k:k + 1, :] * _rows_shifted(prev8, xa, k)
    return xc


def _branch_a_fwd(z, conv_w, conv_b, w_r, b_r, w_i, b_i, lam, comm=None):
    s = z.shape[0]
    ta = min(T_BRANCH, s)
    per16 = ta // 16

    def body(xa_ref, xp_ref, ga_ref, cw_ref, cb_ref, wr_ref, br_ref, wi_ref, bi_ref, lam_ref,
             ya_ref, hs_ref, a_s, b_s, h_s, carry_s):
        i = pl.program_id(0)

        @pl.when(i == 0)
        def _():
            carry_s[...] = jnp.zeros_like(carry_s)

        xa = xa_ref[...].astype(F32)
        prev8 = jnp.where(i > 0, xp_ref[...].astype(F32)[8:16], 0.0)
        xc = _conv_fwd(xa, prev8, cw_ref[...], cb_ref[...])
        sp_lam = _softplus(-lam_ref[...])
        _lru_gates(xc, xc.astype(BF16), wr_ref, br_ref[...], wi_ref, bi_ref[...], sp_lam, a_s, b_s)

        row = lax.broadcasted_iota(jnp.int32, (8, D), 0)

        def group(g, carry):
            off = pl.multiple_of(g * 8, 8)
            a8 = a_s[pl.ds(off, 8), :]
            b8 = b_s[pl.ds(off, 8), :]
            for d in (1, 2, 4):
                a_sh = jnp.where(row >= d, pltpu.roll(a8, d, 0), 1.0)
                b_sh = jnp.where(row >= d, pltpu.roll(b8, d, 0), 0.0)
                b8 = a8 * b_sh + b8
                a8 = a8 * a_sh
            h8 = b8 + a8 * carry
            h_s[pl.ds(off, 8), :] = h8
            return jnp.broadcast_to(h8[7:8, :], (8, D))

        carry_s[...] = lax.fori_loop(0, ta // 8, group, carry_s[...])
        hs = h_s[...]
        hs_ref[...] = hs.astype(BF16)
        ya_ref[...] = (hs * _gelu(ga_ref[...].astype(F32))).astype(BF16)

    vec = pl.BlockSpec((1, D), lambda i: (0, 0))
    gate = pl.BlockSpec((HEADS, HEAD_DIM, HEAD_DIM), lambda i: (0, 0, 0))
    return _call(
        body, name="branch_a_fwd", grid=(s // ta,),
        in_specs=[pl.BlockSpec((ta, D), lambda i: (i, 0)),
                  pl.BlockSpec((16, D), lambda i: (jnp.maximum(i * per16 - 1, 0), 0)),
                  pl.BlockSpec((ta, D), lambda i: (i, 1)),
                  pl.BlockSpec((CONV_K, D), lambda i: (0, 0)), vec, gate, vec, gate, vec, vec],
        out_specs=[pl.BlockSpec((ta, D), lambda i: (i, 0)), pl.BlockSpec((ta, D), lambda i: (i, 0))],
        out_shape=[SDS((s, D), BF16), SDS((s, D), BF16)],
        scratch_shapes=[pltpu.VMEM((ta, D), F32), pltpu.VMEM((ta, D), F32), pltpu.VMEM((ta, D), F32),
                        pltpu.VMEM((8, D), F32)],
        params=_cparams(("arbitrary",), 40), args=(z, z, z, conv_w, conv_b, w_r, b_r, w_i, b_i, lam), comm=comm)


def _sgu_common(ub, vb, lg, lb, with_grad):
    if with_grad:
        u, du = _gelu_and_grad(ub)
        v, dv = _gelu_and_grad(vb)
    else:
        u, v, du, dv = _gelu(ub), _gelu(vb), None, None
    mu = jnp.mean(v, axis=-1, keepdims=True)
    vc = v - mu
    rstd = lax.rsqrt(jnp.mean(vc * vc, axis=-1, keepdims=True) + LN_EPS)
    vhat = vc * rstd
    vln = vhat * lg + lb
    return u, du, dv, rstd, vhat, vln


def _masked_ws(ws_ref):
    t = lax.broadcasted_iota(jnp.int32, (CHUNK, CHUNK), 0)
    c = lax.broadcasted_iota(jnp.int32, (CHUNK, CHUNK), 1)
    keep = c <= t
    return [jnp.where(keep, ws_ref[g], 0.0).astype(BF16) for g in range(GROUPS)]


def _branch_b_fwd(z, ln_g, ln_b, w_s, b_s_t, comm=None):
    s = z.shape[0]
    tb = min(T_BRANCH, s)

    def body(ub_ref, vb_ref, lg_ref, lb_ref, ws_ref, bs_ref, yb_ref):
        u, _, _, _, _, vln = _sgu_common(ub_ref[...].astype(F32), vb_ref[...].astype(F32),
                                         lg_ref[...], lb_ref[...], False)
        vlnb = vln.astype(BF16)
        wm = _masked_ws(ws_ref)
        bs = bs_ref[...]
        for c in range(tb // CHUNK):
            rs = slice(c * CHUNK, (c + 1) * CHUNK)
            for g in range(GROUPS):
                cs = slice(g * GROUP_DIM, (g + 1) * GROUP_DIM)
                sp = _dot(wm[g], vlnb[rs, cs]) + bs[:, g:g + 1]
                yb_ref[rs, cs] = (u[rs, cs] * sp).astype(BF16)

    vec = pl.BlockSpec((1, D), lambda i: (0, 0))
    return _call(
        body, name="branch_b_fwd", grid=(s // tb,),
        in_specs=[pl.BlockSpec((tb, D), lambda i: (i, 2)), pl.BlockSpec((tb, D), lambda i: (i, 3)), vec, vec,
                  pl.BlockSpec((GROUPS, CHUNK, CHUNK), lambda i: (0, 0, 0)),
                  pl.BlockSpec((CHUNK, GROUPS), lambda i: (0, 0))],
        out_specs=[pl.BlockSpec((tb, D), lambda i: (i, 0))],
        out_shape=[SDS((s, D), BF16)], scratch_shapes=[],
        params=_cparams(("arbitrary",), 40), args=(z, z, ln_g, ln_b, w_s, b_s_t), comm=comm)


def _merge_out(ya, yb, z, x, w_oa, w_ob, w_out, comm=None):
    s = x.shape[0]
    tm = min(TM_MERGE, s)

    def body(ya_ref, yb_ref, ma_ref, mb_ref, x_ref, woa_ref, wob_ref, wo_ref, pa_ref, pb_ref, mg_ref, h1_ref):
        pa = _dot(ya_ref[...], woa_ref[...])
        pb = _dot(yb_ref[...], wob_ref[...])
        merged = (jax.nn.sigmoid(ma_ref[...].astype(F32)) * pa
                  + jax.nn.sigmoid(mb_ref[...].astype(F32)) * pb).astype(BF16)
        pa_ref[...] = pa.astype(BF16)
        pb_ref[...] = pb.astype(BF16)
        mg_ref[...] = merged
        h1_ref[...] = x_ref[...] + _dot(merged, wo_ref[...])

    row = pl.BlockSpec((tm, D), lambda i: (i, 0))
    wsp = pl.BlockSpec((D, D), lambda i: (0, 0))
    return _call(
        body, name="merge_out", grid=(s // tm,),
        in_specs=[row, row, pl.BlockSpec((tm, D), lambda i: (i, 4)), pl.BlockSpec((tm, D), lambda i: (i, 5)),
                  row, wsp, wsp, wsp],
        out_specs=[row, row, row, row],
        out_shape=[SDS((s, D), BF16), SDS((s, D), BF16), SDS((s, D), BF16), SDS((s, D), F32)], scratch_shapes=[],
        params=_cparams(("arbitrary",), 48), args=(ya, yb, z, z, x, w_oa, w_ob, w_out), comm=comm)


def _mlp_fwd(h1, g_mlp, w_up_g, w_down, g_fin, tgt):
    s = h1.shape[0]
    tm = min(TM_ROWS, s)
    nj = N_SLOT

    def body(h1_ref, gm_ref, wu_ref, wd_ref, gf_ref, t_ref, r_ref, n2_ref, dh2_ref, loss_ref, dgf_ref, n2_s, acc_s):
        i, j = pl.program_id(0), pl.program_id(1)

        @pl.when(j == 0)
        def _():
            hv = h1_ref[...]
            rstd = lax.rsqrt(jnp.mean(hv * hv, axis=-1, keepdims=True) + NORM_EPS)
            nb = (hv * rstd * gm_ref[...]).astype(BF16)
            n2_s[...] = nb
            n2_ref[...] = nb
            acc_s[...] = jnp.zeros_like(acc_s)

        @pl.when((i == 0) & (j == 0))
        def _():
            loss_ref[...] = jnp.zeros_like(loss_ref)
            dgf_ref[...] = jnp.zeros_like(dgf_ref)

        r = jnp.maximum(_dot(n2_s[...], wu_ref[...]), 0.0)
        r_ref[...] = r.astype(BF16)
        acc_s[...] += _dot((r * r).astype(BF16), wd_ref[...])

        @pl.when(j == nj - 1)
        def _():
            h2 = h1_ref[...] + acc_s[...]
            rstd = lax.rsqrt(jnp.mean(h2 * h2, axis=-1, keepdims=True) + NORM_EPS)
            hh = h2 * rstd
            gf = gf_ref[...]
            e = hh * gf - t_ref[...]
            loss_ref[...] += jnp.sum(e * e) * (0.5 / D)
            dy = e * (1.0 / D)
            dgf_ref[...] += jnp.sum(dy * hh, axis=0, keepdims=True)
            dhh = dy * gf
            dh2_ref[...] = rstd * (dhh - hh * jnp.mean(dhh * hh, axis=-1, keepdims=True))

    row = pl.BlockSpec((tm, D), lambda i, j: (i, 0))
    vec = pl.BlockSpec((1, D), lambda i, j: (0, 0))
    return pl.pallas_call(
        body, name="mlp_fwd", grid=(s // tm, nj),
        in_specs=[row, vec, pl.BlockSpec((None, D, FF_COLS), lambda i, j: (j, 0, 0)),
                  pl.BlockSpec((FF_COLS, D), lambda i, j: (j, 0)), vec, row],
        out_specs=[pl.BlockSpec((tm, FF_COLS), lambda i, j: (i, j)), row, row,
                   pl.BlockSpec((8, 128), lambda i, j: (0, 0)), vec],
        out_shape=[SDS((s, nj * FF_COLS), BF16), SDS((s, D), BF16), SDS((s, D), F32),
                   SDS((8, 128), F32), SDS((1, D), F32)],
        scratch_shapes=[pltpu.VMEM((tm, D), BF16), pltpu.VMEM((tm, D), F32)],
        compiler_params=_cparams(("arbitrary", "arbitrary"), 52),
    )(h1, g_mlp, w_up_g, w_down, g_fin, tgt)


def _mlp_bwd(dh2, r, w_down, w_up_g, h1, g_mlp, comm=None):
    s = h1.shape[0]
    tm = min(TM_ROWS, s)
    nj = N_SLOT

    def body(dh2_ref, r_ref, wd_ref, wu_ref, h1_ref, gm_ref, df_ref, dh1_ref, dgm_ref, dh2b_s, acc_s):
        i, j = pl.program_id(0), pl.program_id(1)

        @pl.when(j == 0)
        def _():
            dh2b_s[...] = dh2_ref[...].astype(BF16)
            acc_s[...] = jnp.zeros_like(acc_s)

        @pl.when((i == 0) & (j == 0))
        def _():
            dgm_ref[...] = jnp.zeros_like(dgm_ref)

        d_act = _dot_nt(dh2b_s[...], wd_ref[...])
        df = (d_act * (2.0 * r_ref[...].astype(F32))).astype(BF16)
        df_ref[...] = df
        acc_s[...] += _dot_nt(df, wu_ref[...])

        @pl.when(j == nj - 1)
        def _():
            hv = h1_ref[...]
            rstd = lax.rsqrt(jnp.mean(hv * hv, axis=-1, keepdims=True) + NORM_EPS)
            hh = hv * rstd
            dn2 = acc_s[...]
            dgm_ref[...] += jnp.sum(dn2 * hh, axis=0, keepdims=True)
            dhat = dn2 * gm_ref[...]
            dh1_ref[...] = dh2_ref[...] + rstd * (dhat - hh * jnp.mean(dhat * hh, axis=-1, keepdims=True))

    row = pl.BlockSpec((tm, D), lambda i, j: (i, 0))
    vec = pl.BlockSpec((1, D), lambda i, j: (0, 0))
    ffb = pl.BlockSpec((tm, FF_COLS), lambda i, j: (i, j))
    return _call(
        body, name="mlp_bwd", grid=(s // tm, nj),
        in_specs=[row, ffb, pl.BlockSpec((FF_COLS, D), lambda i, j: (j, 0)),
                  pl.BlockSpec((None, D, FF_COLS), lambda i, j: (j, 0, 0)), row, vec],
        out_specs=[ffb, row, vec],
        out_shape=[SDS((s, nj * FF_COLS), BF16), SDS((s, D), F32), SDS((1, D), F32)],
        scratch_shapes=[pltpu.VMEM((tm, D), BF16), pltpu.VMEM((tm, D), F32)],
        params=_cparams(("arbitrary", "arbitrary"), 52), args=(dh2, r, w_down, w_up_g, h1, g_mlp), comm=comm)


def _merge_bwd(dh1, z, pa, pb, w_out, w_oa, w_ob, comm=None):
    s = dh1.shape[0]
    tm = min(TM_MERGE, s)

    def body(dh1_ref, ma_ref, mb_ref, pa_ref, pb_ref, wo_ref, woa_ref, wob_ref,
             dz_ref, dpa_ref, dpb_ref, dya_ref, dyb_ref):
        dm = _dot_nt(dh1_ref[...].astype(BF16), wo_ref[...])
        sa = jax.nn.sigmoid(ma_ref[...].astype(F32))
        sb = jax.nn.sigmoid(mb_ref[...].astype(F32))
        dpa = (dm * sa).astype(BF16)
        dpb = (dm * sb).astype(BF16)
        dz_ref[:, 0:D] = (dm * pa_ref[...].astype(F32) * sa * (1.0 - sa)).astype(BF16)
        dz_ref[:, D:2 * D] = (dm * pb_ref[...].astype(F32) * sb * (1.0 - sb)).astype(BF16)
        dpa_ref[...] = dpa
        dpb_ref[...] = dpb
        dya_ref[...] = _dot_nt(dpa, woa_ref[...]).astype(BF16)
        dyb_ref[...] = _dot_nt(dpb, wob_ref[...]).astype(BF16)

    row = pl.BlockSpec((tm, D), lambda i: (i, 0))
    wsp = pl.BlockSpec((D, D), lambda i: (0, 0))
    return _call(
        body, name="merge_bwd", grid=(s // tm,),
        in_specs=[row, pl.BlockSpec((tm, D), lambda i: (i, 4)), pl.BlockSpec((tm, D), lambda i: (i, 5)),
                  row, row, wsp, wsp, wsp],
        out_specs=[pl.BlockSpec((tm, 2 * D), lambda i: (i, 2)), row, row, row, row],
        out_shape=[SDS((s, 6 * D), BF16)] + [SDS((s, D), BF16)] * 4, scratch_shapes=[],
        params=_cparams(("arbitrary",), 48), args=(dh1, z, z, pa, pb, w_out, w_oa, w_ob), comm=comm)


def _branch_b_bwd(dz, dyb, z, ln_g, ln_b, w_s, b_s_t, comm=None):
    s = z.shape[0]
    tb = min(T_BRANCH, s)

    def body(dz_in, dyb_ref, ub_ref, vb_ref, lg_ref, lb_ref, ws_ref, bs_ref,
             dz_ref, dws_ref, dbs_ref, dln_ref, du_s, dvln_s):
        del dz_in

        @pl.when(pl.program_id(0) == 0)
        def _():
            dws_ref[...] = jnp.zeros_like(dws_ref)
            dbs_ref[...] = jnp.zeros_like(dbs_ref)
            dln_ref[...] = jnp.zeros_like(dln_ref)

        lg = lg_ref[...]
        u, du, dv, rstd, vhat, vln = _sgu_common(ub_ref[...].astype(F32), vb_ref[...].astype(F32),
                                                 lg, lb_ref[...], True)
        vlnb = vln.astype(BF16)
        dyb_v = dyb_ref[...].astype(F32)
        wm = _masked_ws(ws_ref)
        keep = (lax.broadcasted_iota(jnp.int32, (CHUNK, CHUNK), 1)
                <= lax.broadcasted_iota(jnp.int32, (CHUNK, CHUNK), 0))
        bs = bs_ref[...]
        for c in range(tb // CHUNK):
            rs = slice(c * CHUNK, (c + 1) * CHUNK)
            for g in range(GROUPS):
                cs = slice(g * GROUP_DIM, (g + 1) * GROUP_DIM)
                v_blk = vlnb[rs, cs]
                sp = _dot(wm[g], v_blk) + bs[:, g:g + 1]
                d_sp = dyb_v[rs, cs] * u[rs, cs]
                d_spb = d_sp.astype(BF16)
                du_s[rs, cs] = dyb_v[rs, cs] * sp
                dvln_s[rs, cs] = _dot_tn(wm[g], d_spb)
                dws_ref[g] += jnp.where(keep, _dot_nt(d_spb, v_blk), 0.0)
                dbs_ref[g] += jnp.broadcast_to(jnp.sum(d_sp, axis=-1, keepdims=True), (CHUNK, CHUNK))
        dvln = dvln_s[...]
        dln_ref[0:1, :] += jnp.sum(dvln * vhat, axis=0, keepdims=True)
        dln_ref[1:2, :] += jnp.sum(dvln, axis=0, keepdims=True)
        dvh = dvln * lg
        d_v = rstd * (dvh - jnp.mean(dvh, axis=-1, keepdims=True)
                      - vhat * jnp.mean(dvh * vhat, axis=-1, keepdims=True))
        dz_ref[:, 0:D] = (du_s[...] * du).astype(BF16)
        dz_ref[:, D:2 * D] = (d_v * dv).astype(BF16)

    vec = pl.BlockSpec((1, D), lambda i: (0, 0))
    sq = pl.BlockSpec((GROUPS, CHUNK, CHUNK), lambda i: (0, 0, 0))
    return _call(
        body, name="branch_b_bwd", grid=(s // tb,),
        in_specs=[ANY, pl.BlockSpec((tb, D), lambda i: (i, 0)),
                  pl.BlockSpec((tb, D), lambda i: (i, 2)), pl.BlockSpec((tb, D), lambda i: (i, 3)), vec, vec, sq,
                  pl.BlockSpec((CHUNK, GROUPS), lambda i: (0, 0))],
        out_specs=[pl.BlockSpec((tb, 2 * D), lambda i: (i, 1)), sq, sq, pl.BlockSpec((8, D), lambda i: (0, 0))],
        out_shape=[SDS(dz.shape, BF16), SDS((GROUPS, CHUNK, CHUNK), F32), SDS((GROUPS, CHUNK, CHUNK), F32),
                   SDS((8, D), F32)],
        scratch_shapes=[pltpu.VMEM((tb, D), F32), pltpu.VMEM((tb, D), F32)], aliases={0: 0},
        params=_cparams(("arbitrary",), 40), args=(dz, dyb, z, z, ln_g, ln_b, w_s, b_s_t), comm=comm)


def _branch_a_bwd(dz, dya, z, hs, conv_w, conv_b, w_r, b_r, w_i, b_i, lam, comm=None):
    s = z.shape[0]
    ta = min(T_BRANCH, s)
    nb = s // ta
    per16 = ta // 16

    def body(dz_in, dya_ref, xa_ref, xp_ref, ga_ref, hs_ref, hp_ref, cw_ref, cb_ref, wr_ref, br_ref, wi_ref,
             bi_ref, lam_ref, dz_ref, vec_ref, dwr_ref, dwi_ref,
             a_s, b_s, h_s, r_s, i_s, m_s, dcar_s, acar_s, dxc_s):
        del dz_in
        i = pl.program_id(0)
        blk = nb - 1 - i

        @pl.when(i == 0)
        def _():
            dcar_s[...] = jnp.zeros_like(dcar_s)
            acar_s[...] = jnp.zeros_like(acar_s)
            dxc_s[...] = jnp.zeros_like(dxc_s)
            vec_ref[...] = jnp.zeros_like(vec_ref)
            dwr_ref[...] = jnp.zeros_like(dwr_ref)
            dwi_ref[...] = jnp.zeros_like(dwi_ref)

        cw = cw_ref[...]
        lam_v = lam_ref[...]
        xa = xa_ref[...].astype(F32)
        prev8 = jnp.where(blk > 0, xp_ref[...].astype(F32)[8:16], 0.0)
        xc = _conv_fwd(xa, prev8, cw, cb_ref[...])
        xcb = xc.astype(BF16)
        sp_lam = _softplus(-lam_v)
        _lru_gates(xc, xcb, wr_ref, br_ref[...], wi_ref, bi_ref[...], sp_lam, a_s, b_s, r_s, i_s, m_s)

        hs_v = hs_ref[...].astype(F32)
        hprev8 = jnp.where(blk > 0, hp_ref[...].astype(F32)[8:16], 0.0)
        h_m1 = _rows_shifted(hprev8, hs_v, 1)
        gg, dgg = _gelu_and_grad(ga_ref[...].astype(F32))
        dya_v = dya_ref[...].astype(F32)
        dz_ref[:, D:2 * D] = (dya_v * hs_v * dgg).astype(BF16)

        a_v = a_s[...]
        a_s[...] = _rows_advanced(a_v, acar_s[...], 1)
        b_s[...] = dya_v * gg

        row = lax.broadcasted_iota(jnp.int32, (8, D), 0)
        ng = ta // 8

        def group(gi, carry):
            off = pl.multiple_of((ng - 1 - gi) * 8, 8)
            c8 = a_s[pl.ds(off, 8), :]
            d8 = b_s[pl.ds(off, 8), :]
            for d in (1, 2, 4):
                c_sh = jnp.where(row < 8 - d, pltpu.roll(c8, 8 - d, 0), 1.0)
                d_sh = jnp.where(row < 8 - d, pltpu.roll(d8, 8 - d, 0), 0.0)
                d8 = c8 * d_sh + d8
                c8 = c8 * c_sh
            dh8 = d8 + c8 * carry
            h_s[pl.ds(off, 8), :] = dh8
            return jnp.broadcast_to(dh8[0:1, :], (8, D))

        dcar_s[...] = lax.fori_loop(0, ng, group, dcar_s[...])
        acar_s[...] = jnp.broadcast_to(a_v[0:1, :], (8, D))

        dbx = h_s[...]
        r_v, i_v, m_v = r_s[...], i_s[...], m_s[...]
        d_mult = dbx * xc * i_v
        d_loga = dbx * h_m1 * a_v - d_mult * (a_v * a_v) / m_v
        d_pr = d_loga * ((-LRU_C) * sp_lam) * r_v * (1.0 - r_v)
        d_pi = dbx * xc * m_v * i_v * (1.0 - i_v)
        vec_ref[7:8, :] += jnp.sum(d_loga * r_v, axis=0, keepdims=True) * (LRU_C * jax.nn.sigmoid(-lam_v))
        vec_ref[5:6, :] += jnp.sum(d_pr, axis=0, keepdims=True)
        vec_ref[6:7, :] += jnp.sum(d_pi, axis=0, keepdims=True)
        d_prb = d_pr.astype(BF16)
        d_pib = d_pi.astype(BF16)
        h_s[...] = dbx * i_v * m_v
        for h in range(HEADS):
            sl = slice(h * HEAD_DIM, (h + 1) * HEAD_DIM)
            h_s[:, sl] += _dot_nt(d_prb[:, sl], wr_ref[h]) + _dot_nt(d_pib[:, sl], wi_ref[h])
            dwr_ref[h] += _dot_tn(xcb[:, sl], d_prb[:, sl])
            dwi_ref[h] += _dot_tn(xcb[:, sl], d_pib[:, sl])
        d_xc = h_s[...]
        vec_ref[4:5, :] += jnp.sum(d_xc, axis=0, keepdims=True)
        vec_ref[0:1, :] += jnp.sum(d_xc * xa, axis=0, keepdims=True)
        d_xa = cw[0:1, :] * d_xc
        nxt = dxc_s[...]
        for k in range(1, CONV_K):
            vec_ref[k:k + 1, :] += jnp.sum(d_xc * _rows_shifted(prev8, xa, k), axis=0, keepdims=True)
            d_xa = d_xa + cw[k:k + 1, :] * _rows_advanced(d_xc, nxt, k)
        dz_ref[:, 0:D] = d_xa.astype(BF16)
        dxc_s[...] = d_xc[0:8, :]

    vec = pl.BlockSpec((1, D), lambda i: (0, 0))
    gate = pl.BlockSpec((HEADS, HEAD_DIM, HEAD_DIM), lambda i: (0, 0, 0))
    cur = lambda c: pl.BlockSpec((ta, D), lambda i: (nb - 1 - i, c))
    before = lambda c: pl.BlockSpec((16, D), lambda i: (jnp.maximum((nb - 1 - i) * per16 - 1, 0), c))
    return _call(
        body, name="branch_a_bwd", grid=(nb,),
        in_specs=[ANY, cur(0), cur(0), before(0), cur(1), cur(0), before(0),
                  pl.BlockSpec((CONV_K, D), lambda i: (0, 0)), vec, gate, vec, gate, vec, vec],
        out_specs=[pl.BlockSpec((ta, 2 * D), lambda i: (nb - 1 - i, 0)), pl.BlockSpec((8, D), lambda i: (0, 0)),
                   gate, gate],
        out_shape=[SDS(dz.shape, BF16), SDS((8, D), F32), SDS((HEADS, HEAD_DIM, HEAD_DIM), F32),
                   SDS((HEADS, HEAD_DIM, HEAD_DIM), F32)],
        scratch_shapes=[pltpu.VMEM((ta, D), F32)] * 6 + [pltpu.VMEM((8, D), F32)] * 3, aliases={0: 0},
        params=_cparams(("arbitrary",), 48),
        args=(dz, dya, z, z, z, hs, hs, conv_w, conv_b, w_r, b_r, w_i, b_i, lam), comm=comm)


def _in_bwd(dz, w_in_g, x, dh1, g_mix, comm=None):
    s = x.shape[0]
    tm = min(TM_ROWS, s)
    nj = N_SLOT

    def body(dz_ref, w_ref, x_ref, dh1_ref, g_ref, dx_ref, dg_ref, acc_s):
        i, j = pl.program_id(0), pl.program_id(1)

        @pl.when(j == 0)
        def _():
            acc_s[...] = jnp.zeros_like(acc_s)

        @pl.when((i == 0) & (j == 0))
        def _():
            dg_ref[...] = jnp.zeros_like(dg_ref)

        acc_s[...] += _dot_nt(dz_ref[...], w_ref[...])

        @pl.when(j == nj - 1)
        def _():
            xv = x_ref[...]
            rstd = lax.rsqrt(jnp.mean(xv * xv, axis=-1, keepdims=True) + NORM_EPS)
            xh = xv * rstd
            dn = acc_s[...]
            dg_ref[...] += jnp.sum(dn * xh, axis=0, keepdims=True)
            dhat = dn * g_ref[...]
            dx_ref[...] = dh1_ref[...] + rstd * (dhat - xh * jnp.mean(dhat * xh, axis=-1, keepdims=True))

    row = pl.BlockSpec((tm, D), lambda i, j: (i, 0))
    vec = pl.BlockSpec((1, D), lambda i, j: (0, 0))
    return _call(
        body, name="in_bwd", grid=(s // tm, nj),
        in_specs=[pl.BlockSpec((tm, W_IN_COLS), lambda i, j: (i, j)),
                  pl.BlockSpec((None, D, W_IN_COLS), lambda i, j: (j, 0, 0)), row, row, vec],
        out_specs=[row, vec],
        out_shape=[SDS((s, D), F32), SDS((1, D), F32)],
        scratch_shapes=[pltpu.VMEM((tm, D), F32)],
        params=_cparams(("arbitrary", "arbitrary"), 48), args=(dz, w_in_g, x, dh1, g_mix), comm=comm)


def _wgrad(name, a, b, nblk, a_split, b_split, square_a=False):
    s = a.shape[0]
    ts = min(TM_ROWS, s)
    a_w = a.shape[1] // nblk if a_split else a.shape[1]
    b_w = b.shape[1] // nblk if b_split else b.shape[1]

    def body(a_ref, b_ref, o_ref, acc_s):
        t = pl.program_id(1)

        @pl.when(t == 0)
        def _():
            acc_s[...] = jnp.zeros_like(acc_s)

        av = a_ref[...]
        if square_a:
            av = av * av
        acc_s[...] += _dot_tn(av.astype(BF16), b_ref[...].astype(BF16))

        @pl.when(t == pl.num_programs(1) - 1)
        def _():
            o_ref[...] = acc_s[...].astype(BF16)

    return pl.pallas_call(
        body, name=name, grid=(nblk, s // ts),
        in_specs=[pl.BlockSpec((ts, a_w), (lambda k, t: (t, k)) if a_split else (lambda k, t: (t, 0))),
                  pl.BlockSpec((ts, b_w), (lambda k, t: (t, k)) if b_split else (lambda k, t: (t, 0)))],
        out_specs=pl.BlockSpec((None, a_w, b_w), lambda k, t: (k, 0, 0)),
        out_shape=SDS((nblk, a_w, b_w), BF16),
        scratch_shapes=[pltpu.VMEM((a_w, b_w), F32)],
        compiler_params=_cparams(("arbitrary", "arbitrary"), 48),
    )(a, b)


def _place():
    x, y, c = lax.axis_index("x"), lax.axis_index("y"), lax.axis_index("c")
    return x, y, c


def _other_chips(x, y):
    return [(x, 1 - y, 2 * x + 1 - y), (1 - x, y, 2 * (1 - x) + y), (1 - x, 1 - y, 2 * (1 - x) + 1 - y)]


class _Plan:
    def __init__(self, arrays, out_shape, sems, start, finish):
        self.arrays, self.out_shape, self.sems, self.start, self.finish = arrays, out_shape, sems, start, finish


def _gather_plan(shards):
    n = len(shards)

    def copies(ins, outs, sems):
        send_sems, recv_sems, local_sems = sems
        x, y, c = _place()
        chip = 2 * x + y
        me = 2 * chip + c
        sib = (x, y, 1 - c)
        chips = _other_chips(x, y)

        def rc(k, t, src, blk, to):
            return pltpu.make_async_remote_copy(
                src_ref=src, dst_ref=outs[t].at[blk], send_sem=send_sems.at[k * n + t],
                recv_sem=recv_sems.at[k * n + t], device_id=to, device_id_type=MESH)

        local = [pltpu.make_async_copy(ins[t], outs[t].at[me], local_sems.at[t]) for t in range(n)]
        sends = [rc(0, t, ins[t], me, sib) for t in range(n)]
        for j, (px, py, _) in enumerate(chips):
            sends += [rc(1 + j, t, ins[t], me, (px, py, c)) for t in range(n)]
        return rc, local, sends, chips, chip, c, sib

    def start(ins, outs, sems):
        _, local, sends, _, _, _, _ = copies(ins, outs, sems)
        for cp in local + sends:
            cp.start()

    def finish(ins, outs, sems):
        rc, local, sends, chips, chip, c, sib = copies(ins, outs, sems)
        passed = []
        for j, (px, py, pc) in enumerate(chips):
            blk = 2 * pc + c
            for t in range(n):
                rc(1 + j, t, ins[t], blk, sib).wait_recv()
            for t in range(n):
                cp = rc(4 + j, t, outs[t].at[blk], blk, sib)
                cp.start()
                passed.append(cp)
        for t in range(n):
            rc(0, t, ins[t], 2 * chip + 1 - c, sib).wait_recv()
        for j, (px, py, pc) in enumerate(chips):
            for t in range(n):
                rc(4 + j, t, ins[t], 2 * pc + 1 - c, sib).wait_recv()
        for cp in sends + passed:
            cp.wait_send()
        for cp in local:
            cp.wait()

    return _Plan(list(shards), [SDS((N_SLOT,) + tuple(a.shape), a.dtype) for a in shards],
                 [pltpu.SemaphoreType.DMA((7 * n,)), pltpu.SemaphoreType.DMA((7 * n,)),
                  pltpu.SemaphoreType.DMA((n,))], start, finish)


def _sibling_plan(grads, whole=()):
    n, m = len(grads), len(whole)

    def copies(ins, outs, sems):
        send_sems, recv_sems = sems
        x, y, c = _place()
        sib = (x, y, 1 - c)

        def rc(t, src, dst):
            return pltpu.make_async_remote_copy(src_ref=src, dst_ref=dst, send_sem=send_sems.at[t],
                                                recv_sem=recv_sems.at[t], device_id=sib, device_id_type=MESH)
        return rc, c

    def start(ins, outs, sems):
        rc, c = copies(ins, outs, sems)
        for t in range(n):
            for j in range(4):
                rc(t, ins[t].at[2 * j + 1 - c], outs[t].at[j]).start()
        for t in range(n, n + m):
            rc(t, ins[t], outs[t]).start()

    def finish(ins, outs, sems):
        rc, _ = copies(ins, outs, sems)
        for t in range(n):
            rc(t, ins[t].at[pl.ds(0, 4)], outs[t]).wait()
        for t in range(n, n + m):
            rc(t, ins[t], outs[t]).wait()

    return _Plan(list(grads) + list(whole),
                 [SDS((4,) + tuple(g.shape[1:]), g.dtype) for g in grads] + [SDS(a.shape, a.dtype) for a in whole],
                 [pltpu.SemaphoreType.DMA((n + m,)), pltpu.SemaphoreType.DMA((n + m,))], start, finish)


def _chips_plan(parts, whole=()):
    n, m = len(parts), len(whole)

    def src_of(ins, t, pc):
        return ins[t].at[pc] if t < n else ins[t]

    def local_copies(ins, outs, sems, chip):
        return [pltpu.make_async_copy(src_of(ins, t, chip), outs[t].at[chip], sems[2].at[t]) for t in range(n + m)]

    def start(ins, outs, sems):
        send_sems, recv_sems, _ = sems
        x, y, c = _place()
        chip = 2 * x + y
        for cp in local_copies(ins, outs, sems, chip):
            cp.start()
        for px, py, pc in _other_chips(x, y):
            for t in range(n + m):
                pltpu.make_async_remote_copy(src_ref=src_of(ins, t, pc), dst_ref=outs[t].at[chip],
                                             send_sem=send_sems.at[t], recv_sem=recv_sems.at[t],
                                             device_id=(px, py, c), device_id_type=MESH).start()

    def finish(ins, outs, sems):
        send_sems, recv_sems, _ = sems
        x, y, c = _place()
        for t in range(n + m):
            three = outs[t].at[pl.ds(0, 3)]
            pltpu.make_async_remote_copy(src_ref=three, dst_ref=three, send_sem=send_sems.at[t],
                                         recv_sem=recv_sems.at[t], device_id=(x, y, c), device_id_type=MESH).wait()
        for cp in local_copies(ins, outs, sems, 2 * x + y):
            cp.wait()

    return _Plan(list(parts) + list(whole),
                 [SDS(p.shape, p.dtype) for p in parts] + [SDS((4,) + tuple(a.shape), a.dtype) for a in whole],
                 [pltpu.SemaphoreType.DMA((n + m,)), pltpu.SemaphoreType.DMA((n + m,)),
                  pltpu.SemaphoreType.DMA((n + m,))], start, finish)


def _exchange_plan(arr):
    def peers(x, y, c):
        flip = lambda v, f: 1 - v if f else v
        return [(flip(x, fx), flip(y, fy), flip(c, fc))
                for fx in (0, 1) for fy in (0, 1) for fc in (0, 1) if fx or fy or fc]

    def start(ins, outs, sems):
        x, y, c = _place()
        me = 4 * x + 2 * y + c
        pltpu.make_async_copy(ins[0], outs[0].at[me], sems[2].at[0]).start()
        for to in peers(x, y, c):
            pltpu.make_async_remote_copy(src_ref=ins[0], dst_ref=outs[0].at[me], send_sem=sems[0].at[0],
                                         recv_sem=sems[1].at[0], device_id=to, device_id_type=MESH).start()

    def finish(ins, outs, sems):
        x, y, c = _place()
        seven = outs[0].at[pl.ds(0, 7)]
        pltpu.make_async_remote_copy(src_ref=seven, dst_ref=seven, send_sem=sems[0].at[0], recv_sem=sems[1].at[0],
                                     device_id=(x, y, c), device_id_type=MESH).wait()
        pltpu.make_async_copy(ins[0], outs[0].at[4 * x + 2 * y + c], sems[2].at[0]).wait()

    return _Plan([arr], [SDS((N_SLOT,) + tuple(arr.shape), arr.dtype)],
                 [pltpu.SemaphoreType.DMA((1,)), pltpu.SemaphoreType.DMA((1,)), pltpu.SemaphoreType.DMA((1,))],
                 start, finish)


def _join(*plans):
    def cut(seq, sizes):
        out, at = [], 0
        for k in sizes:
            out.append(seq[at:at + k])
            at += k
        return out

    n_arr = [len(p.arrays) for p in plans]
    n_sem = [len(p.sems) for p in plans]

    def start(ins, outs, sems):
        for p, i, o, s in zip(plans, cut(ins, n_arr), cut(outs, n_arr), cut(sems, n_sem)):
            p.start(i, o, s)

    def finish(ins, outs, sems):
        for p, i, o, s in zip(plans, cut(ins, n_arr), cut(outs, n_arr), cut(sems, n_sem)):
            p.finish(i, o, s)

    return _Plan([a for p in plans for a in p.arrays], [o for p in plans for o in p.out_shape],
                 [s for p in plans for s in p.sems], start, finish)


def _run_plan(name, plan):
    k = len(plan.arrays)

    def body(*refs):
        ins, outs, sems = refs[:k], refs[k:2 * k], refs[2 * k:]
        plan.start(ins, outs, sems)
        plan.finish(ins, outs, sems)

    return pl.pallas_call(
        body, name=name, in_specs=[ANY] * k, out_specs=[ANY] * k, out_shape=plan.out_shape,
        scratch_shapes=plan.sems, compiler_params=pltpu.CompilerParams(has_side_effects=True),
    )(*plan.arrays)


def _call(body, *, name, grid, in_specs, out_specs, out_shape, scratch_shapes, params, args, comm=None,
          aliases=None, prefetch=()):
    aliases = aliases or {}
    n_pre = len(prefetch)

    def launch(fn, ins_specs, outs_specs, outs_shape, scratch, operands):
        spec = pltpu.PrefetchScalarGridSpec(num_scalar_prefetch=n_pre, grid=grid, in_specs=ins_specs,
                                            out_specs=outs_specs, scratch_shapes=scratch)
        return pl.pallas_call(fn, name=name, grid_spec=spec, out_shape=outs_shape, compiler_params=params,
                              input_output_aliases=aliases)(*prefetch, *operands)

    if comm is None:
        return list(launch(body, in_specs, out_specs, out_shape, scratch_shapes, args)), []
    n_in, n_out, n_scr, k = len(in_specs), len(out_specs), len(scratch_shapes), len(comm.arrays)

    def wrapped(*refs):
        pre, refs = refs[:n_pre], refs[n_pre:]
        ins = refs[:n_in]
        c_in = refs[n_in:n_in + k]
        outs = refs[n_in + k:n_in + k + n_out]
        c_out = refs[n_in + k + n_out:n_in + 2 * k + n_out]
        scr = refs[n_in + 2 * k + n_out:n_in + 2 * k + n_out + n_scr]
        sems = refs[n_in + 2 * k + n_out + n_scr:]
        ids = [pl.program_id(d) for d in range(len(grid))]
        first = ids[0] == 0
        last = ids[0] == grid[0] - 1
        for d in range(1, len(grid)):
            first = first & (ids[d] == 0)
            last = last & (ids[d] == grid[d] - 1)

        @pl.when(first)
        def _():
            comm.start(c_in, c_out, sems)

        body(*pre, *ins, *outs, *scr)

        @pl.when(last)
        def _():
            comm.finish(c_in, c_out, sems)

    res = launch(wrapped, list(in_specs) + [ANY] * k, list(out_specs) + [ANY] * k,
                 list(out_shape) + list(comm.out_shape), list(scratch_shapes) + list(comm.sems),
                 tuple(args) + tuple(comm.arrays))
    return list(res[:n_out]), list(res[n_out:])


def _row_tile(rows):
    for t in (512, 256, 128, 64, 32, 16, 8):
        if rows % t == 0:
            return t
    return rows


def _pair_sum(name, g8, recv4, core):
    _, rows, cols = recv4.shape
    tr = _row_tile(rows)
    g42 = g8.reshape(4, 2, rows, cols)

    def body(c_ref, g_ref, r_ref, o_ref):
        del c_ref
        o_ref[...] = (g_ref[...].astype(F32) + r_ref[...].astype(F32)).astype(o_ref.dtype)

    return pl.pallas_call(
        body, name=name,
        grid_spec=pltpu.PrefetchScalarGridSpec(
            num_scalar_prefetch=1, grid=(4, rows // tr),
            in_specs=[pl.BlockSpec((None, None, tr, cols), lambda j, i, c_ref: (j, c_ref[0], i, 0)),
                      pl.BlockSpec((None, tr, cols), lambda j, i, c_ref: (j, i, 0))],
            out_specs=pl.BlockSpec((None, tr, cols), lambda j, i, c_ref: (j, i, 0))),
        out_shape=SDS(recv4.shape, g8.dtype),
        compiler_params=_cparams(("arbitrary", "arbitrary"), 32),
    )(core, g42, recv4)


def _add2(name, a, b):
    rows, cols = a.shape
    tr = _row_tile(rows)

    def body(a_ref, b_ref, o_ref):
        o_ref[...] = a_ref[...] + b_ref[...]

    blk = pl.BlockSpec((tr, cols), lambda i: (i, 0))
    return pl.pallas_call(body, name=name, grid=(rows // tr,), in_specs=[blk, blk], out_specs=blk,
                          out_shape=SDS(a.shape, a.dtype),
                          compiler_params=_cparams(("arbitrary",), 32))(a, b)


def _sum_terms(name, terms):
    k, rows, cols = terms.shape
    tr = _row_tile(rows)

    def body(r_ref, o_ref):
        acc = r_ref[0]
        for q in range(1, k):
            acc = acc + r_ref[q]
        o_ref[...] = acc

    return pl.pallas_call(body, name=name, grid=(rows // tr,),
                          in_specs=[pl.BlockSpec((k, tr, cols), lambda i: (0, i, 0))],
                          out_specs=pl.BlockSpec((tr, cols), lambda i: (i, 0)),
                          out_shape=SDS((rows, cols), terms.dtype),
                          compiler_params=_cparams(("arbitrary",), 32))(terms)


def _adam_update(g, w, m, v):
    c1 = 1.0 / (1.0 - ADAM_B1 ** ADAM_STEP)
    c2 = 1.0 / (1.0 - ADAM_B2 ** ADAM_STEP)
    mn = ADAM_B1 * m + (1.0 - ADAM_B1) * g
    vn = ADAM_B2 * v + (1.0 - ADAM_B2) * (g * g)
    delta = (-ADAM_LR) * ((mn * c1) / (jnp.sqrt(vn * c2) + ADAM_EPS) + ADAM_WD * w)
    return delta, mn, vn


def _adamw_many(name, gs, ws, ms, vs):
    n = len(gs)

    def body(*refs):
        for p in range(n):
            g, w, m, v = (refs[q * n + p][...] for q in range(4))
            d, mn, vn = _adam_update(g, w, m, v)
            refs[4 * n + p][...] = d
            refs[5 * n + p][...] = mn
            refs[6 * n + p][...] = vn

    full = [pl.BlockSpec(memory_space=pltpu.VMEM)] * n
    shapes = [SDS(w.shape, F32) for w in ws]
    res = pl.pallas_call(body, name=name, in_specs=full * 4, out_specs=full * 3, out_shape=shapes * 3,
                         compiler_params=pltpu.CompilerParams(vmem_limit_bytes=32 * MiB))(*gs, *ws, *ms, *vs)
    return [(res[p], res[n + p], res[2 * n + p]) for p in range(n)]


def _adamw(name, terms, w, m, v):
    k, rows, cols = terms.shape
    tr = _row_tile(rows)

    def body(t_ref, w_ref, m_ref, v_ref, g_ref, d_ref, mo_ref, vo_ref):
        g = t_ref[0].astype(F32)
        for q in range(1, k):
            g = g + t_ref[q].astype(F32)
        g_ref[...] = g
        d_ref[...], mo_ref[...], vo_ref[...] = _adam_update(g, w_ref[...], m_ref[...], v_ref[...])

    blk = pl.BlockSpec((tr, cols), lambda i: (i, 0))
    return pl.pallas_call(body, name=name, grid=(rows // tr,),
                          in_specs=[pl.BlockSpec((k, tr, cols), lambda i: (0, i, 0)), blk, blk, blk],
                          out_specs=[blk] * 4, out_shape=[SDS((rows, cols), F32)] * 4,
                          compiler_params=_cparams(("arbitrary",), 40))(terms, w, m, v)


def kernel(x, norm_mix_g, w_in, conv_w, conv_b, w_rgate, b_rgate, w_igate, b_igate, lru_lambda, w_out_a, sgu_ln_g, sgu_ln_b, sgu_w_s, sgu_b_s, w_out_b, w_out, norm_mlp_g, w_up, w_down, norm_final_g, loss_target, m_norm_mix_g, m_w_in, m_conv_w, m_conv_b, m_w_rgate, m_b_rgate, m_w_igate, m_b_igate, m_lru_lambda, m_w_out_a, m_sgu_ln_g, m_sgu_ln_b, m_sgu_w_s, m_sgu_b_s, m_w_out_b, m_w_out, m_norm_mlp_g, m_w_up, m_w_down, m_norm_final_g, v_norm_mix_g, v_w_in, v_conv_w, v_conv_b, v_w_rgate, v_b_rgate, v_w_igate, v_b_igate, v_lru_lambda, v_w_out_a, v_sgu_ln_g, v_sgu_ln_b, v_sgu_w_s, v_sgu_b_s, v_w_out_b, v_w_out, v_norm_mlp_g, v_w_up, v_w_down, v_norm_final_g):
    cx, cy, cc = _place()
    me = 4 * cx + 2 * cy + cc
    core = jnp.reshape(cc, (1,)).astype(jnp.int32)
    xs = x[0]
    tgt = loss_target[0]
    s = xs.shape[0]

    gate_shard = jnp.stack([w_rgate[0], w_igate[0]]).astype(BF16).reshape(2 * HEADS * 32, HEAD_DIM)
    vec_shard = jnp.concatenate([conv_w[0], b_rgate[0], b_igate[0]], axis=1)
    vec_shard = jnp.pad(vec_shard, ((0, 4), (0, 256 - vec_shard.shape[1])))
    shards = [w_in[0].astype(BF16), w_out_a[0].astype(BF16), w_out_b[0].astype(BF16), w_out[0].astype(BF16),
              w_up[0].astype(BF16), w_down[0].astype(BF16), gate_shard, vec_shard]
    (z, n1, w_in_g), (gate_g, vec_g) = _in_proj(xs, norm_mix_g, shards[0], _slot_order(cx, cy, cc),
                                                comm=_gather_plan(shards[6:8]))
    gates = gate_g.reshape(N_SLOT, 2, HEADS, 32, HEAD_DIM).transpose(1, 2, 0, 3, 4).reshape(2, HEADS, HEAD_DIM, HEAD_DIM)
    w_r_f, w_i_f = gates[0], gates[1]
    conv_w_f = vec_g[:, 0:4, 0:128].transpose(1, 0, 2).reshape(CONV_K, D)
    b_r_f = vec_g[:, 0:4, 128:160].transpose(1, 0, 2).reshape(1, D)
    b_i_f = vec_g[:, 0:4, 160:192].transpose(1, 0, 2).reshape(1, D)
    b_s_t = jnp.transpose(sgu_b_s[0])

    up_top, up_bottom = shards[4][:D // 2], shards[4][D // 2:]
    (ya, hs), (w_oa_g, w_ob_g, w_out_g, up_top_g) = _branch_a_fwd(
        z, conv_w_f, conv_b, w_r_f, b_r_f, w_i_f, b_i_f, lru_lambda, comm=_gather_plan(shards[1:4] + [up_top]))
    w_oa_f = w_oa_g.reshape(D, D)
    w_ob_f = w_ob_g.reshape(D, D)
    w_out_f = w_out_g.reshape(D, D)
    (yb,), (up_bottom_g,) = _branch_b_fwd(z, sgu_ln_g, sgu_ln_b, sgu_w_s[0], b_s_t, comm=_gather_plan([up_bottom]))
    w_up_g = jnp.concatenate([up_top_g, up_bottom_g], axis=1)
    (pa, pb, merged, h1), (w_down_g,) = _merge_out(ya, yb, z, xs, w_oa_f, w_ob_f, w_out_f,
                                                   comm=_gather_plan(shards[5:6]))
    w_down_f = w_down_g.reshape(N_SLOT * FF_COLS, D)
    gf2 = norm_final_g.reshape(1, D)
    r_act, n2, dh2, loss_acc, d_gfin = _mlp_fwd(h1, norm_mlp_g, w_up_g, w_down_f, gf2, tgt)

    def pair(names, grads, recv):
        return [_pair_sum("pair_sum_" + nm, g, r, core) for nm, g, r in zip(names, grads, recv)]

    g_down = _wgrad("wgrad_down", r_act, dh2, N_SLOT, True, False, square_a=True)
    (df, dh1, d_gmlp), (r_down,) = _mlp_bwd(dh2, r_act, w_down_f, w_up_g, h1, norm_mlp_g,
                                            comm=_sibling_plan([g_down]))
    (p_down,) = pair(["down"], [g_down], [r_down])
    g_up = _wgrad("wgrad_up", n2, df, N_SLOT, False, True)
    g_out = _wgrad("wgrad_out", merged, dh1, 1, False, False).reshape(N_SLOT, D // N_SLOT, D)
    (dz, dpa, dpb, dya, dyb), (got_down, r_up, r_out) = _merge_bwd(
        dh1, z, pa, pb, w_out_f, w_oa_f, w_ob_f, comm=_join(_chips_plan([p_down]), _sibling_plan([g_up, g_out])))
    p_up, p_out = pair(["up", "out"], [g_up, g_out], [r_up, r_out])
    g_oa = _wgrad("wgrad_out_a", ya, dpa, 1, False, False).reshape(N_SLOT, D // N_SLOT, D)
    g_ob = _wgrad("wgrad_out_b", yb, dpb, 1, False, False).reshape(N_SLOT, D // N_SLOT, D)
    (dz, d_ws, d_bs, d_ln), (got_up, r_oa, r_ob) = _branch_b_bwd(
        dz, dyb, z, sgu_ln_g, sgu_ln_b, sgu_w_s[0], b_s_t,
        comm=_join(_chips_plan([p_up]), _sibling_plan([g_oa, g_ob])))
    p_oa, p_ob = pair(["out_a", "out_b"], [g_oa, g_ob], [r_oa, r_ob])
    (dz, d_vec, d_wr, d_wi), (got_out, got_oa, got_ob) = _branch_a_bwd(
        dz, dya, z, hs, conv_w_f, conv_b, w_r_f, b_r_f, w_i_f, b_i_f, lru_lambda,
        comm=_chips_plan([p_out, p_oa, p_ob]))
    g_in = _wgrad("wgrad_in", n1, dz, N_SLOT, False, True)
    g_gate = jnp.stack([d_wr, d_wi]).reshape(2, HEADS, N_SLOT, 32, HEAD_DIM).transpose(2, 0, 1, 3, 4)
    g_gate = g_gate.reshape(N_SLOT, 2 * HEADS * 32, HEAD_DIM).astype(BF16)

    d_bs_row = jnp.pad(d_bs[:, :, 0].reshape(1, GROUPS * CHUNK), ((0, 0), (0, D - GROUPS * CHUNK)))
    vecs = jnp.concatenate([d_vec, jnp.concatenate([d_ln[0:2], d_gmlp, d_gfin, d_bs_row, jnp.zeros((3, D), F32)])])
    d_ws2 = d_ws.reshape(GROUPS * CHUNK, CHUNK)
    r_in, r_gate, r_vecs, r_ws = _run_plan("rs_sibling_in", _sibling_plan([g_in, g_gate], [vecs, d_ws2]))
    p_in, p_gate = pair(["in", "gate"], [g_in, g_gate], [r_in, r_gate])
    vecs_chip = _add2("pair_sum_vecs", vecs, r_vecs)
    ws_chip = _add2("pair_sum_ws", d_ws2, r_ws)
    (dx, d_gmix), (got_in, got_gate, got_vecs, got_ws) = _in_bwd(
        dz, w_in_g, xs, dh1, norm_mix_g, comm=_chips_plan([p_in, p_gate], [vecs_chip, ws_chip]))
    vecs_sum = _sum_terms("sum_vecs", got_vecs)
    last = jnp.concatenate([d_gmix, jnp.pad(loss_acc[0:1], ((0, 0), (0, D - 128))), jnp.zeros((6, D), F32)])
    (last_all,) = _run_plan("exchange_last", _exchange_plan(last))
    last_sum = _sum_terms("sum_last", last_all)
    loss = last_sum[1, 0]
    got = [got_in, got_oa, got_ob, got_out, got_up, got_down, got_gate]

    def step(nm, terms, w, m, v, rows, cols):
        g, d, mn, vn = _adamw("adamw_" + nm, terms.reshape(4, rows, cols), w.reshape(rows, cols),
                              m.reshape(rows, cols), v.reshape(rows, cols))
        return [a.reshape(w.shape) for a in (g, d, mn, vn)]

    o_in = step("in", got[0], w_in, m_w_in, v_w_in, D, W_IN_COLS)
    o_oa = step("out_a", got[1], w_out_a, m_w_out_a, v_w_out_a, D // N_SLOT, D)
    o_ob = step("out_b", got[2], w_out_b, m_w_out_b, v_w_out_b, D // N_SLOT, D)
    o_out = step("out", got[3], w_out, m_w_out, v_w_out, D // N_SLOT, D)
    o_up = step("up", got[4], w_up, m_w_up, v_w_up, D, FF_COLS)
    o_down = step("down", got[5], w_down, m_w_down, v_w_down, FF_COLS, D)
    gate_w = jnp.stack([w_rgate[0], w_igate[0]]).reshape(2 * HEADS * 32, HEAD_DIM)
    gate_m = jnp.stack([m_w_rgate[0], m_w_igate[0]]).reshape(2 * HEADS * 32, HEAD_DIM)
    gate_v = jnp.stack([v_w_rgate[0], v_w_igate[0]]).reshape(2 * HEADS * 32, HEAD_DIM)
    o_gate = _adamw("adamw_gate", got[6], gate_w, gate_m, gate_v)
    o_gate = [a.reshape(2, 1, HEADS, 32, HEAD_DIM) for a in o_gate]
    o_wr = [a[0] for a in o_gate]
    o_wi = [a[1] for a in o_gate]

    def own(full, width):
        return lax.dynamic_slice_in_dim(full, me * width, width, axis=1)

    small_g = {
        "norm_mix_g": last_sum[0:1], "conv_w": own(vecs_sum[0:4], 128), "conv_b": vecs_sum[4:5],
        "b_rgate": own(vecs_sum[5:6].reshape(HEADS, HEAD_DIM), 32),
        "b_igate": own(vecs_sum[6:7].reshape(HEADS, HEAD_DIM), 32),
        "lru_lambda": vecs_sum[7:8], "sgu_ln_g": vecs_sum[8:9], "sgu_ln_b": vecs_sum[9:10],
        "norm_mlp_g": vecs_sum[10:11], "norm_final_g": vecs_sum[11:12],
        "sgu_b_s": vecs_sum[12, 0:GROUPS * CHUNK].reshape(GROUPS, CHUNK),
    }
    small_w = {"norm_mix_g": (norm_mix_g, m_norm_mix_g, v_norm_mix_g), "conv_w": (conv_w, m_conv_w, v_conv_w),
               "conv_b": (conv_b, m_conv_b, v_conv_b), "b_rgate": (b_rgate, m_b_rgate, v_b_rgate),
               "b_igate": (b_igate, m_b_igate, v_b_igate), "lru_lambda": (lru_lambda, m_lru_lambda, v_lru_lambda),
               "sgu_ln_g": (sgu_ln_g, m_sgu_ln_g, v_sgu_ln_g), "sgu_ln_b": (sgu_ln_b, m_sgu_ln_b, v_sgu_ln_b),
               "norm_mlp_g": (norm_mlp_g, m_norm_mlp_g, v_norm_mlp_g),
               "norm_final_g": (norm_final_g, m_norm_final_g, v_norm_final_g),
               "sgu_b_s": (sgu_b_s, m_sgu_b_s, v_sgu_b_s), "sgu_w_s": (sgu_w_s, m_sgu_w_s, v_sgu_w_s)}
    order = list(small_g)
    as2d = lambda k, a: a.reshape(small_g[k].shape)
    upd = _adamw_many("adamw_small", [small_g[k] for k in order], *[[as2d(k, small_w[k][q]) for k in order]
                                                                     for q in range(3)])
    o_small = {k: [a.reshape(small_w[k][0].shape) for a in (small_g[k],) + u] for k, u in zip(order, upd)}
    ws3 = [a[0].reshape(GROUPS * CHUNK, CHUNK) for a in small_w.pop("sgu_w_s")]
    o_small["sgu_w_s"] = [a.reshape(sgu_w_s.shape) for a in _adamw("adamw_ws", got_ws, *ws3)]

    per_weight = {"norm_mix_g": o_small["norm_mix_g"], "w_in": o_in, "conv_w": o_small["conv_w"],
                  "conv_b": o_small["conv_b"], "w_rgate": o_wr, "b_rgate": o_small["b_rgate"], "w_igate": o_wi,
                  "b_igate": o_small["b_igate"], "lru_lambda": o_small["lru_lambda"], "w_out_a": o_oa,
                  "sgu_ln_g": o_small["sgu_ln_g"], "sgu_ln_b": o_small["sgu_ln_b"], "sgu_w_s": o_small["sgu_w_s"],
                  "sgu_b_s": o_small["sgu_b_s"], "w_out_b": o_ob, "w_out": o_out, "norm_mlp_g": o_small["norm_mlp_g"],
                  "w_up": o_up, "w_down": o_down, "norm_final_g": o_small["norm_final_g"]}
    names_w = list(per_weight)
    return (loss, dx[None], *[per_weight[k][0] for k in names_w], *[per_weight[k][1] for k in names_w],
            *[per_weight[k][2] for k in names_w], *[per_weight[k][3] for k in names_w])
```

```python
import jax
import jax.numpy as jnp
from jax import lax
from jax.experimental import pallas as pl
from jax.experimental.pallas import tpu as pltpu

F32 = jnp.float32
BF16 = jnp.bfloat16
SDS = jax.ShapeDtypeStruct
MESH = pl.DeviceIdType.MESH
ANY = pl.BlockSpec(memory_space=pltpu.HBM)

D = 1024
N_SLOT = 8
W_IN_COLS = 768
FF_COLS = 512
HEADS, HEAD_DIM = 4, 256
GROUPS, GROUP_DIM = 4, 256
CHUNK = 128
CONV_K = 4
NORM_EPS = 1e-6
LN_EPS = 1e-5
LRU_C = 8.0
ADAM_LR, ADAM_B1, ADAM_B2, ADAM_EPS, ADAM_WD, ADAM_STEP = 0.001, 0.9, 0.999, 1e-08, 0.01, 10

TM_ROWS = 1024
TM_MERGE = 512
T_BRANCH = 256
MiB = 1024 * 1024

_GELU_C = 0.7978845608028654
_GELU_A = 0.044715


def _cparams(sem, vmem_mib):
    return pltpu.CompilerParams(dimension_semantics=sem, vmem_limit_bytes=vmem_mib * MiB)


def _gelu(x):
    t = jnp.tanh(_GELU_C * (x + _GELU_A * x * x * x))
    return 0.5 * x * (1.0 + t)


def _gelu_and_grad(x):
    x2 = x * x
    t = jnp.tanh(_GELU_C * x * (1.0 + _GELU_A * x2))
    g = 0.5 * x * (1.0 + t)
    dg = 0.5 * (1.0 + t) + 0.5 * x * (1.0 - t * t) * _GELU_C * (1.0 + 3.0 * _GELU_A * x2)
    return g, dg


def _softplus(x):
    return jnp.maximum(x, 0.0) + jnp.log1p(jnp.exp(-jnp.abs(x)))


def _dot(a, b):
    return jnp.dot(a, b, preferred_element_type=F32)


def _dot_nt(a, b):
    return lax.dot_general(a, b, (((1,), (1,)), ((), ())), preferred_element_type=F32)


def _dot_tn(a, b):
    return lax.dot_general(a, b, (((0,), (0,)), ((), ())), preferred_element_type=F32)


def _rows_shifted(prev8, cur, k):
    ext = jnp.concatenate([prev8, cur], axis=0)
    return pltpu.roll(ext, k, 0)[8:]


def _rows_advanced(cur, next8, k):
    t = cur.shape[0]
    ext = jnp.concatenate([cur, next8], axis=0)
    return pltpu.roll(ext, t + 8 - k, 0)[:t]


def _slot_order(x, y, c):
    chip = 2 * x + y
    order = [2 * chip + c, 2 * chip + 1 - c]
    for _, _, pc in _other_chips(x, y):
        order += [2 * pc + c, 2 * pc + 1 - c]
    return jnp.stack(order).astype(jnp.int32)


def _in_proj(x, g_mix, w_in_own, order, comm=None):
    s = x.shape[0]
    tm = min(TM_ROWS, s)
    ni = s // tm

    def body(order_ref, x_ref, g_ref, own_ref, z_ref, n_ref, wg_ref, n_s, w_s, send_sems, recv_sems, local_sems):
        j, i = pl.program_id(0), pl.program_id(1)
        px, py, c = _place()
        chip = 2 * px + py
        me = 2 * chip + c
        sib = (px, py, 1 - c)
        chips = _other_chips(px, py)

        def rc(k, src, blk, to):
            return pltpu.make_async_remote_copy(src_ref=src, dst_ref=w_s.at[blk], send_sem=send_sems.at[k],
                                                recv_sem=recv_sems.at[k], device_id=to, device_id_type=MESH)

        own_in = pltpu.make_async_copy(own_ref, w_s.at[me], local_sems.at[0])
        sends = [rc(0, own_ref, me, sib)] + [rc(1 + q, own_ref, me, (qx, qy, c)) for q, (qx, qy, _) in enumerate(chips)]
        passed = [rc(4 + q, w_s.at[2 * pc + c], 2 * pc + c, sib) for q, (_, _, pc) in enumerate(chips)]
        keep = pltpu.make_async_copy(w_s, wg_ref, local_sems.at[1])

        @pl.when((i == 0) & (j == 0))
        def _():
            own_in.start()
            for cp in sends:
                cp.start()
            own_in.wait()

        @pl.when((i == 0) & (j == 1))
        def _():
            rc(0, own_ref, 2 * chip + 1 - c, sib).wait_recv()

        for q, (_, _, pc) in enumerate(chips):
            @pl.when((i == 0) & (j == 2 + 2 * q))
            def _():
                rc(1 + q, own_ref, 2 * pc + c, sib).wait_recv()
                passed[q].start()

            @pl.when((i == 0) & (j == 3 + 2 * q))
            def _():
                rc(4 + q, own_ref, 2 * pc + 1 - c, sib).wait_recv()

        rows = pl.ds(pl.multiple_of(i * tm, tm), tm)

        @pl.when(j == 0)
        def _():
            xv = x_ref[...]
            rstd = lax.rsqrt(jnp.mean(xv * xv, axis=-1, keepdims=True) + NORM_EPS)
            nb = (xv * rstd * g_ref[...]).astype(BF16)
            n_s[rows, :] = nb
            n_ref[...] = nb

        z_ref[...] = _dot(n_s[rows, :], w_s[order_ref[j]]).astype(BF16)

        @pl.when((i == 0) & (j == N_SLOT - 1))
        def _():
            keep.start()

        @pl.when((i == ni - 1) & (j == N_SLOT - 1))
        def _():
            for cp in sends + passed:
                cp.wait_send()
            keep.wait()

    first_pass = lambda j, i, o: (jnp.where(j == 0, i, ni - 1), 0)
    (z, n1, w_in_g), extra = _call(
        body, name="in_proj", grid=(N_SLOT, ni), prefetch=(order,),
        in_specs=[pl.BlockSpec((tm, D), first_pass),
                  pl.BlockSpec((1, D), lambda j, i, o: (0, 0)), ANY],
        out_specs=[pl.BlockSpec((tm, W_IN_COLS), lambda j, i, o: (i, o[j])),
                   pl.BlockSpec((tm, D), first_pass), ANY],
        out_shape=[SDS((s, N_SLOT * W_IN_COLS), BF16), SDS((s, D), BF16), SDS((N_SLOT, D, W_IN_COLS), BF16)],
        scratch_shapes=[pltpu.VMEM((s, D), BF16), pltpu.VMEM((N_SLOT, D, W_IN_COLS), BF16),
                        pltpu.SemaphoreType.DMA((7,)), pltpu.SemaphoreType.DMA((7,)), pltpu.SemaphoreType.DMA((2,))],
        params=_cparams(("arbitrary", "arbitrary"), 56), args=(x, g_mix, w_in_own), comm=comm)
    return (z, n1, w_in_g), extra


def _lru_gates(xc, xcb, wr_ref, br, wi_ref, bi, sp_lam, a_s, b_s, r_s=None, i_s=None, m_s=None):
    for h in range(HEADS):
        sl = slice(h * HEAD_DIM, (h + 1) * HEAD_DIM)
        r = jax.nn.sigmoid(_dot(xcb[:, sl], wr_ref[h]) + br[:, sl])
        ig = jax.nn.sigmoid(_dot(xcb[:, sl], wi_ref[h]) + bi[:, sl])
        log_a = (-LRU_C) * r * sp_lam[:, sl]
        a = jnp.exp(log_a)
        mult = jnp.sqrt(-jnp.tanh(log_a) * (a * a + 1.0))
        a_s[:, sl] = a
        b_s[:, sl] = xc[:, sl] * ig * mult
        if r_s is not None:
            r_s[:, sl] = r
            i_s[:, sl] = ig
            m_s[:, sl] = mult


def _conv_fwd(xa, prev8, cw, cb):
    xc = cb + cw[0:1, :] * xa
    for k in range(1, CONV_K):
        xc = xc + cw[k:k + 1, :] * _rows_shifted(prev8, xa, k)
    return xc


def _branch_a_fwd(z, conv_w, conv_b, w_r, b_r, w_i, b_i, lam, comm=None):
    s = z.shape[0]
    ta = min(T_BRANCH, s)
    per16 = ta // 16

    def body(xa_ref, xp_ref, ga_ref, cw_ref, cb_ref, wr_ref, br_ref, wi_ref, bi_ref, lam_ref,
             ya_ref, hs_ref, a_s, b_s, h_s, carry_s):
        i = pl.program_id(0)

        @pl.when(i == 0)
        def _():
            carry_s[...] = jnp.zeros_like(carry_s)

        xa = xa_ref[...].astype(F32)
        prev8 = jnp.where(i > 0, xp_ref[...].astype(F32)[8:16], 0.0)
        xc = _conv_fwd(xa, prev8, cw_ref[...], cb_ref[...])
        sp_lam = _softplus(-lam_ref[...])
        _lru_gates(xc, xc.astype(BF16), wr_ref, br_ref[...], wi_ref, bi_ref[...], sp_lam, a_s, b_s)

        row = lax.broadcasted_iota(jnp.int32, (8, D), 0)

        def group(g, carry):
            off = pl.multiple_of(g * 8, 8)
            a8 = a_s[pl.ds(off, 8), :]
            b8 = b_s[pl.ds(off, 8), :]
            for d in (1, 2, 4):
                a_sh = jnp.where(row >= d, pltpu.roll(a8, d, 0), 1.0)
                b_sh = jnp.where(row >= d, pltpu.roll(b8, d, 0), 0.0)
                b8 = a8 * b_sh + b8
                a8 = a8 * a_sh
            h8 = b8 + a8 * carry
            h_s[pl.ds(off, 8), :] = h8
            return jnp.broadcast_to(h8[7:8, :], (8, D))

        carry_s[...] = lax.fori_loop(0, ta // 8, group, carry_s[...])
        hs = h_s[...]
        hs_ref[...] = hs.astype(BF16)
        ya_ref[...] = (hs * _gelu(ga_ref[...].astype(F32))).astype(BF16)

    vec = pl.BlockSpec((1, D), lambda i: (0, 0))
    gate = pl.BlockSpec((HEADS, HEAD_DIM, HEAD_DIM), lambda i: (0, 0, 0))
    return _call(
        body, name="branch_a_fwd", grid=(s // ta,),
        in_specs=[pl.BlockSpec((ta, D), lambda i: (i, 0)),
                  pl.BlockSpec((16, D), lambda i: (jnp.maximum(i * per16 - 1, 0), 0)),
                  pl.BlockSpec((ta, D), lambda i: (i, 1)),
                  pl.BlockSpec((CONV_K, D), lambda i: (0, 0)), vec, gate, vec, gate, vec, vec],
        out_specs=[pl.BlockSpec((ta, D), lambda i: (i, 0)), pl.BlockSpec((ta, D), lambda i: (i, 0))],
        out_shape=[SDS((s, D), BF16), SDS((s, D), BF16)],
        scratch_shapes=[pltpu.VMEM((ta, D), F32), pltpu.VMEM((ta, D), F32), pltpu.VMEM((ta, D), F32),
                        pltpu.VMEM((8, D), F32)],
        params=_cparams(("arbitrary",), 40), args=(z, z, z, conv_w, conv_b, w_r, b_r, w_i, b_i, lam), comm=comm)


def _sgu_common(ub, vb, lg, lb, with_grad):
    if with_grad:
        u, du = _gelu_and_grad(ub)
        v, dv = _gelu_and_grad(vb)
    else:
        u, v, du, dv = _gelu(ub), _gelu(vb), None, None
    mu = jnp.mean(v, axis=-1, keepdims=True)
    vc = v - mu
    rstd = lax.rsqrt(jnp.mean(vc * vc, axis=-1, keepdims=True) + LN_EPS)
    vhat = vc * rstd
    vln = vhat * lg + lb
    return u, du, dv, rstd, vhat, vln


def _masked_ws(ws_ref):
    t = lax.broadcasted_iota(jnp.int32, (CHUNK, CHUNK), 0)
    c = lax.broadcasted_iota(jnp.int32, (CHUNK, CHUNK), 1)
    keep = c <= t
    return [jnp.where(keep, ws_ref[g], 0.0).astype(BF16) for g in range(GROUPS)]


def _branch_b_fwd(z, ln_g, ln_b, w_s, b_s_t, comm=None):
    s = z.shape[0]
    tb = min(T_BRANCH, s)

    def body(ub_ref, vb_ref, lg_ref, lb_ref, ws_ref, bs_ref, yb_ref):
        u, _, _, _, _, vln = _sgu_common(ub_ref[...].astype(F32), vb_ref[...].astype(F32),
                                         lg_ref[...], lb_ref[...], False)
        vlnb = vln.astype(BF16)
        wm = _masked_ws(ws_ref)
        bs = bs_ref[...]
        for c in range(tb // CHUNK):
            rs = slice(c * CHUNK, (c + 1) * CHUNK)
            for g in range(GROUPS):
                cs = slice(g * GROUP_DIM, (g + 1) * GROUP_DIM)
                sp = _dot(wm[g], vlnb[rs, cs]) + bs[:, g:g + 1]
                yb_ref[rs, cs] = (u[rs, cs] * sp).astype(BF16)

    vec = pl.BlockSpec((1, D), lambda i: (0, 0))
    return _call(
        body, name="branch_b_fwd", grid=(s // tb,),
        in_specs=[pl.BlockSpec((tb, D), lambda i: (i, 2)), pl.BlockSpec((tb, D), lambda i: (i, 3)), vec, vec,
                  pl.BlockSpec((GROUPS, CHUNK, CHUNK), lambda i: (0, 0, 0)),
                  pl.BlockSpec((CHUNK, GROUPS), lambda i: (0, 0))],
        out_specs=[pl.BlockSpec((tb, D), lambda i: (i, 0))],
        out_shape=[SDS((s, D), BF16)], scratch_shapes=[],
        params=_cparams(("arbitrary",), 40), args=(z, z, ln_g, ln_b, w_s, b_s_t), comm=comm)


def _merge_out(ya, yb, z, x, w_oa, w_ob, w_out, comm=None):
    s = x.shape[0]
    tm = min(TM_MERGE, s)

    def body(ya_ref, yb_ref, ma_ref, mb_ref, x_ref, woa_ref, wob_ref, wo_ref, pa_ref, pb_ref, mg_ref, h1_ref):
        pa = _dot(ya_ref[...], woa_ref[...])
        pb = _dot(yb_ref[...], wob_ref[...])
        merged = (jax.nn.sigmoid(ma_ref[...].astype(F32)) * pa
                  + jax.nn.sigmoid(mb_ref[...].astype(F32)) * pb).astype(BF16)
        pa_ref[...] = pa.astype(BF16)
        pb_ref[...] = pb.astype(BF16)
        mg_ref[...] = merged
        h1_ref[...] = x_ref[...] + _dot(merged, wo_ref[...])

    row = pl.BlockSpec((tm, D), lambda i: (i, 0))
    wsp = pl.BlockSpec((D, D), lambda i: (0, 0))
    return _call(
        body, name="merge_out", grid=(s // tm,),
        in_specs=[row, row, pl.BlockSpec((tm, D), lambda i: (i, 4)), pl.BlockSpec((tm, D), lambda i: (i, 5)),
                  row, wsp, wsp, wsp],
        out_specs=[row, row, row, row],
        out_shape=[SDS((s, D), BF16), SDS((s, D), BF16), SDS((s, D), BF16), SDS((s, D), F32)], scratch_shapes=[],
        params=_cparams(("arbitrary",), 48), args=(ya, yb, z, z, x, w_oa, w_ob, w_out), comm=comm)


def _mlp_fwd(h1, g_mlp, w_up_g, w_down, g_fin, tgt):
    s = h1.shape[0]
    tm = min(TM_ROWS, s)
    nj = N_SLOT

    def body(h1_ref, gm_ref, wu_ref, wd_ref, gf_ref, t_ref, r_ref, n2_ref, dh2_ref, loss_ref, dgf_ref, n2_s, acc_s):
        i, j = pl.program_id(0), pl.program_id(1)

        @pl.when(j == 0)
        def _():
            hv = h1_ref[...]
            rstd = lax.rsqrt(jnp.mean(hv * hv, axis=-1, keepdims=True) + NORM_EPS)
            nb = (hv * rstd * gm_ref[...]).astype(BF16)
            n2_s[...] = nb
            n2_ref[...] = nb
            acc_s[...] = jnp.zeros_like(acc_s)

        @pl.when((i == 0) & (j == 0))
        def _():
            loss_ref[...] = jnp.zeros_like(loss_ref)
            dgf_ref[...] = jnp.zeros_like(dgf_ref)

        r = jnp.maximum(_dot(n2_s[...], wu_ref[...]), 0.0)
        r_ref[...] = r.astype(BF16)
        acc_s[...] += _dot((r * r).astype(BF16), wd_ref[...])

        @pl.when(j == nj - 1)
        def _():
            h2 = h1_ref[...] + acc_s[...]
            rstd = lax.rsqrt(jnp.mean(h2 * h2, axis=-1, keepdims=True) + NORM_EPS)
            hh = h2 * rstd
            gf = gf_ref[...]
            e = hh * gf - t_ref[...]
            loss_ref[...] += jnp.sum(e * e) * (0.5 / D)
            dy = e * (1.0 / D)
            dgf_ref[...] += jnp.sum(dy * hh, axis=0, keepdims=True)
            dhh = dy * gf
            dh2_ref[...] = rstd * (dhh - hh * jnp.mean(dhh * hh, axis=-1, keepdims=True))

    row = pl.BlockSpec((tm, D), lambda i, j: (i, 0))
    vec = pl.BlockSpec((1, D), lambda i, j: (0, 0))
    return pl.pallas_call(
        body, name="mlp_fwd", grid=(s // tm, nj),
        in_specs=[row, vec, pl.BlockSpec((None, D, FF_COLS), lambda i, j: (j, 0, 0)),
                  pl.BlockSpec((FF_COLS, D), lambda i, j: (j, 0)), vec, row],
        out_specs=[pl.BlockSpec((tm, FF_COLS), lambda i, j: (i, j)), row, row,
                   pl.BlockSpec((8, 128), lambda i, j: (0, 0)), vec],
        out_shape=[SDS((s, nj * FF_COLS), BF16), SDS((s, D), BF16), SDS((s, D), F32),
                   SDS((8, 128), F32), SDS((1, D), F32)],
        scratch_shapes=[pltpu.VMEM((tm, D), BF16), pltpu.VMEM((tm, D), F32)],
        compiler_params=_cparams(("arbitrary", "arbitrary"), 52),
    )(h1, g_mlp, w_up_g, w_down, g_fin, tgt)


def _mlp_bwd(dh2, r, w_down, w_up_g, h1, g_mlp, comm=None):
    s = h1.shape[0]
    tm = min(TM_ROWS, s)
    nj = N_SLOT

    def body(dh2_ref, r_ref, wd_ref, wu_ref, h1_ref, gm_ref, df_ref, dh1_ref, dgm_ref, dh2b_s, acc_s):
        i, j = pl.program_id(0), pl.program_id(1)

        @pl.when(j == 0)
        def _():
            dh2b_s[...] = dh2_ref[...].astype(BF16)
            acc_s[...] = jnp.zeros_like(acc_s)

        @pl.when((i == 0) & (j == 0))
        def _():
            dgm_ref[...] = jnp.zeros_like(dgm_ref)

        d_act = _dot_nt(dh2b_s[...], wd_ref[...])
        df = (d_act * (2.0 * r_ref[...].astype(F32))).astype(BF16)
        df_ref[...] = df
        acc_s[...] += _dot_nt(df, wu_ref[...])

        @pl.when(j == nj - 1)
        def _():
            hv = h1_ref[...]
            rstd = lax.rsqrt(jnp.mean(hv * hv, axis=-1, keepdims=True) + NORM_EPS)
            hh = hv * rstd
            dn2 = acc_s[...]
            dgm_ref[...] += jnp.sum(dn2 * hh, axis=0, keepdims=True)
            dhat = dn2 * gm_ref[...]
            dh1_ref[...] = dh2_ref[...] + rstd * (dhat - hh * jnp.mean(dhat * hh, axis=-1, keepdims=True))

    row = pl.BlockSpec((tm, D), lambda i, j: (i, 0))
    vec = pl.BlockSpec((1, D), lambda i, j: (0, 0))
    ffb = pl.BlockSpec((tm, FF_COLS), lambda i, j: (i, j))
    return _call(
        body, name="mlp_bwd", grid=(s // tm, nj),
        in_specs=[row, ffb, pl.BlockSpec((FF_COLS, D), lambda i, j: (j, 0)),
                  pl.BlockSpec((None, D, FF_COLS), lambda i, j: (j, 0, 0)), row, vec],
        out_specs=[ffb, row, vec],
        out_shape=[SDS((s, nj * FF_COLS), BF16), SDS((s, D), F32), SDS((1, D), F32)],
        scratch_shapes=[pltpu.VMEM((tm, D), BF16), pltpu.VMEM((tm, D), F32)],
        params=_cparams(("arbitrary", "arbitrary"), 52), args=(dh2, r, w_down, w_up_g, h1, g_mlp), comm=comm)


def _merge_bwd(dh1, z, pa, pb, w_out, w_oa, w_ob, comm=None):
    s = dh1.shape[0]
    tm = min(TM_MERGE, s)

    def body(dh1_ref, ma_ref, mb_ref, pa_ref, pb_ref, wo_ref, woa_ref, wob_ref,
             dz_ref, dpa_ref, dpb_ref, dya_ref, dyb_ref):
        dm = _dot_nt(dh1_ref[...].astype(BF16), wo_ref[...])
        sa = jax.nn.sigmoid(ma_ref[...].astype(F32))
        sb = jax.nn.sigmoid(mb_ref[...].astype(F32))
        dpa = (dm * sa).astype(BF16)
        dpb = (dm * sb).astype(BF16)
        dz_ref[:, 0:D] = (dm * pa_ref[...].astype(F32) * sa * (1.0 - sa)).astype(BF16)
        dz_ref[:, D:2 * D] = (dm * pb_ref[...].astype(F32) * sb * (1.0 - sb)).astype(BF16)
        dpa_ref[...] = dpa
        dpb_ref[...] = dpb
        dya_ref[...] = _dot_nt(dpa, woa_ref[...]).astype(BF16)
        dyb_ref[...] = _dot_nt(dpb, wob_ref[...]).astype(BF16)

    row = pl.BlockSpec((tm, D), lambda i: (i, 0))
    wsp = pl.BlockSpec((D, D), lambda i: (0, 0))
    return _call(
        body, name="merge_bwd", grid=(s // tm,),
        in_specs=[row, pl.BlockSpec((tm, D), lambda i: (i, 4)), pl.BlockSpec((tm, D), lambda i: (i, 5)),
                  row, row, wsp, wsp, wsp],
        out_specs=[pl.BlockSpec((tm, 2 * D), lambda i: (i, 2)), row, row, row, row],
        out_shape=[SDS((s, 6 * D), BF16)] + [SDS((s, D), BF16)] * 4, scratch_shapes=[],
        params=_cparams(("arbitrary",), 48), args=(dh1, z, z, pa, pb, w_out, w_oa, w_ob), comm=comm)


def _branch_b_bwd(dz, dyb, z, ln_g, ln_b, w_s, b_s_t, comm=None):
    s = z.shape[0]
    tb = min(T_BRANCH, s)

    def body(dz_in, dyb_ref, ub_ref, vb_ref, lg_ref, lb_ref, ws_ref, bs_ref,
             dz_ref, dws_ref, dbs_ref, dln_ref, du_s, dvln_s):
        del dz_in

        @pl.when(pl.program_id(0) == 0)
        def _():
            dws_ref[...] = jnp.zeros_like(dws_ref)
            dbs_ref[...] = jnp.zeros_like(dbs_ref)
            dln_ref[...] = jnp.zeros_like(dln_ref)

        lg = lg_ref[...]
        u, du, dv, rstd, vhat, vln = _sgu_common(ub_ref[...].astype(F32), vb_ref[...].astype(F32),
                                                 lg, lb_ref[...], True)
        vlnb = vln.astype(BF16)
        dyb_v = dyb_ref[...].astype(F32)
        wm = _masked_ws(ws_ref)
        keep = (lax.broadcasted_iota(jnp.int32, (CHUNK, CHUNK), 1)
                <= lax.broadcasted_iota(jnp.int32, (CHUNK, CHUNK), 0))
        bs = bs_ref[...]
        for c in range(tb // CHUNK):
            rs = slice(c * CHUNK, (c + 1) * CHUNK)
            for g in range(GROUPS):
                cs = slice(g * GROUP_DIM, (g + 1) * GROUP_DIM)
                v_blk = vlnb[rs, cs]
                sp = _dot(wm[g], v_blk) + bs[:, g:g + 1]
                d_sp = dyb_v[rs, cs] * u[rs, cs]
                d_spb = d_sp.astype(BF16)
                du_s[rs, cs] = dyb_v[rs, cs] * sp
                dvln_s[rs, cs] = _dot_tn(wm[g], d_spb)
                dws_ref[g] += jnp.where(keep, _dot_nt(d_spb, v_blk), 0.0)
                dbs_ref[g] += jnp.broadcast_to(jnp.sum(d_sp, axis=-1, keepdims=True), (CHUNK, CHUNK))
        dvln = dvln_s[...]
        dln_ref[0:1, :] += jnp.sum(dvln * vhat, axis=0, keepdims=True)
        dln_ref[1:2, :] += jnp.sum(dvln, axis=0, keepdims=True)
        dvh = dvln * lg
        d_v = rstd * (dvh - jnp.mean(dvh, axis=-1, keepdims=True)
                      - vhat * jnp.mean(dvh * vhat, axis=-1, keepdims=True))
        dz_ref[:, 0:D] = (du_s[...] * du).astype(BF16)
        dz_ref[:, D:2 * D] = (d_v * dv).astype(BF16)

    vec = pl.BlockSpec((1, D), lambda i: (0, 0))
    sq = pl.BlockSpec((GROUPS, CHUNK, CHUNK), lambda i: (0, 0, 0))
    return _call(
        body, name="branch_b_bwd", grid=(s // tb,),
        in_specs=[ANY, pl.BlockSpec((tb, D), lambda i: (i, 0)),
                  pl.BlockSpec((tb, D), lambda i: (i, 2)), pl.BlockSpec((tb, D), lambda i: (i, 3)), vec, vec, sq,
                  pl.BlockSpec((CHUNK, GROUPS), lambda i: (0, 0))],
        out_specs=[pl.BlockSpec((tb, 2 * D), lambda i: (i, 1)), sq, sq, pl.BlockSpec((8, D), lambda i: (0, 0))],
        out_shape=[SDS(dz.shape, BF16), SDS((GROUPS, CHUNK, CHUNK), F32), SDS((GROUPS, CHUNK, CHUNK), F32),
                   SDS((8, D), F32)],
        scratch_shapes=[pltpu.VMEM((tb, D), F32), pltpu.VMEM((tb, D), F32)], aliases={0: 0},
        params=_cparams(("arbitrary",), 40), args=(dz, dyb, z, z, ln_g, ln_b, w_s, b_s_t), comm=comm)


def _branch_a_bwd(dz, dya, z, hs, conv_w, conv_b, w_r, b_r, w_i, b_i, lam, comm=None):
    s = z.shape[0]
    ta = min(T_BRANCH, s)
    nb = s // ta
    per16 = ta // 16

    def body(dz_in, dya_ref, xa_ref, xp_ref, ga_ref, hs_ref, hp_ref, cw_ref, cb_ref, wr_ref, br_ref, wi_ref,
             bi_ref, lam_ref, dz_ref, vec_ref, dwr_ref, dwi_ref,
             a_s, b_s, h_s, r_s, i_s, m_s, dcar_s, acar_s, dxc_s):
        del dz_in
        i = pl.program_id(0)
        blk = nb - 1 - i

        @pl.when(i == 0)
        def _():
            dcar_s[...] = jnp.zeros_like(dcar_s)
            acar_s[...] = jnp.zeros_like(acar_s)
            dxc_s[...] = jnp.zeros_like(dxc_s)
            vec_ref[...] = jnp.zeros_like(vec_ref)
            dwr_ref[...] = jnp.zeros_like(dwr_ref)
            dwi_ref[...] = jnp.zeros_like(dwi_ref)

        cw = cw_ref[...]
        lam_v = lam_ref[...]
        xa = xa_ref[...].astype(F32)
        prev8 = jnp.where(blk > 0, xp_ref[...].astype(F32)[8:16], 0.0)
        xc = _conv_fwd(xa, prev8, cw, cb_ref[...])
        xcb = xc.astype(BF16)
        sp_lam = _softplus(-lam_v)
        _lru_gates(xc, xcb, wr_ref, br_ref[...], wi_ref, bi_ref[...], sp_lam, a_s, b_s, r_s, i_s, m_s)

        hs_v = hs_ref[...].astype(F32)
        hprev8 = jnp.where(blk > 0, hp_ref[...].astype(F32)[8:16], 0.0)
        h_m1 = _rows_shifted(hprev8, hs_v, 1)
        gg, dgg = _gelu_and_grad(ga_ref[...].astype(F32))
        dya_v = dya_ref[...].astype(F32)
        dz_ref[:, D:2 * D] = (dya_v * hs_v * dgg).astype(BF16)

        a_v = a_s[...]
        a_s[...] = _rows_advanced(a_v, acar_s[...], 1)
        b_s[...] = dya_v * gg

        row = lax.broadcasted_iota(jnp.int32, (8, D), 0)
        ng = ta // 8

        def group(gi, carry):
            off = pl.multiple_of((ng - 1 - gi) * 8, 8)
            c8 = a_s[pl.ds(off, 8), :]
            d8 = b_s[pl.ds(off, 8), :]
            for d in (1, 2, 4):
                c_sh = jnp.where(row < 8 - d, pltpu.roll(c8, 8 - d, 0), 1.0)
                d_sh = jnp.where(row < 8 - d, pltpu.roll(d8, 8 - d, 0), 0.0)
                d8 = c8 * d_sh + d8
                c8 = c8 * c_sh
            dh8 = d8 + c8 * carry
            h_s[pl.ds(off, 8), :] = dh8
            return jnp.broadcast_to(dh8[0:1, :], (8, D))

        dcar_s[...] = lax.fori_loop(0, ng, group, dcar_s[...])
        acar_s[...] = jnp.broadcast_to(a_v[0:1, :], (8, D))

        dbx = h_s[...]
        r_v, i_v, m_v = r_s[...], i_s[...], m_s[...]
        d_mult = dbx * xc * i_v
        d_loga = dbx * h_m1 * a_v - d_mult * (a_v * a_v) / m_v
        d_pr = d_loga * ((-LRU_C) * sp_lam) * r_v * (1.0 - r_v)
        d_pi = dbx * xc * m_v * i_v * (1.0 - i_v)
        vec_ref[7:8, :] += jnp.sum(d_loga * r_v, axis=0, keepdims=True) * (LRU_C * jax.nn.sigmoid(-lam_v))
        vec_ref[5:6, :] += jnp.sum(d_pr, axis=0, keepdims=True)
        vec_ref[6:7, :] += jnp.sum(d_pi, axis=0, keepdims=True)
        d_prb = d_pr.astype(BF16)
        d_pib = d_pi.astype(BF16)
        h_s[...] = dbx * i_v * m_v
        for h in range(HEADS):
            sl = slice(h * HEAD_DIM, (h + 1) * HEAD_DIM)
            h_s[:, sl] += _dot_nt(d_prb[:, sl], wr_ref[h]) + _dot_nt(d_pib[:, sl], wi_ref[h])
            dwr_ref[h] += _dot_tn(xcb[:, sl], d_prb[:, sl])
            dwi_ref[h] += _dot_tn(xcb[:, sl], d_pib[:, sl])
        d_xc = h_s[...]
        vec_ref[4:5, :] += jnp.sum(d_xc, axis=0, keepdims=True)
        vec_ref[0:1, :] += jnp.sum(d_xc * xa, axis=0, keepdims=True)
        d_xa = cw[0:1, :] * d_xc
        nxt = dxc_s[...]
        for k in range(1, CONV_K):
            vec_ref[k:k + 1, :] += jnp.sum(d_xc * _rows_shifted(prev8, xa, k), axis=0, keepdims=True)
            d_xa = d_xa + cw[k:k + 1, :] * _rows_advanced(d_xc, nxt, k)
        dz_ref[:, 0:D] = d_xa.astype(BF16)
        dxc_s[...] = d_xc[0:8, :]

    vec = pl.BlockSpec((1, D), lambda i: (0, 0))
    gate = pl.BlockSpec((HEADS, HEAD_DIM, HEAD_DIM), lambda i: (0, 0, 0))
    cur = lambda c: pl.BlockSpec((ta, D), lambda i: (nb - 1 - i, c))
    before = lambda c: pl.BlockSpec((16, D), lambda i: (jnp.maximum((nb - 1 - i) * per16 - 1, 0), c))
    return _call(
        body, name="branch_a_bwd", grid=(nb,),
        in_specs=[ANY, cur(0), cur(0), before(0), cur(1), cur(0), before(0),
                  pl.BlockSpec((CONV_K, D), lambda i: (0, 0)), vec, gate, vec, gate, vec, vec],
        out_specs=[pl.BlockSpec((ta, 2 * D), lambda i: (nb - 1 - i, 0)), pl.BlockSpec((8, D), lambda i: (0, 0)),
                   gate, gate],
        out_shape=[SDS(dz.shape, BF16), SDS((8, D), F32), SDS((HEADS, HEAD_DIM, HEAD_DIM), F32),
                   SDS((HEADS, HEAD_DIM, HEAD_DIM), F32)],
        scratch_shapes=[pltpu.VMEM((ta, D), F32)] * 6 + [pltpu.VMEM((8, D), F32)] * 3, aliases={0: 0},
        params=_cparams(("arbitrary",), 48),
        args=(dz, dya, z, z, z, hs, hs, conv_w, conv_b, w_r, b_r, w_i, b_i, lam), comm=comm)


def _in_bwd(dz, w_in_g, x, dh1, g_mix, comm=None):
    s = x.shape[0]
    tm = min(TM_ROWS, s)
    nj = N_SLOT

    def body(dz_ref, w_ref, x_ref, dh1_ref, g_ref, dx_ref, dg_ref, acc_s):
        i, j = pl.program_id(0), pl.program_id(1)

        @pl.when(j == 0)
        def _():
            acc_s[...] = jnp.zeros_like(acc_s)

        @pl.when((i == 0) & (j == 0))
        def _():
            dg_ref[...] = jnp.zeros_like(dg_ref)

        acc_s[...] += _dot_nt(dz_ref[...], w_ref[...])

        @pl.when(j == nj - 1)
        def _():
            xv = x_ref[...]
            rstd = lax.rsqrt(jnp.mean(xv * xv, axis=-1, keepdims=True) + NORM_EPS)
            xh = xv * rstd
            dn = acc_s[...]
            dg_ref[...] += jnp.sum(dn * xh, axis=0, keepdims=True)
            dhat = dn * g_ref[...]
            dx_ref[...] = dh1_ref[...] + rstd * (dhat - xh * jnp.mean(dhat * xh, axis=-1, keepdims=True))

    row = pl.BlockSpec((tm, D), lambda i, j: (i, 0))
    vec = pl.BlockSpec((1, D), lambda i, j: (0, 0))
    return _call(
        body, name="in_bwd", grid=(s // tm, nj),
        in_specs=[pl.BlockSpec((tm, W_IN_COLS), lambda i, j: (i, j)),
                  pl.BlockSpec((None, D, W_IN_COLS), lambda i, j: (j, 0, 0)), row, row, vec],
        out_specs=[row, vec],
        out_shape=[SDS((s, D), F32), SDS((1, D), F32)],
        scratch_shapes=[pltpu.VMEM((tm, D), F32)],
        params=_cparams(("arbitrary", "arbitrary"), 48), args=(dz, w_in_g, x, dh1, g_mix), comm=comm)


def _wgrad(name, a, b, nblk, a_split, b_split, square_a=False):
    s = a.shape[0]
    ts = min(TM_ROWS, s)
    a_w = a.shape[1] // nblk if a_split else a.shape[1]
    b_w = b.shape[1] // nblk if b_split else b.shape[1]

    def body(a_ref, b_ref, o_ref, acc_s):
        t = pl.program_id(1)

        @pl.when(t == 0)
        def _():
            acc_s[...] = jnp.zeros_like(acc_s)

        av = a_ref[...]
        if square_a:
            av = av * av
        acc_s[...] += _dot_tn(av.astype(BF16), b_ref[...].astype(BF16))

        @pl.when(t == pl.num_programs(1) - 1)
        def _():
            o_ref[...] = acc_s[...].astype(BF16)

    return pl.pallas_call(
        body, name=name, grid=(nblk, s // ts),
        in_specs=[pl.BlockSpec((ts, a_w), (lambda k, t: (t, k)) if a_split else (lambda k, t: (t, 0))),
                  pl.BlockSpec((ts, b_w), (lambda k, t: (t, k)) if b_split else (lambda k, t: (t, 0)))],
        out_specs=pl.BlockSpec((None, a_w, b_w), lambda k, t: (k, 0, 0)),
        out_shape=SDS((nblk, a_w, b_w), BF16),
        scratch_shapes=[pltpu.VMEM((a_w, b_w), F32)],
        compiler_params=_cparams(("arbitrary", "arbitrary"), 48),
    )(a, b)


def _place():
    x, y, c = lax.axis_index("x"), lax.axis_index("y"), lax.axis_index("c")
    return x, y, c


def _other_chips(x, y):
    return [(x, 1 - y, 2 * x + 1 - y), (1 - x, y, 2 * (1 - x) + y), (1 - x, 1 - y, 2 * (1 - x) + 1 - y)]


class _Plan:
    def __init__(self, arrays, out_shape, sems, start, finish):
        self.arrays, self.out_shape, self.sems, self.start, self.finish = arrays, out_shape, sems, start, finish


def _gather_plan(shards):
    n = len(shards)

    def copies(ins, outs, sems):
        send_sems, recv_sems, local_sems = sems
        x, y, c = _place()
        chip = 2 * x + y
        me = 2 * chip + c
        sib = (x, y, 1 - c)
        chips = _other_chips(x, y)

        def rc(k, t, src, blk, to):
            return pltpu.make_async_remote_copy(
                src_ref=src, dst_ref=outs[t].at[blk], send_sem=send_sems.at[k * n + t],
                recv_sem=recv_sems.at[k * n + t], device_id=to, device_id_type=MESH)

        local = [pltpu.make_async_copy(ins[t], outs[t].at[me], local_sems.at[t]) for t in range(n)]
        sends = [rc(0, t, ins[t], me, sib) for t in range(n)]
        for j, (px, py, _) in enumerate(chips):
            sends += [rc(1 + j, t, ins[t], me, (px, py, c)) for t in range(n)]
        return rc, local, sends, chips, chip, c, sib

    def start(ins, outs, sems):
        _, local, sends, _, _, _, _ = copies(ins, outs, sems)
        for cp in local + sends:
            cp.start()

    def finish(ins, outs, sems):
        rc, local, sends, chips, chip, c, sib = copies(ins, outs, sems)
        passed = []
        for j, (px, py, pc) in enumerate(chips):
            blk = 2 * pc + c
            for t in range(n):
                rc(1 + j, t, ins[t], blk, sib).wait_recv()
            for t in range(n):
                cp = rc(4 + j, t, outs[t].at[blk], blk, sib)
                cp.start()
                passed.append(cp)
        for t in range(n):
            rc(0, t, ins[t], 2 * chip + 1 - c, sib).wait_recv()
        for j, (px, py, pc) in enumerate(chips):
            for t in range(n):
                rc(4 + j, t, ins[t], 2 * pc + 1 - c, sib).wait_recv()
        for cp in sends + passed:
            cp.wait_send()
        for cp in local:
            cp.wait()

    return _Plan(list(shards), [SDS((N_SLOT,) + tuple(a.shape), a.dtype) for a in shards],
                 [pltpu.SemaphoreType.DMA((7 * n,)), pltpu.SemaphoreType.DMA((7 * n,)),
                  pltpu.SemaphoreType.DMA((n,))], start, finish)


def _sibling_plan(grads, whole=()):
    n, m = len(grads), len(whole)

    def copies(ins, outs, sems):
        send_sems, recv_sems = sems
        x, y, c = _place()
        sib = (x, y, 1 - c)

        def rc(t, src, dst):
            return pltpu.make_async_remote_copy(src_ref=src, dst_ref=dst, send_sem=send_sems.at[t],
                                                recv_sem=recv_sems.at[t], device_id=sib, device_id_type=MESH)
        return rc, c

    def start(ins, outs, sems):
        rc, c = copies(ins, outs, sems)
        for t in range(n):
            for j in range(4):
                rc(t, ins[t].at[2 * j + 1 - c], outs[t].at[j]).start()
        for t in range(n, n + m):
            rc(t, ins[t], outs[t]).start()

    def finish(ins, outs, sems):
        rc, _ = copies(ins, outs, sems)
        for t in range(n):
            rc(t, ins[t].at[pl.ds(0, 4)], outs[t]).wait()
        for t in range(n, n + m):
            rc(t, ins[t], outs[t]).wait()

    return _Plan(list(grads) + list(whole),
                 [SDS((4,) + tuple(g.shape[1:]), g.dtype) for g in grads] + [SDS(a.shape, a.dtype) for a in whole],
                 [pltpu.SemaphoreType.DMA((n + m,)), pltpu.SemaphoreType.DMA((n + m,))], start, finish)


def _chips_plan(parts, whole=()):
    n, m = len(parts), len(whole)

    def src_of(ins, t, pc):
        return ins[t].at[pc] if t < n else ins[t]

    def local_copies(ins, outs, sems, chip):
        return [pltpu.make_async_copy(src_of(ins, t, chip), outs[t].at[chip], sems[2].at[t]) for t in range(n + m)]

    def start(ins, outs, sems):
        send_sems, recv_sems, _ = sems
        x, y, c = _place()
        chip = 2 * x + y
        for cp in local_copies(ins, outs, sems, chip):
            cp.start()
        for px, py, pc in _other_chips(x, y):
            for t in range(n + m):
                pltpu.make_async_remote_copy(src_ref=src_of(ins, t, pc), dst_ref=outs[t].at[chip],
                                             send_sem=send_sems.at[t], recv_sem=recv_sems.at[t],
                                             device_id=(px, py, c), device_id_type=MESH).start()

    def finish(ins, outs, sems):
        send_sems, recv_sems, _ = sems
        x, y, c = _place()
        for t in range(n + m):
            three = outs[t].at[pl.ds(0, 3)]
            pltpu.make_async_remote_copy(src_ref=three, dst_ref=three, send_sem=send_sems.at[t],
                                         recv_sem=recv_sems.at[t], device_id=(x, y, c), device_id_type=MESH).wait()
        for cp in local_copies(ins, outs, sems, 2 * x + y):
            cp.wait()

    return _Plan(list(parts) + list(whole),
                 [SDS(p.shape, p.dtype) for p in parts] + [SDS((4,) + tuple(a.shape), a.dtype) for a in whole],
                 [pltpu.SemaphoreType.DMA((n + m,)), pltpu.SemaphoreType.DMA((n + m,)),
                  pltpu.SemaphoreType.DMA((n + m,))], start, finish)


def _exchange_plan(arr):
    def peers(x, y, c):
        flip = lambda v, f: 1 - v if f else v
        return [(flip(x, fx), flip(y, fy), flip(c, fc))
                for fx in (0, 1) for fy in (0, 1) for fc in (0, 1) if fx or fy or fc]

    def start(ins, outs, sems):
        x, y, c = _place()
        me = 4 * x + 2 * y + c
        pltpu.make_async_copy(ins[0], outs[0].at[me], sems[2].at[0]).start()
        for to in peers(x, y, c):
            pltpu.make_async_remote_copy(src_ref=ins[0], dst_ref=outs[0].at[me], send_sem=sems[0].at[0],
                                         recv_sem=sems[1].at[0], device_id=to, device_id_type=MESH).start()

    def finish(ins, outs, sems):
        x, y, c = _place()
        seven = outs[0].at[pl.ds(0, 7)]
        pltpu.make_async_remote_copy(src_ref=seven, dst_ref=seven, send_sem=sems[0].at[0], recv_sem=sems[1].at[0],
                                     device_id=(x, y, c), device_id_type=MESH).wait()
        pltpu.make_async_copy(ins[0], outs[0].at[4 * x + 2 * y + c], sems[2].at[0]).wait()

    return _Plan([arr], [SDS((N_SLOT,) + tuple(arr.shape), arr.dtype)],
                 [pltpu.SemaphoreType.DMA((1,)), pltpu.SemaphoreType.DMA((1,)), pltpu.SemaphoreType.DMA((1,))],
                 start, finish)


def _join(*plans):
    def cut(seq, sizes):
        out, at = [], 0
        for k in sizes:
            out.append(seq[at:at + k])
            at += k
        return out

    n_arr = [len(p.arrays) for p in plans]
    n_sem = [len(p.sems) for p in plans]

    def start(ins, outs, sems):
        for p, i, o, s in zip(plans, cut(ins, n_arr), cut(outs, n_arr), cut(sems, n_sem)):
            p.start(i, o, s)

    def finish(ins, outs, sems):
        for p, i, o, s in zip(plans, cut(ins, n_arr), cut(outs, n_arr), cut(sems, n_sem)):
            p.finish(i, o, s)

    return _Plan([a for p in plans for a in p.arrays], [o for p in plans for o in p.out_shape],
                 [s for p in plans for s in p.sems], start, finish)


def _run_plan(name, plan):
    k = len(plan.arrays)

    def body(*refs):
        ins, outs, sems = refs[:k], refs[k:2 * k], refs[2 * k:]
        plan.start(ins, outs, sems)
        plan.finish(ins, outs, sems)

    return pl.pallas_call(
        body, name=name, in_specs=[ANY] * k, out_specs=[ANY] * k, out_shape=plan.out_shape,
        scratch_shapes=plan.sems, compiler_params=pltpu.CompilerParams(has_side_effects=True),
    )(*plan.arrays)


def _call(body, *, name, grid, in_specs, out_specs, out_shape, scratch_shapes, params, args, comm=None,
          aliases=None, prefetch=()):
    aliases = aliases or {}
    n_pre = len(prefetch)

    def launch(fn, ins_specs, outs_specs, outs_shape, scratch, operands):
        spec = pltpu.PrefetchScalarGridSpec(num_scalar_prefetch=n_pre, grid=grid, in_specs=ins_specs,
                                            out_specs=outs_specs, scratch_shapes=scratch)
        return pl.pallas_call(fn, name=name, grid_spec=spec, out_shape=outs_shape, compiler_params=params,
                              input_output_aliases=aliases)(*prefetch, *operands)

    if comm is None:
        return list(launch(body, in_specs, out_specs, out_shape, scratch_shapes, args)), []
    n_in, n_out, n_scr, k = len(in_specs), len(out_specs), len(scratch_shapes), len(comm.arrays)

    def wrapped(*refs):
        pre, refs = refs[:n_pre], refs[n_pre:]
        ins = refs[:n_in]
        c_in = refs[n_in:n_in + k]
        outs = refs[n_in + k:n_in + k + n_out]
        c_out = refs[n_in + k + n_out:n_in + 2 * k + n_out]
        scr = refs[n_in + 2 * k + n_out:n_in + 2 * k + n_out + n_scr]
        sems = refs[n_in + 2 * k + n_out + n_scr:]
        ids = [pl.program_id(d) for d in range(len(grid))]
        first = ids[0] == 0
        last = ids[0] == grid[0] - 1
        for d in range(1, len(grid)):
            first = first & (ids[d] == 0)
            last = last & (ids[d] == grid[d] - 1)

        @pl.when(first)
        def _():
            comm.start(c_in, c_out, sems)

        body(*pre, *ins, *outs, *scr)

        @pl.when(last)
        def _():
            comm.finish(c_in, c_out, sems)

    res = launch(wrapped, list(in_specs) + [ANY] * k, list(out_specs) + [ANY] * k,
                 list(out_shape) + list(comm.out_shape), list(scratch_shapes) + list(comm.sems),
                 tuple(args) + tuple(comm.arrays))
    return list(res[:n_out]), list(res[n_out:])


def _row_tile(rows):
    for t in (512, 256, 128, 64, 32, 16, 8):
        if rows % t == 0:
            return t
    return rows


def _pair_sum(name, g8, recv4, core):
    _, rows, cols = recv4.shape
    tr = _row_tile(rows)
    g42 = g8.reshape(4, 2, rows, cols)

    def body(c_ref, g_ref, r_ref, o_ref):
        del c_ref
        o_ref[...] = (g_ref[...].astype(F32) + r_ref[...].astype(F32)).astype(o_ref.dtype)

    return pl.pallas_call(
        body, name=name,
        grid_spec=pltpu.PrefetchScalarGridSpec(
            num_scalar_prefetch=1, grid=(4, rows // tr),
            in_specs=[pl.BlockSpec((None, None, tr, cols), lambda j, i, c_ref: (j, c_ref[0], i, 0)),
                      pl.BlockSpec((None, tr, cols), lambda j, i, c_ref: (j, i, 0))],
            out_specs=pl.BlockSpec((None, tr, cols), lambda j, i, c_ref: (j, i, 0))),
        out_shape=SDS(recv4.shape, g8.dtype),
        compiler_params=_cparams(("arbitrary", "arbitrary"), 32),
    )(core, g42, recv4)


def _add2(name, a, b):
    rows, cols = a.shape
    tr = _row_tile(rows)

    def body(a_ref, b_ref, o_ref):
        o_ref[...] = a_ref[...] + b_ref[...]

    blk = pl.BlockSpec((tr, cols), lambda i: (i, 0))
    return pl.pallas_call(body, name=name, grid=(rows // tr,), in_specs=[blk, blk], out_specs=blk,
                          out_shape=SDS(a.shape, a.dtype),
                          compiler_params=_cparams(("arbitrary",), 32))(a, b)


def _sum_terms(name, terms):
    k, rows, cols = terms.shape
    tr = _row_tile(rows)

    def body(r_ref, o_ref):
        acc = r_ref[0]
        for q in range(1, k):
            acc = acc + r_ref[q]
        o_ref[...] = acc

    return pl.pallas_call(body, name=name, grid=(rows // tr,),
                          in_specs=[pl.BlockSpec((k, tr, cols), lambda i: (0, i, 0))],
                          out_specs=pl.BlockSpec((tr, cols), lambda i: (i, 0)),
                          out_shape=SDS((rows, cols), terms.dtype),
                          compiler_params=_cparams(("arbitrary",), 32))(terms)


def _adam_update(g, w, m, v):
    c1 = 1.0 / (1.0 - ADAM_B1 ** ADAM_STEP)
    c2 = 1.0 / (1.0 - ADAM_B2 ** ADAM_STEP)
    mn = ADAM_B1 * m + (1.0 - ADAM_B1) * g
    vn = ADAM_B2 * v + (1.0 - ADAM_B2) * (g * g)
    delta = (-ADAM_LR) * ((mn * c1) / (jnp.sqrt(vn * c2) + ADAM_EPS) + ADAM_WD * w)
    return delta, mn, vn


def _adamw_many(name, gs, ws, ms, vs):
    n = len(gs)

    def body(*refs):
        for p in range(n):
            g, w, m, v = (refs[q * n + p][...] for q in range(4))
            d, mn, vn = _adam_update(g, w, m, v)
            refs[4 * n + p][...] = d
            refs[5 * n + p][...] = mn
            refs[6 * n + p][...] = vn

    full = [pl.BlockSpec(memory_space=pltpu.VMEM)] * n
    shapes = [SDS(w.shape, F32) for w in ws]
    res = pl.pallas_call(body, name=name, in_specs=full * 4, out_specs=full * 3, out_shape=shapes * 3,
                         compiler_params=pltpu.CompilerParams(vmem_limit_bytes=32 * MiB))(*gs, *ws, *ms, *vs)
    return [(res[p], res[n + p], res[2 * n + p]) for p in range(n)]


def _adamw(name, terms, w, m, v):
    k, rows, cols = terms.shape
    tr = _row_tile(rows)

    def body(t_ref, w_ref, m_ref, v_ref, g_ref, d_ref, mo_ref, vo_ref):
        g = t_ref[0].astype(F32)
        for q in range(1, k):
            g = g + t_ref[q].astype(F32)
        g_ref[...] = g
        d_ref[...], mo_ref[...], vo_ref[...] = _adam_update(g, w_ref[...], m_ref[...], v_ref[...])

    blk = pl.BlockSpec((tr, cols), lambda i: (i, 0))
    return pl.pallas_call(body, name=name, grid=(rows // tr,),
                          in_specs=[pl.BlockSpec((k, tr, cols), lambda i: (0, i, 0)), blk, blk, blk],
                          out_specs=[blk] * 4, out_shape=[SDS((rows, cols), F32)] * 4,
                          compiler_params=_cparams(("arbitrary",), 40))(terms, w, m, v)


def kernel(x, norm_mix_g, w_in, conv_w, conv_b, w_rgate, b_rgate, w_igate, b_igate, lru_lambda, w_out_a, sgu_ln_g, sgu_ln_b, sgu_w_s, sgu_b_s, w_out_b, w_out, norm_mlp_g, w_up, w_down, norm_final_g, loss_target, m_norm_mix_g, m_w_in, m_conv_w, m_conv_b, m_w_rgate, m_b_rgate, m_w_igate, m_b_igate, m_lru_lambda, m_w_out_a, m_sgu_ln_g, m_sgu_ln_b, m_sgu_w_s, m_sgu_b_s, m_w_out_b, m_w_out, m_norm_mlp_g, m_w_up, m_w_down, m_norm_final_g, v_norm_mix_g, v_w_in, v_conv_w, v_conv_b, v_w_rgate, v_b_rgate, v_w_igate, v_b_igate, v_lru_lambda, v_w_out_a, v_sgu_ln_g, v_sgu_ln_b, v_sgu_w_s, v_sgu_b_s, v_w_out_b, v_w_out, v_norm_mlp_g, v_w_up, v_w_down, v_norm_final_g):
    cx, cy, cc = _place()
    me = 4 * cx + 2 * cy + cc
    core = jnp.reshape(cc, (1,)).astype(jnp.int32)
    xs = x[0]
    tgt = loss_target[0]
    s = xs.shape[0]

    gate_shard = jnp.stack([w_rgate[0], w_igate[0]]).astype(BF16).reshape(2 * HEADS * 32, HEAD_DIM)
    vec_shard = jnp.concatenate([conv_w[0], b_rgate[0], b_igate[0]], axis=1)
    vec_shard = jnp.pad(vec_shard, ((0, 4), (0, 256 - vec_shard.shape[1])))
    shards = [w_in[0].astype(BF16), w_out_a[0].astype(BF16), w_out_b[0].astype(BF16), w_out[0].astype(BF16),
              w_up[0].astype(BF16), w_down[0].astype(BF16), gate_shard, vec_shard]
    (z, n1, w_in_g), (gate_g, vec_g) = _in_proj(xs, norm_mix_g, shards[0], _slot_order(cx, cy, cc),
                                                comm=_gather_plan(shards[6:8]))
    gates = gate_g.reshape(N_SLOT, 2, HEADS, 32, HEAD_DIM).transpose(1, 2, 0, 3, 4).reshape(2, HEADS, HEAD_DIM, HEAD_DIM)
    w_r_f, w_i_f = gates[0], gates[1]
    conv_w_f = vec_g[:, 0:4, 0:128].transpose(1, 0, 2).reshape(CONV_K, D)
    b_r_f = vec_g[:, 0:4, 128:160].transpose(1, 0, 2).reshape(1, D)
    b_i_f = vec_g[:, 0:4, 160:192].transpose(1, 0, 2).reshape(1, D)
    b_s_t = jnp.transpose(sgu_b_s[0])

    (ya, hs), (w_oa_g, w_ob_g, w_out_g) = _branch_a_fwd(
        z, conv_w_f, conv_b, w_r_f, b_r_f, w_i_f, b_i_f, lru_lambda, comm=_gather_plan(shards[1:4]))
    w_oa_f = w_oa_g.reshape(D, D)
    w_ob_f = w_ob_g.reshape(D, D)
    w_out_f = w_out_g.reshape(D, D)
    (yb,), (w_up_g,) = _branch_b_fwd(z, sgu_ln_g, sgu_ln_b, sgu_w_s[0], b_s_t, comm=_gather_plan(shards[4:5]))
    (pa, pb, merged, h1), (w_down_g,) = _merge_out(ya, yb, z, xs, w_oa_f, w_ob_f, w_out_f,
                                                   comm=_gather_plan(shards[5:6]))
    w_down_f = w_down_g.reshape(N_SLOT * FF_COLS, D)
    gf2 = norm_final_g.reshape(1, D)
    r_act, n2, dh2, loss_acc, d_gfin = _mlp_fwd(h1, norm_mlp_g, w_up_g, w_down_f, gf2, tgt)

    def pair(names, grads, recv):
        return [_pair_sum("pair_sum_" + nm, g, r, core) for nm, g, r in zip(names, grads, recv)]

    g_down = _wgrad("wgrad_down", r_act, dh2, N_SLOT, True, False, square_a=True)
    (df, dh1, d_gmlp), (r_down,) = _mlp_bwd(dh2, r_act, w_down_f, w_up_g, h1, norm_mlp_g,
                                            comm=_sibling_plan([g_down]))
    (p_down,) = pair(["down"], [g_down], [r_down])
    g_up = _wgrad("wgrad_up", n2, df, N_SLOT, False, True)
    g_out = _wgrad("wgrad_out", merged, dh1, 1, False, False).reshape(N_SLOT, D // N_SLOT, D)
    (dz, dpa, dpb, dya, dyb), (got_down, r_up, r_out) = _merge_bwd(
        dh1, z, pa, pb, w_out_f, w_oa_f, w_ob_f, comm=_join(_chips_plan([p_down]), _sibling_plan([g_up, g_out])))
    p_up, p_out = pair(["up", "out"], [g_up, g_out], [r_up, r_out])
    g_oa = _wgrad("wgrad_out_a", ya, dpa, 1, False, False).reshape(N_SLOT, D // N_SLOT, D)
    g_ob = _wgrad("wgrad_out_b", yb, dpb, 1, False, False).reshape(N_SLOT, D // N_SLOT, D)
    (dz, d_ws, d_bs, d_ln), (got_up, r_oa, r_ob) = _branch_b_bwd(
        dz, dyb, z, sgu_ln_g, sgu_ln_b, sgu_w_s[0], b_s_t,
        comm=_join(_chips_plan([p_up]), _sibling_plan([g_oa, g_ob])))
    p_oa, p_ob = pair(["out_a", "out_b"], [g_oa, g_ob], [r_oa, r_ob])
    (dz, d_vec, d_wr, d_wi), (got_out, got_oa, got_ob) = _branch_a_bwd(
        dz, dya, z, hs, conv_w_f, conv_b, w_r_f, b_r_f, w_i_f, b_i_f, lru_lambda,
        comm=_chips_plan([p_out, p_oa, p_ob]))
    g_in = _wgrad("wgrad_in", n1, dz, N_SLOT, False, True)
    g_gate = jnp.stack([d_wr, d_wi]).reshape(2, HEADS, N_SLOT, 32, HEAD_DIM).transpose(2, 0, 1, 3, 4)
    g_gate = g_gate.reshape(N_SLOT, 2 * HEADS * 32, HEAD_DIM).astype(BF16)

    d_bs_row = jnp.pad(d_bs[:, :, 0].reshape(1, GROUPS * CHUNK), ((0, 0), (0, D - GROUPS * CHUNK)))
    vecs = jnp.concatenate([d_vec, jnp.concatenate([d_ln[0:2], d_gmlp, d_gfin, d_bs_row, jnp.zeros((3, D), F32)])])
    d_ws2 = d_ws.reshape(GROUPS * CHUNK, CHUNK)
    r_in, r_gate, r_vecs, r_ws = _run_plan("rs_sibling_in", _sibling_plan([g_in, g_gate], [vecs, d_ws2]))
    p_in, p_gate = pair(["in", "gate"], [g_in, g_gate], [r_in, r_gate])
    vecs_chip = _add2("pair_sum_vecs", vecs, r_vecs)
    ws_chip = _add2("pair_sum_ws", d_ws2, r_ws)
    (dx, d_gmix), (got_in, got_gate, got_vecs, got_ws) = _in_bwd(
        dz, w_in_g, xs, dh1, norm_mix_g, comm=_chips_plan([p_in, p_gate], [vecs_chip, ws_chip]))
    vecs_sum = _sum_terms("sum_vecs", got_vecs)
    last = jnp.concatenate([d_gmix, jnp.pad(loss_acc[0:1], ((0, 0), (0, D - 128))), jnp.zeros((6, D), F32)])
    (last_all,) = _run_plan("exchange_last", _exchange_plan(last))
    last_sum = _sum_terms("sum_last", last_all)
    loss = last_sum[1, 0]
    got = [got_in, got_oa, got_ob, got_out, got_up, got_down, got_gate]

    def step(nm, terms, w, m, v, rows, cols):
        g, d, mn, vn = _adamw("adamw_" + nm, terms.reshape(4, rows, cols), w.reshape(rows, cols),
                              m.reshape(rows, cols), v.reshape(rows, cols))
        return [a.reshape(w.shape) for a in (g, d, mn, vn)]

    o_in = step("in", got[0], w_in, m_w_in, v_w_in, D, W_IN_COLS)
    o_oa = step("out_a", got[1], w_out_a, m_w_out_a, v_w_out_a, D // N_SLOT, D)
    o_ob = step("out_b", got[2], w_out_b, m_w_out_b, v_w_out_b, D // N_SLOT, D)
    o_out = step("out", got[3], w_out, m_w_out, v_w_out, D // N_SLOT, D)
    o_up = step("up", got[4], w_up, m_w_up, v_w_up, D, FF_COLS)
    o_down = step("down", got[5], w_down, m_w_down, v_w_down, FF_COLS, D)
    gate_w = jnp.stack([w_rgate[0], w_igate[0]]).reshape(2 * HEADS * 32, HEAD_DIM)
    gate_m = jnp.stack([m_w_rgate[0], m_w_igate[0]]).reshape(2 * HEADS * 32, HEAD_DIM)
    gate_v = jnp.stack([v_w_rgate[0], v_w_igate[0]]).reshape(2 * HEADS * 32, HEAD_DIM)
    o_gate = _adamw("adamw_gate", got[6], gate_w, gate_m, gate_v)
    o_gate = [a.reshape(2, 1, HEADS, 32, HEAD_DIM) for a in o_gate]
    o_wr = [a[0] for a in o_gate]
    o_wi = [a[1] for a in o_gate]

    def own(full, width):
        return lax.dynamic_slice_in_dim(full, me * width, width, axis=1)

    small_g = {
        "norm_mix_g": last_sum[0:1], "conv_w": own(vecs_sum[0:4], 128), "conv_b": vecs_sum[4:5],
        "b_rgate": own(vecs_sum[5:6].reshape(HEADS, HEAD_DIM), 32),
        "b_igate": own(vecs_sum[6:7].reshape(HEADS, HEAD_DIM), 32),
        "lru_lambda": vecs_sum[7:8], "sgu_ln_g": vecs_sum[8:9], "sgu_ln_b": vecs_sum[9:10],
        "norm_mlp_g": vecs_sum[10:11], "norm_final_g": vecs_sum[11:12],
        "sgu_b_s": vecs_sum[12, 0:GROUPS * CHUNK].reshape(GROUPS, CHUNK),
    }
    small_w = {"norm_mix_g": (norm_mix_g, m_norm_mix_g, v_norm_mix_g), "conv_w": (conv_w, m_conv_w, v_conv_w),
               "conv_b": (conv_b, m_conv_b, v_conv_b), "b_rgate": (b_rgate, m_b_rgate, v_b_rgate),
               "b_igate": (b_igate, m_b_igate, v_b_igate), "lru_lambda": (lru_lambda, m_lru_lambda, v_lru_lambda),
               "sgu_ln_g": (sgu_ln_g, m_sgu_ln_g, v_sgu_ln_g), "sgu_ln_b": (sgu_ln_b, m_sgu_ln_b, v_sgu_ln_b),
               "norm_mlp_g": (norm_mlp_g, m_norm_mlp_g, v_norm_mlp_g),
               "norm_final_g": (norm_final_g, m_norm_final_g, v_norm_final_g),
               "sgu_b_s": (sgu_b_s, m_sgu_b_s, v_sgu_b_s), "sgu_w_s": (sgu_w_s, m_sgu_w_s, v_sgu_w_s)}
    order = list(small_g)
    as2d = lambda k, a: a.reshape(small_g[k].shape)
    upd = _adamw_many("adamw_small", [small_g[k] for k in order], *[[as2d(k, small_w[k][q]) for k in order]
                                                                     for q in range(3)])
    o_small = {k: [a.reshape(small_w[k][0].shape) for a in (small_g[k],) + u] for k, u in zip(order, upd)}
    ws3 = [a[0].reshape(GROUPS * CHUNK, CHUNK) for a in small_w.pop("sgu_w_s")]
    o_small["sgu_w_s"] = [a.reshape(sgu_w_s.shape) for a in _adamw("adamw_ws", got_ws, *ws3)]

    per_weight = {"norm_mix_g": o_small["norm_mix_g"], "w_in": o_in, "conv_w": o_small["conv_w"],
                  "conv_b": o_small["conv_b"], "w_rgate": o_wr, "b_rgate": o_small["b_rgate"], "w_igate": o_wi,
                  "b_igate": o_small["b_igate"], "lru_lambda": o_small["lru_lambda"], "w_out_a": o_oa,
                  "sgu_ln_g": o_small["sgu_ln_g"], "sgu_ln_b": o_small["sgu_ln_b"], "sgu_w_s": o_small["sgu_w_s"],
                  "sgu_b_s": o_small["sgu_b_s"], "w_out_b": o_ob, "w_out": o_out, "norm_mlp_g": o_small["norm_mlp_g"],
                  "w_up": o_up, "w_down": o_down, "norm_final_g": o_small["norm_final_g"]}
    names_w = list(per_weight)
    return (loss, dx[None], *[per_weight[k][0] for k in names_w], *[per_weight[k][1] for k in names_w],
            *[per_weight[k][2] for k in names_w], *[per_weight[k][3] for k in names_w])
```

```python
import jax
import jax.numpy as jnp
from jax import lax
from jax.experimental import pallas as pl
from jax.experimental.pallas import tpu as pltpu

F32 = jnp.float32
BF16 = jnp.bfloat16
SDS = jax.ShapeDtypeStruct
MESH = pl.DeviceIdType.MESH
ANY = pl.BlockSpec(memory_space=pltpu.HBM)

D = 1024
N_SLOT = 8
W_IN_COLS = 768
FF_COLS = 512
HEADS, HEAD_DIM = 4, 256
GROUPS, GROUP_DIM = 4, 256
CHUNK = 128
CONV_K = 4
NORM_EPS = 1e-6
LN_EPS = 1e-5
LRU_C = 8.0
ADAM_LR, ADAM_B1, ADAM_B2, ADAM_EPS, ADAM_WD, ADAM_STEP = 0.001, 0.9, 0.999, 1e-08, 0.01, 10

TM_ROWS = 1024
TM_MERGE = 512
T_BRANCH = 256
MiB = 1024 * 1024

_GELU_C = 0.7978845608028654
_GELU_A = 0.044715


def _cparams(sem, vmem_mib):
    return pltpu.CompilerParams(dimension_semantics=sem, vmem_limit_bytes=vmem_mib * MiB)


def _gelu(x):
    t = jnp.tanh(_GELU_C * (x + _GELU_A * x * x * x))
    return 0.5 * x * (1.0 + t)


def _gelu_and_grad(x):
    x2 = x * x
    t = jnp.tanh(_GELU_C * x * (1.0 + _GELU_A * x2))
    g = 0.5 * x * (1.0 + t)
    dg = 0.5 * (1.0 + t) + 0.5 * x * (1.0 - t * t) * _GELU_C * (1.0 + 3.0 * _GELU_A * x2)
    return g, dg


def _softplus(x):
    return jnp.maximum(x, 0.0) + jnp.log1p(jnp.exp(-jnp.abs(x)))


def _dot(a, b):
    return jnp.dot(a, b, preferred_element_type=F32)


def _dot_nt(a, b):
    return lax.dot_general(a, b, (((1,), (1,)), ((), ())), preferred_element_type=F32)


def _dot_tn(a, b):
    return lax.dot_general(a, b, (((0,), (0,)), ((), ())), preferred_element_type=F32)


def _rows_shifted(prev8, cur, k):
    ext = jnp.concatenate([prev8, cur], axis=0)
    return pltpu.roll(ext, k, 0)[8:]


def _rows_advanced(cur, next8, k):
    t = cur.shape[0]
    ext = jnp.concatenate([cur, next8], axis=0)
    return pltpu.roll(ext, t + 8 - k, 0)[:t]


def _slot_order(x, y, c):
    chip = 2 * x + y
    order = [2 * chip + c, 2 * chip + 1 - c]
    for _, _, pc in _other_chips(x, y):
        order += [2 * pc + c, 2 * pc + 1 - c]
    return jnp.stack(order).astype(jnp.int32)


def _in_proj(x, g_mix, w_in_own, order, comm=None):
    s = x.shape[0]
    tm = min(TM_ROWS, s)
    ni = s // tm

    def body(order_ref, x_ref, g_ref, own_ref, z_ref, nt_ref, wg_ref, n_s, w_s, send_sems, recv_sems, local_sems):
        j, i = pl.program_id(0), pl.program_id(1)
        px, py, c = _place()
        chip = 2 * px + py
        me = 2 * chip + c
        sib = (px, py, 1 - c)
        chips = _other_chips(px, py)

        def rc(k, src, blk, to):
            return pltpu.make_async_remote_copy(src_ref=src, dst_ref=w_s.at[blk], send_sem=send_sems.at[k],
                                                recv_sem=recv_sems.at[k], device_id=to, device_id_type=MESH)

        own_in = pltpu.make_async_copy(own_ref, w_s.at[me], local_sems.at[0])
        sends = [rc(0, own_ref, me, sib)] + [rc(1 + q, own_ref, me, (qx, qy, c)) for q, (qx, qy, _) in enumerate(chips)]
        passed = [rc(4 + q, w_s.at[2 * pc + c], 2 * pc + c, sib) for q, (_, _, pc) in enumerate(chips)]
        keep = pltpu.make_async_copy(w_s, wg_ref, local_sems.at[1])

        @pl.when((i == 0) & (j == 0))
        def _():
            own_in.start()
            for cp in sends:
                cp.start()
            own_in.wait()

        @pl.when((i == 0) & (j == 1))
        def _():
            rc(0, own_ref, 2 * chip + 1 - c, sib).wait_recv()

        for q, (_, _, pc) in enumerate(chips):
            @pl.when((i == 0) & (j == 2 + 2 * q))
            def _():
                rc(1 + q, own_ref, 2 * pc + c, sib).wait_recv()
                passed[q].start()

            @pl.when((i == 0) & (j == 3 + 2 * q))
            def _():
                rc(4 + q, own_ref, 2 * pc + 1 - c, sib).wait_recv()

        rows = pl.ds(pl.multiple_of(i * tm, tm), tm)

        @pl.when(j == 0)
        def _():
            xv = x_ref[...]
            rstd = lax.rsqrt(jnp.mean(xv * xv, axis=-1, keepdims=True) + NORM_EPS)
            nb = (xv * rstd * g_ref[...]).astype(BF16)
            n_s[rows, :] = nb
            nt_ref[...] = nb.T

        z_ref[...] = _dot(n_s[rows, :], w_s[order_ref[j]]).astype(BF16)

        @pl.when((i == 0) & (j == N_SLOT - 1))
        def _():
            keep.start()

        @pl.when((i == ni - 1) & (j == N_SLOT - 1))
        def _():
            for cp in sends + passed:
                cp.wait_send()
            keep.wait()

    first_pass = lambda j, i, o: (jnp.where(j == 0, i, ni - 1), 0)
    (z, n1, w_in_g), extra = _call(
        body, name="in_proj", grid=(N_SLOT, ni), prefetch=(order,),
        in_specs=[pl.BlockSpec((tm, D), first_pass),
                  pl.BlockSpec((1, D), lambda j, i, o: (0, 0)), ANY],
        out_specs=[pl.BlockSpec((tm, W_IN_COLS), lambda j, i, o: (i, o[j])),
                   pl.BlockSpec((D, tm), lambda j, i, o: (0, jnp.where(j == 0, i, ni - 1))), ANY],
        out_shape=[SDS((s, N_SLOT * W_IN_COLS), BF16), SDS((D, s), BF16), SDS((N_SLOT, D, W_IN_COLS), BF16)],
        scratch_shapes=[pltpu.VMEM((s, D), BF16), pltpu.VMEM((N_SLOT, D, W_IN_COLS), BF16),
                        pltpu.SemaphoreType.DMA((7,)), pltpu.SemaphoreType.DMA((7,)), pltpu.SemaphoreType.DMA((2,))],
        params=_cparams(("arbitrary", "arbitrary"), 56), args=(x, g_mix, w_in_own), comm=comm)
    return (z, n1, w_in_g), extra


def _lru_gates(xc, xcb, wr_ref, br, wi_ref, bi, sp_lam, a_s, b_s, r_s=None, i_s=None, m_s=None):
    for h in range(HEADS):
        sl = slice(h * HEAD_DIM, (h + 1) * HEAD_DIM)
        r = jax.nn.sigmoid(_dot(xcb[:, sl], wr_ref[h]) + br[:, sl])
        ig = jax.nn.sigmoid(_dot(xcb[:, sl], wi_ref[h]) + bi[:, sl])
        log_a = (-LRU_C) * r * sp_lam[:, sl]
        a = jnp.exp(log_a)
        mult = jnp.sqrt(-jnp.tanh(log_a) * (a * a + 1.0))
        a_s[:, sl] = a
        b_s[:, sl] = xc[:, sl] * ig * mult
        if r_s is not None:
            r_s[:, sl] = r
            i_s[:, sl] = ig
            m_s[:, sl] = mult


def _conv_fwd(xa, prev8, cw, cb):
    xc = cb + cw[0:1, :] * xa
    for k in range(1, CONV_K):
        xc = xc + cw[k:k + 1, :] * _rows_shifted(prev8, xa, k)
    return xc


def _branch_a_fwd(z, conv_w, conv_b, w_r, b_r, w_i, b_i, lam, comm=None):
    s = z.shape[0]
    ta = min(T_BRANCH, s)
    per16 = ta // 16

    def body(xa_ref, xp_ref, ga_ref, cw_ref, cb_ref, wr_ref, br_ref, wi_ref, bi_ref, lam_ref,
             ya_ref, hs_ref, a_s, b_s, h_s, carry_s):
        i = pl.program_id(0)

        @pl.when(i == 0)
        def _():
            carry_s[...] = jnp.zeros_like(carry_s)

        xa = xa_ref[...].astype(F32)
        prev8 = jnp.where(i > 0, xp_ref[...].astype(F32)[8:16], 0.0)
        xc = _conv_fwd(xa, prev8, cw_ref[...], cb_ref[...])
        sp_lam = _softplus(-lam_ref[...])
        _lru_gates(xc, xc.astype(BF16), wr_ref, br_ref[...], wi_ref, bi_ref[...], sp_lam, a_s, b_s)

        row = lax.broadcasted_iota(jnp.int32, (8, D), 0)

        def group(g, carry):
            off = pl.multiple_of(g * 8, 8)
            a8 = a_s[pl.ds(off, 8), :]
            b8 = b_s[pl.ds(off, 8), :]
            for d in (1, 2, 4):
                a_sh = jnp.where(row >= d, pltpu.roll(a8, d, 0), 1.0)
                b_sh = jnp.where(row >= d, pltpu.roll(b8, d, 0), 0.0)
                b8 = a8 * b_sh + b8
                a8 = a8 * a_sh
            h8 = b8 + a8 * carry
            h_s[pl.ds(off, 8), :] = h8
            return jnp.broadcast_to(h8[7:8, :], (8, D))

        carry_s[...] = lax.fori_loop(0, ta // 8, group, carry_s[...])
        hs = h_s[...]
        hs_ref[...] = hs.astype(BF16)
        ya_ref[...] = (hs * _gelu(ga_ref[...].astype(F32))).astype(BF16)

    vec = pl.BlockSpec((1, D), lambda i: (0, 0))
    gate = pl.BlockSpec((HEADS, HEAD_DIM, HEAD_DIM), lambda i: (0, 0, 0))
    return _call(
        body, name="branch_a_fwd", grid=(s // ta,),
        in_specs=[pl.BlockSpec((ta, D), lambda i: (i, 0)),
                  pl.BlockSpec((16, D), lambda i: (jnp.maximum(i * per16 - 1, 0), 0)),
                  pl.BlockSpec((ta, D), lambda i: (i, 1)),
                  pl.BlockSpec((CONV_K, D), lambda i: (0, 0)), vec, gate, vec, gate, vec, vec],
        out_specs=[pl.BlockSpec((ta, D), lambda i: (i, 0)), pl.BlockSpec((ta, D), lambda i: (i, 0))],
        out_shape=[SDS((s, D), BF16), SDS((s, D), BF16)],
        scratch_shapes=[pltpu.VMEM((ta, D), F32), pltpu.VMEM((ta, D), F32), pltpu.VMEM((ta, D), F32),
                        pltpu.VMEM((8, D), F32)],
        params=_cparams(("arbitrary",), 40), args=(z, z, z, conv_w, conv_b, w_r, b_r, w_i, b_i, lam), comm=comm)


def _sgu_common(ub, vb, lg, lb, with_grad):
    if with_grad:
        u, du = _gelu_and_grad(ub)
        v, dv = _gelu_and_grad(vb)
    else:
        u, v, du, dv = _gelu(ub), _gelu(vb), None, None
    mu = jnp.mean(v, axis=-1, keepdims=True)
    vc = v - mu
    rstd = lax.rsqrt(jnp.mean(vc * vc, axis=-1, keepdims=True) + LN_EPS)
    vhat = vc * rstd
    vln = vhat * lg + lb
    return u, du, dv, rstd, vhat, vln


def _masked_ws(ws_ref):
    t = lax.broadcasted_iota(jnp.int32, (CHUNK, CHUNK), 0)
    c = lax.broadcasted_iota(jnp.int32, (CHUNK, CHUNK), 1)
    keep = c <= t
    return [jnp.where(keep, ws_ref[g], 0.0).astype(BF16) for g in range(GROUPS)]


def _branch_b_fwd(z, ln_g, ln_b, w_s, b_s_t, comm=None):
    s = z.shape[0]
    tb = min(T_BRANCH, s)

    def body(ub_ref, vb_ref, lg_ref, lb_ref, ws_ref, bs_ref, yb_ref):
        u, _, _, _, _, vln = _sgu_common(ub_ref[...].astype(F32), vb_ref[...].astype(F32),
                                         lg_ref[...], lb_ref[...], False)
        vlnb = vln.astype(BF16)
        wm = _masked_ws(ws_ref)
        bs = bs_ref[...]
        for c in range(tb // CHUNK):
            rs = slice(c * CHUNK, (c + 1) * CHUNK)
            for g in range(GROUPS):
                cs = slice(g * GROUP_DIM, (g + 1) * GROUP_DIM)
                sp = _dot(wm[g], vlnb[rs, cs]) + bs[:, g:g + 1]
                yb_ref[rs, cs] = (u[rs, cs] * sp).astype(BF16)

    vec = pl.BlockSpec((1, D), lambda i: (0, 0))
    return _call(
        body, name="branch_b_fwd", grid=(s // tb,),
        in_specs=[pl.BlockSpec((tb, D), lambda i: (i, 2)), pl.BlockSpec((tb, D), lambda i: (i, 3)), vec, vec,
                  pl.BlockSpec((GROUPS, CHUNK, CHUNK), lambda i: (0, 0, 0)),
                  pl.BlockSpec((CHUNK, GROUPS), lambda i: (0, 0))],
        out_specs=[pl.BlockSpec((tb, D), lambda i: (i, 0))],
        out_shape=[SDS((s, D), BF16)], scratch_shapes=[],
        params=_cparams(("arbitrary",), 40), args=(z, z, ln_g, ln_b, w_s, b_s_t), comm=comm)


def _merge_out(ya, yb, z, x, w_oa, w_ob, w_out, comm=None):
    s = x.shape[0]
    tm = min(TM_MERGE, s)

    def body(ya_ref, yb_ref, ma_ref, mb_ref, x_ref, woa_ref, wob_ref, wo_ref, pa_ref, pb_ref, mg_ref, h1_ref):
        pa = _dot(ya_ref[...], woa_ref[...])
        pb = _dot(yb_ref[...], wob_ref[...])
        merged = (jax.nn.sigmoid(ma_ref[...].astype(F32)) * pa
                  + jax.nn.sigmoid(mb_ref[...].astype(F32)) * pb).astype(BF16)
        pa_ref[...] = pa.astype(BF16)
        pb_ref[...] = pb.astype(BF16)
        mg_ref[...] = merged
        h1_ref[...] = x_ref[...] + _dot(merged, wo_ref[...])

    row = pl.BlockSpec((tm, D), lambda i: (i, 0))
    wsp = pl.BlockSpec((D, D), lambda i: (0, 0))
    return _call(
        body, name="merge_out", grid=(s // tm,),
        in_specs=[row, row, pl.BlockSpec((tm, D), lambda i: (i, 4)), pl.BlockSpec((tm, D), lambda i: (i, 5)),
                  row, wsp, wsp, wsp],
        out_specs=[row, row, row, row],
        out_shape=[SDS((s, D), BF16), SDS((s, D), BF16), SDS((s, D), BF16), SDS((s, D), F32)], scratch_shapes=[],
        params=_cparams(("arbitrary",), 48), args=(ya, yb, z, z, x, w_oa, w_ob, w_out), comm=comm)


def _mlp_fwd(h1, g_mlp, w_up_g, w_down, g_fin, tgt):
    s = h1.shape[0]
    tm = min(TM_ROWS, s)
    nj = N_SLOT

    def body(h1_ref, gm_ref, wu_ref, wd_ref, gf_ref, t_ref, r_ref, at_ref, n2t_ref, dh2_ref, loss_ref, dgf_ref,
             n2_s, acc_s):
        i, j = pl.program_id(0), pl.program_id(1)

        @pl.when(j == 0)
        def _():
            hv = h1_ref[...]
            rstd = lax.rsqrt(jnp.mean(hv * hv, axis=-1, keepdims=True) + NORM_EPS)
            nb = (hv * rstd * gm_ref[...]).astype(BF16)
            n2_s[...] = nb
            n2t_ref[...] = nb.T
            acc_s[...] = jnp.zeros_like(acc_s)

        @pl.when((i == 0) & (j == 0))
        def _():
            loss_ref[...] = jnp.zeros_like(loss_ref)
            dgf_ref[...] = jnp.zeros_like(dgf_ref)

        r = jnp.maximum(_dot(n2_s[...], wu_ref[...]), 0.0)
        r_ref[...] = r.astype(BF16)
        act = (r * r).astype(BF16)
        at_ref[...] = act.T
        acc_s[...] += _dot(act, wd_ref[...])

        @pl.when(j == nj - 1)
        def _():
            h2 = h1_ref[...] + acc_s[...]
            rstd = lax.rsqrt(jnp.mean(h2 * h2, axis=-1, keepdims=True) + NORM_EPS)
            hh = h2 * rstd
            gf = gf_ref[...]
            e = hh * gf - t_ref[...]
            loss_ref[...] += jnp.sum(e * e) * (0.5 / D)
            dy = e * (1.0 / D)
            dgf_ref[...] += jnp.sum(dy * hh, axis=0, keepdims=True)
            dhh = dy * gf
            dh2_ref[...] = rstd * (dhh - hh * jnp.mean(dhh * hh, axis=-1, keepdims=True))

    row = pl.BlockSpec((tm, D), lambda i, j: (i, 0))
    vec = pl.BlockSpec((1, D), lambda i, j: (0, 0))
    return pl.pallas_call(
        body, name="mlp_fwd", grid=(s // tm, nj),
        in_specs=[row, vec, pl.BlockSpec((None, D, FF_COLS), lambda i, j: (j, 0, 0)),
                  pl.BlockSpec((FF_COLS, D), lambda i, j: (j, 0)), vec, row],
        out_specs=[pl.BlockSpec((tm, FF_COLS), lambda i, j: (i, j)), pl.BlockSpec((FF_COLS, tm), lambda i, j: (j, i)),
                   pl.BlockSpec((D, tm), lambda i, j: (0, i)), row, pl.BlockSpec((8, 128), lambda i, j: (0, 0)), vec],
        out_shape=[SDS((s, nj * FF_COLS), BF16), SDS((nj * FF_COLS, s), BF16), SDS((D, s), BF16), SDS((s, D), F32),
                   SDS((8, 128), F32), SDS((1, D), F32)],
        scratch_shapes=[pltpu.VMEM((tm, D), BF16), pltpu.VMEM((tm, D), F32)],
        compiler_params=_cparams(("arbitrary", "arbitrary"), 52),
    )(h1, g_mlp, w_up_g, w_down, g_fin, tgt)


def _mlp_bwd(dh2, r, w_down, w_up_g, h1, g_mlp, comm=None):
    s = h1.shape[0]
    tm = min(TM_ROWS, s)
    nj = N_SLOT

    def body(dh2_ref, r_ref, wd_ref, wu_ref, h1_ref, gm_ref, df_ref, dh1_ref, dgm_ref, dh2b_s, acc_s):
        i, j = pl.program_id(0), pl.program_id(1)

        @pl.when(j == 0)
        def _():
            dh2b_s[...] = dh2_ref[...].astype(BF16)
            acc_s[...] = jnp.zeros_like(acc_s)

        @pl.when((i == 0) & (j == 0))
        def _():
            dgm_ref[...] = jnp.zeros_like(dgm_ref)

        d_act = _dot_nt(dh2b_s[...], wd_ref[...])
        df = (d_act * (2.0 * r_ref[...].astype(F32))).astype(BF16)
        df_ref[...] = df
        acc_s[...] += _dot_nt(df, wu_ref[...])

        @pl.when(j == nj - 1)
        def _():
            hv = h1_ref[...]
            rstd = lax.rsqrt(jnp.mean(hv * hv, axis=-1, keepdims=True) + NORM_EPS)
            hh = hv * rstd
            dn2 = acc_s[...]
            dgm_ref[...] += jnp.sum(dn2 * hh, axis=0, keepdims=True)
            dhat = dn2 * gm_ref[...]
            dh1_ref[...] = dh2_ref[...] + rstd * (dhat - hh * jnp.mean(dhat * hh, axis=-1, keepdims=True))

    row = pl.BlockSpec((tm, D), lambda i, j: (i, 0))
    vec = pl.BlockSpec((1, D), lambda i, j: (0, 0))
    ffb = pl.BlockSpec((tm, FF_COLS), lambda i, j: (i, j))
    return _call(
        body, name="mlp_bwd", grid=(s // tm, nj),
        in_specs=[row, ffb, pl.BlockSpec((FF_COLS, D), lambda i, j: (j, 0)),
                  pl.BlockSpec((None, D, FF_COLS), lambda i, j: (j, 0, 0)), row, vec],
        out_specs=[ffb, row, vec],
        out_shape=[SDS((s, nj * FF_COLS), BF16), SDS((s, D), F32), SDS((1, D), F32)],
        scratch_shapes=[pltpu.VMEM((tm, D), BF16), pltpu.VMEM((tm, D), F32)],
        params=_cparams(("arbitrary", "arbitrary"), 52), args=(dh2, r, w_down, w_up_g, h1, g_mlp), comm=comm)


def _merge_bwd(dh1, z, pa, pb, w_out, w_oa, w_ob, comm=None):
    s = dh1.shape[0]
    tm = min(TM_MERGE, s)

    def body(dh1_ref, ma_ref, mb_ref, pa_ref, pb_ref, wo_ref, woa_ref, wob_ref,
             dz_ref, dpa_ref, dpb_ref, dya_ref, dyb_ref):
        dm = _dot_nt(dh1_ref[...].astype(BF16), wo_ref[...])
        sa = jax.nn.sigmoid(ma_ref[...].astype(F32))
        sb = jax.nn.sigmoid(mb_ref[...].astype(F32))
        dpa = (dm * sa).astype(BF16)
        dpb = (dm * sb).astype(BF16)
        dz_ref[:, 0:D] = (dm * pa_ref[...].astype(F32) * sa * (1.0 - sa)).astype(BF16)
        dz_ref[:, D:2 * D] = (dm * pb_ref[...].astype(F32) * sb * (1.0 - sb)).astype(BF16)
        dpa_ref[...] = dpa
        dpb_ref[...] = dpb
        dya_ref[...] = _dot_nt(dpa, woa_ref[...]).astype(BF16)
        dyb_ref[...] = _dot_nt(dpb, wob_ref[...]).astype(BF16)

    row = pl.BlockSpec((tm, D), lambda i: (i, 0))
    wsp = pl.BlockSpec((D, D), lambda i: (0, 0))
    return _call(
        body, name="merge_bwd", grid=(s // tm,),
        in_specs=[row, pl.BlockSpec((tm, D), lambda i: (i, 4)), pl.BlockSpec((tm, D), lambda i: (i, 5)),
                  row, row, wsp, wsp, wsp],
        out_specs=[pl.BlockSpec((tm, 2 * D), lambda i: (i, 2)), row, row, row, row],
        out_shape=[SDS((s, 6 * D), BF16)] + [SDS((s, D), BF16)] * 4, scratch_shapes=[],
        params=_cparams(("arbitrary",), 48), args=(dh1, z, z, pa, pb, w_out, w_oa, w_ob), comm=comm)


def _branch_b_bwd(dz, dyb, z, ln_g, ln_b, w_s, b_s_t, comm=None):
    s = z.shape[0]
    tb = min(T_BRANCH, s)

    def body(dz_in, dyb_ref, ub_ref, vb_ref, lg_ref, lb_ref, ws_ref, bs_ref,
             dz_ref, dws_ref, dbs_ref, dln_ref, du_s, dvln_s):
        del dz_in

        @pl.when(pl.program_id(0) == 0)
        def _():
            dws_ref[...] = jnp.zeros_like(dws_ref)
            dbs_ref[...] = jnp.zeros_like(dbs_ref)
            dln_ref[...] = jnp.zeros_like(dln_ref)

        lg = lg_ref[...]
        u, du, dv, rstd, vhat, vln = _sgu_common(ub_ref[...].astype(F32), vb_ref[...].astype(F32),
                                                 lg, lb_ref[...], True)
        vlnb = vln.astype(BF16)
        dyb_v = dyb_ref[...].astype(F32)
        wm = _masked_ws(ws_ref)
        keep = (lax.broadcasted_iota(jnp.int32, (CHUNK, CHUNK), 1)
                <= lax.broadcasted_iota(jnp.int32, (CHUNK, CHUNK), 0))
        bs = bs_ref[...]
        for c in range(tb // CHUNK):
            rs = slice(c * CHUNK, (c + 1) * CHUNK)
            for g in range(GROUPS):
                cs = slice(g * GROUP_DIM, (g + 1) * GROUP_DIM)
                v_blk = vlnb[rs, cs]
                sp = _dot(wm[g], v_blk) + bs[:, g:g + 1]
                d_sp = dyb_v[rs, cs] * u[rs, cs]
                d_spb = d_sp.astype(BF16)
                du_s[rs, cs] = dyb_v[rs, cs] * sp
                dvln_s[rs, cs] = _dot_tn(wm[g], d_spb)
                dws_ref[g] += jnp.where(keep, _dot_nt(d_spb, v_blk), 0.0)
                dbs_ref[g] += jnp.broadcast_to(jnp.sum(d_sp, axis=-1, keepdims=True), (CHUNK, CHUNK))
        dvln = dvln_s[...]
        dln_ref[0:1, :] += jnp.sum(dvln * vhat, axis=0, keepdims=True)
        dln_ref[1:2, :] += jnp.sum(dvln, axis=0, keepdims=True)
        dvh = dvln * lg
        d_v = rstd * (dvh - jnp.mean(dvh, axis=-1, keepdims=True)
                      - vhat * jnp.mean(dvh * vhat, axis=-1, keepdims=True))
        dz_ref[:, 0:D] = (du_s[...] * du).astype(BF16)
        dz_ref[:, D:2 * D] = (d_v * dv).astype(BF16)

    vec = pl.BlockSpec((1, D), lambda i: (0, 0))
    sq = pl.BlockSpec((GROUPS, CHUNK, CHUNK), lambda i: (0, 0, 0))
    return _call(
        body, name="branch_b_bwd", grid=(s // tb,),
        in_specs=[ANY, pl.BlockSpec((tb, D), lambda i: (i, 0)),
                  pl.BlockSpec((tb, D), lambda i: (i, 2)), pl.BlockSpec((tb, D), lambda i: (i, 3)), vec, vec, sq,
                  pl.BlockSpec((CHUNK, GROUPS), lambda i: (0, 0))],
        out_specs=[pl.BlockSpec((tb, 2 * D), lambda i: (i, 1)), sq, sq, pl.BlockSpec((8, D), lambda i: (0, 0))],
        out_shape=[SDS(dz.shape, BF16), SDS((GROUPS, CHUNK, CHUNK), F32), SDS((GROUPS, CHUNK, CHUNK), F32),
                   SDS((8, D), F32)],
        scratch_shapes=[pltpu.VMEM((tb, D), F32), pltpu.VMEM((tb, D), F32)], aliases={0: 0},
        params=_cparams(("arbitrary",), 40), args=(dz, dyb, z, z, ln_g, ln_b, w_s, b_s_t), comm=comm)


def _branch_a_bwd(dz, dya, z, hs, conv_w, conv_b, w_r, b_r, w_i, b_i, lam, comm=None):
    s = z.shape[0]
    ta = min(T_BRANCH, s)
    nb = s // ta
    per16 = ta // 16

    def body(dz_in, dya_ref, xa_ref, xp_ref, ga_ref, hs_ref, hp_ref, cw_ref, cb_ref, wr_ref, br_ref, wi_ref,
             bi_ref, lam_ref, dz_ref, vec_ref, dwr_ref, dwi_ref,
             a_s, b_s, h_s, r_s, i_s, m_s, dcar_s, acar_s, dxc_s):
        del dz_in
        i = pl.program_id(0)
        blk = nb - 1 - i

        @pl.when(i == 0)
        def _():
            dcar_s[...] = jnp.zeros_like(dcar_s)
            acar_s[...] = jnp.zeros_like(acar_s)
            dxc_s[...] = jnp.zeros_like(dxc_s)
            vec_ref[...] = jnp.zeros_like(vec_ref)
            dwr_ref[...] = jnp.zeros_like(dwr_ref)
            dwi_ref[...] = jnp.zeros_like(dwi_ref)

        cw = cw_ref[...]
        lam_v = lam_ref[...]
        xa = xa_ref[...].astype(F32)
        prev8 = jnp.where(blk > 0, xp_ref[...].astype(F32)[8:16], 0.0)
        xc = _conv_fwd(xa, prev8, cw, cb_ref[...])
        xcb = xc.astype(BF16)
        sp_lam = _softplus(-lam_v)
        _lru_gates(xc, xcb, wr_ref, br_ref[...], wi_ref, bi_ref[...], sp_lam, a_s, b_s, r_s, i_s, m_s)

        hs_v = hs_ref[...].astype(F32)
        hprev8 = jnp.where(blk > 0, hp_ref[...].astype(F32)[8:16], 0.0)
        h_m1 = _rows_shifted(hprev8, hs_v, 1)
        gg, dgg = _gelu_and_grad(ga_ref[...].astype(F32))
        dya_v = dya_ref[...].astype(F32)
        dz_ref[:, D:2 * D] = (dya_v * hs_v * dgg).astype(BF16)

        a_v = a_s[...]
        a_s[...] = _rows_advanced(a_v, acar_s[...], 1)
        b_s[...] = dya_v * gg

        row = lax.broadcasted_iota(jnp.int32, (8, D), 0)
        ng = ta // 8

        def group(gi, carry):
            off = pl.multiple_of((ng - 1 - gi) * 8, 8)
            c8 = a_s[pl.ds(off, 8), :]
            d8 = b_s[pl.ds(off, 8), :]
            for d in (1, 2, 4):
                c_sh = jnp.where(row < 8 - d, pltpu.roll(c8, 8 - d, 0), 1.0)
                d_sh = jnp.where(row < 8 - d, pltpu.roll(d8, 8 - d, 0), 0.0)
                d8 = c8 * d_sh + d8
                c8 = c8 * c_sh
            dh8 = d8 + c8 * carry
            h_s[pl.ds(off, 8), :] = dh8
            return jnp.broadcast_to(dh8[0:1, :], (8, D))

        dcar_s[...] = lax.fori_loop(0, ng, group, dcar_s[...])
        acar_s[...] = jnp.broadcast_to(a_v[0:1, :], (8, D))

        dbx = h_s[...]
        r_v, i_v, m_v = r_s[...], i_s[...], m_s[...]
        d_mult = dbx * xc * i_v
        d_loga = dbx * h_m1 * a_v - d_mult * (a_v * a_v) / m_v
        d_pr = d_loga * ((-LRU_C) * sp_lam) * r_v * (1.0 - r_v)
        d_pi = dbx * xc * m_v * i_v * (1.0 - i_v)
        vec_ref[7:8, :] += jnp.sum(d_loga * r_v, axis=0, keepdims=True) * (LRU_C * jax.nn.sigmoid(-lam_v))
        vec_ref[5:6, :] += jnp.sum(d_pr, axis=0, keepdims=True)
        vec_ref[6:7, :] += jnp.sum(d_pi, axis=0, keepdims=True)
        d_prb = d_pr.astype(BF16)
        d_pib = d_pi.astype(BF16)
        h_s[...] = dbx * i_v * m_v
        for h in range(HEADS):
            sl = slice(h * HEAD_DIM, (h + 1) * HEAD_DIM)
            h_s[:, sl] += _dot_nt(d_prb[:, sl], wr_ref[h]) + _dot_nt(d_pib[:, sl], wi_ref[h])
            dwr_ref[h] += _dot_tn(xcb[:, sl], d_prb[:, sl])
            dwi_ref[h] += _dot_tn(xcb[:, sl], d_pib[:, sl])
        d_xc = h_s[...]
        vec_ref[4:5, :] += jnp.sum(d_xc, axis=0, keepdims=True)
        vec_ref[0:1, :] += jnp.sum(d_xc * xa, axis=0, keepdims=True)
        d_xa = cw[0:1, :] * d_xc
        nxt = dxc_s[...]
        for k in range(1, CONV_K):
            vec_ref[k:k + 1, :] += jnp.sum(d_xc * _rows_shifted(prev8, xa, k), axis=0, keepdims=True)
            d_xa = d_xa + cw[k:k + 1, :] * _rows_advanced(d_xc, nxt, k)
        dz_ref[:, 0:D] = d_xa.astype(BF16)
        dxc_s[...] = d_xc[0:8, :]

    vec = pl.BlockSpec((1, D), lambda i: (0, 0))
    gate = pl.BlockSpec((HEADS, HEAD_DIM, HEAD_DIM), lambda i: (0, 0, 0))
    cur = lambda c: pl.BlockSpec((ta, D), lambda i: (nb - 1 - i, c))
    before = lambda c: pl.BlockSpec((16, D), lambda i: (jnp.maximum((nb - 1 - i) * per16 - 1, 0), c))
    return _call(
        body, name="branch_a_bwd", grid=(nb,),
        in_specs=[ANY, cur(0), cur(0), before(0), cur(1), cur(0), before(0),
                  pl.BlockSpec((CONV_K, D), lambda i: (0, 0)), vec, gate, vec, gate, vec, vec],
        out_specs=[pl.BlockSpec((ta, 2 * D), lambda i: (nb - 1 - i, 0)), pl.BlockSpec((8, D), lambda i: (0, 0)),
                   gate, gate],
        out_shape=[SDS(dz.shape, BF16), SDS((8, D), F32), SDS((HEADS, HEAD_DIM, HEAD_DIM), F32),
                   SDS((HEADS, HEAD_DIM, HEAD_DIM), F32)],
        scratch_shapes=[pltpu.VMEM((ta, D), F32)] * 6 + [pltpu.VMEM((8, D), F32)] * 3, aliases={0: 0},
        params=_cparams(("arbitrary",), 48),
        args=(dz, dya, z, z, z, hs, hs, conv_w, conv_b, w_r, b_r, w_i, b_i, lam), comm=comm)


def _in_bwd(dz, w_in_g, x, dh1, g_mix, comm=None):
    s = x.shape[0]
    tm = min(TM_ROWS, s)
    nj = N_SLOT

    def body(dz_ref, w_ref, x_ref, dh1_ref, g_ref, dx_ref, dg_ref, acc_s):
        i, j = pl.program_id(0), pl.program_id(1)

        @pl.when(j == 0)
        def _():
            acc_s[...] = jnp.zeros_like(acc_s)

        @pl.when((i == 0) & (j == 0))
        def _():
            dg_ref[...] = jnp.zeros_like(dg_ref)

        acc_s[...] += _dot_nt(dz_ref[...], w_ref[...])

        @pl.when(j == nj - 1)
        def _():
            xv = x_ref[...]
            rstd = lax.rsqrt(jnp.mean(xv * xv, axis=-1, keepdims=True) + NORM_EPS)
            xh = xv * rstd
            dn = acc_s[...]
            dg_ref[...] += jnp.sum(dn * xh, axis=0, keepdims=True)
            dhat = dn * g_ref[...]
            dx_ref[...] = dh1_ref[...] + rstd * (dhat - xh * jnp.mean(dhat * xh, axis=-1, keepdims=True))

    row = pl.BlockSpec((tm, D), lambda i, j: (i, 0))
    vec = pl.BlockSpec((1, D), lambda i, j: (0, 0))
    return _call(
        body, name="in_bwd", grid=(s // tm, nj),
        in_specs=[pl.BlockSpec((tm, W_IN_COLS), lambda i, j: (i, j)),
                  pl.BlockSpec((None, D, W_IN_COLS), lambda i, j: (j, 0, 0)), row, row, vec],
        out_specs=[row, vec],
        out_shape=[SDS((s, D), F32), SDS((1, D), F32)],
        scratch_shapes=[pltpu.VMEM((tm, D), F32)],
        params=_cparams(("arbitrary", "arbitrary"), 48), args=(dz, w_in_g, x, dh1, g_mix), comm=comm)


def _wgrad(name, a, b, nblk, a_split, b_split):
    s = a.shape[0]
    ts = min(TM_ROWS, s)
    a_w = a.shape[1] // nblk if a_split else a.shape[1]
    b_w = b.shape[1] // nblk if b_split else b.shape[1]

    def body(a_ref, b_ref, o_ref, acc_s):
        t = pl.program_id(1)

        @pl.when(t == 0)
        def _():
            acc_s[...] = jnp.zeros_like(acc_s)

        acc_s[...] += _dot_tn(a_ref[...].astype(BF16), b_ref[...].astype(BF16))

        @pl.when(t == pl.num_programs(1) - 1)
        def _():
            o_ref[...] = acc_s[...].astype(BF16)

    return pl.pallas_call(
        body, name=name, grid=(nblk, s // ts),
        in_specs=[pl.BlockSpec((ts, a_w), (lambda k, t: (t, k)) if a_split else (lambda k, t: (t, 0))),
                  pl.BlockSpec((ts, b_w), (lambda k, t: (t, k)) if b_split else (lambda k, t: (t, 0)))],
        out_specs=pl.BlockSpec((None, a_w, b_w), lambda k, t: (k, 0, 0)),
        out_shape=SDS((nblk, a_w, b_w), BF16),
        scratch_shapes=[pltpu.VMEM((a_w, b_w), F32)],
        compiler_params=_cparams(("arbitrary", "arbitrary"), 48),
    )(a, b)


def _wgrad_t(name, a_t, b, nblk, a_split, b_split):
    s = b.shape[0]
    ts = min(TM_ROWS, s)
    a_w = a_t.shape[0] // nblk if a_split else a_t.shape[0]
    b_w = b.shape[1] // nblk if b_split else b.shape[1]

    def body(a_ref, b_ref, o_ref, acc_s):
        t = pl.program_id(1)

        @pl.when(t == 0)
        def _():
            acc_s[...] = jnp.zeros_like(acc_s)

        acc_s[...] += _dot(a_ref[...], b_ref[...].astype(BF16))

        @pl.when(t == pl.num_programs(1) - 1)
        def _():
            o_ref[...] = acc_s[...].astype(BF16)

    return pl.pallas_call(
        body, name=name, grid=(nblk, s // ts),
        in_specs=[pl.BlockSpec((a_w, ts), (lambda k, t: (k, t)) if a_split else (lambda k, t: (0, t))),
                  pl.BlockSpec((ts, b_w), (lambda k, t: (t, k)) if b_split else (lambda k, t: (t, 0)))],
        out_specs=pl.BlockSpec((None, a_w, b_w), lambda k, t: (k, 0, 0)),
        out_shape=SDS((nblk, a_w, b_w), BF16),
        scratch_shapes=[pltpu.VMEM((a_w, b_w), F32)],
        compiler_params=_cparams(("arbitrary", "arbitrary"), 48),
    )(a_t, b)


def _place():
    x, y, c = lax.axis_index("x"), lax.axis_index("y"), lax.axis_index("c")
    return x, y, c


def _other_chips(x, y):
    return [(x, 1 - y, 2 * x + 1 - y), (1 - x, y, 2 * (1 - x) + y), (1 - x, 1 - y, 2 * (1 - x) + 1 - y)]


class _Plan:
    def __init__(self, arrays, out_shape, sems, start, finish):
        self.arrays, self.out_shape, self.sems, self.start, self.finish = arrays, out_shape, sems, start, finish


def _gather_plan(shards):
    n = len(shards)

    def copies(ins, outs, sems):
        send_sems, recv_sems, local_sems = sems
        x, y, c = _place()
        chip = 2 * x + y
        me = 2 * chip + c
        sib = (x, y, 1 - c)
        chips = _other_chips(x, y)

        def rc(k, t, src, blk, to):
            return pltpu.make_async_remote_copy(
                src_ref=src, dst_ref=outs[t].at[blk], send_sem=send_sems.at[k * n + t],
                recv_sem=recv_sems.at[k * n + t], device_id=to, device_id_type=MESH)

        local = [pltpu.make_async_copy(ins[t], outs[t].at[me], local_sems.at[t]) for t in range(n)]
        sends = [rc(0, t, ins[t], me, sib) for t in range(n)]
        for j, (px, py, _) in enumerate(chips):
            sends += [rc(1 + j, t, ins[t], me, (px, py, c)) for t in range(n)]
        return rc, local, sends, chips, chip, c, sib

    def start(ins, outs, sems):
        _, local, sends, _, _, _, _ = copies(ins, outs, sems)
        for cp in local + sends:
            cp.start()

    def finish(ins, outs, sems):
        rc, local, sends, chips, chip, c, sib = copies(ins, outs, sems)
        passed = []
        for j, (px, py, pc) in enumerate(chips):
            blk = 2 * pc + c
            for t in range(n):
                rc(1 + j, t, ins[t], blk, sib).wait_recv()
            for t in range(n):
                cp = rc(4 + j, t, outs[t].at[blk], blk, sib)
                cp.start()
                passed.append(cp)
        for t in range(n):
            rc(0, t, ins[t], 2 * chip + 1 - c, sib).wait_recv()
        for j, (px, py, pc) in enumerate(chips):
            for t in range(n):
                rc(4 + j, t, ins[t], 2 * pc + 1 - c, sib).wait_recv()
        for cp in sends + passed:
            cp.wait_send()
        for cp in local:
            cp.wait()

    return _Plan(list(shards), [SDS((N_SLOT,) + tuple(a.shape), a.dtype) for a in shards],
                 [pltpu.SemaphoreType.DMA((7 * n,)), pltpu.SemaphoreType.DMA((7 * n,)),
                  pltpu.SemaphoreType.DMA((n,))], start, finish)


def _sibling_plan(grads, whole=()):
    n, m = len(grads), len(whole)

    def copies(ins, outs, sems):
        send_sems, recv_sems = sems
        x, y, c = _place()
        sib = (x, y, 1 - c)

        def rc(t, src, dst):
            return pltpu.make_async_remote_copy(src_ref=src, dst_ref=dst, send_sem=send_sems.at[t],
                                                recv_sem=recv_sems.at[t], device_id=sib, device_id_type=MESH)
        return rc, c

    def start(ins, outs, sems):
        rc, c = copies(ins, outs, sems)
        for t in range(n):
            for j in range(4):
                rc(t, ins[t].at[2 * j + 1 - c], outs[t].at[j]).start()
        for t in range(n, n + m):
            rc(t, ins[t], outs[t]).start()

    def finish(ins, outs, sems):
        rc, _ = copies(ins, outs, sems)
        for t in range(n):
            rc(t, ins[t].at[pl.ds(0, 4)], outs[t]).wait()
        for t in range(n, n + m):
            rc(t, ins[t], outs[t]).wait()

    return _Plan(list(grads) + list(whole),
                 [SDS((4,) + tuple(g.shape[1:]), g.dtype) for g in grads] + [SDS(a.shape, a.dtype) for a in whole],
                 [pltpu.SemaphoreType.DMA((n + m,)), pltpu.SemaphoreType.DMA((n + m,))], start, finish)


def _chips_plan(parts, whole=()):
    n, m = len(parts), len(whole)

    def src_of(ins, t, pc):
        return ins[t].at[pc] if t < n else ins[t]

    def local_copies(ins, outs, sems, chip):
        return [pltpu.make_async_copy(src_of(ins, t, chip), outs[t].at[chip], sems[2].at[t]) for t in range(n + m)]

    def start(ins, outs, sems):
        send_sems, recv_sems, _ = sems
        x, y, c = _place()
        chip = 2 * x + y
        for cp in local_copies(ins, outs, sems, chip):
            cp.start()
        for px, py, pc in _other_chips(x, y):
            for t in range(n + m):
                pltpu.make_async_remote_copy(src_ref=src_of(ins, t, pc), dst_ref=outs[t].at[chip],
                                             send_sem=send_sems.at[t], recv_sem=recv_sems.at[t],
                                             device_id=(px, py, c), device_id_type=MESH).start()

    def finish(ins, outs, sems):
        send_sems, recv_sems, _ = sems
        x, y, c = _place()
        for t in range(n + m):
            three = outs[t].at[pl.ds(0, 3)]
            pltpu.make_async_remote_copy(src_ref=three, dst_ref=three, send_sem=send_sems.at[t],
                                         recv_sem=recv_sems.at[t], device_id=(x, y, c), device_id_type=MESH).wait()
        for cp in local_copies(ins, outs, sems, 2 * x + y):
            cp.wait()

    return _Plan(list(parts) + list(whole),
                 [SDS(p.shape, p.dtype) for p in parts] + [SDS((4,) + tuple(a.shape), a.dtype) for a in whole],
                 [pltpu.SemaphoreType.DMA((n + m,)), pltpu.SemaphoreType.DMA((n + m,)),
                  pltpu.SemaphoreType.DMA((n + m,))], start, finish)


def _exchange_plan(arr):
    def peers(x, y, c):
        flip = lambda v, f: 1 - v if f else v
        return [(flip(x, fx), flip(y, fy), flip(c, fc))
                for fx in (0, 1) for fy in (0, 1) for fc in (0, 1) if fx or fy or fc]

    def start(ins, outs, sems):
        x, y, c = _place()
        me = 4 * x + 2 * y + c
        pltpu.make_async_copy(ins[0], outs[0].at[me], sems[2].at[0]).start()
        for to in peers(x, y, c):
            pltpu.make_async_remote_copy(src_ref=ins[0], dst_ref=outs[0].at[me], send_sem=sems[0].at[0],
                                         recv_sem=sems[1].at[0], device_id=to, device_id_type=MESH).start()

    def finish(ins, outs, sems):
        x, y, c = _place()
        seven = outs[0].at[pl.ds(0, 7)]
        pltpu.make_async_remote_copy(src_ref=seven, dst_ref=seven, send_sem=sems[0].at[0], recv_sem=sems[1].at[0],
                                     device_id=(x, y, c), device_id_type=MESH).wait()
        pltpu.make_async_copy(ins[0], outs[0].at[4 * x + 2 * y + c], sems[2].at[0]).wait()

    return _Plan([arr], [SDS((N_SLOT,) + tuple(arr.shape), arr.dtype)],
                 [pltpu.SemaphoreType.DMA((1,)), pltpu.SemaphoreType.DMA((1,)), pltpu.SemaphoreType.DMA((1,))],
                 start, finish)


def _join(*plans):
    def cut(seq, sizes):
        out, at = [], 0
        for k in sizes:
            out.append(seq[at:at + k])
            at += k
        return out

    n_arr = [len(p.arrays) for p in plans]
    n_sem = [len(p.sems) for p in plans]

    def start(ins, outs, sems):
        for p, i, o, s in zip(plans, cut(ins, n_arr), cut(outs, n_arr), cut(sems, n_sem)):
            p.start(i, o, s)

    def finish(ins, outs, sems):
        for p, i, o, s in zip(plans, cut(ins, n_arr), cut(outs, n_arr), cut(sems, n_sem)):
            p.finish(i, o, s)

    return _Plan([a for p in plans for a in p.arrays], [o for p in plans for o in p.out_shape],
                 [s for p in plans for s in p.sems], start, finish)


def _run_plan(name, plan):
    k = len(plan.arrays)

    def body(*refs):
        ins, outs, sems = refs[:k], refs[k:2 * k], refs[2 * k:]
        plan.start(ins, outs, sems)
        plan.finish(ins, outs, sems)

    return pl.pallas_call(
        body, name=name, in_specs=[ANY] * k, out_specs=[ANY] * k, out_shape=plan.out_shape,
        scratch_shapes=plan.sems, compiler_params=pltpu.CompilerParams(has_side_effects=True),
    )(*plan.arrays)


def _call(body, *, name, grid, in_specs, out_specs, out_shape, scratch_shapes, params, args, comm=None,
          aliases=None, prefetch=()):
    aliases = aliases or {}
    n_pre = len(prefetch)

    def launch(fn, ins_specs, outs_specs, outs_shape, scratch, operands):
        spec = pltpu.PrefetchScalarGridSpec(num_scalar_prefetch=n_pre, grid=grid, in_specs=ins_specs,
                                            out_specs=outs_specs, scratch_shapes=scratch)
        return pl.pallas_call(fn, name=name, grid_spec=spec, out_shape=outs_shape, compiler_params=params,
                              input_output_aliases=aliases)(*prefetch, *operands)

    if comm is None:
        return list(launch(body, in_specs, out_specs, out_shape, scratch_shapes, args)), []
    n_in, n_out, n_scr, k = len(in_specs), len(out_specs), len(scratch_shapes), len(comm.arrays)

    def wrapped(*refs):
        pre, refs = refs[:n_pre], refs[n_pre:]
        ins = refs[:n_in]
        c_in = refs[n_in:n_in + k]
        outs = refs[n_in + k:n_in + k + n_out]
        c_out = refs[n_in + k + n_out:n_in + 2 * k + n_out]
        scr = refs[n_in + 2 * k + n_out:n_in + 2 * k + n_out + n_scr]
        sems = refs[n_in + 2 * k + n_out + n_scr:]
        ids = [pl.program_id(d) for d in range(len(grid))]
        first = ids[0] == 0
        last = ids[0] == grid[0] - 1
        for d in range(1, len(grid)):
            first = first & (ids[d] == 0)
            last = last & (ids[d] == grid[d] - 1)

        @pl.when(first)
        def _():
            comm.start(c_in, c_out, sems)

        body(*pre, *ins, *outs, *scr)

        @pl.when(last)
        def _():
            comm.finish(c_in, c_out, sems)

    res = launch(wrapped, list(in_specs) + [ANY] * k, list(out_specs) + [ANY] * k,
                 list(out_shape) + list(comm.out_shape), list(scratch_shapes) + list(comm.sems),
                 tuple(args) + tuple(comm.arrays))
    return list(res[:n_out]), list(res[n_out:])


def _row_tile(rows):
    for t in (512, 256, 128, 64, 32, 16, 8):
        if rows % t == 0:
            return t
    return rows


def _pair_sum(name, g8, recv4, core):
    _, rows, cols = recv4.shape
    tr = _row_tile(rows)
    g42 = g8.reshape(4, 2, rows, cols)

    def body(c_ref, g_ref, r_ref, o_ref):
        del c_ref
        o_ref[...] = (g_ref[...].astype(F32) + r_ref[...].astype(F32)).astype(o_ref.dtype)

    return pl.pallas_call(
        body, name=name,
        grid_spec=pltpu.PrefetchScalarGridSpec(
            num_scalar_prefetch=1, grid=(4, rows // tr),
            in_specs=[pl.BlockSpec((None, None, tr, cols), lambda j, i, c_ref: (j, c_ref[0], i, 0)),
                      pl.BlockSpec((None, tr, cols), lambda j, i, c_ref: (j, i, 0))],
            out_specs=pl.BlockSpec((None, tr, cols), lambda j, i, c_ref: (j, i, 0))),
        out_shape=SDS(recv4.shape, g8.dtype),
        compiler_params=_cparams(("arbitrary", "arbitrary"), 32),
    )(core, g42, recv4)


def _add2(name, a, b):
    rows, cols = a.shape
    tr = _row_tile(rows)

    def body(a_ref, b_ref, o_ref):
        o_ref[...] = a_ref[...] + b_ref[...]

    blk = pl.BlockSpec((tr, cols), lambda i: (i, 0))
    return pl.pallas_call(body, name=name, grid=(rows // tr,), in_specs=[blk, blk], out_specs=blk,
                          out_shape=SDS(a.shape, a.dtype),
                          compiler_params=_cparams(("arbitrary",), 32))(a, b)


def _sum_terms(name, terms):
    k, rows, cols = terms.shape
    tr = _row_tile(rows)

    def body(r_ref, o_ref):
        acc = r_ref[0]
        for q in range(1, k):
            acc = acc + r_ref[q]
        o_ref[...] = acc

    return pl.pallas_call(body, name=name, grid=(rows // tr,),
                          in_specs=[pl.BlockSpec((k, tr, cols), lambda i: (0, i, 0))],
                          out_specs=pl.BlockSpec((tr, cols), lambda i: (i, 0)),
                          out_shape=SDS((rows, cols), terms.dtype),
                          compiler_params=_cparams(("arbitrary",), 32))(terms)


def _adam_update(g, w, m, v):
    c1 = 1.0 / (1.0 - ADAM_B1 ** ADAM_STEP)
    c2 = 1.0 / (1.0 - ADAM_B2 ** ADAM_STEP)
    mn = ADAM_B1 * m + (1.0 - ADAM_B1) * g
    vn = ADAM_B2 * v + (1.0 - ADAM_B2) * (g * g)
    delta = (-ADAM_LR) * ((mn * c1) / (jnp.sqrt(vn * c2) + ADAM_EPS) + ADAM_WD * w)
    return delta, mn, vn


def _adamw_many(name, gs, ws, ms, vs):
    n = len(gs)

    def body(*refs):
        for p in range(n):
            g, w, m, v = (refs[q * n + p][...] for q in range(4))
            d, mn, vn = _adam_update(g, w, m, v)
            refs[4 * n + p][...] = d
            refs[5 * n + p][...] = mn
            refs[6 * n + p][...] = vn

    full = [pl.BlockSpec(memory_space=pltpu.VMEM)] * n
    shapes = [SDS(w.shape, F32) for w in ws]
    res = pl.pallas_call(body, name=name, in_specs=full * 4, out_specs=full * 3, out_shape=shapes * 3,
                         compiler_params=pltpu.CompilerParams(vmem_limit_bytes=32 * MiB))(*gs, *ws, *ms, *vs)
    return [(res[p], res[n + p], res[2 * n + p]) for p in range(n)]


def _adamw(name, terms, w, m, v):
    k, rows, cols = terms.shape
    tr = _row_tile(rows)

    def body(t_ref, w_ref, m_ref, v_ref, g_ref, d_ref, mo_ref, vo_ref):
        g = t_ref[0].astype(F32)
        for q in range(1, k):
            g = g + t_ref[q].astype(F32)
        g_ref[...] = g
        d_ref[...], mo_ref[...], vo_ref[...] = _adam_update(g, w_ref[...], m_ref[...], v_ref[...])

    blk = pl.BlockSpec((tr, cols), lambda i: (i, 0))
    return pl.pallas_call(body, name=name, grid=(rows // tr,),
                          in_specs=[pl.BlockSpec((k, tr, cols), lambda i: (0, i, 0)), blk, blk, blk],
                          out_specs=[blk] * 4, out_shape=[SDS((rows, cols), F32)] * 4,
                          compiler_params=_cparams(("arbitrary",), 40))(terms, w, m, v)


def kernel(x, norm_mix_g, w_in, conv_w, conv_b, w_rgate, b_rgate, w_igate, b_igate, lru_lambda, w_out_a, sgu_ln_g, sgu_ln_b, sgu_w_s, sgu_b_s, w_out_b, w_out, norm_mlp_g, w_up, w_down, norm_final_g, loss_target, m_norm_mix_g, m_w_in, m_conv_w, m_conv_b, m_w_rgate, m_b_rgate, m_w_igate, m_b_igate, m_lru_lambda, m_w_out_a, m_sgu_ln_g, m_sgu_ln_b, m_sgu_w_s, m_sgu_b_s, m_w_out_b, m_w_out, m_norm_mlp_g, m_w_up, m_w_down, m_norm_final_g, v_norm_mix_g, v_w_in, v_conv_w, v_conv_b, v_w_rgate, v_b_rgate, v_w_igate, v_b_igate, v_lru_lambda, v_w_out_a, v_sgu_ln_g, v_sgu_ln_b, v_sgu_w_s, v_sgu_b_s, v_w_out_b, v_w_out, v_norm_mlp_g, v_w_up, v_w_down, v_norm_final_g):
    cx, cy, cc = _place()
    me = 4 * cx + 2 * cy + cc
    core = jnp.reshape(cc, (1,)).astype(jnp.int32)
    xs = x[0]
    tgt = loss_target[0]
    s = xs.shape[0]

    gate_shard = jnp.stack([w_rgate[0], w_igate[0]]).astype(BF16).reshape(2 * HEADS * 32, HEAD_DIM)
    vec_shard = jnp.concatenate([conv_w[0], b_rgate[0], b_igate[0]], axis=1)
    vec_shard = jnp.pad(vec_shard, ((0, 4), (0, 256 - vec_shard.shape[1])))
    shards = [w_in[0].astype(BF16), w_out_a[0].astype(BF16), w_out_b[0].astype(BF16), w_out[0].astype(BF16),
              w_up[0].astype(BF16), w_down[0].astype(BF16), gate_shard, vec_shard]
    (z, n1_t, w_in_g), (gate_g, vec_g) = _in_proj(xs, norm_mix_g, shards[0], _slot_order(cx, cy, cc),
                                                comm=_gather_plan(shards[6:8]))
    gates = gate_g.reshape(N_SLOT, 2, HEADS, 32, HEAD_DIM).transpose(1, 2, 0, 3, 4).reshape(2, HEADS, HEAD_DIM, HEAD_DIM)
    w_r_f, w_i_f = gates[0], gates[1]
    conv_w_f = vec_g[:, 0:4, 0:128].transpose(1, 0, 2).reshape(CONV_K, D)
    b_r_f = vec_g[:, 0:4, 128:160].transpose(1, 0, 2).reshape(1, D)
    b_i_f = vec_g[:, 0:4, 160:192].transpose(1, 0, 2).reshape(1, D)
    b_s_t = jnp.transpose(sgu_b_s[0])

    (ya, hs), (w_oa_g, w_ob_g, w_out_g) = _branch_a_fwd(
        z, conv_w_f, conv_b, w_r_f, b_r_f, w_i_f, b_i_f, lru_lambda, comm=_gather_plan(shards[1:4]))
    w_oa_f = w_oa_g.reshape(D, D)
    w_ob_f = w_ob_g.reshape(D, D)
    w_out_f = w_out_g.reshape(D, D)
    (yb,), (w_up_g,) = _branch_b_fwd(z, sgu_ln_g, sgu_ln_b, sgu_w_s[0], b_s_t, comm=_gather_plan(shards[4:5]))
    (pa, pb, merged, h1), (w_down_g,) = _merge_out(ya, yb, z, xs, w_oa_f, w_ob_f, w_out_f,
                                                   comm=_gather_plan(shards[5:6]))
    w_down_f = w_down_g.reshape(N_SLOT * FF_COLS, D)
    gf2 = norm_final_g.reshape(1, D)
    r_act, act_t, n2_t, dh2, loss_acc, d_gfin = _mlp_fwd(h1, norm_mlp_g, w_up_g, w_down_f, gf2, tgt)

    def pair(names, grads, recv):
        return [_pair_sum("pair_sum_" + nm, g, r, core) for nm, g, r in zip(names, grads, recv)]

    g_down = _wgrad_t("wgrad_down", act_t, dh2, N_SLOT, True, False)
    (df, dh1, d_gmlp), (r_down,) = _mlp_bwd(dh2, r_act, w_down_f, w_up_g, h1, norm_mlp_g,
                                            comm=_sibling_plan([g_down]))
    (p_down,) = pair(["down"], [g_down], [r_down])
    g_up = _wgrad_t("wgrad_up", n2_t, df, N_SLOT, False, True)
    g_out = _wgrad("wgrad_out", merged, dh1, 1, False, False).reshape(N_SLOT, D // N_SLOT, D)
    (dz, dpa, dpb, dya, dyb), (got_down, r_up, r_out) = _merge_bwd(
        dh1, z, pa, pb, w_out_f, w_oa_f, w_ob_f, comm=_join(_chips_plan([p_down]), _sibling_plan([g_up, g_out])))
    p_up, p_out = pair(["up", "out"], [g_up, g_out], [r_up, r_out])
    g_oa = _wgrad("wgrad_out_a", ya, dpa, 1, False, False).reshape(N_SLOT, D // N_SLOT, D)
    g_ob = _wgrad("wgrad_out_b", yb, dpb, 1, False, False).reshape(N_SLOT, D // N_SLOT, D)
    (dz, d_ws, d_bs, d_ln), (got_up, r_oa, r_ob) = _branch_b_bwd(
        dz, dyb, z, sgu_ln_g, sgu_ln_b, sgu_w_s[0], b_s_t,
        comm=_join(_chips_plan([p_up]), _sibling_plan([g_oa, g_ob])))
    p_oa, p_ob = pair(["out_a", "out_b"], [g_oa, g_ob], [r_oa, r_ob])
    (dz, d_vec, d_wr, d_wi), (got_out, got_oa, got_ob) = _branch_a_bwd(
        dz, dya, z, hs, conv_w_f, conv_b, w_r_f, b_r_f, w_i_f, b_i_f, lru_lambda,
        comm=_chips_plan([p_out, p_oa, p_ob]))
    g_in = _wgrad_t("wgrad_in", n1_t, dz, N_SLOT, False, True)
    g_gate = jnp.stack([d_wr, d_wi]).reshape(2, HEADS, N_SLOT, 32, HEAD_DIM).transpose(2, 0, 1, 3, 4)
    g_gate = g_gate.reshape(N_SLOT, 2 * HEADS * 32, HEAD_DIM).astype(BF16)

    d_bs_row = jnp.pad(d_bs[:, :, 0].reshape(1, GROUPS * CHUNK), ((0, 0), (0, D - GROUPS * CHUNK)))
    vecs = jnp.concatenate([d_vec, jnp.concatenate([d_ln[0:2], d_gmlp, d_gfin, d_bs_row, jnp.zeros((3, D), F32)])])
    d_ws2 = d_ws.reshape(GROUPS * CHUNK, CHUNK)
    r_in, r_gate, r_vecs, r_ws = _run_plan("rs_sibling_in", _sibling_plan([g_in, g_gate], [vecs, d_ws2]))
    p_in, p_gate = pair(["in", "gate"], [g_in, g_gate], [r_in, r_gate])
    vecs_chip = _add2("pair_sum_vecs", vecs, r_vecs)
    ws_chip = _add2("pair_sum_ws", d_ws2, r_ws)
    (dx, d_gmix), (got_in, got_gate, got_vecs, got_ws) = _in_bwd(
        dz, w_in_g, xs, dh1, norm_mix_g, comm=_chips_plan([p_in, p_gate], [vecs_chip, ws_chip]))
    vecs_sum = _sum_terms("sum_vecs", got_vecs)
    last = jnp.concatenate([d_gmix, jnp.pad(loss_acc[0:1], ((0, 0), (0, D - 128))), jnp.zeros((6, D), F32)])
    (last_all,) = _run_plan("exchange_last", _exchange_plan(last))
    last_sum = _sum_terms("sum_last", last_all)
    loss = last_sum[1, 0]
    got = [got_in, got_oa, got_ob, got_out, got_up, got_down, got_gate]

    def step(nm, terms, w, m, v, rows, cols):
        g, d, mn, vn = _adamw("adamw_" + nm, terms.reshape(4, rows, cols), w.reshape(rows, cols),
                              m.reshape(rows, cols), v.reshape(rows, cols))
        return [a.reshape(w.shape) for a in (g, d, mn, vn)]

    o_in = step("in", got[0], w_in, m_w_in, v_w_in, D, W_IN_COLS)
    o_oa = step("out_a", got[1], w_out_a, m_w_out_a, v_w_out_a, D // N_SLOT, D)
    o_ob = step("out_b", got[2], w_out_b, m_w_out_b, v_w_out_b, D // N_SLOT, D)
    o_out = step("out", got[3], w_out, m_w_out, v_w_out, D // N_SLOT, D)
    o_up = step("up", got[4], w_up, m_w_up, v_w_up, D, FF_COLS)
    o_down = step("down", got[5], w_down, m_w_down, v_w_down, FF_COLS, D)
    gate_w = jnp.stack([w_rgate[0], w_igate[0]]).reshape(2 * HEADS * 32, HEAD_DIM)
    gate_m = jnp.stack([m_w_rgate[0], m_w_igate[0]]).reshape(2 * HEADS * 32, HEAD_DIM)
    gate_v = jnp.stack([v_w_rgate[0], v_w_igate[0]]).reshape(2 * HEADS * 32, HEAD_DIM)
    o_gate = _adamw("adamw_gate", got[6], gate_w, gate_m, gate_v)
    o_gate = [a.reshape(2, 1, HEADS, 32, HEAD_DIM) for a in o_gate]
    o_wr = [a[0] for a in o_gate]
    o_wi = [a[1] for a in o_gate]

    def own(full, width):
        return lax.dynamic_slice_in_dim(full, me * width, width, axis=1)

    small_g = {
        "norm_mix_g": last_sum[0:1], "conv_w": own(vecs_sum[0:4], 128), "conv_b": vecs_sum[4:5],
        "b_rgate": own(vecs_sum[5:6].reshape(HEADS, HEAD_DIM), 32),
        "b_igate": own(vecs_sum[6:7].reshape(HEADS, HEAD_DIM), 32),
        "lru_lambda": vecs_sum[7:8], "sgu_ln_g": vecs_sum[8:9], "sgu_ln_b": vecs_sum[9:10],
        "norm_mlp_g": vecs_sum[10:11], "norm_final_g": vecs_sum[11:12],
        "sgu_b_s": vecs_sum[12, 0:GROUPS * CHUNK].reshape(GROUPS, CHUNK),
    }
    small_w = {"norm_mix_g": (norm_mix_g, m_norm_mix_g, v_norm_mix_g), "conv_w": (conv_w, m_conv_w, v_conv_w),
               "conv_b": (conv_b, m_conv_b, v_conv_b), "b_rgate": (b_rgate, m_b_rgate, v_b_rgate),
               "b_igate": (b_igate, m_b_igate, v_b_igate), "lru_lambda": (lru_lambda, m_lru_lambda, v_lru_lambda),
               "sgu_ln_g": (sgu_ln_g, m_sgu_ln_g, v_sgu_ln_g), "sgu_ln_b": (sgu_ln_b, m_sgu_ln_b, v_sgu_ln_b),
               "norm_mlp_g": (norm_mlp_g, m_norm_mlp_g, v_norm_mlp_g),
               "norm_final_g": (norm_final_g, m_norm_final_g, v_norm_final_g),
               "sgu_b_s": (sgu_b_s, m_sgu_b_s, v_sgu_b_s), "sgu_w_s": (sgu_w_s, m_sgu_w_s, v_sgu_w_s)}
    order = list(small_g)
    as2d = lambda k, a: a.reshape(small_g[k].shape)
    upd = _adamw_many("adamw_small", [small_g[k] for k in order], *[[as2d(k, small_w[k][q]) for k in order]
                                                                     for q in range(3)])
    o_small = {k: [a.reshape(small_w[k][0].shape) for a in (small_g[k],) + u] for k, u in zip(order, upd)}
    ws3 = [a[0].reshape(GROUPS * CHUNK, CHUNK) for a in small_w.pop("sgu_w_s")]
    o_small["sgu_w_s"] = [a.reshape(sgu_w_s.shape) for a in _adamw("adamw_ws", got_ws, *ws3)]

    per_weight = {"norm_mix_g": o_small["norm_mix_g"], "w_in": o_in, "conv_w": o_small["conv_w"],
                  "conv_b": o_small["conv_b"], "w_rgate": o_wr, "b_rgate": o_small["b_rgate"], "w_igate": o_wi,
                  "b_igate": o_small["b_igate"], "lru_lambda": o_small["lru_lambda"], "w_out_a": o_oa,
                  "sgu_ln_g": o_small["sgu_ln_g"], "sgu_ln_b": o_small["sgu_ln_b"], "sgu_w_s": o_small["sgu_w_s"],
                  "sgu_b_s": o_small["sgu_b_s"], "w_out_b": o_ob, "w_out": o_out, "norm_mlp_g": o_small["norm_mlp_g"],
                  "w_up": o_up, "w_down": o_down, "norm_final_g": o_small["norm_final_g"]}
    names_w = list(per_weight)
    return (loss, dx[None], *[per_weight[k][0] for k in names_w], *[per_weight[k][1] for k in names_w],
            *[per_weight[k][2] for k in names_w], *[per_weight[k][3] for k in names_w])
```

```python
import jax
import jax.numpy as jnp
from jax import lax
from jax.experimental import pallas as pl
from jax.experimental.pallas import tpu as pltpu

F32 = jnp.float32
BF16 = jnp.bfloat16
SDS = jax.ShapeDtypeStruct
MESH = pl.DeviceIdType.MESH
ANY = pl.BlockSpec(memory_space=pltpu.HBM)

D = 1024
N_SLOT = 8
W_IN_COLS = 768
FF_COLS = 512
HEADS, HEAD_DIM = 4, 256
GROUPS, GROUP_DIM = 4, 256
CHUNK = 128
CONV_K = 4
NORM_EPS = 1e-6
LN_EPS = 1e-5
LRU_C = 8.0
ADAM_LR, ADAM_B1, ADAM_B2, ADAM_EPS, ADAM_WD, ADAM_STEP = 0.001, 0.9, 0.999, 1e-08, 0.01, 10

TM_ROWS = 1024
TM_MERGE = 512
T_BRANCH = 256
MiB = 1024 * 1024

_GELU_C = 0.7978845608028654
_GELU_A = 0.044715


def _cparams(sem, vmem_mib):
    return pltpu.CompilerParams(dimension_semantics=sem, vmem_limit_bytes=vmem_mib * MiB)


def _gelu(x):
    t = jnp.tanh(_GELU_C * (x + _GELU_A * x * x * x))
    return 0.5 * x * (1.0 + t)


def _gelu_and_grad(x):
    x2 = x * x
    t = jnp.tanh(_GELU_C * x * (1.0 + _GELU_A * x2))
    g = 0.5 * x * (1.0 + t)
    dg = 0.5 * (1.0 + t) + 0.5 * x * (1.0 - t * t) * _GELU_C * (1.0 + 3.0 * _GELU_A * x2)
    return g, dg


def _softplus(x):
    return jnp.maximum(x, 0.0) + jnp.log1p(jnp.exp(-jnp.abs(x)))


def _dot(a, b):
    return jnp.dot(a, b, preferred_element_type=F32)


def _dot_nt(a, b):
    return lax.dot_general(a, b, (((1,), (1,)), ((), ())), preferred_element_type=F32)


def _dot_tn(a, b):
    return lax.dot_general(a, b, (((0,), (0,)), ((), ())), preferred_element_type=F32)


def _rows_shifted(prev8, cur, k):
    ext = jnp.concatenate([prev8, cur], axis=0)
    return pltpu.roll(ext, k, 0)[8:]


def _rows_advanced(cur, next8, k):
    t = cur.shape[0]
    ext = jnp.concatenate([cur, next8], axis=0)
    return pltpu.roll(ext, t + 8 - k, 0)[:t]


def _slot_order(x, y, c):
    chip = 2 * x + y
    order = [2 * chip + c, 2 * chip + 1 - c]
    for _, _, pc in _other_chips(x, y):
        order += [2 * pc + c, 2 * pc + 1 - c]
    return jnp.stack(order).astype(jnp.int32)


def _in_proj(x, g_mix, w_in_own, order, comm=None):
    s = x.shape[0]
    tm = min(TM_ROWS, s)
    ni = s // tm

    def body(order_ref, x_ref, g_ref, own_ref, z_ref, nt_ref, wg_ref, n_s, w_s, send_sems, recv_sems, local_sems):
        j, i = pl.program_id(0), pl.program_id(1)
        px, py, c = _place()
        chip = 2 * px + py
        me = 2 * chip + c
        sib = (px, py, 1 - c)
        chips = _other_chips(px, py)

        def rc(k, src, blk, to):
            return pltpu.make_async_remote_copy(src_ref=src, dst_ref=w_s.at[blk], send_sem=send_sems.at[k],
                                                recv_sem=recv_sems.at[k], device_id=to, device_id_type=MESH)

        own_in = pltpu.make_async_copy(own_ref, w_s.at[me], local_sems.at[0])
        sends = [rc(0, own_ref, me, sib)] + [rc(1 + q, own_ref, me, (qx, qy, c)) for q, (qx, qy, _) in enumerate(chips)]
        passed = [rc(4 + q, w_s.at[2 * pc + c], 2 * pc + c, sib) for q, (_, _, pc) in enumerate(chips)]
        keep = pltpu.make_async_copy(w_s, wg_ref, local_sems.at[1])

        @pl.when((i == 0) & (j == 0))
        def _():
            own_in.start()
            for cp in sends:
                cp.start()
            own_in.wait()

        @pl.when((i == 0) & (j == 1))
        def _():
            rc(0, own_ref, 2 * chip + 1 - c, sib).wait_recv()

        for q, (_, _, pc) in enumerate(chips):
            @pl.when((i == 0) & (j == 2 + 2 * q))
            def _():
                rc(1 + q, own_ref, 2 * pc + c, sib).wait_recv()
                passed[q].start()

            @pl.when((i == 0) & (j == 3 + 2 * q))
            def _():
                rc(4 + q, own_ref, 2 * pc + 1 - c, sib).wait_recv()

        rows = pl.ds(pl.multiple_of(i * tm, tm), tm)

        @pl.when(j == 0)
        def _():
            xv = x_ref[...]
            rstd = lax.rsqrt(jnp.mean(xv * xv, axis=-1, keepdims=True) + NORM_EPS)
            nb = (xv * rstd * g_ref[...]).astype(BF16)
            n_s[rows, :] = nb
            nt_ref[...] = nb.T

        z_ref[...] = _dot(n_s[rows, :], w_s[order_ref[j]]).astype(BF16)

        @pl.when((i == 0) & (j == N_SLOT - 1))
        def _():
            keep.start()

        @pl.when((i == ni - 1) & (j == N_SLOT - 1))
        def _():
            for cp in sends + passed:
                cp.wait_send()
            keep.wait()

    first_pass = lambda j, i, o: (jnp.where(j == 0, i, ni - 1), 0)
    (z, n1, w_in_g), extra = _call(
        body, name="in_proj", grid=(N_SLOT, ni), prefetch=(order,),
        in_specs=[pl.BlockSpec((tm, D), first_pass),
                  pl.BlockSpec((1, D), lambda j, i, o: (0, 0)), ANY],
        out_specs=[pl.BlockSpec((tm, W_IN_COLS), lambda j, i, o: (i, o[j])),
                   pl.BlockSpec((D, tm), lambda j, i, o: (0, jnp.where(j == 0, i, ni - 1))), ANY],
        out_shape=[SDS((s, N_SLOT * W_IN_COLS), BF16), SDS((D, s), BF16), SDS((N_SLOT, D, W_IN_COLS), BF16)],
        scratch_shapes=[pltpu.VMEM((s, D), BF16), pltpu.VMEM((N_SLOT, D, W_IN_COLS), BF16),
                        pltpu.SemaphoreType.DMA((7,)), pltpu.SemaphoreType.DMA((7,)), pltpu.SemaphoreType.DMA((2,))],
        params=_cparams(("arbitrary", "arbitrary"), 56), args=(x, g_mix, w_in_own), comm=comm)
    return (z, n1, w_in_g), extra


def _lru_gates(xc, xcb, wr_ref, br, wi_ref, bi, sp_lam, a_s, b_s, r_s=None, i_s=None, m_s=None):
    for h in range(HEADS):
        sl = slice(h * HEAD_DIM, (h + 1) * HEAD_DIM)
        r = jax.nn.sigmoid(_dot(xcb[:, sl], wr_ref[h]) + br[:, sl])
        ig = jax.nn.sigmoid(_dot(xcb[:, sl], wi_ref[h]) + bi[:, sl])
        log_a = (-LRU_C) * r * sp_lam[:, sl]
        a = jnp.exp(log_a)
        mult = jnp.sqrt(-jnp.tanh(log_a) * (a * a + 1.0))
        a_s[:, sl] = a
        b_s[:, sl] = xc[:, sl] * ig * mult
        if r_s is not None:
            r_s[:, sl] = r
            i_s[:, sl] = ig
            m_s[:, sl] = mult


def _conv_fwd(xa, prev8, cw, cb):
    xc = cb + cw[0:1, :] * xa
    for k in range(1, CONV_K):
        xc = xc + cw[k:k + 1, :] * _rows_shifted(prev8, xa, k)
    return xc


def _branch_a_fwd(z, conv_w, conv_b, w_r, b_r, w_i, b_i, lam, comm=None):
    s = z.shape[0]
    ta = min(T_BRANCH, s)
    per16 = ta // 16

    def body(xa_ref, xp_ref, ga_ref, cw_ref, cb_ref, wr_ref, br_ref, wi_ref, bi_ref, lam_ref,
             ya_ref, hs_ref, a_s, b_s, h_s, carry_s):
        i = pl.program_id(0)

        @pl.when(i == 0)
        def _():
            carry_s[...] = jnp.zeros_like(carry_s)

        xa = xa_ref[...].astype(F32)
        prev8 = jnp.where(i > 0, xp_ref[...].astype(F32)[8:16], 0.0)
        xc = _conv_fwd(xa, prev8, cw_ref[...], cb_ref[...])
        sp_lam = _softplus(-lam_ref[...])
        _lru_gates(xc, xc.astype(BF16), wr_ref, br_ref[...], wi_ref, bi_ref[...], sp_lam, a_s, b_s)

        row = lax.broadcasted_iota(jnp.int32, (8, D), 0)

        def group(g, carry):
            off = pl.multiple_of(g * 8, 8)
            a8 = a_s[pl.ds(off, 8), :]
            b8 = b_s[pl.ds(off, 8), :]
            for d in (1, 2, 4):
                a_sh = jnp.where(row >= d, pltpu.roll(a8, d, 0), 1.0)
                b_sh = jnp.where(row >= d, pltpu.roll(b8, d, 0), 0.0)
                b8 = a8 * b_sh + b8
                a8 = a8 * a_sh
            h8 = b8 + a8 * carry
            h_s[pl.ds(off, 8), :] = h8
            return jnp.broadcast_to(h8[7:8, :], (8, D))

        carry_s[...] = lax.fori_loop(0, ta // 8, group, carry_s[...])
        hs = h_s[...]
        hs_ref[...] = hs.astype(BF16)
        ya_ref[...] = (hs * _gelu(ga_ref[...].astype(F32))).astype(BF16)

    vec = pl.BlockSpec((1, D), lambda i: (0, 0))
    gate = pl.BlockSpec((HEADS, HEAD_DIM, HEAD_DIM), lambda i: (0, 0, 0))
    return _call(
        body, name="branch_a_fwd", grid=(s // ta,),
        in_specs=[pl.BlockSpec((ta, D), lambda i: (i, 0)),
                  pl.BlockSpec((16, D), lambda i: (jnp.maximum(i * per16 - 1, 0), 0)),
                  pl.BlockSpec((ta, D), lambda i: (i, 1)),
                  pl.BlockSpec((CONV_K, D), lambda i: (0, 0)), vec, gate, vec, gate, vec, vec],
        out_specs=[pl.BlockSpec((ta, D), lambda i: (i, 0)), pl.BlockSpec((ta, D), lambda i: (i, 0))],
        out_shape=[SDS((s, D), BF16), SDS((s, D), BF16)],
        scratch_shapes=[pltpu.VMEM((ta, D), F32), pltpu.VMEM((ta, D), F32), pltpu.VMEM((ta, D), F32),
                        pltpu.VMEM((8, D), F32)],
        params=_cparams(("arbitrary",), 40), args=(z, z, z, conv_w, conv_b, w_r, b_r, w_i, b_i, lam), comm=comm)


def _sgu_common(ub, vb, lg, lb, with_grad):
    if with_grad:
        u, du = _gelu_and_grad(ub)
        v, dv = _gelu_and_grad(vb)
    else:
        u, v, du, dv = _gelu(ub), _gelu(vb), None, None
    mu = jnp.mean(v, axis=-1, keepdims=True)
    vc = v - mu
    rstd = lax.rsqrt(jnp.mean(vc * vc, axis=-1, keepdims=True) + LN_EPS)
    vhat = vc * rstd
    vln = vhat * lg + lb
    return u, du, dv, rstd, vhat, vln


def _masked_ws(ws_ref):
    t = lax.broadcasted_iota(jnp.int32, (CHUNK, CHUNK), 0)
    c = lax.broadcasted_iota(jnp.int32, (CHUNK, CHUNK), 1)
    keep = c <= t
    return [jnp.where(keep, ws_ref[g], 0.0).astype(BF16) for g in range(GROUPS)]


def _branch_b_fwd(z, ln_g, ln_b, w_s, b_s_t, comm=None):
    s = z.shape[0]
    tb = min(T_BRANCH, s)

    def body(ub_ref, vb_ref, lg_ref, lb_ref, ws_ref, bs_ref, yb_ref):
        u, _, _, _, _, vln = _sgu_common(ub_ref[...].astype(F32), vb_ref[...].astype(F32),
                                         lg_ref[...], lb_ref[...], False)
        vlnb = vln.astype(BF16)
        wm = _masked_ws(ws_ref)
        bs = bs_ref[...]
        for c in range(tb // CHUNK):
            rs = slice(c * CHUNK, (c + 1) * CHUNK)
            for g in range(GROUPS):
                cs = slice(g * GROUP_DIM, (g + 1) * GROUP_DIM)
                sp = _dot(wm[g], vlnb[rs, cs]) + bs[:, g:g + 1]
                yb_ref[rs, cs] = (u[rs, cs] * sp).astype(BF16)

    vec = pl.BlockSpec((1, D), lambda i: (0, 0))
    return _call(
        body, name="branch_b_fwd", grid=(s // tb,),
        in_specs=[pl.BlockSpec((tb, D), lambda i: (i, 2)), pl.BlockSpec((tb, D), lambda i: (i, 3)), vec, vec,
                  pl.BlockSpec((GROUPS, CHUNK, CHUNK), lambda i: (0, 0, 0)),
                  pl.BlockSpec((CHUNK, GROUPS), lambda i: (0, 0))],
        out_specs=[pl.BlockSpec((tb, D), lambda i: (i, 0))],
        out_shape=[SDS((s, D), BF16)], scratch_shapes=[],
        params=_cparams(("arbitrary",), 40), args=(z, z, ln_g, ln_b, w_s, b_s_t), comm=comm)


def _merge_out(ya, yb, z, x, w_oa, w_ob, w_out, comm=None):
    s = x.shape[0]
    tm = min(TM_MERGE, s)

    def body(ya_ref, yb_ref, ma_ref, mb_ref, x_ref, woa_ref, wob_ref, wo_ref, pa_ref, pb_ref, mg_ref, h1_ref):
        pa = _dot(ya_ref[...], woa_ref[...])
        pb = _dot(yb_ref[...], wob_ref[...])
        merged = (jax.nn.sigmoid(ma_ref[...].astype(F32)) * pa
                  + jax.nn.sigmoid(mb_ref[...].astype(F32)) * pb).astype(BF16)
        pa_ref[...] = pa.astype(BF16)
        pb_ref[...] = pb.astype(BF16)
        mg_ref[...] = merged
        h1_ref[...] = x_ref[...] + _dot(merged, wo_ref[...])

    row = pl.BlockSpec((tm, D), lambda i: (i, 0))
    wsp = pl.BlockSpec((D, D), lambda i: (0, 0))
    return _call(
        body, name="merge_out", grid=(s // tm,),
        in_specs=[row, row, pl.BlockSpec((tm, D), lambda i: (i, 4)), pl.BlockSpec((tm, D), lambda i: (i, 5)),
                  row, wsp, wsp, wsp],
        out_specs=[row, row, row, row],
        out_shape=[SDS((s, D), BF16), SDS((s, D), BF16), SDS((s, D), BF16), SDS((s, D), F32)], scratch_shapes=[],
        params=_cparams(("arbitrary",), 48), args=(ya, yb, z, z, x, w_oa, w_ob, w_out), comm=comm)


def _mlp_fwd(h1, g_mlp, w_up_g, w_down, g_fin, tgt):
    s = h1.shape[0]
    tm = min(TM_ROWS, s)
    nj = N_SLOT

    def body(h1_ref, gm_ref, wu_ref, wd_ref, gf_ref, t_ref, r_ref, at_ref, n2t_ref, dh2_ref, loss_ref, dgf_ref,
             n2_s, acc_s):
        i, j = pl.program_id(0), pl.program_id(1)

        @pl.when(j == 0)
        def _():
            hv = h1_ref[...]
            rstd = lax.rsqrt(jnp.mean(hv * hv, axis=-1, keepdims=True) + NORM_EPS)
            nb = (hv * rstd * gm_ref[...]).astype(BF16)
            n2_s[...] = nb
            n2t_ref[...] = nb.T
            acc_s[...] = jnp.zeros_like(acc_s)

        @pl.when((i == 0) & (j == 0))
        def _():
            loss_ref[...] = jnp.zeros_like(loss_ref)
            dgf_ref[...] = jnp.zeros_like(dgf_ref)

        r = jnp.maximum(_dot(n2_s[...], wu_ref[...]), 0.0)
        r_ref[...] = r.astype(BF16)
        act = (r * r).astype(BF16)
        at_ref[...] = act.T
        acc_s[...] += _dot(act, wd_ref[...])

        @pl.when(j == nj - 1)
        def _():
            h2 = h1_ref[...] + acc_s[...]
            rstd = lax.rsqrt(jnp.mean(h2 * h2, axis=-1, keepdims=True) + NORM_EPS)
            hh = h2 * rstd
            gf = gf_ref[...]
            e = hh * gf - t_ref[...]
            loss_ref[...] += jnp.sum(e * e) * (0.5 / D)
            dy = e * (1.0 / D)
            dgf_ref[...] += jnp.sum(dy * hh, axis=0, keepdims=True)
            dhh = dy * gf
            dh2_ref[...] = rstd * (dhh - hh * jnp.mean(dhh * hh, axis=-1, keepdims=True))

    row = pl.BlockSpec((tm, D), lambda i, j: (i, 0))
    vec = pl.BlockSpec((1, D), lambda i, j: (0, 0))
    return pl.pallas_call(
        body, name="mlp_fwd", grid=(s // tm, nj),
        in_specs=[row, vec, pl.BlockSpec((None, D, FF_COLS), lambda i, j: (j, 0, 0)),
                  pl.BlockSpec((FF_COLS, D), lambda i, j: (j, 0)), vec, row],
        out_specs=[pl.BlockSpec((tm, FF_COLS), lambda i, j: (i, j)), pl.BlockSpec((FF_COLS, tm), lambda i, j: (j, i)),
                   pl.BlockSpec((D, tm), lambda i, j: (0, i)), row, pl.BlockSpec((8, 128), lambda i, j: (0, 0)), vec],
        out_shape=[SDS((s, nj * FF_COLS), BF16), SDS((nj * FF_COLS, s), BF16), SDS((D, s), BF16), SDS((s, D), F32),
                   SDS((8, 128), F32), SDS((1, D), F32)],
        scratch_shapes=[pltpu.VMEM((tm, D), BF16), pltpu.VMEM((tm, D), F32)],
        compiler_params=_cparams(("arbitrary", "arbitrary"), 52),
    )(h1, g_mlp, w_up_g, w_down, g_fin, tgt)


def _mlp_bwd(dh2, r, w_down, w_up_g, h1, g_mlp, comm=None):
    s = h1.shape[0]
    tm = min(TM_ROWS, s)
    nj = N_SLOT

    def body(dh2_ref, r_ref, wd_ref, wu_ref, h1_ref, gm_ref, df_ref, dh1_ref, dgm_ref, dh2b_s, acc_s):
        i, j = pl.program_id(0), pl.program_id(1)

        @pl.when(j == 0)
        def _():
            dh2b_s[...] = dh2_ref[...].astype(BF16)
            acc_s[...] = jnp.zeros_like(acc_s)

        @pl.when((i == 0) & (j == 0))
        def _():
            dgm_ref[...] = jnp.zeros_like(dgm_ref)

        d_act = _dot_nt(dh2b_s[...], wd_ref[...])
        df = (d_act * (2.0 * r_ref[...].astype(F32))).astype(BF16)
        df_ref[...] = df
        acc_s[...] += _dot_nt(df, wu_ref[...])

        @pl.when(j == nj - 1)
        def _():
            hv = h1_ref[...]
            rstd = lax.rsqrt(jnp.mean(hv * hv, axis=-1, keepdims=True) + NORM_EPS)
            hh = hv * rstd
            dn2 = acc_s[...]
            dgm_ref[...] += jnp.sum(dn2 * hh, axis=0, keepdims=True)
            dhat = dn2 * gm_ref[...]
            dh1_ref[...] = dh2_ref[...] + rstd * (dhat - hh * jnp.mean(dhat * hh, axis=-1, keepdims=True))

    row = pl.BlockSpec((tm, D), lambda i, j: (i, 0))
    vec = pl.BlockSpec((1, D), lambda i, j: (0, 0))
    ffb = pl.BlockSpec((tm, FF_COLS), lambda i, j: (i, j))
    return _call(
        body, name="mlp_bwd", grid=(s // tm, nj),
        in_specs=[row, ffb, pl.BlockSpec((FF_COLS, D), lambda i, j: (j, 0)),
                  pl.BlockSpec((None, D, FF_COLS), lambda i, j: (j, 0, 0)), row, vec],
        out_specs=[ffb, row, vec],
        out_shape=[SDS((s, nj * FF_COLS), BF16), SDS((s, D), F32), SDS((1, D), F32)],
        scratch_shapes=[pltpu.VMEM((tm, D), BF16), pltpu.VMEM((tm, D), F32)],
        params=_cparams(("arbitrary", "arbitrary"), 52), args=(dh2, r, w_down, w_up_g, h1, g_mlp), comm=comm)


def _merge_bwd(dh1, z, pa, pb, w_out, w_oa, w_ob, comm=None):
    s = dh1.shape[0]
    tm = min(TM_MERGE, s)

    def body(dh1_ref, ma_ref, mb_ref, pa_ref, pb_ref, wo_ref, woa_ref, wob_ref,
             dz_ref, dpa_ref, dpb_ref, dya_ref, dyb_ref):
        dm = _dot_nt(dh1_ref[...].astype(BF16), wo_ref[...])
        sa = jax.nn.sigmoid(ma_ref[...].astype(F32))
        sb = jax.nn.sigmoid(mb_ref[...].astype(F32))
        dpa = (dm * sa).astype(BF16)
        dpb = (dm * sb).astype(BF16)
        dz_ref[:, 0:D] = (dm * pa_ref[...].astype(F32) * sa * (1.0 - sa)).astype(BF16)
        dz_ref[:, D:2 * D] = (dm * pb_ref[...].astype(F32) * sb * (1.0 - sb)).astype(BF16)
        dpa_ref[...] = dpa
        dpb_ref[...] = dpb
        dya_ref[...] = _dot_nt(dpa, woa_ref[...]).astype(BF16)
        dyb_ref[...] = _dot_nt(dpb, wob_ref[...]).astype(BF16)

    row = pl.BlockSpec((tm, D), lambda i: (i, 0))
    wsp = pl.BlockSpec((D, D), lambda i: (0, 0))
    return _call(
        body, name="merge_bwd", grid=(s // tm,),
        in_specs=[row, pl.BlockSpec((tm, D), lambda i: (i, 4)), pl.BlockSpec((tm, D), lambda i: (i, 5)),
                  row, row, wsp, wsp, wsp],
        out_specs=[pl.BlockSpec((tm, 2 * D), lambda i: (i, 2)), row, row, row, row],
        out_shape=[SDS((s, 6 * D), BF16)] + [SDS((s, D), BF16)] * 4, scratch_shapes=[],
        params=_cparams(("arbitrary",), 48), args=(dh1, z, z, pa, pb, w_out, w_oa, w_ob), comm=comm)


def _branch_b_bwd(dz, dyb, z, ln_g, ln_b, w_s, b_s_t, comm=None):
    s = z.shape[0]
    tb = min(T_BRANCH, s)

    def body(dz_in, dyb_ref, ub_ref, vb_ref, lg_ref, lb_ref, ws_ref, bs_ref,
             dz_ref, dws_ref, dbs_ref, dln_ref, du_s, dvln_s):
        del dz_in

        @pl.when(pl.program_id(0) == 0)
        def _():
            dws_ref[...] = jnp.zeros_like(dws_ref)
            dbs_ref[...] = jnp.zeros_like(dbs_ref)
            dln_ref[...] = jnp.zeros_like(dln_ref)

        lg = lg_ref[...]
        u, du, dv, rstd, vhat, vln = _sgu_common(ub_ref[...].astype(F32), vb_ref[...].astype(F32),
                                                 lg, lb_ref[...], True)
        vlnb = vln.astype(BF16)
        dyb_v = dyb_ref[...].astype(F32)
        wm = _masked_ws(ws_ref)
        keep = (lax.broadcasted_iota(jnp.int32, (CHUNK, CHUNK), 1)
                <= lax.broadcasted_iota(jnp.int32, (CHUNK, CHUNK), 0))
        bs = bs_ref[...]
        for c in range(tb // CHUNK):
            rs = slice(c * CHUNK, (c + 1) * CHUNK)
            for g in range(GROUPS):
                cs = slice(g * GROUP_DIM, (g + 1) * GROUP_DIM)
                v_blk = vlnb[rs, cs]
                sp = _dot(wm[g], v_blk) + bs[:, g:g + 1]
                d_sp = dyb_v[rs, cs] * u[rs, cs]
                d_spb = d_sp.astype(BF16)
                du_s[rs, cs] = dyb_v[rs, cs] * sp
                dvln_s[rs, cs] = _dot_tn(wm[g], d_spb)
                dws_ref[g] += jnp.where(keep, _dot_nt(d_spb, v_blk), 0.0)
                dbs_ref[g] += jnp.broadcast_to(jnp.sum(d_sp, axis=-1, keepdims=True), (CHUNK, CHUNK))
        dvln = dvln_s[...]
        dln_ref[0:1, :] += jnp.sum(dvln * vhat, axis=0, keepdims=True)
        dln_ref[1:2, :] += jnp.sum(dvln, axis=0, keepdims=True)
        dvh = dvln * lg
        d_v = rstd * (dvh - jnp.mean(dvh, axis=-1, keepdims=True)
                      - vhat * jnp.mean(dvh * vhat, axis=-1, keepdims=True))
        dz_ref[:, 0:D] = (du_s[...] * du).astype(BF16)
        dz_ref[:, D:2 * D] = (d_v * dv).astype(BF16)

    vec = pl.BlockSpec((1, D), lambda i: (0, 0))
    sq = pl.BlockSpec((GROUPS, CHUNK, CHUNK), lambda i: (0, 0, 0))
    return _call(
        body, name="branch_b_bwd", grid=(s // tb,),
        in_specs=[ANY, pl.BlockSpec((tb, D), lambda i: (i, 0)),
                  pl.BlockSpec((tb, D), lambda i: (i, 2)), pl.BlockSpec((tb, D), lambda i: (i, 3)), vec, vec, sq,
                  pl.BlockSpec((CHUNK, GROUPS), lambda i: (0, 0))],
        out_specs=[pl.BlockSpec((tb, 2 * D), lambda i: (i, 1)), sq, sq, pl.BlockSpec((8, D), lambda i: (0, 0))],
        out_shape=[SDS(dz.shape, BF16), SDS((GROUPS, CHUNK, CHUNK), F32), SDS((GROUPS, CHUNK, CHUNK), F32),
                   SDS((8, D), F32)],
        scratch_shapes=[pltpu.VMEM((tb, D), F32), pltpu.VMEM((tb, D), F32)], aliases={0: 0},
        params=_cparams(("arbitrary",), 40), args=(dz, dyb, z, z, ln_g, ln_b, w_s, b_s_t), comm=comm)


def _branch_a_bwd(dz, dya, z, hs, conv_w, conv_b, w_r, b_r, w_i, b_i, lam, comm=None):
    s = z.shape[0]
    ta = min(T_BRANCH, s)
    nb = s // ta
    per16 = ta // 16

    def body(dz_in, dya_ref, xa_ref, xp_ref, ga_ref, hs_ref, hp_ref, cw_ref, cb_ref, wr_ref, br_ref, wi_ref,
             bi_ref, lam_ref, dz_ref, vec_ref, dwr_ref, dwi_ref,
             a_s, b_s, h_s, r_s, i_s, m_s, dcar_s, acar_s, dxc_s):
        del dz_in
        i = pl.program_id(0)
        blk = nb - 1 - i

        @pl.when(i == 0)
        def _():
            dcar_s[...] = jnp.zeros_like(dcar_s)
            acar_s[...] = jnp.zeros_like(acar_s)
            dxc_s[...] = jnp.zeros_like(dxc_s)
            vec_ref[...] = jnp.zeros_like(vec_ref)
            dwr_ref[...] = jnp.zeros_like(dwr_ref)
            dwi_ref[...] = jnp.zeros_like(dwi_ref)

        cw = cw_ref[...]
        lam_v = lam_ref[...]
        xa = xa_ref[...].astype(F32)
        prev8 = jnp.where(blk > 0, xp_ref[...].astype(F32)[8:16], 0.0)
        xc = _conv_fwd(xa, prev8, cw, cb_ref[...])
        xcb = xc.astype(BF16)
        sp_lam = _softplus(-lam_v)
        _lru_gates(xc, xcb, wr_ref, br_ref[...], wi_ref, bi_ref[...], sp_lam, a_s, b_s, r_s, i_s, m_s)

        hs_v = hs_ref[...].astype(F32)
        hprev8 = jnp.where(blk > 0, hp_ref[...].astype(F32)[8:16], 0.0)
        h_m1 = _rows_shifted(hprev8, hs_v, 1)
        gg, dgg = _gelu_and_grad(ga_ref[...].astype(F32))
        dya_v = dya_ref[...].astype(F32)
        dz_ref[:, D:2 * D] = (dya_v * hs_v * dgg).astype(BF16)

        a_v = a_s[...]
        a_s[...] = _rows_advanced(a_v, acar_s[...], 1)
        b_s[...] = dya_v * gg

        row = lax.broadcasted_iota(jnp.int32, (8, D), 0)
        ng = ta // 8

        def group(gi, carry):
            off = pl.multiple_of((ng - 1 - gi) * 8, 8)
            c8 = a_s[pl.ds(off, 8), :]
            d8 = b_s[pl.ds(off, 8), :]
            for d in (1, 2, 4):
                c_sh = jnp.where(row < 8 - d, pltpu.roll(c8, 8 - d, 0), 1.0)
                d_sh = jnp.where(row < 8 - d, pltpu.roll(d8, 8 - d, 0), 0.0)
                d8 = c8 * d_sh + d8
                c8 = c8 * c_sh
            dh8 = d8 + c8 * carry
            h_s[pl.ds(off, 8), :] = dh8
            return jnp.broadcast_to(dh8[0:1, :], (8, D))

        dcar_s[...] = lax.fori_loop(0, ng, group, dcar_s[...])
        acar_s[...] = jnp.broadcast_to(a_v[0:1, :], (8, D))

        dbx = h_s[...]
        r_v, i_v, m_v = r_s[...], i_s[...], m_s[...]
        d_mult = dbx * xc * i_v
        d_loga = dbx * h_m1 * a_v - d_mult * (a_v * a_v) / m_v
        d_pr = d_loga * ((-LRU_C) * sp_lam) * r_v * (1.0 - r_v)
        d_pi = dbx * xc * m_v * i_v * (1.0 - i_v)
        vec_ref[7:8, :] += jnp.sum(d_loga * r_v, axis=0, keepdims=True) * (LRU_C * jax.nn.sigmoid(-lam_v))
        vec_ref[5:6, :] += jnp.sum(d_pr, axis=0, keepdims=True)
        vec_ref[6:7, :] += jnp.sum(d_pi, axis=0, keepdims=True)
        d_prb = d_pr.astype(BF16)
        d_pib = d_pi.astype(BF16)
        h_s[...] = dbx * i_v * m_v
        for h in range(HEADS):
            sl = slice(h * HEAD_DIM, (h + 1) * HEAD_DIM)
            h_s[:, sl] += _dot_nt(d_prb[:, sl], wr_ref[h]) + _dot_nt(d_pib[:, sl], wi_ref[h])
            dwr_ref[h] += _dot_tn(xcb[:, sl], d_prb[:, sl])
            dwi_ref[h] += _dot_tn(xcb[:, sl], d_pib[:, sl])
        d_xc = h_s[...]
        vec_ref[4:5, :] += jnp.sum(d_xc, axis=0, keepdims=True)
        vec_ref[0:1, :] += jnp.sum(d_xc * xa, axis=0, keepdims=True)
        d_xa = cw[0:1, :] * d_xc
        nxt = dxc_s[...]
        for k in range(1, CONV_K):
            vec_ref[k:k + 1, :] += jnp.sum(d_xc * _rows_shifted(prev8, xa, k), axis=0, keepdims=True)
            d_xa = d_xa + cw[k:k + 1, :] * _rows_advanced(d_xc, nxt, k)
        dz_ref[:, 0:D] = d_xa.astype(BF16)
        dxc_s[...] = d_xc[0:8, :]

    vec = pl.BlockSpec((1, D), lambda i: (0, 0))
    gate = pl.BlockSpec((HEADS, HEAD_DIM, HEAD_DIM), lambda i: (0, 0, 0))
    cur = lambda c: pl.BlockSpec((ta, D), lambda i: (nb - 1 - i, c))
    before = lambda c: pl.BlockSpec((16, D), lambda i: (jnp.maximum((nb - 1 - i) * per16 - 1, 0), c))
    return _call(
        body, name="branch_a_bwd", grid=(nb,),
        in_specs=[ANY, cur(0), cur(0), before(0), cur(1), cur(0), before(0),
                  pl.BlockSpec((CONV_K, D), lambda i: (0, 0)), vec, gate, vec, gate, vec, vec],
        out_specs=[pl.BlockSpec((ta, 2 * D), lambda i: (nb - 1 - i, 0)), pl.BlockSpec((8, D), lambda i: (0, 0)),
                   gate, gate],
        out_shape=[SDS(dz.shape, BF16), SDS((8, D), F32), SDS((HEADS, HEAD_DIM, HEAD_DIM), F32),
                   SDS((HEADS, HEAD_DIM, HEAD_DIM), F32)],
        scratch_shapes=[pltpu.VMEM((ta, D), F32)] * 6 + [pltpu.VMEM((8, D), F32)] * 3, aliases={0: 0},
        params=_cparams(("arbitrary",), 48),
        args=(dz, dya, z, z, z, hs, hs, conv_w, conv_b, w_r, b_r, w_i, b_i, lam), comm=comm)


def _in_bwd(dz, w_in_g, x, dh1, g_mix, comm=None):
    s = x.shape[0]
    tm = min(TM_ROWS, s)
    nj = N_SLOT

    def body(dz_ref, w_ref, x_ref, dh1_ref, g_ref, dx_ref, dg_ref, acc_s):
        i, j = pl.program_id(0), pl.program_id(1)

        @pl.when(j == 0)
        def _():
            acc_s[...] = jnp.zeros_like(acc_s)

        @pl.when((i == 0) & (j == 0))
        def _():
            dg_ref[...] = jnp.zeros_like(dg_ref)

        acc_s[...] += _dot_nt(dz_ref[...], w_ref[...])

        @pl.when(j == nj - 1)
        def _():
            xv = x_ref[...]
            rstd = lax.rsqrt(jnp.mean(xv * xv, axis=-1, keepdims=True) + NORM_EPS)
            xh = xv * rstd
            dn = acc_s[...]
            dg_ref[...] += jnp.sum(dn * xh, axis=0, keepdims=True)
            dhat = dn * g_ref[...]
            dx_ref[...] = dh1_ref[...] + rstd * (dhat - xh * jnp.mean(dhat * xh, axis=-1, keepdims=True))

    row = pl.BlockSpec((tm, D), lambda i, j: (i, 0))
    vec = pl.BlockSpec((1, D), lambda i, j: (0, 0))
    return _call(
        body, name="in_bwd", grid=(s // tm, nj),
        in_specs=[pl.BlockSpec((tm, W_IN_COLS), lambda i, j: (i, j)),
                  pl.BlockSpec((None, D, W_IN_COLS), lambda i, j: (j, 0, 0)), row, row, vec],
        out_specs=[row, vec],
        out_shape=[SDS((s, D), F32), SDS((1, D), F32)],
        scratch_shapes=[pltpu.VMEM((tm, D), F32)],
        params=_cparams(("arbitrary", "arbitrary"), 48), args=(dz, w_in_g, x, dh1, g_mix), comm=comm)


def _wgrad(name, a, b, nblk, a_split, b_split):
    s = a.shape[0]
    ts = min(TM_ROWS, s)
    a_w = a.shape[1] // nblk if a_split else a.shape[1]
    b_w = b.shape[1] // nblk if b_split else b.shape[1]

    def body(a_ref, b_ref, o_ref, acc_s):
        t = pl.program_id(1)

        @pl.when(t == 0)
        def _():
            acc_s[...] = jnp.zeros_like(acc_s)

        acc_s[...] += _dot_tn(a_ref[...].astype(BF16), b_ref[...].astype(BF16))

        @pl.when(t == pl.num_programs(1) - 1)
        def _():
            o_ref[...] = acc_s[...].astype(BF16)

    return pl.pallas_call(
        body, name=name, grid=(nblk, s // ts),
        in_specs=[pl.BlockSpec((ts, a_w), (lambda k, t: (t, k)) if a_split else (lambda k, t: (t, 0))),
                  pl.BlockSpec((ts, b_w), (lambda k, t: (t, k)) if b_split else (lambda k, t: (t, 0)))],
        out_specs=pl.BlockSpec((None, a_w, b_w), lambda k, t: (k, 0, 0)),
        out_shape=SDS((nblk, a_w, b_w), BF16),
        scratch_shapes=[pltpu.VMEM((a_w, b_w), F32)],
        compiler_params=_cparams(("arbitrary", "arbitrary"), 48),
    )(a, b)


def _wgrad_t(name, a_t, b, nblk, a_split, b_split, tokens):
    s = b.shape[0]
    ts = min(tokens, s)
    a_w = a_t.shape[0] // nblk if a_split else a_t.shape[0]
    b_w = b.shape[1] // nblk if b_split else b.shape[1]

    def body(a_ref, b_ref, o_ref, acc_s):
        t = pl.program_id(1)

        @pl.when(t == 0)
        def _():
            acc_s[...] = jnp.zeros_like(acc_s)

        acc_s[...] += _dot(a_ref[...], b_ref[...].astype(BF16))

        @pl.when(t == pl.num_programs(1) - 1)
        def _():
            o_ref[...] = acc_s[...].astype(BF16)

    return pl.pallas_call(
        body, name=name, grid=(nblk, s // ts),
        in_specs=[pl.BlockSpec((a_w, ts), (lambda k, t: (k, t)) if a_split else (lambda k, t: (0, t))),
                  pl.BlockSpec((ts, b_w), (lambda k, t: (t, k)) if b_split else (lambda k, t: (t, 0)))],
        out_specs=pl.BlockSpec((None, a_w, b_w), lambda k, t: (k, 0, 0)),
        out_shape=SDS((nblk, a_w, b_w), BF16),
        scratch_shapes=[pltpu.VMEM((a_w, b_w), F32)],
        compiler_params=_cparams(("arbitrary", "arbitrary"), 48),
    )(a_t, b)


def _place():
    x, y, c = lax.axis_index("x"), lax.axis_index("y"), lax.axis_index("c")
    return x, y, c


def _other_chips(x, y):
    return [(x, 1 - y, 2 * x + 1 - y), (1 - x, y, 2 * (1 - x) + y), (1 - x, 1 - y, 2 * (1 - x) + 1 - y)]


class _Plan:
    def __init__(self, arrays, out_shape, sems, start, finish):
        self.arrays, self.out_shape, self.sems, self.start, self.finish = arrays, out_shape, sems, start, finish


def _gather_plan(shards):
    n = len(shards)

    def copies(ins, outs, sems):
        send_sems, recv_sems, local_sems = sems
        x, y, c = _place()
        chip = 2 * x + y
        me = 2 * chip + c
        sib = (x, y, 1 - c)
        chips = _other_chips(x, y)

        def rc(k, t, src, blk, to):
            return pltpu.make_async_remote_copy(
                src_ref=src, dst_ref=outs[t].at[blk], send_sem=send_sems.at[k * n + t],
                recv_sem=recv_sems.at[k * n + t], device_id=to, device_id_type=MESH)

        local = [pltpu.make_async_copy(ins[t], outs[t].at[me], local_sems.at[t]) for t in range(n)]
        sends = [rc(0, t, ins[t], me, sib) for t in range(n)]
        for j, (px, py, _) in enumerate(chips):
            sends += [rc(1 + j, t, ins[t], me, (px, py, c)) for t in range(n)]
        return rc, local, sends, chips, chip, c, sib

    def start(ins, outs, sems):
        _, local, sends, _, _, _, _ = copies(ins, outs, sems)
        for cp in local + sends:
            cp.start()

    def finish(ins, outs, sems):
        rc, local, sends, chips, chip, c, sib = copies(ins, outs, sems)
        passed = []
        for j, (px, py, pc) in enumerate(chips):
            blk = 2 * pc + c
            for t in range(n):
                rc(1 + j, t, ins[t], blk, sib).wait_recv()
            for t in range(n):
                cp = rc(4 + j, t, outs[t].at[blk], blk, sib)
                cp.start()
                passed.append(cp)
        for t in range(n):
            rc(0, t, ins[t], 2 * chip + 1 - c, sib).wait_recv()
        for j, (px, py, pc) in enumerate(chips):
            for t in range(n):
                rc(4 + j, t, ins[t], 2 * pc + 1 - c, sib).wait_recv()
        for cp in sends + passed:
            cp.wait_send()
        for cp in local:
            cp.wait()

    return _Plan(list(shards), [SDS((N_SLOT,) + tuple(a.shape), a.dtype) for a in shards],
                 [pltpu.SemaphoreType.DMA((7 * n,)), pltpu.SemaphoreType.DMA((7 * n,)),
                  pltpu.SemaphoreType.DMA((n,))], start, finish)


def _sibling_plan(grads, whole=()):
    n, m = len(grads), len(whole)

    def copies(ins, outs, sems):
        send_sems, recv_sems = sems
        x, y, c = _place()
        sib = (x, y, 1 - c)

        def rc(t, src, dst):
            return pltpu.make_async_remote_copy(src_ref=src, dst_ref=dst, send_sem=send_sems.at[t],
                                                recv_sem=recv_sems.at[t], device_id=sib, device_id_type=MESH)
        return rc, c

    def start(ins, outs, sems):
        rc, c = copies(ins, outs, sems)
        for t in range(n):
            for j in range(4):
                rc(t, ins[t].at[2 * j + 1 - c], outs[t].at[j]).start()
        for t in range(n, n + m):
            rc(t, ins[t], outs[t]).start()

    def finish(ins, outs, sems):
        rc, _ = copies(ins, outs, sems)
        for t in range(n):
            rc(t, ins[t].at[pl.ds(0, 4)], outs[t]).wait()
        for t in range(n, n + m):
            rc(t, ins[t], outs[t]).wait()

    return _Plan(list(grads) + list(whole),
                 [SDS((4,) + tuple(g.shape[1:]), g.dtype) for g in grads] + [SDS(a.shape, a.dtype) for a in whole],
                 [pltpu.SemaphoreType.DMA((n + m,)), pltpu.SemaphoreType.DMA((n + m,))], start, finish)


def _chips_plan(parts, whole=()):
    n, m = len(parts), len(whole)

    def src_of(ins, t, pc):
        return ins[t].at[pc] if t < n else ins[t]

    def local_copies(ins, outs, sems, chip):
        return [pltpu.make_async_copy(src_of(ins, t, chip), outs[t].at[chip], sems[2].at[t]) for t in range(n + m)]

    def start(ins, outs, sems):
        send_sems, recv_sems, _ = sems
        x, y, c = _place()
        chip = 2 * x + y
        for cp in local_copies(ins, outs, sems, chip):
            cp.start()
        for px, py, pc in _other_chips(x, y):
            for t in range(n + m):
                pltpu.make_async_remote_copy(src_ref=src_of(ins, t, pc), dst_ref=outs[t].at[chip],
                                             send_sem=send_sems.at[t], recv_sem=recv_sems.at[t],
                                             device_id=(px, py, c), device_id_type=MESH).start()

    def finish(ins, outs, sems):
        send_sems, recv_sems, _ = sems
        x, y, c = _place()
        for t in range(n + m):
            three = outs[t].at[pl.ds(0, 3)]
            pltpu.make_async_remote_copy(src_ref=three, dst_ref=three, send_sem=send_sems.at[t],
                                         recv_sem=recv_sems.at[t], device_id=(x, y, c), device_id_type=MESH).wait()
        for cp in local_copies(ins, outs, sems, 2 * x + y):
            cp.wait()

    return _Plan(list(parts) + list(whole),
                 [SDS(p.shape, p.dtype) for p in parts] + [SDS((4,) + tuple(a.shape), a.dtype) for a in whole],
                 [pltpu.SemaphoreType.DMA((n + m,)), pltpu.SemaphoreType.DMA((n + m,)),
                  pltpu.SemaphoreType.DMA((n + m,))], start, finish)


def _exchange_plan(arr):
    def peers(x, y, c):
        flip = lambda v, f: 1 - v if f else v
        return [(flip(x, fx), flip(y, fy), flip(c, fc))
                for fx in (0, 1) for fy in (0, 1) for fc in (0, 1) if fx or fy or fc]

    def start(ins, outs, sems):
        x, y, c = _place()
        me = 4 * x + 2 * y + c
        pltpu.make_async_copy(ins[0], outs[0].at[me], sems[2].at[0]).start()
        for to in peers(x, y, c):
            pltpu.make_async_remote_copy(src_ref=ins[0], dst_ref=outs[0].at[me], send_sem=sems[0].at[0],
                                         recv_sem=sems[1].at[0], device_id=to, device_id_type=MESH).start()

    def finish(ins, outs, sems):
        x, y, c = _place()
        seven = outs[0].at[pl.ds(0, 7)]
        pltpu.make_async_remote_copy(src_ref=seven, dst_ref=seven, send_sem=sems[0].at[0], recv_sem=sems[1].at[0],
                                     device_id=(x, y, c), device_id_type=MESH).wait()
        pltpu.make_async_copy(ins[0], outs[0].at[4 * x + 2 * y + c], sems[2].at[0]).wait()

    return _Plan([arr], [SDS((N_SLOT,) + tuple(arr.shape), arr.dtype)],
                 [pltpu.SemaphoreType.DMA((1,)), pltpu.SemaphoreType.DMA((1,)), pltpu.SemaphoreType.DMA((1,))],
                 start, finish)


def _join(*plans):
    def cut(seq, sizes):
        out, at = [], 0
        for k in sizes:
            out.append(seq[at:at + k])
            at += k
        return out

    n_arr = [len(p.arrays) for p in plans]
    n_sem = [len(p.sems) for p in plans]

    def start(ins, outs, sems):
        for p, i, o, s in zip(plans, cut(ins, n_arr), cut(outs, n_arr), cut(sems, n_sem)):
            p.start(i, o, s)

    def finish(ins, outs, sems):
        for p, i, o, s in zip(plans, cut(ins, n_arr), cut(outs, n_arr), cut(sems, n_sem)):
            p.finish(i, o, s)

    return _Plan([a for p in plans for a in p.arrays], [o for p in plans for o in p.out_shape],
                 [s for p in plans for s in p.sems], start, finish)


def _run_plan(name, plan):
    k = len(plan.arrays)

    def body(*refs):
        ins, outs, sems = refs[:k], refs[k:2 * k], refs[2 * k:]
        plan.start(ins, outs, sems)
        plan.finish(ins, outs, sems)

    return pl.pallas_call(
        body, name=name, in_specs=[ANY] * k, out_specs=[ANY] * k, out_shape=plan.out_shape,
        scratch_shapes=plan.sems, compiler_params=pltpu.CompilerParams(has_side_effects=True),
    )(*plan.arrays)


def _call(body, *, name, grid, in_specs, out_specs, out_shape, scratch_shapes, params, args, comm=None,
          aliases=None, prefetch=()):
    aliases = aliases or {}
    n_pre = len(prefetch)

    def launch(fn, ins_specs, outs_specs, outs_shape, scratch, operands):
        spec = pltpu.PrefetchScalarGridSpec(num_scalar_prefetch=n_pre, grid=grid, in_specs=ins_specs,
                                            out_specs=outs_specs, scratch_shapes=scratch)
        return pl.pallas_call(fn, name=name, grid_spec=spec, out_shape=outs_shape, compiler_params=params,
                              input_output_aliases=aliases)(*prefetch, *operands)

    if comm is None:
        return list(launch(body, in_specs, out_specs, out_shape, scratch_shapes, args)), []
    n_in, n_out, n_scr, k = len(in_specs), len(out_specs), len(scratch_shapes), len(comm.arrays)

    def wrapped(*refs):
        pre, refs = refs[:n_pre], refs[n_pre:]
        ins = refs[:n_in]
        c_in = refs[n_in:n_in + k]
        outs = refs[n_in + k:n_in + k + n_out]
        c_out = refs[n_in + k + n_out:n_in + 2 * k + n_out]
        scr = refs[n_in + 2 * k + n_out:n_in + 2 * k + n_out + n_scr]
        sems = refs[n_in + 2 * k + n_out + n_scr:]
        ids = [pl.program_id(d) for d in range(len(grid))]
        first = ids[0] == 0
        last = ids[0] == grid[0] - 1
        for d in range(1, len(grid)):
            first = first & (ids[d] == 0)
            last = last & (ids[d] == grid[d] - 1)

        @pl.when(first)
        def _():
            comm.start(c_in, c_out, sems)

        body(*pre, *ins, *outs, *scr)

        @pl.when(last)
        def _():
            comm.finish(c_in, c_out, sems)

    res = launch(wrapped, list(in_specs) + [ANY] * k, list(out_specs) + [ANY] * k,
                 list(out_shape) + list(comm.out_shape), list(scratch_shapes) + list(comm.sems),
                 tuple(args) + tuple(comm.arrays))
    return list(res[:n_out]), list(res[n_out:])


def _row_tile(rows):
    for t in (512, 256, 128, 64, 32, 16, 8):
        if rows % t == 0:
            return t
    return rows


def _pair_sum(name, g8, recv4, core):
    _, rows, cols = recv4.shape
    tr = _row_tile(rows)
    g42 = g8.reshape(4, 2, rows, cols)

    def body(c_ref, g_ref, r_ref, o_ref):
        del c_ref
        o_ref[...] = (g_ref[...].astype(F32) + r_ref[...].astype(F32)).astype(o_ref.dtype)

    return pl.pallas_call(
        body, name=name,
        grid_spec=pltpu.PrefetchScalarGridSpec(
            num_scalar_prefetch=1, grid=(4, rows // tr),
            in_specs=[pl.BlockSpec((None, None, tr, cols), lambda j, i, c_ref: (j, c_ref[0], i, 0)),
                      pl.BlockSpec((None, tr, cols), lambda j, i, c_ref: (j, i, 0))],
            out_specs=pl.BlockSpec((None, tr, cols), lambda j, i, c_ref: (j, i, 0))),
        out_shape=SDS(recv4.shape, g8.dtype),
        compiler_params=_cparams(("arbitrary", "arbitrary"), 32),
    )(core, g42, recv4)


def _add2(name, a, b):
    rows, cols = a.shape
    tr = _row_tile(rows)

    def body(a_ref, b_ref, o_ref):
        o_ref[...] = a_ref[...] + b_ref[...]

    blk = pl.BlockSpec((tr, cols), lambda i: (i, 0))
    return pl.pallas_call(body, name=name, grid=(rows // tr,), in_specs=[blk, blk], out_specs=blk,
                          out_shape=SDS(a.shape, a.dtype),
                          compiler_params=_cparams(("arbitrary",), 32))(a, b)


def _sum_terms(name, terms):
    k, rows, cols = terms.shape
    tr = _row_tile(rows)

    def body(r_ref, o_ref):
        acc = r_ref[0]
        for q in range(1, k):
            acc = acc + r_ref[q]
        o_ref[...] = acc

    return pl.pallas_call(body, name=name, grid=(rows // tr,),
                          in_specs=[pl.BlockSpec((k, tr, cols), lambda i: (0, i, 0))],
                          out_specs=pl.BlockSpec((tr, cols), lambda i: (i, 0)),
                          out_shape=SDS((rows, cols), terms.dtype),
                          compiler_params=_cparams(("arbitrary",), 32))(terms)


def _adam_update(g, w, m, v):
    c1 = 1.0 / (1.0 - ADAM_B1 ** ADAM_STEP)
    c2 = 1.0 / (1.0 - ADAM_B2 ** ADAM_STEP)
    mn = ADAM_B1 * m + (1.0 - ADAM_B1) * g
    vn = ADAM_B2 * v + (1.0 - ADAM_B2) * (g * g)
    delta = (-ADAM_LR) * ((mn * c1) / (jnp.sqrt(vn * c2) + ADAM_EPS) + ADAM_WD * w)
    return delta, mn, vn


def _adamw_many(name, gs, ws, ms, vs):
    n = len(gs)

    def body(*refs):
        for p in range(n):
            g, w, m, v = (refs[q * n + p][...] for q in range(4))
            d, mn, vn = _adam_update(g, w, m, v)
            refs[4 * n + p][...] = d
            refs[5 * n + p][...] = mn
            refs[6 * n + p][...] = vn

    full = [pl.BlockSpec(memory_space=pltpu.VMEM)] * n
    shapes = [SDS(w.shape, F32) for w in ws]
    res = pl.pallas_call(body, name=name, in_specs=full * 4, out_specs=full * 3, out_shape=shapes * 3,
                         compiler_params=pltpu.CompilerParams(vmem_limit_bytes=32 * MiB))(*gs, *ws, *ms, *vs)
    return [(res[p], res[n + p], res[2 * n + p]) for p in range(n)]


def _adamw(name, terms, w, m, v):
    k, rows, cols = terms.shape
    tr = _row_tile(rows)

    def body(t_ref, w_ref, m_ref, v_ref, g_ref, d_ref, mo_ref, vo_ref):
        g = t_ref[0].astype(F32)
        for q in range(1, k):
            g = g + t_ref[q].astype(F32)
        g_ref[...] = g
        d_ref[...], mo_ref[...], vo_ref[...] = _adam_update(g, w_ref[...], m_ref[...], v_ref[...])

    blk = pl.BlockSpec((tr, cols), lambda i: (i, 0))
    return pl.pallas_call(body, name=name, grid=(rows // tr,),
                          in_specs=[pl.BlockSpec((k, tr, cols), lambda i: (0, i, 0)), blk, blk, blk],
                          out_specs=[blk] * 4, out_shape=[SDS((rows, cols), F32)] * 4,
                          compiler_params=_cparams(("arbitrary",), 40))(terms, w, m, v)


def kernel(x, norm_mix_g, w_in, conv_w, conv_b, w_rgate, b_rgate, w_igate, b_igate, lru_lambda, w_out_a, sgu_ln_g, sgu_ln_b, sgu_w_s, sgu_b_s, w_out_b, w_out, norm_mlp_g, w_up, w_down, norm_final_g, loss_target, m_norm_mix_g, m_w_in, m_conv_w, m_conv_b, m_w_rgate, m_b_rgate, m_w_igate, m_b_igate, m_lru_lambda, m_w_out_a, m_sgu_ln_g, m_sgu_ln_b, m_sgu_w_s, m_sgu_b_s, m_w_out_b, m_w_out, m_norm_mlp_g, m_w_up, m_w_down, m_norm_final_g, v_norm_mix_g, v_w_in, v_conv_w, v_conv_b, v_w_rgate, v_b_rgate, v_w_igate, v_b_igate, v_lru_lambda, v_w_out_a, v_sgu_ln_g, v_sgu_ln_b, v_sgu_w_s, v_sgu_b_s, v_w_out_b, v_w_out, v_norm_mlp_g, v_w_up, v_w_down, v_norm_final_g):
    cx, cy, cc = _place()
    me = 4 * cx + 2 * cy + cc
    core = jnp.reshape(cc, (1,)).astype(jnp.int32)
    xs = x[0]
    tgt = loss_target[0]
    s = xs.shape[0]

    gate_shard = jnp.stack([w_rgate[0], w_igate[0]]).astype(BF16).reshape(2 * HEADS * 32, HEAD_DIM)
    vec_shard = jnp.concatenate([conv_w[0], b_rgate[0], b_igate[0]], axis=1)
    vec_shard = jnp.pad(vec_shard, ((0, 4), (0, 256 - vec_shard.shape[1])))
    shards = [w_in[0].astype(BF16), w_out_a[0].astype(BF16), w_out_b[0].astype(BF16), w_out[0].astype(BF16),
              w_up[0].astype(BF16), w_down[0].astype(BF16), gate_shard, vec_shard]
    (z, n1_t, w_in_g), (gate_g, vec_g) = _in_proj(xs, norm_mix_g, shards[0], _slot_order(cx, cy, cc),
                                                comm=_gather_plan(shards[6:8]))
    gates = gate_g.reshape(N_SLOT, 2, HEADS, 32, HEAD_DIM).transpose(1, 2, 0, 3, 4).reshape(2, HEADS, HEAD_DIM, HEAD_DIM)
    w_r_f, w_i_f = gates[0], gates[1]
    conv_w_f = vec_g[:, 0:4, 0:128].transpose(1, 0, 2).reshape(CONV_K, D)
    b_r_f = vec_g[:, 0:4, 128:160].transpose(1, 0, 2).reshape(1, D)
    b_i_f = vec_g[:, 0:4, 160:192].transpose(1, 0, 2).reshape(1, D)
    b_s_t = jnp.transpose(sgu_b_s[0])

    (ya, hs), (w_oa_g, w_ob_g, w_out_g) = _branch_a_fwd(
        z, conv_w_f, conv_b, w_r_f, b_r_f, w_i_f, b_i_f, lru_lambda, comm=_gather_plan(shards[1:4]))
    w_oa_f = w_oa_g.reshape(D, D)
    w_ob_f = w_ob_g.reshape(D, D)
    w_out_f = w_out_g.reshape(D, D)
    (yb,), (w_up_g,) = _branch_b_fwd(z, sgu_ln_g, sgu_ln_b, sgu_w_s[0], b_s_t, comm=_gather_plan(shards[4:5]))
    (pa, pb, merged, h1), (w_down_g,) = _merge_out(ya, yb, z, xs, w_oa_f, w_ob_f, w_out_f,
                                                   comm=_gather_plan(shards[5:6]))
    w_down_f = w_down_g.reshape(N_SLOT * FF_COLS, D)
    gf2 = norm_final_g.reshape(1, D)
    r_act, act_t, n2_t, dh2, loss_acc, d_gfin = _mlp_fwd(h1, norm_mlp_g, w_up_g, w_down_f, gf2, tgt)

    def pair(names, grads, recv):
        return [_pair_sum("pair_sum_" + nm, g, r, core) for nm, g, r in zip(names, grads, recv)]

    g_down = _wgrad_t("wgrad_down", act_t, dh2, N_SLOT // 2, True, False, 2048)
    g_down = g_down.reshape(N_SLOT, FF_COLS, D)
    (df, dh1, d_gmlp), (r_down,) = _mlp_bwd(dh2, r_act, w_down_f, w_up_g, h1, norm_mlp_g,
                                            comm=_sibling_plan([g_down]))
    (p_down,) = pair(["down"], [g_down], [r_down])
    g_up = _wgrad_t("wgrad_up", n2_t, df, N_SLOT, False, True, 4096)
    g_out = _wgrad("wgrad_out", merged, dh1, 1, False, False).reshape(N_SLOT, D // N_SLOT, D)
    (dz, dpa, dpb, dya, dyb), (got_down, r_up, r_out) = _merge_bwd(
        dh1, z, pa, pb, w_out_f, w_oa_f, w_ob_f, comm=_join(_chips_plan([p_down]), _sibling_plan([g_up, g_out])))
    p_up, p_out = pair(["up", "out"], [g_up, g_out], [r_up, r_out])
    g_oa = _wgrad("wgrad_out_a", ya, dpa, 1, False, False).reshape(N_SLOT, D // N_SLOT, D)
    g_ob = _wgrad("wgrad_out_b", yb, dpb, 1, False, False).reshape(N_SLOT, D // N_SLOT, D)
    (dz, d_ws, d_bs, d_ln), (got_up, r_oa, r_ob) = _branch_b_bwd(
        dz, dyb, z, sgu_ln_g, sgu_ln_b, sgu_w_s[0], b_s_t,
        comm=_join(_chips_plan([p_up]), _sibling_plan([g_oa, g_ob])))
    p_oa, p_ob = pair(["out_a", "out_b"], [g_oa, g_ob], [r_oa, r_ob])
    (dz, d_vec, d_wr, d_wi), (got_out, got_oa, got_ob) = _branch_a_bwd(
        dz, dya, z, hs, conv_w_f, conv_b, w_r_f, b_r_f, w_i_f, b_i_f, lru_lambda,
        comm=_chips_plan([p_out, p_oa, p_ob]))
    g_in = _wgrad_t("wgrad_in", n1_t, dz, N_SLOT, False, True, 4096)
    g_gate = jnp.stack([d_wr, d_wi]).reshape(2, HEADS, N_SLOT, 32, HEAD_DIM).transpose(2, 0, 1, 3, 4)
    g_gate = g_gate.reshape(N_SLOT, 2 * HEADS * 32, HEAD_DIM).astype(BF16)

    d_bs_row = jnp.pad(d_bs[:, :, 0].reshape(1, GROUPS * CHUNK), ((0, 0), (0, D - GROUPS * CHUNK)))
    vecs = jnp.concatenate([d_vec, jnp.concatenate([d_ln[0:2], d_gmlp, d_gfin, d_bs_row, jnp.zeros((3, D), F32)])])
    d_ws2 = d_ws.reshape(GROUPS * CHUNK, CHUNK)
    r_in, r_gate, r_vecs, r_ws = _run_plan("rs_sibling_in", _sibling_plan([g_in, g_gate], [vecs, d_ws2]))
    p_in, p_gate = pair(["in", "gate"], [g_in, g_gate], [r_in, r_gate])
    vecs_chip = _add2("pair_sum_vecs", vecs, r_vecs)
    ws_chip = _add2("pair_sum_ws", d_ws2, r_ws)
    (dx, d_gmix), (got_in, got_gate, got_vecs, got_ws) = _in_bwd(
        dz, w_in_g, xs, dh1, norm_mix_g, comm=_chips_plan([p_in, p_gate], [vecs_chip, ws_chip]))
    vecs_sum = _sum_terms("sum_vecs", got_vecs)
    last = jnp.concatenate([d_gmix, jnp.pad(loss_acc[0:1], ((0, 0), (0, D - 128))), jnp.zeros((6, D), F32)])
    (last_all,) = _run_plan("exchange_last", _exchange_plan(last))
    last_sum = _sum_terms("sum_last", last_all)
    loss = last_sum[1, 0]
    got = [got_in, got_oa, got_ob, got_out, got_up, got_down, got_gate]

    def step(nm, terms, w, m, v, rows, cols):
        g, d, mn, vn = _adamw("adamw_" + nm, terms.reshape(4, rows, cols), w.reshape(rows, cols),
                              m.reshape(rows, cols), v.reshape(rows, cols))
        return [a.reshape(w.shape) for a in (g, d, mn, vn)]

    o_in = step("in", got[0], w_in, m_w_in, v_w_in, D, W_IN_COLS)
    o_oa = step("out_a", got[1], w_out_a, m_w_out_a, v_w_out_a, D // N_SLOT, D)
    o_ob = step("out_b", got[2], w_out_b, m_w_out_b, v_w_out_b, D // N_SLOT, D)
    o_out = step("out", got[3], w_out, m_w_out, v_w_out, D // N_SLOT, D)
    o_up = step("up", got[4], w_up, m_w_up, v_w_up, D, FF_COLS)
    o_down = step("down", got[5], w_down, m_w_down, v_w_down, FF_COLS, D)
    gate_w = jnp.stack([w_rgate[0], w_igate[0]]).reshape(2 * HEADS * 32, HEAD_DIM)
    gate_m = jnp.stack([m_w_rgate[0], m_w_igate[0]]).reshape(2 * HEADS * 32, HEAD_DIM)
    gate_v = jnp.stack([v_w_rgate[0], v_w_igate[0]]).reshape(2 * HEADS * 32, HEAD_DIM)
    o_gate = _adamw("adamw_gate", got[6], gate_w, gate_m, gate_v)
    o_gate = [a.reshape(2, 1, HEADS, 32, HEAD_DIM) for a in o_gate]
    o_wr = [a[0] for a in o_gate]
    o_wi = [a[1] for a in o_gate]

    def own(full, width):
        return lax.dynamic_slice_in_dim(full, me * width, width, axis=1)

    small_g = {
        "norm_mix_g": last_sum[0:1], "conv_w": own(vecs_sum[0:4], 128), "conv_b": vecs_sum[4:5],
        "b_rgate": own(vecs_sum[5:6].reshape(HEADS, HEAD_DIM), 32),
        "b_igate": own(vecs_sum[6:7].reshape(HEADS, HEAD_DIM), 32),
        "lru_lambda": vecs_sum[7:8], "sgu_ln_g": vecs_sum[8:9], "sgu_ln_b": vecs_sum[9:10],
        "norm_mlp_g": vecs_sum[10:11], "norm_final_g": vecs_sum[11:12],
        "sgu_b_s": vecs_sum[12, 0:GROUPS * CHUNK].reshape(GROUPS, CHUNK),
    }
    small_w = {"norm_mix_g": (norm_mix_g, m_norm_mix_g, v_norm_mix_g), "conv_w": (conv_w, m_conv_w, v_conv_w),
               "conv_b": (conv_b, m_conv_b, v_conv_b), "b_rgate": (b_rgate, m_b_rgate, v_b_rgate),
               "b_igate": (b_igate, m_b_igate, v_b_igate), "lru_lambda": (lru_lambda, m_lru_lambda, v_lru_lambda),
               "sgu_ln_g": (sgu_ln_g, m_sgu_ln_g, v_sgu_ln_g), "sgu_ln_b": (sgu_ln_b, m_sgu_ln_b, v_sgu_ln_b),
               "norm_mlp_g": (norm_mlp_g, m_norm_mlp_g, v_norm_mlp_g),
               "norm_final_g": (norm_final_g, m_norm_final_g, v_norm_final_g),
               "sgu_b_s": (sgu_b_s, m_sgu_b_s, v_sgu_b_s), "sgu_w_s": (sgu_w_s, m_sgu_w_s, v_sgu_w_s)}
    order = list(small_g)
    as2d = lambda k, a: a.reshape(small_g[k].shape)
    upd = _adamw_many("adamw_small", [small_g[k] for k in order], *[[as2d(k, small_w[k][q]) for k in order]
                                                                     for q in range(3)])
    o_small = {k: [a.reshape(small_w[k][0].shape) for a in (small_g[k],) + u] for k, u in zip(order, upd)}
    ws3 = [a[0].reshape(GROUPS * CHUNK, CHUNK) for a in small_w.pop("sgu_w_s")]
    o_small["sgu_w_s"] = [a.reshape(sgu_w_s.shape) for a in _adamw("adamw_ws", got_ws, *ws3)]

    per_weight = {"norm_mix_g": o_small["norm_mix_g"], "w_in": o_in, "conv_w": o_small["conv_w"],
                  "conv_b": o_small["conv_b"], "w_rgate": o_wr, "b_rgate": o_small["b_rgate"], "w_igate": o_wi,
                  "b_igate": o_small["b_igate"], "lru_lambda": o_small["lru_lambda"], "w_out_a": o_oa,
                  "sgu_ln_g": o_small["sgu_ln_g"], "sgu_ln_b": o_small["sgu_ln_b"], "sgu_w_s": o_small["sgu_w_s"],
                  "sgu_b_s": o_small["sgu_b_s"], "w_out_b": o_ob, "w_out": o_out, "norm_mlp_g": o_small["norm_mlp_g"],
                  "w_up": o_up, "w_down": o_down, "norm_final_g": o_small["norm_final_g"]}
    names_w = list(per_weight)
    return (loss, dx[None], *[per_weight[k][0] for k in names_w], *[per_weight[k][1] for k in names_w],
            *[per_weight[k][2] for k in names_w], *[per_weight[k][3] for k in names_w])
```

```python
import jax
import jax.numpy as jnp
from jax import lax
from jax.experimental import pallas as pl
from jax.experimental.pallas import tpu as pltpu

F32 = jnp.float32
BF16 = jnp.bfloat16
def SDS(shape, dtype):
    return pltpu.HBM(tuple(shape), dtype)
MESH = pl.DeviceIdType.MESH
ANY = pl.BlockSpec(memory_space=pltpu.HBM)

D = 1024
N_SLOT = 8
W_IN_COLS = 768
FF_COLS = 512
HEADS, HEAD_DIM = 4, 256
GROUPS, GROUP_DIM = 4, 256
CHUNK = 128
CONV_K = 4
NORM_EPS = 1e-6
LN_EPS = 1e-5
LRU_C = 8.0
ADAM_LR, ADAM_B1, ADAM_B2, ADAM_EPS, ADAM_WD, ADAM_STEP = 0.001, 0.9, 0.999, 1e-08, 0.01, 10

TM_ROWS = 1024
TM_MERGE = 512
T_BRANCH = 256
MiB = 1024 * 1024

_GELU_C = 0.7978845608028654
_GELU_A = 0.044715


def _pcall(body, **kw):
    call = pl.pallas_call(body, **kw)

    def run(*args):
        return call(*[a if a.dtype == jnp.int32 else pltpu.with_memory_space_constraint(a, pltpu.HBM)
                      for a in args])

    return run


def _cparams(sem, vmem_mib):
    return pltpu.CompilerParams(dimension_semantics=sem, vmem_limit_bytes=vmem_mib * MiB)


def _gelu(x):
    t = jnp.tanh(_GELU_C * (x + _GELU_A * x * x * x))
    return 0.5 * x * (1.0 + t)


def _gelu_and_grad(x):
    x2 = x * x
    t = jnp.tanh(_GELU_C * x * (1.0 + _GELU_A * x2))
    g = 0.5 * x * (1.0 + t)
    dg = 0.5 * (1.0 + t) + 0.5 * x * (1.0 - t * t) * _GELU_C * (1.0 + 3.0 * _GELU_A * x2)
    return g, dg


def _softplus(x):
    return jnp.maximum(x, 0.0) + jnp.log1p(jnp.exp(-jnp.abs(x)))


def _dot(a, b):
    return jnp.dot(a, b, preferred_element_type=F32)


def _dot_nt(a, b):
    return lax.dot_general(a, b, (((1,), (1,)), ((), ())), preferred_element_type=F32)


def _dot_tn(a, b):
    return lax.dot_general(a, b, (((0,), (0,)), ((), ())), preferred_element_type=F32)


def _rows_shifted(prev8, cur, k):
    ext = jnp.concatenate([prev8, cur], axis=0)
    return pltpu.roll(ext, k, 0)[8:]


def _rows_advanced(cur, next8, k):
    t = cur.shape[0]
    ext = jnp.concatenate([cur, next8], axis=0)
    return pltpu.roll(ext, t + 8 - k, 0)[:t]


def _slot_order(x, y, c):
    chip = 2 * x + y
    order = [2 * chip + c, 2 * chip + 1 - c]
    for _, _, pc in _other_chips(x, y):
        order += [2 * pc + c, 2 * pc + 1 - c]
    return jnp.stack(order).astype(jnp.int32)


def _in_proj(x, g_mix, w_in_own, order, comm=None):
    s = x.shape[0]
    tm = min(TM_ROWS, s)
    ni = s // tm

    def body(order_ref, x_ref, g_ref, own_ref, z_ref, nt_ref, wg_ref, n_s, w_s, send_sems, recv_sems, local_sems):
        j, i = pl.program_id(0), pl.program_id(1)
        px, py, c = _place()
        chip = 2 * px + py
        me = 2 * chip + c
        sib = (px, py, 1 - c)
        chips = _other_chips(px, py)

        def rc(k, src, blk, to):
            return pltpu.make_async_remote_copy(src_ref=src, dst_ref=w_s.at[blk], send_sem=send_sems.at[k],
                                                recv_sem=recv_sems.at[k], device_id=to, device_id_type=MESH)

        own_in = pltpu.make_async_copy(own_ref, w_s.at[me], local_sems.at[0])
        sends = [rc(0, own_ref, me, sib)] + [rc(1 + q, own_ref, me, (qx, qy, c)) for q, (qx, qy, _) in enumerate(chips)]
        passed = [rc(4 + q, w_s.at[2 * pc + c], 2 * pc + c, sib) for q, (_, _, pc) in enumerate(chips)]
        keep = pltpu.make_async_copy(w_s, wg_ref, local_sems.at[1])

        @pl.when((i == 0) & (j == 0))
        def _():
            own_in.start()
            for cp in sends:
                cp.start()
            own_in.wait()

        @pl.when((i == 0) & (j == 1))
        def _():
            rc(0, own_ref, 2 * chip + 1 - c, sib).wait_recv()

        for q, (_, _, pc) in enumerate(chips):
            @pl.when((i == 0) & (j == 2 + 2 * q))
            def _():
                rc(1 + q, own_ref, 2 * pc + c, sib).wait_recv()
                passed[q].start()

            @pl.when((i == 0) & (j == 3 + 2 * q))
            def _():
                rc(4 + q, own_ref, 2 * pc + 1 - c, sib).wait_recv()

        rows = pl.ds(pl.multiple_of(i * tm, tm), tm)

        @pl.when(j == 0)
        def _():
            xv = x_ref[...]
            rstd = lax.rsqrt(jnp.mean(xv * xv, axis=-1, keepdims=True) + NORM_EPS)
            nb = (xv * rstd * g_ref[...]).astype(BF16)
            n_s[rows, :] = nb
            nt_ref[...] = nb.T

        z_ref[...] = _dot(n_s[rows, :], w_s[order_ref[j]]).astype(BF16)

        @pl.when((i == 0) & (j == N_SLOT - 1))
        def _():
            keep.start()

        @pl.when((i == ni - 1) & (j == N_SLOT - 1))
        def _():
            for cp in sends + passed:
                cp.wait_send()
            keep.wait()

    first_pass = lambda j, i, o: (jnp.where(j == 0, i, ni - 1), 0)
    (z, n1, w_in_g), extra = _call(
        body, name="in_proj", grid=(N_SLOT, ni), prefetch=(order,),
        in_specs=[pl.BlockSpec((tm, D), first_pass),
                  pl.BlockSpec((1, D), lambda j, i, o: (0, 0)), ANY],
        out_specs=[pl.BlockSpec((tm, W_IN_COLS), lambda j, i, o: (i, o[j])),
                   pl.BlockSpec((D, tm), lambda j, i, o: (0, jnp.where(j == 0, i, ni - 1))), ANY],
        out_shape=[SDS((s, N_SLOT * W_IN_COLS), BF16), SDS((D, s), BF16), SDS((N_SLOT, D, W_IN_COLS), BF16)],
        scratch_shapes=[pltpu.VMEM((s, D), BF16), pltpu.VMEM((N_SLOT, D, W_IN_COLS), BF16),
                        pltpu.SemaphoreType.DMA((7,)), pltpu.SemaphoreType.DMA((7,)), pltpu.SemaphoreType.DMA((2,))],
        params=_cparams(("arbitrary", "arbitrary"), 56), args=(x, g_mix, w_in_own), comm=comm)
    return (z, n1, w_in_g), extra


def _lru_gates(xc, xcb, wr_ref, br, wi_ref, bi, sp_lam, a_s, b_s, r_s=None, i_s=None, m_s=None):
    for h in range(HEADS):
        sl = slice(h * HEAD_DIM, (h + 1) * HEAD_DIM)
        r = jax.nn.sigmoid(_dot(xcb[:, sl], wr_ref[h]) + br[:, sl])
        ig = jax.nn.sigmoid(_dot(xcb[:, sl], wi_ref[h]) + bi[:, sl])
        log_a = (-LRU_C) * r * sp_lam[:, sl]
        a = jnp.exp(log_a)
        mult = jnp.sqrt(-jnp.tanh(log_a) * (a * a + 1.0))
        a_s[:, sl] = a
        b_s[:, sl] = xc[:, sl] * ig * mult
        if r_s is not None:
            r_s[:, sl] = r
            i_s[:, sl] = ig
            m_s[:, sl] = mult


def _conv_fwd(xa, prev8, cw, cb):
    xc = cb + cw[0:1, :] * xa
    for k in range(1, CONV_K):
        xc = xc + cw[k:k + 1, :] * _rows_shifted(prev8, xa, k)
    return xc


def _branch_a_fwd(z, conv_w, conv_b, w_r, b_r, w_i, b_i, lam, comm=None):
    s = z.shape[0]
    ta = min(T_BRANCH, s)
    per16 = ta // 16

    def body(xa_ref, xp_ref, ga_ref, cw_ref, cb_ref, wr_ref, br_ref, wi_ref, bi_ref, lam_ref,
             ya_ref, hs_ref, a_s, b_s, h_s, carry_s):
        i = pl.program_id(0)

        @pl.when(i == 0)
        def _():
            carry_s[...] = jnp.zeros_like(carry_s)

        xa = xa_ref[...].astype(F32)
        prev8 = jnp.where(i > 0, xp_ref[...].astype(F32)[8:16], 0.0)
        xc = _conv_fwd(xa, prev8, cw_ref[...], cb_ref[...])
        sp_lam = _softplus(-lam_ref[...])
        _lru_gates(xc, xc.astype(BF16), wr_ref, br_ref[...], wi_ref, bi_ref[...], sp_lam, a_s, b_s)

        row = lax.broadcasted_iota(jnp.int32, (8, D), 0)

        def group(g, carry):
            off = pl.multiple_of(g * 8, 8)
            a8 = a_s[pl.ds(off, 8), :]
            b8 = b_s[pl.ds(off, 8), :]
            for d in (1, 2, 4):
                a_sh = jnp.where(row >= d, pltpu.roll(a8, d, 0), 1.0)
                b_sh = jnp.where(row >= d, pltpu.roll(b8, d, 0), 0.0)
                b8 = a8 * b_sh + b8
                a8 = a8 * a_sh
            h8 = b8 + a8 * carry
            h_s[pl.ds(off, 8), :] = h8
            return jnp.broadcast_to(h8[7:8, :], (8, D))

        carry_s[...] = lax.fori_loop(0, ta // 8, group, carry_s[...])
        hs = h_s[...]
        hs_ref[...] = hs.astype(BF16)
        ya_ref[...] = (hs * _gelu(ga_ref[...].astype(F32))).astype(BF16)

    vec = pl.BlockSpec((1, D), lambda i: (0, 0))
    gate = pl.BlockSpec((HEADS, HEAD_DIM, HEAD_DIM), lambda i: (0, 0, 0))
    return _call(
        body, name="branch_a_fwd", grid=(s // ta,),
        in_specs=[pl.BlockSpec((ta, D), lambda i: (i, 0)),
                  pl.BlockSpec((16, D), lambda i: (jnp.maximum(i * per16 - 1, 0), 0)),
                  pl.BlockSpec((ta, D), lambda i: (i, 1)),
                  pl.BlockSpec((CONV_K, D), lambda i: (0, 0)), vec, gate, vec, gate, vec, vec],
        out_specs=[pl.BlockSpec((ta, D), lambda i: (i, 0)), pl.BlockSpec((ta, D), lambda i: (i, 0))],
        out_shape=[SDS((s, D), BF16), SDS((s, D), BF16)],
        scratch_shapes=[pltpu.VMEM((ta, D), F32), pltpu.VMEM((ta, D), F32), pltpu.VMEM((ta, D), F32),
                        pltpu.VMEM((8, D), F32)],
        params=_cparams(("arbitrary",), 40), args=(z, z, z, conv_w, conv_b, w_r, b_r, w_i, b_i, lam), comm=comm)


def _sgu_common(ub, vb, lg, lb, with_grad):
    if with_grad:
        u, du = _gelu_and_grad(ub)
        v, dv = _gelu_and_grad(vb)
    else:
        u, v, du, dv = _gelu(ub), _gelu(vb), None, None
    mu = jnp.mean(v, axis=-1, keepdims=True)
    vc = v - mu
    rstd = lax.rsqrt(jnp.mean(vc * vc, axis=-1, keepdims=True) + LN_EPS)
    vhat = vc * rstd
    vln = vhat * lg + lb
    return u, du, dv, rstd, vhat, vln


def _masked_ws(ws_ref):
    t = lax.broadcasted_iota(jnp.int32, (CHUNK, CHUNK), 0)
    c = lax.broadcasted_iota(jnp.int32, (CHUNK, CHUNK), 1)
    keep = c <= t
    return [jnp.where(keep, ws_ref[g], 0.0).astype(BF16) for g in range(GROUPS)]


def _branch_b_fwd(z, ln_g, ln_b, w_s, b_s_t, comm=None):
    s = z.shape[0]
    tb = min(T_BRANCH, s)

    def body(ub_ref, vb_ref, lg_ref, lb_ref, ws_ref, bs_ref, yb_ref):
        u, _, _, _, _, vln = _sgu_common(ub_ref[...].astype(F32), vb_ref[...].astype(F32),
                                         lg_ref[...], lb_ref[...], False)
        vlnb = vln.astype(BF16)
        wm = _masked_ws(ws_ref)
        bs = bs_ref[...]
        for c in range(tb // CHUNK):
            rs = slice(c * CHUNK, (c + 1) * CHUNK)
            for g in range(GROUPS):
                cs = slice(g * GROUP_DIM, (g + 1) * GROUP_DIM)
                sp = _dot(wm[g], vlnb[rs, cs]) + bs[:, g:g + 1]
                yb_ref[rs, cs] = (u[rs, cs] * sp).astype(BF16)

    vec = pl.BlockSpec((1, D), lambda i: (0, 0))
    return _call(
        body, name="branch_b_fwd", grid=(s // tb,),
        in_specs=[pl.BlockSpec((tb, D), lambda i: (i, 2)), pl.BlockSpec((tb, D), lambda i: (i, 3)), vec, vec,
                  pl.BlockSpec((GROUPS, CHUNK, CHUNK), lambda i: (0, 0, 0)),
                  pl.BlockSpec((CHUNK, GROUPS), lambda i: (0, 0))],
        out_specs=[pl.BlockSpec((tb, D), lambda i: (i, 0))],
        out_shape=[SDS((s, D), BF16)], scratch_shapes=[],
        params=_cparams(("arbitrary",), 40), args=(z, z, ln_g, ln_b, w_s, b_s_t), comm=comm)


def _merge_out(ya, yb, z, x, w_oa, w_ob, w_out, comm=None):
    s = x.shape[0]
    tm = min(TM_MERGE, s)

    def body(ya_ref, yb_ref, ma_ref, mb_ref, x_ref, woa_ref, wob_ref, wo_ref, pa_ref, pb_ref, mg_ref, h1_ref):
        pa = _dot(ya_ref[...], woa_ref[...])
        pb = _dot(yb_ref[...], wob_ref[...])
        merged = (jax.nn.sigmoid(ma_ref[...].astype(F32)) * pa
                  + jax.nn.sigmoid(mb_ref[...].astype(F32)) * pb).astype(BF16)
        pa_ref[...] = pa.astype(BF16)
        pb_ref[...] = pb.astype(BF16)
        mg_ref[...] = merged
        h1_ref[...] = x_ref[...] + _dot(merged, wo_ref[...])

    row = pl.BlockSpec((tm, D), lambda i: (i, 0))
    wsp = pl.BlockSpec((D, D), lambda i: (0, 0))
    return _call(
        body, name="merge_out", grid=(s // tm,),
        in_specs=[row, row, pl.BlockSpec((tm, D), lambda i: (i, 4)), pl.BlockSpec((tm, D), lambda i: (i, 5)),
                  row, wsp, wsp, wsp],
        out_specs=[row, row, row, row],
        out_shape=[SDS((s, D), BF16), SDS((s, D), BF16), SDS((s, D), BF16), SDS((s, D), F32)], scratch_shapes=[],
        params=_cparams(("arbitrary",), 48), args=(ya, yb, z, z, x, w_oa, w_ob, w_out), comm=comm)


def _mlp_fwd(h1, g_mlp, w_up_g, w_down, g_fin, tgt):
    s = h1.shape[0]
    tm = min(TM_ROWS, s)
    nj = N_SLOT

    def body(h1_ref, gm_ref, wu_ref, wd_ref, gf_ref, t_ref, r_ref, at_ref, n2t_ref, dh2_ref, loss_ref, dgf_ref,
             n2_s, acc_s):
        i, j = pl.program_id(0), pl.program_id(1)

        @pl.when(j == 0)
        def _():
            hv = h1_ref[...]
            rstd = lax.rsqrt(jnp.mean(hv * hv, axis=-1, keepdims=True) + NORM_EPS)
            nb = (hv * rstd * gm_ref[...]).astype(BF16)
            n2_s[...] = nb
            n2t_ref[...] = nb.T
            acc_s[...] = jnp.zeros_like(acc_s)

        @pl.when((i == 0) & (j == 0))
        def _():
            loss_ref[...] = jnp.zeros_like(loss_ref)
            dgf_ref[...] = jnp.zeros_like(dgf_ref)

        r = jnp.maximum(_dot(n2_s[...], wu_ref[...]), 0.0)
        r_ref[...] = r.astype(BF16)
        act = (r * r).astype(BF16)
        at_ref[...] = act.T
        acc_s[...] += _dot(act, wd_ref[...])

        @pl.when(j == nj - 1)
        def _():
            h2 = h1_ref[...] + acc_s[...]
            rstd = lax.rsqrt(jnp.mean(h2 * h2, axis=-1, keepdims=True) + NORM_EPS)
            hh = h2 * rstd
            gf = gf_ref[...]
            e = hh * gf - t_ref[...]
            loss_ref[...] += jnp.sum(e * e) * (0.5 / D)
            dy = e * (1.0 / D)
            dgf_ref[...] += jnp.sum(dy * hh, axis=0, keepdims=True)
            dhh = dy * gf
            dh2_ref[...] = rstd * (dhh - hh * jnp.mean(dhh * hh, axis=-1, keepdims=True))

    row = pl.BlockSpec((tm, D), lambda i, j: (i, 0))
    vec = pl.BlockSpec((1, D), lambda i, j: (0, 0))
    return _pcall(
        body, name="mlp_fwd", grid=(s // tm, nj),
        in_specs=[row, vec, pl.BlockSpec((None, D, FF_COLS), lambda i, j: (j, 0, 0)),
                  pl.BlockSpec((FF_COLS, D), lambda i, j: (j, 0)), vec, row],
        out_specs=[pl.BlockSpec((tm, FF_COLS), lambda i, j: (i, j)), pl.BlockSpec((FF_COLS, tm), lambda i, j: (j, i)),
                   pl.BlockSpec((D, tm), lambda i, j: (0, i)), row, pl.BlockSpec((8, 128), lambda i, j: (0, 0)), vec],
        out_shape=[SDS((s, nj * FF_COLS), BF16), SDS((nj * FF_COLS, s), BF16), SDS((D, s), BF16), SDS((s, D), F32),
                   SDS((8, 128), F32), SDS((1, D), F32)],
        scratch_shapes=[pltpu.VMEM((tm, D), BF16), pltpu.VMEM((tm, D), F32)],
        compiler_params=_cparams(("arbitrary", "arbitrary"), 52),
    )(h1, g_mlp, w_up_g, w_down, g_fin, tgt)


def _mlp_bwd(dh2, r, w_down, w_up_g, h1, g_mlp, comm=None):
    s = h1.shape[0]
    tm = min(TM_ROWS, s)
    nj = N_SLOT

    def body(dh2_ref, r_ref, wd_ref, wu_ref, h1_ref, gm_ref, df_ref, dh1_ref, dgm_ref, dh2b_s, acc_s):
        i, j = pl.program_id(0), pl.program_id(1)

        @pl.when(j == 0)
        def _():
            dh2b_s[...] = dh2_ref[...].astype(BF16)
            acc_s[...] = jnp.zeros_like(acc_s)

        @pl.when((i == 0) & (j == 0))
        def _():
            dgm_ref[...] = jnp.zeros_like(dgm_ref)

        d_act = _dot_nt(dh2b_s[...], wd_ref[...])
        df = (d_act * (2.0 * r_ref[...].astype(F32))).astype(BF16)
        df_ref[...] = df
        acc_s[...] += _dot_nt(df, wu_ref[...])

        @pl.when(j == nj - 1)
        def _():
            hv = h1_ref[...]
            rstd = lax.rsqrt(jnp.mean(hv * hv, axis=-1, keepdims=True) + NORM_EPS)
            hh = hv * rstd
            dn2 = acc_s[...]
            dgm_ref[...] += jnp.sum(dn2 * hh, axis=0, keepdims=True)
            dhat = dn2 * gm_ref[...]
            dh1_ref[...] = dh2_ref[...] + rstd * (dhat - hh * jnp.mean(dhat * hh, axis=-1, keepdims=True))

    row = pl.BlockSpec((tm, D), lambda i, j: (i, 0))
    vec = pl.BlockSpec((1, D), lambda i, j: (0, 0))
    ffb = pl.BlockSpec((tm, FF_COLS), lambda i, j: (i, j))
    return _call(
        body, name="mlp_bwd", grid=(s // tm, nj),
        in_specs=[row, ffb, pl.BlockSpec((FF_COLS, D), lambda i, j: (j, 0)),
                  pl.BlockSpec((None, D, FF_COLS), lambda i, j: (j, 0, 0)), row, vec],
        out_specs=[ffb, row, vec],
        out_shape=[SDS((s, nj * FF_COLS), BF16), SDS((s, D), F32), SDS((1, D), F32)],
        scratch_shapes=[pltpu.VMEM((tm, D), BF16), pltpu.VMEM((tm, D), F32)],
        params=_cparams(("arbitrary", "arbitrary"), 52), args=(dh2, r, w_down, w_up_g, h1, g_mlp), comm=comm)


def _merge_bwd(dh1, z, pa, pb, w_out, w_oa, w_ob, comm=None):
    s = dh1.shape[0]
    tm = min(TM_MERGE, s)

    def body(dh1_ref, ma_ref, mb_ref, pa_ref, pb_ref, wo_ref, woa_ref, wob_ref,
             dz_ref, dpa_ref, dpb_ref, dya_ref, dyb_ref):
        dm = _dot_nt(dh1_ref[...].astype(BF16), wo_ref[...])
        sa = jax.nn.sigmoid(ma_ref[...].astype(F32))
        sb = jax.nn.sigmoid(mb_ref[...].astype(F32))
        dpa = (dm * sa).astype(BF16)
        dpb = (dm * sb).astype(BF16)
        dz_ref[:, 0:D] = (dm * pa_ref[...].astype(F32) * sa * (1.0 - sa)).astype(BF16)
        dz_ref[:, D:2 * D] = (dm * pb_ref[...].astype(F32) * sb * (1.0 - sb)).astype(BF16)
        dpa_ref[...] = dpa
        dpb_ref[...] = dpb
        dya_ref[...] = _dot_nt(dpa, woa_ref[...]).astype(BF16)
        dyb_ref[...] = _dot_nt(dpb, wob_ref[...]).astype(BF16)

    row = pl.BlockSpec((tm, D), lambda i: (i, 0))
    wsp = pl.BlockSpec((D, D), lambda i: (0, 0))
    return _call(
        body, name="merge_bwd", grid=(s // tm,),
        in_specs=[row, pl.BlockSpec((tm, D), lambda i: (i, 4)), pl.BlockSpec((tm, D), lambda i: (i, 5)),
                  row, row, wsp, wsp, wsp],
        out_specs=[pl.BlockSpec((tm, 2 * D), lambda i: (i, 2)), row, row, row, row],
        out_shape=[SDS((s, 6 * D), BF16)] + [SDS((s, D), BF16)] * 4, scratch_shapes=[],
        params=_cparams(("arbitrary",), 48), args=(dh1, z, z, pa, pb, w_out, w_oa, w_ob), comm=comm)


def _branch_b_bwd(dz, dyb, z, ln_g, ln_b, w_s, b_s_t, comm=None):
    s = z.shape[0]
    tb = min(T_BRANCH, s)

    def body(dz_in, dyb_ref, ub_ref, vb_ref, lg_ref, lb_ref, ws_ref, bs_ref,
             dz_ref, dws_ref, dbs_ref, dln_ref, du_s, dvln_s):
        del dz_in

        @pl.when(pl.program_id(0) == 0)
        def _():
            dws_ref[...] = jnp.zeros_like(dws_ref)
            dbs_ref[...] = jnp.zeros_like(dbs_ref)
            dln_ref[...] = jnp.zeros_like(dln_ref)

        lg = lg_ref[...]
        u, du, dv, rstd, vhat, vln = _sgu_common(ub_ref[...].astype(F32), vb_ref[...].astype(F32),
                                                 lg, lb_ref[...], True)
        vlnb = vln.astype(BF16)
        dyb_v = dyb_ref[...].astype(F32)
        wm = _masked_ws(ws_ref)
        keep = (lax.broadcasted_iota(jnp.int32, (CHUNK, CHUNK), 1)
                <= lax.broadcasted_iota(jnp.int32, (CHUNK, CHUNK), 0))
        bs = bs_ref[...]
        for c in range(tb // CHUNK):
            rs = slice(c * CHUNK, (c + 1) * CHUNK)
            for g in range(GROUPS):
                cs = slice(g * GROUP_DIM, (g + 1) * GROUP_DIM)
                v_blk = vlnb[rs, cs]
                sp = _dot(wm[g], v_blk) + bs[:, g:g + 1]
                d_sp = dyb_v[rs, cs] * u[rs, cs]
                d_spb = d_sp.astype(BF16)
                du_s[rs, cs] = dyb_v[rs, cs] * sp
                dvln_s[rs, cs] = _dot_tn(wm[g], d_spb)
                dws_ref[g] += jnp.where(keep, _dot_nt(d_spb, v_blk), 0.0)
                dbs_ref[g] += jnp.broadcast_to(jnp.sum(d_sp, axis=-1, keepdims=True), (CHUNK, CHUNK))
        dvln = dvln_s[...]
        dln_ref[0:1, :] += jnp.sum(dvln * vhat, axis=0, keepdims=True)
        dln_ref[1:2, :] += jnp.sum(dvln, axis=0, keepdims=True)
        dvh = dvln * lg
        d_v = rstd * (dvh - jnp.mean(dvh, axis=-1, keepdims=True)
                      - vhat * jnp.mean(dvh * vhat, axis=-1, keepdims=True))
        dz_ref[:, 0:D] = (du_s[...] * du).astype(BF16)
        dz_ref[:, D:2 * D] = (d_v * dv).astype(BF16)

    vec = pl.BlockSpec((1, D), lambda i: (0, 0))
    sq = pl.BlockSpec((GROUPS, CHUNK, CHUNK), lambda i: (0, 0, 0))
    return _call(
        body, name="branch_b_bwd", grid=(s // tb,),
        in_specs=[ANY, pl.BlockSpec((tb, D), lambda i: (i, 0)),
                  pl.BlockSpec((tb, D), lambda i: (i, 2)), pl.BlockSpec((tb, D), lambda i: (i, 3)), vec, vec, sq,
                  pl.BlockSpec((CHUNK, GROUPS), lambda i: (0, 0))],
        out_specs=[pl.BlockSpec((tb, 2 * D), lambda i: (i, 1)), sq, sq, pl.BlockSpec((8, D), lambda i: (0, 0))],
        out_shape=[SDS(dz.shape, BF16), SDS((GROUPS, CHUNK, CHUNK), F32), SDS((GROUPS, CHUNK, CHUNK), F32),
                   SDS((8, D), F32)],
        scratch_shapes=[pltpu.VMEM((tb, D), F32), pltpu.VMEM((tb, D), F32)], aliases={0: 0},
        params=_cparams(("arbitrary",), 40), args=(dz, dyb, z, z, ln_g, ln_b, w_s, b_s_t), comm=comm)


def _branch_a_bwd(dz, dya, z, hs, conv_w, conv_b, w_r, b_r, w_i, b_i, lam, comm=None):
    s = z.shape[0]
    ta = min(T_BRANCH, s)
    nb = s // ta
    per16 = ta // 16

    def body(dz_in, dya_ref, xa_ref, xp_ref, ga_ref, hs_ref, hp_ref, cw_ref, cb_ref, wr_ref, br_ref, wi_ref,
             bi_ref, lam_ref, dz_ref, vec_ref, dwr_ref, dwi_ref,
             a_s, b_s, h_s, r_s, i_s, m_s, dcar_s, acar_s, dxc_s):
        del dz_in
        i = pl.program_id(0)
        blk = nb - 1 - i

        @pl.when(i == 0)
        def _():
            dcar_s[...] = jnp.zeros_like(dcar_s)
            acar_s[...] = jnp.zeros_like(acar_s)
            dxc_s[...] = jnp.zeros_like(dxc_s)
            vec_ref[...] = jnp.zeros_like(vec_ref)
            dwr_ref[...] = jnp.zeros_like(dwr_ref)
            dwi_ref[...] = jnp.zeros_like(dwi_ref)

        cw = cw_ref[...]
        lam_v = lam_ref[...]
        xa = xa_ref[...].astype(F32)
        prev8 = jnp.where(blk > 0, xp_ref[...].astype(F32)[8:16], 0.0)
        xc = _conv_fwd(xa, prev8, cw, cb_ref[...])
        xcb = xc.astype(BF16)
        sp_lam = _softplus(-lam_v)
        _lru_gates(xc, xcb, wr_ref, br_ref[...], wi_ref, bi_ref[...], sp_lam, a_s, b_s, r_s, i_s, m_s)

        hs_v = hs_ref[...].astype(F32)
        hprev8 = jnp.where(blk > 0, hp_ref[...].astype(F32)[8:16], 0.0)
        h_m1 = _rows_shifted(hprev8, hs_v, 1)
        gg, dgg = _gelu_and_grad(ga_ref[...].astype(F32))
        dya_v = dya_ref[...].astype(F32)
        dz_ref[:, D:2 * D] = (dya_v * hs_v * dgg).astype(BF16)

        a_v = a_s[...]
        a_s[...] = _rows_advanced(a_v, acar_s[...], 1)
        b_s[...] = dya_v * gg

        row = lax.broadcasted_iota(jnp.int32, (8, D), 0)
        ng = ta // 8

        def group(gi, carry):
            off = pl.multiple_of((ng - 1 - gi) * 8, 8)
            c8 = a_s[pl.ds(off, 8), :]
            d8 = b_s[pl.ds(off, 8), :]
            for d in (1, 2, 4):
                c_sh = jnp.where(row < 8 - d, pltpu.roll(c8, 8 - d, 0), 1.0)
                d_sh = jnp.where(row < 8 - d, pltpu.roll(d8, 8 - d, 0), 0.0)
                d8 = c8 * d_sh + d8
                c8 = c8 * c_sh
            dh8 = d8 + c8 * carry
            h_s[pl.ds(off, 8), :] = dh8
            return jnp.broadcast_to(dh8[0:1, :], (8, D))

        dcar_s[...] = lax.fori_loop(0, ng, group, dcar_s[...])
        acar_s[...] = jnp.broadcast_to(a_v[0:1, :], (8, D))

        dbx = h_s[...]
        r_v, i_v, m_v = r_s[...], i_s[...], m_s[...]
        d_mult = dbx * xc * i_v
        d_loga = dbx * h_m1 * a_v - d_mult * (a_v * a_v) / m_v
        d_pr = d_loga * ((-LRU_C) * sp_lam) * r_v * (1.0 - r_v)
        d_pi = dbx * xc * m_v * i_v * (1.0 - i_v)
        vec_ref[7:8, :] += jnp.sum(d_loga * r_v, axis=0, keepdims=True) * (LRU_C * jax.nn.sigmoid(-lam_v))
        vec_ref[5:6, :] += jnp.sum(d_pr, axis=0, keepdims=True)
        vec_ref[6:7, :] += jnp.sum(d_pi, axis=0, keepdims=True)
        d_prb = d_pr.astype(BF16)
        d_pib = d_pi.astype(BF16)
        h_s[...] = dbx * i_v * m_v
        for h in range(HEADS):
            sl = slice(h * HEAD_DIM, (h + 1) * HEAD_DIM)
            h_s[:, sl] += _dot_nt(d_prb[:, sl], wr_ref[h]) + _dot_nt(d_pib[:, sl], wi_ref[h])
            dwr_ref[h] += _dot_tn(xcb[:, sl], d_prb[:, sl])
            dwi_ref[h] += _dot_tn(xcb[:, sl], d_pib[:, sl])
        d_xc = h_s[...]
        vec_ref[4:5, :] += jnp.sum(d_xc, axis=0, keepdims=True)
        vec_ref[0:1, :] += jnp.sum(d_xc * xa, axis=0, keepdims=True)
        d_xa = cw[0:1, :] * d_xc
        nxt = dxc_s[...]
        for k in range(1, CONV_K):
            vec_ref[k:k + 1, :] += jnp.sum(d_xc * _rows_shifted(prev8, xa, k), axis=0, keepdims=True)
            d_xa = d_xa + cw[k:k + 1, :] * _rows_advanced(d_xc, nxt, k)
        dz_ref[:, 0:D] = d_xa.astype(BF16)
        dxc_s[...] = d_xc[0:8, :]

    vec = pl.BlockSpec((1, D), lambda i: (0, 0))
    gate = pl.BlockSpec((HEADS, HEAD_DIM, HEAD_DIM), lambda i: (0, 0, 0))
    cur = lambda c: pl.BlockSpec((ta, D), lambda i: (nb - 1 - i, c))
    before = lambda c: pl.BlockSpec((16, D), lambda i: (jnp.maximum((nb - 1 - i) * per16 - 1, 0), c))
    return _call(
        body, name="branch_a_bwd", grid=(nb,),
        in_specs=[ANY, cur(0), cur(0), before(0), cur(1), cur(0), before(0),
                  pl.BlockSpec((CONV_K, D), lambda i: (0, 0)), vec, gate, vec, gate, vec, vec],
        out_specs=[pl.BlockSpec((ta, 2 * D), lambda i: (nb - 1 - i, 0)), pl.BlockSpec((8, D), lambda i: (0, 0)),
                   gate, gate],
        out_shape=[SDS(dz.shape, BF16), SDS((8, D), F32), SDS((HEADS, HEAD_DIM, HEAD_DIM), F32),
                   SDS((HEADS, HEAD_DIM, HEAD_DIM), F32)],
        scratch_shapes=[pltpu.VMEM((ta, D), F32)] * 6 + [pltpu.VMEM((8, D), F32)] * 3, aliases={0: 0},
        params=_cparams(("arbitrary",), 48),
        args=(dz, dya, z, z, z, hs, hs, conv_w, conv_b, w_r, b_r, w_i, b_i, lam), comm=comm)


def _in_bwd(dz, w_in_g, x, dh1, g_mix, comm=None):
    s = x.shape[0]
    tm = min(TM_ROWS, s)
    nj = N_SLOT

    def body(dz_ref, w_ref, x_ref, dh1_ref, g_ref, dx_ref, dg_ref, acc_s):
        i, j = pl.program_id(0), pl.program_id(1)

        @pl.when(j == 0)
        def _():
            acc_s[...] = jnp.zeros_like(acc_s)

        @pl.when((i == 0) & (j == 0))
        def _():
            dg_ref[...] = jnp.zeros_like(dg_ref)

        acc_s[...] += _dot_nt(dz_ref[...], w_ref[...])

        @pl.when(j == nj - 1)
        def _():
            xv = x_ref[...]
            rstd = lax.rsqrt(jnp.mean(xv * xv, axis=-1, keepdims=True) + NORM_EPS)
            xh = xv * rstd
            dn = acc_s[...]
            dg_ref[...] += jnp.sum(dn * xh, axis=0, keepdims=True)
            dhat = dn * g_ref[...]
            dx_ref[...] = dh1_ref[...] + rstd * (dhat - xh * jnp.mean(dhat * xh, axis=-1, keepdims=True))

    row = pl.BlockSpec((tm, D), lambda i, j: (i, 0))
    vec = pl.BlockSpec((1, D), lambda i, j: (0, 0))
    return _call(
        body, name="in_bwd", grid=(s // tm, nj),
        in_specs=[pl.BlockSpec((tm, W_IN_COLS), lambda i, j: (i, j)),
                  pl.BlockSpec((None, D, W_IN_COLS), lambda i, j: (j, 0, 0)), row, row, vec],
        out_specs=[row, vec],
        out_shape=[SDS((s, D), F32), SDS((1, D), F32)],
        scratch_shapes=[pltpu.VMEM((tm, D), F32)],
        params=_cparams(("arbitrary", "arbitrary"), 48), args=(dz, w_in_g, x, dh1, g_mix), comm=comm)


def _wgrad(name, a, b, nblk, a_split, b_split):
    s = a.shape[0]
    ts = min(TM_ROWS, s)
    a_w = a.shape[1] // nblk if a_split else a.shape[1]
    b_w = b.shape[1] // nblk if b_split else b.shape[1]

    def body(a_ref, b_ref, o_ref, acc_s):
        t = pl.program_id(1)

        @pl.when(t == 0)
        def _():
            acc_s[...] = jnp.zeros_like(acc_s)

        acc_s[...] += _dot_tn(a_ref[...].astype(BF16), b_ref[...].astype(BF16))

        @pl.when(t == pl.num_programs(1) - 1)
        def _():
            o_ref[...] = acc_s[...].astype(BF16)

    return _pcall(
        body, name=name, grid=(nblk, s // ts),
        in_specs=[pl.BlockSpec((ts, a_w), (lambda k, t: (t, k)) if a_split else (lambda k, t: (t, 0))),
                  pl.BlockSpec((ts, b_w), (lambda k, t: (t, k)) if b_split else (lambda k, t: (t, 0)))],
        out_specs=pl.BlockSpec((None, a_w, b_w), lambda k, t: (k, 0, 0)),
        out_shape=SDS((nblk, a_w, b_w), BF16),
        scratch_shapes=[pltpu.VMEM((a_w, b_w), F32)],
        compiler_params=_cparams(("arbitrary", "arbitrary"), 48),
    )(a, b)


def _wgrad_t(name, a_t, b, nblk, a_split, b_split, tokens):
    s = b.shape[0]
    ts = min(tokens, s)
    a_w = a_t.shape[0] // nblk if a_split else a_t.shape[0]
    b_w = b.shape[1] // nblk if b_split else b.shape[1]

    def body(a_ref, b_ref, o_ref, acc_s):
        t = pl.program_id(1)

        @pl.when(t == 0)
        def _():
            acc_s[...] = jnp.zeros_like(acc_s)

        acc_s[...] += _dot(a_ref[...], b_ref[...].astype(BF16))

        @pl.when(t == pl.num_programs(1) - 1)
        def _():
            o_ref[...] = acc_s[...].astype(BF16)

    return _pcall(
        body, name=name, grid=(nblk, s // ts),
        in_specs=[pl.BlockSpec((a_w, ts), (lambda k, t: (k, t)) if a_split else (lambda k, t: (0, t))),
                  pl.BlockSpec((ts, b_w), (lambda k, t: (t, k)) if b_split else (lambda k, t: (t, 0)))],
        out_specs=pl.BlockSpec((None, a_w, b_w), lambda k, t: (k, 0, 0)),
        out_shape=SDS((nblk, a_w, b_w), BF16),
        scratch_shapes=[pltpu.VMEM((a_w, b_w), F32)],
        compiler_params=_cparams(("arbitrary", "arbitrary"), 48),
    )(a_t, b)


def _place():
    x, y, c = lax.axis_index("x"), lax.axis_index("y"), lax.axis_index("c")
    return x, y, c


def _other_chips(x, y):
    return [(x, 1 - y, 2 * x + 1 - y), (1 - x, y, 2 * (1 - x) + y), (1 - x, 1 - y, 2 * (1 - x) + 1 - y)]


class _Plan:
    def __init__(self, arrays, out_shape, sems, start, finish):
        self.arrays, self.out_shape, self.sems, self.start, self.finish = arrays, out_shape, sems, start, finish


def _gather_plan(shards):
    n = len(shards)

    def copies(ins, outs, sems):
        send_sems, recv_sems, local_sems = sems
        x, y, c = _place()
        chip = 2 * x + y
        me = 2 * chip + c
        sib = (x, y, 1 - c)
        chips = _other_chips(x, y)

        def rc(k, t, src, blk, to):
            return pltpu.make_async_remote_copy(
                src_ref=src, dst_ref=outs[t].at[blk], send_sem=send_sems.at[k * n + t],
                recv_sem=recv_sems.at[k * n + t], device_id=to, device_id_type=MESH)

        local = [pltpu.make_async_copy(ins[t], outs[t].at[me], local_sems.at[t]) for t in range(n)]
        sends = [rc(0, t, ins[t], me, sib) for t in range(n)]
        for j, (px, py, _) in enumerate(chips):
            sends += [rc(1 + j, t, ins[t], me, (px, py, c)) for t in range(n)]
        return rc, local, sends, chips, chip, c, sib

    def start(ins, outs, sems):
        _, local, sends, _, _, _, _ = copies(ins, outs, sems)
        for cp in local + sends:
            cp.start()

    def finish(ins, outs, sems):
        rc, local, sends, chips, chip, c, sib = copies(ins, outs, sems)
        passed = []
        for j, (px, py, pc) in enumerate(chips):
            blk = 2 * pc + c
            for t in range(n):
                rc(1 + j, t, ins[t], blk, sib).wait_recv()
            for t in range(n):
                cp = rc(4 + j, t, outs[t].at[blk], blk, sib)
                cp.start()
                passed.append(cp)
        for t in range(n):
            rc(0, t, ins[t], 2 * chip + 1 - c, sib).wait_recv()
        for j, (px, py, pc) in enumerate(chips):
            for t in range(n):
                rc(4 + j, t, ins[t], 2 * pc + 1 - c, sib).wait_recv()
        for cp in sends + passed:
            cp.wait_send()
        for cp in local:
            cp.wait()

    return _Plan(list(shards), [SDS((N_SLOT,) + tuple(a.shape), a.dtype) for a in shards],
                 [pltpu.SemaphoreType.DMA((7 * n,)), pltpu.SemaphoreType.DMA((7 * n,)),
                  pltpu.SemaphoreType.DMA((n,))], start, finish)


def _sibling_plan(grads, whole=()):
    n, m = len(grads), len(whole)

    def copies(ins, outs, sems):
        send_sems, recv_sems = sems
        x, y, c = _place()
        sib = (x, y, 1 - c)

        def rc(t, src, dst):
            return pltpu.make_async_remote_copy(src_ref=src, dst_ref=dst, send_sem=send_sems.at[t],
                                                recv_sem=recv_sems.at[t], device_id=sib, device_id_type=MESH)
        return rc, c

    def start(ins, outs, sems):
        rc, c = copies(ins, outs, sems)
        for t in range(n):
            for j in range(4):
                rc(t, ins[t].at[2 * j + 1 - c], outs[t].at[j]).start()
        for t in range(n, n + m):
            rc(t, ins[t], outs[t]).start()

    def finish(ins, outs, sems):
        rc, _ = copies(ins, outs, sems)
        for t in range(n):
            rc(t, ins[t].at[pl.ds(0, 4)], outs[t]).wait()
        for t in range(n, n + m):
            rc(t, ins[t], outs[t]).wait()

    return _Plan(list(grads) + list(whole),
                 [SDS((4,) + tuple(g.shape[1:]), g.dtype) for g in grads] + [SDS(a.shape, a.dtype) for a in whole],
                 [pltpu.SemaphoreType.DMA((n + m,)), pltpu.SemaphoreType.DMA((n + m,))], start, finish)


def _chips_plan(parts, whole=()):
    n, m = len(parts), len(whole)

    def src_of(ins, t, pc):
        return ins[t].at[pc] if t < n else ins[t]

    def local_copies(ins, outs, sems, chip):
        return [pltpu.make_async_copy(src_of(ins, t, chip), outs[t].at[chip], sems[2].at[t]) for t in range(n + m)]

    def start(ins, outs, sems):
        send_sems, recv_sems, _ = sems
        x, y, c = _place()
        chip = 2 * x + y
        for cp in local_copies(ins, outs, sems, chip):
            cp.start()
        for px, py, pc in _other_chips(x, y):
            for t in range(n + m):
                pltpu.make_async_remote_copy(src_ref=src_of(ins, t, pc), dst_ref=outs[t].at[chip],
                                             send_sem=send_sems.at[t], recv_sem=recv_sems.at[t],
                                             device_id=(px, py, c), device_id_type=MESH).start()

    def finish(ins, outs, sems):
        send_sems, recv_sems, _ = sems
        x, y, c = _place()
        for t in range(n + m):
            three = outs[t].at[pl.ds(0, 3)]
            pltpu.make_async_remote_copy(src_ref=three, dst_ref=three, send_sem=send_sems.at[t],
                                         recv_sem=recv_sems.at[t], device_id=(x, y, c), device_id_type=MESH).wait()
        for cp in local_copies(ins, outs, sems, 2 * x + y):
            cp.wait()

    return _Plan(list(parts) + list(whole),
                 [SDS(p.shape, p.dtype) for p in parts] + [SDS((4,) + tuple(a.shape), a.dtype) for a in whole],
                 [pltpu.SemaphoreType.DMA((n + m,)), pltpu.SemaphoreType.DMA((n + m,)),
                  pltpu.SemaphoreType.DMA((n + m,))], start, finish)


def _exchange_plan(arr):
    def peers(x, y, c):
        flip = lambda v, f: 1 - v if f else v
        return [(flip(x, fx), flip(y, fy), flip(c, fc))
                for fx in (0, 1) for fy in (0, 1) for fc in (0, 1) if fx or fy or fc]

    def start(ins, outs, sems):
        x, y, c = _place()
        me = 4 * x + 2 * y + c
        pltpu.make_async_copy(ins[0], outs[0].at[me], sems[2].at[0]).start()
        for to in peers(x, y, c):
            pltpu.make_async_remote_copy(src_ref=ins[0], dst_ref=outs[0].at[me], send_sem=sems[0].at[0],
                                         recv_sem=sems[1].at[0], device_id=to, device_id_type=MESH).start()

    def finish(ins, outs, sems):
        x, y, c = _place()
        seven = outs[0].at[pl.ds(0, 7)]
        pltpu.make_async_remote_copy(src_ref=seven, dst_ref=seven, send_sem=sems[0].at[0], recv_sem=sems[1].at[0],
                                     device_id=(x, y, c), device_id_type=MESH).wait()
        pltpu.make_async_copy(ins[0], outs[0].at[4 * x + 2 * y + c], sems[2].at[0]).wait()

    return _Plan([arr], [SDS((N_SLOT,) + tuple(arr.shape), arr.dtype)],
                 [pltpu.SemaphoreType.DMA((1,)), pltpu.SemaphoreType.DMA((1,)), pltpu.SemaphoreType.DMA((1,))],
                 start, finish)


def _join(*plans):
    def cut(seq, sizes):
        out, at = [], 0
        for k in sizes:
            out.append(seq[at:at + k])
            at += k
        return out

    n_arr = [len(p.arrays) for p in plans]
    n_sem = [len(p.sems) for p in plans]

    def start(ins, outs, sems):
        for p, i, o, s in zip(plans, cut(ins, n_arr), cut(outs, n_arr), cut(sems, n_sem)):
            p.start(i, o, s)

    def finish(ins, outs, sems):
        for p, i, o, s in zip(plans, cut(ins, n_arr), cut(outs, n_arr), cut(sems, n_sem)):
            p.finish(i, o, s)

    return _Plan([a for p in plans for a in p.arrays], [o for p in plans for o in p.out_shape],
                 [s for p in plans for s in p.sems], start, finish)


def _run_plan(name, plan):
    k = len(plan.arrays)

    def body(*refs):
        ins, outs, sems = refs[:k], refs[k:2 * k], refs[2 * k:]
        plan.start(ins, outs, sems)
        plan.finish(ins, outs, sems)

    return _pcall(
        body, name=name, in_specs=[ANY] * k, out_specs=[ANY] * k, out_shape=plan.out_shape,
        scratch_shapes=plan.sems, compiler_params=pltpu.CompilerParams(has_side_effects=True),
    )(*plan.arrays)


def _call(body, *, name, grid, in_specs, out_specs, out_shape, scratch_shapes, params, args, comm=None,
          aliases=None, prefetch=()):
    aliases = aliases or {}
    n_pre = len(prefetch)

    def launch(fn, ins_specs, outs_specs, outs_shape, scratch, operands):
        spec = pltpu.PrefetchScalarGridSpec(num_scalar_prefetch=n_pre, grid=grid, in_specs=ins_specs,
                                            out_specs=outs_specs, scratch_shapes=scratch)
        return _pcall(fn, name=name, grid_spec=spec, out_shape=outs_shape, compiler_params=params,
                              input_output_aliases=aliases)(*prefetch, *operands)

    if comm is None:
        return list(launch(body, in_specs, out_specs, out_shape, scratch_shapes, args)), []
    n_in, n_out, n_scr, k = len(in_specs), len(out_specs), len(scratch_shapes), len(comm.arrays)

    def wrapped(*refs):
        pre, refs = refs[:n_pre], refs[n_pre:]
        ins = refs[:n_in]
        c_in = refs[n_in:n_in + k]
        outs = refs[n_in + k:n_in + k + n_out]
        c_out = refs[n_in + k + n_out:n_in + 2 * k + n_out]
        scr = refs[n_in + 2 * k + n_out:n_in + 2 * k + n_out + n_scr]
        sems = refs[n_in + 2 * k + n_out + n_scr:]
        ids = [pl.program_id(d) for d in range(len(grid))]
        first = ids[0] == 0
        last = ids[0] == grid[0] - 1
        for d in range(1, len(grid)):
            first = first & (ids[d] == 0)
            last = last & (ids[d] == grid[d] - 1)

        @pl.when(first)
        def _():
            comm.start(c_in, c_out, sems)

        body(*pre, *ins, *outs, *scr)

        @pl.when(last)
        def _():
            comm.finish(c_in, c_out, sems)

    res = launch(wrapped, list(in_specs) + [ANY] * k, list(out_specs) + [ANY] * k,
                 list(out_shape) + list(comm.out_shape), list(scratch_shapes) + list(comm.sems),
                 tuple(args) + tuple(comm.arrays))
    return list(res[:n_out]), list(res[n_out:])


def _row_tile(rows):
    for t in (512, 256, 128, 64, 32, 16, 8):
        if rows % t == 0:
            return t
    return rows


def _pair_sum(name, g8, recv4, core):
    _, rows, cols = recv4.shape
    tr = _row_tile(rows)
    g42 = g8.reshape(4, 2, rows, cols)

    def body(c_ref, g_ref, r_ref, o_ref):
        del c_ref
        o_ref[...] = (g_ref[...].astype(F32) + r_ref[...].astype(F32)).astype(o_ref.dtype)

    return _pcall(
        body, name=name,
        grid_spec=pltpu.PrefetchScalarGridSpec(
            num_scalar_prefetch=1, grid=(4, rows // tr),
            in_specs=[pl.BlockSpec((None, None, tr, cols), lambda j, i, c_ref: (j, c_ref[0], i, 0)),
                      pl.BlockSpec((None, tr, cols), lambda j, i, c_ref: (j, i, 0))],
            out_specs=pl.BlockSpec((None, tr, cols), lambda j, i, c_ref: (j, i, 0))),
        out_shape=SDS(recv4.shape, g8.dtype),
        compiler_params=_cparams(("arbitrary", "arbitrary"), 32),
    )(core, g42, recv4)


def _add2(name, a, b):
    rows, cols = a.shape
    tr = _row_tile(rows)

    def body(a_ref, b_ref, o_ref):
        o_ref[...] = a_ref[...] + b_ref[...]

    blk = pl.BlockSpec((tr, cols), lambda i: (i, 0))
    return _pcall(body, name=name, grid=(rows // tr,), in_specs=[blk, blk], out_specs=blk,
                          out_shape=SDS(a.shape, a.dtype),
                          compiler_params=_cparams(("arbitrary",), 32))(a, b)


def _sum_terms(name, terms):
    k, rows, cols = terms.shape
    tr = _row_tile(rows)

    def body(r_ref, o_ref):
        acc = r_ref[0]
        for q in range(1, k):
            acc = acc + r_ref[q]
        o_ref[...] = acc

    return _pcall(body, name=name, grid=(rows // tr,),
                          in_specs=[pl.BlockSpec((k, tr, cols), lambda i: (0, i, 0))],
                          out_specs=pl.BlockSpec((tr, cols), lambda i: (i, 0)),
                          out_shape=SDS((rows, cols), terms.dtype),
                          compiler_params=_cparams(("arbitrary",), 32))(terms)


def _adam_update(g, w, m, v):
    c1 = 1.0 / (1.0 - ADAM_B1 ** ADAM_STEP)
    c2 = 1.0 / (1.0 - ADAM_B2 ** ADAM_STEP)
    mn = ADAM_B1 * m + (1.0 - ADAM_B1) * g
    vn = ADAM_B2 * v + (1.0 - ADAM_B2) * (g * g)
    delta = (-ADAM_LR) * ((mn * c1) / (jnp.sqrt(vn * c2) + ADAM_EPS) + ADAM_WD * w)
    return delta, mn, vn


def _adamw_many(name, gs, ws, ms, vs):
    n = len(gs)

    def body(*refs):
        for p in range(n):
            g, w, m, v = (refs[q * n + p][...] for q in range(4))
            d, mn, vn = _adam_update(g, w, m, v)
            refs[4 * n + p][...] = d
            refs[5 * n + p][...] = mn
            refs[6 * n + p][...] = vn

    full = [pl.BlockSpec(memory_space=pltpu.VMEM)] * n
    shapes = [jax.ShapeDtypeStruct(w.shape, F32) for w in ws]
    res = pl.pallas_call(body, name=name, in_specs=full * 4, out_specs=full * 3, out_shape=shapes * 3,
                         compiler_params=pltpu.CompilerParams(vmem_limit_bytes=32 * MiB))(*gs, *ws, *ms, *vs)
    return [(res[p], res[n + p], res[2 * n + p]) for p in range(n)]


def _adamw(name, terms, w, m, v):
    k, rows, cols = terms.shape
    tr = _row_tile(rows)

    def body(t_ref, w_ref, m_ref, v_ref, g_ref, d_ref, mo_ref, vo_ref):
        g = t_ref[0].astype(F32)
        for q in range(1, k):
            g = g + t_ref[q].astype(F32)
        g_ref[...] = g
        d_ref[...], mo_ref[...], vo_ref[...] = _adam_update(g, w_ref[...], m_ref[...], v_ref[...])

    blk = pl.BlockSpec((tr, cols), lambda i: (i, 0))
    return _pcall(body, name=name, grid=(rows // tr,),
                          in_specs=[pl.BlockSpec((k, tr, cols), lambda i: (0, i, 0)), blk, blk, blk],
                          out_specs=[blk] * 4, out_shape=[SDS((rows, cols), F32)] * 4,
                          compiler_params=_cparams(("arbitrary",), 40))(terms, w, m, v)


def kernel(x, norm_mix_g, w_in, conv_w, conv_b, w_rgate, b_rgate, w_igate, b_igate, lru_lambda, w_out_a, sgu_ln_g, sgu_ln_b, sgu_w_s, sgu_b_s, w_out_b, w_out, norm_mlp_g, w_up, w_down, norm_final_g, loss_target, m_norm_mix_g, m_w_in, m_conv_w, m_conv_b, m_w_rgate, m_b_rgate, m_w_igate, m_b_igate, m_lru_lambda, m_w_out_a, m_sgu_ln_g, m_sgu_ln_b, m_sgu_w_s, m_sgu_b_s, m_w_out_b, m_w_out, m_norm_mlp_g, m_w_up, m_w_down, m_norm_final_g, v_norm_mix_g, v_w_in, v_conv_w, v_conv_b, v_w_rgate, v_b_rgate, v_w_igate, v_b_igate, v_lru_lambda, v_w_out_a, v_sgu_ln_g, v_sgu_ln_b, v_sgu_w_s, v_sgu_b_s, v_w_out_b, v_w_out, v_norm_mlp_g, v_w_up, v_w_down, v_norm_final_g):
    cx, cy, cc = _place()
    me = 4 * cx + 2 * cy + cc
    core = jnp.reshape(cc, (1,)).astype(jnp.int32)
    xs = x[0]
    tgt = loss_target[0]
    s = xs.shape[0]

    gate_shard = jnp.stack([w_rgate[0], w_igate[0]]).astype(BF16).reshape(2 * HEADS * 32, HEAD_DIM)
    vec_shard = jnp.concatenate([conv_w[0], b_rgate[0], b_igate[0]], axis=1)
    vec_shard = jnp.pad(vec_shard, ((0, 4), (0, 256 - vec_shard.shape[1])))
    shards = [w_in[0].astype(BF16), w_out_a[0].astype(BF16), w_out_b[0].astype(BF16), w_out[0].astype(BF16),
              w_up[0].astype(BF16), w_down[0].astype(BF16), gate_shard, vec_shard]
    (z, n1_t, w_in_g), (gate_g, vec_g) = _in_proj(xs, norm_mix_g, shards[0], _slot_order(cx, cy, cc),
                                                comm=_gather_plan(shards[6:8]))
    gates = gate_g.reshape(N_SLOT, 2, HEADS, 32, HEAD_DIM).transpose(1, 2, 0, 3, 4).reshape(2, HEADS, HEAD_DIM, HEAD_DIM)
    w_r_f, w_i_f = gates[0], gates[1]
    conv_w_f = vec_g[:, 0:4, 0:128].transpose(1, 0, 2).reshape(CONV_K, D)
    b_r_f = vec_g[:, 0:4, 128:160].transpose(1, 0, 2).reshape(1, D)
    b_i_f = vec_g[:, 0:4, 160:192].transpose(1, 0, 2).reshape(1, D)
    b_s_t = jnp.transpose(sgu_b_s[0])

    (ya, hs), (w_oa_g, w_ob_g, w_out_g) = _branch_a_fwd(
        z, conv_w_f, conv_b, w_r_f, b_r_f, w_i_f, b_i_f, lru_lambda, comm=_gather_plan(shards[1:4]))
    w_oa_f = w_oa_g.reshape(D, D)
    w_ob_f = w_ob_g.reshape(D, D)
    w_out_f = w_out_g.reshape(D, D)
    (yb,), (w_up_g,) = _branch_b_fwd(z, sgu_ln_g, sgu_ln_b, sgu_w_s[0], b_s_t, comm=_gather_plan(shards[4:5]))
    (pa, pb, merged, h1), (w_down_g,) = _merge_out(ya, yb, z, xs, w_oa_f, w_ob_f, w_out_f,
                                                   comm=_gather_plan(shards[5:6]))
    w_down_f = w_down_g.reshape(N_SLOT * FF_COLS, D)
    gf2 = norm_final_g.reshape(1, D)
    r_act, act_t, n2_t, dh2, loss_acc, d_gfin = _mlp_fwd(h1, norm_mlp_g, w_up_g, w_down_f, gf2, tgt)

    def pair(names, grads, recv):
        return [_pair_sum("pair_sum_" + nm, g, r, core) for nm, g, r in zip(names, grads, recv)]

    g_down = _wgrad_t("wgrad_down", act_t, dh2, N_SLOT // 2, True, False, 2048)
    g_down = g_down.reshape(N_SLOT, FF_COLS, D)
    (df, dh1, d_gmlp), (r_down,) = _mlp_bwd(dh2, r_act, w_down_f, w_up_g, h1, norm_mlp_g,
                                            comm=_sibling_plan([g_down]))
    (p_down,) = pair(["down"], [g_down], [r_down])
    g_up = _wgrad_t("wgrad_up", n2_t, df, N_SLOT, False, True, 4096)
    g_out = _wgrad("wgrad_out", merged, dh1, 1, False, False).reshape(N_SLOT, D // N_SLOT, D)
    (dz, dpa, dpb, dya, dyb), (got_down, r_up, r_out) = _merge_bwd(
        dh1, z, pa, pb, w_out_f, w_oa_f, w_ob_f, comm=_join(_chips_plan([p_down]), _sibling_plan([g_up, g_out])))
    p_up, p_out = pair(["up", "out"], [g_up, g_out], [r_up, r_out])
    g_oa = _wgrad("wgrad_out_a", ya, dpa, 1, False, False).reshape(N_SLOT, D // N_SLOT, D)
    g_ob = _wgrad("wgrad_out_b", yb, dpb, 1, False, False).reshape(N_SLOT, D // N_SLOT, D)
    (dz, d_ws, d_bs, d_ln), (got_up, r_oa, r_ob) = _branch_b_bwd(
        dz, dyb, z, sgu_ln_g, sgu_ln_b, sgu_w_s[0], b_s_t,
        comm=_join(_chips_plan([p_up]), _sibling_plan([g_oa, g_ob])))
    p_oa, p_ob = pair(["out_a", "out_b"], [g_oa, g_ob], [r_oa, r_ob])
    (dz, d_vec, d_wr, d_wi), (got_out, got_oa, got_ob) = _branch_a_bwd(
        dz, dya, z, hs, conv_w_f, conv_b, w_r_f, b_r_f, w_i_f, b_i_f, lru_lambda,
        comm=_chips_plan([p_out, p_oa, p_ob]))
    g_in = _wgrad_t("wgrad_in", n1_t, dz, N_SLOT, False, True, 4096)
    g_gate = jnp.stack([d_wr, d_wi]).reshape(2, HEADS, N_SLOT, 32, HEAD_DIM).transpose(2, 0, 1, 3, 4)
    g_gate = g_gate.reshape(N_SLOT, 2 * HEADS * 32, HEAD_DIM).astype(BF16)

    d_bs_row = jnp.pad(d_bs[:, :, 0].reshape(1, GROUPS * CHUNK), ((0, 0), (0, D - GROUPS * CHUNK)))
    vecs = jnp.concatenate([d_vec, jnp.concatenate([d_ln[0:2], d_gmlp, d_gfin, d_bs_row, jnp.zeros((3, D), F32)])])
    d_ws2 = d_ws.reshape(GROUPS * CHUNK, CHUNK)
    r_in, r_gate, r_vecs, r_ws = _run_plan("rs_sibling_in", _sibling_plan([g_in, g_gate], [vecs, d_ws2]))
    p_in, p_gate = pair(["in", "gate"], [g_in, g_gate], [r_in, r_gate])
    vecs_chip = _add2("pair_sum_vecs", vecs, r_vecs)
    ws_chip = _add2("pair_sum_ws", d_ws2, r_ws)
    (dx, d_gmix), (got_in, got_gate, got_vecs, got_ws) = _in_bwd(
        dz, w_in_g, xs, dh1, norm_mix_g, comm=_chips_plan([p_in, p_gate], [vecs_chip, ws_chip]))
    vecs_sum = _sum_terms("sum_vecs", got_vecs)
    last = jnp.concatenate([d_gmix, jnp.pad(loss_acc[0:1], ((0, 0), (0, D - 128))), jnp.zeros((6, D), F32)])
    (last_all,) = _run_plan("exchange_last", _exchange_plan(last))
    last_sum = _sum_terms("sum_last", last_all)
    loss = last_sum[1, 0]
    got = [got_in, got_oa, got_ob, got_out, got_up, got_down, got_gate]

    def step(nm, terms, w, m, v, rows, cols):
        g, d, mn, vn = _adamw("adamw_" + nm, terms.reshape(4, rows, cols), w.reshape(rows, cols),
                              m.reshape(rows, cols), v.reshape(rows, cols))
        return [a.reshape(w.shape) for a in (g, d, mn, vn)]

    o_in = step("in", got[0], w_in, m_w_in, v_w_in, D, W_IN_COLS)
    o_oa = step("out_a", got[1], w_out_a, m_w_out_a, v_w_out_a, D // N_SLOT, D)
    o_ob = step("out_b", got[2], w_out_b, m_w_out_b, v_w_out_b, D // N_SLOT, D)
    o_out = step("out", got[3], w_out, m_w_out, v_w_out, D // N_SLOT, D)
    o_up = step("up", got[4], w_up, m_w_up, v_w_up, D, FF_COLS)
    o_down = step("down", got[5], w_down, m_w_down, v_w_down, FF_COLS, D)
    gate_w = jnp.stack([w_rgate[0], w_igate[0]]).reshape(2 * HEADS * 32, HEAD_DIM)
    gate_m = jnp.stack([m_w_rgate[0], m_w_igate[0]]).reshape(2 * HEADS * 32, HEAD_DIM)
    gate_v = jnp.stack([v_w_rgate[0], v_w_igate[0]]).reshape(2 * HEADS * 32, HEAD_DIM)
    o_gate = _adamw("adamw_gate", got[6], gate_w, gate_m, gate_v)
    o_gate = [a.reshape(2, 1, HEADS, 32, HEAD_DIM) for a in o_gate]
    o_wr = [a[0] for a in o_gate]
    o_wi = [a[1] for a in o_gate]

    def own(full, width):
        return lax.dynamic_slice_in_dim(full, me * width, width, axis=1)

    small_g = {
        "norm_mix_g": last_sum[0:1], "conv_w": own(vecs_sum[0:4], 128), "conv_b": vecs_sum[4:5],
        "b_rgate": own(vecs_sum[5:6].reshape(HEADS, HEAD_DIM), 32),
        "b_igate": own(vecs_sum[6:7].reshape(HEADS, HEAD_DIM), 32),
        "lru_lambda": vecs_sum[7:8], "sgu_ln_g": vecs_sum[8:9], "sgu_ln_b": vecs_sum[9:10],
        "norm_mlp_g": vecs_sum[10:11], "norm_final_g": vecs_sum[11:12],
        "sgu_b_s": vecs_sum[12, 0:GROUPS * CHUNK].reshape(GROUPS, CHUNK),
    }
    small_w = {"norm_mix_g": (norm_mix_g, m_norm_mix_g, v_norm_mix_g), "conv_w": (conv_w, m_conv_w, v_conv_w),
               "conv_b": (conv_b, m_conv_b, v_conv_b), "b_rgate": (b_rgate, m_b_rgate, v_b_rgate),
               "b_igate": (b_igate, m_b_igate, v_b_igate), "lru_lambda": (lru_lambda, m_lru_lambda, v_lru_lambda),
               "sgu_ln_g": (sgu_ln_g, m_sgu_ln_g, v_sgu_ln_g), "sgu_ln_b": (sgu_ln_b, m_sgu_ln_b, v_sgu_ln_b),
               "norm_mlp_g": (norm_mlp_g, m_norm_mlp_g, v_norm_mlp_g),
               "norm_final_g": (norm_final_g, m_norm_final_g, v_norm_final_g),
               "sgu_b_s": (sgu_b_s, m_sgu_b_s, v_sgu_b_s), "sgu_w_s": (sgu_w_s, m_sgu_w_s, v_sgu_w_s)}
    order = list(small_g)
    as2d = lambda k, a: a.reshape(small_g[k].shape)
    upd = _adamw_many("adamw_small", [small_g[k] for k in order], *[[as2d(k, small_w[k][q]) for k in order]
                                                                     for q in range(3)])
    o_small = {k: [a.reshape(small_w[k][0].shape) for a in (small_g[k],) + u] for k, u in zip(order, upd)}
    ws3 = [a[0].reshape(GROUPS * CHUNK, CHUNK) for a in small_w.pop("sgu_w_s")]
    o_small["sgu_w_s"] = [a.reshape(sgu_w_s.shape) for a in _adamw("adamw_ws", got_ws, *ws3)]

    per_weight = {"norm_mix_g": o_small["norm_mix_g"], "w_in": o_in, "conv_w": o_small["conv_w"],
                  "conv_b": o_small["conv_b"], "w_rgate": o_wr, "b_rgate": o_small["b_rgate"], "w_igate": o_wi,
                  "b_igate": o_small["b_igate"], "lru_lambda": o_small["lru_lambda"], "w_out_a": o_oa,
                  "sgu_ln_g": o_small["sgu_ln_g"], "sgu_ln_b": o_small["sgu_ln_b"], "sgu_w_s": o_small["sgu_w_s"],
                  "sgu_b_s": o_small["sgu_b_s"], "w_out_b": o_ob, "w_out": o_out, "norm_mlp_g": o_small["norm_mlp_g"],
                  "w_up": o_up, "w_down": o_down, "norm_final_g": o_small["norm_final_g"]}
    names_w = list(per_weight)
    return (loss, dx[None], *[per_weight[k][0] for k in names_w], *[per_weight[k][1] for k in names_w],
            *[per_weight[k][2] for k in names_w], *[per_weight[k][3] for k in names_w])
```

```python
import jax
import jax.numpy as jnp
from jax import lax
from jax.experimental import pallas as pl
from jax.experimental.pallas import tpu as pltpu

F32 = jnp.float32
BF16 = jnp.bfloat16
SDS = jax.ShapeDtypeStruct
MESH = pl.DeviceIdType.MESH
ANY = pl.BlockSpec(memory_space=pltpu.HBM)

D = 1024
N_SLOT = 8
W_IN_COLS = 768
FF_COLS = 512
HEADS, HEAD_DIM = 4, 256
GROUPS, GROUP_DIM = 4, 256
CHUNK = 128
CONV_K = 4
NORM_EPS = 1e-6
LN_EPS = 1e-5
LRU_C = 8.0
ADAM_LR, ADAM_B1, ADAM_B2, ADAM_EPS, ADAM_WD, ADAM_STEP = 0.001, 0.9, 0.999, 1e-08, 0.01, 10

TM_ROWS = 1024
TM_MERGE = 512
T_BRANCH = 256
MiB = 1024 * 1024

_GELU_C = 0.7978845608028654
_GELU_A = 0.044715


def _cparams(sem, vmem_mib):
    return pltpu.CompilerParams(dimension_semantics=sem, vmem_limit_bytes=vmem_mib * MiB)


def _gelu(x):
    t = jnp.tanh(_GELU_C * (x + _GELU_A * x * x * x))
    return 0.5 * x * (1.0 + t)


def _gelu_and_grad(x):
    x2 = x * x
    t = jnp.tanh(_GELU_C * x * (1.0 + _GELU_A * x2))
    g = 0.5 * x * (1.0 + t)
    dg = 0.5 * (1.0 + t) + 0.5 * x * (1.0 - t * t) * _GELU_C * (1.0 + 3.0 * _GELU_A * x2)
    return g, dg


def _softplus(x):
    return jnp.maximum(x, 0.0) + jnp.log1p(jnp.exp(-jnp.abs(x)))


def _dot(a, b):
    return jnp.dot(a, b, preferred_element_type=F32)


def _dot_nt(a, b):
    return lax.dot_general(a, b, (((1,), (1,)), ((), ())), preferred_element_type=F32)


def _dot_tn(a, b):
    return lax.dot_general(a, b, (((0,), (0,)), ((), ())), preferred_element_type=F32)


def _rows_shifted(prev8, cur, k):
    ext = jnp.concatenate([prev8, cur], axis=0)
    return pltpu.roll(ext, k, 0)[8:]


def _rows_advanced(cur, next8, k):
    t = cur.shape[0]
    ext = jnp.concatenate([cur, next8], axis=0)
    return pltpu.roll(ext, t + 8 - k, 0)[:t]


def _slot_order(x, y, c):
    chip = 2 * x + y
    order = [2 * chip + c, 2 * chip + 1 - c]
    for _, _, pc in _other_chips(x, y):
        order += [2 * pc + c, 2 * pc + 1 - c]
    return jnp.stack(order).astype(jnp.int32)


def _in_proj(x, g_mix, w_in_own, order, comm=None):
    s = x.shape[0]
    tm = min(TM_ROWS, s)
    ni = s // tm

    def body(order_ref, x_ref, g_ref, own_ref, z_ref, nt_ref, wg_ref, n_s, w_s, send_sems, recv_sems, local_sems):
        j, i = pl.program_id(0), pl.program_id(1)
        px, py, c = _place()
        chip = 2 * px + py
        me = 2 * chip + c
        sib = (px, py, 1 - c)
        chips = _other_chips(px, py)

        def rc(k, src, blk, to):
            return pltpu.make_async_remote_copy(src_ref=src, dst_ref=w_s.at[blk], send_sem=send_sems.at[k],
                                                recv_sem=recv_sems.at[k], device_id=to, device_id_type=MESH)

        own_in = pltpu.make_async_copy(own_ref, w_s.at[me], local_sems.at[0])
        sends = [rc(0, own_ref, me, sib)] + [rc(1 + q, own_ref, me, (qx, qy, c)) for q, (qx, qy, _) in enumerate(chips)]
        passed = [rc(4 + q, w_s.at[2 * pc + c], 2 * pc + c, sib) for q, (_, _, pc) in enumerate(chips)]
        keep = pltpu.make_async_copy(w_s, wg_ref, local_sems.at[1])

        @pl.when((i == 0) & (j == 0))
        def _():
            own_in.start()
            for cp in sends:
                cp.start()
            own_in.wait()

        @pl.when((i == 0) & (j == 1))
        def _():
            rc(0, own_ref, 2 * chip + 1 - c, sib).wait_recv()

        for q, (_, _, pc) in enumerate(chips):
            @pl.when((i == 0) & (j == 2 + 2 * q))
            def _():
                rc(1 + q, own_ref, 2 * pc + c, sib).wait_recv()
                passed[q].start()

            @pl.when((i == 0) & (j == 3 + 2 * q))
            def _():
                rc(4 + q, own_ref, 2 * pc + 1 - c, sib).wait_recv()

        rows = pl.ds(pl.multiple_of(i * tm, tm), tm)

        @pl.when(j == 0)
        def _():
            xv = x_ref[...]
            rstd = lax.rsqrt(jnp.mean(xv * xv, axis=-1, keepdims=True) + NORM_EPS)
            nb = (xv * rstd * g_ref[...]).astype(BF16)
            n_s[rows, :] = nb
            nt_ref[...] = nb.T

        z_ref[...] = _dot_nt(n_s[rows, :], w_s[order_ref[j]]).astype(BF16)

        @pl.when((i == 0) & (j == N_SLOT - 1))
        def _():
            keep.start()

        @pl.when((i == ni - 1) & (j == N_SLOT - 1))
        def _():
            for cp in sends + passed:
                cp.wait_send()
            keep.wait()

    first_pass = lambda j, i, o: (jnp.where(j == 0, i, ni - 1), 0)
    (z, n1, w_in_g), extra = _call(
        body, name="in_proj", grid=(N_SLOT, ni), prefetch=(order,),
        in_specs=[pl.BlockSpec((tm, D), first_pass),
                  pl.BlockSpec((1, D), lambda j, i, o: (0, 0)), ANY],
        out_specs=[pl.BlockSpec((tm, W_IN_COLS), lambda j, i, o: (i, o[j])),
                   pl.BlockSpec((D, tm), lambda j, i, o: (0, jnp.where(j == 0, i, ni - 1))), ANY],
        out_shape=[SDS((s, N_SLOT * W_IN_COLS), BF16), SDS((D, s), BF16), SDS((N_SLOT, W_IN_COLS, D), BF16)],
        scratch_shapes=[pltpu.VMEM((s, D), BF16), pltpu.VMEM((N_SLOT, W_IN_COLS, D), BF16),
                        pltpu.SemaphoreType.DMA((7,)), pltpu.SemaphoreType.DMA((7,)), pltpu.SemaphoreType.DMA((2,))],
        params=_cparams(("arbitrary", "arbitrary"), 56), args=(x, g_mix, w_in_own), comm=comm)
    return (z, n1, w_in_g), extra


def _lru_gates(xc, xcb, wr_ref, br, wi_ref, bi, sp_lam, a_s, b_s, r_s=None, i_s=None, m_s=None):
    for h in range(HEADS):
        sl = slice(h * HEAD_DIM, (h + 1) * HEAD_DIM)
        r = jax.nn.sigmoid(_dot(xcb[:, sl], wr_ref[h]) + br[:, sl])
        ig = jax.nn.sigmoid(_dot(xcb[:, sl], wi_ref[h]) + bi[:, sl])
        log_a = (-LRU_C) * r * sp_lam[:, sl]
        a = jnp.exp(log_a)
        mult = jnp.sqrt(-jnp.tanh(log_a) * (a * a + 1.0))
        a_s[:, sl] = a
        b_s[:, sl] = xc[:, sl] * ig * mult
        if r_s is not None:
            r_s[:, sl] = r
            i_s[:, sl] = ig
            m_s[:, sl] = mult


def _conv_fwd(xa, prev8, cw, cb):
    xc = cb + cw[0:1, :] * xa
    for k in range(1, CONV_K):
        xc = xc + cw[k:k + 1, :] * _rows_shifted(prev8, xa, k)
    return xc


def _branch_a_fwd(z, conv_w, conv_b, w_r, b_r, w_i, b_i, lam, comm=None):
    s = z.shape[0]
    ta = min(T_BRANCH, s)
    per16 = ta // 16

    def body(xa_ref, xp_ref, ga_ref, cw_ref, cb_ref, wr_ref, br_ref, wi_ref, bi_ref, lam_ref,
             ya_ref, hs_ref, a_s, b_s, h_s, carry_s):
        i = pl.program_id(0)

        @pl.when(i == 0)
        def _():
            carry_s[...] = jnp.zeros_like(carry_s)

        xa = xa_ref[...].astype(F32)
        prev8 = jnp.where(i > 0, xp_ref[...].astype(F32)[8:16], 0.0)
        xc = _conv_fwd(xa, prev8, cw_ref[...], cb_ref[...])
        sp_lam = _softplus(-lam_ref[...])
        _lru_gates(xc, xc.astype(BF16), wr_ref, br_ref[...], wi_ref, bi_ref[...], sp_lam, a_s, b_s)

        row = lax.broadcasted_iota(jnp.int32, (8, D), 0)

        def group(g, carry):
            off = pl.multiple_of(g * 8, 8)
            a8 = a_s[pl.ds(off, 8), :]
            b8 = b_s[pl.ds(off, 8), :]
            for d in (1, 2, 4):
                a_sh = jnp.where(row >= d, pltpu.roll(a8, d, 0), 1.0)
                b_sh = jnp.where(row >= d, pltpu.roll(b8, d, 0), 0.0)
                b8 = a8 * b_sh + b8
                a8 = a8 * a_sh
            h8 = b8 + a8 * carry
            h_s[pl.ds(off, 8), :] = h8
            return jnp.broadcast_to(h8[7:8, :], (8, D))

        carry_s[...] = lax.fori_loop(0, ta // 8, group, carry_s[...])
        hs = h_s[...]
        hs_ref[...] = hs.astype(BF16)
        ya_ref[...] = (hs * _gelu(ga_ref[...].astype(F32))).astype(BF16)

    vec = pl.BlockSpec((1, D), lambda i: (0, 0))
    gate = pl.BlockSpec((HEADS, HEAD_DIM, HEAD_DIM), lambda i: (0, 0, 0))
    return _call(
        body, name="branch_a_fwd", grid=(s // ta,),
        in_specs=[pl.BlockSpec((ta, D), lambda i: (i, 0)),
                  pl.BlockSpec((16, D), lambda i: (jnp.maximum(i * per16 - 1, 0), 0)),
                  pl.BlockSpec((ta, D), lambda i: (i, 1)),
                  pl.BlockSpec((CONV_K, D), lambda i: (0, 0)), vec, gate, vec, gate, vec, vec],
        out_specs=[pl.BlockSpec((ta, D), lambda i: (i, 0)), pl.BlockSpec((ta, D), lambda i: (i, 0))],
        out_shape=[SDS((s, D), BF16), SDS((s, D), BF16)],
        scratch_shapes=[pltpu.VMEM((ta, D), F32), pltpu.VMEM((ta, D), F32), pltpu.VMEM((ta, D), F32),
                        pltpu.VMEM((8, D), F32)],
        params=_cparams(("arbitrary",), 40), args=(z, z, z, conv_w, conv_b, w_r, b_r, w_i, b_i, lam), comm=comm)


def _sgu_common(ub, vb, lg, lb, with_grad):
    if with_grad:
        u, du = _gelu_and_grad(ub)
        v, dv = _gelu_and_grad(vb)
    else:
        u, v, du, dv = _gelu(ub), _gelu(vb), None, None
    mu = jnp.mean(v, axis=-1, keepdims=True)
    vc = v - mu
    rstd = lax.rsqrt(jnp.mean(vc * vc, axis=-1, keepdims=True) + LN_EPS)
    vhat = vc * rstd
    vln = vhat * lg + lb
    return u, du, dv, rstd, vhat, vln


def _masked_ws(ws_ref):
    t = lax.broadcasted_iota(jnp.int32, (CHUNK, CHUNK), 0)
    c = lax.broadcasted_iota(jnp.int32, (CHUNK, CHUNK), 1)
    keep = c <= t
    return [jnp.where(keep, ws_ref[g], 0.0).astype(BF16) for g in range(GROUPS)]


def _branch_b_fwd(z, ln_g, ln_b, w_s, b_s_t, comm=None):
    s = z.shape[0]
    tb = min(T_BRANCH, s)

    def body(ub_ref, vb_ref, lg_ref, lb_ref, ws_ref, bs_ref, yb_ref):
        u, _, _, _, _, vln = _sgu_common(ub_ref[...].astype(F32), vb_ref[...].astype(F32),
                                         lg_ref[...], lb_ref[...], False)
        vlnb = vln.astype(BF16)
        wm = _masked_ws(ws_ref)
        bs = bs_ref[...]
        for c in range(tb // CHUNK):
            rs = slice(c * CHUNK, (c + 1) * CHUNK)
            for g in range(GROUPS):
                cs = slice(g * GROUP_DIM, (g + 1) * GROUP_DIM)
                sp = _dot(wm[g], vlnb[rs, cs]) + bs[:, g:g + 1]
                yb_ref[rs, cs] = (u[rs, cs] * sp).astype(BF16)

    vec = pl.BlockSpec((1, D), lambda i: (0, 0))
    return _call(
        body, name="branch_b_fwd", grid=(s // tb,),
        in_specs=[pl.BlockSpec((tb, D), lambda i: (i, 2)), pl.BlockSpec((tb, D), lambda i: (i, 3)), vec, vec,
                  pl.BlockSpec((GROUPS, CHUNK, CHUNK), lambda i: (0, 0, 0)),
                  pl.BlockSpec((CHUNK, GROUPS), lambda i: (0, 0))],
        out_specs=[pl.BlockSpec((tb, D), lambda i: (i, 0))],
        out_shape=[SDS((s, D), BF16)], scratch_shapes=[],
        params=_cparams(("arbitrary",), 40), args=(z, z, ln_g, ln_b, w_s, b_s_t), comm=comm)


def _merge_out(ya, yb, z, x, w_oa, w_ob, w_out, comm=None):
    s = x.shape[0]
    tm = min(TM_MERGE, s)

    def body(ya_ref, yb_ref, ma_ref, mb_ref, x_ref, woa_ref, wob_ref, wo_ref, pa_ref, pb_ref, mg_ref, h1_ref):
        pa = _dot(ya_ref[...], woa_ref[...])
        pb = _dot(yb_ref[...], wob_ref[...])
        merged = (jax.nn.sigmoid(ma_ref[...].astype(F32)) * pa
                  + jax.nn.sigmoid(mb_ref[...].astype(F32)) * pb).astype(BF16)
        pa_ref[...] = pa.astype(BF16)
        pb_ref[...] = pb.astype(BF16)
        mg_ref[...] = merged
        h1_ref[...] = x_ref[...] + _dot(merged, wo_ref[...])

    row = pl.BlockSpec((tm, D), lambda i: (i, 0))
    wsp = pl.BlockSpec((D, D), lambda i: (0, 0))
    return _call(
        body, name="merge_out", grid=(s // tm,),
        in_specs=[row, row, pl.BlockSpec((tm, D), lambda i: (i, 4)), pl.BlockSpec((tm, D), lambda i: (i, 5)),
                  row, wsp, wsp, wsp],
        out_specs=[row, row, row, row],
        out_shape=[SDS((s, D), BF16), SDS((s, D), BF16), SDS((s, D), BF16), SDS((s, D), F32)], scratch_shapes=[],
        params=_cparams(("arbitrary",), 48), args=(ya, yb, z, z, x, w_oa, w_ob, w_out), comm=comm)


def _mlp_up(h1, g_mlp, w_up_t, comm=None):
    s = h1.shape[0]
    tm = min(TM_ROWS, s)
    nj = N_SLOT

    def body(h1_ref, gm_ref, wu_ref, r_ref, a_ref, at_ref, n2t_ref, n2_s):
        @pl.when(pl.program_id(1) == 0)
        def _():
            hv = h1_ref[...]
            rstd = lax.rsqrt(jnp.mean(hv * hv, axis=-1, keepdims=True) + NORM_EPS)
            nb = (hv * rstd * gm_ref[...]).astype(BF16)
            n2_s[...] = nb
            n2t_ref[...] = nb.T

        r = jnp.maximum(_dot_nt(n2_s[...], wu_ref[...]), 0.0)
        act = (r * r).astype(BF16)
        r_ref[...] = r.astype(BF16)
        a_ref[...] = act
        at_ref[...] = act.T

    ffb = pl.BlockSpec((tm, FF_COLS), lambda i, j: (i, j))
    return _call(
        body, name="mlp_up", grid=(s // tm, nj),
        in_specs=[pl.BlockSpec((tm, D), lambda i, j: (i, 0)), pl.BlockSpec((1, D), lambda i, j: (0, 0)),
                  pl.BlockSpec((FF_COLS, D), lambda i, j: (j, 0))],
        out_specs=[ffb, ffb, pl.BlockSpec((FF_COLS, tm), lambda i, j: (j, i)),
                   pl.BlockSpec((D, tm), lambda i, j: (0, i))],
        out_shape=[SDS((s, nj * FF_COLS), BF16), SDS((s, nj * FF_COLS), BF16), SDS((nj * FF_COLS, s), BF16),
                   SDS((D, s), BF16)],
        scratch_shapes=[pltpu.VMEM((tm, D), BF16)],
        params=_cparams(("arbitrary", "arbitrary"), 40), args=(h1, g_mlp, w_up_t), comm=comm)


def _mlp_down(act, w_down, h1, g_fin, tgt):
    s = h1.shape[0]
    tm = min(TM_MERGE, s)

    def body(a_ref, wd_ref, h1_ref, gf_ref, t_ref, dh2_ref, loss_ref, dgf_ref):
        @pl.when(pl.program_id(0) == 0)
        def _():
            loss_ref[...] = jnp.zeros_like(loss_ref)
            dgf_ref[...] = jnp.zeros_like(dgf_ref)

        h2 = h1_ref[...] + _dot(a_ref[...], wd_ref[...])
        rstd = lax.rsqrt(jnp.mean(h2 * h2, axis=-1, keepdims=True) + NORM_EPS)
        hh = h2 * rstd
        gf = gf_ref[...]
        e = hh * gf - t_ref[...]
        loss_ref[...] += jnp.sum(e * e) * (0.5 / D)
        dy = e * (1.0 / D)
        dgf_ref[...] += jnp.sum(dy * hh, axis=0, keepdims=True)
        dhh = dy * gf
        dh2_ref[...] = rstd * (dhh - hh * jnp.mean(dhh * hh, axis=-1, keepdims=True))

    ff = act.shape[1]
    row = pl.BlockSpec((tm, D), lambda i: (i, 0))
    vec = pl.BlockSpec((1, D), lambda i: (0, 0))
    return pl.pallas_call(
        body, name="mlp_down", grid=(s // tm,),
        in_specs=[pl.BlockSpec((tm, ff), lambda i: (i, 0)), pl.BlockSpec((ff, D), lambda i: (0, 0)), row, vec, row],
        out_specs=[row, pl.BlockSpec((8, 128), lambda i: (0, 0)), vec],
        out_shape=[SDS((s, D), F32), SDS((8, 128), F32), SDS((1, D), F32)],
        compiler_params=_cparams(("arbitrary",), 48),
    )(act, w_down, h1, g_fin, tgt)


def _mlp_bwd_ff(dh2, r, w_down, comm=None):
    s = dh2.shape[0]
    tm = min(TM_ROWS, s)
    nj = N_SLOT

    def body(dh2_ref, r_ref, wd_ref, df_ref, dh2b_s):
        @pl.when(pl.program_id(1) == 0)
        def _():
            dh2b_s[...] = dh2_ref[...].astype(BF16)

        d_act = _dot_nt(dh2b_s[...], wd_ref[...])
        df_ref[...] = (d_act * (2.0 * r_ref[...].astype(F32))).astype(BF16)

    ffb = pl.BlockSpec((tm, FF_COLS), lambda i, j: (i, j))
    return _call(
        body, name="mlp_bwd_ff", grid=(s // tm, nj),
        in_specs=[pl.BlockSpec((tm, D), lambda i, j: (i, 0)), ffb, pl.BlockSpec((FF_COLS, D), lambda i, j: (j, 0))],
        out_specs=[ffb], out_shape=[SDS((s, nj * FF_COLS), BF16)],
        scratch_shapes=[pltpu.VMEM((tm, D), BF16)],
        params=_cparams(("arbitrary", "arbitrary"), 40), args=(dh2, r, w_down), comm=comm)


def _mlp_bwd_in(df, w_up_t, dh2, h1, g_mlp, comm=None):
    s = h1.shape[0]
    tm = min(TM_MERGE, s)

    def body(df_ref, wu_ref, dh2_ref, h1_ref, gm_ref, dh1_ref, dgm_ref):
        @pl.when(pl.program_id(0) == 0)
        def _():
            dgm_ref[...] = jnp.zeros_like(dgm_ref)

        dn2 = _dot(df_ref[...], wu_ref[...])
        hv = h1_ref[...]
        rstd = lax.rsqrt(jnp.mean(hv * hv, axis=-1, keepdims=True) + NORM_EPS)
        hh = hv * rstd
        dgm_ref[...] += jnp.sum(dn2 * hh, axis=0, keepdims=True)
        dhat = dn2 * gm_ref[...]
        dh1_ref[...] = dh2_ref[...] + rstd * (dhat - hh * jnp.mean(dhat * hh, axis=-1, keepdims=True))

    ff = df.shape[1]
    row = pl.BlockSpec((tm, D), lambda i: (i, 0))
    vec = pl.BlockSpec((1, D), lambda i: (0, 0))
    return _call(
        body, name="mlp_bwd_in", grid=(s // tm,),
        in_specs=[pl.BlockSpec((tm, ff), lambda i: (i, 0)), pl.BlockSpec((ff, D), lambda i: (0, 0)), row, row, vec],
        out_specs=[row, vec], out_shape=[SDS((s, D), F32), SDS((1, D), F32)], scratch_shapes=[],
        params=_cparams(("arbitrary",), 48), args=(df, w_up_t, dh2, h1, g_mlp), comm=comm)


def _merge_bwd(dh1, z, pa, pb, w_out, w_oa, w_ob, comm=None):
    s = dh1.shape[0]
    tm = min(TM_MERGE, s)

    def body(dh1_ref, ma_ref, mb_ref, pa_ref, pb_ref, wo_ref, woa_ref, wob_ref,
             dz_ref, dpa_ref, dpb_ref, dya_ref, dyb_ref):
        dm = _dot_nt(dh1_ref[...].astype(BF16), wo_ref[...])
        sa = jax.nn.sigmoid(ma_ref[...].astype(F32))
        sb = jax.nn.sigmoid(mb_ref[...].astype(F32))
        dpa = (dm * sa).astype(BF16)
        dpb = (dm * sb).astype(BF16)
        dz_ref[:, 0:D] = (dm * pa_ref[...].astype(F32) * sa * (1.0 - sa)).astype(BF16)
        dz_ref[:, D:2 * D] = (dm * pb_ref[...].astype(F32) * sb * (1.0 - sb)).astype(BF16)
        dpa_ref[...] = dpa
        dpb_ref[...] = dpb
        dya_ref[...] = _dot_nt(dpa, woa_ref[...]).astype(BF16)
        dyb_ref[...] = _dot_nt(dpb, wob_ref[...]).astype(BF16)

    row = pl.BlockSpec((tm, D), lambda i: (i, 0))
    wsp = pl.BlockSpec((D, D), lambda i: (0, 0))
    return _call(
        body, name="merge_bwd", grid=(s // tm,),
        in_specs=[row, pl.BlockSpec((tm, D), lambda i: (i, 4)), pl.BlockSpec((tm, D), lambda i: (i, 5)),
                  row, row, wsp, wsp, wsp],
        out_specs=[pl.BlockSpec((tm, 2 * D), lambda i: (i, 2)), row, row, row, row],
        out_shape=[SDS((s, 6 * D), BF16)] + [SDS((s, D), BF16)] * 4, scratch_shapes=[],
        params=_cparams(("arbitrary",), 48), args=(dh1, z, z, pa, pb, w_out, w_oa, w_ob), comm=comm)


def _branch_b_bwd(dz, dyb, z, ln_g, ln_b, w_s, b_s_t, comm=None):
    s = z.shape[0]
    tb = min(T_BRANCH, s)

    def body(dz_in, dyb_ref, ub_ref, vb_ref, lg_ref, lb_ref, ws_ref, bs_ref,
             dz_ref, dws_ref, dbs_ref, dln_ref, du_s, dvln_s):
        del dz_in

        @pl.when(pl.program_id(0) == 0)
        def _():
            dws_ref[...] = jnp.zeros_like(dws_ref)
            dbs_ref[...] = jnp.zeros_like(dbs_ref)
            dln_ref[...] = jnp.zeros_like(dln_ref)

        lg = lg_ref[...]
        u, du, dv, rstd, vhat, vln = _sgu_common(ub_ref[...].astype(F32), vb_ref[...].astype(F32),
                                                 lg, lb_ref[...], True)
        vlnb = vln.astype(BF16)
        dyb_v = dyb_ref[...].astype(F32)
        wm = _masked_ws(ws_ref)
        keep = (lax.broadcasted_iota(jnp.int32, (CHUNK, CHUNK), 1)
                <= lax.broadcasted_iota(jnp.int32, (CHUNK, CHUNK), 0))
        bs = bs_ref[...]
        for c in range(tb // CHUNK):
            rs = slice(c * CHUNK, (c + 1) * CHUNK)
            for g in range(GROUPS):
                cs = slice(g * GROUP_DIM, (g + 1) * GROUP_DIM)
                v_blk = vlnb[rs, cs]
                sp = _dot(wm[g], v_blk) + bs[:, g:g + 1]
                d_sp = dyb_v[rs, cs] * u[rs, cs]
                d_spb = d_sp.astype(BF16)
                du_s[rs, cs] = dyb_v[rs, cs] * sp
                dvln_s[rs, cs] = _dot_tn(wm[g], d_spb)
                dws_ref[g] += jnp.where(keep, _dot_nt(d_spb, v_blk), 0.0)
                dbs_ref[g] += jnp.broadcast_to(jnp.sum(d_sp, axis=-1, keepdims=True), (CHUNK, CHUNK))
        dvln = dvln_s[...]
        dln_ref[0:1, :] += jnp.sum(dvln * vhat, axis=0, keepdims=True)
        dln_ref[1:2, :] += jnp.sum(dvln, axis=0, keepdims=True)
        dvh = dvln * lg
        d_v = rstd * (dvh - jnp.mean(dvh, axis=-1, keepdims=True)
                      - vhat * jnp.mean(dvh * vhat, axis=-1, keepdims=True))
        dz_ref[:, 0:D] = (du_s[...] * du).astype(BF16)
        dz_ref[:, D:2 * D] = (d_v * dv).astype(BF16)

    vec = pl.BlockSpec((1, D), lambda i: (0, 0))
    sq = pl.BlockSpec((GROUPS, CHUNK, CHUNK), lambda i: (0, 0, 0))
    return _call(
        body, name="branch_b_bwd", grid=(s // tb,),
        in_specs=[ANY, pl.BlockSpec((tb, D), lambda i: (i, 0)),
                  pl.BlockSpec((tb, D), lambda i: (i, 2)), pl.BlockSpec((tb, D), lambda i: (i, 3)), vec, vec, sq,
                  pl.BlockSpec((CHUNK, GROUPS), lambda i: (0, 0))],
        out_specs=[pl.BlockSpec((tb, 2 * D), lambda i: (i, 1)), sq, sq, pl.BlockSpec((8, D), lambda i: (0, 0))],
        out_shape=[SDS(dz.shape, BF16), SDS((GROUPS, CHUNK, CHUNK), F32), SDS((GROUPS, CHUNK, CHUNK), F32),
                   SDS((8, D), F32)],
        scratch_shapes=[pltpu.VMEM((tb, D), F32), pltpu.VMEM((tb, D), F32)], aliases={0: 0},
        params=_cparams(("arbitrary",), 40), args=(dz, dyb, z, z, ln_g, ln_b, w_s, b_s_t), comm=comm)


def _branch_a_bwd(dz, dya, z, hs, conv_w, conv_b, w_r, b_r, w_i, b_i, lam, comm=None):
    s = z.shape[0]
    ta = min(T_BRANCH, s)
    nb = s // ta
    per16 = ta // 16

    def body(dz_in, dya_ref, xa_ref, xp_ref, ga_ref, hs_ref, hp_ref, cw_ref, cb_ref, wr_ref, br_ref, wi_ref,
             bi_ref, lam_ref, dz_ref, vec_ref, dwr_ref, dwi_ref,
             a_s, b_s, h_s, r_s, i_s, m_s, dcar_s, acar_s, dxc_s):
        del dz_in
        i = pl.program_id(0)
        blk = nb - 1 - i

        @pl.when(i == 0)
        def _():
            dcar_s[...] = jnp.zeros_like(dcar_s)
            acar_s[...] = jnp.zeros_like(acar_s)
            dxc_s[...] = jnp.zeros_like(dxc_s)
            vec_ref[...] = jnp.zeros_like(vec_ref)
            dwr_ref[...] = jnp.zeros_like(dwr_ref)
            dwi_ref[...] = jnp.zeros_like(dwi_ref)

        cw = cw_ref[...]
        lam_v = lam_ref[...]
        xa = xa_ref[...].astype(F32)
        prev8 = jnp.where(blk > 0, xp_ref[...].astype(F32)[8:16], 0.0)
        xc = _conv_fwd(xa, prev8, cw, cb_ref[...])
        xcb = xc.astype(BF16)
        sp_lam = _softplus(-lam_v)
        _lru_gates(xc, xcb, wr_ref, br_ref[...], wi_ref, bi_ref[...], sp_lam, a_s, b_s, r_s, i_s, m_s)

        hs_v = hs_ref[...].astype(F32)
        hprev8 = jnp.where(blk > 0, hp_ref[...].astype(F32)[8:16], 0.0)
        h_m1 = _rows_shifted(hprev8, hs_v, 1)
        gg, dgg = _gelu_and_grad(ga_ref[...].astype(F32))
        dya_v = dya_ref[...].astype(F32)
        dz_ref[:, D:2 * D] = (dya_v * hs_v * dgg).astype(BF16)

        a_v = a_s[...]
        a_s[...] = _rows_advanced(a_v, acar_s[...], 1)
        b_s[...] = dya_v * gg

        row = lax.broadcasted_iota(jnp.int32, (8, D), 0)
        ng = ta // 8

        def group(gi, carry):
            off = pl.multiple_of((ng - 1 - gi) * 8, 8)
            c8 = a_s[pl.ds(off, 8), :]
            d8 = b_s[pl.ds(off, 8), :]
            for d in (1, 2, 4):
                c_sh = jnp.where(row < 8 - d, pltpu.roll(c8, 8 - d, 0), 1.0)
                d_sh = jnp.where(row < 8 - d, pltpu.roll(d8, 8 - d, 0), 0.0)
                d8 = c8 * d_sh + d8
                c8 = c8 * c_sh
            dh8 = d8 + c8 * carry
            h_s[pl.ds(off, 8), :] = dh8
            return jnp.broadcast_to(dh8[0:1, :], (8, D))

        dcar_s[...] = lax.fori_loop(0, ng, group, dcar_s[...])
        acar_s[...] = jnp.broadcast_to(a_v[0:1, :], (8, D))

        dbx = h_s[...]
        r_v, i_v, m_v = r_s[...], i_s[...], m_s[...]
        d_mult = dbx * xc * i_v
        d_loga = dbx * h_m1 * a_v - d_mult * (a_v * a_v) / m_v
        d_pr = d_loga * ((-LRU_C) * sp_lam) * r_v * (1.0 - r_v)
        d_pi = dbx * xc * m_v * i_v * (1.0 - i_v)
        vec_ref[7:8, :] += jnp.sum(d_loga * r_v, axis=0, keepdims=True) * (LRU_C * jax.nn.sigmoid(-lam_v))
        vec_ref[5:6, :] += jnp.sum(d_pr, axis=0, keepdims=True)
        vec_ref[6:7, :] += jnp.sum(d_pi, axis=0, keepdims=True)
        d_prb = d_pr.astype(BF16)
        d_pib = d_pi.astype(BF16)
        h_s[...] = dbx * i_v * m_v
        for h in range(HEADS):
            sl = slice(h * HEAD_DIM, (h + 1) * HEAD_DIM)
            h_s[:, sl] += _dot_nt(d_prb[:, sl], wr_ref[h]) + _dot_nt(d_pib[:, sl], wi_ref[h])
            dwr_ref[h] += _dot_tn(xcb[:, sl], d_prb[:, sl])
            dwi_ref[h] += _dot_tn(xcb[:, sl], d_pib[:, sl])
        d_xc = h_s[...]
        vec_ref[4:5, :] += jnp.sum(d_xc, axis=0, keepdims=True)
        vec_ref[0:1, :] += jnp.sum(d_xc * xa, axis=0, keepdims=True)
        d_xa = cw[0:1, :] * d_xc
        nxt = dxc_s[...]
        for k in range(1, CONV_K):
            vec_ref[k:k + 1, :] += jnp.sum(d_xc * _rows_shifted(prev8, xa, k), axis=0, keepdims=True)
            d_xa = d_xa + cw[k:k + 1, :] * _rows_advanced(d_xc, nxt, k)
        dz_ref[:, 0:D] = d_xa.astype(BF16)
        dxc_s[...] = d_xc[0:8, :]

    vec = pl.BlockSpec((1, D), lambda i: (0, 0))
    gate = pl.BlockSpec((HEADS, HEAD_DIM, HEAD_DIM), lambda i: (0, 0, 0))
    cur = lambda c: pl.BlockSpec((ta, D), lambda i: (nb - 1 - i, c))
    before = lambda c: pl.BlockSpec((16, D), lambda i: (jnp.maximum((nb - 1 - i) * per16 - 1, 0), c))
    return _call(
        body, name="branch_a_bwd", grid=(nb,),
        in_specs=[ANY, cur(0), cur(0), before(0), cur(1), cur(0), before(0),
                  pl.BlockSpec((CONV_K, D), lambda i: (0, 0)), vec, gate, vec, gate, vec, vec],
        out_specs=[pl.BlockSpec((ta, 2 * D), lambda i: (nb - 1 - i, 0)), pl.BlockSpec((8, D), lambda i: (0, 0)),
                   gate, gate],
        out_shape=[SDS(dz.shape, BF16), SDS((8, D), F32), SDS((HEADS, HEAD_DIM, HEAD_DIM), F32),
                   SDS((HEADS, HEAD_DIM, HEAD_DIM), F32)],
        scratch_shapes=[pltpu.VMEM((ta, D), F32)] * 6 + [pltpu.VMEM((8, D), F32)] * 3, aliases={0: 0},
        params=_cparams(("arbitrary",), 48),
        args=(dz, dya, z, z, z, hs, hs, conv_w, conv_b, w_r, b_r, w_i, b_i, lam), comm=comm)


def _in_bwd(dz, w_in_t, x, dh1, g_mix, comm=None):
    s = x.shape[0]
    tm = min(TM_MERGE, s)

    def body(dz_ref, w_ref, x_ref, dh1_ref, g_ref, dx_ref, dg_ref):
        @pl.when(pl.program_id(0) == 0)
        def _():
            dg_ref[...] = jnp.zeros_like(dg_ref)

        dn = _dot(dz_ref[...], w_ref[...])
        xv = x_ref[...]
        rstd = lax.rsqrt(jnp.mean(xv * xv, axis=-1, keepdims=True) + NORM_EPS)
        xh = xv * rstd
        dg_ref[...] += jnp.sum(dn * xh, axis=0, keepdims=True)
        dhat = dn * g_ref[...]
        dx_ref[...] = dh1_ref[...] + rstd * (dhat - xh * jnp.mean(dhat * xh, axis=-1, keepdims=True))

    cols = dz.shape[1]
    row = pl.BlockSpec((tm, D), lambda i: (i, 0))
    vec = pl.BlockSpec((1, D), lambda i: (0, 0))
    return _call(
        body, name="in_bwd", grid=(s // tm,),
        in_specs=[pl.BlockSpec((tm, cols), lambda i: (i, 0)), pl.BlockSpec((cols, D), lambda i: (0, 0)), row, row, vec],
        out_specs=[row, vec], out_shape=[SDS((s, D), F32), SDS((1, D), F32)], scratch_shapes=[],
        params=_cparams(("arbitrary",), 56), args=(dz, w_in_t, x, dh1, g_mix), comm=comm)


def _wgrad(name, a, b, nblk, a_split, b_split, tokens):
    s = a.shape[0]
    ts = min(tokens, s)
    a_w = a.shape[1] // nblk if a_split else a.shape[1]
    b_w = b.shape[1] // nblk if b_split else b.shape[1]

    def body(a_ref, b_ref, o_ref, acc_s):
        t = pl.program_id(1)

        @pl.when(t == 0)
        def _():
            acc_s[...] = jnp.zeros_like(acc_s)

        acc_s[...] += _dot_tn(a_ref[...].astype(BF16), b_ref[...].astype(BF16))

        @pl.when(t == pl.num_programs(1) - 1)
        def _():
            o_ref[...] = acc_s[...].astype(BF16)

    return pl.pallas_call(
        body, name=name, grid=(nblk, s // ts),
        in_specs=[pl.BlockSpec((ts, a_w), (lambda k, t: (t, k)) if a_split else (lambda k, t: (t, 0))),
                  pl.BlockSpec((ts, b_w), (lambda k, t: (t, k)) if b_split else (lambda k, t: (t, 0)))],
        out_specs=pl.BlockSpec((None, a_w, b_w), lambda k, t: (k, 0, 0)),
        out_shape=SDS((nblk, a_w, b_w), BF16),
        scratch_shapes=[pltpu.VMEM((a_w, b_w), F32)],
        compiler_params=_cparams(("arbitrary", "arbitrary"), 48),
    )(a, b)


def _wgrad_t(name, a_t, b, nblk, a_split, b_split, tokens):
    s = b.shape[0]
    ts = min(tokens, s)
    a_w = a_t.shape[0] // nblk if a_split else a_t.shape[0]
    b_w = b.shape[1] // nblk if b_split else b.shape[1]

    def body(a_ref, b_ref, o_ref, acc_s):
        t = pl.program_id(1)

        @pl.when(t == 0)
        def _():
            acc_s[...] = jnp.zeros_like(acc_s)

        acc_s[...] += _dot(a_ref[...], b_ref[...].astype(BF16))

        @pl.when(t == pl.num_programs(1) - 1)
        def _():
            o_ref[...] = acc_s[...].astype(BF16)

    return pl.pallas_call(
        body, name=name, grid=(nblk, s // ts),
        in_specs=[pl.BlockSpec((a_w, ts), (lambda k, t: (k, t)) if a_split else (lambda k, t: (0, t))),
                  pl.BlockSpec((ts, b_w), (lambda k, t: (t, k)) if b_split else (lambda k, t: (t, 0)))],
        out_specs=pl.BlockSpec((None, a_w, b_w), lambda k, t: (k, 0, 0)),
        out_shape=SDS((nblk, a_w, b_w), BF16),
        scratch_shapes=[pltpu.VMEM((a_w, b_w), F32)],
        compiler_params=_cparams(("arbitrary", "arbitrary"), 48),
    )(a_t, b)


def _place():
    x, y, c = lax.axis_index("x"), lax.axis_index("y"), lax.axis_index("c")
    return x, y, c


def _other_chips(x, y):
    return [(x, 1 - y, 2 * x + 1 - y), (1 - x, y, 2 * (1 - x) + y), (1 - x, 1 - y, 2 * (1 - x) + 1 - y)]


class _Plan:
    def __init__(self, arrays, out_shape, sems, start, finish):
        self.arrays, self.out_shape, self.sems, self.start, self.finish = arrays, out_shape, sems, start, finish


def _gather_plan(shards):
    n = len(shards)

    def copies(ins, outs, sems):
        send_sems, recv_sems, local_sems = sems
        x, y, c = _place()
        chip = 2 * x + y
        me = 2 * chip + c
        sib = (x, y, 1 - c)
        chips = _other_chips(x, y)

        def rc(k, t, src, blk, to):
            return pltpu.make_async_remote_copy(
                src_ref=src, dst_ref=outs[t].at[blk], send_sem=send_sems.at[k * n + t],
                recv_sem=recv_sems.at[k * n + t], device_id=to, device_id_type=MESH)

        local = [pltpu.make_async_copy(ins[t], outs[t].at[me], local_sems.at[t]) for t in range(n)]
        sends = [rc(0, t, ins[t], me, sib) for t in range(n)]
        for j, (px, py, _) in enumerate(chips):
            sends += [rc(1 + j, t, ins[t], me, (px, py, c)) for t in range(n)]
        return rc, local, sends, chips, chip, c, sib

    def start(ins, outs, sems):
        _, local, sends, _, _, _, _ = copies(ins, outs, sems)
        for cp in local + sends:
            cp.start()

    def finish(ins, outs, sems):
        rc, local, sends, chips, chip, c, sib = copies(ins, outs, sems)
        passed = []
        for j, (px, py, pc) in enumerate(chips):
            blk = 2 * pc + c
            for t in range(n):
                rc(1 + j, t, ins[t], blk, sib).wait_recv()
            for t in range(n):
                cp = rc(4 + j, t, outs[t].at[blk], blk, sib)
                cp.start()
                passed.append(cp)
        for t in range(n):
            rc(0, t, ins[t], 2 * chip + 1 - c, sib).wait_recv()
        for j, (px, py, pc) in enumerate(chips):
            for t in range(n):
                rc(4 + j, t, ins[t], 2 * pc + 1 - c, sib).wait_recv()
        for cp in sends + passed:
            cp.wait_send()
        for cp in local:
            cp.wait()

    return _Plan(list(shards), [SDS((N_SLOT,) + tuple(a.shape), a.dtype) for a in shards],
                 [pltpu.SemaphoreType.DMA((7 * n,)), pltpu.SemaphoreType.DMA((7 * n,)),
                  pltpu.SemaphoreType.DMA((n,))], start, finish)


def _sibling_plan(grads, whole=()):
    n, m = len(grads), len(whole)

    def copies(ins, outs, sems):
        send_sems, recv_sems = sems
        x, y, c = _place()
        sib = (x, y, 1 - c)

        def rc(t, src, dst):
            return pltpu.make_async_remote_copy(src_ref=src, dst_ref=dst, send_sem=send_sems.at[t],
                                                recv_sem=recv_sems.at[t], device_id=sib, device_id_type=MESH)
        return rc, c

    def start(ins, outs, sems):
        rc, c = copies(ins, outs, sems)
        for t in range(n):
            for j in range(4):
                rc(t, ins[t].at[2 * j + 1 - c], outs[t].at[j]).start()
        for t in range(n, n + m):
            rc(t, ins[t], outs[t]).start()

    def finish(ins, outs, sems):
        rc, _ = copies(ins, outs, sems)
        for t in range(n):
            rc(t, ins[t].at[pl.ds(0, 4)], outs[t]).wait()
        for t in range(n, n + m):
            rc(t, ins[t], outs[t]).wait()

    return _Plan(list(grads) + list(whole),
                 [SDS((4,) + tuple(g.shape[1:]), g.dtype) for g in grads] + [SDS(a.shape, a.dtype) for a in whole],
                 [pltpu.SemaphoreType.DMA((n + m,)), pltpu.SemaphoreType.DMA((n + m,))], start, finish)


def _chips_plan(parts, whole=()):
    n, m = len(parts), len(whole)

    def src_of(ins, t, pc):
        return ins[t].at[pc] if t < n else ins[t]

    def local_copies(ins, outs, sems, chip):
        return [pltpu.make_async_copy(src_of(ins, t, chip), outs[t].at[chip], sems[2].at[t]) for t in range(n + m)]

    def start(ins, outs, sems):
        send_sems, recv_sems, _ = sems
        x, y, c = _place()
        chip = 2 * x + y
        for cp in local_copies(ins, outs, sems, chip):
            cp.start()
        for px, py, pc in _other_chips(x, y):
            for t in range(n + m):
                pltpu.make_async_remote_copy(src_ref=src_of(ins, t, pc), dst_ref=outs[t].at[chip],
                                             send_sem=send_sems.at[t], recv_sem=recv_sems.at[t],
                                             device_id=(px, py, c), device_id_type=MESH).start()

    def finish(ins, outs, sems):
        send_sems, recv_sems, _ = sems
        x, y, c = _place()
        for t in range(n + m):
            three = outs[t].at[pl.ds(0, 3)]
            pltpu.make_async_remote_copy(src_ref=three, dst_ref=three, send_sem=send_sems.at[t],
                                         recv_sem=recv_sems.at[t], device_id=(x, y, c), device_id_type=MESH).wait()
        for cp in local_copies(ins, outs, sems, 2 * x + y):
            cp.wait()

    return _Plan(list(parts) + list(whole),
                 [SDS(p.shape, p.dtype) for p in parts] + [SDS((4,) + tuple(a.shape), a.dtype) for a in whole],
                 [pltpu.SemaphoreType.DMA((n + m,)), pltpu.SemaphoreType.DMA((n + m,)),
                  pltpu.SemaphoreType.DMA((n + m,))], start, finish)


def _exchange_plan(arr):
    def peers(x, y, c):
        flip = lambda v, f: 1 - v if f else v
        return [(flip(x, fx), flip(y, fy), flip(c, fc))
                for fx in (0, 1) for fy in (0, 1) for fc in (0, 1) if fx or fy or fc]

    def start(ins, outs, sems):
        x, y, c = _place()
        me = 4 * x + 2 * y + c
        pltpu.make_async_copy(ins[0], outs[0].at[me], sems[2].at[0]).start()
        for to in peers(x, y, c):
            pltpu.make_async_remote_copy(src_ref=ins[0], dst_ref=outs[0].at[me], send_sem=sems[0].at[0],
                                         recv_sem=sems[1].at[0], device_id=to, device_id_type=MESH).start()

    def finish(ins, outs, sems):
        x, y, c = _place()
        seven = outs[0].at[pl.ds(0, 7)]
        pltpu.make_async_remote_copy(src_ref=seven, dst_ref=seven, send_sem=sems[0].at[0], recv_sem=sems[1].at[0],
                                     device_id=(x, y, c), device_id_type=MESH).wait()
        pltpu.make_async_copy(ins[0], outs[0].at[4 * x + 2 * y + c], sems[2].at[0]).wait()

    return _Plan([arr], [SDS((N_SLOT,) + tuple(arr.shape), arr.dtype)],
                 [pltpu.SemaphoreType.DMA((1,)), pltpu.SemaphoreType.DMA((1,)), pltpu.SemaphoreType.DMA((1,))],
                 start, finish)


def _join(*plans):
    def cut(seq, sizes):
        out, at = [], 0
        for k in sizes:
            out.append(seq[at:at + k])
            at += k
        return out

    n_arr = [len(p.arrays) for p in plans]
    n_sem = [len(p.sems) for p in plans]

    def start(ins, outs, sems):
        for p, i, o, s in zip(plans, cut(ins, n_arr), cut(outs, n_arr), cut(sems, n_sem)):
            p.start(i, o, s)

    def finish(ins, outs, sems):
        for p, i, o, s in zip(plans, cut(ins, n_arr), cut(outs, n_arr), cut(sems, n_sem)):
            p.finish(i, o, s)

    return _Plan([a for p in plans for a in p.arrays], [o for p in plans for o in p.out_shape],
                 [s for p in plans for s in p.sems], start, finish)


def _run_plan(name, plan):
    k = len(plan.arrays)

    def body(*refs):
        ins, outs, sems = refs[:k], refs[k:2 * k], refs[2 * k:]
        plan.start(ins, outs, sems)
        plan.finish(ins, outs, sems)

    return pl.pallas_call(
        body, name=name, in_specs=[ANY] * k, out_specs=[ANY] * k, out_shape=plan.out_shape,
        scratch_shapes=plan.sems, compiler_params=pltpu.CompilerParams(has_side_effects=True),
    )(*plan.arrays)


def _call(body, *, name, grid, in_specs, out_specs, out_shape, scratch_shapes, params, args, comm=None,
          aliases=None, prefetch=()):
    aliases = aliases or {}
    n_pre = len(prefetch)

    def launch(fn, ins_specs, outs_specs, outs_shape, scratch, operands):
        spec = pltpu.PrefetchScalarGridSpec(num_scalar_prefetch=n_pre, grid=grid, in_specs=ins_specs,
                                            out_specs=outs_specs, scratch_shapes=scratch)
        return pl.pallas_call(fn, name=name, grid_spec=spec, out_shape=outs_shape, compiler_params=params,
                              input_output_aliases=aliases)(*prefetch, *operands)

    if comm is None:
        return list(launch(body, in_specs, out_specs, out_shape, scratch_shapes, args)), []
    n_in, n_out, n_scr, k = len(in_specs), len(out_specs), len(scratch_shapes), len(comm.arrays)

    def wrapped(*refs):
        pre, refs = refs[:n_pre], refs[n_pre:]
        ins = refs[:n_in]
        c_in = refs[n_in:n_in + k]
        outs = refs[n_in + k:n_in + k + n_out]
        c_out = refs[n_in + k + n_out:n_in + 2 * k + n_out]
        scr = refs[n_in + 2 * k + n_out:n_in + 2 * k + n_out + n_scr]
        sems = refs[n_in + 2 * k + n_out + n_scr:]
        ids = [pl.program_id(d) for d in range(len(grid))]
        first = ids[0] == 0
        last = ids[0] == grid[0] - 1
        for d in range(1, len(grid)):
            first = first & (ids[d] == 0)
            last = last & (ids[d] == grid[d] - 1)

        @pl.when(first)
        def _():
            comm.start(c_in, c_out, sems)

        body(*pre, *ins, *outs, *scr)

        @pl.when(last)
        def _():
            comm.finish(c_in, c_out, sems)

    res = launch(wrapped, list(in_specs) + [ANY] * k, list(out_specs) + [ANY] * k,
                 list(out_shape) + list(comm.out_shape), list(scratch_shapes) + list(comm.sems),
                 tuple(args) + tuple(comm.arrays))
    return list(res[:n_out]), list(res[n_out:])


def _row_tile(rows):
    for t in (512, 256, 128, 64, 32, 16, 8):
        if rows % t == 0:
            return t
    return rows


def _pair_sum(name, g8, recv4, core):
    _, rows, cols = recv4.shape
    tr = _row_tile(rows)
    g42 = g8.reshape(4, 2, rows, cols)

    def body(c_ref, g_ref, r_ref, o_ref):
        del c_ref
        o_ref[...] = (g_ref[...].astype(F32) + r_ref[...].astype(F32)).astype(o_ref.dtype)

    return pl.pallas_call(
        body, name=name,
        grid_spec=pltpu.PrefetchScalarGridSpec(
            num_scalar_prefetch=1, grid=(4, rows // tr),
            in_specs=[pl.BlockSpec((None, None, tr, cols), lambda j, i, c_ref: (j, c_ref[0], i, 0)),
                      pl.BlockSpec((None, tr, cols), lambda j, i, c_ref: (j, i, 0))],
            out_specs=pl.BlockSpec((None, tr, cols), lambda j, i, c_ref: (j, i, 0))),
        out_shape=SDS(recv4.shape, g8.dtype),
        compiler_params=_cparams(("arbitrary", "arbitrary"), 32),
    )(core, g42, recv4)


def _add2(name, a, b):
    rows, cols = a.shape
    tr = _row_tile(rows)

    def body(a_ref, b_ref, o_ref):
        o_ref[...] = a_ref[...] + b_ref[...]

    blk = pl.BlockSpec((tr, cols), lambda i: (i, 0))
    return pl.pallas_call(body, name=name, grid=(rows // tr,), in_specs=[blk, blk], out_specs=blk,
                          out_shape=SDS(a.shape, a.dtype),
                          compiler_params=_cparams(("arbitrary",), 32))(a, b)


def _sum_terms(name, terms):
    k, rows, cols = terms.shape
    tr = _row_tile(rows)

    def body(r_ref, o_ref):
        acc = r_ref[0]
        for q in range(1, k):
            acc = acc + r_ref[q]
        o_ref[...] = acc

    return pl.pallas_call(body, name=name, grid=(rows // tr,),
                          in_specs=[pl.BlockSpec((k, tr, cols), lambda i: (0, i, 0))],
                          out_specs=pl.BlockSpec((tr, cols), lambda i: (i, 0)),
                          out_shape=SDS((rows, cols), terms.dtype),
                          compiler_params=_cparams(("arbitrary",), 32))(terms)


def _adam_update(g, w, m, v):
    c1 = 1.0 / (1.0 - ADAM_B1 ** ADAM_STEP)
    c2 = 1.0 / (1.0 - ADAM_B2 ** ADAM_STEP)
    mn = ADAM_B1 * m + (1.0 - ADAM_B1) * g
    vn = ADAM_B2 * v + (1.0 - ADAM_B2) * (g * g)
    delta = (-ADAM_LR) * ((mn * c1) / (jnp.sqrt(vn * c2) + ADAM_EPS) + ADAM_WD * w)
    return delta, mn, vn


def _adamw_many(name, gs, ws, ms, vs):
    n = len(gs)

    def body(*refs):
        for p in range(n):
            g, w, m, v = (refs[q * n + p][...] for q in range(4))
            d, mn, vn = _adam_update(g, w, m, v)
            refs[4 * n + p][...] = d
            refs[5 * n + p][...] = mn
            refs[6 * n + p][...] = vn

    full = [pl.BlockSpec(memory_space=pltpu.VMEM)] * n
    shapes = [SDS(w.shape, F32) for w in ws]
    res = pl.pallas_call(body, name=name, in_specs=full * 4, out_specs=full * 3, out_shape=shapes * 3,
                         compiler_params=pltpu.CompilerParams(vmem_limit_bytes=32 * MiB))(*gs, *ws, *ms, *vs)
    return [(res[p], res[n + p], res[2 * n + p]) for p in range(n)]


def _adamw(name, terms, w, m, v):
    k, rows, cols = terms.shape
    tr = _row_tile(rows)

    def body(t_ref, w_ref, m_ref, v_ref, g_ref, d_ref, mo_ref, vo_ref):
        g = t_ref[0].astype(F32)
        for q in range(1, k):
            g = g + t_ref[q].astype(F32)
        g_ref[...] = g
        d_ref[...], mo_ref[...], vo_ref[...] = _adam_update(g, w_ref[...], m_ref[...], v_ref[...])

    blk = pl.BlockSpec((tr, cols), lambda i: (i, 0))
    return pl.pallas_call(body, name=name, grid=(rows // tr,),
                          in_specs=[pl.BlockSpec((k, tr, cols), lambda i: (0, i, 0)), blk, blk, blk],
                          out_specs=[blk] * 4, out_shape=[SDS((rows, cols), F32)] * 4,
                          compiler_params=_cparams(("arbitrary",), 40))(terms, w, m, v)


def kernel(x, norm_mix_g, w_in, conv_w, conv_b, w_rgate, b_rgate, w_igate, b_igate, lru_lambda, w_out_a, sgu_ln_g, sgu_ln_b, sgu_w_s, sgu_b_s, w_out_b, w_out, norm_mlp_g, w_up, w_down, norm_final_g, loss_target, m_norm_mix_g, m_w_in, m_conv_w, m_conv_b, m_w_rgate, m_b_rgate, m_w_igate, m_b_igate, m_lru_lambda, m_w_out_a, m_sgu_ln_g, m_sgu_ln_b, m_sgu_w_s, m_sgu_b_s, m_w_out_b, m_w_out, m_norm_mlp_g, m_w_up, m_w_down, m_norm_final_g, v_norm_mix_g, v_w_in, v_conv_w, v_conv_b, v_w_rgate, v_b_rgate, v_w_igate, v_b_igate, v_lru_lambda, v_w_out_a, v_sgu_ln_g, v_sgu_ln_b, v_sgu_w_s, v_sgu_b_s, v_w_out_b, v_w_out, v_norm_mlp_g, v_w_up, v_w_down, v_norm_final_g):
    cx, cy, cc = _place()
    me = 4 * cx + 2 * cy + cc
    core = jnp.reshape(cc, (1,)).astype(jnp.int32)
    xs = x[0]
    tgt = loss_target[0]
    s = xs.shape[0]

    gate_shard = jnp.stack([w_rgate[0], w_igate[0]]).astype(BF16).reshape(2 * HEADS * 32, HEAD_DIM)
    vec_shard = jnp.concatenate([conv_w[0], b_rgate[0], b_igate[0]], axis=1)
    vec_shard = jnp.pad(vec_shard, ((0, 4), (0, 256 - vec_shard.shape[1])))
    shards = [w_in[0].astype(BF16).T, w_out_a[0].astype(BF16), w_out_b[0].astype(BF16), w_out[0].astype(BF16),
              w_up[0].astype(BF16).T, w_down[0].astype(BF16), gate_shard, vec_shard]
    (z, n1_t, w_in_tg), (gate_g, vec_g) = _in_proj(xs, norm_mix_g, shards[0], _slot_order(cx, cy, cc),
                                                comm=_gather_plan(shards[6:8]))
    gates = gate_g.reshape(N_SLOT, 2, HEADS, 32, HEAD_DIM).transpose(1, 2, 0, 3, 4).reshape(2, HEADS, HEAD_DIM, HEAD_DIM)
    w_r_f, w_i_f = gates[0], gates[1]
    conv_w_f = vec_g[:, 0:4, 0:128].transpose(1, 0, 2).reshape(CONV_K, D)
    b_r_f = vec_g[:, 0:4, 128:160].transpose(1, 0, 2).reshape(1, D)
    b_i_f = vec_g[:, 0:4, 160:192].transpose(1, 0, 2).reshape(1, D)
    b_s_t = jnp.transpose(sgu_b_s[0])

    (ya, hs), (w_oa_g, w_ob_g, w_out_g) = _branch_a_fwd(
        z, conv_w_f, conv_b, w_r_f, b_r_f, w_i_f, b_i_f, lru_lambda, comm=_gather_plan(shards[1:4]))
    w_oa_f = w_oa_g.reshape(D, D)
    w_ob_f = w_ob_g.reshape(D, D)
    w_out_f = w_out_g.reshape(D, D)
    (yb,), _ = _branch_b_fwd(z, sgu_ln_g, sgu_ln_b, sgu_w_s[0], b_s_t)
    (pa, pb, merged, h1), (w_up_tg,) = _merge_out(ya, yb, z, xs, w_oa_f, w_ob_f, w_out_f,
                                                  comm=_gather_plan(shards[4:5]))
    w_up_t = w_up_tg.reshape(N_SLOT * FF_COLS, D)
    (r_act, act, act_t, n2_t), (w_down_g,) = _mlp_up(h1, norm_mlp_g, w_up_t, comm=_gather_plan(shards[5:6]))
    w_down_f = w_down_g.reshape(N_SLOT * FF_COLS, D)
    dh2, loss_acc, d_gfin = _mlp_down(act, w_down_f, h1, norm_final_g.reshape(1, D), tgt)
    w_in_t = w_in_tg.reshape(N_SLOT * W_IN_COLS, D)

    def pair(names, grads, recv):
        return [_pair_sum("pair_sum_" + nm, g, r, core) for nm, g, r in zip(names, grads, recv)]

    g_down = _wgrad_t("wgrad_down", act_t, dh2, N_SLOT // 2, True, False, 2048)
    g_down = g_down.reshape(N_SLOT, FF_COLS, D)
    (df,), (r_down,) = _mlp_bwd_ff(dh2, r_act, w_down_f, comm=_sibling_plan([g_down]))
    (p_down,) = pair(["down"], [g_down], [r_down])
    g_up = _wgrad_t("wgrad_up", n2_t, df, N_SLOT, False, True, 4096)
    (dh1, d_gmlp), (got_down,) = _mlp_bwd_in(df, w_up_t, dh2, h1, norm_mlp_g, comm=_chips_plan([p_down]))
    g_out = _wgrad("wgrad_out", merged, dh1, 1, False, False, 2048).reshape(N_SLOT, D // N_SLOT, D)
    (dz, dpa, dpb, dya, dyb), (r_up, r_out) = _merge_bwd(
        dh1, z, pa, pb, w_out_f, w_oa_f, w_ob_f, comm=_sibling_plan([g_up, g_out]))
    p_up, p_out = pair(["up", "out"], [g_up, g_out], [r_up, r_out])
    g_oa = _wgrad("wgrad_out_a", ya, dpa, 1, False, False, 2048).reshape(N_SLOT, D // N_SLOT, D)
    g_ob = _wgrad("wgrad_out_b", yb, dpb, 1, False, False, 2048).reshape(N_SLOT, D // N_SLOT, D)
    (dz, d_ws, d_bs, d_ln), (got_up, r_oa, r_ob) = _branch_b_bwd(
        dz, dyb, z, sgu_ln_g, sgu_ln_b, sgu_w_s[0], b_s_t,
        comm=_join(_chips_plan([p_up]), _sibling_plan([g_oa, g_ob])))
    p_oa, p_ob = pair(["out_a", "out_b"], [g_oa, g_ob], [r_oa, r_ob])
    (dz, d_vec, d_wr, d_wi), (got_out, got_oa, got_ob) = _branch_a_bwd(
        dz, dya, z, hs, conv_w_f, conv_b, w_r_f, b_r_f, w_i_f, b_i_f, lru_lambda,
        comm=_chips_plan([p_out, p_oa, p_ob]))
    g_in = _wgrad_t("wgrad_in", n1_t, dz, N_SLOT, False, True, 4096)
    g_gate = jnp.stack([d_wr, d_wi]).reshape(2, HEADS, N_SLOT, 32, HEAD_DIM).transpose(2, 0, 1, 3, 4)
    g_gate = g_gate.reshape(N_SLOT, 2 * HEADS * 32, HEAD_DIM).astype(BF16)

    d_bs_row = jnp.pad(d_bs[:, :, 0].reshape(1, GROUPS * CHUNK), ((0, 0), (0, D - GROUPS * CHUNK)))
    vecs = jnp.concatenate([d_vec, jnp.concatenate([d_ln[0:2], d_gmlp, d_gfin, d_bs_row, jnp.zeros((3, D), F32)])])
    d_ws2 = d_ws.reshape(GROUPS * CHUNK, CHUNK)
    r_in, r_gate, r_vecs, r_ws = _run_plan("rs_sibling_in", _sibling_plan([g_in, g_gate], [vecs, d_ws2]))
    p_in, p_gate = pair(["in", "gate"], [g_in, g_gate], [r_in, r_gate])
    vecs_chip = _add2("pair_sum_vecs", vecs, r_vecs)
    ws_chip = _add2("pair_sum_ws", d_ws2, r_ws)
    (dx, d_gmix), (got_in, got_gate, got_vecs, got_ws) = _in_bwd(
        dz, w_in_t, xs, dh1, norm_mix_g, comm=_chips_plan([p_in, p_gate], [vecs_chip, ws_chip]))
    vecs_sum = _sum_terms("sum_vecs", got_vecs)
    last = jnp.concatenate([d_gmix, jnp.pad(loss_acc[0:1], ((0, 0), (0, D - 128))), jnp.zeros((6, D), F32)])
    (last_all,) = _run_plan("exchange_last", _exchange_plan(last))
    last_sum = _sum_terms("sum_last", last_all)
    loss = last_sum[1, 0]
    got = [got_in, got_oa, got_ob, got_out, got_up, got_down, got_gate]

    def step(nm, terms, w, m, v, rows, cols):
        g, d, mn, vn = _adamw("adamw_" + nm, terms.reshape(4, rows, cols), w.reshape(rows, cols),
                              m.reshape(rows, cols), v.reshape(rows, cols))
        return [a.reshape(w.shape) for a in (g, d, mn, vn)]

    o_in = step("in", got[0], w_in, m_w_in, v_w_in, D, W_IN_COLS)
    o_oa = step("out_a", got[1], w_out_a, m_w_out_a, v_w_out_a, D // N_SLOT, D)
    o_ob = step("out_b", got[2], w_out_b, m_w_out_b, v_w_out_b, D // N_SLOT, D)
    o_out = step("out", got[3], w_out, m_w_out, v_w_out, D // N_SLOT, D)
    o_up = step("up", got[4], w_up, m_w_up, v_w_up, D, FF_COLS)
    o_down = step("down", got[5], w_down, m_w_down, v_w_down, FF_COLS, D)
    gate_w = jnp.stack([w_rgate[0], w_igate[0]]).reshape(2 * HEADS * 32, HEAD_DIM)
    gate_m = jnp.stack([m_w_rgate[0], m_w_igate[0]]).reshape(2 * HEADS * 32, HEAD_DIM)
    gate_v = jnp.stack([v_w_rgate[0], v_w_igate[0]]).reshape(2 * HEADS * 32, HEAD_DIM)
    o_gate = _adamw("adamw_gate", got[6], gate_w, gate_m, gate_v)
    o_gate = [a.reshape(2, 1, HEADS, 32, HEAD_DIM) for a in o_gate]
    o_wr = [a[0] for a in o_gate]
    o_wi = [a[1] for a in o_gate]

    def own(full, width):
        return lax.dynamic_slice_in_dim(full, me * width, width, axis=1)

    small_g = {
        "norm_mix_g": last_sum[0:1], "conv_w": own(vecs_sum[0:4], 128), "conv_b": vecs_sum[4:5],
        "b_rgate": own(vecs_sum[5:6].reshape(HEADS, HEAD_DIM), 32),
        "b_igate": own(vecs_sum[6:7].reshape(HEADS, HEAD_DIM), 32),
        "lru_lambda": vecs_sum[7:8], "sgu_ln_g": vecs_sum[8:9], "sgu_ln_b": vecs_sum[9:10],
        "norm_mlp_g": vecs_sum[10:11], "norm_final_g": vecs_sum[11:12],
        "sgu_b_s": vecs_sum[12, 0:GROUPS * CHUNK].reshape(GROUPS, CHUNK),
    }
    small_w = {"norm_mix_g": (norm_mix_g, m_norm_mix_g, v_norm_mix_g), "conv_w": (conv_w, m_conv_w, v_conv_w),
               "conv_b": (conv_b, m_conv_b, v_conv_b), "b_rgate": (b_rgate, m_b_rgate, v_b_rgate),
               "b_igate": (b_igate, m_b_igate, v_b_igate), "lru_lambda": (lru_lambda, m_lru_lambda, v_lru_lambda),
               "sgu_ln_g": (sgu_ln_g, m_sgu_ln_g, v_sgu_ln_g), "sgu_ln_b": (sgu_ln_b, m_sgu_ln_b, v_sgu_ln_b),
               "norm_mlp_g": (norm_mlp_g, m_norm_mlp_g, v_norm_mlp_g),
               "norm_final_g": (norm_final_g, m_norm_final_g, v_norm_final_g),
               "sgu_b_s": (sgu_b_s, m_sgu_b_s, v_sgu_b_s), "sgu_w_s": (sgu_w_s, m_sgu_w_s, v_sgu_w_s)}
    order = list(small_g)
    as2d = lambda k, a: a.reshape(small_g[k].shape)
    upd = _adamw_many("adamw_small", [small_g[k] for k in order], *[[as2d(k, small_w[k][q]) for k in order]
                                                                     for q in range(3)])
    o_small = {k: [a.reshape(small_w[k][0].shape) for a in (small_g[k],) + u] for k, u in zip(order, upd)}
    ws3 = [a[0].reshape(GROUPS * CHUNK, CHUNK) for a in small_w.pop("sgu_w_s")]
    o_small["sgu_w_s"] = [a.reshape(sgu_w_s.shape) for a in _adamw("adamw_ws", got_ws, *ws3)]

    per_weight = {"norm_mix_g": o_small["norm_mix_g"], "w_in": o_in, "conv_w": o_small["conv_w"],
                  "conv_b": o_small["conv_b"], "w_rgate": o_wr, "b_rgate": o_small["b_rgate"], "w_igate": o_wi,
                  "b_igate": o_small["b_igate"], "lru_lambda": o_small["lru_lambda"], "w_out_a": o_oa,
                  "sgu_ln_g": o_small["sgu_ln_g"], "sgu_ln_b": o_small["sgu_ln_b"], "sgu_w_s": o_small["sgu_w_s"],
                  "sgu_b_s": o_small["sgu_b_s"], "w_out_b": o_ob, "w_out": o_out, "norm_mlp_g": o_small["norm_mlp_g"],
                  "w_up": o_up, "w_down": o_down, "norm_final_g": o_small["norm_final_g"]}
    names_w = list(per_weight)
    return (loss, dx[None], *[per_weight[k][0] for k in names_w], *[per_weight[k][1] for k in names_w],
            *[per_weight[k][2] for k in names_w], *[per_weight[k][3] for k in names_w])
```

```python
import jax
import jax.numpy as jnp
from jax import lax
from jax.experimental import pallas as pl
from jax.experimental.pallas import tpu as pltpu

F32 = jnp.float32
BF16 = jnp.bfloat16
SDS = jax.ShapeDtypeStruct
MESH = pl.DeviceIdType.MESH
ANY = pl.BlockSpec(memory_space=pltpu.HBM)

D = 1024
N_SLOT = 8
W_IN_COLS = 768
WG_COLS = 256
FF_COLS = 512
HEADS, HEAD_DIM = 4, 256
GROUPS, GROUP_DIM = 4, 256
CHUNK = 128
CONV_K = 4
NORM_EPS = 1e-6
LN_EPS = 1e-5
LRU_C = 8.0
ADAM_LR, ADAM_B1, ADAM_B2, ADAM_EPS, ADAM_WD, ADAM_STEP = 0.001, 0.9, 0.999, 1e-08, 0.01, 10

TM_ROWS = 1024
TM_MERGE = 512
T_BRANCH_A = 256
T_BRANCH_B = 256
MiB = 1024 * 1024

_GELU_C = 0.7978845608028654
_GELU_A = 0.044715


def _cparams(sem, vmem_mib):
    return pltpu.CompilerParams(dimension_semantics=sem, vmem_limit_bytes=vmem_mib * MiB)


def _gelu(x):
    t = jnp.tanh(_GELU_C * (x + _GELU_A * x * x * x))
    return 0.5 * x * (1.0 + t)


def _gelu_and_grad(x):
    x2 = x * x
    t = jnp.tanh(_GELU_C * x * (1.0 + _GELU_A * x2))
    g = 0.5 * x * (1.0 + t)
    dg = 0.5 * (1.0 + t) + 0.5 * x * (1.0 - t * t) * _GELU_C * (1.0 + 3.0 * _GELU_A * x2)
    return g, dg


def _softplus(x):
    return jnp.maximum(x, 0.0) + jnp.log1p(jnp.exp(-jnp.abs(x)))


def _dot(a, b):
    return jnp.dot(a, b, preferred_element_type=F32)


def _dot_nt(a, b):
    return lax.dot_general(a, b, (((1,), (1,)), ((), ())), preferred_element_type=F32)


def _dot_tn(a, b):
    return lax.dot_general(a, b, (((0,), (0,)), ((), ())), preferred_element_type=F32)


def _rows_shifted(prev8, cur, k):
    ext = jnp.concatenate([prev8, cur], axis=0)
    return pltpu.roll(ext, k, 0)[8:]


def _rows_advanced(cur, next8, k):
    t = cur.shape[0]
    ext = jnp.concatenate([cur, next8], axis=0)
    return pltpu.roll(ext, t + 8 - k, 0)[:t]


def _slot_order(x, y, c):
    chip = 2 * x + y
    order = [2 * chip + c, 2 * chip + 1 - c]
    for _, _, pc in _other_chips(x, y):
        order += [2 * pc + c, 2 * pc + 1 - c]
    return jnp.stack(order).astype(jnp.int32)


def _in_proj(x, g_mix, w_in_own, order, comm=None):
    s = x.shape[0]
    tm = min(TM_ROWS, s)
    ni = s // tm

    def body(order_ref, x_ref, g_ref, own_ref, z_ref, nt_ref, wg_ref, n_s, w_s, send_sems, recv_sems, local_sems):
        j, i = pl.program_id(0), pl.program_id(1)
        px, py, c = _place()
        chip = 2 * px + py
        me = 2 * chip + c
        sib = (px, py, 1 - c)
        chips = _other_chips(px, py)

        def rc(k, src, blk, to):
            return pltpu.make_async_remote_copy(src_ref=src, dst_ref=w_s.at[blk], send_sem=send_sems.at[k],
                                                recv_sem=recv_sems.at[k], device_id=to, device_id_type=MESH)

        own_in = pltpu.make_async_copy(own_ref, w_s.at[me], local_sems.at[0])
        sends = [rc(0, own_ref, me, sib)] + [rc(1 + q, own_ref, me, (qx, qy, c)) for q, (qx, qy, _) in enumerate(chips)]
        passed = [rc(4 + q, w_s.at[2 * pc + c], 2 * pc + c, sib) for q, (_, _, pc) in enumerate(chips)]
        keep = pltpu.make_async_copy(w_s, wg_ref, local_sems.at[1])

        @pl.when((i == 0) & (j == 0))
        def _():
            own_in.start()
            for cp in sends:
                cp.start()
            own_in.wait()

        @pl.when((i == 0) & (j == 1))
        def _():
            rc(0, own_ref, 2 * chip + 1 - c, sib).wait_recv()

        for q, (_, _, pc) in enumerate(chips):
            @pl.when((i == 0) & (j == 2 + 2 * q))
            def _():
                rc(1 + q, own_ref, 2 * pc + c, sib).wait_recv()
                passed[q].start()

            @pl.when((i == 0) & (j == 3 + 2 * q))
            def _():
                rc(4 + q, own_ref, 2 * pc + 1 - c, sib).wait_recv()

        rows = pl.ds(pl.multiple_of(i * tm, tm), tm)

        @pl.when(j == 0)
        def _():
            xv = x_ref[...]
            rstd = lax.rsqrt(jnp.mean(xv * xv, axis=-1, keepdims=True) + NORM_EPS)
            nb = (xv * rstd * g_ref[...]).astype(BF16)
            n_s[rows, :] = nb
            nt_ref[...] = nb.T

        z_ref[...] = _dot(n_s[rows, :], w_s[order_ref[j]]).astype(BF16)

        @pl.when((i == 0) & (j == N_SLOT - 1))
        def _():
            keep.start()

        @pl.when((i == ni - 1) & (j == N_SLOT - 1))
        def _():
            for cp in sends + passed:
                cp.wait_send()
            keep.wait()

    first_pass = lambda j, i, o: (jnp.where(j == 0, i, ni - 1), 0)
    (z, n1, w_in_g), extra = _call(
        body, name="in_proj", grid=(N_SLOT, ni), prefetch=(order,),
        in_specs=[pl.BlockSpec((tm, D), first_pass),
                  pl.BlockSpec((1, D), lambda j, i, o: (0, 0)), ANY],
        out_specs=[pl.BlockSpec((tm, W_IN_COLS), lambda j, i, o: (i, o[j])),
                   pl.BlockSpec((D, tm), lambda j, i, o: (0, jnp.where(j == 0, i, ni - 1))), ANY],
        out_shape=[SDS((s, N_SLOT * W_IN_COLS), BF16), SDS((D, s), BF16), SDS((N_SLOT, D, W_IN_COLS), BF16)],
        scratch_shapes=[pltpu.VMEM((s, D), BF16), pltpu.VMEM((N_SLOT, D, W_IN_COLS), BF16),
                        pltpu.SemaphoreType.DMA((7,)), pltpu.SemaphoreType.DMA((7,)), pltpu.SemaphoreType.DMA((2,))],
        params=_cparams(("arbitrary", "arbitrary"), 56), args=(x, g_mix, w_in_own), comm=comm)
    return (z, n1, w_in_g), extra


def _lru_gates(xc, xcb, wr_ref, br, wi_ref, bi, sp_lam, a_s, b_s, r_s=None, i_s=None, m_s=None):
    for h in range(HEADS):
        sl = slice(h * HEAD_DIM, (h + 1) * HEAD_DIM)
        r = jax.nn.sigmoid(_dot(xcb[:, sl], wr_ref[h]) + br[:, sl])
        ig = jax.nn.sigmoid(_dot(xcb[:, sl], wi_ref[h]) + bi[:, sl])
        log_a = (-LRU_C) * r * sp_lam[:, sl]
        a = jnp.exp(log_a)
        mult = jnp.sqrt(-jnp.tanh(log_a) * (a * a + 1.0))
        a_s[:, sl] = a
        b_s[:, sl] = xc[:, sl] * ig * mult
        if r_s is not None:
            r_s[:, sl] = r
            i_s[:, sl] = ig
            m_s[:, sl] = mult


def _conv_fwd(xa, prev8, cw, cb):
    xc = cb + cw[0:1, :] * xa
    for k in range(1, CONV_K):
        xc = xc + cw[k:k + 1, :] * _rows_shifted(prev8, xa, k)
    return xc


def _branch_a_fwd(z, conv_w, conv_b, w_r, b_r, w_i, b_i, lam, comm=None):
    s = z.shape[0]
    ta = min(T_BRANCH_A, s)
    per16 = ta // 16

    def body(xa_ref, xp_ref, ga_ref, cw_ref, cb_ref, wr_ref, br_ref, wi_ref, bi_ref, lam_ref,
             ya_ref, hs_ref, a_s, b_s, h_s, carry_s):
        i = pl.program_id(0)

        @pl.when(i == 0)
        def _():
            carry_s[...] = jnp.zeros_like(carry_s)

        xa = xa_ref[...].astype(F32)
        prev8 = jnp.where(i > 0, xp_ref[...].astype(F32)[8:16], 0.0)
        xc = _conv_fwd(xa, prev8, cw_ref[...], cb_ref[...])
        sp_lam = _softplus(-lam_ref[...])
        _lru_gates(xc, xc.astype(BF16), wr_ref, br_ref[...], wi_ref, bi_ref[...], sp_lam, a_s, b_s)

        row = lax.broadcasted_iota(jnp.int32, (8, D), 0)

        def group(g, carry):
            off = pl.multiple_of(g * 8, 8)
            a8 = a_s[pl.ds(off, 8), :]
            b8 = b_s[pl.ds(off, 8), :]
            for d in (1, 2, 4):
                a_sh = jnp.where(row >= d, pltpu.roll(a8, d, 0), 1.0)
                b_sh = jnp.where(row >= d, pltpu.roll(b8, d, 0), 0.0)
                b8 = a8 * b_sh + b8
                a8 = a8 * a_sh
            h8 = b8 + a8 * carry
            h_s[pl.ds(off, 8), :] = h8
            return jnp.broadcast_to(h8[7:8, :], (8, D))

        carry_s[...] = lax.fori_loop(0, ta // 8, group, carry_s[...])
        hs = h_s[...]
        hs_ref[...] = hs.astype(BF16)
        ya_ref[...] = (hs * _gelu(ga_ref[...].astype(F32))).astype(BF16)

    vec = pl.BlockSpec((1, D), lambda i: (0, 0))
    gate = pl.BlockSpec((HEADS, HEAD_DIM, HEAD_DIM), lambda i: (0, 0, 0))
    return _call(
        body, name="branch_a_fwd", grid=(s // ta,),
        in_specs=[pl.BlockSpec((ta, D), lambda i: (i, 0)),
                  pl.BlockSpec((16, D), lambda i: (jnp.maximum(i * per16 - 1, 0), 0)),
                  pl.BlockSpec((ta, D), lambda i: (i, 1)),
                  pl.BlockSpec((CONV_K, D), lambda i: (0, 0)), vec, gate, vec, gate, vec, vec],
        out_specs=[pl.BlockSpec((ta, D), lambda i: (i, 0)), pl.BlockSpec((ta, D), lambda i: (i, 0))],
        out_shape=[SDS((s, D), BF16), SDS((s, D), BF16)],
        scratch_shapes=[pltpu.VMEM((ta, D), F32), pltpu.VMEM((ta, D), F32), pltpu.VMEM((ta, D), F32),
                        pltpu.VMEM((8, D), F32)],
        params=_cparams(("arbitrary",), 40), args=(z, z, z, conv_w, conv_b, w_r, b_r, w_i, b_i, lam), comm=comm)


def _sgu_common(ub, vb, lg, lb, with_grad):
    if with_grad:
        u, du = _gelu_and_grad(ub)
        v, dv = _gelu_and_grad(vb)
    else:
        u, v, du, dv = _gelu(ub), _gelu(vb), None, None
    mu = jnp.mean(v, axis=-1, keepdims=True)
    vc = v - mu
    rstd = lax.rsqrt(jnp.mean(vc * vc, axis=-1, keepdims=True) + LN_EPS)
    vhat = vc * rstd
    vln = vhat * lg + lb
    return u, du, dv, rstd, vhat, vln


def _masked_ws(ws_ref):
    t = lax.broadcasted_iota(jnp.int32, (CHUNK, CHUNK), 0)
    c = lax.broadcasted_iota(jnp.int32, (CHUNK, CHUNK), 1)
    keep = c <= t
    return [jnp.where(keep, ws_ref[g], 0.0).astype(BF16) for g in range(GROUPS)]


def _branch_b_fwd(z, ln_g, ln_b, w_s, b_s_t, comm=None):
    s = z.shape[0]
    tb = min(T_BRANCH_B, s)

    def body(ub_ref, vb_ref, lg_ref, lb_ref, ws_ref, bs_ref, yb_ref):
        u, _, _, _, _, vln = _sgu_common(ub_ref[...].astype(F32), vb_ref[...].astype(F32),
                                         lg_ref[...], lb_ref[...], False)
        vlnb = vln.astype(BF16)
        wm = _masked_ws(ws_ref)
        bs = bs_ref[...]
        for c in range(tb // CHUNK):
            rs = slice(c * CHUNK, (c + 1) * CHUNK)
            for g in range(GROUPS):
                cs = slice(g * GROUP_DIM, (g + 1) * GROUP_DIM)
                sp = _dot(wm[g], vlnb[rs, cs]) + bs[:, g:g + 1]
                yb_ref[rs, cs] = (u[rs, cs] * sp).astype(BF16)

    vec = pl.BlockSpec((1, D), lambda i: (0, 0))
    return _call(
        body, name="branch_b_fwd", grid=(s // tb,),
        in_specs=[pl.BlockSpec((tb, D), lambda i: (i, 2)), pl.BlockSpec((tb, D), lambda i: (i, 3)), vec, vec,
                  pl.BlockSpec((GROUPS, CHUNK, CHUNK), lambda i: (0, 0, 0)),
                  pl.BlockSpec((CHUNK, GROUPS), lambda i: (0, 0))],
        out_specs=[pl.BlockSpec((tb, D), lambda i: (i, 0))],
        out_shape=[SDS((s, D), BF16)], scratch_shapes=[],
        params=_cparams(("arbitrary",), 40), args=(z, z, ln_g, ln_b, w_s, b_s_t), comm=comm)


def _merge_out(ya, yb, z, x, w_oa, w_ob, w_out, comm=None):
    s = x.shape[0]
    tm = min(TM_MERGE, s)

    def body(ya_ref, yb_ref, ma_ref, mb_ref, x_ref, woa_ref, wob_ref, wo_ref, pa_ref, pb_ref, mg_ref, h1_ref):
        pa = _dot(ya_ref[...], woa_ref[...])
        pb = _dot(yb_ref[...], wob_ref[...])
        merged = (jax.nn.sigmoid(ma_ref[...].astype(F32)) * pa
                  + jax.nn.sigmoid(mb_ref[...].astype(F32)) * pb).astype(BF16)
        pa_ref[...] = pa.astype(BF16)
        pb_ref[...] = pb.astype(BF16)
        mg_ref[...] = merged
        h1_ref[...] = x_ref[...] + _dot(merged, wo_ref[...])

    row = pl.BlockSpec((tm, D), lambda i: (i, 0))
    wsp = pl.BlockSpec((D, D), lambda i: (0, 0))
    return _call(
        body, name="merge_out", grid=(s // tm,),
        in_specs=[row, row, pl.BlockSpec((tm, D), lambda i: (i, 4)), pl.BlockSpec((tm, D), lambda i: (i, 5)),
                  row, wsp, wsp, wsp],
        out_specs=[row, row, row, row],
        out_shape=[SDS((s, D), BF16), SDS((s, D), BF16), SDS((s, D), BF16), SDS((s, D), F32)], scratch_shapes=[],
        params=_cparams(("arbitrary",), 48), args=(ya, yb, z, z, x, w_oa, w_ob, w_out), comm=comm)


def _mlp_fwd(h1, g_mlp, w_up_g, w_down, g_fin, tgt):
    s = h1.shape[0]
    tm = min(TM_ROWS, s)
    nj = N_SLOT

    def body(h1_ref, gm_ref, wu_ref, wd_ref, gf_ref, t_ref, r_ref, at_ref, n2t_ref, dh2_ref, loss_ref, dgf_ref,
             n2_s, acc_s):
        i, j = pl.program_id(0), pl.program_id(1)

        @pl.when(j == 0)
        def _():
            hv = h1_ref[...]
            rstd = lax.rsqrt(jnp.mean(hv * hv, axis=-1, keepdims=True) + NORM_EPS)
            nb = (hv * rstd * gm_ref[...]).astype(BF16)
            n2_s[...] = nb
            n2t_ref[...] = nb.T
            acc_s[...] = jnp.zeros_like(acc_s)

        @pl.when((i == 0) & (j == 0))
        def _():
            loss_ref[...] = jnp.zeros_like(loss_ref)
            dgf_ref[...] = jnp.zeros_like(dgf_ref)

        r = jnp.maximum(_dot(n2_s[...], wu_ref[...]), 0.0)
        r_ref[...] = r.astype(BF16)
        act = (r * r).astype(BF16)
        at_ref[...] = act.T
        acc_s[...] += _dot(act, wd_ref[...])

        @pl.when(j == nj - 1)
        def _():
            h2 = h1_ref[...] + acc_s[...]
            rstd = lax.rsqrt(jnp.mean(h2 * h2, axis=-1, keepdims=True) + NORM_EPS)
            hh = h2 * rstd
            gf = gf_ref[...]
            e = hh * gf - t_ref[...]
            loss_ref[...] += jnp.sum(e * e) * (0.5 / D)
            dy = e * (1.0 / D)
            dgf_ref[...] += jnp.sum(dy * hh, axis=0, keepdims=True)
            dhh = dy * gf
            dh2_ref[...] = rstd * (dhh - hh * jnp.mean(dhh * hh, axis=-1, keepdims=True))

    row = pl.BlockSpec((tm, D), lambda i, j: (i, 0))
    vec = pl.BlockSpec((1, D), lambda i, j: (0, 0))
    return pl.pallas_call(
        body, name="mlp_fwd", grid=(s // tm, nj),
        in_specs=[row, vec, pl.BlockSpec((None, D, FF_COLS), lambda i, j: (j, 0, 0)),
                  pl.BlockSpec((FF_COLS, D), lambda i, j: (j, 0)), vec, row],
        out_specs=[pl.BlockSpec((tm, FF_COLS), lambda i, j: (i, j)), pl.BlockSpec((FF_COLS, tm), lambda i, j: (j, i)),
                   pl.BlockSpec((D, tm), lambda i, j: (0, i)), row, pl.BlockSpec((8, 128), lambda i, j: (0, 0)), vec],
        out_shape=[SDS((s, nj * FF_COLS), BF16), SDS((nj * FF_COLS, s), BF16), SDS((D, s), BF16), SDS((s, D), F32),
                   SDS((8, 128), F32), SDS((1, D), F32)],
        scratch_shapes=[pltpu.VMEM((tm, D), BF16), pltpu.VMEM((tm, D), F32)],
        compiler_params=_cparams(("arbitrary", "arbitrary"), 52),
    )(h1, g_mlp, w_up_g, w_down, g_fin, tgt)


def _mlp_bwd(dh2, r, w_down, w_up_g, h1, g_mlp, comm=None):
    s = h1.shape[0]
    tm = min(TM_ROWS, s)
    nj = N_SLOT

    def body(dh2_ref, r_ref, wd_ref, wu_ref, h1_ref, gm_ref, df_ref, dh1_ref, dgm_ref, dh2b_s, acc_s):
        i, j = pl.program_id(0), pl.program_id(1)

        @pl.when(j == 0)
        def _():
            dh2b_s[...] = dh2_ref[...].astype(BF16)
            acc_s[...] = jnp.zeros_like(acc_s)

        @pl.when((i == 0) & (j == 0))
        def _():
            dgm_ref[...] = jnp.zeros_like(dgm_ref)

        d_act = _dot_nt(dh2b_s[...], wd_ref[...])
        df = (d_act * (2.0 * r_ref[...].astype(F32))).astype(BF16)
        df_ref[...] = df
        acc_s[...] += _dot_nt(df, wu_ref[...])

        @pl.when(j == nj - 1)
        def _():
            hv = h1_ref[...]
            rstd = lax.rsqrt(jnp.mean(hv * hv, axis=-1, keepdims=True) + NORM_EPS)
            hh = hv * rstd
            dn2 = acc_s[...]
            dgm_ref[...] += jnp.sum(dn2 * hh, axis=0, keepdims=True)
            dhat = dn2 * gm_ref[...]
            dh1_ref[...] = dh2_ref[...] + rstd * (dhat - hh * jnp.mean(dhat * hh, axis=-1, keepdims=True))

    row = pl.BlockSpec((tm, D), lambda i, j: (i, 0))
    vec = pl.BlockSpec((1, D), lambda i, j: (0, 0))
    ffb = pl.BlockSpec((tm, FF_COLS), lambda i, j: (i, j))
    return _call(
        body, name="mlp_bwd", grid=(s // tm, nj),
        in_specs=[row, ffb, pl.BlockSpec((FF_COLS, D), lambda i, j: (j, 0)),
                  pl.BlockSpec((None, D, FF_COLS), lambda i, j: (j, 0, 0)), row, vec],
        out_specs=[ffb, row, vec],
        out_shape=[SDS((s, nj * FF_COLS), BF16), SDS((s, D), F32), SDS((1, D), F32)],
        scratch_shapes=[pltpu.VMEM((tm, D), BF16), pltpu.VMEM((tm, D), F32)],
        params=_cparams(("arbitrary", "arbitrary"), 52), args=(dh2, r, w_down, w_up_g, h1, g_mlp), comm=comm)


def _merge_bwd(dh1, z, pa, pb, w_out, w_oa, w_ob, comm=None):
    s = dh1.shape[0]
    tm = min(TM_MERGE, s)

    def body(dh1_ref, ma_ref, mb_ref, pa_ref, pb_ref, wo_ref, woa_ref, wob_ref,
             dz_ref, dzh_ref, dpa_ref, dpb_ref, dya_ref, dyb_ref):
        dm = _dot_nt(dh1_ref[...].astype(BF16), wo_ref[...])
        sa = jax.nn.sigmoid(ma_ref[...].astype(F32))
        sb = jax.nn.sigmoid(mb_ref[...].astype(F32))
        dpa = (dm * sa).astype(BF16)
        dpb = (dm * sb).astype(BF16)
        dz_ref[:, 0:D] = (dm * pa_ref[...].astype(F32) * sa * (1.0 - sa)).astype(BF16)
        dz_ref[:, D:2 * D] = (dm * pb_ref[...].astype(F32) * sb * (1.0 - sb)).astype(BF16)
        dzh_ref[...] = dz_ref[...]
        dpa_ref[...] = dpa
        dpb_ref[...] = dpb
        dya_ref[...] = _dot_nt(dpa, woa_ref[...]).astype(BF16)
        dyb_ref[...] = _dot_nt(dpb, wob_ref[...]).astype(BF16)

    row = pl.BlockSpec((tm, D), lambda i: (i, 0))
    wsp = pl.BlockSpec((D, D), lambda i: (0, 0))
    return _call(
        body, name="merge_bwd", grid=(s // tm,),
        in_specs=[row, pl.BlockSpec((tm, D), lambda i: (i, 4)), pl.BlockSpec((tm, D), lambda i: (i, 5)),
                  row, row, wsp, wsp, wsp],
        out_specs=[pl.BlockSpec((tm, 2 * D), lambda i: (i, 2)), pl.BlockSpec((tm, 2 * D), lambda i: (i, 1)),
                   row, row, row, row],
        out_shape=[SDS((s, 6 * D), BF16), SDS((s, 4 * D), BF16)] + [SDS((s, D), BF16)] * 4, scratch_shapes=[],
        params=_cparams(("arbitrary",), 48), args=(dh1, z, z, pa, pb, w_out, w_oa, w_ob), comm=comm)


def _branch_b_bwd(dz, dz_hi, dyb, z, ln_g, ln_b, w_s, b_s_t, comm=None):
    s = z.shape[0]
    tb = min(T_BRANCH_B, s)

    def body(dz_in, dzh_in, dyb_ref, ub_ref, vb_ref, lg_ref, lb_ref, ws_ref, bs_ref,
             dz_ref, dzh_ref, dws_ref, dbs_ref, dln_ref, du_s, dvln_s):
        del dz_in, dzh_in

        @pl.when(pl.program_id(0) == 0)
        def _():
            dws_ref[...] = jnp.zeros_like(dws_ref)
            dbs_ref[...] = jnp.zeros_like(dbs_ref)
            dln_ref[...] = jnp.zeros_like(dln_ref)

        lg = lg_ref[...]
        u, du, dv, rstd, vhat, vln = _sgu_common(ub_ref[...].astype(F32), vb_ref[...].astype(F32),
                                                 lg, lb_ref[...], True)
        vlnb = vln.astype(BF16)
        dyb_v = dyb_ref[...].astype(F32)
        wm = _masked_ws(ws_ref)
        keep = (lax.broadcasted_iota(jnp.int32, (CHUNK, CHUNK), 1)
                <= lax.broadcasted_iota(jnp.int32, (CHUNK, CHUNK), 0))
        bs = bs_ref[...]
        for c in range(tb // CHUNK):
            rs = slice(c * CHUNK, (c + 1) * CHUNK)
            for g in range(GROUPS):
                cs = slice(g * GROUP_DIM, (g + 1) * GROUP_DIM)
                v_blk = vlnb[rs, cs]
                sp = _dot(wm[g], v_blk) + bs[:, g:g + 1]
                d_sp = dyb_v[rs, cs] * u[rs, cs]
                d_spb = d_sp.astype(BF16)
                du_s[rs, cs] = dyb_v[rs, cs] * sp
                dvln_s[rs, cs] = _dot_tn(wm[g], d_spb)
                dws_ref[g] += jnp.where(keep, _dot_nt(d_spb, v_blk), 0.0)
                dbs_ref[g] += jnp.broadcast_to(jnp.sum(d_sp, axis=-1, keepdims=True), (CHUNK, CHUNK))
        dvln = dvln_s[...]
        dln_ref[0:1, :] += jnp.sum(dvln * vhat, axis=0, keepdims=True)
        dln_ref[1:2, :] += jnp.sum(dvln, axis=0, keepdims=True)
        dvh = dvln * lg
        d_v = rstd * (dvh - jnp.mean(dvh, axis=-1, keepdims=True)
                      - vhat * jnp.mean(dvh * vhat, axis=-1, keepdims=True))
        dz_ref[:, 0:D] = (du_s[...] * du).astype(BF16)
        dz_ref[:, D:2 * D] = (d_v * dv).astype(BF16)
        dzh_ref[...] = dz_ref[...]

    vec = pl.BlockSpec((1, D), lambda i: (0, 0))
    sq = pl.BlockSpec((GROUPS, CHUNK, CHUNK), lambda i: (0, 0, 0))
    return _call(
        body, name="branch_b_bwd", grid=(s // tb,),
        in_specs=[ANY, ANY, pl.BlockSpec((tb, D), lambda i: (i, 0)),
                  pl.BlockSpec((tb, D), lambda i: (i, 2)), pl.BlockSpec((tb, D), lambda i: (i, 3)), vec, vec, sq,
                  pl.BlockSpec((CHUNK, GROUPS), lambda i: (0, 0))],
        out_specs=[pl.BlockSpec((tb, 2 * D), lambda i: (i, 1)), pl.BlockSpec((tb, 2 * D), lambda i: (i, 0)),
                   sq, sq, pl.BlockSpec((8, D), lambda i: (0, 0))],
        out_shape=[SDS(dz.shape, BF16), SDS(dz_hi.shape, BF16), SDS((GROUPS, CHUNK, CHUNK), F32),
                   SDS((GROUPS, CHUNK, CHUNK), F32), SDS((8, D), F32)],
        scratch_shapes=[pltpu.VMEM((tb, D), F32), pltpu.VMEM((tb, D), F32)], aliases={0: 0, 1: 1},
        params=_cparams(("arbitrary",), 40), args=(dz, dz_hi, dyb, z, z, ln_g, ln_b, w_s, b_s_t), comm=comm)


def _branch_a_bwd(dz, dz_hi, n1_t, dya, z, hs, conv_w, conv_b, w_r, b_r, w_i, b_i, lam, comm=None):
    s = z.shape[0]
    ta = min(T_BRANCH_A, s)
    nb = s // ta
    per16 = ta // 16

    assert dz_hi.shape[1] // WG_COLS == nb, "one column block of the hosted w_in gradient per time block"
    lo_blocks = 2 * D // WG_COLS

    def body(dz_in, dzh_ref, n1t_ref, dya_ref, xa_ref, xp_ref, ga_ref, hs_ref, hp_ref, cw_ref, cb_ref, wr_ref,
             br_ref, wi_ref, bi_ref, lam_ref, dz_ref, vec_ref, dwr_ref, dwi_ref, gin_ref,
             a_s, b_s, h_s, r_s, i_s, m_s, dcar_s, acar_s, dxc_s):
        del dz_in
        i = pl.program_id(0)
        gin_ref[...] = _dot(n1t_ref[...], dzh_ref[...]).astype(BF16)
        blk = nb - 1 - i

        @pl.when(i == 0)
        def _():
            dcar_s[...] = jnp.zeros_like(dcar_s)
            acar_s[...] = jnp.zeros_like(acar_s)
            dxc_s[...] = jnp.zeros_like(dxc_s)
            vec_ref[...] = jnp.zeros_like(vec_ref)
            dwr_ref[...] = jnp.zeros_like(dwr_ref)
            dwi_ref[...] = jnp.zeros_like(dwi_ref)

        cw = cw_ref[...]
        lam_v = lam_ref[...]
        xa = xa_ref[...].astype(F32)
        prev8 = jnp.where(blk > 0, xp_ref[...].astype(F32)[8:16], 0.0)
        xc = _conv_fwd(xa, prev8, cw, cb_ref[...])
        xcb = xc.astype(BF16)
        sp_lam = _softplus(-lam_v)
        _lru_gates(xc, xcb, wr_ref, br_ref[...], wi_ref, bi_ref[...], sp_lam, a_s, b_s, r_s, i_s, m_s)

        hs_v = hs_ref[...].astype(F32)
        hprev8 = jnp.where(blk > 0, hp_ref[...].astype(F32)[8:16], 0.0)
        h_m1 = _rows_shifted(hprev8, hs_v, 1)
        gg, dgg = _gelu_and_grad(ga_ref[...].astype(F32))
        dya_v = dya_ref[...].astype(F32)
        dz_ref[:, D:2 * D] = (dya_v * hs_v * dgg).astype(BF16)

        a_v = a_s[...]
        a_s[...] = _rows_advanced(a_v, acar_s[...], 1)
        b_s[...] = dya_v * gg

        row = lax.broadcasted_iota(jnp.int32, (8, D), 0)
        ng = ta // 8

        def group(gi, carry):
            off = pl.multiple_of((ng - 1 - gi) * 8, 8)
            c8 = a_s[pl.ds(off, 8), :]
            d8 = b_s[pl.ds(off, 8), :]
            for d in (1, 2, 4):
                c_sh = jnp.where(row < 8 - d, pltpu.roll(c8, 8 - d, 0), 1.0)
                d_sh = jnp.where(row < 8 - d, pltpu.roll(d8, 8 - d, 0), 0.0)
                d8 = c8 * d_sh + d8
                c8 = c8 * c_sh
            dh8 = d8 + c8 * carry
            h_s[pl.ds(off, 8), :] = dh8
            return jnp.broadcast_to(dh8[0:1, :], (8, D))

        dcar_s[...] = lax.fori_loop(0, ng, group, dcar_s[...])
        acar_s[...] = jnp.broadcast_to(a_v[0:1, :], (8, D))

        dbx = h_s[...]
        r_v, i_v, m_v = r_s[...], i_s[...], m_s[...]
        d_mult = dbx * xc * i_v
        d_loga = dbx * h_m1 * a_v - d_mult * (a_v * a_v) / m_v
        d_pr = d_loga * ((-LRU_C) * sp_lam) * r_v * (1.0 - r_v)
        d_pi = dbx * xc * m_v * i_v * (1.0 - i_v)
        vec_ref[7:8, :] += jnp.sum(d_loga * r_v, axis=0, keepdims=True) * (LRU_C * jax.nn.sigmoid(-lam_v))
        vec_ref[5:6, :] += jnp.sum(d_pr, axis=0, keepdims=True)
        vec_ref[6:7, :] += jnp.sum(d_pi, axis=0, keepdims=True)
        d_prb = d_pr.astype(BF16)
        d_pib = d_pi.astype(BF16)
        h_s[...] = dbx * i_v * m_v
        for h in range(HEADS):
            sl = slice(h * HEAD_DIM, (h + 1) * HEAD_DIM)
            h_s[:, sl] += _dot_nt(d_prb[:, sl], wr_ref[h]) + _dot_nt(d_pib[:, sl], wi_ref[h])
            dwr_ref[h] += _dot_tn(xcb[:, sl], d_prb[:, sl])
            dwi_ref[h] += _dot_tn(xcb[:, sl], d_pib[:, sl])
        d_xc = h_s[...]
        vec_ref[4:5, :] += jnp.sum(d_xc, axis=0, keepdims=True)
        vec_ref[0:1, :] += jnp.sum(d_xc * xa, axis=0, keepdims=True)
        d_xa = cw[0:1, :] * d_xc
        nxt = dxc_s[...]
        for k in range(1, CONV_K):
            vec_ref[k:k + 1, :] += jnp.sum(d_xc * _rows_shifted(prev8, xa, k), axis=0, keepdims=True)
            d_xa = d_xa + cw[k:k + 1, :] * _rows_advanced(d_xc, nxt, k)
        dz_ref[:, 0:D] = d_xa.astype(BF16)
        dxc_s[...] = d_xc[0:8, :]

    vec = pl.BlockSpec((1, D), lambda i: (0, 0))
    gate = pl.BlockSpec((HEADS, HEAD_DIM, HEAD_DIM), lambda i: (0, 0, 0))
    cur = lambda c: pl.BlockSpec((ta, D), lambda i: (nb - 1 - i, c))
    before = lambda c: pl.BlockSpec((16, D), lambda i: (jnp.maximum((nb - 1 - i) * per16 - 1, 0), c))
    return _call(
        body, name="branch_a_bwd", grid=(nb,),
        in_specs=[ANY, pl.BlockSpec((s, WG_COLS), lambda i: (0, i)), pl.BlockSpec((D, s), lambda i: (0, 0)),
                  cur(0), cur(0), before(0), cur(1), cur(0), before(0),
                  pl.BlockSpec((CONV_K, D), lambda i: (0, 0)), vec, gate, vec, gate, vec, vec],
        out_specs=[pl.BlockSpec((ta, 2 * D), lambda i: (nb - 1 - i, 0)), pl.BlockSpec((8, D), lambda i: (0, 0)),
                   gate, gate,
                   pl.BlockSpec((None, D, WG_COLS), lambda i: ((lo_blocks + i) // 3, 0, (lo_blocks + i) % 3))],
        out_shape=[SDS(dz.shape, BF16), SDS((8, D), F32), SDS((HEADS, HEAD_DIM, HEAD_DIM), F32),
                   SDS((HEADS, HEAD_DIM, HEAD_DIM), F32), SDS((N_SLOT, D, W_IN_COLS), BF16)],
        scratch_shapes=[pltpu.VMEM((ta, D), F32)] * 6 + [pltpu.VMEM((8, D), F32)] * 3, aliases={0: 0},
        params=_cparams(("arbitrary",), 56),
        args=(dz, dz_hi, n1_t, dya, z, z, z, hs, hs, conv_w, conv_b, w_r, b_r, w_i, b_i, lam), comm=comm)


def _in_bwd(dz, w_in_g, x, dh1, g_mix, comm=None):
    s = x.shape[0]
    tm = min(TM_ROWS, s)
    nj = N_SLOT

    def body(dz_ref, w_ref, x_ref, dh1_ref, g_ref, dx_ref, dg_ref, acc_s):
        i, j = pl.program_id(0), pl.program_id(1)

        @pl.when(j == 0)
        def _():
            acc_s[...] = jnp.zeros_like(acc_s)

        @pl.when((i == 0) & (j == 0))
        def _():
            dg_ref[...] = jnp.zeros_like(dg_ref)

        acc_s[...] += _dot_nt(dz_ref[...], w_ref[...])

        @pl.when(j == nj - 1)
        def _():
            xv = x_ref[...]
            rstd = lax.rsqrt(jnp.mean(xv * xv, axis=-1, keepdims=True) + NORM_EPS)
            xh = xv * rstd
            dn = acc_s[...]
            dg_ref[...] += jnp.sum(dn * xh, axis=0, keepdims=True)
            dhat = dn * g_ref[...]
            dx_ref[...] = dh1_ref[...] + rstd * (dhat - xh * jnp.mean(dhat * xh, axis=-1, keepdims=True))

    row = pl.BlockSpec((tm, D), lambda i, j: (i, 0))
    vec = pl.BlockSpec((1, D), lambda i, j: (0, 0))
    return _call(
        body, name="in_bwd", grid=(s // tm, nj),
        in_specs=[pl.BlockSpec((tm, W_IN_COLS), lambda i, j: (i, j)),
                  pl.BlockSpec((None, D, W_IN_COLS), lambda i, j: (j, 0, 0)), row, row, vec],
        out_specs=[row, vec],
        out_shape=[SDS((s, D), F32), SDS((1, D), F32)],
        scratch_shapes=[pltpu.VMEM((tm, D), F32)],
        params=_cparams(("arbitrary", "arbitrary"), 48), args=(dz, w_in_g, x, dh1, g_mix), comm=comm)


def _wgrad(name, a, b, nblk, a_split, b_split):
    s = a.shape[0]
    ts = min(TM_ROWS, s)
    a_w = a.shape[1] // nblk if a_split else a.shape[1]
    b_w = b.shape[1] // nblk if b_split else b.shape[1]

    def body(a_ref, b_ref, o_ref, acc_s):
        t = pl.program_id(1)

        @pl.when(t == 0)
        def _():
            acc_s[...] = jnp.zeros_like(acc_s)

        acc_s[...] += _dot_tn(a_ref[...].astype(BF16), b_ref[...].astype(BF16))

        @pl.when(t == pl.num_programs(1) - 1)
        def _():
            o_ref[...] = acc_s[...].astype(BF16)

    return pl.pallas_call(
        body, name=name, grid=(nblk, s // ts),
        in_specs=[pl.BlockSpec((ts, a_w), (lambda k, t: (t, k)) if a_split else (lambda k, t: (t, 0))),
                  pl.BlockSpec((ts, b_w), (lambda k, t: (t, k)) if b_split else (lambda k, t: (t, 0)))],
        out_specs=pl.BlockSpec((None, a_w, b_w), lambda k, t: (k, 0, 0)),
        out_shape=SDS((nblk, a_w, b_w), BF16),
        scratch_shapes=[pltpu.VMEM((a_w, b_w), F32)],
        compiler_params=_cparams(("arbitrary", "arbitrary"), 48),
    )(a, b)


def _wgrad_in_rest(g_in, n1_t, dz):
    s = dz.shape[0]
    blocks = 2 * D // WG_COLS

    def body(g_any, a_ref, b_ref, o_ref):
        del g_any
        o_ref[...] = _dot(a_ref[...], b_ref[...]).astype(BF16)

    return pl.pallas_call(
        body, name="wgrad_in_rest", grid=(blocks,),
        in_specs=[ANY, pl.BlockSpec((D, s), lambda q: (0, 0)), pl.BlockSpec((s, WG_COLS), lambda q: (0, q))],
        out_specs=pl.BlockSpec((None, D, WG_COLS), lambda q: (q // 3, 0, q % 3)),
        out_shape=SDS(g_in.shape, BF16), input_output_aliases={0: 0},
        compiler_params=_cparams(("arbitrary",), 40),
    )(g_in, n1_t, dz)


def _wgrad_t(name, a_t, b, nblk, a_split, b_split, tokens):
    s = b.shape[0]
    ts = min(tokens, s)
    a_w = a_t.shape[0] // nblk if a_split else a_t.shape[0]
    b_w = b.shape[1] // nblk if b_split else b.shape[1]

    def body(a_ref, b_ref, o_ref, acc_s):
        t = pl.program_id(1)

        @pl.when(t == 0)
        def _():
            acc_s[...] = jnp.zeros_like(acc_s)

        acc_s[...] += _dot(a_ref[...], b_ref[...].astype(BF16))

        @pl.when(t == pl.num_programs(1) - 1)
        def _():
            o_ref[...] = acc_s[...].astype(BF16)

    return pl.pallas_call(
        body, name=name, grid=(nblk, s // ts),
        in_specs=[pl.BlockSpec((a_w, ts), (lambda k, t: (k, t)) if a_split else (lambda k, t: (0, t))),
                  pl.BlockSpec((ts, b_w), (lambda k, t: (t, k)) if b_split else (lambda k, t: (t, 0)))],
        out_specs=pl.BlockSpec((None, a_w, b_w), lambda k, t: (k, 0, 0)),
        out_shape=SDS((nblk, a_w, b_w), BF16),
        scratch_shapes=[pltpu.VMEM((a_w, b_w), F32)],
        compiler_params=_cparams(("arbitrary", "arbitrary"), 48),
    )(a_t, b)


def _place():
    x, y, c = lax.axis_index("x"), lax.axis_index("y"), lax.axis_index("c")
    return x, y, c


def _other_chips(x, y):
    return [(x, 1 - y, 2 * x + 1 - y), (1 - x, y, 2 * (1 - x) + y), (1 - x, 1 - y, 2 * (1 - x) + 1 - y)]


class _Plan:
    def __init__(self, arrays, out_shape, sems, start, finish):
        self.arrays, self.out_shape, self.sems, self.start, self.finish = arrays, out_shape, sems, start, finish


def _gather_plan(shards):
    n = len(shards)

    def copies(ins, outs, sems):
        send_sems, recv_sems, local_sems = sems
        x, y, c = _place()
        chip = 2 * x + y
        me = 2 * chip + c
        sib = (x, y, 1 - c)
        chips = _other_chips(x, y)

        def rc(k, t, src, blk, to):
            return pltpu.make_async_remote_copy(
                src_ref=src, dst_ref=outs[t].at[blk], send_sem=send_sems.at[k * n + t],
                recv_sem=recv_sems.at[k * n + t], device_id=to, device_id_type=MESH)

        local = [pltpu.make_async_copy(ins[t], outs[t].at[me], local_sems.at[t]) for t in range(n)]
        sends = [rc(0, t, ins[t], me, sib) for t in range(n)]
        for j, (px, py, _) in enumerate(chips):
            sends += [rc(1 + j, t, ins[t], me, (px, py, c)) for t in range(n)]
        return rc, local, sends, chips, chip, c, sib

    def start(ins, outs, sems):
        _, local, sends, _, _, _, _ = copies(ins, outs, sems)
        for cp in local + sends:
            cp.start()

    def finish(ins, outs, sems):
        rc, local, sends, chips, chip, c, sib = copies(ins, outs, sems)
        passed = []
        for j, (px, py, pc) in enumerate(chips):
            blk = 2 * pc + c
            for t in range(n):
                rc(1 + j, t, ins[t], blk, sib).wait_recv()
            for t in range(n):
                cp = rc(4 + j, t, outs[t].at[blk], blk, sib)
                cp.start()
                passed.append(cp)
        for t in range(n):
            rc(0, t, ins[t], 2 * chip + 1 - c, sib).wait_recv()
        for j, (px, py, pc) in enumerate(chips):
            for t in range(n):
                rc(4 + j, t, ins[t], 2 * pc + 1 - c, sib).wait_recv()
        for cp in sends + passed:
            cp.wait_send()
        for cp in local:
            cp.wait()

    return _Plan(list(shards), [SDS((N_SLOT,) + tuple(a.shape), a.dtype) for a in shards],
                 [pltpu.SemaphoreType.DMA((7 * n,)), pltpu.SemaphoreType.DMA((7 * n,)),
                  pltpu.SemaphoreType.DMA((n,))], start, finish)


def _sibling_plan(grads, whole=()):
    n, m = len(grads), len(whole)

    def copies(ins, outs, sems):
        send_sems, recv_sems = sems
        x, y, c = _place()
        sib = (x, y, 1 - c)

        def rc(t, src, dst):
            return pltpu.make_async_remote_copy(src_ref=src, dst_ref=dst, send_sem=send_sems.at[t],
                                                recv_sem=recv_sems.at[t], device_id=sib, device_id_type=MESH)
        return rc, c

    def start(ins, outs, sems):
        rc, c = copies(ins, outs, sems)
        for t in range(n):
            for j in range(4):
                rc(t, ins[t].at[2 * j + 1 - c], outs[t].at[j]).start()
        for t in range(n, n + m):
            rc(t, ins[t], outs[t]).start()

    def finish(ins, outs, sems):
        rc, _ = copies(ins, outs, sems)
        for t in range(n):
            rc(t, ins[t].at[pl.ds(0, 4)], outs[t]).wait()
        for t in range(n, n + m):
            rc(t, ins[t], outs[t]).wait()

    return _Plan(list(grads) + list(whole),
                 [SDS((4,) + tuple(g.shape[1:]), g.dtype) for g in grads] + [SDS(a.shape, a.dtype) for a in whole],
                 [pltpu.SemaphoreType.DMA((n + m,)), pltpu.SemaphoreType.DMA((n + m,))], start, finish)


def _chips_plan(parts, whole=()):
    n, m = len(parts), len(whole)

    def src_of(ins, t, pc):
        return ins[t].at[pc] if t < n else ins[t]

    def local_copies(ins, outs, sems, chip):
        return [pltpu.make_async_copy(src_of(ins, t, chip), outs[t].at[chip], sems[2].at[t]) for t in range(n + m)]

    def start(ins, outs, sems):
        send_sems, recv_sems, _ = sems
        x, y, c = _place()
        chip = 2 * x + y
        for cp in local_copies(ins, outs, sems, chip):
            cp.start()
        for px, py, pc in _other_chips(x, y):
            for t in range(n + m):
                pltpu.make_async_remote_copy(src_ref=src_of(ins, t, pc), dst_ref=outs[t].at[chip],
                                             send_sem=send_sems.at[t], recv_sem=recv_sems.at[t],
                                             device_id=(px, py, c), device_id_type=MESH).start()

    def finish(ins, outs, sems):
        send_sems, recv_sems, _ = sems
        x, y, c = _place()
        for t in range(n + m):
            three = outs[t].at[pl.ds(0, 3)]
            pltpu.make_async_remote_copy(src_ref=three, dst_ref=three, send_sem=send_sems.at[t],
                                         recv_sem=recv_sems.at[t], device_id=(x, y, c), device_id_type=MESH).wait()
        for cp in local_copies(ins, outs, sems, 2 * x + y):
            cp.wait()

    return _Plan(list(parts) + list(whole),
                 [SDS(p.shape, p.dtype) for p in parts] + [SDS((4,) + tuple(a.shape), a.dtype) for a in whole],
                 [pltpu.SemaphoreType.DMA((n + m,)), pltpu.SemaphoreType.DMA((n + m,)),
                  pltpu.SemaphoreType.DMA((n + m,))], start, finish)


def _exchange_plan(arr):
    def peers(x, y, c):
        flip = lambda v, f: 1 - v if f else v
        return [(flip(x, fx), flip(y, fy), flip(c, fc))
                for fx in (0, 1) for fy in (0, 1) for fc in (0, 1) if fx or fy or fc]

    def start(ins, outs, sems):
        x, y, c = _place()
        me = 4 * x + 2 * y + c
        pltpu.make_async_copy(ins[0], outs[0].at[me], sems[2].at[0]).start()
        for to in peers(x, y, c):
            pltpu.make_async_remote_copy(src_ref=ins[0], dst_ref=outs[0].at[me], send_sem=sems[0].at[0],
                                         recv_sem=sems[1].at[0], device_id=to, device_id_type=MESH).start()

    def finish(ins, outs, sems):
        x, y, c = _place()
        seven = outs[0].at[pl.ds(0, 7)]
        pltpu.make_async_remote_copy(src_ref=seven, dst_ref=seven, send_sem=sems[0].at[0], recv_sem=sems[1].at[0],
                                     device_id=(x, y, c), device_id_type=MESH).wait()
        pltpu.make_async_copy(ins[0], outs[0].at[4 * x + 2 * y + c], sems[2].at[0]).wait()

    return _Plan([arr], [SDS((N_SLOT,) + tuple(arr.shape), arr.dtype)],
                 [pltpu.SemaphoreType.DMA((1,)), pltpu.SemaphoreType.DMA((1,)), pltpu.SemaphoreType.DMA((1,))],
                 start, finish)


def _join(*plans):
    def cut(seq, sizes):
        out, at = [], 0
        for k in sizes:
            out.append(seq[at:at + k])
            at += k
        return out

    n_arr = [len(p.arrays) for p in plans]
    n_sem = [len(p.sems) for p in plans]

    def start(ins, outs, sems):
        for p, i, o, s in zip(plans, cut(ins, n_arr), cut(outs, n_arr), cut(sems, n_sem)):
            p.start(i, o, s)

    def finish(ins, outs, sems):
        for p, i, o, s in zip(plans, cut(ins, n_arr), cut(outs, n_arr), cut(sems, n_sem)):
            p.finish(i, o, s)

    return _Plan([a for p in plans for a in p.arrays], [o for p in plans for o in p.out_shape],
                 [s for p in plans for s in p.sems], start, finish)


def _run_plan(name, plan):
    k = len(plan.arrays)

    def body(*refs):
        ins, outs, sems = refs[:k], refs[k:2 * k], refs[2 * k:]
        plan.start(ins, outs, sems)
        plan.finish(ins, outs, sems)

    return pl.pallas_call(
        body, name=name, in_specs=[ANY] * k, out_specs=[ANY] * k, out_shape=plan.out_shape,
        scratch_shapes=plan.sems, compiler_params=pltpu.CompilerParams(has_side_effects=True),
    )(*plan.arrays)


def _call(body, *, name, grid, in_specs, out_specs, out_shape, scratch_shapes, params, args, comm=None,
          aliases=None, prefetch=()):
    aliases = aliases or {}
    n_pre = len(prefetch)

    def launch(fn, ins_specs, outs_specs, outs_shape, scratch, operands):
        spec = pltpu.PrefetchScalarGridSpec(num_scalar_prefetch=n_pre, grid=grid, in_specs=ins_specs,
                                            out_specs=outs_specs, scratch_shapes=scratch)
        return pl.pallas_call(fn, name=name, grid_spec=spec, out_shape=outs_shape, compiler_params=params,
                              input_output_aliases=aliases)(*prefetch, *operands)

    if comm is None:
        return list(launch(body, in_specs, out_specs, out_shape, scratch_shapes, args)), []
    n_in, n_out, n_scr, k = len(in_specs), len(out_specs), len(scratch_shapes), len(comm.arrays)

    def wrapped(*refs):
        pre, refs = refs[:n_pre], refs[n_pre:]
        ins = refs[:n_in]
        c_in = refs[n_in:n_in + k]
        outs = refs[n_in + k:n_in + k + n_out]
        c_out = refs[n_in + k + n_out:n_in + 2 * k + n_out]
        scr = refs[n_in + 2 * k + n_out:n_in + 2 * k + n_out + n_scr]
        sems = refs[n_in + 2 * k + n_out + n_scr:]
        ids = [pl.program_id(d) for d in range(len(grid))]
        first = ids[0] == 0
        last = ids[0] == grid[0] - 1
        for d in range(1, len(grid)):
            first = first & (ids[d] == 0)
            last = last & (ids[d] == grid[d] - 1)

        @pl.when(first)
        def _():
            comm.start(c_in, c_out, sems)

        body(*pre, *ins, *outs, *scr)

        @pl.when(last)
        def _():
            comm.finish(c_in, c_out, sems)

    res = launch(wrapped, list(in_specs) + [ANY] * k, list(out_specs) + [ANY] * k,
                 list(out_shape) + list(comm.out_shape), list(scratch_shapes) + list(comm.sems),
                 tuple(args) + tuple(comm.arrays))
    return list(res[:n_out]), list(res[n_out:])


def _row_tile(rows):
    for t in (512, 256, 128, 64, 32, 16, 8):
        if rows % t == 0:
            return t
    return rows


def _pair_sum(name, g8, recv4, core):
    _, rows, cols = recv4.shape
    tr = _row_tile(rows)
    g42 = g8.reshape(4, 2, rows, cols)

    def body(c_ref, g_ref, r_ref, o_ref):
        del c_ref
        o_ref[...] = (g_ref[...].astype(F32) + r_ref[...].astype(F32)).astype(o_ref.dtype)

    return pl.pallas_call(
        body, name=name,
        grid_spec=pltpu.PrefetchScalarGridSpec(
            num_scalar_prefetch=1, grid=(4, rows // tr),
            in_specs=[pl.BlockSpec((None, None, tr, cols), lambda j, i, c_ref: (j, c_ref[0], i, 0)),
                      pl.BlockSpec((None, tr, cols), lambda j, i, c_ref: (j, i, 0))],
            out_specs=pl.BlockSpec((None, tr, cols), lambda j, i, c_ref: (j, i, 0))),
        out_shape=SDS(recv4.shape, g8.dtype),
        compiler_params=_cparams(("arbitrary", "arbitrary"), 32),
    )(core, g42, recv4)


def _add2(name, a, b):
    rows, cols = a.shape
    tr = _row_tile(rows)

    def body(a_ref, b_ref, o_ref):
        o_ref[...] = a_ref[...] + b_ref[...]

    blk = pl.BlockSpec((tr, cols), lambda i: (i, 0))
    return pl.pallas_call(body, name=name, grid=(rows // tr,), in_specs=[blk, blk], out_specs=blk,
                          out_shape=SDS(a.shape, a.dtype),
                          compiler_params=_cparams(("arbitrary",), 32))(a, b)


def _sum_terms(name, terms):
    k, rows, cols = terms.shape
    tr = _row_tile(rows)

    def body(r_ref, o_ref):
        acc = r_ref[0]
        for q in range(1, k):
            acc = acc + r_ref[q]
        o_ref[...] = acc

    return pl.pallas_call(body, name=name, grid=(rows // tr,),
                          in_specs=[pl.BlockSpec((k, tr, cols), lambda i: (0, i, 0))],
                          out_specs=pl.BlockSpec((tr, cols), lambda i: (i, 0)),
                          out_shape=SDS((rows, cols), terms.dtype),
                          compiler_params=_cparams(("arbitrary",), 32))(terms)


def _adam_update(g, w, m, v):
    c1 = 1.0 / (1.0 - ADAM_B1 ** ADAM_STEP)
    c2 = 1.0 / (1.0 - ADAM_B2 ** ADAM_STEP)
    mn = ADAM_B1 * m + (1.0 - ADAM_B1) * g
    vn = ADAM_B2 * v + (1.0 - ADAM_B2) * (g * g)
    delta = (-ADAM_LR) * ((mn * c1) / (jnp.sqrt(vn * c2) + ADAM_EPS) + ADAM_WD * w)
    return delta, mn, vn


def _adamw_many(name, gs, ws, ms, vs):
    n = len(gs)

    def body(*refs):
        for p in range(n):
            g, w, m, v = (refs[q * n + p][...] for q in range(4))
            d, mn, vn = _adam_update(g, w, m, v)
            refs[4 * n + p][...] = d
            refs[5 * n + p][...] = mn
            refs[6 * n + p][...] = vn

    full = [pl.BlockSpec(memory_space=pltpu.VMEM)] * n
    shapes = [SDS(w.shape, F32) for w in ws]
    res = pl.pallas_call(body, name=name, in_specs=full * 4, out_specs=full * 3, out_shape=shapes * 3,
                         compiler_params=pltpu.CompilerParams(vmem_limit_bytes=32 * MiB))(*gs, *ws, *ms, *vs)
    return [(res[p], res[n + p], res[2 * n + p]) for p in range(n)]


def _adamw(name, terms, w, m, v):
    k, rows, cols = terms.shape
    tr = _row_tile(rows)

    def body(t_ref, w_ref, m_ref, v_ref, g_ref, d_ref, mo_ref, vo_ref):
        g = t_ref[0].astype(F32)
        for q in range(1, k):
            g = g + t_ref[q].astype(F32)
        g_ref[...] = g
        d_ref[...], mo_ref[...], vo_ref[...] = _adam_update(g, w_ref[...], m_ref[...], v_ref[...])

    blk = pl.BlockSpec((tr, cols), lambda i: (i, 0))
    return pl.pallas_call(body, name=name, grid=(rows // tr,),
                          in_specs=[pl.BlockSpec((k, tr, cols), lambda i: (0, i, 0)), blk, blk, blk],
                          out_specs=[blk] * 4, out_shape=[SDS((rows, cols), F32)] * 4,
                          compiler_params=_cparams(("arbitrary",), 40))(terms, w, m, v)


def kernel(x, norm_mix_g, w_in, conv_w, conv_b, w_rgate, b_rgate, w_igate, b_igate, lru_lambda, w_out_a, sgu_ln_g, sgu_ln_b, sgu_w_s, sgu_b_s, w_out_b, w_out, norm_mlp_g, w_up, w_down, norm_final_g, loss_target, m_norm_mix_g, m_w_in, m_conv_w, m_conv_b, m_w_rgate, m_b_rgate, m_w_igate, m_b_igate, m_lru_lambda, m_w_out_a, m_sgu_ln_g, m_sgu_ln_b, m_sgu_w_s, m_sgu_b_s, m_w_out_b, m_w_out, m_norm_mlp_g, m_w_up, m_w_down, m_norm_final_g, v_norm_mix_g, v_w_in, v_conv_w, v_conv_b, v_w_rgate, v_b_rgate, v_w_igate, v_b_igate, v_lru_lambda, v_w_out_a, v_sgu_ln_g, v_sgu_ln_b, v_sgu_w_s, v_sgu_b_s, v_w_out_b, v_w_out, v_norm_mlp_g, v_w_up, v_w_down, v_norm_final_g):
    cx, cy, cc = _place()
    me = 4 * cx + 2 * cy + cc
    core = jnp.reshape(cc, (1,)).astype(jnp.int32)
    xs = x[0]
    tgt = loss_target[0]
    s = xs.shape[0]

    gate_shard = jnp.stack([w_rgate[0], w_igate[0]]).astype(BF16).reshape(2 * HEADS * 32, HEAD_DIM)
    vec_shard = jnp.concatenate([conv_w[0], b_rgate[0], b_igate[0]], axis=1)
    vec_shard = jnp.pad(vec_shard, ((0, 4), (0, 256 - vec_shard.shape[1])))
    shards = [w_in[0].astype(BF16), w_out_a[0].astype(BF16), w_out_b[0].astype(BF16), w_out[0].astype(BF16),
              w_up[0].astype(BF16), w_down[0].astype(BF16), gate_shard, vec_shard]
    (z, n1_t, w_in_g), (gate_g, vec_g) = _in_proj(xs, norm_mix_g, shards[0], _slot_order(cx, cy, cc),
                                                comm=_gather_plan(shards[6:8]))
    gates = gate_g.reshape(N_SLOT, 2, HEADS, 32, HEAD_DIM).transpose(1, 2, 0, 3, 4).reshape(2, HEADS, HEAD_DIM, HEAD_DIM)
    w_r_f, w_i_f = gates[0], gates[1]
    conv_w_f = vec_g[:, 0:4, 0:128].transpose(1, 0, 2).reshape(CONV_K, D)
    b_r_f = vec_g[:, 0:4, 128:160].transpose(1, 0, 2).reshape(1, D)
    b_i_f = vec_g[:, 0:4, 160:192].transpose(1, 0, 2).reshape(1, D)
    b_s_t = jnp.transpose(sgu_b_s[0])

    (ya, hs), (w_oa_g, w_ob_g, w_out_g) = _branch_a_fwd(
        z, conv_w_f, conv_b, w_r_f, b_r_f, w_i_f, b_i_f, lru_lambda, comm=_gather_plan(shards[1:4]))
    w_oa_f = w_oa_g.reshape(D, D)
    w_ob_f = w_ob_g.reshape(D, D)
    w_out_f = w_out_g.reshape(D, D)
    (yb,), (w_up_g,) = _branch_b_fwd(z, sgu_ln_g, sgu_ln_b, sgu_w_s[0], b_s_t, comm=_gather_plan(shards[4:5]))
    (pa, pb, merged, h1), (w_down_g,) = _merge_out(ya, yb, z, xs, w_oa_f, w_ob_f, w_out_f,
                                                   comm=_gather_plan(shards[5:6]))
    w_down_f = w_down_g.reshape(N_SLOT * FF_COLS, D)
    gf2 = norm_final_g.reshape(1, D)
    r_act, act_t, n2_t, dh2, loss_acc, d_gfin = _mlp_fwd(h1, norm_mlp_g, w_up_g, w_down_f, gf2, tgt)

    def pair(names, grads, recv):
        return [_pair_sum("pair_sum_" + nm, g, r, core) for nm, g, r in zip(names, grads, recv)]

    g_down = _wgrad_t("wgrad_down", act_t, dh2, N_SLOT // 2, True, False, 2048)
    g_down = g_down.reshape(N_SLOT, FF_COLS, D)
    (df, dh1, d_gmlp), (r_down,) = _mlp_bwd(dh2, r_act, w_down_f, w_up_g, h1, norm_mlp_g,
                                            comm=_sibling_plan([g_down]))
    (p_down,) = pair(["down"], [g_down], [r_down])
    g_up = _wgrad_t("wgrad_up", n2_t, df, N_SLOT, False, True, 4096)
    g_out = _wgrad("wgrad_out", merged, dh1, 1, False, False).reshape(N_SLOT, D // N_SLOT, D)
    (dz, dz_hi, dpa, dpb, dya, dyb), (got_down, r_up, r_out) = _merge_bwd(
        dh1, z, pa, pb, w_out_f, w_oa_f, w_ob_f, comm=_join(_chips_plan([p_down]), _sibling_plan([g_up, g_out])))
    p_up, p_out = pair(["up", "out"], [g_up, g_out], [r_up, r_out])
    g_oa = _wgrad("wgrad_out_a", ya, dpa, 1, False, False).reshape(N_SLOT, D // N_SLOT, D)
    g_ob = _wgrad("wgrad_out_b", yb, dpb, 1, False, False).reshape(N_SLOT, D // N_SLOT, D)
    (dz, dz_hi, d_ws, d_bs, d_ln), (got_up, r_oa, r_ob) = _branch_b_bwd(
        dz, dz_hi, dyb, z, sgu_ln_g, sgu_ln_b, sgu_w_s[0], b_s_t,
        comm=_join(_chips_plan([p_up]), _sibling_plan([g_oa, g_ob])))
    p_oa, p_ob = pair(["out_a", "out_b"], [g_oa, g_ob], [r_oa, r_ob])
    (dz, d_vec, d_wr, d_wi, g_in), (got_out, got_oa, got_ob) = _branch_a_bwd(
        dz, dz_hi, n1_t, dya, z, hs, conv_w_f, conv_b, w_r_f, b_r_f, w_i_f, b_i_f, lru_lambda,
        comm=_chips_plan([p_out, p_oa, p_ob]))
    g_in = _wgrad_in_rest(g_in, n1_t, dz)
    g_gate = jnp.stack([d_wr, d_wi]).reshape(2, HEADS, N_SLOT, 32, HEAD_DIM).transpose(2, 0, 1, 3, 4)
    g_gate = g_gate.reshape(N_SLOT, 2 * HEADS * 32, HEAD_DIM).astype(BF16)

    d_bs_row = jnp.pad(d_bs[:, :, 0].reshape(1, GROUPS * CHUNK), ((0, 0), (0, D - GROUPS * CHUNK)))
    vecs = jnp.concatenate([d_vec, jnp.concatenate([d_ln[0:2], d_gmlp, d_gfin, d_bs_row, jnp.zeros((3, D), F32)])])
    d_ws2 = d_ws.reshape(GROUPS * CHUNK, CHUNK)
    r_in, r_gate, r_vecs, r_ws = _run_plan("rs_sibling_in", _sibling_plan([g_in, g_gate], [vecs, d_ws2]))
    p_in, p_gate = pair(["in", "gate"], [g_in, g_gate], [r_in, r_gate])
    vecs_chip = _add2("pair_sum_vecs", vecs, r_vecs)
    ws_chip = _add2("pair_sum_ws", d_ws2, r_ws)
    (dx, d_gmix), (got_in, got_gate, got_vecs, got_ws) = _in_bwd(
        dz, w_in_g, xs, dh1, norm_mix_g, comm=_chips_plan([p_in, p_gate], [vecs_chip, ws_chip]))
    vecs_sum = _sum_terms("sum_vecs", got_vecs)
    last = jnp.concatenate([d_gmix, jnp.pad(loss_acc[0:1], ((0, 0), (0, D - 128))), jnp.zeros((6, D), F32)])
    (last_all,) = _run_plan("exchange_last", _exchange_plan(last))
    last_sum = _sum_terms("sum_last", last_all)
    loss = last_sum[1, 0]
    got = [got_in, got_oa, got_ob, got_out, got_up, got_down, got_gate]

    def step(nm, terms, w, m, v, rows, cols):
        g, d, mn, vn = _adamw("adamw_" + nm, terms.reshape(4, rows, cols), w.reshape(rows, cols),
                              m.reshape(rows, cols), v.reshape(rows, cols))
        return [a.reshape(w.shape) for a in (g, d, mn, vn)]

    o_in = step("in", got[0], w_in, m_w_in, v_w_in, D, W_IN_COLS)
    o_oa = step("out_a", got[1], w_out_a, m_w_out_a, v_w_out_a, D // N_SLOT, D)
    o_ob = step("out_b", got[2], w_out_b, m_w_out_b, v_w_out_b, D // N_SLOT, D)
    o_out = step("out", got[3], w_out, m_w_out, v_w_out, D // N_SLOT, D)
    o_up = step("up", got[4], w_up, m_w_up, v_w_up, D, FF_COLS)
    o_down = step("down", got[5], w_down, m_w_down, v_w_down, FF_COLS, D)
    gate_w = jnp.stack([w_rgate[0], w_igate[0]]).reshape(2 * HEADS * 32, HEAD_DIM)
    gate_m = jnp.stack([m_w_rgate[0], m_w_igate[0]]).reshape(2 * HEADS * 32, HEAD_DIM)
    gate_v = jnp.stack([v_w_rgate[0], v_w_igate[0]]).reshape(2 * HEADS * 32, HEAD_DIM)
    o_gate = _adamw("adamw_gate", got[6], gate_w, gate_m, gate_v)
    o_gate = [a.reshape(2, 1, HEADS, 32, HEAD_DIM) for a in o_gate]
    o_wr = [a[0] for a in o_gate]
    o_wi = [a[1] for a in o_gate]

    def own(full, width):
        return lax.dynamic_slice_in_dim(full, me * width, width, axis=1)

    small_g = {
        "norm_mix_g": last_sum[0:1], "conv_w": own(vecs_sum[0:4], 128), "conv_b": vecs_sum[4:5],
        "b_rgate": own(vecs_sum[5:6].reshape(HEADS, HEAD_DIM), 32),
        "b_igate": own(vecs_sum[6:7].reshape(HEADS, HEAD_DIM), 32),
        "lru_lambda": vecs_sum[7:8], "sgu_ln_g": vecs_sum[8:9], "sgu_ln_b": vecs_sum[9:10],
        "norm_mlp_g": vecs_sum[10:11], "norm_final_g": vecs_sum[11:12],
        "sgu_b_s": vecs_sum[12, 0:GROUPS * CHUNK].reshape(GROUPS, CHUNK),
    }
    small_w = {"norm_mix_g": (norm_mix_g, m_norm_mix_g, v_norm_mix_g), "conv_w": (conv_w, m_conv_w, v_conv_w),
               "conv_b": (conv_b, m_conv_b, v_conv_b), "b_rgate": (b_rgate, m_b_rgate, v_b_rgate),
               "b_igate": (b_igate, m_b_igate, v_b_igate), "lru_lambda": (lru_lambda, m_lru_lambda, v_lru_lambda),
               "sgu_ln_g": (sgu_ln_g, m_sgu_ln_g, v_sgu_ln_g), "sgu_ln_b": (sgu_ln_b, m_sgu_ln_b, v_sgu_ln_b),
               "norm_mlp_g": (norm_mlp_g, m_norm_mlp_g, v_norm_mlp_g),
               "norm_final_g": (norm_final_g, m_norm_final_g, v_norm_final_g),
               "sgu_b_s": (sgu_b_s, m_sgu_b_s, v_sgu_b_s), "sgu_w_s": (sgu_w_s, m_sgu_w_s, v_sgu_w_s)}
    order = list(small_g)
    as2d = lambda k, a: a.reshape(small_g[k].shape)
    upd = _adamw_many("adamw_small", [small_g[k] for k in order], *[[as2d(k, small_w[k][q]) for k in order]
                                                                     for q in range(3)])
    o_small = {k: [a.reshape(small_w[k][0].shape) for a in (small_g[k],) + u] for k, u in zip(order, upd)}
    ws3 = [a[0].reshape(GROUPS * CHUNK, CHUNK) for a in small_w.pop("sgu_w_s")]
    o_small["sgu_w_s"] = [a.reshape(sgu_w_s.shape) for a in _adamw("adamw_ws", got_ws, *ws3)]

    per_weight = {"norm_mix_g": o_small["norm_mix_g"], "w_in": o_in, "conv_w": o_small["conv_w"],
                  "conv_b": o_small["conv_b"], "w_rgate": o_wr, "b_rgate": o_small["b_rgate"], "w_igate": o_wi,
                  "b_igate": o_small["b_igate"], "lru_lambda": o_small["lru_lambda"], "w_out_a": o_oa,
                  "sgu_ln_g": o_small["sgu_ln_g"], "sgu_ln_b": o_small["sgu_ln_b"], "sgu_w_s": o_small["sgu_w_s"],
                  "sgu_b_s": o_small["sgu_b_s"], "w_out_b": o_ob, "w_out": o_out, "norm_mlp_g": o_small["norm_mlp_g"],
                  "w_up": o_up, "w_down": o_down, "norm_final_g": o_small["norm_final_g"]}
    names_w = list(per_weight)
    return (loss, dx[None], *[per_weight[k][0] for k in names_w], *[per_weight[k][1] for k in names_w],
            *[per_weight[k][2] for k in names_w], *[per_weight[k][3] for k in names_w])
```

```python
import jax
import jax.numpy as jnp
from jax import lax
from jax.experimental import pallas as pl
from jax.experimental.pallas import tpu as pltpu

F32 = jnp.float32
BF16 = jnp.bfloat16
SDS = jax.ShapeDtypeStruct
MESH = pl.DeviceIdType.MESH
ANY = pl.BlockSpec(memory_space=pltpu.HBM)

D = 1024
N_SLOT = 8
W_IN_COLS = 768
FF_COLS = 512
HEADS, HEAD_DIM = 4, 256
GROUPS, GROUP_DIM = 4, 256
CHUNK = 128
CONV_K = 4
NORM_EPS = 1e-6
LN_EPS = 1e-5
LRU_C = 8.0
ADAM_LR, ADAM_B1, ADAM_B2, ADAM_EPS, ADAM_WD, ADAM_STEP = 0.001, 0.9, 0.999, 1e-08, 0.01, 10

TM_ROWS = 1024
TM_MERGE = 512
T_BRANCH_A = 256
T_BRANCH_B = 256
MiB = 1024 * 1024

_GELU_C = 0.7978845608028654
_GELU_A = 0.044715


def _cparams(sem, vmem_mib):
    return pltpu.CompilerParams(dimension_semantics=sem, vmem_limit_bytes=vmem_mib * MiB)


def _gelu(x):
    t = jnp.tanh(_GELU_C * (x + _GELU_A * x * x * x))
    return 0.5 * x * (1.0 + t)


def _gelu_and_grad(x):
    x2 = x * x
    t = jnp.tanh(_GELU_C * x * (1.0 + _GELU_A * x2))
    g = 0.5 * x * (1.0 + t)
    dg = 0.5 * (1.0 + t) + 0.5 * x * (1.0 - t * t) * _GELU_C * (1.0 + 3.0 * _GELU_A * x2)
    return g, dg


def _softplus(x):
    return jnp.maximum(x, 0.0) + jnp.log1p(jnp.exp(-jnp.abs(x)))


def _dot(a, b):
    return jnp.dot(a, b, preferred_element_type=F32)


def _dot_nt(a, b):
    return lax.dot_general(a, b, (((1,), (1,)), ((), ())), preferred_element_type=F32)


def _dot_tn(a, b):
    return lax.dot_general(a, b, (((0,), (0,)), ((), ())), preferred_element_type=F32)


def _rows_shifted(prev8, cur, k):
    ext = jnp.concatenate([prev8, cur], axis=0)
    return pltpu.roll(ext, k, 0)[8:]


def _rows_advanced(cur, next8, k):
    t = cur.shape[0]
    ext = jnp.concatenate([cur, next8], axis=0)
    return pltpu.roll(ext, t + 8 - k, 0)[:t]


def _slot_order(x, y, c):
    chip = 2 * x + y
    order = [2 * chip + c, 2 * chip + 1 - c]
    for _, _, pc in _other_chips(x, y):
        order += [2 * pc + c, 2 * pc + 1 - c]
    return jnp.stack(order).astype(jnp.int32)


def _in_proj(x, g_mix, w_in_own, order, comm=None):
    s = x.shape[0]
    tm = min(TM_ROWS, s)
    ni = s // tm

    def body(order_ref, x_ref, g_ref, own_ref, z_ref, nt_ref, wg_ref, n_s, w_s, send_sems, recv_sems, local_sems):
        j, i = pl.program_id(0), pl.program_id(1)
        px, py, c = _place()
        chip = 2 * px + py
        me = 2 * chip + c
        sib = (px, py, 1 - c)
        chips = _other_chips(px, py)

        def rc(k, src, blk, to):
            return pltpu.make_async_remote_copy(src_ref=src, dst_ref=w_s.at[blk], send_sem=send_sems.at[k],
                                                recv_sem=recv_sems.at[k], device_id=to, device_id_type=MESH)

        own_in = pltpu.make_async_copy(own_ref, w_s.at[me], local_sems.at[0])
        sends = [rc(0, own_ref, me, sib)] + [rc(1 + q, own_ref, me, (qx, qy, c)) for q, (qx, qy, _) in enumerate(chips)]
        passed = [rc(4 + q, w_s.at[2 * pc + c], 2 * pc + c, sib) for q, (_, _, pc) in enumerate(chips)]
        keep = pltpu.make_async_copy(w_s, wg_ref, local_sems.at[1])

        @pl.when((i == 0) & (j == 0))
        def _():
            own_in.start()
            for cp in sends:
                cp.start()
            own_in.wait()

        @pl.when((i == 0) & (j == 1))
        def _():
            rc(0, own_ref, 2 * chip + 1 - c, sib).wait_recv()

        for q, (_, _, pc) in enumerate(chips):
            @pl.when((i == 0) & (j == 2 + 2 * q))
            def _():
                rc(1 + q, own_ref, 2 * pc + c, sib).wait_recv()
                passed[q].start()

            @pl.when((i == 0) & (j == 3 + 2 * q))
            def _():
                rc(4 + q, own_ref, 2 * pc + 1 - c, sib).wait_recv()

        rows = pl.ds(pl.multiple_of(i * tm, tm), tm)

        @pl.when(j == 0)
        def _():
            xv = x_ref[...]
            rstd = lax.rsqrt(jnp.mean(xv * xv, axis=-1, keepdims=True) + NORM_EPS)
            nb = (xv * rstd * g_ref[...]).astype(BF16)
            n_s[rows, :] = nb
            nt_ref[...] = nb.T

        z_ref[...] = _dot(n_s[rows, :], w_s[order_ref[j]]).astype(BF16)

        @pl.when((i == 0) & (j == N_SLOT - 1))
        def _():
            keep.start()

        @pl.when((i == ni - 1) & (j == N_SLOT - 1))
        def _():
            for cp in sends + passed:
                cp.wait_send()
            keep.wait()

    first_pass = lambda j, i, o: (jnp.where(j == 0, i, ni - 1), 0)
    (z, n1, w_in_g), extra = _call(
        body, name="in_proj", grid=(N_SLOT, ni), prefetch=(order,),
        in_specs=[pl.BlockSpec((tm, D), first_pass),
                  pl.BlockSpec((1, D), lambda j, i, o: (0, 0)), ANY],
        out_specs=[pl.BlockSpec((tm, W_IN_COLS), lambda j, i, o: (i, o[j])),
                   pl.BlockSpec((D, tm), lambda j, i, o: (0, jnp.where(j == 0, i, ni - 1))), ANY],
        out_shape=[SDS((s, N_SLOT * W_IN_COLS), BF16), SDS((D, s), BF16), SDS((N_SLOT, D, W_IN_COLS), BF16)],
        scratch_shapes=[pltpu.VMEM((s, D), BF16), pltpu.VMEM((N_SLOT, D, W_IN_COLS), BF16),
                        pltpu.SemaphoreType.DMA((7,)), pltpu.SemaphoreType.DMA((7,)), pltpu.SemaphoreType.DMA((2,))],
        params=_cparams(("arbitrary", "arbitrary"), 56), args=(x, g_mix, w_in_own), comm=comm)
    return (z, n1, w_in_g), extra


def _lru_gates(xc, xcb, wr_ref, br, wi_ref, bi, sp_lam, a_s, b_s, r_s=None, i_s=None, m_s=None):
    for h in range(HEADS):
        sl = slice(h * HEAD_DIM, (h + 1) * HEAD_DIM)
        r = jax.nn.sigmoid(_dot(xcb[:, sl], wr_ref[h]) + br[:, sl])
        ig = jax.nn.sigmoid(_dot(xcb[:, sl], wi_ref[h]) + bi[:, sl])
        log_a = (-LRU_C) * r * sp_lam[:, sl]
        a = jnp.exp(log_a)
        mult = jnp.sqrt(-jnp.tanh(log_a) * (a * a + 1.0))
        a_s[:, sl] = a
        b_s[:, sl] = xc[:, sl] * ig * mult
        if r_s is not None:
            r_s[:, sl] = r
            i_s[:, sl] = ig
            m_s[:, sl] = mult


def _conv_fwd(xa, prev8, cw, cb):
    xc = cb + cw[0:1, :] * xa
    for k in range(1, CONV_K):
        xc = xc + cw[k:k + 1, :] * _rows_shifted(prev8, xa, k)
    return xc


def _branch_a_fwd(z, conv_w, conv_b, w_r, b_r, w_i, b_i, lam, comm=None):
    s = z.shape[0]
    ta = min(T_BRANCH_A, s)
    per16 = ta // 16

    def body(xa_ref, xp_ref, ga_ref, cw_ref, cb_ref, wr_ref, br_ref, wi_ref, bi_ref, lam_ref,
             ya_ref, hs_ref, a_s, b_s, h_s, carry_s):
        i = pl.program_id(0)

        @pl.when(i == 0)
        def _():
            carry_s[...] = jnp.zeros_like(carry_s)

        xa = xa_ref[...].astype(F32)
        prev8 = jnp.where(i > 0, xp_ref[...].astype(F32)[8:16], 0.0)
        xc = _conv_fwd(xa, prev8, cw_ref[...], cb_ref[...])
        sp_lam = _softplus(-lam_ref[...])
        _lru_gates(xc, xc.astype(BF16), wr_ref, br_ref[...], wi_ref, bi_ref[...], sp_lam, a_s, b_s)

        row = lax.broadcasted_iota(jnp.int32, (8, D), 0)

        def group(g, carry):
            off = pl.multiple_of(g * 8, 8)
            a8 = a_s[pl.ds(off, 8), :]
            b8 = b_s[pl.ds(off, 8), :]
            for d in (1, 2, 4):
                a_sh = jnp.where(row >= d, pltpu.roll(a8, d, 0), 1.0)
                b_sh = jnp.where(row >= d, pltpu.roll(b8, d, 0), 0.0)
                b8 = a8 * b_sh + b8
                a8 = a8 * a_sh
            h8 = b8 + a8 * carry
            h_s[pl.ds(off, 8), :] = h8
            return jnp.broadcast_to(h8[7:8, :], (8, D))

        carry_s[...] = lax.fori_loop(0, ta // 8, group, carry_s[...])
        hs = h_s[...]
        hs_ref[...] = hs.astype(BF16)
        ya_ref[...] = (hs * _gelu(ga_ref[...].astype(F32))).astype(BF16)

    vec = pl.BlockSpec((1, D), lambda i: (0, 0))
    gate = pl.BlockSpec((HEADS, HEAD_DIM, HEAD_DIM), lambda i: (0, 0, 0))
    return _call(
        body, name="branch_a_fwd", grid=(s // ta,),
        in_specs=[pl.BlockSpec((ta, D), lambda i: (i, 0)),
                  pl.BlockSpec((16, D), lambda i: (jnp.maximum(i * per16 - 1, 0), 0)),
                  pl.BlockSpec((ta, D), lambda i: (i, 1)),
                  pl.BlockSpec((CONV_K, D), lambda i: (0, 0)), vec, gate, vec, gate, vec, vec],
        out_specs=[pl.BlockSpec((ta, D), lambda i: (i, 0)), pl.BlockSpec((ta, D), lambda i: (i, 0))],
        out_shape=[SDS((s, D), BF16), SDS((s, D), BF16)],
        scratch_shapes=[pltpu.VMEM((ta, D), F32), pltpu.VMEM((ta, D), F32), pltpu.VMEM((ta, D), F32),
                        pltpu.VMEM((8, D), F32)],
        params=_cparams(("arbitrary",), 40), args=(z, z, z, conv_w, conv_b, w_r, b_r, w_i, b_i, lam), comm=comm)


def _sgu_common(ub, vb, lg, lb, with_grad):
    if with_grad:
        u, du = _gelu_and_grad(ub)
        v, dv = _gelu_and_grad(vb)
    else:
        u, v, du, dv = _gelu(ub), _gelu(vb), None, None
    mu = jnp.mean(v, axis=-1, keepdims=True)
    vc = v - mu
    rstd = lax.rsqrt(jnp.mean(vc * vc, axis=-1, keepdims=True) + LN_EPS)
    vhat = vc * rstd
    vln = vhat * lg + lb
    return u, du, dv, rstd, vhat, vln


def _masked_ws(ws_ref):
    t = lax.broadcasted_iota(jnp.int32, (CHUNK, CHUNK), 0)
    c = lax.broadcasted_iota(jnp.int32, (CHUNK, CHUNK), 1)
    keep = c <= t
    return [jnp.where(keep, ws_ref[g], 0.0).astype(BF16) for g in range(GROUPS)]


def _branch_b_fwd(z, ln_g, ln_b, w_s, b_s_t, comm=None):
    s = z.shape[0]
    tb = min(T_BRANCH_B, s)

    def body(ub_ref, vb_ref, lg_ref, lb_ref, ws_ref, bs_ref, yb_ref):
        u, _, _, _, _, vln = _sgu_common(ub_ref[...].astype(F32), vb_ref[...].astype(F32),
                                         lg_ref[...], lb_ref[...], False)
        vlnb = vln.astype(BF16)
        wm = _masked_ws(ws_ref)
        bs = bs_ref[...]
        for c in range(tb // CHUNK):
            rs = slice(c * CHUNK, (c + 1) * CHUNK)
            for g in range(GROUPS):
                cs = slice(g * GROUP_DIM, (g + 1) * GROUP_DIM)
                sp = _dot(wm[g], vlnb[rs, cs]) + bs[:, g:g + 1]
                yb_ref[rs, cs] = (u[rs, cs] * sp).astype(BF16)

    vec = pl.BlockSpec((1, D), lambda i: (0, 0))
    return _call(
        body, name="branch_b_fwd", grid=(s // tb,),
        in_specs=[pl.BlockSpec((tb, D), lambda i: (i, 2)), pl.BlockSpec((tb, D), lambda i: (i, 3)), vec, vec,
                  pl.BlockSpec((GROUPS, CHUNK, CHUNK), lambda i: (0, 0, 0)),
                  pl.BlockSpec((CHUNK, GROUPS), lambda i: (0, 0))],
        out_specs=[pl.BlockSpec((tb, D), lambda i: (i, 0))],
        out_shape=[SDS((s, D), BF16)], scratch_shapes=[],
        params=_cparams(("arbitrary",), 40), args=(z, z, ln_g, ln_b, w_s, b_s_t), comm=comm)


def _merge_out(ya, yb, z, x, w_oa, w_ob, w_out, comm=None):
    s = x.shape[0]
    tm = min(TM_MERGE, s)

    def body(ya_ref, yb_ref, ma_ref, mb_ref, x_ref, woa_ref, wob_ref, wo_ref, pa_ref, pb_ref, mg_ref, h1_ref):
        pa = _dot(ya_ref[...], woa_ref[...])
        pb = _dot(yb_ref[...], wob_ref[...])
        merged = (jax.nn.sigmoid(ma_ref[...].astype(F32)) * pa
                  + jax.nn.sigmoid(mb_ref[...].astype(F32)) * pb).astype(BF16)
        pa_ref[...] = pa.astype(BF16)
        pb_ref[...] = pb.astype(BF16)
        mg_ref[...] = merged
        h1_ref[...] = x_ref[...] + _dot(merged, wo_ref[...])

    row = pl.BlockSpec((tm, D), lambda i: (i, 0))
    wsp = pl.BlockSpec((D, D), lambda i: (0, 0))
    return _call(
        body, name="merge_out", grid=(s // tm,),
        in_specs=[row, row, pl.BlockSpec((tm, D), lambda i: (i, 4)), pl.BlockSpec((tm, D), lambda i: (i, 5)),
                  row, wsp, wsp, wsp],
        out_specs=[row, row, row, row],
        out_shape=[SDS((s, D), BF16), SDS((s, D), BF16), SDS((s, D), BF16), SDS((s, D), F32)], scratch_shapes=[],
        params=_cparams(("arbitrary",), 48), args=(ya, yb, z, z, x, w_oa, w_ob, w_out), comm=comm)


def _mlp_fwd(h1, g_mlp, w_up_g, w_down, g_fin, tgt):
    s = h1.shape[0]
    tm = min(TM_ROWS, s)
    nj = N_SLOT

    def body(h1_ref, gm_ref, wu_ref, wd_ref, gf_ref, t_ref, r_ref, at_ref, n2t_ref, dh2_ref, loss_ref, dgf_ref,
             n2_s, acc_s):
        i, j = pl.program_id(0), pl.program_id(1)

        @pl.when(j == 0)
        def _():
            hv = h1_ref[...]
            rstd = lax.rsqrt(jnp.mean(hv * hv, axis=-1, keepdims=True) + NORM_EPS)
            nb = (hv * rstd * gm_ref[...]).astype(BF16)
            n2_s[...] = nb
            n2t_ref[...] = nb.T
            acc_s[...] = jnp.zeros_like(acc_s)

        @pl.when((i == 0) & (j == 0))
        def _():
            loss_ref[...] = jnp.zeros_like(loss_ref)
            dgf_ref[...] = jnp.zeros_like(dgf_ref)

        r = jnp.maximum(_dot(n2_s[...], wu_ref[...]), 0.0)
        r_ref[...] = r.astype(BF16)
        act = (r * r).astype(BF16)
        at_ref[...] = act.T
        acc_s[...] += _dot(act, wd_ref[...])

        @pl.when(j == nj - 1)
        def _():
            h2 = h1_ref[...] + acc_s[...]
            rstd = lax.rsqrt(jnp.mean(h2 * h2, axis=-1, keepdims=True) + NORM_EPS)
            hh = h2 * rstd
            gf = gf_ref[...]
            e = hh * gf - t_ref[...]
            loss_ref[...] += jnp.sum(e * e) * (0.5 / D)
            dy = e * (1.0 / D)
            dgf_ref[...] += jnp.sum(dy * hh, axis=0, keepdims=True)
            dhh = dy * gf
            dh2_ref[...] = rstd * (dhh - hh * jnp.mean(dhh * hh, axis=-1, keepdims=True))

    row = pl.BlockSpec((tm, D), lambda i, j: (i, 0))
    vec = pl.BlockSpec((1, D), lambda i, j: (0, 0))
    return pl.pallas_call(
        body, name="mlp_fwd", grid=(s // tm, nj),
        in_specs=[row, vec, pl.BlockSpec((None, D, FF_COLS), lambda i, j: (j, 0, 0)),
                  pl.BlockSpec((FF_COLS, D), lambda i, j: (j, 0)), vec, row],
        out_specs=[pl.BlockSpec((tm, FF_COLS), lambda i, j: (i, j)), pl.BlockSpec((FF_COLS, tm), lambda i, j: (j, i)),
                   pl.BlockSpec((D, tm), lambda i, j: (0, i)), row, pl.BlockSpec((8, 128), lambda i, j: (0, 0)), vec],
        out_shape=[SDS((s, nj * FF_COLS), BF16), SDS((nj * FF_COLS, s), BF16), SDS((D, s), BF16), SDS((s, D), F32),
                   SDS((8, 128), F32), SDS((1, D), F32)],
        scratch_shapes=[pltpu.VMEM((tm, D), BF16), pltpu.VMEM((tm, D), F32)],
        compiler_params=_cparams(("arbitrary", "arbitrary"), 52),
    )(h1, g_mlp, w_up_g, w_down, g_fin, tgt)


def _mlp_bwd(dh2, r, w_down, w_up_g, h1, g_mlp, comm=None):
    s = h1.shape[0]
    tm = min(TM_ROWS, s)
    nj = N_SLOT

    def body(dh2_ref, r_ref, wd_ref, wu_ref, h1_ref, gm_ref, df_ref, dh1_ref, dgm_ref, dh2b_s, acc_s):
        i, j = pl.program_id(0), pl.program_id(1)

        @pl.when(j == 0)
        def _():
            dh2b_s[...] = dh2_ref[...].astype(BF16)
            acc_s[...] = jnp.zeros_like(acc_s)

        @pl.when((i == 0) & (j == 0))
        def _():
            dgm_ref[...] = jnp.zeros_like(dgm_ref)

        d_act = _dot_nt(dh2b_s[...], wd_ref[...])
        df = (d_act * (2.0 * r_ref[...].astype(F32))).astype(BF16)
        df_ref[...] = df
        acc_s[...] += _dot_nt(df, wu_ref[...])

        @pl.when(j == nj - 1)
        def _():
            hv = h1_ref[...]
            rstd = lax.rsqrt(jnp.mean(hv * hv, axis=-1, keepdims=True) + NORM_EPS)
            hh = hv * rstd
            dn2 = acc_s[...]
            dgm_ref[...] += jnp.sum(dn2 * hh, axis=0, keepdims=True)
            dhat = dn2 * gm_ref[...]
            dh1_ref[...] = dh2_ref[...] + rstd * (dhat - hh * jnp.mean(dhat * hh, axis=-1, keepdims=True))

    row = pl.BlockSpec((tm, D), lambda i, j: (i, 0))
    vec = pl.BlockSpec((1, D), lambda i, j: (0, 0))
    ffb = pl.BlockSpec((tm, FF_COLS), lambda i, j: (i, j))
    return _call(
        body, name="mlp_bwd", grid=(s // tm, nj),
        in_specs=[row, ffb, pl.BlockSpec((FF_COLS, D), lambda i, j: (j, 0)),
                  pl.BlockSpec((None, D, FF_COLS), lambda i, j: (j, 0, 0)), row, vec],
        out_specs=[ffb, row, vec],
        out_shape=[SDS((s, nj * FF_COLS), BF16), SDS((s, D), F32), SDS((1, D), F32)],
        scratch_shapes=[pltpu.VMEM((tm, D), BF16), pltpu.VMEM((tm, D), F32)],
        params=_cparams(("arbitrary", "arbitrary"), 52), args=(dh2, r, w_down, w_up_g, h1, g_mlp), comm=comm)


def _merge_bwd(dh1, z, pa, pb, w_out, w_oa, w_ob, comm=None):
    s = dh1.shape[0]
    tm = min(TM_MERGE, s)

    def body(dh1_ref, ma_ref, mb_ref, pa_ref, pb_ref, wo_ref, woa_ref, wob_ref,
             dz_ref, dpa_ref, dpb_ref, dya_ref, dyb_ref):
        dm = _dot_nt(dh1_ref[...].astype(BF16), wo_ref[...])
        sa = jax.nn.sigmoid(ma_ref[...].astype(F32))
        sb = jax.nn.sigmoid(mb_ref[...].astype(F32))
        dpa = (dm * sa).astype(BF16)
        dpb = (dm * sb).astype(BF16)
        dz_ref[:, 0:D] = (dm * pa_ref[...].astype(F32) * sa * (1.0 - sa)).astype(BF16)
        dz_ref[:, D:2 * D] = (dm * pb_ref[...].astype(F32) * sb * (1.0 - sb)).astype(BF16)
        dpa_ref[...] = dpa
        dpb_ref[...] = dpb
        dya_ref[...] = _dot_nt(dpa, woa_ref[...]).astype(BF16)
        dyb_ref[...] = _dot_nt(dpb, wob_ref[...]).astype(BF16)

    row = pl.BlockSpec((tm, D), lambda i: (i, 0))
    wsp = pl.BlockSpec((D, D), lambda i: (0, 0))
    return _call(
        body, name="merge_bwd", grid=(s // tm,),
        in_specs=[row, pl.BlockSpec((tm, D), lambda i: (i, 4)), pl.BlockSpec((tm, D), lambda i: (i, 5)),
                  row, row, wsp, wsp, wsp],
        out_specs=[pl.BlockSpec((tm, 2 * D), lambda i: (i, 2)), row, row, row, row],
        out_shape=[SDS((s, 6 * D), BF16)] + [SDS((s, D), BF16)] * 4, scratch_shapes=[],
        params=_cparams(("arbitrary",), 48), args=(dh1, z, z, pa, pb, w_out, w_oa, w_ob), comm=comm)


def _branch_b_bwd(dz, dyb, z, ln_g, ln_b, w_s, b_s_t, comm=None):
    s = z.shape[0]
    tb = min(T_BRANCH_B, s)

    def body(dz_in, dyb_ref, ub_ref, vb_ref, lg_ref, lb_ref, ws_ref, bs_ref,
             dz_ref, dws_ref, dbs_ref, dln_ref, du_s, dvln_s):
        del dz_in

        @pl.when(pl.program_id(0) == 0)
        def _():
            dws_ref[...] = jnp.zeros_like(dws_ref)
            dbs_ref[...] = jnp.zeros_like(dbs_ref)
            dln_ref[...] = jnp.zeros_like(dln_ref)

        lg = lg_ref[...]
        u, du, dv, rstd, vhat, vln = _sgu_common(ub_ref[...].astype(F32), vb_ref[...].astype(F32),
                                                 lg, lb_ref[...], True)
        vlnb = vln.astype(BF16)
        dyb_v = dyb_ref[...].astype(F32)
        wm = _masked_ws(ws_ref)
        keep = (lax.broadcasted_iota(jnp.int32, (CHUNK, CHUNK), 1)
                <= lax.broadcasted_iota(jnp.int32, (CHUNK, CHUNK), 0))
        bs = bs_ref[...]
        for c in range(tb // CHUNK):
            rs = slice(c * CHUNK, (c + 1) * CHUNK)
            for g in range(GROUPS):
                cs = slice(g * GROUP_DIM, (g + 1) * GROUP_DIM)
                v_blk = vlnb[rs, cs]
                sp = _dot(wm[g], v_blk) + bs[:, g:g + 1]
                d_sp = dyb_v[rs, cs] * u[rs, cs]
                d_spb = d_sp.astype(BF16)
                du_s[rs, cs] = dyb_v[rs, cs] * sp
                dvln_s[rs, cs] = _dot_tn(wm[g], d_spb)
                dws_ref[g] += jnp.where(keep, _dot_nt(d_spb, v_blk), 0.0)
                dbs_ref[g] += jnp.broadcast_to(jnp.sum(d_sp, axis=-1, keepdims=True), (CHUNK, CHUNK))
        dvln = dvln_s[...]
        dln_ref[0:1, :] += jnp.sum(dvln * vhat, axis=0, keepdims=True)
        dln_ref[1:2, :] += jnp.sum(dvln, axis=0, keepdims=True)
        dvh = dvln * lg
        d_v = rstd * (dvh - jnp.mean(dvh, axis=-1, keepdims=True)
                      - vhat * jnp.mean(dvh * vhat, axis=-1, keepdims=True))
        dz_ref[:, 0:D] = (du_s[...] * du).astype(BF16)
        dz_ref[:, D:2 * D] = (d_v * dv).astype(BF16)

    vec = pl.BlockSpec((1, D), lambda i: (0, 0))
    sq = pl.BlockSpec((GROUPS, CHUNK, CHUNK), lambda i: (0, 0, 0))
    return _call(
        body, name="branch_b_bwd", grid=(s // tb,),
        in_specs=[ANY, pl.BlockSpec((tb, D), lambda i: (i, 0)),
                  pl.BlockSpec((tb, D), lambda i: (i, 2)), pl.BlockSpec((tb, D), lambda i: (i, 3)), vec, vec, sq,
                  pl.BlockSpec((CHUNK, GROUPS), lambda i: (0, 0))],
        out_specs=[pl.BlockSpec((tb, 2 * D), lambda i: (i, 1)), sq, sq, pl.BlockSpec((8, D), lambda i: (0, 0))],
        out_shape=[SDS(dz.shape, BF16), SDS((GROUPS, CHUNK, CHUNK), F32), SDS((GROUPS, CHUNK, CHUNK), F32),
                   SDS((8, D), F32)],
        scratch_shapes=[pltpu.VMEM((tb, D), F32), pltpu.VMEM((tb, D), F32)], aliases={0: 0},
        params=_cparams(("arbitrary",), 40), args=(dz, dyb, z, z, ln_g, ln_b, w_s, b_s_t), comm=comm)


def _branch_a_bwd(dz, dya, z, hs, conv_w, conv_b, w_r, b_r, w_i, b_i, lam, comm=None):
    s = z.shape[0]
    ta = min(T_BRANCH_A, s)
    nb = s // ta
    per16 = ta // 16

    def body(dz_in, dya_ref, xa_ref, xp_ref, ga_ref, hs_ref, hp_ref, cw_ref, cb_ref, wr_ref, br_ref, wi_ref,
             bi_ref, lam_ref, dz_ref, vec_ref, dwr_ref, dwi_ref,
             a_s, b_s, h_s, r_s, i_s, m_s, dcar_s, acar_s, dxc_s):
        del dz_in
        i = pl.program_id(0)
        blk = nb - 1 - i

        @pl.when(i == 0)
        def _():
            dcar_s[...] = jnp.zeros_like(dcar_s)
            acar_s[...] = jnp.zeros_like(acar_s)
            dxc_s[...] = jnp.zeros_like(dxc_s)
            vec_ref[...] = jnp.zeros_like(vec_ref)
            dwr_ref[...] = jnp.zeros_like(dwr_ref)
            dwi_ref[...] = jnp.zeros_like(dwi_ref)

        cw = cw_ref[...]
        lam_v = lam_ref[...]
        xa = xa_ref[...].astype(F32)
        prev8 = jnp.where(blk > 0, xp_ref[...].astype(F32)[8:16], 0.0)
        xc = _conv_fwd(xa, prev8, cw, cb_ref[...])
        xcb = xc.astype(BF16)
        sp_lam = _softplus(-lam_v)
        _lru_gates(xc, xcb, wr_ref, br_ref[...], wi_ref, bi_ref[...], sp_lam, a_s, b_s, r_s, i_s, m_s)

        hs_v = hs_ref[...].astype(F32)
        hprev8 = jnp.where(blk > 0, hp_ref[...].astype(F32)[8:16], 0.0)
        h_m1 = _rows_shifted(hprev8, hs_v, 1)
        gg, dgg = _gelu_and_grad(ga_ref[...].astype(F32))
        dya_v = dya_ref[...].astype(F32)
        dz_ref[:, D:2 * D] = (dya_v * hs_v * dgg).astype(BF16)

        a_v = a_s[...]
        a_s[...] = _rows_advanced(a_v, acar_s[...], 1)
        b_s[...] = dya_v * gg

        row = lax.broadcasted_iota(jnp.int32, (8, D), 0)
        ng = ta // 8

        def group(gi, carry):
            off = pl.multiple_of((ng - 1 - gi) * 8, 8)
            c8 = a_s[pl.ds(off, 8), :]
            d8 = b_s[pl.ds(off, 8), :]
            for d in (1, 2, 4):
                c_sh = jnp.where(row < 8 - d, pltpu.roll(c8, 8 - d, 0), 1.0)
                d_sh = jnp.where(row < 8 - d, pltpu.roll(d8, 8 - d, 0), 0.0)
                d8 = c8 * d_sh + d8
                c8 = c8 * c_sh
            dh8 = d8 + c8 * carry
            h_s[pl.ds(off, 8), :] = dh8
            return jnp.broadcast_to(dh8[0:1, :], (8, D))

        dcar_s[...] = lax.fori_loop(0, ng, group, dcar_s[...])
        acar_s[...] = jnp.broadcast_to(a_v[0:1, :], (8, D))

        dbx = h_s[...]
        r_v, i_v, m_v = r_s[...], i_s[...], m_s[...]
        d_mult = dbx * xc * i_v
        d_loga = dbx * h_m1 * a_v - d_mult * (a_v * a_v) / m_v
        d_pr = d_loga * ((-LRU_C) * sp_lam) * r_v * (1.0 - r_v)
        d_pi = dbx * xc * m_v * i_v * (1.0 - i_v)
        vec_ref[7:8, :] += jnp.sum(d_loga * r_v, axis=0, keepdims=True) * (LRU_C * jax.nn.sigmoid(-lam_v))
        vec_ref[5:6, :] += jnp.sum(d_pr, axis=0, keepdims=True)
        vec_ref[6:7, :] += jnp.sum(d_pi, axis=0, keepdims=True)
        d_prb = d_pr.astype(BF16)
        d_pib = d_pi.astype(BF16)
        h_s[...] = dbx * i_v * m_v
        for h in range(HEADS):
            sl = slice(h * HEAD_DIM, (h + 1) * HEAD_DIM)
            h_s[:, sl] += _dot_nt(d_prb[:, sl], wr_ref[h]) + _dot_nt(d_pib[:, sl], wi_ref[h])
            dwr_ref[h] += _dot_tn(xcb[:, sl], d_prb[:, sl])
            dwi_ref[h] += _dot_tn(xcb[:, sl], d_pib[:, sl])
        d_xc = h_s[...]
        vec_ref[4:5, :] += jnp.sum(d_xc, axis=0, keepdims=True)
        vec_ref[0:1, :] += jnp.sum(d_xc * xa, axis=0, keepdims=True)
        d_xa = cw[0:1, :] * d_xc
        nxt = dxc_s[...]
        for k in range(1, CONV_K):
            vec_ref[k:k + 1, :] += jnp.sum(d_xc * _rows_shifted(prev8, xa, k), axis=0, keepdims=True)
            d_xa = d_xa + cw[k:k + 1, :] * _rows_advanced(d_xc, nxt, k)
        dz_ref[:, 0:D] = d_xa.astype(BF16)
        dxc_s[...] = d_xc[0:8, :]

    vec = pl.BlockSpec((1, D), lambda i: (0, 0))
    gate = pl.BlockSpec((HEADS, HEAD_DIM, HEAD_DIM), lambda i: (0, 0, 0))
    cur = lambda c: pl.BlockSpec((ta, D), lambda i: (nb - 1 - i, c))
    before = lambda c: pl.BlockSpec((16, D), lambda i: (jnp.maximum((nb - 1 - i) * per16 - 1, 0), c))
    return _call(
        body, name="branch_a_bwd", grid=(nb,),
        in_specs=[ANY, cur(0), cur(0), before(0), cur(1), cur(0), before(0),
                  pl.BlockSpec((CONV_K, D), lambda i: (0, 0)), vec, gate, vec, gate, vec, vec],
        out_specs=[pl.BlockSpec((ta, 2 * D), lambda i: (nb - 1 - i, 0)), pl.BlockSpec((8, D), lambda i: (0, 0)),
                   gate, gate],
        out_shape=[SDS(dz.shape, BF16), SDS((8, D), F32), SDS((HEADS, HEAD_DIM, HEAD_DIM), F32),
                   SDS((HEADS, HEAD_DIM, HEAD_DIM), F32)],
        scratch_shapes=[pltpu.VMEM((ta, D), F32)] * 6 + [pltpu.VMEM((8, D), F32)] * 3, aliases={0: 0},
        params=_cparams(("arbitrary",), 48),
        args=(dz, dya, z, z, z, hs, hs, conv_w, conv_b, w_r, b_r, w_i, b_i, lam), comm=comm)


def _in_bwd(dz, w_in_g, x, dh1, g_mix, comm=None):
    s = x.shape[0]
    tm = min(TM_ROWS, s)
    nj = N_SLOT

    def body(dz_ref, w_ref, x_ref, dh1_ref, g_ref, dx_ref, dg_ref, acc_s):
        i, j = pl.program_id(0), pl.program_id(1)

        @pl.when(j == 0)
        def _():
            acc_s[...] = jnp.zeros_like(acc_s)

        @pl.when((i == 0) & (j == 0))
        def _():
            dg_ref[...] = jnp.zeros_like(dg_ref)

        acc_s[...] += _dot_nt(dz_ref[...], w_ref[...])

        @pl.when(j == nj - 1)
        def _():
            xv = x_ref[...]
            rstd = lax.rsqrt(jnp.mean(xv * xv, axis=-1, keepdims=True) + NORM_EPS)
            xh = xv * rstd
            dn = acc_s[...]
            dg_ref[...] += jnp.sum(dn * xh, axis=0, keepdims=True)
            dhat = dn * g_ref[...]
            dx_ref[...] = dh1_ref[...] + rstd * (dhat - xh * jnp.mean(dhat * xh, axis=-1, keepdims=True))

    row = pl.BlockSpec((tm, D), lambda i, j: (i, 0))
    vec = pl.BlockSpec((1, D), lambda i, j: (0, 0))
    return _call(
        body, name="in_bwd", grid=(s // tm, nj),
        in_specs=[pl.BlockSpec((tm, W_IN_COLS), lambda i, j: (i, j)),
                  pl.BlockSpec((None, D, W_IN_COLS), lambda i, j: (j, 0, 0)), row, row, vec],
        out_specs=[row, vec],
        out_shape=[SDS((s, D), F32), SDS((1, D), F32)],
        scratch_shapes=[pltpu.VMEM((tm, D), F32)],
        params=_cparams(("arbitrary", "arbitrary"), 48), args=(dz, w_in_g, x, dh1, g_mix), comm=comm)


def _wgrad(name, a, b, nblk, a_split, b_split):
    s = a.shape[0]
    ts = min(TM_ROWS, s)
    a_w = a.shape[1] // nblk if a_split else a.shape[1]
    b_w = b.shape[1] // nblk if b_split else b.shape[1]

    def body(a_ref, b_ref, o_ref, acc_s):
        t = pl.program_id(1)

        @pl.when(t == 0)
        def _():
            acc_s[...] = jnp.zeros_like(acc_s)

        acc_s[...] += _dot_tn(a_ref[...].astype(BF16), b_ref[...].astype(BF16))

        @pl.when(t == pl.num_programs(1) - 1)
        def _():
            o_ref[...] = acc_s[...].astype(BF16)

    return pl.pallas_call(
        body, name=name, grid=(nblk, s // ts),
        in_specs=[pl.BlockSpec((ts, a_w), (lambda k, t: (t, k)) if a_split else (lambda k, t: (t, 0))),
                  pl.BlockSpec((ts, b_w), (lambda k, t: (t, k)) if b_split else (lambda k, t: (t, 0)))],
        out_specs=pl.BlockSpec((None, a_w, b_w), lambda k, t: (k, 0, 0)),
        out_shape=SDS((nblk, a_w, b_w), BF16),
        scratch_shapes=[pltpu.VMEM((a_w, b_w), F32)],
        compiler_params=_cparams(("arbitrary", "arbitrary"), 48),
    )(a, b)


def _wgrad_t(name, a_t, b, nblk, a_split, b_split, tokens):
    s = b.shape[0]
    ts = min(tokens, s)
    a_w = a_t.shape[0] // nblk if a_split else a_t.shape[0]
    b_w = b.shape[1] // nblk if b_split else b.shape[1]

    def body(a_ref, b_ref, o_ref, acc_s):
        t = pl.program_id(1)

        @pl.when(t == 0)
        def _():
            acc_s[...] = jnp.zeros_like(acc_s)

        acc_s[...] += _dot(a_ref[...], b_ref[...].astype(BF16))

        @pl.when(t == pl.num_programs(1) - 1)
        def _():
            o_ref[...] = acc_s[...].astype(BF16)

    return pl.pallas_call(
        body, name=name, grid=(nblk, s // ts),
        in_specs=[pl.BlockSpec((a_w, ts), (lambda k, t: (k, t)) if a_split else (lambda k, t: (0, t))),
                  pl.BlockSpec((ts, b_w), (lambda k, t: (t, k)) if b_split else (lambda k, t: (t, 0)))],
        out_specs=pl.BlockSpec((None, a_w, b_w), lambda k, t: (k, 0, 0)),
        out_shape=SDS((nblk, a_w, b_w), BF16),
        scratch_shapes=[pltpu.VMEM((a_w, b_w), F32)],
        compiler_params=_cparams(("arbitrary", "arbitrary"), 48),
    )(a_t, b)


def _place():
    x, y, c = lax.axis_index("x"), lax.axis_index("y"), lax.axis_index("c")
    return x, y, c


def _other_chips(x, y):
    return [(x, 1 - y, 2 * x + 1 - y), (1 - x, y, 2 * (1 - x) + y), (1 - x, 1 - y, 2 * (1 - x) + 1 - y)]


class _Plan:
    def __init__(self, arrays, out_shape, sems, start, finish, stages=()):
        self.arrays, self.out_shape, self.sems, self.start, self.finish = arrays, out_shape, sems, start, finish
        self.stages = list(stages)


def _gather_plan(shards):
    n = len(shards)

    def copies(ins, outs, sems):
        send_sems, recv_sems, local_sems = sems
        x, y, c = _place()
        chip = 2 * x + y
        me = 2 * chip + c
        sib = (x, y, 1 - c)
        chips = _other_chips(x, y)

        def rc(k, t, src, blk, to):
            return pltpu.make_async_remote_copy(
                src_ref=src, dst_ref=outs[t].at[blk], send_sem=send_sems.at[k * n + t],
                recv_sem=recv_sems.at[k * n + t], device_id=to, device_id_type=MESH)

        local = [pltpu.make_async_copy(ins[t], outs[t].at[me], local_sems.at[t]) for t in range(n)]
        sends = [rc(0, t, ins[t], me, sib) for t in range(n)]
        for j, (px, py, _) in enumerate(chips):
            sends += [rc(1 + j, t, ins[t], me, (px, py, c)) for t in range(n)]
        return rc, local, sends, chips, chip, c, sib

    def start(ins, outs, sems):
        _, local, sends, _, _, _, _ = copies(ins, outs, sems)
        for cp in local + sends:
            cp.start()

    def pass_on(j):
        def stage(ins, outs, sems):
            rc, _, _, chips, _, c, sib = copies(ins, outs, sems)
            blk = 2 * chips[j][2] + c
            for t in range(n):
                rc(1 + j, t, ins[t], blk, sib).wait_recv()
            for t in range(n):
                rc(4 + j, t, outs[t].at[blk], blk, sib).start()
        return stage

    def finish(ins, outs, sems):
        rc, local, sends, chips, chip, c, sib = copies(ins, outs, sems)
        passed = [rc(4 + j, t, outs[t].at[2 * pc + c], 2 * pc + c, sib)
                  for j, (_, _, pc) in enumerate(chips) for t in range(n)]
        for t in range(n):
            rc(0, t, ins[t], 2 * chip + 1 - c, sib).wait_recv()
        for j, (px, py, pc) in enumerate(chips):
            for t in range(n):
                rc(4 + j, t, ins[t], 2 * pc + 1 - c, sib).wait_recv()
        for cp in sends + passed:
            cp.wait_send()
        for cp in local:
            cp.wait()

    return _Plan(list(shards), [SDS((N_SLOT,) + tuple(a.shape), a.dtype) for a in shards],
                 [pltpu.SemaphoreType.DMA((7 * n,)), pltpu.SemaphoreType.DMA((7 * n,)),
                  pltpu.SemaphoreType.DMA((n,))], start, finish, [pass_on(j) for j in range(3)])


def _sibling_plan(grads, whole=()):
    n, m = len(grads), len(whole)

    def copies(ins, outs, sems):
        send_sems, recv_sems = sems
        x, y, c = _place()
        sib = (x, y, 1 - c)

        def rc(t, src, dst):
            return pltpu.make_async_remote_copy(src_ref=src, dst_ref=dst, send_sem=send_sems.at[t],
                                                recv_sem=recv_sems.at[t], device_id=sib, device_id_type=MESH)
        return rc, c

    def start(ins, outs, sems):
        rc, c = copies(ins, outs, sems)
        for t in range(n):
            for j in range(4):
                rc(t, ins[t].at[2 * j + 1 - c], outs[t].at[j]).start()
        for t in range(n, n + m):
            rc(t, ins[t], outs[t]).start()

    def finish(ins, outs, sems):
        rc, _ = copies(ins, outs, sems)
        for t in range(n):
            rc(t, ins[t].at[pl.ds(0, 4)], outs[t]).wait()
        for t in range(n, n + m):
            rc(t, ins[t], outs[t]).wait()

    return _Plan(list(grads) + list(whole),
                 [SDS((4,) + tuple(g.shape[1:]), g.dtype) for g in grads] + [SDS(a.shape, a.dtype) for a in whole],
                 [pltpu.SemaphoreType.DMA((n + m,)), pltpu.SemaphoreType.DMA((n + m,))], start, finish)


def _chips_plan(parts, whole=()):
    n, m = len(parts), len(whole)

    def src_of(ins, t, pc):
        return ins[t].at[pc] if t < n else ins[t]

    def local_copies(ins, outs, sems, chip):
        return [pltpu.make_async_copy(src_of(ins, t, chip), outs[t].at[chip], sems[2].at[t]) for t in range(n + m)]

    def start(ins, outs, sems):
        send_sems, recv_sems, _ = sems
        x, y, c = _place()
        chip = 2 * x + y
        for cp in local_copies(ins, outs, sems, chip):
            cp.start()
        for px, py, pc in _other_chips(x, y):
            for t in range(n + m):
                pltpu.make_async_remote_copy(src_ref=src_of(ins, t, pc), dst_ref=outs[t].at[chip],
                                             send_sem=send_sems.at[t], recv_sem=recv_sems.at[t],
                                             device_id=(px, py, c), device_id_type=MESH).start()

    def finish(ins, outs, sems):
        send_sems, recv_sems, _ = sems
        x, y, c = _place()
        for t in range(n + m):
            three = outs[t].at[pl.ds(0, 3)]
            pltpu.make_async_remote_copy(src_ref=three, dst_ref=three, send_sem=send_sems.at[t],
                                         recv_sem=recv_sems.at[t], device_id=(x, y, c), device_id_type=MESH).wait()
        for cp in local_copies(ins, outs, sems, 2 * x + y):
            cp.wait()

    return _Plan(list(parts) + list(whole),
                 [SDS(p.shape, p.dtype) for p in parts] + [SDS((4,) + tuple(a.shape), a.dtype) for a in whole],
                 [pltpu.SemaphoreType.DMA((n + m,)), pltpu.SemaphoreType.DMA((n + m,)),
                  pltpu.SemaphoreType.DMA((n + m,))], start, finish)


def _exchange_plan(arr):
    def peers(x, y, c):
        flip = lambda v, f: 1 - v if f else v
        return [(flip(x, fx), flip(y, fy), flip(c, fc))
                for fx in (0, 1) for fy in (0, 1) for fc in (0, 1) if fx or fy or fc]

    def start(ins, outs, sems):
        x, y, c = _place()
        me = 4 * x + 2 * y + c
        pltpu.make_async_copy(ins[0], outs[0].at[me], sems[2].at[0]).start()
        for to in peers(x, y, c):
            pltpu.make_async_remote_copy(src_ref=ins[0], dst_ref=outs[0].at[me], send_sem=sems[0].at[0],
                                         recv_sem=sems[1].at[0], device_id=to, device_id_type=MESH).start()

    def finish(ins, outs, sems):
        x, y, c = _place()
        seven = outs[0].at[pl.ds(0, 7)]
        pltpu.make_async_remote_copy(src_ref=seven, dst_ref=seven, send_sem=sems[0].at[0], recv_sem=sems[1].at[0],
                                     device_id=(x, y, c), device_id_type=MESH).wait()
        pltpu.make_async_copy(ins[0], outs[0].at[4 * x + 2 * y + c], sems[2].at[0]).wait()

    return _Plan([arr], [SDS((N_SLOT,) + tuple(arr.shape), arr.dtype)],
                 [pltpu.SemaphoreType.DMA((1,)), pltpu.SemaphoreType.DMA((1,)), pltpu.SemaphoreType.DMA((1,))],
                 start, finish)


def _join(*plans):
    def cut(seq, sizes):
        out, at = [], 0
        for k in sizes:
            out.append(seq[at:at + k])
            at += k
        return out

    n_arr = [len(p.arrays) for p in plans]
    n_sem = [len(p.sems) for p in plans]

    def start(ins, outs, sems):
        for p, i, o, s in zip(plans, cut(ins, n_arr), cut(outs, n_arr), cut(sems, n_sem)):
            p.start(i, o, s)

    def finish(ins, outs, sems):
        for p, i, o, s in zip(plans, cut(ins, n_arr), cut(outs, n_arr), cut(sems, n_sem)):
            p.finish(i, o, s)

    def stage(q):
        def run(ins, outs, sems):
            for p, i, o, s in zip(plans, cut(ins, n_arr), cut(outs, n_arr), cut(sems, n_sem)):
                if q < len(p.stages):
                    p.stages[q](i, o, s)
        return run

    return _Plan([a for p in plans for a in p.arrays], [o for p in plans for o in p.out_shape],
                 [s for p in plans for s in p.sems], start, finish,
                 [stage(q) for q in range(max(len(p.stages) for p in plans))])


def _run_plan(name, plan):
    k = len(plan.arrays)

    def body(*refs):
        ins, outs, sems = refs[:k], refs[k:2 * k], refs[2 * k:]
        plan.start(ins, outs, sems)
        for stage in plan.stages:
            stage(ins, outs, sems)
        plan.finish(ins, outs, sems)

    return pl.pallas_call(
        body, name=name, in_specs=[ANY] * k, out_specs=[ANY] * k, out_shape=plan.out_shape,
        scratch_shapes=plan.sems, compiler_params=pltpu.CompilerParams(has_side_effects=True),
    )(*plan.arrays)


def _call(body, *, name, grid, in_specs, out_specs, out_shape, scratch_shapes, params, args, comm=None,
          aliases=None, prefetch=()):
    aliases = aliases or {}
    n_pre = len(prefetch)

    def launch(fn, ins_specs, outs_specs, outs_shape, scratch, operands):
        spec = pltpu.PrefetchScalarGridSpec(num_scalar_prefetch=n_pre, grid=grid, in_specs=ins_specs,
                                            out_specs=outs_specs, scratch_shapes=scratch)
        return pl.pallas_call(fn, name=name, grid_spec=spec, out_shape=outs_shape, compiler_params=params,
                              input_output_aliases=aliases)(*prefetch, *operands)

    if comm is None:
        return list(launch(body, in_specs, out_specs, out_shape, scratch_shapes, args)), []
    n_in, n_out, n_scr, k = len(in_specs), len(out_specs), len(scratch_shapes), len(comm.arrays)

    def wrapped(*refs):
        pre, refs = refs[:n_pre], refs[n_pre:]
        ins = refs[:n_in]
        c_in = refs[n_in:n_in + k]
        outs = refs[n_in + k:n_in + k + n_out]
        c_out = refs[n_in + k + n_out:n_in + 2 * k + n_out]
        scr = refs[n_in + 2 * k + n_out:n_in + 2 * k + n_out + n_scr]
        sems = refs[n_in + 2 * k + n_out + n_scr:]
        step, steps = pl.program_id(0), grid[0]
        for d in range(1, len(grid)):
            step, steps = step * grid[d] + pl.program_id(d), steps * grid[d]

        @pl.when(step == 0)
        def _():
            comm.start(c_in, c_out, sems)

        gap = max(1, steps // 8)
        for q, stage in enumerate(comm.stages):
            @pl.when(step == max(0, steps - 1 - (len(comm.stages) - q) * gap))
            def _():
                stage(c_in, c_out, sems)

        body(*pre, *ins, *outs, *scr)

        @pl.when(step == steps - 1)
        def _():
            comm.finish(c_in, c_out, sems)

    res = launch(wrapped, list(in_specs) + [ANY] * k, list(out_specs) + [ANY] * k,
                 list(out_shape) + list(comm.out_shape), list(scratch_shapes) + list(comm.sems),
                 tuple(args) + tuple(comm.arrays))
    return list(res[:n_out]), list(res[n_out:])


def _row_tile(rows):
    for t in (512, 256, 128, 64, 32, 16, 8):
        if rows % t == 0:
            return t
    return rows


def _pair_sum(name, g8, recv4, core):
    _, rows, cols = recv4.shape
    tr = _row_tile(rows)
    g42 = g8.reshape(4, 2, rows, cols)

    def body(c_ref, g_ref, r_ref, o_ref):
        del c_ref
        o_ref[...] = (g_ref[...].astype(F32) + r_ref[...].astype(F32)).astype(o_ref.dtype)

    return pl.pallas_call(
        body, name=name,
        grid_spec=pltpu.PrefetchScalarGridSpec(
            num_scalar_prefetch=1, grid=(4, rows // tr),
            in_specs=[pl.BlockSpec((None, None, tr, cols), lambda j, i, c_ref: (j, c_ref[0], i, 0)),
                      pl.BlockSpec((None, tr, cols), lambda j, i, c_ref: (j, i, 0))],
            out_specs=pl.BlockSpec((None, tr, cols), lambda j, i, c_ref: (j, i, 0))),
        out_shape=SDS(recv4.shape, g8.dtype),
        compiler_params=_cparams(("arbitrary", "arbitrary"), 32),
    )(core, g42, recv4)


def _add2(name, a, b):
    rows, cols = a.shape
    tr = _row_tile(rows)

    def body(a_ref, b_ref, o_ref):
        o_ref[...] = a_ref[...] + b_ref[...]

    blk = pl.BlockSpec((tr, cols), lambda i: (i, 0))
    return pl.pallas_call(body, name=name, grid=(rows // tr,), in_specs=[blk, blk], out_specs=blk,
                          out_shape=SDS(a.shape, a.dtype),
                          compiler_params=_cparams(("arbitrary",), 32))(a, b)


def _sum_terms(name, terms):
    k, rows, cols = terms.shape
    tr = _row_tile(rows)

    def body(r_ref, o_ref):
        acc = r_ref[0]
        for q in range(1, k):
            acc = acc + r_ref[q]
        o_ref[...] = acc

    return pl.pallas_call(body, name=name, grid=(rows // tr,),
                          in_specs=[pl.BlockSpec((k, tr, cols), lambda i: (0, i, 0))],
                          out_specs=pl.BlockSpec((tr, cols), lambda i: (i, 0)),
                          out_shape=SDS((rows, cols), terms.dtype),
                          compiler_params=_cparams(("arbitrary",), 32))(terms)


def _adam_update(g, w, m, v):
    c1 = 1.0 / (1.0 - ADAM_B1 ** ADAM_STEP)
    c2 = 1.0 / (1.0 - ADAM_B2 ** ADAM_STEP)
    mn = ADAM_B1 * m + (1.0 - ADAM_B1) * g
    vn = ADAM_B2 * v + (1.0 - ADAM_B2) * (g * g)
    delta = (-ADAM_LR) * ((mn * c1) / (jnp.sqrt(vn * c2) + ADAM_EPS) + ADAM_WD * w)
    return delta, mn, vn


def _adamw_many(name, gs, ws, ms, vs):
    n = len(gs)

    def body(*refs):
        for p in range(n):
            g, w, m, v = (refs[q * n + p][...] for q in range(4))
            d, mn, vn = _adam_update(g, w, m, v)
            refs[4 * n + p][...] = d
            refs[5 * n + p][...] = mn
            refs[6 * n + p][...] = vn

    full = [pl.BlockSpec(memory_space=pltpu.VMEM)] * n
    shapes = [SDS(w.shape, F32) for w in ws]
    res = pl.pallas_call(body, name=name, in_specs=full * 4, out_specs=full * 3, out_shape=shapes * 3,
                         compiler_params=pltpu.CompilerParams(vmem_limit_bytes=32 * MiB))(*gs, *ws, *ms, *vs)
    return [(res[p], res[n + p], res[2 * n + p]) for p in range(n)]


def _adamw(name, terms, w, m, v):
    k, rows, cols = terms.shape
    tr = _row_tile(rows)

    def body(t_ref, w_ref, m_ref, v_ref, g_ref, d_ref, mo_ref, vo_ref):
        g = t_ref[0].astype(F32)
        for q in range(1, k):
            g = g + t_ref[q].astype(F32)
        g_ref[...] = g
        d_ref[...], mo_ref[...], vo_ref[...] = _adam_update(g, w_ref[...], m_ref[...], v_ref[...])

    blk = pl.BlockSpec((tr, cols), lambda i: (i, 0))
    return pl.pallas_call(body, name=name, grid=(rows // tr,),
                          in_specs=[pl.BlockSpec((k, tr, cols), lambda i: (0, i, 0)), blk, blk, blk],
                          out_specs=[blk] * 4, out_shape=[SDS((rows, cols), F32)] * 4,
                          compiler_params=_cparams(("arbitrary",), 40))(terms, w, m, v)


def kernel(x, norm_mix_g, w_in, conv_w, conv_b, w_rgate, b_rgate, w_igate, b_igate, lru_lambda, w_out_a, sgu_ln_g, sgu_ln_b, sgu_w_s, sgu_b_s, w_out_b, w_out, norm_mlp_g, w_up, w_down, norm_final_g, loss_target, m_norm_mix_g, m_w_in, m_conv_w, m_conv_b, m_w_rgate, m_b_rgate, m_w_igate, m_b_igate, m_lru_lambda, m_w_out_a, m_sgu_ln_g, m_sgu_ln_b, m_sgu_w_s, m_sgu_b_s, m_w_out_b, m_w_out, m_norm_mlp_g, m_w_up, m_w_down, m_norm_final_g, v_norm_mix_g, v_w_in, v_conv_w, v_conv_b, v_w_rgate, v_b_rgate, v_w_igate, v_b_igate, v_lru_lambda, v_w_out_a, v_sgu_ln_g, v_sgu_ln_b, v_sgu_w_s, v_sgu_b_s, v_w_out_b, v_w_out, v_norm_mlp_g, v_w_up, v_w_down, v_norm_final_g):
    cx, cy, cc = _place()
    me = 4 * cx + 2 * cy + cc
    core = jnp.reshape(cc, (1,)).astype(jnp.int32)
    xs = x[0]
    tgt = loss_target[0]
    s = xs.shape[0]

    gate_shard = jnp.stack([w_rgate[0], w_igate[0]]).astype(BF16).reshape(2 * HEADS * 32, HEAD_DIM)
    vec_shard = jnp.concatenate([conv_w[0], b_rgate[0], b_igate[0]], axis=1)
    vec_shard = jnp.pad(vec_shard, ((0, 4), (0, 256 - vec_shard.shape[1])))
    shards = [w_in[0].astype(BF16), w_out_a[0].astype(BF16), w_out_b[0].astype(BF16), w_out[0].astype(BF16),
              w_up[0].astype(BF16), w_down[0].astype(BF16), gate_shard, vec_shard]
    (z, n1_t, w_in_g), (gate_g, vec_g) = _in_proj(xs, norm_mix_g, shards[0], _slot_order(cx, cy, cc),
                                                comm=_gather_plan(shards[6:8]))
    gates = gate_g.reshape(N_SLOT, 2, HEADS, 32, HEAD_DIM).transpose(1, 2, 0, 3, 4).reshape(2, HEADS, HEAD_DIM, HEAD_DIM)
    w_r_f, w_i_f = gates[0], gates[1]
    conv_w_f = vec_g[:, 0:4, 0:128].transpose(1, 0, 2).reshape(CONV_K, D)
    b_r_f = vec_g[:, 0:4, 128:160].transpose(1, 0, 2).reshape(1, D)
    b_i_f = vec_g[:, 0:4, 160:192].transpose(1, 0, 2).reshape(1, D)
    b_s_t = jnp.transpose(sgu_b_s[0])

    (ya, hs), (w_oa_g, w_ob_g, w_out_g) = _branch_a_fwd(
        z, conv_w_f, conv_b, w_r_f, b_r_f, w_i_f, b_i_f, lru_lambda, comm=_gather_plan(shards[1:4]))
    w_oa_f = w_oa_g.reshape(D, D)
    w_ob_f = w_ob_g.reshape(D, D)
    w_out_f = w_out_g.reshape(D, D)
    (yb,), (w_up_g,) = _branch_b_fwd(z, sgu_ln_g, sgu_ln_b, sgu_w_s[0], b_s_t, comm=_gather_plan(shards[4:5]))
    (pa, pb, merged, h1), (w_down_g,) = _merge_out(ya, yb, z, xs, w_oa_f, w_ob_f, w_out_f,
                                                   comm=_gather_plan(shards[5:6]))
    w_down_f = w_down_g.reshape(N_SLOT * FF_COLS, D)
    gf2 = norm_final_g.reshape(1, D)
    r_act, act_t, n2_t, dh2, loss_acc, d_gfin = _mlp_fwd(h1, norm_mlp_g, w_up_g, w_down_f, gf2, tgt)

    def pair(names, grads, recv):
        return [_pair_sum("pair_sum_" + nm, g, r, core) for nm, g, r in zip(names, grads, recv)]

    g_down = _wgrad_t("wgrad_down", act_t, dh2, N_SLOT // 2, True, False, 2048)
    g_down = g_down.reshape(N_SLOT, FF_COLS, D)
    (df, dh1, d_gmlp), (r_down,) = _mlp_bwd(dh2, r_act, w_down_f, w_up_g, h1, norm_mlp_g,
                                            comm=_sibling_plan([g_down]))
    (p_down,) = pair(["down"], [g_down], [r_down])
    g_up = _wgrad_t("wgrad_up", n2_t, df, N_SLOT, False, True, 4096)
    g_out = _wgrad("wgrad_out", merged, dh1, 1, False, False).reshape(N_SLOT, D // N_SLOT, D)
    (dz, dpa, dpb, dya, dyb), (got_down, r_up, r_out) = _merge_bwd(
        dh1, z, pa, pb, w_out_f, w_oa_f, w_ob_f, comm=_join(_chips_plan([p_down]), _sibling_plan([g_up, g_out])))
    p_up, p_out = pair(["up", "out"], [g_up, g_out], [r_up, r_out])
    g_oa = _wgrad("wgrad_out_a", ya, dpa, 1, False, False).reshape(N_SLOT, D // N_SLOT, D)
    g_ob = _wgrad("wgrad_out_b", yb, dpb, 1, False, False).reshape(N_SLOT, D // N_SLOT, D)
    (dz, d_ws, d_bs, d_ln), (got_up, r_oa, r_ob) = _branch_b_bwd(
        dz, dyb, z, sgu_ln_g, sgu_ln_b, sgu_w_s[0], b_s_t,
        comm=_join(_chips_plan([p_up]), _sibling_plan([g_oa, g_ob])))
    p_oa, p_ob = pair(["out_a", "out_b"], [g_oa, g_ob], [r_oa, r_ob])
    (dz, d_vec, d_wr, d_wi), (got_out, got_oa, got_ob) = _branch_a_bwd(
        dz, dya, z, hs, conv_w_f, conv_b, w_r_f, b_r_f, w_i_f, b_i_f, lru_lambda,
        comm=_chips_plan([p_out, p_oa, p_ob]))
    g_in = _wgrad_t("wgrad_in", n1_t, dz, N_SLOT, False, True, 4096)
    g_gate = jnp.stack([d_wr, d_wi]).reshape(2, HEADS, N_SLOT, 32, HEAD_DIM).transpose(2, 0, 1, 3, 4)
    g_gate = g_gate.reshape(N_SLOT, 2 * HEADS * 32, HEAD_DIM).astype(BF16)

    d_bs_row = jnp.pad(d_bs[:, :, 0].reshape(1, GROUPS * CHUNK), ((0, 0), (0, D - GROUPS * CHUNK)))
    vecs = jnp.concatenate([d_vec, jnp.concatenate([d_ln[0:2], d_gmlp, d_gfin, d_bs_row, jnp.zeros((3, D), F32)])])
    d_ws2 = d_ws.reshape(GROUPS * CHUNK, CHUNK)
    r_in, r_gate, r_vecs, r_ws = _run_plan("rs_sibling_in", _sibling_plan([g_in, g_gate], [vecs, d_ws2]))
    p_in, p_gate = pair(["in", "gate"], [g_in, g_gate], [r_in, r_gate])
    vecs_chip = _add2("pair_sum_vecs", vecs, r_vecs)
    ws_chip = _add2("pair_sum_ws", d_ws2, r_ws)
    (dx, d_gmix), (got_in, got_gate, got_vecs, got_ws) = _in_bwd(
        dz, w_in_g, xs, dh1, norm_mix_g, comm=_chips_plan([p_in, p_gate], [vecs_chip, ws_chip]))
    vecs_sum = _sum_terms("sum_vecs", got_vecs)
    last = jnp.concatenate([d_gmix, jnp.pad(loss_acc[0:1], ((0, 0), (0, D - 128))), jnp.zeros((6, D), F32)])
    (last_all,) = _run_plan("exchange_last", _exchange_plan(last))
    last_sum = _sum_terms("sum_last", last_all)
    loss = last_sum[1, 0]
    got = [got_in, got_oa, got_ob, got_out, got_up, got_down, got_gate]

    def step(nm, terms, w, m, v, rows, cols):
        g, d, mn, vn = _adamw("adamw_" + nm, terms.reshape(4, rows, cols), w.reshape(rows, cols),
                              m.reshape(rows, cols), v.reshape(rows, cols))
        return [a.reshape(w.shape) for a in (g, d, mn, vn)]

    o_in = step("in", got[0], w_in, m_w_in, v_w_in, D, W_IN_COLS)
    o_oa = step("out_a", got[1], w_out_a, m_w_out_a, v_w_out_a, D // N_SLOT, D)
    o_ob = step("out_b", got[2], w_out_b, m_w_out_b, v_w_out_b, D // N_SLOT, D)
    o_out = step("out", got[3], w_out, m_w_out, v_w_out, D // N_SLOT, D)
    o_up = step("up", got[4], w_up, m_w_up, v_w_up, D, FF_COLS)
    o_down = step("down", got[5], w_down, m_w_down, v_w_down, FF_COLS, D)
    gate_w = jnp.stack([w_rgate[0], w_igate[0]]).reshape(2 * HEADS * 32, HEAD_DIM)
    gate_m = jnp.stack([m_w_rgate[0], m_w_igate[0]]).reshape(2 * HEADS * 32, HEAD_DIM)
    gate_v = jnp.stack([v_w_rgate[0], v_w_igate[0]]).reshape(2 * HEADS * 32, HEAD_DIM)
    o_gate = _adamw("adamw_gate", got[6], gate_w, gate_m, gate_v)
    o_gate = [a.reshape(2, 1, HEADS, 32, HEAD_DIM) for a in o_gate]
    o_wr = [a[0] for a in o_gate]
    o_wi = [a[1] for a in o_gate]

    def own(full, width):
        return lax.dynamic_slice_in_dim(full, me * width, width, axis=1)

    small_g = {
        "norm_mix_g": last_sum[0:1], "conv_w": own(vecs_sum[0:4], 128), "conv_b": vecs_sum[4:5],
        "b_rgate": own(vecs_sum[5:6].reshape(HEADS, HEAD_DIM), 32),
        "b_igate": own(vecs_sum[6:7].reshape(HEADS, HEAD_DIM), 32),
        "lru_lambda": vecs_sum[7:8], "sgu_ln_g": vecs_sum[8:9], "sgu_ln_b": vecs_sum[9:10],
        "norm_mlp_g": vecs_sum[10:11], "norm_final_g": vecs_sum[11:12],
        "sgu_b_s": vecs_sum[12, 0:GROUPS * CHUNK].reshape(GROUPS, CHUNK),
    }
    small_w = {"norm_mix_g": (norm_mix_g, m_norm_mix_g, v_norm_mix_g), "conv_w": (conv_w, m_conv_w, v_conv_w),
               "conv_b": (conv_b, m_conv_b, v_conv_b), "b_rgate": (b_rgate, m_b_rgate, v_b_rgate),
               "b_igate": (b_igate, m_b_igate, v_b_igate), "lru_lambda": (lru_lambda, m_lru_lambda, v_lru_lambda),
               "sgu_ln_g": (sgu_ln_g, m_sgu_ln_g, v_sgu_ln_g), "sgu_ln_b": (sgu_ln_b, m_sgu_ln_b, v_sgu_ln_b),
               "norm_mlp_g": (norm_mlp_g, m_norm_mlp_g, v_norm_mlp_g),
               "norm_final_g": (norm_final_g, m_norm_final_g, v_norm_final_g),
               "sgu_b_s": (sgu_b_s, m_sgu_b_s, v_sgu_b_s), "sgu_w_s": (sgu_w_s, m_sgu_w_s, v_sgu_w_s)}
    order = list(small_g)
    as2d = lambda k, a: a.reshape(small_g[k].shape)
    upd = _adamw_many("adamw_small", [small_g[k] for k in order], *[[as2d(k, small_w[k][q]) for k in order]
                                                                     for q in range(3)])
    o_small = {k: [a.reshape(small_w[k][0].shape) for a in (small_g[k],) + u] for k, u in zip(order, upd)}
    ws3 = [a[0].reshape(GROUPS * CHUNK, CHUNK) for a in small_w.pop("sgu_w_s")]
    o_small["sgu_w_s"] = [a.reshape(sgu_w_s.shape) for a in _adamw("adamw_ws", got_ws, *ws3)]

    per_weight = {"norm_mix_g": o_small["norm_mix_g"], "w_in": o_in, "conv_w": o_small["conv_w"],
                  "conv_b": o_small["conv_b"], "w_rgate": o_wr, "b_rgate": o_small["b_rgate"], "w_igate": o_wi,
                  "b_igate": o_small["b_igate"], "lru_lambda": o_small["lru_lambda"], "w_out_a": o_oa,
                  "sgu_ln_g": o_small["sgu_ln_g"], "sgu_ln_b": o_small["sgu_ln_b"], "sgu_w_s": o_small["sgu_w_s"],
                  "sgu_b_s": o_small["sgu_b_s"], "w_out_b": o_ob, "w_out": o_out, "norm_mlp_g": o_small["norm_mlp_g"],
                  "w_up": o_up, "w_down": o_down, "norm_final_g": o_small["norm_final_g"]}
    names_w = list(per_weight)
    return (loss, dx[None], *[per_weight[k][0] for k in names_w], *[per_weight[k][1] for k in names_w],
            *[per_weight[k][2] for k in names_w], *[per_weight[k][3] for k in names_w])
```

```python
import jax
import jax.numpy as jnp
from jax import lax
from jax.experimental import pallas as pl
from jax.experimental.pallas import tpu as pltpu

F32 = jnp.float32
BF16 = jnp.bfloat16
SDS = jax.ShapeDtypeStruct
MESH = pl.DeviceIdType.MESH
ANY = pl.BlockSpec(memory_space=pltpu.HBM)

D = 1024
N_SLOT = 8
W_IN_COLS = 768
FF_COLS = 512
HEADS, HEAD_DIM = 4, 256
GROUPS, GROUP_DIM = 4, 256
CHUNK = 128
CONV_K = 4
NORM_EPS = 1e-6
LN_EPS = 1e-5
LRU_C = 8.0
ADAM_LR, ADAM_B1, ADAM_B2, ADAM_EPS, ADAM_WD, ADAM_STEP = 0.001, 0.9, 0.999, 1e-08, 0.01, 10

TM_ROWS = 1024
TM_MERGE = 512
T_BRANCH_A = 256
T_BRANCH_B = 256
MiB = 1024 * 1024
SMALL_OPERAND = 16 * 1024

_GELU_C = 0.7978845608028654
_GELU_A = 0.044715


def _small_in_hbm(a):
    return pltpu.with_memory_space_constraint(a, pltpu.HBM) if a.size <= SMALL_OPERAND else a


def _cparams(sem, vmem_mib):
    return pltpu.CompilerParams(dimension_semantics=sem, vmem_limit_bytes=vmem_mib * MiB)


def _gelu(x):
    t = jnp.tanh(_GELU_C * (x + _GELU_A * x * x * x))
    return 0.5 * x * (1.0 + t)


def _gelu_and_grad(x):
    x2 = x * x
    t = jnp.tanh(_GELU_C * x * (1.0 + _GELU_A * x2))
    g = 0.5 * x * (1.0 + t)
    dg = 0.5 * (1.0 + t) + 0.5 * x * (1.0 - t * t) * _GELU_C * (1.0 + 3.0 * _GELU_A * x2)
    return g, dg


def _softplus(x):
    return jnp.maximum(x, 0.0) + jnp.log1p(jnp.exp(-jnp.abs(x)))


def _dot(a, b):
    return jnp.dot(a, b, preferred_element_type=F32)


def _dot_nt(a, b):
    return lax.dot_general(a, b, (((1,), (1,)), ((), ())), preferred_element_type=F32)


def _dot_tn(a, b):
    return lax.dot_general(a, b, (((0,), (0,)), ((), ())), preferred_element_type=F32)


def _rows_shifted(prev8, cur, k):
    ext = jnp.concatenate([prev8, cur], axis=0)
    return pltpu.roll(ext, k, 0)[8:]


def _rows_advanced(cur, next8, k):
    t = cur.shape[0]
    ext = jnp.concatenate([cur, next8], axis=0)
    return pltpu.roll(ext, t + 8 - k, 0)[:t]


def _slot_order(x, y, c):
    chip = 2 * x + y
    order = [2 * chip + c, 2 * chip + 1 - c]
    for _, _, pc in _other_chips(x, y):
        order += [2 * pc + c, 2 * pc + 1 - c]
    return jnp.stack(order).astype(jnp.int32)


def _in_proj(x, g_mix, w_in_own, order, comm=None):
    s = x.shape[0]
    tm = min(TM_ROWS, s)
    ni = s // tm

    def body(order_ref, x_ref, g_ref, own_ref, z_ref, nt_ref, wg_ref, n_s, w_s, send_sems, recv_sems, local_sems):
        j, i = pl.program_id(0), pl.program_id(1)
        px, py, c = _place()
        chip = 2 * px + py
        me = 2 * chip + c
        sib = (px, py, 1 - c)
        chips = _other_chips(px, py)

        def rc(k, src, blk, to):
            return pltpu.make_async_remote_copy(src_ref=src, dst_ref=w_s.at[blk], send_sem=send_sems.at[k],
                                                recv_sem=recv_sems.at[k], device_id=to, device_id_type=MESH)

        own_in = pltpu.make_async_copy(own_ref, w_s.at[me], local_sems.at[0])
        sends = [rc(0, own_ref, me, sib)] + [rc(1 + q, own_ref, me, (qx, qy, c)) for q, (qx, qy, _) in enumerate(chips)]
        passed = [rc(4 + q, w_s.at[2 * pc + c], 2 * pc + c, sib) for q, (_, _, pc) in enumerate(chips)]
        keep = pltpu.make_async_copy(w_s, wg_ref, local_sems.at[1])

        @pl.when((i == 0) & (j == 0))
        def _():
            own_in.start()
            for cp in sends:
                cp.start()
            own_in.wait()

        @pl.when((i == 0) & (j == 1))
        def _():
            rc(0, own_ref, 2 * chip + 1 - c, sib).wait_recv()

        for q, (_, _, pc) in enumerate(chips):
            @pl.when((i == 0) & (j == 2 + 2 * q))
            def _():
                rc(1 + q, own_ref, 2 * pc + c, sib).wait_recv()
                passed[q].start()

            @pl.when((i == 0) & (j == 3 + 2 * q))
            def _():
                rc(4 + q, own_ref, 2 * pc + 1 - c, sib).wait_recv()

        rows = pl.ds(pl.multiple_of(i * tm, tm), tm)

        @pl.when(j == 0)
        def _():
            xv = x_ref[...]
            rstd = lax.rsqrt(jnp.mean(xv * xv, axis=-1, keepdims=True) + NORM_EPS)
            nb = (xv * rstd * g_ref[...]).astype(BF16)
            n_s[rows, :] = nb
            nt_ref[...] = nb.T

        z_ref[...] = _dot(n_s[rows, :], w_s[order_ref[j]]).astype(BF16)

        @pl.when((i == 0) & (j == N_SLOT - 1))
        def _():
            keep.start()

        @pl.when((i == ni - 1) & (j == N_SLOT - 1))
        def _():
            for cp in sends + passed:
                cp.wait_send()
            keep.wait()

    first_pass = lambda j, i, o: (jnp.where(j == 0, i, ni - 1), 0)
    (z, n1, w_in_g), extra = _call(
        body, name="in_proj", grid=(N_SLOT, ni), prefetch=(order,),
        in_specs=[pl.BlockSpec((tm, D), first_pass),
                  pl.BlockSpec((1, D), lambda j, i, o: (0, 0)), ANY],
        out_specs=[pl.BlockSpec((tm, W_IN_COLS), lambda j, i, o: (i, o[j])),
                   pl.BlockSpec((D, tm), lambda j, i, o: (0, jnp.where(j == 0, i, ni - 1))), ANY],
        out_shape=[SDS((s, N_SLOT * W_IN_COLS), BF16), SDS((D, s), BF16), SDS((N_SLOT, D, W_IN_COLS), BF16)],
        scratch_shapes=[pltpu.VMEM((s, D), BF16), pltpu.VMEM((N_SLOT, D, W_IN_COLS), BF16),
                        pltpu.SemaphoreType.DMA((7,)), pltpu.SemaphoreType.DMA((7,)), pltpu.SemaphoreType.DMA((2,))],
        params=_cparams(("arbitrary", "arbitrary"), 56), args=(x, g_mix, w_in_own), comm=comm)
    return (z, n1, w_in_g), extra


def _lru_gates(xc, xcb, wr_ref, br, wi_ref, bi, sp_lam, a_s, b_s, r_s=None, i_s=None, m_s=None):
    for h in range(HEADS):
        sl = slice(h * HEAD_DIM, (h + 1) * HEAD_DIM)
        r = jax.nn.sigmoid(_dot(xcb[:, sl], wr_ref[h]) + br[:, sl])
        ig = jax.nn.sigmoid(_dot(xcb[:, sl], wi_ref[h]) + bi[:, sl])
        log_a = (-LRU_C) * r * sp_lam[:, sl]
        a = jnp.exp(log_a)
        mult = jnp.sqrt(-jnp.tanh(log_a) * (a * a + 1.0))
        a_s[:, sl] = a
        b_s[:, sl] = xc[:, sl] * ig * mult
        if r_s is not None:
            r_s[:, sl] = r
            i_s[:, sl] = ig
            m_s[:, sl] = mult


def _conv_fwd(xa, prev8, cw, cb):
    xc = cb + cw[0:1, :] * xa
    for k in range(1, CONV_K):
        xc = xc + cw[k:k + 1, :] * _rows_shifted(prev8, xa, k)
    return xc


def _branch_a_fwd(z, conv_w, conv_b, w_r, b_r, w_i, b_i, lam, comm=None):
    s = z.shape[0]
    ta = min(T_BRANCH_A, s)
    per16 = ta // 16

    def body(xa_ref, xp_ref, ga_ref, cw_ref, cb_ref, wr_ref, br_ref, wi_ref, bi_ref, lam_ref,
             ya_ref, hs_ref, a_s, b_s, h_s, carry_s):
        i = pl.program_id(0)

        @pl.when(i == 0)
        def _():
            carry_s[...] = jnp.zeros_like(carry_s)

        xa = xa_ref[...].astype(F32)
        prev8 = jnp.where(i > 0, xp_ref[...].astype(F32)[8:16], 0.0)
        xc = _conv_fwd(xa, prev8, cw_ref[...], cb_ref[...])
        sp_lam = _softplus(-lam_ref[...])
        _lru_gates(xc, xc.astype(BF16), wr_ref, br_ref[...], wi_ref, bi_ref[...], sp_lam, a_s, b_s)

        row = lax.broadcasted_iota(jnp.int32, (8, D), 0)

        def group(g, carry):
            off = pl.multiple_of(g * 8, 8)
            a8 = a_s[pl.ds(off, 8), :]
            b8 = b_s[pl.ds(off, 8), :]
            for d in (1, 2, 4):
                a_sh = jnp.where(row >= d, pltpu.roll(a8, d, 0), 1.0)
                b_sh = jnp.where(row >= d, pltpu.roll(b8, d, 0), 0.0)
                b8 = a8 * b_sh + b8
                a8 = a8 * a_sh
            h8 = b8 + a8 * carry
            h_s[pl.ds(off, 8), :] = h8
            return jnp.broadcast_to(h8[7:8, :], (8, D))

        carry_s[...] = lax.fori_loop(0, ta // 8, group, carry_s[...])
        hs = h_s[...]
        hs_ref[...] = hs.astype(BF16)
        ya_ref[...] = (hs * _gelu(ga_ref[...].astype(F32))).astype(BF16)

    vec = pl.BlockSpec((1, D), lambda i: (0, 0))
    gate = pl.BlockSpec((HEADS, HEAD_DIM, HEAD_DIM), lambda i: (0, 0, 0))
    return _call(
        body, name="branch_a_fwd", grid=(s // ta,),
        in_specs=[pl.BlockSpec((ta, D), lambda i: (i, 0)),
                  pl.BlockSpec((16, D), lambda i: (jnp.maximum(i * per16 - 1, 0), 0)),
                  pl.BlockSpec((ta, D), lambda i: (i, 1)),
                  pl.BlockSpec((CONV_K, D), lambda i: (0, 0)), vec, gate, vec, gate, vec, vec],
        out_specs=[pl.BlockSpec((ta, D), lambda i: (i, 0)), pl.BlockSpec((ta, D), lambda i: (i, 0))],
        out_shape=[SDS((s, D), BF16), SDS((s, D), BF16)],
        scratch_shapes=[pltpu.VMEM((ta, D), F32), pltpu.VMEM((ta, D), F32), pltpu.VMEM((ta, D), F32),
                        pltpu.VMEM((8, D), F32)],
        params=_cparams(("arbitrary",), 40), args=(z, z, z, conv_w, conv_b, w_r, b_r, w_i, b_i, lam), comm=comm)


def _sgu_common(ub, vb, lg, lb, with_grad):
    if with_grad:
        u, du = _gelu_and_grad(ub)
        v, dv = _gelu_and_grad(vb)
    else:
        u, v, du, dv = _gelu(ub), _gelu(vb), None, None
    mu = jnp.mean(v, axis=-1, keepdims=True)
    vc = v - mu
    rstd = lax.rsqrt(jnp.mean(vc * vc, axis=-1, keepdims=True) + LN_EPS)
    vhat = vc * rstd
    vln = vhat * lg + lb
    return u, du, dv, rstd, vhat, vln


def _masked_ws(ws_ref):
    t = lax.broadcasted_iota(jnp.int32, (CHUNK, CHUNK), 0)
    c = lax.broadcasted_iota(jnp.int32, (CHUNK, CHUNK), 1)
    keep = c <= t
    return [jnp.where(keep, ws_ref[g], 0.0).astype(BF16) for g in range(GROUPS)]


def _branch_b_fwd(z, ln_g, ln_b, w_s, b_s_t, comm=None):
    s = z.shape[0]
    tb = min(T_BRANCH_B, s)

    def body(ub_ref, vb_ref, lg_ref, lb_ref, ws_ref, bs_ref, yb_ref):
        u, _, _, _, _, vln = _sgu_common(ub_ref[...].astype(F32), vb_ref[...].astype(F32),
                                         lg_ref[...], lb_ref[...], False)
        vlnb = vln.astype(BF16)
        wm = _masked_ws(ws_ref)
        bs = bs_ref[...]
        for c in range(tb // CHUNK):
            rs = slice(c * CHUNK, (c + 1) * CHUNK)
            for g in range(GROUPS):
                cs = slice(g * GROUP_DIM, (g + 1) * GROUP_DIM)
                sp = _dot(wm[g], vlnb[rs, cs]) + bs[:, g:g + 1]
                yb_ref[rs, cs] = (u[rs, cs] * sp).astype(BF16)

    vec = pl.BlockSpec((1, D), lambda i: (0, 0))
    return _call(
        body, name="branch_b_fwd", grid=(s // tb,),
        in_specs=[pl.BlockSpec((tb, D), lambda i: (i, 2)), pl.BlockSpec((tb, D), lambda i: (i, 3)), vec, vec,
                  pl.BlockSpec((GROUPS, CHUNK, CHUNK), lambda i: (0, 0, 0)),
                  pl.BlockSpec((CHUNK, GROUPS), lambda i: (0, 0))],
        out_specs=[pl.BlockSpec((tb, D), lambda i: (i, 0))],
        out_shape=[SDS((s, D), BF16)], scratch_shapes=[],
        params=_cparams(("arbitrary",), 40), args=(z, z, ln_g, ln_b, w_s, b_s_t), comm=comm)


def _merge_out(ya, yb, z, x, w_oa, w_ob, w_out, comm=None):
    s = x.shape[0]
    tm = min(TM_MERGE, s)

    def body(ya_ref, yb_ref, ma_ref, mb_ref, x_ref, woa_ref, wob_ref, wo_ref, pa_ref, pb_ref, mg_ref, h1_ref):
        pa = _dot(ya_ref[...], woa_ref[...])
        pb = _dot(yb_ref[...], wob_ref[...])
        merged = (jax.nn.sigmoid(ma_ref[...].astype(F32)) * pa
                  + jax.nn.sigmoid(mb_ref[...].astype(F32)) * pb).astype(BF16)
        pa_ref[...] = pa.astype(BF16)
        pb_ref[...] = pb.astype(BF16)
        mg_ref[...] = merged
        h1_ref[...] = x_ref[...] + _dot(merged, wo_ref[...])

    row = pl.BlockSpec((tm, D), lambda i: (i, 0))
    wsp = pl.BlockSpec((D, D), lambda i: (0, 0))
    return _call(
        body, name="merge_out", grid=(s // tm,),
        in_specs=[row, row, pl.BlockSpec((tm, D), lambda i: (i, 4)), pl.BlockSpec((tm, D), lambda i: (i, 5)),
                  row, wsp, wsp, wsp],
        out_specs=[row, row, row, row],
        out_shape=[SDS((s, D), BF16), SDS((s, D), BF16), SDS((s, D), BF16), SDS((s, D), F32)], scratch_shapes=[],
        params=_cparams(("arbitrary",), 48), args=(ya, yb, z, z, x, w_oa, w_ob, w_out), comm=comm)


def _mlp_fwd(h1, g_mlp, w_up_g, w_down, g_fin, tgt):
    s = h1.shape[0]
    tm = min(TM_ROWS, s)
    nj = N_SLOT

    def body(h1_ref, gm_ref, wu_ref, wd_ref, gf_ref, t_ref, r_ref, at_ref, n2t_ref, dh2_ref, loss_ref, dgf_ref,
             n2_s, acc_s):
        i, j = pl.program_id(0), pl.program_id(1)

        @pl.when(j == 0)
        def _():
            hv = h1_ref[...]
            rstd = lax.rsqrt(jnp.mean(hv * hv, axis=-1, keepdims=True) + NORM_EPS)
            nb = (hv * rstd * gm_ref[...]).astype(BF16)
            n2_s[...] = nb
            n2t_ref[...] = nb.T
            acc_s[...] = jnp.zeros_like(acc_s)

        @pl.when((i == 0) & (j == 0))
        def _():
            loss_ref[...] = jnp.zeros_like(loss_ref)
            dgf_ref[...] = jnp.zeros_like(dgf_ref)

        r = jnp.maximum(_dot(n2_s[...], wu_ref[...]), 0.0)
        r_ref[...] = r.astype(BF16)
        act = (r * r).astype(BF16)
        at_ref[...] = act.T
        acc_s[...] += _dot(act, wd_ref[...])

        @pl.when(j == nj - 1)
        def _():
            h2 = h1_ref[...] + acc_s[...]
            rstd = lax.rsqrt(jnp.mean(h2 * h2, axis=-1, keepdims=True) + NORM_EPS)
            hh = h2 * rstd
            gf = gf_ref[...]
            e = hh * gf - t_ref[...]
            loss_ref[...] += jnp.sum(e * e) * (0.5 / D)
            dy = e * (1.0 / D)
            dgf_ref[...] += jnp.sum(dy * hh, axis=0, keepdims=True)
            dhh = dy * gf
            dh2_ref[...] = rstd * (dhh - hh * jnp.mean(dhh * hh, axis=-1, keepdims=True))

    row = pl.BlockSpec((tm, D), lambda i, j: (i, 0))
    vec = pl.BlockSpec((1, D), lambda i, j: (0, 0))
    return pl.pallas_call(
        body, name="mlp_fwd", grid=(s // tm, nj),
        in_specs=[row, vec, pl.BlockSpec((None, D, FF_COLS), lambda i, j: (j, 0, 0)),
                  pl.BlockSpec((FF_COLS, D), lambda i, j: (j, 0)), vec, row],
        out_specs=[pl.BlockSpec((tm, FF_COLS), lambda i, j: (i, j)), pl.BlockSpec((FF_COLS, tm), lambda i, j: (j, i)),
                   pl.BlockSpec((D, tm), lambda i, j: (0, i)), row, pl.BlockSpec((8, 128), lambda i, j: (0, 0)), vec],
        out_shape=[SDS((s, nj * FF_COLS), BF16), SDS((nj * FF_COLS, s), BF16), SDS((D, s), BF16), SDS((s, D), F32),
                   SDS((8, 128), F32), SDS((1, D), F32)],
        scratch_shapes=[pltpu.VMEM((tm, D), BF16), pltpu.VMEM((tm, D), F32)],
        compiler_params=_cparams(("arbitrary", "arbitrary"), 52),
    )(h1, _small_in_hbm(g_mlp), w_up_g, w_down, _small_in_hbm(g_fin), tgt)


def _mlp_bwd(dh2, r, w_down, w_up_g, h1, g_mlp, comm=None):
    s = h1.shape[0]
    tm = min(TM_ROWS, s)
    nj = N_SLOT

    def body(dh2_ref, r_ref, wd_ref, wu_ref, h1_ref, gm_ref, df_ref, dh1_ref, dgm_ref, dh2b_s, acc_s):
        i, j = pl.program_id(0), pl.program_id(1)

        @pl.when(j == 0)
        def _():
            dh2b_s[...] = dh2_ref[...].astype(BF16)
            acc_s[...] = jnp.zeros_like(acc_s)

        @pl.when((i == 0) & (j == 0))
        def _():
            dgm_ref[...] = jnp.zeros_like(dgm_ref)

        d_act = _dot_nt(dh2b_s[...], wd_ref[...])
        df = (d_act * (2.0 * r_ref[...].astype(F32))).astype(BF16)
        df_ref[...] = df
        acc_s[...] += _dot_nt(df, wu_ref[...])

        @pl.when(j == nj - 1)
        def _():
            hv = h1_ref[...]
            rstd = lax.rsqrt(jnp.mean(hv * hv, axis=-1, keepdims=True) + NORM_EPS)
            hh = hv * rstd
            dn2 = acc_s[...]
            dgm_ref[...] += jnp.sum(dn2 * hh, axis=0, keepdims=True)
            dhat = dn2 * gm_ref[...]
            dh1_ref[...] = dh2_ref[...] + rstd * (dhat - hh * jnp.mean(dhat * hh, axis=-1, keepdims=True))

    row = pl.BlockSpec((tm, D), lambda i, j: (i, 0))
    vec = pl.BlockSpec((1, D), lambda i, j: (0, 0))
    ffb = pl.BlockSpec((tm, FF_COLS), lambda i, j: (i, j))
    return _call(
        body, name="mlp_bwd", grid=(s // tm, nj),
        in_specs=[row, ffb, pl.BlockSpec((FF_COLS, D), lambda i, j: (j, 0)),
                  pl.BlockSpec((None, D, FF_COLS), lambda i, j: (j, 0, 0)), row, vec],
        out_specs=[ffb, row, vec],
        out_shape=[SDS((s, nj * FF_COLS), BF16), SDS((s, D), F32), SDS((1, D), F32)],
        scratch_shapes=[pltpu.VMEM((tm, D), BF16), pltpu.VMEM((tm, D), F32)],
        params=_cparams(("arbitrary", "arbitrary"), 52), args=(dh2, r, w_down, w_up_g, h1, g_mlp), comm=comm)


def _merge_bwd(dh1, z, pa, pb, w_out, w_oa, w_ob, comm=None):
    s = dh1.shape[0]
    tm = min(TM_MERGE, s)

    def body(dh1_ref, ma_ref, mb_ref, pa_ref, pb_ref, wo_ref, woa_ref, wob_ref,
             dz_ref, dpa_ref, dpb_ref, dya_ref, dyb_ref):
        dm = _dot_nt(dh1_ref[...].astype(BF16), wo_ref[...])
        sa = jax.nn.sigmoid(ma_ref[...].astype(F32))
        sb = jax.nn.sigmoid(mb_ref[...].astype(F32))
        dpa = (dm * sa).astype(BF16)
        dpb = (dm * sb).astype(BF16)
        dz_ref[:, 0:D] = (dm * pa_ref[...].astype(F32) * sa * (1.0 - sa)).astype(BF16)
        dz_ref[:, D:2 * D] = (dm * pb_ref[...].astype(F32) * sb * (1.0 - sb)).astype(BF16)
        dpa_ref[...] = dpa
        dpb_ref[...] = dpb
        dya_ref[...] = _dot_nt(dpa, woa_ref[...]).astype(BF16)
        dyb_ref[...] = _dot_nt(dpb, wob_ref[...]).astype(BF16)

    row = pl.BlockSpec((tm, D), lambda i: (i, 0))
    wsp = pl.BlockSpec((D, D), lambda i: (0, 0))
    return _call(
        body, name="merge_bwd", grid=(s // tm,),
        in_specs=[row, pl.BlockSpec((tm, D), lambda i: (i, 4)), pl.BlockSpec((tm, D), lambda i: (i, 5)),
                  row, row, wsp, wsp, wsp],
        out_specs=[pl.BlockSpec((tm, 2 * D), lambda i: (i, 2)), row, row, row, row],
        out_shape=[SDS((s, 6 * D), BF16)] + [SDS((s, D), BF16)] * 4, scratch_shapes=[],
        params=_cparams(("arbitrary",), 48), args=(dh1, z, z, pa, pb, w_out, w_oa, w_ob), comm=comm)


def _branch_b_bwd(dz, dyb, z, ln_g, ln_b, w_s, b_s_t, comm=None):
    s = z.shape[0]
    tb = min(T_BRANCH_B, s)

    def body(dz_in, dyb_ref, ub_ref, vb_ref, lg_ref, lb_ref, ws_ref, bs_ref,
             dz_ref, dws_ref, dbs_ref, dln_ref, du_s, dvln_s):
        del dz_in

        @pl.when(pl.program_id(0) == 0)
        def _():
            dws_ref[...] = jnp.zeros_like(dws_ref)
            dbs_ref[...] = jnp.zeros_like(dbs_ref)
            dln_ref[...] = jnp.zeros_like(dln_ref)

        lg = lg_ref[...]
        u, du, dv, rstd, vhat, vln = _sgu_common(ub_ref[...].astype(F32), vb_ref[...].astype(F32),
                                                 lg, lb_ref[...], True)
        vlnb = vln.astype(BF16)
        dyb_v = dyb_ref[...].astype(F32)
        wm = _masked_ws(ws_ref)
        keep = (lax.broadcasted_iota(jnp.int32, (CHUNK, CHUNK), 1)
                <= lax.broadcasted_iota(jnp.int32, (CHUNK, CHUNK), 0))
        bs = bs_ref[...]
        for c in range(tb // CHUNK):
            rs = slice(c * CHUNK, (c + 1) * CHUNK)
            for g in range(GROUPS):
                cs = slice(g * GROUP_DIM, (g + 1) * GROUP_DIM)
                v_blk = vlnb[rs, cs]
                sp = _dot(wm[g], v_blk) + bs[:, g:g + 1]
                d_sp = dyb_v[rs, cs] * u[rs, cs]
                d_spb = d_sp.astype(BF16)
                du_s[rs, cs] = dyb_v[rs, cs] * sp
                dvln_s[rs, cs] = _dot_tn(wm[g], d_spb)
                dws_ref[g] += jnp.where(keep, _dot_nt(d_spb, v_blk), 0.0)
                dbs_ref[g] += jnp.broadcast_to(jnp.sum(d_sp, axis=-1, keepdims=True), (CHUNK, CHUNK))
        dvln = dvln_s[...]
        dln_ref[0:1, :] += jnp.sum(dvln * vhat, axis=0, keepdims=True)
        dln_ref[1:2, :] += jnp.sum(dvln, axis=0, keepdims=True)
        dvh = dvln * lg
        d_v = rstd * (dvh - jnp.mean(dvh, axis=-1, keepdims=True)
                      - vhat * jnp.mean(dvh * vhat, axis=-1, keepdims=True))
        dz_ref[:, 0:D] = (du_s[...] * du).astype(BF16)
        dz_ref[:, D:2 * D] = (d_v * dv).astype(BF16)

    vec = pl.BlockSpec((1, D), lambda i: (0, 0))
    sq = pl.BlockSpec((GROUPS, CHUNK, CHUNK), lambda i: (0, 0, 0))
    return _call(
        body, name="branch_b_bwd", grid=(s // tb,),
        in_specs=[ANY, pl.BlockSpec((tb, D), lambda i: (i, 0)),
                  pl.BlockSpec((tb, D), lambda i: (i, 2)), pl.BlockSpec((tb, D), lambda i: (i, 3)), vec, vec, sq,
                  pl.BlockSpec((CHUNK, GROUPS), lambda i: (0, 0))],
        out_specs=[pl.BlockSpec((tb, 2 * D), lambda i: (i, 1)), sq, sq, pl.BlockSpec((8, D), lambda i: (0, 0))],
        out_shape=[SDS(dz.shape, BF16), SDS((GROUPS, CHUNK, CHUNK), F32), SDS((GROUPS, CHUNK, CHUNK), F32),
                   SDS((8, D), F32)],
        scratch_shapes=[pltpu.VMEM((tb, D), F32), pltpu.VMEM((tb, D), F32)], aliases={0: 0},
        params=_cparams(("arbitrary",), 40), args=(dz, dyb, z, z, ln_g, ln_b, w_s, b_s_t), comm=comm)


def _branch_a_bwd(dz, dya, z, hs, conv_w, conv_b, w_r, b_r, w_i, b_i, lam, comm=None):
    s = z.shape[0]
    ta = min(T_BRANCH_A, s)
    nb = s // ta
    per16 = ta // 16

    def body(dz_in, dya_ref, xa_ref, xp_ref, ga_ref, hs_ref, hp_ref, cw_ref, cb_ref, wr_ref, br_ref, wi_ref,
             bi_ref, lam_ref, dz_ref, vec_ref, dwr_ref, dwi_ref,
             a_s, b_s, h_s, r_s, i_s, m_s, dcar_s, acar_s, dxc_s):
        del dz_in
        i = pl.program_id(0)
        blk = nb - 1 - i

        @pl.when(i == 0)
        def _():
            dcar_s[...] = jnp.zeros_like(dcar_s)
            acar_s[...] = jnp.zeros_like(acar_s)
            dxc_s[...] = jnp.zeros_like(dxc_s)
            vec_ref[...] = jnp.zeros_like(vec_ref)
            dwr_ref[...] = jnp.zeros_like(dwr_ref)
            dwi_ref[...] = jnp.zeros_like(dwi_ref)

        cw = cw_ref[...]
        lam_v = lam_ref[...]
        xa = xa_ref[...].astype(F32)
        prev8 = jnp.where(blk > 0, xp_ref[...].astype(F32)[8:16], 0.0)
        xc = _conv_fwd(xa, prev8, cw, cb_ref[...])
        xcb = xc.astype(BF16)
        sp_lam = _softplus(-lam_v)
        _lru_gates(xc, xcb, wr_ref, br_ref[...], wi_ref, bi_ref[...], sp_lam, a_s, b_s, r_s, i_s, m_s)

        hs_v = hs_ref[...].astype(F32)
        hprev8 = jnp.where(blk > 0, hp_ref[...].astype(F32)[8:16], 0.0)
        h_m1 = _rows_shifted(hprev8, hs_v, 1)
        gg, dgg = _gelu_and_grad(ga_ref[...].astype(F32))
        dya_v = dya_ref[...].astype(F32)
        dz_ref[:, D:2 * D] = (dya_v * hs_v * dgg).astype(BF16)

        a_v = a_s[...]
        a_s[...] = _rows_advanced(a_v, acar_s[...], 1)
        b_s[...] = dya_v * gg

        row = lax.broadcasted_iota(jnp.int32, (8, D), 0)
        ng = ta // 8

        def group(gi, carry):
            off = pl.multiple_of((ng - 1 - gi) * 8, 8)
            c8 = a_s[pl.ds(off, 8), :]
            d8 = b_s[pl.ds(off, 8), :]
            for d in (1, 2, 4):
                c_sh = jnp.where(row < 8 - d, pltpu.roll(c8, 8 - d, 0), 1.0)
                d_sh = jnp.where(row < 8 - d, pltpu.roll(d8, 8 - d, 0), 0.0)
                d8 = c8 * d_sh + d8
                c8 = c8 * c_sh
            dh8 = d8 + c8 * carry
            h_s[pl.ds(off, 8), :] = dh8
            return jnp.broadcast_to(dh8[0:1, :], (8, D))

        dcar_s[...] = lax.fori_loop(0, ng, group, dcar_s[...])
        acar_s[...] = jnp.broadcast_to(a_v[0:1, :], (8, D))

        dbx = h_s[...]
        r_v, i_v, m_v = r_s[...], i_s[...], m_s[...]
        d_mult = dbx * xc * i_v
        d_loga = dbx * h_m1 * a_v - d_mult * (a_v * a_v) / m_v
        d_pr = d_loga * ((-LRU_C) * sp_lam) * r_v * (1.0 - r_v)
        d_pi = dbx * xc * m_v * i_v * (1.0 - i_v)
        vec_ref[7:8, :] += jnp.sum(d_loga * r_v, axis=0, keepdims=True) * (LRU_C * jax.nn.sigmoid(-lam_v))
        vec_ref[5:6, :] += jnp.sum(d_pr, axis=0, keepdims=True)
        vec_ref[6:7, :] += jnp.sum(d_pi, axis=0, keepdims=True)
        d_prb = d_pr.astype(BF16)
        d_pib = d_pi.astype(BF16)
        h_s[...] = dbx * i_v * m_v
        for h in range(HEADS):
            sl = slice(h * HEAD_DIM, (h + 1) * HEAD_DIM)
            h_s[:, sl] += _dot_nt(d_prb[:, sl], wr_ref[h]) + _dot_nt(d_pib[:, sl], wi_ref[h])
            dwr_ref[h] += _dot_tn(xcb[:, sl], d_prb[:, sl])
            dwi_ref[h] += _dot_tn(xcb[:, sl], d_pib[:, sl])
        d_xc = h_s[...]
        vec_ref[4:5, :] += jnp.sum(d_xc, axis=0, keepdims=True)
        vec_ref[0:1, :] += jnp.sum(d_xc * xa, axis=0, keepdims=True)
        d_xa = cw[0:1, :] * d_xc
        nxt = dxc_s[...]
        for k in range(1, CONV_K):
            vec_ref[k:k + 1, :] += jnp.sum(d_xc * _rows_shifted(prev8, xa, k), axis=0, keepdims=True)
            d_xa = d_xa + cw[k:k + 1, :] * _rows_advanced(d_xc, nxt, k)
        dz_ref[:, 0:D] = d_xa.astype(BF16)
        dxc_s[...] = d_xc[0:8, :]

    vec = pl.BlockSpec((1, D), lambda i: (0, 0))
    gate = pl.BlockSpec((HEADS, HEAD_DIM, HEAD_DIM), lambda i: (0, 0, 0))
    cur = lambda c: pl.BlockSpec((ta, D), lambda i: (nb - 1 - i, c))
    before = lambda c: pl.BlockSpec((16, D), lambda i: (jnp.maximum((nb - 1 - i) * per16 - 1, 0), c))
    return _call(
        body, name="branch_a_bwd", grid=(nb,),
        in_specs=[ANY, cur(0), cur(0), before(0), cur(1), cur(0), before(0),
                  pl.BlockSpec((CONV_K, D), lambda i: (0, 0)), vec, gate, vec, gate, vec, vec],
        out_specs=[pl.BlockSpec((ta, 2 * D), lambda i: (nb - 1 - i, 0)), pl.BlockSpec((8, D), lambda i: (0, 0)),
                   gate, gate],
        out_shape=[SDS(dz.shape, BF16), SDS((8, D), F32), SDS((HEADS, HEAD_DIM, HEAD_DIM), F32),
                   SDS((HEADS, HEAD_DIM, HEAD_DIM), F32)],
        scratch_shapes=[pltpu.VMEM((ta, D), F32)] * 6 + [pltpu.VMEM((8, D), F32)] * 3, aliases={0: 0},
        params=_cparams(("arbitrary",), 48),
        args=(dz, dya, z, z, z, hs, hs, conv_w, conv_b, w_r, b_r, w_i, b_i, lam), comm=comm)


def _in_bwd(dz, w_in_g, x, dh1, g_mix, comm=None):
    s = x.shape[0]
    tm = min(TM_ROWS, s)
    nj = N_SLOT

    def body(dz_ref, w_ref, x_ref, dh1_ref, g_ref, dx_ref, dg_ref, acc_s):
        i, j = pl.program_id(0), pl.program_id(1)

        @pl.when(j == 0)
        def _():
            acc_s[...] = jnp.zeros_like(acc_s)

        @pl.when((i == 0) & (j == 0))
        def _():
            dg_ref[...] = jnp.zeros_like(dg_ref)

        acc_s[...] += _dot_nt(dz_ref[...], w_ref[...])

        @pl.when(j == nj - 1)
        def _():
            xv = x_ref[...]
            rstd = lax.rsqrt(jnp.mean(xv * xv, axis=-1, keepdims=True) + NORM_EPS)
            xh = xv * rstd
            dn = acc_s[...]
            dg_ref[...] += jnp.sum(dn * xh, axis=0, keepdims=True)
            dhat = dn * g_ref[...]
            dx_ref[...] = dh1_ref[...] + rstd * (dhat - xh * jnp.mean(dhat * xh, axis=-1, keepdims=True))

    row = pl.BlockSpec((tm, D), lambda i, j: (i, 0))
    vec = pl.BlockSpec((1, D), lambda i, j: (0, 0))
    return _call(
        body, name="in_bwd", grid=(s // tm, nj),
        in_specs=[pl.BlockSpec((tm, W_IN_COLS), lambda i, j: (i, j)),
                  pl.BlockSpec((None, D, W_IN_COLS), lambda i, j: (j, 0, 0)), row, row, vec],
        out_specs=[row, vec],
        out_shape=[SDS((s, D), F32), SDS((1, D), F32)],
        scratch_shapes=[pltpu.VMEM((tm, D), F32)],
        params=_cparams(("arbitrary", "arbitrary"), 48), args=(dz, w_in_g, x, dh1, g_mix), comm=comm)


def _wgrad(name, a, b, nblk, a_split, b_split):
    s = a.shape[0]
    ts = min(TM_ROWS, s)
    a_w = a.shape[1] // nblk if a_split else a.shape[1]
    b_w = b.shape[1] // nblk if b_split else b.shape[1]

    def body(a_ref, b_ref, o_ref, acc_s):
        t = pl.program_id(1)

        @pl.when(t == 0)
        def _():
            acc_s[...] = jnp.zeros_like(acc_s)

        acc_s[...] += _dot_tn(a_ref[...].astype(BF16), b_ref[...].astype(BF16))

        @pl.when(t == pl.num_programs(1) - 1)
        def _():
            o_ref[...] = acc_s[...].astype(BF16)

    return pl.pallas_call(
        body, name=name, grid=(nblk, s // ts),
        in_specs=[pl.BlockSpec((ts, a_w), (lambda k, t: (t, k)) if a_split else (lambda k, t: (t, 0))),
                  pl.BlockSpec((ts, b_w), (lambda k, t: (t, k)) if b_split else (lambda k, t: (t, 0)))],
        out_specs=pl.BlockSpec((None, a_w, b_w), lambda k, t: (k, 0, 0)),
        out_shape=SDS((nblk, a_w, b_w), BF16),
        scratch_shapes=[pltpu.VMEM((a_w, b_w), F32)],
        compiler_params=_cparams(("arbitrary", "arbitrary"), 48),
    )(a, b)


def _wgrad_t(name, a_t, b, nblk, a_split, b_split, tokens):
    s = b.shape[0]
    ts = min(tokens, s)
    a_w = a_t.shape[0] // nblk if a_split else a_t.shape[0]
    b_w = b.shape[1] // nblk if b_split else b.shape[1]

    def body(a_ref, b_ref, o_ref, acc_s):
        t = pl.program_id(1)

        @pl.when(t == 0)
        def _():
            acc_s[...] = jnp.zeros_like(acc_s)

        acc_s[...] += _dot(a_ref[...], b_ref[...].astype(BF16))

        @pl.when(t == pl.num_programs(1) - 1)
        def _():
            o_ref[...] = acc_s[...].astype(BF16)

    return pl.pallas_call(
        body, name=name, grid=(nblk, s // ts),
        in_specs=[pl.BlockSpec((a_w, ts), (lambda k, t: (k, t)) if a_split else (lambda k, t: (0, t))),
                  pl.BlockSpec((ts, b_w), (lambda k, t: (t, k)) if b_split else (lambda k, t: (t, 0)))],
        out_specs=pl.BlockSpec((None, a_w, b_w), lambda k, t: (k, 0, 0)),
        out_shape=SDS((nblk, a_w, b_w), BF16),
        scratch_shapes=[pltpu.VMEM((a_w, b_w), F32)],
        compiler_params=_cparams(("arbitrary", "arbitrary"), 48),
    )(a_t, b)


def _place():
    x, y, c = lax.axis_index("x"), lax.axis_index("y"), lax.axis_index("c")
    return x, y, c


def _other_chips(x, y):
    return [(x, 1 - y, 2 * x + 1 - y), (1 - x, y, 2 * (1 - x) + y), (1 - x, 1 - y, 2 * (1 - x) + 1 - y)]


class _Plan:
    def __init__(self, arrays, out_shape, sems, start, finish):
        self.arrays, self.out_shape, self.sems, self.start, self.finish = arrays, out_shape, sems, start, finish


def _gather_plan(shards):
    n = len(shards)

    def copies(ins, outs, sems):
        send_sems, recv_sems, local_sems = sems
        x, y, c = _place()
        chip = 2 * x + y
        me = 2 * chip + c
        sib = (x, y, 1 - c)
        chips = _other_chips(x, y)

        def rc(k, t, src, blk, to):
            return pltpu.make_async_remote_copy(
                src_ref=src, dst_ref=outs[t].at[blk], send_sem=send_sems.at[k * n + t],
                recv_sem=recv_sems.at[k * n + t], device_id=to, device_id_type=MESH)

        local = [pltpu.make_async_copy(ins[t], outs[t].at[me], local_sems.at[t]) for t in range(n)]
        sends = [rc(0, t, ins[t], me, sib) for t in range(n)]
        for j, (px, py, _) in enumerate(chips):
            sends += [rc(1 + j, t, ins[t], me, (px, py, c)) for t in range(n)]
        return rc, local, sends, chips, chip, c, sib

    def start(ins, outs, sems):
        _, local, sends, _, _, _, _ = copies(ins, outs, sems)
        for cp in local + sends:
            cp.start()

    def finish(ins, outs, sems):
        rc, local, sends, chips, chip, c, sib = copies(ins, outs, sems)
        passed = []
        for j, (px, py, pc) in enumerate(chips):
            blk = 2 * pc + c
            for t in range(n):
                rc(1 + j, t, ins[t], blk, sib).wait_recv()
            for t in range(n):
                cp = rc(4 + j, t, outs[t].at[blk], blk, sib)
                cp.start()
                passed.append(cp)
        for t in range(n):
            rc(0, t, ins[t], 2 * chip + 1 - c, sib).wait_recv()
        for j, (px, py, pc) in enumerate(chips):
            for t in range(n):
                rc(4 + j, t, ins[t], 2 * pc + 1 - c, sib).wait_recv()
        for cp in sends + passed:
            cp.wait_send()
        for cp in local:
            cp.wait()

    return _Plan(list(shards), [SDS((N_SLOT,) + tuple(a.shape), a.dtype) for a in shards],
                 [pltpu.SemaphoreType.DMA((7 * n,)), pltpu.SemaphoreType.DMA((7 * n,)),
                  pltpu.SemaphoreType.DMA((n,))], start, finish)


def _sibling_plan(grads, whole=()):
    n, m = len(grads), len(whole)

    def copies(ins, outs, sems):
        send_sems, recv_sems = sems
        x, y, c = _place()
        sib = (x, y, 1 - c)

        def rc(t, src, dst):
            return pltpu.make_async_remote_copy(src_ref=src, dst_ref=dst, send_sem=send_sems.at[t],
                                                recv_sem=recv_sems.at[t], device_id=sib, device_id_type=MESH)
        return rc, c

    def start(ins, outs, sems):
        rc, c = copies(ins, outs, sems)
        for t in range(n):
            for j in range(4):
                rc(t, ins[t].at[2 * j + 1 - c], outs[t].at[j]).start()
        for t in range(n, n + m):
            rc(t, ins[t], outs[t]).start()

    def finish(ins, outs, sems):
        rc, _ = copies(ins, outs, sems)
        for t in range(n):
            rc(t, ins[t].at[pl.ds(0, 4)], outs[t]).wait()
        for t in range(n, n + m):
            rc(t, ins[t], outs[t]).wait()

    return _Plan(list(grads) + list(whole),
                 [SDS((4,) + tuple(g.shape[1:]), g.dtype) for g in grads] + [SDS(a.shape, a.dtype) for a in whole],
                 [pltpu.SemaphoreType.DMA((n + m,)), pltpu.SemaphoreType.DMA((n + m,))], start, finish)


def _chips_plan(parts, whole=()):
    n, m = len(parts), len(whole)

    def src_of(ins, t, pc):
        return ins[t].at[pc] if t < n else ins[t]

    def local_copies(ins, outs, sems, chip):
        return [pltpu.make_async_copy(src_of(ins, t, chip), outs[t].at[chip], sems[2].at[t]) for t in range(n + m)]

    def start(ins, outs, sems):
        send_sems, recv_sems, _ = sems
        x, y, c = _place()
        chip = 2 * x + y
        for cp in local_copies(ins, outs, sems, chip):
            cp.start()
        for px, py, pc in _other_chips(x, y):
            for t in range(n + m):
                pltpu.make_async_remote_copy(src_ref=src_of(ins, t, pc), dst_ref=outs[t].at[chip],
                                             send_sem=send_sems.at[t], recv_sem=recv_sems.at[t],
                                             device_id=(px, py, c), device_id_type=MESH).start()

    def finish(ins, outs, sems):
        send_sems, recv_sems, _ = sems
        x, y, c = _place()
        for t in range(n + m):
            three = outs[t].at[pl.ds(0, 3)]
            pltpu.make_async_remote_copy(src_ref=three, dst_ref=three, send_sem=send_sems.at[t],
                                         recv_sem=recv_sems.at[t], device_id=(x, y, c), device_id_type=MESH).wait()
        for cp in local_copies(ins, outs, sems, 2 * x + y):
            cp.wait()

    return _Plan(list(parts) + list(whole),
                 [SDS(p.shape, p.dtype) for p in parts] + [SDS((4,) + tuple(a.shape), a.dtype) for a in whole],
                 [pltpu.SemaphoreType.DMA((n + m,)), pltpu.SemaphoreType.DMA((n + m,)),
                  pltpu.SemaphoreType.DMA((n + m,))], start, finish)


def _exchange_plan(arr):
    def peers(x, y, c):
        flip = lambda v, f: 1 - v if f else v
        return [(flip(x, fx), flip(y, fy), flip(c, fc))
                for fx in (0, 1) for fy in (0, 1) for fc in (0, 1) if fx or fy or fc]

    def start(ins, outs, sems):
        x, y, c = _place()
        me = 4 * x + 2 * y + c
        pltpu.make_async_copy(ins[0], outs[0].at[me], sems[2].at[0]).start()
        for to in peers(x, y, c):
            pltpu.make_async_remote_copy(src_ref=ins[0], dst_ref=outs[0].at[me], send_sem=sems[0].at[0],
                                         recv_sem=sems[1].at[0], device_id=to, device_id_type=MESH).start()

    def finish(ins, outs, sems):
        x, y, c = _place()
        seven = outs[0].at[pl.ds(0, 7)]
        pltpu.make_async_remote_copy(src_ref=seven, dst_ref=seven, send_sem=sems[0].at[0], recv_sem=sems[1].at[0],
                                     device_id=(x, y, c), device_id_type=MESH).wait()
        pltpu.make_async_copy(ins[0], outs[0].at[4 * x + 2 * y + c], sems[2].at[0]).wait()

    return _Plan([arr], [SDS((N_SLOT,) + tuple(arr.shape), arr.dtype)],
                 [pltpu.SemaphoreType.DMA((1,)), pltpu.SemaphoreType.DMA((1,)), pltpu.SemaphoreType.DMA((1,))],
                 start, finish)


def _join(*plans):
    def cut(seq, sizes):
        out, at = [], 0
        for k in sizes:
            out.append(seq[at:at + k])
            at += k
        return out

    n_arr = [len(p.arrays) for p in plans]
    n_sem = [len(p.sems) for p in plans]

    def start(ins, outs, sems):
        for p, i, o, s in zip(plans, cut(ins, n_arr), cut(outs, n_arr), cut(sems, n_sem)):
            p.start(i, o, s)

    def finish(ins, outs, sems):
        for p, i, o, s in zip(plans, cut(ins, n_arr), cut(outs, n_arr), cut(sems, n_sem)):
            p.finish(i, o, s)

    return _Plan([a for p in plans for a in p.arrays], [o for p in plans for o in p.out_shape],
                 [s for p in plans for s in p.sems], start, finish)


def _run_plan(name, plan):
    k = len(plan.arrays)

    def body(*refs):
        ins, outs, sems = refs[:k], refs[k:2 * k], refs[2 * k:]
        plan.start(ins, outs, sems)
        plan.finish(ins, outs, sems)

    return pl.pallas_call(
        body, name=name, in_specs=[ANY] * k, out_specs=[ANY] * k, out_shape=plan.out_shape,
        scratch_shapes=plan.sems, compiler_params=pltpu.CompilerParams(has_side_effects=True),
    )(*plan.arrays)


def _call(body, *, name, grid, in_specs, out_specs, out_shape, scratch_shapes, params, args, comm=None,
          aliases=None, prefetch=()):
    aliases = aliases or {}
    n_pre = len(prefetch)

    def launch(fn, ins_specs, outs_specs, outs_shape, scratch, operands):
        spec = pltpu.PrefetchScalarGridSpec(num_scalar_prefetch=n_pre, grid=grid, in_specs=ins_specs,
                                            out_specs=outs_specs, scratch_shapes=scratch)
        return pl.pallas_call(fn, name=name, grid_spec=spec, out_shape=outs_shape, compiler_params=params,
                              input_output_aliases=aliases)(*prefetch, *[_small_in_hbm(a) for a in operands])

    if comm is None:
        return list(launch(body, in_specs, out_specs, out_shape, scratch_shapes, args)), []
    n_in, n_out, n_scr, k = len(in_specs), len(out_specs), len(scratch_shapes), len(comm.arrays)

    def wrapped(*refs):
        pre, refs = refs[:n_pre], refs[n_pre:]
        ins = refs[:n_in]
        c_in = refs[n_in:n_in + k]
        outs = refs[n_in + k:n_in + k + n_out]
        c_out = refs[n_in + k + n_out:n_in + 2 * k + n_out]
        scr = refs[n_in + 2 * k + n_out:n_in + 2 * k + n_out + n_scr]
        sems = refs[n_in + 2 * k + n_out + n_scr:]
        ids = [pl.program_id(d) for d in range(len(grid))]
        first = ids[0] == 0
        last = ids[0] == grid[0] - 1
        for d in range(1, len(grid)):
            first = first & (ids[d] == 0)
            last = last & (ids[d] == grid[d] - 1)

        @pl.when(first)
        def _():
            comm.start(c_in, c_out, sems)

        body(*pre, *ins, *outs, *scr)

        @pl.when(last)
        def _():
            comm.finish(c_in, c_out, sems)

    res = launch(wrapped, list(in_specs) + [ANY] * k, list(out_specs) + [ANY] * k,
                 list(out_shape) + list(comm.out_shape), list(scratch_shapes) + list(comm.sems),
                 tuple(args) + tuple(comm.arrays))
    return list(res[:n_out]), list(res[n_out:])


def _row_tile(rows):
    for t in (512, 256, 128, 64, 32, 16, 8):
        if rows % t == 0:
            return t
    return rows


def _pair_sum(name, g8, recv4, core):
    _, rows, cols = recv4.shape
    tr = _row_tile(rows)
    g42 = g8.reshape(4, 2, rows, cols)

    def body(c_ref, g_ref, r_ref, o_ref):
        del c_ref
        o_ref[...] = (g_ref[...].astype(F32) + r_ref[...].astype(F32)).astype(o_ref.dtype)

    return pl.pallas_call(
        body, name=name,
        grid_spec=pltpu.PrefetchScalarGridSpec(
            num_scalar_prefetch=1, grid=(4, rows // tr),
            in_specs=[pl.BlockSpec((None, None, tr, cols), lambda j, i, c_ref: (j, c_ref[0], i, 0)),
                      pl.BlockSpec((None, tr, cols), lambda j, i, c_ref: (j, i, 0))],
            out_specs=pl.BlockSpec((None, tr, cols), lambda j, i, c_ref: (j, i, 0))),
        out_shape=SDS(recv4.shape, g8.dtype),
        compiler_params=_cparams(("arbitrary", "arbitrary"), 32),
    )(core, g42, recv4)


def _add2(name, a, b):
    rows, cols = a.shape
    tr = _row_tile(rows)

    def body(a_ref, b_ref, o_ref):
        o_ref[...] = a_ref[...] + b_ref[...]

    blk = pl.BlockSpec((tr, cols), lambda i: (i, 0))
    return pl.pallas_call(body, name=name, grid=(rows // tr,), in_specs=[blk, blk], out_specs=blk,
                          out_shape=SDS(a.shape, a.dtype),
                          compiler_params=_cparams(("arbitrary",), 32))(a, b)


def _sum_terms(name, terms):
    k, rows, cols = terms.shape
    tr = _row_tile(rows)

    def body(r_ref, o_ref):
        acc = r_ref[0]
        for q in range(1, k):
            acc = acc + r_ref[q]
        o_ref[...] = acc

    return pl.pallas_call(body, name=name, grid=(rows // tr,),
                          in_specs=[pl.BlockSpec((k, tr, cols), lambda i: (0, i, 0))],
                          out_specs=pl.BlockSpec((tr, cols), lambda i: (i, 0)),
                          out_shape=SDS((rows, cols), terms.dtype),
                          compiler_params=_cparams(("arbitrary",), 32))(terms)


def _adam_update(g, w, m, v):
    c1 = 1.0 / (1.0 - ADAM_B1 ** ADAM_STEP)
    c2 = 1.0 / (1.0 - ADAM_B2 ** ADAM_STEP)
    mn = ADAM_B1 * m + (1.0 - ADAM_B1) * g
    vn = ADAM_B2 * v + (1.0 - ADAM_B2) * (g * g)
    delta = (-ADAM_LR) * ((mn * c1) / (jnp.sqrt(vn * c2) + ADAM_EPS) + ADAM_WD * w)
    return delta, mn, vn


def _adamw_many(name, gs, ws, ms, vs):
    n = len(gs)

    def body(*refs):
        for p in range(n):
            g, w, m, v = (refs[q * n + p][...] for q in range(4))
            d, mn, vn = _adam_update(g, w, m, v)
            refs[4 * n + p][...] = d
            refs[5 * n + p][...] = mn
            refs[6 * n + p][...] = vn

    full = [pl.BlockSpec(memory_space=pltpu.VMEM)] * n
    shapes = [SDS(w.shape, F32) for w in ws]
    res = pl.pallas_call(body, name=name, in_specs=full * 4, out_specs=full * 3, out_shape=shapes * 3,
                         compiler_params=pltpu.CompilerParams(vmem_limit_bytes=32 * MiB))(*gs, *ws, *ms, *vs)
    return [(res[p], res[n + p], res[2 * n + p]) for p in range(n)]


def _adamw(name, terms, w, m, v):
    k, rows, cols = terms.shape
    tr = _row_tile(rows)

    def body(t_ref, w_ref, m_ref, v_ref, g_ref, d_ref, mo_ref, vo_ref):
        g = t_ref[0].astype(F32)
        for q in range(1, k):
            g = g + t_ref[q].astype(F32)
        g_ref[...] = g
        d_ref[...], mo_ref[...], vo_ref[...] = _adam_update(g, w_ref[...], m_ref[...], v_ref[...])

    blk = pl.BlockSpec((tr, cols), lambda i: (i, 0))
    return pl.pallas_call(body, name=name, grid=(rows // tr,),
                          in_specs=[pl.BlockSpec((k, tr, cols), lambda i: (0, i, 0)), blk, blk, blk],
                          out_specs=[blk] * 4, out_shape=[SDS((rows, cols), F32)] * 4,
                          compiler_params=_cparams(("arbitrary",), 40))(terms, w, m, v)


def kernel(x, norm_mix_g, w_in, conv_w, conv_b, w_rgate, b_rgate, w_igate, b_igate, lru_lambda, w_out_a, sgu_ln_g, sgu_ln_b, sgu_w_s, sgu_b_s, w_out_b, w_out, norm_mlp_g, w_up, w_down, norm_final_g, loss_target, m_norm_mix_g, m_w_in, m_conv_w, m_conv_b, m_w_rgate, m_b_rgate, m_w_igate, m_b_igate, m_lru_lambda, m_w_out_a, m_sgu_ln_g, m_sgu_ln_b, m_sgu_w_s, m_sgu_b_s, m_w_out_b, m_w_out, m_norm_mlp_g, m_w_up, m_w_down, m_norm_final_g, v_norm_mix_g, v_w_in, v_conv_w, v_conv_b, v_w_rgate, v_b_rgate, v_w_igate, v_b_igate, v_lru_lambda, v_w_out_a, v_sgu_ln_g, v_sgu_ln_b, v_sgu_w_s, v_sgu_b_s, v_w_out_b, v_w_out, v_norm_mlp_g, v_w_up, v_w_down, v_norm_final_g):
    cx, cy, cc = _place()
    me = 4 * cx + 2 * cy + cc
    core = jnp.reshape(cc, (1,)).astype(jnp.int32)
    xs = x[0]
    tgt = loss_target[0]
    s = xs.shape[0]

    gate_shard = jnp.stack([w_rgate[0], w_igate[0]]).astype(BF16).reshape(2 * HEADS * 32, HEAD_DIM)
    vec_shard = jnp.concatenate([conv_w[0], b_rgate[0], b_igate[0]], axis=1)
    vec_shard = jnp.pad(vec_shard, ((0, 4), (0, 256 - vec_shard.shape[1])))
    shards = [w_in[0].astype(BF16), w_out_a[0].astype(BF16), w_out_b[0].astype(BF16), w_out[0].astype(BF16),
              w_up[0].astype(BF16), w_down[0].astype(BF16), gate_shard, vec_shard]
    (z, n1_t, w_in_g), (gate_g, vec_g) = _in_proj(xs, norm_mix_g, shards[0], _slot_order(cx, cy, cc),
                                                comm=_gather_plan(shards[6:8]))
    gates = gate_g.reshape(N_SLOT, 2, HEADS, 32, HEAD_DIM).transpose(1, 2, 0, 3, 4).reshape(2, HEADS, HEAD_DIM, HEAD_DIM)
    w_r_f, w_i_f = gates[0], gates[1]
    conv_w_f = vec_g[:, 0:4, 0:128].transpose(1, 0, 2).reshape(CONV_K, D)
    b_r_f = vec_g[:, 0:4, 128:160].transpose(1, 0, 2).reshape(1, D)
    b_i_f = vec_g[:, 0:4, 160:192].transpose(1, 0, 2).reshape(1, D)
    b_s_t = jnp.transpose(sgu_b_s[0])

    (ya, hs), (w_oa_g, w_ob_g, w_out_g) = _branch_a_fwd(
        z, conv_w_f, conv_b, w_r_f, b_r_f, w_i_f, b_i_f, lru_lambda, comm=_gather_plan(shards[1:4]))
    w_oa_f = w_oa_g.reshape(D, D)
    w_ob_f = w_ob_g.reshape(D, D)
    w_out_f = w_out_g.reshape(D, D)
    (yb,), (w_up_g,) = _branch_b_fwd(z, sgu_ln_g, sgu_ln_b, sgu_w_s[0], b_s_t, comm=_gather_plan(shards[4:5]))
    (pa, pb, merged, h1), (w_down_g,) = _merge_out(ya, yb, z, xs, w_oa_f, w_ob_f, w_out_f,
                                                   comm=_gather_plan(shards[5:6]))
    w_down_f = w_down_g.reshape(N_SLOT * FF_COLS, D)
    gf2 = norm_final_g.reshape(1, D)
    r_act, act_t, n2_t, dh2, loss_acc, d_gfin = _mlp_fwd(h1, norm_mlp_g, w_up_g, w_down_f, gf2, tgt)

    def pair(names, grads, recv):
        return [_pair_sum("pair_sum_" + nm, g, r, core) for nm, g, r in zip(names, grads, recv)]

    g_down = _wgrad_t("wgrad_down", act_t, dh2, N_SLOT // 2, True, False, 2048)
    g_down = g_down.reshape(N_SLOT, FF_COLS, D)
    (df, dh1, d_gmlp), (r_down,) = _mlp_bwd(dh2, r_act, w_down_f, w_up_g, h1, norm_mlp_g,
                                            comm=_sibling_plan([g_down]))
    (p_down,) = pair(["down"], [g_down], [r_down])
    g_up = _wgrad_t("wgrad_up", n2_t, df, N_SLOT, False, True, 4096)
    g_out = _wgrad("wgrad_out", merged, dh1, 1, False, False).reshape(N_SLOT, D // N_SLOT, D)
    (dz, dpa, dpb, dya, dyb), (got_down, r_up, r_out) = _merge_bwd(
        dh1, z, pa, pb, w_out_f, w_oa_f, w_ob_f, comm=_join(_chips_plan([p_down]), _sibling_plan([g_up, g_out])))
    p_up, p_out = pair(["up", "out"], [g_up, g_out], [r_up, r_out])
    g_oa = _wgrad("wgrad_out_a", ya, dpa, 1, False, False).reshape(N_SLOT, D // N_SLOT, D)
    g_ob = _wgrad("wgrad_out_b", yb, dpb, 1, False, False).reshape(N_SLOT, D // N_SLOT, D)
    (dz, d_ws, d_bs, d_ln), (got_up, r_oa, r_ob) = _branch_b_bwd(
        dz, dyb, z, sgu_ln_g, sgu_ln_b, sgu_w_s[0], b_s_t,
        comm=_join(_chips_plan([p_up]), _sibling_plan([g_oa, g_ob])))
    p_oa, p_ob = pair(["out_a", "out_b"], [g_oa, g_ob], [r_oa, r_ob])
    (dz, d_vec, d_wr, d_wi), (got_out, got_oa, got_ob) = _branch_a_bwd(
        dz, dya, z, hs, conv_w_f, conv_b, w_r_f, b_r_f, w_i_f, b_i_f, lru_lambda,
        comm=_chips_plan([p_out, p_oa, p_ob]))
    g_in = _wgrad_t("wgrad_in", n1_t, dz, N_SLOT, False, True, 4096)
    g_gate = jnp.stack([d_wr, d_wi]).reshape(2, HEADS, N_SLOT, 32, HEAD_DIM).transpose(2, 0, 1, 3, 4)
    g_gate = g_gate.reshape(N_SLOT, 2 * HEADS * 32, HEAD_DIM).astype(BF16)

    d_bs_row = jnp.pad(d_bs[:, :, 0].reshape(1, GROUPS * CHUNK), ((0, 0), (0, D - GROUPS * CHUNK)))
    vecs = jnp.concatenate([d_vec, jnp.concatenate([d_ln[0:2], d_gmlp, d_gfin, d_bs_row, jnp.zeros((3, D), F32)])])
    d_ws2 = d_ws.reshape(GROUPS * CHUNK, CHUNK)
    r_in, r_gate, r_vecs, r_ws = _run_plan("rs_sibling_in", _sibling_plan([g_in, g_gate], [vecs, d_ws2]))
    p_in, p_gate = pair(["in", "gate"], [g_in, g_gate], [r_in, r_gate])
    vecs_chip = _add2("pair_sum_vecs", vecs, r_vecs)
    ws_chip = _add2("pair_sum_ws", d_ws2, r_ws)
    (dx, d_gmix), (got_in, got_gate, got_vecs, got_ws) = _in_bwd(
        dz, w_in_g, xs, dh1, norm_mix_g, comm=_chips_plan([p_in, p_gate], [vecs_chip, ws_chip]))
    vecs_sum = _sum_terms("sum_vecs", got_vecs)
    last = jnp.concatenate([d_gmix, jnp.pad(loss_acc[0:1], ((0, 0), (0, D - 128))), jnp.zeros((6, D), F32)])
    (last_all,) = _run_plan("exchange_last", _exchange_plan(last))
    last_sum = _sum_terms("sum_last", last_all)
    loss = last_sum[1, 0]
    got = [got_in, got_oa, got_ob, got_out, got_up, got_down, got_gate]

    def step(nm, terms, w, m, v, rows, cols):
        g, d, mn, vn = _adamw("adamw_" + nm, terms.reshape(4, rows, cols), w.reshape(rows, cols),
                              m.reshape(rows, cols), v.reshape(rows, cols))
        return [a.reshape(w.shape) for a in (g, d, mn, vn)]

    o_in = step("in", got[0], w_in, m_w_in, v_w_in, D, W_IN_COLS)
    o_oa = step("out_a", got[1], w_out_a, m_w_out_a, v_w_out_a, D // N_SLOT, D)
    o_ob = step("out_b", got[2], w_out_b, m_w_out_b, v_w_out_b, D // N_SLOT, D)
    o_out = step("out", got[3], w_out, m_w_out, v_w_out, D // N_SLOT, D)
    o_up = step("up", got[4], w_up, m_w_up, v_w_up, D, FF_COLS)
    o_down = step("down", got[5], w_down, m_w_down, v_w_down, FF_COLS, D)
    gate_w = jnp.stack([w_rgate[0], w_igate[0]]).reshape(2 * HEADS * 32, HEAD_DIM)
    gate_m = jnp.stack([m_w_rgate[0], m_w_igate[0]]).reshape(2 * HEADS * 32, HEAD_DIM)
    gate_v = jnp.stack([v_w_rgate[0], v_w_igate[0]]).reshape(2 * HEADS * 32, HEAD_DIM)
    o_gate = _adamw("adamw_gate", got[6], gate_w, gate_m, gate_v)
    o_gate = [a.reshape(2, 1, HEADS, 32, HEAD_DIM) for a in o_gate]
    o_wr = [a[0] for a in o_gate]
    o_wi = [a[1] for a in o_gate]

    def own(full, width):
        return lax.dynamic_slice_in_dim(full, me * width, width, axis=1)

    small_g = {
        "norm_mix_g": last_sum[0:1], "conv_w": own(vecs_sum[0:4], 128), "conv_b": vecs_sum[4:5],
        "b_rgate": own(vecs_sum[5:6].reshape(HEADS, HEAD_DIM), 32),
        "b_igate": own(vecs_sum[6:7].reshape(HEADS, HEAD_DIM), 32),
        "lru_lambda": vecs_sum[7:8], "sgu_ln_g": vecs_sum[8:9], "sgu_ln_b": vecs_sum[9:10],
        "norm_mlp_g": vecs_sum[10:11], "norm_final_g": vecs_sum[11:12],
        "sgu_b_s": vecs_sum[12, 0:GROUPS * CHUNK].reshape(GROUPS, CHUNK),
    }
    small_w = {"norm_mix_g": (norm_mix_g, m_norm_mix_g, v_norm_mix_g), "conv_w": (conv_w, m_conv_w, v_conv_w),
               "conv_b": (conv_b, m_conv_b, v_conv_b), "b_rgate": (b_rgate, m_b_rgate, v_b_rgate),
               "b_igate": (b_igate, m_b_igate, v_b_igate), "lru_lambda": (lru_lambda, m_lru_lambda, v_lru_lambda),
               "sgu_ln_g": (sgu_ln_g, m_sgu_ln_g, v_sgu_ln_g), "sgu_ln_b": (sgu_ln_b, m_sgu_ln_b, v_sgu_ln_b),
               "norm_mlp_g": (norm_mlp_g, m_norm_mlp_g, v_norm_mlp_g),
               "norm_final_g": (norm_final_g, m_norm_final_g, v_norm_final_g),
               "sgu_b_s": (sgu_b_s, m_sgu_b_s, v_sgu_b_s), "sgu_w_s": (sgu_w_s, m_sgu_w_s, v_sgu_w_s)}
    order = list(small_g)
    as2d = lambda k, a: a.reshape(small_g[k].shape)
    upd = _adamw_many("adamw_small", [small_g[k] for k in order], *[[as2d(k, small_w[k][q]) for k in order]
                                                                     for q in range(3)])
    o_small = {k: [a.reshape(small_w[k][0].shape) for a in (small_g[k],) + u] for k, u in zip(order, upd)}
    ws3 = [a[0].reshape(GROUPS * CHUNK, CHUNK) for a in small_w.pop("sgu_w_s")]
    o_small["sgu_w_s"] = [a.reshape(sgu_w_s.shape) for a in _adamw("adamw_ws", got_ws, *ws3)]

    per_weight = {"norm_mix_g": o_small["norm_mix_g"], "w_in": o_in, "conv_w": o_small["conv_w"],
                  "conv_b": o_small["conv_b"], "w_rgate": o_wr, "b_rgate": o_small["b_rgate"], "w_igate": o_wi,
                  "b_igate": o_small["b_igate"], "lru_lambda": o_small["lru_lambda"], "w_out_a": o_oa,
                  "sgu_ln_g": o_small["sgu_ln_g"], "sgu_ln_b": o_small["sgu_ln_b"], "sgu_w_s": o_small["sgu_w_s"],
                  "sgu_b_s": o_small["sgu_b_s"], "w_out_b": o_ob, "w_out": o_out, "norm_mlp_g": o_small["norm_mlp_g"],
                  "w_up": o_up, "w_down": o_down, "norm_final_g": o_small["norm_final_g"]}
    names_w = list(per_weight)
    return (loss, dx[None], *[per_weight[k][0] for k in names_w], *[per_weight[k][1] for k in names_w],
            *[per_weight[k][2] for k in names_w], *[per_weight[k][3] for k in names_w])
```

```python
import jax
import jax.numpy as jnp
from jax import lax
from jax.experimental import pallas as pl
from jax.experimental.pallas import tpu as pltpu

F32 = jnp.float32
BF16 = jnp.bfloat16
SDS = jax.ShapeDtypeStruct
MESH = pl.DeviceIdType.MESH
ANY = pl.BlockSpec(memory_space=pltpu.HBM)

D = 1024
N_SLOT = 8
W_IN_COLS = 768
FF_COLS = 512
HEADS, HEAD_DIM = 4, 256
GROUPS, GROUP_DIM = 4, 256
CHUNK = 128
CONV_K = 4
NORM_EPS = 1e-6
LN_EPS = 1e-5
LRU_C = 8.0
ADAM_LR, ADAM_B1, ADAM_B2, ADAM_EPS, ADAM_WD, ADAM_STEP = 0.001, 0.9, 0.999, 1e-08, 0.01, 10

TM_ROWS = 1024
TM_MERGE = 512
T_BRANCH_A = 256
T_BRANCH_B = 256
MiB = 1024 * 1024
SMALL_OPERAND = 16 * 1024

_GELU_C = 0.7978845608028654
_GELU_A = 0.044715


def _small_in_hbm(a):
    return pltpu.with_memory_space_constraint(a, pltpu.HBM) if a.size <= SMALL_OPERAND else a


def _cparams(sem, vmem_mib):
    return pltpu.CompilerParams(dimension_semantics=sem, vmem_limit_bytes=vmem_mib * MiB)


def _gelu(x):
    t = jnp.tanh(_GELU_C * (x + _GELU_A * x * x * x))
    return 0.5 * x * (1.0 + t)


def _gelu_and_grad(x):
    x2 = x * x
    t = jnp.tanh(_GELU_C * x * (1.0 + _GELU_A * x2))
    g = 0.5 * x * (1.0 + t)
    dg = 0.5 * (1.0 + t) + 0.5 * x * (1.0 - t * t) * _GELU_C * (1.0 + 3.0 * _GELU_A * x2)
    return g, dg


def _softplus(x):
    return jnp.maximum(x, 0.0) + jnp.log1p(jnp.exp(-jnp.abs(x)))


def _dot(a, b):
    return jnp.dot(a, b, preferred_element_type=F32)


def _dot_nt(a, b):
    return lax.dot_general(a, b, (((1,), (1,)), ((), ())), preferred_element_type=F32)


def _dot_tn(a, b):
    return lax.dot_general(a, b, (((0,), (0,)), ((), ())), preferred_element_type=F32)


def _rows_shifted(prev8, cur, k):
    ext = jnp.concatenate([prev8, cur], axis=0)
    return pltpu.roll(ext, k, 0)[8:]


def _rows_advanced(cur, next8, k):
    t = cur.shape[0]
    ext = jnp.concatenate([cur, next8], axis=0)
    return pltpu.roll(ext, t + 8 - k, 0)[:t]


def _slot_order(x, y, c):
    chip = 2 * x + y
    order = [2 * chip + c, 2 * chip + 1 - c]
    for _, _, pc in _other_chips(x, y):
        order += [2 * pc + c, 2 * pc + 1 - c]
    return jnp.stack(order).astype(jnp.int32)


def _in_proj(x, g_mix, w_in_own, order, comm=None):
    s = x.shape[0]
    tm = min(TM_ROWS, s)
    ni = s // tm

    def body(order_ref, x_ref, g_ref, own_ref, z_ref, nt_ref, wg_ref, n_s, w_s, send_sems, recv_sems, local_sems):
        j, i = pl.program_id(0), pl.program_id(1)
        px, py, c = _place()
        chip = 2 * px + py
        me = 2 * chip + c
        sib = (px, py, 1 - c)
        chips = _other_chips(px, py)

        def rc(k, src, blk, to):
            return pltpu.make_async_remote_copy(src_ref=src, dst_ref=w_s.at[blk], send_sem=send_sems.at[k],
                                                recv_sem=recv_sems.at[k], device_id=to, device_id_type=MESH)

        (yx, yy, y_chip), (xx, xy, x_chip), _ = chips
        own_in = pltpu.make_async_copy(own_ref, w_s.at[me], local_sems.at[0])
        sends = [rc(0, own_ref, me, sib), rc(1, own_ref, me, (yx, yy, c)), rc(2, own_ref, me, (xx, xy, c))]
        passed = [rc(4 + q, w_s.at[2 * pc + c], 2 * pc + c, sib) for q, (_, _, pc) in enumerate(chips)]
        relays = [rc(3, w_s.at[2 * y_chip + c], 2 * y_chip + c, (xx, xy, c)),
                  rc(3, w_s.at[2 * x_chip + c], 2 * x_chip + c, (yx, yy, c))]
        keep = pltpu.make_async_copy(w_s, wg_ref, local_sems.at[1])

        @pl.when((i == 0) & (j == 0))
        def _():
            own_in.start()
            for cp in sends:
                cp.start()
            own_in.wait()

        @pl.when((i == 0) & (j == 1))
        def _():
            rc(0, own_ref, 2 * chip + 1 - c, sib).wait_recv()

        for q, (_, _, pc) in enumerate(chips):
            @pl.when((i == 0) & (j == 2 + 2 * q))
            def _():
                rc(min(1 + q, 3), own_ref, 2 * pc + c, sib).wait_recv()
                passed[q].start()
                if q < 2:
                    @pl.when(c == q)
                    def _():
                        relays[q].start()

            @pl.when((i == 0) & (j == 3 + 2 * q))
            def _():
                rc(4 + q, own_ref, 2 * pc + 1 - c, sib).wait_recv()

        rows = pl.ds(pl.multiple_of(i * tm, tm), tm)

        @pl.when(j == 0)
        def _():
            xv = x_ref[...]
            rstd = lax.rsqrt(jnp.mean(xv * xv, axis=-1, keepdims=True) + NORM_EPS)
            nb = (xv * rstd * g_ref[...]).astype(BF16)
            n_s[rows, :] = nb
            nt_ref[...] = nb.T

        z_ref[...] = _dot(n_s[rows, :], w_s[order_ref[j]]).astype(BF16)

        @pl.when((i == 0) & (j == N_SLOT - 1))
        def _():
            keep.start()

        @pl.when((i == ni - 1) & (j == N_SLOT - 1))
        def _():
            for cp in sends + passed:
                cp.wait_send()
            for q in range(2):
                @pl.when(c == q)
                def _():
                    relays[q].wait_send()
            keep.wait()

    first_pass = lambda j, i, o: (jnp.where(j == 0, i, ni - 1), 0)
    (z, n1, w_in_g), extra = _call(
        body, name="in_proj", grid=(N_SLOT, ni), prefetch=(order,),
        in_specs=[pl.BlockSpec((tm, D), first_pass),
                  pl.BlockSpec((1, D), lambda j, i, o: (0, 0)), ANY],
        out_specs=[pl.BlockSpec((tm, W_IN_COLS), lambda j, i, o: (i, o[j])),
                   pl.BlockSpec((D, tm), lambda j, i, o: (0, jnp.where(j == 0, i, ni - 1))), ANY],
        out_shape=[SDS((s, N_SLOT * W_IN_COLS), BF16), SDS((D, s), BF16), SDS((N_SLOT, D, W_IN_COLS), BF16)],
        scratch_shapes=[pltpu.VMEM((s, D), BF16), pltpu.VMEM((N_SLOT, D, W_IN_COLS), BF16),
                        pltpu.SemaphoreType.DMA((7,)), pltpu.SemaphoreType.DMA((7,)), pltpu.SemaphoreType.DMA((2,))],
        params=_cparams(("arbitrary", "arbitrary"), 56), args=(x, g_mix, w_in_own), comm=comm)
    return (z, n1, w_in_g), extra


def _lru_gates(xc, xcb, wr_ref, br, wi_ref, bi, sp_lam, a_s, b_s, r_s=None, i_s=None, m_s=None):
    for h in range(HEADS):
        sl = slice(h * HEAD_DIM, (h + 1) * HEAD_DIM)
        r = jax.nn.sigmoid(_dot(xcb[:, sl], wr_ref[h]) + br[:, sl])
        ig = jax.nn.sigmoid(_dot(xcb[:, sl], wi_ref[h]) + bi[:, sl])
        log_a = (-LRU_C) * r * sp_lam[:, sl]
        a = jnp.exp(log_a)
        mult = jnp.sqrt(-jnp.tanh(log_a) * (a * a + 1.0))
        a_s[:, sl] = a
        b_s[:, sl] = xc[:, sl] * ig * mult
        if r_s is not None:
            r_s[:, sl] = r
            i_s[:, sl] = ig
            m_s[:, sl] = mult


def _conv_fwd(xa, prev8, cw, cb):
    xc = cb + cw[0:1, :] * xa
    for k in range(1, CONV_K):
        xc = xc + cw[k:k + 1, :] * _rows_shifted(prev8, xa, k)
    return xc


def _branch_a_fwd(z, conv_w, conv_b, w_r, b_r, w_i, b_i, lam, comm=None):
    s = z.shape[0]
    ta = min(T_BRANCH_A, s)
    per16 = ta // 16

    def body(xa_ref, xp_ref, ga_ref, cw_ref, cb_ref, wr_ref, br_ref, wi_ref, bi_ref, lam_ref,
             ya_ref, hs_ref, a_s, b_s, h_s, carry_s):
        i = pl.program_id(0)

        @pl.when(i == 0)
        def _():
            carry_s[...] = jnp.zeros_like(carry_s)

        xa = xa_ref[...].astype(F32)
        prev8 = jnp.where(i > 0, xp_ref[...].astype(F32)[8:16], 0.0)
        xc = _conv_fwd(xa, prev8, cw_ref[...], cb_ref[...])
        sp_lam = _softplus(-lam_ref[...])
        _lru_gates(xc, xc.astype(BF16), wr_ref, br_ref[...], wi_ref, bi_ref[...], sp_lam, a_s, b_s)

        row = lax.broadcasted_iota(jnp.int32, (8, D), 0)

        def group(g, carry):
            off = pl.multiple_of(g * 8, 8)
            a8 = a_s[pl.ds(off, 8), :]
            b8 = b_s[pl.ds(off, 8), :]
            for d in (1, 2, 4):
                a_sh = jnp.where(row >= d, pltpu.roll(a8, d, 0), 1.0)
                b_sh = jnp.where(row >= d, pltpu.roll(b8, d, 0), 0.0)
                b8 = a8 * b_sh + b8
                a8 = a8 * a_sh
            h8 = b8 + a8 * carry
            h_s[pl.ds(off, 8), :] = h8
            return jnp.broadcast_to(h8[7:8, :], (8, D))

        carry_s[...] = lax.fori_loop(0, ta // 8, group, carry_s[...])
        hs = h_s[...]
        hs_ref[...] = hs.astype(BF16)
        ya_ref[...] = (hs * _gelu(ga_ref[...].astype(F32))).astype(BF16)

    vec = pl.BlockSpec((1, D), lambda i: (0, 0))
    gate = pl.BlockSpec((HEADS, HEAD_DIM, HEAD_DIM), lambda i: (0, 0, 0))
    return _call(
        body, name="branch_a_fwd", grid=(s // ta,),
        in_specs=[pl.BlockSpec((ta, D), lambda i: (i, 0)),
                  pl.BlockSpec((16, D), lambda i: (jnp.maximum(i * per16 - 1, 0), 0)),
                  pl.BlockSpec((ta, D), lambda i: (i, 1)),
                  pl.BlockSpec((CONV_K, D), lambda i: (0, 0)), vec, gate, vec, gate, vec, vec],
        out_specs=[pl.BlockSpec((ta, D), lambda i: (i, 0)), pl.BlockSpec((ta, D), lambda i: (i, 0))],
        out_shape=[SDS((s, D), BF16), SDS((s, D), BF16)],
        scratch_shapes=[pltpu.VMEM((ta, D), F32), pltpu.VMEM((ta, D), F32), pltpu.VMEM((ta, D), F32),
                        pltpu.VMEM((8, D), F32)],
        params=_cparams(("arbitrary",), 40), args=(z, z, z, conv_w, conv_b, w_r, b_r, w_i, b_i, lam), comm=comm)


def _sgu_common(ub, vb, lg, lb, with_grad):
    if with_grad:
        u, du = _gelu_and_grad(ub)
        v, dv = _gelu_and_grad(vb)
    else:
        u, v, du, dv = _gelu(ub), _gelu(vb), None, None
    mu = jnp.mean(v, axis=-1, keepdims=True)
    vc = v - mu
    rstd = lax.rsqrt(jnp.mean(vc * vc, axis=-1, keepdims=True) + LN_EPS)
    vhat = vc * rstd
    vln = vhat * lg + lb
    return u, du, dv, rstd, vhat, vln


def _masked_ws(ws_ref):
    t = lax.broadcasted_iota(jnp.int32, (CHUNK, CHUNK), 0)
    c = lax.broadcasted_iota(jnp.int32, (CHUNK, CHUNK), 1)
    keep = c <= t
    return [jnp.where(keep, ws_ref[g], 0.0).astype(BF16) for g in range(GROUPS)]


def _branch_b_fwd(z, ln_g, ln_b, w_s, b_s_t, comm=None):
    s = z.shape[0]
    tb = min(T_BRANCH_B, s)

    def body(ub_ref, vb_ref, lg_ref, lb_ref, ws_ref, bs_ref, yb_ref):
        u, _, _, _, _, vln = _sgu_common(ub_ref[...].astype(F32), vb_ref[...].astype(F32),
                                         lg_ref[...], lb_ref[...], False)
        vlnb = vln.astype(BF16)
        wm = _masked_ws(ws_ref)
        bs = bs_ref[...]
        for c in range(tb // CHUNK):
            rs = slice(c * CHUNK, (c + 1) * CHUNK)
            for g in range(GROUPS):
                cs = slice(g * GROUP_DIM, (g + 1) * GROUP_DIM)
                sp = _dot(wm[g], vlnb[rs, cs]) + bs[:, g:g + 1]
                yb_ref[rs, cs] = (u[rs, cs] * sp).astype(BF16)

    vec = pl.BlockSpec((1, D), lambda i: (0, 0))
    return _call(
        body, name="branch_b_fwd", grid=(s // tb,),
        in_specs=[pl.BlockSpec((tb, D), lambda i: (i, 2)), pl.BlockSpec((tb, D), lambda i: (i, 3)), vec, vec,
                  pl.BlockSpec((GROUPS, CHUNK, CHUNK), lambda i: (0, 0, 0)),
                  pl.BlockSpec((CHUNK, GROUPS), lambda i: (0, 0))],
        out_specs=[pl.BlockSpec((tb, D), lambda i: (i, 0))],
        out_shape=[SDS((s, D), BF16)], scratch_shapes=[],
        params=_cparams(("arbitrary",), 40), args=(z, z, ln_g, ln_b, w_s, b_s_t), comm=comm)


def _merge_out(ya, yb, z, x, w_oa, w_ob, w_out, comm=None):
    s = x.shape[0]
    tm = min(TM_MERGE, s)

    def body(ya_ref, yb_ref, ma_ref, mb_ref, x_ref, woa_ref, wob_ref, wo_ref, pa_ref, pb_ref, mg_ref, h1_ref):
        pa = _dot(ya_ref[...], woa_ref[...])
        pb = _dot(yb_ref[...], wob_ref[...])
        merged = (jax.nn.sigmoid(ma_ref[...].astype(F32)) * pa
                  + jax.nn.sigmoid(mb_ref[...].astype(F32)) * pb).astype(BF16)
        pa_ref[...] = pa.astype(BF16)
        pb_ref[...] = pb.astype(BF16)
        mg_ref[...] = merged
        h1_ref[...] = x_ref[...] + _dot(merged, wo_ref[...])

    row = pl.BlockSpec((tm, D), lambda i: (i, 0))
    wsp = pl.BlockSpec((D, D), lambda i: (0, 0))
    return _call(
        body, name="merge_out", grid=(s // tm,),
        in_specs=[row, row, pl.BlockSpec((tm, D), lambda i: (i, 4)), pl.BlockSpec((tm, D), lambda i: (i, 5)),
                  row, wsp, wsp, wsp],
        out_specs=[row, row, row, row],
        out_shape=[SDS((s, D), BF16), SDS((s, D), BF16), SDS((s, D), BF16), SDS((s, D), F32)], scratch_shapes=[],
        params=_cparams(("arbitrary",), 48), args=(ya, yb, z, z, x, w_oa, w_ob, w_out), comm=comm)


def _mlp_fwd(h1, g_mlp, w_up_g, w_down, g_fin, tgt):
    s = h1.shape[0]
    tm = min(TM_ROWS, s)
    nj = N_SLOT

    def body(h1_ref, gm_ref, wu_ref, wd_ref, gf_ref, t_ref, r_ref, at_ref, n2t_ref, dh2_ref, loss_ref, dgf_ref,
             n2_s, acc_s):
        i, j = pl.program_id(0), pl.program_id(1)

        @pl.when(j == 0)
        def _():
            hv = h1_ref[...]
            rstd = lax.rsqrt(jnp.mean(hv * hv, axis=-1, keepdims=True) + NORM_EPS)
            nb = (hv * rstd * gm_ref[...]).astype(BF16)
            n2_s[...] = nb
            n2t_ref[...] = nb.T
            acc_s[...] = jnp.zeros_like(acc_s)

        @pl.when((i == 0) & (j == 0))
        def _():
            loss_ref[...] = jnp.zeros_like(loss_ref)
            dgf_ref[...] = jnp.zeros_like(dgf_ref)

        r = jnp.maximum(_dot(n2_s[...], wu_ref[...]), 0.0)
        r_ref[...] = r.astype(BF16)
        act = (r * r).astype(BF16)
        at_ref[...] = act.T
        acc_s[...] += _dot(act, wd_ref[...])

        @pl.when(j == nj - 1)
        def _():
            h2 = h1_ref[...] + acc_s[...]
            rstd = lax.rsqrt(jnp.mean(h2 * h2, axis=-1, keepdims=True) + NORM_EPS)
            hh = h2 * rstd
            gf = gf_ref[...]
            e = hh * gf - t_ref[...]
            loss_ref[...] += jnp.sum(e * e) * (0.5 / D)
            dy = e * (1.0 / D)
            dgf_ref[...] += jnp.sum(dy * hh, axis=0, keepdims=True)
            dhh = dy * gf
            dh2_ref[...] = rstd * (dhh - hh * jnp.mean(dhh * hh, axis=-1, keepdims=True))

    row = pl.BlockSpec((tm, D), lambda i, j: (i, 0))
    vec = pl.BlockSpec((1, D), lambda i, j: (0, 0))
    return pl.pallas_call(
        body, name="mlp_fwd", grid=(s // tm, nj),
        in_specs=[row, vec, pl.BlockSpec((None, D, FF_COLS), lambda i, j: (j, 0, 0)),
                  pl.BlockSpec((FF_COLS, D), lambda i, j: (j, 0)), vec, row],
        out_specs=[pl.BlockSpec((tm, FF_COLS), lambda i, j: (i, j)), pl.BlockSpec((FF_COLS, tm), lambda i, j: (j, i)),
                   pl.BlockSpec((D, tm), lambda i, j: (0, i)), row, pl.BlockSpec((8, 128), lambda i, j: (0, 0)), vec],
        out_shape=[SDS((s, nj * FF_COLS), BF16), SDS((nj * FF_COLS, s), BF16), SDS((D, s), BF16), SDS((s, D), F32),
                   SDS((8, 128), F32), SDS((1, D), F32)],
        scratch_shapes=[pltpu.VMEM((tm, D), BF16), pltpu.VMEM((tm, D), F32)],
        compiler_params=_cparams(("arbitrary", "arbitrary"), 52),
    )(h1, _small_in_hbm(g_mlp), w_up_g, w_down, _small_in_hbm(g_fin), tgt)


def _mlp_bwd(dh2, r, w_down, w_up_g, h1, g_mlp, comm=None):
    s = h1.shape[0]
    tm = min(TM_ROWS, s)
    nj = N_SLOT

    def body(dh2_ref, r_ref, wd_ref, wu_ref, h1_ref, gm_ref, df_ref, dh1_ref, dgm_ref, dh2b_s, acc_s):
        i, j = pl.program_id(0), pl.program_id(1)

        @pl.when(j == 0)
        def _():
            dh2b_s[...] = dh2_ref[...].astype(BF16)
            acc_s[...] = jnp.zeros_like(acc_s)

        @pl.when((i == 0) & (j == 0))
        def _():
            dgm_ref[...] = jnp.zeros_like(dgm_ref)

        d_act = _dot_nt(dh2b_s[...], wd_ref[...])
        df = (d_act * (2.0 * r_ref[...].astype(F32))).astype(BF16)
        df_ref[...] = df
        acc_s[...] += _dot_nt(df, wu_ref[...])

        @pl.when(j == nj - 1)
        def _():
            hv = h1_ref[...]
            rstd = lax.rsqrt(jnp.mean(hv * hv, axis=-1, keepdims=True) + NORM_EPS)
            hh = hv * rstd
            dn2 = acc_s[...]
            dgm_ref[...] += jnp.sum(dn2 * hh, axis=0, keepdims=True)
            dhat = dn2 * gm_ref[...]
            dh1_ref[...] = dh2_ref[...] + rstd * (dhat - hh * jnp.mean(dhat * hh, axis=-1, keepdims=True))

    row = pl.BlockSpec((tm, D), lambda i, j: (i, 0))
    vec = pl.BlockSpec((1, D), lambda i, j: (0, 0))
    ffb = pl.BlockSpec((tm, FF_COLS), lambda i, j: (i, j))
    return _call(
        body, name="mlp_bwd", grid=(s // tm, nj),
        in_specs=[row, ffb, pl.BlockSpec((FF_COLS, D), lambda i, j: (j, 0)),
                  pl.BlockSpec((None, D, FF_COLS), lambda i, j: (j, 0, 0)), row, vec],
        out_specs=[ffb, row, vec],
        out_shape=[SDS((s, nj * FF_COLS), BF16), SDS((s, D), F32), SDS((1, D), F32)],
        scratch_shapes=[pltpu.VMEM((tm, D), BF16), pltpu.VMEM((tm, D), F32)],
        params=_cparams(("arbitrary", "arbitrary"), 52), args=(dh2, r, w_down, w_up_g, h1, g_mlp), comm=comm)


def _merge_bwd(dh1, z, pa, pb, w_out, w_oa, w_ob, comm=None):
    s = dh1.shape[0]
    tm = min(TM_MERGE, s)

    def body(dh1_ref, ma_ref, mb_ref, pa_ref, pb_ref, wo_ref, woa_ref, wob_ref,
             dz_ref, dpa_ref, dpb_ref, dya_ref, dyb_ref):
        dm = _dot_nt(dh1_ref[...].astype(BF16), wo_ref[...])
        sa = jax.nn.sigmoid(ma_ref[...].astype(F32))
        sb = jax.nn.sigmoid(mb_ref[...].astype(F32))
        dpa = (dm * sa).astype(BF16)
        dpb = (dm * sb).astype(BF16)
        dz_ref[:, 0:D] = (dm * pa_ref[...].astype(F32) * sa * (1.0 - sa)).astype(BF16)
        dz_ref[:, D:2 * D] = (dm * pb_ref[...].astype(F32) * sb * (1.0 - sb)).astype(BF16)
        dpa_ref[...] = dpa
        dpb_ref[...] = dpb
        dya_ref[...] = _dot_nt(dpa, woa_ref[...]).astype(BF16)
        dyb_ref[...] = _dot_nt(dpb, wob_ref[...]).astype(BF16)

    row = pl.BlockSpec((tm, D), lambda i: (i, 0))
    wsp = pl.BlockSpec((D, D), lambda i: (0, 0))
    return _call(
        body, name="merge_bwd", grid=(s // tm,),
        in_specs=[row, pl.BlockSpec((tm, D), lambda i: (i, 4)), pl.BlockSpec((tm, D), lambda i: (i, 5)),
                  row, row, wsp, wsp, wsp],
        out_specs=[pl.BlockSpec((tm, 2 * D), lambda i: (i, 2)), row, row, row, row],
        out_shape=[SDS((s, 6 * D), BF16)] + [SDS((s, D), BF16)] * 4, scratch_shapes=[],
        params=_cparams(("arbitrary",), 48), args=(dh1, z, z, pa, pb, w_out, w_oa, w_ob), comm=comm)


def _branch_b_bwd(dz, dyb, z, ln_g, ln_b, w_s, b_s_t, comm=None):
    s = z.shape[0]
    tb = min(T_BRANCH_B, s)

    def body(dz_in, dyb_ref, ub_ref, vb_ref, lg_ref, lb_ref, ws_ref, bs_ref,
             dz_ref, dws_ref, dbs_ref, dln_ref, du_s, dvln_s):
        del dz_in

        @pl.when(pl.program_id(0) == 0)
        def _():
            dws_ref[...] = jnp.zeros_like(dws_ref)
            dbs_ref[...] = jnp.zeros_like(dbs_ref)
            dln_ref[...] = jnp.zeros_like(dln_ref)

        lg = lg_ref[...]
        u, du, dv, rstd, vhat, vln = _sgu_common(ub_ref[...].astype(F32), vb_ref[...].astype(F32),
                                                 lg, lb_ref[...], True)
        vlnb = vln.astype(BF16)
        dyb_v = dyb_ref[...].astype(F32)
        wm = _masked_ws(ws_ref)
        keep = (lax.broadcasted_iota(jnp.int32, (CHUNK, CHUNK), 1)
                <= lax.broadcasted_iota(jnp.int32, (CHUNK, CHUNK), 0))
        bs = bs_ref[...]
        for c in range(tb // CHUNK):
            rs = slice(c * CHUNK, (c + 1) * CHUNK)
            for g in range(GROUPS):
                cs = slice(g * GROUP_DIM, (g + 1) * GROUP_DIM)
                v_blk = vlnb[rs, cs]
                sp = _dot(wm[g], v_blk) + bs[:, g:g + 1]
                d_sp = dyb_v[rs, cs] * u[rs, cs]
                d_spb = d_sp.astype(BF16)
                du_s[rs, cs] = dyb_v[rs, cs] * sp
                dvln_s[rs, cs] = _dot_tn(wm[g], d_spb)
                dws_ref[g] += jnp.where(keep, _dot_nt(d_spb, v_blk), 0.0)
                dbs_ref[g] += jnp.broadcast_to(jnp.sum(d_sp, axis=-1, keepdims=True), (CHUNK, CHUNK))
        dvln = dvln_s[...]
        dln_ref[0:1, :] += jnp.sum(dvln * vhat, axis=0, keepdims=True)
        dln_ref[1:2, :] += jnp.sum(dvln, axis=0, keepdims=True)
        dvh = dvln * lg
        d_v = rstd * (dvh - jnp.mean(dvh, axis=-1, keepdims=True)
                      - vhat * jnp.mean(dvh * vhat, axis=-1, keepdims=True))
        dz_ref[:, 0:D] = (du_s[...] * du).astype(BF16)
        dz_ref[:, D:2 * D] = (d_v * dv).astype(BF16)

    vec = pl.BlockSpec((1, D), lambda i: (0, 0))
    sq = pl.BlockSpec((GROUPS, CHUNK, CHUNK), lambda i: (0, 0, 0))
    return _call(
        body, name="branch_b_bwd", grid=(s // tb,),
        in_specs=[ANY, pl.BlockSpec((tb, D), lambda i: (i, 0)),
                  pl.BlockSpec((tb, D), lambda i: (i, 2)), pl.BlockSpec((tb, D), lambda i: (i, 3)), vec, vec, sq,
                  pl.BlockSpec((CHUNK, GROUPS), lambda i: (0, 0))],
        out_specs=[pl.BlockSpec((tb, 2 * D), lambda i: (i, 1)), sq, sq, pl.BlockSpec((8, D), lambda i: (0, 0))],
        out_shape=[SDS(dz.shape, BF16), SDS((GROUPS, CHUNK, CHUNK), F32), SDS((GROUPS, CHUNK, CHUNK), F32),
                   SDS((8, D), F32)],
        scratch_shapes=[pltpu.VMEM((tb, D), F32), pltpu.VMEM((tb, D), F32)], aliases={0: 0},
        params=_cparams(("arbitrary",), 40), args=(dz, dyb, z, z, ln_g, ln_b, w_s, b_s_t), comm=comm)


def _branch_a_bwd(dz, dya, z, hs, conv_w, conv_b, w_r, b_r, w_i, b_i, lam, comm=None):
    s = z.shape[0]
    ta = min(T_BRANCH_A, s)
    nb = s // ta
    per16 = ta // 16

    def body(dz_in, dya_ref, xa_ref, xp_ref, ga_ref, hs_ref, hp_ref, cw_ref, cb_ref, wr_ref, br_ref, wi_ref,
             bi_ref, lam_ref, dz_ref, vec_ref, dwr_ref, dwi_ref,
             a_s, b_s, h_s, r_s, i_s, m_s, dcar_s, acar_s, dxc_s):
        del dz_in
        i = pl.program_id(0)
        blk = nb - 1 - i

        @pl.when(i == 0)
        def _():
            dcar_s[...] = jnp.zeros_like(dcar_s)
            acar_s[...] = jnp.zeros_like(acar_s)
            dxc_s[...] = jnp.zeros_like(dxc_s)
            vec_ref[...] = jnp.zeros_like(vec_ref)
            dwr_ref[...] = jnp.zeros_like(dwr_ref)
            dwi_ref[...] = jnp.zeros_like(dwi_ref)

        cw = cw_ref[...]
        lam_v = lam_ref[...]
        xa = xa_ref[...].astype(F32)
        prev8 = jnp.where(blk > 0, xp_ref[...].astype(F32)[8:16], 0.0)
        xc = _conv_fwd(xa, prev8, cw, cb_ref[...])
        xcb = xc.astype(BF16)
        sp_lam = _softplus(-lam_v)
        _lru_gates(xc, xcb, wr_ref, br_ref[...], wi_ref, bi_ref[...], sp_lam, a_s, b_s, r_s, i_s, m_s)

        hs_v = hs_ref[...].astype(F32)
        hprev8 = jnp.where(blk > 0, hp_ref[...].astype(F32)[8:16], 0.0)
        h_m1 = _rows_shifted(hprev8, hs_v, 1)
        gg, dgg = _gelu_and_grad(ga_ref[...].astype(F32))
        dya_v = dya_ref[...].astype(F32)
        dz_ref[:, D:2 * D] = (dya_v * hs_v * dgg).astype(BF16)

        a_v = a_s[...]
        a_s[...] = _rows_advanced(a_v, acar_s[...], 1)
        b_s[...] = dya_v * gg

        row = lax.broadcasted_iota(jnp.int32, (8, D), 0)
        ng = ta // 8

        def group(gi, carry):
            off = pl.multiple_of((ng - 1 - gi) * 8, 8)
            c8 = a_s[pl.ds(off, 8), :]
            d8 = b_s[pl.ds(off, 8), :]
            for d in (1, 2, 4):
                c_sh = jnp.where(row < 8 - d, pltpu.roll(c8, 8 - d, 0), 1.0)
                d_sh = jnp.where(row < 8 - d, pltpu.roll(d8, 8 - d, 0), 0.0)
                d8 = c8 * d_sh + d8
                c8 = c8 * c_sh
            dh8 = d8 + c8 * carry
            h_s[pl.ds(off, 8), :] = dh8
            return jnp.broadcast_to(dh8[0:1, :], (8, D))

        dcar_s[...] = lax.fori_loop(0, ng, group, dcar_s[...])
        acar_s[...] = jnp.broadcast_to(a_v[0:1, :], (8, D))

        dbx = h_s[...]
        r_v, i_v, m_v = r_s[...], i_s[...], m_s[...]
        d_mult = dbx * xc * i_v
        d_loga = dbx * h_m1 * a_v - d_mult * (a_v * a_v) / m_v
        d_pr = d_loga * ((-LRU_C) * sp_lam) * r_v * (1.0 - r_v)
        d_pi = dbx * xc * m_v * i_v * (1.0 - i_v)
        vec_ref[7:8, :] += jnp.sum(d_loga * r_v, axis=0, keepdims=True) * (LRU_C * jax.nn.sigmoid(-lam_v))
        vec_ref[5:6, :] += jnp.sum(d_pr, axis=0, keepdims=True)
        vec_ref[6:7, :] += jnp.sum(d_pi, axis=0, keepdims=True)
        d_prb = d_pr.astype(BF16)
        d_pib = d_pi.astype(BF16)
        h_s[...] = dbx * i_v * m_v
        for h in range(HEADS):
            sl = slice(h * HEAD_DIM, (h + 1) * HEAD_DIM)
            h_s[:, sl] += _dot_nt(d_prb[:, sl], wr_ref[h]) + _dot_nt(d_pib[:, sl], wi_ref[h])
            dwr_ref[h] += _dot_tn(xcb[:, sl], d_prb[:, sl])
            dwi_ref[h] += _dot_tn(xcb[:, sl], d_pib[:, sl])
        d_xc = h_s[...]
        vec_ref[4:5, :] += jnp.sum(d_xc, axis=0, keepdims=True)
        vec_ref[0:1, :] += jnp.sum(d_xc * xa, axis=0, keepdims=True)
        d_xa = cw[0:1, :] * d_xc
        nxt = dxc_s[...]
        for k in range(1, CONV_K):
            vec_ref[k:k + 1, :] += jnp.sum(d_xc * _rows_shifted(prev8, xa, k), axis=0, keepdims=True)
            d_xa = d_xa + cw[k:k + 1, :] * _rows_advanced(d_xc, nxt, k)
        dz_ref[:, 0:D] = d_xa.astype(BF16)
        dxc_s[...] = d_xc[0:8, :]

    vec = pl.BlockSpec((1, D), lambda i: (0, 0))
    gate = pl.BlockSpec((HEADS, HEAD_DIM, HEAD_DIM), lambda i: (0, 0, 0))
    cur = lambda c: pl.BlockSpec((ta, D), lambda i: (nb - 1 - i, c))
    before = lambda c: pl.BlockSpec((16, D), lambda i: (jnp.maximum((nb - 1 - i) * per16 - 1, 0), c))
    return _call(
        body, name="branch_a_bwd", grid=(nb,),
        in_specs=[ANY, cur(0), cur(0), before(0), cur(1), cur(0), before(0),
                  pl.BlockSpec((CONV_K, D), lambda i: (0, 0)), vec, gate, vec, gate, vec, vec],
        out_specs=[pl.BlockSpec((ta, 2 * D), lambda i: (nb - 1 - i, 0)), pl.BlockSpec((8, D), lambda i: (0, 0)),
                   gate, gate],
        out_shape=[SDS(dz.shape, BF16), SDS((8, D), F32), SDS((HEADS, HEAD_DIM, HEAD_DIM), F32),
                   SDS((HEADS, HEAD_DIM, HEAD_DIM), F32)],
        scratch_shapes=[pltpu.VMEM((ta, D), F32)] * 6 + [pltpu.VMEM((8, D), F32)] * 3, aliases={0: 0},
        params=_cparams(("arbitrary",), 48),
        args=(dz, dya, z, z, z, hs, hs, conv_w, conv_b, w_r, b_r, w_i, b_i, lam), comm=comm)


def _in_bwd(dz, w_in_g, x, dh1, g_mix, comm=None):
    s = x.shape[0]
    tm = min(TM_ROWS, s)
    nj = N_SLOT

    def body(dz_ref, w_ref, x_ref, dh1_ref, g_ref, dx_ref, dg_ref, acc_s):
        i, j = pl.program_id(0), pl.program_id(1)

        @pl.when(j == 0)
        def _():
            acc_s[...] = jnp.zeros_like(acc_s)

        @pl.when((i == 0) & (j == 0))
        def _():
            dg_ref[...] = jnp.zeros_like(dg_ref)

        acc_s[...] += _dot_nt(dz_ref[...], w_ref[...])

        @pl.when(j == nj - 1)
        def _():
            xv = x_ref[...]
            rstd = lax.rsqrt(jnp.mean(xv * xv, axis=-1, keepdims=True) + NORM_EPS)
            xh = xv * rstd
            dn = acc_s[...]
            dg_ref[...] += jnp.sum(dn * xh, axis=0, keepdims=True)
            dhat = dn * g_ref[...]
            dx_ref[...] = dh1_ref[...] + rstd * (dhat - xh * jnp.mean(dhat * xh, axis=-1, keepdims=True))

    row = pl.BlockSpec((tm, D), lambda i, j: (i, 0))
    vec = pl.BlockSpec((1, D), lambda i, j: (0, 0))
    return _call(
        body, name="in_bwd", grid=(s // tm, nj),
        in_specs=[pl.BlockSpec((tm, W_IN_COLS), lambda i, j: (i, j)),
                  pl.BlockSpec((None, D, W_IN_COLS), lambda i, j: (j, 0, 0)), row, row, vec],
        out_specs=[row, vec],
        out_shape=[SDS((s, D), F32), SDS((1, D), F32)],
        scratch_shapes=[pltpu.VMEM((tm, D), F32)],
        params=_cparams(("arbitrary", "arbitrary"), 48), args=(dz, w_in_g, x, dh1, g_mix), comm=comm)


def _wgrad(name, a, b, nblk, a_split, b_split):
    s = a.shape[0]
    ts = min(TM_ROWS, s)
    a_w = a.shape[1] // nblk if a_split else a.shape[1]
    b_w = b.shape[1] // nblk if b_split else b.shape[1]

    def body(a_ref, b_ref, o_ref, acc_s):
        t = pl.program_id(1)

        @pl.when(t == 0)
        def _():
            acc_s[...] = jnp.zeros_like(acc_s)

        acc_s[...] += _dot_tn(a_ref[...].astype(BF16), b_ref[...].astype(BF16))

        @pl.when(t == pl.num_programs(1) - 1)
        def _():
            o_ref[...] = acc_s[...].astype(BF16)

    return pl.pallas_call(
        body, name=name, grid=(nblk, s // ts),
        in_specs=[pl.BlockSpec((ts, a_w), (lambda k, t: (t, k)) if a_split else (lambda k, t: (t, 0))),
                  pl.BlockSpec((ts, b_w), (lambda k, t: (t, k)) if b_split else (lambda k, t: (t, 0)))],
        out_specs=pl.BlockSpec((None, a_w, b_w), lambda k, t: (k, 0, 0)),
        out_shape=SDS((nblk, a_w, b_w), BF16),
        scratch_shapes=[pltpu.VMEM((a_w, b_w), F32)],
        compiler_params=_cparams(("arbitrary", "arbitrary"), 48),
    )(a, b)


def _wgrad_t(name, a_t, b, nblk, a_split, b_split, tokens):
    s = b.shape[0]
    ts = min(tokens, s)
    a_w = a_t.shape[0] // nblk if a_split else a_t.shape[0]
    b_w = b.shape[1] // nblk if b_split else b.shape[1]

    def body(a_ref, b_ref, o_ref, acc_s):
        t = pl.program_id(1)

        @pl.when(t == 0)
        def _():
            acc_s[...] = jnp.zeros_like(acc_s)

        acc_s[...] += _dot(a_ref[...], b_ref[...].astype(BF16))

        @pl.when(t == pl.num_programs(1) - 1)
        def _():
            o_ref[...] = acc_s[...].astype(BF16)

    return pl.pallas_call(
        body, name=name, grid=(nblk, s // ts),
        in_specs=[pl.BlockSpec((a_w, ts), (lambda k, t: (k, t)) if a_split else (lambda k, t: (0, t))),
                  pl.BlockSpec((ts, b_w), (lambda k, t: (t, k)) if b_split else (lambda k, t: (t, 0)))],
        out_specs=pl.BlockSpec((None, a_w, b_w), lambda k, t: (k, 0, 0)),
        out_shape=SDS((nblk, a_w, b_w), BF16),
        scratch_shapes=[pltpu.VMEM((a_w, b_w), F32)],
        compiler_params=_cparams(("arbitrary", "arbitrary"), 48),
    )(a_t, b)


def _place():
    x, y, c = lax.axis_index("x"), lax.axis_index("y"), lax.axis_index("c")
    return x, y, c


def _other_chips(x, y):
    return [(x, 1 - y, 2 * x + 1 - y), (1 - x, y, 2 * (1 - x) + y), (1 - x, 1 - y, 2 * (1 - x) + 1 - y)]


class _Plan:
    def __init__(self, arrays, out_shape, sems, start, finish):
        self.arrays, self.out_shape, self.sems, self.start, self.finish = arrays, out_shape, sems, start, finish


def _gather_plan(shards):
    n = len(shards)

    def copies(ins, outs, sems):
        send_sems, recv_sems, local_sems = sems
        x, y, c = _place()
        chip = 2 * x + y
        me = 2 * chip + c
        sib = (x, y, 1 - c)
        chips = _other_chips(x, y)

        def rc(k, t, src, blk, to):
            return pltpu.make_async_remote_copy(
                src_ref=src, dst_ref=outs[t].at[blk], send_sem=send_sems.at[k * n + t],
                recv_sem=recv_sems.at[k * n + t], device_id=to, device_id_type=MESH)

        local = [pltpu.make_async_copy(ins[t], outs[t].at[me], local_sems.at[t]) for t in range(n)]
        sends = [rc(0, t, ins[t], me, sib) for t in range(n)]
        for j, (px, py, _) in enumerate(chips):
            sends += [rc(1 + j, t, ins[t], me, (px, py, c)) for t in range(n)]
        return rc, local, sends, chips, chip, c, sib

    def start(ins, outs, sems):
        _, local, sends, _, _, _, _ = copies(ins, outs, sems)
        for cp in local + sends:
            cp.start()

    def finish(ins, outs, sems):
        rc, local, sends, chips, chip, c, sib = copies(ins, outs, sems)
        passed = []
        for j, (px, py, pc) in enumerate(chips):
            blk = 2 * pc + c
            for t in range(n):
                rc(1 + j, t, ins[t], blk, sib).wait_recv()
            for t in range(n):
                cp = rc(4 + j, t, outs[t].at[blk], blk, sib)
                cp.start()
                passed.append(cp)
        for t in range(n):
            rc(0, t, ins[t], 2 * chip + 1 - c, sib).wait_recv()
        for j, (px, py, pc) in enumerate(chips):
            for t in range(n):
                rc(4 + j, t, ins[t], 2 * pc + 1 - c, sib).wait_recv()
        for cp in sends + passed:
            cp.wait_send()
        for cp in local:
            cp.wait()

    return _Plan(list(shards), [SDS((N_SLOT,) + tuple(a.shape), a.dtype) for a in shards],
                 [pltpu.SemaphoreType.DMA((7 * n,)), pltpu.SemaphoreType.DMA((7 * n,)),
                  pltpu.SemaphoreType.DMA((n,))], start, finish)


def _sibling_plan(grads, whole=()):
    n, m = len(grads), len(whole)

    def copies(ins, outs, sems):
        send_sems, recv_sems = sems
        x, y, c = _place()
        sib = (x, y, 1 - c)

        def rc(t, src, dst):
            return pltpu.make_async_remote_copy(src_ref=src, dst_ref=dst, send_sem=send_sems.at[t],
                                                recv_sem=recv_sems.at[t], device_id=sib, device_id_type=MESH)
        return rc, c

    def start(ins, outs, sems):
        rc, c = copies(ins, outs, sems)
        for t in range(n):
            for j in range(4):
                rc(t, ins[t].at[2 * j + 1 - c], outs[t].at[j]).start()
        for t in range(n, n + m):
            rc(t, ins[t], outs[t]).start()

    def finish(ins, outs, sems):
        rc, _ = copies(ins, outs, sems)
        for t in range(n):
            rc(t, ins[t].at[pl.ds(0, 4)], outs[t]).wait()
        for t in range(n, n + m):
            rc(t, ins[t], outs[t]).wait()

    return _Plan(list(grads) + list(whole),
                 [SDS((4,) + tuple(g.shape[1:]), g.dtype) for g in grads] + [SDS(a.shape, a.dtype) for a in whole],
                 [pltpu.SemaphoreType.DMA((n + m,)), pltpu.SemaphoreType.DMA((n + m,))], start, finish)


def _chips_plan(parts, whole=()):
    n, m = len(parts), len(whole)

    def src_of(ins, t, pc):
        return ins[t].at[pc] if t < n else ins[t]

    def local_copies(ins, outs, sems, chip):
        return [pltpu.make_async_copy(src_of(ins, t, chip), outs[t].at[chip], sems[2].at[t]) for t in range(n + m)]

    def start(ins, outs, sems):
        send_sems, recv_sems, _ = sems
        x, y, c = _place()
        chip = 2 * x + y
        for cp in local_copies(ins, outs, sems, chip):
            cp.start()
        for px, py, pc in _other_chips(x, y):
            for t in range(n + m):
                pltpu.make_async_remote_copy(src_ref=src_of(ins, t, pc), dst_ref=outs[t].at[chip],
                                             send_sem=send_sems.at[t], recv_sem=recv_sems.at[t],
                                             device_id=(px, py, c), device_id_type=MESH).start()

    def finish(ins, outs, sems):
        send_sems, recv_sems, _ = sems
        x, y, c = _place()
        for t in range(n + m):
            three = outs[t].at[pl.ds(0, 3)]
            pltpu.make_async_remote_copy(src_ref=three, dst_ref=three, send_sem=send_sems.at[t],
                                         recv_sem=recv_sems.at[t], device_id=(x, y, c), device_id_type=MESH).wait()
        for cp in local_copies(ins, outs, sems, 2 * x + y):
            cp.wait()

    return _Plan(list(parts) + list(whole),
                 [SDS(p.shape, p.dtype) for p in parts] + [SDS((4,) + tuple(a.shape), a.dtype) for a in whole],
                 [pltpu.SemaphoreType.DMA((n + m,)), pltpu.SemaphoreType.DMA((n + m,)),
                  pltpu.SemaphoreType.DMA((n + m,))], start, finish)


def _exchange_plan(arr):
    def peers(x, y, c):
        flip = lambda v, f: 1 - v if f else v
        return [(flip(x, fx), flip(y, fy), flip(c, fc))
                for fx in (0, 1) for fy in (0, 1) for fc in (0, 1) if fx or fy or fc]

    def start(ins, outs, sems):
        x, y, c = _place()
        me = 4 * x + 2 * y + c
        pltpu.make_async_copy(ins[0], outs[0].at[me], sems[2].at[0]).start()
        for to in peers(x, y, c):
            pltpu.make_async_remote_copy(src_ref=ins[0], dst_ref=outs[0].at[me], send_sem=sems[0].at[0],
                                         recv_sem=sems[1].at[0], device_id=to, device_id_type=MESH).start()

    def finish(ins, outs, sems):
        x, y, c = _place()
        seven = outs[0].at[pl.ds(0, 7)]
        pltpu.make_async_remote_copy(src_ref=seven, dst_ref=seven, send_sem=sems[0].at[0], recv_sem=sems[1].at[0],
                                     device_id=(x, y, c), device_id_type=MESH).wait()
        pltpu.make_async_copy(ins[0], outs[0].at[4 * x + 2 * y + c], sems[2].at[0]).wait()

    return _Plan([arr], [SDS((N_SLOT,) + tuple(arr.shape), arr.dtype)],
                 [pltpu.SemaphoreType.DMA((1,)), pltpu.SemaphoreType.DMA((1,)), pltpu.SemaphoreType.DMA((1,))],
                 start, finish)


def _join(*plans):
    def cut(seq, sizes):
        out, at = [], 0
        for k in sizes:
            out.append(seq[at:at + k])
            at += k
        return out

    n_arr = [len(p.arrays) for p in plans]
    n_sem = [len(p.sems) for p in plans]

    def start(ins, outs, sems):
        for p, i, o, s in zip(plans, cut(ins, n_arr), cut(outs, n_arr), cut(sems, n_sem)):
            p.start(i, o, s)

    def finish(ins, outs, sems):
        for p, i, o, s in zip(plans, cut(ins, n_arr), cut(outs, n_arr), cut(sems, n_sem)):
            p.finish(i, o, s)

    return _Plan([a for p in plans for a in p.arrays], [o for p in plans for o in p.out_shape],
                 [s for p in plans for s in p.sems], start, finish)


def _run_plan(name, plan):
    k = len(plan.arrays)

    def body(*refs):
        ins, outs, sems = refs[:k], refs[k:2 * k], refs[2 * k:]
        plan.start(ins, outs, sems)
        plan.finish(ins, outs, sems)

    return pl.pallas_call(
        body, name=name, in_specs=[ANY] * k, out_specs=[ANY] * k, out_shape=plan.out_shape,
        scratch_shapes=plan.sems, compiler_params=pltpu.CompilerParams(has_side_effects=True),
    )(*plan.arrays)


def _call(body, *, name, grid, in_specs, out_specs, out_shape, scratch_shapes, params, args, comm=None,
          aliases=None, prefetch=()):
    aliases = aliases or {}
    n_pre = len(prefetch)

    def launch(fn, ins_specs, outs_specs, outs_shape, scratch, operands):
        spec = pltpu.PrefetchScalarGridSpec(num_scalar_prefetch=n_pre, grid=grid, in_specs=ins_specs,
                                            out_specs=outs_specs, scratch_shapes=scratch)
        return pl.pallas_call(fn, name=name, grid_spec=spec, out_shape=outs_shape, compiler_params=params,
                              input_output_aliases=aliases)(*prefetch, *[_small_in_hbm(a) for a in operands])

    if comm is None:
        return list(launch(body, in_specs, out_specs, out_shape, scratch_shapes, args)), []
    n_in, n_out, n_scr, k = len(in_specs), len(out_specs), len(scratch_shapes), len(comm.arrays)

    def wrapped(*refs):
        pre, refs = refs[:n_pre], refs[n_pre:]
        ins = refs[:n_in]
        c_in = refs[n_in:n_in + k]
        outs = refs[n_in + k:n_in + k + n_out]
        c_out = refs[n_in + k + n_out:n_in + 2 * k + n_out]
        scr = refs[n_in + 2 * k + n_out:n_in + 2 * k + n_out + n_scr]
        sems = refs[n_in + 2 * k + n_out + n_scr:]
        ids = [pl.program_id(d) for d in range(len(grid))]
        first = ids[0] == 0
        last = ids[0] == grid[0] - 1
        for d in range(1, len(grid)):
            first = first & (ids[d] == 0)
            last = last & (ids[d] == grid[d] - 1)

        @pl.when(first)
        def _():
            comm.start(c_in, c_out, sems)

        body(*pre, *ins, *outs, *scr)

        @pl.when(last)
        def _():
            comm.finish(c_in, c_out, sems)

    res = launch(wrapped, list(in_specs) + [ANY] * k, list(out_specs) + [ANY] * k,
                 list(out_shape) + list(comm.out_shape), list(scratch_shapes) + list(comm.sems),
                 tuple(args) + tuple(comm.arrays))
    return list(res[:n_out]), list(res[n_out:])


def _row_tile(rows):
    for t in (512, 256, 128, 64, 32, 16, 8):
        if rows % t == 0:
            return t
    return rows


def _pair_sum(name, g8, recv4, core):
    _, rows, cols = recv4.shape
    tr = _row_tile(rows)
    g42 = g8.reshape(4, 2, rows, cols)

    def body(c_ref, g_ref, r_ref, o_ref):
        del c_ref
        o_ref[...] = (g_ref[...].astype(F32) + r_ref[...].astype(F32)).astype(o_ref.dtype)

    return pl.pallas_call(
        body, name=name,
        grid_spec=pltpu.PrefetchScalarGridSpec(
            num_scalar_prefetch=1, grid=(4, rows // tr),
            in_specs=[pl.BlockSpec((None, None, tr, cols), lambda j, i, c_ref: (j, c_ref[0], i, 0)),
                      pl.BlockSpec((None, tr, cols), lambda j, i, c_ref: (j, i, 0))],
            out_specs=pl.BlockSpec((None, tr, cols), lambda j, i, c_ref: (j, i, 0))),
        out_shape=SDS(recv4.shape, g8.dtype),
        compiler_params=_cparams(("arbitrary", "arbitrary"), 32),
    )(core, g42, recv4)


def _add2(name, a, b):
    rows, cols = a.shape
    tr = _row_tile(rows)

    def body(a_ref, b_ref, o_ref):
        o_ref[...] = a_ref[...] + b_ref[...]

    blk = pl.BlockSpec((tr, cols), lambda i: (i, 0))
    return pl.pallas_call(body, name=name, grid=(rows // tr,), in_specs=[blk, blk], out_specs=blk,
                          out_shape=SDS(a.shape, a.dtype),
                          compiler_params=_cparams(("arbitrary",), 32))(a, b)


def _sum_terms(name, terms):
    k, rows, cols = terms.shape
    tr = _row_tile(rows)

    def body(r_ref, o_ref):
        acc = r_ref[0]
        for q in range(1, k):
            acc = acc + r_ref[q]
        o_ref[...] = acc

    return pl.pallas_call(body, name=name, grid=(rows // tr,),
                          in_specs=[pl.BlockSpec((k, tr, cols), lambda i: (0, i, 0))],
                          out_specs=pl.BlockSpec((tr, cols), lambda i: (i, 0)),
                          out_shape=SDS((rows, cols), terms.dtype),
                          compiler_params=_cparams(("arbitrary",), 32))(terms)


def _adam_update(g, w, m, v):
    c1 = 1.0 / (1.0 - ADAM_B1 ** ADAM_STEP)
    c2 = 1.0 / (1.0 - ADAM_B2 ** ADAM_STEP)
    mn = ADAM_B1 * m + (1.0 - ADAM_B1) * g
    vn = ADAM_B2 * v + (1.0 - ADAM_B2) * (g * g)
    delta = (-ADAM_LR) * ((mn * c1) / (jnp.sqrt(vn * c2) + ADAM_EPS) + ADAM_WD * w)
    return delta, mn, vn


def _adamw_many(name, gs, ws, ms, vs):
    n = len(gs)

    def body(*refs):
        for p in range(n):
            g, w, m, v = (refs[q * n + p][...] for q in range(4))
            d, mn, vn = _adam_update(g, w, m, v)
            refs[4 * n + p][...] = d
            refs[5 * n + p][...] = mn
            refs[6 * n + p][...] = vn

    full = [pl.BlockSpec(memory_space=pltpu.VMEM)] * n
    shapes = [SDS(w.shape, F32) for w in ws]
    res = pl.pallas_call(body, name=name, in_specs=full * 4, out_specs=full * 3, out_shape=shapes * 3,
                         compiler_params=pltpu.CompilerParams(vmem_limit_bytes=32 * MiB))(*gs, *ws, *ms, *vs)
    return [(res[p], res[n + p], res[2 * n + p]) for p in range(n)]


def _adamw(name, terms, w, m, v):
    k, rows, cols = terms.shape
    tr = _row_tile(rows)

    def body(t_ref, w_ref, m_ref, v_ref, g_ref, d_ref, mo_ref, vo_ref):
        g = t_ref[0].astype(F32)
        for q in range(1, k):
            g = g + t_ref[q].astype(F32)
        g_ref[...] = g
        d_ref[...], mo_ref[...], vo_ref[...] = _adam_update(g, w_ref[...], m_ref[...], v_ref[...])

    blk = pl.BlockSpec((tr, cols), lambda i: (i, 0))
    return pl.pallas_call(body, name=name, grid=(rows // tr,),
                          in_specs=[pl.BlockSpec((k, tr, cols), lambda i: (0, i, 0)), blk, blk, blk],
                          out_specs=[blk] * 4, out_shape=[SDS((rows, cols), F32)] * 4,
                          compiler_params=_cparams(("arbitrary",), 40))(terms, w, m, v)


def kernel(x, norm_mix_g, w_in, conv_w, conv_b, w_rgate, b_rgate, w_igate, b_igate, lru_lambda, w_out_a, sgu_ln_g, sgu_ln_b, sgu_w_s, sgu_b_s, w_out_b, w_out, norm_mlp_g, w_up, w_down, norm_final_g, loss_target, m_norm_mix_g, m_w_in, m_conv_w, m_conv_b, m_w_rgate, m_b_rgate, m_w_igate, m_b_igate, m_lru_lambda, m_w_out_a, m_sgu_ln_g, m_sgu_ln_b, m_sgu_w_s, m_sgu_b_s, m_w_out_b, m_w_out, m_norm_mlp_g, m_w_up, m_w_down, m_norm_final_g, v_norm_mix_g, v_w_in, v_conv_w, v_conv_b, v_w_rgate, v_b_rgate, v_w_igate, v_b_igate, v_lru_lambda, v_w_out_a, v_sgu_ln_g, v_sgu_ln_b, v_sgu_w_s, v_sgu_b_s, v_w_out_b, v_w_out, v_norm_mlp_g, v_w_up, v_w_down, v_norm_final_g):
    cx, cy, cc = _place()
    me = 4 * cx + 2 * cy + cc
    core = jnp.reshape(cc, (1,)).astype(jnp.int32)
    xs = x[0]
    tgt = loss_target[0]
    s = xs.shape[0]

    gate_shard = jnp.stack([w_rgate[0], w_igate[0]]).astype(BF16).reshape(2 * HEADS * 32, HEAD_DIM)
    vec_shard = jnp.concatenate([conv_w[0], b_rgate[0], b_igate[0]], axis=1)
    vec_shard = jnp.pad(vec_shard, ((0, 4), (0, 256 - vec_shard.shape[1])))
    shards = [w_in[0].astype(BF16), w_out_a[0].astype(BF16), w_out_b[0].astype(BF16), w_out[0].astype(BF16),
              w_up[0].astype(BF16), w_down[0].astype(BF16), gate_shard, vec_shard]
    (z, n1_t, w_in_g), (gate_g, vec_g) = _in_proj(xs, norm_mix_g, shards[0], _slot_order(cx, cy, cc),
                                                comm=_gather_plan(shards[6:8]))
    gates = gate_g.reshape(N_SLOT, 2, HEADS, 32, HEAD_DIM).transpose(1, 2, 0, 3, 4).reshape(2, HEADS, HEAD_DIM, HEAD_DIM)
    w_r_f, w_i_f = gates[0], gates[1]
    conv_w_f = vec_g[:, 0:4, 0:128].transpose(1, 0, 2).reshape(CONV_K, D)
    b_r_f = vec_g[:, 0:4, 128:160].transpose(1, 0, 2).reshape(1, D)
    b_i_f = vec_g[:, 0:4, 160:192].transpose(1, 0, 2).reshape(1, D)
    b_s_t = jnp.transpose(sgu_b_s[0])

    (ya, hs), (w_oa_g, w_ob_g, w_out_g) = _branch_a_fwd(
        z, conv_w_f, conv_b, w_r_f, b_r_f, w_i_f, b_i_f, lru_lambda, comm=_gather_plan(shards[1:4]))
    w_oa_f = w_oa_g.reshape(D, D)
    w_ob_f = w_ob_g.reshape(D, D)
    w_out_f = w_out_g.reshape(D, D)
    (yb,), (w_up_g,) = _branch_b_fwd(z, sgu_ln_g, sgu_ln_b, sgu_w_s[0], b_s_t, comm=_gather_plan(shards[4:5]))
    (pa, pb, merged, h1), (w_down_g,) = _merge_out(ya, yb, z, xs, w_oa_f, w_ob_f, w_out_f,
                                                   comm=_gather_plan(shards[5:6]))
    w_down_f = w_down_g.reshape(N_SLOT * FF_COLS, D)
    gf2 = norm_final_g.reshape(1, D)
    r_act, act_t, n2_t, dh2, loss_acc, d_gfin = _mlp_fwd(h1, norm_mlp_g, w_up_g, w_down_f, gf2, tgt)

    def pair(names, grads, recv):
        return [_pair_sum("pair_sum_" + nm, g, r, core) for nm, g, r in zip(names, grads, recv)]

    g_down = _wgrad_t("wgrad_down", act_t, dh2, N_SLOT // 2, True, False, 2048)
    g_down = g_down.reshape(N_SLOT, FF_COLS, D)
    (df, dh1, d_gmlp), (r_down,) = _mlp_bwd(dh2, r_act, w_down_f, w_up_g, h1, norm_mlp_g,
                                            comm=_sibling_plan([g_down]))
    (p_down,) = pair(["down"], [g_down], [r_down])
    g_up = _wgrad_t("wgrad_up", n2_t, df, N_SLOT, False, True, 4096)
    g_out = _wgrad("wgrad_out", merged, dh1, 1, False, False).reshape(N_SLOT, D // N_SLOT, D)
    (dz, dpa, dpb, dya, dyb), (got_down, r_up, r_out) = _merge_bwd(
        dh1, z, pa, pb, w_out_f, w_oa_f, w_ob_f, comm=_join(_chips_plan([p_down]), _sibling_plan([g_up, g_out])))
    p_up, p_out = pair(["up", "out"], [g_up, g_out], [r_up, r_out])
    g_oa = _wgrad("wgrad_out_a", ya, dpa, 1, False, False).reshape(N_SLOT, D // N_SLOT, D)
    g_ob = _wgrad("wgrad_out_b", yb, dpb, 1, False, False).reshape(N_SLOT, D // N_SLOT, D)
    (dz, d_ws, d_bs, d_ln), (got_up, r_oa, r_ob) = _branch_b_bwd(
        dz, dyb, z, sgu_ln_g, sgu_ln_b, sgu_w_s[0], b_s_t,
        comm=_join(_chips_plan([p_up]), _sibling_plan([g_oa, g_ob])))
    p_oa, p_ob = pair(["out_a", "out_b"], [g_oa, g_ob], [r_oa, r_ob])
    (dz, d_vec, d_wr, d_wi), (got_out, got_oa, got_ob) = _branch_a_bwd(
        dz, dya, z, hs, conv_w_f, conv_b, w_r_f, b_r_f, w_i_f, b_i_f, lru_lambda,
        comm=_chips_plan([p_out, p_oa, p_ob]))
    g_in = _wgrad_t("wgrad_in", n1_t, dz, N_SLOT, False, True, 4096)
    g_gate = jnp.stack([d_wr, d_wi]).reshape(2, HEADS, N_SLOT, 32, HEAD_DIM).transpose(2, 0, 1, 3, 4)
    g_gate = g_gate.reshape(N_SLOT, 2 * HEADS * 32, HEAD_DIM).astype(BF16)

    d_bs_row = jnp.pad(d_bs[:, :, 0].reshape(1, GROUPS * CHUNK), ((0, 0), (0, D - GROUPS * CHUNK)))
    vecs = jnp.concatenate([d_vec, jnp.concatenate([d_ln[0:2], d_gmlp, d_gfin, d_bs_row, jnp.zeros((3, D), F32)])])
    d_ws2 = d_ws.reshape(GROUPS * CHUNK, CHUNK)
    r_in, r_gate, r_vecs, r_ws = _run_plan("rs_sibling_in", _sibling_plan([g_in, g_gate], [vecs, d_ws2]))
    p_in, p_gate = pair(["in", "gate"], [g_in, g_gate], [r_in, r_gate])
    vecs_chip = _add2("pair_sum_vecs", vecs, r_vecs)
    ws_chip = _add2("pair_sum_ws", d_ws2, r_ws)
    (dx, d_gmix), (got_in, got_gate, got_vecs, got_ws) = _in_bwd(
        dz, w_in_g, xs, dh1, norm_mix_g, comm=_chips_plan([p_in, p_gate], [vecs_chip, ws_chip]))
    vecs_sum = _sum_terms("sum_vecs", got_vecs)
    last = jnp.concatenate([d_gmix, jnp.pad(loss_acc[0:1], ((0, 0), (0, D - 128))), jnp.zeros((6, D), F32)])
    (last_all,) = _run_plan("exchange_last", _exchange_plan(last))
    last_sum = _sum_terms("sum_last", last_all)
    loss = last_sum[1, 0]
    got = [got_in, got_oa, got_ob, got_out, got_up, got_down, got_gate]

    def step(nm, terms, w, m, v, rows, cols):
        g, d, mn, vn = _adamw("adamw_" + nm, terms.reshape(4, rows, cols), w.reshape(rows, cols),
                              m.reshape(rows, cols), v.reshape(rows, cols))
        return [a.reshape(w.shape) for a in (g, d, mn, vn)]

    o_in = step("in", got[0], w_in, m_w_in, v_w_in, D, W_IN_COLS)
    o_oa = step("out_a", got[1], w_out_a, m_w_out_a, v_w_out_a, D // N_SLOT, D)
    o_ob = step("out_b", got[2], w_out_b, m_w_out_b, v_w_out_b, D // N_SLOT, D)
    o_out = step("out", got[3], w_out, m_w_out, v_w_out, D // N_SLOT, D)
    o_up = step("up", got[4], w_up, m_w_up, v_w_up, D, FF_COLS)
    o_down = step("down", got[5], w_down, m_w_down, v_w_down, FF_COLS, D)
    gate_w = jnp.stack([w_rgate[0], w_igate[0]]).reshape(2 * HEADS * 32, HEAD_DIM)
    gate_m = jnp.stack([m_w_rgate[0], m_w_igate[0]]).reshape(2 * HEADS * 32, HEAD_DIM)
    gate_v = jnp.stack([v_w_rgate[0], v_w_igate[0]]).reshape(2 * HEADS * 32, HEAD_DIM)
    o_gate = _adamw("adamw_gate", got[6], gate_w, gate_m, gate_v)
    o_gate = [a.reshape(2, 1, HEADS, 32, HEAD_DIM) for a in o_gate]
    o_wr = [a[0] for a in o_gate]
    o_wi = [a[1] for a in o_gate]

    def own(full, width):
        return lax.dynamic_slice_in_dim(full, me * width, width, axis=1)

    small_g = {
        "norm_mix_g": last_sum[0:1], "conv_w": own(vecs_sum[0:4], 128), "conv_b": vecs_sum[4:5],
        "b_rgate": own(vecs_sum[5:6].reshape(HEADS, HEAD_DIM), 32),
        "b_igate": own(vecs_sum[6:7].reshape(HEADS, HEAD_DIM), 32),
        "lru_lambda": vecs_sum[7:8], "sgu_ln_g": vecs_sum[8:9], "sgu_ln_b": vecs_sum[9:10],
        "norm_mlp_g": vecs_sum[10:11], "norm_final_g": vecs_sum[11:12],
        "sgu_b_s": vecs_sum[12, 0:GROUPS * CHUNK].reshape(GROUPS, CHUNK),
    }
    small_w = {"norm_mix_g": (norm_mix_g, m_norm_mix_g, v_norm_mix_g), "conv_w": (conv_w, m_conv_w, v_conv_w),
               "conv_b": (conv_b, m_conv_b, v_conv_b), "b_rgate": (b_rgate, m_b_rgate, v_b_rgate),
               "b_igate": (b_igate, m_b_igate, v_b_igate), "lru_lambda": (lru_lambda, m_lru_lambda, v_lru_lambda),
               "sgu_ln_g": (sgu_ln_g, m_sgu_ln_g, v_sgu_ln_g), "sgu_ln_b": (sgu_ln_b, m_sgu_ln_b, v_sgu_ln_b),
               "norm_mlp_g": (norm_mlp_g, m_norm_mlp_g, v_norm_mlp_g),
               "norm_final_g": (norm_final_g, m_norm_final_g, v_norm_final_g),
               "sgu_b_s": (sgu_b_s, m_sgu_b_s, v_sgu_b_s), "sgu_w_s": (sgu_w_s, m_sgu_w_s, v_sgu_w_s)}
    order = list(small_g)
    as2d = lambda k, a: a.reshape(small_g[k].shape)
    upd = _adamw_many("adamw_small", [small_g[k] for k in order], *[[as2d(k, small_w[k][q]) for k in order]
                                                                     for q in range(3)])
    o_small = {k: [a.reshape(small_w[k][0].shape) for a in (small_g[k],) + u] for k, u in zip(order, upd)}
    ws3 = [a[0].reshape(GROUPS * CHUNK, CHUNK) for a in small_w.pop("sgu_w_s")]
    o_small["sgu_w_s"] = [a.reshape(sgu_w_s.shape) for a in _adamw("adamw_ws", got_ws, *ws3)]

    per_weight = {"norm_mix_g": o_small["norm_mix_g"], "w_in": o_in, "conv_w": o_small["conv_w"],
                  "conv_b": o_small["conv_b"], "w_rgate": o_wr, "b_rgate": o_small["b_rgate"], "w_igate": o_wi,
                  "b_igate": o_small["b_igate"], "lru_lambda": o_small["lru_lambda"], "w_out_a": o_oa,
                  "sgu_ln_g": o_small["sgu_ln_g"], "sgu_ln_b": o_small["sgu_ln_b"], "sgu_w_s": o_small["sgu_w_s"],
                  "sgu_b_s": o_small["sgu_b_s"], "w_out_b": o_ob, "w_out": o_out, "norm_mlp_g": o_small["norm_mlp_g"],
                  "w_up": o_up, "w_down": o_down, "norm_final_g": o_small["norm_final_g"]}
    names_w = list(per_weight)
    return (loss, dx[None], *[per_weight[k][0] for k in names_w], *[per_weight[k][1] for k in names_w],
            *[per_weight[k][2] for k in names_w], *[per_weight[k][3] for k in names_w])
```

```python
import jax
import jax.numpy as jnp
from jax import lax
from jax.experimental import pallas as pl
from jax.experimental.pallas import tpu as pltpu

F32 = jnp.float32
BF16 = jnp.bfloat16
SDS = jax.ShapeDtypeStruct
MESH = pl.DeviceIdType.MESH
ANY = pl.BlockSpec(memory_space=pltpu.HBM)

D = 1024
N_SLOT = 8
W_IN_COLS = 768
FF_COLS = 512
HEADS, HEAD_DIM = 4, 256
GROUPS, GROUP_DIM = 4, 256
CHUNK = 128
CONV_K = 4
NORM_EPS = 1e-6
LN_EPS = 1e-5
LRU_C = 8.0
ADAM_LR, ADAM_B1, ADAM_B2, ADAM_EPS, ADAM_WD, ADAM_STEP = 0.001, 0.9, 0.999, 1e-08, 0.01, 10

TM_ROWS = 1024
TM_MERGE = 512
T_BRANCH_A = 256
T_BRANCH_B = 256
MiB = 1024 * 1024
SMALL_OPERAND = 16 * 1024

_GELU_C = 0.7978845608028654
_GELU_A = 0.044715


def _small_in_hbm(a):
    return pltpu.with_memory_space_constraint(a, pltpu.HBM) if a.size <= SMALL_OPERAND else a


def _cparams(sem, vmem_mib):
    return pltpu.CompilerParams(dimension_semantics=sem, vmem_limit_bytes=vmem_mib * MiB)


def _gelu(x):
    t = jnp.tanh(_GELU_C * (x + _GELU_A * x * x * x))
    return 0.5 * x * (1.0 + t)


def _gelu_and_grad(x):
    x2 = x * x
    t = jnp.tanh(_GELU_C * x * (1.0 + _GELU_A * x2))
    g = 0.5 * x * (1.0 + t)
    dg = 0.5 * (1.0 + t) + 0.5 * x * (1.0 - t * t) * _GELU_C * (1.0 + 3.0 * _GELU_A * x2)
    return g, dg


def _softplus(x):
    return jnp.maximum(x, 0.0) + jnp.log1p(jnp.exp(-jnp.abs(x)))


def _dot(a, b):
    return jnp.dot(a, b, preferred_element_type=F32)


def _dot_nt(a, b):
    return lax.dot_general(a, b, (((1,), (1,)), ((), ())), preferred_element_type=F32)


def _dot_tn(a, b):
    return lax.dot_general(a, b, (((0,), (0,)), ((), ())), preferred_element_type=F32)


def _rows_shifted(prev8, cur, k):
    ext = jnp.concatenate([prev8, cur], axis=0)
    return pltpu.roll(ext, k, 0)[8:]


def _rows_advanced(cur, next8, k):
    t = cur.shape[0]
    ext = jnp.concatenate([cur, next8], axis=0)
    return pltpu.roll(ext, t + 8 - k, 0)[:t]


def _slot_order(x, y, c):
    chip = 2 * x + y
    order = [2 * chip + c, 2 * chip + 1 - c]
    for _, _, pc in _other_chips(x, y):
        order += [2 * pc + c, 2 * pc + 1 - c]
    return jnp.stack(order).astype(jnp.int32)


def _in_proj(x, g_mix, w_in_own, order, comm=None):
    s = x.shape[0]
    tm = min(TM_ROWS, s)
    ni = s // tm

    def body(order_ref, x_ref, g_ref, own_ref, z_ref, nt_ref, wg_ref, n_s, w_s, send_sems, recv_sems, local_sems):
        j, i = pl.program_id(0), pl.program_id(1)
        px, py, c = _place()
        chip = 2 * px + py
        me = 2 * chip + c
        sib = (px, py, 1 - c)
        chips = _other_chips(px, py)

        def rc(k, src, blk, to):
            return pltpu.make_async_remote_copy(src_ref=src, dst_ref=w_s.at[blk], send_sem=send_sems.at[k],
                                                recv_sem=recv_sems.at[k], device_id=to, device_id_type=MESH)

        (yx, yy, y_chip), (xx, xy, x_chip), _ = chips
        own_in = pltpu.make_async_copy(own_ref, w_s.at[me], local_sems.at[0])
        sends = [rc(0, own_ref, me, sib), rc(1, own_ref, me, (yx, yy, c)), rc(2, own_ref, me, (xx, xy, c))]
        passed = [rc(4 + q, w_s.at[2 * pc + c], 2 * pc + c, sib) for q, (_, _, pc) in enumerate(chips)]
        relays = [rc(3, w_s.at[2 * y_chip + c], 2 * y_chip + c, (xx, xy, c)),
                  rc(3, w_s.at[2 * x_chip + c], 2 * x_chip + c, (yx, yy, c))]
        keep = pltpu.make_async_copy(w_s, wg_ref, local_sems.at[1])

        @pl.when((i == 0) & (j == 0))
        def _():
            own_in.start()
            for cp in sends:
                cp.start()
            own_in.wait()

        @pl.when((i == 0) & (j == 1))
        def _():
            rc(0, own_ref, 2 * chip + 1 - c, sib).wait_recv()

        for q, (_, _, pc) in enumerate(chips):
            @pl.when((i == 0) & (j == 2 + 2 * q))
            def _():
                rc(min(1 + q, 3), own_ref, 2 * pc + c, sib).wait_recv()
                passed[q].start()
                if q < 2:
                    @pl.when(c == q)
                    def _():
                        relays[q].start()

            @pl.when((i == 0) & (j == 3 + 2 * q))
            def _():
                rc(4 + q, own_ref, 2 * pc + 1 - c, sib).wait_recv()

        rows = pl.ds(pl.multiple_of(i * tm, tm), tm)

        @pl.when(j == 0)
        def _():
            xv = x_ref[...]
            rstd = lax.rsqrt(jnp.mean(xv * xv, axis=-1, keepdims=True) + NORM_EPS)
            nb = (xv * rstd * g_ref[...]).astype(BF16)
            n_s[rows, :] = nb
            nt_ref[...] = nb.T

        z_ref[...] = _dot(n_s[rows, :], w_s[order_ref[j]]).astype(BF16)

        @pl.when((i == 0) & (j == N_SLOT - 1))
        def _():
            keep.start()

        @pl.when((i == ni - 1) & (j == N_SLOT - 1))
        def _():
            for cp in sends + passed:
                cp.wait_send()
            for q in range(2):
                @pl.when(c == q)
                def _():
                    relays[q].wait_send()
            keep.wait()

    first_pass = lambda j, i, o: (jnp.where(j == 0, i, ni - 1), 0)
    (z, n1, w_in_g), extra = _call(
        body, name="in_proj", grid=(N_SLOT, ni), prefetch=(order,),
        in_specs=[pl.BlockSpec((tm, D), first_pass),
                  pl.BlockSpec((1, D), lambda j, i, o: (0, 0)), ANY],
        out_specs=[pl.BlockSpec((tm, W_IN_COLS), lambda j, i, o: (i, o[j])),
                   pl.BlockSpec((D, tm), lambda j, i, o: (0, jnp.where(j == 0, i, ni - 1))), ANY],
        out_shape=[SDS((s, N_SLOT * W_IN_COLS), BF16), SDS((D, s), BF16), SDS((N_SLOT, D, W_IN_COLS), BF16)],
        scratch_shapes=[pltpu.VMEM((s, D), BF16), pltpu.VMEM((N_SLOT, D, W_IN_COLS), BF16),
                        pltpu.SemaphoreType.DMA((7,)), pltpu.SemaphoreType.DMA((7,)), pltpu.SemaphoreType.DMA((2,))],
        params=_cparams(("arbitrary", "arbitrary"), 56), args=(x, g_mix, w_in_own), comm=comm)
    return (z, n1, w_in_g), extra


def _lru_gates(xc, xcb, wr_ref, br, wi_ref, bi, sp_lam, a_s, b_s, r_s=None, i_s=None, m_s=None):
    for h in range(HEADS):
        sl = slice(h * HEAD_DIM, (h + 1) * HEAD_DIM)
        r = jax.nn.sigmoid(_dot(xcb[:, sl], wr_ref[h]) + br[:, sl])
        ig = jax.nn.sigmoid(_dot(xcb[:, sl], wi_ref[h]) + bi[:, sl])
        log_a = (-LRU_C) * r * sp_lam[:, sl]
        a = jnp.exp(log_a)
        mult = jnp.sqrt(-jnp.tanh(log_a) * (a * a + 1.0))
        a_s[:, sl] = a
        b_s[:, sl] = xc[:, sl] * ig * mult
        if r_s is not None:
            r_s[:, sl] = r
            i_s[:, sl] = ig
            m_s[:, sl] = mult


def _conv_fwd(xa, prev8, cw, cb):
    xc = cb + cw[0:1, :] * xa
    for k in range(1, CONV_K):
        xc = xc + cw[k:k + 1, :] * _rows_shifted(prev8, xa, k)
    return xc


def _branch_a_fwd(z, conv_w, conv_b, w_r, b_r, w_i, b_i, lam, comm=None):
    s = z.shape[0]
    ta = min(T_BRANCH_A, s)
    per16 = ta // 16

    def body(xa_ref, xp_ref, ga_ref, cw_ref, cb_ref, wr_ref, br_ref, wi_ref, bi_ref, lam_ref,
             ya_ref, hs_ref, a_s, b_s, h_s, carry_s):
        i = pl.program_id(0)

        @pl.when(i == 0)
        def _():
            carry_s[...] = jnp.zeros_like(carry_s)

        xa = xa_ref[...].astype(F32)
        prev8 = jnp.where(i > 0, xp_ref[...].astype(F32)[8:16], 0.0)
        xc = _conv_fwd(xa, prev8, cw_ref[...], cb_ref[...])
        sp_lam = _softplus(-lam_ref[...])
        _lru_gates(xc, xc.astype(BF16), wr_ref, br_ref[...], wi_ref, bi_ref[...], sp_lam, a_s, b_s)

        row = lax.broadcasted_iota(jnp.int32, (8, D), 0)

        def group(g, carry):
            off = pl.multiple_of(g * 8, 8)
            a8 = a_s[pl.ds(off, 8), :]
            b8 = b_s[pl.ds(off, 8), :]
            for d in (1, 2, 4):
                a_sh = jnp.where(row >= d, pltpu.roll(a8, d, 0), 1.0)
                b_sh = jnp.where(row >= d, pltpu.roll(b8, d, 0), 0.0)
                b8 = a8 * b_sh + b8
                a8 = a8 * a_sh
            h8 = b8 + a8 * carry
            h_s[pl.ds(off, 8), :] = h8
            return jnp.broadcast_to(h8[7:8, :], (8, D))

        carry_s[...] = lax.fori_loop(0, ta // 8, group, carry_s[...])
        hs = h_s[...]
        hs_ref[...] = hs.astype(BF16)
        ya_ref[...] = (hs * _gelu(ga_ref[...].astype(F32))).astype(BF16)

    vec = pl.BlockSpec((1, D), lambda i: (0, 0))
    gate = pl.BlockSpec((HEADS, HEAD_DIM, HEAD_DIM), lambda i: (0, 0, 0))
    return _call(
        body, name="branch_a_fwd", grid=(s // ta,),
        in_specs=[pl.BlockSpec((ta, D), lambda i: (i, 0)),
                  pl.BlockSpec((16, D), lambda i: (jnp.maximum(i * per16 - 1, 0), 0)),
                  pl.BlockSpec((ta, D), lambda i: (i, 1)),
                  pl.BlockSpec((CONV_K, D), lambda i: (0, 0)), vec, gate, vec, gate, vec, vec],
        out_specs=[pl.BlockSpec((ta, D), lambda i: (i, 0)), pl.BlockSpec((ta, D), lambda i: (i, 0))],
        out_shape=[SDS((s, D), BF16), SDS((s, D), BF16)],
        scratch_shapes=[pltpu.VMEM((ta, D), F32), pltpu.VMEM((ta, D), F32), pltpu.VMEM((ta, D), F32),
                        pltpu.VMEM((8, D), F32)],
        params=_cparams(("arbitrary",), 40), args=(z, z, z, conv_w, conv_b, w_r, b_r, w_i, b_i, lam), comm=comm)


def _sgu_common(ub, vb, lg, lb, with_grad):
    if with_grad:
        u, du = _gelu_and_grad(ub)
        v, dv = _gelu_and_grad(vb)
    else:
        u, v, du, dv = _gelu(ub), _gelu(vb), None, None
    mu = jnp.mean(v, axis=-1, keepdims=True)
    vc = v - mu
    rstd = lax.rsqrt(jnp.mean(vc * vc, axis=-1, keepdims=True) + LN_EPS)
    vhat = vc * rstd
    vln = vhat * lg + lb
    return u, du, dv, rstd, vhat, vln


def _masked_ws(ws_ref):
    t = lax.broadcasted_iota(jnp.int32, (CHUNK, CHUNK), 0)
    c = lax.broadcasted_iota(jnp.int32, (CHUNK, CHUNK), 1)
    keep = c <= t
    return [jnp.where(keep, ws_ref[g], 0.0).astype(BF16) for g in range(GROUPS)]


def _branch_b_fwd(z, ln_g, ln_b, w_s, b_s_t, comm=None):
    s = z.shape[0]
    tb = min(T_BRANCH_B, s)

    def body(ub_ref, vb_ref, lg_ref, lb_ref, ws_ref, bs_ref, yb_ref):
        u, _, _, _, _, vln = _sgu_common(ub_ref[...].astype(F32), vb_ref[...].astype(F32),
                                         lg_ref[...], lb_ref[...], False)
        vlnb = vln.astype(BF16)
        wm = _masked_ws(ws_ref)
        bs = bs_ref[...]
        for c in range(tb // CHUNK):
            rs = slice(c * CHUNK, (c + 1) * CHUNK)
            for g in range(GROUPS):
                cs = slice(g * GROUP_DIM, (g + 1) * GROUP_DIM)
                sp = _dot(wm[g], vlnb[rs, cs]) + bs[:, g:g + 1]
                yb_ref[rs, cs] = (u[rs, cs] * sp).astype(BF16)

    vec = pl.BlockSpec((1, D), lambda i: (0, 0))
    return _call(
        body, name="branch_b_fwd", grid=(s // tb,),
        in_specs=[pl.BlockSpec((tb, D), lambda i: (i, 2)), pl.BlockSpec((tb, D), lambda i: (i, 3)), vec, vec,
                  pl.BlockSpec((GROUPS, CHUNK, CHUNK), lambda i: (0, 0, 0)),
                  pl.BlockSpec((CHUNK, GROUPS), lambda i: (0, 0))],
        out_specs=[pl.BlockSpec((tb, D), lambda i: (i, 0))],
        out_shape=[SDS((s, D), BF16)], scratch_shapes=[],
        params=_cparams(("arbitrary",), 40), args=(z, z, ln_g, ln_b, w_s, b_s_t), comm=comm)


def _merge_out(ya, yb, z, x, w_oa, w_ob, w_out, comm=None):
    s = x.shape[0]
    tm = min(TM_MERGE, s)

    def body(ya_ref, yb_ref, ma_ref, mb_ref, x_ref, woa_ref, wob_ref, wo_ref, pa_ref, pb_ref, mg_ref, h1_ref):
        pa = _dot(ya_ref[...], woa_ref[...])
        pb = _dot(yb_ref[...], wob_ref[...])
        merged = (jax.nn.sigmoid(ma_ref[...].astype(F32)) * pa
                  + jax.nn.sigmoid(mb_ref[...].astype(F32)) * pb).astype(BF16)
        pa_ref[...] = pa.astype(BF16)
        pb_ref[...] = pb.astype(BF16)
        mg_ref[...] = merged
        h1_ref[...] = x_ref[...] + _dot(merged, wo_ref[...])

    row = pl.BlockSpec((tm, D), lambda i: (i, 0))
    wsp = pl.BlockSpec((D, D), lambda i: (0, 0))
    return _call(
        body, name="merge_out", grid=(s // tm,),
        in_specs=[row, row, pl.BlockSpec((tm, D), lambda i: (i, 4)), pl.BlockSpec((tm, D), lambda i: (i, 5)),
                  row, wsp, wsp, wsp],
        out_specs=[row, row, row, row],
        out_shape=[SDS((s, D), BF16), SDS((s, D), BF16), SDS((s, D), BF16), SDS((s, D), F32)], scratch_shapes=[],
        params=_cparams(("arbitrary",), 48), args=(ya, yb, z, z, x, w_oa, w_ob, w_out), comm=comm)


def _mlp_fwd(h1, g_mlp, w_up_g, w_down, g_fin, tgt):
    s = h1.shape[0]
    tm = min(TM_ROWS, s)
    nj = N_SLOT

    def body(h1_ref, gm_ref, wu_ref, wd_ref, gf_ref, t_ref, r_ref, at_ref, n2t_ref, dh2_ref, loss_ref, dgf_ref,
             n2_s, acc_s):
        i, j = pl.program_id(0), pl.program_id(1)

        @pl.when(j == 0)
        def _():
            hv = h1_ref[...]
            rstd = lax.rsqrt(jnp.mean(hv * hv, axis=-1, keepdims=True) + NORM_EPS)
            nb = (hv * rstd * gm_ref[...]).astype(BF16)
            n2_s[...] = nb
            n2t_ref[...] = nb.T
            acc_s[...] = jnp.zeros_like(acc_s)

        @pl.when((i == 0) & (j == 0))
        def _():
            loss_ref[...] = jnp.zeros_like(loss_ref)
            dgf_ref[...] = jnp.zeros_like(dgf_ref)

        r = jnp.maximum(_dot(n2_s[...], wu_ref[...]), 0.0)
        r_ref[...] = r.astype(BF16)
        act = (r * r).astype(BF16)
        at_ref[...] = act.T
        acc_s[...] += _dot(act, wd_ref[...])

        @pl.when(j == nj - 1)
        def _():
            h2 = h1_ref[...] + acc_s[...]
            rstd = lax.rsqrt(jnp.mean(h2 * h2, axis=-1, keepdims=True) + NORM_EPS)
            hh = h2 * rstd
            gf = gf_ref[...]
            e = hh * gf - t_ref[...]
            loss_ref[...] += jnp.sum(e * e) * (0.5 / D)
            dy = e * (1.0 / D)
            dgf_ref[...] += jnp.sum(dy * hh, axis=0, keepdims=True)
            dhh = dy * gf
            dh2_ref[...] = rstd * (dhh - hh * jnp.mean(dhh * hh, axis=-1, keepdims=True))

    row = pl.BlockSpec((tm, D), lambda i, j: (i, 0))
    vec = pl.BlockSpec((1, D), lambda i, j: (0, 0))
    return pl.pallas_call(
        body, name="mlp_fwd", grid=(s // tm, nj),
        in_specs=[row, vec, pl.BlockSpec((None, D, FF_COLS), lambda i, j: (j, 0, 0)),
                  pl.BlockSpec((FF_COLS, D), lambda i, j: (j, 0)), vec, row],
        out_specs=[pl.BlockSpec((tm, FF_COLS), lambda i, j: (i, j)), pl.BlockSpec((FF_COLS, tm), lambda i, j: (j, i)),
                   pl.BlockSpec((D, tm), lambda i, j: (0, i)), row, pl.BlockSpec((8, 128), lambda i, j: (0, 0)), vec],
        out_shape=[SDS((s, nj * FF_COLS), BF16), SDS((nj * FF_COLS, s), BF16), SDS((D, s), BF16), SDS((s, D), F32),
                   SDS((8, 128), F32), SDS((1, D), F32)],
        scratch_shapes=[pltpu.VMEM((tm, D), BF16), pltpu.VMEM((tm, D), F32)],
        compiler_params=_cparams(("arbitrary", "arbitrary"), 52),
    )(h1, _small_in_hbm(g_mlp), w_up_g, w_down, _small_in_hbm(g_fin), tgt)


def _mlp_bwd(dh2, r, w_down, w_up_g, h1, g_mlp, comm=None):
    s = h1.shape[0]
    tm = min(TM_ROWS, s)
    nj = N_SLOT

    def body(dh2_ref, r_ref, wd_ref, wu_ref, h1_ref, gm_ref, df_ref, dh1_ref, dgm_ref, dh2b_s, acc_s):
        i, j = pl.program_id(0), pl.program_id(1)

        @pl.when(j == 0)
        def _():
            dh2b_s[...] = dh2_ref[...].astype(BF16)
            acc_s[...] = jnp.zeros_like(acc_s)

        @pl.when((i == 0) & (j == 0))
        def _():
            dgm_ref[...] = jnp.zeros_like(dgm_ref)

        d_act = _dot_nt(dh2b_s[...], wd_ref[...])
        df = (d_act * (2.0 * r_ref[...].astype(F32))).astype(BF16)
        df_ref[...] = df
        acc_s[...] += _dot_nt(df, wu_ref[...])

        @pl.when(j == nj - 1)
        def _():
            hv = h1_ref[...]
            rstd = lax.rsqrt(jnp.mean(hv * hv, axis=-1, keepdims=True) + NORM_EPS)
            hh = hv * rstd
            dn2 = acc_s[...]
            dgm_ref[...] += jnp.sum(dn2 * hh, axis=0, keepdims=True)
            dhat = dn2 * gm_ref[...]
            dh1_ref[...] = dh2_ref[...] + rstd * (dhat - hh * jnp.mean(dhat * hh, axis=-1, keepdims=True))

    row = pl.BlockSpec((tm, D), lambda i, j: (i, 0))
    vec = pl.BlockSpec((1, D), lambda i, j: (0, 0))
    ffb = pl.BlockSpec((tm, FF_COLS), lambda i, j: (i, j))
    return _call(
        body, name="mlp_bwd", grid=(s // tm, nj),
        in_specs=[row, ffb, pl.BlockSpec((FF_COLS, D), lambda i, j: (j, 0)),
                  pl.BlockSpec((None, D, FF_COLS), lambda i, j: (j, 0, 0)), row, vec],
        out_specs=[ffb, row, vec],
        out_shape=[SDS((s, nj * FF_COLS), BF16), SDS((s, D), F32), SDS((1, D), F32)],
        scratch_shapes=[pltpu.VMEM((tm, D), BF16), pltpu.VMEM((tm, D), F32)],
        params=_cparams(("arbitrary", "arbitrary"), 52), args=(dh2, r, w_down, w_up_g, h1, g_mlp), comm=comm)


def _merge_bwd(dh1, z, pa, pb, w_out, w_oa, w_ob, comm=None):
    s = dh1.shape[0]
    tm = min(TM_MERGE, s)

    def body(dh1_ref, ma_ref, mb_ref, pa_ref, pb_ref, wo_ref, woa_ref, wob_ref,
             dz_ref, dpa_ref, dpb_ref, dya_ref, dyb_ref):
        dm = _dot_nt(dh1_ref[...].astype(BF16), wo_ref[...])
        sa = jax.nn.sigmoid(ma_ref[...].astype(F32))
        sb = jax.nn.sigmoid(mb_ref[...].astype(F32))
        dpa = (dm * sa).astype(BF16)
        dpb = (dm * sb).astype(BF16)
        dz_ref[:, 0:D] = (dm * pa_ref[...].astype(F32) * sa * (1.0 - sa)).astype(BF16)
        dz_ref[:, D:2 * D] = (dm * pb_ref[...].astype(F32) * sb * (1.0 - sb)).astype(BF16)
        dpa_ref[...] = dpa
        dpb_ref[...] = dpb
        dya_ref[...] = _dot_nt(dpa, woa_ref[...]).astype(BF16)
        dyb_ref[...] = _dot_nt(dpb, wob_ref[...]).astype(BF16)

    row = pl.BlockSpec((tm, D), lambda i: (i, 0))
    wsp = pl.BlockSpec((D, D), lambda i: (0, 0))
    return _call(
        body, name="merge_bwd", grid=(s // tm,),
        in_specs=[row, pl.BlockSpec((tm, D), lambda i: (i, 4)), pl.BlockSpec((tm, D), lambda i: (i, 5)),
                  row, row, wsp, wsp, wsp],
        out_specs=[pl.BlockSpec((tm, 2 * D), lambda i: (i, 2)), row, row, row, row],
        out_shape=[SDS((s, 6 * D), BF16)] + [SDS((s, D), BF16)] * 4, scratch_shapes=[],
        params=_cparams(("arbitrary",), 48), args=(dh1, z, z, pa, pb, w_out, w_oa, w_ob), comm=comm)


def _branch_b_bwd(dz, dyb, z, ln_g, ln_b, w_s, b_s_t, comm=None):
    s = z.shape[0]
    tb = min(T_BRANCH_B, s)

    def body(dz_in, dyb_ref, ub_ref, vb_ref, lg_ref, lb_ref, ws_ref, bs_ref,
             dz_ref, dws_ref, dbs_ref, dln_ref, du_s, dvln_s):
        del dz_in

        @pl.when(pl.program_id(0) == 0)
        def _():
            dws_ref[...] = jnp.zeros_like(dws_ref)
            dbs_ref[...] = jnp.zeros_like(dbs_ref)
            dln_ref[...] = jnp.zeros_like(dln_ref)

        lg = lg_ref[...]
        u, du, dv, rstd, vhat, vln = _sgu_common(ub_ref[...].astype(F32), vb_ref[...].astype(F32),
                                                 lg, lb_ref[...], True)
        vlnb = vln.astype(BF16)
        dyb_v = dyb_ref[...].astype(F32)
        wm = _masked_ws(ws_ref)
        keep = (lax.broadcasted_iota(jnp.int32, (CHUNK, CHUNK), 1)
                <= lax.broadcasted_iota(jnp.int32, (CHUNK, CHUNK), 0))
        bs = bs_ref[...]
        for c in range(tb // CHUNK):
            rs = slice(c * CHUNK, (c + 1) * CHUNK)
            for g in range(GROUPS):
                cs = slice(g * GROUP_DIM, (g + 1) * GROUP_DIM)
                v_blk = vlnb[rs, cs]
                sp = _dot(wm[g], v_blk) + bs[:, g:g + 1]
                d_sp = dyb_v[rs, cs] * u[rs, cs]
                d_spb = d_sp.astype(BF16)
                du_s[rs, cs] = dyb_v[rs, cs] * sp
                dvln_s[rs, cs] = _dot_tn(wm[g], d_spb)
                dws_ref[g] += jnp.where(keep, _dot_nt(d_spb, v_blk), 0.0)
                dbs_ref[g] += jnp.broadcast_to(jnp.sum(d_sp, axis=-1, keepdims=True), (CHUNK, CHUNK))
        dvln = dvln_s[...]
        dln_ref[0:1, :] += jnp.sum(dvln * vhat, axis=0, keepdims=True)
        dln_ref[1:2, :] += jnp.sum(dvln, axis=0, keepdims=True)
        dvh = dvln * lg
        d_v = rstd * (dvh - jnp.mean(dvh, axis=-1, keepdims=True)
                      - vhat * jnp.mean(dvh * vhat, axis=-1, keepdims=True))
        dz_ref[:, 0:D] = (du_s[...] * du).astype(BF16)
        dz_ref[:, D:2 * D] = (d_v * dv).astype(BF16)

    vec = pl.BlockSpec((1, D), lambda i: (0, 0))
    sq = pl.BlockSpec((GROUPS, CHUNK, CHUNK), lambda i: (0, 0, 0))
    return _call(
        body, name="branch_b_bwd", grid=(s // tb,),
        in_specs=[ANY, pl.BlockSpec((tb, D), lambda i: (i, 0)),
                  pl.BlockSpec((tb, D), lambda i: (i, 2)), pl.BlockSpec((tb, D), lambda i: (i, 3)), vec, vec, sq,
                  pl.BlockSpec((CHUNK, GROUPS), lambda i: (0, 0))],
        out_specs=[pl.BlockSpec((tb, 2 * D), lambda i: (i, 1)), sq, sq, pl.BlockSpec((8, D), lambda i: (0, 0))],
        out_shape=[SDS(dz.shape, BF16), SDS((GROUPS, CHUNK, CHUNK), F32), SDS((GROUPS, CHUNK, CHUNK), F32),
                   SDS((8, D), F32)],
        scratch_shapes=[pltpu.VMEM((tb, D), F32), pltpu.VMEM((tb, D), F32)], aliases={0: 0},
        params=_cparams(("arbitrary",), 40), args=(dz, dyb, z, z, ln_g, ln_b, w_s, b_s_t), comm=comm)


def _branch_a_bwd(dz, dya, z, hs, conv_w, conv_b, w_r, b_r, w_i, b_i, lam, comm=None):
    s = z.shape[0]
    ta = min(T_BRANCH_A, s)
    nb = s // ta
    per16 = ta // 16

    def body(dz_in, dya_ref, xa_ref, xp_ref, ga_ref, hs_ref, hp_ref, cw_ref, cb_ref, wr_ref, br_ref, wi_ref,
             bi_ref, lam_ref, dz_ref, vec_ref, dwr_ref, dwi_ref,
             a_s, b_s, h_s, r_s, i_s, m_s, dcar_s, acar_s, dxc_s):
        del dz_in
        i = pl.program_id(0)
        blk = nb - 1 - i

        @pl.when(i == 0)
        def _():
            dcar_s[...] = jnp.zeros_like(dcar_s)
            acar_s[...] = jnp.zeros_like(acar_s)
            dxc_s[...] = jnp.zeros_like(dxc_s)
            vec_ref[...] = jnp.zeros_like(vec_ref)
            dwr_ref[...] = jnp.zeros_like(dwr_ref)
            dwi_ref[...] = jnp.zeros_like(dwi_ref)

        cw = cw_ref[...]
        lam_v = lam_ref[...]
        xa = xa_ref[...].astype(F32)
        prev8 = jnp.where(blk > 0, xp_ref[...].astype(F32)[8:16], 0.0)
        xc = _conv_fwd(xa, prev8, cw, cb_ref[...])
        xcb = xc.astype(BF16)
        sp_lam = _softplus(-lam_v)
        _lru_gates(xc, xcb, wr_ref, br_ref[...], wi_ref, bi_ref[...], sp_lam, a_s, b_s, r_s, i_s, m_s)

        hs_v = hs_ref[...].astype(F32)
        hprev8 = jnp.where(blk > 0, hp_ref[...].astype(F32)[8:16], 0.0)
        h_m1 = _rows_shifted(hprev8, hs_v, 1)
        gg, dgg = _gelu_and_grad(ga_ref[...].astype(F32))
        dya_v = dya_ref[...].astype(F32)
        dz_ref[:, D:2 * D] = (dya_v * hs_v * dgg).astype(BF16)

        a_v = a_s[...]
        a_s[...] = _rows_advanced(a_v, acar_s[...], 1)
        b_s[...] = dya_v * gg

        row = lax.broadcasted_iota(jnp.int32, (8, D), 0)
        ng = ta // 8

        def group(gi, carry):
            off = pl.multiple_of((ng - 1 - gi) * 8, 8)
            c8 = a_s[pl.ds(off, 8), :]
            d8 = b_s[pl.ds(off, 8), :]
            for d in (1, 2, 4):
                c_sh = jnp.where(row < 8 - d, pltpu.roll(c8, 8 - d, 0), 1.0)
                d_sh = jnp.where(row < 8 - d, pltpu.roll(d8, 8 - d, 0), 0.0)
                d8 = c8 * d_sh + d8
                c8 = c8 * c_sh
            dh8 = d8 + c8 * carry
            h_s[pl.ds(off, 8), :] = dh8
            return jnp.broadcast_to(dh8[0:1, :], (8, D))

        dcar_s[...] = lax.fori_loop(0, ng, group, dcar_s[...])
        acar_s[...] = jnp.broadcast_to(a_v[0:1, :], (8, D))

        dbx = h_s[...]
        r_v, i_v, m_v = r_s[...], i_s[...], m_s[...]
        d_mult = dbx * xc * i_v
        d_loga = dbx * h_m1 * a_v - d_mult * (a_v * a_v) / m_v
        d_pr = d_loga * ((-LRU_C) * sp_lam) * r_v * (1.0 - r_v)
        d_pi = dbx * xc * m_v * i_v * (1.0 - i_v)
        vec_ref[7:8, :] += jnp.sum(d_loga * r_v, axis=0, keepdims=True) * (LRU_C * jax.nn.sigmoid(-lam_v))
        vec_ref[5:6, :] += jnp.sum(d_pr, axis=0, keepdims=True)
        vec_ref[6:7, :] += jnp.sum(d_pi, axis=0, keepdims=True)
        d_prb = d_pr.astype(BF16)
        d_pib = d_pi.astype(BF16)
        h_s[...] = dbx * i_v * m_v
        for h in range(HEADS):
            sl = slice(h * HEAD_DIM, (h + 1) * HEAD_DIM)
            h_s[:, sl] += _dot_nt(d_prb[:, sl], wr_ref[h]) + _dot_nt(d_pib[:, sl], wi_ref[h])
            dwr_ref[h] += _dot_tn(xcb[:, sl], d_prb[:, sl])
            dwi_ref[h] += _dot_tn(xcb[:, sl], d_pib[:, sl])
        d_xc = h_s[...]
        vec_ref[4:5, :] += jnp.sum(d_xc, axis=0, keepdims=True)
        vec_ref[0:1, :] += jnp.sum(d_xc * xa, axis=0, keepdims=True)
        d_xa = cw[0:1, :] * d_xc
        nxt = dxc_s[...]
        for k in range(1, CONV_K):
            vec_ref[k:k + 1, :] += jnp.sum(d_xc * _rows_shifted(prev8, xa, k), axis=0, keepdims=True)
            d_xa = d_xa + cw[k:k + 1, :] * _rows_advanced(d_xc, nxt, k)
        dz_ref[:, 0:D] = d_xa.astype(BF16)
        dxc_s[...] = d_xc[0:8, :]

    vec = pl.BlockSpec((1, D), lambda i: (0, 0))
    gate = pl.BlockSpec((HEADS, HEAD_DIM, HEAD_DIM), lambda i: (0, 0, 0))
    cur = lambda c: pl.BlockSpec((ta, D), lambda i: (nb - 1 - i, c))
    before = lambda c: pl.BlockSpec((16, D), lambda i: (jnp.maximum((nb - 1 - i) * per16 - 1, 0), c))
    return _call(
        body, name="branch_a_bwd", grid=(nb,),
        in_specs=[ANY, cur(0), cur(0), before(0), cur(1), cur(0), before(0),
                  pl.BlockSpec((CONV_K, D), lambda i: (0, 0)), vec, gate, vec, gate, vec, vec],
        out_specs=[pl.BlockSpec((ta, 2 * D), lambda i: (nb - 1 - i, 0)), pl.BlockSpec((8, D), lambda i: (0, 0)),
                   gate, gate],
        out_shape=[SDS(dz.shape, BF16), SDS((8, D), F32), SDS((HEADS, HEAD_DIM, HEAD_DIM), F32),
                   SDS((HEADS, HEAD_DIM, HEAD_DIM), F32)],
        scratch_shapes=[pltpu.VMEM((ta, D), F32)] * 6 + [pltpu.VMEM((8, D), F32)] * 3, aliases={0: 0},
        params=_cparams(("arbitrary",), 48),
        args=(dz, dya, z, z, z, hs, hs, conv_w, conv_b, w_r, b_r, w_i, b_i, lam), comm=comm)


def _in_bwd(dz, w_in_g, x, dh1, g_mix, comm=None):
    s = x.shape[0]
    tm = min(TM_ROWS, s)
    nj = N_SLOT

    def body(dz_ref, w_ref, x_ref, dh1_ref, g_ref, dx_ref, dg_ref, acc_s):
        i, j = pl.program_id(0), pl.program_id(1)

        @pl.when(j == 0)
        def _():
            acc_s[...] = jnp.zeros_like(acc_s)

        @pl.when((i == 0) & (j == 0))
        def _():
            dg_ref[...] = jnp.zeros_like(dg_ref)

        acc_s[...] += _dot_nt(dz_ref[...], w_ref[...])

        @pl.when(j == nj - 1)
        def _():
            xv = x_ref[...]
            rstd = lax.rsqrt(jnp.mean(xv * xv, axis=-1, keepdims=True) + NORM_EPS)
            xh = xv * rstd
            dn = acc_s[...]
            dg_ref[...] += jnp.sum(dn * xh, axis=0, keepdims=True)
            dhat = dn * g_ref[...]
            dx_ref[...] = dh1_ref[...] + rstd * (dhat - xh * jnp.mean(dhat * xh, axis=-1, keepdims=True))

    row = pl.BlockSpec((tm, D), lambda i, j: (i, 0))
    vec = pl.BlockSpec((1, D), lambda i, j: (0, 0))
    return _call(
        body, name="in_bwd", grid=(s // tm, nj),
        in_specs=[pl.BlockSpec((tm, W_IN_COLS), lambda i, j: (i, j)),
                  pl.BlockSpec((None, D, W_IN_COLS), lambda i, j: (j, 0, 0)), row, row, vec],
        out_specs=[row, vec],
        out_shape=[SDS((s, D), F32), SDS((1, D), F32)],
        scratch_shapes=[pltpu.VMEM((tm, D), F32)],
        params=_cparams(("arbitrary", "arbitrary"), 48), args=(dz, w_in_g, x, dh1, g_mix), comm=comm)


def _wgrad(name, a, b, nblk, a_split, b_split):
    s = a.shape[0]
    ts = min(TM_ROWS, s)
    a_w = a.shape[1] // nblk if a_split else a.shape[1]
    b_w = b.shape[1] // nblk if b_split else b.shape[1]

    def body(a_ref, b_ref, o_ref, acc_s):
        t = pl.program_id(1)

        @pl.when(t == 0)
        def _():
            acc_s[...] = jnp.zeros_like(acc_s)

        acc_s[...] += _dot_tn(a_ref[...].astype(BF16), b_ref[...].astype(BF16))

        @pl.when(t == pl.num_programs(1) - 1)
        def _():
            o_ref[...] = acc_s[...].astype(BF16)

    return pl.pallas_call(
        body, name=name, grid=(nblk, s // ts),
        in_specs=[pl.BlockSpec((ts, a_w), (lambda k, t: (t, k)) if a_split else (lambda k, t: (t, 0))),
                  pl.BlockSpec((ts, b_w), (lambda k, t: (t, k)) if b_split else (lambda k, t: (t, 0)))],
        out_specs=pl.BlockSpec((None, a_w, b_w), lambda k, t: (k, 0, 0)),
        out_shape=SDS((nblk, a_w, b_w), BF16),
        scratch_shapes=[pltpu.VMEM((a_w, b_w), F32)],
        compiler_params=_cparams(("arbitrary", "arbitrary"), 48),
    )(a, b)


def _wgrad_t(name, a_t, b, nblk, a_split, b_split, tokens):
    s = b.shape[0]
    ts = min(tokens, s)
    a_w = a_t.shape[0] // nblk if a_split else a_t.shape[0]
    b_w = b.shape[1] // nblk if b_split else b.shape[1]

    def body(a_ref, b_ref, o_ref, acc_s):
        t = pl.program_id(1)

        @pl.when(t == 0)
        def _():
            acc_s[...] = jnp.zeros_like(acc_s)

        acc_s[...] += _dot(a_ref[...], b_ref[...].astype(BF16))

        @pl.when(t == pl.num_programs(1) - 1)
        def _():
            o_ref[...] = acc_s[...].astype(BF16)

    return pl.pallas_call(
        body, name=name, grid=(nblk, s // ts),
        in_specs=[pl.BlockSpec((a_w, ts), (lambda k, t: (k, t)) if a_split else (lambda k, t: (0, t))),
                  pl.BlockSpec((ts, b_w), (lambda k, t: (t, k)) if b_split else (lambda k, t: (t, 0)))],
        out_specs=pl.BlockSpec((None, a_w, b_w), lambda k, t: (k, 0, 0)),
        out_shape=SDS((nblk, a_w, b_w), BF16),
        scratch_shapes=[pltpu.VMEM((a_w, b_w), F32)],
        compiler_params=_cparams(("arbitrary", "arbitrary"), 48),
    )(a_t, b)


def _place():
    x, y, c = lax.axis_index("x"), lax.axis_index("y"), lax.axis_index("c")
    return x, y, c


def _other_chips(x, y):
    return [(x, 1 - y, 2 * x + 1 - y), (1 - x, y, 2 * (1 - x) + y), (1 - x, 1 - y, 2 * (1 - x) + 1 - y)]


class _Plan:
    def __init__(self, arrays, out_shape, sems, start, finish, middle=None):
        self.arrays, self.out_shape, self.sems, self.start, self.finish = arrays, out_shape, sems, start, finish
        self.middle = middle


def _gather_plan(shards):
    n = len(shards)

    def copies(ins, outs, sems):
        send_sems, recv_sems, local_sems = sems
        x, y, c = _place()
        chip = 2 * x + y
        me = 2 * chip + c
        sib = (x, y, 1 - c)
        chips = _other_chips(x, y)

        def rc(k, t, src, blk, to):
            return pltpu.make_async_remote_copy(
                src_ref=src, dst_ref=outs[t].at[blk], send_sem=send_sems.at[k * n + t],
                recv_sem=recv_sems.at[k * n + t], device_id=to, device_id_type=MESH)

        (yx, yy, y_chip), (xx, xy, x_chip), _ = chips
        local = [pltpu.make_async_copy(ins[t], outs[t].at[me], local_sems.at[t]) for t in range(n)]
        sends = ([rc(0, t, ins[t], me, sib) for t in range(n)] + [rc(1, t, ins[t], me, (yx, yy, c)) for t in range(n)]
                 + [rc(2, t, ins[t], me, (xx, xy, c)) for t in range(n)])
        passed = [[rc(4 + j, t, outs[t].at[2 * pc + c], 2 * pc + c, sib) for t in range(n)]
                  for j, (_, _, pc) in enumerate(chips)]
        relays = [[rc(3, t, outs[t].at[2 * y_chip + c], 2 * y_chip + c, (xx, xy, c)) for t in range(n)],
                  [rc(3, t, outs[t].at[2 * x_chip + c], 2 * x_chip + c, (yx, yy, c)) for t in range(n)]]
        return rc, local, sends, passed, relays, chips, chip, c, sib

    def start(ins, outs, sems):
        _, local, sends, _, _, _, _, _, _ = copies(ins, outs, sems)
        for cp in local + sends:
            cp.start()

    def middle(ins, outs, sems):
        rc, _, _, passed, relays, chips, _, c, sib = copies(ins, outs, sems)
        for j in range(2):
            for t in range(n):
                rc(1 + j, t, ins[t], 2 * chips[j][2] + c, sib).wait_recv()
        for j in range(2):
            for cp in passed[j]:
                cp.start()

            @pl.when(c == j)
            def _():
                for cp in relays[j]:
                    cp.start()

    def finish(ins, outs, sems):
        rc, local, sends, passed, relays, chips, chip, c, sib = copies(ins, outs, sems)
        far = 2 * chips[2][2] + c
        for t in range(n):
            rc(3, t, ins[t], far, sib).wait_recv()
        for cp in passed[2]:
            cp.start()
        for t in range(n):
            rc(0, t, ins[t], 2 * chip + 1 - c, sib).wait_recv()
        for j, (px, py, pc) in enumerate(chips):
            for t in range(n):
                rc(4 + j, t, ins[t], 2 * pc + 1 - c, sib).wait_recv()
        for cp in sends + passed[0] + passed[1] + passed[2]:
            cp.wait_send()
        for j in range(2):
            @pl.when(c == j)
            def _():
                for cp in relays[j]:
                    cp.wait_send()
        for cp in local:
            cp.wait()

    return _Plan(list(shards), [SDS((N_SLOT,) + tuple(a.shape), a.dtype) for a in shards],
                 [pltpu.SemaphoreType.DMA((7 * n,)), pltpu.SemaphoreType.DMA((7 * n,)),
                  pltpu.SemaphoreType.DMA((n,))], start, finish, middle)


def _sibling_plan(grads, whole=()):
    n, m = len(grads), len(whole)

    def copies(ins, outs, sems):
        send_sems, recv_sems = sems
        x, y, c = _place()
        sib = (x, y, 1 - c)

        def rc(t, src, dst):
            return pltpu.make_async_remote_copy(src_ref=src, dst_ref=dst, send_sem=send_sems.at[t],
                                                recv_sem=recv_sems.at[t], device_id=sib, device_id_type=MESH)
        return rc, c

    def start(ins, outs, sems):
        rc, c = copies(ins, outs, sems)
        for t in range(n):
            for j in range(4):
                rc(t, ins[t].at[2 * j + 1 - c], outs[t].at[j]).start()
        for t in range(n, n + m):
            rc(t, ins[t], outs[t]).start()

    def finish(ins, outs, sems):
        rc, _ = copies(ins, outs, sems)
        for t in range(n):
            rc(t, ins[t].at[pl.ds(0, 4)], outs[t]).wait()
        for t in range(n, n + m):
            rc(t, ins[t], outs[t]).wait()

    return _Plan(list(grads) + list(whole),
                 [SDS((4,) + tuple(g.shape[1:]), g.dtype) for g in grads] + [SDS(a.shape, a.dtype) for a in whole],
                 [pltpu.SemaphoreType.DMA((n + m,)), pltpu.SemaphoreType.DMA((n + m,))], start, finish)


def _chips_plan(parts, whole=()):
    n, m = len(parts), len(whole)

    def src_of(ins, t, pc):
        return ins[t].at[pc] if t < n else ins[t]

    def local_copies(ins, outs, sems, chip):
        return [pltpu.make_async_copy(src_of(ins, t, chip), outs[t].at[chip], sems[2].at[t]) for t in range(n + m)]

    def start(ins, outs, sems):
        send_sems, recv_sems, _ = sems
        x, y, c = _place()
        chip = 2 * x + y
        for cp in local_copies(ins, outs, sems, chip):
            cp.start()
        for px, py, pc in _other_chips(x, y):
            for t in range(n + m):
                pltpu.make_async_remote_copy(src_ref=src_of(ins, t, pc), dst_ref=outs[t].at[chip],
                                             send_sem=send_sems.at[t], recv_sem=recv_sems.at[t],
                                             device_id=(px, py, c), device_id_type=MESH).start()

    def finish(ins, outs, sems):
        send_sems, recv_sems, _ = sems
        x, y, c = _place()
        for t in range(n + m):
            three = outs[t].at[pl.ds(0, 3)]
            pltpu.make_async_remote_copy(src_ref=three, dst_ref=three, send_sem=send_sems.at[t],
                                         recv_sem=recv_sems.at[t], device_id=(x, y, c), device_id_type=MESH).wait()
        for cp in local_copies(ins, outs, sems, 2 * x + y):
            cp.wait()

    return _Plan(list(parts) + list(whole),
                 [SDS(p.shape, p.dtype) for p in parts] + [SDS((4,) + tuple(a.shape), a.dtype) for a in whole],
                 [pltpu.SemaphoreType.DMA((n + m,)), pltpu.SemaphoreType.DMA((n + m,)),
                  pltpu.SemaphoreType.DMA((n + m,))], start, finish)


def _exchange_plan(arr):
    def peers(x, y, c):
        flip = lambda v, f: 1 - v if f else v
        return [(flip(x, fx), flip(y, fy), flip(c, fc))
                for fx in (0, 1) for fy in (0, 1) for fc in (0, 1) if fx or fy or fc]

    def start(ins, outs, sems):
        x, y, c = _place()
        me = 4 * x + 2 * y + c
        pltpu.make_async_copy(ins[0], outs[0].at[me], sems[2].at[0]).start()
        for to in peers(x, y, c):
            pltpu.make_async_remote_copy(src_ref=ins[0], dst_ref=outs[0].at[me], send_sem=sems[0].at[0],
                                         recv_sem=sems[1].at[0], device_id=to, device_id_type=MESH).start()

    def finish(ins, outs, sems):
        x, y, c = _place()
        seven = outs[0].at[pl.ds(0, 7)]
        pltpu.make_async_remote_copy(src_ref=seven, dst_ref=seven, send_sem=sems[0].at[0], recv_sem=sems[1].at[0],
                                     device_id=(x, y, c), device_id_type=MESH).wait()
        pltpu.make_async_copy(ins[0], outs[0].at[4 * x + 2 * y + c], sems[2].at[0]).wait()

    return _Plan([arr], [SDS((N_SLOT,) + tuple(arr.shape), arr.dtype)],
                 [pltpu.SemaphoreType.DMA((1,)), pltpu.SemaphoreType.DMA((1,)), pltpu.SemaphoreType.DMA((1,))],
                 start, finish)


def _join(*plans):
    def cut(seq, sizes):
        out, at = [], 0
        for k in sizes:
            out.append(seq[at:at + k])
            at += k
        return out

    n_arr = [len(p.arrays) for p in plans]
    n_sem = [len(p.sems) for p in plans]

    def start(ins, outs, sems):
        for p, i, o, s in zip(plans, cut(ins, n_arr), cut(outs, n_arr), cut(sems, n_sem)):
            p.start(i, o, s)

    def finish(ins, outs, sems):
        for p, i, o, s in zip(plans, cut(ins, n_arr), cut(outs, n_arr), cut(sems, n_sem)):
            p.finish(i, o, s)

    def middle(ins, outs, sems):
        for p, i, o, s in zip(plans, cut(ins, n_arr), cut(outs, n_arr), cut(sems, n_sem)):
            if p.middle is not None:
                p.middle(i, o, s)

    return _Plan([a for p in plans for a in p.arrays], [o for p in plans for o in p.out_shape],
                 [s for p in plans for s in p.sems], start, finish,
                 middle if any(p.middle is not None for p in plans) else None)


def _run_plan(name, plan):
    k = len(plan.arrays)

    def body(*refs):
        ins, outs, sems = refs[:k], refs[k:2 * k], refs[2 * k:]
        plan.start(ins, outs, sems)
        if plan.middle is not None:
            plan.middle(ins, outs, sems)
        plan.finish(ins, outs, sems)

    return pl.pallas_call(
        body, name=name, in_specs=[ANY] * k, out_specs=[ANY] * k, out_shape=plan.out_shape,
        scratch_shapes=plan.sems, compiler_params=pltpu.CompilerParams(has_side_effects=True),
    )(*plan.arrays)


def _call(body, *, name, grid, in_specs, out_specs, out_shape, scratch_shapes, params, args, comm=None,
          aliases=None, prefetch=()):
    aliases = aliases or {}
    n_pre = len(prefetch)

    def launch(fn, ins_specs, outs_specs, outs_shape, scratch, operands):
        spec = pltpu.PrefetchScalarGridSpec(num_scalar_prefetch=n_pre, grid=grid, in_specs=ins_specs,
                                            out_specs=outs_specs, scratch_shapes=scratch)
        return pl.pallas_call(fn, name=name, grid_spec=spec, out_shape=outs_shape, compiler_params=params,
                              input_output_aliases=aliases)(*prefetch, *[_small_in_hbm(a) for a in operands])

    if comm is None:
        return list(launch(body, in_specs, out_specs, out_shape, scratch_shapes, args)), []
    n_in, n_out, n_scr, k = len(in_specs), len(out_specs), len(scratch_shapes), len(comm.arrays)

    def wrapped(*refs):
        pre, refs = refs[:n_pre], refs[n_pre:]
        ins = refs[:n_in]
        c_in = refs[n_in:n_in + k]
        outs = refs[n_in + k:n_in + k + n_out]
        c_out = refs[n_in + k + n_out:n_in + 2 * k + n_out]
        scr = refs[n_in + 2 * k + n_out:n_in + 2 * k + n_out + n_scr]
        sems = refs[n_in + 2 * k + n_out + n_scr:]
        step, steps = pl.program_id(0), grid[0]
        for d in range(1, len(grid)):
            step, steps = step * grid[d] + pl.program_id(d), steps * grid[d]

        @pl.when(step == 0)
        def _():
            comm.start(c_in, c_out, sems)

        if comm.middle is not None:
            @pl.when(step == (3 * steps) // 4)
            def _():
                comm.middle(c_in, c_out, sems)

        body(*pre, *ins, *outs, *scr)

        @pl.when(step == steps - 1)
        def _():
            comm.finish(c_in, c_out, sems)

    res = launch(wrapped, list(in_specs) + [ANY] * k, list(out_specs) + [ANY] * k,
                 list(out_shape) + list(comm.out_shape), list(scratch_shapes) + list(comm.sems),
                 tuple(args) + tuple(comm.arrays))
    return list(res[:n_out]), list(res[n_out:])


def _row_tile(rows):
    for t in (512, 256, 128, 64, 32, 16, 8):
        if rows % t == 0:
            return t
    return rows


def _pair_sum(name, g8, recv4, core):
    _, rows, cols = recv4.shape
    tr = _row_tile(rows)
    g42 = g8.reshape(4, 2, rows, cols)

    def body(c_ref, g_ref, r_ref, o_ref):
        del c_ref
        o_ref[...] = (g_ref[...].astype(F32) + r_ref[...].astype(F32)).astype(o_ref.dtype)

    return pl.pallas_call(
        body, name=name,
        grid_spec=pltpu.PrefetchScalarGridSpec(
            num_scalar_prefetch=1, grid=(4, rows // tr),
            in_specs=[pl.BlockSpec((None, None, tr, cols), lambda j, i, c_ref: (j, c_ref[0], i, 0)),
                      pl.BlockSpec((None, tr, cols), lambda j, i, c_ref: (j, i, 0))],
            out_specs=pl.BlockSpec((None, tr, cols), lambda j, i, c_ref: (j, i, 0))),
        out_shape=SDS(recv4.shape, g8.dtype),
        compiler_params=_cparams(("arbitrary", "arbitrary"), 32),
    )(core, g42, recv4)


def _add2(name, a, b):
    rows, cols = a.shape
    tr = _row_tile(rows)

    def body(a_ref, b_ref, o_ref):
        o_ref[...] = a_ref[...] + b_ref[...]

    blk = pl.BlockSpec((tr, cols), lambda i: (i, 0))
    return pl.pallas_call(body, name=name, grid=(rows // tr,), in_specs=[blk, blk], out_specs=blk,
                          out_shape=SDS(a.shape, a.dtype),
                          compiler_params=_cparams(("arbitrary",), 32))(a, b)


def _sum_terms(name, terms):
    k, rows, cols = terms.shape
    tr = _row_tile(rows)

    def body(r_ref, o_ref):
        acc = r_ref[0]
        for q in range(1, k):
            acc = acc + r_ref[q]
        o_ref[...] = acc

    return pl.pallas_call(body, name=name, grid=(rows // tr,),
                          in_specs=[pl.BlockSpec((k, tr, cols), lambda i: (0, i, 0))],
                          out_specs=pl.BlockSpec((tr, cols), lambda i: (i, 0)),
                          out_shape=SDS((rows, cols), terms.dtype),
                          compiler_params=_cparams(("arbitrary",), 32))(terms)


def _adam_update(g, w, m, v):
    c1 = 1.0 / (1.0 - ADAM_B1 ** ADAM_STEP)
    c2 = 1.0 / (1.0 - ADAM_B2 ** ADAM_STEP)
    mn = ADAM_B1 * m + (1.0 - ADAM_B1) * g
    vn = ADAM_B2 * v + (1.0 - ADAM_B2) * (g * g)
    delta = (-ADAM_LR) * ((mn * c1) / (jnp.sqrt(vn * c2) + ADAM_EPS) + ADAM_WD * w)
    return delta, mn, vn


def _adamw_many(name, gs, ws, ms, vs):
    n = len(gs)

    def body(*refs):
        for p in range(n):
            g, w, m, v = (refs[q * n + p][...] for q in range(4))
            d, mn, vn = _adam_update(g, w, m, v)
            refs[4 * n + p][...] = d
            refs[5 * n + p][...] = mn
            refs[6 * n + p][...] = vn

    full = [pl.BlockSpec(memory_space=pltpu.VMEM)] * n
    shapes = [SDS(w.shape, F32) for w in ws]
    res = pl.pallas_call(body, name=name, in_specs=full * 4, out_specs=full * 3, out_shape=shapes * 3,
                         compiler_params=pltpu.CompilerParams(vmem_limit_bytes=32 * MiB))(*gs, *ws, *ms, *vs)
    return [(res[p], res[n + p], res[2 * n + p]) for p in range(n)]


def _adamw(name, terms, w, m, v):
    k, rows, cols = terms.shape
    tr = _row_tile(rows)

    def body(t_ref, w_ref, m_ref, v_ref, g_ref, d_ref, mo_ref, vo_ref):
        g = t_ref[0].astype(F32)
        for q in range(1, k):
            g = g + t_ref[q].astype(F32)
        g_ref[...] = g
        d_ref[...], mo_ref[...], vo_ref[...] = _adam_update(g, w_ref[...], m_ref[...], v_ref[...])

    blk = pl.BlockSpec((tr, cols), lambda i: (i, 0))
    return pl.pallas_call(body, name=name, grid=(rows // tr,),
                          in_specs=[pl.BlockSpec((k, tr, cols), lambda i: (0, i, 0)), blk, blk, blk],
                          out_specs=[blk] * 4, out_shape=[SDS((rows, cols), F32)] * 4,
                          compiler_params=_cparams(("arbitrary",), 40))(terms, w, m, v)


def kernel(x, norm_mix_g, w_in, conv_w, conv_b, w_rgate, b_rgate, w_igate, b_igate, lru_lambda, w_out_a, sgu_ln_g, sgu_ln_b, sgu_w_s, sgu_b_s, w_out_b, w_out, norm_mlp_g, w_up, w_down, norm_final_g, loss_target, m_norm_mix_g, m_w_in, m_conv_w, m_conv_b, m_w_rgate, m_b_rgate, m_w_igate, m_b_igate, m_lru_lambda, m_w_out_a, m_sgu_ln_g, m_sgu_ln_b, m_sgu_w_s, m_sgu_b_s, m_w_out_b, m_w_out, m_norm_mlp_g, m_w_up, m_w_down, m_norm_final_g, v_norm_mix_g, v_w_in, v_conv_w, v_conv_b, v_w_rgate, v_b_rgate, v_w_igate, v_b_igate, v_lru_lambda, v_w_out_a, v_sgu_ln_g, v_sgu_ln_b, v_sgu_w_s, v_sgu_b_s, v_w_out_b, v_w_out, v_norm_mlp_g, v_w_up, v_w_down, v_norm_final_g):
    cx, cy, cc = _place()
    me = 4 * cx + 2 * cy + cc
    core = jnp.reshape(cc, (1,)).astype(jnp.int32)
    xs = x[0]
    tgt = loss_target[0]
    s = xs.shape[0]

    gate_shard = jnp.stack([w_rgate[0], w_igate[0]]).astype(BF16).reshape(2 * HEADS * 32, HEAD_DIM)
    vec_shard = jnp.concatenate([conv_w[0], b_rgate[0], b_igate[0]], axis=1)
    vec_shard = jnp.pad(vec_shard, ((0, 4), (0, 256 - vec_shard.shape[1])))
    shards = [w_in[0].astype(BF16), w_out_a[0].astype(BF16), w_out_b[0].astype(BF16), w_out[0].astype(BF16),
              w_up[0].astype(BF16), w_down[0].astype(BF16), gate_shard, vec_shard]
    (z, n1_t, w_in_g), (gate_g, vec_g) = _in_proj(xs, norm_mix_g, shards[0], _slot_order(cx, cy, cc),
                                                comm=_gather_plan(shards[6:8]))
    gates = gate_g.reshape(N_SLOT, 2, HEADS, 32, HEAD_DIM).transpose(1, 2, 0, 3, 4).reshape(2, HEADS, HEAD_DIM, HEAD_DIM)
    w_r_f, w_i_f = gates[0], gates[1]
    conv_w_f = vec_g[:, 0:4, 0:128].transpose(1, 0, 2).reshape(CONV_K, D)
    b_r_f = vec_g[:, 0:4, 128:160].transpose(1, 0, 2).reshape(1, D)
    b_i_f = vec_g[:, 0:4, 160:192].transpose(1, 0, 2).reshape(1, D)
    b_s_t = jnp.transpose(sgu_b_s[0])

    (ya, hs), (w_oa_g, w_ob_g, w_out_g) = _branch_a_fwd(
        z, conv_w_f, conv_b, w_r_f, b_r_f, w_i_f, b_i_f, lru_lambda, comm=_gather_plan(shards[1:4]))
    w_oa_f = w_oa_g.reshape(D, D)
    w_ob_f = w_ob_g.reshape(D, D)
    w_out_f = w_out_g.reshape(D, D)
    (yb,), (w_up_g,) = _branch_b_fwd(z, sgu_ln_g, sgu_ln_b, sgu_w_s[0], b_s_t, comm=_gather_plan(shards[4:5]))
    (pa, pb, merged, h1), (w_down_g,) = _merge_out(ya, yb, z, xs, w_oa_f, w_ob_f, w_out_f,
                                                   comm=_gather_plan(shards[5:6]))
    w_down_f = w_down_g.reshape(N_SLOT * FF_COLS, D)
    gf2 = norm_final_g.reshape(1, D)
    r_act, act_t, n2_t, dh2, loss_acc, d_gfin = _mlp_fwd(h1, norm_mlp_g, w_up_g, w_down_f, gf2, tgt)

    def pair(names, grads, recv):
        return [_pair_sum("pair_sum_" + nm, g, r, core) for nm, g, r in zip(names, grads, recv)]

    g_down = _wgrad_t("wgrad_down", act_t, dh2, N_SLOT // 2, True, False, 2048)
    g_down = g_down.reshape(N_SLOT, FF_COLS, D)
    (df, dh1, d_gmlp), (r_down,) = _mlp_bwd(dh2, r_act, w_down_f, w_up_g, h1, norm_mlp_g,
                                            comm=_sibling_plan([g_down]))
    (p_down,) = pair(["down"], [g_down], [r_down])
    g_up = _wgrad_t("wgrad_up", n2_t, df, N_SLOT, False, True, 4096)
    g_out = _wgrad("wgrad_out", merged, dh1, 1, False, False).reshape(N_SLOT, D // N_SLOT, D)
    (dz, dpa, dpb, dya, dyb), (got_down, r_up, r_out) = _merge_bwd(
        dh1, z, pa, pb, w_out_f, w_oa_f, w_ob_f, comm=_join(_chips_plan([p_down]), _sibling_plan([g_up, g_out])))
    p_up, p_out = pair(["up", "out"], [g_up, g_out], [r_up, r_out])
    g_oa = _wgrad("wgrad_out_a", ya, dpa, 1, False, False).reshape(N_SLOT, D // N_SLOT, D)
    g_ob = _wgrad("wgrad_out_b", yb, dpb, 1, False, False).reshape(N_SLOT, D // N_SLOT, D)
    (dz, d_ws, d_bs, d_ln), (got_up, r_oa, r_ob) = _branch_b_bwd(
        dz, dyb, z, sgu_ln_g, sgu_ln_b, sgu_w_s[0], b_s_t,
        comm=_join(_chips_plan([p_up]), _sibling_plan([g_oa, g_ob])))
    p_oa, p_ob = pair(["out_a", "out_b"], [g_oa, g_ob], [r_oa, r_ob])
    (dz, d_vec, d_wr, d_wi), (got_out, got_oa, got_ob) = _branch_a_bwd(
        dz, dya, z, hs, conv_w_f, conv_b, w_r_f, b_r_f, w_i_f, b_i_f, lru_lambda,
        comm=_chips_plan([p_out, p_oa, p_ob]))
    g_in = _wgrad_t("wgrad_in", n1_t, dz, N_SLOT, False, True, 4096)
    g_gate = jnp.stack([d_wr, d_wi]).reshape(2, HEADS, N_SLOT, 32, HEAD_DIM).transpose(2, 0, 1, 3, 4)
    g_gate = g_gate.reshape(N_SLOT, 2 * HEADS * 32, HEAD_DIM).astype(BF16)

    d_bs_row = jnp.pad(d_bs[:, :, 0].reshape(1, GROUPS * CHUNK), ((0, 0), (0, D - GROUPS * CHUNK)))
    vecs = jnp.concatenate([d_vec, jnp.concatenate([d_ln[0:2], d_gmlp, d_gfin, d_bs_row, jnp.zeros((3, D), F32)])])
    d_ws2 = d_ws.reshape(GROUPS * CHUNK, CHUNK)
    r_in, r_gate, r_vecs, r_ws = _run_plan("rs_sibling_in", _sibling_plan([g_in, g_gate], [vecs, d_ws2]))
    p_in, p_gate = pair(["in", "gate"], [g_in, g_gate], [r_in, r_gate])
    vecs_chip = _add2("pair_sum_vecs", vecs, r_vecs)
    ws_chip = _add2("pair_sum_ws", d_ws2, r_ws)
    (dx, d_gmix), (got_in, got_gate, got_vecs, got_ws) = _in_bwd(
        dz, w_in_g, xs, dh1, norm_mix_g, comm=_chips_plan([p_in, p_gate], [vecs_chip, ws_chip]))
    vecs_sum = _sum_terms("sum_vecs", got_vecs)
    last = jnp.concatenate([d_gmix, jnp.pad(loss_acc[0:1], ((0, 0), (0, D - 128))), jnp.zeros((6, D), F32)])
    (last_all,) = _run_plan("exchange_last", _exchange_plan(last))
    last_sum = _sum_terms("sum_last", last_all)
    loss = last_sum[1, 0]
    got = [got_in, got_oa, got_ob, got_out, got_up, got_down, got_gate]

    def step(nm, terms, w, m, v, rows, cols):
        g, d, mn, vn = _adamw("adamw_" + nm, terms.reshape(4, rows, cols), w.reshape(rows, cols),
                              m.reshape(rows, cols), v.reshape(rows, cols))
        return [a.reshape(w.shape) for a in (g, d, mn, vn)]

    o_in = step("in", got[0], w_in, m_w_in, v_w_in, D, W_IN_COLS)
    o_oa = step("out_a", got[1], w_out_a, m_w_out_a, v_w_out_a, D // N_SLOT, D)
    o_ob = step("out_b", got[2], w_out_b, m_w_out_b, v_w_out_b, D // N_SLOT, D)
    o_out = step("out", got[3], w_out, m_w_out, v_w_out, D // N_SLOT, D)
    o_up = step("up", got[4], w_up, m_w_up, v_w_up, D, FF_COLS)
    o_down = step("down", got[5], w_down, m_w_down, v_w_down, FF_COLS, D)
    gate_w = jnp.stack([w_rgate[0], w_igate[0]]).reshape(2 * HEADS * 32, HEAD_DIM)
    gate_m = jnp.stack([m_w_rgate[0], m_w_igate[0]]).reshape(2 * HEADS * 32, HEAD_DIM)
    gate_v = jnp.stack([v_w_rgate[0], v_w_igate[0]]).reshape(2 * HEADS * 32, HEAD_DIM)
    o_gate = _adamw("adamw_gate", got[6], gate_w, gate_m, gate_v)
    o_gate = [a.reshape(2, 1, HEADS, 32, HEAD_DIM) for a in o_gate]
    o_wr = [a[0] for a in o_gate]
    o_wi = [a[1] for a in o_gate]

    def own(full, width):
        return lax.dynamic_slice_in_dim(full, me * width, width, axis=1)

    small_g = {
        "norm_mix_g": last_sum[0:1], "conv_w": own(vecs_sum[0:4], 128), "conv_b": vecs_sum[4:5],
        "b_rgate": own(vecs_sum[5:6].reshape(HEADS, HEAD_DIM), 32),
        "b_igate": own(vecs_sum[6:7].reshape(HEADS, HEAD_DIM), 32),
        "lru_lambda": vecs_sum[7:8], "sgu_ln_g": vecs_sum[8:9], "sgu_ln_b": vecs_sum[9:10],
        "norm_mlp_g": vecs_sum[10:11], "norm_final_g": vecs_sum[11:12],
        "sgu_b_s": vecs_sum[12, 0:GROUPS * CHUNK].reshape(GROUPS, CHUNK),
    }
    small_w = {"norm_mix_g": (norm_mix_g, m_norm_mix_g, v_norm_mix_g), "conv_w": (conv_w, m_conv_w, v_conv_w),
               "conv_b": (conv_b, m_conv_b, v_conv_b), "b_rgate": (b_rgate, m_b_rgate, v_b_rgate),
               "b_igate": (b_igate, m_b_igate, v_b_igate), "lru_lambda": (lru_lambda, m_lru_lambda, v_lru_lambda),
               "sgu_ln_g": (sgu_ln_g, m_sgu_ln_g, v_sgu_ln_g), "sgu_ln_b": (sgu_ln_b, m_sgu_ln_b, v_sgu_ln_b),
               "norm_mlp_g": (norm_mlp_g, m_norm_mlp_g, v_norm_mlp_g),
               "norm_final_g": (norm_final_g, m_norm_final_g, v_norm_final_g),
               "sgu_b_s": (sgu_b_s, m_sgu_b_s, v_sgu_b_s), "sgu_w_s": (sgu_w_s, m_sgu_w_s, v_sgu_w_s)}
    order = list(small_g)
    as2d = lambda k, a: a.reshape(small_g[k].shape)
    upd = _adamw_many("adamw_small", [small_g[k] for k in order], *[[as2d(k, small_w[k][q]) for k in order]
                                                                     for q in range(3)])
    o_small = {k: [a.reshape(small_w[k][0].shape) for a in (small_g[k],) + u] for k, u in zip(order, upd)}
    ws3 = [a[0].reshape(GROUPS * CHUNK, CHUNK) for a in small_w.pop("sgu_w_s")]
    o_small["sgu_w_s"] = [a.reshape(sgu_w_s.shape) for a in _adamw("adamw_ws", got_ws, *ws3)]

    per_weight = {"norm_mix_g": o_small["norm_mix_g"], "w_in": o_in, "conv_w": o_small["conv_w"],
                  "conv_b": o_small["conv_b"], "w_rgate": o_wr, "b_rgate": o_small["b_rgate"], "w_igate": o_wi,
                  "b_igate": o_small["b_igate"], "lru_lambda": o_small["lru_lambda"], "w_out_a": o_oa,
                  "sgu_ln_g": o_small["sgu_ln_g"], "sgu_ln_b": o_small["sgu_ln_b"], "sgu_w_s": o_small["sgu_w_s"],
                  "sgu_b_s": o_small["sgu_b_s"], "w_out_b": o_ob, "w_out": o_out, "norm_mlp_g": o_small["norm_mlp_g"],
                  "w_up": o_up, "w_down": o_down, "norm_final_g": o_small["norm_final_g"]}
    names_w = list(per_weight)
    return (loss, dx[None], *[per_weight[k][0] for k in names_w], *[per_weight[k][1] for k in names_w],
            *[per_weight[k][2] for k in names_w], *[per_weight[k][3] for k in names_w])
```

```python
import jax
import jax.numpy as jnp
from jax import lax
from jax.experimental import pallas as pl
from jax.experimental.pallas import tpu as pltpu

F32 = jnp.float32
BF16 = jnp.bfloat16
SDS = jax.ShapeDtypeStruct
MESH = pl.DeviceIdType.MESH
ANY = pl.BlockSpec(memory_space=pltpu.HBM)

D = 1024
N_SLOT = 8
W_IN_COLS = 768
FF_COLS = 512
HEADS, HEAD_DIM = 4, 256
GROUPS, GROUP_DIM = 4, 256
CHUNK = 128
CONV_K = 4
NORM_EPS = 1e-6
LN_EPS = 1e-5
LRU_C = 8.0
ADAM_LR, ADAM_B1, ADAM_B2, ADAM_EPS, ADAM_WD, ADAM_STEP = 0.001, 0.9, 0.999, 1e-08, 0.01, 10

TM_ROWS = 1024
TM_MERGE = 512
T_BRANCH_A = 256
T_BRANCH_B = 256
MiB = 1024 * 1024
SMALL_OPERAND = 16 * 1024

_GELU_C = 0.7978845608028654
_GELU_A = 0.044715


def _small_in_hbm(a):
    return pltpu.with_memory_space_constraint(a, pltpu.HBM) if a.size <= SMALL_OPERAND else a


def _cparams(sem, vmem_mib):
    return pltpu.CompilerParams(dimension_semantics=sem, vmem_limit_bytes=vmem_mib * MiB)


def _gelu(x):
    t = jnp.tanh(_GELU_C * (x + _GELU_A * x * x * x))
    return 0.5 * x * (1.0 + t)


def _gelu_and_grad(x):
    x2 = x * x
    t = jnp.tanh(_GELU_C * x * (1.0 + _GELU_A * x2))
    g = 0.5 * x * (1.0 + t)
    dg = 0.5 * (1.0 + t) + 0.5 * x * (1.0 - t * t) * _GELU_C * (1.0 + 3.0 * _GELU_A * x2)
    return g, dg


def _softplus(x):
    return jnp.maximum(x, 0.0) + jnp.log1p(jnp.exp(-jnp.abs(x)))


def _dot(a, b):
    return jnp.dot(a, b, preferred_element_type=F32)


def _dot_nt(a, b):
    return lax.dot_general(a, b, (((1,), (1,)), ((), ())), preferred_element_type=F32)


def _dot_tn(a, b):
    return lax.dot_general(a, b, (((0,), (0,)), ((), ())), preferred_element_type=F32)


def _rows_shifted(prev8, cur, k):
    ext = jnp.concatenate([prev8, cur], axis=0)
    return pltpu.roll(ext, k, 0)[8:]


def _rows_advanced(cur, next8, k):
    t = cur.shape[0]
    ext = jnp.concatenate([cur, next8], axis=0)
    return pltpu.roll(ext, t + 8 - k, 0)[:t]


def _first_second(x, y, c):
    ny, nx, far = _other_chips(x, y)
    pick = lambda a, b: a * (1 - c) + b * c
    first = tuple(pick(a, b) for a, b in zip(ny, nx))
    second = tuple(pick(b, a) for a, b in zip(ny, nx))
    return first, second, far


def _slot_order(x, y, c):
    chip = 2 * x + y
    first, second, far = _first_second(x, y, c)
    order = [2 * chip + c, 2 * chip + 1 - c, 2 * first[2] + c, 2 * second[2] + 1 - c, 2 * second[2] + c,
             2 * first[2] + 1 - c, 2 * far[2] + c, 2 * far[2] + 1 - c]
    return jnp.stack(order).astype(jnp.int32)


def _in_proj(x, g_mix, w_in_own, order, comm=None):
    s = x.shape[0]
    tm = min(TM_ROWS, s)
    ni = s // tm

    def body(order_ref, x_ref, g_ref, own_ref, z_ref, nt_ref, wg_ref, n_s, w_s, send_sems, recv_sems, local_sems):
        j, i = pl.program_id(0), pl.program_id(1)
        px, py, c = _place()
        chip = 2 * px + py
        me = 2 * chip + c
        sib = (px, py, 1 - c)
        chips = _other_chips(px, py)

        def rc(k, src, blk, to):
            return pltpu.make_async_remote_copy(src_ref=src, dst_ref=w_s.at[blk], send_sem=send_sems.at[k],
                                                recv_sem=recv_sems.at[k], device_id=to, device_id_type=MESH)

        del chips
        first, second, far = _first_second(px, py, c)
        blocks = [2 * first[2] + c, 2 * second[2] + c, 2 * far[2] + c]
        own_in = pltpu.make_async_copy(own_ref, w_s.at[me], local_sems.at[0])
        to_first = rc(1, own_ref, me, (first[0], first[1], c))
        to_second = rc(2, own_ref, me, (second[0], second[1], c))
        relay = rc(3, w_s.at[blocks[0]], blocks[0], (second[0], second[1], c))
        sends = [rc(0, own_ref, me, sib), to_first, to_second, relay]
        passed = [rc(4 + q, w_s.at[blk], blk, sib) for q, blk in enumerate(blocks)]
        keep = pltpu.make_async_copy(w_s, wg_ref, local_sems.at[1])

        @pl.when((i == 0) & (j == 0))
        def _():
            own_in.start()
            sends[0].start()
            to_first.start()
            own_in.wait()

        @pl.when((i == 0) & (j == 1))
        def _():
            rc(0, own_ref, 2 * chip + 1 - c, sib).wait_recv()

        for q, blk in enumerate(blocks):
            @pl.when((i == 0) & (j == 2 + 2 * q))
            def _():
                rc(1 + q, own_ref, blk, sib).wait_recv()
                passed[q].start()
                if q == 0:
                    to_second.start()
                    relay.start()

            @pl.when((i == 0) & (j == 3 + 2 * q))
            def _():
                rc(4 + q, own_ref, order_ref[j], sib).wait_recv()

        rows = pl.ds(pl.multiple_of(i * tm, tm), tm)

        @pl.when(j == 0)
        def _():
            xv = x_ref[...]
            rstd = lax.rsqrt(jnp.mean(xv * xv, axis=-1, keepdims=True) + NORM_EPS)
            nb = (xv * rstd * g_ref[...]).astype(BF16)
            n_s[rows, :] = nb
            nt_ref[...] = nb.T

        z_ref[...] = _dot(n_s[rows, :], w_s[order_ref[j]]).astype(BF16)

        @pl.when((i == 0) & (j == N_SLOT - 1))
        def _():
            keep.start()

        @pl.when((i == ni - 1) & (j == N_SLOT - 1))
        def _():
            for cp in sends + passed:
                cp.wait_send()
            keep.wait()

    first_pass = lambda j, i, o: (jnp.where(j == 0, i, ni - 1), 0)
    (z, n1, w_in_g), extra = _call(
        body, name="in_proj", grid=(N_SLOT, ni), prefetch=(order,),
        in_specs=[pl.BlockSpec((tm, D), first_pass),
                  pl.BlockSpec((1, D), lambda j, i, o: (0, 0)), ANY],
        out_specs=[pl.BlockSpec((tm, W_IN_COLS), lambda j, i, o: (i, o[j])),
                   pl.BlockSpec((D, tm), lambda j, i, o: (0, jnp.where(j == 0, i, ni - 1))), ANY],
        out_shape=[SDS((s, N_SLOT * W_IN_COLS), BF16), SDS((D, s), BF16), SDS((N_SLOT, D, W_IN_COLS), BF16)],
        scratch_shapes=[pltpu.VMEM((s, D), BF16), pltpu.VMEM((N_SLOT, D, W_IN_COLS), BF16),
                        pltpu.SemaphoreType.DMA((7,)), pltpu.SemaphoreType.DMA((7,)), pltpu.SemaphoreType.DMA((2,))],
        params=_cparams(("arbitrary", "arbitrary"), 56), args=(x, g_mix, w_in_own), comm=comm)
    return (z, n1, w_in_g), extra


def _lru_gates(xc, xcb, wr_ref, br, wi_ref, bi, sp_lam, a_s, b_s, r_s=None, i_s=None, m_s=None):
    for h in range(HEADS):
        sl = slice(h * HEAD_DIM, (h + 1) * HEAD_DIM)
        r = jax.nn.sigmoid(_dot(xcb[:, sl], wr_ref[h]) + br[:, sl])
        ig = jax.nn.sigmoid(_dot(xcb[:, sl], wi_ref[h]) + bi[:, sl])
        log_a = (-LRU_C) * r * sp_lam[:, sl]
        a = jnp.exp(log_a)
        mult = jnp.sqrt(-jnp.tanh(log_a) * (a * a + 1.0))
        a_s[:, sl] = a
        b_s[:, sl] = xc[:, sl] * ig * mult
        if r_s is not None:
            r_s[:, sl] = r
            i_s[:, sl] = ig
            m_s[:, sl] = mult


def _conv_fwd(xa, prev8, cw, cb):
    xc = cb + cw[0:1, :] * xa
    for k in range(1, CONV_K):
        xc = xc + cw[k:k + 1, :] * _rows_shifted(prev8, xa, k)
    return xc


def _branch_a_fwd(z, conv_w, conv_b, w_r, b_r, w_i, b_i, lam, comm=None):
    s = z.shape[0]
    ta = min(T_BRANCH_A, s)
    per16 = ta // 16

    def body(xa_ref, xp_ref, ga_ref, cw_ref, cb_ref, wr_ref, br_ref, wi_ref, bi_ref, lam_ref,
             ya_ref, hs_ref, a_s, b_s, h_s, carry_s):
        i = pl.program_id(0)

        @pl.when(i == 0)
        def _():
            carry_s[...] = jnp.zeros_like(carry_s)

        xa = xa_ref[...].astype(F32)
        prev8 = jnp.where(i > 0, xp_ref[...].astype(F32)[8:16], 0.0)
        xc = _conv_fwd(xa, prev8, cw_ref[...], cb_ref[...])
        sp_lam = _softplus(-lam_ref[...])
        _lru_gates(xc, xc.astype(BF16), wr_ref, br_ref[...], wi_ref, bi_ref[...], sp_lam, a_s, b_s)

        row = lax.broadcasted_iota(jnp.int32, (8, D), 0)

        def group(g, carry):
            off = pl.multiple_of(g * 8, 8)
            a8 = a_s[pl.ds(off, 8), :]
            b8 = b_s[pl.ds(off, 8), :]
            for d in (1, 2, 4):
                a_sh = jnp.where(row >= d, pltpu.roll(a8, d, 0), 1.0)
                b_sh = jnp.where(row >= d, pltpu.roll(b8, d, 0), 0.0)
                b8 = a8 * b_sh + b8
                a8 = a8 * a_sh
            h8 = b8 + a8 * carry
            h_s[pl.ds(off, 8), :] = h8
            return jnp.broadcast_to(h8[7:8, :], (8, D))

        carry_s[...] = lax.fori_loop(0, ta // 8, group, carry_s[...])
        hs = h_s[...]
        hs_ref[...] = hs.astype(BF16)
        ya_ref[...] = (hs * _gelu(ga_ref[...].astype(F32))).astype(BF16)

    vec = pl.BlockSpec((1, D), lambda i: (0, 0))
    gate = pl.BlockSpec((HEADS, HEAD_DIM, HEAD_DIM), lambda i: (0, 0, 0))
    return _call(
        body, name="branch_a_fwd", grid=(s // ta,),
        in_specs=[pl.BlockSpec((ta, D), lambda i: (i, 0)),
                  pl.BlockSpec((16, D), lambda i: (jnp.maximum(i * per16 - 1, 0), 0)),
                  pl.BlockSpec((ta, D), lambda i: (i, 1)),
                  pl.BlockSpec((CONV_K, D), lambda i: (0, 0)), vec, gate, vec, gate, vec, vec],
        out_specs=[pl.BlockSpec((ta, D), lambda i: (i, 0)), pl.BlockSpec((ta, D), lambda i: (i, 0))],
        out_shape=[SDS((s, D), BF16), SDS((s, D), BF16)],
        scratch_shapes=[pltpu.VMEM((ta, D), F32), pltpu.VMEM((ta, D), F32), pltpu.VMEM((ta, D), F32),
                        pltpu.VMEM((8, D), F32)],
        params=_cparams(("arbitrary",), 40), args=(z, z, z, conv_w, conv_b, w_r, b_r, w_i, b_i, lam), comm=comm)


def _sgu_common(ub, vb, lg, lb, with_grad):
    if with_grad:
        u, du = _gelu_and_grad(ub)
        v, dv = _gelu_and_grad(vb)
    else:
        u, v, du, dv = _gelu(ub), _gelu(vb), None, None
    mu = jnp.mean(v, axis=-1, keepdims=True)
    vc = v - mu
    rstd = lax.rsqrt(jnp.mean(vc * vc, axis=-1, keepdims=True) + LN_EPS)
    vhat = vc * rstd
    vln = vhat * lg + lb
    return u, du, dv, rstd, vhat, vln


def _masked_ws(ws_ref):
    t = lax.broadcasted_iota(jnp.int32, (CHUNK, CHUNK), 0)
    c = lax.broadcasted_iota(jnp.int32, (CHUNK, CHUNK), 1)
    keep = c <= t
    return [jnp.where(keep, ws_ref[g], 0.0).astype(BF16) for g in range(GROUPS)]


def _branch_b_fwd(z, ln_g, ln_b, w_s, b_s_t, comm=None):
    s = z.shape[0]
    tb = min(T_BRANCH_B, s)

    def body(ub_ref, vb_ref, lg_ref, lb_ref, ws_ref, bs_ref, yb_ref):
        u, _, _, _, _, vln = _sgu_common(ub_ref[...].astype(F32), vb_ref[...].astype(F32),
                                         lg_ref[...], lb_ref[...], False)
        vlnb = vln.astype(BF16)
        wm = _masked_ws(ws_ref)
        bs = bs_ref[...]
        for c in range(tb // CHUNK):
            rs = slice(c * CHUNK, (c + 1) * CHUNK)
            for g in range(GROUPS):
                cs = slice(g * GROUP_DIM, (g + 1) * GROUP_DIM)
                sp = _dot(wm[g], vlnb[rs, cs]) + bs[:, g:g + 1]
                yb_ref[rs, cs] = (u[rs, cs] * sp).astype(BF16)

    vec = pl.BlockSpec((1, D), lambda i: (0, 0))
    return _call(
        body, name="branch_b_fwd", grid=(s // tb,),
        in_specs=[pl.BlockSpec((tb, D), lambda i: (i, 2)), pl.BlockSpec((tb, D), lambda i: (i, 3)), vec, vec,
                  pl.BlockSpec((GROUPS, CHUNK, CHUNK), lambda i: (0, 0, 0)),
                  pl.BlockSpec((CHUNK, GROUPS), lambda i: (0, 0))],
        out_specs=[pl.BlockSpec((tb, D), lambda i: (i, 0))],
        out_shape=[SDS((s, D), BF16)], scratch_shapes=[],
        params=_cparams(("arbitrary",), 40), args=(z, z, ln_g, ln_b, w_s, b_s_t), comm=comm)


def _merge_out(ya, yb, z, x, w_oa, w_ob, w_out, comm=None):
    s = x.shape[0]
    tm = min(TM_MERGE, s)

    def body(ya_ref, yb_ref, ma_ref, mb_ref, x_ref, woa_ref, wob_ref, wo_ref, pa_ref, pb_ref, mg_ref, h1_ref):
        pa = _dot(ya_ref[...], woa_ref[...])
        pb = _dot(yb_ref[...], wob_ref[...])
        merged = (jax.nn.sigmoid(ma_ref[...].astype(F32)) * pa
                  + jax.nn.sigmoid(mb_ref[...].astype(F32)) * pb).astype(BF16)
        pa_ref[...] = pa.astype(BF16)
        pb_ref[...] = pb.astype(BF16)
        mg_ref[...] = merged
        h1_ref[...] = x_ref[...] + _dot(merged, wo_ref[...])

    row = pl.BlockSpec((tm, D), lambda i: (i, 0))
    wsp = pl.BlockSpec((D, D), lambda i: (0, 0))
    return _call(
        body, name="merge_out", grid=(s // tm,),
        in_specs=[row, row, pl.BlockSpec((tm, D), lambda i: (i, 4)), pl.BlockSpec((tm, D), lambda i: (i, 5)),
                  row, wsp, wsp, wsp],
        out_specs=[row, row, row, row],
        out_shape=[SDS((s, D), BF16), SDS((s, D), BF16), SDS((s, D), BF16), SDS((s, D), F32)], scratch_shapes=[],
        params=_cparams(("arbitrary",), 48), args=(ya, yb, z, z, x, w_oa, w_ob, w_out), comm=comm)


def _mlp_fwd(h1, g_mlp, w_up_g, w_down, g_fin, tgt):
    s = h1.shape[0]
    tm = min(TM_ROWS, s)
    nj = N_SLOT

    def body(h1_ref, gm_ref, wu_ref, wd_ref, gf_ref, t_ref, r_ref, at_ref, n2t_ref, dh2_ref, loss_ref, dgf_ref,
             n2_s, acc_s):
        i, j = pl.program_id(0), pl.program_id(1)

        @pl.when(j == 0)
        def _():
            hv = h1_ref[...]
            rstd = lax.rsqrt(jnp.mean(hv * hv, axis=-1, keepdims=True) + NORM_EPS)
            nb = (hv * rstd * gm_ref[...]).astype(BF16)
            n2_s[...] = nb
            n2t_ref[...] = nb.T
            acc_s[...] = jnp.zeros_like(acc_s)

        @pl.when((i == 0) & (j == 0))
        def _():
            loss_ref[...] = jnp.zeros_like(loss_ref)
            dgf_ref[...] = jnp.zeros_like(dgf_ref)

        r = jnp.maximum(_dot(n2_s[...], wu_ref[...]), 0.0)
        r_ref[...] = r.astype(BF16)
        act = (r * r).astype(BF16)
        at_ref[...] = act.T
        acc_s[...] += _dot(act, wd_ref[...])

        @pl.when(j == nj - 1)
        def _():
            h2 = h1_ref[...] + acc_s[...]
            rstd = lax.rsqrt(jnp.mean(h2 * h2, axis=-1, keepdims=True) + NORM_EPS)
            hh = h2 * rstd
            gf = gf_ref[...]
            e = hh * gf - t_ref[...]
            loss_ref[...] += jnp.sum(e * e) * (0.5 / D)
            dy = e * (1.0 / D)
            dgf_ref[...] += jnp.sum(dy * hh, axis=0, keepdims=True)
            dhh = dy * gf
            dh2_ref[...] = rstd * (dhh - hh * jnp.mean(dhh * hh, axis=-1, keepdims=True))

    row = pl.BlockSpec((tm, D), lambda i, j: (i, 0))
    vec = pl.BlockSpec((1, D), lambda i, j: (0, 0))
    return pl.pallas_call(
        body, name="mlp_fwd", grid=(s // tm, nj),
        in_specs=[row, vec, pl.BlockSpec((None, D, FF_COLS), lambda i, j: (j, 0, 0)),
                  pl.BlockSpec((FF_COLS, D), lambda i, j: (j, 0)), vec, row],
        out_specs=[pl.BlockSpec((tm, FF_COLS), lambda i, j: (i, j)), pl.BlockSpec((FF_COLS, tm), lambda i, j: (j, i)),
                   pl.BlockSpec((D, tm), lambda i, j: (0, i)), row, pl.BlockSpec((8, 128), lambda i, j: (0, 0)), vec],
        out_shape=[SDS((s, nj * FF_COLS), BF16), SDS((nj * FF_COLS, s), BF16), SDS((D, s), BF16), SDS((s, D), F32),
                   SDS((8, 128), F32), SDS((1, D), F32)],
        scratch_shapes=[pltpu.VMEM((tm, D), BF16), pltpu.VMEM((tm, D), F32)],
        compiler_params=_cparams(("arbitrary", "arbitrary"), 52),
    )(h1, _small_in_hbm(g_mlp), w_up_g, w_down, _small_in_hbm(g_fin), tgt)


def _mlp_bwd(dh2, r, w_down, w_up_g, h1, g_mlp, comm=None):
    s = h1.shape[0]
    tm = min(TM_ROWS, s)
    nj = N_SLOT

    def body(dh2_ref, r_ref, wd_ref, wu_ref, h1_ref, gm_ref, df_ref, dh1_ref, dgm_ref, dh2b_s, acc_s):
        i, j = pl.program_id(0), pl.program_id(1)

        @pl.when(j == 0)
        def _():
            dh2b_s[...] = dh2_ref[...].astype(BF16)
            acc_s[...] = jnp.zeros_like(acc_s)

        @pl.when((i == 0) & (j == 0))
        def _():
            dgm_ref[...] = jnp.zeros_like(dgm_ref)

        d_act = _dot_nt(dh2b_s[...], wd_ref[...])
        df = (d_act * (2.0 * r_ref[...].astype(F32))).astype(BF16)
        df_ref[...] = df
        acc_s[...] += _dot_nt(df, wu_ref[...])

        @pl.when(j == nj - 1)
        def _():
            hv = h1_ref[...]
            rstd = lax.rsqrt(jnp.mean(hv * hv, axis=-1, keepdims=True) + NORM_EPS)
            hh = hv * rstd
            dn2 = acc_s[...]
            dgm_ref[...] += jnp.sum(dn2 * hh, axis=0, keepdims=True)
            dhat = dn2 * gm_ref[...]
            dh1_ref[...] = dh2_ref[...] + rstd * (dhat - hh * jnp.mean(dhat * hh, axis=-1, keepdims=True))

    row = pl.BlockSpec((tm, D), lambda i, j: (i, 0))
    vec = pl.BlockSpec((1, D), lambda i, j: (0, 0))
    ffb = pl.BlockSpec((tm, FF_COLS), lambda i, j: (i, j))
    return _call(
        body, name="mlp_bwd", grid=(s // tm, nj),
        in_specs=[row, ffb, pl.BlockSpec((FF_COLS, D), lambda i, j: (j, 0)),
                  pl.BlockSpec((None, D, FF_COLS), lambda i, j: (j, 0, 0)), row, vec],
        out_specs=[ffb, row, vec],
        out_shape=[SDS((s, nj * FF_COLS), BF16), SDS((s, D), F32), SDS((1, D), F32)],
        scratch_shapes=[pltpu.VMEM((tm, D), BF16), pltpu.VMEM((tm, D), F32)],
        params=_cparams(("arbitrary", "arbitrary"), 52), args=(dh2, r, w_down, w_up_g, h1, g_mlp), comm=comm)


def _merge_bwd(dh1, z, pa, pb, w_out, w_oa, w_ob, comm=None):
    s = dh1.shape[0]
    tm = min(TM_MERGE, s)

    def body(dh1_ref, ma_ref, mb_ref, pa_ref, pb_ref, wo_ref, woa_ref, wob_ref,
             dz_ref, dpa_ref, dpb_ref, dya_ref, dyb_ref):
        dm = _dot_nt(dh1_ref[...].astype(BF16), wo_ref[...])
        sa = jax.nn.sigmoid(ma_ref[...].astype(F32))
        sb = jax.nn.sigmoid(mb_ref[...].astype(F32))
        dpa = (dm * sa).astype(BF16)
        dpb = (dm * sb).astype(BF16)
        dz_ref[:, 0:D] = (dm * pa_ref[...].astype(F32) * sa * (1.0 - sa)).astype(BF16)
        dz_ref[:, D:2 * D] = (dm * pb_ref[...].astype(F32) * sb * (1.0 - sb)).astype(BF16)
        dpa_ref[...] = dpa
        dpb_ref[...] = dpb
        dya_ref[...] = _dot_nt(dpa, woa_ref[...]).astype(BF16)
        dyb_ref[...] = _dot_nt(dpb, wob_ref[...]).astype(BF16)

    row = pl.BlockSpec((tm, D), lambda i: (i, 0))
    wsp = pl.BlockSpec((D, D), lambda i: (0, 0))
    return _call(
        body, name="merge_bwd", grid=(s // tm,),
        in_specs=[row, pl.BlockSpec((tm, D), lambda i: (i, 4)), pl.BlockSpec((tm, D), lambda i: (i, 5)),
                  row, row, wsp, wsp, wsp],
        out_specs=[pl.BlockSpec((tm, 2 * D), lambda i: (i, 2)), row, row, row, row],
        out_shape=[SDS((s, 6 * D), BF16)] + [SDS((s, D), BF16)] * 4, scratch_shapes=[],
        params=_cparams(("arbitrary",), 48), args=(dh1, z, z, pa, pb, w_out, w_oa, w_ob), comm=comm)


def _branch_b_bwd(dz, dyb, z, ln_g, ln_b, w_s, b_s_t, comm=None):
    s = z.shape[0]
    tb = min(T_BRANCH_B, s)

    def body(dz_in, dyb_ref, ub_ref, vb_ref, lg_ref, lb_ref, ws_ref, bs_ref,
             dz_ref, dws_ref, dbs_ref, dln_ref, du_s, dvln_s):
        del dz_in

        @pl.when(pl.program_id(0) == 0)
        def _():
            dws_ref[...] = jnp.zeros_like(dws_ref)
            dbs_ref[...] = jnp.zeros_like(dbs_ref)
            dln_ref[...] = jnp.zeros_like(dln_ref)

        lg = lg_ref[...]
        u, du, dv, rstd, vhat, vln = _sgu_common(ub_ref[...].astype(F32), vb_ref[...].astype(F32),
                                                 lg, lb_ref[...], True)
        vlnb = vln.astype(BF16)
        dyb_v = dyb_ref[...].astype(F32)
        wm = _masked_ws(ws_ref)
        keep = (lax.broadcasted_iota(jnp.int32, (CHUNK, CHUNK), 1)
                <= lax.broadcasted_iota(jnp.int32, (CHUNK, CHUNK), 0))
        bs = bs_ref[...]
        for c in range(tb // CHUNK):
            rs = slice(c * CHUNK, (c + 1) * CHUNK)
            for g in range(GROUPS):
                cs = slice(g * GROUP_DIM, (g + 1) * GROUP_DIM)
                v_blk = vlnb[rs, cs]
                sp = _dot(wm[g], v_blk) + bs[:, g:g + 1]
                d_sp = dyb_v[rs, cs] * u[rs, cs]
                d_spb = d_sp.astype(BF16)
                du_s[rs, cs] = dyb_v[rs, cs] * sp
                dvln_s[rs, cs] = _dot_tn(wm[g], d_spb)
                dws_ref[g] += jnp.where(keep, _dot_nt(d_spb, v_blk), 0.0)
                dbs_ref[g] += jnp.broadcast_to(jnp.sum(d_sp, axis=-1, keepdims=True), (CHUNK, CHUNK))
        dvln = dvln_s[...]
        dln_ref[0:1, :] += jnp.sum(dvln * vhat, axis=0, keepdims=True)
        dln_ref[1:2, :] += jnp.sum(dvln, axis=0, keepdims=True)
        dvh = dvln * lg
        d_v = rstd * (dvh - jnp.mean(dvh, axis=-1, keepdims=True)
                      - vhat * jnp.mean(dvh * vhat, axis=-1, keepdims=True))
        dz_ref[:, 0:D] = (du_s[...] * du).astype(BF16)
        dz_ref[:, D:2 * D] = (d_v * dv).astype(BF16)

    vec = pl.BlockSpec((1, D), lambda i: (0, 0))
    sq = pl.BlockSpec((GROUPS, CHUNK, CHUNK), lambda i: (0, 0, 0))
    return _call(
        body, name="branch_b_bwd", grid=(s // tb,),
        in_specs=[ANY, pl.BlockSpec((tb, D), lambda i: (i, 0)),
                  pl.BlockSpec((tb, D), lambda i: (i, 2)), pl.BlockSpec((tb, D), lambda i: (i, 3)), vec, vec, sq,
                  pl.BlockSpec((CHUNK, GROUPS), lambda i: (0, 0))],
        out_specs=[pl.BlockSpec((tb, 2 * D), lambda i: (i, 1)), sq, sq, pl.BlockSpec((8, D), lambda i: (0, 0))],
        out_shape=[SDS(dz.shape, BF16), SDS((GROUPS, CHUNK, CHUNK), F32), SDS((GROUPS, CHUNK, CHUNK), F32),
                   SDS((8, D), F32)],
        scratch_shapes=[pltpu.VMEM((tb, D), F32), pltpu.VMEM((tb, D), F32)], aliases={0: 0},
        params=_cparams(("arbitrary",), 40), args=(dz, dyb, z, z, ln_g, ln_b, w_s, b_s_t), comm=comm)


def _branch_a_bwd(dz, dya, z, hs, conv_w, conv_b, w_r, b_r, w_i, b_i, lam, comm=None):
    s = z.shape[0]
    ta = min(T_BRANCH_A, s)
    nb = s // ta
    per16 = ta // 16

    def body(dz_in, dya_ref, xa_ref, xp_ref, ga_ref, hs_ref, hp_ref, cw_ref, cb_ref, wr_ref, br_ref, wi_ref,
             bi_ref, lam_ref, dz_ref, vec_ref, dwr_ref, dwi_ref,
             a_s, b_s, h_s, r_s, i_s, m_s, dcar_s, acar_s, dxc_s):
        del dz_in
        i = pl.program_id(0)
        blk = nb - 1 - i

        @pl.when(i == 0)
        def _():
            dcar_s[...] = jnp.zeros_like(dcar_s)
            acar_s[...] = jnp.zeros_like(acar_s)
            dxc_s[...] = jnp.zeros_like(dxc_s)
            vec_ref[...] = jnp.zeros_like(vec_ref)
            dwr_ref[...] = jnp.zeros_like(dwr_ref)
            dwi_ref[...] = jnp.zeros_like(dwi_ref)

        cw = cw_ref[...]
        lam_v = lam_ref[...]
        xa = xa_ref[...].astype(F32)
        prev8 = jnp.where(blk > 0, xp_ref[...].astype(F32)[8:16], 0.0)
        xc = _conv_fwd(xa, prev8, cw, cb_ref[...])
        xcb = xc.astype(BF16)
        sp_lam = _softplus(-lam_v)
        _lru_gates(xc, xcb, wr_ref, br_ref[...], wi_ref, bi_ref[...], sp_lam, a_s, b_s, r_s, i_s, m_s)

        hs_v = hs_ref[...].astype(F32)
        hprev8 = jnp.where(blk > 0, hp_ref[...].astype(F32)[8:16], 0.0)
        h_m1 = _rows_shifted(hprev8, hs_v, 1)
        gg, dgg = _gelu_and_grad(ga_ref[...].astype(F32))
        dya_v = dya_ref[...].astype(F32)
        dz_ref[:, D:2 * D] = (dya_v * hs_v * dgg).astype(BF16)

        a_v = a_s[...]
        a_s[...] = _rows_advanced(a_v, acar_s[...], 1)
        b_s[...] = dya_v * gg

        row = lax.broadcasted_iota(jnp.int32, (8, D), 0)
        ng = ta // 8

        def group(gi, carry):
            off = pl.multiple_of((ng - 1 - gi) * 8, 8)
            c8 = a_s[pl.ds(off, 8), :]
            d8 = b_s[pl.ds(off, 8), :]
            for d in (1, 2, 4):
                c_sh = jnp.where(row < 8 - d, pltpu.roll(c8, 8 - d, 0), 1.0)
                d_sh = jnp.where(row < 8 - d, pltpu.roll(d8, 8 - d, 0), 0.0)
                d8 = c8 * d_sh + d8
                c8 = c8 * c_sh
            dh8 = d8 + c8 * carry
            h_s[pl.ds(off, 8), :] = dh8
            return jnp.broadcast_to(dh8[0:1, :], (8, D))

        dcar_s[...] = lax.fori_loop(0, ng, group, dcar_s[...])
        acar_s[...] = jnp.broadcast_to(a_v[0:1, :], (8, D))

        dbx = h_s[...]
        r_v, i_v, m_v = r_s[...], i_s[...], m_s[...]
        d_mult = dbx * xc * i_v
        d_loga = dbx * h_m1 * a_v - d_mult * (a_v * a_v) / m_v
        d_pr = d_loga * ((-LRU_C) * sp_lam) * r_v * (1.0 - r_v)
        d_pi = dbx * xc * m_v * i_v * (1.0 - i_v)
        vec_ref[7:8, :] += jnp.sum(d_loga * r_v, axis=0, keepdims=True) * (LRU_C * jax.nn.sigmoid(-lam_v))
        vec_ref[5:6, :] += jnp.sum(d_pr, axis=0, keepdims=True)
        vec_ref[6:7, :] += jnp.sum(d_pi, axis=0, keepdims=True)
        d_prb = d_pr.astype(BF16)
        d_pib = d_pi.astype(BF16)
        h_s[...] = dbx * i_v * m_v
        for h in range(HEADS):
            sl = slice(h * HEAD_DIM, (h + 1) * HEAD_DIM)
            h_s[:, sl] += _dot_nt(d_prb[:, sl], wr_ref[h]) + _dot_nt(d_pib[:, sl], wi_ref[h])
            dwr_ref[h] += _dot_tn(xcb[:, sl], d_prb[:, sl])
            dwi_ref[h] += _dot_tn(xcb[:, sl], d_pib[:, sl])
        d_xc = h_s[...]
        vec_ref[4:5, :] += jnp.sum(d_xc, axis=0, keepdims=True)
        vec_ref[0:1, :] += jnp.sum(d_xc * xa, axis=0, keepdims=True)
        d_xa = cw[0:1, :] * d_xc
        nxt = dxc_s[...]
        for k in range(1, CONV_K):
            vec_ref[k:k + 1, :] += jnp.sum(d_xc * _rows_shifted(prev8, xa, k), axis=0, keepdims=True)
            d_xa = d_xa + cw[k:k + 1, :] * _rows_advanced(d_xc, nxt, k)
        dz_ref[:, 0:D] = d_xa.astype(BF16)
        dxc_s[...] = d_xc[0:8, :]

    vec = pl.BlockSpec((1, D), lambda i: (0, 0))
    gate = pl.BlockSpec((HEADS, HEAD_DIM, HEAD_DIM), lambda i: (0, 0, 0))
    cur = lambda c: pl.BlockSpec((ta, D), lambda i: (nb - 1 - i, c))
    before = lambda c: pl.BlockSpec((16, D), lambda i: (jnp.maximum((nb - 1 - i) * per16 - 1, 0), c))
    return _call(
        body, name="branch_a_bwd", grid=(nb,),
        in_specs=[ANY, cur(0), cur(0), before(0), cur(1), cur(0), before(0),
                  pl.BlockSpec((CONV_K, D), lambda i: (0, 0)), vec, gate, vec, gate, vec, vec],
        out_specs=[pl.BlockSpec((ta, 2 * D), lambda i: (nb - 1 - i, 0)), pl.BlockSpec((8, D), lambda i: (0, 0)),
                   gate, gate],
        out_shape=[SDS(dz.shape, BF16), SDS((8, D), F32), SDS((HEADS, HEAD_DIM, HEAD_DIM), F32),
                   SDS((HEADS, HEAD_DIM, HEAD_DIM), F32)],
        scratch_shapes=[pltpu.VMEM((ta, D), F32)] * 6 + [pltpu.VMEM((8, D), F32)] * 3, aliases={0: 0},
        params=_cparams(("arbitrary",), 48),
        args=(dz, dya, z, z, z, hs, hs, conv_w, conv_b, w_r, b_r, w_i, b_i, lam), comm=comm)


def _in_bwd(dz, w_in_g, x, dh1, g_mix, comm=None):
    s = x.shape[0]
    tm = min(TM_ROWS, s)
    nj = N_SLOT

    def body(dz_ref, w_ref, x_ref, dh1_ref, g_ref, dx_ref, dg_ref, acc_s):
        i, j = pl.program_id(0), pl.program_id(1)

        @pl.when(j == 0)
        def _():
            acc_s[...] = jnp.zeros_like(acc_s)

        @pl.when((i == 0) & (j == 0))
        def _():
            dg_ref[...] = jnp.zeros_like(dg_ref)

        acc_s[...] += _dot_nt(dz_ref[...], w_ref[...])

        @pl.when(j == nj - 1)
        def _():
            xv = x_ref[...]
            rstd = lax.rsqrt(jnp.mean(xv * xv, axis=-1, keepdims=True) + NORM_EPS)
            xh = xv * rstd
            dn = acc_s[...]
            dg_ref[...] += jnp.sum(dn * xh, axis=0, keepdims=True)
            dhat = dn * g_ref[...]
            dx_ref[...] = dh1_ref[...] + rstd * (dhat - xh * jnp.mean(dhat * xh, axis=-1, keepdims=True))

    row = pl.BlockSpec((tm, D), lambda i, j: (i, 0))
    vec = pl.BlockSpec((1, D), lambda i, j: (0, 0))
    return _call(
        body, name="in_bwd", grid=(s // tm, nj),
        in_specs=[pl.BlockSpec((tm, W_IN_COLS), lambda i, j: (i, j)),
                  pl.BlockSpec((None, D, W_IN_COLS), lambda i, j: (j, 0, 0)), row, row, vec],
        out_specs=[row, vec],
        out_shape=[SDS((s, D), F32), SDS((1, D), F32)],
        scratch_shapes=[pltpu.VMEM((tm, D), F32)],
        params=_cparams(("arbitrary", "arbitrary"), 48), args=(dz, w_in_g, x, dh1, g_mix), comm=comm)


def _wgrad(name, a, b, nblk, a_split, b_split):
    s = a.shape[0]
    ts = min(TM_ROWS, s)
    a_w = a.shape[1] // nblk if a_split else a.shape[1]
    b_w = b.shape[1] // nblk if b_split else b.shape[1]

    def body(a_ref, b_ref, o_ref, acc_s):
        t = pl.program_id(1)

        @pl.when(t == 0)
        def _():
            acc_s[...] = jnp.zeros_like(acc_s)

        acc_s[...] += _dot_tn(a_ref[...].astype(BF16), b_ref[...].astype(BF16))

        @pl.when(t == pl.num_programs(1) - 1)
        def _():
            o_ref[...] = acc_s[...].astype(BF16)

    return pl.pallas_call(
        body, name=name, grid=(nblk, s // ts),
        in_specs=[pl.BlockSpec((ts, a_w), (lambda k, t: (t, k)) if a_split else (lambda k, t: (t, 0))),
                  pl.BlockSpec((ts, b_w), (lambda k, t: (t, k)) if b_split else (lambda k, t: (t, 0)))],
        out_specs=pl.BlockSpec((None, a_w, b_w), lambda k, t: (k, 0, 0)),
        out_shape=SDS((nblk, a_w, b_w), BF16),
        scratch_shapes=[pltpu.VMEM((a_w, b_w), F32)],
        compiler_params=_cparams(("arbitrary", "arbitrary"), 48),
    )(a, b)


def _wgrad_t(name, a_t, b, nblk, a_split, b_split, tokens):
    s = b.shape[0]
    ts = min(tokens, s)
    a_w = a_t.shape[0] // nblk if a_split else a_t.shape[0]
    b_w = b.shape[1] // nblk if b_split else b.shape[1]

    def body(a_ref, b_ref, o_ref, acc_s):
        t = pl.program_id(1)

        @pl.when(t == 0)
        def _():
            acc_s[...] = jnp.zeros_like(acc_s)

        acc_s[...] += _dot(a_ref[...], b_ref[...].astype(BF16))

        @pl.when(t == pl.num_programs(1) - 1)
        def _():
            o_ref[...] = acc_s[...].astype(BF16)

    return pl.pallas_call(
        body, name=name, grid=(nblk, s // ts),
        in_specs=[pl.BlockSpec((a_w, ts), (lambda k, t: (k, t)) if a_split else (lambda k, t: (0, t))),
                  pl.BlockSpec((ts, b_w), (lambda k, t: (t, k)) if b_split else (lambda k, t: (t, 0)))],
        out_specs=pl.BlockSpec((None, a_w, b_w), lambda k, t: (k, 0, 0)),
        out_shape=SDS((nblk, a_w, b_w), BF16),
        scratch_shapes=[pltpu.VMEM((a_w, b_w), F32)],
        compiler_params=_cparams(("arbitrary", "arbitrary"), 48),
    )(a_t, b)


def _place():
    x, y, c = lax.axis_index("x"), lax.axis_index("y"), lax.axis_index("c")
    return x, y, c


def _other_chips(x, y):
    return [(x, 1 - y, 2 * x + 1 - y), (1 - x, y, 2 * (1 - x) + y), (1 - x, 1 - y, 2 * (1 - x) + 1 - y)]


class _Plan:
    def __init__(self, arrays, out_shape, sems, start, finish, middle=None):
        self.arrays, self.out_shape, self.sems, self.start, self.finish = arrays, out_shape, sems, start, finish
        self.middle = middle


def _gather_plan(shards):
    n = len(shards)

    def copies(ins, outs, sems):
        send_sems, recv_sems, local_sems = sems
        x, y, c = _place()
        chip = 2 * x + y
        me = 2 * chip + c
        sib = (x, y, 1 - c)
        chips = _other_chips(x, y)

        def rc(k, t, src, blk, to):
            return pltpu.make_async_remote_copy(
                src_ref=src, dst_ref=outs[t].at[blk], send_sem=send_sems.at[k * n + t],
                recv_sem=recv_sems.at[k * n + t], device_id=to, device_id_type=MESH)

        (yx, yy, y_chip), (xx, xy, x_chip), _ = chips
        local = [pltpu.make_async_copy(ins[t], outs[t].at[me], local_sems.at[t]) for t in range(n)]
        sends = ([rc(0, t, ins[t], me, sib) for t in range(n)] + [rc(1, t, ins[t], me, (yx, yy, c)) for t in range(n)]
                 + [rc(2, t, ins[t], me, (xx, xy, c)) for t in range(n)])
        passed = [[rc(4 + j, t, outs[t].at[2 * pc + c], 2 * pc + c, sib) for t in range(n)]
                  for j, (_, _, pc) in enumerate(chips)]
        relays = [[rc(3, t, outs[t].at[2 * y_chip + c], 2 * y_chip + c, (xx, xy, c)) for t in range(n)],
                  [rc(3, t, outs[t].at[2 * x_chip + c], 2 * x_chip + c, (yx, yy, c)) for t in range(n)]]
        return rc, local, sends, passed, relays, chips, chip, c, sib

    def start(ins, outs, sems):
        _, local, sends, _, _, _, _, _, _ = copies(ins, outs, sems)
        for cp in local + sends:
            cp.start()

    def middle(ins, outs, sems):
        rc, _, _, passed, relays, chips, _, c, sib = copies(ins, outs, sems)
        for j in range(2):
            for t in range(n):
                rc(1 + j, t, ins[t], 2 * chips[j][2] + c, sib).wait_recv()
        for j in range(2):
            for cp in passed[j]:
                cp.start()

            @pl.when(c == j)
            def _():
                for cp in relays[j]:
                    cp.start()

    def finish(ins, outs, sems):
        rc, local, sends, passed, relays, chips, chip, c, sib = copies(ins, outs, sems)
        far = 2 * chips[2][2] + c
        for t in range(n):
            rc(3, t, ins[t], far, sib).wait_recv()
        for cp in passed[2]:
            cp.start()
        for t in range(n):
            rc(0, t, ins[t], 2 * chip + 1 - c, sib).wait_recv()
        for j, (px, py, pc) in enumerate(chips):
            for t in range(n):
                rc(4 + j, t, ins[t], 2 * pc + 1 - c, sib).wait_recv()
        for cp in sends + passed[0] + passed[1] + passed[2]:
            cp.wait_send()
        for j in range(2):
            @pl.when(c == j)
            def _():
                for cp in relays[j]:
                    cp.wait_send()
        for cp in local:
            cp.wait()

    return _Plan(list(shards), [SDS((N_SLOT,) + tuple(a.shape), a.dtype) for a in shards],
                 [pltpu.SemaphoreType.DMA((7 * n,)), pltpu.SemaphoreType.DMA((7 * n,)),
                  pltpu.SemaphoreType.DMA((n,))], start, finish, middle)


def _sibling_plan(grads, whole=()):
    n, m = len(grads), len(whole)

    def copies(ins, outs, sems):
        send_sems, recv_sems = sems
        x, y, c = _place()
        sib = (x, y, 1 - c)

        def rc(t, src, dst):
            return pltpu.make_async_remote_copy(src_ref=src, dst_ref=dst, send_sem=send_sems.at[t],
                                                recv_sem=recv_sems.at[t], device_id=sib, device_id_type=MESH)
        return rc, c

    def start(ins, outs, sems):
        rc, c = copies(ins, outs, sems)
        for t in range(n):
            for j in range(4):
                rc(t, ins[t].at[2 * j + 1 - c], outs[t].at[j]).start()
        for t in range(n, n + m):
            rc(t, ins[t], outs[t]).start()

    def finish(ins, outs, sems):
        rc, _ = copies(ins, outs, sems)
        for t in range(n):
            rc(t, ins[t].at[pl.ds(0, 4)], outs[t]).wait()
        for t in range(n, n + m):
            rc(t, ins[t], outs[t]).wait()

    return _Plan(list(grads) + list(whole),
                 [SDS((4,) + tuple(g.shape[1:]), g.dtype) for g in grads] + [SDS(a.shape, a.dtype) for a in whole],
                 [pltpu.SemaphoreType.DMA((n + m,)), pltpu.SemaphoreType.DMA((n + m,))], start, finish)


def _chips_plan(parts, whole=()):
    n, m = len(parts), len(whole)

    def src_of(ins, t, pc):
        return ins[t].at[pc] if t < n else ins[t]

    def local_copies(ins, outs, sems, chip):
        return [pltpu.make_async_copy(src_of(ins, t, chip), outs[t].at[chip], sems[2].at[t]) for t in range(n + m)]

    def start(ins, outs, sems):
        send_sems, recv_sems, _ = sems
        x, y, c = _place()
        chip = 2 * x + y
        for cp in local_copies(ins, outs, sems, chip):
            cp.start()
        for px, py, pc in _other_chips(x, y):
            for t in range(n + m):
                pltpu.make_async_remote_copy(src_ref=src_of(ins, t, pc), dst_ref=outs[t].at[chip],
                                             send_sem=send_sems.at[t], recv_sem=recv_sems.at[t],
                                             device_id=(px, py, c), device_id_type=MESH).start()

    def finish(ins, outs, sems):
        send_sems, recv_sems, _ = sems
        x, y, c = _place()
        for t in range(n + m):
            three = outs[t].at[pl.ds(0, 3)]
            pltpu.make_async_remote_copy(src_ref=three, dst_ref=three, send_sem=send_sems.at[t],
                                         recv_sem=recv_sems.at[t], device_id=(x, y, c), device_id_type=MESH).wait()
        for cp in local_copies(ins, outs, sems, 2 * x + y):
            cp.wait()

    return _Plan(list(parts) + list(whole),
                 [SDS(p.shape, p.dtype) for p in parts] + [SDS((4,) + tuple(a.shape), a.dtype) for a in whole],
                 [pltpu.SemaphoreType.DMA((n + m,)), pltpu.SemaphoreType.DMA((n + m,)),
                  pltpu.SemaphoreType.DMA((n + m,))], start, finish)


def _exchange_plan(arr):
    def peers(x, y, c):
        flip = lambda v, f: 1 - v if f else v
        return [(flip(x, fx), flip(y, fy), flip(c, fc))
                for fx in (0, 1) for fy in (0, 1) for fc in (0, 1) if fx or fy or fc]

    def start(ins, outs, sems):
        x, y, c = _place()
        me = 4 * x + 2 * y + c
        pltpu.make_async_copy(ins[0], outs[0].at[me], sems[2].at[0]).start()
        for to in peers(x, y, c):
            pltpu.make_async_remote_copy(src_ref=ins[0], dst_ref=outs[0].at[me], send_sem=sems[0].at[0],
                                         recv_sem=sems[1].at[0], device_id=to, device_id_type=MESH).start()

    def finish(ins, outs, sems):
        x, y, c = _place()
        seven = outs[0].at[pl.ds(0, 7)]
        pltpu.make_async_remote_copy(src_ref=seven, dst_ref=seven, send_sem=sems[0].at[0], recv_sem=sems[1].at[0],
                                     device_id=(x, y, c), device_id_type=MESH).wait()
        pltpu.make_async_copy(ins[0], outs[0].at[4 * x + 2 * y + c], sems[2].at[0]).wait()

    return _Plan([arr], [SDS((N_SLOT,) + tuple(arr.shape), arr.dtype)],
                 [pltpu.SemaphoreType.DMA((1,)), pltpu.SemaphoreType.DMA((1,)), pltpu.SemaphoreType.DMA((1,))],
                 start, finish)


def _join(*plans):
    def cut(seq, sizes):
        out, at = [], 0
        for k in sizes:
            out.append(seq[at:at + k])
            at += k
        return out

    n_arr = [len(p.arrays) for p in plans]
    n_sem = [len(p.sems) for p in plans]

    def start(ins, outs, sems):
        for p, i, o, s in zip(plans, cut(ins, n_arr), cut(outs, n_arr), cut(sems, n_sem)):
            p.start(i, o, s)

    def finish(ins, outs, sems):
        for p, i, o, s in zip(plans, cut(ins, n_arr), cut(outs, n_arr), cut(sems, n_sem)):
            p.finish(i, o, s)

    def middle(ins, outs, sems):
        for p, i, o, s in zip(plans, cut(ins, n_arr), cut(outs, n_arr), cut(sems, n_sem)):
            if p.middle is not None:
                p.middle(i, o, s)

    return _Plan([a for p in plans for a in p.arrays], [o for p in plans for o in p.out_shape],
                 [s for p in plans for s in p.sems], start, finish,
                 middle if any(p.middle is not None for p in plans) else None)


def _run_plan(name, plan):
    k = len(plan.arrays)

    def body(*refs):
        ins, outs, sems = refs[:k], refs[k:2 * k], refs[2 * k:]
        plan.start(ins, outs, sems)
        if plan.middle is not None:
            plan.middle(ins, outs, sems)
        plan.finish(ins, outs, sems)

    return pl.pallas_call(
        body, name=name, in_specs=[ANY] * k, out_specs=[ANY] * k, out_shape=plan.out_shape,
        scratch_shapes=plan.sems, compiler_params=pltpu.CompilerParams(has_side_effects=True),
    )(*plan.arrays)


def _call(body, *, name, grid, in_specs, out_specs, out_shape, scratch_shapes, params, args, comm=None,
          aliases=None, prefetch=()):
    aliases = aliases or {}
    n_pre = len(prefetch)

    def launch(fn, ins_specs, outs_specs, outs_shape, scratch, operands):
        spec = pltpu.PrefetchScalarGridSpec(num_scalar_prefetch=n_pre, grid=grid, in_specs=ins_specs,
                                            out_specs=outs_specs, scratch_shapes=scratch)
        return pl.pallas_call(fn, name=name, grid_spec=spec, out_shape=outs_shape, compiler_params=params,
                              input_output_aliases=aliases)(*prefetch, *[_small_in_hbm(a) for a in operands])

    if comm is None:
        return list(launch(body, in_specs, out_specs, out_shape, scratch_shapes, args)), []
    n_in, n_out, n_scr, k = len(in_specs), len(out_specs), len(scratch_shapes), len(comm.arrays)

    def wrapped(*refs):
        pre, refs = refs[:n_pre], refs[n_pre:]
        ins = refs[:n_in]
        c_in = refs[n_in:n_in + k]
        outs = refs[n_in + k:n_in + k + n_out]
        c_out = refs[n_in + k + n_out:n_in + 2 * k + n_out]
        scr = refs[n_in + 2 * k + n_out:n_in + 2 * k + n_out + n_scr]
        sems = refs[n_in + 2 * k + n_out + n_scr:]
        step, steps = pl.program_id(0), grid[0]
        for d in range(1, len(grid)):
            step, steps = step * grid[d] + pl.program_id(d), steps * grid[d]

        @pl.when(step == 0)
        def _():
            comm.start(c_in, c_out, sems)

        if comm.middle is not None:
            @pl.when(step == (3 * steps) // 4)
            def _():
                comm.middle(c_in, c_out, sems)

        body(*pre, *ins, *outs, *scr)

        @pl.when(step == steps - 1)
        def _():
            comm.finish(c_in, c_out, sems)

    res = launch(wrapped, list(in_specs) + [ANY] * k, list(out_specs) + [ANY] * k,
                 list(out_shape) + list(comm.out_shape), list(scratch_shapes) + list(comm.sems),
                 tuple(args) + tuple(comm.arrays))
    return list(res[:n_out]), list(res[n_out:])


def _row_tile(rows):
    for t in (512, 256, 128, 64, 32, 16, 8):
        if rows % t == 0:
            return t
    return rows


def _pair_sum(name, g8, recv4, core):
    _, rows, cols = recv4.shape
    tr = _row_tile(rows)
    g42 = g8.reshape(4, 2, rows, cols)

    def body(c_ref, g_ref, r_ref, o_ref):
        del c_ref
        o_ref[...] = (g_ref[...].astype(F32) + r_ref[...].astype(F32)).astype(o_ref.dtype)

    return pl.pallas_call(
        body, name=name,
        grid_spec=pltpu.PrefetchScalarGridSpec(
            num_scalar_prefetch=1, grid=(4, rows // tr),
            in_specs=[pl.BlockSpec((None, None, tr, cols), lambda j, i, c_ref: (j, c_ref[0], i, 0)),
                      pl.BlockSpec((None, tr, cols), lambda j, i, c_ref: (j, i, 0))],
            out_specs=pl.BlockSpec((None, tr, cols), lambda j, i, c_ref: (j, i, 0))),
        out_shape=SDS(recv4.shape, g8.dtype),
        compiler_params=_cparams(("arbitrary", "arbitrary"), 32),
    )(core, g42, recv4)


def _add2(name, a, b):
    rows, cols = a.shape
    tr = _row_tile(rows)

    def body(a_ref, b_ref, o_ref):
        o_ref[...] = a_ref[...] + b_ref[...]

    blk = pl.BlockSpec((tr, cols), lambda i: (i, 0))
    return pl.pallas_call(body, name=name, grid=(rows // tr,), in_specs=[blk, blk], out_specs=blk,
                          out_shape=SDS(a.shape, a.dtype),
                          compiler_params=_cparams(("arbitrary",), 32))(a, b)


def _sum_terms(name, terms):
    k, rows, cols = terms.shape
    tr = _row_tile(rows)

    def body(r_ref, o_ref):
        acc = r_ref[0]
        for q in range(1, k):
            acc = acc + r_ref[q]
        o_ref[...] = acc

    return pl.pallas_call(body, name=name, grid=(rows // tr,),
                          in_specs=[pl.BlockSpec((k, tr, cols), lambda i: (0, i, 0))],
                          out_specs=pl.BlockSpec((tr, cols), lambda i: (i, 0)),
                          out_shape=SDS((rows, cols), terms.dtype),
                          compiler_params=_cparams(("arbitrary",), 32))(terms)


def _adam_update(g, w, m, v):
    c1 = 1.0 / (1.0 - ADAM_B1 ** ADAM_STEP)
    c2 = 1.0 / (1.0 - ADAM_B2 ** ADAM_STEP)
    mn = ADAM_B1 * m + (1.0 - ADAM_B1) * g
    vn = ADAM_B2 * v + (1.0 - ADAM_B2) * (g * g)
    delta = (-ADAM_LR) * ((mn * c1) / (jnp.sqrt(vn * c2) + ADAM_EPS) + ADAM_WD * w)
    return delta, mn, vn


def _adamw_many(name, gs, ws, ms, vs):
    n = len(gs)

    def body(*refs):
        for p in range(n):
            g, w, m, v = (refs[q * n + p][...] for q in range(4))
            d, mn, vn = _adam_update(g, w, m, v)
            refs[4 * n + p][...] = d
            refs[5 * n + p][...] = mn
            refs[6 * n + p][...] = vn

    full = [pl.BlockSpec(memory_space=pltpu.VMEM)] * n
    shapes = [SDS(w.shape, F32) for w in ws]
    res = pl.pallas_call(body, name=name, in_specs=full * 4, out_specs=full * 3, out_shape=shapes * 3,
                         compiler_params=pltpu.CompilerParams(vmem_limit_bytes=32 * MiB))(*gs, *ws, *ms, *vs)
    return [(res[p], res[n + p], res[2 * n + p]) for p in range(n)]


def _adamw(name, terms, w, m, v):
    k, rows, cols = terms.shape
    tr = _row_tile(rows)

    def body(t_ref, w_ref, m_ref, v_ref, g_ref, d_ref, mo_ref, vo_ref):
        g = t_ref[0].astype(F32)
        for q in range(1, k):
            g = g + t_ref[q].astype(F32)
        g_ref[...] = g
        d_ref[...], mo_ref[...], vo_ref[...] = _adam_update(g, w_ref[...], m_ref[...], v_ref[...])

    blk = pl.BlockSpec((tr, cols), lambda i: (i, 0))
    return pl.pallas_call(body, name=name, grid=(rows // tr,),
                          in_specs=[pl.BlockSpec((k, tr, cols), lambda i: (0, i, 0)), blk, blk, blk],
                          out_specs=[blk] * 4, out_shape=[SDS((rows, cols), F32)] * 4,
                          compiler_params=_cparams(("arbitrary",), 40))(terms, w, m, v)


def kernel(x, norm_mix_g, w_in, conv_w, conv_b, w_rgate, b_rgate, w_igate, b_igate, lru_lambda, w_out_a, sgu_ln_g, sgu_ln_b, sgu_w_s, sgu_b_s, w_out_b, w_out, norm_mlp_g, w_up, w_down, norm_final_g, loss_target, m_norm_mix_g, m_w_in, m_conv_w, m_conv_b, m_w_rgate, m_b_rgate, m_w_igate, m_b_igate, m_lru_lambda, m_w_out_a, m_sgu_ln_g, m_sgu_ln_b, m_sgu_w_s, m_sgu_b_s, m_w_out_b, m_w_out, m_norm_mlp_g, m_w_up, m_w_down, m_norm_final_g, v_norm_mix_g, v_w_in, v_conv_w, v_conv_b, v_w_rgate, v_b_rgate, v_w_igate, v_b_igate, v_lru_lambda, v_w_out_a, v_sgu_ln_g, v_sgu_ln_b, v_sgu_w_s, v_sgu_b_s, v_w_out_b, v_w_out, v_norm_mlp_g, v_w_up, v_w_down, v_norm_final_g):
    cx, cy, cc = _place()
    me = 4 * cx + 2 * cy + cc
    core = jnp.reshape(cc, (1,)).astype(jnp.int32)
    xs = x[0]
    tgt = loss_target[0]
    s = xs.shape[0]

    gate_shard = jnp.stack([w_rgate[0], w_igate[0]]).astype(BF16).reshape(2 * HEADS * 32, HEAD_DIM)
    vec_shard = jnp.concatenate([conv_w[0], b_rgate[0], b_igate[0]], axis=1)
    vec_shard = jnp.pad(vec_shard, ((0, 4), (0, 256 - vec_shard.shape[1])))
    shards = [w_in[0].astype(BF16), w_out_a[0].astype(BF16), w_out_b[0].astype(BF16), w_out[0].astype(BF16),
              w_up[0].astype(BF16), w_down[0].astype(BF16), gate_shard, vec_shard]
    (z, n1_t, w_in_g), (gate_g, vec_g) = _in_proj(xs, norm_mix_g, shards[0], _slot_order(cx, cy, cc),
                                                comm=_gather_plan(shards[6:8]))
    gates = gate_g.reshape(N_SLOT, 2, HEADS, 32, HEAD_DIM).transpose(1, 2, 0, 3, 4).reshape(2, HEADS, HEAD_DIM, HEAD_DIM)
    w_r_f, w_i_f = gates[0], gates[1]
    conv_w_f = vec_g[:, 0:4, 0:128].transpose(1, 0, 2).reshape(CONV_K, D)
    b_r_f = vec_g[:, 0:4, 128:160].transpose(1, 0, 2).reshape(1, D)
    b_i_f = vec_g[:, 0:4, 160:192].transpose(1, 0, 2).reshape(1, D)
    b_s_t = jnp.transpose(sgu_b_s[0])

    (ya, hs), (w_oa_g, w_ob_g, w_out_g) = _branch_a_fwd(
        z, conv_w_f, conv_b, w_r_f, b_r_f, w_i_f, b_i_f, lru_lambda, comm=_gather_plan(shards[1:4]))
    w_oa_f = w_oa_g.reshape(D, D)
    w_ob_f = w_ob_g.reshape(D, D)
    w_out_f = w_out_g.reshape(D, D)
    (yb,), (w_up_g,) = _branch_b_fwd(z, sgu_ln_g, sgu_ln_b, sgu_w_s[0], b_s_t, comm=_gather_plan(shards[4:5]))
    (pa, pb, merged, h1), (w_down_g,) = _merge_out(ya, yb, z, xs, w_oa_f, w_ob_f, w_out_f,
                                                   comm=_gather_plan(shards[5:6]))
    w_down_f = w_down_g.reshape(N_SLOT * FF_COLS, D)
    gf2 = norm_final_g.reshape(1, D)
    r_act, act_t, n2_t, dh2, loss_acc, d_gfin = _mlp_fwd(h1, norm_mlp_g, w_up_g, w_down_f, gf2, tgt)

    def pair(names, grads, recv):
        return [_pair_sum("pair_sum_" + nm, g, r, core) for nm, g, r in zip(names, grads, recv)]

    g_down = _wgrad_t("wgrad_down", act_t, dh2, N_SLOT // 2, True, False, 2048)
    g_down = g_down.reshape(N_SLOT, FF_COLS, D)
    (df, dh1, d_gmlp), (r_down,) = _mlp_bwd(dh2, r_act, w_down_f, w_up_g, h1, norm_mlp_g,
                                            comm=_sibling_plan([g_down]))
    (p_down,) = pair(["down"], [g_down], [r_down])
    g_up = _wgrad_t("wgrad_up", n2_t, df, N_SLOT, False, True, 4096)
    g_out = _wgrad("wgrad_out", merged, dh1, 1, False, False).reshape(N_SLOT, D // N_SLOT, D)
    (dz, dpa, dpb, dya, dyb), (got_down, r_up, r_out) = _merge_bwd(
        dh1, z, pa, pb, w_out_f, w_oa_f, w_ob_f, comm=_join(_chips_plan([p_down]), _sibling_plan([g_up, g_out])))
    p_up, p_out = pair(["up", "out"], [g_up, g_out], [r_up, r_out])
    g_oa = _wgrad("wgrad_out_a", ya, dpa, 1, False, False).reshape(N_SLOT, D // N_SLOT, D)
    g_ob = _wgrad("wgrad_out_b", yb, dpb, 1, False, False).reshape(N_SLOT, D // N_SLOT, D)
    (dz, d_ws, d_bs, d_ln), (got_up, r_oa, r_ob) = _branch_b_bwd(
        dz, dyb, z, sgu_ln_g, sgu_ln_b, sgu_w_s[0], b_s_t,
        comm=_join(_chips_plan([p_up]), _sibling_plan([g_oa, g_ob])))
    p_oa, p_ob = pair(["out_a", "out_b"], [g_oa, g_ob], [r_oa, r_ob])
    (dz, d_vec, d_wr, d_wi), (got_out, got_oa, got_ob) = _branch_a_bwd(
        dz, dya, z, hs, conv_w_f, conv_b, w_r_f, b_r_f, w_i_f, b_i_f, lru_lambda,
        comm=_chips_plan([p_out, p_oa, p_ob]))
    g_in = _wgrad_t("wgrad_in", n1_t, dz, N_SLOT, False, True, 4096)
    g_gate = jnp.stack([d_wr, d_wi]).reshape(2, HEADS, N_SLOT, 32, HEAD_DIM).transpose(2, 0, 1, 3, 4)
    g_gate = g_gate.reshape(N_SLOT, 2 * HEADS * 32, HEAD_DIM).astype(BF16)

    d_bs_row = jnp.pad(d_bs[:, :, 0].reshape(1, GROUPS * CHUNK), ((0, 0), (0, D - GROUPS * CHUNK)))
    vecs = jnp.concatenate([d_vec, jnp.concatenate([d_ln[0:2], d_gmlp, d_gfin, d_bs_row, jnp.zeros((3, D), F32)])])
    d_ws2 = d_ws.reshape(GROUPS * CHUNK, CHUNK)
    r_in, r_gate, r_vecs, r_ws = _run_plan("rs_sibling_in", _sibling_plan([g_in, g_gate], [vecs, d_ws2]))
    p_in, p_gate = pair(["in", "gate"], [g_in, g_gate], [r_in, r_gate])
    vecs_chip = _add2("pair_sum_vecs", vecs, r_vecs)
    ws_chip = _add2("pair_sum_ws", d_ws2, r_ws)
    (dx, d_gmix), (got_in, got_gate, got_vecs, got_ws) = _in_bwd(
        dz, w_in_g, xs, dh1, norm_mix_g, comm=_chips_plan([p_in, p_gate], [vecs_chip, ws_chip]))
    vecs_sum = _sum_terms("sum_vecs", got_vecs)
    last = jnp.concatenate([d_gmix, jnp.pad(loss_acc[0:1], ((0, 0), (0, D - 128))), jnp.zeros((6, D), F32)])
    (last_all,) = _run_plan("exchange_last", _exchange_plan(last))
    last_sum = _sum_terms("sum_last", last_all)
    loss = last_sum[1, 0]
    got = [got_in, got_oa, got_ob, got_out, got_up, got_down, got_gate]

    def step(nm, terms, w, m, v, rows, cols):
        g, d, mn, vn = _adamw("adamw_" + nm, terms.reshape(4, rows, cols), w.reshape(rows, cols),
                              m.reshape(rows, cols), v.reshape(rows, cols))
        return [a.reshape(w.shape) for a in (g, d, mn, vn)]

    o_in = step("in", got[0], w_in, m_w_in, v_w_in, D, W_IN_COLS)
    o_oa = step("out_a", got[1], w_out_a, m_w_out_a, v_w_out_a, D // N_SLOT, D)
    o_ob = step("out_b", got[2], w_out_b, m_w_out_b, v_w_out_b, D // N_SLOT, D)
    o_out = step("out", got[3], w_out, m_w_out, v_w_out, D // N_SLOT, D)
    o_up = step("up", got[4], w_up, m_w_up, v_w_up, D, FF_COLS)
    o_down = step("down", got[5], w_down, m_w_down, v_w_down, FF_COLS, D)
    gate_w = jnp.stack([w_rgate[0], w_igate[0]]).reshape(2 * HEADS * 32, HEAD_DIM)
    gate_m = jnp.stack([m_w_rgate[0], m_w_igate[0]]).reshape(2 * HEADS * 32, HEAD_DIM)
    gate_v = jnp.stack([v_w_rgate[0], v_w_igate[0]]).reshape(2 * HEADS * 32, HEAD_DIM)
    o_gate = _adamw("adamw_gate", got[6], gate_w, gate_m, gate_v)
    o_gate = [a.reshape(2, 1, HEADS, 32, HEAD_DIM) for a in o_gate]
    o_wr = [a[0] for a in o_gate]
    o_wi = [a[1] for a in o_gate]

    def own(full, width):
        return lax.dynamic_slice_in_dim(full, me * width, width, axis=1)

    small_g = {
        "norm_mix_g": last_sum[0:1], "conv_w": own(vecs_sum[0:4], 128), "conv_b": vecs_sum[4:5],
        "b_rgate": own(vecs_sum[5:6].reshape(HEADS, HEAD_DIM), 32),
        "b_igate": own(vecs_sum[6:7].reshape(HEADS, HEAD_DIM), 32),
        "lru_lambda": vecs_sum[7:8], "sgu_ln_g": vecs_sum[8:9], "sgu_ln_b": vecs_sum[9:10],
        "norm_mlp_g": vecs_sum[10:11], "norm_final_g": vecs_sum[11:12],
        "sgu_b_s": vecs_sum[12, 0:GROUPS * CHUNK].reshape(GROUPS, CHUNK),
    }
    small_w = {"norm_mix_g": (norm_mix_g, m_norm_mix_g, v_norm_mix_g), "conv_w": (conv_w, m_conv_w, v_conv_w),
               "conv_b": (conv_b, m_conv_b, v_conv_b), "b_rgate": (b_rgate, m_b_rgate, v_b_rgate),
               "b_igate": (b_igate, m_b_igate, v_b_igate), "lru_lambda": (lru_lambda, m_lru_lambda, v_lru_lambda),
               "sgu_ln_g": (sgu_ln_g, m_sgu_ln_g, v_sgu_ln_g), "sgu_ln_b": (sgu_ln_b, m_sgu_ln_b, v_sgu_ln_b),
               "norm_mlp_g": (norm_mlp_g, m_norm_mlp_g, v_norm_mlp_g),
               "norm_final_g": (norm_final_g, m_norm_final_g, v_norm_final_g),
               "sgu_b_s": (sgu_b_s, m_sgu_b_s, v_sgu_b_s), "sgu_w_s": (sgu_w_s, m_sgu_w_s, v_sgu_w_s)}
    order = list(small_g)
    as2d = lambda k, a: a.reshape(small_g[k].shape)
    upd = _adamw_many("adamw_small", [small_g[k] for k in order], *[[as2d(k, small_w[k][q]) for k in order]
                                                                     for q in range(3)])
    o_small = {k: [a.reshape(small_w[k][0].shape) for a in (small_g[k],) + u] for k, u in zip(order, upd)}
    ws3 = [a[0].reshape(GROUPS * CHUNK, CHUNK) for a in small_w.pop("sgu_w_s")]
    o_small["sgu_w_s"] = [a.reshape(sgu_w_s.shape) for a in _adamw("adamw_ws", got_ws, *ws3)]

    per_weight = {"norm_mix_g": o_small["norm_mix_g"], "w_in": o_in, "conv_w": o_small["conv_w"],
                  "conv_b": o_small["conv_b"], "w_rgate": o_wr, "b_rgate": o_small["b_rgate"], "w_igate": o_wi,
                  "b_igate": o_small["b_igate"], "lru_lambda": o_small["lru_lambda"], "w_out_a": o_oa,
                  "sgu_ln_g": o_small["sgu_ln_g"], "sgu_ln_b": o_small["sgu_ln_b"], "sgu_w_s": o_small["sgu_w_s"],
                  "sgu_b_s": o_small["sgu_b_s"], "w_out_b": o_ob, "w_out": o_out, "norm_mlp_g": o_small["norm_mlp_g"],
                  "w_up": o_up, "w_down": o_down, "norm_final_g": o_small["norm_final_g"]}
    names_w = list(per_weight)
    return (loss, dx[None], *[per_weight[k][0] for k in names_w], *[per_weight[k][1] for k in names_w],
            *[per_weight[k][2] for k in names_w], *[per_weight[k][3] for k in names_w])
```

```python
import jax
import jax.numpy as jnp
from jax import lax
from jax.experimental import pallas as pl
from jax.experimental.pallas import tpu as pltpu

F32 = jnp.float32
BF16 = jnp.bfloat16
SDS = jax.ShapeDtypeStruct
MESH = pl.DeviceIdType.MESH
ANY = pl.BlockSpec(memory_space=pltpu.HBM)

D = 1024
N_SLOT = 8
W_IN_COLS = 768
FF_COLS = 512
HEADS, HEAD_DIM = 4, 256
GROUPS, GROUP_DIM = 4, 256
CHUNK = 128
CONV_K = 4
NORM_EPS = 1e-6
LN_EPS = 1e-5
LRU_C = 8.0
ADAM_LR, ADAM_B1, ADAM_B2, ADAM_EPS, ADAM_WD, ADAM_STEP = 0.001, 0.9, 0.999, 1e-08, 0.01, 10

TM_ROWS = 1024
TM_MERGE = 512
T_BRANCH_A = 512
T_BRANCH_B = 256
MiB = 1024 * 1024
SMALL_OPERAND = 16 * 1024

_GELU_C = 0.7978845608028654
_GELU_A = 0.044715


def _small_in_hbm(a):
    return pltpu.with_memory_space_constraint(a, pltpu.HBM) if a.size <= SMALL_OPERAND else a


def _cparams(sem, vmem_mib):
    return pltpu.CompilerParams(dimension_semantics=sem, vmem_limit_bytes=vmem_mib * MiB)


def _gelu(x):
    t = jnp.tanh(_GELU_C * (x + _GELU_A * x * x * x))
    return 0.5 * x * (1.0 + t)


def _gelu_and_grad(x):
    x2 = x * x
    t = jnp.tanh(_GELU_C * x * (1.0 + _GELU_A * x2))
    g = 0.5 * x * (1.0 + t)
    dg = 0.5 * (1.0 + t) + 0.5 * x * (1.0 - t * t) * _GELU_C * (1.0 + 3.0 * _GELU_A * x2)
    return g, dg


def _softplus(x):
    return jnp.maximum(x, 0.0) + jnp.log1p(jnp.exp(-jnp.abs(x)))


def _dot(a, b):
    return jnp.dot(a, b, preferred_element_type=F32)


def _dot_nt(a, b):
    return lax.dot_general(a, b, (((1,), (1,)), ((), ())), preferred_element_type=F32)


def _dot_tn(a, b):
    return lax.dot_general(a, b, (((0,), (0,)), ((), ())), preferred_element_type=F32)


def _rows_shifted(prev8, cur, k):
    ext = jnp.concatenate([prev8, cur], axis=0)
    return pltpu.roll(ext, k, 0)[8:]


def _rows_advanced(cur, next8, k):
    t = cur.shape[0]
    ext = jnp.concatenate([cur, next8], axis=0)
    return pltpu.roll(ext, t + 8 - k, 0)[:t]


def _first_second(x, y, c):
    ny, nx, far = _other_chips(x, y)
    pick = lambda a, b: a * (1 - c) + b * c
    first = tuple(pick(a, b) for a, b in zip(ny, nx))
    second = tuple(pick(b, a) for a, b in zip(ny, nx))
    return first, second, far


def _slot_order(x, y, c):
    chip = 2 * x + y
    first, second, far = _first_second(x, y, c)
    order = [2 * chip + c, 2 * chip + 1 - c, 2 * first[2] + c, 2 * second[2] + 1 - c, 2 * second[2] + c,
             2 * first[2] + 1 - c, 2 * far[2] + c, 2 * far[2] + 1 - c]
    return jnp.stack(order).astype(jnp.int32)


def _in_proj(x, g_mix, w_in_own, order, comm=None):
    s = x.shape[0]
    tm = min(TM_ROWS, s)
    ni = s // tm

    def body(order_ref, x_ref, g_ref, own_ref, z_ref, nt_ref, wg_ref, n_s, w_s, send_sems, recv_sems, local_sems):
        j, i = pl.program_id(0), pl.program_id(1)
        px, py, c = _place()
        chip = 2 * px + py
        me = 2 * chip + c
        sib = (px, py, 1 - c)
        chips = _other_chips(px, py)

        def rc(k, src, blk, to):
            return pltpu.make_async_remote_copy(src_ref=src, dst_ref=w_s.at[blk], send_sem=send_sems.at[k],
                                                recv_sem=recv_sems.at[k], device_id=to, device_id_type=MESH)

        del chips
        first, second, far = _first_second(px, py, c)
        blocks = [2 * first[2] + c, 2 * second[2] + c, 2 * far[2] + c]
        own_in = pltpu.make_async_copy(own_ref, w_s.at[me], local_sems.at[0])
        to_first = rc(1, own_ref, me, (first[0], first[1], c))
        to_second = rc(2, own_ref, me, (second[0], second[1], c))
        relay = rc(3, w_s.at[blocks[0]], blocks[0], (second[0], second[1], c))
        sends = [rc(0, own_ref, me, sib), to_first, to_second, relay]
        passed = [rc(4 + q, w_s.at[blk], blk, sib) for q, blk in enumerate(blocks)]
        keep = pltpu.make_async_copy(w_s, wg_ref, local_sems.at[1])

        @pl.when((i == 0) & (j == 0))
        def _():
            own_in.start()
            sends[0].start()
            to_first.start()
            own_in.wait()

        @pl.when((i == 0) & (j == 1))
        def _():
            rc(0, own_ref, 2 * chip + 1 - c, sib).wait_recv()

        for q, blk in enumerate(blocks):
            @pl.when((i == 0) & (j == 2 + 2 * q))
            def _():
                rc(1 + q, own_ref, blk, sib).wait_recv()
                passed[q].start()
                if q == 0:
                    to_second.start()
                    relay.start()

            @pl.when((i == 0) & (j == 3 + 2 * q))
            def _():
                rc(4 + q, own_ref, order_ref[j], sib).wait_recv()

        rows = pl.ds(pl.multiple_of(i * tm, tm), tm)

        @pl.when(j == 0)
        def _():
            xv = x_ref[...]
            rstd = lax.rsqrt(jnp.mean(xv * xv, axis=-1, keepdims=True) + NORM_EPS)
            nb = (xv * rstd * g_ref[...]).astype(BF16)
            n_s[rows, :] = nb
            nt_ref[...] = nb.T

        z_ref[...] = _dot(n_s[rows, :], w_s[order_ref[j]]).astype(BF16)

        @pl.when((i == 0) & (j == N_SLOT - 1))
        def _():
            keep.start()

        @pl.when((i == ni - 1) & (j == N_SLOT - 1))
        def _():
            for cp in sends + passed:
                cp.wait_send()
            keep.wait()

    first_pass = lambda j, i, o: (jnp.where(j == 0, i, ni - 1), 0)
    (z, n1, w_in_g), extra = _call(
        body, name="in_proj", grid=(N_SLOT, ni), prefetch=(order,),
        in_specs=[pl.BlockSpec((tm, D), first_pass),
                  pl.BlockSpec((1, D), lambda j, i, o: (0, 0)), ANY],
        out_specs=[pl.BlockSpec((tm, W_IN_COLS), lambda j, i, o: (i, o[j])),
                   pl.BlockSpec((D, tm), lambda j, i, o: (0, jnp.where(j == 0, i, ni - 1))), ANY],
        out_shape=[SDS((s, N_SLOT * W_IN_COLS), BF16), SDS((D, s), BF16), SDS((N_SLOT, D, W_IN_COLS), BF16)],
        scratch_shapes=[pltpu.VMEM((s, D), BF16), pltpu.VMEM((N_SLOT, D, W_IN_COLS), BF16),
                        pltpu.SemaphoreType.DMA((7,)), pltpu.SemaphoreType.DMA((7,)), pltpu.SemaphoreType.DMA((2,))],
        params=_cparams(("arbitrary", "arbitrary"), 56), args=(x, g_mix, w_in_own), comm=comm)
    return (z, n1, w_in_g), extra


def _lru_gates(xc, xcb, wr_ref, br, wi_ref, bi, sp_lam, a_s, b_s, r_s=None, i_s=None, m_s=None):
    for h in range(HEADS):
        sl = slice(h * HEAD_DIM, (h + 1) * HEAD_DIM)
        r = jax.nn.sigmoid(_dot(xcb[:, sl], wr_ref[h]) + br[:, sl])
        ig = jax.nn.sigmoid(_dot(xcb[:, sl], wi_ref[h]) + bi[:, sl])
        log_a = (-LRU_C) * r * sp_lam[:, sl]
        a = jnp.exp(log_a)
        mult = jnp.sqrt(-jnp.tanh(log_a) * (a * a + 1.0))
        a_s[:, sl] = a
        b_s[:, sl] = xc[:, sl] * ig * mult
        if r_s is not None:
            r_s[:, sl] = r
            i_s[:, sl] = ig
            m_s[:, sl] = mult


def _conv_fwd(xa, prev8, cw, cb):
    xc = cb + cw[0:1, :] * xa
    for k in range(1, CONV_K):
        xc = xc + cw[k:k + 1, :] * _rows_shifted(prev8, xa, k)
    return xc


def _branch_a_fwd(z, conv_w, conv_b, w_r, b_r, w_i, b_i, lam, comm=None):
    s = z.shape[0]
    ta = min(T_BRANCH_A, s)
    per16 = ta // 16

    def body(xa_ref, xp_ref, ga_ref, cw_ref, cb_ref, wr_ref, br_ref, wi_ref, bi_ref, lam_ref,
             ya_ref, hs_ref, a_s, b_s, h_s, carry_s):
        i = pl.program_id(0)

        @pl.when(i == 0)
        def _():
            carry_s[...] = jnp.zeros_like(carry_s)

        xa = xa_ref[...].astype(F32)
        prev8 = jnp.where(i > 0, xp_ref[...].astype(F32)[8:16], 0.0)
        xc = _conv_fwd(xa, prev8, cw_ref[...], cb_ref[...])
        sp_lam = _softplus(-lam_ref[...])
        _lru_gates(xc, xc.astype(BF16), wr_ref, br_ref[...], wi_ref, bi_ref[...], sp_lam, a_s, b_s)

        row = lax.broadcasted_iota(jnp.int32, (8, D), 0)

        def group(g, carry):
            off = pl.multiple_of(g * 8, 8)
            a8 = a_s[pl.ds(off, 8), :]
            b8 = b_s[pl.ds(off, 8), :]
            for d in (1, 2, 4):
                a_sh = jnp.where(row >= d, pltpu.roll(a8, d, 0), 1.0)
                b_sh = jnp.where(row >= d, pltpu.roll(b8, d, 0), 0.0)
                b8 = a8 * b_sh + b8
                a8 = a8 * a_sh
            h8 = b8 + a8 * carry
            h_s[pl.ds(off, 8), :] = h8
            return jnp.broadcast_to(h8[7:8, :], (8, D))

        carry_s[...] = lax.fori_loop(0, ta // 8, group, carry_s[...])
        hs = h_s[...]
        hs_ref[...] = hs.astype(BF16)
        ya_ref[...] = (hs * _gelu(ga_ref[...].astype(F32))).astype(BF16)

    vec = pl.BlockSpec((1, D), lambda i: (0, 0))
    gate = pl.BlockSpec((HEADS, HEAD_DIM, HEAD_DIM), lambda i: (0, 0, 0))
    return _call(
        body, name="branch_a_fwd", grid=(s // ta,),
        in_specs=[pl.BlockSpec((ta, D), lambda i: (i, 0)),
                  pl.BlockSpec((16, D), lambda i: (jnp.maximum(i * per16 - 1, 0), 0)),
                  pl.BlockSpec((ta, D), lambda i: (i, 1)),
                  pl.BlockSpec((CONV_K, D), lambda i: (0, 0)), vec, gate, vec, gate, vec, vec],
        out_specs=[pl.BlockSpec((ta, D), lambda i: (i, 0)), pl.BlockSpec((ta, D), lambda i: (i, 0))],
        out_shape=[SDS((s, D), BF16), SDS((s, D), BF16)],
        scratch_shapes=[pltpu.VMEM((ta, D), F32), pltpu.VMEM((ta, D), F32), pltpu.VMEM((ta, D), F32),
                        pltpu.VMEM((8, D), F32)],
        params=_cparams(("arbitrary",), 40), args=(z, z, z, conv_w, conv_b, w_r, b_r, w_i, b_i, lam), comm=comm)


def _sgu_common(ub, vb, lg, lb, with_grad):
    if with_grad:
        u, du = _gelu_and_grad(ub)
        v, dv = _gelu_and_grad(vb)
    else:
        u, v, du, dv = _gelu(ub), _gelu(vb), None, None
    mu = jnp.mean(v, axis=-1, keepdims=True)
    vc = v - mu
    rstd = lax.rsqrt(jnp.mean(vc * vc, axis=-1, keepdims=True) + LN_EPS)
    vhat = vc * rstd
    vln = vhat * lg + lb
    return u, du, dv, rstd, vhat, vln


def _masked_ws(ws_ref):
    t = lax.broadcasted_iota(jnp.int32, (CHUNK, CHUNK), 0)
    c = lax.broadcasted_iota(jnp.int32, (CHUNK, CHUNK), 1)
    keep = c <= t
    return [jnp.where(keep, ws_ref[g], 0.0).astype(BF16) for g in range(GROUPS)]


def _branch_b_fwd(z, ln_g, ln_b, w_s, b_s_t, comm=None):
    s = z.shape[0]
    tb = min(T_BRANCH_B, s)

    def body(ub_ref, vb_ref, lg_ref, lb_ref, ws_ref, bs_ref, yb_ref):
        u, _, _, _, _, vln = _sgu_common(ub_ref[...].astype(F32), vb_ref[...].astype(F32),
                                         lg_ref[...], lb_ref[...], False)
        vlnb = vln.astype(BF16)
        wm = _masked_ws(ws_ref)
        bs = bs_ref[...]
        for c in range(tb // CHUNK):
            rs = slice(c * CHUNK, (c + 1) * CHUNK)
            for g in range(GROUPS):
                cs = slice(g * GROUP_DIM, (g + 1) * GROUP_DIM)
                sp = _dot(wm[g], vlnb[rs, cs]) + bs[:, g:g + 1]
                yb_ref[rs, cs] = (u[rs, cs] * sp).astype(BF16)

    vec = pl.BlockSpec((1, D), lambda i: (0, 0))
    return _call(
        body, name="branch_b_fwd", grid=(s // tb,),
        in_specs=[pl.BlockSpec((tb, D), lambda i: (i, 2)), pl.BlockSpec((tb, D), lambda i: (i, 3)), vec, vec,
                  pl.BlockSpec((GROUPS, CHUNK, CHUNK), lambda i: (0, 0, 0)),
                  pl.BlockSpec((CHUNK, GROUPS), lambda i: (0, 0))],
        out_specs=[pl.BlockSpec((tb, D), lambda i: (i, 0))],
        out_shape=[SDS((s, D), BF16)], scratch_shapes=[],
        params=_cparams(("arbitrary",), 40), args=(z, z, ln_g, ln_b, w_s, b_s_t), comm=comm)


def _merge_out(ya, yb, z, x, w_oa, w_ob, w_out, comm=None):
    s = x.shape[0]
    tm = min(TM_MERGE, s)

    def body(ya_ref, yb_ref, ma_ref, mb_ref, x_ref, woa_ref, wob_ref, wo_ref, pa_ref, pb_ref, mg_ref, h1_ref):
        pa = _dot(ya_ref[...], woa_ref[...])
        pb = _dot(yb_ref[...], wob_ref[...])
        merged = (jax.nn.sigmoid(ma_ref[...].astype(F32)) * pa
                  + jax.nn.sigmoid(mb_ref[...].astype(F32)) * pb).astype(BF16)
        pa_ref[...] = pa.astype(BF16)
        pb_ref[...] = pb.astype(BF16)
        mg_ref[...] = merged
        h1_ref[...] = x_ref[...] + _dot(merged, wo_ref[...])

    row = pl.BlockSpec((tm, D), lambda i: (i, 0))
    wsp = pl.BlockSpec((D, D), lambda i: (0, 0))
    return _call(
        body, name="merge_out", grid=(s // tm,),
        in_specs=[row, row, pl.BlockSpec((tm, D), lambda i: (i, 4)), pl.BlockSpec((tm, D), lambda i: (i, 5)),
                  row, wsp, wsp, wsp],
        out_specs=[row, row, row, row],
        out_shape=[SDS((s, D), BF16), SDS((s, D), BF16), SDS((s, D), BF16), SDS((s, D), F32)], scratch_shapes=[],
        params=_cparams(("arbitrary",), 48), args=(ya, yb, z, z, x, w_oa, w_ob, w_out), comm=comm)


def _mlp_fwd(h1, g_mlp, w_up_g, w_down, g_fin, tgt):
    s = h1.shape[0]
    tm = min(TM_ROWS, s)
    nj = N_SLOT

    def body(h1_ref, gm_ref, wu_ref, wd_ref, gf_ref, t_ref, r_ref, at_ref, n2t_ref, dh2_ref, loss_ref, dgf_ref,
             n2_s, acc_s):
        i, j = pl.program_id(0), pl.program_id(1)

        @pl.when(j == 0)
        def _():
            hv = h1_ref[...]
            rstd = lax.rsqrt(jnp.mean(hv * hv, axis=-1, keepdims=True) + NORM_EPS)
            nb = (hv * rstd * gm_ref[...]).astype(BF16)
            n2_s[...] = nb
            n2t_ref[...] = nb.T
            acc_s[...] = jnp.zeros_like(acc_s)

        @pl.when((i == 0) & (j == 0))
        def _():
            loss_ref[...] = jnp.zeros_like(loss_ref)
            dgf_ref[...] = jnp.zeros_like(dgf_ref)

        r = jnp.maximum(_dot(n2_s[...], wu_ref[...]), 0.0)
        r_ref[...] = r.astype(BF16)
        act = (r * r).astype(BF16)
        at_ref[...] = act.T
        acc_s[...] += _dot(act, wd_ref[...])

        @pl.when(j == nj - 1)
        def _():
            h2 = h1_ref[...] + acc_s[...]
            rstd = lax.rsqrt(jnp.mean(h2 * h2, axis=-1, keepdims=True) + NORM_EPS)
            hh = h2 * rstd
            gf = gf_ref[...]
            e = hh * gf - t_ref[...]
            loss_ref[...] += jnp.sum(e * e) * (0.5 / D)
            dy = e * (1.0 / D)
            dgf_ref[...] += jnp.sum(dy * hh, axis=0, keepdims=True)
            dhh = dy * gf
            dh2_ref[...] = rstd * (dhh - hh * jnp.mean(dhh * hh, axis=-1, keepdims=True))

    row = pl.BlockSpec((tm, D), lambda i, j: (i, 0))
    vec = pl.BlockSpec((1, D), lambda i, j: (0, 0))
    return pl.pallas_call(
        body, name="mlp_fwd", grid=(s // tm, nj),
        in_specs=[row, vec, pl.BlockSpec((None, D, FF_COLS), lambda i, j: (j, 0, 0)),
                  pl.BlockSpec((FF_COLS, D), lambda i, j: (j, 0)), vec, row],
        out_specs=[pl.BlockSpec((tm, FF_COLS), lambda i, j: (i, j)), pl.BlockSpec((FF_COLS, tm), lambda i, j: (j, i)),
                   pl.BlockSpec((D, tm), lambda i, j: (0, i)), row, pl.BlockSpec((8, 128), lambda i, j: (0, 0)), vec],
        out_shape=[SDS((s, nj * FF_COLS), BF16), SDS((nj * FF_COLS, s), BF16), SDS((D, s), BF16), SDS((s, D), F32),
                   SDS((8, 128), F32), SDS((1, D), F32)],
        scratch_shapes=[pltpu.VMEM((tm, D), BF16), pltpu.VMEM((tm, D), F32)],
        compiler_params=_cparams(("arbitrary", "arbitrary"), 52),
    )(h1, _small_in_hbm(g_mlp), w_up_g, w_down, _small_in_hbm(g_fin), tgt)


def _mlp_bwd(dh2, r, w_down, w_up_g, h1, g_mlp, comm=None):
    s = h1.shape[0]
    tm = min(TM_ROWS, s)
    nj = N_SLOT

    def body(dh2_ref, r_ref, wd_ref, wu_ref, h1_ref, gm_ref, df_ref, dh1_ref, dgm_ref, dh2b_s, acc_s):
        i, j = pl.program_id(0), pl.program_id(1)

        @pl.when(j == 0)
        def _():
            dh2b_s[...] = dh2_ref[...].astype(BF16)
            acc_s[...] = jnp.zeros_like(acc_s)

        @pl.when((i == 0) & (j == 0))
        def _():
            dgm_ref[...] = jnp.zeros_like(dgm_ref)

        d_act = _dot_nt(dh2b_s[...], wd_ref[...])
        df = (d_act * (2.0 * r_ref[...].astype(F32))).astype(BF16)
        df_ref[...] = df
        acc_s[...] += _dot_nt(df, wu_ref[...])

        @pl.when(j == nj - 1)
        def _():
            hv = h1_ref[...]
            rstd = lax.rsqrt(jnp.mean(hv * hv, axis=-1, keepdims=True) + NORM_EPS)
            hh = hv * rstd
            dn2 = acc_s[...]
            dgm_ref[...] += jnp.sum(dn2 * hh, axis=0, keepdims=True)
            dhat = dn2 * gm_ref[...]
            dh1_ref[...] = dh2_ref[...] + rstd * (dhat - hh * jnp.mean(dhat * hh, axis=-1, keepdims=True))

    row = pl.BlockSpec((tm, D), lambda i, j: (i, 0))
    vec = pl.BlockSpec((1, D), lambda i, j: (0, 0))
    ffb = pl.BlockSpec((tm, FF_COLS), lambda i, j: (i, j))
    return _call(
        body, name="mlp_bwd", grid=(s // tm, nj),
        in_specs=[row, ffb, pl.BlockSpec((FF_COLS, D), lambda i, j: (j, 0)),
                  pl.BlockSpec((None, D, FF_COLS), lambda i, j: (j, 0, 0)), row, vec],
        out_specs=[ffb, row, vec],
        out_shape=[SDS((s, nj * FF_COLS), BF16), SDS((s, D), F32), SDS((1, D), F32)],
        scratch_shapes=[pltpu.VMEM((tm, D), BF16), pltpu.VMEM((tm, D), F32)],
        params=_cparams(("arbitrary", "arbitrary"), 52), args=(dh2, r, w_down, w_up_g, h1, g_mlp), comm=comm)


def _merge_bwd(dh1, z, pa, pb, w_out, w_oa, w_ob, comm=None):
    s = dh1.shape[0]
    tm = min(TM_MERGE, s)

    def body(dh1_ref, ma_ref, mb_ref, pa_ref, pb_ref, wo_ref, woa_ref, wob_ref,
             dz_ref, dpa_ref, dpb_ref, dya_ref, dyb_ref):
        dm = _dot_nt(dh1_ref[...].astype(BF16), wo_ref[...])
        sa = jax.nn.sigmoid(ma_ref[...].astype(F32))
        sb = jax.nn.sigmoid(mb_ref[...].astype(F32))
        dpa = (dm * sa).astype(BF16)
        dpb = (dm * sb).astype(BF16)
        dz_ref[:, 0:D] = (dm * pa_ref[...].astype(F32) * sa * (1.0 - sa)).astype(BF16)
        dz_ref[:, D:2 * D] = (dm * pb_ref[...].astype(F32) * sb * (1.0 - sb)).astype(BF16)
        dpa_ref[...] = dpa
        dpb_ref[...] = dpb
        dya_ref[...] = _dot_nt(dpa, woa_ref[...]).astype(BF16)
        dyb_ref[...] = _dot_nt(dpb, wob_ref[...]).astype(BF16)

    row = pl.BlockSpec((tm, D), lambda i: (i, 0))
    wsp = pl.BlockSpec((D, D), lambda i: (0, 0))
    return _call(
        body, name="merge_bwd", grid=(s // tm,),
        in_specs=[row, pl.BlockSpec((tm, D), lambda i: (i, 4)), pl.BlockSpec((tm, D), lambda i: (i, 5)),
                  row, row, wsp, wsp, wsp],
        out_specs=[pl.BlockSpec((tm, 2 * D), lambda i: (i, 2)), row, row, row, row],
        out_shape=[SDS((s, 6 * D), BF16)] + [SDS((s, D), BF16)] * 4, scratch_shapes=[],
        params=_cparams(("arbitrary",), 48), args=(dh1, z, z, pa, pb, w_out, w_oa, w_ob), comm=comm)


def _branch_b_bwd(dz, dyb, z, ln_g, ln_b, w_s, b_s_t, comm=None):
    s = z.shape[0]
    tb = min(T_BRANCH_B, s)

    def body(dz_in, dyb_ref, ub_ref, vb_ref, lg_ref, lb_ref, ws_ref, bs_ref,
             dz_ref, dws_ref, dbs_ref, dln_ref, du_s, dvln_s):
        del dz_in

        @pl.when(pl.program_id(0) == 0)
        def _():
            dws_ref[...] = jnp.zeros_like(dws_ref)
            dbs_ref[...] = jnp.zeros_like(dbs_ref)
            dln_ref[...] = jnp.zeros_like(dln_ref)

        lg = lg_ref[...]
        u, du, dv, rstd, vhat, vln = _sgu_common(ub_ref[...].astype(F32), vb_ref[...].astype(F32),
                                                 lg, lb_ref[...], True)
        vlnb = vln.astype(BF16)
        dyb_v = dyb_ref[...].astype(F32)
        wm = _masked_ws(ws_ref)
        keep = (lax.broadcasted_iota(jnp.int32, (CHUNK, CHUNK), 1)
                <= lax.broadcasted_iota(jnp.int32, (CHUNK, CHUNK), 0))
        bs = bs_ref[...]
        for c in range(tb // CHUNK):
            rs = slice(c * CHUNK, (c + 1) * CHUNK)
            for g in range(GROUPS):
                cs = slice(g * GROUP_DIM, (g + 1) * GROUP_DIM)
                v_blk = vlnb[rs, cs]
                sp = _dot(wm[g], v_blk) + bs[:, g:g + 1]
                d_sp = dyb_v[rs, cs] * u[rs, cs]
                d_spb = d_sp.astype(BF16)
                du_s[rs, cs] = dyb_v[rs, cs] * sp
                dvln_s[rs, cs] = _dot_tn(wm[g], d_spb)
                dws_ref[g] += jnp.where(keep, _dot_nt(d_spb, v_blk), 0.0)
                dbs_ref[g] += jnp.broadcast_to(jnp.sum(d_sp, axis=-1, keepdims=True), (CHUNK, CHUNK))
        dvln = dvln_s[...]
        dln_ref[0:1, :] += jnp.sum(dvln * vhat, axis=0, keepdims=True)
        dln_ref[1:2, :] += jnp.sum(dvln, axis=0, keepdims=True)
        dvh = dvln * lg
        d_v = rstd * (dvh - jnp.mean(dvh, axis=-1, keepdims=True)
                      - vhat * jnp.mean(dvh * vhat, axis=-1, keepdims=True))
        dz_ref[:, 0:D] = (du_s[...] * du).astype(BF16)
        dz_ref[:, D:2 * D] = (d_v * dv).astype(BF16)

    vec = pl.BlockSpec((1, D), lambda i: (0, 0))
    sq = pl.BlockSpec((GROUPS, CHUNK, CHUNK), lambda i: (0, 0, 0))
    return _call(
        body, name="branch_b_bwd", grid=(s // tb,),
        in_specs=[ANY, pl.BlockSpec((tb, D), lambda i: (i, 0)),
                  pl.BlockSpec((tb, D), lambda i: (i, 2)), pl.BlockSpec((tb, D), lambda i: (i, 3)), vec, vec, sq,
                  pl.BlockSpec((CHUNK, GROUPS), lambda i: (0, 0))],
        out_specs=[pl.BlockSpec((tb, 2 * D), lambda i: (i, 1)), sq, sq, pl.BlockSpec((8, D), lambda i: (0, 0))],
        out_shape=[SDS(dz.shape, BF16), SDS((GROUPS, CHUNK, CHUNK), F32), SDS((GROUPS, CHUNK, CHUNK), F32),
                   SDS((8, D), F32)],
        scratch_shapes=[pltpu.VMEM((tb, D), F32), pltpu.VMEM((tb, D), F32)], aliases={0: 0},
        params=_cparams(("arbitrary",), 40), args=(dz, dyb, z, z, ln_g, ln_b, w_s, b_s_t), comm=comm)


def _branch_a_bwd(dz, dya, z, hs, conv_w, conv_b, w_r, b_r, w_i, b_i, lam, comm=None):
    s = z.shape[0]
    ta = min(T_BRANCH_A, s)
    nb = s // ta
    per16 = ta // 16

    def body(dz_in, dya_ref, xa_ref, xp_ref, ga_ref, hs_ref, hp_ref, cw_ref, cb_ref, wr_ref, br_ref, wi_ref,
             bi_ref, lam_ref, dz_ref, vec_ref, dwr_ref, dwi_ref,
             a_s, b_s, h_s, r_s, i_s, m_s, dcar_s, acar_s, dxc_s):
        del dz_in
        i = pl.program_id(0)
        blk = nb - 1 - i

        @pl.when(i == 0)
        def _():
            dcar_s[...] = jnp.zeros_like(dcar_s)
            acar_s[...] = jnp.zeros_like(acar_s)
            dxc_s[...] = jnp.zeros_like(dxc_s)
            vec_ref[...] = jnp.zeros_like(vec_ref)
            dwr_ref[...] = jnp.zeros_like(dwr_ref)
            dwi_ref[...] = jnp.zeros_like(dwi_ref)

        cw = cw_ref[...]
        lam_v = lam_ref[...]
        xa = xa_ref[...].astype(F32)
        prev8 = jnp.where(blk > 0, xp_ref[...].astype(F32)[8:16], 0.0)
        xc = _conv_fwd(xa, prev8, cw, cb_ref[...])
        xcb = xc.astype(BF16)
        sp_lam = _softplus(-lam_v)
        _lru_gates(xc, xcb, wr_ref, br_ref[...], wi_ref, bi_ref[...], sp_lam, a_s, b_s, r_s, i_s, m_s)

        hs_v = hs_ref[...].astype(F32)
        hprev8 = jnp.where(blk > 0, hp_ref[...].astype(F32)[8:16], 0.0)
        h_m1 = _rows_shifted(hprev8, hs_v, 1)
        gg, dgg = _gelu_and_grad(ga_ref[...].astype(F32))
        dya_v = dya_ref[...].astype(F32)
        dz_ref[:, D:2 * D] = (dya_v * hs_v * dgg).astype(BF16)

        a_v = a_s[...]
        a_s[...] = _rows_advanced(a_v, acar_s[...], 1)
        b_s[...] = dya_v * gg

        row = lax.broadcasted_iota(jnp.int32, (8, D), 0)
        ng = ta // 8

        def group(gi, carry):
            off = pl.multiple_of((ng - 1 - gi) * 8, 8)
            c8 = a_s[pl.ds(off, 8), :]
            d8 = b_s[pl.ds(off, 8), :]
            for d in (1, 2, 4):
                c_sh = jnp.where(row < 8 - d, pltpu.roll(c8, 8 - d, 0), 1.0)
                d_sh = jnp.where(row < 8 - d, pltpu.roll(d8, 8 - d, 0), 0.0)
                d8 = c8 * d_sh + d8
                c8 = c8 * c_sh
            dh8 = d8 + c8 * carry
            h_s[pl.ds(off, 8), :] = dh8
            return jnp.broadcast_to(dh8[0:1, :], (8, D))

        dcar_s[...] = lax.fori_loop(0, ng, group, dcar_s[...])
        acar_s[...] = jnp.broadcast_to(a_v[0:1, :], (8, D))

        dbx = h_s[...]
        r_v, i_v, m_v = r_s[...], i_s[...], m_s[...]
        d_mult = dbx * xc * i_v
        d_loga = dbx * h_m1 * a_v - d_mult * (a_v * a_v) / m_v
        d_pr = d_loga * ((-LRU_C) * sp_lam) * r_v * (1.0 - r_v)
        d_pi = dbx * xc * m_v * i_v * (1.0 - i_v)
        vec_ref[7:8, :] += jnp.sum(d_loga * r_v, axis=0, keepdims=True) * (LRU_C * jax.nn.sigmoid(-lam_v))
        vec_ref[5:6, :] += jnp.sum(d_pr, axis=0, keepdims=True)
        vec_ref[6:7, :] += jnp.sum(d_pi, axis=0, keepdims=True)
        d_prb = d_pr.astype(BF16)
        d_pib = d_pi.astype(BF16)
        h_s[...] = dbx * i_v * m_v
        for h in range(HEADS):
            sl = slice(h * HEAD_DIM, (h + 1) * HEAD_DIM)
            h_s[:, sl] += _dot_nt(d_prb[:, sl], wr_ref[h]) + _dot_nt(d_pib[:, sl], wi_ref[h])
            dwr_ref[h] += _dot_tn(xcb[:, sl], d_prb[:, sl])
            dwi_ref[h] += _dot_tn(xcb[:, sl], d_pib[:, sl])
        d_xc = h_s[...]
        vec_ref[4:5, :] += jnp.sum(d_xc, axis=0, keepdims=True)
        vec_ref[0:1, :] += jnp.sum(d_xc * xa, axis=0, keepdims=True)
        d_xa = cw[0:1, :] * d_xc
        nxt = dxc_s[...]
        for k in range(1, CONV_K):
            vec_ref[k:k + 1, :] += jnp.sum(d_xc * _rows_shifted(prev8, xa, k), axis=0, keepdims=True)
            d_xa = d_xa + cw[k:k + 1, :] * _rows_advanced(d_xc, nxt, k)
        dz_ref[:, 0:D] = d_xa.astype(BF16)
        dxc_s[...] = d_xc[0:8, :]

    vec = pl.BlockSpec((1, D), lambda i: (0, 0))
    gate = pl.BlockSpec((HEADS, HEAD_DIM, HEAD_DIM), lambda i: (0, 0, 0))
    cur = lambda c: pl.BlockSpec((ta, D), lambda i: (nb - 1 - i, c))
    before = lambda c: pl.BlockSpec((16, D), lambda i: (jnp.maximum((nb - 1 - i) * per16 - 1, 0), c))
    return _call(
        body, name="branch_a_bwd", grid=(nb,),
        in_specs=[ANY, cur(0), cur(0), before(0), cur(1), cur(0), before(0),
                  pl.BlockSpec((CONV_K, D), lambda i: (0, 0)), vec, gate, vec, gate, vec, vec],
        out_specs=[pl.BlockSpec((ta, 2 * D), lambda i: (nb - 1 - i, 0)), pl.BlockSpec((8, D), lambda i: (0, 0)),
                   gate, gate],
        out_shape=[SDS(dz.shape, BF16), SDS((8, D), F32), SDS((HEADS, HEAD_DIM, HEAD_DIM), F32),
                   SDS((HEADS, HEAD_DIM, HEAD_DIM), F32)],
        scratch_shapes=[pltpu.VMEM((ta, D), F32)] * 6 + [pltpu.VMEM((8, D), F32)] * 3, aliases={0: 0},
        params=_cparams(("arbitrary",), 48),
        args=(dz, dya, z, z, z, hs, hs, conv_w, conv_b, w_r, b_r, w_i, b_i, lam), comm=comm)


def _in_bwd(dz, w_in_g, x, dh1, g_mix, comm=None):
    s = x.shape[0]
    tm = min(TM_ROWS, s)
    nj = N_SLOT

    def body(dz_ref, w_ref, x_ref, dh1_ref, g_ref, dx_ref, dg_ref, acc_s):
        i, j = pl.program_id(0), pl.program_id(1)

        @pl.when(j == 0)
        def _():
            acc_s[...] = jnp.zeros_like(acc_s)

        @pl.when((i == 0) & (j == 0))
        def _():
            dg_ref[...] = jnp.zeros_like(dg_ref)

        acc_s[...] += _dot_nt(dz_ref[...], w_ref[...])

        @pl.when(j == nj - 1)
        def _():
            xv = x_ref[...]
            rstd = lax.rsqrt(jnp.mean(xv * xv, axis=-1, keepdims=True) + NORM_EPS)
            xh = xv * rstd
            dn = acc_s[...]
            dg_ref[...] += jnp.sum(dn * xh, axis=0, keepdims=True)
            dhat = dn * g_ref[...]
            dx_ref[...] = dh1_ref[...] + rstd * (dhat - xh * jnp.mean(dhat * xh, axis=-1, keepdims=True))

    row = pl.BlockSpec((tm, D), lambda i, j: (i, 0))
    vec = pl.BlockSpec((1, D), lambda i, j: (0, 0))
    return _call(
        body, name="in_bwd", grid=(s // tm, nj),
        in_specs=[pl.BlockSpec((tm, W_IN_COLS), lambda i, j: (i, j)),
                  pl.BlockSpec((None, D, W_IN_COLS), lambda i, j: (j, 0, 0)), row, row, vec],
        out_specs=[row, vec],
        out_shape=[SDS((s, D), F32), SDS((1, D), F32)],
        scratch_shapes=[pltpu.VMEM((tm, D), F32)],
        params=_cparams(("arbitrary", "arbitrary"), 48), args=(dz, w_in_g, x, dh1, g_mix), comm=comm)


def _wgrad(name, a, b, nblk, a_split, b_split):
    s = a.shape[0]
    ts = min(TM_ROWS, s)
    a_w = a.shape[1] // nblk if a_split else a.shape[1]
    b_w = b.shape[1] // nblk if b_split else b.shape[1]

    def body(a_ref, b_ref, o_ref, acc_s):
        t = pl.program_id(1)

        @pl.when(t == 0)
        def _():
            acc_s[...] = jnp.zeros_like(acc_s)

        acc_s[...] += _dot_tn(a_ref[...].astype(BF16), b_ref[...].astype(BF16))

        @pl.when(t == pl.num_programs(1) - 1)
        def _():
            o_ref[...] = acc_s[...].astype(BF16)

    return pl.pallas_call(
        body, name=name, grid=(nblk, s // ts),
        in_specs=[pl.BlockSpec((ts, a_w), (lambda k, t: (t, k)) if a_split else (lambda k, t: (t, 0))),
                  pl.BlockSpec((ts, b_w), (lambda k, t: (t, k)) if b_split else (lambda k, t: (t, 0)))],
        out_specs=pl.BlockSpec((None, a_w, b_w), lambda k, t: (k, 0, 0)),
        out_shape=SDS((nblk, a_w, b_w), BF16),
        scratch_shapes=[pltpu.VMEM((a_w, b_w), F32)],
        compiler_params=_cparams(("arbitrary", "arbitrary"), 48),
    )(a, b)


def _wgrad_t(name, a_t, b, nblk, a_split, b_split, tokens):
    s = b.shape[0]
    ts = min(tokens, s)
    a_w = a_t.shape[0] // nblk if a_split else a_t.shape[0]
    b_w = b.shape[1] // nblk if b_split else b.shape[1]

    def body(a_ref, b_ref, o_ref, acc_s):
        t = pl.program_id(1)

        @pl.when(t == 0)
        def _():
            acc_s[...] = jnp.zeros_like(acc_s)

        acc_s[...] += _dot(a_ref[...], b_ref[...].astype(BF16))

        @pl.when(t == pl.num_programs(1) - 1)
        def _():
            o_ref[...] = acc_s[...].astype(BF16)

    return pl.pallas_call(
        body, name=name, grid=(nblk, s // ts),
        in_specs=[pl.BlockSpec((a_w, ts), (lambda k, t: (k, t)) if a_split else (lambda k, t: (0, t))),
                  pl.BlockSpec((ts, b_w), (lambda k, t: (t, k)) if b_split else (lambda k, t: (t, 0)))],
        out_specs=pl.BlockSpec((None, a_w, b_w), lambda k, t: (k, 0, 0)),
        out_shape=SDS((nblk, a_w, b_w), BF16),
        scratch_shapes=[pltpu.VMEM((a_w, b_w), F32)],
        compiler_params=_cparams(("arbitrary", "arbitrary"), 48),
    )(a_t, b)


def _place():
    x, y, c = lax.axis_index("x"), lax.axis_index("y"), lax.axis_index("c")
    return x, y, c


def _other_chips(x, y):
    return [(x, 1 - y, 2 * x + 1 - y), (1 - x, y, 2 * (1 - x) + y), (1 - x, 1 - y, 2 * (1 - x) + 1 - y)]


class _Plan:
    def __init__(self, arrays, out_shape, sems, start, finish, middle=None):
        self.arrays, self.out_shape, self.sems, self.start, self.finish = arrays, out_shape, sems, start, finish
        self.middle = middle


def _gather_plan(shards):
    n = len(shards)

    def copies(ins, outs, sems):
        send_sems, recv_sems, local_sems = sems
        x, y, c = _place()
        chip = 2 * x + y
        me = 2 * chip + c
        sib = (x, y, 1 - c)
        chips = _other_chips(x, y)

        def rc(k, t, src, blk, to):
            return pltpu.make_async_remote_copy(
                src_ref=src, dst_ref=outs[t].at[blk], send_sem=send_sems.at[k * n + t],
                recv_sem=recv_sems.at[k * n + t], device_id=to, device_id_type=MESH)

        (yx, yy, y_chip), (xx, xy, x_chip), _ = chips
        local = [pltpu.make_async_copy(ins[t], outs[t].at[me], local_sems.at[t]) for t in range(n)]
        sends = ([rc(0, t, ins[t], me, sib) for t in range(n)] + [rc(1, t, ins[t], me, (yx, yy, c)) for t in range(n)]
                 + [rc(2, t, ins[t], me, (xx, xy, c)) for t in range(n)])
        passed = [[rc(4 + j, t, outs[t].at[2 * pc + c], 2 * pc + c, sib) for t in range(n)]
                  for j, (_, _, pc) in enumerate(chips)]
        relays = [[rc(3, t, outs[t].at[2 * y_chip + c], 2 * y_chip + c, (xx, xy, c)) for t in range(n)],
                  [rc(3, t, outs[t].at[2 * x_chip + c], 2 * x_chip + c, (yx, yy, c)) for t in range(n)]]
        return rc, local, sends, passed, relays, chips, chip, c, sib

    def start(ins, outs, sems):
        _, local, sends, _, _, _, _, _, _ = copies(ins, outs, sems)
        for cp in local + sends:
            cp.start()

    def middle(ins, outs, sems):
        rc, _, _, passed, relays, chips, _, c, sib = copies(ins, outs, sems)
        for j in range(2):
            for t in range(n):
                rc(1 + j, t, ins[t], 2 * chips[j][2] + c, sib).wait_recv()
        for j in range(2):
            for cp in passed[j]:
                cp.start()

            @pl.when(c == j)
            def _():
                for cp in relays[j]:
                    cp.start()

    def finish(ins, outs, sems):
        rc, local, sends, passed, relays, chips, chip, c, sib = copies(ins, outs, sems)
        far = 2 * chips[2][2] + c
        for t in range(n):
            rc(3, t, ins[t], far, sib).wait_recv()
        for cp in passed[2]:
            cp.start()
        for t in range(n):
            rc(0, t, ins[t], 2 * chip + 1 - c, sib).wait_recv()
        for j, (px, py, pc) in enumerate(chips):
            for t in range(n):
                rc(4 + j, t, ins[t], 2 * pc + 1 - c, sib).wait_recv()
        for cp in sends + passed[0] + passed[1] + passed[2]:
            cp.wait_send()
        for j in range(2):
            @pl.when(c == j)
            def _():
                for cp in relays[j]:
                    cp.wait_send()
        for cp in local:
            cp.wait()

    return _Plan(list(shards), [SDS((N_SLOT,) + tuple(a.shape), a.dtype) for a in shards],
                 [pltpu.SemaphoreType.DMA((7 * n,)), pltpu.SemaphoreType.DMA((7 * n,)),
                  pltpu.SemaphoreType.DMA((n,))], start, finish, middle)


def _sibling_plan(grads, whole=()):
    n, m = len(grads), len(whole)

    def copies(ins, outs, sems):
        send_sems, recv_sems = sems
        x, y, c = _place()
        sib = (x, y, 1 - c)

        def rc(t, src, dst):
            return pltpu.make_async_remote_copy(src_ref=src, dst_ref=dst, send_sem=send_sems.at[t],
                                                recv_sem=recv_sems.at[t], device_id=sib, device_id_type=MESH)
        return rc, c

    def start(ins, outs, sems):
        rc, c = copies(ins, outs, sems)
        for t in range(n):
            for j in range(4):
                rc(t, ins[t].at[2 * j + 1 - c], outs[t].at[j]).start()
        for t in range(n, n + m):
            rc(t, ins[t], outs[t]).start()

    def finish(ins, outs, sems):
        rc, _ = copies(ins, outs, sems)
        for t in range(n):
            rc(t, ins[t].at[pl.ds(0, 4)], outs[t]).wait()
        for t in range(n, n + m):
            rc(t, ins[t], outs[t]).wait()

    return _Plan(list(grads) + list(whole),
                 [SDS((4,) + tuple(g.shape[1:]), g.dtype) for g in grads] + [SDS(a.shape, a.dtype) for a in whole],
                 [pltpu.SemaphoreType.DMA((n + m,)), pltpu.SemaphoreType.DMA((n + m,))], start, finish)


def _chips_plan(parts, whole=()):
    n, m = len(parts), len(whole)

    def src_of(ins, t, pc):
        return ins[t].at[pc] if t < n else ins[t]

    def local_copies(ins, outs, sems, chip):
        return [pltpu.make_async_copy(src_of(ins, t, chip), outs[t].at[chip], sems[2].at[t]) for t in range(n + m)]

    def start(ins, outs, sems):
        send_sems, recv_sems, _ = sems
        x, y, c = _place()
        chip = 2 * x + y
        for cp in local_copies(ins, outs, sems, chip):
            cp.start()
        for px, py, pc in _other_chips(x, y):
            for t in range(n + m):
                pltpu.make_async_remote_copy(src_ref=src_of(ins, t, pc), dst_ref=outs[t].at[chip],
                                             send_sem=send_sems.at[t], recv_sem=recv_sems.at[t],
                                             device_id=(px, py, c), device_id_type=MESH).start()

    def finish(ins, outs, sems):
        send_sems, recv_sems, _ = sems
        x, y, c = _place()
        for t in range(n + m):
            three = outs[t].at[pl.ds(0, 3)]
            pltpu.make_async_remote_copy(src_ref=three, dst_ref=three, send_sem=send_sems.at[t],
                                         recv_sem=recv_sems.at[t], device_id=(x, y, c), device_id_type=MESH).wait()
        for cp in local_copies(ins, outs, sems, 2 * x + y):
            cp.wait()

    return _Plan(list(parts) + list(whole),
                 [SDS(p.shape, p.dtype) for p in parts] + [SDS((4,) + tuple(a.shape), a.dtype) for a in whole],
                 [pltpu.SemaphoreType.DMA((n + m,)), pltpu.SemaphoreType.DMA((n + m,)),
                  pltpu.SemaphoreType.DMA((n + m,))], start, finish)


def _exchange_plan(arr):
    def peers(x, y, c):
        flip = lambda v, f: 1 - v if f else v
        return [(flip(x, fx), flip(y, fy), flip(c, fc))
                for fx in (0, 1) for fy in (0, 1) for fc in (0, 1) if fx or fy or fc]

    def start(ins, outs, sems):
        x, y, c = _place()
        me = 4 * x + 2 * y + c
        pltpu.make_async_copy(ins[0], outs[0].at[me], sems[2].at[0]).start()
        for to in peers(x, y, c):
            pltpu.make_async_remote_copy(src_ref=ins[0], dst_ref=outs[0].at[me], send_sem=sems[0].at[0],
                                         recv_sem=sems[1].at[0], device_id=to, device_id_type=MESH).start()

    def finish(ins, outs, sems):
        x, y, c = _place()
        seven = outs[0].at[pl.ds(0, 7)]
        pltpu.make_async_remote_copy(src_ref=seven, dst_ref=seven, send_sem=sems[0].at[0], recv_sem=sems[1].at[0],
                                     device_id=(x, y, c), device_id_type=MESH).wait()
        pltpu.make_async_copy(ins[0], outs[0].at[4 * x + 2 * y + c], sems[2].at[0]).wait()

    return _Plan([arr], [SDS((N_SLOT,) + tuple(arr.shape), arr.dtype)],
                 [pltpu.SemaphoreType.DMA((1,)), pltpu.SemaphoreType.DMA((1,)), pltpu.SemaphoreType.DMA((1,))],
                 start, finish)


def _join(*plans):
    def cut(seq, sizes):
        out, at = [], 0
        for k in sizes:
            out.append(seq[at:at + k])
            at += k
        return out

    n_arr = [len(p.arrays) for p in plans]
    n_sem = [len(p.sems) for p in plans]

    def start(ins, outs, sems):
        for p, i, o, s in zip(plans, cut(ins, n_arr), cut(outs, n_arr), cut(sems, n_sem)):
            p.start(i, o, s)

    def finish(ins, outs, sems):
        for p, i, o, s in zip(plans, cut(ins, n_arr), cut(outs, n_arr), cut(sems, n_sem)):
            p.finish(i, o, s)

    def middle(ins, outs, sems):
        for p, i, o, s in zip(plans, cut(ins, n_arr), cut(outs, n_arr), cut(sems, n_sem)):
            if p.middle is not None:
                p.middle(i, o, s)

    return _Plan([a for p in plans for a in p.arrays], [o for p in plans for o in p.out_shape],
                 [s for p in plans for s in p.sems], start, finish,
                 middle if any(p.middle is not None for p in plans) else None)


def _run_plan(name, plan):
    k = len(plan.arrays)

    def body(*refs):
        ins, outs, sems = refs[:k], refs[k:2 * k], refs[2 * k:]
        plan.start(ins, outs, sems)
        if plan.middle is not None:
            plan.middle(ins, outs, sems)
        plan.finish(ins, outs, sems)

    return pl.pallas_call(
        body, name=name, in_specs=[ANY] * k, out_specs=[ANY] * k, out_shape=plan.out_shape,
        scratch_shapes=plan.sems, compiler_params=pltpu.CompilerParams(has_side_effects=True),
    )(*plan.arrays)


def _call(body, *, name, grid, in_specs, out_specs, out_shape, scratch_shapes, params, args, comm=None,
          aliases=None, prefetch=()):
    aliases = aliases or {}
    n_pre = len(prefetch)

    def launch(fn, ins_specs, outs_specs, outs_shape, scratch, operands):
        spec = pltpu.PrefetchScalarGridSpec(num_scalar_prefetch=n_pre, grid=grid, in_specs=ins_specs,
                                            out_specs=outs_specs, scratch_shapes=scratch)
        return pl.pallas_call(fn, name=name, grid_spec=spec, out_shape=outs_shape, compiler_params=params,
                              input_output_aliases=aliases)(*prefetch, *[_small_in_hbm(a) for a in operands])

    if comm is None:
        return list(launch(body, in_specs, out_specs, out_shape, scratch_shapes, args)), []
    n_in, n_out, n_scr, k = len(in_specs), len(out_specs), len(scratch_shapes), len(comm.arrays)

    def wrapped(*refs):
        pre, refs = refs[:n_pre], refs[n_pre:]
        ins = refs[:n_in]
        c_in = refs[n_in:n_in + k]
        outs = refs[n_in + k:n_in + k + n_out]
        c_out = refs[n_in + k + n_out:n_in + 2 * k + n_out]
        scr = refs[n_in + 2 * k + n_out:n_in + 2 * k + n_out + n_scr]
        sems = refs[n_in + 2 * k + n_out + n_scr:]
        step, steps = pl.program_id(0), grid[0]
        for d in range(1, len(grid)):
            step, steps = step * grid[d] + pl.program_id(d), steps * grid[d]

        @pl.when(step == 0)
        def _():
            comm.start(c_in, c_out, sems)

        if comm.middle is not None:
            @pl.when(step == (3 * steps) // 4)
            def _():
                comm.middle(c_in, c_out, sems)

        body(*pre, *ins, *outs, *scr)

        @pl.when(step == steps - 1)
        def _():
            comm.finish(c_in, c_out, sems)

    res = launch(wrapped, list(in_specs) + [ANY] * k, list(out_specs) + [ANY] * k,
                 list(out_shape) + list(comm.out_shape), list(scratch_shapes) + list(comm.sems),
                 tuple(args) + tuple(comm.arrays))
    return list(res[:n_out]), list(res[n_out:])


def _hand_over(name, a):
    def body(a_ref, o_ref):
        del a_ref, o_ref

    return pl.pallas_call(body, name=name, in_specs=[ANY], out_specs=ANY, out_shape=SDS(a.shape, a.dtype),
                          input_output_aliases={0: 0})(a)


def _row_tile(rows):
    for t in (512, 256, 128, 64, 32, 16, 8):
        if rows % t == 0:
            return t
    return rows


def _pair_sum(name, g8, recv4, core):
    _, rows, cols = recv4.shape
    tr = _row_tile(rows)
    g42 = g8.reshape(4, 2, rows, cols)

    def body(c_ref, g_ref, r_ref, o_ref):
        del c_ref
        o_ref[...] = (g_ref[...].astype(F32) + r_ref[...].astype(F32)).astype(o_ref.dtype)

    return pl.pallas_call(
        body, name=name,
        grid_spec=pltpu.PrefetchScalarGridSpec(
            num_scalar_prefetch=1, grid=(4, rows // tr),
            in_specs=[pl.BlockSpec((None, None, tr, cols), lambda j, i, c_ref: (j, c_ref[0], i, 0)),
                      pl.BlockSpec((None, tr, cols), lambda j, i, c_ref: (j, i, 0))],
            out_specs=pl.BlockSpec((None, tr, cols), lambda j, i, c_ref: (j, i, 0))),
        out_shape=SDS(recv4.shape, g8.dtype),
        compiler_params=_cparams(("arbitrary", "arbitrary"), 32),
    )(core, g42, recv4)


def _add2(name, a, b):
    rows, cols = a.shape
    tr = _row_tile(rows)

    def body(a_ref, b_ref, o_ref):
        o_ref[...] = a_ref[...] + b_ref[...]

    blk = pl.BlockSpec((tr, cols), lambda i: (i, 0))
    return pl.pallas_call(body, name=name, grid=(rows // tr,), in_specs=[blk, blk], out_specs=blk,
                          out_shape=SDS(a.shape, a.dtype),
                          compiler_params=_cparams(("arbitrary",), 32))(a, b)


def _sum_terms(name, terms):
    k, rows, cols = terms.shape
    tr = _row_tile(rows)

    def body(r_ref, o_ref):
        acc = r_ref[0]
        for q in range(1, k):
            acc = acc + r_ref[q]
        o_ref[...] = acc

    return pl.pallas_call(body, name=name, grid=(rows // tr,),
                          in_specs=[pl.BlockSpec((k, tr, cols), lambda i: (0, i, 0))],
                          out_specs=pl.BlockSpec((tr, cols), lambda i: (i, 0)),
                          out_shape=SDS((rows, cols), terms.dtype),
                          compiler_params=_cparams(("arbitrary",), 32))(terms)


def _adam_update(g, w, m, v):
    c1 = 1.0 / (1.0 - ADAM_B1 ** ADAM_STEP)
    c2 = 1.0 / (1.0 - ADAM_B2 ** ADAM_STEP)
    mn = ADAM_B1 * m + (1.0 - ADAM_B1) * g
    vn = ADAM_B2 * v + (1.0 - ADAM_B2) * (g * g)
    delta = (-ADAM_LR) * ((mn * c1) / (jnp.sqrt(vn * c2) + ADAM_EPS) + ADAM_WD * w)
    return delta, mn, vn


def _adamw_many(name, gs, ws, ms, vs):
    n = len(gs)

    def body(*refs):
        for p in range(n):
            g, w, m, v = (refs[q * n + p][...] for q in range(4))
            d, mn, vn = _adam_update(g, w, m, v)
            refs[4 * n + p][...] = d
            refs[5 * n + p][...] = mn
            refs[6 * n + p][...] = vn

    full = [pl.BlockSpec(memory_space=pltpu.VMEM)] * n
    shapes = [SDS(w.shape, F32) for w in ws]
    res = pl.pallas_call(body, name=name, in_specs=full * 4, out_specs=full * 3, out_shape=shapes * 3,
                         compiler_params=pltpu.CompilerParams(vmem_limit_bytes=32 * MiB))(*gs, *ws, *ms, *vs)
    return [(res[p], res[n + p], res[2 * n + p]) for p in range(n)]


def _adamw(name, terms, w, m, v):
    k, rows, cols = terms.shape
    tr = _row_tile(rows)

    def body(t_ref, w_ref, m_ref, v_ref, g_ref, d_ref, mo_ref, vo_ref):
        g = t_ref[0].astype(F32)
        for q in range(1, k):
            g = g + t_ref[q].astype(F32)
        g_ref[...] = g
        d_ref[...], mo_ref[...], vo_ref[...] = _adam_update(g, w_ref[...], m_ref[...], v_ref[...])

    blk = pl.BlockSpec((tr, cols), lambda i: (i, 0))
    return pl.pallas_call(body, name=name, grid=(rows // tr,),
                          in_specs=[pl.BlockSpec((k, tr, cols), lambda i: (0, i, 0)), blk, blk, blk],
                          out_specs=[blk] * 4, out_shape=[SDS((rows, cols), F32)] * 4,
                          compiler_params=_cparams(("arbitrary",), 40))(terms, w, m, v)


def kernel(x, norm_mix_g, w_in, conv_w, conv_b, w_rgate, b_rgate, w_igate, b_igate, lru_lambda, w_out_a, sgu_ln_g, sgu_ln_b, sgu_w_s, sgu_b_s, w_out_b, w_out, norm_mlp_g, w_up, w_down, norm_final_g, loss_target, m_norm_mix_g, m_w_in, m_conv_w, m_conv_b, m_w_rgate, m_b_rgate, m_w_igate, m_b_igate, m_lru_lambda, m_w_out_a, m_sgu_ln_g, m_sgu_ln_b, m_sgu_w_s, m_sgu_b_s, m_w_out_b, m_w_out, m_norm_mlp_g, m_w_up, m_w_down, m_norm_final_g, v_norm_mix_g, v_w_in, v_conv_w, v_conv_b, v_w_rgate, v_b_rgate, v_w_igate, v_b_igate, v_lru_lambda, v_w_out_a, v_sgu_ln_g, v_sgu_ln_b, v_sgu_w_s, v_sgu_b_s, v_w_out_b, v_w_out, v_norm_mlp_g, v_w_up, v_w_down, v_norm_final_g):
    cx, cy, cc = _place()
    me = 4 * cx + 2 * cy + cc
    core = jnp.reshape(cc, (1,)).astype(jnp.int32)
    xs = x[0]
    tgt = loss_target[0]
    s = xs.shape[0]

    gate_shard = jnp.stack([w_rgate[0], w_igate[0]]).astype(BF16).reshape(2 * HEADS * 32, HEAD_DIM)
    vec_shard = jnp.concatenate([conv_w[0], b_rgate[0], b_igate[0]], axis=1)
    vec_shard = jnp.pad(vec_shard, ((0, 4), (0, 256 - vec_shard.shape[1])))
    shards = [w_in[0].astype(BF16), w_out_a[0].astype(BF16), w_out_b[0].astype(BF16), w_out[0].astype(BF16),
              w_up[0].astype(BF16), w_down[0].astype(BF16), gate_shard, vec_shard]
    (z, n1_t, w_in_g), (gate_g, vec_g) = _in_proj(xs, norm_mix_g, shards[0], _slot_order(cx, cy, cc),
                                                comm=_gather_plan(shards[6:8]))
    gates = gate_g.reshape(N_SLOT, 2, HEADS, 32, HEAD_DIM).transpose(1, 2, 0, 3, 4).reshape(2, HEADS, HEAD_DIM, HEAD_DIM)
    w_r_f, w_i_f = gates[0], gates[1]
    conv_w_f = vec_g[:, 0:4, 0:128].transpose(1, 0, 2).reshape(CONV_K, D)
    b_r_f = vec_g[:, 0:4, 128:160].transpose(1, 0, 2).reshape(1, D)
    b_i_f = vec_g[:, 0:4, 160:192].transpose(1, 0, 2).reshape(1, D)
    b_s_t = jnp.transpose(sgu_b_s[0])

    (ya, hs), (w_oa_g, w_ob_g, w_out_g) = _branch_a_fwd(
        z, conv_w_f, conv_b, w_r_f, b_r_f, w_i_f, b_i_f, lru_lambda, comm=_gather_plan(shards[1:4]))
    w_oa_f = w_oa_g.reshape(D, D)
    w_ob_f = w_ob_g.reshape(D, D)
    w_out_f = w_out_g.reshape(D, D)
    (yb,), (w_up_g,) = _branch_b_fwd(z, sgu_ln_g, sgu_ln_b, sgu_w_s[0], b_s_t, comm=_gather_plan(shards[4:5]))
    (pa, pb, merged, h1), (w_down_g,) = _merge_out(ya, yb, z, xs, w_oa_f, w_ob_f, w_out_f,
                                                   comm=_gather_plan(shards[5:6]))
    w_down_f = w_down_g.reshape(N_SLOT * FF_COLS, D)
    gf2 = norm_final_g.reshape(1, D)
    r_act, act_t, n2_t, dh2, loss_acc, d_gfin = _mlp_fwd(h1, norm_mlp_g, w_up_g, w_down_f, gf2, tgt)

    def pair(names, grads, recv):
        return [_pair_sum("pair_sum_" + nm, g, r, core) for nm, g, r in zip(names, grads, recv)]

    g_down = _wgrad_t("wgrad_down", act_t, dh2, N_SLOT // 2, True, False, 2048)
    g_down = g_down.reshape(N_SLOT, FF_COLS, D)
    (df, dh1, d_gmlp), (r_down,) = _mlp_bwd(dh2, r_act, w_down_f, w_up_g, h1, norm_mlp_g,
                                            comm=_sibling_plan([g_down]))
    (p_down,) = pair(["down"], [g_down], [r_down])
    g_up = _wgrad_t("wgrad_up", n2_t, df, N_SLOT, False, True, 4096)
    g_out = _wgrad("wgrad_out", merged, dh1, 1, False, False).reshape(N_SLOT, D // N_SLOT, D)
    (dz, dpa, dpb, dya, dyb), (got_down, r_up, r_out) = _merge_bwd(
        dh1, z, pa, pb, w_out_f, w_oa_f, w_ob_f, comm=_join(_chips_plan([p_down]), _sibling_plan([g_up, g_out])))
    p_up, p_out = pair(["up", "out"], [g_up, g_out], [r_up, r_out])
    g_oa = _wgrad("wgrad_out_a", ya, dpa, 1, False, False).reshape(N_SLOT, D // N_SLOT, D)
    g_ob = _wgrad("wgrad_out_b", yb, dpb, 1, False, False).reshape(N_SLOT, D // N_SLOT, D)
    (dz, d_ws, d_bs, d_ln), (got_up, r_oa, r_ob) = _branch_b_bwd(
        dz, dyb, z, sgu_ln_g, sgu_ln_b, sgu_w_s[0], b_s_t,
        comm=_join(_chips_plan([p_up]), _sibling_plan([g_oa, g_ob])))
    p_oa, p_ob = pair(["out_a", "out_b"], [g_oa, g_ob], [r_oa, r_ob])
    (dz, d_vec, d_wr, d_wi), (got_out, got_oa, got_ob) = _branch_a_bwd(
        dz, dya, z, hs, conv_w_f, conv_b, w_r_f, b_r_f, w_i_f, b_i_f, lru_lambda,
        comm=_chips_plan([p_out, p_oa, p_ob]))
    g_in = _wgrad_t("wgrad_in", n1_t, dz, N_SLOT, False, True, 4096)
    g_gate = jnp.stack([d_wr, d_wi]).reshape(2, HEADS, N_SLOT, 32, HEAD_DIM).transpose(2, 0, 1, 3, 4)
    g_gate = g_gate.reshape(N_SLOT, 2 * HEADS * 32, HEAD_DIM).astype(BF16)

    d_bs_row = jnp.pad(d_bs[:, :, 0].reshape(1, GROUPS * CHUNK), ((0, 0), (0, D - GROUPS * CHUNK)))
    vecs = jnp.concatenate([d_vec, jnp.concatenate([d_ln[0:2], d_gmlp, d_gfin, d_bs_row, jnp.zeros((3, D), F32)])])
    d_ws2 = d_ws.reshape(GROUPS * CHUNK, CHUNK)
    r_in, r_gate, r_vecs, r_ws = _run_plan("rs_sibling_in", _sibling_plan([g_in, g_gate], [vecs, d_ws2]))
    p_in, p_gate = pair(["in", "gate"], [g_in, g_gate], [r_in, r_gate])
    vecs_chip = _add2("pair_sum_vecs", vecs, r_vecs)
    ws_chip = _add2("pair_sum_ws", d_ws2, r_ws)
    (dx, d_gmix), (got_in, got_gate, got_vecs, got_ws) = _in_bwd(
        dz, w_in_g, xs, dh1, norm_mix_g, comm=_chips_plan([p_in, p_gate], [vecs_chip, ws_chip]))
    vecs_sum = _sum_terms("sum_vecs", got_vecs)
    last = jnp.concatenate([d_gmix, jnp.pad(loss_acc[0:1], ((0, 0), (0, D - 128))), jnp.zeros((6, D), F32)])
    (last_all,) = _run_plan("exchange_last", _exchange_plan(last))
    last_sum = _sum_terms("sum_last", last_all)
    loss = last_sum[1, 0]
    got = [got_in, got_oa, got_ob, got_out, got_up, got_down, got_gate]

    def step(nm, terms, w, m, v, rows, cols):
        g, d, mn, vn = _adamw("adamw_" + nm, terms.reshape(4, rows, cols), w.reshape(rows, cols),
                              m.reshape(rows, cols), v.reshape(rows, cols))
        return [a.reshape(w.shape) for a in (g, d, mn, vn)]

    o_in = step("in", got[0], w_in, m_w_in, v_w_in, D, W_IN_COLS)
    o_oa = step("out_a", got[1], w_out_a, m_w_out_a, v_w_out_a, D // N_SLOT, D)
    o_ob = step("out_b", got[2], w_out_b, m_w_out_b, v_w_out_b, D // N_SLOT, D)
    o_out = step("out", got[3], w_out, m_w_out, v_w_out, D // N_SLOT, D)
    o_up = step("up", got[4], w_up, m_w_up, v_w_up, D, FF_COLS)
    o_down = step("down", got[5], w_down, m_w_down, v_w_down, FF_COLS, D)
    gate_w = jnp.stack([w_rgate[0], w_igate[0]]).reshape(2 * HEADS * 32, HEAD_DIM)
    gate_m = jnp.stack([m_w_rgate[0], m_w_igate[0]]).reshape(2 * HEADS * 32, HEAD_DIM)
    gate_v = jnp.stack([v_w_rgate[0], v_w_igate[0]]).reshape(2 * HEADS * 32, HEAD_DIM)
    o_gate = _adamw("adamw_gate", got[6], gate_w, gate_m, gate_v)
    o_gate = [a.reshape(2, 1, HEADS, 32, HEAD_DIM) for a in o_gate]
    o_wr = [a[0] for a in o_gate]
    o_wi = [a[1] for a in o_gate]

    def own(full, width):
        return lax.dynamic_slice_in_dim(full, me * width, width, axis=1)

    small_g = {
        "norm_mix_g": last_sum[0:1], "conv_w": own(vecs_sum[0:4], 128), "conv_b": vecs_sum[4:5],
        "b_rgate": own(vecs_sum[5:6].reshape(HEADS, HEAD_DIM), 32),
        "b_igate": own(vecs_sum[6:7].reshape(HEADS, HEAD_DIM), 32),
        "lru_lambda": vecs_sum[7:8], "sgu_ln_g": vecs_sum[8:9], "sgu_ln_b": vecs_sum[9:10],
        "norm_mlp_g": vecs_sum[10:11], "norm_final_g": vecs_sum[11:12],
        "sgu_b_s": vecs_sum[12, 0:GROUPS * CHUNK].reshape(GROUPS, CHUNK),
    }
    small_w = {"norm_mix_g": (norm_mix_g, m_norm_mix_g, v_norm_mix_g), "conv_w": (conv_w, m_conv_w, v_conv_w),
               "conv_b": (conv_b, m_conv_b, v_conv_b), "b_rgate": (b_rgate, m_b_rgate, v_b_rgate),
               "b_igate": (b_igate, m_b_igate, v_b_igate), "lru_lambda": (lru_lambda, m_lru_lambda, v_lru_lambda),
               "sgu_ln_g": (sgu_ln_g, m_sgu_ln_g, v_sgu_ln_g), "sgu_ln_b": (sgu_ln_b, m_sgu_ln_b, v_sgu_ln_b),
               "norm_mlp_g": (norm_mlp_g, m_norm_mlp_g, v_norm_mlp_g),
               "norm_final_g": (norm_final_g, m_norm_final_g, v_norm_final_g),
               "sgu_b_s": (sgu_b_s, m_sgu_b_s, v_sgu_b_s), "sgu_w_s": (sgu_w_s, m_sgu_w_s, v_sgu_w_s)}
    order = list(small_g)
    as2d = lambda k, a: a.reshape(small_g[k].shape)
    upd = _adamw_many("adamw_small", [small_g[k] for k in order], *[[as2d(k, small_w[k][q]) for k in order]
                                                                     for q in range(3)])
    o_small = {k: [a.reshape(small_w[k][0].shape) for a in (small_g[k],) + u] for k, u in zip(order, upd)}
    ws3 = [a[0].reshape(GROUPS * CHUNK, CHUNK) for a in small_w.pop("sgu_w_s")]
    o_small["sgu_w_s"] = [a.reshape(sgu_w_s.shape) for a in _adamw("adamw_ws", got_ws, *ws3)]

    per_weight = {"norm_mix_g": o_small["norm_mix_g"], "w_in": o_in, "conv_w": o_small["conv_w"],
                  "conv_b": o_small["conv_b"], "w_rgate": o_wr, "b_rgate": o_small["b_rgate"], "w_igate": o_wi,
                  "b_igate": o_small["b_igate"], "lru_lambda": o_small["lru_lambda"], "w_out_a": o_oa,
                  "sgu_ln_g": o_small["sgu_ln_g"], "sgu_ln_b": o_small["sgu_ln_b"], "sgu_w_s": o_small["sgu_w_s"],
                  "sgu_b_s": o_small["sgu_b_s"], "w_out_b": o_ob, "w_out": o_out, "norm_mlp_g": o_small["norm_mlp_g"],
                  "w_up": o_up, "w_down": o_down, "norm_final_g": o_small["norm_final_g"]}
    names_w = list(per_weight)
    return (loss, _hand_over("dx_out", dx)[None], *[per_weight[k][0] for k in names_w],
            *[per_weight[k][1] for k in names_w],
            *[per_weight[k][2] for k in names_w], *[per_weight[k][3] for k in names_w])
```

```python
import jax
import jax.numpy as jnp
from jax import lax
from jax.experimental import pallas as pl
from jax.experimental.pallas import tpu as pltpu

F32 = jnp.float32
BF16 = jnp.bfloat16
SDS = jax.ShapeDtypeStruct
MESH = pl.DeviceIdType.MESH
ANY = pl.BlockSpec(memory_space=pltpu.HBM)

D = 1024
N_SLOT = 8
W_IN_COLS = 768
FF_COLS = 512
HEADS, HEAD_DIM = 4, 256
GROUPS, GROUP_DIM = 4, 256
CHUNK = 128
CONV_K = 4
NORM_EPS = 1e-6
LN_EPS = 1e-5
LRU_C = 8.0
ADAM_LR, ADAM_B1, ADAM_B2, ADAM_EPS, ADAM_WD, ADAM_STEP = 0.001, 0.9, 0.999, 1e-08, 0.01, 10

TM_ROWS = 1024
TM_MERGE = 512
T_BRANCH_A = 512
T_BRANCH_B = 512
MiB = 1024 * 1024
SMALL_OPERAND = 16 * 1024

_GELU_C = 0.7978845608028654
_GELU_A = 0.044715


def _small_in_hbm(a):
    return pltpu.with_memory_space_constraint(a, pltpu.HBM) if a.size <= SMALL_OPERAND else a


def _cparams(sem, vmem_mib):
    return pltpu.CompilerParams(dimension_semantics=sem, vmem_limit_bytes=vmem_mib * MiB)


def _gelu(x):
    t = jnp.tanh(_GELU_C * (x + _GELU_A * x * x * x))
    return 0.5 * x * (1.0 + t)


def _gelu_and_grad(x):
    x2 = x * x
    t = jnp.tanh(_GELU_C * x * (1.0 + _GELU_A * x2))
    g = 0.5 * x * (1.0 + t)
    dg = 0.5 * (1.0 + t) + 0.5 * x * (1.0 - t * t) * _GELU_C * (1.0 + 3.0 * _GELU_A * x2)
    return g, dg


def _softplus(x):
    return jnp.maximum(x, 0.0) + jnp.log1p(jnp.exp(-jnp.abs(x)))


def _dot(a, b):
    return jnp.dot(a, b, preferred_element_type=F32)


def _dot_nt(a, b):
    return lax.dot_general(a, b, (((1,), (1,)), ((), ())), preferred_element_type=F32)


def _dot_tn(a, b):
    return lax.dot_general(a, b, (((0,), (0,)), ((), ())), preferred_element_type=F32)


def _rows_shifted(prev8, cur, k):
    ext = jnp.concatenate([prev8, cur], axis=0)
    return pltpu.roll(ext, k, 0)[8:]


def _rows_advanced(cur, next8, k):
    t = cur.shape[0]
    ext = jnp.concatenate([cur, next8], axis=0)
    return pltpu.roll(ext, t + 8 - k, 0)[:t]


def _first_second(x, y, c):
    ny, nx, far = _other_chips(x, y)
    pick = lambda a, b: a * (1 - c) + b * c
    first = tuple(pick(a, b) for a, b in zip(ny, nx))
    second = tuple(pick(b, a) for a, b in zip(ny, nx))
    return first, second, far


def _slot_order(x, y, c):
    chip = 2 * x + y
    first, second, far = _first_second(x, y, c)
    order = [2 * chip + c, 2 * chip + 1 - c, 2 * first[2] + c, 2 * second[2] + 1 - c, 2 * second[2] + c,
             2 * first[2] + 1 - c, 2 * far[2] + c, 2 * far[2] + 1 - c]
    return jnp.stack(order).astype(jnp.int32)


def _in_proj(x, g_mix, w_in_own, order, comm=None):
    s = x.shape[0]
    tm = min(TM_ROWS, s)
    ni = s // tm

    def body(order_ref, x_ref, g_ref, own_ref, z_ref, nt_ref, wg_ref, n_s, w_s, send_sems, recv_sems, local_sems):
        j, i = pl.program_id(0), pl.program_id(1)
        px, py, c = _place()
        chip = 2 * px + py
        me = 2 * chip + c
        sib = (px, py, 1 - c)
        chips = _other_chips(px, py)

        def rc(k, src, blk, to):
            return pltpu.make_async_remote_copy(src_ref=src, dst_ref=w_s.at[blk], send_sem=send_sems.at[k],
                                                recv_sem=recv_sems.at[k], device_id=to, device_id_type=MESH)

        del chips
        first, second, far = _first_second(px, py, c)
        blocks = [2 * first[2] + c, 2 * second[2] + c, 2 * far[2] + c]
        own_in = pltpu.make_async_copy(own_ref, w_s.at[me], local_sems.at[0])
        to_first = rc(1, own_ref, me, (first[0], first[1], c))
        to_second = rc(2, own_ref, me, (second[0], second[1], c))
        relay = rc(3, w_s.at[blocks[0]], blocks[0], (second[0], second[1], c))
        sends = [rc(0, own_ref, me, sib), to_first, to_second, relay]
        passed = [rc(4 + q, w_s.at[blk], blk, sib) for q, blk in enumerate(blocks)]
        keep = pltpu.make_async_copy(w_s, wg_ref, local_sems.at[1])

        @pl.when((i == 0) & (j == 0))
        def _():
            own_in.start()
            sends[0].start()
            to_first.start()
            own_in.wait()

        @pl.when((i == 0) & (j == 1))
        def _():
            rc(0, own_ref, 2 * chip + 1 - c, sib).wait_recv()

        for q, blk in enumerate(blocks):
            @pl.when((i == 0) & (j == 2 + 2 * q))
            def _():
                rc(1 + q, own_ref, blk, sib).wait_recv()
                passed[q].start()
                if q == 0:
                    to_second.start()
                    relay.start()

            @pl.when((i == 0) & (j == 3 + 2 * q))
            def _():
                rc(4 + q, own_ref, order_ref[j], sib).wait_recv()

        rows = pl.ds(pl.multiple_of(i * tm, tm), tm)

        @pl.when(j == 0)
        def _():
            xv = x_ref[...]
            rstd = lax.rsqrt(jnp.mean(xv * xv, axis=-1, keepdims=True) + NORM_EPS)
            nb = (xv * rstd * g_ref[...]).astype(BF16)
            n_s[rows, :] = nb
            nt_ref[...] = nb.T

        z_ref[...] = _dot(n_s[rows, :], w_s[order_ref[j]]).astype(BF16)

        @pl.when((i == 0) & (j == N_SLOT - 1))
        def _():
            keep.start()

        @pl.when((i == ni - 1) & (j == N_SLOT - 1))
        def _():
            for cp in sends + passed:
                cp.wait_send()
            keep.wait()

    first_pass = lambda j, i, o: (jnp.where(j == 0, i, ni - 1), 0)
    (z, n1, w_in_g), extra = _call(
        body, name="in_proj", grid=(N_SLOT, ni), prefetch=(order,),
        in_specs=[pl.BlockSpec((tm, D), first_pass),
                  pl.BlockSpec((1, D), lambda j, i, o: (0, 0)), ANY],
        out_specs=[pl.BlockSpec((tm, W_IN_COLS), lambda j, i, o: (i, o[j])),
                   pl.BlockSpec((D, tm), lambda j, i, o: (0, jnp.where(j == 0, i, ni - 1))), ANY],
        out_shape=[SDS((s, N_SLOT * W_IN_COLS), BF16), SDS((D, s), BF16), SDS((N_SLOT, D, W_IN_COLS), BF16)],
        scratch_shapes=[pltpu.VMEM((s, D), BF16), pltpu.VMEM((N_SLOT, D, W_IN_COLS), BF16),
                        pltpu.SemaphoreType.DMA((7,)), pltpu.SemaphoreType.DMA((7,)), pltpu.SemaphoreType.DMA((2,))],
        params=_cparams(("arbitrary", "arbitrary"), 56), args=(x, g_mix, w_in_own), comm=comm)
    return (z, n1, w_in_g), extra


def _lru_gates(xc, xcb, wr_ref, br, wi_ref, bi, sp_lam, a_s, b_s, r_s=None, i_s=None, m_s=None):
    for h in range(HEADS):
        sl = slice(h * HEAD_DIM, (h + 1) * HEAD_DIM)
        r = jax.nn.sigmoid(_dot(xcb[:, sl], wr_ref[h]) + br[:, sl])
        ig = jax.nn.sigmoid(_dot(xcb[:, sl], wi_ref[h]) + bi[:, sl])
        log_a = (-LRU_C) * r * sp_lam[:, sl]
        a = jnp.exp(log_a)
        mult = jnp.sqrt(-jnp.tanh(log_a) * (a * a + 1.0))
        a_s[:, sl] = a
        b_s[:, sl] = xc[:, sl] * ig * mult
        if r_s is not None:
            r_s[:, sl] = r
            i_s[:, sl] = ig
            m_s[:, sl] = mult


def _conv_fwd(xa, prev8, cw, cb):
    xc = cb + cw[0:1, :] * xa
    for k in range(1, CONV_K):
        xc = xc + cw[k:k + 1, :] * _rows_shifted(prev8, xa, k)
    return xc


def _branch_a_fwd(z, conv_w, conv_b, w_r, b_r, w_i, b_i, lam, comm=None):
    s = z.shape[0]
    ta = min(T_BRANCH_A, s)
    per16 = ta // 16

    def body(xa_ref, xp_ref, ga_ref, cw_ref, cb_ref, wr_ref, br_ref, wi_ref, bi_ref, lam_ref,
             ya_ref, hs_ref, yat_ref, a_s, b_s, h_s, carry_s):
        i = pl.program_id(0)

        @pl.when(i == 0)
        def _():
            carry_s[...] = jnp.zeros_like(carry_s)

        xa = xa_ref[...].astype(F32)
        prev8 = jnp.where(i > 0, xp_ref[...].astype(F32)[8:16], 0.0)
        xc = _conv_fwd(xa, prev8, cw_ref[...], cb_ref[...])
        sp_lam = _softplus(-lam_ref[...])
        _lru_gates(xc, xc.astype(BF16), wr_ref, br_ref[...], wi_ref, bi_ref[...], sp_lam, a_s, b_s)

        row = lax.broadcasted_iota(jnp.int32, (8, D), 0)

        def group(g, carry):
            off = pl.multiple_of(g * 8, 8)
            a8 = a_s[pl.ds(off, 8), :]
            b8 = b_s[pl.ds(off, 8), :]
            for d in (1, 2, 4):
                a_sh = jnp.where(row >= d, pltpu.roll(a8, d, 0), 1.0)
                b_sh = jnp.where(row >= d, pltpu.roll(b8, d, 0), 0.0)
                b8 = a8 * b_sh + b8
                a8 = a8 * a_sh
            h8 = b8 + a8 * carry
            h_s[pl.ds(off, 8), :] = h8
            return jnp.broadcast_to(h8[7:8, :], (8, D))

        carry_s[...] = lax.fori_loop(0, ta // 8, group, carry_s[...])
        hs = h_s[...]
        hs_ref[...] = hs.astype(BF16)
        ya = (hs * _gelu(ga_ref[...].astype(F32))).astype(BF16)
        ya_ref[...] = ya
        yat_ref[...] = ya.T

    vec = pl.BlockSpec((1, D), lambda i: (0, 0))
    gate = pl.BlockSpec((HEADS, HEAD_DIM, HEAD_DIM), lambda i: (0, 0, 0))
    return _call(
        body, name="branch_a_fwd", grid=(s // ta,),
        in_specs=[pl.BlockSpec((ta, D), lambda i: (i, 0)),
                  pl.BlockSpec((16, D), lambda i: (jnp.maximum(i * per16 - 1, 0), 0)),
                  pl.BlockSpec((ta, D), lambda i: (i, 1)),
                  pl.BlockSpec((CONV_K, D), lambda i: (0, 0)), vec, gate, vec, gate, vec, vec],
        out_specs=[pl.BlockSpec((ta, D), lambda i: (i, 0)), pl.BlockSpec((ta, D), lambda i: (i, 0)),
                   pl.BlockSpec((D, ta), lambda i: (0, i))],
        out_shape=[SDS((s, D), BF16), SDS((s, D), BF16), SDS((D, s), BF16)],
        scratch_shapes=[pltpu.VMEM((ta, D), F32), pltpu.VMEM((ta, D), F32), pltpu.VMEM((ta, D), F32),
                        pltpu.VMEM((8, D), F32)],
        params=_cparams(("arbitrary",), 40), args=(z, z, z, conv_w, conv_b, w_r, b_r, w_i, b_i, lam), comm=comm)


def _sgu_common(ub, vb, lg, lb, with_grad):
    if with_grad:
        u, du = _gelu_and_grad(ub)
        v, dv = _gelu_and_grad(vb)
    else:
        u, v, du, dv = _gelu(ub), _gelu(vb), None, None
    mu = jnp.mean(v, axis=-1, keepdims=True)
    vc = v - mu
    rstd = lax.rsqrt(jnp.mean(vc * vc, axis=-1, keepdims=True) + LN_EPS)
    vhat = vc * rstd
    vln = vhat * lg + lb
    return u, du, dv, rstd, vhat, vln


def _masked_ws(ws_ref):
    t = lax.broadcasted_iota(jnp.int32, (CHUNK, CHUNK), 0)
    c = lax.broadcasted_iota(jnp.int32, (CHUNK, CHUNK), 1)
    keep = c <= t
    return [jnp.where(keep, ws_ref[g], 0.0).astype(BF16) for g in range(GROUPS)]


def _branch_b_fwd(z, ln_g, ln_b, w_s, b_s_t, comm=None):
    s = z.shape[0]
    tb = min(T_BRANCH_B, s)

    def body(ub_ref, vb_ref, lg_ref, lb_ref, ws_ref, bs_ref, yb_ref, ybt_ref):
        u, _, _, _, _, vln = _sgu_common(ub_ref[...].astype(F32), vb_ref[...].astype(F32),
                                         lg_ref[...], lb_ref[...], False)
        vlnb = vln.astype(BF16)
        wm = _masked_ws(ws_ref)
        bs = bs_ref[...]
        for c in range(tb // CHUNK):
            rs = slice(c * CHUNK, (c + 1) * CHUNK)
            for g in range(GROUPS):
                cs = slice(g * GROUP_DIM, (g + 1) * GROUP_DIM)
                sp = _dot(wm[g], vlnb[rs, cs]) + bs[:, g:g + 1]
                yb_ref[rs, cs] = (u[rs, cs] * sp).astype(BF16)
        ybt_ref[...] = yb_ref[...].T

    vec = pl.BlockSpec((1, D), lambda i: (0, 0))
    return _call(
        body, name="branch_b_fwd", grid=(s // tb,),
        in_specs=[pl.BlockSpec((tb, D), lambda i: (i, 2)), pl.BlockSpec((tb, D), lambda i: (i, 3)), vec, vec,
                  pl.BlockSpec((GROUPS, CHUNK, CHUNK), lambda i: (0, 0, 0)),
                  pl.BlockSpec((CHUNK, GROUPS), lambda i: (0, 0))],
        out_specs=[pl.BlockSpec((tb, D), lambda i: (i, 0)), pl.BlockSpec((D, tb), lambda i: (0, i))],
        out_shape=[SDS((s, D), BF16), SDS((D, s), BF16)], scratch_shapes=[],
        params=_cparams(("arbitrary",), 40), args=(z, z, ln_g, ln_b, w_s, b_s_t), comm=comm)


def _merge_out(ya, yb, z, x, w_oa, w_ob, w_out, comm=None):
    s = x.shape[0]
    tm = min(TM_MERGE, s)

    def body(ya_ref, yb_ref, ma_ref, mb_ref, x_ref, woa_ref, wob_ref, wo_ref, pa_ref, pb_ref, mg_ref, h1_ref):
        pa = _dot(ya_ref[...], woa_ref[...])
        pb = _dot(yb_ref[...], wob_ref[...])
        merged = (jax.nn.sigmoid(ma_ref[...].astype(F32)) * pa
                  + jax.nn.sigmoid(mb_ref[...].astype(F32)) * pb).astype(BF16)
        pa_ref[...] = pa.astype(BF16)
        pb_ref[...] = pb.astype(BF16)
        mg_ref[...] = merged.T
        h1_ref[...] = x_ref[...] + _dot(merged, wo_ref[...])

    row = pl.BlockSpec((tm, D), lambda i: (i, 0))
    wsp = pl.BlockSpec((D, D), lambda i: (0, 0))
    return _call(
        body, name="merge_out", grid=(s // tm,),
        in_specs=[row, row, pl.BlockSpec((tm, D), lambda i: (i, 4)), pl.BlockSpec((tm, D), lambda i: (i, 5)),
                  row, wsp, wsp, wsp],
        out_specs=[row, row, pl.BlockSpec((D, tm), lambda i: (0, i)), row],
        out_shape=[SDS((s, D), BF16), SDS((s, D), BF16), SDS((D, s), BF16), SDS((s, D), F32)], scratch_shapes=[],
        params=_cparams(("arbitrary",), 48), args=(ya, yb, z, z, x, w_oa, w_ob, w_out), comm=comm)


def _mlp_fwd(h1, g_mlp, w_up_g, w_down, g_fin, tgt):
    s = h1.shape[0]
    tm = min(TM_ROWS, s)
    nj = N_SLOT

    def body(h1_ref, gm_ref, wu_ref, wd_ref, gf_ref, t_ref, r_ref, at_ref, n2t_ref, dh2_ref, loss_ref, dgf_ref,
             n2_s, acc_s):
        i, j = pl.program_id(0), pl.program_id(1)

        @pl.when(j == 0)
        def _():
            hv = h1_ref[...]
            rstd = lax.rsqrt(jnp.mean(hv * hv, axis=-1, keepdims=True) + NORM_EPS)
            nb = (hv * rstd * gm_ref[...]).astype(BF16)
            n2_s[...] = nb
            n2t_ref[...] = nb.T
            acc_s[...] = jnp.zeros_like(acc_s)

        @pl.when((i == 0) & (j == 0))
        def _():
            loss_ref[...] = jnp.zeros_like(loss_ref)
            dgf_ref[...] = jnp.zeros_like(dgf_ref)

        r = jnp.maximum(_dot(n2_s[...], wu_ref[...]), 0.0)
        r_ref[...] = r.astype(BF16)
        act = (r * r).astype(BF16)
        at_ref[...] = act.T
        acc_s[...] += _dot(act, wd_ref[...])

        @pl.when(j == nj - 1)
        def _():
            h2 = h1_ref[...] + acc_s[...]
            rstd = lax.rsqrt(jnp.mean(h2 * h2, axis=-1, keepdims=True) + NORM_EPS)
            hh = h2 * rstd
            gf = gf_ref[...]
            e = hh * gf - t_ref[...]
            loss_ref[...] += jnp.sum(e * e) * (0.5 / D)
            dy = e * (1.0 / D)
            dgf_ref[...] += jnp.sum(dy * hh, axis=0, keepdims=True)
            dhh = dy * gf
            dh2_ref[...] = rstd * (dhh - hh * jnp.mean(dhh * hh, axis=-1, keepdims=True))

    row = pl.BlockSpec((tm, D), lambda i, j: (i, 0))
    vec = pl.BlockSpec((1, D), lambda i, j: (0, 0))
    return pl.pallas_call(
        body, name="mlp_fwd", grid=(s // tm, nj),
        in_specs=[row, vec, pl.BlockSpec((None, D, FF_COLS), lambda i, j: (j, 0, 0)),
                  pl.BlockSpec((FF_COLS, D), lambda i, j: (j, 0)), vec, row],
        out_specs=[pl.BlockSpec((tm, FF_COLS), lambda i, j: (i, j)), pl.BlockSpec((FF_COLS, tm), lambda i, j: (j, i)),
                   pl.BlockSpec((D, tm), lambda i, j: (0, i)), row, pl.BlockSpec((8, 128), lambda i, j: (0, 0)), vec],
        out_shape=[SDS((s, nj * FF_COLS), BF16), SDS((nj * FF_COLS, s), BF16), SDS((D, s), BF16), SDS((s, D), F32),
                   SDS((8, 128), F32), SDS((1, D), F32)],
        scratch_shapes=[pltpu.VMEM((tm, D), BF16), pltpu.VMEM((tm, D), F32)],
        compiler_params=_cparams(("arbitrary", "arbitrary"), 52),
    )(h1, _small_in_hbm(g_mlp), w_up_g, w_down, _small_in_hbm(g_fin), tgt)


def _mlp_bwd(dh2, r, w_down, w_up_g, h1, g_mlp, comm=None):
    s = h1.shape[0]
    tm = min(TM_ROWS, s)
    nj = N_SLOT

    def body(dh2_ref, r_ref, wd_ref, wu_ref, h1_ref, gm_ref, df_ref, dh1_ref, dgm_ref, dh2b_s, acc_s):
        i, j = pl.program_id(0), pl.program_id(1)

        @pl.when(j == 0)
        def _():
            dh2b_s[...] = dh2_ref[...].astype(BF16)
            acc_s[...] = jnp.zeros_like(acc_s)

        @pl.when((i == 0) & (j == 0))
        def _():
            dgm_ref[...] = jnp.zeros_like(dgm_ref)

        d_act = _dot_nt(dh2b_s[...], wd_ref[...])
        df = (d_act * (2.0 * r_ref[...].astype(F32))).astype(BF16)
        df_ref[...] = df
        acc_s[...] += _dot_nt(df, wu_ref[...])

        @pl.when(j == nj - 1)
        def _():
            hv = h1_ref[...]
            rstd = lax.rsqrt(jnp.mean(hv * hv, axis=-1, keepdims=True) + NORM_EPS)
            hh = hv * rstd
            dn2 = acc_s[...]
            dgm_ref[...] += jnp.sum(dn2 * hh, axis=0, keepdims=True)
            dhat = dn2 * gm_ref[...]
            dh1_ref[...] = dh2_ref[...] + rstd * (dhat - hh * jnp.mean(dhat * hh, axis=-1, keepdims=True))

    row = pl.BlockSpec((tm, D), lambda i, j: (i, 0))
    vec = pl.BlockSpec((1, D), lambda i, j: (0, 0))
    ffb = pl.BlockSpec((tm, FF_COLS), lambda i, j: (i, j))
    return _call(
        body, name="mlp_bwd", grid=(s // tm, nj),
        in_specs=[row, ffb, pl.BlockSpec((FF_COLS, D), lambda i, j: (j, 0)),
                  pl.BlockSpec((None, D, FF_COLS), lambda i, j: (j, 0, 0)), row, vec],
        out_specs=[ffb, row, vec],
        out_shape=[SDS((s, nj * FF_COLS), BF16), SDS((s, D), F32), SDS((1, D), F32)],
        scratch_shapes=[pltpu.VMEM((tm, D), BF16), pltpu.VMEM((tm, D), F32)],
        params=_cparams(("arbitrary", "arbitrary"), 52), args=(dh2, r, w_down, w_up_g, h1, g_mlp), comm=comm)


def _merge_bwd(dh1, z, pa, pb, w_out, w_oa, w_ob, comm=None):
    s = dh1.shape[0]
    tm = min(TM_MERGE, s)

    def body(dh1_ref, ma_ref, mb_ref, pa_ref, pb_ref, wo_ref, woa_ref, wob_ref,
             dz_ref, dpa_ref, dpb_ref, dya_ref, dyb_ref):
        dm = _dot_nt(dh1_ref[...].astype(BF16), wo_ref[...])
        sa = jax.nn.sigmoid(ma_ref[...].astype(F32))
        sb = jax.nn.sigmoid(mb_ref[...].astype(F32))
        dpa = (dm * sa).astype(BF16)
        dpb = (dm * sb).astype(BF16)
        dz_ref[:, 0:D] = (dm * pa_ref[...].astype(F32) * sa * (1.0 - sa)).astype(BF16)
        dz_ref[:, D:2 * D] = (dm * pb_ref[...].astype(F32) * sb * (1.0 - sb)).astype(BF16)
        dpa_ref[...] = dpa
        dpb_ref[...] = dpb
        dya_ref[...] = _dot_nt(dpa, woa_ref[...]).astype(BF16)
        dyb_ref[...] = _dot_nt(dpb, wob_ref[...]).astype(BF16)

    row = pl.BlockSpec((tm, D), lambda i: (i, 0))
    wsp = pl.BlockSpec((D, D), lambda i: (0, 0))
    return _call(
        body, name="merge_bwd", grid=(s // tm,),
        in_specs=[row, pl.BlockSpec((tm, D), lambda i: (i, 4)), pl.BlockSpec((tm, D), lambda i: (i, 5)),
                  row, row, wsp, wsp, wsp],
        out_specs=[pl.BlockSpec((tm, 2 * D), lambda i: (i, 2)), row, row, row, row],
        out_shape=[SDS((s, 6 * D), BF16)] + [SDS((s, D), BF16)] * 4, scratch_shapes=[],
        params=_cparams(("arbitrary",), 48), args=(dh1, z, z, pa, pb, w_out, w_oa, w_ob), comm=comm)


def _branch_b_bwd(dz, dyb, z, ln_g, ln_b, w_s, b_s_t, comm=None):
    s = z.shape[0]
    tb = min(T_BRANCH_B, s)

    def body(dz_in, dyb_ref, ub_ref, vb_ref, lg_ref, lb_ref, ws_ref, bs_ref,
             dz_ref, dws_ref, dbs_ref, dln_ref, du_s, dvln_s):
        del dz_in

        @pl.when(pl.program_id(0) == 0)
        def _():
            dws_ref[...] = jnp.zeros_like(dws_ref)
            dbs_ref[...] = jnp.zeros_like(dbs_ref)
            dln_ref[...] = jnp.zeros_like(dln_ref)

        lg = lg_ref[...]
        u, du, dv, rstd, vhat, vln = _sgu_common(ub_ref[...].astype(F32), vb_ref[...].astype(F32),
                                                 lg, lb_ref[...], True)
        vlnb = vln.astype(BF16)
        dyb_v = dyb_ref[...].astype(F32)
        wm = _masked_ws(ws_ref)
        keep = (lax.broadcasted_iota(jnp.int32, (CHUNK, CHUNK), 1)
                <= lax.broadcasted_iota(jnp.int32, (CHUNK, CHUNK), 0))
        bs = bs_ref[...]
        for c in range(tb // CHUNK):
            rs = slice(c * CHUNK, (c + 1) * CHUNK)
            for g in range(GROUPS):
                cs = slice(g * GROUP_DIM, (g + 1) * GROUP_DIM)
                v_blk = vlnb[rs, cs]
                sp = _dot(wm[g], v_blk) + bs[:, g:g + 1]
                d_sp = dyb_v[rs, cs] * u[rs, cs]
                d_spb = d_sp.astype(BF16)
                du_s[rs, cs] = dyb_v[rs, cs] * sp
                dvln_s[rs, cs] = _dot_tn(wm[g], d_spb)
                dws_ref[g] += jnp.where(keep, _dot_nt(d_spb, v_blk), 0.0)
                dbs_ref[g] += jnp.broadcast_to(jnp.sum(d_sp, axis=-1, keepdims=True), (CHUNK, CHUNK))
        dvln = dvln_s[...]
        dln_ref[0:1, :] += jnp.sum(dvln * vhat, axis=0, keepdims=True)
        dln_ref[1:2, :] += jnp.sum(dvln, axis=0, keepdims=True)
        dvh = dvln * lg
        d_v = rstd * (dvh - jnp.mean(dvh, axis=-1, keepdims=True)
                      - vhat * jnp.mean(dvh * vhat, axis=-1, keepdims=True))
        dz_ref[:, 0:D] = (du_s[...] * du).astype(BF16)
        dz_ref[:, D:2 * D] = (d_v * dv).astype(BF16)

    vec = pl.BlockSpec((1, D), lambda i: (0, 0))
    sq = pl.BlockSpec((GROUPS, CHUNK, CHUNK), lambda i: (0, 0, 0))
    return _call(
        body, name="branch_b_bwd", grid=(s // tb,),
        in_specs=[ANY, pl.BlockSpec((tb, D), lambda i: (i, 0)),
                  pl.BlockSpec((tb, D), lambda i: (i, 2)), pl.BlockSpec((tb, D), lambda i: (i, 3)), vec, vec, sq,
                  pl.BlockSpec((CHUNK, GROUPS), lambda i: (0, 0))],
        out_specs=[pl.BlockSpec((tb, 2 * D), lambda i: (i, 1)), sq, sq, pl.BlockSpec((8, D), lambda i: (0, 0))],
        out_shape=[SDS(dz.shape, BF16), SDS((GROUPS, CHUNK, CHUNK), F32), SDS((GROUPS, CHUNK, CHUNK), F32),
                   SDS((8, D), F32)],
        scratch_shapes=[pltpu.VMEM((tb, D), F32), pltpu.VMEM((tb, D), F32)], aliases={0: 0},
        params=_cparams(("arbitrary",), 40), args=(dz, dyb, z, z, ln_g, ln_b, w_s, b_s_t), comm=comm)


def _branch_a_bwd(dz, dya, z, hs, conv_w, conv_b, w_r, b_r, w_i, b_i, lam, comm=None):
    s = z.shape[0]
    ta = min(T_BRANCH_A, s)
    nb = s // ta
    per16 = ta // 16

    def body(dz_in, dya_ref, xa_ref, xp_ref, ga_ref, hs_ref, hp_ref, cw_ref, cb_ref, wr_ref, br_ref, wi_ref,
             bi_ref, lam_ref, dz_ref, vec_ref, dwr_ref, dwi_ref,
             a_s, b_s, h_s, r_s, i_s, m_s, dcar_s, acar_s, dxc_s):
        del dz_in
        i = pl.program_id(0)
        blk = nb - 1 - i

        @pl.when(i == 0)
        def _():
            dcar_s[...] = jnp.zeros_like(dcar_s)
            acar_s[...] = jnp.zeros_like(acar_s)
            dxc_s[...] = jnp.zeros_like(dxc_s)
            vec_ref[...] = jnp.zeros_like(vec_ref)
            dwr_ref[...] = jnp.zeros_like(dwr_ref)
            dwi_ref[...] = jnp.zeros_like(dwi_ref)

        cw = cw_ref[...]
        lam_v = lam_ref[...]
        xa = xa_ref[...].astype(F32)
        prev8 = jnp.where(blk > 0, xp_ref[...].astype(F32)[8:16], 0.0)
        xc = _conv_fwd(xa, prev8, cw, cb_ref[...])
        xcb = xc.astype(BF16)
        sp_lam = _softplus(-lam_v)
        _lru_gates(xc, xcb, wr_ref, br_ref[...], wi_ref, bi_ref[...], sp_lam, a_s, b_s, r_s, i_s, m_s)

        hs_v = hs_ref[...].astype(F32)
        hprev8 = jnp.where(blk > 0, hp_ref[...].astype(F32)[8:16], 0.0)
        h_m1 = _rows_shifted(hprev8, hs_v, 1)
        gg, dgg = _gelu_and_grad(ga_ref[...].astype(F32))
        dya_v = dya_ref[...].astype(F32)
        dz_ref[:, D:2 * D] = (dya_v * hs_v * dgg).astype(BF16)

        a_v = a_s[...]
        a_s[...] = _rows_advanced(a_v, acar_s[...], 1)
        b_s[...] = dya_v * gg

        row = lax.broadcasted_iota(jnp.int32, (8, D), 0)
        ng = ta // 8

        def group(gi, carry):
            off = pl.multiple_of((ng - 1 - gi) * 8, 8)
            c8 = a_s[pl.ds(off, 8), :]
            d8 = b_s[pl.ds(off, 8), :]
            for d in (1, 2, 4):
                c_sh = jnp.where(row < 8 - d, pltpu.roll(c8, 8 - d, 0), 1.0)
                d_sh = jnp.where(row < 8 - d, pltpu.roll(d8, 8 - d, 0), 0.0)
                d8 = c8 * d_sh + d8
                c8 = c8 * c_sh
            dh8 = d8 + c8 * carry
            h_s[pl.ds(off, 8), :] = dh8
            return jnp.broadcast_to(dh8[0:1, :], (8, D))

        dcar_s[...] = lax.fori_loop(0, ng, group, dcar_s[...])
        acar_s[...] = jnp.broadcast_to(a_v[0:1, :], (8, D))

        dbx = h_s[...]
        r_v, i_v, m_v = r_s[...], i_s[...], m_s[...]
        d_mult = dbx * xc * i_v
        d_loga = dbx * h_m1 * a_v - d_mult * (a_v * a_v) / m_v
        d_pr = d_loga * ((-LRU_C) * sp_lam) * r_v * (1.0 - r_v)
        d_pi = dbx * xc * m_v * i_v * (1.0 - i_v)
        vec_ref[7:8, :] += jnp.sum(d_loga * r_v, axis=0, keepdims=True) * (LRU_C * jax.nn.sigmoid(-lam_v))
        vec_ref[5:6, :] += jnp.sum(d_pr, axis=0, keepdims=True)
        vec_ref[6:7, :] += jnp.sum(d_pi, axis=0, keepdims=True)
        d_prb = d_pr.astype(BF16)
        d_pib = d_pi.astype(BF16)
        h_s[...] = dbx * i_v * m_v
        for h in range(HEADS):
            sl = slice(h * HEAD_DIM, (h + 1) * HEAD_DIM)
            h_s[:, sl] += _dot_nt(d_prb[:, sl], wr_ref[h]) + _dot_nt(d_pib[:, sl], wi_ref[h])
            dwr_ref[h] += _dot_tn(xcb[:, sl], d_prb[:, sl])
            dwi_ref[h] += _dot_tn(xcb[:, sl], d_pib[:, sl])
        d_xc = h_s[...]
        vec_ref[4:5, :] += jnp.sum(d_xc, axis=0, keepdims=True)
        vec_ref[0:1, :] += jnp.sum(d_xc * xa, axis=0, keepdims=True)
        d_xa = cw[0:1, :] * d_xc
        nxt = dxc_s[...]
        for k in range(1, CONV_K):
            vec_ref[k:k + 1, :] += jnp.sum(d_xc * _rows_shifted(prev8, xa, k), axis=0, keepdims=True)
            d_xa = d_xa + cw[k:k + 1, :] * _rows_advanced(d_xc, nxt, k)
        dz_ref[:, 0:D] = d_xa.astype(BF16)
        dxc_s[...] = d_xc[0:8, :]

    vec = pl.BlockSpec((1, D), lambda i: (0, 0))
    gate = pl.BlockSpec((HEADS, HEAD_DIM, HEAD_DIM), lambda i: (0, 0, 0))
    cur = lambda c: pl.BlockSpec((ta, D), lambda i: (nb - 1 - i, c))
    before = lambda c: pl.BlockSpec((16, D), lambda i: (jnp.maximum((nb - 1 - i) * per16 - 1, 0), c))
    return _call(
        body, name="branch_a_bwd", grid=(nb,),
        in_specs=[ANY, cur(0), cur(0), before(0), cur(1), cur(0), before(0),
                  pl.BlockSpec((CONV_K, D), lambda i: (0, 0)), vec, gate, vec, gate, vec, vec],
        out_specs=[pl.BlockSpec((ta, 2 * D), lambda i: (nb - 1 - i, 0)), pl.BlockSpec((8, D), lambda i: (0, 0)),
                   gate, gate],
        out_shape=[SDS(dz.shape, BF16), SDS((8, D), F32), SDS((HEADS, HEAD_DIM, HEAD_DIM), F32),
                   SDS((HEADS, HEAD_DIM, HEAD_DIM), F32)],
        scratch_shapes=[pltpu.VMEM((ta, D), F32)] * 6 + [pltpu.VMEM((8, D), F32)] * 3, aliases={0: 0},
        params=_cparams(("arbitrary",), 48),
        args=(dz, dya, z, z, z, hs, hs, conv_w, conv_b, w_r, b_r, w_i, b_i, lam), comm=comm)


def _in_bwd(dz, w_in_g, x, dh1, g_mix, comm=None):
    s = x.shape[0]
    tm = min(TM_ROWS, s)
    nj = N_SLOT

    def body(dz_ref, w_ref, x_ref, dh1_ref, g_ref, dx_ref, dg_ref, acc_s):
        i, j = pl.program_id(0), pl.program_id(1)

        @pl.when(j == 0)
        def _():
            acc_s[...] = jnp.zeros_like(acc_s)

        @pl.when((i == 0) & (j == 0))
        def _():
            dg_ref[...] = jnp.zeros_like(dg_ref)

        acc_s[...] += _dot_nt(dz_ref[...], w_ref[...])

        @pl.when(j == nj - 1)
        def _():
            xv = x_ref[...]
            rstd = lax.rsqrt(jnp.mean(xv * xv, axis=-1, keepdims=True) + NORM_EPS)
            xh = xv * rstd
            dn = acc_s[...]
            dg_ref[...] += jnp.sum(dn * xh, axis=0, keepdims=True)
            dhat = dn * g_ref[...]
            dx_ref[...] = dh1_ref[...] + rstd * (dhat - xh * jnp.mean(dhat * xh, axis=-1, keepdims=True))

    row = pl.BlockSpec((tm, D), lambda i, j: (i, 0))
    vec = pl.BlockSpec((1, D), lambda i, j: (0, 0))
    return _call(
        body, name="in_bwd", grid=(s // tm, nj),
        in_specs=[pl.BlockSpec((tm, W_IN_COLS), lambda i, j: (i, j)),
                  pl.BlockSpec((None, D, W_IN_COLS), lambda i, j: (j, 0, 0)), row, row, vec],
        out_specs=[row, vec],
        out_shape=[SDS((s, D), F32), SDS((1, D), F32)],
        scratch_shapes=[pltpu.VMEM((tm, D), F32)],
        params=_cparams(("arbitrary", "arbitrary"), 48), args=(dz, w_in_g, x, dh1, g_mix), comm=comm)


def _wgrad_t(name, a_t, b, nblk, a_split, b_split, tokens):
    s = b.shape[0]
    ts = min(tokens, s)
    a_w = a_t.shape[0] // nblk if a_split else a_t.shape[0]
    b_w = b.shape[1] // nblk if b_split else b.shape[1]

    def body(a_ref, b_ref, o_ref, acc_s):
        t = pl.program_id(1)

        @pl.when(t == 0)
        def _():
            acc_s[...] = jnp.zeros_like(acc_s)

        acc_s[...] += _dot(a_ref[...], b_ref[...].astype(BF16))

        @pl.when(t == pl.num_programs(1) - 1)
        def _():
            o_ref[...] = acc_s[...].astype(BF16)

    return pl.pallas_call(
        body, name=name, grid=(nblk, s // ts),
        in_specs=[pl.BlockSpec((a_w, ts), (lambda k, t: (k, t)) if a_split else (lambda k, t: (0, t))),
                  pl.BlockSpec((ts, b_w), (lambda k, t: (t, k)) if b_split else (lambda k, t: (t, 0)))],
        out_specs=pl.BlockSpec((None, a_w, b_w), lambda k, t: (k, 0, 0)),
        out_shape=SDS((nblk, a_w, b_w), BF16),
        scratch_shapes=[pltpu.VMEM((a_w, b_w), F32)],
        compiler_params=_cparams(("arbitrary", "arbitrary"), 48),
    )(a_t, b)


def _place():
    x, y, c = lax.axis_index("x"), lax.axis_index("y"), lax.axis_index("c")
    return x, y, c


def _other_chips(x, y):
    return [(x, 1 - y, 2 * x + 1 - y), (1 - x, y, 2 * (1 - x) + y), (1 - x, 1 - y, 2 * (1 - x) + 1 - y)]


class _Plan:
    def __init__(self, arrays, out_shape, sems, start, finish, middle=None):
        self.arrays, self.out_shape, self.sems, self.start, self.finish = arrays, out_shape, sems, start, finish
        self.middle = middle


def _gather_plan(shards):
    n = len(shards)

    def copies(ins, outs, sems):
        send_sems, recv_sems, local_sems = sems
        x, y, c = _place()
        chip = 2 * x + y
        me = 2 * chip + c
        sib = (x, y, 1 - c)
        chips = _other_chips(x, y)

        def rc(k, t, src, blk, to):
            return pltpu.make_async_remote_copy(
                src_ref=src, dst_ref=outs[t].at[blk], send_sem=send_sems.at[k * n + t],
                recv_sem=recv_sems.at[k * n + t], device_id=to, device_id_type=MESH)

        (yx, yy, y_chip), (xx, xy, x_chip), _ = chips
        local = [pltpu.make_async_copy(ins[t], outs[t].at[me], local_sems.at[t]) for t in range(n)]
        sends = ([rc(0, t, ins[t], me, sib) for t in range(n)] + [rc(1, t, ins[t], me, (yx, yy, c)) for t in range(n)]
                 + [rc(2, t, ins[t], me, (xx, xy, c)) for t in range(n)])
        passed = [[rc(4 + j, t, outs[t].at[2 * pc + c], 2 * pc + c, sib) for t in range(n)]
                  for j, (_, _, pc) in enumerate(chips)]
        relays = [[rc(3, t, outs[t].at[2 * y_chip + c], 2 * y_chip + c, (xx, xy, c)) for t in range(n)],
                  [rc(3, t, outs[t].at[2 * x_chip + c], 2 * x_chip + c, (yx, yy, c)) for t in range(n)]]
        return rc, local, sends, passed, relays, chips, chip, c, sib

    def start(ins, outs, sems):
        _, local, sends, _, _, _, _, _, _ = copies(ins, outs, sems)
        for cp in local + sends:
            cp.start()

    def middle(ins, outs, sems):
        rc, _, _, passed, relays, chips, _, c, sib = copies(ins, outs, sems)
        for j in range(2):
            for t in range(n):
                rc(1 + j, t, ins[t], 2 * chips[j][2] + c, sib).wait_recv()
        for j in range(2):
            for cp in passed[j]:
                cp.start()

            @pl.when(c == j)
            def _():
                for cp in relays[j]:
                    cp.start()

    def finish(ins, outs, sems):
        rc, local, sends, passed, relays, chips, chip, c, sib = copies(ins, outs, sems)
        far = 2 * chips[2][2] + c
        for t in range(n):
            rc(3, t, ins[t], far, sib).wait_recv()
        for cp in passed[2]:
            cp.start()
        for t in range(n):
            rc(0, t, ins[t], 2 * chip + 1 - c, sib).wait_recv()
        for j, (px, py, pc) in enumerate(chips):
            for t in range(n):
                rc(4 + j, t, ins[t], 2 * pc + 1 - c, sib).wait_recv()
        for cp in sends + passed[0] + passed[1] + passed[2]:
            cp.wait_send()
        for j in range(2):
            @pl.when(c == j)
            def _():
                for cp in relays[j]:
                    cp.wait_send()
        for cp in local:
            cp.wait()

    return _Plan(list(shards), [SDS((N_SLOT,) + tuple(a.shape), a.dtype) for a in shards],
                 [pltpu.SemaphoreType.DMA((7 * n,)), pltpu.SemaphoreType.DMA((7 * n,)),
                  pltpu.SemaphoreType.DMA((n,))], start, finish, middle)


def _sibling_plan(grads, whole=()):
    n, m = len(grads), len(whole)

    def copies(ins, outs, sems):
        send_sems, recv_sems = sems
        x, y, c = _place()
        sib = (x, y, 1 - c)

        def rc(t, src, dst):
            return pltpu.make_async_remote_copy(src_ref=src, dst_ref=dst, send_sem=send_sems.at[t],
                                                recv_sem=recv_sems.at[t], device_id=sib, device_id_type=MESH)
        return rc, c

    def start(ins, outs, sems):
        rc, c = copies(ins, outs, sems)
        for t in range(n):
            for j in range(4):
                rc(t, ins[t].at[2 * j + 1 - c], outs[t].at[j]).start()
        for t in range(n, n + m):
            rc(t, ins[t], outs[t]).start()

    def finish(ins, outs, sems):
        rc, _ = copies(ins, outs, sems)
        for t in range(n):
            rc(t, ins[t].at[pl.ds(0, 4)], outs[t]).wait()
        for t in range(n, n + m):
            rc(t, ins[t], outs[t]).wait()

    return _Plan(list(grads) + list(whole),
                 [SDS((4,) + tuple(g.shape[1:]), g.dtype) for g in grads] + [SDS(a.shape, a.dtype) for a in whole],
                 [pltpu.SemaphoreType.DMA((n + m,)), pltpu.SemaphoreType.DMA((n + m,))], start, finish)


def _chips_plan(parts, whole=()):
    n, m = len(parts), len(whole)

    def src_of(ins, t, pc):
        return ins[t].at[pc] if t < n else ins[t]

    def local_copies(ins, outs, sems, chip):
        return [pltpu.make_async_copy(src_of(ins, t, chip), outs[t].at[chip], sems[2].at[t]) for t in range(n + m)]

    def start(ins, outs, sems):
        send_sems, recv_sems, _ = sems
        x, y, c = _place()
        chip = 2 * x + y
        for cp in local_copies(ins, outs, sems, chip):
            cp.start()
        for px, py, pc in _other_chips(x, y):
            for t in range(n + m):
                pltpu.make_async_remote_copy(src_ref=src_of(ins, t, pc), dst_ref=outs[t].at[chip],
                                             send_sem=send_sems.at[t], recv_sem=recv_sems.at[t],
                                             device_id=(px, py, c), device_id_type=MESH).start()

    def finish(ins, outs, sems):
        send_sems, recv_sems, _ = sems
        x, y, c = _place()
        for t in range(n + m):
            three = outs[t].at[pl.ds(0, 3)]
            pltpu.make_async_remote_copy(src_ref=three, dst_ref=three, send_sem=send_sems.at[t],
                                         recv_sem=recv_sems.at[t], device_id=(x, y, c), device_id_type=MESH).wait()
        for cp in local_copies(ins, outs, sems, 2 * x + y):
            cp.wait()

    return _Plan(list(parts) + list(whole),
                 [SDS(p.shape, p.dtype) for p in parts] + [SDS((4,) + tuple(a.shape), a.dtype) for a in whole],
                 [pltpu.SemaphoreType.DMA((n + m,)), pltpu.SemaphoreType.DMA((n + m,)),
                  pltpu.SemaphoreType.DMA((n + m,))], start, finish)


def _exchange_plan(arr):
    def peers(x, y, c):
        flip = lambda v, f: 1 - v if f else v
        return [(flip(x, fx), flip(y, fy), flip(c, fc))
                for fx in (0, 1) for fy in (0, 1) for fc in (0, 1) if fx or fy or fc]

    def start(ins, outs, sems):
        x, y, c = _place()
        me = 4 * x + 2 * y + c
        pltpu.make_async_copy(ins[0], outs[0].at[me], sems[2].at[0]).start()
        for to in peers(x, y, c):
            pltpu.make_async_remote_copy(src_ref=ins[0], dst_ref=outs[0].at[me], send_sem=sems[0].at[0],
                                         recv_sem=sems[1].at[0], device_id=to, device_id_type=MESH).start()

    def finish(ins, outs, sems):
        x, y, c = _place()
        seven = outs[0].at[pl.ds(0, 7)]
        pltpu.make_async_remote_copy(src_ref=seven, dst_ref=seven, send_sem=sems[0].at[0], recv_sem=sems[1].at[0],
                                     device_id=(x, y, c), device_id_type=MESH).wait()
        pltpu.make_async_copy(ins[0], outs[0].at[4 * x + 2 * y + c], sems[2].at[0]).wait()

    return _Plan([arr], [SDS((N_SLOT,) + tuple(arr.shape), arr.dtype)],
                 [pltpu.SemaphoreType.DMA((1,)), pltpu.SemaphoreType.DMA((1,)), pltpu.SemaphoreType.DMA((1,))],
                 start, finish)


def _join(*plans):
    def cut(seq, sizes):
        out, at = [], 0
        for k in sizes:
            out.append(seq[at:at + k])
            at += k
        return out

    n_arr = [len(p.arrays) for p in plans]
    n_sem = [len(p.sems) for p in plans]

    def start(ins, outs, sems):
        for p, i, o, s in zip(plans, cut(ins, n_arr), cut(outs, n_arr), cut(sems, n_sem)):
            p.start(i, o, s)

    def finish(ins, outs, sems):
        for p, i, o, s in zip(plans, cut(ins, n_arr), cut(outs, n_arr), cut(sems, n_sem)):
            p.finish(i, o, s)

    def middle(ins, outs, sems):
        for p, i, o, s in zip(plans, cut(ins, n_arr), cut(outs, n_arr), cut(sems, n_sem)):
            if p.middle is not None:
                p.middle(i, o, s)

    return _Plan([a for p in plans for a in p.arrays], [o for p in plans for o in p.out_shape],
                 [s for p in plans for s in p.sems], start, finish,
                 middle if any(p.middle is not None for p in plans) else None)


def _run_plan(name, plan):
    k = len(plan.arrays)

    def body(*refs):
        ins, outs, sems = refs[:k], refs[k:2 * k], refs[2 * k:]
        plan.start(ins, outs, sems)
        if plan.middle is not None:
            plan.middle(ins, outs, sems)
        plan.finish(ins, outs, sems)

    return pl.pallas_call(
        body, name=name, in_specs=[ANY] * k, out_specs=[ANY] * k, out_shape=plan.out_shape,
        scratch_shapes=plan.sems, compiler_params=pltpu.CompilerParams(has_side_effects=True),
    )(*plan.arrays)


def _call(body, *, name, grid, in_specs, out_specs, out_shape, scratch_shapes, params, args, comm=None,
          aliases=None, prefetch=()):
    aliases = aliases or {}
    n_pre = len(prefetch)

    def launch(fn, ins_specs, outs_specs, outs_shape, scratch, operands):
        spec = pltpu.PrefetchScalarGridSpec(num_scalar_prefetch=n_pre, grid=grid, in_specs=ins_specs,
                                            out_specs=outs_specs, scratch_shapes=scratch)
        return pl.pallas_call(fn, name=name, grid_spec=spec, out_shape=outs_shape, compiler_params=params,
                              input_output_aliases=aliases)(*prefetch, *[_small_in_hbm(a) for a in operands])

    if comm is None:
        return list(launch(body, in_specs, out_specs, out_shape, scratch_shapes, args)), []
    n_in, n_out, n_scr, k = len(in_specs), len(out_specs), len(scratch_shapes), len(comm.arrays)

    def wrapped(*refs):
        pre, refs = refs[:n_pre], refs[n_pre:]
        ins = refs[:n_in]
        c_in = refs[n_in:n_in + k]
        outs = refs[n_in + k:n_in + k + n_out]
        c_out = refs[n_in + k + n_out:n_in + 2 * k + n_out]
        scr = refs[n_in + 2 * k + n_out:n_in + 2 * k + n_out + n_scr]
        sems = refs[n_in + 2 * k + n_out + n_scr:]
        step, steps = pl.program_id(0), grid[0]
        for d in range(1, len(grid)):
            step, steps = step * grid[d] + pl.program_id(d), steps * grid[d]

        @pl.when(step == 0)
        def _():
            comm.start(c_in, c_out, sems)

        if comm.middle is not None:
            @pl.when(step == (3 * steps) // 4)
            def _():
                comm.middle(c_in, c_out, sems)

        body(*pre, *ins, *outs, *scr)

        @pl.when(step == steps - 1)
        def _():
            comm.finish(c_in, c_out, sems)

    res = launch(wrapped, list(in_specs) + [ANY] * k, list(out_specs) + [ANY] * k,
                 list(out_shape) + list(comm.out_shape), list(scratch_shapes) + list(comm.sems),
                 tuple(args) + tuple(comm.arrays))
    return list(res[:n_out]), list(res[n_out:])


def _row_tile(rows):
    for t in (512, 256, 128, 64, 32, 16, 8):
        if rows % t == 0:
            return t
    return rows


def _pair_sum(name, g8, recv4, core):
    _, rows, cols = recv4.shape
    tr = _row_tile(rows)
    g42 = g8.reshape(4, 2, rows, cols)

    def body(c_ref, g_ref, r_ref, o_ref):
        del c_ref
        o_ref[...] = (g_ref[...].astype(F32) + r_ref[...].astype(F32)).astype(o_ref.dtype)

    return pl.pallas_call(
        body, name=name,
        grid_spec=pltpu.PrefetchScalarGridSpec(
            num_scalar_prefetch=1, grid=(4, rows // tr),
            in_specs=[pl.BlockSpec((None, None, tr, cols), lambda j, i, c_ref: (j, c_ref[0], i, 0)),
                      pl.BlockSpec((None, tr, cols), lambda j, i, c_ref: (j, i, 0))],
            out_specs=pl.BlockSpec((None, tr, cols), lambda j, i, c_ref: (j, i, 0))),
        out_shape=SDS(recv4.shape, g8.dtype),
        compiler_params=_cparams(("arbitrary", "arbitrary"), 32),
    )(core, g42, recv4)


def _add2(name, a, b):
    rows, cols = a.shape
    tr = _row_tile(rows)

    def body(a_ref, b_ref, o_ref):
        o_ref[...] = a_ref[...] + b_ref[...]

    blk = pl.BlockSpec((tr, cols), lambda i: (i, 0))
    return pl.pallas_call(body, name=name, grid=(rows // tr,), in_specs=[blk, blk], out_specs=blk,
                          out_shape=SDS(a.shape, a.dtype),
                          compiler_params=_cparams(("arbitrary",), 32))(a, b)


def _sum_terms(name, terms):
    k, rows, cols = terms.shape
    tr = _row_tile(rows)

    def body(r_ref, o_ref):
        acc = r_ref[0]
        for q in range(1, k):
            acc = acc + r_ref[q]
        o_ref[...] = acc

    return pl.pallas_call(body, name=name, grid=(rows // tr,),
                          in_specs=[pl.BlockSpec((k, tr, cols), lambda i: (0, i, 0))],
                          out_specs=pl.BlockSpec((tr, cols), lambda i: (i, 0)),
                          out_shape=SDS((rows, cols), terms.dtype),
                          compiler_params=_cparams(("arbitrary",), 32))(terms)


def _adam_update(g, w, m, v):
    c1 = 1.0 / (1.0 - ADAM_B1 ** ADAM_STEP)
    c2 = 1.0 / (1.0 - ADAM_B2 ** ADAM_STEP)
    mn = ADAM_B1 * m + (1.0 - ADAM_B1) * g
    vn = ADAM_B2 * v + (1.0 - ADAM_B2) * (g * g)
    delta = (-ADAM_LR) * ((mn * c1) / (jnp.sqrt(vn * c2) + ADAM_EPS) + ADAM_WD * w)
    return delta, mn, vn


def _adamw_many(name, gs, ws, ms, vs):
    n = len(gs)

    def body(*refs):
        for p in range(n):
            g, w, m, v = (refs[q * n + p][...] for q in range(4))
            d, mn, vn = _adam_update(g, w, m, v)
            refs[4 * n + p][...] = d
            refs[5 * n + p][...] = mn
            refs[6 * n + p][...] = vn

    full = [pl.BlockSpec(memory_space=pltpu.VMEM)] * n
    shapes = [SDS(w.shape, F32) for w in ws]
    res = pl.pallas_call(body, name=name, in_specs=full * 4, out_specs=full * 3, out_shape=shapes * 3,
                         compiler_params=pltpu.CompilerParams(vmem_limit_bytes=32 * MiB))(*gs, *ws, *ms, *vs)
    return [(res[p], res[n + p], res[2 * n + p]) for p in range(n)]


def _adamw(name, terms, w, m, v):
    k, rows, cols = terms.shape
    tr = _row_tile(rows)

    def body(t_ref, w_ref, m_ref, v_ref, g_ref, d_ref, mo_ref, vo_ref):
        g = t_ref[0].astype(F32)
        for q in range(1, k):
            g = g + t_ref[q].astype(F32)
        g_ref[...] = g
        d_ref[...], mo_ref[...], vo_ref[...] = _adam_update(g, w_ref[...], m_ref[...], v_ref[...])

    blk = pl.BlockSpec((tr, cols), lambda i: (i, 0))
    return pl.pallas_call(body, name=name, grid=(rows // tr,),
                          in_specs=[pl.BlockSpec((k, tr, cols), lambda i: (0, i, 0)), blk, blk, blk],
                          out_specs=[blk] * 4, out_shape=[SDS((rows, cols), F32)] * 4,
                          compiler_params=_cparams(("arbitrary",), 40))(terms, w, m, v)


def kernel(x, norm_mix_g, w_in, conv_w, conv_b, w_rgate, b_rgate, w_igate, b_igate, lru_lambda, w_out_a, sgu_ln_g, sgu_ln_b, sgu_w_s, sgu_b_s, w_out_b, w_out, norm_mlp_g, w_up, w_down, norm_final_g, loss_target, m_norm_mix_g, m_w_in, m_conv_w, m_conv_b, m_w_rgate, m_b_rgate, m_w_igate, m_b_igate, m_lru_lambda, m_w_out_a, m_sgu_ln_g, m_sgu_ln_b, m_sgu_w_s, m_sgu_b_s, m_w_out_b, m_w_out, m_norm_mlp_g, m_w_up, m_w_down, m_norm_final_g, v_norm_mix_g, v_w_in, v_conv_w, v_conv_b, v_w_rgate, v_b_rgate, v_w_igate, v_b_igate, v_lru_lambda, v_w_out_a, v_sgu_ln_g, v_sgu_ln_b, v_sgu_w_s, v_sgu_b_s, v_w_out_b, v_w_out, v_norm_mlp_g, v_w_up, v_w_down, v_norm_final_g):
    cx, cy, cc = _place()
    me = 4 * cx + 2 * cy + cc
    core = jnp.reshape(cc, (1,)).astype(jnp.int32)
    xs = x[0]
    tgt = loss_target[0]
    s = xs.shape[0]

    gate_shard = jnp.stack([w_rgate[0], w_igate[0]]).astype(BF16).reshape(2 * HEADS * 32, HEAD_DIM)
    vec_shard = jnp.concatenate([conv_w[0], b_rgate[0], b_igate[0]], axis=1)
    vec_shard = jnp.pad(vec_shard, ((0, 4), (0, 256 - vec_shard.shape[1])))
    shards = [w_in[0].astype(BF16), w_out_a[0].astype(BF16), w_out_b[0].astype(BF16), w_out[0].astype(BF16),
              w_up[0].astype(BF16), w_down[0].astype(BF16), gate_shard, vec_shard]
    (z, n1_t, w_in_g), (gate_g, vec_g) = _in_proj(xs, norm_mix_g, shards[0], _slot_order(cx, cy, cc),
                                                comm=_gather_plan(shards[6:8]))
    gates = gate_g.reshape(N_SLOT, 2, HEADS, 32, HEAD_DIM).transpose(1, 2, 0, 3, 4).reshape(2, HEADS, HEAD_DIM, HEAD_DIM)
    w_r_f, w_i_f = gates[0], gates[1]
    conv_w_f = vec_g[:, 0:4, 0:128].transpose(1, 0, 2).reshape(CONV_K, D)
    b_r_f = vec_g[:, 0:4, 128:160].transpose(1, 0, 2).reshape(1, D)
    b_i_f = vec_g[:, 0:4, 160:192].transpose(1, 0, 2).reshape(1, D)
    b_s_t = jnp.transpose(sgu_b_s[0])

    (ya, hs, ya_t), (w_oa_g, w_ob_g, w_out_g) = _branch_a_fwd(
        z, conv_w_f, conv_b, w_r_f, b_r_f, w_i_f, b_i_f, lru_lambda, comm=_gather_plan(shards[1:4]))
    w_oa_f = w_oa_g.reshape(D, D)
    w_ob_f = w_ob_g.reshape(D, D)
    w_out_f = w_out_g.reshape(D, D)
    (yb, yb_t), (w_up_g,) = _branch_b_fwd(z, sgu_ln_g, sgu_ln_b, sgu_w_s[0], b_s_t, comm=_gather_plan(shards[4:5]))
    (pa, pb, merged_t, h1), (w_down_g,) = _merge_out(ya, yb, z, xs, w_oa_f, w_ob_f, w_out_f,
                                                   comm=_gather_plan(shards[5:6]))
    w_down_f = w_down_g.reshape(N_SLOT * FF_COLS, D)
    gf2 = norm_final_g.reshape(1, D)
    r_act, act_t, n2_t, dh2, loss_acc, d_gfin = _mlp_fwd(h1, norm_mlp_g, w_up_g, w_down_f, gf2, tgt)

    def pair(names, grads, recv):
        return [_pair_sum("pair_sum_" + nm, g, r, core) for nm, g, r in zip(names, grads, recv)]

    g_down = _wgrad_t("wgrad_down", act_t, dh2, N_SLOT // 2, True, False, 2048)
    g_down = g_down.reshape(N_SLOT, FF_COLS, D)
    (df, dh1, d_gmlp), (r_down,) = _mlp_bwd(dh2, r_act, w_down_f, w_up_g, h1, norm_mlp_g,
                                            comm=_sibling_plan([g_down]))
    (p_down,) = pair(["down"], [g_down], [r_down])
    g_up = _wgrad_t("wgrad_up", n2_t, df, N_SLOT, False, True, 4096)
    g_out = _wgrad_t("wgrad_out", merged_t, dh1, 1, False, False, 2048).reshape(N_SLOT, D // N_SLOT, D)
    (dz, dpa, dpb, dya, dyb), (got_down, r_up, r_out) = _merge_bwd(
        dh1, z, pa, pb, w_out_f, w_oa_f, w_ob_f, comm=_join(_chips_plan([p_down]), _sibling_plan([g_up, g_out])))
    p_up, p_out = pair(["up", "out"], [g_up, g_out], [r_up, r_out])
    g_oa = _wgrad_t("wgrad_out_a", ya_t, dpa, 1, False, False, 4096).reshape(N_SLOT, D // N_SLOT, D)
    g_ob = _wgrad_t("wgrad_out_b", yb_t, dpb, 1, False, False, 4096).reshape(N_SLOT, D // N_SLOT, D)
    (dz, d_ws, d_bs, d_ln), (got_up, r_oa, r_ob) = _branch_b_bwd(
        dz, dyb, z, sgu_ln_g, sgu_ln_b, sgu_w_s[0], b_s_t,
        comm=_join(_chips_plan([p_up]), _sibling_plan([g_oa, g_ob])))
    p_oa, p_ob = pair(["out_a", "out_b"], [g_oa, g_ob], [r_oa, r_ob])
    (dz, d_vec, d_wr, d_wi), (got_out, got_oa, got_ob) = _branch_a_bwd(
        dz, dya, z, hs, conv_w_f, conv_b, w_r_f, b_r_f, w_i_f, b_i_f, lru_lambda,
        comm=_chips_plan([p_out, p_oa, p_ob]))
    g_in = _wgrad_t("wgrad_in", n1_t, dz, N_SLOT, False, True, 4096)
    g_gate = jnp.stack([d_wr, d_wi]).reshape(2, HEADS, N_SLOT, 32, HEAD_DIM).transpose(2, 0, 1, 3, 4)
    g_gate = g_gate.reshape(N_SLOT, 2 * HEADS * 32, HEAD_DIM).astype(BF16)

    d_bs_row = jnp.pad(d_bs[:, :, 0].reshape(1, GROUPS * CHUNK), ((0, 0), (0, D - GROUPS * CHUNK)))
    vecs = jnp.concatenate([d_vec, jnp.concatenate([d_ln[0:2], d_gmlp, d_gfin, d_bs_row, jnp.zeros((3, D), F32)])])
    d_ws2 = d_ws.reshape(GROUPS * CHUNK, CHUNK)
    r_in, r_gate, r_vecs, r_ws = _run_plan("rs_sibling_in", _sibling_plan([g_in, g_gate], [vecs, d_ws2]))
    p_in, p_gate = pair(["in", "gate"], [g_in, g_gate], [r_in, r_gate])
    vecs_chip = _add2("pair_sum_vecs", vecs, r_vecs)
    ws_chip = _add2("pair_sum_ws", d_ws2, r_ws)
    (dx, d_gmix), (got_in, got_gate, got_vecs, got_ws) = _in_bwd(
        dz, w_in_g, xs, dh1, norm_mix_g, comm=_chips_plan([p_in, p_gate], [vecs_chip, ws_chip]))
    vecs_sum = _sum_terms("sum_vecs", got_vecs)
    last = jnp.concatenate([d_gmix, jnp.pad(loss_acc[0:1], ((0, 0), (0, D - 128))), jnp.zeros((6, D), F32)])
    (last_all,) = _run_plan("exchange_last", _exchange_plan(last))
    last_sum = _sum_terms("sum_last", last_all)
    loss = last_sum[1, 0]
    got = [got_in, got_oa, got_ob, got_out, got_up, got_down, got_gate]

    def step(nm, terms, w, m, v, rows, cols):
        g, d, mn, vn = _adamw("adamw_" + nm, terms.reshape(4, rows, cols), w.reshape(rows, cols),
                              m.reshape(rows, cols), v.reshape(rows, cols))
        return [a.reshape(w.shape) for a in (g, d, mn, vn)]

    o_in = step("in", got[0], w_in, m_w_in, v_w_in, D, W_IN_COLS)
    o_oa = step("out_a", got[1], w_out_a, m_w_out_a, v_w_out_a, D // N_SLOT, D)
    o_ob = step("out_b", got[2], w_out_b, m_w_out_b, v_w_out_b, D // N_SLOT, D)
    o_out = step("out", got[3], w_out, m_w_out, v_w_out, D // N_SLOT, D)
    o_up = step("up", got[4], w_up, m_w_up, v_w_up, D, FF_COLS)
    o_down = step("down", got[5], w_down, m_w_down, v_w_down, FF_COLS, D)
    gate_w = jnp.stack([w_rgate[0], w_igate[0]]).reshape(2 * HEADS * 32, HEAD_DIM)
    gate_m = jnp.stack([m_w_rgate[0], m_w_igate[0]]).reshape(2 * HEADS * 32, HEAD_DIM)
    gate_v = jnp.stack([v_w_rgate[0], v_w_igate[0]]).reshape(2 * HEADS * 32, HEAD_DIM)
    o_gate = _adamw("adamw_gate", got[6], gate_w, gate_m, gate_v)
    o_gate = [a.reshape(2, 1, HEADS, 32, HEAD_DIM) for a in o_gate]
    o_wr = [a[0] for a in o_gate]
    o_wi = [a[1] for a in o_gate]

    def own(full, width):
        return lax.dynamic_slice_in_dim(full, me * width, width, axis=1)

    small_g = {
        "norm_mix_g": last_sum[0:1], "conv_w": own(vecs_sum[0:4], 128), "conv_b": vecs_sum[4:5],
        "b_rgate": own(vecs_sum[5:6].reshape(HEADS, HEAD_DIM), 32),
        "b_igate": own(vecs_sum[6:7].reshape(HEADS, HEAD_DIM), 32),
        "lru_lambda": vecs_sum[7:8], "sgu_ln_g": vecs_sum[8:9], "sgu_ln_b": vecs_sum[9:10],
        "norm_mlp_g": vecs_sum[10:11], "norm_final_g": vecs_sum[11:12],
        "sgu_b_s": vecs_sum[12, 0:GROUPS * CHUNK].reshape(GROUPS, CHUNK),
    }
    small_w = {"norm_mix_g": (norm_mix_g, m_norm_mix_g, v_norm_mix_g), "conv_w": (conv_w, m_conv_w, v_conv_w),
               "conv_b": (conv_b, m_conv_b, v_conv_b), "b_rgate": (b_rgate, m_b_rgate, v_b_rgate),
               "b_igate": (b_igate, m_b_igate, v_b_igate), "lru_lambda": (lru_lambda, m_lru_lambda, v_lru_lambda),
               "sgu_ln_g": (sgu_ln_g, m_sgu_ln_g, v_sgu_ln_g), "sgu_ln_b": (sgu_ln_b, m_sgu_ln_b, v_sgu_ln_b),
               "norm_mlp_g": (norm_mlp_g, m_norm_mlp_g, v_norm_mlp_g),
               "norm_final_g": (norm_final_g, m_norm_final_g, v_norm_final_g),
               "sgu_b_s": (sgu_b_s, m_sgu_b_s, v_sgu_b_s), "sgu_w_s": (sgu_w_s, m_sgu_w_s, v_sgu_w_s)}
    order = list(small_g)
    as2d = lambda k, a: a.reshape(small_g[k].shape)
    upd = _adamw_many("adamw_small", [small_g[k] for k in order], *[[as2d(k, small_w[k][q]) for k in order]
                                                                     for q in range(3)])
    o_small = {k: [a.reshape(small_w[k][0].shape) for a in (small_g[k],) + u] for k, u in zip(order, upd)}
    ws3 = [a[0].reshape(GROUPS * CHUNK, CHUNK) for a in small_w.pop("sgu_w_s")]
    o_small["sgu_w_s"] = [a.reshape(sgu_w_s.shape) for a in _adamw("adamw_ws", got_ws, *ws3)]

    per_weight = {"norm_mix_g": o_small["norm_mix_g"], "w_in": o_in, "conv_w": o_small["conv_w"],
                  "conv_b": o_small["conv_b"], "w_rgate": o_wr, "b_rgate": o_small["b_rgate"], "w_igate": o_wi,
                  "b_igate": o_small["b_igate"], "lru_lambda": o_small["lru_lambda"], "w_out_a": o_oa,
                  "sgu_ln_g": o_small["sgu_ln_g"], "sgu_ln_b": o_small["sgu_ln_b"], "sgu_w_s": o_small["sgu_w_s"],
                  "sgu_b_s": o_small["sgu_b_s"], "w_out_b": o_ob, "w_out": o_out, "norm_mlp_g": o_small["norm_mlp_g"],
                  "w_up": o_up, "w_down": o_down, "norm_final_g": o_small["norm_final_g"]}
    names_w = list(per_weight)
    return (loss, dx[None], *[per_weight[k][0] for k in names_w], *[per_weight[k][1] for k in names_w],
            *[per_weight[k][2] for k in names_w], *[per_weight[k][3] for k in names_w])
```

```python
import jax
import jax.numpy as jnp
from jax import lax
from jax.experimental import pallas as pl
from jax.experimental.pallas import tpu as pltpu

F32 = jnp.float32
BF16 = jnp.bfloat16
SDS = jax.ShapeDtypeStruct
MESH = pl.DeviceIdType.MESH
ANY = pl.BlockSpec(memory_space=pltpu.HBM)

D = 1024
N_SLOT = 8
W_IN_COLS = 768
FF_COLS = 512
HEADS, HEAD_DIM = 4, 256
GROUPS, GROUP_DIM = 4, 256
CHUNK = 128
CONV_K = 4
NORM_EPS = 1e-6
LN_EPS = 1e-5
LRU_C = 8.0
ADAM_LR, ADAM_B1, ADAM_B2, ADAM_EPS, ADAM_WD, ADAM_STEP = 0.001, 0.9, 0.999, 1e-08, 0.01, 10

TM_ROWS = 1024
TM_MERGE = 512
T_BRANCH_A = 512
T_BRANCH_B = 256
MiB = 1024 * 1024
SMALL_OPERAND = 16 * 1024

_GELU_C = 0.7978845608028654
_GELU_A = 0.044715


def _small_in_hbm(a):
    return pltpu.with_memory_space_constraint(a, pltpu.HBM) if a.size <= SMALL_OPERAND else a


def _cparams(sem, vmem_mib):
    return pltpu.CompilerParams(dimension_semantics=sem, vmem_limit_bytes=vmem_mib * MiB)


def _gelu(x):
    t = jnp.tanh(_GELU_C * (x + _GELU_A * x * x * x))
    return 0.5 * x * (1.0 + t)


def _gelu_and_grad(x):
    x2 = x * x
    t = jnp.tanh(_GELU_C * x * (1.0 + _GELU_A * x2))
    g = 0.5 * x * (1.0 + t)
    dg = 0.5 * (1.0 + t) + 0.5 * x * (1.0 - t * t) * _GELU_C * (1.0 + 3.0 * _GELU_A * x2)
    return g, dg


def _softplus(x):
    return jnp.maximum(x, 0.0) + jnp.log1p(jnp.exp(-jnp.abs(x)))


def _dot(a, b):
    return jnp.dot(a, b, preferred_element_type=F32)


def _dot_nt(a, b):
    return lax.dot_general(a, b, (((1,), (1,)), ((), ())), preferred_element_type=F32)


def _dot_tn(a, b):
    return lax.dot_general(a, b, (((0,), (0,)), ((), ())), preferred_element_type=F32)


def _rows_shifted(prev8, cur, k):
    ext = jnp.concatenate([prev8, cur], axis=0)
    return pltpu.roll(ext, k, 0)[8:]


def _rows_advanced(cur, next8, k):
    t = cur.shape[0]
    ext = jnp.concatenate([cur, next8], axis=0)
    return pltpu.roll(ext, t + 8 - k, 0)[:t]


def _first_second(x, y, c):
    ny, nx, far = _other_chips(x, y)
    pick = lambda a, b: a * (1 - c) + b * c
    first = tuple(pick(a, b) for a, b in zip(ny, nx))
    second = tuple(pick(b, a) for a, b in zip(ny, nx))
    return first, second, far


def _slot_order(x, y, c):
    chip = 2 * x + y
    first, second, far = _first_second(x, y, c)
    order = [2 * chip + c, 2 * chip + 1 - c, 2 * first[2] + c, 2 * second[2] + 1 - c, 2 * second[2] + c,
             2 * first[2] + 1 - c, 2 * far[2] + c, 2 * far[2] + 1 - c]
    return jnp.stack(order).astype(jnp.int32)


def _in_proj(x, g_mix, w_in_own, order, comm=None):
    s = x.shape[0]
    tm = min(TM_ROWS, s)
    ni = s // tm

    def body(order_ref, x_ref, g_ref, own_ref, z_ref, nt_ref, wg_ref, n_s, w_s, send_sems, recv_sems, local_sems):
        j, i = pl.program_id(0), pl.program_id(1)
        px, py, c = _place()
        chip = 2 * px + py
        me = 2 * chip + c
        sib = (px, py, 1 - c)
        chips = _other_chips(px, py)

        def rc(k, src, blk, to):
            return pltpu.make_async_remote_copy(src_ref=src, dst_ref=w_s.at[blk], send_sem=send_sems.at[k],
                                                recv_sem=recv_sems.at[k], device_id=to, device_id_type=MESH)

        del chips
        first, second, far = _first_second(px, py, c)
        blocks = [2 * first[2] + c, 2 * second[2] + c, 2 * far[2] + c]
        own_in = pltpu.make_async_copy(own_ref, w_s.at[me], local_sems.at[0])
        to_first = rc(1, own_ref, me, (first[0], first[1], c))
        to_second = rc(2, own_ref, me, (second[0], second[1], c))
        relay = rc(3, w_s.at[blocks[0]], blocks[0], (second[0], second[1], c))
        sends = [rc(0, own_ref, me, sib), to_first, to_second, relay]
        passed = [rc(4 + q, w_s.at[blk], blk, sib) for q, blk in enumerate(blocks)]
        keep = pltpu.make_async_copy(w_s, wg_ref, local_sems.at[1])

        @pl.when((i == 0) & (j == 0))
        def _():
            own_in.start()
            sends[0].start()
            to_first.start()
            own_in.wait()

        @pl.when((i == 0) & (j == 1))
        def _():
            rc(0, own_ref, 2 * chip + 1 - c, sib).wait_recv()

        for q, blk in enumerate(blocks):
            @pl.when((i == 0) & (j == 2 + 2 * q))
            def _():
                rc(1 + q, own_ref, blk, sib).wait_recv()
                passed[q].start()
                if q == 0:
                    to_second.start()
                    relay.start()

            @pl.when((i == 0) & (j == 3 + 2 * q))
            def _():
                rc(4 + q, own_ref, order_ref[j], sib).wait_recv()

        rows = pl.ds(pl.multiple_of(i * tm, tm), tm)

        @pl.when(j == 0)
        def _():
            xv = x_ref[...]
            rstd = lax.rsqrt(jnp.mean(xv * xv, axis=-1, keepdims=True) + NORM_EPS)
            nb = (xv * rstd * g_ref[...]).astype(BF16)
            n_s[rows, :] = nb
            nt_ref[...] = nb.T

        z_ref[...] = _dot(n_s[rows, :], w_s[order_ref[j]]).astype(BF16)

        @pl.when((i == 0) & (j == N_SLOT - 1))
        def _():
            keep.start()

        @pl.when((i == ni - 1) & (j == N_SLOT - 1))
        def _():
            for cp in sends + passed:
                cp.wait_send()
            keep.wait()

    first_pass = lambda j, i, o: (jnp.where(j == 0, i, ni - 1), 0)
    (z, n1, w_in_g), extra = _call(
        body, name="in_proj", grid=(N_SLOT, ni), prefetch=(order,),
        in_specs=[pl.BlockSpec((tm, D), first_pass),
                  pl.BlockSpec((1, D), lambda j, i, o: (0, 0)), ANY],
        out_specs=[pl.BlockSpec((tm, W_IN_COLS), lambda j, i, o: (i, o[j])),
                   pl.BlockSpec((D, tm), lambda j, i, o: (0, jnp.where(j == 0, i, ni - 1))), ANY],
        out_shape=[SDS((s, N_SLOT * W_IN_COLS), BF16), SDS((D, s), BF16), SDS((N_SLOT, D, W_IN_COLS), BF16)],
        scratch_shapes=[pltpu.VMEM((s, D), BF16), pltpu.VMEM((N_SLOT, D, W_IN_COLS), BF16),
                        pltpu.SemaphoreType.DMA((7,)), pltpu.SemaphoreType.DMA((7,)), pltpu.SemaphoreType.DMA((2,))],
        params=_cparams(("arbitrary", "arbitrary"), 56), args=(x, g_mix, w_in_own), comm=comm)
    return (z, n1, w_in_g), extra


def _lru_gates(xc, xcb, wr_ref, br, wi_ref, bi, sp_lam, a_s, b_s, r_s=None, i_s=None, m_s=None):
    for h in range(HEADS):
        sl = slice(h * HEAD_DIM, (h + 1) * HEAD_DIM)
        r = jax.nn.sigmoid(_dot(xcb[:, sl], wr_ref[h]) + br[:, sl])
        ig = jax.nn.sigmoid(_dot(xcb[:, sl], wi_ref[h]) + bi[:, sl])
        log_a = (-LRU_C) * r * sp_lam[:, sl]
        a = jnp.exp(log_a)
        mult = jnp.sqrt(-jnp.tanh(log_a) * (a * a + 1.0))
        a_s[:, sl] = a
        b_s[:, sl] = xc[:, sl] * ig * mult
        if r_s is not None:
            r_s[:, sl] = r
            i_s[:, sl] = ig
            m_s[:, sl] = mult


def _conv_fwd(xa, prev8, cw, cb):
    xc = cb + cw[0:1, :] * xa
    for k in range(1, CONV_K):
        xc = xc + cw[k:k + 1, :] * _rows_shifted(prev8, xa, k)
    return xc


def _branch_a_fwd(z, conv_w, conv_b, w_r, b_r, w_i, b_i, lam, comm=None):
    s = z.shape[0]
    ta = min(T_BRANCH_A, s)
    per16 = ta // 16

    def body(xa_ref, xp_ref, ga_ref, cw_ref, cb_ref, wr_ref, br_ref, wi_ref, bi_ref, lam_ref,
             ya_ref, hs_ref, a_s, b_s, h_s, carry_s):
        i = pl.program_id(0)

        @pl.when(i == 0)
        def _():
            carry_s[...] = jnp.zeros_like(carry_s)

        xa = xa_ref[...].astype(F32)
        prev8 = jnp.where(i > 0, xp_ref[...].astype(F32)[8:16], 0.0)
        xc = _conv_fwd(xa, prev8, cw_ref[...], cb_ref[...])
        sp_lam = _softplus(-lam_ref[...])
        _lru_gates(xc, xc.astype(BF16), wr_ref, br_ref[...], wi_ref, bi_ref[...], sp_lam, a_s, b_s)

        row = lax.broadcasted_iota(jnp.int32, (8, D), 0)

        def group(g, carry):
            off = pl.multiple_of(g * 8, 8)
            a8 = a_s[pl.ds(off, 8), :]
            b8 = b_s[pl.ds(off, 8), :]
            for d in (1, 2, 4):
                a_sh = jnp.where(row >= d, pltpu.roll(a8, d, 0), 1.0)
                b_sh = jnp.where(row >= d, pltpu.roll(b8, d, 0), 0.0)
                b8 = a8 * b_sh + b8
                a8 = a8 * a_sh
            h8 = b8 + a8 * carry
            h_s[pl.ds(off, 8), :] = h8
            return jnp.broadcast_to(h8[7:8, :], (8, D))

        carry_s[...] = lax.fori_loop(0, ta // 8, group, carry_s[...])
        hs = h_s[...]
        hs_ref[...] = hs.astype(BF16)
        ya_ref[...] = (hs * _gelu(ga_ref[...].astype(F32))).astype(BF16)

    vec = pl.BlockSpec((1, D), lambda i: (0, 0))
    gate = pl.BlockSpec((HEADS, HEAD_DIM, HEAD_DIM), lambda i: (0, 0, 0))
    return _call(
        body, name="branch_a_fwd", grid=(s // ta,),
        in_specs=[pl.BlockSpec((ta, D), lambda i: (i, 0)),
                  pl.BlockSpec((16, D), lambda i: (jnp.maximum(i * per16 - 1, 0), 0)),
                  pl.BlockSpec((ta, D), lambda i: (i, 1)),
                  pl.BlockSpec((CONV_K, D), lambda i: (0, 0)), vec, gate, vec, gate, vec, vec],
        out_specs=[pl.BlockSpec((ta, D), lambda i: (i, 0)), pl.BlockSpec((ta, D), lambda i: (i, 0))],
        out_shape=[SDS((s, D), BF16), SDS((s, D), BF16)],
        scratch_shapes=[pltpu.VMEM((ta, D), F32), pltpu.VMEM((ta, D), F32), pltpu.VMEM((ta, D), F32),
                        pltpu.VMEM((8, D), F32)],
        params=_cparams(("arbitrary",), 40), args=(z, z, z, conv_w, conv_b, w_r, b_r, w_i, b_i, lam), comm=comm)


def _sgu_common(ub, vb, lg, lb, with_grad):
    if with_grad:
        u, du = _gelu_and_grad(ub)
        v, dv = _gelu_and_grad(vb)
    else:
        u, v, du, dv = _gelu(ub), _gelu(vb), None, None
    mu = jnp.mean(v, axis=-1, keepdims=True)
    vc = v - mu
    rstd = lax.rsqrt(jnp.mean(vc * vc, axis=-1, keepdims=True) + LN_EPS)
    vhat = vc * rstd
    vln = vhat * lg + lb
    return u, du, dv, rstd, vhat, vln


def _masked_ws(ws_ref):
    t = lax.broadcasted_iota(jnp.int32, (CHUNK, CHUNK), 0)
    c = lax.broadcasted_iota(jnp.int32, (CHUNK, CHUNK), 1)
    keep = c <= t
    return [jnp.where(keep, ws_ref[g], 0.0).astype(BF16) for g in range(GROUPS)]


def _branch_b_fwd(z, ln_g, ln_b, w_s, b_s_t, comm=None):
    s = z.shape[0]
    tb = min(T_BRANCH_B, s)

    def body(ub_ref, vb_ref, lg_ref, lb_ref, ws_ref, bs_ref, yb_ref):
        u, _, _, _, _, vln = _sgu_common(ub_ref[...].astype(F32), vb_ref[...].astype(F32),
                                         lg_ref[...], lb_ref[...], False)
        vlnb = vln.astype(BF16)
        wm = _masked_ws(ws_ref)
        bs = bs_ref[...]
        for c in range(tb // CHUNK):
            rs = slice(c * CHUNK, (c + 1) * CHUNK)
            for g in range(GROUPS):
                cs = slice(g * GROUP_DIM, (g + 1) * GROUP_DIM)
                sp = _dot(wm[g], vlnb[rs, cs]) + bs[:, g:g + 1]
                yb_ref[rs, cs] = (u[rs, cs] * sp).astype(BF16)

    vec = pl.BlockSpec((1, D), lambda i: (0, 0))
    return _call(
        body, name="branch_b_fwd", grid=(s // tb,),
        in_specs=[pl.BlockSpec((tb, D), lambda i: (i, 2)), pl.BlockSpec((tb, D), lambda i: (i, 3)), vec, vec,
                  pl.BlockSpec((GROUPS, CHUNK, CHUNK), lambda i: (0, 0, 0)),
                  pl.BlockSpec((CHUNK, GROUPS), lambda i: (0, 0))],
        out_specs=[pl.BlockSpec((tb, D), lambda i: (i, 0))],
        out_shape=[SDS((s, D), BF16)], scratch_shapes=[],
        params=_cparams(("arbitrary",), 40), args=(z, z, ln_g, ln_b, w_s, b_s_t), comm=comm)


def _merge_out(ya, yb, z, x, w_oa, w_ob, w_out, comm=None):
    s = x.shape[0]
    tm = min(TM_MERGE, s)

    def body(ya_ref, yb_ref, ma_ref, mb_ref, x_ref, woa_ref, wob_ref, wo_ref, pa_ref, pb_ref, mg_ref, h1_ref):
        pa = _dot(ya_ref[...], woa_ref[...])
        pb = _dot(yb_ref[...], wob_ref[...])
        merged = (jax.nn.sigmoid(ma_ref[...].astype(F32)) * pa
                  + jax.nn.sigmoid(mb_ref[...].astype(F32)) * pb).astype(BF16)
        pa_ref[...] = pa.astype(BF16)
        pb_ref[...] = pb.astype(BF16)
        mg_ref[...] = merged
        h1_ref[...] = x_ref[...] + _dot(merged, wo_ref[...])

    row = pl.BlockSpec((tm, D), lambda i: (i, 0))
    wsp = pl.BlockSpec((D, D), lambda i: (0, 0))
    return _call(
        body, name="merge_out", grid=(s // tm,),
        in_specs=[row, row, pl.BlockSpec((tm, D), lambda i: (i, 4)), pl.BlockSpec((tm, D), lambda i: (i, 5)),
                  row, wsp, wsp, wsp],
        out_specs=[row, row, row, row],
        out_shape=[SDS((s, D), BF16), SDS((s, D), BF16), SDS((s, D), BF16), SDS((s, D), F32)], scratch_shapes=[],
        params=_cparams(("arbitrary",), 48), args=(ya, yb, z, z, x, w_oa, w_ob, w_out), comm=comm)


def _mlp_fwd(h1, g_mlp, w_up_g, w_down, g_fin, tgt):
    s = h1.shape[0]
    tm = min(TM_ROWS, s)
    nj = N_SLOT

    def body(h1_ref, gm_ref, wu_ref, wd_ref, gf_ref, t_ref, r_ref, at_ref, n2t_ref, dh2_ref, loss_ref, dgf_ref,
             n2_s, acc_s):
        i, j = pl.program_id(0), pl.program_id(1)

        @pl.when(j == 0)
        def _():
            hv = h1_ref[...]
            rstd = lax.rsqrt(jnp.mean(hv * hv, axis=-1, keepdims=True) + NORM_EPS)
            nb = (hv * rstd * gm_ref[...]).astype(BF16)
            n2_s[...] = nb
            n2t_ref[...] = nb.T
            acc_s[...] = jnp.zeros_like(acc_s)

        @pl.when((i == 0) & (j == 0))
        def _():
            loss_ref[...] = jnp.zeros_like(loss_ref)
            dgf_ref[...] = jnp.zeros_like(dgf_ref)

        r = jnp.maximum(_dot(n2_s[...], wu_ref[...]), 0.0)
        r_ref[...] = r.astype(BF16)
        act = (r * r).astype(BF16)
        at_ref[...] = act.T
        acc_s[...] += _dot(act, wd_ref[...])

        @pl.when(j == nj - 1)
        def _():
            h2 = h1_ref[...] + acc_s[...]
            rstd = lax.rsqrt(jnp.mean(h2 * h2, axis=-1, keepdims=True) + NORM_EPS)
            hh = h2 * rstd
            gf = gf_ref[...]
            e = hh * gf - t_ref[...]
            loss_ref[...] += jnp.sum(e * e) * (0.5 / D)
            dy = e * (1.0 / D)
            dgf_ref[...] += jnp.sum(dy * hh, axis=0, keepdims=True)
            dhh = dy * gf
            dh2_ref[...] = rstd * (dhh - hh * jnp.mean(dhh * hh, axis=-1, keepdims=True))

    row = pl.BlockSpec((tm, D), lambda i, j: (i, 0))
    vec = pl.BlockSpec((1, D), lambda i, j: (0, 0))
    return pl.pallas_call(
        body, name="mlp_fwd", grid=(s // tm, nj),
        in_specs=[row, vec, pl.BlockSpec((None, D, FF_COLS), lambda i, j: (j, 0, 0)),
                  pl.BlockSpec((FF_COLS, D), lambda i, j: (j, 0)), vec, row],
        out_specs=[pl.BlockSpec((tm, FF_COLS), lambda i, j: (i, j)), pl.BlockSpec((FF_COLS, tm), lambda i, j: (j, i)),
                   pl.BlockSpec((D, tm), lambda i, j: (0, i)), row, pl.BlockSpec((8, 128), lambda i, j: (0, 0)), vec],
        out_shape=[SDS((s, nj * FF_COLS), BF16), SDS((nj * FF_COLS, s), BF16), SDS((D, s), BF16), SDS((s, D), F32),
                   SDS((8, 128), F32), SDS((1, D), F32)],
        scratch_shapes=[pltpu.VMEM((tm, D), BF16), pltpu.VMEM((tm, D), F32)],
        compiler_params=_cparams(("arbitrary", "arbitrary"), 52),
    )(h1, _small_in_hbm(g_mlp), w_up_g, w_down, _small_in_hbm(g_fin), tgt)


def _mlp_bwd(dh2, r, w_down, w_up_g, h1, g_mlp, comm=None):
    s = h1.shape[0]
    tm = min(TM_ROWS, s)
    nj = N_SLOT

    def body(dh2_ref, r_ref, wd_ref, wu_ref, h1_ref, gm_ref, df_ref, dh1_ref, dgm_ref, dh2b_s, acc_s):
        i, j = pl.program_id(0), pl.program_id(1)

        @pl.when(j == 0)
        def _():
            dh2b_s[...] = dh2_ref[...].astype(BF16)
            acc_s[...] = jnp.zeros_like(acc_s)

        @pl.when((i == 0) & (j == 0))
        def _():
            dgm_ref[...] = jnp.zeros_like(dgm_ref)

        d_act = _dot_nt(dh2b_s[...], wd_ref[...])
        df = (d_act * (2.0 * r_ref[...].astype(F32))).astype(BF16)
        df_ref[...] = df
        acc_s[...] += _dot_nt(df, wu_ref[...])

        @pl.when(j == nj - 1)
        def _():
            hv = h1_ref[...]
            rstd = lax.rsqrt(jnp.mean(hv * hv, axis=-1, keepdims=True) + NORM_EPS)
            hh = hv * rstd
            dn2 = acc_s[...]
            dgm_ref[...] += jnp.sum(dn2 * hh, axis=0, keepdims=True)
            dhat = dn2 * gm_ref[...]
            dh1_ref[...] = dh2_ref[...] + rstd * (dhat - hh * jnp.mean(dhat * hh, axis=-1, keepdims=True))

    row = pl.BlockSpec((tm, D), lambda i, j: (i, 0))
    vec = pl.BlockSpec((1, D), lambda i, j: (0, 0))
    ffb = pl.BlockSpec((tm, FF_COLS), lambda i, j: (i, j))
    return _call(
        body, name="mlp_bwd", grid=(s // tm, nj),
        in_specs=[row, ffb, pl.BlockSpec((FF_COLS, D), lambda i, j: (j, 0)),
                  pl.BlockSpec((None, D, FF_COLS), lambda i, j: (j, 0, 0)), row, vec],
        out_specs=[ffb, row, vec],
        out_shape=[SDS((s, nj * FF_COLS), BF16), SDS((s, D), F32), SDS((1, D), F32)],
        scratch_shapes=[pltpu.VMEM((tm, D), BF16), pltpu.VMEM((tm, D), F32)],
        params=_cparams(("arbitrary", "arbitrary"), 52), args=(dh2, r, w_down, w_up_g, h1, g_mlp), comm=comm)


def _merge_bwd(dh1, z, pa, pb, w_out, w_oa, w_ob, comm=None):
    s = dh1.shape[0]
    tm = min(TM_MERGE, s)

    def body(dh1_ref, ma_ref, mb_ref, pa_ref, pb_ref, wo_ref, woa_ref, wob_ref,
             dz_ref, dpa_ref, dpb_ref, dya_ref, dyb_ref):
        dm = _dot_nt(dh1_ref[...].astype(BF16), wo_ref[...])
        sa = jax.nn.sigmoid(ma_ref[...].astype(F32))
        sb = jax.nn.sigmoid(mb_ref[...].astype(F32))
        dpa = (dm * sa).astype(BF16)
        dpb = (dm * sb).astype(BF16)
        dz_ref[:, 0:D] = (dm * pa_ref[...].astype(F32) * sa * (1.0 - sa)).astype(BF16)
        dz_ref[:, D:2 * D] = (dm * pb_ref[...].astype(F32) * sb * (1.0 - sb)).astype(BF16)
        dpa_ref[...] = dpa
        dpb_ref[...] = dpb
        dya_ref[...] = _dot_nt(dpa, woa_ref[...]).astype(BF16)
        dyb_ref[...] = _dot_nt(dpb, wob_ref[...]).astype(BF16)

    row = pl.BlockSpec((tm, D), lambda i: (i, 0))
    wsp = pl.BlockSpec((D, D), lambda i: (0, 0))
    return _call(
        body, name="merge_bwd", grid=(s // tm,),
        in_specs=[row, pl.BlockSpec((tm, D), lambda i: (i, 4)), pl.BlockSpec((tm, D), lambda i: (i, 5)),
                  row, row, wsp, wsp, wsp],
        out_specs=[pl.BlockSpec((tm, 2 * D), lambda i: (i, 2)), row, row, row, row],
        out_shape=[SDS((s, 6 * D), BF16)] + [SDS((s, D), BF16)] * 4, scratch_shapes=[],
        params=_cparams(("arbitrary",), 48), args=(dh1, z, z, pa, pb, w_out, w_oa, w_ob), comm=comm)


def _branch_b_bwd(dz, dyb, z, ln_g, ln_b, w_s, b_s_t, comm=None):
    s = z.shape[0]
    tb = min(T_BRANCH_B, s)

    def body(dz_in, dyb_ref, ub_ref, vb_ref, lg_ref, lb_ref, ws_ref, bs_ref,
             dz_ref, dws_ref, dbs_ref, dln_ref, du_s, dvln_s):
        del dz_in

        @pl.when(pl.program_id(0) == 0)
        def _():
            dws_ref[...] = jnp.zeros_like(dws_ref)
            dbs_ref[...] = jnp.zeros_like(dbs_ref)
            dln_ref[...] = jnp.zeros_like(dln_ref)

        lg = lg_ref[...]
        u, du, dv, rstd, vhat, vln = _sgu_common(ub_ref[...].astype(F32), vb_ref[...].astype(F32),
                                                 lg, lb_ref[...], True)
        vlnb = vln.astype(BF16)
        dyb_v = dyb_ref[...].astype(F32)
        wm = _masked_ws(ws_ref)
        keep = (lax.broadcasted_iota(jnp.int32, (CHUNK, CHUNK), 1)
                <= lax.broadcasted_iota(jnp.int32, (CHUNK, CHUNK), 0))
        bs = bs_ref[...]
        for c in range(tb // CHUNK):
            rs = slice(c * CHUNK, (c + 1) * CHUNK)
            for g in range(GROUPS):
                cs = slice(g * GROUP_DIM, (g + 1) * GROUP_DIM)
                v_blk = vlnb[rs, cs]
                sp = _dot(wm[g], v_blk) + bs[:, g:g + 1]
                d_sp = dyb_v[rs, cs] * u[rs, cs]
                d_spb = d_sp.astype(BF16)
                du_s[rs, cs] = dyb_v[rs, cs] * sp
                dvln_s[rs, cs] = _dot_tn(wm[g], d_spb)
                dws_ref[g] += jnp.where(keep, _dot_nt(d_spb, v_blk), 0.0)
                dbs_ref[g] += jnp.broadcast_to(jnp.sum(d_sp, axis=-1, keepdims=True), (CHUNK, CHUNK))
        dvln = dvln_s[...]
        dln_ref[0:1, :] += jnp.sum(dvln * vhat, axis=0, keepdims=True)
        dln_ref[1:2, :] += jnp.sum(dvln, axis=0, keepdims=True)
        dvh = dvln * lg
        d_v = rstd * (dvh - jnp.mean(dvh, axis=-1, keepdims=True)
                      - vhat * jnp.mean(dvh * vhat, axis=-1, keepdims=True))
        dz_ref[:, 0:D] = (du_s[...] * du).astype(BF16)
        dz_ref[:, D:2 * D] = (d_v * dv).astype(BF16)

    vec = pl.BlockSpec((1, D), lambda i: (0, 0))
    sq = pl.BlockSpec((GROUPS, CHUNK, CHUNK), lambda i: (0, 0, 0))
    return _call(
        body, name="branch_b_bwd", grid=(s // tb,),
        in_specs=[ANY, pl.BlockSpec((tb, D), lambda i: (i, 0)),
                  pl.BlockSpec((tb, D), lambda i: (i, 2)), pl.BlockSpec((tb, D), lambda i: (i, 3)), vec, vec, sq,
                  pl.BlockSpec((CHUNK, GROUPS), lambda i: (0, 0))],
        out_specs=[pl.BlockSpec((tb, 2 * D), lambda i: (i, 1)), sq, sq, pl.BlockSpec((8, D), lambda i: (0, 0))],
        out_shape=[SDS(dz.shape, BF16), SDS((GROUPS, CHUNK, CHUNK), F32), SDS((GROUPS, CHUNK, CHUNK), F32),
                   SDS((8, D), F32)],
        scratch_shapes=[pltpu.VMEM((tb, D), F32), pltpu.VMEM((tb, D), F32)], aliases={0: 0},
        params=_cparams(("arbitrary",), 40), args=(dz, dyb, z, z, ln_g, ln_b, w_s, b_s_t), comm=comm)


def _branch_a_bwd(dz, dya, z, hs, conv_w, conv_b, w_r, b_r, w_i, b_i, lam, comm=None):
    s = z.shape[0]
    ta = min(T_BRANCH_A, s)
    nb = s // ta
    per16 = ta // 16

    def body(dz_in, dya_ref, xa_ref, xp_ref, ga_ref, hs_ref, hp_ref, cw_ref, cb_ref, wr_ref, br_ref, wi_ref,
             bi_ref, lam_ref, dz_ref, vec_ref, dwr_ref, dwi_ref,
             a_s, b_s, h_s, r_s, i_s, m_s, dcar_s, acar_s, dxc_s):
        del dz_in
        i = pl.program_id(0)
        blk = nb - 1 - i

        @pl.when(i == 0)
        def _():
            dcar_s[...] = jnp.zeros_like(dcar_s)
            acar_s[...] = jnp.zeros_like(acar_s)
            dxc_s[...] = jnp.zeros_like(dxc_s)
            vec_ref[...] = jnp.zeros_like(vec_ref)
            dwr_ref[...] = jnp.zeros_like(dwr_ref)
            dwi_ref[...] = jnp.zeros_like(dwi_ref)

        cw = cw_ref[...]
        lam_v = lam_ref[...]
        xa = xa_ref[...].astype(F32)
        prev8 = jnp.where(blk > 0, xp_ref[...].astype(F32)[8:16], 0.0)
        xc = _conv_fwd(xa, prev8, cw, cb_ref[...])
        xcb = xc.astype(BF16)
        sp_lam = _softplus(-lam_v)
        _lru_gates(xc, xcb, wr_ref, br_ref[...], wi_ref, bi_ref[...], sp_lam, a_s, b_s, r_s, i_s, m_s)

        hs_v = hs_ref[...].astype(F32)
        hprev8 = jnp.where(blk > 0, hp_ref[...].astype(F32)[8:16], 0.0)
        h_m1 = _rows_shifted(hprev8, hs_v, 1)
        gg, dgg = _gelu_and_grad(ga_ref[...].astype(F32))
        dya_v = dya_ref[...].astype(F32)
        dz_ref[:, D:2 * D] = (dya_v * hs_v * dgg).astype(BF16)

        a_v = a_s[...]
        a_s[...] = _rows_advanced(a_v, acar_s[...], 1)
        b_s[...] = dya_v * gg

        row = lax.broadcasted_iota(jnp.int32, (8, D), 0)
        ng = ta // 8

        def group(gi, carry):
            off = pl.multiple_of((ng - 1 - gi) * 8, 8)
            c8 = a_s[pl.ds(off, 8), :]
            d8 = b_s[pl.ds(off, 8), :]
            for d in (1, 2, 4):
                c_sh = jnp.where(row < 8 - d, pltpu.roll(c8, 8 - d, 0), 1.0)
                d_sh = jnp.where(row < 8 - d, pltpu.roll(d8, 8 - d, 0), 0.0)
                d8 = c8 * d_sh + d8
                c8 = c8 * c_sh
            dh8 = d8 + c8 * carry
            h_s[pl.ds(off, 8), :] = dh8
            return jnp.broadcast_to(dh8[0:1, :], (8, D))

        dcar_s[...] = lax.fori_loop(0, ng, group, dcar_s[...])
        acar_s[...] = jnp.broadcast_to(a_v[0:1, :], (8, D))

        dbx = h_s[...]
        r_v, i_v, m_v = r_s[...], i_s[...], m_s[...]
        d_mult = dbx * xc * i_v
        d_loga = dbx * h_m1 * a_v - d_mult * (a_v * a_v) / m_v
        d_pr = d_loga * ((-LRU_C) * sp_lam) * r_v * (1.0 - r_v)
        d_pi = dbx * xc * m_v * i_v * (1.0 - i_v)
        vec_ref[7:8, :] += jnp.sum(d_loga * r_v, axis=0, keepdims=True) * (LRU_C * jax.nn.sigmoid(-lam_v))
        vec_ref[5:6, :] += jnp.sum(d_pr, axis=0, keepdims=True)
        vec_ref[6:7, :] += jnp.sum(d_pi, axis=0, keepdims=True)
        d_prb = d_pr.astype(BF16)
        d_pib = d_pi.astype(BF16)
        h_s[...] = dbx * i_v * m_v
        for h in range(HEADS):
            sl = slice(h * HEAD_DIM, (h + 1) * HEAD_DIM)
            h_s[:, sl] += _dot_nt(d_prb[:, sl], wr_ref[h]) + _dot_nt(d_pib[:, sl], wi_ref[h])
            dwr_ref[h] += _dot_tn(xcb[:, sl], d_prb[:, sl])
            dwi_ref[h] += _dot_tn(xcb[:, sl], d_pib[:, sl])
        d_xc = h_s[...]
        vec_ref[4:5, :] += jnp.sum(d_xc, axis=0, keepdims=True)
        vec_ref[0:1, :] += jnp.sum(d_xc * xa, axis=0, keepdims=True)
        d_xa = cw[0:1, :] * d_xc
        nxt = dxc_s[...]
        for k in range(1, CONV_K):
            vec_ref[k:k + 1, :] += jnp.sum(d_xc * _rows_shifted(prev8, xa, k), axis=0, keepdims=True)
            d_xa = d_xa + cw[k:k + 1, :] * _rows_advanced(d_xc, nxt, k)
        dz_ref[:, 0:D] = d_xa.astype(BF16)
        dxc_s[...] = d_xc[0:8, :]

    vec = pl.BlockSpec((1, D), lambda i: (0, 0))
    gate = pl.BlockSpec((HEADS, HEAD_DIM, HEAD_DIM), lambda i: (0, 0, 0))
    cur = lambda c: pl.BlockSpec((ta, D), lambda i: (nb - 1 - i, c))
    before = lambda c: pl.BlockSpec((16, D), lambda i: (jnp.maximum((nb - 1 - i) * per16 - 1, 0), c))
    return _call(
        body, name="branch_a_bwd", grid=(nb,),
        in_specs=[ANY, cur(0), cur(0), before(0), cur(1), cur(0), before(0),
                  pl.BlockSpec((CONV_K, D), lambda i: (0, 0)), vec, gate, vec, gate, vec, vec],
        out_specs=[pl.BlockSpec((ta, 2 * D), lambda i: (nb - 1 - i, 0)), pl.BlockSpec((8, D), lambda i: (0, 0)),
                   gate, gate],
        out_shape=[SDS(dz.shape, BF16), SDS((8, D), F32), SDS((HEADS, HEAD_DIM, HEAD_DIM), F32),
                   SDS((HEADS, HEAD_DIM, HEAD_DIM), F32)],
        scratch_shapes=[pltpu.VMEM((ta, D), F32)] * 6 + [pltpu.VMEM((8, D), F32)] * 3, aliases={0: 0},
        params=_cparams(("arbitrary",), 48),
        args=(dz, dya, z, z, z, hs, hs, conv_w, conv_b, w_r, b_r, w_i, b_i, lam), comm=comm)


def _in_bwd(dz, w_in_g, x, dh1, g_mix, comm=None):
    s = x.shape[0]
    tm = min(TM_ROWS, s)
    nj = N_SLOT

    def body(dz_ref, w_ref, x_ref, dh1_ref, g_ref, dx_ref, dg_ref, acc_s):
        i, j = pl.program_id(0), pl.program_id(1)

        @pl.when(j == 0)
        def _():
            acc_s[...] = jnp.zeros_like(acc_s)

        @pl.when((i == 0) & (j == 0))
        def _():
            dg_ref[...] = jnp.zeros_like(dg_ref)

        acc_s[...] += _dot_nt(dz_ref[...], w_ref[...])

        @pl.when(j == nj - 1)
        def _():
            xv = x_ref[...]
            rstd = lax.rsqrt(jnp.mean(xv * xv, axis=-1, keepdims=True) + NORM_EPS)
            xh = xv * rstd
            dn = acc_s[...]
            dg_ref[...] += jnp.sum(dn * xh, axis=0, keepdims=True)
            dhat = dn * g_ref[...]
            dx_ref[...] = dh1_ref[...] + rstd * (dhat - xh * jnp.mean(dhat * xh, axis=-1, keepdims=True))

    row = pl.BlockSpec((tm, D), lambda i, j: (i, 0))
    vec = pl.BlockSpec((1, D), lambda i, j: (0, 0))
    return _call(
        body, name="in_bwd", grid=(s // tm, nj),
        in_specs=[pl.BlockSpec((tm, W_IN_COLS), lambda i, j: (i, j)),
                  pl.BlockSpec((None, D, W_IN_COLS), lambda i, j: (j, 0, 0)), row, row, vec],
        out_specs=[row, vec],
        out_shape=[SDS((s, D), F32), SDS((1, D), F32)],
        scratch_shapes=[pltpu.VMEM((tm, D), F32)],
        params=_cparams(("arbitrary", "arbitrary"), 48), args=(dz, w_in_g, x, dh1, g_mix), comm=comm)


def _wgrad(name, a, b, nblk, a_split, b_split):
    s = a.shape[0]
    ts = min(TM_ROWS, s)
    a_w = a.shape[1] // nblk if a_split else a.shape[1]
    b_w = b.shape[1] // nblk if b_split else b.shape[1]

    def body(a_ref, b_ref, o_ref, acc_s):
        t = pl.program_id(1)

        @pl.when(t == 0)
        def _():
            acc_s[...] = jnp.zeros_like(acc_s)

        acc_s[...] += _dot_tn(a_ref[...].astype(BF16), b_ref[...].astype(BF16))

        @pl.when(t == pl.num_programs(1) - 1)
        def _():
            o_ref[...] = acc_s[...].astype(BF16)

    return pl.pallas_call(
        body, name=name, grid=(nblk, s // ts),
        in_specs=[pl.BlockSpec((ts, a_w), (lambda k, t: (t, k)) if a_split else (lambda k, t: (t, 0))),
                  pl.BlockSpec((ts, b_w), (lambda k, t: (t, k)) if b_split else (lambda k, t: (t, 0)))],
        out_specs=pl.BlockSpec((None, a_w, b_w), lambda k, t: (k, 0, 0)),
        out_shape=SDS((nblk, a_w, b_w), BF16),
        scratch_shapes=[pltpu.VMEM((a_w, b_w), F32)],
        compiler_params=_cparams(("arbitrary", "arbitrary"), 48),
    )(a, b)


def _wgrad_t(name, a_t, b, nblk, a_split, b_split, tokens):
    s = b.shape[0]
    ts = min(tokens, s)
    a_w = a_t.shape[0] // nblk if a_split else a_t.shape[0]
    b_w = b.shape[1] // nblk if b_split else b.shape[1]

    def body(a_ref, b_ref, o_ref, acc_s):
        t = pl.program_id(1)

        @pl.when(t == 0)
        def _():
            acc_s[...] = jnp.zeros_like(acc_s)

        acc_s[...] += _dot(a_ref[...], b_ref[...].astype(BF16))

        @pl.when(t == pl.num_programs(1) - 1)
        def _():
            o_ref[...] = acc_s[...].astype(BF16)

    return pl.pallas_call(
        body, name=name, grid=(nblk, s // ts),
        in_specs=[pl.BlockSpec((a_w, ts), (lambda k, t: (k, t)) if a_split else (lambda k, t: (0, t))),
                  pl.BlockSpec((ts, b_w), (lambda k, t: (t, k)) if b_split else (lambda k, t: (t, 0)))],
        out_specs=pl.BlockSpec((None, a_w, b_w), lambda k, t: (k, 0, 0)),
        out_shape=SDS((nblk, a_w, b_w), BF16),
        scratch_shapes=[pltpu.VMEM((a_w, b_w), F32)],
        compiler_params=_cparams(("arbitrary", "arbitrary"), 48),
    )(a_t, b)


def _place():
    x, y, c = lax.axis_index("x"), lax.axis_index("y"), lax.axis_index("c")
    return x, y, c


def _other_chips(x, y):
    return [(x, 1 - y, 2 * x + 1 - y), (1 - x, y, 2 * (1 - x) + y), (1 - x, 1 - y, 2 * (1 - x) + 1 - y)]


class _Plan:
    def __init__(self, arrays, out_shape, sems, start, finish, middle=None):
        self.arrays, self.out_shape, self.sems, self.start, self.finish = arrays, out_shape, sems, start, finish
        self.middle = middle


def _gather_plan(shards):
    n = len(shards)

    def copies(ins, outs, sems):
        send_sems, recv_sems, local_sems = sems
        x, y, c = _place()
        chip = 2 * x + y
        me = 2 * chip + c
        sib = (x, y, 1 - c)
        chips = _other_chips(x, y)

        def rc(k, t, src, blk, to):
            return pltpu.make_async_remote_copy(
                src_ref=src, dst_ref=outs[t].at[blk], send_sem=send_sems.at[k * n + t],
                recv_sem=recv_sems.at[k * n + t], device_id=to, device_id_type=MESH)

        (yx, yy, y_chip), (xx, xy, x_chip), _ = chips
        local = [pltpu.make_async_copy(ins[t], outs[t].at[me], local_sems.at[t]) for t in range(n)]
        sends = ([rc(0, t, ins[t], me, sib) for t in range(n)] + [rc(1, t, ins[t], me, (yx, yy, c)) for t in range(n)]
                 + [rc(2, t, ins[t], me, (xx, xy, c)) for t in range(n)])
        passed = [[rc(4 + j, t, outs[t].at[2 * pc + c], 2 * pc + c, sib) for t in range(n)]
                  for j, (_, _, pc) in enumerate(chips)]
        relays = [[rc(3, t, outs[t].at[2 * y_chip + c], 2 * y_chip + c, (xx, xy, c)) for t in range(n)],
                  [rc(3, t, outs[t].at[2 * x_chip + c], 2 * x_chip + c, (yx, yy, c)) for t in range(n)]]
        return rc, local, sends, passed, relays, chips, chip, c, sib

    def start(ins, outs, sems):
        _, local, sends, _, _, _, _, _, _ = copies(ins, outs, sems)
        for cp in local + sends:
            cp.start()

    def middle(ins, outs, sems):
        rc, _, _, passed, relays, chips, _, c, sib = copies(ins, outs, sems)
        for j in range(2):
            for t in range(n):
                rc(1 + j, t, ins[t], 2 * chips[j][2] + c, sib).wait_recv()
        for j in range(2):
            for cp in passed[j]:
                cp.start()

            @pl.when(c == j)
            def _():
                for cp in relays[j]:
                    cp.start()

    def finish(ins, outs, sems):
        rc, local, sends, passed, relays, chips, chip, c, sib = copies(ins, outs, sems)
        far = 2 * chips[2][2] + c
        for t in range(n):
            rc(3, t, ins[t], far, sib).wait_recv()
        for cp in passed[2]:
            cp.start()
        for t in range(n):
            rc(0, t, ins[t], 2 * chip + 1 - c, sib).wait_recv()
        for j, (px, py, pc) in enumerate(chips):
            for t in range(n):
                rc(4 + j, t, ins[t], 2 * pc + 1 - c, sib).wait_recv()
        for cp in sends + passed[0] + passed[1] + passed[2]:
            cp.wait_send()
        for j in range(2):
            @pl.when(c == j)
            def _():
                for cp in relays[j]:
                    cp.wait_send()
        for cp in local:
            cp.wait()

    return _Plan(list(shards), [SDS((N_SLOT,) + tuple(a.shape), a.dtype) for a in shards],
                 [pltpu.SemaphoreType.DMA((7 * n,)), pltpu.SemaphoreType.DMA((7 * n,)),
                  pltpu.SemaphoreType.DMA((n,))], start, finish, middle)


def _sibling_plan(grads, whole=()):
    n, m = len(grads), len(whole)

    def copies(ins, outs, sems):
        send_sems, recv_sems = sems
        x, y, c = _place()
        sib = (x, y, 1 - c)

        def rc(t, src, dst):
            return pltpu.make_async_remote_copy(src_ref=src, dst_ref=dst, send_sem=send_sems.at[t],
                                                recv_sem=recv_sems.at[t], device_id=sib, device_id_type=MESH)
        return rc, c

    def start(ins, outs, sems):
        rc, c = copies(ins, outs, sems)
        for t in range(n):
            for j in range(4):
                rc(t, ins[t].at[2 * j + 1 - c], outs[t].at[j]).start()
        for t in range(n, n + m):
            rc(t, ins[t], outs[t]).start()

    def finish(ins, outs, sems):
        rc, _ = copies(ins, outs, sems)
        for t in range(n):
            rc(t, ins[t].at[pl.ds(0, 4)], outs[t]).wait()
        for t in range(n, n + m):
            rc(t, ins[t], outs[t]).wait()

    return _Plan(list(grads) + list(whole),
                 [SDS((4,) + tuple(g.shape[1:]), g.dtype) for g in grads] + [SDS(a.shape, a.dtype) for a in whole],
                 [pltpu.SemaphoreType.DMA((n + m,)), pltpu.SemaphoreType.DMA((n + m,))], start, finish)


def _chips_plan(parts, whole=()):
    n, m = len(parts), len(whole)

    def src_of(ins, t, pc):
        return ins[t].at[pc] if t < n else ins[t]

    def local_copies(ins, outs, sems, chip):
        return [pltpu.make_async_copy(src_of(ins, t, chip), outs[t].at[chip], sems[2].at[t]) for t in range(n + m)]

    def start(ins, outs, sems):
        send_sems, recv_sems, _ = sems
        x, y, c = _place()
        chip = 2 * x + y
        for cp in local_copies(ins, outs, sems, chip):
            cp.start()
        for px, py, pc in _other_chips(x, y):
            for t in range(n + m):
                pltpu.make_async_remote_copy(src_ref=src_of(ins, t, pc), dst_ref=outs[t].at[chip],
                                             send_sem=send_sems.at[t], recv_sem=recv_sems.at[t],
                                             device_id=(px, py, c), device_id_type=MESH).start()

    def finish(ins, outs, sems):
        send_sems, recv_sems, _ = sems
        x, y, c = _place()
        for t in range(n + m):
            three = outs[t].at[pl.ds(0, 3)]
            pltpu.make_async_remote_copy(src_ref=three, dst_ref=three, send_sem=send_sems.at[t],
                                         recv_sem=recv_sems.at[t], device_id=(x, y, c), device_id_type=MESH).wait()
        for cp in local_copies(ins, outs, sems, 2 * x + y):
            cp.wait()

    return _Plan(list(parts) + list(whole),
                 [SDS(p.shape, p.dtype) for p in parts] + [SDS((4,) + tuple(a.shape), a.dtype) for a in whole],
                 [pltpu.SemaphoreType.DMA((n + m,)), pltpu.SemaphoreType.DMA((n + m,)),
                  pltpu.SemaphoreType.DMA((n + m,))], start, finish)


def _exchange_plan(arr):
    def peers(x, y, c):
        flip = lambda v, f: 1 - v if f else v
        return [(flip(x, fx), flip(y, fy), flip(c, fc))
                for fx in (0, 1) for fy in (0, 1) for fc in (0, 1) if fx or fy or fc]

    def start(ins, outs, sems):
        x, y, c = _place()
        me = 4 * x + 2 * y + c
        pltpu.make_async_copy(ins[0], outs[0].at[me], sems[2].at[0]).start()
        for to in peers(x, y, c):
            pltpu.make_async_remote_copy(src_ref=ins[0], dst_ref=outs[0].at[me], send_sem=sems[0].at[0],
                                         recv_sem=sems[1].at[0], device_id=to, device_id_type=MESH).start()

    def finish(ins, outs, sems):
        x, y, c = _place()
        seven = outs[0].at[pl.ds(0, 7)]
        pltpu.make_async_remote_copy(src_ref=seven, dst_ref=seven, send_sem=sems[0].at[0], recv_sem=sems[1].at[0],
                                     device_id=(x, y, c), device_id_type=MESH).wait()
        pltpu.make_async_copy(ins[0], outs[0].at[4 * x + 2 * y + c], sems[2].at[0]).wait()

    return _Plan([arr], [SDS((N_SLOT,) + tuple(arr.shape), arr.dtype)],
                 [pltpu.SemaphoreType.DMA((1,)), pltpu.SemaphoreType.DMA((1,)), pltpu.SemaphoreType.DMA((1,))],
                 start, finish)


def _join(*plans):
    def cut(seq, sizes):
        out, at = [], 0
        for k in sizes:
            out.append(seq[at:at + k])
            at += k
        return out

    n_arr = [len(p.arrays) for p in plans]
    n_sem = [len(p.sems) for p in plans]

    def start(ins, outs, sems):
        for p, i, o, s in zip(plans, cut(ins, n_arr), cut(outs, n_arr), cut(sems, n_sem)):
            p.start(i, o, s)

    def finish(ins, outs, sems):
        for p, i, o, s in zip(plans, cut(ins, n_arr), cut(outs, n_arr), cut(sems, n_sem)):
            p.finish(i, o, s)

    def middle(ins, outs, sems):
        for p, i, o, s in zip(plans, cut(ins, n_arr), cut(outs, n_arr), cut(sems, n_sem)):
            if p.middle is not None:
                p.middle(i, o, s)

    return _Plan([a for p in plans for a in p.arrays], [o for p in plans for o in p.out_shape],
                 [s for p in plans for s in p.sems], start, finish,
                 middle if any(p.middle is not None for p in plans) else None)


def _run_plan(name, plan):
    k = len(plan.arrays)

    def body(*refs):
        ins, outs, sems = refs[:k], refs[k:2 * k], refs[2 * k:]
        plan.start(ins, outs, sems)
        if plan.middle is not None:
            plan.middle(ins, outs, sems)
        plan.finish(ins, outs, sems)

    return pl.pallas_call(
        body, name=name, in_specs=[ANY] * k, out_specs=[ANY] * k, out_shape=plan.out_shape,
        scratch_shapes=plan.sems, compiler_params=pltpu.CompilerParams(has_side_effects=True),
    )(*plan.arrays)


def _call(body, *, name, grid, in_specs, out_specs, out_shape, scratch_shapes, params, args, comm=None,
          aliases=None, prefetch=()):
    aliases = aliases or {}
    n_pre = len(prefetch)

    def launch(fn, ins_specs, outs_specs, outs_shape, scratch, operands):
        spec = pltpu.PrefetchScalarGridSpec(num_scalar_prefetch=n_pre, grid=grid, in_specs=ins_specs,
                                            out_specs=outs_specs, scratch_shapes=scratch)
        return pl.pallas_call(fn, name=name, grid_spec=spec, out_shape=outs_shape, compiler_params=params,
                              input_output_aliases=aliases)(*prefetch, *[_small_in_hbm(a) for a in operands])

    if comm is None:
        return list(launch(body, in_specs, out_specs, out_shape, scratch_shapes, args)), []
    n_in, n_out, n_scr, k = len(in_specs), len(out_specs), len(scratch_shapes), len(comm.arrays)

    def wrapped(*refs):
        pre, refs = refs[:n_pre], refs[n_pre:]
        ins = refs[:n_in]
        c_in = refs[n_in:n_in + k]
        outs = refs[n_in + k:n_in + k + n_out]
        c_out = refs[n_in + k + n_out:n_in + 2 * k + n_out]
        scr = refs[n_in + 2 * k + n_out:n_in + 2 * k + n_out + n_scr]
        sems = refs[n_in + 2 * k + n_out + n_scr:]
        step, steps = pl.program_id(0), grid[0]
        for d in range(1, len(grid)):
            step, steps = step * grid[d] + pl.program_id(d), steps * grid[d]

        @pl.when(step == 0)
        def _():
            comm.start(c_in, c_out, sems)

        if comm.middle is not None:
            @pl.when(step == (3 * steps) // 4)
            def _():
                comm.middle(c_in, c_out, sems)

        body(*pre, *ins, *outs, *scr)

        @pl.when(step == steps - 1)
        def _():
            comm.finish(c_in, c_out, sems)

    res = launch(wrapped, list(in_specs) + [ANY] * k, list(out_specs) + [ANY] * k,
                 list(out_shape) + list(comm.out_shape), list(scratch_shapes) + list(comm.sems),
                 tuple(args) + tuple(comm.arrays))
    return list(res[:n_out]), list(res[n_out:])


def _row_tile(rows):
    for t in (512, 256, 128, 64, 32, 16, 8):
        if rows % t == 0:
            return t
    return rows


def _pair_sum(name, g8, recv4, core):
    _, rows, cols = recv4.shape
    tr = _row_tile(rows)
    g42 = g8.reshape(4, 2, rows, cols)

    def body(c_ref, g_ref, r_ref, o_ref):
        del c_ref
        o_ref[...] = (g_ref[...].astype(F32) + r_ref[...].astype(F32)).astype(o_ref.dtype)

    return pl.pallas_call(
        body, name=name,
        grid_spec=pltpu.PrefetchScalarGridSpec(
            num_scalar_prefetch=1, grid=(4, rows // tr),
            in_specs=[pl.BlockSpec((None, None, tr, cols), lambda j, i, c_ref: (j, c_ref[0], i, 0)),
                      pl.BlockSpec((None, tr, cols), lambda j, i, c_ref: (j, i, 0))],
            out_specs=pl.BlockSpec((None, tr, cols), lambda j, i, c_ref: (j, i, 0))),
        out_shape=SDS(recv4.shape, g8.dtype),
        compiler_params=_cparams(("arbitrary", "arbitrary"), 32),
    )(core, g42, recv4)


def _add2(name, a, b):
    rows, cols = a.shape
    tr = _row_tile(rows)

    def body(a_ref, b_ref, o_ref):
        o_ref[...] = a_ref[...] + b_ref[...]

    blk = pl.BlockSpec((tr, cols), lambda i: (i, 0))
    return pl.pallas_call(body, name=name, grid=(rows // tr,), in_specs=[blk, blk], out_specs=blk,
                          out_shape=SDS(a.shape, a.dtype),
                          compiler_params=_cparams(("arbitrary",), 32))(a, b)


def _sum_terms(name, terms):
    k, rows, cols = terms.shape
    tr = _row_tile(rows)

    def body(r_ref, o_ref):
        acc = r_ref[0]
        for q in range(1, k):
            acc = acc + r_ref[q]
        o_ref[...] = acc

    return pl.pallas_call(body, name=name, grid=(rows // tr,),
                          in_specs=[pl.BlockSpec((k, tr, cols), lambda i: (0, i, 0))],
                          out_specs=pl.BlockSpec((tr, cols), lambda i: (i, 0)),
                          out_shape=SDS((rows, cols), terms.dtype),
                          compiler_params=_cparams(("arbitrary",), 32))(terms)


def _adam_update(g, w, m, v):
    c1 = 1.0 / (1.0 - ADAM_B1 ** ADAM_STEP)
    c2 = 1.0 / (1.0 - ADAM_B2 ** ADAM_STEP)
    mn = ADAM_B1 * m + (1.0 - ADAM_B1) * g
    vn = ADAM_B2 * v + (1.0 - ADAM_B2) * (g * g)
    delta = (-ADAM_LR) * ((mn * c1) / (jnp.sqrt(vn * c2) + ADAM_EPS) + ADAM_WD * w)
    return delta, mn, vn


def _adamw_many(name, gs, ws, ms, vs):
    n = len(gs)

    def body(*refs):
        for p in range(n):
            g, w, m, v = (refs[q * n + p][...] for q in range(4))
            d, mn, vn = _adam_update(g, w, m, v)
            refs[4 * n + p][...] = d
            refs[5 * n + p][...] = mn
            refs[6 * n + p][...] = vn

    full = [pl.BlockSpec(w.shape, lambda i: (0, 0)) for w in ws]
    shapes = [SDS(w.shape, F32) for w in ws]
    res = pl.pallas_call(body, name=name, grid=(1,), in_specs=full * 4, out_specs=full * 3, out_shape=shapes * 3,
                         compiler_params=_cparams(("arbitrary",), 32),
                         )(*[_small_in_hbm(a) for a in (*gs, *ws, *ms, *vs)])
    return [(res[p], res[n + p], res[2 * n + p]) for p in range(n)]


def _adamw(name, terms, w, m, v):
    k, rows, cols = terms.shape
    tr = _row_tile(rows)

    def body(t_ref, w_ref, m_ref, v_ref, g_ref, d_ref, mo_ref, vo_ref):
        g = t_ref[0].astype(F32)
        for q in range(1, k):
            g = g + t_ref[q].astype(F32)
        g_ref[...] = g
        d_ref[...], mo_ref[...], vo_ref[...] = _adam_update(g, w_ref[...], m_ref[...], v_ref[...])

    blk = pl.BlockSpec((tr, cols), lambda i: (i, 0))
    return pl.pallas_call(body, name=name, grid=(rows // tr,),
                          in_specs=[pl.BlockSpec((k, tr, cols), lambda i: (0, i, 0)), blk, blk, blk],
                          out_specs=[blk] * 4, out_shape=[SDS((rows, cols), F32)] * 4,
                          compiler_params=_cparams(("arbitrary",), 40))(terms, w, m, v)


def kernel(x, norm_mix_g, w_in, conv_w, conv_b, w_rgate, b_rgate, w_igate, b_igate, lru_lambda, w_out_a, sgu_ln_g, sgu_ln_b, sgu_w_s, sgu_b_s, w_out_b, w_out, norm_mlp_g, w_up, w_down, norm_final_g, loss_target, m_norm_mix_g, m_w_in, m_conv_w, m_conv_b, m_w_rgate, m_b_rgate, m_w_igate, m_b_igate, m_lru_lambda, m_w_out_a, m_sgu_ln_g, m_sgu_ln_b, m_sgu_w_s, m_sgu_b_s, m_w_out_b, m_w_out, m_norm_mlp_g, m_w_up, m_w_down, m_norm_final_g, v_norm_mix_g, v_w_in, v_conv_w, v_conv_b, v_w_rgate, v_b_rgate, v_w_igate, v_b_igate, v_lru_lambda, v_w_out_a, v_sgu_ln_g, v_sgu_ln_b, v_sgu_w_s, v_sgu_b_s, v_w_out_b, v_w_out, v_norm_mlp_g, v_w_up, v_w_down, v_norm_final_g):
    cx, cy, cc = _place()
    me = 4 * cx + 2 * cy + cc
    core = jnp.reshape(cc, (1,)).astype(jnp.int32)
    xs = x[0]
    tgt = loss_target[0]
    s = xs.shape[0]

    gate_shard = jnp.stack([w_rgate[0], w_igate[0]]).astype(BF16).reshape(2 * HEADS * 32, HEAD_DIM)
    vec_shard = jnp.concatenate([conv_w[0], b_rgate[0], b_igate[0]], axis=1)
    vec_shard = jnp.pad(vec_shard, ((0, 4), (0, 256 - vec_shard.shape[1])))
    shards = [w_in[0].astype(BF16), w_out_a[0].astype(BF16), w_out_b[0].astype(BF16), w_out[0].astype(BF16),
              w_up[0].astype(BF16), w_down[0].astype(BF16), gate_shard, vec_shard]
    (z, n1_t, w_in_g), (gate_g, vec_g) = _in_proj(xs, norm_mix_g, shards[0], _slot_order(cx, cy, cc),
                                                comm=_gather_plan(shards[6:8]))
    gates = gate_g.reshape(N_SLOT, 2, HEADS, 32, HEAD_DIM).transpose(1, 2, 0, 3, 4).reshape(2, HEADS, HEAD_DIM, HEAD_DIM)
    w_r_f, w_i_f = gates[0], gates[1]
    conv_w_f = vec_g[:, 0:4, 0:128].transpose(1, 0, 2).reshape(CONV_K, D)
    b_r_f = vec_g[:, 0:4, 128:160].transpose(1, 0, 2).reshape(1, D)
    b_i_f = vec_g[:, 0:4, 160:192].transpose(1, 0, 2).reshape(1, D)
    b_s_t = jnp.transpose(sgu_b_s[0])

    (ya, hs), (w_oa_g, w_ob_g, w_out_g) = _branch_a_fwd(
        z, conv_w_f, conv_b, w_r_f, b_r_f, w_i_f, b_i_f, lru_lambda, comm=_gather_plan(shards[1:4]))
    w_oa_f = w_oa_g.reshape(D, D)
    w_ob_f = w_ob_g.reshape(D, D)
    w_out_f = w_out_g.reshape(D, D)
    (yb,), (w_up_g,) = _branch_b_fwd(z, sgu_ln_g, sgu_ln_b, sgu_w_s[0], b_s_t, comm=_gather_plan(shards[4:5]))
    (pa, pb, merged, h1), (w_down_g,) = _merge_out(ya, yb, z, xs, w_oa_f, w_ob_f, w_out_f,
                                                   comm=_gather_plan(shards[5:6]))
    w_down_f = w_down_g.reshape(N_SLOT * FF_COLS, D)
    gf2 = norm_final_g.reshape(1, D)
    r_act, act_t, n2_t, dh2, loss_acc, d_gfin = _mlp_fwd(h1, norm_mlp_g, w_up_g, w_down_f, gf2, tgt)

    def pair(names, grads, recv):
        return [_pair_sum("pair_sum_" + nm, g, r, core) for nm, g, r in zip(names, grads, recv)]

    g_down = _wgrad_t("wgrad_down", act_t, dh2, N_SLOT // 2, True, False, 2048)
    g_down = g_down.reshape(N_SLOT, FF_COLS, D)
    (df, dh1, d_gmlp), (r_down,) = _mlp_bwd(dh2, r_act, w_down_f, w_up_g, h1, norm_mlp_g,
                                            comm=_sibling_plan([g_down]))
    (p_down,) = pair(["down"], [g_down], [r_down])
    g_up = _wgrad_t("wgrad_up", n2_t, df, N_SLOT, False, True, 4096)
    g_out = _wgrad("wgrad_out", merged, dh1, 1, False, False).reshape(N_SLOT, D // N_SLOT, D)
    (dz, dpa, dpb, dya, dyb), (got_down, r_up, r_out) = _merge_bwd(
        dh1, z, pa, pb, w_out_f, w_oa_f, w_ob_f, comm=_join(_chips_plan([p_down]), _sibling_plan([g_up, g_out])))
    p_up, p_out = pair(["up", "out"], [g_up, g_out], [r_up, r_out])
    g_oa = _wgrad("wgrad_out_a", ya, dpa, 1, False, False).reshape(N_SLOT, D // N_SLOT, D)
    g_ob = _wgrad("wgrad_out_b", yb, dpb, 1, False, False).reshape(N_SLOT, D // N_SLOT, D)
    (dz, d_ws, d_bs, d_ln), (got_up, r_oa, r_ob) = _branch_b_bwd(
        dz, dyb, z, sgu_ln_g, sgu_ln_b, sgu_w_s[0], b_s_t,
        comm=_join(_chips_plan([p_up]), _sibling_plan([g_oa, g_ob])))
    p_oa, p_ob = pair(["out_a", "out_b"], [g_oa, g_ob], [r_oa, r_ob])
    (dz, d_vec, d_wr, d_wi), (got_out, got_oa, got_ob) = _branch_a_bwd(
        dz, dya, z, hs, conv_w_f, conv_b, w_r_f, b_r_f, w_i_f, b_i_f, lru_lambda,
        comm=_chips_plan([p_out, p_oa, p_ob]))
    g_in = _wgrad_t("wgrad_in", n1_t, dz, N_SLOT, False, True, 4096)
    g_gate = jnp.stack([d_wr, d_wi]).reshape(2, HEADS, N_SLOT, 32, HEAD_DIM).transpose(2, 0, 1, 3, 4)
    g_gate = g_gate.reshape(N_SLOT, 2 * HEADS * 32, HEAD_DIM).astype(BF16)

    d_bs_row = jnp.pad(d_bs[:, :, 0].reshape(1, GROUPS * CHUNK), ((0, 0), (0, D - GROUPS * CHUNK)))
    vecs = jnp.concatenate([d_vec, jnp.concatenate([d_ln[0:2], d_gmlp, d_gfin, d_bs_row, jnp.zeros((3, D), F32)])])
    d_ws2 = d_ws.reshape(GROUPS * CHUNK, CHUNK)
    r_in, r_gate, r_vecs, r_ws = _run_plan("rs_sibling_in", _sibling_plan([g_in, g_gate], [vecs, d_ws2]))
    p_in, p_gate = pair(["in", "gate"], [g_in, g_gate], [r_in, r_gate])
    vecs_chip = _add2("pair_sum_vecs", vecs, r_vecs)
    ws_chip = _add2("pair_sum_ws", d_ws2, r_ws)
    (dx, d_gmix), (got_in, got_gate, got_vecs, got_ws) = _in_bwd(
        dz, w_in_g, xs, dh1, norm_mix_g, comm=_chips_plan([p_in, p_gate], [vecs_chip, ws_chip]))
    vecs_sum = _sum_terms("sum_vecs", got_vecs)
    last = jnp.concatenate([d_gmix, jnp.pad(loss_acc[0:1], ((0, 0), (0, D - 128))), jnp.zeros((6, D), F32)])
    (last_all,) = _run_plan("exchange_last", _exchange_plan(last))
    last_sum = _sum_terms("sum_last", last_all)
    loss = last_sum[1, 0]
    got = [got_in, got_oa, got_ob, got_out, got_up, got_down, got_gate]

    def step(nm, terms, w, m, v, rows, cols):
        g, d, mn, vn = _adamw("adamw_" + nm, terms.reshape(4, rows, cols), w.reshape(rows, cols),
                              m.reshape(rows, cols), v.reshape(rows, cols))
        return [a.reshape(w.shape) for a in (g, d, mn, vn)]

    o_in = step("in", got[0], w_in, m_w_in, v_w_in, D, W_IN_COLS)
    o_oa = step("out_a", got[1], w_out_a, m_w_out_a, v_w_out_a, D // N_SLOT, D)
    o_ob = step("out_b", got[2], w_out_b, m_w_out_b, v_w_out_b, D // N_SLOT, D)
    o_out = step("out", got[3], w_out, m_w_out, v_w_out, D // N_SLOT, D)
    o_up = step("up", got[4], w_up, m_w_up, v_w_up, D, FF_COLS)
    o_down = step("down", got[5], w_down, m_w_down, v_w_down, FF_COLS, D)
    gate_w = jnp.stack([w_rgate[0], w_igate[0]]).reshape(2 * HEADS * 32, HEAD_DIM)
    gate_m = jnp.stack([m_w_rgate[0], m_w_igate[0]]).reshape(2 * HEADS * 32, HEAD_DIM)
    gate_v = jnp.stack([v_w_rgate[0], v_w_igate[0]]).reshape(2 * HEADS * 32, HEAD_DIM)
    o_gate = _adamw("adamw_gate", got[6], gate_w, gate_m, gate_v)
    o_gate = [a.reshape(2, 1, HEADS, 32, HEAD_DIM) for a in o_gate]
    o_wr = [a[0] for a in o_gate]
    o_wi = [a[1] for a in o_gate]

    def own(full, width):
        return lax.dynamic_slice_in_dim(full, me * width, width, axis=1)

    small_g = {
        "norm_mix_g": last_sum[0:1], "conv_w": own(vecs_sum[0:4], 128), "conv_b": vecs_sum[4:5],
        "b_rgate": own(vecs_sum[5:6].reshape(HEADS, HEAD_DIM), 32),
        "b_igate": own(vecs_sum[6:7].reshape(HEADS, HEAD_DIM), 32),
        "lru_lambda": vecs_sum[7:8], "sgu_ln_g": vecs_sum[8:9], "sgu_ln_b": vecs_sum[9:10],
        "norm_mlp_g": vecs_sum[10:11], "norm_final_g": vecs_sum[11:12],
        "sgu_b_s": vecs_sum[12, 0:GROUPS * CHUNK].reshape(GROUPS, CHUNK),
    }
    small_w = {"norm_mix_g": (norm_mix_g, m_norm_mix_g, v_norm_mix_g), "conv_w": (conv_w, m_conv_w, v_conv_w),
               "conv_b": (conv_b, m_conv_b, v_conv_b), "b_rgate": (b_rgate, m_b_rgate, v_b_rgate),
               "b_igate": (b_igate, m_b_igate, v_b_igate), "lru_lambda": (lru_lambda, m_lru_lambda, v_lru_lambda),
               "sgu_ln_g": (sgu_ln_g, m_sgu_ln_g, v_sgu_ln_g), "sgu_ln_b": (sgu_ln_b, m_sgu_ln_b, v_sgu_ln_b),
               "norm_mlp_g": (norm_mlp_g, m_norm_mlp_g, v_norm_mlp_g),
               "norm_final_g": (norm_final_g, m_norm_final_g, v_norm_final_g),
               "sgu_b_s": (sgu_b_s, m_sgu_b_s, v_sgu_b_s), "sgu_w_s": (sgu_w_s, m_sgu_w_s, v_sgu_w_s)}
    order = list(small_g)
    as2d = lambda k, a: a.reshape(small_g[k].shape)
    upd = _adamw_many("adamw_small", [small_g[k] for k in order], *[[as2d(k, small_w[k][q]) for k in order]
                                                                     for q in range(3)])
    o_small = {k: [a.reshape(small_w[k][0].shape) for a in (small_g[k],) + u] for k, u in zip(order, upd)}
    ws3 = [a[0].reshape(GROUPS * CHUNK, CHUNK) for a in small_w.pop("sgu_w_s")]
    o_small["sgu_w_s"] = [a.reshape(sgu_w_s.shape) for a in _adamw("adamw_ws", got_ws, *ws3)]

    per_weight = {"norm_mix_g": o_small["norm_mix_g"], "w_in": o_in, "conv_w": o_small["conv_w"],
                  "conv_b": o_small["conv_b"], "w_rgate": o_wr, "b_rgate": o_small["b_rgate"], "w_igate": o_wi,
                  "b_igate": o_small["b_igate"], "lru_lambda": o_small["lru_lambda"], "w_out_a": o_oa,
                  "sgu_ln_g": o_small["sgu_ln_g"], "sgu_ln_b": o_small["sgu_ln_b"], "sgu_w_s": o_small["sgu_w_s"],
                  "sgu_b_s": o_small["sgu_b_s"], "w_out_b": o_ob, "w_out": o_out, "norm_mlp_g": o_small["norm_mlp_g"],
                  "w_up": o_up, "w_down": o_down, "norm_final_g": o_small["norm_final_g"]}
    names_w = list(per_weight)
    return (loss, dx[None], *[per_weight[k][0] for k in names_w], *[per_weight[k][1] for k in names_w],
            *[per_weight[k][2] for k in names_w], *[per_weight[k][3] for k in names_w])
```

```python
import jax
import jax.numpy as jnp
from jax import lax
from jax.experimental import pallas as pl
from jax.experimental.pallas import tpu as pltpu

F32 = jnp.float32
BF16 = jnp.bfloat16
SDS = jax.ShapeDtypeStruct
MESH = pl.DeviceIdType.MESH
ANY = pl.BlockSpec(memory_space=pltpu.HBM)

D = 1024
N_SLOT = 8
W_IN_COLS = 768
FF_COLS = 512
HEADS, HEAD_DIM = 4, 256
GROUPS, GROUP_DIM = 4, 256
CHUNK = 128
CONV_K = 4
NORM_EPS = 1e-6
LN_EPS = 1e-5
LRU_C = 8.0
ADAM_LR, ADAM_B1, ADAM_B2, ADAM_EPS, ADAM_WD, ADAM_STEP = 0.001, 0.9, 0.999, 1e-08, 0.01, 10

TM_ROWS = 1024
TM_MERGE = 512
T_BRANCH_A = 512
T_BRANCH_B = 256
MiB = 1024 * 1024
SMALL_OPERAND = 16 * 1024

_GELU_C = 0.7978845608028654
_GELU_A = 0.044715


def _small_in_hbm(a):
    return pltpu.with_memory_space_constraint(a, pltpu.HBM) if a.size <= SMALL_OPERAND else a


def _cparams(sem, vmem_mib):
    return pltpu.CompilerParams(dimension_semantics=sem, vmem_limit_bytes=vmem_mib * MiB)


def _gelu(x):
    t = jnp.tanh(_GELU_C * (x + _GELU_A * x * x * x))
    return 0.5 * x * (1.0 + t)


def _gelu_and_grad(x):
    x2 = x * x
    t = jnp.tanh(_GELU_C * x * (1.0 + _GELU_A * x2))
    g = 0.5 * x * (1.0 + t)
    dg = 0.5 * (1.0 + t) + 0.5 * x * (1.0 - t * t) * _GELU_C * (1.0 + 3.0 * _GELU_A * x2)
    return g, dg


def _softplus(x):
    return jnp.maximum(x, 0.0) + jnp.log1p(jnp.exp(-jnp.abs(x)))


def _dot(a, b):
    return jnp.dot(a, b, preferred_element_type=F32)


def _dot_nt(a, b):
    return lax.dot_general(a, b, (((1,), (1,)), ((), ())), preferred_element_type=F32)


def _dot_tn(a, b):
    return lax.dot_general(a, b, (((0,), (0,)), ((), ())), preferred_element_type=F32)


def _rows_shifted(prev8, cur, k):
    ext = jnp.concatenate([prev8, cur], axis=0)
    return pltpu.roll(ext, k, 0)[8:]


def _rows_advanced(cur, next8, k):
    t = cur.shape[0]
    ext = jnp.concatenate([cur, next8], axis=0)
    return pltpu.roll(ext, t + 8 - k, 0)[:t]


def _first_second(x, y, c):
    ny, nx, far = _other_chips(x, y)
    pick = lambda a, b: a * (1 - c) + b * c
    first = tuple(pick(a, b) for a, b in zip(ny, nx))
    second = tuple(pick(b, a) for a, b in zip(ny, nx))
    return first, second, far


def _slot_order(x, y, c):
    chip = 2 * x + y
    first, second, far = _first_second(x, y, c)
    order = [2 * chip + c, 2 * chip + 1 - c, 2 * first[2] + c, 2 * second[2] + 1 - c, 2 * second[2] + c,
             2 * first[2] + 1 - c, 2 * far[2] + c, 2 * far[2] + 1 - c]
    return jnp.stack(order).astype(jnp.int32)


def _in_proj(x, g_mix, w_in_own, order, comm=None):
    s = x.shape[0]
    tm = min(TM_ROWS, s)
    ni = s // tm

    def body(order_ref, x_ref, g_ref, own_ref, z_ref, nt_ref, wg_ref, n_s, w_s, send_sems, recv_sems, local_sems):
        j, i = pl.program_id(0), pl.program_id(1)
        px, py, c = _place()
        chip = 2 * px + py
        me = 2 * chip + c
        sib = (px, py, 1 - c)
        chips = _other_chips(px, py)

        def rc(k, src, blk, to):
            return pltpu.make_async_remote_copy(src_ref=src, dst_ref=w_s.at[blk], send_sem=send_sems.at[k],
                                                recv_sem=recv_sems.at[k], device_id=to, device_id_type=MESH)

        del chips
        first, second, far = _first_second(px, py, c)
        blocks = [2 * first[2] + c, 2 * second[2] + c, 2 * far[2] + c]
        own_in = pltpu.make_async_copy(own_ref, w_s.at[me], local_sems.at[0])
        to_first = rc(1, own_ref, me, (first[0], first[1], c))
        to_second = rc(2, own_ref, me, (second[0], second[1], c))
        relay = rc(3, w_s.at[blocks[0]], blocks[0], (second[0], second[1], c))
        sends = [rc(0, own_ref, me, sib), to_first, to_second, relay]
        passed = [rc(4 + q, w_s.at[blk], blk, sib) for q, blk in enumerate(blocks)]
        keep = pltpu.make_async_copy(w_s, wg_ref, local_sems.at[1])

        @pl.when((i == 0) & (j == 0))
        def _():
            own_in.start()
            sends[0].start()
            to_first.start()
            own_in.wait()

        @pl.when((i == 0) & (j == 1))
        def _():
            rc(0, own_ref, 2 * chip + 1 - c, sib).wait_recv()

        for q, blk in enumerate(blocks):
            @pl.when((i == 0) & (j == 2 + 2 * q))
            def _():
                rc(1 + q, own_ref, blk, sib).wait_recv()
                passed[q].start()
                if q == 0:
                    to_second.start()
                    relay.start()

            @pl.when((i == 0) & (j == 3 + 2 * q))
            def _():
                rc(4 + q, own_ref, order_ref[j], sib).wait_recv()

        rows = pl.ds(pl.multiple_of(i * tm, tm), tm)

        @pl.when(j == 0)
        def _():
            xv = x_ref[...]
            rstd = lax.rsqrt(jnp.mean(xv * xv, axis=-1, keepdims=True) + NORM_EPS)
            nb = (xv * rstd * g_ref[...]).astype(BF16)
            n_s[rows, :] = nb
            nt_ref[...] = nb.T

        z_ref[...] = _dot(n_s[rows, :], w_s[order_ref[j]]).astype(BF16)

        @pl.when((i == 0) & (j == N_SLOT - 1))
        def _():
            keep.start()

        @pl.when((i == ni - 1) & (j == N_SLOT - 1))
        def _():
            for cp in sends + passed:
                cp.wait_send()
            keep.wait()

    first_pass = lambda j, i, o: (jnp.where(j == 0, i, ni - 1), 0)
    (z, n1, w_in_g), extra = _call(
        body, name="in_proj", grid=(N_SLOT, ni), prefetch=(order,),
        in_specs=[pl.BlockSpec((tm, D), first_pass),
                  pl.BlockSpec((1, D), lambda j, i, o: (0, 0)), ANY],
        out_specs=[pl.BlockSpec((tm, W_IN_COLS), lambda j, i, o: (i, o[j])),
                   pl.BlockSpec((D, tm), lambda j, i, o: (0, jnp.where(j == 0, i, ni - 1))), ANY],
        out_shape=[SDS((s, N_SLOT * W_IN_COLS), BF16), SDS((D, s), BF16), SDS((N_SLOT, D, W_IN_COLS), BF16)],
        scratch_shapes=[pltpu.VMEM((s, D), BF16), pltpu.VMEM((N_SLOT, D, W_IN_COLS), BF16),
                        pltpu.SemaphoreType.DMA((7,)), pltpu.SemaphoreType.DMA((7,)), pltpu.SemaphoreType.DMA((2,))],
        params=_cparams(("arbitrary", "arbitrary"), 56), args=(x, g_mix, w_in_own), comm=comm)
    return (z, n1, w_in_g), extra


def _lru_gates(xc, xcb, wr_ref, br, wi_ref, bi, sp_lam, a_s, b_s, r_s=None, i_s=None, m_s=None):
    for h in range(HEADS):
        sl = slice(h * HEAD_DIM, (h + 1) * HEAD_DIM)
        r = jax.nn.sigmoid(_dot(xcb[:, sl], wr_ref[h]) + br[:, sl])
        ig = jax.nn.sigmoid(_dot(xcb[:, sl], wi_ref[h]) + bi[:, sl])
        log_a = (-LRU_C) * r * sp_lam[:, sl]
        a = jnp.exp(log_a)
        mult = jnp.sqrt(-jnp.tanh(log_a) * (a * a + 1.0))
        a_s[:, sl] = a
        b_s[:, sl] = xc[:, sl] * ig * mult
        if r_s is not None:
            r_s[:, sl] = r
            i_s[:, sl] = ig
            m_s[:, sl] = mult


def _conv_fwd(xa, prev8, cw, cb):
    xc = cb + cw[0:1, :] * xa
    for k in range(1, CONV_K):
        xc = xc + cw[k:k + 1, :] * _rows_shifted(prev8, xa, k)
    return xc


def _branch_a_fwd(z, conv_w, conv_b, w_r, b_r, w_i, b_i, lam, comm=None):
    s = z.shape[0]
    ta = min(T_BRANCH_A, s)
    per16 = ta // 16

    def body(xa_ref, xp_ref, ga_ref, cw_ref, cb_ref, wr_ref, br_ref, wi_ref, bi_ref, lam_ref,
             ya_ref, hs_ref, a_s, b_s, h_s, carry_s):
        i = pl.program_id(0)

        @pl.when(i == 0)
        def _():
            carry_s[...] = jnp.zeros_like(carry_s)

        xa = xa_ref[...].astype(F32)
        prev8 = jnp.where(i > 0, xp_ref[...].astype(F32)[8:16], 0.0)
        xc = _conv_fwd(xa, prev8, cw_ref[...], cb_ref[...])
        sp_lam = _softplus(-lam_ref[...])
        _lru_gates(xc, xc.astype(BF16), wr_ref, br_ref[...], wi_ref, bi_ref[...], sp_lam, a_s, b_s)

        row = lax.broadcasted_iota(jnp.int32, (8, D), 0)

        def group(g, carry):
            off = pl.multiple_of(g * 8, 8)
            a8 = a_s[pl.ds(off, 8), :]
            b8 = b_s[pl.ds(off, 8), :]
            for d in (1, 2, 4):
                a_sh = jnp.where(row >= d, pltpu.roll(a8, d, 0), 1.0)
                b_sh = jnp.where(row >= d, pltpu.roll(b8, d, 0), 0.0)
                b8 = a8 * b_sh + b8
                a8 = a8 * a_sh
            h8 = b8 + a8 * carry
            h_s[pl.ds(off, 8), :] = h8
            return jnp.broadcast_to(h8[7:8, :], (8, D))

        carry_s[...] = lax.fori_loop(0, ta // 8, group, carry_s[...])
        hs = h_s[...]
        hs_ref[...] = hs.astype(BF16)
        ya_ref[...] = (hs * _gelu(ga_ref[...].astype(F32))).astype(BF16)

    vec = pl.BlockSpec((1, D), lambda i: (0, 0))
    gate = pl.BlockSpec((HEADS, HEAD_DIM, HEAD_DIM), lambda i: (0, 0, 0))
    return _call(
        body, name="branch_a_fwd", grid=(s // ta,),
        in_specs=[pl.BlockSpec((ta, D), lambda i: (i, 0)),
                  pl.BlockSpec((16, D), lambda i: (jnp.maximum(i * per16 - 1, 0), 0)),
                  pl.BlockSpec((ta, D), lambda i: (i, 1)),
                  pl.BlockSpec((CONV_K, D), lambda i: (0, 0)), vec, gate, vec, gate, vec, vec],
        out_specs=[pl.BlockSpec((ta, D), lambda i: (i, 0)), pl.BlockSpec((ta, D), lambda i: (i, 0))],
        out_shape=[SDS((s, D), BF16), SDS((s, D), BF16)],
        scratch_shapes=[pltpu.VMEM((ta, D), F32), pltpu.VMEM((ta, D), F32), pltpu.VMEM((ta, D), F32),
                        pltpu.VMEM((8, D), F32)],
        params=_cparams(("arbitrary",), 40), args=(z, z, z, conv_w, conv_b, w_r, b_r, w_i, b_i, lam), comm=comm)


def _sgu_common(ub, vb, lg, lb, with_grad):
    if with_grad:
        u, du = _gelu_and_grad(ub)
        v, dv = _gelu_and_grad(vb)
    else:
        u, v, du, dv = _gelu(ub), _gelu(vb), None, None
    mu = jnp.mean(v, axis=-1, keepdims=True)
    vc = v - mu
    rstd = lax.rsqrt(jnp.mean(vc * vc, axis=-1, keepdims=True) + LN_EPS)
    vhat = vc * rstd
    vln = vhat * lg + lb
    return u, du, dv, rstd, vhat, vln


def _masked_ws(ws_ref):
    t = lax.broadcasted_iota(jnp.int32, (CHUNK, CHUNK), 0)
    c = lax.broadcasted_iota(jnp.int32, (CHUNK, CHUNK), 1)
    keep = c <= t
    return [jnp.where(keep, ws_ref[g], 0.0).astype(BF16) for g in range(GROUPS)]


def _branch_b_fwd(z, ln_g, ln_b, w_s, b_s_t, comm=None):
    s = z.shape[0]
    tb = min(T_BRANCH_B, s)

    def body(ub_ref, vb_ref, lg_ref, lb_ref, ws_ref, bs_ref, yb_ref):
        u, _, _, _, _, vln = _sgu_common(ub_ref[...].astype(F32), vb_ref[...].astype(F32),
                                         lg_ref[...], lb_ref[...], False)
        vlnb = vln.astype(BF16)
        wm = _masked_ws(ws_ref)
        bs = bs_ref[...]
        for c in range(tb // CHUNK):
            rs = slice(c * CHUNK, (c + 1) * CHUNK)
            for g in range(GROUPS):
                cs = slice(g * GROUP_DIM, (g + 1) * GROUP_DIM)
                sp = _dot(wm[g], vlnb[rs, cs]) + bs[:, g:g + 1]
                yb_ref[rs, cs] = (u[rs, cs] * sp).astype(BF16)

    vec = pl.BlockSpec((1, D), lambda i: (0, 0))
    return _call(
        body, name="branch_b_fwd", grid=(s // tb,),
        in_specs=[pl.BlockSpec((tb, D), lambda i: (i, 2)), pl.BlockSpec((tb, D), lambda i: (i, 3)), vec, vec,
                  pl.BlockSpec((GROUPS, CHUNK, CHUNK), lambda i: (0, 0, 0)),
                  pl.BlockSpec((CHUNK, GROUPS), lambda i: (0, 0))],
        out_specs=[pl.BlockSpec((tb, D), lambda i: (i, 0))],
        out_shape=[SDS((s, D), BF16)], scratch_shapes=[],
        params=_cparams(("arbitrary",), 40), args=(z, z, ln_g, ln_b, w_s, b_s_t), comm=comm)


def _merge_out(ya, yb, z, x, w_oa, w_ob, w_out, comm=None):
    s = x.shape[0]
    tm = min(TM_MERGE, s)

    def body(ya_ref, yb_ref, ma_ref, mb_ref, x_ref, woa_ref, wob_ref, wo_ref, pa_ref, pb_ref, mg_ref, h1_ref):
        pa = _dot(ya_ref[...], woa_ref[...])
        pb = _dot(yb_ref[...], wob_ref[...])
        merged = (jax.nn.sigmoid(ma_ref[...].astype(F32)) * pa
                  + jax.nn.sigmoid(mb_ref[...].astype(F32)) * pb).astype(BF16)
        pa_ref[...] = pa.astype(BF16)
        pb_ref[...] = pb.astype(BF16)
        mg_ref[...] = merged
        h1_ref[...] = x_ref[...] + _dot(merged, wo_ref[...])

    row = pl.BlockSpec((tm, D), lambda i: (i, 0))
    wsp = pl.BlockSpec((D, D), lambda i: (0, 0))
    return _call(
        body, name="merge_out", grid=(s // tm,),
        in_specs=[row, row, pl.BlockSpec((tm, D), lambda i: (i, 4)), pl.BlockSpec((tm, D), lambda i: (i, 5)),
                  row, wsp, wsp, wsp],
        out_specs=[row, row, row, row],
        out_shape=[SDS((s, D), BF16), SDS((s, D), BF16), SDS((s, D), BF16), SDS((s, D), F32)], scratch_shapes=[],
        params=_cparams(("arbitrary",), 48), args=(ya, yb, z, z, x, w_oa, w_ob, w_out), comm=comm)


def _mlp_fwd(h1, g_mlp, w_up_g, w_down, g_fin, tgt):
    s = h1.shape[0]
    tm = min(TM_ROWS, s)
    nj = N_SLOT

    def body(h1_ref, gm_ref, wu_ref, wd_ref, gf_ref, t_ref, r_ref, at_ref, n2t_ref, dh2_ref, loss_ref, dgf_ref,
             n2_s, acc_s):
        i, j = pl.program_id(0), pl.program_id(1)

        @pl.when(j == 0)
        def _():
            hv = h1_ref[...]
            rstd = lax.rsqrt(jnp.mean(hv * hv, axis=-1, keepdims=True) + NORM_EPS)
            nb = (hv * rstd * gm_ref[...]).astype(BF16)
            n2_s[...] = nb
            n2t_ref[...] = nb.T
            acc_s[...] = jnp.zeros_like(acc_s)

        @pl.when((i == 0) & (j == 0))
        def _():
            loss_ref[...] = jnp.zeros_like(loss_ref)
            dgf_ref[...] = jnp.zeros_like(dgf_ref)

        r = jnp.maximum(_dot(n2_s[...], wu_ref[...]), 0.0)
        r_ref[...] = r.astype(BF16)
        act = (r * r).astype(BF16)
        at_ref[...] = act.T
        acc_s[...] += _dot(act, wd_ref[...])

        @pl.when(j == nj - 1)
        def _():
            h2 = h1_ref[...] + acc_s[...]
            rstd = lax.rsqrt(jnp.mean(h2 * h2, axis=-1, keepdims=True) + NORM_EPS)
            hh = h2 * rstd
            gf = gf_ref[...]
            e = hh * gf - t_ref[...]
            loss_ref[...] += jnp.sum(e * e) * (0.5 / D)
            dy = e * (1.0 / D)
            dgf_ref[...] += jnp.sum(dy * hh, axis=0, keepdims=True)
            dhh = dy * gf
            dh2_ref[...] = rstd * (dhh - hh * jnp.mean(dhh * hh, axis=-1, keepdims=True))

    row = pl.BlockSpec((tm, D), lambda i, j: (i, 0))
    vec = pl.BlockSpec((1, D), lambda i, j: (0, 0))
    return pl.pallas_call(
        body, name="mlp_fwd", grid=(s // tm, nj),
        in_specs=[row, vec, pl.BlockSpec((None, D, FF_COLS), lambda i, j: (j, 0, 0)),
                  pl.BlockSpec((FF_COLS, D), lambda i, j: (j, 0)), vec, row],
        out_specs=[pl.BlockSpec((tm, FF_COLS), lambda i, j: (i, j)), pl.BlockSpec((FF_COLS, tm), lambda i, j: (j, i)),
                   pl.BlockSpec((D, tm), lambda i, j: (0, i)), row, pl.BlockSpec((8, 128), lambda i, j: (0, 0)), vec],
        out_shape=[SDS((s, nj * FF_COLS), BF16), SDS((nj * FF_COLS, s), BF16), SDS((D, s), BF16), SDS((s, D), F32),
                   SDS((8, 128), F32), SDS((1, D), F32)],
        scratch_shapes=[pltpu.VMEM((tm, D), BF16), pltpu.VMEM((tm, D), F32)],
        compiler_params=_cparams(("arbitrary", "arbitrary"), 52),
    )(h1, _small_in_hbm(g_mlp), w_up_g, w_down, _small_in_hbm(g_fin), tgt)


def _mlp_bwd(dh2, r, w_down, w_up_g, h1, g_mlp, comm=None):
    s = h1.shape[0]
    tm = min(TM_ROWS, s)
    nj = N_SLOT

    def body(dh2_ref, r_ref, wd_ref, wu_ref, h1_ref, gm_ref, df_ref, dh1_ref, dgm_ref, dh2b_s, acc_s):
        i, j = pl.program_id(0), pl.program_id(1)

        @pl.when(j == 0)
        def _():
            dh2b_s[...] = dh2_ref[...].astype(BF16)
            acc_s[...] = jnp.zeros_like(acc_s)

        @pl.when((i == 0) & (j == 0))
        def _():
            dgm_ref[...] = jnp.zeros_like(dgm_ref)

        d_act = _dot_nt(dh2b_s[...], wd_ref[...])
        df = (d_act * (2.0 * r_ref[...].astype(F32))).astype(BF16)
        df_ref[...] = df
        acc_s[...] += _dot_nt(df, wu_ref[...])

        @pl.when(j == nj - 1)
        def _():
            hv = h1_ref[...]
            rstd = lax.rsqrt(jnp.mean(hv * hv, axis=-1, keepdims=True) + NORM_EPS)
            hh = hv * rstd
            dn2 = acc_s[...]
            dgm_ref[...] += jnp.sum(dn2 * hh, axis=0, keepdims=True)
            dhat = dn2 * gm_ref[...]
            dh1_ref[...] = dh2_ref[...] + rstd * (dhat - hh * jnp.mean(dhat * hh, axis=-1, keepdims=True))

    row = pl.BlockSpec((tm, D), lambda i, j: (i, 0))
    vec = pl.BlockSpec((1, D), lambda i, j: (0, 0))
    ffb = pl.BlockSpec((tm, FF_COLS), lambda i, j: (i, j))
    return _call(
        body, name="mlp_bwd", grid=(s // tm, nj),
        in_specs=[row, ffb, pl.BlockSpec((FF_COLS, D), lambda i, j: (j, 0)),
                  pl.BlockSpec((None, D, FF_COLS), lambda i, j: (j, 0, 0)), row, vec],
        out_specs=[ffb, row, vec],
        out_shape=[SDS((s, nj * FF_COLS), BF16), SDS((s, D), F32), SDS((1, D), F32)],
        scratch_shapes=[pltpu.VMEM((tm, D), BF16), pltpu.VMEM((tm, D), F32)],
        params=_cparams(("arbitrary", "arbitrary"), 52), args=(dh2, r, w_down, w_up_g, h1, g_mlp), comm=comm)


def _merge_bwd(dh1, z, pa, pb, w_out, w_oa, w_ob, comm=None):
    s = dh1.shape[0]
    tm = min(TM_MERGE, s)

    def body(dh1_ref, ma_ref, mb_ref, pa_ref, pb_ref, wo_ref, woa_ref, wob_ref,
             dz_ref, dpa_ref, dpb_ref, dya_ref, dyb_ref):
        dm = _dot_nt(dh1_ref[...].astype(BF16), wo_ref[...])
        sa = jax.nn.sigmoid(ma_ref[...].astype(F32))
        sb = jax.nn.sigmoid(mb_ref[...].astype(F32))
        dpa = (dm * sa).astype(BF16)
        dpb = (dm * sb).astype(BF16)
        dz_ref[:, 0:D] = (dm * pa_ref[...].astype(F32) * sa * (1.0 - sa)).astype(BF16)
        dz_ref[:, D:2 * D] = (dm * pb_ref[...].astype(F32) * sb * (1.0 - sb)).astype(BF16)
        dpa_ref[...] = dpa
        dpb_ref[...] = dpb
        dya_ref[...] = _dot_nt(dpa, woa_ref[...]).astype(BF16)
        dyb_ref[...] = _dot_nt(dpb, wob_ref[...]).astype(BF16)

    row = pl.BlockSpec((tm, D), lambda i: (i, 0))
    wsp = pl.BlockSpec((D, D), lambda i: (0, 0))
    return _call(
        body, name="merge_bwd", grid=(s // tm,),
        in_specs=[row, pl.BlockSpec((tm, D), lambda i: (i, 4)), pl.BlockSpec((tm, D), lambda i: (i, 5)),
                  row, row, wsp, wsp, wsp],
        out_specs=[pl.BlockSpec((tm, 2 * D), lambda i: (i, 2)), row, row, row, row],
        out_shape=[SDS((s, 6 * D), BF16)] + [SDS((s, D), BF16)] * 4, scratch_shapes=[],
        params=_cparams(("arbitrary",), 48), args=(dh1, z, z, pa, pb, w_out, w_oa, w_ob), comm=comm)


def _branch_b_bwd(dz, dyb, z, ln_g, ln_b, w_s, b_s_t, comm=None):
    s = z.shape[0]
    tb = min(T_BRANCH_B, s)

    def body(dz_in, dyb_ref, ub_ref, vb_ref, lg_ref, lb_ref, ws_ref, bs_ref,
             dz_ref, dws_ref, dbs_ref, dln_ref, du_s, dvln_s):
        del dz_in

        @pl.when(pl.program_id(0) == 0)
        def _():
            dws_ref[...] = jnp.zeros_like(dws_ref)
            dbs_ref[...] = jnp.zeros_like(dbs_ref)
            dln_ref[...] = jnp.zeros_like(dln_ref)

        lg = lg_ref[...]
        u, du, dv, rstd, vhat, vln = _sgu_common(ub_ref[...].astype(F32), vb_ref[...].astype(F32),
                                                 lg, lb_ref[...], True)
        vlnb = vln.astype(BF16)
        dyb_v = dyb_ref[...].astype(F32)
        wm = _masked_ws(ws_ref)
        keep = (lax.broadcasted_iota(jnp.int32, (CHUNK, CHUNK), 1)
                <= lax.broadcasted_iota(jnp.int32, (CHUNK, CHUNK), 0))
        bs = bs_ref[...]
        for c in range(tb // CHUNK):
            rs = slice(c * CHUNK, (c + 1) * CHUNK)
            for g in range(GROUPS):
                cs = slice(g * GROUP_DIM, (g + 1) * GROUP_DIM)
                v_blk = vlnb[rs, cs]
                sp = _dot(wm[g], v_blk) + bs[:, g:g + 1]
                d_sp = dyb_v[rs, cs] * u[rs, cs]
                d_spb = d_sp.astype(BF16)
                du_s[rs, cs] = dyb_v[rs, cs] * sp
                dvln_s[rs, cs] = _dot_tn(wm[g], d_spb)
                dws_ref[g] += jnp.where(keep, _dot_nt(d_spb, v_blk), 0.0)
                dbs_ref[g] += jnp.broadcast_to(jnp.sum(d_sp, axis=-1, keepdims=True), (CHUNK, CHUNK))
        dvln = dvln_s[...]
        dln_ref[0:1, :] += jnp.sum(dvln * vhat, axis=0, keepdims=True)
        dln_ref[1:2, :] += jnp.sum(dvln, axis=0, keepdims=True)
        dvh = dvln * lg
        d_v = rstd * (dvh - jnp.mean(dvh, axis=-1, keepdims=True)
                      - vhat * jnp.mean(dvh * vhat, axis=-1, keepdims=True))
        dz_ref[:, 0:D] = (du_s[...] * du).astype(BF16)
        dz_ref[:, D:2 * D] = (d_v * dv).astype(BF16)

    vec = pl.BlockSpec((1, D), lambda i: (0, 0))
    sq = pl.BlockSpec((GROUPS, CHUNK, CHUNK), lambda i: (0, 0, 0))
    return _call(
        body, name="branch_b_bwd", grid=(s // tb,),
        in_specs=[ANY, pl.BlockSpec((tb, D), lambda i: (i, 0)),
                  pl.BlockSpec((tb, D), lambda i: (i, 2)), pl.BlockSpec((tb, D), lambda i: (i, 3)), vec, vec, sq,
                  pl.BlockSpec((CHUNK, GROUPS), lambda i: (0, 0))],
        out_specs=[pl.BlockSpec((tb, 2 * D), lambda i: (i, 1)), sq, sq, pl.BlockSpec((8, D), lambda i: (0, 0))],
        out_shape=[SDS(dz.shape, BF16), SDS((GROUPS, CHUNK, CHUNK), F32), SDS((GROUPS, CHUNK, CHUNK), F32),
                   SDS((8, D), F32)],
        scratch_shapes=[pltpu.VMEM((tb, D), F32), pltpu.VMEM((tb, D), F32)], aliases={0: 0},
        params=_cparams(("arbitrary",), 40), args=(dz, dyb, z, z, ln_g, ln_b, w_s, b_s_t), comm=comm)


def _branch_a_bwd(dz, dya, z, hs, conv_w, conv_b, w_r, b_r, w_i, b_i, lam, comm=None):
    s = z.shape[0]
    ta = min(T_BRANCH_A, s)
    nb = s // ta
    per16 = ta // 16

    def body(dz_in, dya_ref, xa_ref, xp_ref, ga_ref, hs_ref, hp_ref, cw_ref, cb_ref, wr_ref, br_ref, wi_ref,
             bi_ref, lam_ref, dz_ref, vec_ref, dwr_ref, dwi_ref,
             a_s, b_s, h_s, r_s, i_s, m_s, dcar_s, acar_s, dxc_s):
        del dz_in
        i = pl.program_id(0)
        blk = nb - 1 - i

        @pl.when(i == 0)
        def _():
            dcar_s[...] = jnp.zeros_like(dcar_s)
            acar_s[...] = jnp.zeros_like(acar_s)
            dxc_s[...] = jnp.zeros_like(dxc_s)
            vec_ref[...] = jnp.zeros_like(vec_ref)
            dwr_ref[...] = jnp.zeros_like(dwr_ref)
            dwi_ref[...] = jnp.zeros_like(dwi_ref)

        cw = cw_ref[...]
        lam_v = lam_ref[...]
        xa = xa_ref[...].astype(F32)
        prev8 = jnp.where(blk > 0, xp_ref[...].astype(F32)[8:16], 0.0)
        xc = _conv_fwd(xa, prev8, cw, cb_ref[...])
        xcb = xc.astype(BF16)
        sp_lam = _softplus(-lam_v)
        _lru_gates(xc, xcb, wr_ref, br_ref[...], wi_ref, bi_ref[...], sp_lam, a_s, b_s, r_s, i_s, m_s)

        hs_v = hs_ref[...].astype(F32)
        hprev8 = jnp.where(blk > 0, hp_ref[...].astype(F32)[8:16], 0.0)
        h_m1 = _rows_shifted(hprev8, hs_v, 1)
        gg, dgg = _gelu_and_grad(ga_ref[...].astype(F32))
        dya_v = dya_ref[...].astype(F32)
        dz_ref[:, D:2 * D] = (dya_v * hs_v * dgg).astype(BF16)

        a_v = a_s[...]
        a_s[...] = _rows_advanced(a_v, acar_s[...], 1)
        b_s[...] = dya_v * gg

        row = lax.broadcasted_iota(jnp.int32, (8, D), 0)
        ng = ta // 8

        def group(gi, carry):
            off = pl.multiple_of((ng - 1 - gi) * 8, 8)
            c8 = a_s[pl.ds(off, 8), :]
            d8 = b_s[pl.ds(off, 8), :]
            for d in (1, 2, 4):
                c_sh = jnp.where(row < 8 - d, pltpu.roll(c8, 8 - d, 0), 1.0)
                d_sh = jnp.where(row < 8 - d, pltpu.roll(d8, 8 - d, 0), 0.0)
                d8 = c8 * d_sh + d8
                c8 = c8 * c_sh
            dh8 = d8 + c8 * carry
            h_s[pl.ds(off, 8), :] = dh8
            return jnp.broadcast_to(dh8[0:1, :], (8, D))

        dcar_s[...] = lax.fori_loop(0, ng, group, dcar_s[...])
        acar_s[...] = jnp.broadcast_to(a_v[0:1, :], (8, D))

        dbx = h_s[...]
        r_v, i_v, m_v = r_s[...], i_s[...], m_s[...]
        d_mult = dbx * xc * i_v
        d_loga = dbx * h_m1 * a_v - d_mult * (a_v * a_v) / m_v
        d_pr = d_loga * ((-LRU_C) * sp_lam) * r_v * (1.0 - r_v)
        d_pi = dbx * xc * m_v * i_v * (1.0 - i_v)
        vec_ref[7:8, :] += jnp.sum(d_loga * r_v, axis=0, keepdims=True) * (LRU_C * jax.nn.sigmoid(-lam_v))
        vec_ref[5:6, :] += jnp.sum(d_pr, axis=0, keepdims=True)
        vec_ref[6:7, :] += jnp.sum(d_pi, axis=0, keepdims=True)
        d_prb = d_pr.astype(BF16)
        d_pib = d_pi.astype(BF16)
        h_s[...] = dbx * i_v * m_v
        for h in range(HEADS):
            sl = slice(h * HEAD_DIM, (h + 1) * HEAD_DIM)
            h_s[:, sl] += _dot_nt(d_prb[:, sl], wr_ref[h]) + _dot_nt(d_pib[:, sl], wi_ref[h])
            dwr_ref[h] += _dot_tn(xcb[:, sl], d_prb[:, sl])
            dwi_ref[h] += _dot_tn(xcb[:, sl], d_pib[:, sl])
        d_xc = h_s[...]
        vec_ref[4:5, :] += jnp.sum(d_xc, axis=0, keepdims=True)
        vec_ref[0:1, :] += jnp.sum(d_xc * xa, axis=0, keepdims=True)
        d_xa = cw[0:1, :] * d_xc
        nxt = dxc_s[...]
        for k in range(1, CONV_K):
            vec_ref[k:k + 1, :] += jnp.sum(d_xc * _rows_shifted(prev8, xa, k), axis=0, keepdims=True)
            d_xa = d_xa + cw[k:k + 1, :] * _rows_advanced(d_xc, nxt, k)
        dz_ref[:, 0:D] = d_xa.astype(BF16)
        dxc_s[...] = d_xc[0:8, :]

    vec = pl.BlockSpec((1, D), lambda i: (0, 0))
    gate = pl.BlockSpec((HEADS, HEAD_DIM, HEAD_DIM), lambda i: (0, 0, 0))
    cur = lambda c: pl.BlockSpec((ta, D), lambda i: (nb - 1 - i, c))
    before = lambda c: pl.BlockSpec((16, D), lambda i: (jnp.maximum((nb - 1 - i) * per16 - 1, 0), c))
    return _call(
        body, name="branch_a_bwd", grid=(nb,),
        in_specs=[ANY, cur(0), cur(0), before(0), cur(1), cur(0), before(0),
                  pl.BlockSpec((CONV_K, D), lambda i: (0, 0)), vec, gate, vec, gate, vec, vec],
        out_specs=[pl.BlockSpec((ta, 2 * D), lambda i: (nb - 1 - i, 0)), pl.BlockSpec((8, D), lambda i: (0, 0)),
                   gate, gate],
        out_shape=[SDS(dz.shape, BF16), SDS((8, D), F32), SDS((HEADS, HEAD_DIM, HEAD_DIM), F32),
                   SDS((HEADS, HEAD_DIM, HEAD_DIM), F32)],
        scratch_shapes=[pltpu.VMEM((ta, D), F32)] * 6 + [pltpu.VMEM((8, D), F32)] * 3, aliases={0: 0},
        params=_cparams(("arbitrary",), 48),
        args=(dz, dya, z, z, z, hs, hs, conv_w, conv_b, w_r, b_r, w_i, b_i, lam), comm=comm)


def _in_bwd(dz, w_in_g, x, dh1, g_mix, comm=None):
    s = x.shape[0]
    tm = min(TM_ROWS, s)
    nj = N_SLOT

    def body(dz_ref, w_ref, x_ref, dh1_ref, g_ref, dx_ref, dg_ref, acc_s):
        i, j = pl.program_id(0), pl.program_id(1)

        @pl.when(j == 0)
        def _():
            acc_s[...] = jnp.zeros_like(acc_s)

        @pl.when((i == 0) & (j == 0))
        def _():
            dg_ref[...] = jnp.zeros_like(dg_ref)

        acc_s[...] += _dot_nt(dz_ref[...], w_ref[...])

        @pl.when(j == nj - 1)
        def _():
            xv = x_ref[...]
            rstd = lax.rsqrt(jnp.mean(xv * xv, axis=-1, keepdims=True) + NORM_EPS)
            xh = xv * rstd
            dn = acc_s[...]
            dg_ref[...] += jnp.sum(dn * xh, axis=0, keepdims=True)
            dhat = dn * g_ref[...]
            dx_ref[...] = dh1_ref[...] + rstd * (dhat - xh * jnp.mean(dhat * xh, axis=-1, keepdims=True))

    row = pl.BlockSpec((tm, D), lambda i, j: (i, 0))
    vec = pl.BlockSpec((1, D), lambda i, j: (0, 0))
    return _call(
        body, name="in_bwd", grid=(s // tm, nj),
        in_specs=[pl.BlockSpec((tm, W_IN_COLS), lambda i, j: (i, j)),
                  pl.BlockSpec((None, D, W_IN_COLS), lambda i, j: (j, 0, 0)), row, row, vec],
        out_specs=[row, vec],
        out_shape=[SDS((s, D), F32), SDS((1, D), F32)],
        scratch_shapes=[pltpu.VMEM((tm, D), F32)],
        params=_cparams(("arbitrary", "arbitrary"), 48), args=(dz, w_in_g, x, dh1, g_mix), comm=comm)


def _wgrad(name, a, b, nblk, a_split, b_split):
    s = a.shape[0]
    ts = min(TM_ROWS, s)
    a_w = a.shape[1] // nblk if a_split else a.shape[1]
    b_w = b.shape[1] // nblk if b_split else b.shape[1]

    def body(a_ref, b_ref, o_ref, acc_s):
        t = pl.program_id(1)

        @pl.when(t == 0)
        def _():
            acc_s[...] = jnp.zeros_like(acc_s)

        acc_s[...] += _dot_tn(a_ref[...].astype(BF16), b_ref[...].astype(BF16))

        @pl.when(t == pl.num_programs(1) - 1)
        def _():
            o_ref[...] = acc_s[...].astype(BF16)

    return pl.pallas_call(
        body, name=name, grid=(nblk, s // ts),
        in_specs=[pl.BlockSpec((ts, a_w), (lambda k, t: (t, k)) if a_split else (lambda k, t: (t, 0))),
                  pl.BlockSpec((ts, b_w), (lambda k, t: (t, k)) if b_split else (lambda k, t: (t, 0)))],
        out_specs=pl.BlockSpec((None, a_w, b_w), lambda k, t: (k, 0, 0)),
        out_shape=SDS((nblk, a_w, b_w), BF16),
        scratch_shapes=[pltpu.VMEM((a_w, b_w), F32)],
        compiler_params=_cparams(("arbitrary", "arbitrary"), 48),
    )(a, b)


def _wgrad_t(name, a_t, b, nblk, a_split, b_split, tokens):
    s = b.shape[0]
    ts = min(tokens, s)
    a_w = a_t.shape[0] // nblk if a_split else a_t.shape[0]
    b_w = b.shape[1] // nblk if b_split else b.shape[1]

    def body(a_ref, b_ref, o_ref, acc_s):
        t = pl.program_id(1)

        @pl.when(t == 0)
        def _():
            acc_s[...] = jnp.zeros_like(acc_s)

        acc_s[...] += _dot(a_ref[...], b_ref[...].astype(BF16))

        @pl.when(t == pl.num_programs(1) - 1)
        def _():
            o_ref[...] = acc_s[...].astype(BF16)

    return pl.pallas_call(
        body, name=name, grid=(nblk, s // ts),
        in_specs=[pl.BlockSpec((a_w, ts), (lambda k, t: (k, t)) if a_split else (lambda k, t: (0, t))),
                  pl.BlockSpec((ts, b_w), (lambda k, t: (t, k)) if b_split else (lambda k, t: (t, 0)))],
        out_specs=pl.BlockSpec((None, a_w, b_w), lambda k, t: (k, 0, 0)),
        out_shape=SDS((nblk, a_w, b_w), BF16),
        scratch_shapes=[pltpu.VMEM((a_w, b_w), F32)],
        compiler_params=_cparams(("arbitrary", "arbitrary"), 48),
    )(a_t, b)


def _place():
    x, y, c = lax.axis_index("x"), lax.axis_index("y"), lax.axis_index("c")
    return x, y, c


def _other_chips(x, y):
    return [(x, 1 - y, 2 * x + 1 - y), (1 - x, y, 2 * (1 - x) + y), (1 - x, 1 - y, 2 * (1 - x) + 1 - y)]


class _Plan:
    def __init__(self, arrays, out_shape, sems, start, finish, middle=None):
        self.arrays, self.out_shape, self.sems, self.start, self.finish = arrays, out_shape, sems, start, finish
        self.middle = middle


def _gather_plan(shards):
    n = len(shards)

    def copies(ins, outs, sems):
        send_sems, recv_sems, local_sems = sems
        x, y, c = _place()
        chip = 2 * x + y
        me = 2 * chip + c
        sib = (x, y, 1 - c)
        chips = _other_chips(x, y)

        def rc(k, t, src, blk, to):
            return pltpu.make_async_remote_copy(
                src_ref=src, dst_ref=outs[t].at[blk], send_sem=send_sems.at[k * n + t],
                recv_sem=recv_sems.at[k * n + t], device_id=to, device_id_type=MESH)

        (yx, yy, y_chip), (xx, xy, x_chip), _ = chips
        local = [pltpu.make_async_copy(ins[t], outs[t].at[me], local_sems.at[t]) for t in range(n)]
        sends = ([rc(0, t, ins[t], me, sib) for t in range(n)] + [rc(1, t, ins[t], me, (yx, yy, c)) for t in range(n)]
                 + [rc(2, t, ins[t], me, (xx, xy, c)) for t in range(n)])
        passed = [[rc(4 + j, t, outs[t].at[2 * pc + c], 2 * pc + c, sib) for t in range(n)]
                  for j, (_, _, pc) in enumerate(chips)]
        relays = [[rc(3, t, outs[t].at[2 * y_chip + c], 2 * y_chip + c, (xx, xy, c)) for t in range(n)],
                  [rc(3, t, outs[t].at[2 * x_chip + c], 2 * x_chip + c, (yx, yy, c)) for t in range(n)]]
        return rc, local, sends, passed, relays, chips, chip, c, sib

    def start(ins, outs, sems):
        _, local, sends, _, _, _, _, _, _ = copies(ins, outs, sems)
        for cp in local + sends:
            cp.start()

    def middle(ins, outs, sems):
        rc, _, _, passed, relays, chips, _, c, sib = copies(ins, outs, sems)
        for j in range(2):
            for t in range(n):
                rc(1 + j, t, ins[t], 2 * chips[j][2] + c, sib).wait_recv()
        for j in range(2):
            for cp in passed[j]:
                cp.start()

            @pl.when(c == j)
            def _():
                for cp in relays[j]:
                    cp.start()

    def finish(ins, outs, sems):
        rc, local, sends, passed, relays, chips, chip, c, sib = copies(ins, outs, sems)
        far = 2 * chips[2][2] + c
        for t in range(n):
            rc(3, t, ins[t], far, sib).wait_recv()
        for cp in passed[2]:
            cp.start()
        for t in range(n):
            rc(0, t, ins[t], 2 * chip + 1 - c, sib).wait_recv()
        for j, (px, py, pc) in enumerate(chips):
            for t in range(n):
                rc(4 + j, t, ins[t], 2 * pc + 1 - c, sib).wait_recv()
        for cp in sends + passed[0] + passed[1] + passed[2]:
            cp.wait_send()
        for j in range(2):
            @pl.when(c == j)
            def _():
                for cp in relays[j]:
                    cp.wait_send()
        for cp in local:
            cp.wait()

    return _Plan(list(shards), [SDS((N_SLOT,) + tuple(a.shape), a.dtype) for a in shards],
                 [pltpu.SemaphoreType.DMA((7 * n,)), pltpu.SemaphoreType.DMA((7 * n,)),
                  pltpu.SemaphoreType.DMA((n,))], start, finish, middle)


def _sibling_plan(grads, whole=()):
    n, m = len(grads), len(whole)

    def copies(ins, outs, sems):
        send_sems, recv_sems = sems
        x, y, c = _place()
        sib = (x, y, 1 - c)

        def rc(t, src, dst):
            return pltpu.make_async_remote_copy(src_ref=src, dst_ref=dst, send_sem=send_sems.at[t],
                                                recv_sem=recv_sems.at[t], device_id=sib, device_id_type=MESH)
        return rc, c

    def start(ins, outs, sems):
        rc, c = copies(ins, outs, sems)
        for t in range(n):
            for j in range(4):
                rc(t, ins[t].at[2 * j + 1 - c], outs[t].at[j]).start()
        for t in range(n, n + m):
            rc(t, ins[t], outs[t]).start()

    def finish(ins, outs, sems):
        rc, _ = copies(ins, outs, sems)
        for t in range(n):
            rc(t, ins[t].at[pl.ds(0, 4)], outs[t]).wait()
        for t in range(n, n + m):
            rc(t, ins[t], outs[t]).wait()

    return _Plan(list(grads) + list(whole),
                 [SDS((4,) + tuple(g.shape[1:]), g.dtype) for g in grads] + [SDS(a.shape, a.dtype) for a in whole],
                 [pltpu.SemaphoreType.DMA((n + m,)), pltpu.SemaphoreType.DMA((n + m,))], start, finish)


def _chips_plan(parts, whole=()):
    n, m = len(parts), len(whole)

    def src_of(ins, t, pc):
        return ins[t].at[pc] if t < n else ins[t]

    def local_copies(ins, outs, sems, chip):
        return [pltpu.make_async_copy(src_of(ins, t, chip), outs[t].at[chip], sems[2].at[t]) for t in range(n + m)]

    def start(ins, outs, sems):
        send_sems, recv_sems, _ = sems
        x, y, c = _place()
        chip = 2 * x + y
        for cp in local_copies(ins, outs, sems, chip):
            cp.start()
        for px, py, pc in _other_chips(x, y):
            for t in range(n + m):
                pltpu.make_async_remote_copy(src_ref=src_of(ins, t, pc), dst_ref=outs[t].at[chip],
                                             send_sem=send_sems.at[t], recv_sem=recv_sems.at[t],
                                             device_id=(px, py, c), device_id_type=MESH).start()

    def finish(ins, outs, sems):
        send_sems, recv_sems, _ = sems
        x, y, c = _place()
        for t in range(n + m):
            three = outs[t].at[pl.ds(0, 3)]
            pltpu.make_async_remote_copy(src_ref=three, dst_ref=three, send_sem=send_sems.at[t],
                                         recv_sem=recv_sems.at[t], device_id=(x, y, c), device_id_type=MESH).wait()
        for cp in local_copies(ins, outs, sems, 2 * x + y):
            cp.wait()

    return _Plan(list(parts) + list(whole),
                 [SDS(p.shape, p.dtype) for p in parts] + [SDS((4,) + tuple(a.shape), a.dtype) for a in whole],
                 [pltpu.SemaphoreType.DMA((n + m,)), pltpu.SemaphoreType.DMA((n + m,)),
                  pltpu.SemaphoreType.DMA((n + m,))], start, finish)


def _exchange_plan(arr):
    def peers(x, y, c):
        flip = lambda v, f: 1 - v if f else v
        return [(flip(x, fx), flip(y, fy), flip(c, fc))
                for fx in (0, 1) for fy in (0, 1) for fc in (0, 1) if fx or fy or fc]

    def start(ins, outs, sems):
        x, y, c = _place()
        me = 4 * x + 2 * y + c
        pltpu.make_async_copy(ins[0], outs[0].at[me], sems[2].at[0]).start()
        for to in peers(x, y, c):
            pltpu.make_async_remote_copy(src_ref=ins[0], dst_ref=outs[0].at[me], send_sem=sems[0].at[0],
                                         recv_sem=sems[1].at[0], device_id=to, device_id_type=MESH).start()

    def finish(ins, outs, sems):
        x, y, c = _place()
        seven = outs[0].at[pl.ds(0, 7)]
        pltpu.make_async_remote_copy(src_ref=seven, dst_ref=seven, send_sem=sems[0].at[0], recv_sem=sems[1].at[0],
                                     device_id=(x, y, c), device_id_type=MESH).wait()
        pltpu.make_async_copy(ins[0], outs[0].at[4 * x + 2 * y + c], sems[2].at[0]).wait()

    return _Plan([arr], [SDS((N_SLOT,) + tuple(arr.shape), arr.dtype)],
                 [pltpu.SemaphoreType.DMA((1,)), pltpu.SemaphoreType.DMA((1,)), pltpu.SemaphoreType.DMA((1,))],
                 start, finish)


def _join(*plans):
    def cut(seq, sizes):
        out, at = [], 0
        for k in sizes:
            out.append(seq[at:at + k])
            at += k
        return out

    n_arr = [len(p.arrays) for p in plans]
    n_sem = [len(p.sems) for p in plans]

    def start(ins, outs, sems):
        for p, i, o, s in zip(plans, cut(ins, n_arr), cut(outs, n_arr), cut(sems, n_sem)):
            p.start(i, o, s)

    def finish(ins, outs, sems):
        for p, i, o, s in zip(plans, cut(ins, n_arr), cut(outs, n_arr), cut(sems, n_sem)):
            p.finish(i, o, s)

    def middle(ins, outs, sems):
        for p, i, o, s in zip(plans, cut(ins, n_arr), cut(outs, n_arr), cut(sems, n_sem)):
            if p.middle is not None:
                p.middle(i, o, s)

    return _Plan([a for p in plans for a in p.arrays], [o for p in plans for o in p.out_shape],
                 [s for p in plans for s in p.sems], start, finish,
                 middle if any(p.middle is not None for p in plans) else None)


def _run_plan(name, plan):
    k = len(plan.arrays)

    def body(*refs):
        ins, outs, sems = refs[:k], refs[k:2 * k], refs[2 * k:]
        plan.start(ins, outs, sems)
        if plan.middle is not None:
            plan.middle(ins, outs, sems)
        plan.finish(ins, outs, sems)

    return pl.pallas_call(
        body, name=name, in_specs=[ANY] * k, out_specs=[ANY] * k, out_shape=plan.out_shape,
        scratch_shapes=plan.sems, compiler_params=pltpu.CompilerParams(has_side_effects=True),
    )(*plan.arrays)


def _call(body, *, name, grid, in_specs, out_specs, out_shape, scratch_shapes, params, args, comm=None,
          aliases=None, prefetch=()):
    aliases = aliases or {}
    n_pre = len(prefetch)

    def launch(fn, ins_specs, outs_specs, outs_shape, scratch, operands):
        spec = pltpu.PrefetchScalarGridSpec(num_scalar_prefetch=n_pre, grid=grid, in_specs=ins_specs,
                                            out_specs=outs_specs, scratch_shapes=scratch)
        return pl.pallas_call(fn, name=name, grid_spec=spec, out_shape=outs_shape, compiler_params=params,
                              input_output_aliases=aliases)(*prefetch, *[_small_in_hbm(a) for a in operands])

    if comm is None:
        return list(launch(body, in_specs, out_specs, out_shape, scratch_shapes, args)), []
    n_in, n_out, n_scr, k = len(in_specs), len(out_specs), len(scratch_shapes), len(comm.arrays)

    def wrapped(*refs):
        pre, refs = refs[:n_pre], refs[n_pre:]
        ins = refs[:n_in]
        c_in = refs[n_in:n_in + k]
        outs = refs[n_in + k:n_in + k + n_out]
        c_out = refs[n_in + k + n_out:n_in + 2 * k + n_out]
        scr = refs[n_in + 2 * k + n_out:n_in + 2 * k + n_out + n_scr]
        sems = refs[n_in + 2 * k + n_out + n_scr:]
        step, steps = pl.program_id(0), grid[0]
        for d in range(1, len(grid)):
            step, steps = step * grid[d] + pl.program_id(d), steps * grid[d]

        @pl.when(step == 0)
        def _():
            comm.start(c_in, c_out, sems)

        if comm.middle is not None:
            @pl.when(step == (3 * steps) // 4)
            def _():
                comm.middle(c_in, c_out, sems)

        body(*pre, *ins, *outs, *scr)

        @pl.when(step == steps - 1)
        def _():
            comm.finish(c_in, c_out, sems)

    res = launch(wrapped, list(in_specs) + [ANY] * k, list(out_specs) + [ANY] * k,
                 list(out_shape) + list(comm.out_shape), list(scratch_shapes) + list(comm.sems),
                 tuple(args) + tuple(comm.arrays))
    return list(res[:n_out]), list(res[n_out:])


def _row_tile(rows):
    for t in (512, 256, 128, 64, 32, 16, 8):
        if rows % t == 0:
            return t
    return rows


def _pair_sum(name, g8, recv4, core):
    _, rows, cols = recv4.shape
    tr = _row_tile(rows)
    g42 = g8.reshape(4, 2, rows, cols)

    def body(c_ref, g_ref, r_ref, o_ref):
        del c_ref
        o_ref[...] = (g_ref[...].astype(F32) + r_ref[...].astype(F32)).astype(o_ref.dtype)

    return pl.pallas_call(
        body, name=name,
        grid_spec=pltpu.PrefetchScalarGridSpec(
            num_scalar_prefetch=1, grid=(4, rows // tr),
            in_specs=[pl.BlockSpec((None, None, tr, cols), lambda j, i, c_ref: (j, c_ref[0], i, 0)),
                      pl.BlockSpec((None, tr, cols), lambda j, i, c_ref: (j, i, 0))],
            out_specs=pl.BlockSpec((None, tr, cols), lambda j, i, c_ref: (j, i, 0))),
        out_shape=SDS(recv4.shape, g8.dtype),
        compiler_params=_cparams(("arbitrary", "arbitrary"), 32),
    )(core, g42, recv4)


def _add2(name, a, b):
    rows, cols = a.shape
    tr = _row_tile(rows)

    def body(a_ref, b_ref, o_ref):
        o_ref[...] = a_ref[...] + b_ref[...]

    blk = pl.BlockSpec((tr, cols), lambda i: (i, 0))
    return pl.pallas_call(body, name=name, grid=(rows // tr,), in_specs=[blk, blk], out_specs=blk,
                          out_shape=SDS(a.shape, a.dtype),
                          compiler_params=_cparams(("arbitrary",), 32))(a, b)


def _sum_terms(name, terms):
    k, rows, cols = terms.shape
    tr = _row_tile(rows)

    def body(r_ref, o_ref):
        acc = r_ref[0]
        for q in range(1, k):
            acc = acc + r_ref[q]
        o_ref[...] = acc

    return pl.pallas_call(body, name=name, grid=(rows // tr,),
                          in_specs=[pl.BlockSpec((k, tr, cols), lambda i: (0, i, 0))],
                          out_specs=pl.BlockSpec((tr, cols), lambda i: (i, 0)),
                          out_shape=SDS((rows, cols), terms.dtype),
                          compiler_params=_cparams(("arbitrary",), 32))(terms)


def _adam_update(g, w, m, v):
    c1 = 1.0 / (1.0 - ADAM_B1 ** ADAM_STEP)
    c2 = 1.0 / (1.0 - ADAM_B2 ** ADAM_STEP)
    mn = ADAM_B1 * m + (1.0 - ADAM_B1) * g
    vn = ADAM_B2 * v + (1.0 - ADAM_B2) * (g * g)
    delta = (-ADAM_LR) * ((mn * c1) / (jnp.sqrt(vn * c2) + ADAM_EPS) + ADAM_WD * w)
    return delta, mn, vn


def _adamw_many(name, gs, ws, ms, vs):
    n = len(gs)

    def body(*refs):
        for p in range(n):
            g, w, m, v = (refs[q * n + p][...] for q in range(4))
            d, mn, vn = _adam_update(g, w, m, v)
            refs[4 * n + p][...] = d
            refs[5 * n + p][...] = mn
            refs[6 * n + p][...] = vn

    full = [pl.BlockSpec(w.shape, lambda i: (0, 0)) for w in ws]
    shapes = [SDS(w.shape, F32) for w in ws]
    res = pl.pallas_call(body, name=name, grid=(1,), in_specs=full * 4, out_specs=full * 3, out_shape=shapes * 3,
                         compiler_params=_cparams(("arbitrary",), 32),
                         )(*[_small_in_hbm(a) for a in (*gs, *ws, *ms, *vs)])
    return [(res[p], res[n + p], res[2 * n + p]) for p in range(n)]


def _adamw(name, terms, w, m, v):
    k, rows, cols = terms.shape
    tr = _row_tile(rows)

    def body(t_ref, w_ref, m_ref, v_ref, g_ref, d_ref, mo_ref, vo_ref):
        g = t_ref[0].astype(F32)
        for q in range(1, k):
            g = g + t_ref[q].astype(F32)
        g_ref[...] = g
        d_ref[...], mo_ref[...], vo_ref[...] = _adam_update(g, w_ref[...], m_ref[...], v_ref[...])

    blk = pl.BlockSpec((tr, cols), lambda i: (i, 0))
    return pl.pallas_call(body, name=name, grid=(rows // tr,),
                          in_specs=[pl.BlockSpec((k, tr, cols), lambda i: (0, i, 0)), blk, blk, blk],
                          out_specs=[blk] * 4, out_shape=[SDS((rows, cols), F32)] * 4,
                          compiler_params=_cparams(("arbitrary",), 40),
                          )(*[pltpu.with_memory_space_constraint(a, pltpu.HBM) for a in (terms, w, m, v)])


def kernel(x, norm_mix_g, w_in, conv_w, conv_b, w_rgate, b_rgate, w_igate, b_igate, lru_lambda, w_out_a, sgu_ln_g, sgu_ln_b, sgu_w_s, sgu_b_s, w_out_b, w_out, norm_mlp_g, w_up, w_down, norm_final_g, loss_target, m_norm_mix_g, m_w_in, m_conv_w, m_conv_b, m_w_rgate, m_b_rgate, m_w_igate, m_b_igate, m_lru_lambda, m_w_out_a, m_sgu_ln_g, m_sgu_ln_b, m_sgu_w_s, m_sgu_b_s, m_w_out_b, m_w_out, m_norm_mlp_g, m_w_up, m_w_down, m_norm_final_g, v_norm_mix_g, v_w_in, v_conv_w, v_conv_b, v_w_rgate, v_b_rgate, v_w_igate, v_b_igate, v_lru_lambda, v_w_out_a, v_sgu_ln_g, v_sgu_ln_b, v_sgu_w_s, v_sgu_b_s, v_w_out_b, v_w_out, v_norm_mlp_g, v_w_up, v_w_down, v_norm_final_g):
    cx, cy, cc = _place()
    me = 4 * cx + 2 * cy + cc
    core = jnp.reshape(cc, (1,)).astype(jnp.int32)
    xs = x[0]
    tgt = loss_target[0]
    s = xs.shape[0]

    gate_shard = jnp.stack([w_rgate[0], w_igate[0]]).astype(BF16).reshape(2 * HEADS * 32, HEAD_DIM)
    vec_shard = jnp.concatenate([conv_w[0], b_rgate[0], b_igate[0]], axis=1)
    vec_shard = jnp.pad(vec_shard, ((0, 4), (0, 256 - vec_shard.shape[1])))
    shards = [w_in[0].astype(BF16), w_out_a[0].astype(BF16), w_out_b[0].astype(BF16), w_out[0].astype(BF16),
              w_up[0].astype(BF16), w_down[0].astype(BF16), gate_shard, vec_shard]
    (z, n1_t, w_in_g), (gate_g, vec_g) = _in_proj(xs, norm_mix_g, shards[0], _slot_order(cx, cy, cc),
                                                comm=_gather_plan(shards[6:8]))
    gates = gate_g.reshape(N_SLOT, 2, HEADS, 32, HEAD_DIM).transpose(1, 2, 0, 3, 4).reshape(2, HEADS, HEAD_DIM, HEAD_DIM)
    w_r_f, w_i_f = gates[0], gates[1]
    conv_w_f = vec_g[:, 0:4, 0:128].transpose(1, 0, 2).reshape(CONV_K, D)
    b_r_f = vec_g[:, 0:4, 128:160].transpose(1, 0, 2).reshape(1, D)
    b_i_f = vec_g[:, 0:4, 160:192].transpose(1, 0, 2).reshape(1, D)
    b_s_t = jnp.transpose(sgu_b_s[0])

    (ya, hs), (w_oa_g, w_ob_g, w_out_g) = _branch_a_fwd(
        z, conv_w_f, conv_b, w_r_f, b_r_f, w_i_f, b_i_f, lru_lambda, comm=_gather_plan(shards[1:4]))
    w_oa_f = w_oa_g.reshape(D, D)
    w_ob_f = w_ob_g.reshape(D, D)
    w_out_f = w_out_g.reshape(D, D)
    (yb,), (w_up_g,) = _branch_b_fwd(z, sgu_ln_g, sgu_ln_b, sgu_w_s[0], b_s_t, comm=_gather_plan(shards[4:5]))
    (pa, pb, merged, h1), (w_down_g,) = _merge_out(ya, yb, z, xs, w_oa_f, w_ob_f, w_out_f,
                                                   comm=_gather_plan(shards[5:6]))
    w_down_f = w_down_g.reshape(N_SLOT * FF_COLS, D)
    gf2 = norm_final_g.reshape(1, D)
    r_act, act_t, n2_t, dh2, loss_acc, d_gfin = _mlp_fwd(h1, norm_mlp_g, w_up_g, w_down_f, gf2, tgt)

    def pair(names, grads, recv):
        return [_pair_sum("pair_sum_" + nm, g, r, core) for nm, g, r in zip(names, grads, recv)]

    g_down = _wgrad_t("wgrad_down", act_t, dh2, N_SLOT // 2, True, False, 2048)
    g_down = g_down.reshape(N_SLOT, FF_COLS, D)
    (df, dh1, d_gmlp), (r_down,) = _mlp_bwd(dh2, r_act, w_down_f, w_up_g, h1, norm_mlp_g,
                                            comm=_sibling_plan([g_down]))
    (p_down,) = pair(["down"], [g_down], [r_down])
    g_up = _wgrad_t("wgrad_up", n2_t, df, N_SLOT, False, True, 4096)
    g_out = _wgrad("wgrad_out", merged, dh1, 1, False, False).reshape(N_SLOT, D // N_SLOT, D)
    (dz, dpa, dpb, dya, dyb), (got_down, r_up, r_out) = _merge_bwd(
        dh1, z, pa, pb, w_out_f, w_oa_f, w_ob_f, comm=_join(_chips_plan([p_down]), _sibling_plan([g_up, g_out])))
    p_up, p_out = pair(["up", "out"], [g_up, g_out], [r_up, r_out])
    g_oa = _wgrad("wgrad_out_a", ya, dpa, 1, False, False).reshape(N_SLOT, D // N_SLOT, D)
    g_ob = _wgrad("wgrad_out_b", yb, dpb, 1, False, False).reshape(N_SLOT, D // N_SLOT, D)
    (dz, d_ws, d_bs, d_ln), (got_up, r_oa, r_ob) = _branch_b_bwd(
        dz, dyb, z, sgu_ln_g, sgu_ln_b, sgu_w_s[0], b_s_t,
        comm=_join(_chips_plan([p_up]), _sibling_plan([g_oa, g_ob])))
    p_oa, p_ob = pair(["out_a", "out_b"], [g_oa, g_ob], [r_oa, r_ob])
    (dz, d_vec, d_wr, d_wi), (got_out, got_oa, got_ob) = _branch_a_bwd(
        dz, dya, z, hs, conv_w_f, conv_b, w_r_f, b_r_f, w_i_f, b_i_f, lru_lambda,
        comm=_chips_plan([p_out, p_oa, p_ob]))
    g_in = _wgrad_t("wgrad_in", n1_t, dz, N_SLOT, False, True, 4096)
    g_gate = jnp.stack([d_wr, d_wi]).reshape(2, HEADS, N_SLOT, 32, HEAD_DIM).transpose(2, 0, 1, 3, 4)
    g_gate = g_gate.reshape(N_SLOT, 2 * HEADS * 32, HEAD_DIM).astype(BF16)

    d_bs_row = jnp.pad(d_bs[:, :, 0].reshape(1, GROUPS * CHUNK), ((0, 0), (0, D - GROUPS * CHUNK)))
    vecs = jnp.concatenate([d_vec, jnp.concatenate([d_ln[0:2], d_gmlp, d_gfin, d_bs_row, jnp.zeros((3, D), F32)])])
    d_ws2 = d_ws.reshape(GROUPS * CHUNK, CHUNK)
    r_in, r_gate, r_vecs, r_ws = _run_plan("rs_sibling_in", _sibling_plan([g_in, g_gate], [vecs, d_ws2]))
    p_in, p_gate = pair(["in", "gate"], [g_in, g_gate], [r_in, r_gate])
    vecs_chip = _add2("pair_sum_vecs", vecs, r_vecs)
    ws_chip = _add2("pair_sum_ws", d_ws2, r_ws)
    (dx, d_gmix), (got_in, got_gate, got_vecs, got_ws) = _in_bwd(
        dz, w_in_g, xs, dh1, norm_mix_g, comm=_chips_plan([p_in, p_gate], [vecs_chip, ws_chip]))
    vecs_sum = _sum_terms("sum_vecs", got_vecs)
    last = jnp.concatenate([d_gmix, jnp.pad(loss_acc[0:1], ((0, 0), (0, D - 128))), jnp.zeros((6, D), F32)])
    (last_all,) = _run_plan("exchange_last", _exchange_plan(last))
    last_sum = _sum_terms("sum_last", last_all)
    loss = last_sum[1, 0]
    got = [got_in, got_oa, got_ob, got_out, got_up, got_down, got_gate]

    def step(nm, terms, w, m, v, rows, cols):
        g, d, mn, vn = _adamw("adamw_" + nm, terms.reshape(4, rows, cols), w.reshape(rows, cols),
                              m.reshape(rows, cols), v.reshape(rows, cols))
        return [a.reshape(w.shape) for a in (g, d, mn, vn)]

    o_in = step("in", got[0], w_in, m_w_in, v_w_in, D, W_IN_COLS)
    o_oa = step("out_a", got[1], w_out_a, m_w_out_a, v_w_out_a, D // N_SLOT, D)
    o_ob = step("out_b", got[2], w_out_b, m_w_out_b, v_w_out_b, D // N_SLOT, D)
    o_out = step("out", got[3], w_out, m_w_out, v_w_out, D // N_SLOT, D)
    o_up = step("up", got[4], w_up, m_w_up, v_w_up, D, FF_COLS)
    o_down = step("down", got[5], w_down, m_w_down, v_w_down, FF_COLS, D)
    gate_w = jnp.stack([w_rgate[0], w_igate[0]]).reshape(2 * HEADS * 32, HEAD_DIM)
    gate_m = jnp.stack([m_w_rgate[0], m_w_igate[0]]).reshape(2 * HEADS * 32, HEAD_DIM)
    gate_v = jnp.stack([v_w_rgate[0], v_w_igate[0]]).reshape(2 * HEADS * 32, HEAD_DIM)
    o_gate = _adamw("adamw_gate", got[6], gate_w, gate_m, gate_v)
    o_gate = [a.reshape(2, 1, HEADS, 32, HEAD_DIM) for a in o_gate]
    o_wr = [a[0] for a in o_gate]
    o_wi = [a[1] for a in o_gate]

    def own(full, width):
        return lax.dynamic_slice_in_dim(full, me * width, width, axis=1)

    small_g = {
        "norm_mix_g": last_sum[0:1], "conv_w": own(vecs_sum[0:4], 128), "conv_b": vecs_sum[4:5],
        "b_rgate": own(vecs_sum[5:6].reshape(HEADS, HEAD_DIM), 32),
        "b_igate": own(vecs_sum[6:7].reshape(HEADS, HEAD_DIM), 32),
        "lru_lambda": vecs_sum[7:8], "sgu_ln_g": vecs_sum[8:9], "sgu_ln_b": vecs_sum[9:10],
        "norm_mlp_g": vecs_sum[10:11], "norm_final_g": vecs_sum[11:12],
        "sgu_b_s": vecs_sum[12, 0:GROUPS * CHUNK].reshape(GROUPS, CHUNK),
    }
    small_w = {"norm_mix_g": (norm_mix_g, m_norm_mix_g, v_norm_mix_g), "conv_w": (conv_w, m_conv_w, v_conv_w),
               "conv_b": (conv_b, m_conv_b, v_conv_b), "b_rgate": (b_rgate, m_b_rgate, v_b_rgate),
               "b_igate": (b_igate, m_b_igate, v_b_igate), "lru_lambda": (lru_lambda, m_lru_lambda, v_lru_lambda),
               "sgu_ln_g": (sgu_ln_g, m_sgu_ln_g, v_sgu_ln_g), "sgu_ln_b": (sgu_ln_b, m_sgu_ln_b, v_sgu_ln_b),
               "norm_mlp_g": (norm_mlp_g, m_norm_mlp_g, v_norm_mlp_g),
               "norm_final_g": (norm_final_g, m_norm_final_g, v_norm_final_g),
               "sgu_b_s": (sgu_b_s, m_sgu_b_s, v_sgu_b_s), "sgu_w_s": (sgu_w_s, m_sgu_w_s, v_sgu_w_s)}
    order = list(small_g)
    as2d = lambda k, a: a.reshape(small_g[k].shape)
    upd = _adamw_many("adamw_small", [small_g[k] for k in order], *[[as2d(k, small_w[k][q]) for k in order]
                                                                     for q in range(3)])
    o_small = {k: [a.reshape(small_w[k][0].shape) for a in (small_g[k],) + u] for k, u in zip(order, upd)}
    ws3 = [a[0].reshape(GROUPS * CHUNK, CHUNK) for a in small_w.pop("sgu_w_s")]
    o_small["sgu_w_s"] = [a.reshape(sgu_w_s.shape) for a in _adamw("adamw_ws", got_ws, *ws3)]

    per_weight = {"norm_mix_g": o_small["norm_mix_g"], "w_in": o_in, "conv_w": o_small["conv_w"],
                  "conv_b": o_small["conv_b"], "w_rgate": o_wr, "b_rgate": o_small["b_rgate"], "w_igate": o_wi,
                  "b_igate": o_small["b_igate"], "lru_lambda": o_small["lru_lambda"], "w_out_a": o_oa,
                  "sgu_ln_g": o_small["sgu_ln_g"], "sgu_ln_b": o_small["sgu_ln_b"], "sgu_w_s": o_small["sgu_w_s"],
                  "sgu_b_s": o_small["sgu_b_s"], "w_out_b": o_ob, "w_out": o_out, "norm_mlp_g": o_small["norm_mlp_g"],
                  "w_up": o_up, "w_down": o_down, "norm_final_g": o_small["norm_final_g"]}
    names_w = list(per_weight)
    return (loss, dx[None], *[per_weight[k][0] for k in names_w], *[per_weight[k][1] for k in names_w],
            *[per_weight[k][2] for k in names_w], *[per_weight[k][3] for k in names_w])
```

```python
import jax
import jax.numpy as jnp
from jax import lax
from jax.experimental import pallas as pl
from jax.experimental.pallas import tpu as pltpu

F32 = jnp.float32
BF16 = jnp.bfloat16
SDS = jax.ShapeDtypeStruct
MESH = pl.DeviceIdType.MESH
ANY = pl.BlockSpec(memory_space=pltpu.HBM)

D = 1024
N_SLOT = 8
W_IN_COLS = 768
FF_COLS = 512
HEADS, HEAD_DIM = 4, 256
GROUPS, GROUP_DIM = 4, 256
CHUNK = 128
CONV_K = 4
NORM_EPS = 1e-6
LN_EPS = 1e-5
LRU_C = 8.0
ADAM_LR, ADAM_B1, ADAM_B2, ADAM_EPS, ADAM_WD, ADAM_STEP = 0.001, 0.9, 0.999, 1e-08, 0.01, 10

TM_ROWS = 1024
TM_MERGE = 512
T_BRANCH_A = 512
T_BRANCH_B = 256
MiB = 1024 * 1024
SMALL_OPERAND = 16 * 1024

_GELU_C = 0.7978845608028654
_GELU_A = 0.044715


def _small_in_hbm(a):
    return pltpu.with_memory_space_constraint(a, pltpu.HBM) if a.size <= SMALL_OPERAND else a


def _cparams(sem, vmem_mib):
    return pltpu.CompilerParams(dimension_semantics=sem, vmem_limit_bytes=vmem_mib * MiB)


def _gelu(x):
    t = jnp.tanh(_GELU_C * (x + _GELU_A * x * x * x))
    return 0.5 * x * (1.0 + t)


def _gelu_and_grad(x):
    x2 = x * x
    t = jnp.tanh(_GELU_C * x * (1.0 + _GELU_A * x2))
    g = 0.5 * x * (1.0 + t)
    dg = 0.5 * (1.0 + t) + 0.5 * x * (1.0 - t * t) * _GELU_C * (1.0 + 3.0 * _GELU_A * x2)
    return g, dg


def _softplus(x):
    return jnp.maximum(x, 0.0) + jnp.log1p(jnp.exp(-jnp.abs(x)))


def _dot(a, b):
    return jnp.dot(a, b, preferred_element_type=F32)


def _dot_nt(a, b):
    return lax.dot_general(a, b, (((1,), (1,)), ((), ())), preferred_element_type=F32)


def _dot_tn(a, b):
    return lax.dot_general(a, b, (((0,), (0,)), ((), ())), preferred_element_type=F32)


def _rows_shifted(prev8, cur, k):
    ext = jnp.concatenate([prev8, cur], axis=0)
    return pltpu.roll(ext, k, 0)[8:]


def _rows_advanced(cur, next8, k):
    t = cur.shape[0]
    ext = jnp.concatenate([cur, next8], axis=0)
    return pltpu.roll(ext, t + 8 - k, 0)[:t]


def _first_second(x, y, c):
    ny, nx, far = _other_chips(x, y)
    pick = lambda a, b: a * (1 - c) + b * c
    first = tuple(pick(a, b) for a, b in zip(ny, nx))
    second = tuple(pick(b, a) for a, b in zip(ny, nx))
    return first, second, far


def _slot_order(x, y, c):
    chip = 2 * x + y
    first, second, far = _first_second(x, y, c)
    order = [2 * chip + c, 2 * chip + 1 - c, 2 * first[2] + c, 2 * second[2] + 1 - c, 2 * second[2] + c,
             2 * first[2] + 1 - c, 2 * far[2] + c, 2 * far[2] + 1 - c]
    return jnp.stack(order).astype(jnp.int32)


def _in_proj(x, g_mix, w_in_own, order, comm=None):
    s = x.shape[0]
    tm = min(TM_ROWS, s)
    ni = s // tm

    def body(order_ref, x_ref, g_ref, own_ref, z_ref, nt_ref, wg_ref, n_s, w_s, send_sems, recv_sems, local_sems):
        j, i = pl.program_id(0), pl.program_id(1)
        px, py, c = _place()
        chip = 2 * px + py
        me = 2 * chip + c
        sib = (px, py, 1 - c)
        chips = _other_chips(px, py)

        def rc(k, src, blk, to):
            return pltpu.make_async_remote_copy(src_ref=src, dst_ref=w_s.at[blk], send_sem=send_sems.at[k],
                                                recv_sem=recv_sems.at[k], device_id=to, device_id_type=MESH)

        del chips
        first, second, far = _first_second(px, py, c)
        blocks = [2 * first[2] + c, 2 * second[2] + c, 2 * far[2] + c]
        own_in = pltpu.make_async_copy(own_ref, w_s.at[me], local_sems.at[0])
        to_first = rc(1, own_ref, me, (first[0], first[1], c))
        to_second = rc(2, own_ref, me, (second[0], second[1], c))
        relay = rc(3, w_s.at[blocks[0]], blocks[0], (second[0], second[1], c))
        sends = [rc(0, own_ref, me, sib), to_first, to_second, relay]
        passed = [rc(4 + q, w_s.at[blk], blk, sib) for q, blk in enumerate(blocks)]
        keep = pltpu.make_async_copy(w_s, wg_ref, local_sems.at[1])

        @pl.when((i == 0) & (j == 0))
        def _():
            own_in.start()
            sends[0].start()
            to_first.start()
            own_in.wait()

        @pl.when((i == 0) & (j == 1))
        def _():
            rc(0, own_ref, 2 * chip + 1 - c, sib).wait_recv()

        for q, blk in enumerate(blocks):
            @pl.when((i == 0) & (j == 2 + 2 * q))
            def _():
                rc(1 + q, own_ref, blk, sib).wait_recv()
                passed[q].start()
                if q == 0:
                    to_second.start()
                    relay.start()

            @pl.when((i == 0) & (j == 3 + 2 * q))
            def _():
                rc(4 + q, own_ref, order_ref[j], sib).wait_recv()

        rows = pl.ds(pl.multiple_of(i * tm, tm), tm)

        @pl.when(j == 0)
        def _():
            xv = x_ref[...]
            rstd = lax.rsqrt(jnp.mean(xv * xv, axis=-1, keepdims=True) + NORM_EPS)
            nb = (xv * rstd * g_ref[...]).astype(BF16)
            n_s[rows, :] = nb
            nt_ref[...] = nb.T

        z_ref[...] = _dot(n_s[rows, :], w_s[order_ref[j]]).astype(BF16)

        @pl.when((i == 0) & (j == N_SLOT - 1))
        def _():
            keep.start()

        @pl.when((i == ni - 1) & (j == N_SLOT - 1))
        def _():
            for cp in sends + passed:
                cp.wait_send()
            keep.wait()

    first_pass = lambda j, i, o: (jnp.where(j == 0, i, ni - 1), 0)
    (z, n1, w_in_g), extra = _call(
        body, name="in_proj", grid=(N_SLOT, ni), prefetch=(order,),
        in_specs=[pl.BlockSpec((tm, D), first_pass),
                  pl.BlockSpec((1, D), lambda j, i, o: (0, 0)), ANY],
        out_specs=[pl.BlockSpec((tm, W_IN_COLS), lambda j, i, o: (i, o[j])),
                   pl.BlockSpec((D, tm), lambda j, i, o: (0, jnp.where(j == 0, i, ni - 1))), ANY],
        out_shape=[SDS((s, N_SLOT * W_IN_COLS), BF16), SDS((D, s), BF16), SDS((N_SLOT, D, W_IN_COLS), BF16)],
        scratch_shapes=[pltpu.VMEM((s, D), BF16), pltpu.VMEM((N_SLOT, D, W_IN_COLS), BF16),
                        pltpu.SemaphoreType.DMA((7,)), pltpu.SemaphoreType.DMA((7,)), pltpu.SemaphoreType.DMA((2,))],
        params=_cparams(("arbitrary", "arbitrary"), 56), args=(x, g_mix, w_in_own), comm=comm)
    return (z, n1, w_in_g), extra


def _lru_gates(xc, xcb, wr_ref, br, wi_ref, bi, sp_lam, a_s, b_s, r_s=None, i_s=None, m_s=None):
    for h in range(HEADS):
        sl = slice(h * HEAD_DIM, (h + 1) * HEAD_DIM)
        r = jax.nn.sigmoid(_dot(xcb[:, sl], wr_ref[h]) + br[:, sl])
        ig = jax.nn.sigmoid(_dot(xcb[:, sl], wi_ref[h]) + bi[:, sl])
        log_a = (-LRU_C) * r * sp_lam[:, sl]
        a = jnp.exp(log_a)
        mult = jnp.sqrt(-jnp.tanh(log_a) * (a * a + 1.0))
        a_s[:, sl] = a
        b_s[:, sl] = xc[:, sl] * ig * mult
        if r_s is not None:
            r_s[:, sl] = r
            i_s[:, sl] = ig
            m_s[:, sl] = mult


def _conv_fwd(xa, prev8, cw, cb):
    xc = cb + cw[0:1, :] * xa
    for k in range(1, CONV_K):
        xc = xc + cw[k:k + 1, :] * _rows_shifted(prev8, xa, k)
    return xc


def _branch_a_fwd(z, conv_w, conv_b, w_r, b_r, w_i, b_i, lam, comm=None):
    s = z.shape[0]
    ta = min(T_BRANCH_A, s)
    per16 = ta // 16

    def body(xa_ref, xp_ref, ga_ref, cw_ref, cb_ref, wr_ref, br_ref, wi_ref, bi_ref, lam_ref,
             ya_ref, hs_ref, a_s, b_s, h_s, carry_s):
        i = pl.program_id(0)

        @pl.when(i == 0)
        def _():
            carry_s[...] = jnp.zeros_like(carry_s)

        xa = xa_ref[...].astype(F32)
        prev8 = jnp.where(i > 0, xp_ref[...].astype(F32)[8:16], 0.0)
        xc = _conv_fwd(xa, prev8, cw_ref[...], cb_ref[...])
        sp_lam = _softplus(-lam_ref[...])
        _lru_gates(xc, xc.astype(BF16), wr_ref, br_ref[...], wi_ref, bi_ref[...], sp_lam, a_s, b_s)

        row = lax.broadcasted_iota(jnp.int32, (8, D), 0)

        def group(g, carry):
            off = pl.multiple_of(g * 8, 8)
            a8 = a_s[pl.ds(off, 8), :]
            b8 = b_s[pl.ds(off, 8), :]
            for d in (1, 2, 4):
                a_sh = jnp.where(row >= d, pltpu.roll(a8, d, 0), 1.0)
                b_sh = jnp.where(row >= d, pltpu.roll(b8, d, 0), 0.0)
                b8 = a8 * b_sh + b8
                a8 = a8 * a_sh
            h8 = b8 + a8 * carry
            h_s[pl.ds(off, 8), :] = h8
            return jnp.broadcast_to(h8[7:8, :], (8, D))

        carry_s[...] = lax.fori_loop(0, ta // 8, group, carry_s[...])
        hs = h_s[...]
        hs_ref[...] = hs.astype(BF16)
        ya_ref[...] = (hs * _gelu(ga_ref[...].astype(F32))).astype(BF16)

    vec = pl.BlockSpec((1, D), lambda i: (0, 0))
    gate = pl.BlockSpec((HEADS, HEAD_DIM, HEAD_DIM), lambda i: (0, 0, 0))
    return _call(
        body, name="branch_a_fwd", grid=(s // ta,),
        in_specs=[pl.BlockSpec((ta, D), lambda i: (i, 0)),
                  pl.BlockSpec((16, D), lambda i: (jnp.maximum(i * per16 - 1, 0), 0)),
                  pl.BlockSpec((ta, D), lambda i: (i, 1)),
                  pl.BlockSpec((CONV_K, D), lambda i: (0, 0)), vec, gate, vec, gate, vec, vec],
        out_specs=[pl.BlockSpec((ta, D), lambda i: (i, 0)), pl.BlockSpec((ta, D), lambda i: (i, 0))],
        out_shape=[SDS((s, D), BF16), SDS((s, D), BF16)],
        scratch_shapes=[pltpu.VMEM((ta, D), F32), pltpu.VMEM((ta, D), F32), pltpu.VMEM((ta, D), F32),
                        pltpu.VMEM((8, D), F32)],
        params=_cparams(("arbitrary",), 40), args=(z, z, z, conv_w, conv_b, w_r, b_r, w_i, b_i, lam), comm=comm)


def _sgu_common(ub, vb, lg, lb, with_grad):
    if with_grad:
        u, du = _gelu_and_grad(ub)
        v, dv = _gelu_and_grad(vb)
    else:
        u, v, du, dv = _gelu(ub), _gelu(vb), None, None
    mu = jnp.mean(v, axis=-1, keepdims=True)
    vc = v - mu
    rstd = lax.rsqrt(jnp.mean(vc * vc, axis=-1, keepdims=True) + LN_EPS)
    vhat = vc * rstd
    vln = vhat * lg + lb
    return u, du, dv, rstd, vhat, vln


def _masked_ws(ws_ref):
    t = lax.broadcasted_iota(jnp.int32, (CHUNK, CHUNK), 0)
    c = lax.broadcasted_iota(jnp.int32, (CHUNK, CHUNK), 1)
    keep = c <= t
    return [jnp.where(keep, ws_ref[g], 0.0).astype(BF16) for g in range(GROUPS)]


def _branch_b_fwd(z, ln_g, ln_b, w_s, b_s_t, comm=None):
    s = z.shape[0]
    tb = min(T_BRANCH_B, s)

    def body(ub_ref, vb_ref, lg_ref, lb_ref, ws_ref, bs_ref, yb_ref):
        u, _, _, _, _, vln = _sgu_common(ub_ref[...].astype(F32), vb_ref[...].astype(F32),
                                         lg_ref[...], lb_ref[...], False)
        vlnb = vln.astype(BF16)
        wm = _masked_ws(ws_ref)
        bs = bs_ref[...]
        for c in range(tb // CHUNK):
            rs = slice(c * CHUNK, (c + 1) * CHUNK)
            for g in range(GROUPS):
                cs = slice(g * GROUP_DIM, (g + 1) * GROUP_DIM)
                sp = _dot(wm[g], vlnb[rs, cs]) + bs[:, g:g + 1]
                yb_ref[rs, cs] = (u[rs, cs] * sp).astype(BF16)

    vec = pl.BlockSpec((1, D), lambda i: (0, 0))
    return _call(
        body, name="branch_b_fwd", grid=(s // tb,),
        in_specs=[pl.BlockSpec((tb, D), lambda i: (i, 2)), pl.BlockSpec((tb, D), lambda i: (i, 3)), vec, vec,
                  pl.BlockSpec((GROUPS, CHUNK, CHUNK), lambda i: (0, 0, 0)),
                  pl.BlockSpec((CHUNK, GROUPS), lambda i: (0, 0))],
        out_specs=[pl.BlockSpec((tb, D), lambda i: (i, 0))],
        out_shape=[SDS((s, D), BF16)], scratch_shapes=[],
        params=_cparams(("arbitrary",), 40), args=(z, z, ln_g, ln_b, w_s, b_s_t), comm=comm)


def _merge_out(ya, yb, z, x, w_oa, w_ob, w_out, comm=None):
    s = x.shape[0]
    tm = min(TM_MERGE, s)

    def body(ya_ref, yb_ref, ma_ref, mb_ref, x_ref, woa_ref, wob_ref, wo_ref, pa_ref, pb_ref, mg_ref, h1_ref):
        pa = _dot(ya_ref[...], woa_ref[...])
        pb = _dot(yb_ref[...], wob_ref[...])
        merged = (jax.nn.sigmoid(ma_ref[...].astype(F32)) * pa
                  + jax.nn.sigmoid(mb_ref[...].astype(F32)) * pb).astype(BF16)
        pa_ref[...] = pa.astype(BF16)
        pb_ref[...] = pb.astype(BF16)
        mg_ref[...] = merged
        h1_ref[...] = x_ref[...] + _dot(merged, wo_ref[...])

    row = pl.BlockSpec((tm, D), lambda i: (i, 0))
    wsp = pl.BlockSpec((D, D), lambda i: (0, 0))
    return _call(
        body, name="merge_out", grid=(s // tm,),
        in_specs=[row, row, pl.BlockSpec((tm, D), lambda i: (i, 4)), pl.BlockSpec((tm, D), lambda i: (i, 5)),
                  row, wsp, wsp, wsp],
        out_specs=[row, row, row, row],
        out_shape=[SDS((s, D), BF16), SDS((s, D), BF16), SDS((s, D), BF16), SDS((s, D), F32)], scratch_shapes=[],
        params=_cparams(("arbitrary",), 48), args=(ya, yb, z, z, x, w_oa, w_ob, w_out), comm=comm)


def _mlp_fwd(h1, g_mlp, w_up_g, w_down, g_fin, tgt):
    s = h1.shape[0]
    tm = min(TM_ROWS, s)
    nj = N_SLOT

    def body(h1_ref, gm_ref, wu_ref, wd_ref, gf_ref, t_ref, r_ref, at_ref, n2t_ref, dh2_ref, loss_ref, dgf_ref,
             n2_s, acc_s):
        i, j = pl.program_id(0), pl.program_id(1)

        @pl.when(j == 0)
        def _():
            hv = h1_ref[...]
            rstd = lax.rsqrt(jnp.mean(hv * hv, axis=-1, keepdims=True) + NORM_EPS)
            nb = (hv * rstd * gm_ref[...]).astype(BF16)
            n2_s[...] = nb
            n2t_ref[...] = nb.T
            acc_s[...] = jnp.zeros_like(acc_s)

        @pl.when((i == 0) & (j == 0))
        def _():
            loss_ref[...] = jnp.zeros_like(loss_ref)
            dgf_ref[...] = jnp.zeros_like(dgf_ref)

        r = jnp.maximum(_dot(n2_s[...], wu_ref[...]), 0.0)
        r_ref[...] = r.astype(BF16)
        act = (r * r).astype(BF16)
        at_ref[...] = act.T
        acc_s[...] += _dot(act, wd_ref[...])

        @pl.when(j == nj - 1)
        def _():
            h2 = h1_ref[...] + acc_s[...]
            rstd = lax.rsqrt(jnp.mean(h2 * h2, axis=-1, keepdims=True) + NORM_EPS)
            hh = h2 * rstd
            gf = gf_ref[...]
            e = hh * gf - t_ref[...]
            loss_ref[...] += jnp.sum(e * e) * (0.5 / D)
            dy = e * (1.0 / D)
            dgf_ref[...] += jnp.sum(dy * hh, axis=0, keepdims=True)
            dhh = dy * gf
            dh2_ref[...] = rstd * (dhh - hh * jnp.mean(dhh * hh, axis=-1, keepdims=True))

    row = pl.BlockSpec((tm, D), lambda i, j: (i, 0))
    vec = pl.BlockSpec((1, D), lambda i, j: (0, 0))
    return pl.pallas_call(
        body, name="mlp_fwd", grid=(s // tm, nj),
        in_specs=[row, vec, pl.BlockSpec((None, D, FF_COLS), lambda i, j: (j, 0, 0)),
                  pl.BlockSpec((FF_COLS, D), lambda i, j: (j, 0)), vec, row],
        out_specs=[pl.BlockSpec((tm, FF_COLS), lambda i, j: (i, j)), pl.BlockSpec((FF_COLS, tm), lambda i, j: (j, i)),
                   pl.BlockSpec((D, tm), lambda i, j: (0, i)), row, pl.BlockSpec((8, 128), lambda i, j: (0, 0)), vec],
        out_shape=[SDS((s, nj * FF_COLS), BF16), SDS((nj * FF_COLS, s), BF16), SDS((D, s), BF16), SDS((s, D), F32),
                   SDS((8, 128), F32), SDS((1, D), F32)],
        scratch_shapes=[pltpu.VMEM((tm, D), BF16), pltpu.VMEM((tm, D), F32)],
        compiler_params=_cparams(("arbitrary", "arbitrary"), 52),
    )(h1, _small_in_hbm(g_mlp), w_up_g, w_down, _small_in_hbm(g_fin), tgt)


def _mlp_bwd(dh2, r, w_down, w_up_g, h1, g_mlp, comm=None):
    s = h1.shape[0]
    tm = min(TM_ROWS, s)
    nj = N_SLOT

    def body(dh2_ref, r_ref, wd_ref, wu_ref, h1_ref, gm_ref, df_ref, dh1_ref, dgm_ref, dh2b_s, acc_s):
        i, j = pl.program_id(0), pl.program_id(1)

        @pl.when(j == 0)
        def _():
            dh2b_s[...] = dh2_ref[...].astype(BF16)
            acc_s[...] = jnp.zeros_like(acc_s)

        @pl.when((i == 0) & (j == 0))
        def _():
            dgm_ref[...] = jnp.zeros_like(dgm_ref)

        d_act = _dot_nt(dh2b_s[...], wd_ref[...])
        df = (d_act * (2.0 * r_ref[...].astype(F32))).astype(BF16)
        df_ref[...] = df
        acc_s[...] += _dot_nt(df, wu_ref[...])

        @pl.when(j == nj - 1)
        def _():
            hv = h1_ref[...]
            rstd = lax.rsqrt(jnp.mean(hv * hv, axis=-1, keepdims=True) + NORM_EPS)
            hh = hv * rstd
            dn2 = acc_s[...]
            dgm_ref[...] += jnp.sum(dn2 * hh, axis=0, keepdims=True)
            dhat = dn2 * gm_ref[...]
            dh1_ref[...] = dh2_ref[...] + rstd * (dhat - hh * jnp.mean(dhat * hh, axis=-1, keepdims=True))

    row = pl.BlockSpec((tm, D), lambda i, j: (i, 0))
    vec = pl.BlockSpec((1, D), lambda i, j: (0, 0))
    ffb = pl.BlockSpec((tm, FF_COLS), lambda i, j: (i, j))
    return _call(
        body, name="mlp_bwd", grid=(s // tm, nj),
        in_specs=[row, ffb, pl.BlockSpec((FF_COLS, D), lambda i, j: (j, 0)),
                  pl.BlockSpec((None, D, FF_COLS), lambda i, j: (j, 0, 0)), row, vec],
        out_specs=[ffb, row, vec],
        out_shape=[SDS((s, nj * FF_COLS), BF16), SDS((s, D), F32), SDS((1, D), F32)],
        scratch_shapes=[pltpu.VMEM((tm, D), BF16), pltpu.VMEM((tm, D), F32)],
        params=_cparams(("arbitrary", "arbitrary"), 52), args=(dh2, r, w_down, w_up_g, h1, g_mlp), comm=comm)


def _merge_bwd(dh1, z, pa, pb, w_out, w_oa, w_ob, comm=None):
    s = dh1.shape[0]
    tm = min(TM_MERGE, s)

    def body(dh1_ref, ma_ref, mb_ref, pa_ref, pb_ref, wo_ref, woa_ref, wob_ref,
             dz_ref, dpa_ref, dpb_ref, dya_ref, dyb_ref):
        dm = _dot_nt(dh1_ref[...].astype(BF16), wo_ref[...])
        sa = jax.nn.sigmoid(ma_ref[...].astype(F32))
        sb = jax.nn.sigmoid(mb_ref[...].astype(F32))
        dpa = (dm * sa).astype(BF16)
        dpb = (dm * sb).astype(BF16)
        dz_ref[:, 0:D] = (dm * pa_ref[...].astype(F32) * sa * (1.0 - sa)).astype(BF16)
        dz_ref[:, D:2 * D] = (dm * pb_ref[...].astype(F32) * sb * (1.0 - sb)).astype(BF16)
        dpa_ref[...] = dpa
        dpb_ref[...] = dpb
        dya_ref[...] = _dot_nt(dpa, woa_ref[...]).astype(BF16)
        dyb_ref[...] = _dot_nt(dpb, wob_ref[...]).astype(BF16)

    row = pl.BlockSpec((tm, D), lambda i: (i, 0))
    wsp = pl.BlockSpec((D, D), lambda i: (0, 0))
    return _call(
        body, name="merge_bwd", grid=(s // tm,),
        in_specs=[row, pl.BlockSpec((tm, D), lambda i: (i, 4)), pl.BlockSpec((tm, D), lambda i: (i, 5)),
                  row, row, wsp, wsp, wsp],
        out_specs=[pl.BlockSpec((tm, 2 * D), lambda i: (i, 2)), row, row, row, row],
        out_shape=[SDS((s, 6 * D), BF16)] + [SDS((s, D), BF16)] * 4, scratch_shapes=[],
        params=_cparams(("arbitrary",), 48), args=(dh1, z, z, pa, pb, w_out, w_oa, w_ob), comm=comm)


def _branch_b_bwd(dz, dyb, z, ln_g, ln_b, w_s, b_s_t, comm=None):
    s = z.shape[0]
    tb = min(T_BRANCH_B, s)

    def body(dz_in, dyb_ref, ub_ref, vb_ref, lg_ref, lb_ref, ws_ref, bs_ref,
             dz_ref, dws_ref, dbs_ref, dln_ref, du_s, dvln_s):
        del dz_in

        @pl.when(pl.program_id(0) == 0)
        def _():
            dws_ref[...] = jnp.zeros_like(dws_ref)
            dbs_ref[...] = jnp.zeros_like(dbs_ref)
            dln_ref[...] = jnp.zeros_like(dln_ref)

        lg = lg_ref[...]
        u, du, dv, rstd, vhat, vln = _sgu_common(ub_ref[...].astype(F32), vb_ref[...].astype(F32),
                                                 lg, lb_ref[...], True)
        vlnb = vln.astype(BF16)
        dyb_v = dyb_ref[...].astype(F32)
        wm = _masked_ws(ws_ref)
        keep = (lax.broadcasted_iota(jnp.int32, (CHUNK, CHUNK), 1)
                <= lax.broadcasted_iota(jnp.int32, (CHUNK, CHUNK), 0))
        bs = bs_ref[...]
        for c in range(tb // CHUNK):
            rs = slice(c * CHUNK, (c + 1) * CHUNK)
            for g in range(GROUPS):
                cs = slice(g * GROUP_DIM, (g + 1) * GROUP_DIM)
                v_blk = vlnb[rs, cs]
                sp = _dot(wm[g], v_blk) + bs[:, g:g + 1]
                d_sp = dyb_v[rs, cs] * u[rs, cs]
                d_spb = d_sp.astype(BF16)
                du_s[rs, cs] = dyb_v[rs, cs] * sp
                dvln_s[rs, cs] = _dot_tn(wm[g], d_spb)
                dws_ref[g] += jnp.where(keep, _dot_nt(d_spb, v_blk), 0.0)
                dbs_ref[g] += jnp.broadcast_to(jnp.sum(d_sp, axis=-1, keepdims=True), (CHUNK, CHUNK))
        dvln = dvln_s[...]
        dln_ref[0:1, :] += jnp.sum(dvln * vhat, axis=0, keepdims=True)
        dln_ref[1:2, :] += jnp.sum(dvln, axis=0, keepdims=True)
        dvh = dvln * lg
        d_v = rstd * (dvh - jnp.mean(dvh, axis=-1, keepdims=True)
                      - vhat * jnp.mean(dvh * vhat, axis=-1, keepdims=True))
        dz_ref[:, 0:D] = (du_s[...] * du).astype(BF16)
        dz_ref[:, D:2 * D] = (d_v * dv).astype(BF16)

    vec = pl.BlockSpec((1, D), lambda i: (0, 0))
    sq = pl.BlockSpec((GROUPS, CHUNK, CHUNK), lambda i: (0, 0, 0))
    return _call(
        body, name="branch_b_bwd", grid=(s // tb,),
        in_specs=[ANY, pl.BlockSpec((tb, D), lambda i: (i, 0)),
                  pl.BlockSpec((tb, D), lambda i: (i, 2)), pl.BlockSpec((tb, D), lambda i: (i, 3)), vec, vec, sq,
                  pl.BlockSpec((CHUNK, GROUPS), lambda i: (0, 0))],
        out_specs=[pl.BlockSpec((tb, 2 * D), lambda i: (i, 1)), sq, sq, pl.BlockSpec((8, D), lambda i: (0, 0))],
        out_shape=[SDS(dz.shape, BF16), SDS((GROUPS, CHUNK, CHUNK), F32), SDS((GROUPS, CHUNK, CHUNK), F32),
                   SDS((8, D), F32)],
        scratch_shapes=[pltpu.VMEM((tb, D), F32), pltpu.VMEM((tb, D), F32)], aliases={0: 0},
        params=_cparams(("arbitrary",), 40), args=(dz, dyb, z, z, ln_g, ln_b, w_s, b_s_t), comm=comm)


def _branch_a_bwd(dz, dya, z, hs, conv_w, conv_b, w_r, b_r, w_i, b_i, lam, comm=None):
    s = z.shape[0]
    ta = min(T_BRANCH_A, s)
    nb = s // ta
    per16 = ta // 16

    def body(dz_in, dya_ref, xa_ref, xp_ref, ga_ref, hs_ref, hp_ref, cw_ref, cb_ref, wr_ref, br_ref, wi_ref,
             bi_ref, lam_ref, dz_ref, vec_ref, dwr_ref, dwi_ref,
             a_s, b_s, h_s, r_s, i_s, m_s, dcar_s, acar_s, dxc_s):
        del dz_in
        i = pl.program_id(0)
        blk = nb - 1 - i

        @pl.when(i == 0)
        def _():
            dcar_s[...] = jnp.zeros_like(dcar_s)
            acar_s[...] = jnp.zeros_like(acar_s)
            dxc_s[...] = jnp.zeros_like(dxc_s)
            vec_ref[...] = jnp.zeros_like(vec_ref)
            dwr_ref[...] = jnp.zeros_like(dwr_ref)
            dwi_ref[...] = jnp.zeros_like(dwi_ref)

        cw = cw_ref[...]
        lam_v = lam_ref[...]
        xa = xa_ref[...].astype(F32)
        prev8 = jnp.where(blk > 0, xp_ref[...].astype(F32)[8:16], 0.0)
        xc = _conv_fwd(xa, prev8, cw, cb_ref[...])
        xcb = xc.astype(BF16)
        sp_lam = _softplus(-lam_v)
        _lru_gates(xc, xcb, wr_ref, br_ref[...], wi_ref, bi_ref[...], sp_lam, a_s, b_s, r_s, i_s, m_s)

        hs_v = hs_ref[...].astype(F32)
        hprev8 = jnp.where(blk > 0, hp_ref[...].astype(F32)[8:16], 0.0)
        h_m1 = _rows_shifted(hprev8, hs_v, 1)
        gg, dgg = _gelu_and_grad(ga_ref[...].astype(F32))
        dya_v = dya_ref[...].astype(F32)
        dz_ref[:, D:2 * D] = (dya_v * hs_v * dgg).astype(BF16)

        a_v = a_s[...]
        a_s[...] = _rows_advanced(a_v, acar_s[...], 1)
        b_s[...] = dya_v * gg

        row = lax.broadcasted_iota(jnp.int32, (8, D), 0)
        ng = ta // 8

        def group(gi, carry):
            off = pl.multiple_of((ng - 1 - gi) * 8, 8)
            c8 = a_s[pl.ds(off, 8), :]
            d8 = b_s[pl.ds(off, 8), :]
            for d in (1, 2, 4):
                c_sh = jnp.where(row < 8 - d, pltpu.roll(c8, 8 - d, 0), 1.0)
                d_sh = jnp.where(row < 8 - d, pltpu.roll(d8, 8 - d, 0), 0.0)
                d8 = c8 * d_sh + d8
                c8 = c8 * c_sh
            dh8 = d8 + c8 * carry
            h_s[pl.ds(off, 8), :] = dh8
            return jnp.broadcast_to(dh8[0:1, :], (8, D))

        dcar_s[...] = lax.fori_loop(0, ng, group, dcar_s[...])
        acar_s[...] = jnp.broadcast_to(a_v[0:1, :], (8, D))

        dbx = h_s[...]
        r_v, i_v, m_v = r_s[...], i_s[...], m_s[...]
        d_mult = dbx * xc * i_v
        d_loga = dbx * h_m1 * a_v - d_mult * (a_v * a_v) / m_v
        d_pr = d_loga * ((-LRU_C) * sp_lam) * r_v * (1.0 - r_v)
        d_pi = dbx * xc * m_v * i_v * (1.0 - i_v)
        vec_ref[7:8, :] += jnp.sum(d_loga * r_v, axis=0, keepdims=True) * (LRU_C * jax.nn.sigmoid(-lam_v))
        vec_ref[5:6, :] += jnp.sum(d_pr, axis=0, keepdims=True)
        vec_ref[6:7, :] += jnp.sum(d_pi, axis=0, keepdims=True)
        d_prb = d_pr.astype(BF16)
        d_pib = d_pi.astype(BF16)
        h_s[...] = dbx * i_v * m_v
        for h in range(HEADS):
            sl = slice(h * HEAD_DIM, (h + 1) * HEAD_DIM)
            h_s[:, sl] += _dot_nt(d_prb[:, sl], wr_ref[h]) + _dot_nt(d_pib[:, sl], wi_ref[h])
            dwr_ref[h] += _dot_tn(xcb[:, sl], d_prb[:, sl])
            dwi_ref[h] += _dot_tn(xcb[:, sl], d_pib[:, sl])
        d_xc = h_s[...]
        vec_ref[4:5, :] += jnp.sum(d_xc, axis=0, keepdims=True)
        vec_ref[0:1, :] += jnp.sum(d_xc * xa, axis=0, keepdims=True)
        d_xa = cw[0:1, :] * d_xc
        nxt = dxc_s[...]
        for k in range(1, CONV_K):
            vec_ref[k:k + 1, :] += jnp.sum(d_xc * _rows_shifted(prev8, xa, k), axis=0, keepdims=True)
            d_xa = d_xa + cw[k:k + 1, :] * _rows_advanced(d_xc, nxt, k)
        dz_ref[:, 0:D] = d_xa.astype(BF16)
        dxc_s[...] = d_xc[0:8, :]

    vec = pl.BlockSpec((1, D), lambda i: (0, 0))
    gate = pl.BlockSpec((HEADS, HEAD_DIM, HEAD_DIM), lambda i: (0, 0, 0))
    cur = lambda c: pl.BlockSpec((ta, D), lambda i: (nb - 1 - i, c))
    before = lambda c: pl.BlockSpec((16, D), lambda i: (jnp.maximum((nb - 1 - i) * per16 - 1, 0), c))
    return _call(
        body, name="branch_a_bwd", grid=(nb,),
        in_specs=[ANY, cur(0), cur(0), before(0), cur(1), cur(0), before(0),
                  pl.BlockSpec((CONV_K, D), lambda i: (0, 0)), vec, gate, vec, gate, vec, vec],
        out_specs=[pl.BlockSpec((ta, 2 * D), lambda i: (nb - 1 - i, 0)), pl.BlockSpec((8, D), lambda i: (0, 0)),
                   gate, gate],
        out_shape=[SDS(dz.shape, BF16), SDS((8, D), F32), SDS((HEADS, HEAD_DIM, HEAD_DIM), F32),
                   SDS((HEADS, HEAD_DIM, HEAD_DIM), F32)],
        scratch_shapes=[pltpu.VMEM((ta, D), F32)] * 6 + [pltpu.VMEM((8, D), F32)] * 3, aliases={0: 0},
        params=_cparams(("arbitrary",), 48),
        args=(dz, dya, z, z, z, hs, hs, conv_w, conv_b, w_r, b_r, w_i, b_i, lam), comm=comm)


def _in_bwd(dz, w_in_g, x, dh1, g_mix, comm=None):
    s = x.shape[0]
    tm = min(TM_ROWS, s)
    nj = N_SLOT

    def body(dz_ref, w_ref, x_ref, dh1_ref, g_ref, dx_ref, dg_ref, acc_s):
        i, j = pl.program_id(0), pl.program_id(1)

        @pl.when(j == 0)
        def _():
            acc_s[...] = jnp.zeros_like(acc_s)

        @pl.when((i == 0) & (j == 0))
        def _():
            dg_ref[...] = jnp.zeros_like(dg_ref)

        acc_s[...] += _dot_nt(dz_ref[...], w_ref[...])

        @pl.when(j == nj - 1)
        def _():
            xv = x_ref[...]
            rstd = lax.rsqrt(jnp.mean(xv * xv, axis=-1, keepdims=True) + NORM_EPS)
            xh = xv * rstd
            dn = acc_s[...]
            dg_ref[...] += jnp.sum(dn * xh, axis=0, keepdims=True)
            dhat = dn * g_ref[...]
            dx_ref[...] = dh1_ref[...] + rstd * (dhat - xh * jnp.mean(dhat * xh, axis=-1, keepdims=True))

    row = pl.BlockSpec((tm, D), lambda i, j: (i, 0))
    vec = pl.BlockSpec((1, D), lambda i, j: (0, 0))
    return _call(
        body, name="in_bwd", grid=(s // tm, nj),
        in_specs=[pl.BlockSpec((tm, W_IN_COLS), lambda i, j: (i, j)),
                  pl.BlockSpec((None, D, W_IN_COLS), lambda i, j: (j, 0, 0)), row, row, vec],
        out_specs=[row, vec],
        out_shape=[SDS((s, D), F32), SDS((1, D), F32)],
        scratch_shapes=[pltpu.VMEM((tm, D), F32)],
        params=_cparams(("arbitrary", "arbitrary"), 48), args=(dz, w_in_g, x, dh1, g_mix), comm=comm)


def _wgrad(name, a, b, nblk, a_split, b_split):
    s = a.shape[0]
    ts = min(TM_ROWS, s)
    a_w = a.shape[1] // nblk if a_split else a.shape[1]
    b_w = b.shape[1] // nblk if b_split else b.shape[1]

    def body(a_ref, b_ref, o_ref, acc_s):
        t = pl.program_id(1)

        @pl.when(t == 0)
        def _():
            acc_s[...] = jnp.zeros_like(acc_s)

        acc_s[...] += _dot_tn(a_ref[...].astype(BF16), b_ref[...].astype(BF16))

        @pl.when(t == pl.num_programs(1) - 1)
        def _():
            o_ref[...] = acc_s[...].astype(BF16)

    return pl.pallas_call(
        body, name=name, grid=(nblk, s // ts),
        in_specs=[pl.BlockSpec((ts, a_w), (lambda k, t: (t, k)) if a_split else (lambda k, t: (t, 0))),
                  pl.BlockSpec((ts, b_w), (lambda k, t: (t, k)) if b_split else (lambda k, t: (t, 0)))],
        out_specs=pl.BlockSpec((None, a_w, b_w), lambda k, t: (k, 0, 0)),
        out_shape=pltpu.HBM((nblk, a_w, b_w), BF16),
        scratch_shapes=[pltpu.VMEM((a_w, b_w), F32)],
        compiler_params=_cparams(("arbitrary", "arbitrary"), 48),
    )(a, b)


def _wgrad_t(name, a_t, b, nblk, a_split, b_split, tokens):
    s = b.shape[0]
    ts = min(tokens, s)
    a_w = a_t.shape[0] // nblk if a_split else a_t.shape[0]
    b_w = b.shape[1] // nblk if b_split else b.shape[1]

    def body(a_ref, b_ref, o_ref, acc_s):
        t = pl.program_id(1)

        @pl.when(t == 0)
        def _():
            acc_s[...] = jnp.zeros_like(acc_s)

        acc_s[...] += _dot(a_ref[...], b_ref[...].astype(BF16))

        @pl.when(t == pl.num_programs(1) - 1)
        def _():
            o_ref[...] = acc_s[...].astype(BF16)

    return pl.pallas_call(
        body, name=name, grid=(nblk, s // ts),
        in_specs=[pl.BlockSpec((a_w, ts), (lambda k, t: (k, t)) if a_split else (lambda k, t: (0, t))),
                  pl.BlockSpec((ts, b_w), (lambda k, t: (t, k)) if b_split else (lambda k, t: (t, 0)))],
        out_specs=pl.BlockSpec((None, a_w, b_w), lambda k, t: (k, 0, 0)),
        out_shape=pltpu.HBM((nblk, a_w, b_w), BF16),
        scratch_shapes=[pltpu.VMEM((a_w, b_w), F32)],
        compiler_params=_cparams(("arbitrary", "arbitrary"), 48),
    )(a_t, b)


def _place():
    x, y, c = lax.axis_index("x"), lax.axis_index("y"), lax.axis_index("c")
    return x, y, c


def _other_chips(x, y):
    return [(x, 1 - y, 2 * x + 1 - y), (1 - x, y, 2 * (1 - x) + y), (1 - x, 1 - y, 2 * (1 - x) + 1 - y)]


class _Plan:
    def __init__(self, arrays, out_shape, sems, start, finish, middle=None):
        self.arrays, self.out_shape, self.sems, self.start, self.finish = arrays, out_shape, sems, start, finish
        self.middle = middle


def _gather_plan(shards):
    n = len(shards)

    def copies(ins, outs, sems):
        send_sems, recv_sems, local_sems = sems
        x, y, c = _place()
        chip = 2 * x + y
        me = 2 * chip + c
        sib = (x, y, 1 - c)
        chips = _other_chips(x, y)

        def rc(k, t, src, blk, to):
            return pltpu.make_async_remote_copy(
                src_ref=src, dst_ref=outs[t].at[blk], send_sem=send_sems.at[k * n + t],
                recv_sem=recv_sems.at[k * n + t], device_id=to, device_id_type=MESH)

        (yx, yy, y_chip), (xx, xy, x_chip), _ = chips
        local = [pltpu.make_async_copy(ins[t], outs[t].at[me], local_sems.at[t]) for t in range(n)]
        sends = ([rc(0, t, ins[t], me, sib) for t in range(n)] + [rc(1, t, ins[t], me, (yx, yy, c)) for t in range(n)]
                 + [rc(2, t, ins[t], me, (xx, xy, c)) for t in range(n)])
        passed = [[rc(4 + j, t, outs[t].at[2 * pc + c], 2 * pc + c, sib) for t in range(n)]
                  for j, (_, _, pc) in enumerate(chips)]
        relays = [[rc(3, t, outs[t].at[2 * y_chip + c], 2 * y_chip + c, (xx, xy, c)) for t in range(n)],
                  [rc(3, t, outs[t].at[2 * x_chip + c], 2 * x_chip + c, (yx, yy, c)) for t in range(n)]]
        return rc, local, sends, passed, relays, chips, chip, c, sib

    def start(ins, outs, sems):
        _, local, sends, _, _, _, _, _, _ = copies(ins, outs, sems)
        for cp in local + sends:
            cp.start()

    def middle(ins, outs, sems):
        rc, _, _, passed, relays, chips, _, c, sib = copies(ins, outs, sems)
        for j in range(2):
            for t in range(n):
                rc(1 + j, t, ins[t], 2 * chips[j][2] + c, sib).wait_recv()
        for j in range(2):
            for cp in passed[j]:
                cp.start()

            @pl.when(c == j)
            def _():
                for cp in relays[j]:
                    cp.start()

    def finish(ins, outs, sems):
        rc, local, sends, passed, relays, chips, chip, c, sib = copies(ins, outs, sems)
        far = 2 * chips[2][2] + c
        for t in range(n):
            rc(3, t, ins[t], far, sib).wait_recv()
        for cp in passed[2]:
            cp.start()
        for t in range(n):
            rc(0, t, ins[t], 2 * chip + 1 - c, sib).wait_recv()
        for j, (px, py, pc) in enumerate(chips):
            for t in range(n):
                rc(4 + j, t, ins[t], 2 * pc + 1 - c, sib).wait_recv()
        for cp in sends + passed[0] + passed[1] + passed[2]:
            cp.wait_send()
        for j in range(2):
            @pl.when(c == j)
            def _():
                for cp in relays[j]:
                    cp.wait_send()
        for cp in local:
            cp.wait()

    return _Plan(list(shards), [SDS((N_SLOT,) + tuple(a.shape), a.dtype) for a in shards],
                 [pltpu.SemaphoreType.DMA((7 * n,)), pltpu.SemaphoreType.DMA((7 * n,)),
                  pltpu.SemaphoreType.DMA((n,))], start, finish, middle)


def _sibling_plan(grads, whole=()):
    n, m = len(grads), len(whole)

    def copies(ins, outs, sems):
        send_sems, recv_sems = sems
        x, y, c = _place()
        sib = (x, y, 1 - c)

        def rc(t, src, dst):
            return pltpu.make_async_remote_copy(src_ref=src, dst_ref=dst, send_sem=send_sems.at[t],
                                                recv_sem=recv_sems.at[t], device_id=sib, device_id_type=MESH)
        return rc, c

    def start(ins, outs, sems):
        rc, c = copies(ins, outs, sems)
        for t in range(n):
            for j in range(4):
                rc(t, ins[t].at[2 * j + 1 - c], outs[t].at[j]).start()
        for t in range(n, n + m):
            rc(t, ins[t], outs[t]).start()

    def finish(ins, outs, sems):
        rc, _ = copies(ins, outs, sems)
        for t in range(n):
            rc(t, ins[t].at[pl.ds(0, 4)], outs[t]).wait()
        for t in range(n, n + m):
            rc(t, ins[t], outs[t]).wait()

    return _Plan(list(grads) + list(whole),
                 [SDS((4,) + tuple(g.shape[1:]), g.dtype) for g in grads] + [SDS(a.shape, a.dtype) for a in whole],
                 [pltpu.SemaphoreType.DMA((n + m,)), pltpu.SemaphoreType.DMA((n + m,))], start, finish)


def _chips_plan(parts, whole=()):
    n, m = len(parts), len(whole)

    def src_of(ins, t, pc):
        return ins[t].at[pc] if t < n else ins[t]

    def local_copies(ins, outs, sems, chip):
        return [pltpu.make_async_copy(src_of(ins, t, chip), outs[t].at[chip], sems[2].at[t]) for t in range(n + m)]

    def start(ins, outs, sems):
        send_sems, recv_sems, _ = sems
        x, y, c = _place()
        chip = 2 * x + y
        for cp in local_copies(ins, outs, sems, chip):
            cp.start()
        for px, py, pc in _other_chips(x, y):
            for t in range(n + m):
                pltpu.make_async_remote_copy(src_ref=src_of(ins, t, pc), dst_ref=outs[t].at[chip],
                                             send_sem=send_sems.at[t], recv_sem=recv_sems.at[t],
                                             device_id=(px, py, c), device_id_type=MESH).start()

    def finish(ins, outs, sems):
        send_sems, recv_sems, _ = sems
        x, y, c = _place()
        for t in range(n + m):
            three = outs[t].at[pl.ds(0, 3)]
            pltpu.make_async_remote_copy(src_ref=three, dst_ref=three, send_sem=send_sems.at[t],
                                         recv_sem=recv_sems.at[t], device_id=(x, y, c), device_id_type=MESH).wait()
        for cp in local_copies(ins, outs, sems, 2 * x + y):
            cp.wait()

    return _Plan(list(parts) + list(whole),
                 [SDS(p.shape, p.dtype) for p in parts] + [SDS((4,) + tuple(a.shape), a.dtype) for a in whole],
                 [pltpu.SemaphoreType.DMA((n + m,)), pltpu.SemaphoreType.DMA((n + m,)),
                  pltpu.SemaphoreType.DMA((n + m,))], start, finish)


def _exchange_plan(arr):
    def peers(x, y, c):
        flip = lambda v, f: 1 - v if f else v
        return [(flip(x, fx), flip(y, fy), flip(c, fc))
                for fx in (0, 1) for fy in (0, 1) for fc in (0, 1) if fx or fy or fc]

    def start(ins, outs, sems):
        x, y, c = _place()
        me = 4 * x + 2 * y + c
        pltpu.make_async_copy(ins[0], outs[0].at[me], sems[2].at[0]).start()
        for to in peers(x, y, c):
            pltpu.make_async_remote_copy(src_ref=ins[0], dst_ref=outs[0].at[me], send_sem=sems[0].at[0],
                                         recv_sem=sems[1].at[0], device_id=to, device_id_type=MESH).start()

    def finish(ins, outs, sems):
        x, y, c = _place()
        seven = outs[0].at[pl.ds(0, 7)]
        pltpu.make_async_remote_copy(src_ref=seven, dst_ref=seven, send_sem=sems[0].at[0], recv_sem=sems[1].at[0],
                                     device_id=(x, y, c), device_id_type=MESH).wait()
        pltpu.make_async_copy(ins[0], outs[0].at[4 * x + 2 * y + c], sems[2].at[0]).wait()

    return _Plan([arr], [SDS((N_SLOT,) + tuple(arr.shape), arr.dtype)],
                 [pltpu.SemaphoreType.DMA((1,)), pltpu.SemaphoreType.DMA((1,)), pltpu.SemaphoreType.DMA((1,))],
                 start, finish)


def _join(*plans):
    def cut(seq, sizes):
        out, at = [], 0
        for k in sizes:
            out.append(seq[at:at + k])
            at += k
        return out

    n_arr = [len(p.arrays) for p in plans]
    n_sem = [len(p.sems) for p in plans]

    def start(ins, outs, sems):
        for p, i, o, s in zip(plans, cut(ins, n_arr), cut(outs, n_arr), cut(sems, n_sem)):
            p.start(i, o, s)

    def finish(ins, outs, sems):
        for p, i, o, s in zip(plans, cut(ins, n_arr), cut(outs, n_arr), cut(sems, n_sem)):
            p.finish(i, o, s)

    def middle(ins, outs, sems):
        for p, i, o, s in zip(plans, cut(ins, n_arr), cut(outs, n_arr), cut(sems, n_sem)):
            if p.middle is not None:
                p.middle(i, o, s)

    return _Plan([a for p in plans for a in p.arrays], [o for p in plans for o in p.out_shape],
                 [s for p in plans for s in p.sems], start, finish,
                 middle if any(p.middle is not None for p in plans) else None)


def _run_plan(name, plan):
    k = len(plan.arrays)

    def body(*refs):
        ins, outs, sems = refs[:k], refs[k:2 * k], refs[2 * k:]
        plan.start(ins, outs, sems)
        if plan.middle is not None:
            plan.middle(ins, outs, sems)
        plan.finish(ins, outs, sems)

    return pl.pallas_call(
        body, name=name, in_specs=[ANY] * k, out_specs=[ANY] * k, out_shape=plan.out_shape,
        scratch_shapes=plan.sems, compiler_params=pltpu.CompilerParams(has_side_effects=True),
    )(*plan.arrays)


def _call(body, *, name, grid, in_specs, out_specs, out_shape, scratch_shapes, params, args, comm=None,
          aliases=None, prefetch=()):
    aliases = aliases or {}
    n_pre = len(prefetch)

    def launch(fn, ins_specs, outs_specs, outs_shape, scratch, operands):
        spec = pltpu.PrefetchScalarGridSpec(num_scalar_prefetch=n_pre, grid=grid, in_specs=ins_specs,
                                            out_specs=outs_specs, scratch_shapes=scratch)
        return pl.pallas_call(fn, name=name, grid_spec=spec, out_shape=outs_shape, compiler_params=params,
                              input_output_aliases=aliases)(*prefetch, *[_small_in_hbm(a) for a in operands])

    if comm is None:
        return list(launch(body, in_specs, out_specs, out_shape, scratch_shapes, args)), []
    n_in, n_out, n_scr, k = len(in_specs), len(out_specs), len(scratch_shapes), len(comm.arrays)

    def wrapped(*refs):
        pre, refs = refs[:n_pre], refs[n_pre:]
        ins = refs[:n_in]
        c_in = refs[n_in:n_in + k]
        outs = refs[n_in + k:n_in + k + n_out]
        c_out = refs[n_in + k + n_out:n_in + 2 * k + n_out]
        scr = refs[n_in + 2 * k + n_out:n_in + 2 * k + n_out + n_scr]
        sems = refs[n_in + 2 * k + n_out + n_scr:]
        step, steps = pl.program_id(0), grid[0]
        for d in range(1, len(grid)):
            step, steps = step * grid[d] + pl.program_id(d), steps * grid[d]

        @pl.when(step == 0)
        def _():
            comm.start(c_in, c_out, sems)

        if comm.middle is not None:
            @pl.when(step == (3 * steps) // 4)
            def _():
                comm.middle(c_in, c_out, sems)

        body(*pre, *ins, *outs, *scr)

        @pl.when(step == steps - 1)
        def _():
            comm.finish(c_in, c_out, sems)

    res = launch(wrapped, list(in_specs) + [ANY] * k, list(out_specs) + [ANY] * k,
                 list(out_shape) + list(comm.out_shape), list(scratch_shapes) + list(comm.sems),
                 tuple(args) + tuple(comm.arrays))
    return list(res[:n_out]), list(res[n_out:])


def _row_tile(rows):
    for t in (512, 256, 128, 64, 32, 16, 8):
        if rows % t == 0:
            return t
    return rows


def _pair_sum(name, g8, recv4, core):
    _, rows, cols = recv4.shape
    tr = _row_tile(rows)
    g42 = g8.reshape(4, 2, rows, cols)

    def body(c_ref, g_ref, r_ref, o_ref):
        del c_ref
        o_ref[...] = (g_ref[...].astype(F32) + r_ref[...].astype(F32)).astype(o_ref.dtype)

    return pl.pallas_call(
        body, name=name,
        grid_spec=pltpu.PrefetchScalarGridSpec(
            num_scalar_prefetch=1, grid=(4, rows // tr),
            in_specs=[pl.BlockSpec((None, None, tr, cols), lambda j, i, c_ref: (j, c_ref[0], i, 0)),
                      pl.BlockSpec((None, tr, cols), lambda j, i, c_ref: (j, i, 0))],
            out_specs=pl.BlockSpec((None, tr, cols), lambda j, i, c_ref: (j, i, 0))),
        out_shape=pltpu.HBM(recv4.shape, g8.dtype),
        compiler_params=_cparams(("arbitrary", "arbitrary"), 32),
    )(core, g42, recv4)


def _add2(name, a, b):
    rows, cols = a.shape
    tr = _row_tile(rows)

    def body(a_ref, b_ref, o_ref):
        o_ref[...] = a_ref[...] + b_ref[...]

    blk = pl.BlockSpec((tr, cols), lambda i: (i, 0))
    return pl.pallas_call(body, name=name, grid=(rows // tr,), in_specs=[blk, blk], out_specs=blk,
                          out_shape=pltpu.HBM(a.shape, a.dtype),
                          compiler_params=_cparams(("arbitrary",), 32))(a, b)


def _sum_terms(name, terms):
    k, rows, cols = terms.shape
    tr = _row_tile(rows)

    def body(r_ref, o_ref):
        acc = r_ref[0]
        for q in range(1, k):
            acc = acc + r_ref[q]
        o_ref[...] = acc

    return pl.pallas_call(body, name=name, grid=(rows // tr,),
                          in_specs=[pl.BlockSpec((k, tr, cols), lambda i: (0, i, 0))],
                          out_specs=pl.BlockSpec((tr, cols), lambda i: (i, 0)),
                          out_shape=SDS((rows, cols), terms.dtype),
                          compiler_params=_cparams(("arbitrary",), 32))(terms)


def _adam_update(g, w, m, v):
    c1 = 1.0 / (1.0 - ADAM_B1 ** ADAM_STEP)
    c2 = 1.0 / (1.0 - ADAM_B2 ** ADAM_STEP)
    mn = ADAM_B1 * m + (1.0 - ADAM_B1) * g
    vn = ADAM_B2 * v + (1.0 - ADAM_B2) * (g * g)
    delta = (-ADAM_LR) * ((mn * c1) / (jnp.sqrt(vn * c2) + ADAM_EPS) + ADAM_WD * w)
    return delta, mn, vn


def _adamw_many(name, gs, ws, ms, vs):
    n = len(gs)

    def body(*refs):
        for p in range(n):
            g, w, m, v = (refs[q * n + p][...] for q in range(4))
            d, mn, vn = _adam_update(g, w, m, v)
            refs[4 * n + p][...] = d
            refs[5 * n + p][...] = mn
            refs[6 * n + p][...] = vn

    full = [pl.BlockSpec(w.shape, lambda i: (0, 0)) for w in ws]
    shapes = [SDS(w.shape, F32) for w in ws]
    res = pl.pallas_call(body, name=name, grid=(1,), in_specs=full * 4, out_specs=full * 3, out_shape=shapes * 3,
                         compiler_params=_cparams(("arbitrary",), 32),
                         )(*[_small_in_hbm(a) for a in (*gs, *ws, *ms, *vs)])
    return [(res[p], res[n + p], res[2 * n + p]) for p in range(n)]


def _adamw(name, terms, w, m, v):
    k, rows, cols = terms.shape
    tr = rows // 4 if rows % 64 == 0 else rows

    def body(t_ref, w_ref, m_ref, v_ref, g_ref, d_ref, mo_ref, vo_ref):
        g = t_ref[0].astype(F32)
        for q in range(1, k):
            g = g + t_ref[q].astype(F32)
        g_ref[...] = g
        d_ref[...], mo_ref[...], vo_ref[...] = _adam_update(g, w_ref[...], m_ref[...], v_ref[...])

    blk = pl.BlockSpec((tr, cols), lambda i: (i, 0))
    return pl.pallas_call(body, name=name, grid=(rows // tr,),
                          in_specs=[pl.BlockSpec((k, tr, cols), lambda i: (0, i, 0)), blk, blk, blk],
                          out_specs=[blk] * 4, out_shape=[SDS((rows, cols), F32)] * 4,
                          compiler_params=_cparams(("arbitrary",), 40),
                          )(*[pltpu.with_memory_space_constraint(a, pltpu.HBM) for a in (terms, w, m, v)])


def kernel(x, norm_mix_g, w_in, conv_w, conv_b, w_rgate, b_rgate, w_igate, b_igate, lru_lambda, w_out_a, sgu_ln_g, sgu_ln_b, sgu_w_s, sgu_b_s, w_out_b, w_out, norm_mlp_g, w_up, w_down, norm_final_g, loss_target, m_norm_mix_g, m_w_in, m_conv_w, m_conv_b, m_w_rgate, m_b_rgate, m_w_igate, m_b_igate, m_lru_lambda, m_w_out_a, m_sgu_ln_g, m_sgu_ln_b, m_sgu_w_s, m_sgu_b_s, m_w_out_b, m_w_out, m_norm_mlp_g, m_w_up, m_w_down, m_norm_final_g, v_norm_mix_g, v_w_in, v_conv_w, v_conv_b, v_w_rgate, v_b_rgate, v_w_igate, v_b_igate, v_lru_lambda, v_w_out_a, v_sgu_ln_g, v_sgu_ln_b, v_sgu_w_s, v_sgu_b_s, v_w_out_b, v_w_out, v_norm_mlp_g, v_w_up, v_w_down, v_norm_final_g):
    cx, cy, cc = _place()
    me = 4 * cx + 2 * cy + cc
    core = jnp.reshape(cc, (1,)).astype(jnp.int32)
    xs = x[0]
    tgt = loss_target[0]
    s = xs.shape[0]

    gate_shard = jnp.stack([w_rgate[0], w_igate[0]]).astype(BF16).reshape(2 * HEADS * 32, HEAD_DIM)
    vec_shard = jnp.concatenate([conv_w[0], b_rgate[0], b_igate[0]], axis=1)
    vec_shard = jnp.pad(vec_shard, ((0, 4), (0, 256 - vec_shard.shape[1])))
    shards = [w_in[0].astype(BF16), w_out_a[0].astype(BF16), w_out_b[0].astype(BF16), w_out[0].astype(BF16),
              w_up[0].astype(BF16), w_down[0].astype(BF16), gate_shard, vec_shard]
    (z, n1_t, w_in_g), (gate_g, vec_g) = _in_proj(xs, norm_mix_g, shards[0], _slot_order(cx, cy, cc),
                                                comm=_gather_plan(shards[6:8]))
    gates = gate_g.reshape(N_SLOT, 2, HEADS, 32, HEAD_DIM).transpose(1, 2, 0, 3, 4).reshape(2, HEADS, HEAD_DIM, HEAD_DIM)
    w_r_f, w_i_f = gates[0], gates[1]
    conv_w_f = vec_g[:, 0:4, 0:128].transpose(1, 0, 2).reshape(CONV_K, D)
    b_r_f = vec_g[:, 0:4, 128:160].transpose(1, 0, 2).reshape(1, D)
    b_i_f = vec_g[:, 0:4, 160:192].transpose(1, 0, 2).reshape(1, D)
    b_s_t = jnp.transpose(sgu_b_s[0])

    (ya, hs), (w_oa_g, w_ob_g, w_out_g) = _branch_a_fwd(
        z, conv_w_f, conv_b, w_r_f, b_r_f, w_i_f, b_i_f, lru_lambda, comm=_gather_plan(shards[1:4]))
    w_oa_f = w_oa_g.reshape(D, D)
    w_ob_f = w_ob_g.reshape(D, D)
    w_out_f = w_out_g.reshape(D, D)
    (yb,), (w_up_g,) = _branch_b_fwd(z, sgu_ln_g, sgu_ln_b, sgu_w_s[0], b_s_t, comm=_gather_plan(shards[4:5]))
    (pa, pb, merged, h1), (w_down_g,) = _merge_out(ya, yb, z, xs, w_oa_f, w_ob_f, w_out_f,
                                                   comm=_gather_plan(shards[5:6]))
    w_down_f = w_down_g.reshape(N_SLOT * FF_COLS, D)
    gf2 = norm_final_g.reshape(1, D)
    r_act, act_t, n2_t, dh2, loss_acc, d_gfin = _mlp_fwd(h1, norm_mlp_g, w_up_g, w_down_f, gf2, tgt)

    def pair(names, grads, recv):
        return [_pair_sum("pair_sum_" + nm, g, r, core) for nm, g, r in zip(names, grads, recv)]

    g_down = _wgrad_t("wgrad_down", act_t, dh2, N_SLOT // 2, True, False, 2048)
    g_down = g_down.reshape(N_SLOT, FF_COLS, D)
    (df, dh1, d_gmlp), (r_down,) = _mlp_bwd(dh2, r_act, w_down_f, w_up_g, h1, norm_mlp_g,
                                            comm=_sibling_plan([g_down]))
    (p_down,) = pair(["down"], [g_down], [r_down])
    g_up = _wgrad_t("wgrad_up", n2_t, df, N_SLOT, False, True, 4096)
    g_out = _wgrad("wgrad_out", merged, dh1, 1, False, False).reshape(N_SLOT, D // N_SLOT, D)
    (dz, dpa, dpb, dya, dyb), (got_down, r_up, r_out) = _merge_bwd(
        dh1, z, pa, pb, w_out_f, w_oa_f, w_ob_f, comm=_join(_chips_plan([p_down]), _sibling_plan([g_up, g_out])))
    p_up, p_out = pair(["up", "out"], [g_up, g_out], [r_up, r_out])
    g_oa = _wgrad("wgrad_out_a", ya, dpa, 1, False, False).reshape(N_SLOT, D // N_SLOT, D)
    g_ob = _wgrad("wgrad_out_b", yb, dpb, 1, False, False).reshape(N_SLOT, D // N_SLOT, D)
    (dz, d_ws, d_bs, d_ln), (got_up, r_oa, r_ob) = _branch_b_bwd(
        dz, dyb, z, sgu_ln_g, sgu_ln_b, sgu_w_s[0], b_s_t,
        comm=_join(_chips_plan([p_up]), _sibling_plan([g_oa, g_ob])))
    p_oa, p_ob = pair(["out_a", "out_b"], [g_oa, g_ob], [r_oa, r_ob])
    (dz, d_vec, d_wr, d_wi), (got_out, got_oa, got_ob) = _branch_a_bwd(
        dz, dya, z, hs, conv_w_f, conv_b, w_r_f, b_r_f, w_i_f, b_i_f, lru_lambda,
        comm=_chips_plan([p_out, p_oa, p_ob]))
    g_in = _wgrad_t("wgrad_in", n1_t, dz, N_SLOT, False, True, 4096)
    g_gate = jnp.stack([d_wr, d_wi]).reshape(2, HEADS, N_SLOT, 32, HEAD_DIM).transpose(2, 0, 1, 3, 4)
    g_gate = g_gate.reshape(N_SLOT, 2 * HEADS * 32, HEAD_DIM).astype(BF16)

    d_bs_row = jnp.pad(d_bs[:, :, 0].reshape(1, GROUPS * CHUNK), ((0, 0), (0, D - GROUPS * CHUNK)))
    vecs = jnp.concatenate([d_vec, jnp.concatenate([d_ln[0:2], d_gmlp, d_gfin, d_bs_row, jnp.zeros((3, D), F32)])])
    d_ws2 = d_ws.reshape(GROUPS * CHUNK, CHUNK)
    r_in, r_gate, r_vecs, r_ws = _run_plan("rs_sibling_in", _sibling_plan([g_in, g_gate], [vecs, d_ws2]))
    p_in, p_gate = pair(["in", "gate"], [g_in, g_gate], [r_in, r_gate])
    vecs_chip = _add2("pair_sum_vecs", vecs, r_vecs)
    ws_chip = _add2("pair_sum_ws", d_ws2, r_ws)
    (dx, d_gmix), (got_in, got_gate, got_vecs, got_ws) = _in_bwd(
        dz, w_in_g, xs, dh1, norm_mix_g, comm=_chips_plan([p_in, p_gate], [vecs_chip, ws_chip]))
    vecs_sum = _sum_terms("sum_vecs", got_vecs)
    last = jnp.concatenate([d_gmix, jnp.pad(loss_acc[0:1], ((0, 0), (0, D - 128))), jnp.zeros((6, D), F32)])
    (last_all,) = _run_plan("exchange_last", _exchange_plan(last))
    last_sum = _sum_terms("sum_last", last_all)
    loss = last_sum[1, 0]
    got = [got_in, got_oa, got_ob, got_out, got_up, got_down, got_gate]

    def step(nm, terms, w, m, v, rows, cols):
        g, d, mn, vn = _adamw("adamw_" + nm, terms.reshape(4, rows, cols), w.reshape(rows, cols),
                              m.reshape(rows, cols), v.reshape(rows, cols))
        return [a.reshape(w.shape) for a in (g, d, mn, vn)]

    o_in = step("in", got[0], w_in, m_w_in, v_w_in, D, W_IN_COLS)
    o_oa = step("out_a", got[1], w_out_a, m_w_out_a, v_w_out_a, D // N_SLOT, D)
    o_ob = step("out_b", got[2], w_out_b, m_w_out_b, v_w_out_b, D // N_SLOT, D)
    o_out = step("out", got[3], w_out, m_w_out, v_w_out, D // N_SLOT, D)
    o_up = step("up", got[4], w_up, m_w_up, v_w_up, D, FF_COLS)
    o_down = step("down", got[5], w_down, m_w_down, v_w_down, FF_COLS, D)
    gate_w = jnp.stack([w_rgate[0], w_igate[0]]).reshape(2 * HEADS * 32, HEAD_DIM)
    gate_m = jnp.stack([m_w_rgate[0], m_w_igate[0]]).reshape(2 * HEADS * 32, HEAD_DIM)
    gate_v = jnp.stack([v_w_rgate[0], v_w_igate[0]]).reshape(2 * HEADS * 32, HEAD_DIM)
    o_gate = _adamw("adamw_gate", got[6], gate_w, gate_m, gate_v)
    o_gate = [a.reshape(2, 1, HEADS, 32, HEAD_DIM) for a in o_gate]
    o_wr = [a[0] for a in o_gate]
    o_wi = [a[1] for a in o_gate]

    def own(full, width):
        return lax.dynamic_slice_in_dim(full, me * width, width, axis=1)

    small_g = {
        "norm_mix_g": last_sum[0:1], "conv_w": own(vecs_sum[0:4], 128), "conv_b": vecs_sum[4:5],
        "b_rgate": own(vecs_sum[5:6].reshape(HEADS, HEAD_DIM), 32),
        "b_igate": own(vecs_sum[6:7].reshape(HEADS, HEAD_DIM), 32),
        "lru_lambda": vecs_sum[7:8], "sgu_ln_g": vecs_sum[8:9], "sgu_ln_b": vecs_sum[9:10],
        "norm_mlp_g": vecs_sum[10:11], "norm_final_g": vecs_sum[11:12],
        "sgu_b_s": vecs_sum[12, 0:GROUPS * CHUNK].reshape(GROUPS, CHUNK),
    }
    small_w = {"norm_mix_g": (norm_mix_g, m_norm_mix_g, v_norm_mix_g), "conv_w": (conv_w, m_conv_w, v_conv_w),
               "conv_b": (conv_b, m_conv_b, v_conv_b), "b_rgate": (b_rgate, m_b_rgate, v_b_rgate),
               "b_igate": (b_igate, m_b_igate, v_b_igate), "lru_lambda": (lru_lambda, m_lru_lambda, v_lru_lambda),
               "sgu_ln_g": (sgu_ln_g, m_sgu_ln_g, v_sgu_ln_g), "sgu_ln_b": (sgu_ln_b, m_sgu_ln_b, v_sgu_ln_b),
               "norm_mlp_g": (norm_mlp_g, m_norm_mlp_g, v_norm_mlp_g),
               "norm_final_g": (norm_final_g, m_norm_final_g, v_norm_final_g),
               "sgu_b_s": (sgu_b_s, m_sgu_b_s, v_sgu_b_s), "sgu_w_s": (sgu_w_s, m_sgu_w_s, v_sgu_w_s)}
    order = list(small_g)
    as2d = lambda k, a: a.reshape(small_g[k].shape)
    upd = _adamw_many("adamw_small", [small_g[k] for k in order], *[[as2d(k, small_w[k][q]) for k in order]
                                                                     for q in range(3)])
    o_small = {k: [a.reshape(small_w[k][0].shape) for a in (small_g[k],) + u] for k, u in zip(order, upd)}
    ws3 = [a[0].reshape(GROUPS * CHUNK, CHUNK) for a in small_w.pop("sgu_w_s")]
    o_small["sgu_w_s"] = [a.reshape(sgu_w_s.shape) for a in _adamw("adamw_ws", got_ws, *ws3)]

    per_weight = {"norm_mix_g": o_small["norm_mix_g"], "w_in": o_in, "conv_w": o_small["conv_w"],
                  "conv_b": o_small["conv_b"], "w_rgate": o_wr, "b_rgate": o_small["b_rgate"], "w_igate": o_wi,
                  "b_igate": o_small["b_igate"], "lru_lambda": o_small["lru_lambda"], "w_out_a": o_oa,
                  "sgu_ln_g": o_small["sgu_ln_g"], "sgu_ln_b": o_small["sgu_ln_b"], "sgu_w_s": o_small["sgu_w_s"],
                  "sgu_b_s": o_small["sgu_b_s"], "w_out_b": o_ob, "w_out": o_out, "norm_mlp_g": o_small["norm_mlp_g"],
                  "w_up": o_up, "w_down": o_down, "norm_final_g": o_small["norm_final_g"]}
    names_w = list(per_weight)
    return (loss, dx[None], *[per_weight[k][0] for k in names_w], *[per_weight[k][1] for k in names_w],
            *[per_weight[k][2] for k in names_w], *[per_weight[k][3] for k in names_w])
```

```python
import jax
import jax.numpy as jnp
from jax import lax
from jax.experimental import pallas as pl
from jax.experimental.pallas import tpu as pltpu

F32 = jnp.float32
BF16 = jnp.bfloat16
SDS = jax.ShapeDtypeStruct
MESH = pl.DeviceIdType.MESH
ANY = pl.BlockSpec(memory_space=pltpu.HBM)

D = 1024
N_SLOT = 8
W_IN_COLS = 768
FF_COLS = 512
HEADS, HEAD_DIM = 4, 256
GROUPS, GROUP_DIM = 4, 256
CHUNK = 128
CONV_K = 4
NORM_EPS = 1e-6
LN_EPS = 1e-5
LRU_C = 8.0
ADAM_LR, ADAM_B1, ADAM_B2, ADAM_EPS, ADAM_WD, ADAM_STEP = 0.001, 0.9, 0.999, 1e-08, 0.01, 10

TM_ROWS = 1024
TM_MERGE = 512
T_BRANCH_A = 512
T_BRANCH_B = 256
MiB = 1024 * 1024
SMALL_OPERAND = 16 * 1024

_GELU_C = 0.7978845608028654
_GELU_A = 0.044715


def _small_in_hbm(a):
    return pltpu.with_memory_space_constraint(a, pltpu.HBM) if a.size <= SMALL_OPERAND else a


def _cparams(sem, vmem_mib):
    return pltpu.CompilerParams(dimension_semantics=sem, vmem_limit_bytes=vmem_mib * MiB)


def _gelu(x):
    t = jnp.tanh(_GELU_C * (x + _GELU_A * x * x * x))
    return 0.5 * x * (1.0 + t)


def _gelu_and_grad(x):
    x2 = x * x
    t = jnp.tanh(_GELU_C * x * (1.0 + _GELU_A * x2))
    g = 0.5 * x * (1.0 + t)
    dg = 0.5 * (1.0 + t) + 0.5 * x * (1.0 - t * t) * _GELU_C * (1.0 + 3.0 * _GELU_A * x2)
    return g, dg


def _softplus(x):
    return jnp.maximum(x, 0.0) + jnp.log1p(jnp.exp(-jnp.abs(x)))


def _dot(a, b):
    return jnp.dot(a, b, preferred_element_type=F32)


def _dot_nt(a, b):
    return lax.dot_general(a, b, (((1,), (1,)), ((), ())), preferred_element_type=F32)


def _dot_tn(a, b):
    return lax.dot_general(a, b, (((0,), (0,)), ((), ())), preferred_element_type=F32)


def _rows_shifted(prev8, cur, k):
    ext = jnp.concatenate([prev8, cur], axis=0)
    return pltpu.roll(ext, k, 0)[8:]


def _rows_advanced(cur, next8, k):
    t = cur.shape[0]
    ext = jnp.concatenate([cur, next8], axis=0)
    return pltpu.roll(ext, t + 8 - k, 0)[:t]


def _first_second(x, y, c):
    ny, nx, far = _other_chips(x, y)
    pick = lambda a, b: a * (1 - c) + b * c
    first = tuple(pick(a, b) for a, b in zip(ny, nx))
    second = tuple(pick(b, a) for a, b in zip(ny, nx))
    return first, second, far


def _slot_order(x, y, c):
    chip = 2 * x + y
    first, second, far = _first_second(x, y, c)
    order = [2 * chip + c, 2 * chip + 1 - c, 2 * first[2] + c, 2 * second[2] + 1 - c, 2 * second[2] + c,
             2 * first[2] + 1 - c, 2 * far[2] + c, 2 * far[2] + 1 - c]
    return jnp.stack(order).astype(jnp.int32)


def _in_proj(x, g_mix, w_in_own, order, comm=None):
    s = x.shape[0]
    tm = min(TM_ROWS, s)
    ni = s // tm

    def body(order_ref, x_ref, g_ref, own_ref, z_ref, nt_ref, wg_ref, n_s, w_s, send_sems, recv_sems, local_sems):
        j, i = pl.program_id(0), pl.program_id(1)
        px, py, c = _place()
        chip = 2 * px + py
        me = 2 * chip + c
        sib = (px, py, 1 - c)
        chips = _other_chips(px, py)

        def rc(k, src, blk, to):
            return pltpu.make_async_remote_copy(src_ref=src, dst_ref=w_s.at[blk], send_sem=send_sems.at[k],
                                                recv_sem=recv_sems.at[k], device_id=to, device_id_type=MESH)

        del chips
        first, second, far = _first_second(px, py, c)
        blocks = [2 * first[2] + c, 2 * second[2] + c, 2 * far[2] + c]
        own_in = pltpu.make_async_copy(own_ref, w_s.at[me], local_sems.at[0])
        to_first = rc(1, own_ref, me, (first[0], first[1], c))
        to_second = rc(2, own_ref, me, (second[0], second[1], c))
        relay = rc(3, w_s.at[blocks[0]], blocks[0], (second[0], second[1], c))
        sends = [rc(0, own_ref, me, sib), to_first, to_second, relay]
        passed = [rc(4 + q, w_s.at[blk], blk, sib) for q, blk in enumerate(blocks)]
        keep = pltpu.make_async_copy(w_s, wg_ref, local_sems.at[1])

        @pl.when((i == 0) & (j == 0))
        def _():
            own_in.start()
            sends[0].start()
            to_first.start()
            own_in.wait()

        @pl.when((i == 0) & (j == 1))
        def _():
            rc(0, own_ref, 2 * chip + 1 - c, sib).wait_recv()

        for q, blk in enumerate(blocks):
            @pl.when((i == 0) & (j == 2 + 2 * q))
            def _():
                rc(1 + q, own_ref, blk, sib).wait_recv()
                passed[q].start()
                if q == 0:
                    to_second.start()
                    relay.start()

            @pl.when((i == 0) & (j == 3 + 2 * q))
            def _():
                rc(4 + q, own_ref, order_ref[j], sib).wait_recv()

        rows = pl.ds(pl.multiple_of(i * tm, tm), tm)

        @pl.when(j == 0)
        def _():
            xv = x_ref[...]
            rstd = lax.rsqrt(jnp.mean(xv * xv, axis=-1, keepdims=True) + NORM_EPS)
            nb = (xv * rstd * g_ref[...]).astype(BF16)
            n_s[rows, :] = nb
            nt_ref[...] = nb.T

        z_ref[...] = _dot(n_s[rows, :], w_s[order_ref[j]]).astype(BF16)

        @pl.when((i == 0) & (j == N_SLOT - 1))
        def _():
            keep.start()

        @pl.when((i == ni - 1) & (j == N_SLOT - 1))
        def _():
            for cp in sends + passed:
                cp.wait_send()
            keep.wait()

    first_pass = lambda j, i, o: (jnp.where(j == 0, i, ni - 1), 0)
    (z, n1, w_in_g), extra = _call(
        body, name="in_proj", grid=(N_SLOT, ni), prefetch=(order,),
        in_specs=[pl.BlockSpec((tm, D), first_pass),
                  pl.BlockSpec((1, D), lambda j, i, o: (0, 0)), ANY],
        out_specs=[pl.BlockSpec((tm, W_IN_COLS), lambda j, i, o: (i, o[j])),
                   pl.BlockSpec((D, tm), lambda j, i, o: (0, jnp.where(j == 0, i, ni - 1))), ANY],
        out_shape=[SDS((s, N_SLOT * W_IN_COLS), BF16), SDS((D, s), BF16), SDS((N_SLOT, D, W_IN_COLS), BF16)],
        scratch_shapes=[pltpu.VMEM((s, D), BF16), pltpu.VMEM((N_SLOT, D, W_IN_COLS), BF16),
                        pltpu.SemaphoreType.DMA((7,)), pltpu.SemaphoreType.DMA((7,)), pltpu.SemaphoreType.DMA((2,))],
        params=_cparams(("arbitrary", "arbitrary"), 56), args=(x, g_mix, w_in_own), comm=comm)
    return (z, n1, w_in_g), extra


def _lru_gates(xc, xcb, wr_ref, br, wi_ref, bi, sp_lam, a_s, b_s, r_s=None, i_s=None, m_s=None):
    for h in range(HEADS):
        sl = slice(h * HEAD_DIM, (h + 1) * HEAD_DIM)
        r = jax.nn.sigmoid(_dot(xcb[:, sl], wr_ref[h]) + br[:, sl])
        ig = jax.nn.sigmoid(_dot(xcb[:, sl], wi_ref[h]) + bi[:, sl])
        log_a = (-LRU_C) * r * sp_lam[:, sl]
        a = jnp.exp(log_a)
        mult = jnp.sqrt(-jnp.tanh(log_a) * (a * a + 1.0))
        a_s[:, sl] = a
        b_s[:, sl] = xc[:, sl] * ig * mult
        if r_s is not None:
            r_s[:, sl] = r
            i_s[:, sl] = ig
            m_s[:, sl] = mult


def _conv_fwd(xa, prev8, cw, cb):
    taps = [xa] + [_rows_shifted(prev8, xa, k) for k in range(1, CONV_K)]
    xc = cb + cw[0:1, :] * xa
    for k in range(1, CONV_K):
        xc = xc + cw[k:k + 1, :] * taps[k]
    return xc, taps


def _branch_a_fwd(z, conv_w, conv_b, w_r, b_r, w_i, b_i, lam, comm=None):
    s = z.shape[0]
    ta = min(T_BRANCH_A, s)
    per16 = ta // 16

    def body(xa_ref, xp_ref, ga_ref, cw_ref, cb_ref, wr_ref, br_ref, wi_ref, bi_ref, lam_ref,
             ya_ref, hs_ref, a_s, b_s, h_s, carry_s):
        i = pl.program_id(0)

        @pl.when(i == 0)
        def _():
            carry_s[...] = jnp.zeros_like(carry_s)

        xa = xa_ref[...].astype(F32)
        prev8 = jnp.where(i > 0, xp_ref[...].astype(F32)[8:16], 0.0)
        xc, _ = _conv_fwd(xa, prev8, cw_ref[...], cb_ref[...])
        sp_lam = _softplus(-lam_ref[...])
        _lru_gates(xc, xc.astype(BF16), wr_ref, br_ref[...], wi_ref, bi_ref[...], sp_lam, a_s, b_s)

        row = lax.broadcasted_iota(jnp.int32, (8, D), 0)

        def group(g, carry):
            off = pl.multiple_of(g * 8, 8)
            a8 = a_s[pl.ds(off, 8), :]
            b8 = b_s[pl.ds(off, 8), :]
            for d in (1, 2, 4):
                a_sh = jnp.where(row >= d, pltpu.roll(a8, d, 0), 1.0)
                b_sh = jnp.where(row >= d, pltpu.roll(b8, d, 0), 0.0)
                b8 = a8 * b_sh + b8
                a8 = a8 * a_sh
            h8 = b8 + a8 * carry
            h_s[pl.ds(off, 8), :] = h8
            return jnp.broadcast_to(h8[7:8, :], (8, D))

        carry_s[...] = lax.fori_loop(0, ta // 8, group, carry_s[...])
        hs = h_s[...]
        hs_ref[...] = hs.astype(BF16)
        ya_ref[...] = (hs * _gelu(ga_ref[...].astype(F32))).astype(BF16)

    vec = pl.BlockSpec((1, D), lambda i: (0, 0))
    gate = pl.BlockSpec((HEADS, HEAD_DIM, HEAD_DIM), lambda i: (0, 0, 0))
    return _call(
        body, name="branch_a_fwd", grid=(s // ta,),
        in_specs=[pl.BlockSpec((ta, D), lambda i: (i, 0)),
                  pl.BlockSpec((16, D), lambda i: (jnp.maximum(i * per16 - 1, 0), 0)),
                  pl.BlockSpec((ta, D), lambda i: (i, 1)),
                  pl.BlockSpec((CONV_K, D), lambda i: (0, 0)), vec, gate, vec, gate, vec, vec],
        out_specs=[pl.BlockSpec((ta, D), lambda i: (i, 0)), pl.BlockSpec((ta, D), lambda i: (i, 0))],
        out_shape=[SDS((s, D), BF16), SDS((s, D), BF16)],
        scratch_shapes=[pltpu.VMEM((ta, D), F32), pltpu.VMEM((ta, D), F32), pltpu.VMEM((ta, D), F32),
                        pltpu.VMEM((8, D), F32)],
        params=_cparams(("arbitrary",), 40), args=(z, z, z, conv_w, conv_b, w_r, b_r, w_i, b_i, lam), comm=comm)


def _sgu_common(ub, vb, lg, lb, with_grad):
    if with_grad:
        u, du = _gelu_and_grad(ub)
        v, dv = _gelu_and_grad(vb)
    else:
        u, v, du, dv = _gelu(ub), _gelu(vb), None, None
    mu = jnp.mean(v, axis=-1, keepdims=True)
    vc = v - mu
    rstd = lax.rsqrt(jnp.mean(vc * vc, axis=-1, keepdims=True) + LN_EPS)
    vhat = vc * rstd
    vln = vhat * lg + lb
    return u, du, dv, rstd, vhat, vln


def _masked_ws(ws_ref):
    t = lax.broadcasted_iota(jnp.int32, (CHUNK, CHUNK), 0)
    c = lax.broadcasted_iota(jnp.int32, (CHUNK, CHUNK), 1)
    keep = c <= t
    return [jnp.where(keep, ws_ref[g], 0.0).astype(BF16) for g in range(GROUPS)]


def _branch_b_fwd(z, ln_g, ln_b, w_s, b_s_t, comm=None):
    s = z.shape[0]
    tb = min(T_BRANCH_B, s)

    def body(ub_ref, vb_ref, lg_ref, lb_ref, ws_ref, bs_ref, yb_ref):
        u, _, _, _, _, vln = _sgu_common(ub_ref[...].astype(F32), vb_ref[...].astype(F32),
                                         lg_ref[...], lb_ref[...], False)
        vlnb = vln.astype(BF16)
        wm = _masked_ws(ws_ref)
        bs = bs_ref[...]
        for c in range(tb // CHUNK):
            rs = slice(c * CHUNK, (c + 1) * CHUNK)
            for g in range(GROUPS):
                cs = slice(g * GROUP_DIM, (g + 1) * GROUP_DIM)
                sp = _dot(wm[g], vlnb[rs, cs]) + bs[:, g:g + 1]
                yb_ref[rs, cs] = (u[rs, cs] * sp).astype(BF16)

    vec = pl.BlockSpec((1, D), lambda i: (0, 0))
    return _call(
        body, name="branch_b_fwd", grid=(s // tb,),
        in_specs=[pl.BlockSpec((tb, D), lambda i: (i, 2)), pl.BlockSpec((tb, D), lambda i: (i, 3)), vec, vec,
                  pl.BlockSpec((GROUPS, CHUNK, CHUNK), lambda i: (0, 0, 0)),
                  pl.BlockSpec((CHUNK, GROUPS), lambda i: (0, 0))],
        out_specs=[pl.BlockSpec((tb, D), lambda i: (i, 0))],
        out_shape=[SDS((s, D), BF16)], scratch_shapes=[],
        params=_cparams(("arbitrary",), 40), args=(z, z, ln_g, ln_b, w_s, b_s_t), comm=comm)


def _merge_out(ya, yb, z, x, w_oa, w_ob, w_out, comm=None):
    s = x.shape[0]
    tm = min(TM_MERGE, s)

    def body(ya_ref, yb_ref, ma_ref, mb_ref, x_ref, woa_ref, wob_ref, wo_ref, pa_ref, pb_ref, mg_ref, h1_ref):
        pa = _dot(ya_ref[...], woa_ref[...])
        pb = _dot(yb_ref[...], wob_ref[...])
        merged = (jax.nn.sigmoid(ma_ref[...].astype(F32)) * pa
                  + jax.nn.sigmoid(mb_ref[...].astype(F32)) * pb).astype(BF16)
        pa_ref[...] = pa.astype(BF16)
        pb_ref[...] = pb.astype(BF16)
        mg_ref[...] = merged
        h1_ref[...] = x_ref[...] + _dot(merged, wo_ref[...])

    row = pl.BlockSpec((tm, D), lambda i: (i, 0))
    wsp = pl.BlockSpec((D, D), lambda i: (0, 0))
    return _call(
        body, name="merge_out", grid=(s // tm,),
        in_specs=[row, row, pl.BlockSpec((tm, D), lambda i: (i, 4)), pl.BlockSpec((tm, D), lambda i: (i, 5)),
                  row, wsp, wsp, wsp],
        out_specs=[row, row, row, row],
        out_shape=[SDS((s, D), BF16), SDS((s, D), BF16), SDS((s, D), BF16), SDS((s, D), F32)], scratch_shapes=[],
        params=_cparams(("arbitrary",), 48), args=(ya, yb, z, z, x, w_oa, w_ob, w_out), comm=comm)


def _mlp_fwd(h1, g_mlp, w_up_g, w_down, g_fin, tgt):
    s = h1.shape[0]
    tm = min(TM_ROWS, s)
    nj = N_SLOT

    def body(h1_ref, gm_ref, wu_ref, wd_ref, gf_ref, t_ref, r_ref, at_ref, n2t_ref, dh2_ref, loss_ref, dgf_ref,
             n2_s, acc_s):
        i, j = pl.program_id(0), pl.program_id(1)

        @pl.when(j == 0)
        def _():
            hv = h1_ref[...]
            rstd = lax.rsqrt(jnp.mean(hv * hv, axis=-1, keepdims=True) + NORM_EPS)
            nb = (hv * rstd * gm_ref[...]).astype(BF16)
            n2_s[...] = nb
            n2t_ref[...] = nb.T
            acc_s[...] = jnp.zeros_like(acc_s)

        @pl.when((i == 0) & (j == 0))
        def _():
            loss_ref[...] = jnp.zeros_like(loss_ref)
            dgf_ref[...] = jnp.zeros_like(dgf_ref)

        r = jnp.maximum(_dot(n2_s[...], wu_ref[...]), 0.0)
        r_ref[...] = r.astype(BF16)
        act = (r * r).astype(BF16)
        at_ref[...] = act.T
        acc_s[...] += _dot(act, wd_ref[...])

        @pl.when(j == nj - 1)
        def _():
            h2 = h1_ref[...] + acc_s[...]
            rstd = lax.rsqrt(jnp.mean(h2 * h2, axis=-1, keepdims=True) + NORM_EPS)
            hh = h2 * rstd
            gf = gf_ref[...]
            e = hh * gf - t_ref[...]
            loss_ref[...] += jnp.sum(e * e) * (0.5 / D)
            dy = e * (1.0 / D)
            dgf_ref[...] += jnp.sum(dy * hh, axis=0, keepdims=True)
            dhh = dy * gf
            dh2_ref[...] = rstd * (dhh - hh * jnp.mean(dhh * hh, axis=-1, keepdims=True))

    row = pl.BlockSpec((tm, D), lambda i, j: (i, 0))
    vec = pl.BlockSpec((1, D), lambda i, j: (0, 0))
    return pl.pallas_call(
        body, name="mlp_fwd", grid=(s // tm, nj),
        in_specs=[row, vec, pl.BlockSpec((None, D, FF_COLS), lambda i, j: (j, 0, 0)),
                  pl.BlockSpec((FF_COLS, D), lambda i, j: (j, 0)), vec, row],
        out_specs=[pl.BlockSpec((tm, FF_COLS), lambda i, j: (i, j)), pl.BlockSpec((FF_COLS, tm), lambda i, j: (j, i)),
                   pl.BlockSpec((D, tm), lambda i, j: (0, i)), row, pl.BlockSpec((8, 128), lambda i, j: (0, 0)), vec],
        out_shape=[SDS((s, nj * FF_COLS), BF16), SDS((nj * FF_COLS, s), BF16), SDS((D, s), BF16), SDS((s, D), F32),
                   SDS((8, 128), F32), SDS((1, D), F32)],
        scratch_shapes=[pltpu.VMEM((tm, D), BF16), pltpu.VMEM((tm, D), F32)],
        compiler_params=_cparams(("arbitrary", "arbitrary"), 52),
    )(h1, _small_in_hbm(g_mlp), w_up_g, w_down, _small_in_hbm(g_fin), tgt)


def _mlp_bwd(dh2, r, w_down, w_up_g, h1, g_mlp, comm=None):
    s = h1.shape[0]
    tm = min(TM_ROWS, s)
    nj = N_SLOT

    def body(dh2_ref, r_ref, wd_ref, wu_ref, h1_ref, gm_ref, df_ref, dh1_ref, dgm_ref, dh2b_s, acc_s):
        i, j = pl.program_id(0), pl.program_id(1)

        @pl.when(j == 0)
        def _():
            dh2b_s[...] = dh2_ref[...].astype(BF16)
            acc_s[...] = jnp.zeros_like(acc_s)

        @pl.when((i == 0) & (j == 0))
        def _():
            dgm_ref[...] = jnp.zeros_like(dgm_ref)

        d_act = _dot_nt(dh2b_s[...], wd_ref[...])
        df = (d_act * (2.0 * r_ref[...].astype(F32))).astype(BF16)
        df_ref[...] = df
        acc_s[...] += _dot_nt(df, wu_ref[...])

        @pl.when(j == nj - 1)
        def _():
            hv = h1_ref[...]
            rstd = lax.rsqrt(jnp.mean(hv * hv, axis=-1, keepdims=True) + NORM_EPS)
            hh = hv * rstd
            dn2 = acc_s[...]
            dgm_ref[...] += jnp.sum(dn2 * hh, axis=0, keepdims=True)
            dhat = dn2 * gm_ref[...]
            dh1_ref[...] = dh2_ref[...] + rstd * (dhat - hh * jnp.mean(dhat * hh, axis=-1, keepdims=True))

    row = pl.BlockSpec((tm, D), lambda i, j: (i, 0))
    vec = pl.BlockSpec((1, D), lambda i, j: (0, 0))
    ffb = pl.BlockSpec((tm, FF_COLS), lambda i, j: (i, j))
    return _call(
        body, name="mlp_bwd", grid=(s // tm, nj),
        in_specs=[row, ffb, pl.BlockSpec((FF_COLS, D), lambda i, j: (j, 0)),
                  pl.BlockSpec((None, D, FF_COLS), lambda i, j: (j, 0, 0)), row, vec],
        out_specs=[ffb, row, vec],
        out_shape=[SDS((s, nj * FF_COLS), BF16), SDS((s, D), F32), SDS((1, D), F32)],
        scratch_shapes=[pltpu.VMEM((tm, D), BF16), pltpu.VMEM((tm, D), F32)],
        params=_cparams(("arbitrary", "arbitrary"), 52), args=(dh2, r, w_down, w_up_g, h1, g_mlp), comm=comm)


def _merge_bwd(dh1, z, pa, pb, w_out, w_oa, w_ob, comm=None):
    s = dh1.shape[0]
    tm = min(TM_MERGE, s)

    def body(dh1_ref, ma_ref, mb_ref, pa_ref, pb_ref, wo_ref, woa_ref, wob_ref,
             dz_ref, dpa_ref, dpb_ref, dya_ref, dyb_ref):
        dm = _dot_nt(dh1_ref[...].astype(BF16), wo_ref[...])
        sa = jax.nn.sigmoid(ma_ref[...].astype(F32))
        sb = jax.nn.sigmoid(mb_ref[...].astype(F32))
        dpa = (dm * sa).astype(BF16)
        dpb = (dm * sb).astype(BF16)
        dz_ref[:, 0:D] = (dm * pa_ref[...].astype(F32) * sa * (1.0 - sa)).astype(BF16)
        dz_ref[:, D:2 * D] = (dm * pb_ref[...].astype(F32) * sb * (1.0 - sb)).astype(BF16)
        dpa_ref[...] = dpa
        dpb_ref[...] = dpb
        dya_ref[...] = _dot_nt(dpa, woa_ref[...]).astype(BF16)
        dyb_ref[...] = _dot_nt(dpb, wob_ref[...]).astype(BF16)

    row = pl.BlockSpec((tm, D), lambda i: (i, 0))
    wsp = pl.BlockSpec((D, D), lambda i: (0, 0))
    return _call(
        body, name="merge_bwd", grid=(s // tm,),
        in_specs=[row, pl.BlockSpec((tm, D), lambda i: (i, 4)), pl.BlockSpec((tm, D), lambda i: (i, 5)),
                  row, row, wsp, wsp, wsp],
        out_specs=[pl.BlockSpec((tm, 2 * D), lambda i: (i, 2)), row, row, row, row],
        out_shape=[SDS((s, 6 * D), BF16)] + [SDS((s, D), BF16)] * 4, scratch_shapes=[],
        params=_cparams(("arbitrary",), 48), args=(dh1, z, z, pa, pb, w_out, w_oa, w_ob), comm=comm)


def _branch_b_bwd(dz, dyb, z, ln_g, ln_b, w_s, b_s_t, comm=None):
    s = z.shape[0]
    tb = min(T_BRANCH_B, s)

    def body(dz_in, dyb_ref, ub_ref, vb_ref, lg_ref, lb_ref, ws_ref, bs_ref,
             dz_ref, dws_ref, dbs_ref, dln_ref, du_s, dvln_s):
        del dz_in

        @pl.when(pl.program_id(0) == 0)
        def _():
            dws_ref[...] = jnp.zeros_like(dws_ref)
            dbs_ref[...] = jnp.zeros_like(dbs_ref)
            dln_ref[...] = jnp.zeros_like(dln_ref)

        lg = lg_ref[...]
        u, du, dv, rstd, vhat, vln = _sgu_common(ub_ref[...].astype(F32), vb_ref[...].astype(F32),
                                                 lg, lb_ref[...], True)
        vlnb = vln.astype(BF16)
        dyb_v = dyb_ref[...].astype(F32)
        wm = _masked_ws(ws_ref)
        keep = (lax.broadcasted_iota(jnp.int32, (CHUNK, CHUNK), 1)
                <= lax.broadcasted_iota(jnp.int32, (CHUNK, CHUNK), 0))
        bs = bs_ref[...]
        for c in range(tb // CHUNK):
            rs = slice(c * CHUNK, (c + 1) * CHUNK)
            for g in range(GROUPS):
                cs = slice(g * GROUP_DIM, (g + 1) * GROUP_DIM)
                v_blk = vlnb[rs, cs]
                sp = _dot(wm[g], v_blk) + bs[:, g:g + 1]
                d_sp = dyb_v[rs, cs] * u[rs, cs]
                d_spb = d_sp.astype(BF16)
                du_s[rs, cs] = dyb_v[rs, cs] * sp
                dvln_s[rs, cs] = _dot_tn(wm[g], d_spb)
                dws_ref[g] += jnp.where(keep, _dot_nt(d_spb, v_blk), 0.0)
                dbs_ref[g] += jnp.broadcast_to(jnp.sum(d_sp, axis=-1, keepdims=True), (CHUNK, CHUNK))
        dvln = dvln_s[...]
        dln_ref[0:1, :] += jnp.sum(dvln * vhat, axis=0, keepdims=True)
        dln_ref[1:2, :] += jnp.sum(dvln, axis=0, keepdims=True)
        dvh = dvln * lg
        d_v = rstd * (dvh - jnp.mean(dvh, axis=-1, keepdims=True)
                      - vhat * jnp.mean(dvh * vhat, axis=-1, keepdims=True))
        dz_ref[:, 0:D] = (du_s[...] * du).astype(BF16)
        dz_ref[:, D:2 * D] = (d_v * dv).astype(BF16)

    vec = pl.BlockSpec((1, D), lambda i: (0, 0))
    sq = pl.BlockSpec((GROUPS, CHUNK, CHUNK), lambda i: (0, 0, 0))
    return _call(
        body, name="branch_b_bwd", grid=(s // tb,),
        in_specs=[ANY, pl.BlockSpec((tb, D), lambda i: (i, 0)),
                  pl.BlockSpec((tb, D), lambda i: (i, 2)), pl.BlockSpec((tb, D), lambda i: (i, 3)), vec, vec, sq,
                  pl.BlockSpec((CHUNK, GROUPS), lambda i: (0, 0))],
        out_specs=[pl.BlockSpec((tb, 2 * D), lambda i: (i, 1)), sq, sq, pl.BlockSpec((8, D), lambda i: (0, 0))],
        out_shape=[SDS(dz.shape, BF16), SDS((GROUPS, CHUNK, CHUNK), F32), SDS((GROUPS, CHUNK, CHUNK), F32),
                   SDS((8, D), F32)],
        scratch_shapes=[pltpu.VMEM((tb, D), F32), pltpu.VMEM((tb, D), F32)], aliases={0: 0},
        params=_cparams(("arbitrary",), 40), args=(dz, dyb, z, z, ln_g, ln_b, w_s, b_s_t), comm=comm)


def _branch_a_bwd(dz, dya, z, hs, conv_w, conv_b, w_r, b_r, w_i, b_i, lam, comm=None):
    s = z.shape[0]
    ta = min(T_BRANCH_A, s)
    nb = s // ta
    per16 = ta // 16

    def body(dz_in, dya_ref, xa_ref, xp_ref, ga_ref, hs_ref, hp_ref, cw_ref, cb_ref, wr_ref, br_ref, wi_ref,
             bi_ref, lam_ref, dz_ref, vec_ref, dwr_ref, dwi_ref,
             a_s, b_s, h_s, r_s, i_s, m_s, dcar_s, acar_s, dxc_s):
        del dz_in
        i = pl.program_id(0)
        blk = nb - 1 - i

        @pl.when(i == 0)
        def _():
            dcar_s[...] = jnp.zeros_like(dcar_s)
            acar_s[...] = jnp.zeros_like(acar_s)
            dxc_s[...] = jnp.zeros_like(dxc_s)
            vec_ref[...] = jnp.zeros_like(vec_ref)
            dwr_ref[...] = jnp.zeros_like(dwr_ref)
            dwi_ref[...] = jnp.zeros_like(dwi_ref)

        cw = cw_ref[...]
        lam_v = lam_ref[...]
        xa = xa_ref[...].astype(F32)
        prev8 = jnp.where(blk > 0, xp_ref[...].astype(F32)[8:16], 0.0)
        xc, taps = _conv_fwd(xa, prev8, cw, cb_ref[...])
        xcb = xc.astype(BF16)
        sp_lam = _softplus(-lam_v)
        _lru_gates(xc, xcb, wr_ref, br_ref[...], wi_ref, bi_ref[...], sp_lam, a_s, b_s, r_s, i_s, m_s)

        hs_v = hs_ref[...].astype(F32)
        hprev8 = jnp.where(blk > 0, hp_ref[...].astype(F32)[8:16], 0.0)
        h_m1 = _rows_shifted(hprev8, hs_v, 1)
        gg, dgg = _gelu_and_grad(ga_ref[...].astype(F32))
        dya_v = dya_ref[...].astype(F32)
        dz_ref[:, D:2 * D] = (dya_v * hs_v * dgg).astype(BF16)

        a_v = a_s[...]
        a_s[...] = _rows_advanced(a_v, acar_s[...], 1)
        b_s[...] = dya_v * gg

        row = lax.broadcasted_iota(jnp.int32, (8, D), 0)
        ng = ta // 8

        def group(gi, carry):
            off = pl.multiple_of((ng - 1 - gi) * 8, 8)
            c8 = a_s[pl.ds(off, 8), :]
            d8 = b_s[pl.ds(off, 8), :]
            for d in (1, 2, 4):
                c_sh = jnp.where(row < 8 - d, pltpu.roll(c8, 8 - d, 0), 1.0)
                d_sh = jnp.where(row < 8 - d, pltpu.roll(d8, 8 - d, 0), 0.0)
                d8 = c8 * d_sh + d8
                c8 = c8 * c_sh
            dh8 = d8 + c8 * carry
            h_s[pl.ds(off, 8), :] = dh8
            return jnp.broadcast_to(dh8[0:1, :], (8, D))

        dcar_s[...] = lax.fori_loop(0, ng, group, dcar_s[...])
        acar_s[...] = jnp.broadcast_to(a_v[0:1, :], (8, D))

        dbx = h_s[...]
        r_v, i_v, m_v = r_s[...], i_s[...], m_s[...]
        d_mult = dbx * xc * i_v
        d_loga = dbx * h_m1 * a_v - d_mult * (a_v * a_v) / m_v
        d_pr = d_loga * ((-LRU_C) * sp_lam) * r_v * (1.0 - r_v)
        d_pi = dbx * xc * m_v * i_v * (1.0 - i_v)
        vec_ref[7:8, :] += jnp.sum(d_loga * r_v, axis=0, keepdims=True) * (LRU_C * jax.nn.sigmoid(-lam_v))
        vec_ref[5:6, :] += jnp.sum(d_pr, axis=0, keepdims=True)
        vec_ref[6:7, :] += jnp.sum(d_pi, axis=0, keepdims=True)
        d_prb = d_pr.astype(BF16)
        d_pib = d_pi.astype(BF16)
        h_s[...] = dbx * i_v * m_v
        for h in range(HEADS):
            sl = slice(h * HEAD_DIM, (h + 1) * HEAD_DIM)
            h_s[:, sl] += _dot_nt(d_prb[:, sl], wr_ref[h]) + _dot_nt(d_pib[:, sl], wi_ref[h])
            dwr_ref[h] += _dot_tn(xcb[:, sl], d_prb[:, sl])
            dwi_ref[h] += _dot_tn(xcb[:, sl], d_pib[:, sl])
        d_xc = h_s[...]
        vec_ref[4:5, :] += jnp.sum(d_xc, axis=0, keepdims=True)
        vec_ref[0:1, :] += jnp.sum(d_xc * xa, axis=0, keepdims=True)
        d_xa = cw[0:1, :] * d_xc
        nxt = dxc_s[...]
        for k in range(1, CONV_K):
            vec_ref[k:k + 1, :] += jnp.sum(d_xc * taps[k], axis=0, keepdims=True)
            d_xa = d_xa + cw[k:k + 1, :] * _rows_advanced(d_xc, nxt, k)
        dz_ref[:, 0:D] = d_xa.astype(BF16)
        dxc_s[...] = d_xc[0:8, :]

    vec = pl.BlockSpec((1, D), lambda i: (0, 0))
    gate = pl.BlockSpec((HEADS, HEAD_DIM, HEAD_DIM), lambda i: (0, 0, 0))
    cur = lambda c: pl.BlockSpec((ta, D), lambda i: (nb - 1 - i, c))
    before = lambda c: pl.BlockSpec((16, D), lambda i: (jnp.maximum((nb - 1 - i) * per16 - 1, 0), c))
    return _call(
        body, name="branch_a_bwd", grid=(nb,),
        in_specs=[ANY, cur(0), cur(0), before(0), cur(1), cur(0), before(0),
                  pl.BlockSpec((CONV_K, D), lambda i: (0, 0)), vec, gate, vec, gate, vec, vec],
        out_specs=[pl.BlockSpec((ta, 2 * D), lambda i: (nb - 1 - i, 0)), pl.BlockSpec((8, D), lambda i: (0, 0)),
                   gate, gate],
        out_shape=[SDS(dz.shape, BF16), SDS((8, D), F32), SDS((HEADS, HEAD_DIM, HEAD_DIM), F32),
                   SDS((HEADS, HEAD_DIM, HEAD_DIM), F32)],
        scratch_shapes=[pltpu.VMEM((ta, D), F32)] * 6 + [pltpu.VMEM((8, D), F32)] * 3, aliases={0: 0},
        params=_cparams(("arbitrary",), 48),
        args=(dz, dya, z, z, z, hs, hs, conv_w, conv_b, w_r, b_r, w_i, b_i, lam), comm=comm)


def _in_bwd(dz, w_in_g, x, dh1, g_mix, comm=None):
    s = x.shape[0]
    tm = min(TM_ROWS, s)
    nj = N_SLOT

    def body(dz_ref, w_ref, x_ref, dh1_ref, g_ref, dx_ref, dg_ref, acc_s):
        i, j = pl.program_id(0), pl.program_id(1)

        @pl.when(j == 0)
        def _():
            acc_s[...] = jnp.zeros_like(acc_s)

        @pl.when((i == 0) & (j == 0))
        def _():
            dg_ref[...] = jnp.zeros_like(dg_ref)

        acc_s[...] += _dot_nt(dz_ref[...], w_ref[...])

        @pl.when(j == nj - 1)
        def _():
            xv = x_ref[...]
            rstd = lax.rsqrt(jnp.mean(xv * xv, axis=-1, keepdims=True) + NORM_EPS)
            xh = xv * rstd
            dn = acc_s[...]
            dg_ref[...] += jnp.sum(dn * xh, axis=0, keepdims=True)
            dhat = dn * g_ref[...]
            dx_ref[...] = dh1_ref[...] + rstd * (dhat - xh * jnp.mean(dhat * xh, axis=-1, keepdims=True))

    row = pl.BlockSpec((tm, D), lambda i, j: (i, 0))
    vec = pl.BlockSpec((1, D), lambda i, j: (0, 0))
    return _call(
        body, name="in_bwd", grid=(s // tm, nj),
        in_specs=[pl.BlockSpec((tm, W_IN_COLS), lambda i, j: (i, j)),
                  pl.BlockSpec((None, D, W_IN_COLS), lambda i, j: (j, 0, 0)), row, row, vec],
        out_specs=[row, vec],
        out_shape=[SDS((s, D), F32), SDS((1, D), F32)],
        scratch_shapes=[pltpu.VMEM((tm, D), F32)],
        params=_cparams(("arbitrary", "arbitrary"), 48), args=(dz, w_in_g, x, dh1, g_mix), comm=comm)


def _wgrad(name, a, b, nblk, a_split, b_split):
    s = a.shape[0]
    ts = min(TM_ROWS, s)
    a_w = a.shape[1] // nblk if a_split else a.shape[1]
    b_w = b.shape[1] // nblk if b_split else b.shape[1]

    def body(a_ref, b_ref, o_ref, acc_s):
        t = pl.program_id(1)

        @pl.when(t == 0)
        def _():
            acc_s[...] = jnp.zeros_like(acc_s)

        acc_s[...] += _dot_tn(a_ref[...].astype(BF16), b_ref[...].astype(BF16))

        @pl.when(t == pl.num_programs(1) - 1)
        def _():
            o_ref[...] = acc_s[...].astype(BF16)

    return pl.pallas_call(
        body, name=name, grid=(nblk, s // ts),
        in_specs=[pl.BlockSpec((ts, a_w), (lambda k, t: (t, k)) if a_split else (lambda k, t: (t, 0))),
                  pl.BlockSpec((ts, b_w), (lambda k, t: (t, k)) if b_split else (lambda k, t: (t, 0)))],
        out_specs=pl.BlockSpec((None, a_w, b_w), lambda k, t: (k, 0, 0)),
        out_shape=SDS((nblk, a_w, b_w), BF16),
        scratch_shapes=[pltpu.VMEM((a_w, b_w), F32)],
        compiler_params=_cparams(("arbitrary", "arbitrary"), 48),
    )(a, b)


def _wgrad_t(name, a_t, b, nblk, a_split, b_split, tokens):
    s = b.shape[0]
    ts = min(tokens, s)
    a_w = a_t.shape[0] // nblk if a_split else a_t.shape[0]
    b_w = b.shape[1] // nblk if b_split else b.shape[1]

    def body(a_ref, b_ref, o_ref, acc_s):
        t = pl.program_id(1)

        @pl.when(t == 0)
        def _():
            acc_s[...] = jnp.zeros_like(acc_s)

        acc_s[...] += _dot(a_ref[...], b_ref[...].astype(BF16))

        @pl.when(t == pl.num_programs(1) - 1)
        def _():
            o_ref[...] = acc_s[...].astype(BF16)

    return pl.pallas_call(
        body, name=name, grid=(nblk, s // ts),
        in_specs=[pl.BlockSpec((a_w, ts), (lambda k, t: (k, t)) if a_split else (lambda k, t: (0, t))),
                  pl.BlockSpec((ts, b_w), (lambda k, t: (t, k)) if b_split else (lambda k, t: (t, 0)))],
        out_specs=pl.BlockSpec((None, a_w, b_w), lambda k, t: (k, 0, 0)),
        out_shape=SDS((nblk, a_w, b_w), BF16),
        scratch_shapes=[pltpu.VMEM((a_w, b_w), F32)],
        compiler_params=_cparams(("arbitrary", "arbitrary"), 48),
    )(a_t, b)


def _place():
    x, y, c = lax.axis_index("x"), lax.axis_index("y"), lax.axis_index("c")
    return x, y, c


def _other_chips(x, y):
    return [(x, 1 - y, 2 * x + 1 - y), (1 - x, y, 2 * (1 - x) + y), (1 - x, 1 - y, 2 * (1 - x) + 1 - y)]


class _Plan:
    def __init__(self, arrays, out_shape, sems, start, finish, middle=None):
        self.arrays, self.out_shape, self.sems, self.start, self.finish = arrays, out_shape, sems, start, finish
        self.middle = middle


def _gather_plan(shards):
    n = len(shards)

    def copies(ins, outs, sems):
        send_sems, recv_sems, local_sems = sems
        x, y, c = _place()
        chip = 2 * x + y
        me = 2 * chip + c
        sib = (x, y, 1 - c)
        chips = _other_chips(x, y)

        def rc(k, t, src, blk, to):
            return pltpu.make_async_remote_copy(
                src_ref=src, dst_ref=outs[t].at[blk], send_sem=send_sems.at[k * n + t],
                recv_sem=recv_sems.at[k * n + t], device_id=to, device_id_type=MESH)

        (yx, yy, y_chip), (xx, xy, x_chip), _ = chips
        local = [pltpu.make_async_copy(ins[t], outs[t].at[me], local_sems.at[t]) for t in range(n)]
        sends = ([rc(0, t, ins[t], me, sib) for t in range(n)] + [rc(1, t, ins[t], me, (yx, yy, c)) for t in range(n)]
                 + [rc(2, t, ins[t], me, (xx, xy, c)) for t in range(n)])
        passed = [[rc(4 + j, t, outs[t].at[2 * pc + c], 2 * pc + c, sib) for t in range(n)]
                  for j, (_, _, pc) in enumerate(chips)]
        relays = [[rc(3, t, outs[t].at[2 * y_chip + c], 2 * y_chip + c, (xx, xy, c)) for t in range(n)],
                  [rc(3, t, outs[t].at[2 * x_chip + c], 2 * x_chip + c, (yx, yy, c)) for t in range(n)]]
        return rc, local, sends, passed, relays, chips, chip, c, sib

    def start(ins, outs, sems):
        _, local, sends, _, _, _, _, _, _ = copies(ins, outs, sems)
        for cp in local + sends:
            cp.start()

    def middle(ins, outs, sems):
        rc, _, _, passed, relays, chips, _, c, sib = copies(ins, outs, sems)
        for j in range(2):
            for t in range(n):
                rc(1 + j, t, ins[t], 2 * chips[j][2] + c, sib).wait_recv()
        for j in range(2):
            for cp in passed[j]:
                cp.start()

            @pl.when(c == j)
            def _():
                for cp in relays[j]:
                    cp.start()

    def finish(ins, outs, sems):
        rc, local, sends, passed, relays, chips, chip, c, sib = copies(ins, outs, sems)
        far = 2 * chips[2][2] + c
        for t in range(n):
            rc(3, t, ins[t], far, sib).wait_recv()
        for cp in passed[2]:
            cp.start()
        for t in range(n):
            rc(0, t, ins[t], 2 * chip + 1 - c, sib).wait_recv()
        for j, (px, py, pc) in enumerate(chips):
            for t in range(n):
                rc(4 + j, t, ins[t], 2 * pc + 1 - c, sib).wait_recv()
        for cp in sends + passed[0] + passed[1] + passed[2]:
            cp.wait_send()
        for j in range(2):
            @pl.when(c == j)
            def _():
                for cp in relays[j]:
                    cp.wait_send()
        for cp in local:
            cp.wait()

    return _Plan(list(shards), [SDS((N_SLOT,) + tuple(a.shape), a.dtype) for a in shards],
                 [pltpu.SemaphoreType.DMA((7 * n,)), pltpu.SemaphoreType.DMA((7 * n,)),
                  pltpu.SemaphoreType.DMA((n,))], start, finish, middle)


def _sibling_plan(grads, whole=()):
    n, m = len(grads), len(whole)

    def copies(ins, outs, sems):
        send_sems, recv_sems = sems
        x, y, c = _place()
        sib = (x, y, 1 - c)

        def rc(t, src, dst):
            return pltpu.make_async_remote_copy(src_ref=src, dst_ref=dst, send_sem=send_sems.at[t],
                                                recv_sem=recv_sems.at[t], device_id=sib, device_id_type=MESH)
        return rc, c

    def start(ins, outs, sems):
        rc, c = copies(ins, outs, sems)
        for t in range(n):
            for j in range(4):
                rc(t, ins[t].at[2 * j + 1 - c], outs[t].at[j]).start()
        for t in range(n, n + m):
            rc(t, ins[t], outs[t]).start()

    def finish(ins, outs, sems):
        rc, _ = copies(ins, outs, sems)
        for t in range(n):
            rc(t, ins[t].at[pl.ds(0, 4)], outs[t]).wait()
        for t in range(n, n + m):
            rc(t, ins[t], outs[t]).wait()

    return _Plan(list(grads) + list(whole),
                 [SDS((4,) + tuple(g.shape[1:]), g.dtype) for g in grads] + [SDS(a.shape, a.dtype) for a in whole],
                 [pltpu.SemaphoreType.DMA((n + m,)), pltpu.SemaphoreType.DMA((n + m,))], start, finish)


def _chips_plan(parts, whole=()):
    n, m = len(parts), len(whole)

    def src_of(ins, t, pc):
        return ins[t].at[pc] if t < n else ins[t]

    def local_copies(ins, outs, sems, chip):
        return [pltpu.make_async_copy(src_of(ins, t, chip), outs[t].at[chip], sems[2].at[t]) for t in range(n + m)]

    def start(ins, outs, sems):
        send_sems, recv_sems, _ = sems
        x, y, c = _place()
        chip = 2 * x + y
        for cp in local_copies(ins, outs, sems, chip):
            cp.start()
        for px, py, pc in _other_chips(x, y):
            for t in range(n + m):
                pltpu.make_async_remote_copy(src_ref=src_of(ins, t, pc), dst_ref=outs[t].at[chip],
                                             send_sem=send_sems.at[t], recv_sem=recv_sems.at[t],
                                             device_id=(px, py, c), device_id_type=MESH).start()

    def finish(ins, outs, sems):
        send_sems, recv_sems, _ = sems
        x, y, c = _place()
        for t in range(n + m):
            three = outs[t].at[pl.ds(0, 3)]
            pltpu.make_async_remote_copy(src_ref=three, dst_ref=three, send_sem=send_sems.at[t],
                                         recv_sem=recv_sems.at[t], device_id=(x, y, c), device_id_type=MESH).wait()
        for cp in local_copies(ins, outs, sems, 2 * x + y):
            cp.wait()

    return _Plan(list(parts) + list(whole),
                 [SDS(p.shape, p.dtype) for p in parts] + [SDS((4,) + tuple(a.shape), a.dtype) for a in whole],
                 [pltpu.SemaphoreType.DMA((n + m,)), pltpu.SemaphoreType.DMA((n + m,)),
                  pltpu.SemaphoreType.DMA((n + m,))], start, finish)


def _exchange_plan(arr):
    def peers(x, y, c):
        flip = lambda v, f: 1 - v if f else v
        return [(flip(x, fx), flip(y, fy), flip(c, fc))
                for fx in (0, 1) for fy in (0, 1) for fc in (0, 1) if fx or fy or fc]

    def start(ins, outs, sems):
        x, y, c = _place()
        me = 4 * x + 2 * y + c
        pltpu.make_async_copy(ins[0], outs[0].at[me], sems[2].at[0]).start()
        for to in peers(x, y, c):
            pltpu.make_async_remote_copy(src_ref=ins[0], dst_ref=outs[0].at[me], send_sem=sems[0].at[0],
                                         recv_sem=sems[1].at[0], device_id=to, device_id_type=MESH).start()

    def finish(ins, outs, sems):
        x, y, c = _place()
        seven = outs[0].at[pl.ds(0, 7)]
        pltpu.make_async_remote_copy(src_ref=seven, dst_ref=seven, send_sem=sems[0].at[0], recv_sem=sems[1].at[0],
                                     device_id=(x, y, c), device_id_type=MESH).wait()
        pltpu.make_async_copy(ins[0], outs[0].at[4 * x + 2 * y + c], sems[2].at[0]).wait()

    return _Plan([arr], [SDS((N_SLOT,) + tuple(arr.shape), arr.dtype)],
                 [pltpu.SemaphoreType.DMA((1,)), pltpu.SemaphoreType.DMA((1,)), pltpu.SemaphoreType.DMA((1,))],
                 start, finish)


def _join(*plans):
    def cut(seq, sizes):
        out, at = [], 0
        for k in sizes:
            out.append(seq[at:at + k])
            at += k
        return out

    n_arr = [len(p.arrays) for p in plans]
    n_sem = [len(p.sems) for p in plans]

    def start(ins, outs, sems):
        for p, i, o, s in zip(plans, cut(ins, n_arr), cut(outs, n_arr), cut(sems, n_sem)):
            p.start(i, o, s)

    def finish(ins, outs, sems):
        for p, i, o, s in zip(plans, cut(ins, n_arr), cut(outs, n_arr), cut(sems, n_sem)):
            p.finish(i, o, s)

    def middle(ins, outs, sems):
        for p, i, o, s in zip(plans, cut(ins, n_arr), cut(outs, n_arr), cut(sems, n_sem)):
            if p.middle is not None:
                p.middle(i, o, s)

    return _Plan([a for p in plans for a in p.arrays], [o for p in plans for o in p.out_shape],
                 [s for p in plans for s in p.sems], start, finish,
                 middle if any(p.middle is not None for p in plans) else None)


def _run_plan(name, plan):
    k = len(plan.arrays)

    def body(*refs):
        ins, outs, sems = refs[:k], refs[k:2 * k], refs[2 * k:]
        plan.start(ins, outs, sems)
        if plan.middle is not None:
            plan.middle(ins, outs, sems)
        plan.finish(ins, outs, sems)

    return pl.pallas_call(
        body, name=name, in_specs=[ANY] * k, out_specs=[ANY] * k, out_shape=plan.out_shape,
        scratch_shapes=plan.sems, compiler_params=pltpu.CompilerParams(has_side_effects=True),
    )(*plan.arrays)


def _call(body, *, name, grid, in_specs, out_specs, out_shape, scratch_shapes, params, args, comm=None,
          aliases=None, prefetch=()):
    aliases = aliases or {}
    n_pre = len(prefetch)

    def launch(fn, ins_specs, outs_specs, outs_shape, scratch, operands):
        spec = pltpu.PrefetchScalarGridSpec(num_scalar_prefetch=n_pre, grid=grid, in_specs=ins_specs,
                                            out_specs=outs_specs, scratch_shapes=scratch)
        return pl.pallas_call(fn, name=name, grid_spec=spec, out_shape=outs_shape, compiler_params=params,
                              input_output_aliases=aliases)(*prefetch, *[_small_in_hbm(a) for a in operands])

    if comm is None:
        return list(launch(body, in_specs, out_specs, out_shape, scratch_shapes, args)), []
    n_in, n_out, n_scr, k = len(in_specs), len(out_specs), len(scratch_shapes), len(comm.arrays)

    def wrapped(*refs):
        pre, refs = refs[:n_pre], refs[n_pre:]
        ins = refs[:n_in]
        c_in = refs[n_in:n_in + k]
        outs = refs[n_in + k:n_in + k + n_out]
        c_out = refs[n_in + k + n_out:n_in + 2 * k + n_out]
        scr = refs[n_in + 2 * k + n_out:n_in + 2 * k + n_out + n_scr]
        sems = refs[n_in + 2 * k + n_out + n_scr:]
        step, steps = pl.program_id(0), grid[0]
        for d in range(1, len(grid)):
            step, steps = step * grid[d] + pl.program_id(d), steps * grid[d]

        @pl.when(step == 0)
        def _():
            comm.start(c_in, c_out, sems)

        if comm.middle is not None:
            @pl.when(step == (3 * steps) // 4)
            def _():
                comm.middle(c_in, c_out, sems)

        body(*pre, *ins, *outs, *scr)

        @pl.when(step == steps - 1)
        def _():
            comm.finish(c_in, c_out, sems)

    res = launch(wrapped, list(in_specs) + [ANY] * k, list(out_specs) + [ANY] * k,
                 list(out_shape) + list(comm.out_shape), list(scratch_shapes) + list(comm.sems),
                 tuple(args) + tuple(comm.arrays))
    return list(res[:n_out]), list(res[n_out:])


def _row_tile(rows):
    for t in (512, 256, 128, 64, 32, 16, 8):
        if rows % t == 0:
            return t
    return rows


def _pair_sum(name, g8, recv4, core):
    _, rows, cols = recv4.shape
    tr = _row_tile(rows)
    g42 = g8.reshape(4, 2, rows, cols)

    def body(c_ref, g_ref, r_ref, o_ref):
        del c_ref
        o_ref[...] = (g_ref[...].astype(F32) + r_ref[...].astype(F32)).astype(o_ref.dtype)

    return pl.pallas_call(
        body, name=name,
        grid_spec=pltpu.PrefetchScalarGridSpec(
            num_scalar_prefetch=1, grid=(4, rows // tr),
            in_specs=[pl.BlockSpec((None, None, tr, cols), lambda j, i, c_ref: (j, c_ref[0], i, 0)),
                      pl.BlockSpec((None, tr, cols), lambda j, i, c_ref: (j, i, 0))],
            out_specs=pl.BlockSpec((None, tr, cols), lambda j, i, c_ref: (j, i, 0))),
        out_shape=SDS(recv4.shape, g8.dtype),
        compiler_params=_cparams(("arbitrary", "arbitrary"), 32),
    )(core, g42, recv4)


def _add2(name, a, b):
    rows, cols = a.shape
    tr = _row_tile(rows)

    def body(a_ref, b_ref, o_ref):
        o_ref[...] = a_ref[...] + b_ref[...]

    blk = pl.BlockSpec((tr, cols), lambda i: (i, 0))
    return pl.pallas_call(body, name=name, grid=(rows // tr,), in_specs=[blk, blk], out_specs=blk,
                          out_shape=SDS(a.shape, a.dtype),
                          compiler_params=_cparams(("arbitrary",), 32))(a, b)


def _sum_terms(name, terms):
    k, rows, cols = terms.shape
    tr = _row_tile(rows)

    def body(r_ref, o_ref):
        acc = r_ref[0]
        for q in range(1, k):
            acc = acc + r_ref[q]
        o_ref[...] = acc

    return pl.pallas_call(body, name=name, grid=(rows // tr,),
                          in_specs=[pl.BlockSpec((k, tr, cols), lambda i: (0, i, 0))],
                          out_specs=pl.BlockSpec((tr, cols), lambda i: (i, 0)),
                          out_shape=SDS((rows, cols), terms.dtype),
                          compiler_params=_cparams(("arbitrary",), 32))(terms)


def _adam_update(g, w, m, v):
    c1 = 1.0 / (1.0 - ADAM_B1 ** ADAM_STEP)
    c2 = 1.0 / (1.0 - ADAM_B2 ** ADAM_STEP)
    mn = ADAM_B1 * m + (1.0 - ADAM_B1) * g
    vn = ADAM_B2 * v + (1.0 - ADAM_B2) * (g * g)
    delta = (-ADAM_LR) * ((mn * c1) / (jnp.sqrt(vn * c2) + ADAM_EPS) + ADAM_WD * w)
    return delta, mn, vn


def _adamw_many(name, gs, ws, ms, vs):
    n = len(gs)

    def body(*refs):
        for p in range(n):
            g, w, m, v = (refs[q * n + p][...] for q in range(4))
            d, mn, vn = _adam_update(g, w, m, v)
            refs[4 * n + p][...] = d
            refs[5 * n + p][...] = mn
            refs[6 * n + p][...] = vn

    full = [pl.BlockSpec(w.shape, lambda i: (0, 0)) for w in ws]
    shapes = [SDS(w.shape, F32) for w in ws]
    res = pl.pallas_call(body, name=name, grid=(1,), in_specs=full * 4, out_specs=full * 3, out_shape=shapes * 3,
                         compiler_params=_cparams(("arbitrary",), 32),
                         )(*[_small_in_hbm(a) for a in (*gs, *ws, *ms, *vs)])
    return [(res[p], res[n + p], res[2 * n + p]) for p in range(n)]


def _adamw(name, terms, w, m, v):
    k, rows, cols = terms.shape
    tr = _row_tile(rows)

    def body(t_ref, w_ref, m_ref, v_ref, g_ref, d_ref, mo_ref, vo_ref):
        g = t_ref[0].astype(F32)
        for q in range(1, k):
            g = g + t_ref[q].astype(F32)
        g_ref[...] = g
        d_ref[...], mo_ref[...], vo_ref[...] = _adam_update(g, w_ref[...], m_ref[...], v_ref[...])

    blk = pl.BlockSpec((tr, cols), lambda i: (i, 0))
    return pl.pallas_call(body, name=name, grid=(rows // tr,),
                          in_specs=[pl.BlockSpec((k, tr, cols), lambda i: (0, i, 0)), blk, blk, blk],
                          out_specs=[blk] * 4, out_shape=[SDS((rows, cols), F32)] * 4,
                          compiler_params=_cparams(("arbitrary",), 40),
                          )(*[pltpu.with_memory_space_constraint(a, pltpu.HBM) for a in (terms, w, m, v)])


def kernel(x, norm_mix_g, w_in, conv_w, conv_b, w_rgate, b_rgate, w_igate, b_igate, lru_lambda, w_out_a, sgu_ln_g, sgu_ln_b, sgu_w_s, sgu_b_s, w_out_b, w_out, norm_mlp_g, w_up, w_down, norm_final_g, loss_target, m_norm_mix_g, m_w_in, m_conv_w, m_conv_b, m_w_rgate, m_b_rgate, m_w_igate, m_b_igate, m_lru_lambda, m_w_out_a, m_sgu_ln_g, m_sgu_ln_b, m_sgu_w_s, m_sgu_b_s, m_w_out_b, m_w_out, m_norm_mlp_g, m_w_up, m_w_down, m_norm_final_g, v_norm_mix_g, v_w_in, v_conv_w, v_conv_b, v_w_rgate, v_b_rgate, v_w_igate, v_b_igate, v_lru_lambda, v_w_out_a, v_sgu_ln_g, v_sgu_ln_b, v_sgu_w_s, v_sgu_b_s, v_w_out_b, v_w_out, v_norm_mlp_g, v_w_up, v_w_down, v_norm_final_g):
    cx, cy, cc = _place()
    me = 4 * cx + 2 * cy + cc
    core = jnp.reshape(cc, (1,)).astype(jnp.int32)
    xs = x[0]
    tgt = loss_target[0]
    s = xs.shape[0]

    gate_shard = jnp.stack([w_rgate[0], w_igate[0]]).astype(BF16).reshape(2 * HEADS * 32, HEAD_DIM)
    vec_shard = jnp.concatenate([conv_w[0], b_rgate[0], b_igate[0]], axis=1)
    vec_shard = jnp.pad(vec_shard, ((0, 4), (0, 256 - vec_shard.shape[1])))
    shards = [w_in[0].astype(BF16), w_out_a[0].astype(BF16), w_out_b[0].astype(BF16), w_out[0].astype(BF16),
              w_up[0].astype(BF16), w_down[0].astype(BF16), gate_shard, vec_shard]
    (z, n1_t, w_in_g), (gate_g, vec_g) = _in_proj(xs, norm_mix_g, shards[0], _slot_order(cx, cy, cc),
                                                comm=_gather_plan(shards[6:8]))
    gates = gate_g.reshape(N_SLOT, 2, HEADS, 32, HEAD_DIM).transpose(1, 2, 0, 3, 4).reshape(2, HEADS, HEAD_DIM, HEAD_DIM)
    w_r_f, w_i_f = gates[0], gates[1]
    conv_w_f = vec_g[:, 0:4, 0:128].transpose(1, 0, 2).reshape(CONV_K, D)
    b_r_f = vec_g[:, 0:4, 128:160].transpose(1, 0, 2).reshape(1, D)
    b_i_f = vec_g[:, 0:4, 160:192].transpose(1, 0, 2).reshape(1, D)
    b_s_t = jnp.transpose(sgu_b_s[0])

    (ya, hs), (w_oa_g, w_ob_g, w_out_g, w_up_g) = _branch_a_fwd(
        z, conv_w_f, conv_b, w_r_f, b_r_f, w_i_f, b_i_f, lru_lambda, comm=_gather_plan(shards[1:5]))
    w_oa_f = w_oa_g.reshape(D, D)
    w_ob_f = w_ob_g.reshape(D, D)
    w_out_f = w_out_g.reshape(D, D)
    (yb,), _ = _branch_b_fwd(z, sgu_ln_g, sgu_ln_b, sgu_w_s[0], b_s_t)
    (pa, pb, merged, h1), (w_down_g,) = _merge_out(ya, yb, z, xs, w_oa_f, w_ob_f, w_out_f,
                                                   comm=_gather_plan(shards[5:6]))
    w_down_f = w_down_g.reshape(N_SLOT * FF_COLS, D)
    gf2 = norm_final_g.reshape(1, D)
    r_act, act_t, n2_t, dh2, loss_acc, d_gfin = _mlp_fwd(h1, norm_mlp_g, w_up_g, w_down_f, gf2, tgt)

    def pair(names, grads, recv):
        return [_pair_sum("pair_sum_" + nm, g, r, core) for nm, g, r in zip(names, grads, recv)]

    g_down = _wgrad_t("wgrad_down", act_t, dh2, N_SLOT // 2, True, False, 2048)
    g_down = g_down.reshape(N_SLOT, FF_COLS, D)
    (df, dh1, d_gmlp), (r_down,) = _mlp_bwd(dh2, r_act, w_down_f, w_up_g, h1, norm_mlp_g,
                                            comm=_sibling_plan([g_down]))
    (p_down,) = pair(["down"], [g_down], [r_down])
    g_up = _wgrad_t("wgrad_up", n2_t, df, N_SLOT, False, True, 4096)
    g_out = _wgrad("wgrad_out", merged, dh1, 1, False, False).reshape(N_SLOT, D // N_SLOT, D)
    (dz, dpa, dpb, dya, dyb), (got_down, r_up, r_out) = _merge_bwd(
        dh1, z, pa, pb, w_out_f, w_oa_f, w_ob_f, comm=_join(_chips_plan([p_down]), _sibling_plan([g_up, g_out])))
    p_up, p_out = pair(["up", "out"], [g_up, g_out], [r_up, r_out])
    g_oa = _wgrad("wgrad_out_a", ya, dpa, 1, False, False).reshape(N_SLOT, D // N_SLOT, D)
    g_ob = _wgrad("wgrad_out_b", yb, dpb, 1, False, False).reshape(N_SLOT, D // N_SLOT, D)
    (dz, d_ws, d_bs, d_ln), (got_up, r_oa, r_ob) = _branch_b_bwd(
        dz, dyb, z, sgu_ln_g, sgu_ln_b, sgu_w_s[0], b_s_t,
        comm=_join(_chips_plan([p_up]), _sibling_plan([g_oa, g_ob])))
    p_oa, p_ob = pair(["out_a", "out_b"], [g_oa, g_ob], [r_oa, r_ob])
    (dz, d_vec, d_wr, d_wi), (got_out, got_oa, got_ob) = _branch_a_bwd(
        dz, dya, z, hs, conv_w_f, conv_b, w_r_f, b_r_f, w_i_f, b_i_f, lru_lambda,
        comm=_chips_plan([p_out, p_oa, p_ob]))
    g_in = _wgrad_t("wgrad_in", n1_t, dz, N_SLOT, False, True, 4096)
    g_gate = jnp.stack([d_wr, d_wi]).reshape(2, HEADS, N_SLOT, 32, HEAD_DIM).transpose(2, 0, 1, 3, 4)
    g_gate = g_gate.reshape(N_SLOT, 2 * HEADS * 32, HEAD_DIM).astype(BF16)

    d_bs_row = jnp.pad(d_bs[:, :, 0].reshape(1, GROUPS * CHUNK), ((0, 0), (0, D - GROUPS * CHUNK)))
    vecs = jnp.concatenate([d_vec, jnp.concatenate([d_ln[0:2], d_gmlp, d_gfin, d_bs_row, jnp.zeros((3, D), F32)])])
    d_ws2 = d_ws.reshape(GROUPS * CHUNK, CHUNK)
    r_in, r_gate, r_vecs, r_ws = _run_plan("rs_sibling_in", _sibling_plan([g_in, g_gate], [vecs, d_ws2]))
    p_in, p_gate = pair(["in", "gate"], [g_in, g_gate], [r_in, r_gate])
    vecs_chip = _add2("pair_sum_vecs", vecs, r_vecs)
    ws_chip = _add2("pair_sum_ws", d_ws2, r_ws)
    (dx, d_gmix), (got_in, got_gate, got_vecs, got_ws) = _in_bwd(
        dz, w_in_g, xs, dh1, norm_mix_g, comm=_chips_plan([p_in, p_gate], [vecs_chip, ws_chip]))
    vecs_sum = _sum_terms("sum_vecs", got_vecs)
    last = jnp.concatenate([d_gmix, jnp.pad(loss_acc[0:1], ((0, 0), (0, D - 128))), jnp.zeros((6, D), F32)])
    (last_all,) = _run_plan("exchange_last", _exchange_plan(last))
    last_sum = _sum_terms("sum_last", last_all)
    loss = last_sum[1, 0]
    got = [got_in, got_oa, got_ob, got_out, got_up, got_down, got_gate]

    def step(nm, terms, w, m, v, rows, cols):
        g, d, mn, vn = _adamw("adamw_" + nm, terms.reshape(4, rows, cols), w.reshape(rows, cols),
                              m.reshape(rows, cols), v.reshape(rows, cols))
        return [a.reshape(w.shape) for a in (g, d, mn, vn)]

    o_in = step("in", got[0], w_in, m_w_in, v_w_in, D, W_IN_COLS)
    o_oa = step("out_a", got[1], w_out_a, m_w_out_a, v_w_out_a, D // N_SLOT, D)
    o_ob = step("out_b", got[2], w_out_b, m_w_out_b, v_w_out_b, D // N_SLOT, D)
    o_out = step("out", got[3], w_out, m_w_out, v_w_out, D // N_SLOT, D)
    o_up = step("up", got[4], w_up, m_w_up, v_w_up, D, FF_COLS)
    o_down = step("down", got[5], w_down, m_w_down, v_w_down, FF_COLS, D)
    gate_w = jnp.stack([w_rgate[0], w_igate[0]]).reshape(2 * HEADS * 32, HEAD_DIM)
    gate_m = jnp.stack([m_w_rgate[0], m_w_igate[0]]).reshape(2 * HEADS * 32, HEAD_DIM)
    gate_v = jnp.stack([v_w_rgate[0], v_w_igate[0]]).reshape(2 * HEADS * 32, HEAD_DIM)
    o_gate = _adamw("adamw_gate", got[6], gate_w, gate_m, gate_v)
    o_gate = [a.reshape(2, 1, HEADS, 32, HEAD_DIM) for a in o_gate]
    o_wr = [a[0] for a in o_gate]
    o_wi = [a[1] for a in o_gate]

    def own(full, width):
        return lax.dynamic_slice_in_dim(full, me * width, width, axis=1)

    small_g = {
        "norm_mix_g": last_sum[0:1], "conv_w": own(vecs_sum[0:4], 128), "conv_b": vecs_sum[4:5],
        "b_rgate": own(vecs_sum[5:6].reshape(HEADS, HEAD_DIM), 32),
        "b_igate": own(vecs_sum[6:7].reshape(HEADS, HEAD_DIM), 32),
        "lru_lambda": vecs_sum[7:8], "sgu_ln_g": vecs_sum[8:9], "sgu_ln_b": vecs_sum[9:10],
        "norm_mlp_g": vecs_sum[10:11], "norm_final_g": vecs_sum[11:12],
        "sgu_b_s": vecs_sum[12, 0:GROUPS * CHUNK].reshape(GROUPS, CHUNK),
    }
    small_w = {"norm_mix_g": (norm_mix_g, m_norm_mix_g, v_norm_mix_g), "conv_w": (conv_w, m_conv_w, v_conv_w),
               "conv_b": (conv_b, m_conv_b, v_conv_b), "b_rgate": (b_rgate, m_b_rgate, v_b_rgate),
               "b_igate": (b_igate, m_b_igate, v_b_igate), "lru_lambda": (lru_lambda, m_lru_lambda, v_lru_lambda),
               "sgu_ln_g": (sgu_ln_g, m_sgu_ln_g, v_sgu_ln_g), "sgu_ln_b": (sgu_ln_b, m_sgu_ln_b, v_sgu_ln_b),
               "norm_mlp_g": (norm_mlp_g, m_norm_mlp_g, v_norm_mlp_g),
               "norm_final_g": (norm_final_g, m_norm_final_g, v_norm_final_g),
               "sgu_b_s": (sgu_b_s, m_sgu_b_s, v_sgu_b_s), "sgu_w_s": (sgu_w_s, m_sgu_w_s, v_sgu_w_s)}
    order = list(small_g)
    as2d = lambda k, a: a.reshape(small_g[k].shape)
    upd = _adamw_many("adamw_small", [small_g[k] for k in order], *[[as2d(k, small_w[k][q]) for k in order]
                                                                     for q in range(3)])
    o_small = {k: [a.reshape(small_w[k][0].shape) for a in (small_g[k],) + u] for k, u in zip(order, upd)}
    ws3 = [a[0].reshape(GROUPS * CHUNK, CHUNK) for a in small_w.pop("sgu_w_s")]
    o_small["sgu_w_s"] = [a.reshape(sgu_w_s.shape) for a in _adamw("adamw_ws", got_ws, *ws3)]

    per_weight = {"norm_mix_g": o_small["norm_mix_g"], "w_in": o_in, "conv_w": o_small["conv_w"],
                  "conv_b": o_small["conv_b"], "w_rgate": o_wr, "b_rgate": o_small["b_rgate"], "w_igate": o_wi,
                  "b_igate": o_small["b_igate"], "lru_lambda": o_small["lru_lambda"], "w_out_a": o_oa,
                  "sgu_ln_g": o_small["sgu_ln_g"], "sgu_ln_b": o_small["sgu_ln_b"], "sgu_w_s": o_small["sgu_w_s"],
                  "sgu_b_s": o_small["sgu_b_s"], "w_out_b": o_ob, "w_out": o_out, "norm_mlp_g": o_small["norm_mlp_g"],
                  "w_up": o_up, "w_down": o_down, "norm_final_g": o_small["norm_final_g"]}
    names_w = list(per_weight)
    return (loss, dx[None], *[per_weight[k][0] for k in names_w], *[per_weight[k][1] for k in names_w],
            *[per_weight[k][2] for k in names_w], *[per_weight[k][3] for k in names_w])
```

```python
import jax
import jax.numpy as jnp
from jax import lax
from jax.experimental import pallas as pl
from jax.experimental.pallas import tpu as pltpu

F32 = jnp.float32
BF16 = jnp.bfloat16
SDS = jax.ShapeDtypeStruct
MESH = pl.DeviceIdType.MESH
ANY = pl.BlockSpec(memory_space=pltpu.HBM)

D = 1024
N_SLOT = 8
W_IN_COLS = 768
FF_COLS = 512
HEADS, HEAD_DIM = 4, 256
GROUPS, GROUP_DIM = 4, 256
CHUNK = 128
CONV_K = 4
NORM_EPS = 1e-6
LN_EPS = 1e-5
LRU_C = 8.0
ADAM_LR, ADAM_B1, ADAM_B2, ADAM_EPS, ADAM_WD, ADAM_STEP = 0.001, 0.9, 0.999, 1e-08, 0.01, 10

TM_ROWS = 1024
TM_MERGE = 512
T_BRANCH_A = 512
T_BRANCH_B = 256
MiB = 1024 * 1024
SMALL_OPERAND = 16 * 1024

_GELU_C = 0.7978845608028654
_GELU_A = 0.044715


def _small_in_hbm(a):
    return pltpu.with_memory_space_constraint(a, pltpu.HBM) if a.size <= SMALL_OPERAND else a


def _cparams(sem, vmem_mib):
    return pltpu.CompilerParams(dimension_semantics=sem, vmem_limit_bytes=vmem_mib * MiB)


def _gelu(x):
    t = jnp.tanh(_GELU_C * (x + _GELU_A * x * x * x))
    return 0.5 * x * (1.0 + t)


def _gelu_and_grad(x):
    x2 = x * x
    t = jnp.tanh(_GELU_C * x * (1.0 + _GELU_A * x2))
    g = 0.5 * x * (1.0 + t)
    dg = 0.5 * (1.0 + t) + 0.5 * x * (1.0 - t * t) * _GELU_C * (1.0 + 3.0 * _GELU_A * x2)
    return g, dg


def _softplus(x):
    return jnp.maximum(x, 0.0) + jnp.log1p(jnp.exp(-jnp.abs(x)))


def _dot(a, b):
    return jnp.dot(a, b, preferred_element_type=F32)


def _dot_nt(a, b):
    return lax.dot_general(a, b, (((1,), (1,)), ((), ())), preferred_element_type=F32)


def _dot_tn(a, b):
    return lax.dot_general(a, b, (((0,), (0,)), ((), ())), preferred_element_type=F32)


def _rows_shifted(prev8, cur, k):
    ext = jnp.concatenate([prev8, cur], axis=0)
    return pltpu.roll(ext, k, 0)[8:]


def _rows_advanced(cur, next8, k):
    t = cur.shape[0]
    ext = jnp.concatenate([cur, next8], axis=0)
    return pltpu.roll(ext, t + 8 - k, 0)[:t]


def _first_second(x, y, c):
    ny, nx, far = _other_chips(x, y)
    pick = lambda a, b: a * (1 - c) + b * c
    first = tuple(pick(a, b) for a, b in zip(ny, nx))
    second = tuple(pick(b, a) for a, b in zip(ny, nx))
    return first, second, far


def _slot_order(x, y, c):
    chip = 2 * x + y
    first, second, far = _first_second(x, y, c)
    order = [2 * chip + c, 2 * chip + 1 - c, 2 * first[2] + c, 2 * second[2] + 1 - c, 2 * second[2] + c,
             2 * first[2] + 1 - c, 2 * far[2] + c, 2 * far[2] + 1 - c]
    return jnp.stack(order).astype(jnp.int32)


def _in_proj(x, g_mix, w_in_own, order, comm=None):
    s = x.shape[0]
    tm = min(TM_ROWS, s)
    ni = s // tm

    def body(order_ref, x_ref, g_ref, own_ref, z_ref, nt_ref, wg_ref, n_s, w_s, send_sems, recv_sems, local_sems):
        j, i = pl.program_id(0), pl.program_id(1)
        px, py, c = _place()
        chip = 2 * px + py
        me = 2 * chip + c
        sib = (px, py, 1 - c)
        chips = _other_chips(px, py)

        def rc(k, src, blk, to):
            return pltpu.make_async_remote_copy(src_ref=src, dst_ref=w_s.at[blk], send_sem=send_sems.at[k],
                                                recv_sem=recv_sems.at[k], device_id=to, device_id_type=MESH)

        del chips
        first, second, far = _first_second(px, py, c)
        blocks = [2 * first[2] + c, 2 * second[2] + c, 2 * far[2] + c]
        own_in = pltpu.make_async_copy(own_ref, w_s.at[me], local_sems.at[0])
        to_first = rc(1, own_ref, me, (first[0], first[1], c))
        to_second = rc(2, own_ref, me, (second[0], second[1], c))
        relay = rc(3, w_s.at[blocks[0]], blocks[0], (second[0], second[1], c))
        sends = [rc(0, own_ref, me, sib), to_first, to_second, relay]
        passed = [rc(4 + q, w_s.at[blk], blk, sib) for q, blk in enumerate(blocks)]
        keep = pltpu.make_async_copy(w_s, wg_ref, local_sems.at[1])

        @pl.when((i == 0) & (j == 0))
        def _():
            own_in.start()
            sends[0].start()
            to_first.start()
            own_in.wait()

        @pl.when((i == 0) & (j == 1))
        def _():
            rc(0, own_ref, 2 * chip + 1 - c, sib).wait_recv()

        for q, blk in enumerate(blocks):
            @pl.when((i == 0) & (j == 2 + 2 * q))
            def _():
                rc(1 + q, own_ref, blk, sib).wait_recv()
                passed[q].start()
                if q == 0:
                    to_second.start()
                    relay.start()

            @pl.when((i == 0) & (j == 3 + 2 * q))
            def _():
                rc(4 + q, own_ref, order_ref[j], sib).wait_recv()

        rows = pl.ds(pl.multiple_of(i * tm, tm), tm)

        @pl.when(j == 0)
        def _():
            xv = x_ref[...]
            rstd = lax.rsqrt(jnp.mean(xv * xv, axis=-1, keepdims=True) + NORM_EPS)
            nb = (xv * rstd * g_ref[...]).astype(BF16)
            n_s[rows, :] = nb
            nt_ref[...] = nb.T

        z_ref[...] = _dot(n_s[rows, :], w_s[order_ref[j]]).astype(BF16)

        @pl.when((i == 0) & (j == N_SLOT - 1))
        def _():
            keep.start()

        @pl.when((i == ni - 1) & (j == N_SLOT - 1))
        def _():
            for cp in sends + passed:
                cp.wait_send()
            keep.wait()

    first_pass = lambda j, i, o: (jnp.where(j == 0, i, ni - 1), 0)
    (z, n1, w_in_g), extra = _call(
        body, name="in_proj", grid=(N_SLOT, ni), prefetch=(order,),
        in_specs=[pl.BlockSpec((tm, D), first_pass),
                  pl.BlockSpec((1, D), lambda j, i, o: (0, 0)), ANY],
        out_specs=[pl.BlockSpec((tm, W_IN_COLS), lambda j, i, o: (i, o[j])),
                   pl.BlockSpec((D, tm), lambda j, i, o: (0, jnp.where(j == 0, i, ni - 1))), ANY],
        out_shape=[SDS((s, N_SLOT * W_IN_COLS), BF16), SDS((D, s), BF16), SDS((N_SLOT, D, W_IN_COLS), BF16)],
        scratch_shapes=[pltpu.VMEM((s, D), BF16), pltpu.VMEM((N_SLOT, D, W_IN_COLS), BF16),
                        pltpu.SemaphoreType.DMA((7,)), pltpu.SemaphoreType.DMA((7,)), pltpu.SemaphoreType.DMA((2,))],
        params=_cparams(("arbitrary", "arbitrary"), 56), args=(x, g_mix, w_in_own), comm=comm)
    return (z, n1, w_in_g), extra


def _lru_gates(xc, xcb, wr_ref, br, wi_ref, bi, sp_lam, a_s, b_s, r_s=None, i_s=None, m_s=None):
    for h in range(HEADS):
        sl = slice(h * HEAD_DIM, (h + 1) * HEAD_DIM)
        r = jax.nn.sigmoid(_dot(xcb[:, sl], wr_ref[h]) + br[:, sl])
        ig = jax.nn.sigmoid(_dot(xcb[:, sl], wi_ref[h]) + bi[:, sl])
        log_a = (-LRU_C) * r * sp_lam[:, sl]
        a = jnp.exp(log_a)
        mult = jnp.sqrt(-jnp.tanh(log_a) * (a * a + 1.0))
        a_s[:, sl] = a
        b_s[:, sl] = xc[:, sl] * ig * mult
        if r_s is not None:
            r_s[:, sl] = r
            i_s[:, sl] = ig
            m_s[:, sl] = mult


def _conv_fwd(xa, prev8, cw, cb):
    taps = [xa] + [_rows_shifted(prev8, xa, k) for k in range(1, CONV_K)]
    xc = cb + cw[0:1, :] * xa
    for k in range(1, CONV_K):
        xc = xc + cw[k:k + 1, :] * taps[k]
    return xc, taps


def _branch_a_fwd(z, conv_w, conv_b, w_r, b_r, w_i, b_i, lam, comm=None):
    s = z.shape[0]
    ta = min(T_BRANCH_A, s)
    per16 = ta // 16

    def body(xa_ref, xp_ref, ga_ref, cw_ref, cb_ref, wr_ref, br_ref, wi_ref, bi_ref, lam_ref,
             ya_ref, hs_ref, a_s, b_s, h_s, carry_s):
        i = pl.program_id(0)

        @pl.when(i == 0)
        def _():
            carry_s[...] = jnp.zeros_like(carry_s)

        xa = xa_ref[...].astype(F32)
        prev8 = jnp.where(i > 0, xp_ref[...].astype(F32)[8:16], 0.0)
        xc, _ = _conv_fwd(xa, prev8, cw_ref[...], cb_ref[...])
        sp_lam = _softplus(-lam_ref[...])
        _lru_gates(xc, xc.astype(BF16), wr_ref, br_ref[...], wi_ref, bi_ref[...], sp_lam, a_s, b_s)

        row = lax.broadcasted_iota(jnp.int32, (8, D), 0)

        def group(g, carry):
            off = pl.multiple_of(g * 8, 8)
            a8 = a_s[pl.ds(off, 8), :]
            b8 = b_s[pl.ds(off, 8), :]
            for d in (1, 2, 4):
                a_sh = jnp.where(row >= d, pltpu.roll(a8, d, 0), 1.0)
                b_sh = jnp.where(row >= d, pltpu.roll(b8, d, 0), 0.0)
                b8 = a8 * b_sh + b8
                a8 = a8 * a_sh
            h8 = b8 + a8 * carry
            h_s[pl.ds(off, 8), :] = h8
            return jnp.broadcast_to(h8[7:8, :], (8, D))

        carry_s[...] = lax.fori_loop(0, ta // 8, group, carry_s[...])
        hs = h_s[...]
        hs_ref[...] = hs.astype(BF16)
        ya_ref[...] = (hs * _gelu(ga_ref[...].astype(F32))).astype(BF16)

    vec = pl.BlockSpec((1, D), lambda i: (0, 0))
    gate = pl.BlockSpec((HEADS, HEAD_DIM, HEAD_DIM), lambda i: (0, 0, 0))
    return _call(
        body, name="branch_a_fwd", grid=(s // ta,),
        in_specs=[pl.BlockSpec((ta, D), lambda i: (i, 0)),
                  pl.BlockSpec((16, D), lambda i: (jnp.maximum(i * per16 - 1, 0), 0)),
                  pl.BlockSpec((ta, D), lambda i: (i, 1)),
                  pl.BlockSpec((CONV_K, D), lambda i: (0, 0)), vec, gate, vec, gate, vec, vec],
        out_specs=[pl.BlockSpec((ta, D), lambda i: (i, 0)), pl.BlockSpec((ta, D), lambda i: (i, 0))],
        out_shape=[SDS((s, D), BF16), SDS((s, D), BF16)],
        scratch_shapes=[pltpu.VMEM((ta, D), F32), pltpu.VMEM((ta, D), F32), pltpu.VMEM((ta, D), F32),
                        pltpu.VMEM((8, D), F32)],
        params=_cparams(("arbitrary",), 40), args=(z, z, z, conv_w, conv_b, w_r, b_r, w_i, b_i, lam), comm=comm)


def _sgu_common(ub, vb, lg, lb, with_grad):
    if with_grad:
        u, du = _gelu_and_grad(ub)
        v, dv = _gelu_and_grad(vb)
    else:
        u, v, du, dv = _gelu(ub), _gelu(vb), None, None
    mu = jnp.mean(v, axis=-1, keepdims=True)
    vc = v - mu
    rstd = lax.rsqrt(jnp.mean(vc * vc, axis=-1, keepdims=True) + LN_EPS)
    vhat = vc * rstd
    vln = vhat * lg + lb
    return u, du, dv, rstd, vhat, vln


def _masked_ws(ws_ref):
    t = lax.broadcasted_iota(jnp.int32, (CHUNK, CHUNK), 0)
    c = lax.broadcasted_iota(jnp.int32, (CHUNK, CHUNK), 1)
    keep = c <= t
    return [jnp.where(keep, ws_ref[g], 0.0).astype(BF16) for g in range(GROUPS)]


def _branch_b_fwd(z, ln_g, ln_b, w_s, b_s_t, comm=None):
    s = z.shape[0]
    tb = min(T_BRANCH_B, s)

    def body(ub_ref, vb_ref, lg_ref, lb_ref, ws_ref, bs_ref, yb_ref):
        u, _, _, _, _, vln = _sgu_common(ub_ref[...].astype(F32), vb_ref[...].astype(F32),
                                         lg_ref[...], lb_ref[...], False)
        vlnb = vln.astype(BF16)
        wm = _masked_ws(ws_ref)
        bs = bs_ref[...]
        for c in range(tb // CHUNK):
            rs = slice(c * CHUNK, (c + 1) * CHUNK)
            for g in range(GROUPS):
                cs = slice(g * GROUP_DIM, (g + 1) * GROUP_DIM)
                sp = _dot(wm[g], vlnb[rs, cs]) + bs[:, g:g + 1]
                yb_ref[rs, cs] = (u[rs, cs] * sp).astype(BF16)

    vec = pl.BlockSpec((1, D), lambda i: (0, 0))
    return _call(
        body, name="branch_b_fwd", grid=(s // tb,),
        in_specs=[pl.BlockSpec((tb, D), lambda i: (i, 2)), pl.BlockSpec((tb, D), lambda i: (i, 3)), vec, vec,
                  pl.BlockSpec((GROUPS, CHUNK, CHUNK), lambda i: (0, 0, 0)),
                  pl.BlockSpec((CHUNK, GROUPS), lambda i: (0, 0))],
        out_specs=[pl.BlockSpec((tb, D), lambda i: (i, 0))],
        out_shape=[SDS((s, D), BF16)], scratch_shapes=[],
        params=_cparams(("arbitrary",), 40), args=(z, z, ln_g, ln_b, w_s, b_s_t), comm=comm)


def _merge_out(ya, yb, z, x, w_oa, w_ob, w_out, comm=None):
    s = x.shape[0]
    tm = min(TM_MERGE, s)

    def body(ya_ref, yb_ref, ma_ref, mb_ref, x_ref, woa_ref, wob_ref, wo_ref, pa_ref, pb_ref, mg_ref, h1_ref):
        pa = _dot(ya_ref[...], woa_ref[...])
        pb = _dot(yb_ref[...], wob_ref[...])
        merged = (jax.nn.sigmoid(ma_ref[...].astype(F32)) * pa
                  + jax.nn.sigmoid(mb_ref[...].astype(F32)) * pb).astype(BF16)
        pa_ref[...] = pa.astype(BF16)
        pb_ref[...] = pb.astype(BF16)
        mg_ref[...] = merged
        h1_ref[...] = x_ref[...] + _dot(merged, wo_ref[...])

    row = pl.BlockSpec((tm, D), lambda i: (i, 0))
    wsp = pl.BlockSpec((D, D), lambda i: (0, 0))
    return _call(
        body, name="merge_out", grid=(s // tm,),
        in_specs=[row, row, pl.BlockSpec((tm, D), lambda i: (i, 4)), pl.BlockSpec((tm, D), lambda i: (i, 5)),
                  row, wsp, wsp, wsp],
        out_specs=[row, row, row, row],
        out_shape=[SDS((s, D), BF16), SDS((s, D), BF16), SDS((s, D), BF16), SDS((s, D), F32)], scratch_shapes=[],
        params=_cparams(("arbitrary",), 48), args=(ya, yb, z, z, x, w_oa, w_ob, w_out), comm=comm)


def _mlp_fwd(h1, g_mlp, w_up_g, w_down, g_fin, tgt):
    s = h1.shape[0]
    tm = min(TM_ROWS, s)
    nj = N_SLOT

    def body(h1_ref, gm_ref, wu_ref, wd_ref, gf_ref, t_ref, r_ref, at_ref, n2t_ref, dh2_ref, loss_ref, dgf_ref,
             n2_s, acc_s):
        i, j = pl.program_id(0), pl.program_id(1)

        @pl.when(j == 0)
        def _():
            hv = h1_ref[...]
            rstd = lax.rsqrt(jnp.mean(hv * hv, axis=-1, keepdims=True) + NORM_EPS)
            nb = (hv * rstd * gm_ref[...]).astype(BF16)
            n2_s[...] = nb
            n2t_ref[...] = nb.T
            acc_s[...] = jnp.zeros_like(acc_s)

        @pl.when((i == 0) & (j == 0))
        def _():
            loss_ref[...] = jnp.zeros_like(loss_ref)
            dgf_ref[...] = jnp.zeros_like(dgf_ref)

        r = jnp.maximum(_dot(n2_s[...], wu_ref[...]), 0.0)
        r_ref[...] = r.astype(BF16)
        act = (r * r).astype(BF16)
        at_ref[...] = act.T
        acc_s[...] += _dot(act, wd_ref[...])

        @pl.when(j == nj - 1)
        def _():
            h2 = h1_ref[...] + acc_s[...]
            rstd = lax.rsqrt(jnp.mean(h2 * h2, axis=-1, keepdims=True) + NORM_EPS)
            hh = h2 * rstd
            gf = gf_ref[...]
            e = hh * gf - t_ref[...]
            loss_ref[...] += jnp.sum(e * e) * (0.5 / D)
            dy = e * (1.0 / D)
            dgf_ref[...] += jnp.sum(dy * hh, axis=0, keepdims=True)
            dhh = dy * gf
            dh2_ref[...] = rstd * (dhh - hh * jnp.mean(dhh * hh, axis=-1, keepdims=True))

    row = pl.BlockSpec((tm, D), lambda i, j: (i, 0))
    vec = pl.BlockSpec((1, D), lambda i, j: (0, 0))
    return pl.pallas_call(
        body, name="mlp_fwd", grid=(s // tm, nj),
        in_specs=[row, vec, pl.BlockSpec((None, D, FF_COLS), lambda i, j: (j, 0, 0)),
                  pl.BlockSpec((FF_COLS, D), lambda i, j: (j, 0)), vec, row],
        out_specs=[pl.BlockSpec((tm, FF_COLS), lambda i, j: (i, j)), pl.BlockSpec((FF_COLS, tm), lambda i, j: (j, i)),
                   pl.BlockSpec((D, tm), lambda i, j: (0, i)), row, pl.BlockSpec((8, 128), lambda i, j: (0, 0)), vec],
        out_shape=[SDS((s, nj * FF_COLS), BF16), SDS((nj * FF_COLS, s), BF16), SDS((D, s), BF16), SDS((s, D), F32),
                   SDS((8, 128), F32), SDS((1, D), F32)],
        scratch_shapes=[pltpu.VMEM((tm, D), BF16), pltpu.VMEM((tm, D), F32)],
        compiler_params=_cparams(("arbitrary", "arbitrary"), 52),
    )(h1, _small_in_hbm(g_mlp), w_up_g, w_down, _small_in_hbm(g_fin), tgt)


def _mlp_bwd(dh2, r, w_down, w_up_g, h1, g_mlp, comm=None):
    s = h1.shape[0]
    tm = min(TM_ROWS, s)
    nj = N_SLOT

    def body(dh2_ref, r_ref, wd_ref, wu_ref, h1_ref, gm_ref, df_ref, dh1_ref, dgm_ref, dh2b_s, acc_s):
        i, j = pl.program_id(0), pl.program_id(1)

        @pl.when(j == 0)
        def _():
            dh2b_s[...] = dh2_ref[...].astype(BF16)
            acc_s[...] = jnp.zeros_like(acc_s)

        @pl.when((i == 0) & (j == 0))
        def _():
            dgm_ref[...] = jnp.zeros_like(dgm_ref)

        d_act = _dot_nt(dh2b_s[...], wd_ref[...])
        df = (d_act * (2.0 * r_ref[...].astype(F32))).astype(BF16)
        df_ref[...] = df
        acc_s[...] += _dot_nt(df, wu_ref[...])

        @pl.when(j == nj - 1)
        def _():
            hv = h1_ref[...]
            rstd = lax.rsqrt(jnp.mean(hv * hv, axis=-1, keepdims=True) + NORM_EPS)
            hh = hv * rstd
            dn2 = acc_s[...]
            dgm_ref[...] += jnp.sum(dn2 * hh, axis=0, keepdims=True)
            dhat = dn2 * gm_ref[...]
            dh1_ref[...] = dh2_ref[...] + rstd * (dhat - hh * jnp.mean(dhat * hh, axis=-1, keepdims=True))

    row = pl.BlockSpec((tm, D), lambda i, j: (i, 0))
    vec = pl.BlockSpec((1, D), lambda i, j: (0, 0))
    ffb = pl.BlockSpec((tm, FF_COLS), lambda i, j: (i, j))
    return _call(
        body, name="mlp_bwd", grid=(s // tm, nj),
        in_specs=[row, ffb, pl.BlockSpec((FF_COLS, D), lambda i, j: (j, 0)),
                  pl.BlockSpec((None, D, FF_COLS), lambda i, j: (j, 0, 0)), row, vec],
        out_specs=[ffb, row, vec],
        out_shape=[SDS((s, nj * FF_COLS), BF16), SDS((s, D), F32), SDS((1, D), F32)],
        scratch_shapes=[pltpu.VMEM((tm, D), BF16), pltpu.VMEM((tm, D), F32)],
        params=_cparams(("arbitrary", "arbitrary"), 52), args=(dh2, r, w_down, w_up_g, h1, g_mlp), comm=comm)


def _merge_bwd(dh1, z, pa, pb, w_out, w_oa, w_ob, comm=None):
    s = dh1.shape[0]
    tm = min(TM_MERGE, s)

    def body(dh1_ref, ma_ref, mb_ref, pa_ref, pb_ref, wo_ref, woa_ref, wob_ref,
             dz_ref, dpa_ref, dpb_ref, dya_ref, dyb_ref):
        dm = _dot_nt(dh1_ref[...].astype(BF16), wo_ref[...])
        sa = jax.nn.sigmoid(ma_ref[...].astype(F32))
        sb = jax.nn.sigmoid(mb_ref[...].astype(F32))
        dpa = (dm * sa).astype(BF16)
        dpb = (dm * sb).astype(BF16)
        dz_ref[:, 0:D] = (dm * pa_ref[...].astype(F32) * sa * (1.0 - sa)).astype(BF16)
        dz_ref[:, D:2 * D] = (dm * pb_ref[...].astype(F32) * sb * (1.0 - sb)).astype(BF16)
        dpa_ref[...] = dpa
        dpb_ref[...] = dpb
        dya_ref[...] = _dot_nt(dpa, woa_ref[...]).astype(BF16)
        dyb_ref[...] = _dot_nt(dpb, wob_ref[...]).astype(BF16)

    row = pl.BlockSpec((tm, D), lambda i: (i, 0))
    wsp = pl.BlockSpec((D, D), lambda i: (0, 0))
    return _call(
        body, name="merge_bwd", grid=(s // tm,),
        in_specs=[row, pl.BlockSpec((tm, D), lambda i: (i, 4)), pl.BlockSpec((tm, D), lambda i: (i, 5)),
                  row, row, wsp, wsp, wsp],
        out_specs=[pl.BlockSpec((tm, 2 * D), lambda i: (i, 2)), row, row, row, row],
        out_shape=[SDS((s, 6 * D), BF16)] + [SDS((s, D), BF16)] * 4, scratch_shapes=[],
        params=_cparams(("arbitrary",), 48), args=(dh1, z, z, pa, pb, w_out, w_oa, w_ob), comm=comm)


def _branch_b_bwd(dz, dyb, z, ln_g, ln_b, w_s, b_s_t, comm=None):
    s = z.shape[0]
    tb = min(T_BRANCH_B, s)

    def body(dz_in, dyb_ref, ub_ref, vb_ref, lg_ref, lb_ref, ws_ref, bs_ref,
             dz_ref, dws_ref, dbs_ref, dln_ref, du_s, dvln_s):
        del dz_in

        @pl.when(pl.program_id(0) == 0)
        def _():
            dws_ref[...] = jnp.zeros_like(dws_ref)
            dbs_ref[...] = jnp.zeros_like(dbs_ref)
            dln_ref[...] = jnp.zeros_like(dln_ref)

        lg = lg_ref[...]
        u, du, dv, rstd, vhat, vln = _sgu_common(ub_ref[...].astype(F32), vb_ref[...].astype(F32),
                                                 lg, lb_ref[...], True)
        vlnb = vln.astype(BF16)
        dyb_v = dyb_ref[...].astype(F32)
        wm = _masked_ws(ws_ref)
        keep = (lax.broadcasted_iota(jnp.int32, (CHUNK, CHUNK), 1)
                <= lax.broadcasted_iota(jnp.int32, (CHUNK, CHUNK), 0))
        bs = bs_ref[...]
        for c in range(tb // CHUNK):
            rs = slice(c * CHUNK, (c + 1) * CHUNK)
            for g in range(GROUPS):
                cs = slice(g * GROUP_DIM, (g + 1) * GROUP_DIM)
                v_blk = vlnb[rs, cs]
                sp = _dot(wm[g], v_blk) + bs[:, g:g + 1]
                d_sp = dyb_v[rs, cs] * u[rs, cs]
                d_spb = d_sp.astype(BF16)
                du_s[rs, cs] = dyb_v[rs, cs] * sp
                dvln_s[rs, cs] = _dot_tn(wm[g], d_spb)
                dws_ref[g] += jnp.where(keep, _dot_nt(d_spb, v_blk), 0.0)
                dbs_ref[g] += jnp.broadcast_to(jnp.sum(d_sp, axis=-1, keepdims=True), (CHUNK, CHUNK))
        dvln = dvln_s[...]
        dln_ref[0:1, :] += jnp.sum(dvln * vhat, axis=0, keepdims=True)
        dln_ref[1:2, :] += jnp.sum(dvln, axis=0, keepdims=True)
        dvh = dvln * lg
        d_v = rstd * (dvh - jnp.mean(dvh, axis=-1, keepdims=True)
                      - vhat * jnp.mean(dvh * vhat, axis=-1, keepdims=True))
        dz_ref[:, 0:D] = (du_s[...] * du).astype(BF16)
        dz_ref[:, D:2 * D] = (d_v * dv).astype(BF16)

    vec = pl.BlockSpec((1, D), lambda i: (0, 0))
    sq = pl.BlockSpec((GROUPS, CHUNK, CHUNK), lambda i: (0, 0, 0))
    return _call(
        body, name="branch_b_bwd", grid=(s // tb,),
        in_specs=[ANY, pl.BlockSpec((tb, D), lambda i: (i, 0)),
                  pl.BlockSpec((tb, D), lambda i: (i, 2)), pl.BlockSpec((tb, D), lambda i: (i, 3)), vec, vec, sq,
                  pl.BlockSpec((CHUNK, GROUPS), lambda i: (0, 0))],
        out_specs=[pl.BlockSpec((tb, 2 * D), lambda i: (i, 1)), sq, sq, pl.BlockSpec((8, D), lambda i: (0, 0))],
        out_shape=[SDS(dz.shape, BF16), SDS((GROUPS, CHUNK, CHUNK), F32), SDS((GROUPS, CHUNK, CHUNK), F32),
                   SDS((8, D), F32)],
        scratch_shapes=[pltpu.VMEM((tb, D), F32), pltpu.VMEM((tb, D), F32)], aliases={0: 0},
        params=_cparams(("arbitrary",), 40), args=(dz, dyb, z, z, ln_g, ln_b, w_s, b_s_t), comm=comm)


def _branch_a_bwd(dz, dya, z, hs, conv_w, conv_b, w_r, b_r, w_i, b_i, lam, comm=None):
    s = z.shape[0]
    ta = min(T_BRANCH_A, s)
    nb = s // ta
    per16 = ta // 16

    def body(dz_in, dya_ref, xa_ref, xp_ref, ga_ref, hs_ref, hp_ref, cw_ref, cb_ref, wr_ref, br_ref, wi_ref,
             bi_ref, lam_ref, dz_ref, vec_ref, dwr_ref, dwi_ref,
             a_s, b_s, h_s, r_s, i_s, m_s, dcar_s, acar_s, dxc_s):
        del dz_in
        i = pl.program_id(0)
        blk = nb - 1 - i

        @pl.when(i == 0)
        def _():
            dcar_s[...] = jnp.zeros_like(dcar_s)
            acar_s[...] = jnp.zeros_like(acar_s)
            dxc_s[...] = jnp.zeros_like(dxc_s)
            vec_ref[...] = jnp.zeros_like(vec_ref)
            dwr_ref[...] = jnp.zeros_like(dwr_ref)
            dwi_ref[...] = jnp.zeros_like(dwi_ref)

        cw = cw_ref[...]
        lam_v = lam_ref[...]
        xa = xa_ref[...].astype(F32)
        prev8 = jnp.where(blk > 0, xp_ref[...].astype(F32)[8:16], 0.0)
        xc, taps = _conv_fwd(xa, prev8, cw, cb_ref[...])
        xcb = xc.astype(BF16)
        sp_lam = _softplus(-lam_v)
        _lru_gates(xc, xcb, wr_ref, br_ref[...], wi_ref, bi_ref[...], sp_lam, a_s, b_s, r_s, i_s, m_s)

        hs_v = hs_ref[...].astype(F32)
        hprev8 = jnp.where(blk > 0, hp_ref[...].astype(F32)[8:16], 0.0)
        h_m1 = _rows_shifted(hprev8, hs_v, 1)
        gg, dgg = _gelu_and_grad(ga_ref[...].astype(F32))
        dya_v = dya_ref[...].astype(F32)
        dz_ref[:, D:2 * D] = (dya_v * hs_v * dgg).astype(BF16)

        a_v = a_s[...]
        a_s[...] = _rows_advanced(a_v, acar_s[...], 1)
        b_s[...] = dya_v * gg

        row = lax.broadcasted_iota(jnp.int32, (8, D), 0)
        ng = ta // 8

        def group(gi, carry):
            off = pl.multiple_of((ng - 1 - gi) * 8, 8)
            c8 = a_s[pl.ds(off, 8), :]
            d8 = b_s[pl.ds(off, 8), :]
            for d in (1, 2, 4):
                c_sh = jnp.where(row < 8 - d, pltpu.roll(c8, 8 - d, 0), 1.0)
                d_sh = jnp.where(row < 8 - d, pltpu.roll(d8, 8 - d, 0), 0.0)
                d8 = c8 * d_sh + d8
                c8 = c8 * c_sh
            dh8 = d8 + c8 * carry
            h_s[pl.ds(off, 8), :] = dh8
            return jnp.broadcast_to(dh8[0:1, :], (8, D))

        dcar_s[...] = lax.fori_loop(0, ng, group, dcar_s[...])
        acar_s[...] = jnp.broadcast_to(a_v[0:1, :], (8, D))

        dbx = h_s[...]
        r_v, i_v, m_v = r_s[...], i_s[...], m_s[...]
        d_mult = dbx * xc * i_v
        d_loga = dbx * h_m1 * a_v - d_mult * (a_v * a_v) / m_v
        d_pr = d_loga * ((-LRU_C) * sp_lam) * r_v * (1.0 - r_v)
        d_pi = dbx * xc * m_v * i_v * (1.0 - i_v)
        vec_ref[7:8, :] += jnp.sum(d_loga * r_v, axis=0, keepdims=True) * (LRU_C * jax.nn.sigmoid(-lam_v))
        vec_ref[5:6, :] += jnp.sum(d_pr, axis=0, keepdims=True)
        vec_ref[6:7, :] += jnp.sum(d_pi, axis=0, keepdims=True)
        d_prb = d_pr.astype(BF16)
        d_pib = d_pi.astype(BF16)
        h_s[...] = dbx * i_v * m_v
        for h in range(HEADS):
            sl = slice(h * HEAD_DIM, (h + 1) * HEAD_DIM)
            h_s[:, sl] += _dot_nt(d_prb[:, sl], wr_ref[h]) + _dot_nt(d_pib[:, sl], wi_ref[h])
            dwr_ref[h] += _dot_tn(xcb[:, sl], d_prb[:, sl])
            dwi_ref[h] += _dot_tn(xcb[:, sl], d_pib[:, sl])
        d_xc = h_s[...]
        vec_ref[4:5, :] += jnp.sum(d_xc, axis=0, keepdims=True)
        vec_ref[0:1, :] += jnp.sum(d_xc * xa, axis=0, keepdims=True)
        d_xa = cw[0:1, :] * d_xc
        nxt = dxc_s[...]
        for k in range(1, CONV_K):
            vec_ref[k:k + 1, :] += jnp.sum(d_xc * taps[k], axis=0, keepdims=True)
            d_xa = d_xa + cw[k:k + 1, :] * _rows_advanced(d_xc, nxt, k)
        dz_ref[:, 0:D] = d_xa.astype(BF16)
        dxc_s[...] = d_xc[0:8, :]

    vec = pl.BlockSpec((1, D), lambda i: (0, 0))
    gate = pl.BlockSpec((HEADS, HEAD_DIM, HEAD_DIM), lambda i: (0, 0, 0))
    cur = lambda c: pl.BlockSpec((ta, D), lambda i: (nb - 1 - i, c))
    before = lambda c: pl.BlockSpec((16, D), lambda i: (jnp.maximum((nb - 1 - i) * per16 - 1, 0), c))
    return _call(
        body, name="branch_a_bwd", grid=(nb,),
        in_specs=[ANY, cur(0), cur(0), before(0), cur(1), cur(0), before(0),
                  pl.BlockSpec((CONV_K, D), lambda i: (0, 0)), vec, gate, vec, gate, vec, vec],
        out_specs=[pl.BlockSpec((ta, 2 * D), lambda i: (nb - 1 - i, 0)), pl.BlockSpec((8, D), lambda i: (0, 0)),
                   gate, gate],
        out_shape=[SDS(dz.shape, BF16), SDS((8, D), F32), SDS((HEADS, HEAD_DIM, HEAD_DIM), F32),
                   SDS((HEADS, HEAD_DIM, HEAD_DIM), F32)],
        scratch_shapes=[pltpu.VMEM((ta, D), F32)] * 6 + [pltpu.VMEM((8, D), F32)] * 3, aliases={0: 0},
        params=_cparams(("arbitrary",), 48),
        args=(dz, dya, z, z, z, hs, hs, conv_w, conv_b, w_r, b_r, w_i, b_i, lam), comm=comm)


def _in_bwd(dz, w_in_g, x, dh1, g_mix, comm=None):
    s = x.shape[0]
    tm = min(TM_ROWS, s)
    nj = N_SLOT

    def body(dz_ref, w_ref, x_ref, dh1_ref, g_ref, dx_ref, dg_ref, acc_s):
        i, j = pl.program_id(0), pl.program_id(1)

        @pl.when(j == 0)
        def _():
            acc_s[...] = jnp.zeros_like(acc_s)

        @pl.when((i == 0) & (j == 0))
        def _():
            dg_ref[...] = jnp.zeros_like(dg_ref)

        acc_s[...] += _dot_nt(dz_ref[...], w_ref[...])

        @pl.when(j == nj - 1)
        def _():
            xv = x_ref[...]
            rstd = lax.rsqrt(jnp.mean(xv * xv, axis=-1, keepdims=True) + NORM_EPS)
            xh = xv * rstd
            dn = acc_s[...]
            dg_ref[...] += jnp.sum(dn * xh, axis=0, keepdims=True)
            dhat = dn * g_ref[...]
            dx_ref[...] = dh1_ref[...] + rstd * (dhat - xh * jnp.mean(dhat * xh, axis=-1, keepdims=True))

    row = pl.BlockSpec((tm, D), lambda i, j: (i, 0))
    vec = pl.BlockSpec((1, D), lambda i, j: (0, 0))
    return _call(
        body, name="in_bwd", grid=(s // tm, nj),
        in_specs=[pl.BlockSpec((tm, W_IN_COLS), lambda i, j: (i, j)),
                  pl.BlockSpec((None, D, W_IN_COLS), lambda i, j: (j, 0, 0)), row, row, vec],
        out_specs=[row, vec],
        out_shape=[SDS((s, D), F32), SDS((1, D), F32)],
        scratch_shapes=[pltpu.VMEM((tm, D), F32)],
        params=_cparams(("arbitrary", "arbitrary"), 48), args=(dz, w_in_g, x, dh1, g_mix), comm=comm)


def _wgrad(name, a, b, nblk, a_split, b_split):
    s = a.shape[0]
    ts = min(TM_ROWS, s)
    a_w = a.shape[1] // nblk if a_split else a.shape[1]
    b_w = b.shape[1] // nblk if b_split else b.shape[1]

    def body(a_ref, b_ref, o_ref, acc_s):
        t = pl.program_id(1)

        @pl.when(t == 0)
        def _():
            acc_s[...] = jnp.zeros_like(acc_s)

        acc_s[...] += _dot_tn(a_ref[...].astype(BF16), b_ref[...].astype(BF16))

        @pl.when(t == pl.num_programs(1) - 1)
        def _():
            o_ref[...] = acc_s[...].astype(BF16)

    return pl.pallas_call(
        body, name=name, grid=(nblk, s // ts),
        in_specs=[pl.BlockSpec((ts, a_w), (lambda k, t: (t, k)) if a_split else (lambda k, t: (t, 0))),
                  pl.BlockSpec((ts, b_w), (lambda k, t: (t, k)) if b_split else (lambda k, t: (t, 0)))],
        out_specs=pl.BlockSpec((None, a_w, b_w), lambda k, t: (k, 0, 0)),
        out_shape=SDS((nblk, a_w, b_w), BF16),
        scratch_shapes=[pltpu.VMEM((a_w, b_w), F32)],
        compiler_params=_cparams(("arbitrary", "arbitrary"), 48),
    )(a, b)


def _wgrad_t(name, a_t, b, nblk, a_split, b_split, tokens):
    s = b.shape[0]
    ts = min(tokens, s)
    a_w = a_t.shape[0] // nblk if a_split else a_t.shape[0]
    b_w = b.shape[1] // nblk if b_split else b.shape[1]

    def body(a_ref, b_ref, o_ref, acc_s):
        t = pl.program_id(1)

        @pl.when(t == 0)
        def _():
            acc_s[...] = jnp.zeros_like(acc_s)

        acc_s[...] += _dot(a_ref[...], b_ref[...].astype(BF16))

        @pl.when(t == pl.num_programs(1) - 1)
        def _():
            o_ref[...] = acc_s[...].astype(BF16)

    return pl.pallas_call(
        body, name=name, grid=(nblk, s // ts),
        in_specs=[pl.BlockSpec((a_w, ts), (lambda k, t: (k, t)) if a_split else (lambda k, t: (0, t))),
                  pl.BlockSpec((ts, b_w), (lambda k, t: (t, k)) if b_split else (lambda k, t: (t, 0)))],
        out_specs=pl.BlockSpec((None, a_w, b_w), lambda k, t: (k, 0, 0)),
        out_shape=SDS((nblk, a_w, b_w), BF16),
        scratch_shapes=[pltpu.VMEM((a_w, b_w), F32)],
        compiler_params=_cparams(("arbitrary", "arbitrary"), 48),
    )(a_t, b)


def _place():
    x, y, c = lax.axis_index("x"), lax.axis_index("y"), lax.axis_index("c")
    return x, y, c


def _other_chips(x, y):
    return [(x, 1 - y, 2 * x + 1 - y), (1 - x, y, 2 * (1 - x) + y), (1 - x, 1 - y, 2 * (1 - x) + 1 - y)]


class _Plan:
    def __init__(self, arrays, out_shape, sems, start, finish, middle=None, middle_at=6):
        self.arrays, self.out_shape, self.sems, self.start, self.finish = arrays, out_shape, sems, start, finish
        self.middle, self.middle_at = middle, middle_at


def _gather_plan(shards, middle_at=6):
    n = len(shards)

    def copies(ins, outs, sems):
        send_sems, recv_sems, local_sems = sems
        x, y, c = _place()
        chip = 2 * x + y
        me = 2 * chip + c
        sib = (x, y, 1 - c)
        chips = _other_chips(x, y)

        def rc(k, t, src, blk, to):
            return pltpu.make_async_remote_copy(
                src_ref=src, dst_ref=outs[t].at[blk], send_sem=send_sems.at[k * n + t],
                recv_sem=recv_sems.at[k * n + t], device_id=to, device_id_type=MESH)

        (yx, yy, y_chip), (xx, xy, x_chip), _ = chips
        local = [pltpu.make_async_copy(ins[t], outs[t].at[me], local_sems.at[t]) for t in range(n)]
        sends = ([rc(0, t, ins[t], me, sib) for t in range(n)] + [rc(1, t, ins[t], me, (yx, yy, c)) for t in range(n)]
                 + [rc(2, t, ins[t], me, (xx, xy, c)) for t in range(n)])
        passed = [[rc(4 + j, t, outs[t].at[2 * pc + c], 2 * pc + c, sib) for t in range(n)]
                  for j, (_, _, pc) in enumerate(chips)]
        relays = [[rc(3, t, outs[t].at[2 * y_chip + c], 2 * y_chip + c, (xx, xy, c)) for t in range(n)],
                  [rc(3, t, outs[t].at[2 * x_chip + c], 2 * x_chip + c, (yx, yy, c)) for t in range(n)]]
        return rc, local, sends, passed, relays, chips, chip, c, sib

    def start(ins, outs, sems):
        _, local, sends, _, _, _, _, _, _ = copies(ins, outs, sems)
        for cp in local + sends:
            cp.start()

    def middle(ins, outs, sems):
        rc, _, _, passed, relays, chips, _, c, sib = copies(ins, outs, sems)
        for j in range(2):
            for t in range(n):
                rc(1 + j, t, ins[t], 2 * chips[j][2] + c, sib).wait_recv()
        for j in range(2):
            for cp in passed[j]:
                cp.start()

            @pl.when(c == j)
            def _():
                for cp in relays[j]:
                    cp.start()

    def finish(ins, outs, sems):
        rc, local, sends, passed, relays, chips, chip, c, sib = copies(ins, outs, sems)
        far = 2 * chips[2][2] + c
        for t in range(n):
            rc(3, t, ins[t], far, sib).wait_recv()
        for cp in passed[2]:
            cp.start()
        for t in range(n):
            rc(0, t, ins[t], 2 * chip + 1 - c, sib).wait_recv()
        for j, (px, py, pc) in enumerate(chips):
            for t in range(n):
                rc(4 + j, t, ins[t], 2 * pc + 1 - c, sib).wait_recv()
        for cp in sends + passed[0] + passed[1] + passed[2]:
            cp.wait_send()
        for j in range(2):
            @pl.when(c == j)
            def _():
                for cp in relays[j]:
                    cp.wait_send()
        for cp in local:
            cp.wait()

    return _Plan(list(shards), [SDS((N_SLOT,) + tuple(a.shape), a.dtype) for a in shards],
                 [pltpu.SemaphoreType.DMA((7 * n,)), pltpu.SemaphoreType.DMA((7 * n,)),
                  pltpu.SemaphoreType.DMA((n,))], start, finish, middle, middle_at)


def _sibling_plan(grads, whole=()):
    n, m = len(grads), len(whole)

    def copies(ins, outs, sems):
        send_sems, recv_sems = sems
        x, y, c = _place()
        sib = (x, y, 1 - c)

        def rc(t, src, dst):
            return pltpu.make_async_remote_copy(src_ref=src, dst_ref=dst, send_sem=send_sems.at[t],
                                                recv_sem=recv_sems.at[t], device_id=sib, device_id_type=MESH)
        return rc, c

    def start(ins, outs, sems):
        rc, c = copies(ins, outs, sems)
        for t in range(n):
            for j in range(4):
                rc(t, ins[t].at[2 * j + 1 - c], outs[t].at[j]).start()
        for t in range(n, n + m):
            rc(t, ins[t], outs[t]).start()

    def finish(ins, outs, sems):
        rc, _ = copies(ins, outs, sems)
        for t in range(n):
            rc(t, ins[t].at[pl.ds(0, 4)], outs[t]).wait()
        for t in range(n, n + m):
            rc(t, ins[t], outs[t]).wait()

    return _Plan(list(grads) + list(whole),
                 [SDS((4,) + tuple(g.shape[1:]), g.dtype) for g in grads] + [SDS(a.shape, a.dtype) for a in whole],
                 [pltpu.SemaphoreType.DMA((n + m,)), pltpu.SemaphoreType.DMA((n + m,))], start, finish)


def _chips_plan(parts, whole=()):
    n, m = len(parts), len(whole)

    def src_of(ins, t, pc):
        return ins[t].at[pc] if t < n else ins[t]

    def local_copies(ins, outs, sems, chip):
        return [pltpu.make_async_copy(src_of(ins, t, chip), outs[t].at[chip], sems[2].at[t]) for t in range(n + m)]

    def start(ins, outs, sems):
        send_sems, recv_sems, _ = sems
        x, y, c = _place()
        chip = 2 * x + y
        for cp in local_copies(ins, outs, sems, chip):
            cp.start()
        for px, py, pc in _other_chips(x, y):
            for t in range(n + m):
                pltpu.make_async_remote_copy(src_ref=src_of(ins, t, pc), dst_ref=outs[t].at[chip],
                                             send_sem=send_sems.at[t], recv_sem=recv_sems.at[t],
                                             device_id=(px, py, c), device_id_type=MESH).start()

    def finish(ins, outs, sems):
        send_sems, recv_sems, _ = sems
        x, y, c = _place()
        for t in range(n + m):
            three = outs[t].at[pl.ds(0, 3)]
            pltpu.make_async_remote_copy(src_ref=three, dst_ref=three, send_sem=send_sems.at[t],
                                         recv_sem=recv_sems.at[t], device_id=(x, y, c), device_id_type=MESH).wait()
        for cp in local_copies(ins, outs, sems, 2 * x + y):
            cp.wait()

    return _Plan(list(parts) + list(whole),
                 [SDS(p.shape, p.dtype) for p in parts] + [SDS((4,) + tuple(a.shape), a.dtype) for a in whole],
                 [pltpu.SemaphoreType.DMA((n + m,)), pltpu.SemaphoreType.DMA((n + m,)),
                  pltpu.SemaphoreType.DMA((n + m,))], start, finish)


def _exchange_plan(arr):
    def peers(x, y, c):
        flip = lambda v, f: 1 - v if f else v
        return [(flip(x, fx), flip(y, fy), flip(c, fc))
                for fx in (0, 1) for fy in (0, 1) for fc in (0, 1) if fx or fy or fc]

    def start(ins, outs, sems):
        x, y, c = _place()
        me = 4 * x + 2 * y + c
        pltpu.make_async_copy(ins[0], outs[0].at[me], sems[2].at[0]).start()
        for to in peers(x, y, c):
            pltpu.make_async_remote_copy(src_ref=ins[0], dst_ref=outs[0].at[me], send_sem=sems[0].at[0],
                                         recv_sem=sems[1].at[0], device_id=to, device_id_type=MESH).start()

    def finish(ins, outs, sems):
        x, y, c = _place()
        seven = outs[0].at[pl.ds(0, 7)]
        pltpu.make_async_remote_copy(src_ref=seven, dst_ref=seven, send_sem=sems[0].at[0], recv_sem=sems[1].at[0],
                                     device_id=(x, y, c), device_id_type=MESH).wait()
        pltpu.make_async_copy(ins[0], outs[0].at[4 * x + 2 * y + c], sems[2].at[0]).wait()

    return _Plan([arr], [SDS((N_SLOT,) + tuple(arr.shape), arr.dtype)],
                 [pltpu.SemaphoreType.DMA((1,)), pltpu.SemaphoreType.DMA((1,)), pltpu.SemaphoreType.DMA((1,))],
                 start, finish)


def _join(*plans):
    def cut(seq, sizes):
        out, at = [], 0
        for k in sizes:
            out.append(seq[at:at + k])
            at += k
        return out

    n_arr = [len(p.arrays) for p in plans]
    n_sem = [len(p.sems) for p in plans]

    def start(ins, outs, sems):
        for p, i, o, s in zip(plans, cut(ins, n_arr), cut(outs, n_arr), cut(sems, n_sem)):
            p.start(i, o, s)

    def finish(ins, outs, sems):
        for p, i, o, s in zip(plans, cut(ins, n_arr), cut(outs, n_arr), cut(sems, n_sem)):
            p.finish(i, o, s)

    def middle(ins, outs, sems):
        for p, i, o, s in zip(plans, cut(ins, n_arr), cut(outs, n_arr), cut(sems, n_sem)):
            if p.middle is not None:
                p.middle(i, o, s)

    return _Plan([a for p in plans for a in p.arrays], [o for p in plans for o in p.out_shape],
                 [s for p in plans for s in p.sems], start, finish,
                 middle if any(p.middle is not None for p in plans) else None,
                 max(p.middle_at for p in plans))


def _run_plan(name, plan):
    k = len(plan.arrays)

    def body(*refs):
        ins, outs, sems = refs[:k], refs[k:2 * k], refs[2 * k:]
        plan.start(ins, outs, sems)
        if plan.middle is not None:
            plan.middle(ins, outs, sems)
        plan.finish(ins, outs, sems)

    return pl.pallas_call(
        body, name=name, in_specs=[ANY] * k, out_specs=[ANY] * k, out_shape=plan.out_shape,
        scratch_shapes=plan.sems, compiler_params=pltpu.CompilerParams(has_side_effects=True),
    )(*plan.arrays)


def _call(body, *, name, grid, in_specs, out_specs, out_shape, scratch_shapes, params, args, comm=None,
          aliases=None, prefetch=()):
    aliases = aliases or {}
    n_pre = len(prefetch)

    def launch(fn, ins_specs, outs_specs, outs_shape, scratch, operands):
        spec = pltpu.PrefetchScalarGridSpec(num_scalar_prefetch=n_pre, grid=grid, in_specs=ins_specs,
                                            out_specs=outs_specs, scratch_shapes=scratch)
        return pl.pallas_call(fn, name=name, grid_spec=spec, out_shape=outs_shape, compiler_params=params,
                              input_output_aliases=aliases)(*prefetch, *[_small_in_hbm(a) for a in operands])

    if comm is None:
        return list(launch(body, in_specs, out_specs, out_shape, scratch_shapes, args)), []
    n_in, n_out, n_scr, k = len(in_specs), len(out_specs), len(scratch_shapes), len(comm.arrays)

    def wrapped(*refs):
        pre, refs = refs[:n_pre], refs[n_pre:]
        ins = refs[:n_in]
        c_in = refs[n_in:n_in + k]
        outs = refs[n_in + k:n_in + k + n_out]
        c_out = refs[n_in + k + n_out:n_in + 2 * k + n_out]
        scr = refs[n_in + 2 * k + n_out:n_in + 2 * k + n_out + n_scr]
        sems = refs[n_in + 2 * k + n_out + n_scr:]
        step, steps = pl.program_id(0), grid[0]
        for d in range(1, len(grid)):
            step, steps = step * grid[d] + pl.program_id(d), steps * grid[d]

        @pl.when(step == 0)
        def _():
            comm.start(c_in, c_out, sems)

        if comm.middle is not None:
            @pl.when(step == (comm.middle_at * steps) // 8)
            def _():
                comm.middle(c_in, c_out, sems)

        body(*pre, *ins, *outs, *scr)

        @pl.when(step == steps - 1)
        def _():
            comm.finish(c_in, c_out, sems)

    res = launch(wrapped, list(in_specs) + [ANY] * k, list(out_specs) + [ANY] * k,
                 list(out_shape) + list(comm.out_shape), list(scratch_shapes) + list(comm.sems),
                 tuple(args) + tuple(comm.arrays))
    return list(res[:n_out]), list(res[n_out:])


def _row_tile(rows):
    for t in (512, 256, 128, 64, 32, 16, 8):
        if rows % t == 0:
            return t
    return rows


def _pair_sum(name, g8, recv4, core):
    _, rows, cols = recv4.shape
    tr = _row_tile(rows)
    g42 = g8.reshape(4, 2, rows, cols)

    def body(c_ref, g_ref, r_ref, o_ref):
        del c_ref
        o_ref[...] = (g_ref[...].astype(F32) + r_ref[...].astype(F32)).astype(o_ref.dtype)

    return pl.pallas_call(
        body, name=name,
        grid_spec=pltpu.PrefetchScalarGridSpec(
            num_scalar_prefetch=1, grid=(4, rows // tr),
            in_specs=[pl.BlockSpec((None, None, tr, cols), lambda j, i, c_ref: (j, c_ref[0], i, 0)),
                      pl.BlockSpec((None, tr, cols), lambda j, i, c_ref: (j, i, 0))],
            out_specs=pl.BlockSpec((None, tr, cols), lambda j, i, c_ref: (j, i, 0))),
        out_shape=SDS(recv4.shape, g8.dtype),
        compiler_params=_cparams(("arbitrary", "arbitrary"), 32),
    )(core, g42, recv4)


def _add2(name, a, b):
    rows, cols = a.shape
    tr = _row_tile(rows)

    def body(a_ref, b_ref, o_ref):
        o_ref[...] = a_ref[...] + b_ref[...]

    blk = pl.BlockSpec((tr, cols), lambda i: (i, 0))
    return pl.pallas_call(body, name=name, grid=(rows // tr,), in_specs=[blk, blk], out_specs=blk,
                          out_shape=SDS(a.shape, a.dtype),
                          compiler_params=_cparams(("arbitrary",), 32))(a, b)


def _sum_terms(name, terms):
    k, rows, cols = terms.shape
    tr = _row_tile(rows)

    def body(r_ref, o_ref):
        acc = r_ref[0]
        for q in range(1, k):
            acc = acc + r_ref[q]
        o_ref[...] = acc

    return pl.pallas_call(body, name=name, grid=(rows // tr,),
                          in_specs=[pl.BlockSpec((k, tr, cols), lambda i: (0, i, 0))],
                          out_specs=pl.BlockSpec((tr, cols), lambda i: (i, 0)),
                          out_shape=SDS((rows, cols), terms.dtype),
                          compiler_params=_cparams(("arbitrary",), 32))(terms)


def _adam_update(g, w, m, v):
    c1 = 1.0 / (1.0 - ADAM_B1 ** ADAM_STEP)
    c2 = 1.0 / (1.0 - ADAM_B2 ** ADAM_STEP)
    mn = ADAM_B1 * m + (1.0 - ADAM_B1) * g
    vn = ADAM_B2 * v + (1.0 - ADAM_B2) * (g * g)
    delta = (-ADAM_LR) * ((mn * c1) / (jnp.sqrt(vn * c2) + ADAM_EPS) + ADAM_WD * w)
    return delta, mn, vn


def _adamw_many(name, gs, ws, ms, vs):
    n = len(gs)

    def body(*refs):
        for p in range(n):
            g, w, m, v = (refs[q * n + p][...] for q in range(4))
            d, mn, vn = _adam_update(g, w, m, v)
            refs[4 * n + p][...] = d
            refs[5 * n + p][...] = mn
            refs[6 * n + p][...] = vn

    full = [pl.BlockSpec(w.shape, lambda i: (0, 0)) for w in ws]
    shapes = [SDS(w.shape, F32) for w in ws]
    res = pl.pallas_call(body, name=name, grid=(1,), in_specs=full * 4, out_specs=full * 3, out_shape=shapes * 3,
                         compiler_params=_cparams(("arbitrary",), 32),
                         )(*[_small_in_hbm(a) for a in (*gs, *ws, *ms, *vs)])
    return [(res[p], res[n + p], res[2 * n + p]) for p in range(n)]


def _adamw(name, terms, w, m, v):
    k, rows, cols = terms.shape
    tr = _row_tile(rows)

    def body(t_ref, w_ref, m_ref, v_ref, g_ref, d_ref, mo_ref, vo_ref):
        g = t_ref[0].astype(F32)
        for q in range(1, k):
            g = g + t_ref[q].astype(F32)
        g_ref[...] = g
        d_ref[...], mo_ref[...], vo_ref[...] = _adam_update(g, w_ref[...], m_ref[...], v_ref[...])

    blk = pl.BlockSpec((tr, cols), lambda i: (i, 0))
    return pl.pallas_call(body, name=name, grid=(rows // tr,),
                          in_specs=[pl.BlockSpec((k, tr, cols), lambda i: (0, i, 0)), blk, blk, blk],
                          out_specs=[blk] * 4, out_shape=[SDS((rows, cols), F32)] * 4,
                          compiler_params=_cparams(("arbitrary",), 40),
                          )(*[pltpu.with_memory_space_constraint(a, pltpu.HBM) for a in (terms, w, m, v)])


def kernel(x, norm_mix_g, w_in, conv_w, conv_b, w_rgate, b_rgate, w_igate, b_igate, lru_lambda, w_out_a, sgu_ln_g, sgu_ln_b, sgu_w_s, sgu_b_s, w_out_b, w_out, norm_mlp_g, w_up, w_down, norm_final_g, loss_target, m_norm_mix_g, m_w_in, m_conv_w, m_conv_b, m_w_rgate, m_b_rgate, m_w_igate, m_b_igate, m_lru_lambda, m_w_out_a, m_sgu_ln_g, m_sgu_ln_b, m_sgu_w_s, m_sgu_b_s, m_w_out_b, m_w_out, m_norm_mlp_g, m_w_up, m_w_down, m_norm_final_g, v_norm_mix_g, v_w_in, v_conv_w, v_conv_b, v_w_rgate, v_b_rgate, v_w_igate, v_b_igate, v_lru_lambda, v_w_out_a, v_sgu_ln_g, v_sgu_ln_b, v_sgu_w_s, v_sgu_b_s, v_w_out_b, v_w_out, v_norm_mlp_g, v_w_up, v_w_down, v_norm_final_g):
    cx, cy, cc = _place()
    me = 4 * cx + 2 * cy + cc
    core = jnp.reshape(cc, (1,)).astype(jnp.int32)
    xs = x[0]
    tgt = loss_target[0]
    s = xs.shape[0]

    gate_shard = jnp.stack([w_rgate[0], w_igate[0]]).astype(BF16).reshape(2 * HEADS * 32, HEAD_DIM)
    vec_shard = jnp.concatenate([conv_w[0], b_rgate[0], b_igate[0]], axis=1)
    vec_shard = jnp.pad(vec_shard, ((0, 4), (0, 256 - vec_shard.shape[1])))
    shards = [w_in[0].astype(BF16), w_out_a[0].astype(BF16), w_out_b[0].astype(BF16), w_out[0].astype(BF16),
              w_up[0].astype(BF16), w_down[0].astype(BF16), gate_shard, vec_shard]
    (z, n1_t, w_in_g), (gate_g, vec_g) = _in_proj(xs, norm_mix_g, shards[0], _slot_order(cx, cy, cc),
                                                comm=_gather_plan(shards[6:8]))
    gates = gate_g.reshape(N_SLOT, 2, HEADS, 32, HEAD_DIM).transpose(1, 2, 0, 3, 4).reshape(2, HEADS, HEAD_DIM, HEAD_DIM)
    w_r_f, w_i_f = gates[0], gates[1]
    conv_w_f = vec_g[:, 0:4, 0:128].transpose(1, 0, 2).reshape(CONV_K, D)
    b_r_f = vec_g[:, 0:4, 128:160].transpose(1, 0, 2).reshape(1, D)
    b_i_f = vec_g[:, 0:4, 160:192].transpose(1, 0, 2).reshape(1, D)
    b_s_t = jnp.transpose(sgu_b_s[0])

    (ya, hs), (w_oa_g, w_ob_g, w_out_g, w_up_g) = _branch_a_fwd(
        z, conv_w_f, conv_b, w_r_f, b_r_f, w_i_f, b_i_f, lru_lambda, comm=_gather_plan(shards[1:5]))
    w_oa_f = w_oa_g.reshape(D, D)
    w_ob_f = w_ob_g.reshape(D, D)
    w_out_f = w_out_g.reshape(D, D)
    (yb,), _ = _branch_b_fwd(z, sgu_ln_g, sgu_ln_b, sgu_w_s[0], b_s_t)
    (pa, pb, merged, h1), (w_down_g,) = _merge_out(ya, yb, z, xs, w_oa_f, w_ob_f, w_out_f,
                                                   comm=_gather_plan(shards[5:6], middle_at=4))
    w_down_f = w_down_g.reshape(N_SLOT * FF_COLS, D)
    gf2 = norm_final_g.reshape(1, D)
    r_act, act_t, n2_t, dh2, loss_acc, d_gfin = _mlp_fwd(h1, norm_mlp_g, w_up_g, w_down_f, gf2, tgt)

    def pair(names, grads, recv):
        return [_pair_sum("pair_sum_" + nm, g, r, core) for nm, g, r in zip(names, grads, recv)]

    g_down = _wgrad_t("wgrad_down", act_t, dh2, N_SLOT // 2, True, False, 2048)
    g_down = g_down.reshape(N_SLOT, FF_COLS, D)
    (df, dh1, d_gmlp), (r_down,) = _mlp_bwd(dh2, r_act, w_down_f, w_up_g, h1, norm_mlp_g,
                                            comm=_sibling_plan([g_down]))
    (p_down,) = pair(["down"], [g_down], [r_down])
    g_up = _wgrad_t("wgrad_up", n2_t, df, N_SLOT, False, True, 4096)
    g_out = _wgrad("wgrad_out", merged, dh1, 1, False, False).reshape(N_SLOT, D // N_SLOT, D)
    (dz, dpa, dpb, dya, dyb), (got_down, r_up, r_out) = _merge_bwd(
        dh1, z, pa, pb, w_out_f, w_oa_f, w_ob_f, comm=_join(_chips_plan([p_down]), _sibling_plan([g_up, g_out])))
    p_up, p_out = pair(["up", "out"], [g_up, g_out], [r_up, r_out])
    g_oa = _wgrad("wgrad_out_a", ya, dpa, 1, False, False).reshape(N_SLOT, D // N_SLOT, D)
    g_ob = _wgrad("wgrad_out_b", yb, dpb, 1, False, False).reshape(N_SLOT, D // N_SLOT, D)
    (dz, d_ws, d_bs, d_ln), (got_up, r_oa, r_ob) = _branch_b_bwd(
        dz, dyb, z, sgu_ln_g, sgu_ln_b, sgu_w_s[0], b_s_t,
        comm=_join(_chips_plan([p_up]), _sibling_plan([g_oa, g_ob])))
    p_oa, p_ob = pair(["out_a", "out_b"], [g_oa, g_ob], [r_oa, r_ob])
    (dz, d_vec, d_wr, d_wi), (got_out, got_oa, got_ob) = _branch_a_bwd(
        dz, dya, z, hs, conv_w_f, conv_b, w_r_f, b_r_f, w_i_f, b_i_f, lru_lambda,
        comm=_chips_plan([p_out, p_oa, p_ob]))
    g_in = _wgrad_t("wgrad_in", n1_t, dz, N_SLOT, False, True, 4096)
    g_gate = jnp.stack([d_wr, d_wi]).reshape(2, HEADS, N_SLOT, 32, HEAD_DIM).transpose(2, 0, 1, 3, 4)
    g_gate = g_gate.reshape(N_SLOT, 2 * HEADS * 32, HEAD_DIM).astype(BF16)

    d_bs_row = jnp.pad(d_bs[:, :, 0].reshape(1, GROUPS * CHUNK), ((0, 0), (0, D - GROUPS * CHUNK)))
    vecs = jnp.concatenate([d_vec, jnp.concatenate([d_ln[0:2], d_gmlp, d_gfin, d_bs_row, jnp.zeros((3, D), F32)])])
    d_ws2 = d_ws.reshape(GROUPS * CHUNK, CHUNK)
    r_in, r_gate, r_vecs, r_ws = _run_plan("rs_sibling_in", _sibling_plan([g_in, g_gate], [vecs, d_ws2]))
    p_in, p_gate = pair(["in", "gate"], [g_in, g_gate], [r_in, r_gate])
    vecs_chip = _add2("pair_sum_vecs", vecs, r_vecs)
    ws_chip = _add2("pair_sum_ws", d_ws2, r_ws)
    (dx, d_gmix), (got_in, got_gate, got_vecs, got_ws) = _in_bwd(
        dz, w_in_g, xs, dh1, norm_mix_g, comm=_chips_plan([p_in, p_gate], [vecs_chip, ws_chip]))
    vecs_sum = _sum_terms("sum_vecs", got_vecs)
    last = jnp.concatenate([d_gmix, jnp.pad(loss_acc[0:1], ((0, 0), (0, D - 128))), jnp.zeros((6, D), F32)])
    (last_all,) = _run_plan("exchange_last", _exchange_plan(last))
    last_sum = _sum_terms("sum_last", last_all)
    loss = last_sum[1, 0]
    got = [got_in, got_oa, got_ob, got_out, got_up, got_down, got_gate]

    def step(nm, terms, w, m, v, rows, cols):
        g, d, mn, vn = _adamw("adamw_" + nm, terms.reshape(4, rows, cols), w.reshape(rows, cols),
                              m.reshape(rows, cols), v.reshape(rows, cols))
        return [a.reshape(w.shape) for a in (g, d, mn, vn)]

    o_in = step("in", got[0], w_in, m_w_in, v_w_in, D, W_IN_COLS)
    o_oa = step("out_a", got[1], w_out_a, m_w_out_a, v_w_out_a, D // N_SLOT, D)
    o_ob = step("out_b", got[2], w_out_b, m_w_out_b, v_w_out_b, D // N_SLOT, D)
    o_out = step("out", got[3], w_out, m_w_out, v_w_out, D // N_SLOT, D)
    o_up = step("up", got[4], w_up, m_w_up, v_w_up, D, FF_COLS)
    o_down = step("down", got[5], w_down, m_w_down, v_w_down, FF_COLS, D)
    gate_w = jnp.stack([w_rgate[0], w_igate[0]]).reshape(2 * HEADS * 32, HEAD_DIM)
    gate_m = jnp.stack([m_w_rgate[0], m_w_igate[0]]).reshape(2 * HEADS * 32, HEAD_DIM)
    gate_v = jnp.stack([v_w_rgate[0], v_w_igate[0]]).reshape(2 * HEADS * 32, HEAD_DIM)
    o_gate = _adamw("adamw_gate", got[6], gate_w, gate_m, gate_v)
    o_gate = [a.reshape(2, 1, HEADS, 32, HEAD_DIM) for a in o_gate]
    o_wr = [a[0] for a in o_gate]
    o_wi = [a[1] for a in o_gate]

    def own(full, width):
        return lax.dynamic_slice_in_dim(full, me * width, width, axis=1)

    small_g = {
        "norm_mix_g": last_sum[0:1], "conv_w": own(vecs_sum[0:4], 128), "conv_b": vecs_sum[4:5],
        "b_rgate": own(vecs_sum[5:6].reshape(HEADS, HEAD_DIM), 32),
        "b_igate": own(vecs_sum[6:7].reshape(HEADS, HEAD_DIM), 32),
        "lru_lambda": vecs_sum[7:8], "sgu_ln_g": vecs_sum[8:9], "sgu_ln_b": vecs_sum[9:10],
        "norm_mlp_g": vecs_sum[10:11], "norm_final_g": vecs_sum[11:12],
        "sgu_b_s": vecs_sum[12, 0:GROUPS * CHUNK].reshape(GROUPS, CHUNK),
    }
    small_w = {"norm_mix_g": (norm_mix_g, m_norm_mix_g, v_norm_mix_g), "conv_w": (conv_w, m_conv_w, v_conv_w),
               "conv_b": (conv_b, m_conv_b, v_conv_b), "b_rgate": (b_rgate, m_b_rgate, v_b_rgate),
               "b_igate": (b_igate, m_b_igate, v_b_igate), "lru_lambda": (lru_lambda, m_lru_lambda, v_lru_lambda),
               "sgu_ln_g": (sgu_ln_g, m_sgu_ln_g, v_sgu_ln_g), "sgu_ln_b": (sgu_ln_b, m_sgu_ln_b, v_sgu_ln_b),
               "norm_mlp_g": (norm_mlp_g, m_norm_mlp_g, v_norm_mlp_g),
               "norm_final_g": (norm_final_g, m_norm_final_g, v_norm_final_g),
               "sgu_b_s": (sgu_b_s, m_sgu_b_s, v_sgu_b_s), "sgu_w_s": (sgu_w_s, m_sgu_w_s, v_sgu_w_s)}
    order = list(small_g)
    as2d = lambda k, a: a.reshape(small_g[k].shape)
    upd = _adamw_many("adamw_small", [small_g[k] for k in order], *[[as2d(k, small_w[k][q]) for k in order]
                                                                     for q in range(3)])
    o_small = {k: [a.reshape(small_w[k][0].shape) for a in (small_g[k],) + u] for k, u in zip(order, upd)}
    ws3 = [a[0].reshape(GROUPS * CHUNK, CHUNK) for a in small_w.pop("sgu_w_s")]
    o_small["sgu_w_s"] = [a.reshape(sgu_w_s.shape) for a in _adamw("adamw_ws", got_ws, *ws3)]

    per_weight = {"norm_mix_g": o_small["norm_mix_g"], "w_in": o_in, "conv_w": o_small["conv_w"],
                  "conv_b": o_small["conv_b"], "w_rgate": o_wr, "b_rgate": o_small["b_rgate"], "w_igate": o_wi,
                  "b_igate": o_small["b_igate"], "lru_lambda": o_small["lru_lambda"], "w_out_a": o_oa,
                  "sgu_ln_g": o_small["sgu_ln_g"], "sgu_ln_b": o_small["sgu_ln_b"], "sgu_w_s": o_small["sgu_w_s"],
                  "sgu_b_s": o_small["sgu_b_s"], "w_out_b": o_ob, "w_out": o_out, "norm_mlp_g": o_small["norm_mlp_g"],
                  "w_up": o_up, "w_down": o_down, "norm_final_g": o_small["norm_final_g"]}
    names_w = list(per_weight)
    return (loss, dx[None], *[per_weight[k][0] for k in names_w], *[per_weight[k][1] for k in names_w],
            *[per_weight[k][2] for k in names_w], *[per_weight[k][3] for k in names_w])
```

```python
import jax
import jax.numpy as jnp
from jax import lax
from jax.experimental import pallas as pl
from jax.experimental.pallas import tpu as pltpu

F32 = jnp.float32
BF16 = jnp.bfloat16
SDS = jax.ShapeDtypeStruct
MESH = pl.DeviceIdType.MESH
ANY = pl.BlockSpec(memory_space=pltpu.HBM)

D = 1024
N_SLOT = 8
W_IN_COLS = 768
FF_COLS = 512
HEADS, HEAD_DIM = 4, 256
GROUPS, GROUP_DIM = 4, 256
CHUNK = 128
CONV_K = 4
NORM_EPS = 1e-6
LN_EPS = 1e-5
LRU_C = 8.0
ADAM_LR, ADAM_B1, ADAM_B2, ADAM_EPS, ADAM_WD, ADAM_STEP = 0.001, 0.9, 0.999, 1e-08, 0.01, 10

TM_ROWS = 1024
TM_MERGE = 512
T_BRANCH_A = 512
T_BRANCH_B = 256
MiB = 1024 * 1024
SMALL_OPERAND = 16 * 1024

_GELU_C = 0.7978845608028654
_GELU_A = 0.044715


def _small_in_hbm(a):
    return pltpu.with_memory_space_constraint(a, pltpu.HBM) if a.size <= SMALL_OPERAND else a


def _cparams(sem, vmem_mib):
    return pltpu.CompilerParams(dimension_semantics=sem, vmem_limit_bytes=vmem_mib * MiB)


def _gelu(x):
    t = jnp.tanh(_GELU_C * (x + _GELU_A * x * x * x))
    return 0.5 * x * (1.0 + t)


def _gelu_and_grad(x):
    x2 = x * x
    t = jnp.tanh(_GELU_C * x * (1.0 + _GELU_A * x2))
    g = 0.5 * x * (1.0 + t)
    dg = 0.5 * (1.0 + t) + 0.5 * x * (1.0 - t * t) * _GELU_C * (1.0 + 3.0 * _GELU_A * x2)
    return g, dg


def _softplus(x):
    return jnp.maximum(x, 0.0) + jnp.log1p(jnp.exp(-jnp.abs(x)))


def _dot(a, b):
    return jnp.dot(a, b, preferred_element_type=F32)


def _dot_nt(a, b):
    return lax.dot_general(a, b, (((1,), (1,)), ((), ())), preferred_element_type=F32)


def _dot_tn(a, b):
    return lax.dot_general(a, b, (((0,), (0,)), ((), ())), preferred_element_type=F32)


def _rows_shifted(prev8, cur, k):
    ext = jnp.concatenate([prev8, cur], axis=0)
    return pltpu.roll(ext, k, 0)[8:]


def _rows_advanced(cur, next8, k):
    t = cur.shape[0]
    ext = jnp.concatenate([cur, next8], axis=0)
    return pltpu.roll(ext, t + 8 - k, 0)[:t]


def _first_second(x, y, c):
    ny, nx, far = _other_chips(x, y)
    pick = lambda a, b: a * (1 - c) + b * c
    first = tuple(pick(a, b) for a, b in zip(ny, nx))
    second = tuple(pick(b, a) for a, b in zip(ny, nx))
    return first, second, far


def _slot_order(x, y, c):
    chip = 2 * x + y
    first, second, far = _first_second(x, y, c)
    order = [2 * chip + c, 2 * chip + 1 - c, 2 * first[2] + c, 2 * second[2] + 1 - c, 2 * second[2] + c,
             2 * first[2] + 1 - c, 2 * far[2] + c, 2 * far[2] + 1 - c]
    return jnp.stack(order).astype(jnp.int32)


def _in_proj(x, g_mix, w_in_own, order, comm=None):
    s = x.shape[0]
    tm = min(TM_ROWS, s)
    ni = s // tm

    def body(order_ref, x_ref, g_ref, own_ref, z_ref, nt_ref, wg_ref, n_s, w_s, send_sems, recv_sems, local_sems):
        j, i = pl.program_id(0), pl.program_id(1)
        px, py, c = _place()
        chip = 2 * px + py
        me = 2 * chip + c
        sib = (px, py, 1 - c)
        chips = _other_chips(px, py)

        def rc(k, src, blk, to):
            return pltpu.make_async_remote_copy(src_ref=src, dst_ref=w_s.at[blk], send_sem=send_sems.at[k],
                                                recv_sem=recv_sems.at[k], device_id=to, device_id_type=MESH)

        del chips
        first, second, far = _first_second(px, py, c)
        blocks = [2 * first[2] + c, 2 * second[2] + c, 2 * far[2] + c]
        own_in = pltpu.make_async_copy(own_ref, w_s.at[me], local_sems.at[0])
        to_first = rc(1, own_ref, me, (first[0], first[1], c))
        to_second = rc(2, own_ref, me, (second[0], second[1], c))
        relay = rc(3, w_s.at[blocks[0]], blocks[0], (second[0], second[1], c))
        sends = [rc(0, own_ref, me, sib), to_first, to_second, relay]
        passed = [rc(4 + q, w_s.at[blk], blk, sib) for q, blk in enumerate(blocks)]
        keep = pltpu.make_async_copy(w_s, wg_ref, local_sems.at[1])

        @pl.when((i == 0) & (j == 0))
        def _():
            own_in.start()
            sends[0].start()
            to_first.start()
            own_in.wait()

        @pl.when((i == 0) & (j == 1))
        def _():
            rc(0, own_ref, 2 * chip + 1 - c, sib).wait_recv()

        for q, blk in enumerate(blocks):
            @pl.when((i == 0) & (j == 2 + 2 * q))
            def _():
                rc(1 + q, own_ref, blk, sib).wait_recv()
                passed[q].start()
                if q == 0:
                    to_second.start()
                    relay.start()

            @pl.when((i == 0) & (j == 3 + 2 * q))
            def _():
                rc(4 + q, own_ref, order_ref[j], sib).wait_recv()

        rows = pl.ds(pl.multiple_of(i * tm, tm), tm)

        @pl.when(j == 0)
        def _():
            xv = x_ref[...]
            rstd = lax.rsqrt(jnp.mean(xv * xv, axis=-1, keepdims=True) + NORM_EPS)
            nb = (xv * rstd * g_ref[...]).astype(BF16)
            n_s[rows, :] = nb
            nt_ref[...] = nb.T

        z_ref[...] = _dot(n_s[rows, :], w_s[order_ref[j]]).astype(BF16)

        @pl.when((i == 0) & (j == N_SLOT - 1))
        def _():
            keep.start()

        @pl.when((i == ni - 1) & (j == N_SLOT - 1))
        def _():
            for cp in sends + passed:
                cp.wait_send()
            keep.wait()

    first_pass = lambda j, i, o: (jnp.where(j == 0, i, ni - 1), 0)
    (z, n1, w_in_g), extra = _call(
        body, name="in_proj", grid=(N_SLOT, ni), prefetch=(order,),
        in_specs=[pl.BlockSpec((tm, D), first_pass),
                  pl.BlockSpec((1, D), lambda j, i, o: (0, 0)), ANY],
        out_specs=[pl.BlockSpec((tm, W_IN_COLS), lambda j, i, o: (i, o[j])),
                   pl.BlockSpec((D, tm), lambda j, i, o: (0, jnp.where(j == 0, i, ni - 1))), ANY],
        out_shape=[SDS((s, N_SLOT * W_IN_COLS), BF16), SDS((D, s), BF16), SDS((N_SLOT, D, W_IN_COLS), BF16)],
        scratch_shapes=[pltpu.VMEM((s, D), BF16), pltpu.VMEM((N_SLOT, D, W_IN_COLS), BF16),
                        pltpu.SemaphoreType.DMA((7,)), pltpu.SemaphoreType.DMA((7,)), pltpu.SemaphoreType.DMA((2,))],
        params=_cparams(("arbitrary", "arbitrary"), 56), args=(x, g_mix, w_in_own), comm=comm)
    return (z, n1, w_in_g), extra


def _lru_gates(xc, xcb, wr_ref, br, wi_ref, bi, sp_lam, a_s, b_s, r_s=None, i_s=None, m_s=None):
    for h in range(HEADS):
        sl = slice(h * HEAD_DIM, (h + 1) * HEAD_DIM)
        r = jax.nn.sigmoid(_dot(xcb[:, sl], wr_ref[h]) + br[:, sl])
        ig = jax.nn.sigmoid(_dot(xcb[:, sl], wi_ref[h]) + bi[:, sl])
        log_a = (-LRU_C) * r * sp_lam[:, sl]
        a = jnp.exp(log_a)
        mult = jnp.sqrt(-jnp.tanh(log_a) * (a * a + 1.0))
        a_s[:, sl] = a
        b_s[:, sl] = xc[:, sl] * ig * mult
        if r_s is not None:
            r_s[:, sl] = r
            i_s[:, sl] = ig
            m_s[:, sl] = mult


def _conv_fwd(xa, prev8, cw, cb):
    taps = [xa] + [_rows_shifted(prev8, xa, k) for k in range(1, CONV_K)]
    xc = cb + cw[0:1, :] * xa
    for k in range(1, CONV_K):
        xc = xc + cw[k:k + 1, :] * taps[k]
    return xc, taps


def _branch_a_fwd(z, conv_w, conv_b, w_r, b_r, w_i, b_i, lam, comm=None):
    s = z.shape[0]
    ta = min(T_BRANCH_A, s)
    per16 = ta // 16

    def body(xa_ref, xp_ref, ga_ref, cw_ref, cb_ref, wr_ref, br_ref, wi_ref, bi_ref, lam_ref,
             ya_ref, hs_ref, a_s, b_s, h_s, carry_s):
        i = pl.program_id(0)

        @pl.when(i == 0)
        def _():
            carry_s[...] = jnp.zeros_like(carry_s)

        xa = xa_ref[...].astype(F32)
        prev8 = jnp.where(i > 0, xp_ref[...].astype(F32)[8:16], 0.0)
        xc, _ = _conv_fwd(xa, prev8, cw_ref[...], cb_ref[...])
        sp_lam = _softplus(-lam_ref[...])
        _lru_gates(xc, xc.astype(BF16), wr_ref, br_ref[...], wi_ref, bi_ref[...], sp_lam, a_s, b_s)

        row = lax.broadcasted_iota(jnp.int32, (8, D), 0)

        def group(g, carry):
            off = pl.multiple_of(g * 8, 8)
            a8 = a_s[pl.ds(off, 8), :]
            b8 = b_s[pl.ds(off, 8), :]
            for d in (1, 2, 4):
                a_sh = jnp.where(row >= d, pltpu.roll(a8, d, 0), 1.0)
                b_sh = jnp.where(row >= d, pltpu.roll(b8, d, 0), 0.0)
                b8 = a8 * b_sh + b8
                a8 = a8 * a_sh
            h8 = b8 + a8 * carry
            h_s[pl.ds(off, 8), :] = h8
            return jnp.broadcast_to(h8[7:8, :], (8, D))

        carry_s[...] = lax.fori_loop(0, ta // 8, group, carry_s[...])
        hs = h_s[...]
        hs_ref[...] = hs.astype(BF16)
        ya_ref[...] = (hs * _gelu(ga_ref[...].astype(F32))).astype(BF16)

    vec = pl.BlockSpec((1, D), lambda i: (0, 0))
    gate = pl.BlockSpec((HEADS, HEAD_DIM, HEAD_DIM), lambda i: (0, 0, 0))
    return _call(
        body, name="branch_a_fwd", grid=(s // ta,),
        in_specs=[pl.BlockSpec((ta, D), lambda i: (i, 0)),
                  pl.BlockSpec((16, D), lambda i: (jnp.maximum(i * per16 - 1, 0), 0)),
                  pl.BlockSpec((ta, D), lambda i: (i, 1)),
                  pl.BlockSpec((CONV_K, D), lambda i: (0, 0)), vec, gate, vec, gate, vec, vec],
        out_specs=[pl.BlockSpec((ta, D), lambda i: (i, 0)), pl.BlockSpec((ta, D), lambda i: (i, 0))],
        out_shape=[SDS((s, D), BF16), SDS((s, D), BF16)],
        scratch_shapes=[pltpu.VMEM((ta, D), F32), pltpu.VMEM((ta, D), F32), pltpu.VMEM((ta, D), F32),
                        pltpu.VMEM((8, D), F32)],
        params=_cparams(("arbitrary",), 40), args=(z, z, z, conv_w, conv_b, w_r, b_r, w_i, b_i, lam), comm=comm)


def _sgu_common(ub, vb, lg, lb, with_grad):
    if with_grad:
        u, du = _gelu_and_grad(ub)
        v, dv = _gelu_and_grad(vb)
    else:
        u, v, du, dv = _gelu(ub), _gelu(vb), None, None
    mu = jnp.mean(v, axis=-1, keepdims=True)
    vc = v - mu
    rstd = lax.rsqrt(jnp.mean(vc * vc, axis=-1, keepdims=True) + LN_EPS)
    vhat = vc * rstd
    vln = vhat * lg + lb
    return u, du, dv, rstd, vhat, vln


def _masked_ws(ws_ref):
    t = lax.broadcasted_iota(jnp.int32, (CHUNK, CHUNK), 0)
    c = lax.broadcasted_iota(jnp.int32, (CHUNK, CHUNK), 1)
    keep = c <= t
    return [jnp.where(keep, ws_ref[g], 0.0).astype(BF16) for g in range(GROUPS)]


def _branch_b_fwd(z, ln_g, ln_b, w_s, b_s_t, comm=None):
    s = z.shape[0]
    tb = min(T_BRANCH_B, s)

    def body(ub_ref, vb_ref, lg_ref, lb_ref, ws_ref, bs_ref, yb_ref):
        u, _, _, _, _, vln = _sgu_common(ub_ref[...].astype(F32), vb_ref[...].astype(F32),
                                         lg_ref[...], lb_ref[...], False)
        vlnb = vln.astype(BF16)
        wm = _masked_ws(ws_ref)
        bs = bs_ref[...]
        for c in range(tb // CHUNK):
            rs = slice(c * CHUNK, (c + 1) * CHUNK)
            for g in range(GROUPS):
                cs = slice(g * GROUP_DIM, (g + 1) * GROUP_DIM)
                sp = _dot(wm[g], vlnb[rs, cs]) + bs[:, g:g + 1]
                yb_ref[rs, cs] = (u[rs, cs] * sp).astype(BF16)

    vec = pl.BlockSpec((1, D), lambda i: (0, 0))
    return _call(
        body, name="branch_b_fwd", grid=(s // tb,),
        in_specs=[pl.BlockSpec((tb, D), lambda i: (i, 2)), pl.BlockSpec((tb, D), lambda i: (i, 3)), vec, vec,
                  pl.BlockSpec((GROUPS, CHUNK, CHUNK), lambda i: (0, 0, 0)),
                  pl.BlockSpec((CHUNK, GROUPS), lambda i: (0, 0))],
        out_specs=[pl.BlockSpec((tb, D), lambda i: (i, 0))],
        out_shape=[SDS((s, D), BF16)], scratch_shapes=[],
        params=_cparams(("arbitrary",), 40), args=(z, z, ln_g, ln_b, w_s, b_s_t), comm=comm)


def _merge_out(ya, yb, z, x, w_oa, w_ob, w_out, comm=None):
    s = x.shape[0]
    tm = min(TM_MERGE, s)

    def body(ya_ref, yb_ref, ma_ref, mb_ref, x_ref, woa_ref, wob_ref, wo_ref, pa_ref, pb_ref, mg_ref, h1_ref):
        pa = _dot(ya_ref[...], woa_ref[...])
        pb = _dot(yb_ref[...], wob_ref[...])
        merged = (jax.nn.sigmoid(ma_ref[...].astype(F32)) * pa
                  + jax.nn.sigmoid(mb_ref[...].astype(F32)) * pb).astype(BF16)
        pa_ref[...] = pa.astype(BF16)
        pb_ref[...] = pb.astype(BF16)
        mg_ref[...] = merged
        h1_ref[...] = x_ref[...] + _dot(merged, wo_ref[...])

    row = pl.BlockSpec((tm, D), lambda i: (i, 0))
    wsp = pl.BlockSpec((D, D), lambda i: (0, 0))
    return _call(
        body, name="merge_out", grid=(s // tm,),
        in_specs=[row, row, pl.BlockSpec((tm, D), lambda i: (i, 4)), pl.BlockSpec((tm, D), lambda i: (i, 5)),
                  row, wsp, wsp, wsp],
        out_specs=[row, row, row, row],
        out_shape=[SDS((s, D), BF16), SDS((s, D), BF16), SDS((s, D), BF16), SDS((s, D), F32)], scratch_shapes=[],
        params=_cparams(("arbitrary",), 48), args=(ya, yb, z, z, x, w_oa, w_ob, w_out), comm=comm)


def _mlp_fwd(h1, g_mlp, w_up_g, w_down, g_fin, tgt):
    s = h1.shape[0]
    tm = min(TM_ROWS, s)
    nj = N_SLOT

    def body(h1_ref, gm_ref, wu_ref, wd_ref, gf_ref, t_ref, r_ref, at_ref, n2t_ref, dh2_ref, loss_ref, dgf_ref,
             n2_s, acc_s):
        i, j = pl.program_id(0), pl.program_id(1)

        @pl.when(j == 0)
        def _():
            hv = h1_ref[...]
            rstd = lax.rsqrt(jnp.mean(hv * hv, axis=-1, keepdims=True) + NORM_EPS)
            nb = (hv * rstd * gm_ref[...]).astype(BF16)
            n2_s[...] = nb
            n2t_ref[...] = nb.T
            acc_s[...] = jnp.zeros_like(acc_s)

        @pl.when((i == 0) & (j == 0))
        def _():
            loss_ref[...] = jnp.zeros_like(loss_ref)
            dgf_ref[...] = jnp.zeros_like(dgf_ref)

        r = jnp.maximum(_dot(n2_s[...], wu_ref[...]), 0.0)
        r_ref[...] = r.astype(BF16)
        act = (r * r).astype(BF16)
        at_ref[...] = act.T
        acc_s[...] += _dot(act, wd_ref[...])

        @pl.when(j == nj - 1)
        def _():
            h2 = h1_ref[...] + acc_s[...]
            rstd = lax.rsqrt(jnp.mean(h2 * h2, axis=-1, keepdims=True) + NORM_EPS)
            hh = h2 * rstd
            gf = gf_ref[...]
            e = hh * gf - t_ref[...]
            loss_ref[...] += jnp.sum(e * e) * (0.5 / D)
            dy = e * (1.0 / D)
            dgf_ref[...] += jnp.sum(dy * hh, axis=0, keepdims=True)
            dhh = dy * gf
            dh2_ref[...] = rstd * (dhh - hh * jnp.mean(dhh * hh, axis=-1, keepdims=True))

    row = pl.BlockSpec((tm, D), lambda i, j: (i, 0))
    vec = pl.BlockSpec((1, D), lambda i, j: (0, 0))
    return pl.pallas_call(
        body, name="mlp_fwd", grid=(s // tm, nj),
        in_specs=[row, vec, pl.BlockSpec((None, D, FF_COLS), lambda i, j: (j, 0, 0)),
                  pl.BlockSpec((FF_COLS, D), lambda i, j: (j, 0)), vec, row],
        out_specs=[pl.BlockSpec((tm, FF_COLS), lambda i, j: (i, j)), pl.BlockSpec((FF_COLS, tm), lambda i, j: (j, i)),
                   pl.BlockSpec((D, tm), lambda i, j: (0, i)), row, pl.BlockSpec((8, 128), lambda i, j: (0, 0)), vec],
        out_shape=[SDS((s, nj * FF_COLS), BF16), SDS((nj * FF_COLS, s), BF16), SDS((D, s), BF16), SDS((s, D), F32),
                   SDS((8, 128), F32), SDS((1, D), F32)],
        scratch_shapes=[pltpu.VMEM((tm, D), BF16), pltpu.VMEM((tm, D), F32)],
        compiler_params=_cparams(("arbitrary", "arbitrary"), 52),
    )(h1, _small_in_hbm(g_mlp), w_up_g, w_down, _small_in_hbm(g_fin), tgt)


def _mlp_bwd(dh2, r, w_down, w_up_g, h1, g_mlp, comm=None):
    s = h1.shape[0]
    tm = min(TM_ROWS, s)
    nj = N_SLOT

    def body(dh2_ref, r_ref, wd_ref, wu_ref, h1_ref, gm_ref, df_ref, dh1_ref, dgm_ref, dh2b_s, acc_s):
        i, j = pl.program_id(0), pl.program_id(1)

        @pl.when(j == 0)
        def _():
            dh2b_s[...] = dh2_ref[...].astype(BF16)
            acc_s[...] = jnp.zeros_like(acc_s)

        @pl.when((i == 0) & (j == 0))
        def _():
            dgm_ref[...] = jnp.zeros_like(dgm_ref)

        d_act = _dot_nt(dh2b_s[...], wd_ref[...])
        df = (d_act * (2.0 * r_ref[...].astype(F32))).astype(BF16)
        df_ref[...] = df
        acc_s[...] += _dot_nt(df, wu_ref[...])

        @pl.when(j == nj - 1)
        def _():
            hv = h1_ref[...]
            rstd = lax.rsqrt(jnp.mean(hv * hv, axis=-1, keepdims=True) + NORM_EPS)
            hh = hv * rstd
            dn2 = acc_s[...]
            dgm_ref[...] += jnp.sum(dn2 * hh, axis=0, keepdims=True)
            dhat = dn2 * gm_ref[...]
            dh1_ref[...] = dh2_ref[...] + rstd * (dhat - hh * jnp.mean(dhat * hh, axis=-1, keepdims=True))

    row = pl.BlockSpec((tm, D), lambda i, j: (i, 0))
    vec = pl.BlockSpec((1, D), lambda i, j: (0, 0))
    ffb = pl.BlockSpec((tm, FF_COLS), lambda i, j: (i, j))
    return _call(
        body, name="mlp_bwd", grid=(s // tm, nj),
        in_specs=[row, ffb, pl.BlockSpec((FF_COLS, D), lambda i, j: (j, 0)),
                  pl.BlockSpec((None, D, FF_COLS), lambda i, j: (j, 0, 0)), row, vec],
        out_specs=[ffb, row, vec],
        out_shape=[SDS((s, nj * FF_COLS), BF16), SDS((s, D), F32), SDS((1, D), F32)],
        scratch_shapes=[pltpu.VMEM((tm, D), BF16), pltpu.VMEM((tm, D), F32)],
        params=_cparams(("arbitrary", "arbitrary"), 52), args=(dh2, r, w_down, w_up_g, h1, g_mlp), comm=comm)


def _merge_bwd(dh1, z, pa, pb, w_out, w_oa, w_ob, comm=None):
    s = dh1.shape[0]
    tm = min(TM_MERGE, s)

    def body(dh1_ref, ma_ref, mb_ref, pa_ref, pb_ref, wo_ref, woa_ref, wob_ref,
             dz_ref, dpa_ref, dpb_ref, dya_ref, dyb_ref):
        dm = _dot_nt(dh1_ref[...].astype(BF16), wo_ref[...])
        sa = jax.nn.sigmoid(ma_ref[...].astype(F32))
        sb = jax.nn.sigmoid(mb_ref[...].astype(F32))
        dpa = (dm * sa).astype(BF16)
        dpb = (dm * sb).astype(BF16)
        dz_ref[:, 0:D] = (dm * pa_ref[...].astype(F32) * sa * (1.0 - sa)).astype(BF16)
        dz_ref[:, D:2 * D] = (dm * pb_ref[...].astype(F32) * sb * (1.0 - sb)).astype(BF16)
        dpa_ref[...] = dpa
        dpb_ref[...] = dpb
        dya_ref[...] = _dot_nt(dpa, woa_ref[...]).astype(BF16)
        dyb_ref[...] = _dot_nt(dpb, wob_ref[...]).astype(BF16)

    row = pl.BlockSpec((tm, D), lambda i: (i, 0))
    wsp = pl.BlockSpec((D, D), lambda i: (0, 0))
    return _call(
        body, name="merge_bwd", grid=(s // tm,),
        in_specs=[row, pl.BlockSpec((tm, D), lambda i: (i, 4)), pl.BlockSpec((tm, D), lambda i: (i, 5)),
                  row, row, wsp, wsp, wsp],
        out_specs=[pl.BlockSpec((tm, 2 * D), lambda i: (i, 2)), row, row, row, row],
        out_shape=[SDS((s, 6 * D), BF16)] + [SDS((s, D), BF16)] * 4, scratch_shapes=[],
        params=_cparams(("arbitrary",), 48), args=(dh1, z, z, pa, pb, w_out, w_oa, w_ob), comm=comm)


def _branch_b_bwd(dz, dyb, z, ln_g, ln_b, w_s, b_s_t, comm=None):
    s = z.shape[0]
    tb = min(T_BRANCH_B, s)

    def body(dz_in, dyb_ref, ub_ref, vb_ref, lg_ref, lb_ref, ws_ref, bs_ref,
             dz_ref, dws_ref, dbs_ref, dln_ref, du_s, dvln_s):
        del dz_in

        @pl.when(pl.program_id(0) == 0)
        def _():
            dws_ref[...] = jnp.zeros_like(dws_ref)
            dbs_ref[...] = jnp.zeros_like(dbs_ref)
            dln_ref[...] = jnp.zeros_like(dln_ref)

        lg = lg_ref[...]
        u, du, dv, rstd, vhat, vln = _sgu_common(ub_ref[...].astype(F32), vb_ref[...].astype(F32),
                                                 lg, lb_ref[...], True)
        vlnb = vln.astype(BF16)
        dyb_v = dyb_ref[...].astype(F32)
        wm = _masked_ws(ws_ref)
        keep = (lax.broadcasted_iota(jnp.int32, (CHUNK, CHUNK), 1)
                <= lax.broadcasted_iota(jnp.int32, (CHUNK, CHUNK), 0))
        bs = bs_ref[...]
        for c in range(tb // CHUNK):
            rs = slice(c * CHUNK, (c + 1) * CHUNK)
            for g in range(GROUPS):
                cs = slice(g * GROUP_DIM, (g + 1) * GROUP_DIM)
                v_blk = vlnb[rs, cs]
                sp = _dot(wm[g], v_blk) + bs[:, g:g + 1]
                d_sp = dyb_v[rs, cs] * u[rs, cs]
                d_spb = d_sp.astype(BF16)
                du_s[rs, cs] = dyb_v[rs, cs] * sp
                dvln_s[rs, cs] = _dot_tn(wm[g], d_spb)
                dws_ref[g] += jnp.where(keep, _dot_nt(d_spb, v_blk), 0.0)
                dbs_ref[g] += jnp.broadcast_to(jnp.sum(d_sp, axis=-1, keepdims=True), (CHUNK, CHUNK))
        dvln = dvln_s[...]
        dln_ref[0:1, :] += jnp.sum(dvln * vhat, axis=0, keepdims=True)
        dln_ref[1:2, :] += jnp.sum(dvln, axis=0, keepdims=True)
        dvh = dvln * lg
        d_v = rstd * (dvh - jnp.mean(dvh, axis=-1, keepdims=True)
                      - vhat * jnp.mean(dvh * vhat, axis=-1, keepdims=True))
        dz_ref[:, 0:D] = (du_s[...] * du).astype(BF16)
        dz_ref[:, D:2 * D] = (d_v * dv).astype(BF16)

    vec = pl.BlockSpec((1, D), lambda i: (0, 0))
    sq = pl.BlockSpec((GROUPS, CHUNK, CHUNK), lambda i: (0, 0, 0))
    return _call(
        body, name="branch_b_bwd", grid=(s // tb,),
        in_specs=[ANY, pl.BlockSpec((tb, D), lambda i: (i, 0)),
                  pl.BlockSpec((tb, D), lambda i: (i, 2)), pl.BlockSpec((tb, D), lambda i: (i, 3)), vec, vec, sq,
                  pl.BlockSpec((CHUNK, GROUPS), lambda i: (0, 0))],
        out_specs=[pl.BlockSpec((tb, 2 * D), lambda i: (i, 1)), sq, sq, pl.BlockSpec((8, D), lambda i: (0, 0))],
        out_shape=[SDS(dz.shape, BF16), SDS((GROUPS, CHUNK, CHUNK), F32), SDS((GROUPS, CHUNK, CHUNK), F32),
                   SDS((8, D), F32)],
        scratch_shapes=[pltpu.VMEM((tb, D), F32), pltpu.VMEM((tb, D), F32)], aliases={0: 0},
        params=_cparams(("arbitrary",), 40), args=(dz, dyb, z, z, ln_g, ln_b, w_s, b_s_t), comm=comm)


def _branch_a_bwd(dz, dya, z, hs, conv_w, conv_b, w_r, b_r, w_i, b_i, lam, comm=None):
    s = z.shape[0]
    ta = min(T_BRANCH_A, s)
    nb = s // ta
    per16 = ta // 16

    def body(dz_in, dya_ref, xa_ref, xp_ref, ga_ref, hs_ref, hp_ref, cw_ref, cb_ref, wr_ref, br_ref, wi_ref,
             bi_ref, lam_ref, dz_ref, vec_ref, dwr_ref, dwi_ref,
             a_s, b_s, h_s, r_s, i_s, m_s, dcar_s, acar_s, dxc_s):
        del dz_in
        i = pl.program_id(0)
        blk = nb - 1 - i

        @pl.when(i == 0)
        def _():
            dcar_s[...] = jnp.zeros_like(dcar_s)
            acar_s[...] = jnp.zeros_like(acar_s)
            dxc_s[...] = jnp.zeros_like(dxc_s)
            vec_ref[...] = jnp.zeros_like(vec_ref)
            dwr_ref[...] = jnp.zeros_like(dwr_ref)
            dwi_ref[...] = jnp.zeros_like(dwi_ref)

        cw = cw_ref[...]
        lam_v = lam_ref[...]
        xa = xa_ref[...].astype(F32)
        prev8 = jnp.where(blk > 0, xp_ref[...].astype(F32)[8:16], 0.0)
        xc, taps = _conv_fwd(xa, prev8, cw, cb_ref[...])
        xcb = xc.astype(BF16)
        sp_lam = _softplus(-lam_v)
        _lru_gates(xc, xcb, wr_ref, br_ref[...], wi_ref, bi_ref[...], sp_lam, a_s, b_s, r_s, i_s, m_s)

        hs_v = hs_ref[...].astype(F32)
        hprev8 = jnp.where(blk > 0, hp_ref[...].astype(F32)[8:16], 0.0)
        h_m1 = _rows_shifted(hprev8, hs_v, 1)
        gg, dgg = _gelu_and_grad(ga_ref[...].astype(F32))
        dya_v = dya_ref[...].astype(F32)
        dz_ref[:, D:2 * D] = (dya_v * hs_v * dgg).astype(BF16)

        a_v = a_s[...]
        a_s[...] = _rows_advanced(a_v, acar_s[...], 1)
        b_s[...] = dya_v * gg

        row = lax.broadcasted_iota(jnp.int32, (8, D), 0)
        ng = ta // 8

        def group(gi, carry):
            off = pl.multiple_of((ng - 1 - gi) * 8, 8)
            c8 = a_s[pl.ds(off, 8), :]
            d8 = b_s[pl.ds(off, 8), :]
            for d in (1, 2, 4):
                c_sh = jnp.where(row < 8 - d, pltpu.roll(c8, 8 - d, 0), 1.0)
                d_sh = jnp.where(row < 8 - d, pltpu.roll(d8, 8 - d, 0), 0.0)
                d8 = c8 * d_sh + d8
                c8 = c8 * c_sh
            dh8 = d8 + c8 * carry
            h_s[pl.ds(off, 8), :] = dh8
            return jnp.broadcast_to(dh8[0:1, :], (8, D))

        dcar_s[...] = lax.fori_loop(0, ng, group, dcar_s[...])
        acar_s[...] = jnp.broadcast_to(a_v[0:1, :], (8, D))

        dbx = h_s[...]
        r_v, i_v, m_v = r_s[...], i_s[...], m_s[...]
        d_mult = dbx * xc * i_v
        d_loga = dbx * h_m1 * a_v - d_mult * (a_v * a_v) / m_v
        d_pr = d_loga * ((-LRU_C) * sp_lam) * r_v * (1.0 - r_v)
        d_pi = dbx * xc * m_v * i_v * (1.0 - i_v)
        vec_ref[7:8, :] += jnp.sum(d_loga * r_v, axis=0, keepdims=True) * (LRU_C * jax.nn.sigmoid(-lam_v))
        vec_ref[5:6, :] += jnp.sum(d_pr, axis=0, keepdims=True)
        vec_ref[6:7, :] += jnp.sum(d_pi, axis=0, keepdims=True)
        d_prb = d_pr.astype(BF16)
        d_pib = d_pi.astype(BF16)
        h_s[...] = dbx * i_v * m_v
        for h in range(HEADS):
            sl = slice(h * HEAD_DIM, (h + 1) * HEAD_DIM)
            h_s[:, sl] += _dot_nt(d_prb[:, sl], wr_ref[h]) + _dot_nt(d_pib[:, sl], wi_ref[h])
            dwr_ref[h] += _dot_tn(xcb[:, sl], d_prb[:, sl])
            dwi_ref[h] += _dot_tn(xcb[:, sl], d_pib[:, sl])
        d_xc = h_s[...]
        vec_ref[4:5, :] += jnp.sum(d_xc, axis=0, keepdims=True)
        vec_ref[0:1, :] += jnp.sum(d_xc * xa, axis=0, keepdims=True)
        d_xa = cw[0:1, :] * d_xc
        nxt = dxc_s[...]
        for k in range(1, CONV_K):
            vec_ref[k:k + 1, :] += jnp.sum(d_xc * taps[k], axis=0, keepdims=True)
            d_xa = d_xa + cw[k:k + 1, :] * _rows_advanced(d_xc, nxt, k)
        dz_ref[:, 0:D] = d_xa.astype(BF16)
        dxc_s[...] = d_xc[0:8, :]

    vec = pl.BlockSpec((1, D), lambda i: (0, 0))
    gate = pl.BlockSpec((HEADS, HEAD_DIM, HEAD_DIM), lambda i: (0, 0, 0))
    cur = lambda c: pl.BlockSpec((ta, D), lambda i: (nb - 1 - i, c))
    before = lambda c: pl.BlockSpec((16, D), lambda i: (jnp.maximum((nb - 1 - i) * per16 - 1, 0), c))
    return _call(
        body, name="branch_a_bwd", grid=(nb,),
        in_specs=[ANY, cur(0), cur(0), before(0), cur(1), cur(0), before(0),
                  pl.BlockSpec((CONV_K, D), lambda i: (0, 0)), vec, gate, vec, gate, vec, vec],
        out_specs=[pl.BlockSpec((ta, 2 * D), lambda i: (nb - 1 - i, 0)), pl.BlockSpec((8, D), lambda i: (0, 0)),
                   gate, gate],
        out_shape=[SDS(dz.shape, BF16), SDS((8, D), F32), SDS((HEADS, HEAD_DIM, HEAD_DIM), F32),
                   SDS((HEADS, HEAD_DIM, HEAD_DIM), F32)],
        scratch_shapes=[pltpu.VMEM((ta, D), F32)] * 6 + [pltpu.VMEM((8, D), F32)] * 3, aliases={0: 0},
        params=_cparams(("arbitrary",), 48),
        args=(dz, dya, z, z, z, hs, hs, conv_w, conv_b, w_r, b_r, w_i, b_i, lam), comm=comm)


def _in_bwd(dz, w_in_g, x, dh1, g_mix, comm=None):
    s = x.shape[0]
    tm = min(TM_ROWS, s)
    nj = N_SLOT

    def body(dz_ref, w_ref, x_ref, dh1_ref, g_ref, dx_ref, dg_ref, acc_s):
        i, j = pl.program_id(0), pl.program_id(1)

        @pl.when(j == 0)
        def _():
            acc_s[...] = jnp.zeros_like(acc_s)

        @pl.when((i == 0) & (j == 0))
        def _():
            dg_ref[...] = jnp.zeros_like(dg_ref)

        acc_s[...] += _dot_nt(dz_ref[...], w_ref[...])

        @pl.when(j == nj - 1)
        def _():
            xv = x_ref[...]
            rstd = lax.rsqrt(jnp.mean(xv * xv, axis=-1, keepdims=True) + NORM_EPS)
            xh = xv * rstd
            dn = acc_s[...]
            dg_ref[...] += jnp.sum(dn * xh, axis=0, keepdims=True)
            dhat = dn * g_ref[...]
            dx_ref[...] = dh1_ref[...] + rstd * (dhat - xh * jnp.mean(dhat * xh, axis=-1, keepdims=True))

    row = pl.BlockSpec((tm, D), lambda i, j: (i, 0))
    vec = pl.BlockSpec((1, D), lambda i, j: (0, 0))
    return _call(
        body, name="in_bwd", grid=(s // tm, nj),
        in_specs=[pl.BlockSpec((tm, W_IN_COLS), lambda i, j: (i, j)),
                  pl.BlockSpec((None, D, W_IN_COLS), lambda i, j: (j, 0, 0)), row, row, vec],
        out_specs=[row, vec],
        out_shape=[SDS((s, D), F32), SDS((1, D), F32)],
        scratch_shapes=[pltpu.VMEM((tm, D), F32)],
        params=_cparams(("arbitrary", "arbitrary"), 48), args=(dz, w_in_g, x, dh1, g_mix), comm=comm)


def _wgrad(name, a, b, nblk, a_split, b_split):
    s = a.shape[0]
    ts = min(TM_ROWS, s)
    a_w = a.shape[1] // nblk if a_split else a.shape[1]
    b_w = b.shape[1] // nblk if b_split else b.shape[1]

    def body(a_ref, b_ref, o_ref, acc_s):
        t = pl.program_id(1)

        @pl.when(t == 0)
        def _():
            acc_s[...] = jnp.zeros_like(acc_s)

        acc_s[...] += _dot_tn(a_ref[...].astype(BF16), b_ref[...].astype(BF16))

        @pl.when(t == pl.num_programs(1) - 1)
        def _():
            o_ref[...] = acc_s[...].astype(BF16)

    return pl.pallas_call(
        body, name=name, grid=(nblk, s // ts),
        in_specs=[pl.BlockSpec((ts, a_w), (lambda k, t: (t, k)) if a_split else (lambda k, t: (t, 0))),
                  pl.BlockSpec((ts, b_w), (lambda k, t: (t, k)) if b_split else (lambda k, t: (t, 0)))],
        out_specs=pl.BlockSpec((None, a_w, b_w), lambda k, t: (k, 0, 0)),
        out_shape=SDS((nblk, a_w, b_w), BF16),
        scratch_shapes=[pltpu.VMEM((a_w, b_w), F32)],
        compiler_params=_cparams(("arbitrary", "arbitrary"), 48),
    )(a, b)


def _wgrad_t(name, a_t, b, nblk, a_split, b_split, tokens):
    s = b.shape[0]
    ts = min(tokens, s)
    a_w = a_t.shape[0] // nblk if a_split else a_t.shape[0]
    b_w = b.shape[1] // nblk if b_split else b.shape[1]

    def body(a_ref, b_ref, o_ref, acc_s):
        t = pl.program_id(1)

        @pl.when(t == 0)
        def _():
            acc_s[...] = jnp.zeros_like(acc_s)

        acc_s[...] += _dot(a_ref[...], b_ref[...].astype(BF16))

        @pl.when(t == pl.num_programs(1) - 1)
        def _():
            o_ref[...] = acc_s[...].astype(BF16)

    return pl.pallas_call(
        body, name=name, grid=(nblk, s // ts),
        in_specs=[pl.BlockSpec((a_w, ts), (lambda k, t: (k, t)) if a_split else (lambda k, t: (0, t))),
                  pl.BlockSpec((ts, b_w), (lambda k, t: (t, k)) if b_split else (lambda k, t: (t, 0)))],
        out_specs=pl.BlockSpec((None, a_w, b_w), lambda k, t: (k, 0, 0)),
        out_shape=SDS((nblk, a_w, b_w), BF16),
        scratch_shapes=[pltpu.VMEM((a_w, b_w), F32)],
        compiler_params=_cparams(("arbitrary", "arbitrary"), 48),
    )(a_t, b)


def _place():
    x, y, c = lax.axis_index("x"), lax.axis_index("y"), lax.axis_index("c")
    return x, y, c


def _other_chips(x, y):
    return [(x, 1 - y, 2 * x + 1 - y), (1 - x, y, 2 * (1 - x) + y), (1 - x, 1 - y, 2 * (1 - x) + 1 - y)]


class _Plan:
    def __init__(self, arrays, out_shape, sems, start, finish, middle=None, middle_at=6):
        self.arrays, self.out_shape, self.sems, self.start, self.finish = arrays, out_shape, sems, start, finish
        self.middle, self.middle_at = middle, middle_at


def _gather_plan(shards, middle_at=6):
    n = len(shards)

    def copies(ins, outs, sems):
        send_sems, recv_sems, local_sems = sems
        x, y, c = _place()
        chip = 2 * x + y
        me = 2 * chip + c
        sib = (x, y, 1 - c)
        chips = _other_chips(x, y)

        def rc(k, t, src, blk, to):
            return pltpu.make_async_remote_copy(
                src_ref=src, dst_ref=outs[t].at[blk], send_sem=send_sems.at[k * n + t],
                recv_sem=recv_sems.at[k * n + t], device_id=to, device_id_type=MESH)

        (yx, yy, y_chip), (xx, xy, x_chip), _ = chips
        local = [pltpu.make_async_copy(ins[t], outs[t].at[me], local_sems.at[t]) for t in range(n)]
        sends = ([rc(0, t, ins[t], me, sib) for t in range(n)] + [rc(1, t, ins[t], me, (yx, yy, c)) for t in range(n)]
                 + [rc(2, t, ins[t], me, (xx, xy, c)) for t in range(n)])
        passed = [[rc(4 + j, t, outs[t].at[2 * pc + c], 2 * pc + c, sib) for t in range(n)]
                  for j, (_, _, pc) in enumerate(chips)]
        relays = [[rc(3, t, outs[t].at[2 * y_chip + c], 2 * y_chip + c, (xx, xy, c)) for t in range(n)],
                  [rc(3, t, outs[t].at[2 * x_chip + c], 2 * x_chip + c, (yx, yy, c)) for t in range(n)]]
        return rc, local, sends, passed, relays, chips, chip, c, sib

    def start(ins, outs, sems):
        _, local, sends, _, _, _, _, _, _ = copies(ins, outs, sems)
        for cp in local + sends:
            cp.start()

    def middle(ins, outs, sems):
        rc, _, _, passed, relays, chips, _, c, sib = copies(ins, outs, sems)
        for j in range(2):
            for t in range(n):
                rc(1 + j, t, ins[t], 2 * chips[j][2] + c, sib).wait_recv()
        for j in range(2):
            for cp in passed[j]:
                cp.start()

            @pl.when(c == j)
            def _():
                for cp in relays[j]:
                    cp.start()

    def finish(ins, outs, sems):
        rc, local, sends, passed, relays, chips, chip, c, sib = copies(ins, outs, sems)
        far = 2 * chips[2][2] + c
        for t in range(n):
            rc(3, t, ins[t], far, sib).wait_recv()
        for cp in passed[2]:
            cp.start()
        for t in range(n):
            rc(0, t, ins[t], 2 * chip + 1 - c, sib).wait_recv()
        for j, (px, py, pc) in enumerate(chips):
            for t in range(n):
                rc(4 + j, t, ins[t], 2 * pc + 1 - c, sib).wait_recv()
        for cp in sends + passed[0] + passed[1] + passed[2]:
            cp.wait_send()
        for j in range(2):
            @pl.when(c == j)
            def _():
                for cp in relays[j]:
                    cp.wait_send()
        for cp in local:
            cp.wait()

    return _Plan(list(shards), [SDS((N_SLOT,) + tuple(a.shape), a.dtype) for a in shards],
                 [pltpu.SemaphoreType.DMA((7 * n,)), pltpu.SemaphoreType.DMA((7 * n,)),
                  pltpu.SemaphoreType.DMA((n,))], start, finish, middle, middle_at)


def _sibling_plan(grads, whole=()):
    n, m = len(grads), len(whole)

    def copies(ins, outs, sems):
        send_sems, recv_sems = sems
        x, y, c = _place()
        sib = (x, y, 1 - c)

        def rc(t, src, dst):
            return pltpu.make_async_remote_copy(src_ref=src, dst_ref=dst, send_sem=send_sems.at[t],
                                                recv_sem=recv_sems.at[t], device_id=sib, device_id_type=MESH)
        return rc, c

    def start(ins, outs, sems):
        rc, c = copies(ins, outs, sems)
        for t in range(n):
            for j in range(4):
                rc(t, ins[t].at[2 * j + 1 - c], outs[t].at[j]).start()
        for t in range(n, n + m):
            rc(t, ins[t], outs[t]).start()

    def finish(ins, outs, sems):
        rc, _ = copies(ins, outs, sems)
        for t in range(n):
            rc(t, ins[t].at[pl.ds(0, 4)], outs[t]).wait()
        for t in range(n, n + m):
            rc(t, ins[t], outs[t]).wait()

    return _Plan(list(grads) + list(whole),
                 [SDS((4,) + tuple(g.shape[1:]), g.dtype) for g in grads] + [SDS(a.shape, a.dtype) for a in whole],
                 [pltpu.SemaphoreType.DMA((n + m,)), pltpu.SemaphoreType.DMA((n + m,))], start, finish)


def _chips_plan(parts, whole=()):
    n, m = len(parts), len(whole)
    stage = 4

    def src_of(ins, t, pc):
        return ins[t].at[pc] if t < n else ins[t]

    def copies(ins, outs, sems):
        send_a, recv_a, send_b, recv_b, send_c, recv_c, local_sems = sems
        x, y, c = _place()
        chip = 2 * x + y
        first, second, far = _first_second(x, y, c)

        def rc(t, send, recv, src, slot, to):
            return pltpu.make_async_remote_copy(src_ref=src, dst_ref=outs[t].at[slot], send_sem=send.at[t],
                                                recv_sem=recv.at[t], device_id=(to[0], to[1], c),
                                                device_id_type=MESH)

        local = [pltpu.make_async_copy(src_of(ins, t, chip), outs[t].at[chip], local_sems.at[t])
                 for t in range(n + m)]
        direct = [rc(t, send_a, recv_a, src_of(ins, t, to[2]), chip, to) for to in (first, second)
                  for t in range(n + m)]
        staged = [rc(t, send_b, recv_b, src_of(ins, t, far[2]), stage, first) for t in range(n + m)]
        relays = [rc(t, send_c, recv_c, outs[t].at[stage], first[2], second) for t in range(n + m)]
        return rc, local, direct, staged, relays, (send_a, recv_a), (x, y, c)

    def start(ins, outs, sems):
        _, local, direct, staged, _, _, _ = copies(ins, outs, sems)
        for cp in local + direct + staged:
            cp.start()

    def middle(ins, outs, sems):
        _, _, _, staged, relays, _, _ = copies(ins, outs, sems)
        for cp in staged:
            cp.wait_recv()
        for cp in relays:
            cp.start()

    def finish(ins, outs, sems):
        _, local, _, staged, relays, (send_a, recv_a), me = copies(ins, outs, sems)
        for t in range(n + m):
            two = outs[t].at[pl.ds(0, 2)]
            pltpu.make_async_remote_copy(src_ref=two, dst_ref=two, send_sem=send_a.at[t], recv_sem=recv_a.at[t],
                                         device_id=me, device_id_type=MESH).wait()
        for cp in staged:
            cp.wait_send()
        for cp in relays:
            cp.wait()
        for cp in local:
            cp.wait()

    return _Plan(list(parts) + list(whole),
                 [SDS((5,) + tuple(p.shape[1:]), p.dtype) for p in parts]
                 + [SDS((5,) + tuple(a.shape), a.dtype) for a in whole],
                 [pltpu.SemaphoreType.DMA((n + m,))] * 7, start, finish, middle)


def _exchange_plan(arr):
    def peers(x, y, c):
        flip = lambda v, f: 1 - v if f else v
        return [(flip(x, fx), flip(y, fy), flip(c, fc))
                for fx in (0, 1) for fy in (0, 1) for fc in (0, 1) if fx or fy or fc]

    def start(ins, outs, sems):
        x, y, c = _place()
        me = 4 * x + 2 * y + c
        pltpu.make_async_copy(ins[0], outs[0].at[me], sems[2].at[0]).start()
        for to in peers(x, y, c):
            pltpu.make_async_remote_copy(src_ref=ins[0], dst_ref=outs[0].at[me], send_sem=sems[0].at[0],
                                         recv_sem=sems[1].at[0], device_id=to, device_id_type=MESH).start()

    def finish(ins, outs, sems):
        x, y, c = _place()
        seven = outs[0].at[pl.ds(0, 7)]
        pltpu.make_async_remote_copy(src_ref=seven, dst_ref=seven, send_sem=sems[0].at[0], recv_sem=sems[1].at[0],
                                     device_id=(x, y, c), device_id_type=MESH).wait()
        pltpu.make_async_copy(ins[0], outs[0].at[4 * x + 2 * y + c], sems[2].at[0]).wait()

    return _Plan([arr], [SDS((N_SLOT,) + tuple(arr.shape), arr.dtype)],
                 [pltpu.SemaphoreType.DMA((1,)), pltpu.SemaphoreType.DMA((1,)), pltpu.SemaphoreType.DMA((1,))],
                 start, finish)


def _join(*plans):
    def cut(seq, sizes):
        out, at = [], 0
        for k in sizes:
            out.append(seq[at:at + k])
            at += k
        return out

    n_arr = [len(p.arrays) for p in plans]
    n_sem = [len(p.sems) for p in plans]

    def start(ins, outs, sems):
        for p, i, o, s in zip(plans, cut(ins, n_arr), cut(outs, n_arr), cut(sems, n_sem)):
            p.start(i, o, s)

    def finish(ins, outs, sems):
        for p, i, o, s in zip(plans, cut(ins, n_arr), cut(outs, n_arr), cut(sems, n_sem)):
            p.finish(i, o, s)

    def middle(ins, outs, sems):
        for p, i, o, s in zip(plans, cut(ins, n_arr), cut(outs, n_arr), cut(sems, n_sem)):
            if p.middle is not None:
                p.middle(i, o, s)

    return _Plan([a for p in plans for a in p.arrays], [o for p in plans for o in p.out_shape],
                 [s for p in plans for s in p.sems], start, finish,
                 middle if any(p.middle is not None for p in plans) else None,
                 max(p.middle_at for p in plans))


def _run_plan(name, plan):
    k = len(plan.arrays)

    def body(*refs):
        ins, outs, sems = refs[:k], refs[k:2 * k], refs[2 * k:]
        plan.start(ins, outs, sems)
        if plan.middle is not None:
            plan.middle(ins, outs, sems)
        plan.finish(ins, outs, sems)

    return pl.pallas_call(
        body, name=name, in_specs=[ANY] * k, out_specs=[ANY] * k, out_shape=plan.out_shape,
        scratch_shapes=plan.sems, compiler_params=pltpu.CompilerParams(has_side_effects=True),
    )(*plan.arrays)


def _call(body, *, name, grid, in_specs, out_specs, out_shape, scratch_shapes, params, args, comm=None,
          aliases=None, prefetch=()):
    aliases = aliases or {}
    n_pre = len(prefetch)

    def launch(fn, ins_specs, outs_specs, outs_shape, scratch, operands):
        spec = pltpu.PrefetchScalarGridSpec(num_scalar_prefetch=n_pre, grid=grid, in_specs=ins_specs,
                                            out_specs=outs_specs, scratch_shapes=scratch)
        return pl.pallas_call(fn, name=name, grid_spec=spec, out_shape=outs_shape, compiler_params=params,
                              input_output_aliases=aliases)(*prefetch, *[_small_in_hbm(a) for a in operands])

    if comm is None:
        return list(launch(body, in_specs, out_specs, out_shape, scratch_shapes, args)), []
    n_in, n_out, n_scr, k = len(in_specs), len(out_specs), len(scratch_shapes), len(comm.arrays)

    def wrapped(*refs):
        pre, refs = refs[:n_pre], refs[n_pre:]
        ins = refs[:n_in]
        c_in = refs[n_in:n_in + k]
        outs = refs[n_in + k:n_in + k + n_out]
        c_out = refs[n_in + k + n_out:n_in + 2 * k + n_out]
        scr = refs[n_in + 2 * k + n_out:n_in + 2 * k + n_out + n_scr]
        sems = refs[n_in + 2 * k + n_out + n_scr:]
        step, steps = pl.program_id(0), grid[0]
        for d in range(1, len(grid)):
            step, steps = step * grid[d] + pl.program_id(d), steps * grid[d]

        @pl.when(step == 0)
        def _():
            comm.start(c_in, c_out, sems)

        if comm.middle is not None:
            @pl.when(step == (comm.middle_at * steps) // 8)
            def _():
                comm.middle(c_in, c_out, sems)

        body(*pre, *ins, *outs, *scr)

        @pl.when(step == steps - 1)
        def _():
            comm.finish(c_in, c_out, sems)

    res = launch(wrapped, list(in_specs) + [ANY] * k, list(out_specs) + [ANY] * k,
                 list(out_shape) + list(comm.out_shape), list(scratch_shapes) + list(comm.sems),
                 tuple(args) + tuple(comm.arrays))
    return list(res[:n_out]), list(res[n_out:])


def _row_tile(rows):
    for t in (512, 256, 128, 64, 32, 16, 8):
        if rows % t == 0:
            return t
    return rows


def _pair_sum(name, g8, recv4, core):
    _, rows, cols = recv4.shape
    tr = _row_tile(rows)
    g42 = g8.reshape(4, 2, rows, cols)

    def body(c_ref, g_ref, r_ref, o_ref):
        del c_ref
        o_ref[...] = (g_ref[...].astype(F32) + r_ref[...].astype(F32)).astype(o_ref.dtype)

    return pl.pallas_call(
        body, name=name,
        grid_spec=pltpu.PrefetchScalarGridSpec(
            num_scalar_prefetch=1, grid=(4, rows // tr),
            in_specs=[pl.BlockSpec((None, None, tr, cols), lambda j, i, c_ref: (j, c_ref[0], i, 0)),
                      pl.BlockSpec((None, tr, cols), lambda j, i, c_ref: (j, i, 0))],
            out_specs=pl.BlockSpec((None, tr, cols), lambda j, i, c_ref: (j, i, 0))),
        out_shape=SDS(recv4.shape, g8.dtype),
        compiler_params=_cparams(("arbitrary", "arbitrary"), 32),
    )(core, g42, recv4)


def _add2(name, a, b):
    rows, cols = a.shape
    tr = _row_tile(rows)

    def body(a_ref, b_ref, o_ref):
        o_ref[...] = a_ref[...] + b_ref[...]

    blk = pl.BlockSpec((tr, cols), lambda i: (i, 0))
    return pl.pallas_call(body, name=name, grid=(rows // tr,), in_specs=[blk, blk], out_specs=blk,
                          out_shape=SDS(a.shape, a.dtype),
                          compiler_params=_cparams(("arbitrary",), 32))(a, b)


def _sum_terms(name, terms, k):
    _, rows, cols = terms.shape
    tr = _row_tile(rows)

    def body(r_ref, o_ref):
        acc = r_ref[0]
        for q in range(1, k):
            acc = acc + r_ref[q]
        o_ref[...] = acc

    return pl.pallas_call(body, name=name, grid=(rows // tr,),
                          in_specs=[pl.BlockSpec((k, tr, cols), lambda i: (0, i, 0))],
                          out_specs=pl.BlockSpec((tr, cols), lambda i: (i, 0)),
                          out_shape=SDS((rows, cols), terms.dtype),
                          compiler_params=_cparams(("arbitrary",), 32))(terms)


def _adam_update(g, w, m, v):
    c1 = 1.0 / (1.0 - ADAM_B1 ** ADAM_STEP)
    c2 = 1.0 / (1.0 - ADAM_B2 ** ADAM_STEP)
    mn = ADAM_B1 * m + (1.0 - ADAM_B1) * g
    vn = ADAM_B2 * v + (1.0 - ADAM_B2) * (g * g)
    delta = (-ADAM_LR) * ((mn * c1) / (jnp.sqrt(vn * c2) + ADAM_EPS) + ADAM_WD * w)
    return delta, mn, vn


def _adamw_many(name, gs, ws, ms, vs):
    n = len(gs)

    def body(*refs):
        for p in range(n):
            g, w, m, v = (refs[q * n + p][...] for q in range(4))
            d, mn, vn = _adam_update(g, w, m, v)
            refs[4 * n + p][...] = d
            refs[5 * n + p][...] = mn
            refs[6 * n + p][...] = vn

    full = [pl.BlockSpec(w.shape, lambda i: (0, 0)) for w in ws]
    shapes = [SDS(w.shape, F32) for w in ws]
    res = pl.pallas_call(body, name=name, grid=(1,), in_specs=full * 4, out_specs=full * 3, out_shape=shapes * 3,
                         compiler_params=_cparams(("arbitrary",), 32),
                         )(*[_small_in_hbm(a) for a in (*gs, *ws, *ms, *vs)])
    return [(res[p], res[n + p], res[2 * n + p]) for p in range(n)]


def _adamw(name, terms, w, m, v, k=4):
    _, rows, cols = terms.shape
    tr = _row_tile(rows)

    def body(t_ref, w_ref, m_ref, v_ref, g_ref, d_ref, mo_ref, vo_ref):
        g = t_ref[0].astype(F32)
        for q in range(1, k):
            g = g + t_ref[q].astype(F32)
        g_ref[...] = g
        d_ref[...], mo_ref[...], vo_ref[...] = _adam_update(g, w_ref[...], m_ref[...], v_ref[...])

    blk = pl.BlockSpec((tr, cols), lambda i: (i, 0))
    return pl.pallas_call(body, name=name, grid=(rows // tr,),
                          in_specs=[pl.BlockSpec((k, tr, cols), lambda i: (0, i, 0)), blk, blk, blk],
                          out_specs=[blk] * 4, out_shape=[SDS((rows, cols), F32)] * 4,
                          compiler_params=_cparams(("arbitrary",), 40),
                          )(*[pltpu.with_memory_space_constraint(a, pltpu.HBM) for a in (terms, w, m, v)])


def kernel(x, norm_mix_g, w_in, conv_w, conv_b, w_rgate, b_rgate, w_igate, b_igate, lru_lambda, w_out_a, sgu_ln_g, sgu_ln_b, sgu_w_s, sgu_b_s, w_out_b, w_out, norm_mlp_g, w_up, w_down, norm_final_g, loss_target, m_norm_mix_g, m_w_in, m_conv_w, m_conv_b, m_w_rgate, m_b_rgate, m_w_igate, m_b_igate, m_lru_lambda, m_w_out_a, m_sgu_ln_g, m_sgu_ln_b, m_sgu_w_s, m_sgu_b_s, m_w_out_b, m_w_out, m_norm_mlp_g, m_w_up, m_w_down, m_norm_final_g, v_norm_mix_g, v_w_in, v_conv_w, v_conv_b, v_w_rgate, v_b_rgate, v_w_igate, v_b_igate, v_lru_lambda, v_w_out_a, v_sgu_ln_g, v_sgu_ln_b, v_sgu_w_s, v_sgu_b_s, v_w_out_b, v_w_out, v_norm_mlp_g, v_w_up, v_w_down, v_norm_final_g):
    cx, cy, cc = _place()
    me = 4 * cx + 2 * cy + cc
    core = jnp.reshape(cc, (1,)).astype(jnp.int32)
    xs = x[0]
    tgt = loss_target[0]
    s = xs.shape[0]

    gate_shard = jnp.stack([w_rgate[0], w_igate[0]]).astype(BF16).reshape(2 * HEADS * 32, HEAD_DIM)
    vec_shard = jnp.concatenate([conv_w[0], b_rgate[0], b_igate[0]], axis=1)
    vec_shard = jnp.pad(vec_shard, ((0, 4), (0, 256 - vec_shard.shape[1])))
    shards = [w_in[0].astype(BF16), w_out_a[0].astype(BF16), w_out_b[0].astype(BF16), w_out[0].astype(BF16),
              w_up[0].astype(BF16), w_down[0].astype(BF16), gate_shard, vec_shard]
    (z, n1_t, w_in_g), (gate_g, vec_g) = _in_proj(xs, norm_mix_g, shards[0], _slot_order(cx, cy, cc),
                                                comm=_gather_plan(shards[6:8]))
    gates = gate_g.reshape(N_SLOT, 2, HEADS, 32, HEAD_DIM).transpose(1, 2, 0, 3, 4).reshape(2, HEADS, HEAD_DIM, HEAD_DIM)
    w_r_f, w_i_f = gates[0], gates[1]
    conv_w_f = vec_g[:, 0:4, 0:128].transpose(1, 0, 2).reshape(CONV_K, D)
    b_r_f = vec_g[:, 0:4, 128:160].transpose(1, 0, 2).reshape(1, D)
    b_i_f = vec_g[:, 0:4, 160:192].transpose(1, 0, 2).reshape(1, D)
    b_s_t = jnp.transpose(sgu_b_s[0])

    (ya, hs), (w_oa_g, w_ob_g, w_out_g, w_up_g) = _branch_a_fwd(
        z, conv_w_f, conv_b, w_r_f, b_r_f, w_i_f, b_i_f, lru_lambda, comm=_gather_plan(shards[1:5]))
    w_oa_f = w_oa_g.reshape(D, D)
    w_ob_f = w_ob_g.reshape(D, D)
    w_out_f = w_out_g.reshape(D, D)
    (yb,), _ = _branch_b_fwd(z, sgu_ln_g, sgu_ln_b, sgu_w_s[0], b_s_t)
    (pa, pb, merged, h1), (w_down_g,) = _merge_out(ya, yb, z, xs, w_oa_f, w_ob_f, w_out_f,
                                                   comm=_gather_plan(shards[5:6], middle_at=4))
    w_down_f = w_down_g.reshape(N_SLOT * FF_COLS, D)
    gf2 = norm_final_g.reshape(1, D)
    r_act, act_t, n2_t, dh2, loss_acc, d_gfin = _mlp_fwd(h1, norm_mlp_g, w_up_g, w_down_f, gf2, tgt)

    def pair(names, grads, recv):
        return [_pair_sum("pair_sum_" + nm, g, r, core) for nm, g, r in zip(names, grads, recv)]

    g_down = _wgrad_t("wgrad_down", act_t, dh2, N_SLOT // 2, True, False, 2048)
    g_down = g_down.reshape(N_SLOT, FF_COLS, D)
    (df, dh1, d_gmlp), (r_down,) = _mlp_bwd(dh2, r_act, w_down_f, w_up_g, h1, norm_mlp_g,
                                            comm=_sibling_plan([g_down]))
    (p_down,) = pair(["down"], [g_down], [r_down])
    g_up = _wgrad_t("wgrad_up", n2_t, df, N_SLOT, False, True, 4096)
    g_out = _wgrad("wgrad_out", merged, dh1, 1, False, False).reshape(N_SLOT, D // N_SLOT, D)
    (dz, dpa, dpb, dya, dyb), (got_down, r_up, r_out) = _merge_bwd(
        dh1, z, pa, pb, w_out_f, w_oa_f, w_ob_f, comm=_join(_chips_plan([p_down]), _sibling_plan([g_up, g_out])))
    p_up, p_out = pair(["up", "out"], [g_up, g_out], [r_up, r_out])
    g_oa = _wgrad("wgrad_out_a", ya, dpa, 1, False, False).reshape(N_SLOT, D // N_SLOT, D)
    g_ob = _wgrad("wgrad_out_b", yb, dpb, 1, False, False).reshape(N_SLOT, D // N_SLOT, D)
    (dz, d_ws, d_bs, d_ln), (got_up, r_oa, r_ob) = _branch_b_bwd(
        dz, dyb, z, sgu_ln_g, sgu_ln_b, sgu_w_s[0], b_s_t,
        comm=_join(_chips_plan([p_up]), _sibling_plan([g_oa, g_ob])))
    p_oa, p_ob = pair(["out_a", "out_b"], [g_oa, g_ob], [r_oa, r_ob])
    (dz, d_vec, d_wr, d_wi), (got_out, got_oa, got_ob) = _branch_a_bwd(
        dz, dya, z, hs, conv_w_f, conv_b, w_r_f, b_r_f, w_i_f, b_i_f, lru_lambda,
        comm=_chips_plan([p_out, p_oa, p_ob]))
    g_in = _wgrad_t("wgrad_in", n1_t, dz, N_SLOT, False, True, 4096)
    g_gate = jnp.stack([d_wr, d_wi]).reshape(2, HEADS, N_SLOT, 32, HEAD_DIM).transpose(2, 0, 1, 3, 4)
    g_gate = g_gate.reshape(N_SLOT, 2 * HEADS * 32, HEAD_DIM).astype(BF16)

    d_bs_row = jnp.pad(d_bs[:, :, 0].reshape(1, GROUPS * CHUNK), ((0, 0), (0, D - GROUPS * CHUNK)))
    vecs = jnp.concatenate([d_vec, jnp.concatenate([d_ln[0:2], d_gmlp, d_gfin, d_bs_row, jnp.zeros((3, D), F32)])])
    d_ws2 = d_ws.reshape(GROUPS * CHUNK, CHUNK)
    r_in, r_gate, r_vecs, r_ws = _run_plan("rs_sibling_in", _sibling_plan([g_in, g_gate], [vecs, d_ws2]))
    p_in, p_gate = pair(["in", "gate"], [g_in, g_gate], [r_in, r_gate])
    vecs_chip = _add2("pair_sum_vecs", vecs, r_vecs)
    ws_chip = _add2("pair_sum_ws", d_ws2, r_ws)
    (dx, d_gmix), (got_in, got_gate, got_vecs, got_ws) = _in_bwd(
        dz, w_in_g, xs, dh1, norm_mix_g, comm=_chips_plan([p_in, p_gate], [vecs_chip, ws_chip]))
    vecs_sum = _sum_terms("sum_vecs", got_vecs, 4)
    last = jnp.concatenate([d_gmix, jnp.pad(loss_acc[0:1], ((0, 0), (0, D - 128))), jnp.zeros((6, D), F32)])
    (last_all,) = _run_plan("exchange_last", _exchange_plan(last))
    last_sum = _sum_terms("sum_last", last_all, N_SLOT)
    loss = last_sum[1, 0]
    got = [got_in, got_oa, got_ob, got_out, got_up, got_down, got_gate]

    def step(nm, terms, w, m, v, rows, cols):
        g, d, mn, vn = _adamw("adamw_" + nm, terms.reshape(5, rows, cols), w.reshape(rows, cols),
                              m.reshape(rows, cols), v.reshape(rows, cols))
        return [a.reshape(w.shape) for a in (g, d, mn, vn)]

    o_in = step("in", got[0], w_in, m_w_in, v_w_in, D, W_IN_COLS)
    o_oa = step("out_a", got[1], w_out_a, m_w_out_a, v_w_out_a, D // N_SLOT, D)
    o_ob = step("out_b", got[2], w_out_b, m_w_out_b, v_w_out_b, D // N_SLOT, D)
    o_out = step("out", got[3], w_out, m_w_out, v_w_out, D // N_SLOT, D)
    o_up = step("up", got[4], w_up, m_w_up, v_w_up, D, FF_COLS)
    o_down = step("down", got[5], w_down, m_w_down, v_w_down, FF_COLS, D)
    gate_w = jnp.stack([w_rgate[0], w_igate[0]]).reshape(2 * HEADS * 32, HEAD_DIM)
    gate_m = jnp.stack([m_w_rgate[0], m_w_igate[0]]).reshape(2 * HEADS * 32, HEAD_DIM)
    gate_v = jnp.stack([v_w_rgate[0], v_w_igate[0]]).reshape(2 * HEADS * 32, HEAD_DIM)
    o_gate = _adamw("adamw_gate", got[6], gate_w, gate_m, gate_v)
    o_gate = [a.reshape(2, 1, HEADS, 32, HEAD_DIM) for a in o_gate]
    o_wr = [a[0] for a in o_gate]
    o_wi = [a[1] for a in o_gate]

    def own(full, width):
        return lax.dynamic_slice_in_dim(full, me * width, width, axis=1)

    small_g = {
        "norm_mix_g": last_sum[0:1], "conv_w": own(vecs_sum[0:4], 128), "conv_b": vecs_sum[4:5],
        "b_rgate": own(vecs_sum[5:6].reshape(HEADS, HEAD_DIM), 32),
        "b_igate": own(vecs_sum[6:7].reshape(HEADS, HEAD_DIM), 32),
        "lru_lambda": vecs_sum[7:8], "sgu_ln_g": vecs_sum[8:9], "sgu_ln_b": vecs_sum[9:10],
        "norm_mlp_g": vecs_sum[10:11], "norm_final_g": vecs_sum[11:12],
        "sgu_b_s": vecs_sum[12, 0:GROUPS * CHUNK].reshape(GROUPS, CHUNK),
    }
    small_w = {"norm_mix_g": (norm_mix_g, m_norm_mix_g, v_norm_mix_g), "conv_w": (conv_w, m_conv_w, v_conv_w),
               "conv_b": (conv_b, m_conv_b, v_conv_b), "b_rgate": (b_rgate, m_b_rgate, v_b_rgate),
               "b_igate": (b_igate, m_b_igate, v_b_igate), "lru_lambda": (lru_lambda, m_lru_lambda, v_lru_lambda),
               "sgu_ln_g": (sgu_ln_g, m_sgu_ln_g, v_sgu_ln_g), "sgu_ln_b": (sgu_ln_b, m_sgu_ln_b, v_sgu_ln_b),
               "norm_mlp_g": (norm_mlp_g, m_norm_mlp_g, v_norm_mlp_g),
               "norm_final_g": (norm_final_g, m_norm_final_g, v_norm_final_g),
               "sgu_b_s": (sgu_b_s, m_sgu_b_s, v_sgu_b_s), "sgu_w_s": (sgu_w_s, m_sgu_w_s, v_sgu_w_s)}
    order = list(small_g)
    as2d = lambda k, a: a.reshape(small_g[k].shape)
    upd = _adamw_many("adamw_small", [small_g[k] for k in order], *[[as2d(k, small_w[k][q]) for k in order]
                                                                     for q in range(3)])
    o_small = {k: [a.reshape(small_w[k][0].shape) for a in (small_g[k],) + u] for k, u in zip(order, upd)}
    ws3 = [a[0].reshape(GROUPS * CHUNK, CHUNK) for a in small_w.pop("sgu_w_s")]
    o_small["sgu_w_s"] = [a.reshape(sgu_w_s.shape) for a in _adamw("adamw_ws", got_ws, *ws3)]

    per_weight = {"norm_mix_g": o_small["norm_mix_g"], "w_in": o_in, "conv_w": o_small["conv_w"],
                  "conv_b": o_small["conv_b"], "w_rgate": o_wr, "b_rgate": o_small["b_rgate"], "w_igate": o_wi,
                  "b_igate": o_small["b_igate"], "lru_lambda": o_small["lru_lambda"], "w_out_a": o_oa,
                  "sgu_ln_g": o_small["sgu_ln_g"], "sgu_ln_b": o_small["sgu_ln_b"], "sgu_w_s": o_small["sgu_w_s"],
                  "sgu_b_s": o_small["sgu_b_s"], "w_out_b": o_ob, "w_out": o_out, "norm_mlp_g": o_small["norm_mlp_g"],
                  "w_up": o_up, "w_down": o_down, "norm_final_g": o_small["norm_final_g"]}
    names_w = list(per_weight)
    return (loss, dx[None], *[per_weight[k][0] for k in names_w], *[per_weight[k][1] for k in names_w],
            *[per_weight[k][2] for k in names_w], *[per_weight[k][3] for k in names_w])
```

```python
import jax
import jax.numpy as jnp
from jax import lax
from jax.experimental import pallas as pl
from jax.experimental.pallas import tpu as pltpu

F32 = jnp.float32
BF16 = jnp.bfloat16
SDS = jax.ShapeDtypeStruct
MESH = pl.DeviceIdType.MESH
ANY = pl.BlockSpec(memory_space=pltpu.HBM)

D = 1024
N_SLOT = 8
W_IN_COLS = 768
FF_COLS = 512
HEADS, HEAD_DIM = 4, 256
GROUPS, GROUP_DIM = 4, 256
CHUNK = 128
CONV_K = 4
NORM_EPS = 1e-6
LN_EPS = 1e-5
LRU_C = 8.0
ADAM_LR, ADAM_B1, ADAM_B2, ADAM_EPS, ADAM_WD, ADAM_STEP = 0.001, 0.9, 0.999, 1e-08, 0.01, 10

TM_ROWS = 1024
TM_MERGE = 512
T_BRANCH_A = 512
T_BRANCH_B = 256
MiB = 1024 * 1024
SMALL_OPERAND = 16 * 1024

_GELU_C = 0.7978845608028654
_GELU_A = 0.044715


def _small_in_hbm(a):
    return pltpu.with_memory_space_constraint(a, pltpu.HBM) if a.size <= SMALL_OPERAND else a


def _cparams(sem, vmem_mib):
    return pltpu.CompilerParams(dimension_semantics=sem, vmem_limit_bytes=vmem_mib * MiB)


def _gelu(x):
    t = jnp.tanh(_GELU_C * (x + _GELU_A * x * x * x))
    return 0.5 * x * (1.0 + t)


def _gelu_and_grad(x):
    x2 = x * x
    t = jnp.tanh(_GELU_C * x * (1.0 + _GELU_A * x2))
    g = 0.5 * x * (1.0 + t)
    dg = 0.5 * (1.0 + t) + 0.5 * x * (1.0 - t * t) * _GELU_C * (1.0 + 3.0 * _GELU_A * x2)
    return g, dg


def _softplus(x):
    return jnp.maximum(x, 0.0) + jnp.log1p(jnp.exp(-jnp.abs(x)))


def _dot(a, b):
    return jnp.dot(a, b, preferred_element_type=F32)


def _dot_nt(a, b):
    return lax.dot_general(a, b, (((1,), (1,)), ((), ())), preferred_element_type=F32)


def _dot_tn(a, b):
    return lax.dot_general(a, b, (((0,), (0,)), ((), ())), preferred_element_type=F32)


def _rows_shifted(prev8, cur, k):
    ext = jnp.concatenate([prev8, cur], axis=0)
    return pltpu.roll(ext, k, 0)[8:]


def _rows_advanced(cur, next8, k):
    t = cur.shape[0]
    ext = jnp.concatenate([cur, next8], axis=0)
    return pltpu.roll(ext, t + 8 - k, 0)[:t]


def _first_second(x, y, c):
    ny, nx, far = _other_chips(x, y)
    pick = lambda a, b: a * (1 - c) + b * c
    first = tuple(pick(a, b) for a, b in zip(ny, nx))
    second = tuple(pick(b, a) for a, b in zip(ny, nx))
    return first, second, far


def _slot_order(x, y, c):
    chip = 2 * x + y
    first, second, far = _first_second(x, y, c)
    order = [2 * chip + c, 2 * chip + 1 - c, 2 * first[2] + c, 2 * second[2] + 1 - c, 2 * second[2] + c,
             2 * first[2] + 1 - c, 2 * far[2] + c, 2 * far[2] + 1 - c]
    return jnp.stack(order).astype(jnp.int32)


def _in_proj(x, g_mix, w_in_own, order, comm=None):
    s = x.shape[0]
    tm = min(TM_ROWS, s)
    ni = s // tm

    def body(order_ref, x_ref, g_ref, own_ref, z_ref, nt_ref, wg_ref, n_s, w_s, send_sems, recv_sems, local_sems):
        j, i = pl.program_id(0), pl.program_id(1)
        px, py, c = _place()
        chip = 2 * px + py
        me = 2 * chip + c
        sib = (px, py, 1 - c)
        chips = _other_chips(px, py)

        def rc(k, src, blk, to):
            return pltpu.make_async_remote_copy(src_ref=src, dst_ref=w_s.at[blk], send_sem=send_sems.at[k],
                                                recv_sem=recv_sems.at[k], device_id=to, device_id_type=MESH)

        del chips
        first, second, far = _first_second(px, py, c)
        blocks = [2 * first[2] + c, 2 * second[2] + c, 2 * far[2] + c]
        own_in = pltpu.make_async_copy(own_ref, w_s.at[me], local_sems.at[0])
        to_first = rc(1, own_ref, me, (first[0], first[1], c))
        to_second = rc(2, own_ref, me, (second[0], second[1], c))
        relay = rc(3, w_s.at[blocks[0]], blocks[0], (second[0], second[1], c))
        sends = [rc(0, own_ref, me, sib), to_first, to_second, relay]
        passed = [rc(4 + q, w_s.at[blk], blk, sib) for q, blk in enumerate(blocks)]
        keep = pltpu.make_async_copy(w_s, wg_ref, local_sems.at[1])

        @pl.when((i == 0) & (j == 0))
        def _():
            own_in.start()
            sends[0].start()
            to_first.start()
            own_in.wait()

        @pl.when((i == 0) & (j == 1))
        def _():
            rc(0, own_ref, 2 * chip + 1 - c, sib).wait_recv()

        for q, blk in enumerate(blocks):
            @pl.when((i == 0) & (j == 2 + 2 * q))
            def _():
                rc(1 + q, own_ref, blk, sib).wait_recv()
                passed[q].start()
                if q == 0:
                    to_second.start()
                    relay.start()

            @pl.when((i == 0) & (j == 3 + 2 * q))
            def _():
                rc(4 + q, own_ref, order_ref[j], sib).wait_recv()

        rows = pl.ds(pl.multiple_of(i * tm, tm), tm)

        @pl.when(j == 0)
        def _():
            xv = x_ref[...]
            rstd = lax.rsqrt(jnp.mean(xv * xv, axis=-1, keepdims=True) + NORM_EPS)
            nb = (xv * rstd * g_ref[...]).astype(BF16)
            n_s[rows, :] = nb
            nt_ref[...] = nb.T

        z_ref[...] = _dot(n_s[rows, :], w_s[order_ref[j]]).astype(BF16)

        @pl.when((i == 0) & (j == N_SLOT - 1))
        def _():
            keep.start()

        @pl.when((i == ni - 1) & (j == N_SLOT - 1))
        def _():
            for cp in sends + passed:
                cp.wait_send()
            keep.wait()

    first_pass = lambda j, i, o: (jnp.where(j == 0, i, ni - 1), 0)
    (z, n1, w_in_g), extra = _call(
        body, name="in_proj", grid=(N_SLOT, ni), prefetch=(order,),
        in_specs=[pl.BlockSpec((tm, D), first_pass),
                  pl.BlockSpec((1, D), lambda j, i, o: (0, 0)), ANY],
        out_specs=[pl.BlockSpec((tm, W_IN_COLS), lambda j, i, o: (i, o[j])),
                   pl.BlockSpec((D, tm), lambda j, i, o: (0, jnp.where(j == 0, i, ni - 1))), ANY],
        out_shape=[SDS((s, N_SLOT * W_IN_COLS), BF16), SDS((D, s), BF16), SDS((N_SLOT, D, W_IN_COLS), BF16)],
        scratch_shapes=[pltpu.VMEM((s, D), BF16), pltpu.VMEM((N_SLOT, D, W_IN_COLS), BF16),
                        pltpu.SemaphoreType.DMA((7,)), pltpu.SemaphoreType.DMA((7,)), pltpu.SemaphoreType.DMA((2,))],
        params=_cparams(("arbitrary", "arbitrary"), 56), args=(x, g_mix, w_in_own), comm=comm)
    return (z, n1, w_in_g), extra


def _decay(r, sp_lam):
    log_a = (-LRU_C) * r * sp_lam
    a = jnp.exp(log_a)
    return a, jnp.sqrt(-jnp.tanh(log_a) * (a * a + 1.0))


def _lru_gates(xc, xcb, wr_ref, br, wi_ref, bi, sp_lam, a_s, b_s, r_ref, i_ref):
    for h in range(HEADS):
        sl = slice(h * HEAD_DIM, (h + 1) * HEAD_DIM)
        r = jax.nn.sigmoid(_dot(xcb[:, sl], wr_ref[h]) + br[:, sl])
        ig = jax.nn.sigmoid(_dot(xcb[:, sl], wi_ref[h]) + bi[:, sl])
        a, mult = _decay(r, sp_lam[:, sl])
        a_s[:, sl] = a
        b_s[:, sl] = xc[:, sl] * ig * mult
        r_ref[:, sl] = r.astype(BF16)
        i_ref[:, sl] = ig.astype(BF16)


def _conv_fwd(xa, prev8, cw, cb):
    xc = cb + cw[0:1, :] * xa
    for k in range(1, CONV_K):
        xc = xc + cw[k:k + 1, :] * _rows_shifted(prev8, xa, k)
    return xc


def _branch_a_fwd(z, conv_w, conv_b, w_r, b_r, w_i, b_i, lam, comm=None):
    s = z.shape[0]
    ta = min(T_BRANCH_A, s)
    per16 = ta // 16

    def body(xa_ref, xp_ref, ga_ref, cw_ref, cb_ref, wr_ref, br_ref, wi_ref, bi_ref, lam_ref,
             ya_ref, hs_ref, xc_ref, r_ref, i_ref, a_s, b_s, h_s, carry_s):
        i = pl.program_id(0)

        @pl.when(i == 0)
        def _():
            carry_s[...] = jnp.zeros_like(carry_s)

        xa = xa_ref[...].astype(F32)
        prev8 = jnp.where(i > 0, xp_ref[...].astype(F32)[8:16], 0.0)
        xc = _conv_fwd(xa, prev8, cw_ref[...], cb_ref[...])
        xcb = xc.astype(BF16)
        xc_ref[...] = xcb
        sp_lam = _softplus(-lam_ref[...])
        _lru_gates(xc, xcb, wr_ref, br_ref[...], wi_ref, bi_ref[...], sp_lam, a_s, b_s, r_ref, i_ref)

        row = lax.broadcasted_iota(jnp.int32, (8, D), 0)

        def group(g, carry):
            off = pl.multiple_of(g * 8, 8)
            a8 = a_s[pl.ds(off, 8), :]
            b8 = b_s[pl.ds(off, 8), :]
            for d in (1, 2, 4):
                a_sh = jnp.where(row >= d, pltpu.roll(a8, d, 0), 1.0)
                b_sh = jnp.where(row >= d, pltpu.roll(b8, d, 0), 0.0)
                b8 = a8 * b_sh + b8
                a8 = a8 * a_sh
            h8 = b8 + a8 * carry
            h_s[pl.ds(off, 8), :] = h8
            return jnp.broadcast_to(h8[7:8, :], (8, D))

        carry_s[...] = lax.fori_loop(0, ta // 8, group, carry_s[...])
        hs = h_s[...]
        hs_ref[...] = hs.astype(BF16)
        ya_ref[...] = (hs * _gelu(ga_ref[...].astype(F32))).astype(BF16)

    vec = pl.BlockSpec((1, D), lambda i: (0, 0))
    gate = pl.BlockSpec((HEADS, HEAD_DIM, HEAD_DIM), lambda i: (0, 0, 0))
    return _call(
        body, name="branch_a_fwd", grid=(s // ta,),
        in_specs=[pl.BlockSpec((ta, D), lambda i: (i, 0)),
                  pl.BlockSpec((16, D), lambda i: (jnp.maximum(i * per16 - 1, 0), 0)),
                  pl.BlockSpec((ta, D), lambda i: (i, 1)),
                  pl.BlockSpec((CONV_K, D), lambda i: (0, 0)), vec, gate, vec, gate, vec, vec],
        out_specs=[pl.BlockSpec((ta, D), lambda i: (i, 0))] * 5,
        out_shape=[SDS((s, D), BF16)] * 5,
        scratch_shapes=[pltpu.VMEM((ta, D), F32), pltpu.VMEM((ta, D), F32), pltpu.VMEM((ta, D), F32),
                        pltpu.VMEM((8, D), F32)],
        params=_cparams(("arbitrary",), 40), args=(z, z, z, conv_w, conv_b, w_r, b_r, w_i, b_i, lam), comm=comm)


def _sgu_common(ub, vb, lg, lb, with_grad):
    if with_grad:
        u, du = _gelu_and_grad(ub)
        v, dv = _gelu_and_grad(vb)
    else:
        u, v, du, dv = _gelu(ub), _gelu(vb), None, None
    mu = jnp.mean(v, axis=-1, keepdims=True)
    vc = v - mu
    rstd = lax.rsqrt(jnp.mean(vc * vc, axis=-1, keepdims=True) + LN_EPS)
    vhat = vc * rstd
    vln = vhat * lg + lb
    return u, du, dv, rstd, vhat, vln


def _masked_ws(ws_ref):
    t = lax.broadcasted_iota(jnp.int32, (CHUNK, CHUNK), 0)
    c = lax.broadcasted_iota(jnp.int32, (CHUNK, CHUNK), 1)
    keep = c <= t
    return [jnp.where(keep, ws_ref[g], 0.0).astype(BF16) for g in range(GROUPS)]


def _branch_b_fwd(z, ln_g, ln_b, w_s, b_s_t, comm=None):
    s = z.shape[0]
    tb = min(T_BRANCH_B, s)

    def body(ub_ref, vb_ref, lg_ref, lb_ref, ws_ref, bs_ref, yb_ref):
        u, _, _, _, _, vln = _sgu_common(ub_ref[...].astype(F32), vb_ref[...].astype(F32),
                                         lg_ref[...], lb_ref[...], False)
        vlnb = vln.astype(BF16)
        wm = _masked_ws(ws_ref)
        bs = bs_ref[...]
        for c in range(tb // CHUNK):
            rs = slice(c * CHUNK, (c + 1) * CHUNK)
            for g in range(GROUPS):
                cs = slice(g * GROUP_DIM, (g + 1) * GROUP_DIM)
                sp = _dot(wm[g], vlnb[rs, cs]) + bs[:, g:g + 1]
                yb_ref[rs, cs] = (u[rs, cs] * sp).astype(BF16)

    vec = pl.BlockSpec((1, D), lambda i: (0, 0))
    return _call(
        body, name="branch_b_fwd", grid=(s // tb,),
        in_specs=[pl.BlockSpec((tb, D), lambda i: (i, 2)), pl.BlockSpec((tb, D), lambda i: (i, 3)), vec, vec,
                  pl.BlockSpec((GROUPS, CHUNK, CHUNK), lambda i: (0, 0, 0)),
                  pl.BlockSpec((CHUNK, GROUPS), lambda i: (0, 0))],
        out_specs=[pl.BlockSpec((tb, D), lambda i: (i, 0))],
        out_shape=[SDS((s, D), BF16)], scratch_shapes=[],
        params=_cparams(("arbitrary",), 40), args=(z, z, ln_g, ln_b, w_s, b_s_t), comm=comm)


def _merge_out(ya, yb, z, x, w_oa, w_ob, w_out, comm=None):
    s = x.shape[0]
    tm = min(TM_MERGE, s)

    def body(ya_ref, yb_ref, ma_ref, mb_ref, x_ref, woa_ref, wob_ref, wo_ref, pa_ref, pb_ref, mg_ref, h1_ref):
        pa = _dot(ya_ref[...], woa_ref[...])
        pb = _dot(yb_ref[...], wob_ref[...])
        merged = (jax.nn.sigmoid(ma_ref[...].astype(F32)) * pa
                  + jax.nn.sigmoid(mb_ref[...].astype(F32)) * pb).astype(BF16)
        pa_ref[...] = pa.astype(BF16)
        pb_ref[...] = pb.astype(BF16)
        mg_ref[...] = merged
        h1_ref[...] = x_ref[...] + _dot(merged, wo_ref[...])

    row = pl.BlockSpec((tm, D), lambda i: (i, 0))
    wsp = pl.BlockSpec((D, D), lambda i: (0, 0))
    return _call(
        body, name="merge_out", grid=(s // tm,),
        in_specs=[row, row, pl.BlockSpec((tm, D), lambda i: (i, 4)), pl.BlockSpec((tm, D), lambda i: (i, 5)),
                  row, wsp, wsp, wsp],
        out_specs=[row, row, row, row],
        out_shape=[SDS((s, D), BF16), SDS((s, D), BF16), SDS((s, D), BF16), SDS((s, D), F32)], scratch_shapes=[],
        params=_cparams(("arbitrary",), 48), args=(ya, yb, z, z, x, w_oa, w_ob, w_out), comm=comm)


def _mlp_fwd(h1, g_mlp, w_up_g, w_down, g_fin, tgt):
    s = h1.shape[0]
    tm = min(TM_ROWS, s)
    nj = N_SLOT

    def body(h1_ref, gm_ref, wu_ref, wd_ref, gf_ref, t_ref, r_ref, at_ref, n2t_ref, dh2_ref, loss_ref, dgf_ref,
             n2_s, acc_s):
        i, j = pl.program_id(0), pl.program_id(1)

        @pl.when(j == 0)
        def _():
            hv = h1_ref[...]
            rstd = lax.rsqrt(jnp.mean(hv * hv, axis=-1, keepdims=True) + NORM_EPS)
            nb = (hv * rstd * gm_ref[...]).astype(BF16)
            n2_s[...] = nb
            n2t_ref[...] = nb.T
            acc_s[...] = jnp.zeros_like(acc_s)

        @pl.when((i == 0) & (j == 0))
        def _():
            loss_ref[...] = jnp.zeros_like(loss_ref)
            dgf_ref[...] = jnp.zeros_like(dgf_ref)

        r = jnp.maximum(_dot(n2_s[...], wu_ref[...]), 0.0)
        r_ref[...] = r.astype(BF16)
        act = (r * r).astype(BF16)
        at_ref[...] = act.T
        acc_s[...] += _dot(act, wd_ref[...])

        @pl.when(j == nj - 1)
        def _():
            h2 = h1_ref[...] + acc_s[...]
            rstd = lax.rsqrt(jnp.mean(h2 * h2, axis=-1, keepdims=True) + NORM_EPS)
            hh = h2 * rstd
            gf = gf_ref[...]
            e = hh * gf - t_ref[...]
            loss_ref[...] += jnp.sum(e * e) * (0.5 / D)
            dy = e * (1.0 / D)
            dgf_ref[...] += jnp.sum(dy * hh, axis=0, keepdims=True)
            dhh = dy * gf
            dh2_ref[...] = rstd * (dhh - hh * jnp.mean(dhh * hh, axis=-1, keepdims=True))

    row = pl.BlockSpec((tm, D), lambda i, j: (i, 0))
    vec = pl.BlockSpec((1, D), lambda i, j: (0, 0))
    return pl.pallas_call(
        body, name="mlp_fwd", grid=(s // tm, nj),
        in_specs=[row, vec, pl.BlockSpec((None, D, FF_COLS), lambda i, j: (j, 0, 0)),
                  pl.BlockSpec((FF_COLS, D), lambda i, j: (j, 0)), vec, row],
        out_specs=[pl.BlockSpec((tm, FF_COLS), lambda i, j: (i, j)), pl.BlockSpec((FF_COLS, tm), lambda i, j: (j, i)),
                   pl.BlockSpec((D, tm), lambda i, j: (0, i)), row, pl.BlockSpec((8, 128), lambda i, j: (0, 0)), vec],
        out_shape=[SDS((s, nj * FF_COLS), BF16), SDS((nj * FF_COLS, s), BF16), SDS((D, s), BF16), SDS((s, D), F32),
                   SDS((8, 128), F32), SDS((1, D), F32)],
        scratch_shapes=[pltpu.VMEM((tm, D), BF16), pltpu.VMEM((tm, D), F32)],
        compiler_params=_cparams(("arbitrary", "arbitrary"), 52),
    )(h1, _small_in_hbm(g_mlp), w_up_g, w_down, _small_in_hbm(g_fin), tgt)


def _mlp_bwd(dh2, r, w_down, w_up_g, h1, g_mlp, comm=None):
    s = h1.shape[0]
    tm = min(TM_ROWS, s)
    nj = N_SLOT

    def body(dh2_ref, r_ref, wd_ref, wu_ref, h1_ref, gm_ref, df_ref, dh1_ref, dgm_ref, dh2b_s, acc_s):
        i, j = pl.program_id(0), pl.program_id(1)

        @pl.when(j == 0)
        def _():
            dh2b_s[...] = dh2_ref[...].astype(BF16)
            acc_s[...] = jnp.zeros_like(acc_s)

        @pl.when((i == 0) & (j == 0))
        def _():
            dgm_ref[...] = jnp.zeros_like(dgm_ref)

        d_act = _dot_nt(dh2b_s[...], wd_ref[...])
        df = (d_act * (2.0 * r_ref[...].astype(F32))).astype(BF16)
        df_ref[...] = df
        acc_s[...] += _dot_nt(df, wu_ref[...])

        @pl.when(j == nj - 1)
        def _():
            hv = h1_ref[...]
            rstd = lax.rsqrt(jnp.mean(hv * hv, axis=-1, keepdims=True) + NORM_EPS)
            hh = hv * rstd
            dn2 = acc_s[...]
            dgm_ref[...] += jnp.sum(dn2 * hh, axis=0, keepdims=True)
            dhat = dn2 * gm_ref[...]
            dh1_ref[...] = dh2_ref[...] + rstd * (dhat - hh * jnp.mean(dhat * hh, axis=-1, keepdims=True))

    row = pl.BlockSpec((tm, D), lambda i, j: (i, 0))
    vec = pl.BlockSpec((1, D), lambda i, j: (0, 0))
    ffb = pl.BlockSpec((tm, FF_COLS), lambda i, j: (i, j))
    return _call(
        body, name="mlp_bwd", grid=(s // tm, nj),
        in_specs=[row, ffb, pl.BlockSpec((FF_COLS, D), lambda i, j: (j, 0)),
                  pl.BlockSpec((None, D, FF_COLS), lambda i, j: (j, 0, 0)), row, vec],
        out_specs=[ffb, row, vec],
        out_shape=[SDS((s, nj * FF_COLS), BF16), SDS((s, D), F32), SDS((1, D), F32)],
        scratch_shapes=[pltpu.VMEM((tm, D), BF16), pltpu.VMEM((tm, D), F32)],
        params=_cparams(("arbitrary", "arbitrary"), 52), args=(dh2, r, w_down, w_up_g, h1, g_mlp), comm=comm)


def _merge_bwd(dh1, z, pa, pb, w_out, w_oa, w_ob, comm=None):
    s = dh1.shape[0]
    tm = min(TM_MERGE, s)

    def body(dh1_ref, ma_ref, mb_ref, pa_ref, pb_ref, wo_ref, woa_ref, wob_ref,
             dz_ref, dpa_ref, dpb_ref, dya_ref, dyb_ref):
        dm = _dot_nt(dh1_ref[...].astype(BF16), wo_ref[...])
        sa = jax.nn.sigmoid(ma_ref[...].astype(F32))
        sb = jax.nn.sigmoid(mb_ref[...].astype(F32))
        dpa = (dm * sa).astype(BF16)
        dpb = (dm * sb).astype(BF16)
        dz_ref[:, 0:D] = (dm * pa_ref[...].astype(F32) * sa * (1.0 - sa)).astype(BF16)
        dz_ref[:, D:2 * D] = (dm * pb_ref[...].astype(F32) * sb * (1.0 - sb)).astype(BF16)
        dpa_ref[...] = dpa
        dpb_ref[...] = dpb
        dya_ref[...] = _dot_nt(dpa, woa_ref[...]).astype(BF16)
        dyb_ref[...] = _dot_nt(dpb, wob_ref[...]).astype(BF16)

    row = pl.BlockSpec((tm, D), lambda i: (i, 0))
    wsp = pl.BlockSpec((D, D), lambda i: (0, 0))
    return _call(
        body, name="merge_bwd", grid=(s // tm,),
        in_specs=[row, pl.BlockSpec((tm, D), lambda i: (i, 4)), pl.BlockSpec((tm, D), lambda i: (i, 5)),
                  row, row, wsp, wsp, wsp],
        out_specs=[pl.BlockSpec((tm, 2 * D), lambda i: (i, 2)), row, row, row, row],
        out_shape=[SDS((s, 6 * D), BF16)] + [SDS((s, D), BF16)] * 4, scratch_shapes=[],
        params=_cparams(("arbitrary",), 48), args=(dh1, z, z, pa, pb, w_out, w_oa, w_ob), comm=comm)


def _branch_b_bwd(dz, dyb, z, ln_g, ln_b, w_s, b_s_t, comm=None):
    s = z.shape[0]
    tb = min(T_BRANCH_B, s)

    def body(dz_in, dyb_ref, ub_ref, vb_ref, lg_ref, lb_ref, ws_ref, bs_ref,
             dz_ref, dws_ref, dbs_ref, dln_ref, du_s, dvln_s):
        del dz_in

        @pl.when(pl.program_id(0) == 0)
        def _():
            dws_ref[...] = jnp.zeros_like(dws_ref)
            dbs_ref[...] = jnp.zeros_like(dbs_ref)
            dln_ref[...] = jnp.zeros_like(dln_ref)

        lg = lg_ref[...]
        u, du, dv, rstd, vhat, vln = _sgu_common(ub_ref[...].astype(F32), vb_ref[...].astype(F32),
                                                 lg, lb_ref[...], True)
        vlnb = vln.astype(BF16)
        dyb_v = dyb_ref[...].astype(F32)
        wm = _masked_ws(ws_ref)
        keep = (lax.broadcasted_iota(jnp.int32, (CHUNK, CHUNK), 1)
                <= lax.broadcasted_iota(jnp.int32, (CHUNK, CHUNK), 0))
        bs = bs_ref[...]
        for c in range(tb // CHUNK):
            rs = slice(c * CHUNK, (c + 1) * CHUNK)
            for g in range(GROUPS):
                cs = slice(g * GROUP_DIM, (g + 1) * GROUP_DIM)
                v_blk = vlnb[rs, cs]
                sp = _dot(wm[g], v_blk) + bs[:, g:g + 1]
                d_sp = dyb_v[rs, cs] * u[rs, cs]
                d_spb = d_sp.astype(BF16)
                du_s[rs, cs] = dyb_v[rs, cs] * sp
                dvln_s[rs, cs] = _dot_tn(wm[g], d_spb)
                dws_ref[g] += jnp.where(keep, _dot_nt(d_spb, v_blk), 0.0)
                dbs_ref[g] += jnp.broadcast_to(jnp.sum(d_sp, axis=-1, keepdims=True), (CHUNK, CHUNK))
        dvln = dvln_s[...]
        dln_ref[0:1, :] += jnp.sum(dvln * vhat, axis=0, keepdims=True)
        dln_ref[1:2, :] += jnp.sum(dvln, axis=0, keepdims=True)
        dvh = dvln * lg
        d_v = rstd * (dvh - jnp.mean(dvh, axis=-1, keepdims=True)
                      - vhat * jnp.mean(dvh * vhat, axis=-1, keepdims=True))
        dz_ref[:, 0:D] = (du_s[...] * du).astype(BF16)
        dz_ref[:, D:2 * D] = (d_v * dv).astype(BF16)

    vec = pl.BlockSpec((1, D), lambda i: (0, 0))
    sq = pl.BlockSpec((GROUPS, CHUNK, CHUNK), lambda i: (0, 0, 0))
    return _call(
        body, name="branch_b_bwd", grid=(s // tb,),
        in_specs=[ANY, pl.BlockSpec((tb, D), lambda i: (i, 0)),
                  pl.BlockSpec((tb, D), lambda i: (i, 2)), pl.BlockSpec((tb, D), lambda i: (i, 3)), vec, vec, sq,
                  pl.BlockSpec((CHUNK, GROUPS), lambda i: (0, 0))],
        out_specs=[pl.BlockSpec((tb, 2 * D), lambda i: (i, 1)), sq, sq, pl.BlockSpec((8, D), lambda i: (0, 0))],
        out_shape=[SDS(dz.shape, BF16), SDS((GROUPS, CHUNK, CHUNK), F32), SDS((GROUPS, CHUNK, CHUNK), F32),
                   SDS((8, D), F32)],
        scratch_shapes=[pltpu.VMEM((tb, D), F32), pltpu.VMEM((tb, D), F32)], aliases={0: 0},
        params=_cparams(("arbitrary",), 40), args=(dz, dyb, z, z, ln_g, ln_b, w_s, b_s_t), comm=comm)


def _branch_a_bwd(dz, dya, z, hs, xc, r, ig, conv_w, w_r, w_i, lam, comm=None):
    s = z.shape[0]
    ta = min(T_BRANCH_A, s)
    nb = s // ta
    per16 = ta // 16

    def body(dz_in, dya_ref, xa_ref, ga_ref, hs_ref, hp_ref, xc_ref, r_ref, i_ref, cw_ref, wr_ref, wi_ref,
             lam_ref, dz_ref, vec_ref, dwr_ref, dwi_ref, a_s, b_s, h_s, dcar_s, acar_s, dxc_s):
        del dz_in
        i = pl.program_id(0)
        blk = nb - 1 - i

        @pl.when(i == 0)
        def _():
            dcar_s[...] = jnp.zeros_like(dcar_s)
            acar_s[...] = jnp.zeros_like(acar_s)
            dxc_s[...] = jnp.zeros_like(dxc_s)
            vec_ref[...] = jnp.zeros_like(vec_ref)
            dwr_ref[...] = jnp.zeros_like(dwr_ref)
            dwi_ref[...] = jnp.zeros_like(dwi_ref)

        cw = cw_ref[...]
        lam_v = lam_ref[...]
        xa = xa_ref[...].astype(F32)
        xcb = xc_ref[...]
        xc = xcb.astype(F32)
        sp_lam = _softplus(-lam_v)
        r_v, i_v = r_ref[...].astype(F32), i_ref[...].astype(F32)
        a_v, m_v = _decay(r_v, sp_lam)

        hs_v = hs_ref[...].astype(F32)
        hprev8 = jnp.where(blk > 0, hp_ref[...].astype(F32)[8:16], 0.0)
        h_m1 = _rows_shifted(hprev8, hs_v, 1)
        gg, dgg = _gelu_and_grad(ga_ref[...].astype(F32))
        dya_v = dya_ref[...].astype(F32)
        dz_ref[:, D:2 * D] = (dya_v * hs_v * dgg).astype(BF16)

        a_s[...] = _rows_advanced(a_v, acar_s[...], 1)
        b_s[...] = dya_v * gg

        row = lax.broadcasted_iota(jnp.int32, (8, D), 0)
        ng = ta // 8

        def group(gi, carry):
            off = pl.multiple_of((ng - 1 - gi) * 8, 8)
            c8 = a_s[pl.ds(off, 8), :]
            d8 = b_s[pl.ds(off, 8), :]
            for d in (1, 2, 4):
                c_sh = jnp.where(row < 8 - d, pltpu.roll(c8, 8 - d, 0), 1.0)
                d_sh = jnp.where(row < 8 - d, pltpu.roll(d8, 8 - d, 0), 0.0)
                d8 = c8 * d_sh + d8
                c8 = c8 * c_sh
            dh8 = d8 + c8 * carry
            h_s[pl.ds(off, 8), :] = dh8
            return jnp.broadcast_to(dh8[0:1, :], (8, D))

        dcar_s[...] = lax.fori_loop(0, ng, group, dcar_s[...])
        acar_s[...] = jnp.broadcast_to(a_v[0:1, :], (8, D))

        dbx = h_s[...]
        d_mult = dbx * xc * i_v
        d_loga = dbx * h_m1 * a_v - d_mult * (a_v * a_v) / m_v
        d_pr = d_loga * ((-LRU_C) * sp_lam) * r_v * (1.0 - r_v)
        d_pi = dbx * xc * m_v * i_v * (1.0 - i_v)
        vec_ref[7:8, :] += jnp.sum(d_loga * r_v, axis=0, keepdims=True) * (LRU_C * jax.nn.sigmoid(-lam_v))
        vec_ref[5:6, :] += jnp.sum(d_pr, axis=0, keepdims=True)
        vec_ref[6:7, :] += jnp.sum(d_pi, axis=0, keepdims=True)
        d_prb = d_pr.astype(BF16)
        d_pib = d_pi.astype(BF16)
        h_s[...] = dbx * i_v * m_v
        for h in range(HEADS):
            sl = slice(h * HEAD_DIM, (h + 1) * HEAD_DIM)
            h_s[:, sl] += _dot_nt(d_prb[:, sl], wr_ref[h]) + _dot_nt(d_pib[:, sl], wi_ref[h])
            dwr_ref[h] += _dot_tn(xcb[:, sl], d_prb[:, sl])
            dwi_ref[h] += _dot_tn(xcb[:, sl], d_pib[:, sl])
        d_xc = h_s[...]
        vec_ref[4:5, :] += jnp.sum(d_xc, axis=0, keepdims=True)
        vec_ref[0:1, :] += jnp.sum(d_xc * xa, axis=0, keepdims=True)
        d_xa = cw[0:1, :] * d_xc
        nxt = dxc_s[...]
        for k in range(1, CONV_K):
            ahead = _rows_advanced(d_xc, nxt, k)
            vec_ref[k:k + 1, :] += jnp.sum(ahead * xa, axis=0, keepdims=True)
            d_xa = d_xa + cw[k:k + 1, :] * ahead
        dz_ref[:, 0:D] = d_xa.astype(BF16)
        dxc_s[...] = d_xc[0:8, :]

    vec = pl.BlockSpec((1, D), lambda i: (0, 0))
    gate = pl.BlockSpec((HEADS, HEAD_DIM, HEAD_DIM), lambda i: (0, 0, 0))
    cur = lambda c: pl.BlockSpec((ta, D), lambda i: (nb - 1 - i, c))
    before = lambda c: pl.BlockSpec((16, D), lambda i: (jnp.maximum((nb - 1 - i) * per16 - 1, 0), c))
    return _call(
        body, name="branch_a_bwd", grid=(nb,),
        in_specs=[ANY, cur(0), cur(0), cur(1), cur(0), before(0), cur(0), cur(0), cur(0),
                  pl.BlockSpec((CONV_K, D), lambda i: (0, 0)), gate, gate, vec],
        out_specs=[pl.BlockSpec((ta, 2 * D), lambda i: (nb - 1 - i, 0)), pl.BlockSpec((8, D), lambda i: (0, 0)),
                   gate, gate],
        out_shape=[SDS(dz.shape, BF16), SDS((8, D), F32), SDS((HEADS, HEAD_DIM, HEAD_DIM), F32),
                   SDS((HEADS, HEAD_DIM, HEAD_DIM), F32)],
        scratch_shapes=[pltpu.VMEM((ta, D), F32)] * 3 + [pltpu.VMEM((8, D), F32)] * 3, aliases={0: 0},
        params=_cparams(("arbitrary",), 48),
        args=(dz, dya, z, z, hs, hs, xc, r, ig, conv_w, w_r, w_i, lam), comm=comm)


def _in_bwd(dz, w_in_g, x, dh1, g_mix, comm=None):
    s = x.shape[0]
    tm = min(TM_ROWS, s)
    nj = N_SLOT

    def body(dz_ref, w_ref, x_ref, dh1_ref, g_ref, dx_ref, dg_ref, acc_s):
        i, j = pl.program_id(0), pl.program_id(1)

        @pl.when(j == 0)
        def _():
            acc_s[...] = jnp.zeros_like(acc_s)

        @pl.when((i == 0) & (j == 0))
        def _():
            dg_ref[...] = jnp.zeros_like(dg_ref)

        acc_s[...] += _dot_nt(dz_ref[...], w_ref[...])

        @pl.when(j == nj - 1)
        def _():
            xv = x_ref[...]
            rstd = lax.rsqrt(jnp.mean(xv * xv, axis=-1, keepdims=True) + NORM_EPS)
            xh = xv * rstd
            dn = acc_s[...]
            dg_ref[...] += jnp.sum(dn * xh, axis=0, keepdims=True)
            dhat = dn * g_ref[...]
            dx_ref[...] = dh1_ref[...] + rstd * (dhat - xh * jnp.mean(dhat * xh, axis=-1, keepdims=True))

    row = pl.BlockSpec((tm, D), lambda i, j: (i, 0))
    vec = pl.BlockSpec((1, D), lambda i, j: (0, 0))
    return _call(
        body, name="in_bwd", grid=(s // tm, nj),
        in_specs=[pl.BlockSpec((tm, W_IN_COLS), lambda i, j: (i, j)),
                  pl.BlockSpec((None, D, W_IN_COLS), lambda i, j: (j, 0, 0)), row, row, vec],
        out_specs=[row, vec],
        out_shape=[SDS((s, D), F32), SDS((1, D), F32)],
        scratch_shapes=[pltpu.VMEM((tm, D), F32)],
        params=_cparams(("arbitrary", "arbitrary"), 48), args=(dz, w_in_g, x, dh1, g_mix), comm=comm)


def _wgrad(name, a, b, nblk, a_split, b_split):
    s = a.shape[0]
    ts = min(TM_ROWS, s)
    a_w = a.shape[1] // nblk if a_split else a.shape[1]
    b_w = b.shape[1] // nblk if b_split else b.shape[1]

    def body(a_ref, b_ref, o_ref, acc_s):
        t = pl.program_id(1)

        @pl.when(t == 0)
        def _():
            acc_s[...] = jnp.zeros_like(acc_s)

        acc_s[...] += _dot_tn(a_ref[...].astype(BF16), b_ref[...].astype(BF16))

        @pl.when(t == pl.num_programs(1) - 1)
        def _():
            o_ref[...] = acc_s[...].astype(BF16)

    return pl.pallas_call(
        body, name=name, grid=(nblk, s // ts),
        in_specs=[pl.BlockSpec((ts, a_w), (lambda k, t: (t, k)) if a_split else (lambda k, t: (t, 0))),
                  pl.BlockSpec((ts, b_w), (lambda k, t: (t, k)) if b_split else (lambda k, t: (t, 0)))],
        out_specs=pl.BlockSpec((None, a_w, b_w), lambda k, t: (k, 0, 0)),
        out_shape=SDS((nblk, a_w, b_w), BF16),
        scratch_shapes=[pltpu.VMEM((a_w, b_w), F32)],
        compiler_params=_cparams(("arbitrary", "arbitrary"), 48),
    )(a, b)


def _wgrad_t(name, a_t, b, nblk, a_split, b_split, tokens):
    s = b.shape[0]
    ts = min(tokens, s)
    a_w = a_t.shape[0] // nblk if a_split else a_t.shape[0]
    b_w = b.shape[1] // nblk if b_split else b.shape[1]

    def body(a_ref, b_ref, o_ref, acc_s):
        t = pl.program_id(1)

        @pl.when(t == 0)
        def _():
            acc_s[...] = jnp.zeros_like(acc_s)

        acc_s[...] += _dot(a_ref[...], b_ref[...].astype(BF16))

        @pl.when(t == pl.num_programs(1) - 1)
        def _():
            o_ref[...] = acc_s[...].astype(BF16)

    return pl.pallas_call(
        body, name=name, grid=(nblk, s // ts),
        in_specs=[pl.BlockSpec((a_w, ts), (lambda k, t: (k, t)) if a_split else (lambda k, t: (0, t))),
                  pl.BlockSpec((ts, b_w), (lambda k, t: (t, k)) if b_split else (lambda k, t: (t, 0)))],
        out_specs=pl.BlockSpec((None, a_w, b_w), lambda k, t: (k, 0, 0)),
        out_shape=SDS((nblk, a_w, b_w), BF16),
        scratch_shapes=[pltpu.VMEM((a_w, b_w), F32)],
        compiler_params=_cparams(("arbitrary", "arbitrary"), 48),
    )(a_t, b)


def _place():
    x, y, c = lax.axis_index("x"), lax.axis_index("y"), lax.axis_index("c")
    return x, y, c


def _other_chips(x, y):
    return [(x, 1 - y, 2 * x + 1 - y), (1 - x, y, 2 * (1 - x) + y), (1 - x, 1 - y, 2 * (1 - x) + 1 - y)]


class _Plan:
    def __init__(self, arrays, out_shape, sems, start, finish, middle=None, middle_at=6):
        self.arrays, self.out_shape, self.sems, self.start, self.finish = arrays, out_shape, sems, start, finish
        self.middle, self.middle_at = middle, middle_at


def _gather_plan(shards, middle_at=6):
    n = len(shards)

    def copies(ins, outs, sems):
        send_sems, recv_sems, local_sems = sems
        x, y, c = _place()
        chip = 2 * x + y
        me = 2 * chip + c
        sib = (x, y, 1 - c)
        chips = _other_chips(x, y)

        def rc(k, t, src, blk, to):
            return pltpu.make_async_remote_copy(
                src_ref=src, dst_ref=outs[t].at[blk], send_sem=send_sems.at[k * n + t],
                recv_sem=recv_sems.at[k * n + t], device_id=to, device_id_type=MESH)

        (yx, yy, y_chip), (xx, xy, x_chip), _ = chips
        local = [pltpu.make_async_copy(ins[t], outs[t].at[me], local_sems.at[t]) for t in range(n)]
        sends = ([rc(0, t, ins[t], me, sib) for t in range(n)] + [rc(1, t, ins[t], me, (yx, yy, c)) for t in range(n)]
                 + [rc(2, t, ins[t], me, (xx, xy, c)) for t in range(n)])
        passed = [[rc(4 + j, t, outs[t].at[2 * pc + c], 2 * pc + c, sib) for t in range(n)]
                  for j, (_, _, pc) in enumerate(chips)]
        relays = [[rc(3, t, outs[t].at[2 * y_chip + c], 2 * y_chip + c, (xx, xy, c)) for t in range(n)],
                  [rc(3, t, outs[t].at[2 * x_chip + c], 2 * x_chip + c, (yx, yy, c)) for t in range(n)]]
        return rc, local, sends, passed, relays, chips, chip, c, sib

    def start(ins, outs, sems):
        _, local, sends, _, _, _, _, _, _ = copies(ins, outs, sems)
        for cp in local + sends:
            cp.start()

    def middle(ins, outs, sems):
        rc, _, _, passed, relays, chips, _, c, sib = copies(ins, outs, sems)
        for j in range(2):
            for t in range(n):
                rc(1 + j, t, ins[t], 2 * chips[j][2] + c, sib).wait_recv()
        for j in range(2):
            for cp in passed[j]:
                cp.start()

            @pl.when(c == j)
            def _():
                for cp in relays[j]:
                    cp.start()

    def finish(ins, outs, sems):
        rc, local, sends, passed, relays, chips, chip, c, sib = copies(ins, outs, sems)
        far = 2 * chips[2][2] + c
        for t in range(n):
            rc(3, t, ins[t], far, sib).wait_recv()
        for cp in passed[2]:
            cp.start()
        for t in range(n):
            rc(0, t, ins[t], 2 * chip + 1 - c, sib).wait_recv()
        for j, (px, py, pc) in enumerate(chips):
            for t in range(n):
                rc(4 + j, t, ins[t], 2 * pc + 1 - c, sib).wait_recv()
        for cp in sends + passed[0] + passed[1] + passed[2]:
            cp.wait_send()
        for j in range(2):
            @pl.when(c == j)
            def _():
                for cp in relays[j]:
                    cp.wait_send()
        for cp in local:
            cp.wait()

    return _Plan(list(shards), [SDS((N_SLOT,) + tuple(a.shape), a.dtype) for a in shards],
                 [pltpu.SemaphoreType.DMA((7 * n,)), pltpu.SemaphoreType.DMA((7 * n,)),
                  pltpu.SemaphoreType.DMA((n,))], start, finish, middle, middle_at)


def _sibling_plan(grads, whole=()):
    n, m = len(grads), len(whole)

    def copies(ins, outs, sems):
        send_sems, recv_sems = sems
        x, y, c = _place()
        sib = (x, y, 1 - c)

        def rc(t, src, dst):
            return pltpu.make_async_remote_copy(src_ref=src, dst_ref=dst, send_sem=send_sems.at[t],
                                                recv_sem=recv_sems.at[t], device_id=sib, device_id_type=MESH)
        return rc, c

    def start(ins, outs, sems):
        rc, c = copies(ins, outs, sems)
        for t in range(n):
            for j in range(4):
                rc(t, ins[t].at[2 * j + 1 - c], outs[t].at[j]).start()
        for t in range(n, n + m):
            rc(t, ins[t], outs[t]).start()

    def finish(ins, outs, sems):
        rc, _ = copies(ins, outs, sems)
        for t in range(n):
            rc(t, ins[t].at[pl.ds(0, 4)], outs[t]).wait()
        for t in range(n, n + m):
            rc(t, ins[t], outs[t]).wait()

    return _Plan(list(grads) + list(whole),
                 [SDS((4,) + tuple(g.shape[1:]), g.dtype) for g in grads] + [SDS(a.shape, a.dtype) for a in whole],
                 [pltpu.SemaphoreType.DMA((n + m,)), pltpu.SemaphoreType.DMA((n + m,))], start, finish)


def _chips_plan(parts, whole=()):
    n, m = len(parts), len(whole)

    def src_of(ins, t, pc):
        return ins[t].at[pc] if t < n else ins[t]

    def local_copies(ins, outs, sems, chip):
        return [pltpu.make_async_copy(src_of(ins, t, chip), outs[t].at[chip], sems[2].at[t]) for t in range(n + m)]

    def start(ins, outs, sems):
        send_sems, recv_sems, _ = sems
        x, y, c = _place()
        chip = 2 * x + y
        for cp in local_copies(ins, outs, sems, chip):
            cp.start()
        for px, py, pc in _other_chips(x, y):
            for t in range(n + m):
                pltpu.make_async_remote_copy(src_ref=src_of(ins, t, pc), dst_ref=outs[t].at[chip],
                                             send_sem=send_sems.at[t], recv_sem=recv_sems.at[t],
                                             device_id=(px, py, c), device_id_type=MESH).start()

    def finish(ins, outs, sems):
        send_sems, recv_sems, _ = sems
        x, y, c = _place()
        for t in range(n + m):
            three = outs[t].at[pl.ds(0, 3)]
            pltpu.make_async_remote_copy(src_ref=three, dst_ref=three, send_sem=send_sems.at[t],
                                         recv_sem=recv_sems.at[t], device_id=(x, y, c), device_id_type=MESH).wait()
        for cp in local_copies(ins, outs, sems, 2 * x + y):
            cp.wait()

    return _Plan(list(parts) + list(whole),
                 [SDS(p.shape, p.dtype) for p in parts] + [SDS((4,) + tuple(a.shape), a.dtype) for a in whole],
                 [pltpu.SemaphoreType.DMA((n + m,)), pltpu.SemaphoreType.DMA((n + m,)),
                  pltpu.SemaphoreType.DMA((n + m,))], start, finish)


def _exchange_plan(arr):
    def peers(x, y, c):
        flip = lambda v, f: 1 - v if f else v
        return [(flip(x, fx), flip(y, fy), flip(c, fc))
                for fx in (0, 1) for fy in (0, 1) for fc in (0, 1) if fx or fy or fc]

    def start(ins, outs, sems):
        x, y, c = _place()
        me = 4 * x + 2 * y + c
        pltpu.make_async_copy(ins[0], outs[0].at[me], sems[2].at[0]).start()
        for to in peers(x, y, c):
            pltpu.make_async_remote_copy(src_ref=ins[0], dst_ref=outs[0].at[me], send_sem=sems[0].at[0],
                                         recv_sem=sems[1].at[0], device_id=to, device_id_type=MESH).start()

    def finish(ins, outs, sems):
        x, y, c = _place()
        seven = outs[0].at[pl.ds(0, 7)]
        pltpu.make_async_remote_copy(src_ref=seven, dst_ref=seven, send_sem=sems[0].at[0], recv_sem=sems[1].at[0],
                                     device_id=(x, y, c), device_id_type=MESH).wait()
        pltpu.make_async_copy(ins[0], outs[0].at[4 * x + 2 * y + c], sems[2].at[0]).wait()

    return _Plan([arr], [SDS((N_SLOT,) + tuple(arr.shape), arr.dtype)],
                 [pltpu.SemaphoreType.DMA((1,)), pltpu.SemaphoreType.DMA((1,)), pltpu.SemaphoreType.DMA((1,))],
                 start, finish)


def _join(*plans):
    def cut(seq, sizes):
        out, at = [], 0
        for k in sizes:
            out.append(seq[at:at + k])
            at += k
        return out

    n_arr = [len(p.arrays) for p in plans]
    n_sem = [len(p.sems) for p in plans]

    def start(ins, outs, sems):
        for p, i, o, s in zip(plans, cut(ins, n_arr), cut(outs, n_arr), cut(sems, n_sem)):
            p.start(i, o, s)

    def finish(ins, outs, sems):
        for p, i, o, s in zip(plans, cut(ins, n_arr), cut(outs, n_arr), cut(sems, n_sem)):
            p.finish(i, o, s)

    def middle(ins, outs, sems):
        for p, i, o, s in zip(plans, cut(ins, n_arr), cut(outs, n_arr), cut(sems, n_sem)):
            if p.middle is not None:
                p.middle(i, o, s)

    return _Plan([a for p in plans for a in p.arrays], [o for p in plans for o in p.out_shape],
                 [s for p in plans for s in p.sems], start, finish,
                 middle if any(p.middle is not None for p in plans) else None,
                 max(p.middle_at for p in plans))


def _run_plan(name, plan):
    k = len(plan.arrays)

    def body(*refs):
        ins, outs, sems = refs[:k], refs[k:2 * k], refs[2 * k:]
        plan.start(ins, outs, sems)
        if plan.middle is not None:
            plan.middle(ins, outs, sems)
        plan.finish(ins, outs, sems)

    return pl.pallas_call(
        body, name=name, in_specs=[ANY] * k, out_specs=[ANY] * k, out_shape=plan.out_shape,
        scratch_shapes=plan.sems, compiler_params=pltpu.CompilerParams(has_side_effects=True),
    )(*plan.arrays)


def _call(body, *, name, grid, in_specs, out_specs, out_shape, scratch_shapes, params, args, comm=None,
          aliases=None, prefetch=()):
    aliases = aliases or {}
    n_pre = len(prefetch)

    def launch(fn, ins_specs, outs_specs, outs_shape, scratch, operands):
        spec = pltpu.PrefetchScalarGridSpec(num_scalar_prefetch=n_pre, grid=grid, in_specs=ins_specs,
                                            out_specs=outs_specs, scratch_shapes=scratch)
        return pl.pallas_call(fn, name=name, grid_spec=spec, out_shape=outs_shape, compiler_params=params,
                              input_output_aliases=aliases)(*prefetch, *[_small_in_hbm(a) for a in operands])

    if comm is None:
        return list(launch(body, in_specs, out_specs, out_shape, scratch_shapes, args)), []
    n_in, n_out, n_scr, k = len(in_specs), len(out_specs), len(scratch_shapes), len(comm.arrays)

    def wrapped(*refs):
        pre, refs = refs[:n_pre], refs[n_pre:]
        ins = refs[:n_in]
        c_in = refs[n_in:n_in + k]
        outs = refs[n_in + k:n_in + k + n_out]
        c_out = refs[n_in + k + n_out:n_in + 2 * k + n_out]
        scr = refs[n_in + 2 * k + n_out:n_in + 2 * k + n_out + n_scr]
        sems = refs[n_in + 2 * k + n_out + n_scr:]
        step, steps = pl.program_id(0), grid[0]
        for d in range(1, len(grid)):
            step, steps = step * grid[d] + pl.program_id(d), steps * grid[d]

        @pl.when(step == 0)
        def _():
            comm.start(c_in, c_out, sems)

        if comm.middle is not None:
            @pl.when(step == (comm.middle_at * steps) // 8)
            def _():
                comm.middle(c_in, c_out, sems)

        body(*pre, *ins, *outs, *scr)

        @pl.when(step == steps - 1)
        def _():
            comm.finish(c_in, c_out, sems)

    res = launch(wrapped, list(in_specs) + [ANY] * k, list(out_specs) + [ANY] * k,
                 list(out_shape) + list(comm.out_shape), list(scratch_shapes) + list(comm.sems),
                 tuple(args) + tuple(comm.arrays))
    return list(res[:n_out]), list(res[n_out:])


def _row_tile(rows):
    for t in (512, 256, 128, 64, 32, 16, 8):
        if rows % t == 0:
            return t
    return rows


def _pair_sum(name, g8, recv4, core):
    _, rows, cols = recv4.shape
    tr = _row_tile(rows)
    g42 = g8.reshape(4, 2, rows, cols)

    def body(c_ref, g_ref, r_ref, o_ref):
        del c_ref
        o_ref[...] = (g_ref[...].astype(F32) + r_ref[...].astype(F32)).astype(o_ref.dtype)

    return pl.pallas_call(
        body, name=name,
        grid_spec=pltpu.PrefetchScalarGridSpec(
            num_scalar_prefetch=1, grid=(4, rows // tr),
            in_specs=[pl.BlockSpec((None, None, tr, cols), lambda j, i, c_ref: (j, c_ref[0], i, 0)),
                      pl.BlockSpec((None, tr, cols), lambda j, i, c_ref: (j, i, 0))],
            out_specs=pl.BlockSpec((None, tr, cols), lambda j, i, c_ref: (j, i, 0))),
        out_shape=SDS(recv4.shape, g8.dtype),
        compiler_params=_cparams(("arbitrary", "arbitrary"), 32),
    )(core, g42, recv4)


def _add2(name, a, b):
    rows, cols = a.shape
    tr = _row_tile(rows)

    def body(a_ref, b_ref, o_ref):
        o_ref[...] = a_ref[...] + b_ref[...]

    blk = pl.BlockSpec((tr, cols), lambda i: (i, 0))
    return pl.pallas_call(body, name=name, grid=(rows // tr,), in_specs=[blk, blk], out_specs=blk,
                          out_shape=SDS(a.shape, a.dtype),
                          compiler_params=_cparams(("arbitrary",), 32))(a, b)


def _sum_terms(name, terms):
    k, rows, cols = terms.shape
    tr = _row_tile(rows)

    def body(r_ref, o_ref):
        acc = r_ref[0]
        for q in range(1, k):
            acc = acc + r_ref[q]
        o_ref[...] = acc

    return pl.pallas_call(body, name=name, grid=(rows // tr,),
                          in_specs=[pl.BlockSpec((k, tr, cols), lambda i: (0, i, 0))],
                          out_specs=pl.BlockSpec((tr, cols), lambda i: (i, 0)),
                          out_shape=SDS((rows, cols), terms.dtype),
                          compiler_params=_cparams(("arbitrary",), 32))(terms)


def _adam_update(g, w, m, v):
    c1 = 1.0 / (1.0 - ADAM_B1 ** ADAM_STEP)
    c2 = 1.0 / (1.0 - ADAM_B2 ** ADAM_STEP)
    mn = ADAM_B1 * m + (1.0 - ADAM_B1) * g
    vn = ADAM_B2 * v + (1.0 - ADAM_B2) * (g * g)
    delta = (-ADAM_LR) * ((mn * c1) / (jnp.sqrt(vn * c2) + ADAM_EPS) + ADAM_WD * w)
    return delta, mn, vn


def _adamw_many(name, gs, ws, ms, vs):
    n = len(gs)

    def body(*refs):
        for p in range(n):
            g, w, m, v = (refs[q * n + p][...] for q in range(4))
            d, mn, vn = _adam_update(g, w, m, v)
            refs[4 * n + p][...] = d
            refs[5 * n + p][...] = mn
            refs[6 * n + p][...] = vn

    full = [pl.BlockSpec(w.shape, lambda i: (0, 0)) for w in ws]
    shapes = [SDS(w.shape, F32) for w in ws]
    res = pl.pallas_call(body, name=name, grid=(1,), in_specs=full * 4, out_specs=full * 3, out_shape=shapes * 3,
                         compiler_params=_cparams(("arbitrary",), 32),
                         )(*[_small_in_hbm(a) for a in (*gs, *ws, *ms, *vs)])
    return [(res[p], res[n + p], res[2 * n + p]) for p in range(n)]


def _adamw(name, terms, w, m, v):
    k, rows, cols = terms.shape
    tr = _row_tile(rows)

    def body(t_ref, w_ref, m_ref, v_ref, g_ref, d_ref, mo_ref, vo_ref):
        g = t_ref[0].astype(F32)
        for q in range(1, k):
            g = g + t_ref[q].astype(F32)
        g_ref[...] = g
        d_ref[...], mo_ref[...], vo_ref[...] = _adam_update(g, w_ref[...], m_ref[...], v_ref[...])

    blk = pl.BlockSpec((tr, cols), lambda i: (i, 0))
    return pl.pallas_call(body, name=name, grid=(rows // tr,),
                          in_specs=[pl.BlockSpec((k, tr, cols), lambda i: (0, i, 0)), blk, blk, blk],
                          out_specs=[blk] * 4, out_shape=[SDS((rows, cols), F32)] * 4,
                          compiler_params=_cparams(("arbitrary",), 40),
                          )(*[pltpu.with_memory_space_constraint(a, pltpu.HBM) for a in (terms, w, m, v)])


def kernel(x, norm_mix_g, w_in, conv_w, conv_b, w_rgate, b_rgate, w_igate, b_igate, lru_lambda, w_out_a, sgu_ln_g, sgu_ln_b, sgu_w_s, sgu_b_s, w_out_b, w_out, norm_mlp_g, w_up, w_down, norm_final_g, loss_target, m_norm_mix_g, m_w_in, m_conv_w, m_conv_b, m_w_rgate, m_b_rgate, m_w_igate, m_b_igate, m_lru_lambda, m_w_out_a, m_sgu_ln_g, m_sgu_ln_b, m_sgu_w_s, m_sgu_b_s, m_w_out_b, m_w_out, m_norm_mlp_g, m_w_up, m_w_down, m_norm_final_g, v_norm_mix_g, v_w_in, v_conv_w, v_conv_b, v_w_rgate, v_b_rgate, v_w_igate, v_b_igate, v_lru_lambda, v_w_out_a, v_sgu_ln_g, v_sgu_ln_b, v_sgu_w_s, v_sgu_b_s, v_w_out_b, v_w_out, v_norm_mlp_g, v_w_up, v_w_down, v_norm_final_g):
    cx, cy, cc = _place()
    me = 4 * cx + 2 * cy + cc
    core = jnp.reshape(cc, (1,)).astype(jnp.int32)
    xs = x[0]
    tgt = loss_target[0]
    s = xs.shape[0]

    gate_shard = jnp.stack([w_rgate[0], w_igate[0]]).astype(BF16).reshape(2 * HEADS * 32, HEAD_DIM)
    vec_shard = jnp.concatenate([conv_w[0], b_rgate[0], b_igate[0]], axis=1)
    vec_shard = jnp.pad(vec_shard, ((0, 4), (0, 256 - vec_shard.shape[1])))
    shards = [w_in[0].astype(BF16), w_out_a[0].astype(BF16), w_out_b[0].astype(BF16), w_out[0].astype(BF16),
              w_up[0].astype(BF16), w_down[0].astype(BF16), gate_shard, vec_shard]
    (z, n1_t, w_in_g), (gate_g, vec_g) = _in_proj(xs, norm_mix_g, shards[0], _slot_order(cx, cy, cc),
                                                comm=_gather_plan(shards[6:8]))
    gates = gate_g.reshape(N_SLOT, 2, HEADS, 32, HEAD_DIM).transpose(1, 2, 0, 3, 4).reshape(2, HEADS, HEAD_DIM, HEAD_DIM)
    w_r_f, w_i_f = gates[0], gates[1]
    conv_w_f = vec_g[:, 0:4, 0:128].transpose(1, 0, 2).reshape(CONV_K, D)
    b_r_f = vec_g[:, 0:4, 128:160].transpose(1, 0, 2).reshape(1, D)
    b_i_f = vec_g[:, 0:4, 160:192].transpose(1, 0, 2).reshape(1, D)
    b_s_t = jnp.transpose(sgu_b_s[0])

    (ya, hs, xc, r_gate, i_gate), (w_oa_g, w_ob_g, w_out_g, w_up_g) = _branch_a_fwd(
        z, conv_w_f, conv_b, w_r_f, b_r_f, w_i_f, b_i_f, lru_lambda, comm=_gather_plan(shards[1:5]))
    w_oa_f = w_oa_g.reshape(D, D)
    w_ob_f = w_ob_g.reshape(D, D)
    w_out_f = w_out_g.reshape(D, D)
    (yb,), _ = _branch_b_fwd(z, sgu_ln_g, sgu_ln_b, sgu_w_s[0], b_s_t)
    (pa, pb, merged, h1), (w_down_g,) = _merge_out(ya, yb, z, xs, w_oa_f, w_ob_f, w_out_f,
                                                   comm=_gather_plan(shards[5:6], middle_at=4))
    w_down_f = w_down_g.reshape(N_SLOT * FF_COLS, D)
    gf2 = norm_final_g.reshape(1, D)
    r_act, act_t, n2_t, dh2, loss_acc, d_gfin = _mlp_fwd(h1, norm_mlp_g, w_up_g, w_down_f, gf2, tgt)

    def pair(names, grads, recv):
        return [_pair_sum("pair_sum_" + nm, g, r, core) for nm, g, r in zip(names, grads, recv)]

    g_down = _wgrad_t("wgrad_down", act_t, dh2, N_SLOT // 2, True, False, 2048)
    g_down = g_down.reshape(N_SLOT, FF_COLS, D)
    (df, dh1, d_gmlp), (r_down,) = _mlp_bwd(dh2, r_act, w_down_f, w_up_g, h1, norm_mlp_g,
                                            comm=_sibling_plan([g_down]))
    (p_down,) = pair(["down"], [g_down], [r_down])
    g_up = _wgrad_t("wgrad_up", n2_t, df, N_SLOT, False, True, 4096)
    g_out = _wgrad("wgrad_out", merged, dh1, 1, False, False).reshape(N_SLOT, D // N_SLOT, D)
    (dz, dpa, dpb, dya, dyb), (got_down, r_up, r_out) = _merge_bwd(
        dh1, z, pa, pb, w_out_f, w_oa_f, w_ob_f, comm=_join(_chips_plan([p_down]), _sibling_plan([g_up, g_out])))
    p_up, p_out = pair(["up", "out"], [g_up, g_out], [r_up, r_out])
    g_oa = _wgrad("wgrad_out_a", ya, dpa, 1, False, False).reshape(N_SLOT, D // N_SLOT, D)
    g_ob = _wgrad("wgrad_out_b", yb, dpb, 1, False, False).reshape(N_SLOT, D // N_SLOT, D)
    (dz, d_ws, d_bs, d_ln), (got_up, r_oa, r_ob) = _branch_b_bwd(
        dz, dyb, z, sgu_ln_g, sgu_ln_b, sgu_w_s[0], b_s_t,
        comm=_join(_chips_plan([p_up]), _sibling_plan([g_oa, g_ob])))
    p_oa, p_ob = pair(["out_a", "out_b"], [g_oa, g_ob], [r_oa, r_ob])
    (dz, d_vec, d_wr, d_wi), (got_out, got_oa, got_ob) = _branch_a_bwd(
        dz, dya, z, hs, xc, r_gate, i_gate, conv_w_f, w_r_f, w_i_f, lru_lambda,
        comm=_chips_plan([p_out, p_oa, p_ob]))
    g_in = _wgrad_t("wgrad_in", n1_t, dz, N_SLOT, False, True, 4096)
    g_gate = jnp.stack([d_wr, d_wi]).reshape(2, HEADS, N_SLOT, 32, HEAD_DIM).transpose(2, 0, 1, 3, 4)
    g_gate = g_gate.reshape(N_SLOT, 2 * HEADS * 32, HEAD_DIM).astype(BF16)

    d_bs_row = jnp.pad(d_bs[:, :, 0].reshape(1, GROUPS * CHUNK), ((0, 0), (0, D - GROUPS * CHUNK)))
    vecs = jnp.concatenate([d_vec, jnp.concatenate([d_ln[0:2], d_gmlp, d_gfin, d_bs_row, jnp.zeros((3, D), F32)])])
    d_ws2 = d_ws.reshape(GROUPS * CHUNK, CHUNK)
    r_in, r_gate, r_vecs, r_ws = _run_plan("rs_sibling_in", _sibling_plan([g_in, g_gate], [vecs, d_ws2]))
    p_in, p_gate = pair(["in", "gate"], [g_in, g_gate], [r_in, r_gate])
    vecs_chip = _add2("pair_sum_vecs", vecs, r_vecs)
    ws_chip = _add2("pair_sum_ws", d_ws2, r_ws)
    (dx, d_gmix), (got_in, got_gate, got_vecs, got_ws) = _in_bwd(
        dz, w_in_g, xs, dh1, norm_mix_g, comm=_chips_plan([p_in, p_gate], [vecs_chip, ws_chip]))
    vecs_sum = _sum_terms("sum_vecs", got_vecs)
    last = jnp.concatenate([d_gmix, jnp.pad(loss_acc[0:1], ((0, 0), (0, D - 128))), jnp.zeros((6, D), F32)])
    (last_all,) = _run_plan("exchange_last", _exchange_plan(last))
    last_sum = _sum_terms("sum_last", last_all)
    loss = last_sum[1, 0]
    got = [got_in, got_oa, got_ob, got_out, got_up, got_down, got_gate]

    def step(nm, terms, w, m, v, rows, cols):
        g, d, mn, vn = _adamw("adamw_" + nm, terms.reshape(4, rows, cols), w.reshape(rows, cols),
                              m.reshape(rows, cols), v.reshape(rows, cols))
        return [a.reshape(w.shape) for a in (g, d, mn, vn)]

    o_in = step("in", got[0], w_in, m_w_in, v_w_in, D, W_IN_COLS)
    o_oa = step("out_a", got[1], w_out_a, m_w_out_a, v_w_out_a, D // N_SLOT, D)
    o_ob = step("out_b", got[2], w_out_b, m_w_out_b, v_w_out_b, D // N_SLOT, D)
    o_out = step("out", got[3], w_out, m_w_out, v_w_out, D // N_SLOT, D)
    o_up = step("up", got[4], w_up, m_w_up, v_w_up, D, FF_COLS)
    o_down = step("down", got[5], w_down, m_w_down, v_w_down, FF_COLS, D)
    gate_w = jnp.stack([w_rgate[0], w_igate[0]]).reshape(2 * HEADS * 32, HEAD_DIM)
    gate_m = jnp.stack([m_w_rgate[0], m_w_igate[0]]).reshape(2 * HEADS * 32, HEAD_DIM)
    gate_v = jnp.stack([v_w_rgate[0], v_w_igate[0]]).reshape(2 * HEADS * 32, HEAD_DIM)
    o_gate = _adamw("adamw_gate", got[6], gate_w, gate_m, gate_v)
    o_gate = [a.reshape(2, 1, HEADS, 32, HEAD_DIM) for a in o_gate]
    o_wr = [a[0] for a in o_gate]
    o_wi = [a[1] for a in o_gate]

    def own(full, width):
        return lax.dynamic_slice_in_dim(full, me * width, width, axis=1)

    small_g = {
        "norm_mix_g": last_sum[0:1], "conv_w": own(vecs_sum[0:4], 128), "conv_b": vecs_sum[4:5],
        "b_rgate": own(vecs_sum[5:6].reshape(HEADS, HEAD_DIM), 32),
        "b_igate": own(vecs_sum[6:7].reshape(HEADS, HEAD_DIM), 32),
        "lru_lambda": vecs_sum[7:8], "sgu_ln_g": vecs_sum[8:9], "sgu_ln_b": vecs_sum[9:10],
        "norm_mlp_g": vecs_sum[10:11], "norm_final_g": vecs_sum[11:12],
        "sgu_b_s": vecs_sum[12, 0:GROUPS * CHUNK].reshape(GROUPS, CHUNK),
    }
    small_w = {"norm_mix_g": (norm_mix_g, m_norm_mix_g, v_norm_mix_g), "conv_w": (conv_w, m_conv_w, v_conv_w),
               "conv_b": (conv_b, m_conv_b, v_conv_b), "b_rgate": (b_rgate, m_b_rgate, v_b_rgate),
               "b_igate": (b_igate, m_b_igate, v_b_igate), "lru_lambda": (lru_lambda, m_lru_lambda, v_lru_lambda),
               "sgu_ln_g": (sgu_ln_g, m_sgu_ln_g, v_sgu_ln_g), "sgu_ln_b": (sgu_ln_b, m_sgu_ln_b, v_sgu_ln_b),
               "norm_mlp_g": (norm_mlp_g, m_norm_mlp_g, v_norm_mlp_g),
               "norm_final_g": (norm_final_g, m_norm_final_g, v_norm_final_g),
               "sgu_b_s": (sgu_b_s, m_sgu_b_s, v_sgu_b_s), "sgu_w_s": (sgu_w_s, m_sgu_w_s, v_sgu_w_s)}
    order = list(small_g)
    as2d = lambda k, a: a.reshape(small_g[k].shape)
    upd = _adamw_many("adamw_small", [small_g[k] for k in order], *[[as2d(k, small_w[k][q]) for k in order]
                                                                     for q in range(3)])
    o_small = {k: [a.reshape(small_w[k][0].shape) for a in (small_g[k],) + u] for k, u in zip(order, upd)}
    ws3 = [a[0].reshape(GROUPS * CHUNK, CHUNK) for a in small_w.pop("sgu_w_s")]
    o_small["sgu_w_s"] = [a.reshape(sgu_w_s.shape) for a in _adamw("adamw_ws", got_ws, *ws3)]

    per_weight = {"norm_mix_g": o_small["norm_mix_g"], "w_in": o_in, "conv_w": o_small["conv_w"],
                  "conv_b": o_small["conv_b"], "w_rgate": o_wr, "b_rgate": o_small["b_rgate"], "w_igate": o_wi,
                  "b_igate": o_small["b_igate"], "lru_lambda": o_small["lru_lambda"], "w_out_a": o_oa,
                  "sgu_ln_g": o_small["sgu_ln_g"], "sgu_ln_b": o_small["sgu_ln_b"], "sgu_w_s": o_small["sgu_w_s"],
                  "sgu_b_s": o_small["sgu_b_s"], "w_out_b": o_ob, "w_out": o_out, "norm_mlp_g": o_small["norm_mlp_g"],
                  "w_up": o_up, "w_down": o_down, "norm_final_g": o_small["norm_final_g"]}
    names_w = list(per_weight)
    return (loss, dx[None], *[per_weight[k][0] for k in names_w], *[per_weight[k][1] for k in names_w],
            *[per_weight[k][2] for k in names_w], *[per_weight[k][3] for k in names_w])
```

```python
import jax
import jax.numpy as jnp
from jax import lax
from jax.experimental import pallas as pl
from jax.experimental.pallas import tpu as pltpu

F32 = jnp.float32
BF16 = jnp.bfloat16
SDS = jax.ShapeDtypeStruct
MESH = pl.DeviceIdType.MESH
ANY = pl.BlockSpec(memory_space=pltpu.HBM)

D = 1024
N_SLOT = 8
W_IN_COLS = 768
FF_COLS = 512
HEADS, HEAD_DIM = 4, 256
GROUPS, GROUP_DIM = 4, 256
CHUNK = 128
CONV_K = 4
NORM_EPS = 1e-6
LN_EPS = 1e-5
LRU_C = 8.0
ADAM_LR, ADAM_B1, ADAM_B2, ADAM_EPS, ADAM_WD, ADAM_STEP = 0.001, 0.9, 0.999, 1e-08, 0.01, 10

TM_ROWS = 1024
TM_MERGE = 512
T_BRANCH_A = 512
T_BRANCH_B = 256
MiB = 1024 * 1024
SMALL_OPERAND = 16 * 1024

_GELU_C = 0.7978845608028654
_GELU_A = 0.044715


def _small_in_hbm(a):
    return pltpu.with_memory_space_constraint(a, pltpu.HBM) if a.size <= SMALL_OPERAND else a


def _cparams(sem, vmem_mib):
    return pltpu.CompilerParams(dimension_semantics=sem, vmem_limit_bytes=vmem_mib * MiB)


def _gelu(x):
    t = jnp.tanh(_GELU_C * (x + _GELU_A * x * x * x))
    return 0.5 * x * (1.0 + t)


def _gelu_and_grad(x):
    x2 = x * x
    t = jnp.tanh(_GELU_C * x * (1.0 + _GELU_A * x2))
    g = 0.5 * x * (1.0 + t)
    dg = 0.5 * (1.0 + t) + 0.5 * x * (1.0 - t * t) * _GELU_C * (1.0 + 3.0 * _GELU_A * x2)
    return g, dg


def _softplus(x):
    return jnp.maximum(x, 0.0) + jnp.log1p(jnp.exp(-jnp.abs(x)))


def _dot(a, b):
    return jnp.dot(a, b, preferred_element_type=F32)


def _dot_nt(a, b):
    return lax.dot_general(a, b, (((1,), (1,)), ((), ())), preferred_element_type=F32)


def _dot_tn(a, b):
    return lax.dot_general(a, b, (((0,), (0,)), ((), ())), preferred_element_type=F32)


def _rows_shifted(prev8, cur, k):
    ext = jnp.concatenate([prev8, cur], axis=0)
    return pltpu.roll(ext, k, 0)[8:]


def _rows_advanced(cur, next8, k):
    t = cur.shape[0]
    ext = jnp.concatenate([cur, next8], axis=0)
    return pltpu.roll(ext, t + 8 - k, 0)[:t]


def _first_second(x, y, c):
    ny, nx, far = _other_chips(x, y)
    pick = lambda a, b: a * (1 - c) + b * c
    first = tuple(pick(a, b) for a, b in zip(ny, nx))
    second = tuple(pick(b, a) for a, b in zip(ny, nx))
    return first, second, far


def _slot_order(x, y, c):
    chip = 2 * x + y
    first, second, far = _first_second(x, y, c)
    order = [2 * chip + c, 2 * chip + 1 - c, 2 * first[2] + c, 2 * second[2] + 1 - c, 2 * second[2] + c,
             2 * first[2] + 1 - c, 2 * far[2] + c, 2 * far[2] + 1 - c]
    return jnp.stack(order).astype(jnp.int32)


def _in_proj(x, g_mix, w_in_own, order, comm=None):
    s = x.shape[0]
    tm = min(TM_ROWS, s)
    ni = s // tm

    def body(order_ref, x_ref, g_ref, own_ref, z_ref, nt_ref, wg_ref, n_s, w_s, send_sems, recv_sems, local_sems):
        j, i = pl.program_id(0), pl.program_id(1)
        px, py, c = _place()
        chip = 2 * px + py
        me = 2 * chip + c
        sib = (px, py, 1 - c)
        chips = _other_chips(px, py)

        def rc(k, src, blk, to):
            return pltpu.make_async_remote_copy(src_ref=src, dst_ref=w_s.at[blk], send_sem=send_sems.at[k],
                                                recv_sem=recv_sems.at[k], device_id=to, device_id_type=MESH)

        del chips
        first, second, far = _first_second(px, py, c)
        blocks = [2 * first[2] + c, 2 * second[2] + c, 2 * far[2] + c]
        own_in = pltpu.make_async_copy(own_ref, w_s.at[me], local_sems.at[0])
        to_first = rc(1, own_ref, me, (first[0], first[1], c))
        to_second = rc(2, own_ref, me, (second[0], second[1], c))
        relay = rc(3, w_s.at[blocks[0]], blocks[0], (second[0], second[1], c))
        sends = [rc(0, own_ref, me, sib), to_first, to_second, relay]
        passed = [rc(4 + q, w_s.at[blk], blk, sib) for q, blk in enumerate(blocks)]
        keep = pltpu.make_async_copy(w_s, wg_ref, local_sems.at[1])

        @pl.when((i == 0) & (j == 0))
        def _():
            own_in.start()
            sends[0].start()
            to_first.start()
            own_in.wait()

        @pl.when((i == 0) & (j == 1))
        def _():
            rc(0, own_ref, 2 * chip + 1 - c, sib).wait_recv()

        for q, blk in enumerate(blocks):
            @pl.when((i == 0) & (j == 2 + 2 * q))
            def _():
                rc(1 + q, own_ref, blk, sib).wait_recv()
                passed[q].start()
                if q == 0:
                    to_second.start()
                    relay.start()

            @pl.when((i == 0) & (j == 3 + 2 * q))
            def _():
                rc(4 + q, own_ref, order_ref[j], sib).wait_recv()

        rows = pl.ds(pl.multiple_of(i * tm, tm), tm)

        @pl.when(j == 0)
        def _():
            xv = x_ref[...]
            rstd = lax.rsqrt(jnp.mean(xv * xv, axis=-1, keepdims=True) + NORM_EPS)
            nb = (xv * rstd * g_ref[...]).astype(BF16)
            n_s[rows, :] = nb
            nt_ref[...] = nb.T

        z_ref[...] = _dot(n_s[rows, :], w_s[order_ref[j]]).astype(BF16)

        @pl.when((i == 0) & (j == N_SLOT - 1))
        def _():
            keep.start()

        @pl.when((i == ni - 1) & (j == N_SLOT - 1))
        def _():
            for cp in sends + passed:
                cp.wait_send()
            keep.wait()

    first_pass = lambda j, i, o: (jnp.where(j == 0, i, ni - 1), 0)
    (z, n1, w_in_g), extra = _call(
        body, name="in_proj", grid=(N_SLOT, ni), prefetch=(order,),
        in_specs=[pl.BlockSpec((tm, D), first_pass),
                  pl.BlockSpec((1, D), lambda j, i, o: (0, 0)), ANY],
        out_specs=[pl.BlockSpec((tm, W_IN_COLS), lambda j, i, o: (i, o[j])),
                   pl.BlockSpec((D, tm), lambda j, i, o: (0, jnp.where(j == 0, i, ni - 1))), ANY],
        out_shape=[SDS((s, N_SLOT * W_IN_COLS), BF16), SDS((D, s), BF16), SDS((N_SLOT, D, W_IN_COLS), BF16)],
        scratch_shapes=[pltpu.VMEM((s, D), BF16), pltpu.VMEM((N_SLOT, D, W_IN_COLS), BF16),
                        pltpu.SemaphoreType.DMA((7,)), pltpu.SemaphoreType.DMA((7,)), pltpu.SemaphoreType.DMA((2,))],
        params=_cparams(("arbitrary", "arbitrary"), 56), args=(x, g_mix, w_in_own), comm=comm)
    return (z, n1, w_in_g), extra


def _decay(r, sp_lam):
    log_a = (-LRU_C) * r * sp_lam
    a = jnp.exp(log_a)
    return a, jnp.sqrt(-jnp.tanh(log_a) * (a * a + 1.0))


def _lru_gates(xc, xcb, wr_ref, br, wi_ref, bi, sp_lam, a_s, b_s, r_ref, i_ref):
    for h in range(HEADS):
        sl = slice(h * HEAD_DIM, (h + 1) * HEAD_DIM)
        r = jax.nn.sigmoid(_dot(xcb[:, sl], wr_ref[h]) + br[:, sl])
        ig = jax.nn.sigmoid(_dot(xcb[:, sl], wi_ref[h]) + bi[:, sl])
        a, mult = _decay(r, sp_lam[:, sl])
        a_s[:, sl] = a
        b_s[:, sl] = xc[:, sl] * ig * mult
        r_ref[:, sl] = r.astype(BF16)
        i_ref[:, sl] = ig.astype(BF16)


def _conv_fwd(xa, prev8, cw, cb):
    xc = cb + cw[0:1, :] * xa
    for k in range(1, CONV_K):
        xc = xc + cw[k:k + 1, :] * _rows_shifted(prev8, xa, k)
    return xc


def _branch_a_fwd(z, conv_w, conv_b, w_r, b_r, w_i, b_i, lam, comm=None):
    s = z.shape[0]
    ta = min(T_BRANCH_A, s)
    per16 = ta // 16

    def body(xa_ref, xp_ref, ga_ref, cw_ref, cb_ref, wr_ref, br_ref, wi_ref, bi_ref, lam_ref,
             ya_ref, hs_ref, xc_ref, r_ref, i_ref, a_s, b_s, h_s, carry_s):
        i = pl.program_id(0)

        @pl.when(i == 0)
        def _():
            carry_s[...] = jnp.zeros_like(carry_s)

        xa = xa_ref[...].astype(F32)
        prev8 = jnp.where(i > 0, xp_ref[...].astype(F32)[8:16], 0.0)
        xc = _conv_fwd(xa, prev8, cw_ref[...], cb_ref[...])
        xcb = xc.astype(BF16)
        xc_ref[...] = xcb
        sp_lam = _softplus(-lam_ref[...])
        _lru_gates(xc, xcb, wr_ref, br_ref[...], wi_ref, bi_ref[...], sp_lam, a_s, b_s, r_ref, i_ref)

        row = lax.broadcasted_iota(jnp.int32, (8, D), 0)

        def group(g, carry):
            off = pl.multiple_of(g * 8, 8)
            a8 = a_s[pl.ds(off, 8), :]
            b8 = b_s[pl.ds(off, 8), :]
            for d in (1, 2, 4):
                a_sh = jnp.where(row >= d, pltpu.roll(a8, d, 0), 1.0)
                b_sh = jnp.where(row >= d, pltpu.roll(b8, d, 0), 0.0)
                b8 = a8 * b_sh + b8
                a8 = a8 * a_sh
            h8 = b8 + a8 * carry
            h_s[pl.ds(off, 8), :] = h8
            return jnp.broadcast_to(h8[7:8, :], (8, D))

        carry_s[...] = lax.fori_loop(0, ta // 8, group, carry_s[...])
        hs = h_s[...]
        hs_ref[...] = hs.astype(BF16)
        ya_ref[...] = (hs * _gelu(ga_ref[...].astype(F32))).astype(BF16)

    vec = pl.BlockSpec((1, D), lambda i: (0, 0))
    gate = pl.BlockSpec((HEADS, HEAD_DIM, HEAD_DIM), lambda i: (0, 0, 0))
    return _call(
        body, name="branch_a_fwd", grid=(s // ta,),
        in_specs=[pl.BlockSpec((ta, D), lambda i: (i, 0)),
                  pl.BlockSpec((16, D), lambda i: (jnp.maximum(i * per16 - 1, 0), 0)),
                  pl.BlockSpec((ta, D), lambda i: (i, 1)),
                  pl.BlockSpec((CONV_K, D), lambda i: (0, 0)), vec, gate, vec, gate, vec, vec],
        out_specs=[pl.BlockSpec((ta, D), lambda i: (i, 0))] * 5,
        out_shape=[SDS((s, D), BF16)] * 5,
        scratch_shapes=[pltpu.VMEM((ta, D), F32), pltpu.VMEM((ta, D), F32), pltpu.VMEM((ta, D), F32),
                        pltpu.VMEM((8, D), F32)],
        params=_cparams(("arbitrary",), 40), args=(z, z, z, conv_w, conv_b, w_r, b_r, w_i, b_i, lam), comm=comm)


def _sgu_common(ub, vb, lg, lb, with_grad):
    if with_grad:
        u, du = _gelu_and_grad(ub)
        v, dv = _gelu_and_grad(vb)
    else:
        u, v, du, dv = _gelu(ub), _gelu(vb), None, None
    mu = jnp.mean(v, axis=-1, keepdims=True)
    vc = v - mu
    rstd = lax.rsqrt(jnp.mean(vc * vc, axis=-1, keepdims=True) + LN_EPS)
    vhat = vc * rstd
    vln = vhat * lg + lb
    return u, du, dv, rstd, vhat, vln


def _masked_ws(ws_ref):
    t = lax.broadcasted_iota(jnp.int32, (CHUNK, CHUNK), 0)
    c = lax.broadcasted_iota(jnp.int32, (CHUNK, CHUNK), 1)
    keep = c <= t
    return [jnp.where(keep, ws_ref[g], 0.0).astype(BF16) for g in range(GROUPS)]


def _branch_b_fwd(z, ln_g, ln_b, w_s, b_s_t, comm=None):
    s = z.shape[0]
    tb = min(T_BRANCH_B, s)

    def body(ub_ref, vb_ref, lg_ref, lb_ref, ws_ref, bs_ref, yb_ref):
        u, _, _, _, _, vln = _sgu_common(ub_ref[...].astype(F32), vb_ref[...].astype(F32),
                                         lg_ref[...], lb_ref[...], False)
        vlnb = vln.astype(BF16)
        wm = _masked_ws(ws_ref)
        bs = bs_ref[...]
        for c in range(tb // CHUNK):
            rs = slice(c * CHUNK, (c + 1) * CHUNK)
            for g in range(GROUPS):
                cs = slice(g * GROUP_DIM, (g + 1) * GROUP_DIM)
                sp = _dot(wm[g], vlnb[rs, cs]) + bs[:, g:g + 1]
                yb_ref[rs, cs] = (u[rs, cs] * sp).astype(BF16)

    vec = pl.BlockSpec((1, D), lambda i: (0, 0))
    return _call(
        body, name="branch_b_fwd", grid=(s // tb,),
        in_specs=[pl.BlockSpec((tb, D), lambda i: (i, 2)), pl.BlockSpec((tb, D), lambda i: (i, 3)), vec, vec,
                  pl.BlockSpec((GROUPS, CHUNK, CHUNK), lambda i: (0, 0, 0)),
                  pl.BlockSpec((CHUNK, GROUPS), lambda i: (0, 0))],
        out_specs=[pl.BlockSpec((tb, D), lambda i: (i, 0))],
        out_shape=[SDS((s, D), BF16)], scratch_shapes=[],
        params=_cparams(("arbitrary",), 40), args=(z, z, ln_g, ln_b, w_s, b_s_t), comm=comm)


def _merge_out(ya, yb, z, x, w_oa, w_ob, w_out, comm=None):
    s = x.shape[0]
    tm = min(TM_MERGE, s)

    def body(ya_ref, yb_ref, ma_ref, mb_ref, x_ref, woa_ref, wob_ref, wo_ref, pa_ref, pb_ref, mg_ref, h1_ref):
        pa = _dot(ya_ref[...], woa_ref[...])
        pb = _dot(yb_ref[...], wob_ref[...])
        merged = (jax.nn.sigmoid(ma_ref[...].astype(F32)) * pa
                  + jax.nn.sigmoid(mb_ref[...].astype(F32)) * pb).astype(BF16)
        pa_ref[...] = pa.astype(BF16)
        pb_ref[...] = pb.astype(BF16)
        mg_ref[...] = merged
        h1_ref[...] = x_ref[...] + _dot(merged, wo_ref[...])

    row = pl.BlockSpec((tm, D), lambda i: (i, 0))
    wsp = pl.BlockSpec((D, D), lambda i: (0, 0))
    return _call(
        body, name="merge_out", grid=(s // tm,),
        in_specs=[row, row, pl.BlockSpec((tm, D), lambda i: (i, 4)), pl.BlockSpec((tm, D), lambda i: (i, 5)),
                  row, wsp, wsp, wsp],
        out_specs=[row, row, row, row],
        out_shape=[SDS((s, D), BF16), SDS((s, D), BF16), SDS((s, D), BF16), SDS((s, D), F32)], scratch_shapes=[],
        params=_cparams(("arbitrary",), 48), args=(ya, yb, z, z, x, w_oa, w_ob, w_out), comm=comm)


def _mlp_fwd(h1, g_mlp, w_up_g, w_down, g_fin, tgt):
    s = h1.shape[0]
    tm = min(TM_ROWS, s)
    nj = N_SLOT

    def body(h1_ref, gm_ref, wu_ref, wd_ref, gf_ref, t_ref, r_ref, at_ref, n2t_ref, dh2_ref, loss_ref, dgf_ref,
             n2_s, acc_s):
        i, j = pl.program_id(0), pl.program_id(1)

        @pl.when(j == 0)
        def _():
            hv = h1_ref[...]
            rstd = lax.rsqrt(jnp.mean(hv * hv, axis=-1, keepdims=True) + NORM_EPS)
            nb = (hv * rstd * gm_ref[...]).astype(BF16)
            n2_s[...] = nb
            n2t_ref[...] = nb.T
            acc_s[...] = jnp.zeros_like(acc_s)

        @pl.when((i == 0) & (j == 0))
        def _():
            loss_ref[...] = jnp.zeros_like(loss_ref)
            dgf_ref[...] = jnp.zeros_like(dgf_ref)

        r = jnp.maximum(_dot(n2_s[...], wu_ref[...]), 0.0)
        r_ref[...] = r.astype(BF16)
        act = (r * r).astype(BF16)
        at_ref[...] = act.T
        acc_s[...] += _dot(act, wd_ref[...])

        @pl.when(j == nj - 1)
        def _():
            h2 = h1_ref[...] + acc_s[...]
            rstd = lax.rsqrt(jnp.mean(h2 * h2, axis=-1, keepdims=True) + NORM_EPS)
            hh = h2 * rstd
            gf = gf_ref[...]
            e = hh * gf - t_ref[...]
            loss_ref[...] += jnp.sum(e * e) * (0.5 / D)
            dy = e * (1.0 / D)
            dgf_ref[...] += jnp.sum(dy * hh, axis=0, keepdims=True)
            dhh = dy * gf
            dh2_ref[...] = rstd * (dhh - hh * jnp.mean(dhh * hh, axis=-1, keepdims=True))

    row = pl.BlockSpec((tm, D), lambda i, j: (i, 0))
    vec = pl.BlockSpec((1, D), lambda i, j: (0, 0))
    return pl.pallas_call(
        body, name="mlp_fwd", grid=(s // tm, nj),
        in_specs=[row, vec, pl.BlockSpec((None, D, FF_COLS), lambda i, j: (j, 0, 0)),
                  pl.BlockSpec((FF_COLS, D), lambda i, j: (j, 0)), vec, row],
        out_specs=[pl.BlockSpec((tm, FF_COLS), lambda i, j: (i, j)), pl.BlockSpec((FF_COLS, tm), lambda i, j: (j, i)),
                   pl.BlockSpec((D, tm), lambda i, j: (0, i)), row, pl.BlockSpec((8, 128), lambda i, j: (0, 0)), vec],
        out_shape=[SDS((s, nj * FF_COLS), BF16), SDS((nj * FF_COLS, s), BF16), SDS((D, s), BF16), SDS((s, D), F32),
                   SDS((8, 128), F32), SDS((1, D), F32)],
        scratch_shapes=[pltpu.VMEM((tm, D), BF16), pltpu.VMEM((tm, D), F32)],
        compiler_params=_cparams(("arbitrary", "arbitrary"), 52),
    )(h1, _small_in_hbm(g_mlp), w_up_g, w_down, _small_in_hbm(g_fin), tgt)


def _mlp_bwd(dh2, r, w_down, w_up_g, h1, g_mlp, comm=None):
    s = h1.shape[0]
    tm = min(TM_ROWS, s)
    nj = N_SLOT

    def body(dh2_ref, r_ref, wd_ref, wu_ref, h1_ref, gm_ref, df_ref, dh1_ref, dgm_ref, dh2b_s, acc_s):
        i, j = pl.program_id(0), pl.program_id(1)

        @pl.when(j == 0)
        def _():
            dh2b_s[...] = dh2_ref[...].astype(BF16)
            acc_s[...] = jnp.zeros_like(acc_s)

        @pl.when((i == 0) & (j == 0))
        def _():
            dgm_ref[...] = jnp.zeros_like(dgm_ref)

        d_act = _dot_nt(dh2b_s[...], wd_ref[...])
        df = (d_act * (2.0 * r_ref[...].astype(F32))).astype(BF16)
        df_ref[...] = df
        acc_s[...] += _dot_nt(df, wu_ref[...])

        @pl.when(j == nj - 1)
        def _():
            hv = h1_ref[...]
            rstd = lax.rsqrt(jnp.mean(hv * hv, axis=-1, keepdims=True) + NORM_EPS)
            hh = hv * rstd
            dn2 = acc_s[...]
            dgm_ref[...] += jnp.sum(dn2 * hh, axis=0, keepdims=True)
            dhat = dn2 * gm_ref[...]
            dh1_ref[...] = dh2_ref[...] + rstd * (dhat - hh * jnp.mean(dhat * hh, axis=-1, keepdims=True))

    row = pl.BlockSpec((tm, D), lambda i, j: (i, 0))
    vec = pl.BlockSpec((1, D), lambda i, j: (0, 0))
    ffb = pl.BlockSpec((tm, FF_COLS), lambda i, j: (i, j))
    return _call(
        body, name="mlp_bwd", grid=(s // tm, nj),
        in_specs=[row, ffb, pl.BlockSpec((FF_COLS, D), lambda i, j: (j, 0)),
                  pl.BlockSpec((None, D, FF_COLS), lambda i, j: (j, 0, 0)), row, vec],
        out_specs=[ffb, row, vec],
        out_shape=[SDS((s, nj * FF_COLS), BF16), SDS((s, D), F32), SDS((1, D), F32)],
        scratch_shapes=[pltpu.VMEM((tm, D), BF16), pltpu.VMEM((tm, D), F32)],
        params=_cparams(("arbitrary", "arbitrary"), 52), args=(dh2, r, w_down, w_up_g, h1, g_mlp), comm=comm)


def _merge_bwd(dh1, z, pa, pb, w_out, w_oa, w_ob, comm=None):
    s = dh1.shape[0]
    tm = min(TM_MERGE, s)

    def body(dh1_ref, ma_ref, mb_ref, pa_ref, pb_ref, wo_ref, woa_ref, wob_ref,
             dz_ref, dpa_ref, dpb_ref, dya_ref, dyb_ref):
        dm = _dot_nt(dh1_ref[...].astype(BF16), wo_ref[...])
        sa = jax.nn.sigmoid(ma_ref[...].astype(F32))
        sb = jax.nn.sigmoid(mb_ref[...].astype(F32))
        dpa = (dm * sa).astype(BF16)
        dpb = (dm * sb).astype(BF16)
        dz_ref[:, 0:D] = (dm * pa_ref[...].astype(F32) * sa * (1.0 - sa)).astype(BF16)
        dz_ref[:, D:2 * D] = (dm * pb_ref[...].astype(F32) * sb * (1.0 - sb)).astype(BF16)
        dpa_ref[...] = dpa
        dpb_ref[...] = dpb
        dya_ref[...] = _dot_nt(dpa, woa_ref[...]).astype(BF16)
        dyb_ref[...] = _dot_nt(dpb, wob_ref[...]).astype(BF16)

    row = pl.BlockSpec((tm, D), lambda i: (i, 0))
    wsp = pl.BlockSpec((D, D), lambda i: (0, 0))
    return _call(
        body, name="merge_bwd", grid=(s // tm,),
        in_specs=[row, pl.BlockSpec((tm, D), lambda i: (i, 4)), pl.BlockSpec((tm, D), lambda i: (i, 5)),
                  row, row, wsp, wsp, wsp],
        out_specs=[pl.BlockSpec((tm, 2 * D), lambda i: (i, 2)), row, row, row, row],
        out_shape=[SDS((s, 6 * D), BF16)] + [SDS((s, D), BF16)] * 4, scratch_shapes=[],
        params=_cparams(("arbitrary",), 48), args=(dh1, z, z, pa, pb, w_out, w_oa, w_ob), comm=comm)


def _branch_b_bwd(dz, dyb, z, ln_g, ln_b, w_s, b_s_t, comm=None):
    s = z.shape[0]
    tb = min(T_BRANCH_B, s)

    def body(dz_in, dyb_ref, ub_ref, vb_ref, lg_ref, lb_ref, ws_ref, bs_ref,
             dz_ref, dws_ref, dbs_ref, dln_ref, du_s, dvln_s):
        del dz_in

        @pl.when(pl.program_id(0) == 0)
        def _():
            dws_ref[...] = jnp.zeros_like(dws_ref)
            dbs_ref[...] = jnp.zeros_like(dbs_ref)
            dln_ref[...] = jnp.zeros_like(dln_ref)

        lg = lg_ref[...]
        u, du, dv, rstd, vhat, vln = _sgu_common(ub_ref[...].astype(F32), vb_ref[...].astype(F32),
                                                 lg, lb_ref[...], True)
        vlnb = vln.astype(BF16)
        dyb_v = dyb_ref[...].astype(F32)
        wm = _masked_ws(ws_ref)
        keep = (lax.broadcasted_iota(jnp.int32, (CHUNK, CHUNK), 1)
                <= lax.broadcasted_iota(jnp.int32, (CHUNK, CHUNK), 0))
        bs = bs_ref[...]
        for c in range(tb // CHUNK):
            rs = slice(c * CHUNK, (c + 1) * CHUNK)
            for g in range(GROUPS):
                cs = slice(g * GROUP_DIM, (g + 1) * GROUP_DIM)
                v_blk = vlnb[rs, cs]
                sp = _dot(wm[g], v_blk) + bs[:, g:g + 1]
                d_sp = dyb_v[rs, cs] * u[rs, cs]
                d_spb = d_sp.astype(BF16)
                du_s[rs, cs] = dyb_v[rs, cs] * sp
                dvln_s[rs, cs] = _dot_tn(wm[g], d_spb)
                dws_ref[g] += jnp.where(keep, _dot_nt(d_spb, v_blk), 0.0)
                dbs_ref[g] += jnp.broadcast_to(jnp.sum(d_sp, axis=-1, keepdims=True), (CHUNK, CHUNK))
        dvln = dvln_s[...]
        dln_ref[0:1, :] += jnp.sum(dvln * vhat, axis=0, keepdims=True)
        dln_ref[1:2, :] += jnp.sum(dvln, axis=0, keepdims=True)
        dvh = dvln * lg
        d_v = rstd * (dvh - jnp.mean(dvh, axis=-1, keepdims=True)
                      - vhat * jnp.mean(dvh * vhat, axis=-1, keepdims=True))
        dz_ref[:, 0:D] = (du_s[...] * du).astype(BF16)
        dz_ref[:, D:2 * D] = (d_v * dv).astype(BF16)

    vec = pl.BlockSpec((1, D), lambda i: (0, 0))
    sq = pl.BlockSpec((GROUPS, CHUNK, CHUNK), lambda i: (0, 0, 0))
    return _call(
        body, name="branch_b_bwd", grid=(s // tb,),
        in_specs=[ANY, pl.BlockSpec((tb, D), lambda i: (i, 0)),
                  pl.BlockSpec((tb, D), lambda i: (i, 2)), pl.BlockSpec((tb, D), lambda i: (i, 3)), vec, vec, sq,
                  pl.BlockSpec((CHUNK, GROUPS), lambda i: (0, 0))],
        out_specs=[pl.BlockSpec((tb, 2 * D), lambda i: (i, 1)), sq, sq, pl.BlockSpec((8, D), lambda i: (0, 0))],
        out_shape=[SDS(dz.shape, BF16), SDS((GROUPS, CHUNK, CHUNK), F32), SDS((GROUPS, CHUNK, CHUNK), F32),
                   SDS((8, D), F32)],
        scratch_shapes=[pltpu.VMEM((tb, D), F32), pltpu.VMEM((tb, D), F32)], aliases={0: 0},
        params=_cparams(("arbitrary",), 40), args=(dz, dyb, z, z, ln_g, ln_b, w_s, b_s_t), comm=comm)


def _branch_a_bwd(dz, dya, z, hs, xc, r, ig, conv_w, w_r, w_i, lam, comm=None):
    s = z.shape[0]
    ta = min(T_BRANCH_A, s)
    nb = s // ta
    per16 = ta // 16

    def body(dz_in, dya_ref, xa_ref, ga_ref, hs_ref, hp_ref, xc_ref, r_ref, i_ref, cw_ref, wr_ref, wi_ref,
             lam_ref, dz_ref, vec_ref, dwr_ref, dwi_ref, a_s, b_s, h_s, dcar_s, acar_s, dxc_s):
        del dz_in
        i = pl.program_id(0)
        blk = nb - 1 - i

        @pl.when(i == 0)
        def _():
            dcar_s[...] = jnp.zeros_like(dcar_s)
            acar_s[...] = jnp.zeros_like(acar_s)
            dxc_s[...] = jnp.zeros_like(dxc_s)
            vec_ref[...] = jnp.zeros_like(vec_ref)
            dwr_ref[...] = jnp.zeros_like(dwr_ref)
            dwi_ref[...] = jnp.zeros_like(dwi_ref)

        cw = cw_ref[...]
        lam_v = lam_ref[...]
        xa = xa_ref[...].astype(F32)
        xcb = xc_ref[...]
        xc = xcb.astype(F32)
        sp_lam = _softplus(-lam_v)
        r_v, i_v = r_ref[...].astype(F32), i_ref[...].astype(F32)
        a_v, m_v = _decay(r_v, sp_lam)

        hs_v = hs_ref[...].astype(F32)
        hprev8 = jnp.where(blk > 0, hp_ref[...].astype(F32)[8:16], 0.0)
        h_m1 = _rows_shifted(hprev8, hs_v, 1)
        gg, dgg = _gelu_and_grad(ga_ref[...].astype(F32))
        dya_v = dya_ref[...].astype(F32)
        dz_ref[:, D:2 * D] = (dya_v * hs_v * dgg).astype(BF16)

        a_s[...] = _rows_advanced(a_v, acar_s[...], 1)
        b_s[...] = dya_v * gg

        row = lax.broadcasted_iota(jnp.int32, (8, D), 0)
        ng = ta // 8

        def group(gi, carry):
            off = pl.multiple_of((ng - 1 - gi) * 8, 8)
            c8 = a_s[pl.ds(off, 8), :]
            d8 = b_s[pl.ds(off, 8), :]
            for d in (1, 2, 4):
                c_sh = jnp.where(row < 8 - d, pltpu.roll(c8, 8 - d, 0), 1.0)
                d_sh = jnp.where(row < 8 - d, pltpu.roll(d8, 8 - d, 0), 0.0)
                d8 = c8 * d_sh + d8
                c8 = c8 * c_sh
            dh8 = d8 + c8 * carry
            h_s[pl.ds(off, 8), :] = dh8
            return jnp.broadcast_to(dh8[0:1, :], (8, D))

        dcar_s[...] = lax.fori_loop(0, ng, group, dcar_s[...])
        acar_s[...] = jnp.broadcast_to(a_v[0:1, :], (8, D))

        dbx = h_s[...]
        d_mult = dbx * xc * i_v
        d_loga = dbx * h_m1 * a_v - d_mult * (a_v * a_v) / m_v
        d_pr = d_loga * ((-LRU_C) * sp_lam) * r_v * (1.0 - r_v)
        d_pi = dbx * xc * m_v * i_v * (1.0 - i_v)
        vec_ref[7:8, :] += jnp.sum(d_loga * r_v, axis=0, keepdims=True) * (LRU_C * jax.nn.sigmoid(-lam_v))
        vec_ref[5:6, :] += jnp.sum(d_pr, axis=0, keepdims=True)
        vec_ref[6:7, :] += jnp.sum(d_pi, axis=0, keepdims=True)
        d_prb = d_pr.astype(BF16)
        d_pib = d_pi.astype(BF16)
        h_s[...] = dbx * i_v * m_v
        for h in range(HEADS):
            sl = slice(h * HEAD_DIM, (h + 1) * HEAD_DIM)
            h_s[:, sl] += _dot_nt(d_prb[:, sl], wr_ref[h]) + _dot_nt(d_pib[:, sl], wi_ref[h])
            dwr_ref[h] += _dot_tn(xcb[:, sl], d_prb[:, sl])
            dwi_ref[h] += _dot_tn(xcb[:, sl], d_pib[:, sl])
        d_xc = h_s[...]
        vec_ref[4:5, :] += jnp.sum(d_xc, axis=0, keepdims=True)
        vec_ref[0:1, :] += jnp.sum(d_xc * xa, axis=0, keepdims=True)
        d_xa = cw[0:1, :] * d_xc
        nxt = dxc_s[...]
        for k in range(1, CONV_K):
            ahead = _rows_advanced(d_xc, nxt, k)
            vec_ref[k:k + 1, :] += jnp.sum(ahead * xa, axis=0, keepdims=True)
            d_xa = d_xa + cw[k:k + 1, :] * ahead
        dz_ref[:, 0:D] = d_xa.astype(BF16)
        dxc_s[...] = d_xc[0:8, :]

    vec = pl.BlockSpec((1, D), lambda i: (0, 0))
    gate = pl.BlockSpec((HEADS, HEAD_DIM, HEAD_DIM), lambda i: (0, 0, 0))
    cur = lambda c: pl.BlockSpec((ta, D), lambda i: (nb - 1 - i, c))
    before = lambda c: pl.BlockSpec((16, D), lambda i: (jnp.maximum((nb - 1 - i) * per16 - 1, 0), c))
    return _call(
        body, name="branch_a_bwd", grid=(nb,),
        in_specs=[ANY, cur(0), cur(0), cur(1), cur(0), before(0), cur(0), cur(0), cur(0),
                  pl.BlockSpec((CONV_K, D), lambda i: (0, 0)), gate, gate, vec],
        out_specs=[pl.BlockSpec((ta, 2 * D), lambda i: (nb - 1 - i, 0)), pl.BlockSpec((8, D), lambda i: (0, 0)),
                   gate, gate],
        out_shape=[SDS(dz.shape, BF16), SDS((8, D), F32), SDS((HEADS, HEAD_DIM, HEAD_DIM), F32),
                   SDS((HEADS, HEAD_DIM, HEAD_DIM), F32)],
        scratch_shapes=[pltpu.VMEM((ta, D), F32)] * 3 + [pltpu.VMEM((8, D), F32)] * 3, aliases={0: 0},
        params=_cparams(("arbitrary",), 48),
        args=(dz, dya, z, z, hs, hs, xc, r, ig, conv_w, w_r, w_i, lam), comm=comm)


def _in_bwd(dz, w_in_g, x, dh1, g_mix, comm=None):
    s = x.shape[0]
    tm = min(TM_ROWS, s)
    nj = N_SLOT

    def body(dz_ref, w_ref, x_ref, dh1_ref, g_ref, dx_ref, dg_ref, acc_s):
        i, j = pl.program_id(0), pl.program_id(1)

        @pl.when(j == 0)
        def _():
            acc_s[...] = jnp.zeros_like(acc_s)

        @pl.when((i == 0) & (j == 0))
        def _():
            dg_ref[...] = jnp.zeros_like(dg_ref)

        acc_s[...] += _dot_nt(dz_ref[...], w_ref[...])

        @pl.when(j == nj - 1)
        def _():
            xv = x_ref[...]
            rstd = lax.rsqrt(jnp.mean(xv * xv, axis=-1, keepdims=True) + NORM_EPS)
            xh = xv * rstd
            dn = acc_s[...]
            dg_ref[...] += jnp.sum(dn * xh, axis=0, keepdims=True)
            dhat = dn * g_ref[...]
            dx_ref[...] = dh1_ref[...] + rstd * (dhat - xh * jnp.mean(dhat * xh, axis=-1, keepdims=True))

    row = pl.BlockSpec((tm, D), lambda i, j: (i, 0))
    vec = pl.BlockSpec((1, D), lambda i, j: (0, 0))
    return _call(
        body, name="in_bwd", grid=(s // tm, nj),
        in_specs=[pl.BlockSpec((tm, W_IN_COLS), lambda i, j: (i, j)),
                  pl.BlockSpec((None, D, W_IN_COLS), lambda i, j: (j, 0, 0)), row, row, vec],
        out_specs=[row, vec],
        out_shape=[SDS((s, D), F32), SDS((1, D), F32)],
        scratch_shapes=[pltpu.VMEM((tm, D), F32)],
        params=_cparams(("arbitrary", "arbitrary"), 48), args=(dz, w_in_g, x, dh1, g_mix), comm=comm)


def _wgrad(name, a, b, nblk, a_split, b_split):
    s = a.shape[0]
    ts = min(TM_ROWS, s)
    a_w = a.shape[1] // nblk if a_split else a.shape[1]
    b_w = b.shape[1] // nblk if b_split else b.shape[1]

    def body(a_ref, b_ref, o_ref, acc_s):
        t = pl.program_id(1)

        @pl.when(t == 0)
        def _():
            acc_s[...] = jnp.zeros_like(acc_s)

        acc_s[...] += _dot_tn(a_ref[...].astype(BF16), b_ref[...].astype(BF16))

        @pl.when(t == pl.num_programs(1) - 1)
        def _():
            o_ref[...] = acc_s[...].astype(BF16)

    return pl.pallas_call(
        body, name=name, grid=(nblk, s // ts),
        in_specs=[pl.BlockSpec((ts, a_w), (lambda k, t: (t, k)) if a_split else (lambda k, t: (t, 0))),
                  pl.BlockSpec((ts, b_w), (lambda k, t: (t, k)) if b_split else (lambda k, t: (t, 0)))],
        out_specs=pl.BlockSpec((None, a_w, b_w), lambda k, t: (k, 0, 0)),
        out_shape=SDS((nblk, a_w, b_w), BF16),
        scratch_shapes=[pltpu.VMEM((a_w, b_w), F32)],
        compiler_params=_cparams(("arbitrary", "arbitrary"), 48),
    )(a, b)


def _wgrad_t(name, a_t, b, nblk, a_split, b_split, tokens, updates=()):
    s = b.shape[0]
    ts = min(tokens, s)
    a_w = a_t.shape[0] // nblk if a_split else a_t.shape[0]
    b_w = b.shape[1] // nblk if b_split else b.shape[1]
    nt = s // ts
    nu = len(updates)

    def body(a_ref, b_ref, *rest):
        upd_in, o_ref, upd_out, acc_s = rest[:4 * nu], rest[4 * nu], rest[4 * nu + 1:-1], rest[-1]
        t = pl.program_id(1)

        @pl.when(t == 0)
        def _():
            acc_s[...] = jnp.zeros_like(acc_s)

        acc_s[...] += _dot(a_ref[...], b_ref[...].astype(BF16))
        for u in range(nu):
            t_ref, w_ref, m_ref, v_ref = upd_in[4 * u:4 * u + 4]
            g_ref, d_ref, mo_ref, vo_ref = upd_out[4 * u:4 * u + 4]
            g = ((t_ref[0].astype(F32) + t_ref[1].astype(F32)) + t_ref[2].astype(F32)) + t_ref[3].astype(F32)
            g_ref[...] = g
            d_ref[...], mo_ref[...], vo_ref[...] = _adam_update(g, w_ref[...], m_ref[...], v_ref[...])

        @pl.when(t == nt - 1)
        def _():
            o_ref[...] = acc_s[...].astype(BF16)

    upd_specs_in, upd_specs_out, upd_shapes, upd_args = [], [], [], []
    for terms, w, m, v in updates:
        rows, cols = w.shape
        rs = rows // (nblk * nt)
        blk = pl.BlockSpec((rs, cols), lambda k, t: (k * nt + t, 0))
        upd_specs_in += [pl.BlockSpec((4, rs, cols), lambda k, t: (0, k * nt + t, 0)), blk, blk, blk]
        upd_specs_out += [blk] * 4
        upd_shapes += [SDS((rows, cols), F32)] * 4
        upd_args += [terms, w, m, v]
    res = pl.pallas_call(
        body, name=name, grid=(nblk, nt),
        in_specs=[pl.BlockSpec((a_w, ts), (lambda k, t: (k, t)) if a_split else (lambda k, t: (0, t))),
                  pl.BlockSpec((ts, b_w), (lambda k, t: (t, k)) if b_split else (lambda k, t: (t, 0)))] + upd_specs_in,
        out_specs=[pl.BlockSpec((None, a_w, b_w), lambda k, t: (k, 0, 0))] + upd_specs_out,
        out_shape=[SDS((nblk, a_w, b_w), BF16)] + upd_shapes,
        scratch_shapes=[pltpu.VMEM((a_w, b_w), F32)],
        compiler_params=_cparams(("arbitrary", "arbitrary"), 52),
    )(a_t, b, *upd_args)
    return (res[0], [list(res[1 + 4 * u:5 + 4 * u]) for u in range(nu)]) if nu else res[0]


def _place():
    x, y, c = lax.axis_index("x"), lax.axis_index("y"), lax.axis_index("c")
    return x, y, c


def _other_chips(x, y):
    return [(x, 1 - y, 2 * x + 1 - y), (1 - x, y, 2 * (1 - x) + y), (1 - x, 1 - y, 2 * (1 - x) + 1 - y)]


class _Plan:
    def __init__(self, arrays, out_shape, sems, start, finish, middle=None, middle_at=6):
        self.arrays, self.out_shape, self.sems, self.start, self.finish = arrays, out_shape, sems, start, finish
        self.middle, self.middle_at = middle, middle_at


def _gather_plan(shards, middle_at=6):
    n = len(shards)

    def copies(ins, outs, sems):
        send_sems, recv_sems, local_sems = sems
        x, y, c = _place()
        chip = 2 * x + y
        me = 2 * chip + c
        sib = (x, y, 1 - c)
        chips = _other_chips(x, y)

        def rc(k, t, src, blk, to):
            return pltpu.make_async_remote_copy(
                src_ref=src, dst_ref=outs[t].at[blk], send_sem=send_sems.at[k * n + t],
                recv_sem=recv_sems.at[k * n + t], device_id=to, device_id_type=MESH)

        (yx, yy, y_chip), (xx, xy, x_chip), _ = chips
        local = [pltpu.make_async_copy(ins[t], outs[t].at[me], local_sems.at[t]) for t in range(n)]
        sends = ([rc(0, t, ins[t], me, sib) for t in range(n)] + [rc(1, t, ins[t], me, (yx, yy, c)) for t in range(n)]
                 + [rc(2, t, ins[t], me, (xx, xy, c)) for t in range(n)])
        passed = [[rc(4 + j, t, outs[t].at[2 * pc + c], 2 * pc + c, sib) for t in range(n)]
                  for j, (_, _, pc) in enumerate(chips)]
        relays = [[rc(3, t, outs[t].at[2 * y_chip + c], 2 * y_chip + c, (xx, xy, c)) for t in range(n)],
                  [rc(3, t, outs[t].at[2 * x_chip + c], 2 * x_chip + c, (yx, yy, c)) for t in range(n)]]
        return rc, local, sends, passed, relays, chips, chip, c, sib

    def start(ins, outs, sems):
        _, local, sends, _, _, _, _, _, _ = copies(ins, outs, sems)
        for cp in local + sends:
            cp.start()

    def middle(ins, outs, sems):
        rc, _, _, passed, relays, chips, _, c, sib = copies(ins, outs, sems)
        for j in range(2):
            for t in range(n):
                rc(1 + j, t, ins[t], 2 * chips[j][2] + c, sib).wait_recv()
        for j in range(2):
            for cp in passed[j]:
                cp.start()

            @pl.when(c == j)
            def _():
                for cp in relays[j]:
                    cp.start()

    def finish(ins, outs, sems):
        rc, local, sends, passed, relays, chips, chip, c, sib = copies(ins, outs, sems)
        far = 2 * chips[2][2] + c
        for t in range(n):
            rc(3, t, ins[t], far, sib).wait_recv()
        for cp in passed[2]:
            cp.start()
        for t in range(n):
            rc(0, t, ins[t], 2 * chip + 1 - c, sib).wait_recv()
        for j, (px, py, pc) in enumerate(chips):
            for t in range(n):
                rc(4 + j, t, ins[t], 2 * pc + 1 - c, sib).wait_recv()
        for cp in sends + passed[0] + passed[1] + passed[2]:
            cp.wait_send()
        for j in range(2):
            @pl.when(c == j)
            def _():
                for cp in relays[j]:
                    cp.wait_send()
        for cp in local:
            cp.wait()

    return _Plan(list(shards), [SDS((N_SLOT,) + tuple(a.shape), a.dtype) for a in shards],
                 [pltpu.SemaphoreType.DMA((7 * n,)), pltpu.SemaphoreType.DMA((7 * n,)),
                  pltpu.SemaphoreType.DMA((n,))], start, finish, middle, middle_at)


def _sibling_plan(grads, whole=()):
    n, m = len(grads), len(whole)

    def copies(ins, outs, sems):
        send_sems, recv_sems = sems
        x, y, c = _place()
        sib = (x, y, 1 - c)

        def rc(t, src, dst):
            return pltpu.make_async_remote_copy(src_ref=src, dst_ref=dst, send_sem=send_sems.at[t],
                                                recv_sem=recv_sems.at[t], device_id=sib, device_id_type=MESH)
        return rc, c

    def start(ins, outs, sems):
        rc, c = copies(ins, outs, sems)
        for t in range(n):
            for j in range(4):
                rc(t, ins[t].at[2 * j + 1 - c], outs[t].at[j]).start()
        for t in range(n, n + m):
            rc(t, ins[t], outs[t]).start()

    def finish(ins, outs, sems):
        rc, _ = copies(ins, outs, sems)
        for t in range(n):
            rc(t, ins[t].at[pl.ds(0, 4)], outs[t]).wait()
        for t in range(n, n + m):
            rc(t, ins[t], outs[t]).wait()

    return _Plan(list(grads) + list(whole),
                 [SDS((4,) + tuple(g.shape[1:]), g.dtype) for g in grads] + [SDS(a.shape, a.dtype) for a in whole],
                 [pltpu.SemaphoreType.DMA((n + m,)), pltpu.SemaphoreType.DMA((n + m,))], start, finish)


def _chips_plan(parts, whole=()):
    n, m = len(parts), len(whole)

    def src_of(ins, t, pc):
        return ins[t].at[pc] if t < n else ins[t]

    def local_copies(ins, outs, sems, chip):
        return [pltpu.make_async_copy(src_of(ins, t, chip), outs[t].at[chip], sems[2].at[t]) for t in range(n + m)]

    def start(ins, outs, sems):
        send_sems, recv_sems, _ = sems
        x, y, c = _place()
        chip = 2 * x + y
        for cp in local_copies(ins, outs, sems, chip):
            cp.start()
        for px, py, pc in _other_chips(x, y):
            for t in range(n + m):
                pltpu.make_async_remote_copy(src_ref=src_of(ins, t, pc), dst_ref=outs[t].at[chip],
                                             send_sem=send_sems.at[t], recv_sem=recv_sems.at[t],
                                             device_id=(px, py, c), device_id_type=MESH).start()

    def finish(ins, outs, sems):
        send_sems, recv_sems, _ = sems
        x, y, c = _place()
        for t in range(n + m):
            three = outs[t].at[pl.ds(0, 3)]
            pltpu.make_async_remote_copy(src_ref=three, dst_ref=three, send_sem=send_sems.at[t],
                                         recv_sem=recv_sems.at[t], device_id=(x, y, c), device_id_type=MESH).wait()
        for cp in local_copies(ins, outs, sems, 2 * x + y):
            cp.wait()

    return _Plan(list(parts) + list(whole),
                 [SDS(p.shape, p.dtype) for p in parts] + [SDS((4,) + tuple(a.shape), a.dtype) for a in whole],
                 [pltpu.SemaphoreType.DMA((n + m,)), pltpu.SemaphoreType.DMA((n + m,)),
                  pltpu.SemaphoreType.DMA((n + m,))], start, finish)


def _exchange_plan(arr):
    def peers(x, y, c):
        flip = lambda v, f: 1 - v if f else v
        return [(flip(x, fx), flip(y, fy), flip(c, fc))
                for fx in (0, 1) for fy in (0, 1) for fc in (0, 1) if fx or fy or fc]

    def start(ins, outs, sems):
        x, y, c = _place()
        me = 4 * x + 2 * y + c
        pltpu.make_async_copy(ins[0], outs[0].at[me], sems[2].at[0]).start()
        for to in peers(x, y, c):
            pltpu.make_async_remote_copy(src_ref=ins[0], dst_ref=outs[0].at[me], send_sem=sems[0].at[0],
                                         recv_sem=sems[1].at[0], device_id=to, device_id_type=MESH).start()

    def finish(ins, outs, sems):
        x, y, c = _place()
        seven = outs[0].at[pl.ds(0, 7)]
        pltpu.make_async_remote_copy(src_ref=seven, dst_ref=seven, send_sem=sems[0].at[0], recv_sem=sems[1].at[0],
                                     device_id=(x, y, c), device_id_type=MESH).wait()
        pltpu.make_async_copy(ins[0], outs[0].at[4 * x + 2 * y + c], sems[2].at[0]).wait()

    return _Plan([arr], [SDS((N_SLOT,) + tuple(arr.shape), arr.dtype)],
                 [pltpu.SemaphoreType.DMA((1,)), pltpu.SemaphoreType.DMA((1,)), pltpu.SemaphoreType.DMA((1,))],
                 start, finish)


def _join(*plans):
    def cut(seq, sizes):
        out, at = [], 0
        for k in sizes:
            out.append(seq[at:at + k])
            at += k
        return out

    n_arr = [len(p.arrays) for p in plans]
    n_sem = [len(p.sems) for p in plans]

    def start(ins, outs, sems):
        for p, i, o, s in zip(plans, cut(ins, n_arr), cut(outs, n_arr), cut(sems, n_sem)):
            p.start(i, o, s)

    def finish(ins, outs, sems):
        for p, i, o, s in zip(plans, cut(ins, n_arr), cut(outs, n_arr), cut(sems, n_sem)):
            p.finish(i, o, s)

    def middle(ins, outs, sems):
        for p, i, o, s in zip(plans, cut(ins, n_arr), cut(outs, n_arr), cut(sems, n_sem)):
            if p.middle is not None:
                p.middle(i, o, s)

    return _Plan([a for p in plans for a in p.arrays], [o for p in plans for o in p.out_shape],
                 [s for p in plans for s in p.sems], start, finish,
                 middle if any(p.middle is not None for p in plans) else None,
                 max(p.middle_at for p in plans))


def _run_plan(name, plan):
    k = len(plan.arrays)

    def body(*refs):
        ins, outs, sems = refs[:k], refs[k:2 * k], refs[2 * k:]
        plan.start(ins, outs, sems)
        if plan.middle is not None:
            plan.middle(ins, outs, sems)
        plan.finish(ins, outs, sems)

    return pl.pallas_call(
        body, name=name, in_specs=[ANY] * k, out_specs=[ANY] * k, out_shape=plan.out_shape,
        scratch_shapes=plan.sems, compiler_params=pltpu.CompilerParams(has_side_effects=True),
    )(*plan.arrays)


def _call(body, *, name, grid, in_specs, out_specs, out_shape, scratch_shapes, params, args, comm=None,
          aliases=None, prefetch=()):
    aliases = aliases or {}
    n_pre = len(prefetch)

    def launch(fn, ins_specs, outs_specs, outs_shape, scratch, operands):
        spec = pltpu.PrefetchScalarGridSpec(num_scalar_prefetch=n_pre, grid=grid, in_specs=ins_specs,
                                            out_specs=outs_specs, scratch_shapes=scratch)
        return pl.pallas_call(fn, name=name, grid_spec=spec, out_shape=outs_shape, compiler_params=params,
                              input_output_aliases=aliases)(*prefetch, *[_small_in_hbm(a) for a in operands])

    if comm is None:
        return list(launch(body, in_specs, out_specs, out_shape, scratch_shapes, args)), []
    n_in, n_out, n_scr, k = len(in_specs), len(out_specs), len(scratch_shapes), len(comm.arrays)

    def wrapped(*refs):
        pre, refs = refs[:n_pre], refs[n_pre:]
        ins = refs[:n_in]
        c_in = refs[n_in:n_in + k]
        outs = refs[n_in + k:n_in + k + n_out]
        c_out = refs[n_in + k + n_out:n_in + 2 * k + n_out]
        scr = refs[n_in + 2 * k + n_out:n_in + 2 * k + n_out + n_scr]
        sems = refs[n_in + 2 * k + n_out + n_scr:]
        step, steps = pl.program_id(0), grid[0]
        for d in range(1, len(grid)):
            step, steps = step * grid[d] + pl.program_id(d), steps * grid[d]

        @pl.when(step == 0)
        def _():
            comm.start(c_in, c_out, sems)

        if comm.middle is not None:
            @pl.when(step == (comm.middle_at * steps) // 8)
            def _():
                comm.middle(c_in, c_out, sems)

        body(*pre, *ins, *outs, *scr)

        @pl.when(step == steps - 1)
        def _():
            comm.finish(c_in, c_out, sems)

    res = launch(wrapped, list(in_specs) + [ANY] * k, list(out_specs) + [ANY] * k,
                 list(out_shape) + list(comm.out_shape), list(scratch_shapes) + list(comm.sems),
                 tuple(args) + tuple(comm.arrays))
    return list(res[:n_out]), list(res[n_out:])


def _row_tile(rows):
    for t in (512, 256, 128, 64, 32, 16, 8):
        if rows % t == 0:
            return t
    return rows


def _pair_sum(name, g8, recv4, core):
    _, rows, cols = recv4.shape
    tr = _row_tile(rows)
    g42 = g8.reshape(4, 2, rows, cols)

    def body(c_ref, g_ref, r_ref, o_ref):
        del c_ref
        o_ref[...] = (g_ref[...].astype(F32) + r_ref[...].astype(F32)).astype(o_ref.dtype)

    return pl.pallas_call(
        body, name=name,
        grid_spec=pltpu.PrefetchScalarGridSpec(
            num_scalar_prefetch=1, grid=(4, rows // tr),
            in_specs=[pl.BlockSpec((None, None, tr, cols), lambda j, i, c_ref: (j, c_ref[0], i, 0)),
                      pl.BlockSpec((None, tr, cols), lambda j, i, c_ref: (j, i, 0))],
            out_specs=pl.BlockSpec((None, tr, cols), lambda j, i, c_ref: (j, i, 0))),
        out_shape=SDS(recv4.shape, g8.dtype),
        compiler_params=_cparams(("arbitrary", "arbitrary"), 32),
    )(core, g42, recv4)


def _add2(name, a, b):
    rows, cols = a.shape
    tr = _row_tile(rows)

    def body(a_ref, b_ref, o_ref):
        o_ref[...] = a_ref[...] + b_ref[...]

    blk = pl.BlockSpec((tr, cols), lambda i: (i, 0))
    return pl.pallas_call(body, name=name, grid=(rows // tr,), in_specs=[blk, blk], out_specs=blk,
                          out_shape=SDS(a.shape, a.dtype),
                          compiler_params=_cparams(("arbitrary",), 32))(a, b)


def _sum_terms(name, terms):
    k, rows, cols = terms.shape
    tr = _row_tile(rows)

    def body(r_ref, o_ref):
        acc = r_ref[0]
        for q in range(1, k):
            acc = acc + r_ref[q]
        o_ref[...] = acc

    return pl.pallas_call(body, name=name, grid=(rows // tr,),
                          in_specs=[pl.BlockSpec((k, tr, cols), lambda i: (0, i, 0))],
                          out_specs=pl.BlockSpec((tr, cols), lambda i: (i, 0)),
                          out_shape=SDS((rows, cols), terms.dtype),
                          compiler_params=_cparams(("arbitrary",), 32))(terms)


def _adam_update(g, w, m, v):
    c1 = 1.0 / (1.0 - ADAM_B1 ** ADAM_STEP)
    c2 = 1.0 / (1.0 - ADAM_B2 ** ADAM_STEP)
    mn = ADAM_B1 * m + (1.0 - ADAM_B1) * g
    vn = ADAM_B2 * v + (1.0 - ADAM_B2) * (g * g)
    delta = (-ADAM_LR) * ((mn * c1) / (jnp.sqrt(vn * c2) + ADAM_EPS) + ADAM_WD * w)
    return delta, mn, vn


def _adamw_many(name, gs, ws, ms, vs):
    n = len(gs)

    def body(*refs):
        for p in range(n):
            g, w, m, v = (refs[q * n + p][...] for q in range(4))
            d, mn, vn = _adam_update(g, w, m, v)
            refs[4 * n + p][...] = d
            refs[5 * n + p][...] = mn
            refs[6 * n + p][...] = vn

    full = [pl.BlockSpec(w.shape, lambda i: (0, 0)) for w in ws]
    shapes = [SDS(w.shape, F32) for w in ws]
    res = pl.pallas_call(body, name=name, grid=(1,), in_specs=full * 4, out_specs=full * 3, out_shape=shapes * 3,
                         compiler_params=_cparams(("arbitrary",), 32),
                         )(*[_small_in_hbm(a) for a in (*gs, *ws, *ms, *vs)])
    return [(res[p], res[n + p], res[2 * n + p]) for p in range(n)]


def _adamw(name, terms, w, m, v):
    k, rows, cols = terms.shape
    tr = _row_tile(rows)

    def body(t_ref, w_ref, m_ref, v_ref, g_ref, d_ref, mo_ref, vo_ref):
        g = t_ref[0].astype(F32)
        for q in range(1, k):
            g = g + t_ref[q].astype(F32)
        g_ref[...] = g
        d_ref[...], mo_ref[...], vo_ref[...] = _adam_update(g, w_ref[...], m_ref[...], v_ref[...])

    blk = pl.BlockSpec((tr, cols), lambda i: (i, 0))
    return pl.pallas_call(body, name=name, grid=(rows // tr,),
                          in_specs=[pl.BlockSpec((k, tr, cols), lambda i: (0, i, 0)), blk, blk, blk],
                          out_specs=[blk] * 4, out_shape=[SDS((rows, cols), F32)] * 4,
                          compiler_params=_cparams(("arbitrary",), 40),
                          )(*[pltpu.with_memory_space_constraint(a, pltpu.HBM) for a in (terms, w, m, v)])


def kernel(x, norm_mix_g, w_in, conv_w, conv_b, w_rgate, b_rgate, w_igate, b_igate, lru_lambda, w_out_a, sgu_ln_g, sgu_ln_b, sgu_w_s, sgu_b_s, w_out_b, w_out, norm_mlp_g, w_up, w_down, norm_final_g, loss_target, m_norm_mix_g, m_w_in, m_conv_w, m_conv_b, m_w_rgate, m_b_rgate, m_w_igate, m_b_igate, m_lru_lambda, m_w_out_a, m_sgu_ln_g, m_sgu_ln_b, m_sgu_w_s, m_sgu_b_s, m_w_out_b, m_w_out, m_norm_mlp_g, m_w_up, m_w_down, m_norm_final_g, v_norm_mix_g, v_w_in, v_conv_w, v_conv_b, v_w_rgate, v_b_rgate, v_w_igate, v_b_igate, v_lru_lambda, v_w_out_a, v_sgu_ln_g, v_sgu_ln_b, v_sgu_w_s, v_sgu_b_s, v_w_out_b, v_w_out, v_norm_mlp_g, v_w_up, v_w_down, v_norm_final_g):
    cx, cy, cc = _place()
    me = 4 * cx + 2 * cy + cc
    core = jnp.reshape(cc, (1,)).astype(jnp.int32)
    xs = x[0]
    tgt = loss_target[0]
    s = xs.shape[0]

    gate_shard = jnp.stack([w_rgate[0], w_igate[0]]).astype(BF16).reshape(2 * HEADS * 32, HEAD_DIM)
    vec_shard = jnp.concatenate([conv_w[0], b_rgate[0], b_igate[0]], axis=1)
    vec_shard = jnp.pad(vec_shard, ((0, 4), (0, 256 - vec_shard.shape[1])))
    shards = [w_in[0].astype(BF16), w_out_a[0].astype(BF16), w_out_b[0].astype(BF16), w_out[0].astype(BF16),
              w_up[0].astype(BF16), w_down[0].astype(BF16), gate_shard, vec_shard]
    (z, n1_t, w_in_g), (gate_g, vec_g) = _in_proj(xs, norm_mix_g, shards[0], _slot_order(cx, cy, cc),
                                                comm=_gather_plan(shards[6:8]))
    gates = gate_g.reshape(N_SLOT, 2, HEADS, 32, HEAD_DIM).transpose(1, 2, 0, 3, 4).reshape(2, HEADS, HEAD_DIM, HEAD_DIM)
    w_r_f, w_i_f = gates[0], gates[1]
    conv_w_f = vec_g[:, 0:4, 0:128].transpose(1, 0, 2).reshape(CONV_K, D)
    b_r_f = vec_g[:, 0:4, 128:160].transpose(1, 0, 2).reshape(1, D)
    b_i_f = vec_g[:, 0:4, 160:192].transpose(1, 0, 2).reshape(1, D)
    b_s_t = jnp.transpose(sgu_b_s[0])

    (ya, hs, xc, r_gate, i_gate), (w_oa_g, w_ob_g, w_out_g, w_up_g) = _branch_a_fwd(
        z, conv_w_f, conv_b, w_r_f, b_r_f, w_i_f, b_i_f, lru_lambda, comm=_gather_plan(shards[1:5]))
    w_oa_f = w_oa_g.reshape(D, D)
    w_ob_f = w_ob_g.reshape(D, D)
    w_out_f = w_out_g.reshape(D, D)
    (yb,), _ = _branch_b_fwd(z, sgu_ln_g, sgu_ln_b, sgu_w_s[0], b_s_t)
    (pa, pb, merged, h1), (w_down_g,) = _merge_out(ya, yb, z, xs, w_oa_f, w_ob_f, w_out_f,
                                                   comm=_gather_plan(shards[5:6], middle_at=4))
    w_down_f = w_down_g.reshape(N_SLOT * FF_COLS, D)
    gf2 = norm_final_g.reshape(1, D)
    r_act, act_t, n2_t, dh2, loss_acc, d_gfin = _mlp_fwd(h1, norm_mlp_g, w_up_g, w_down_f, gf2, tgt)

    def pair(names, grads, recv):
        return [_pair_sum("pair_sum_" + nm, g, r, core) for nm, g, r in zip(names, grads, recv)]

    g_down = _wgrad_t("wgrad_down", act_t, dh2, N_SLOT // 2, True, False, 2048)
    g_down = g_down.reshape(N_SLOT, FF_COLS, D)
    (df, dh1, d_gmlp), (r_down,) = _mlp_bwd(dh2, r_act, w_down_f, w_up_g, h1, norm_mlp_g,
                                            comm=_sibling_plan([g_down]))
    (p_down,) = pair(["down"], [g_down], [r_down])
    g_up = _wgrad_t("wgrad_up", n2_t, df, N_SLOT, False, True, 4096)
    g_out = _wgrad("wgrad_out", merged, dh1, 1, False, False).reshape(N_SLOT, D // N_SLOT, D)
    (dz, dpa, dpb, dya, dyb), (got_down, r_up, r_out) = _merge_bwd(
        dh1, z, pa, pb, w_out_f, w_oa_f, w_ob_f, comm=_join(_chips_plan([p_down]), _sibling_plan([g_up, g_out])))
    p_up, p_out = pair(["up", "out"], [g_up, g_out], [r_up, r_out])
    g_oa = _wgrad("wgrad_out_a", ya, dpa, 1, False, False).reshape(N_SLOT, D // N_SLOT, D)
    g_ob = _wgrad("wgrad_out_b", yb, dpb, 1, False, False).reshape(N_SLOT, D // N_SLOT, D)
    (dz, d_ws, d_bs, d_ln), (got_up, r_oa, r_ob) = _branch_b_bwd(
        dz, dyb, z, sgu_ln_g, sgu_ln_b, sgu_w_s[0], b_s_t,
        comm=_join(_chips_plan([p_up]), _sibling_plan([g_oa, g_ob])))
    p_oa, p_ob = pair(["out_a", "out_b"], [g_oa, g_ob], [r_oa, r_ob])
    (dz, d_vec, d_wr, d_wi), (got_out, got_oa, got_ob) = _branch_a_bwd(
        dz, dya, z, hs, xc, r_gate, i_gate, conv_w_f, w_r_f, w_i_f, lru_lambda,
        comm=_chips_plan([p_out, p_oa, p_ob]))
    riding = [(got_up, w_up[0], m_w_up[0], v_w_up[0]), (got_down, w_down[0], m_w_down[0], v_w_down[0])]
    g_in, (o_up, o_down) = _wgrad_t("wgrad_in", n1_t, dz, N_SLOT, False, True, 4096, riding)
    o_up = [a[None] for a in o_up]
    o_down = [a[None] for a in o_down]
    g_gate = jnp.stack([d_wr, d_wi]).reshape(2, HEADS, N_SLOT, 32, HEAD_DIM).transpose(2, 0, 1, 3, 4)
    g_gate = g_gate.reshape(N_SLOT, 2 * HEADS * 32, HEAD_DIM).astype(BF16)

    d_bs_row = jnp.pad(d_bs[:, :, 0].reshape(1, GROUPS * CHUNK), ((0, 0), (0, D - GROUPS * CHUNK)))
    vecs = jnp.concatenate([d_vec, jnp.concatenate([d_ln[0:2], d_gmlp, d_gfin, d_bs_row, jnp.zeros((3, D), F32)])])
    d_ws2 = d_ws.reshape(GROUPS * CHUNK, CHUNK)
    r_in, r_gate, r_vecs, r_ws = _run_plan("rs_sibling_in", _sibling_plan([g_in, g_gate], [vecs, d_ws2]))
    p_in, p_gate = pair(["in", "gate"], [g_in, g_gate], [r_in, r_gate])
    vecs_chip = _add2("pair_sum_vecs", vecs, r_vecs)
    ws_chip = _add2("pair_sum_ws", d_ws2, r_ws)
    (dx, d_gmix), (got_in, got_gate, got_vecs, got_ws) = _in_bwd(
        dz, w_in_g, xs, dh1, norm_mix_g, comm=_chips_plan([p_in, p_gate], [vecs_chip, ws_chip]))
    vecs_sum = _sum_terms("sum_vecs", got_vecs)
    last = jnp.concatenate([d_gmix, jnp.pad(loss_acc[0:1], ((0, 0), (0, D - 128))), jnp.zeros((6, D), F32)])
    (last_all,) = _run_plan("exchange_last", _exchange_plan(last))
    last_sum = _sum_terms("sum_last", last_all)
    loss = last_sum[1, 0]
    got = [got_in, got_oa, got_ob, got_out, got_up, got_down, got_gate]

    def step(nm, terms, w, m, v, rows, cols):
        g, d, mn, vn = _adamw("adamw_" + nm, terms.reshape(4, rows, cols), w.reshape(rows, cols),
                              m.reshape(rows, cols), v.reshape(rows, cols))
        return [a.reshape(w.shape) for a in (g, d, mn, vn)]

    o_in = step("in", got[0], w_in, m_w_in, v_w_in, D, W_IN_COLS)
    o_oa = step("out_a", got[1], w_out_a, m_w_out_a, v_w_out_a, D // N_SLOT, D)
    o_ob = step("out_b", got[2], w_out_b, m_w_out_b, v_w_out_b, D // N_SLOT, D)
    o_out = step("out", got[3], w_out, m_w_out, v_w_out, D // N_SLOT, D)
    gate_w = jnp.stack([w_rgate[0], w_igate[0]]).reshape(2 * HEADS * 32, HEAD_DIM)
    gate_m = jnp.stack([m_w_rgate[0], m_w_igate[0]]).reshape(2 * HEADS * 32, HEAD_DIM)
    gate_v = jnp.stack([v_w_rgate[0], v_w_igate[0]]).reshape(2 * HEADS * 32, HEAD_DIM)
    o_gate = _adamw("adamw_gate", got[6], gate_w, gate_m, gate_v)
    o_gate = [a.reshape(2, 1, HEADS, 32, HEAD_DIM) for a in o_gate]
    o_wr = [a[0] for a in o_gate]
    o_wi = [a[1] for a in o_gate]

    def own(full, width):
        return lax.dynamic_slice_in_dim(full, me * width, width, axis=1)

    small_g = {
        "norm_mix_g": last_sum[0:1], "conv_w": own(vecs_sum[0:4], 128), "conv_b": vecs_sum[4:5],
        "b_rgate": own(vecs_sum[5:6].reshape(HEADS, HEAD_DIM), 32),
        "b_igate": own(vecs_sum[6:7].reshape(HEADS, HEAD_DIM), 32),
        "lru_lambda": vecs_sum[7:8], "sgu_ln_g": vecs_sum[8:9], "sgu_ln_b": vecs_sum[9:10],
        "norm_mlp_g": vecs_sum[10:11], "norm_final_g": vecs_sum[11:12],
        "sgu_b_s": vecs_sum[12, 0:GROUPS * CHUNK].reshape(GROUPS, CHUNK),
    }
    small_w = {"norm_mix_g": (norm_mix_g, m_norm_mix_g, v_norm_mix_g), "conv_w": (conv_w, m_conv_w, v_conv_w),
               "conv_b": (conv_b, m_conv_b, v_conv_b), "b_rgate": (b_rgate, m_b_rgate, v_b_rgate),
               "b_igate": (b_igate, m_b_igate, v_b_igate), "lru_lambda": (lru_lambda, m_lru_lambda, v_lru_lambda),
               "sgu_ln_g": (sgu_ln_g, m_sgu_ln_g, v_sgu_ln_g), "sgu_ln_b": (sgu_ln_b, m_sgu_ln_b, v_sgu_ln_b),
               "norm_mlp_g": (norm_mlp_g, m_norm_mlp_g, v_norm_mlp_g),
               "norm_final_g": (norm_final_g, m_norm_final_g, v_norm_final_g),
               "sgu_b_s": (sgu_b_s, m_sgu_b_s, v_sgu_b_s), "sgu_w_s": (sgu_w_s, m_sgu_w_s, v_sgu_w_s)}
    order = list(small_g)
    as2d = lambda k, a: a.reshape(small_g[k].shape)
    upd = _adamw_many("adamw_small", [small_g[k] for k in order], *[[as2d(k, small_w[k][q]) for k in order]
                                                                     for q in range(3)])
    o_small = {k: [a.reshape(small_w[k][0].shape) for a in (small_g[k],) + u] for k, u in zip(order, upd)}
    ws3 = [a[0].reshape(GROUPS * CHUNK, CHUNK) for a in small_w.pop("sgu_w_s")]
    o_small["sgu_w_s"] = [a.reshape(sgu_w_s.shape) for a in _adamw("adamw_ws", got_ws, *ws3)]

    per_weight = {"norm_mix_g": o_small["norm_mix_g"], "w_in": o_in, "conv_w": o_small["conv_w"],
                  "conv_b": o_small["conv_b"], "w_rgate": o_wr, "b_rgate": o_small["b_rgate"], "w_igate": o_wi,
                  "b_igate": o_small["b_igate"], "lru_lambda": o_small["lru_lambda"], "w_out_a": o_oa,
                  "sgu_ln_g": o_small["sgu_ln_g"], "sgu_ln_b": o_small["sgu_ln_b"], "sgu_w_s": o_small["sgu_w_s"],
                  "sgu_b_s": o_small["sgu_b_s"], "w_out_b": o_ob, "w_out": o_out, "norm_mlp_g": o_small["norm_mlp_g"],
                  "w_up": o_up, "w_down": o_down, "norm_final_g": o_small["norm_final_g"]}
    names_w = list(per_weight)
    return (loss, dx[None], *[per_weight[k][0] for k in names_w], *[per_weight[k][1] for k in names_w],
            *[per_weight[k][2] for k in names_w], *[per_weight[k][3] for k in names_w])
```

```python
import jax
import jax.numpy as jnp
from jax import lax
from jax.experimental import pallas as pl
from jax.experimental.pallas import tpu as pltpu

F32 = jnp.float32
BF16 = jnp.bfloat16
SDS = jax.ShapeDtypeStruct
MESH = pl.DeviceIdType.MESH
ANY = pl.BlockSpec(memory_space=pl.ANY)

D = 1024
N_SLOT = 8
W_IN_COLS = 768
FF_COLS = 512
HEADS, HEAD_DIM = 4, 256
GROUPS, GROUP_DIM = 4, 256
CHUNK = 128
CONV_K = 4
NORM_EPS = 1e-6
LN_EPS = 1e-5
LRU_C = 8.0
ADAM_LR, ADAM_B1, ADAM_B2, ADAM_EPS, ADAM_WD, ADAM_STEP = 0.001, 0.9, 0.999, 1e-08, 0.01, 10

TM_ROWS = 1024
TM_MERGE = 512
T_BRANCH_A = 512
T_BRANCH_B = 256
MiB = 1024 * 1024
SMALL_OPERAND = 16 * 1024

_GELU_C = 0.7978845608028654
_GELU_A = 0.044715


def _small_in_hbm(a):
    return pltpu.with_memory_space_constraint(a, pltpu.HBM) if a.size <= SMALL_OPERAND else a


def _cparams(sem, vmem_mib):
    return pltpu.CompilerParams(dimension_semantics=sem, vmem_limit_bytes=vmem_mib * MiB)


def _gelu(x):
    t = jnp.tanh(_GELU_C * (x + _GELU_A * x * x * x))
    return 0.5 * x * (1.0 + t)


def _gelu_and_grad(x):
    x2 = x * x
    t = jnp.tanh(_GELU_C * x * (1.0 + _GELU_A * x2))
    g = 0.5 * x * (1.0 + t)
    dg = 0.5 * (1.0 + t) + 0.5 * x * (1.0 - t * t) * _GELU_C * (1.0 + 3.0 * _GELU_A * x2)
    return g, dg


def _softplus(x):
    return jnp.maximum(x, 0.0) + jnp.log1p(jnp.exp(-jnp.abs(x)))


def _dot(a, b):
    return jnp.dot(a, b, preferred_element_type=F32)


def _dot_nt(a, b):
    return lax.dot_general(a, b, (((1,), (1,)), ((), ())), preferred_element_type=F32)


def _dot_tn(a, b):
    return lax.dot_general(a, b, (((0,), (0,)), ((), ())), preferred_element_type=F32)


def _rows_shifted(prev8, cur, k):
    ext = jnp.concatenate([prev8, cur], axis=0)
    return pltpu.roll(ext, k, 0)[8:]


def _rows_advanced(cur, next8, k):
    t = cur.shape[0]
    ext = jnp.concatenate([cur, next8], axis=0)
    return pltpu.roll(ext, t + 8 - k, 0)[:t]


def _first_second(x, y, c):
    ny, nx, far = _other_chips(x, y)
    pick = lambda a, b: a * (1 - c) + b * c
    first = tuple(pick(a, b) for a, b in zip(ny, nx))
    second = tuple(pick(b, a) for a, b in zip(ny, nx))
    return first, second, far


def _slot_order(x, y, c):
    chip = 2 * x + y
    first, second, far = _first_second(x, y, c)
    order = [2 * chip + c, 2 * chip + 1 - c, 2 * first[2] + c, 2 * second[2] + 1 - c, 2 * second[2] + c,
             2 * first[2] + 1 - c, 2 * far[2] + c, 2 * far[2] + 1 - c]
    return jnp.stack(order).astype(jnp.int32)


def _in_proj(x, g_mix, w_in_own, order, comm=None):
    s = x.shape[0]
    tm = min(TM_ROWS, s)
    ni = s // tm

    def body(order_ref, x_ref, g_ref, own_ref, z_ref, nt_ref, wg_ref, n_s, w_s, send_sems, recv_sems, local_sems):
        j, i = pl.program_id(0), pl.program_id(1)
        px, py, c = _place()
        chip = 2 * px + py
        me = 2 * chip + c
        sib = (px, py, 1 - c)
        chips = _other_chips(px, py)

        def rc(k, src, blk, to):
            return pltpu.make_async_remote_copy(src_ref=src, dst_ref=w_s.at[blk], send_sem=send_sems.at[k],
                                                recv_sem=recv_sems.at[k], device_id=to, device_id_type=MESH)

        del chips
        first, second, far = _first_second(px, py, c)
        blocks = [2 * first[2] + c, 2 * second[2] + c, 2 * far[2] + c]
        own_in = pltpu.make_async_copy(own_ref, w_s.at[me], local_sems.at[0])
        to_first = rc(1, own_ref, me, (first[0], first[1], c))
        to_second = rc(2, own_ref, me, (second[0], second[1], c))
        relay = rc(3, w_s.at[blocks[0]], blocks[0], (second[0], second[1], c))
        sends = [rc(0, own_ref, me, sib), to_first, to_second, relay]
        passed = [rc(4 + q, w_s.at[blk], blk, sib) for q, blk in enumerate(blocks)]
        keep = pltpu.make_async_copy(w_s, wg_ref, local_sems.at[1])

        @pl.when((i == 0) & (j == 0))
        def _():
            own_in.start()
            sends[0].start()
            to_first.start()
            own_in.wait()

        @pl.when((i == 0) & (j == 1))
        def _():
            rc(0, own_ref, 2 * chip + 1 - c, sib).wait_recv()

        for q, blk in enumerate(blocks):
            @pl.when((i == 0) & (j == 2 + 2 * q))
            def _():
                rc(1 + q, own_ref, blk, sib).wait_recv()
                passed[q].start()
                if q == 0:
                    to_second.start()
                    relay.start()

            @pl.when((i == 0) & (j == 3 + 2 * q))
            def _():
                rc(4 + q, own_ref, order_ref[j], sib).wait_recv()

        rows = pl.ds(pl.multiple_of(i * tm, tm), tm)

        @pl.when(j == 0)
        def _():
            xv = x_ref[...]
            rstd = lax.rsqrt(jnp.mean(xv * xv, axis=-1, keepdims=True) + NORM_EPS)
            nb = (xv * rstd * g_ref[...]).astype(BF16)
            n_s[rows, :] = nb
            nt_ref[...] = nb.T

        z_ref[...] = _dot(n_s[rows, :], w_s[order_ref[j]]).astype(BF16)

        @pl.when((i == 0) & (j == N_SLOT - 1))
        def _():
            keep.start()

        @pl.when((i == ni - 1) & (j == N_SLOT - 1))
        def _():
            for cp in sends + passed:
                cp.wait_send()
            keep.wait()

    first_pass = lambda j, i, o: (jnp.where(j == 0, i, ni - 1), 0)
    (z, n1, w_in_g), extra = _call(
        body, name="in_proj", grid=(N_SLOT, ni), prefetch=(order,),
        in_specs=[pl.BlockSpec((tm, D), first_pass),
                  pl.BlockSpec((1, D), lambda j, i, o: (0, 0)), ANY],
        out_specs=[pl.BlockSpec((tm, W_IN_COLS), lambda j, i, o: (i, o[j])),
                   pl.BlockSpec((D, tm), lambda j, i, o: (0, jnp.where(j == 0, i, ni - 1))), ANY],
        out_shape=[SDS((s, N_SLOT * W_IN_COLS), BF16), SDS((D, s), BF16), SDS((N_SLOT, D, W_IN_COLS), BF16)],
        scratch_shapes=[pltpu.VMEM((s, D), BF16), pltpu.VMEM((N_SLOT, D, W_IN_COLS), BF16),
                        pltpu.SemaphoreType.DMA((7,)), pltpu.SemaphoreType.DMA((7,)), pltpu.SemaphoreType.DMA((2,))],
        params=_cparams(("arbitrary", "arbitrary"), 56), args=(x, g_mix, w_in_own), comm=comm)
    return (z, n1, w_in_g), extra


def _decay(r, sp_lam):
    log_a = (-LRU_C) * r * sp_lam
    a = jnp.exp(log_a)
    return a, jnp.sqrt(-jnp.tanh(log_a) * (a * a + 1.0))


def _lru_gates(xc, xcb, wr_ref, br, wi_ref, bi, sp_lam, a_s, b_s, r_ref, i_ref):
    for h in range(HEADS):
        sl = slice(h * HEAD_DIM, (h + 1) * HEAD_DIM)
        r = jax.nn.sigmoid(_dot(xcb[:, sl], wr_ref[h]) + br[:, sl])
        ig = jax.nn.sigmoid(_dot(xcb[:, sl], wi_ref[h]) + bi[:, sl])
        a, mult = _decay(r, sp_lam[:, sl])
        a_s[:, sl] = a
        b_s[:, sl] = xc[:, sl] * ig * mult
        r_ref[:, sl] = r.astype(BF16)
        i_ref[:, sl] = ig.astype(BF16)


def _conv_fwd(xa, prev8, cw, cb):
    xc = cb + cw[0:1, :] * xa
    for k in range(1, CONV_K):
        xc = xc + cw[k:k + 1, :] * _rows_shifted(prev8, xa, k)
    return xc


def _branch_a_fwd(z, conv_w, conv_b, w_r, b_r, w_i, b_i, lam, comm=None):
    s = z.shape[0]
    ta = min(T_BRANCH_A, s)
    per16 = ta // 16

    def body(xa_ref, xp_ref, ga_ref, cw_ref, cb_ref, wr_ref, br_ref, wi_ref, bi_ref, lam_ref,
             ya_ref, hs_ref, xc_ref, r_ref, i_ref, a_s, b_s, h_s, carry_s):
        i = pl.program_id(0)

        @pl.when(i == 0)
        def _():
            carry_s[...] = jnp.zeros_like(carry_s)

        xa = xa_ref[...].astype(F32)
        prev8 = jnp.where(i > 0, xp_ref[...].astype(F32)[8:16], 0.0)
        xc = _conv_fwd(xa, prev8, cw_ref[...], cb_ref[...])
        xcb = xc.astype(BF16)
        xc_ref[...] = xcb
        sp_lam = _softplus(-lam_ref[...])
        _lru_gates(xc, xcb, wr_ref, br_ref[...], wi_ref, bi_ref[...], sp_lam, a_s, b_s, r_ref, i_ref)

        row = lax.broadcasted_iota(jnp.int32, (8, D), 0)

        def group(g, carry):
            off = pl.multiple_of(g * 8, 8)
            a8 = a_s[pl.ds(off, 8), :]
            b8 = b_s[pl.ds(off, 8), :]
            for d in (1, 2, 4):
                a_sh = jnp.where(row >= d, pltpu.roll(a8, d, 0), 1.0)
                b_sh = jnp.where(row >= d, pltpu.roll(b8, d, 0), 0.0)
                b8 = a8 * b_sh + b8
                a8 = a8 * a_sh
            h8 = b8 + a8 * carry
            h_s[pl.ds(off, 8), :] = h8
            return jnp.broadcast_to(h8[7:8, :], (8, D))

        carry_s[...] = lax.fori_loop(0, ta // 8, group, carry_s[...])
        hs = h_s[...]
        hs_ref[...] = hs.astype(BF16)
        ya_ref[...] = (hs * _gelu(ga_ref[...].astype(F32))).astype(BF16)

    vec = pl.BlockSpec((1, D), lambda i: (0, 0))
    gate = pl.BlockSpec((HEADS, HEAD_DIM, HEAD_DIM), lambda i: (0, 0, 0))
    return _call(
        body, name="branch_a_fwd", grid=(s // ta,),
        in_specs=[pl.BlockSpec((ta, D), lambda i: (i, 0)),
                  pl.BlockSpec((16, D), lambda i: (jnp.maximum(i * per16 - 1, 0), 0)),
                  pl.BlockSpec((ta, D), lambda i: (i, 1)),
                  pl.BlockSpec((CONV_K, D), lambda i: (0, 0)), vec, gate, vec, gate, vec, vec],
        out_specs=[pl.BlockSpec((ta, D), lambda i: (i, 0))] * 5,
        out_shape=[SDS((s, D), BF16)] * 5,
        scratch_shapes=[pltpu.VMEM((ta, D), F32), pltpu.VMEM((ta, D), F32), pltpu.VMEM((ta, D), F32),
                        pltpu.VMEM((8, D), F32)],
        params=_cparams(("arbitrary",), 40), args=(z, z, z, conv_w, conv_b, w_r, b_r, w_i, b_i, lam), comm=comm)


def _sgu_common(ub, vb, lg, lb, with_grad):
    if with_grad:
        u, du = _gelu_and_grad(ub)
        v, dv = _gelu_and_grad(vb)
    else:
        u, v, du, dv = _gelu(ub), _gelu(vb), None, None
    mu = jnp.mean(v, axis=-1, keepdims=True)
    vc = v - mu
    rstd = lax.rsqrt(jnp.mean(vc * vc, axis=-1, keepdims=True) + LN_EPS)
    vhat = vc * rstd
    vln = vhat * lg + lb
    return u, du, dv, rstd, vhat, vln


def _masked_ws(ws_ref):
    t = lax.broadcasted_iota(jnp.int32, (CHUNK, CHUNK), 0)
    c = lax.broadcasted_iota(jnp.int32, (CHUNK, CHUNK), 1)
    keep = c <= t
    return [jnp.where(keep, ws_ref[g], 0.0).astype(BF16) for g in range(GROUPS)]


def _branch_b_fwd(z, ln_g, ln_b, w_s, b_s_t, comm=None):
    s = z.shape[0]
    tb = min(T_BRANCH_B, s)

    def body(ub_ref, vb_ref, lg_ref, lb_ref, ws_ref, bs_ref, yb_ref):
        u, _, _, _, _, vln = _sgu_common(ub_ref[...].astype(F32), vb_ref[...].astype(F32),
                                         lg_ref[...], lb_ref[...], False)
        vlnb = vln.astype(BF16)
        wm = _masked_ws(ws_ref)
        bs = bs_ref[...]
        for c in range(tb // CHUNK):
            rs = slice(c * CHUNK, (c + 1) * CHUNK)
            for g in range(GROUPS):
                cs = slice(g * GROUP_DIM, (g + 1) * GROUP_DIM)
                sp = _dot(wm[g], vlnb[rs, cs]) + bs[:, g:g + 1]
                yb_ref[rs, cs] = (u[rs, cs] * sp).astype(BF16)

    vec = pl.BlockSpec((1, D), lambda i: (0, 0))
    return _call(
        body, name="branch_b_fwd", grid=(s // tb,),
        in_specs=[pl.BlockSpec((tb, D), lambda i: (i, 2)), pl.BlockSpec((tb, D), lambda i: (i, 3)), vec, vec,
                  pl.BlockSpec((GROUPS, CHUNK, CHUNK), lambda i: (0, 0, 0)),
                  pl.BlockSpec((CHUNK, GROUPS), lambda i: (0, 0))],
        out_specs=[pl.BlockSpec((tb, D), lambda i: (i, 0))],
        out_shape=[SDS((s, D), BF16)], scratch_shapes=[],
        params=_cparams(("arbitrary",), 40), args=(z, z, ln_g, ln_b, w_s, b_s_t), comm=comm)


def _merge_out(ya, yb, z, x, w_oa, w_ob, w_out, comm=None):
    s = x.shape[0]
    tm = min(TM_MERGE, s)

    def body(ya_ref, yb_ref, ma_ref, mb_ref, x_ref, woa_ref, wob_ref, wo_ref, pa_ref, pb_ref, mg_ref, h1_ref):
        pa = _dot(ya_ref[...], woa_ref[...])
        pb = _dot(yb_ref[...], wob_ref[...])
        merged = (jax.nn.sigmoid(ma_ref[...].astype(F32)) * pa
                  + jax.nn.sigmoid(mb_ref[...].astype(F32)) * pb).astype(BF16)
        pa_ref[...] = pa.astype(BF16)
        pb_ref[...] = pb.astype(BF16)
        mg_ref[...] = merged
        h1_ref[...] = x_ref[...] + _dot(merged, wo_ref[...])

    row = pl.BlockSpec((tm, D), lambda i: (i, 0))
    wsp = pl.BlockSpec((D, D), lambda i: (0, 0))
    return _call(
        body, name="merge_out", grid=(s // tm,),
        in_specs=[row, row, pl.BlockSpec((tm, D), lambda i: (i, 4)), pl.BlockSpec((tm, D), lambda i: (i, 5)),
                  row, wsp, wsp, wsp],
        out_specs=[row, row, row, row],
        out_shape=[SDS((s, D), BF16), SDS((s, D), BF16), SDS((s, D), BF16), SDS((s, D), F32)], scratch_shapes=[],
        params=_cparams(("arbitrary",), 48), args=(ya, yb, z, z, x, w_oa, w_ob, w_out), comm=comm)


def _mlp_fwd(h1, g_mlp, w_up_g, w_down, g_fin, tgt):
    s = h1.shape[0]
    tm = min(TM_ROWS, s)
    nj = N_SLOT

    def body(h1_ref, gm_ref, wu_ref, wd_ref, gf_ref, t_ref, r_ref, at_ref, n2t_ref, dh2_ref, loss_ref, dgf_ref,
             n2_s, acc_s):
        i, j = pl.program_id(0), pl.program_id(1)

        @pl.when(j == 0)
        def _():
            hv = h1_ref[...]
            rstd = lax.rsqrt(jnp.mean(hv * hv, axis=-1, keepdims=True) + NORM_EPS)
            nb = (hv * rstd * gm_ref[...]).astype(BF16)
            n2_s[...] = nb
            n2t_ref[...] = nb.T
            acc_s[...] = jnp.zeros_like(acc_s)

        @pl.when((i == 0) & (j == 0))
        def _():
            loss_ref[...] = jnp.zeros_like(loss_ref)
            dgf_ref[...] = jnp.zeros_like(dgf_ref)

        r = jnp.maximum(_dot(n2_s[...], wu_ref[...]), 0.0)
        r_ref[...] = r.astype(BF16)
        act = (r * r).astype(BF16)
        at_ref[...] = act.T
        acc_s[...] += _dot(act, wd_ref[...])

        @pl.when(j == nj - 1)
        def _():
            h2 = h1_ref[...] + acc_s[...]
            rstd = lax.rsqrt(jnp.mean(h2 * h2, axis=-1, keepdims=True) + NORM_EPS)
            hh = h2 * rstd
            gf = gf_ref[...]
            e = hh * gf - t_ref[...]
            loss_ref[...] += jnp.sum(e * e) * (0.5 / D)
            dy = e * (1.0 / D)
            dgf_ref[...] += jnp.sum(dy * hh, axis=0, keepdims=True)
            dhh = dy * gf
            dh2_ref[...] = rstd * (dhh - hh * jnp.mean(dhh * hh, axis=-1, keepdims=True))

    row = pl.BlockSpec((tm, D), lambda i, j: (i, 0))
    vec = pl.BlockSpec((1, D), lambda i, j: (0, 0))
    return pl.pallas_call(
        body, name="mlp_fwd", grid=(s // tm, nj),
        in_specs=[row, vec, pl.BlockSpec((None, D, FF_COLS), lambda i, j: (j, 0, 0)),
                  pl.BlockSpec((FF_COLS, D), lambda i, j: (j, 0)), vec, row],
        out_specs=[pl.BlockSpec((tm, FF_COLS), lambda i, j: (i, j)), pl.BlockSpec((FF_COLS, tm), lambda i, j: (j, i)),
                   pl.BlockSpec((D, tm), lambda i, j: (0, i)), row, pl.BlockSpec((8, 128), lambda i, j: (0, 0)), vec],
        out_shape=[SDS((s, nj * FF_COLS), BF16), SDS((nj * FF_COLS, s), BF16), SDS((D, s), BF16), SDS((s, D), F32),
                   SDS((8, 128), F32), SDS((1, D), F32)],
        scratch_shapes=[pltpu.VMEM((tm, D), BF16), pltpu.VMEM((tm, D), F32)],
        compiler_params=_cparams(("arbitrary", "arbitrary"), 52),
    )(h1, _small_in_hbm(g_mlp), w_up_g, w_down, _small_in_hbm(g_fin), tgt)


def _mlp_bwd(dh2, r, w_down, w_up_g, h1, g_mlp, comm=None):
    s = h1.shape[0]
    tm = min(TM_ROWS, s)
    nj = N_SLOT

    def body(dh2_ref, r_ref, wd_ref, wu_ref, h1_ref, gm_ref, df_ref, dh1_ref, dgm_ref, dh2b_s, acc_s):
        i, j = pl.program_id(0), pl.program_id(1)

        @pl.when(j == 0)
        def _():
            dh2b_s[...] = dh2_ref[...].astype(BF16)
            acc_s[...] = jnp.zeros_like(acc_s)

        @pl.when((i == 0) & (j == 0))
        def _():
            dgm_ref[...] = jnp.zeros_like(dgm_ref)

        d_act = _dot_nt(dh2b_s[...], wd_ref[...])
        df = (d_act * (2.0 * r_ref[...].astype(F32))).astype(BF16)
        df_ref[...] = df
        acc_s[...] += _dot_nt(df, wu_ref[...])

        @pl.when(j == nj - 1)
        def _():
            hv = h1_ref[...]
            rstd = lax.rsqrt(jnp.mean(hv * hv, axis=-1, keepdims=True) + NORM_EPS)
            hh = hv * rstd
            dn2 = acc_s[...]
            dgm_ref[...] += jnp.sum(dn2 * hh, axis=0, keepdims=True)
            dhat = dn2 * gm_ref[...]
            dh1_ref[...] = dh2_ref[...] + rstd * (dhat - hh * jnp.mean(dhat * hh, axis=-1, keepdims=True))

    row = pl.BlockSpec((tm, D), lambda i, j: (i, 0))
    vec = pl.BlockSpec((1, D), lambda i, j: (0, 0))
    ffb = pl.BlockSpec((tm, FF_COLS), lambda i, j: (i, j))
    return _call(
        body, name="mlp_bwd", grid=(s // tm, nj),
        in_specs=[row, ffb, pl.BlockSpec((FF_COLS, D), lambda i, j: (j, 0)),
                  pl.BlockSpec((None, D, FF_COLS), lambda i, j: (j, 0, 0)), row, vec],
        out_specs=[ffb, row, vec],
        out_shape=[SDS((s, nj * FF_COLS), BF16), SDS((s, D), F32), SDS((1, D), F32)],
        scratch_shapes=[pltpu.VMEM((tm, D), BF16), pltpu.VMEM((tm, D), F32)],
        params=_cparams(("arbitrary", "arbitrary"), 52), args=(dh2, r, w_down, w_up_g, h1, g_mlp), comm=comm)


def _merge_bwd(dh1, z, pa, pb, w_out, w_oa, w_ob, comm=None):
    s = dh1.shape[0]
    tm = min(TM_MERGE, s)

    def body(dh1_ref, ma_ref, mb_ref, pa_ref, pb_ref, wo_ref, woa_ref, wob_ref,
             dz_ref, dpa_ref, dpb_ref, dya_ref, dyb_ref):
        dm = _dot_nt(dh1_ref[...].astype(BF16), wo_ref[...])
        sa = jax.nn.sigmoid(ma_ref[...].astype(F32))
        sb = jax.nn.sigmoid(mb_ref[...].astype(F32))
        dpa = (dm * sa).astype(BF16)
        dpb = (dm * sb).astype(BF16)
        dz_ref[:, 0:D] = (dm * pa_ref[...].astype(F32) * sa * (1.0 - sa)).astype(BF16)
        dz_ref[:, D:2 * D] = (dm * pb_ref[...].astype(F32) * sb * (1.0 - sb)).astype(BF16)
        dpa_ref[...] = dpa
        dpb_ref[...] = dpb
        dya_ref[...] = _dot_nt(dpa, woa_ref[...]).astype(BF16)
        dyb_ref[...] = _dot_nt(dpb, wob_ref[...]).astype(BF16)

    row = pl.BlockSpec((tm, D), lambda i: (i, 0))
    wsp = pl.BlockSpec((D, D), lambda i: (0, 0))
    return _call(
        body, name="merge_bwd", grid=(s // tm,),
        in_specs=[row, pl.BlockSpec((tm, D), lambda i: (i, 4)), pl.BlockSpec((tm, D), lambda i: (i, 5)),
                  row, row, wsp, wsp, wsp],
        out_specs=[pl.BlockSpec((tm, 2 * D), lambda i: (i, 2)), row, row, row, row],
        out_shape=[SDS((s, 6 * D), BF16)] + [SDS((s, D), BF16)] * 4, scratch_shapes=[],
        params=_cparams(("arbitrary",), 48), args=(dh1, z, z, pa, pb, w_out, w_oa, w_ob), comm=comm)


def _branch_b_bwd(dz, dyb, z, ln_g, ln_b, w_s, b_s_t, comm=None):
    s = z.shape[0]
    tb = min(T_BRANCH_B, s)

    def body(dz_in, dyb_ref, ub_ref, vb_ref, lg_ref, lb_ref, ws_ref, bs_ref,
             dz_ref, dws_ref, dbs_ref, dln_ref, du_s, dvln_s):
        del dz_in

        @pl.when(pl.program_id(0) == 0)
        def _():
            dws_ref[...] = jnp.zeros_like(dws_ref)
            dbs_ref[...] = jnp.zeros_like(dbs_ref)
            dln_ref[...] = jnp.zeros_like(dln_ref)

        lg = lg_ref[...]
        u, du, dv, rstd, vhat, vln = _sgu_common(ub_ref[...].astype(F32), vb_ref[...].astype(F32),
                                                 lg, lb_ref[...], True)
        vlnb = vln.astype(BF16)
        dyb_v = dyb_ref[...].astype(F32)
        wm = _masked_ws(ws_ref)
        keep = (lax.broadcasted_iota(jnp.int32, (CHUNK, CHUNK), 1)
                <= lax.broadcasted_iota(jnp.int32, (CHUNK, CHUNK), 0))
        bs = bs_ref[...]
        for c in range(tb // CHUNK):
            rs = slice(c * CHUNK, (c + 1) * CHUNK)
            for g in range(GROUPS):
                cs = slice(g * GROUP_DIM, (g + 1) * GROUP_DIM)
                v_blk = vlnb[rs, cs]
                sp = _dot(wm[g], v_blk) + bs[:, g:g + 1]
                d_sp = dyb_v[rs, cs] * u[rs, cs]
                d_spb = d_sp.astype(BF16)
                du_s[rs, cs] = dyb_v[rs, cs] * sp
                dvln_s[rs, cs] = _dot_tn(wm[g], d_spb)
                dws_ref[g] += jnp.where(keep, _dot_nt(d_spb, v_blk), 0.0)
                dbs_ref[g] += jnp.broadcast_to(jnp.sum(d_sp, axis=-1, keepdims=True), (CHUNK, CHUNK))
        dvln = dvln_s[...]
        dln_ref[0:1, :] += jnp.sum(dvln * vhat, axis=0, keepdims=True)
        dln_ref[1:2, :] += jnp.sum(dvln, axis=0, keepdims=True)
        dvh = dvln * lg
        d_v = rstd * (dvh - jnp.mean(dvh, axis=-1, keepdims=True)
                      - vhat * jnp.mean(dvh * vhat, axis=-1, keepdims=True))
        dz_ref[:, 0:D] = (du_s[...] * du).astype(BF16)
        dz_ref[:, D:2 * D] = (d_v * dv).astype(BF16)

    vec = pl.BlockSpec((1, D), lambda i: (0, 0))
    sq = pl.BlockSpec((GROUPS, CHUNK, CHUNK), lambda i: (0, 0, 0))
    return _call(
        body, name="branch_b_bwd", grid=(s // tb,),
        in_specs=[ANY, pl.BlockSpec((tb, D), lambda i: (i, 0)),
                  pl.BlockSpec((tb, D), lambda i: (i, 2)), pl.BlockSpec((tb, D), lambda i: (i, 3)), vec, vec, sq,
                  pl.BlockSpec((CHUNK, GROUPS), lambda i: (0, 0))],
        out_specs=[pl.BlockSpec((tb, 2 * D), lambda i: (i, 1)), sq, sq, pl.BlockSpec((8, D), lambda i: (0, 0))],
        out_shape=[SDS(dz.shape, BF16), SDS((GROUPS, CHUNK, CHUNK), F32), SDS((GROUPS, CHUNK, CHUNK), F32),
                   SDS((8, D), F32)],
        scratch_shapes=[pltpu.VMEM((tb, D), F32), pltpu.VMEM((tb, D), F32)], aliases={0: 0},
        params=_cparams(("arbitrary",), 40), args=(dz, dyb, z, z, ln_g, ln_b, w_s, b_s_t), comm=comm)


def _branch_a_bwd(dz, dya, z, hs, xc, r, ig, conv_w, w_r, w_i, lam, comm=None):
    s = z.shape[0]
    ta = min(T_BRANCH_A, s)
    nb = s // ta
    per16 = ta // 16

    def body(dz_in, dya_ref, xa_ref, ga_ref, hs_ref, hp_ref, xc_ref, r_ref, i_ref, cw_ref, wr_ref, wi_ref,
             lam_ref, dz_ref, vec_ref, dwr_ref, dwi_ref, a_s, b_s, h_s, dcar_s, acar_s, dxc_s):
        del dz_in
        i = pl.program_id(0)
        blk = nb - 1 - i

        @pl.when(i == 0)
        def _():
            dcar_s[...] = jnp.zeros_like(dcar_s)
            acar_s[...] = jnp.zeros_like(acar_s)
            dxc_s[...] = jnp.zeros_like(dxc_s)
            vec_ref[...] = jnp.zeros_like(vec_ref)
            dwr_ref[...] = jnp.zeros_like(dwr_ref)
            dwi_ref[...] = jnp.zeros_like(dwi_ref)

        cw = cw_ref[...]
        lam_v = lam_ref[...]
        xa = xa_ref[...].astype(F32)
        xcb = xc_ref[...]
        xc = xcb.astype(F32)
        sp_lam = _softplus(-lam_v)
        r_v, i_v = r_ref[...].astype(F32), i_ref[...].astype(F32)
        a_v, m_v = _decay(r_v, sp_lam)

        hs_v = hs_ref[...].astype(F32)
        hprev8 = jnp.where(blk > 0, hp_ref[...].astype(F32)[8:16], 0.0)
        h_m1 = _rows_shifted(hprev8, hs_v, 1)
        gg, dgg = _gelu_and_grad(ga_ref[...].astype(F32))
        dya_v = dya_ref[...].astype(F32)
        dz_ref[:, D:2 * D] = (dya_v * hs_v * dgg).astype(BF16)

        a_s[...] = _rows_advanced(a_v, acar_s[...], 1)
        b_s[...] = dya_v * gg

        row = lax.broadcasted_iota(jnp.int32, (8, D), 0)
        ng = ta // 8

        def group(gi, carry):
            off = pl.multiple_of((ng - 1 - gi) * 8, 8)
            c8 = a_s[pl.ds(off, 8), :]
            d8 = b_s[pl.ds(off, 8), :]
            for d in (1, 2, 4):
                c_sh = jnp.where(row < 8 - d, pltpu.roll(c8, 8 - d, 0), 1.0)
                d_sh = jnp.where(row < 8 - d, pltpu.roll(d8, 8 - d, 0), 0.0)
                d8 = c8 * d_sh + d8
                c8 = c8 * c_sh
            dh8 = d8 + c8 * carry
            h_s[pl.ds(off, 8), :] = dh8
            return jnp.broadcast_to(dh8[0:1, :], (8, D))

        dcar_s[...] = lax.fori_loop(0, ng, group, dcar_s[...])
        acar_s[...] = jnp.broadcast_to(a_v[0:1, :], (8, D))

        dbx = h_s[...]
        d_mult = dbx * xc * i_v
        d_loga = dbx * h_m1 * a_v - d_mult * (a_v * a_v) / m_v
        d_pr = d_loga * ((-LRU_C) * sp_lam) * r_v * (1.0 - r_v)
        d_pi = dbx * xc * m_v * i_v * (1.0 - i_v)
        vec_ref[7:8, :] += jnp.sum(d_loga * r_v, axis=0, keepdims=True) * (LRU_C * jax.nn.sigmoid(-lam_v))
        vec_ref[5:6, :] += jnp.sum(d_pr, axis=0, keepdims=True)
        vec_ref[6:7, :] += jnp.sum(d_pi, axis=0, keepdims=True)
        d_prb = d_pr.astype(BF16)
        d_pib = d_pi.astype(BF16)
        h_s[...] = dbx * i_v * m_v
        for h in range(HEADS):
            sl = slice(h * HEAD_DIM, (h + 1) * HEAD_DIM)
            h_s[:, sl] += _dot_nt(d_prb[:, sl], wr_ref[h]) + _dot_nt(d_pib[:, sl], wi_ref[h])
            dwr_ref[h] += _dot_tn(xcb[:, sl], d_prb[:, sl])
            dwi_ref[h] += _dot_tn(xcb[:, sl], d_pib[:, sl])
        d_xc = h_s[...]
        vec_ref[4:5, :] += jnp.sum(d_xc, axis=0, keepdims=True)
        vec_ref[0:1, :] += jnp.sum(d_xc * xa, axis=0, keepdims=True)
        d_xa = cw[0:1, :] * d_xc
        nxt = dxc_s[...]
        for k in range(1, CONV_K):
            ahead = _rows_advanced(d_xc, nxt, k)
            vec_ref[k:k + 1, :] += jnp.sum(ahead * xa, axis=0, keepdims=True)
            d_xa = d_xa + cw[k:k + 1, :] * ahead
        dz_ref[:, 0:D] = d_xa.astype(BF16)
        dxc_s[...] = d_xc[0:8, :]

    vec = pl.BlockSpec((1, D), lambda i: (0, 0))
    gate = pl.BlockSpec((HEADS, HEAD_DIM, HEAD_DIM), lambda i: (0, 0, 0))
    cur = lambda c: pl.BlockSpec((ta, D), lambda i: (nb - 1 - i, c))
    before = lambda c: pl.BlockSpec((16, D), lambda i: (jnp.maximum((nb - 1 - i) * per16 - 1, 0), c))
    return _call(
        body, name="branch_a_bwd", grid=(nb,),
        in_specs=[ANY, cur(0), cur(0), cur(1), cur(0), before(0), cur(0), cur(0), cur(0),
                  pl.BlockSpec((CONV_K, D), lambda i: (0, 0)), gate, gate, vec],
        out_specs=[pl.BlockSpec((ta, 2 * D), lambda i: (nb - 1 - i, 0)), pl.BlockSpec((8, D), lambda i: (0, 0)),
                   gate, gate],
        out_shape=[SDS(dz.shape, BF16), SDS((8, D), F32), SDS((HEADS, HEAD_DIM, HEAD_DIM), F32),
                   SDS((HEADS, HEAD_DIM, HEAD_DIM), F32)],
        scratch_shapes=[pltpu.VMEM((ta, D), F32)] * 3 + [pltpu.VMEM((8, D), F32)] * 3, aliases={0: 0},
        params=_cparams(("arbitrary",), 48),
        args=(dz, dya, z, z, hs, hs, xc, r, ig, conv_w, w_r, w_i, lam), comm=comm)


def _in_bwd(dz, w_in_g, x, dh1, g_mix, comm=None):
    s = x.shape[0]
    tm = min(TM_ROWS, s)
    nj = N_SLOT

    def body(dz_ref, w_ref, x_ref, dh1_ref, g_ref, dx_ref, dg_ref, acc_s):
        i, j = pl.program_id(0), pl.program_id(1)

        @pl.when(j == 0)
        def _():
            acc_s[...] = jnp.zeros_like(acc_s)

        @pl.when((i == 0) & (j == 0))
        def _():
            dg_ref[...] = jnp.zeros_like(dg_ref)

        acc_s[...] += _dot_nt(dz_ref[...], w_ref[...])

        @pl.when(j == nj - 1)
        def _():
            xv = x_ref[...]
            rstd = lax.rsqrt(jnp.mean(xv * xv, axis=-1, keepdims=True) + NORM_EPS)
            xh = xv * rstd
            dn = acc_s[...]
            dg_ref[...] += jnp.sum(dn * xh, axis=0, keepdims=True)
            dhat = dn * g_ref[...]
            dx_ref[...] = dh1_ref[...] + rstd * (dhat - xh * jnp.mean(dhat * xh, axis=-1, keepdims=True))

    row = pl.BlockSpec((tm, D), lambda i, j: (i, 0))
    vec = pl.BlockSpec((1, D), lambda i, j: (0, 0))
    return _call(
        body, name="in_bwd", grid=(s // tm, nj),
        in_specs=[pl.BlockSpec((tm, W_IN_COLS), lambda i, j: (i, j)),
                  pl.BlockSpec((None, D, W_IN_COLS), lambda i, j: (j, 0, 0)), row, row, vec],
        out_specs=[row, vec],
        out_shape=[SDS((s, D), F32), SDS((1, D), F32)],
        scratch_shapes=[pltpu.VMEM((tm, D), F32)],
        params=_cparams(("arbitrary", "arbitrary"), 48), args=(dz, w_in_g, x, dh1, g_mix), comm=comm)


def _wgrad(name, a, b, nblk, a_split, b_split):
    s = a.shape[0]
    ts = min(TM_ROWS, s)
    a_w = a.shape[1] // nblk if a_split else a.shape[1]
    b_w = b.shape[1] // nblk if b_split else b.shape[1]

    def body(a_ref, b_ref, o_ref, acc_s):
        t = pl.program_id(1)

        @pl.when(t == 0)
        def _():
            acc_s[...] = jnp.zeros_like(acc_s)

        acc_s[...] += _dot_tn(a_ref[...].astype(BF16), b_ref[...].astype(BF16))

        @pl.when(t == pl.num_programs(1) - 1)
        def _():
            o_ref[...] = acc_s[...].astype(BF16)

    return pl.pallas_call(
        body, name=name, grid=(nblk, s // ts),
        in_specs=[pl.BlockSpec((ts, a_w), (lambda k, t: (t, k)) if a_split else (lambda k, t: (t, 0))),
                  pl.BlockSpec((ts, b_w), (lambda k, t: (t, k)) if b_split else (lambda k, t: (t, 0)))],
        out_specs=pl.BlockSpec((None, a_w, b_w), lambda k, t: (k, 0, 0)),
        out_shape=SDS((nblk, a_w, b_w), BF16),
        scratch_shapes=[pltpu.VMEM((a_w, b_w), F32)],
        compiler_params=_cparams(("arbitrary", "arbitrary"), 48),
    )(a, b)


def _wgrad_t(name, a_t, b, nblk, a_split, b_split, tokens):
    s = b.shape[0]
    ts = min(tokens, s)
    a_w = a_t.shape[0] // nblk if a_split else a_t.shape[0]
    b_w = b.shape[1] // nblk if b_split else b.shape[1]

    def body(a_ref, b_ref, o_ref, acc_s):
        t = pl.program_id(1)

        @pl.when(t == 0)
        def _():
            acc_s[...] = jnp.zeros_like(acc_s)

        acc_s[...] += _dot(a_ref[...], b_ref[...].astype(BF16))

        @pl.when(t == pl.num_programs(1) - 1)
        def _():
            o_ref[...] = acc_s[...].astype(BF16)

    return pl.pallas_call(
        body, name=name, grid=(nblk, s // ts),
        in_specs=[pl.BlockSpec((a_w, ts), (lambda k, t: (k, t)) if a_split else (lambda k, t: (0, t))),
                  pl.BlockSpec((ts, b_w), (lambda k, t: (t, k)) if b_split else (lambda k, t: (t, 0)))],
        out_specs=pl.BlockSpec((None, a_w, b_w), lambda k, t: (k, 0, 0)),
        out_shape=SDS((nblk, a_w, b_w), BF16),
        scratch_shapes=[pltpu.VMEM((a_w, b_w), F32)],
        compiler_params=_cparams(("arbitrary", "arbitrary"), 48),
    )(a_t, b)


def _place():
    x, y, c = lax.axis_index("x"), lax.axis_index("y"), lax.axis_index("c")
    return x, y, c


def _other_chips(x, y):
    return [(x, 1 - y, 2 * x + 1 - y), (1 - x, y, 2 * (1 - x) + y), (1 - x, 1 - y, 2 * (1 - x) + 1 - y)]


class _Plan:
    def __init__(self, arrays, out_shape, sems, start, finish, middle=None, middle_at=6):
        self.arrays, self.out_shape, self.sems, self.start, self.finish = arrays, out_shape, sems, start, finish
        self.middle, self.middle_at = middle, middle_at


def _gather_plan(shards, middle_at=6):
    n = len(shards)

    def copies(ins, outs, sems):
        send_sems, recv_sems, local_sems = sems
        x, y, c = _place()
        chip = 2 * x + y
        me = 2 * chip + c
        sib = (x, y, 1 - c)
        chips = _other_chips(x, y)

        def rc(k, t, src, blk, to):
            return pltpu.make_async_remote_copy(
                src_ref=src, dst_ref=outs[t].at[blk], send_sem=send_sems.at[k * n + t],
                recv_sem=recv_sems.at[k * n + t], device_id=to, device_id_type=MESH)

        (yx, yy, y_chip), (xx, xy, x_chip), _ = chips
        local = [pltpu.make_async_copy(ins[t], outs[t].at[me], local_sems.at[t]) for t in range(n)]
        sends = ([rc(0, t, ins[t], me, sib) for t in range(n)] + [rc(1, t, ins[t], me, (yx, yy, c)) for t in range(n)]
                 + [rc(2, t, ins[t], me, (xx, xy, c)) for t in range(n)])
        passed = [[rc(4 + j, t, outs[t].at[2 * pc + c], 2 * pc + c, sib) for t in range(n)]
                  for j, (_, _, pc) in enumerate(chips)]
        relays = [[rc(3, t, outs[t].at[2 * y_chip + c], 2 * y_chip + c, (xx, xy, c)) for t in range(n)],
                  [rc(3, t, outs[t].at[2 * x_chip + c], 2 * x_chip + c, (yx, yy, c)) for t in range(n)]]
        return rc, local, sends, passed, relays, chips, chip, c, sib

    def start(ins, outs, sems):
        _, local, sends, _, _, _, _, _, _ = copies(ins, outs, sems)
        for cp in local + sends:
            cp.start()

    def middle(ins, outs, sems):
        rc, _, _, passed, relays, chips, _, c, sib = copies(ins, outs, sems)
        for j in range(2):
            for t in range(n):
                rc(1 + j, t, ins[t], 2 * chips[j][2] + c, sib).wait_recv()
        for j in range(2):
            for cp in passed[j]:
                cp.start()

            @pl.when(c == j)
            def _():
                for cp in relays[j]:
                    cp.start()

    def finish(ins, outs, sems):
        rc, local, sends, passed, relays, chips, chip, c, sib = copies(ins, outs, sems)
        far = 2 * chips[2][2] + c
        for t in range(n):
            rc(3, t, ins[t], far, sib).wait_recv()
        for cp in passed[2]:
            cp.start()
        for t in range(n):
            rc(0, t, ins[t], 2 * chip + 1 - c, sib).wait_recv()
        for j, (px, py, pc) in enumerate(chips):
            for t in range(n):
                rc(4 + j, t, ins[t], 2 * pc + 1 - c, sib).wait_recv()
        for cp in sends + passed[0] + passed[1] + passed[2]:
            cp.wait_send()
        for j in range(2):
            @pl.when(c == j)
            def _():
                for cp in relays[j]:
                    cp.wait_send()
        for cp in local:
            cp.wait()

    return _Plan(list(shards), [SDS((N_SLOT,) + tuple(a.shape), a.dtype) for a in shards],
                 [pltpu.SemaphoreType.DMA((7 * n,)), pltpu.SemaphoreType.DMA((7 * n,)),
                  pltpu.SemaphoreType.DMA((n,))], start, finish, middle, middle_at)


def _sibling_plan(grads, whole=()):
    n, m = len(grads), len(whole)

    def copies(ins, outs, sems):
        send_sems, recv_sems = sems
        x, y, c = _place()
        sib = (x, y, 1 - c)

        def rc(t, src, dst):
            return pltpu.make_async_remote_copy(src_ref=src, dst_ref=dst, send_sem=send_sems.at[t],
                                                recv_sem=recv_sems.at[t], device_id=sib, device_id_type=MESH)
        return rc, c

    def start(ins, outs, sems):
        rc, c = copies(ins, outs, sems)
        for t in range(n):
            for j in range(4):
                rc(t, ins[t].at[2 * j + 1 - c], outs[t].at[j]).start()
        for t in range(n, n + m):
            rc(t, ins[t], outs[t]).start()

    def finish(ins, outs, sems):
        rc, _ = copies(ins, outs, sems)
        for t in range(n):
            rc(t, ins[t].at[pl.ds(0, 4)], outs[t]).wait()
        for t in range(n, n + m):
            rc(t, ins[t], outs[t]).wait()

    return _Plan(list(grads) + list(whole),
                 [SDS((4,) + tuple(g.shape[1:]), g.dtype) for g in grads] + [SDS(a.shape, a.dtype) for a in whole],
                 [pltpu.SemaphoreType.DMA((n + m,)), pltpu.SemaphoreType.DMA((n + m,))], start, finish)


def _chips_plan(parts, whole=()):
    n, m = len(parts), len(whole)

    def src_of(ins, t, pc):
        return ins[t].at[pc] if t < n else ins[t]

    def local_copies(ins, outs, sems, chip):
        return [pltpu.make_async_copy(src_of(ins, t, chip), outs[t].at[chip], sems[2].at[t]) for t in range(n + m)]

    def start(ins, outs, sems):
        send_sems, recv_sems, _ = sems
        x, y, c = _place()
        chip = 2 * x + y
        for cp in local_copies(ins, outs, sems, chip):
            cp.start()
        for px, py, pc in _other_chips(x, y):
            for t in range(n + m):
                pltpu.make_async_remote_copy(src_ref=src_of(ins, t, pc), dst_ref=outs[t].at[chip],
                                             send_sem=send_sems.at[t], recv_sem=recv_sems.at[t],
                                             device_id=(px, py, c), device_id_type=MESH).start()

    def finish(ins, outs, sems):
        send_sems, recv_sems, _ = sems
        x, y, c = _place()
        for t in range(n + m):
            three = outs[t].at[pl.ds(0, 3)]
            pltpu.make_async_remote_copy(src_ref=three, dst_ref=three, send_sem=send_sems.at[t],
                                         recv_sem=recv_sems.at[t], device_id=(x, y, c), device_id_type=MESH).wait()
        for cp in local_copies(ins, outs, sems, 2 * x + y):
            cp.wait()

    return _Plan(list(parts) + list(whole),
                 [SDS(p.shape, p.dtype) for p in parts] + [SDS((4,) + tuple(a.shape), a.dtype) for a in whole],
                 [pltpu.SemaphoreType.DMA((n + m,)), pltpu.SemaphoreType.DMA((n + m,)),
                  pltpu.SemaphoreType.DMA((n + m,))], start, finish)


def _exchange_plan(arr):
    def peers(x, y, c):
        flip = lambda v, f: 1 - v if f else v
        return [(flip(x, fx), flip(y, fy), flip(c, fc))
                for fx in (0, 1) for fy in (0, 1) for fc in (0, 1) if fx or fy or fc]

    def start(ins, outs, sems):
        x, y, c = _place()
        me = 4 * x + 2 * y + c
        pltpu.make_async_copy(ins[0], outs[0].at[me], sems[2].at[0]).start()
        for to in peers(x, y, c):
            pltpu.make_async_remote_copy(src_ref=ins[0], dst_ref=outs[0].at[me], send_sem=sems[0].at[0],
                                         recv_sem=sems[1].at[0], device_id=to, device_id_type=MESH).start()

    def finish(ins, outs, sems):
        x, y, c = _place()
        seven = outs[0].at[pl.ds(0, 7)]
        pltpu.make_async_remote_copy(src_ref=seven, dst_ref=seven, send_sem=sems[0].at[0], recv_sem=sems[1].at[0],
                                     device_id=(x, y, c), device_id_type=MESH).wait()
        pltpu.make_async_copy(ins[0], outs[0].at[4 * x + 2 * y + c], sems[2].at[0]).wait()

    return _Plan([arr], [SDS((N_SLOT,) + tuple(arr.shape), arr.dtype)],
                 [pltpu.SemaphoreType.DMA((1,)), pltpu.SemaphoreType.DMA((1,)), pltpu.SemaphoreType.DMA((1,))],
                 start, finish)


def _join(*plans):
    def cut(seq, sizes):
        out, at = [], 0
        for k in sizes:
            out.append(seq[at:at + k])
            at += k
        return out

    n_arr = [len(p.arrays) for p in plans]
    n_sem = [len(p.sems) for p in plans]

    def start(ins, outs, sems):
        for p, i, o, s in zip(plans, cut(ins, n_arr), cut(outs, n_arr), cut(sems, n_sem)):
            p.start(i, o, s)

    def finish(ins, outs, sems):
        for p, i, o, s in zip(plans, cut(ins, n_arr), cut(outs, n_arr), cut(sems, n_sem)):
            p.finish(i, o, s)

    def middle(ins, outs, sems):
        for p, i, o, s in zip(plans, cut(ins, n_arr), cut(outs, n_arr), cut(sems, n_sem)):
            if p.middle is not None:
                p.middle(i, o, s)

    return _Plan([a for p in plans for a in p.arrays], [o for p in plans for o in p.out_shape],
                 [s for p in plans for s in p.sems], start, finish,
                 middle if any(p.middle is not None for p in plans) else None,
                 max(p.middle_at for p in plans))


def _run_plan(name, plan):
    k = len(plan.arrays)

    def body(*refs):
        ins, outs, sems = refs[:k], refs[k:2 * k], refs[2 * k:]
        plan.start(ins, outs, sems)
        if plan.middle is not None:
            plan.middle(ins, outs, sems)
        plan.finish(ins, outs, sems)

    return pl.pallas_call(
        body, name=name, in_specs=[ANY] * k, out_specs=[ANY] * k, out_shape=plan.out_shape,
        scratch_shapes=plan.sems, compiler_params=pltpu.CompilerParams(has_side_effects=True),
    )(*plan.arrays)


def _call(body, *, name, grid, in_specs, out_specs, out_shape, scratch_shapes, params, args, comm=None,
          aliases=None, prefetch=()):
    aliases = aliases or {}
    n_pre = len(prefetch)

    def launch(fn, ins_specs, outs_specs, outs_shape, scratch, operands):
        spec = pltpu.PrefetchScalarGridSpec(num_scalar_prefetch=n_pre, grid=grid, in_specs=ins_specs,
                                            out_specs=outs_specs, scratch_shapes=scratch)
        return pl.pallas_call(fn, name=name, grid_spec=spec, out_shape=outs_shape, compiler_params=params,
                              input_output_aliases=aliases)(*prefetch, *[_small_in_hbm(a) for a in operands])

    if comm is None:
        return list(launch(body, in_specs, out_specs, out_shape, scratch_shapes, args)), []
    n_in, n_out, n_scr, k = len(in_specs), len(out_specs), len(scratch_shapes), len(comm.arrays)

    def wrapped(*refs):
        pre, refs = refs[:n_pre], refs[n_pre:]
        ins = refs[:n_in]
        c_in = refs[n_in:n_in + k]
        outs = refs[n_in + k:n_in + k + n_out]
        c_out = refs[n_in + k + n_out:n_in + 2 * k + n_out]
        scr = refs[n_in + 2 * k + n_out:n_in + 2 * k + n_out + n_scr]
        sems = refs[n_in + 2 * k + n_out + n_scr:]
        step, steps = pl.program_id(0), grid[0]
        for d in range(1, len(grid)):
            step, steps = step * grid[d] + pl.program_id(d), steps * grid[d]

        @pl.when(step == 0)
        def _():
            comm.start(c_in, c_out, sems)

        if comm.middle is not None:
            @pl.when(step == (comm.middle_at * steps) // 8)
            def _():
                comm.middle(c_in, c_out, sems)

        body(*pre, *ins, *outs, *scr)

        @pl.when(step == steps - 1)
        def _():
            comm.finish(c_in, c_out, sems)

    res = launch(wrapped, list(in_specs) + [ANY] * k, list(out_specs) + [ANY] * k,
                 list(out_shape) + list(comm.out_shape), list(scratch_shapes) + list(comm.sems),
                 tuple(args) + tuple(comm.arrays))
    return list(res[:n_out]), list(res[n_out:])


def _row_tile(rows):
    for t in (512, 256, 128, 64, 32, 16, 8):
        if rows % t == 0:
            return t
    return rows


def _pair_sum(name, g8, recv4, core):
    _, rows, cols = recv4.shape
    tr = _row_tile(rows)
    g42 = g8.reshape(4, 2, rows, cols)

    def body(c_ref, g_ref, r_ref, o_ref):
        del c_ref
        o_ref[...] = (g_ref[...].astype(F32) + r_ref[...].astype(F32)).astype(o_ref.dtype)

    return pl.pallas_call(
        body, name=name,
        grid_spec=pltpu.PrefetchScalarGridSpec(
            num_scalar_prefetch=1, grid=(4, rows // tr),
            in_specs=[pl.BlockSpec((None, None, tr, cols), lambda j, i, c_ref: (j, c_ref[0], i, 0)),
                      pl.BlockSpec((None, tr, cols), lambda j, i, c_ref: (j, i, 0))],
            out_specs=pl.BlockSpec((None, tr, cols), lambda j, i, c_ref: (j, i, 0))),
        out_shape=SDS(recv4.shape, g8.dtype),
        compiler_params=_cparams(("arbitrary", "arbitrary"), 32),
    )(core, g42, recv4)


def _add2(name, a, b):
    rows, cols = a.shape
    tr = _row_tile(rows)

    def body(a_ref, b_ref, o_ref):
        o_ref[...] = a_ref[...] + b_ref[...]

    blk = pl.BlockSpec((tr, cols), lambda i: (i, 0))
    return pl.pallas_call(body, name=name, grid=(rows // tr,), in_specs=[blk, blk], out_specs=blk,
                          out_shape=SDS(a.shape, a.dtype),
                          compiler_params=_cparams(("arbitrary",), 32))(a, b)


def _sum_terms(name, terms):
    k, rows, cols = terms.shape
    tr = _row_tile(rows)

    def body(r_ref, o_ref):
        acc = r_ref[0]
        for q in range(1, k):
            acc = acc + r_ref[q]
        o_ref[...] = acc

    return pl.pallas_call(body, name=name, grid=(rows // tr,),
                          in_specs=[pl.BlockSpec((k, tr, cols), lambda i: (0, i, 0))],
                          out_specs=pl.BlockSpec((tr, cols), lambda i: (i, 0)),
                          out_shape=SDS((rows, cols), terms.dtype),
                          compiler_params=_cparams(("arbitrary",), 32))(terms)


def _adam_update(g, w, m, v):
    c1 = 1.0 / (1.0 - ADAM_B1 ** ADAM_STEP)
    c2 = 1.0 / (1.0 - ADAM_B2 ** ADAM_STEP)
    mn = ADAM_B1 * m + (1.0 - ADAM_B1) * g
    vn = ADAM_B2 * v + (1.0 - ADAM_B2) * (g * g)
    delta = (-ADAM_LR) * ((mn * c1) / (jnp.sqrt(vn * c2) + ADAM_EPS) + ADAM_WD * w)
    return delta, mn, vn


def _adamw_many(name, gs, ws, ms, vs):
    n = len(gs)

    def body(*refs):
        for p in range(n):
            g, w, m, v = (refs[q * n + p][...] for q in range(4))
            d, mn, vn = _adam_update(g, w, m, v)
            refs[4 * n + p][...] = d
            refs[5 * n + p][...] = mn
            refs[6 * n + p][...] = vn

    full = [pl.BlockSpec(w.shape, lambda i: (0, 0)) for w in ws]
    shapes = [SDS(w.shape, F32) for w in ws]
    res = pl.pallas_call(body, name=name, grid=(1,), in_specs=full * 4, out_specs=full * 3, out_shape=shapes * 3,
                         compiler_params=_cparams(("arbitrary",), 32),
                         )(*[_small_in_hbm(a) for a in (*gs, *ws, *ms, *vs)])
    return [(res[p], res[n + p], res[2 * n + p]) for p in range(n)]


def _adamw(name, terms, w, m, v):
    k, rows, cols = terms.shape
    tr = _row_tile(rows)

    def body(t_ref, w_ref, m_ref, v_ref, g_ref, d_ref, mo_ref, vo_ref):
        g = t_ref[0].astype(F32)
        for q in range(1, k):
            g = g + t_ref[q].astype(F32)
        g_ref[...] = g
        d_ref[...], mo_ref[...], vo_ref[...] = _adam_update(g, w_ref[...], m_ref[...], v_ref[...])

    blk = pl.BlockSpec((tr, cols), lambda i: (i, 0))
    return pl.pallas_call(body, name=name, grid=(rows // tr,),
                          in_specs=[pl.BlockSpec((k, tr, cols), lambda i: (0, i, 0)), blk, blk, blk],
                          out_specs=[blk] * 4, out_shape=[SDS((rows, cols), F32)] * 4,
                          compiler_params=_cparams(("arbitrary",), 40),
                          )(*[pltpu.with_memory_space_constraint(a, pltpu.HBM) for a in (terms, w, m, v)])


def kernel(x, norm_mix_g, w_in, conv_w, conv_b, w_rgate, b_rgate, w_igate, b_igate, lru_lambda, w_out_a, sgu_ln_g, sgu_ln_b, sgu_w_s, sgu_b_s, w_out_b, w_out, norm_mlp_g, w_up, w_down, norm_final_g, loss_target, m_norm_mix_g, m_w_in, m_conv_w, m_conv_b, m_w_rgate, m_b_rgate, m_w_igate, m_b_igate, m_lru_lambda, m_w_out_a, m_sgu_ln_g, m_sgu_ln_b, m_sgu_w_s, m_sgu_b_s, m_w_out_b, m_w_out, m_norm_mlp_g, m_w_up, m_w_down, m_norm_final_g, v_norm_mix_g, v_w_in, v_conv_w, v_conv_b, v_w_rgate, v_b_rgate, v_w_igate, v_b_igate, v_lru_lambda, v_w_out_a, v_sgu_ln_g, v_sgu_ln_b, v_sgu_w_s, v_sgu_b_s, v_w_out_b, v_w_out, v_norm_mlp_g, v_w_up, v_w_down, v_norm_final_g):
    cx, cy, cc = _place()
    me = 4 * cx + 2 * cy + cc
    core = jnp.reshape(cc, (1,)).astype(jnp.int32)
    xs = x[0]
    tgt = loss_target[0]
    s = xs.shape[0]

    gate_shard = jnp.stack([w_rgate[0], w_igate[0]]).astype(BF16).reshape(2 * HEADS * 32, HEAD_DIM)
    vec_shard = jnp.concatenate([conv_w[0], b_rgate[0], b_igate[0]], axis=1)
    vec_shard = jnp.pad(vec_shard, ((0, 4), (0, 256 - vec_shard.shape[1])))
    shards = [w_in[0].astype(BF16), w_out_a[0].astype(BF16), w_out_b[0].astype(BF16), w_out[0].astype(BF16),
              w_up[0].astype(BF16), w_down[0].astype(BF16), gate_shard, vec_shard]
    (z, n1_t, w_in_g), (gate_g, vec_g) = _in_proj(xs, norm_mix_g, shards[0], _slot_order(cx, cy, cc),
                                                comm=_gather_plan(shards[6:8]))
    gates = gate_g.reshape(N_SLOT, 2, HEADS, 32, HEAD_DIM).transpose(1, 2, 0, 3, 4).reshape(2, HEADS, HEAD_DIM, HEAD_DIM)
    w_r_f, w_i_f = gates[0], gates[1]
    conv_w_f = vec_g[:, 0:4, 0:128].transpose(1, 0, 2).reshape(CONV_K, D)
    b_r_f = vec_g[:, 0:4, 128:160].transpose(1, 0, 2).reshape(1, D)
    b_i_f = vec_g[:, 0:4, 160:192].transpose(1, 0, 2).reshape(1, D)
    b_s_t = jnp.transpose(sgu_b_s[0])

    (ya, hs, xc, r_gate, i_gate), (w_oa_g, w_ob_g, w_out_g, w_up_g) = _branch_a_fwd(
        z, conv_w_f, conv_b, w_r_f, b_r_f, w_i_f, b_i_f, lru_lambda, comm=_gather_plan(shards[1:5]))
    w_oa_f = w_oa_g.reshape(D, D)
    w_ob_f = w_ob_g.reshape(D, D)
    w_out_f = w_out_g.reshape(D, D)
    (yb,), _ = _branch_b_fwd(z, sgu_ln_g, sgu_ln_b, sgu_w_s[0], b_s_t)
    (pa, pb, merged, h1), (w_down_g,) = _merge_out(ya, yb, z, xs, w_oa_f, w_ob_f, w_out_f,
                                                   comm=_gather_plan(shards[5:6], middle_at=4))
    w_down_f = w_down_g.reshape(N_SLOT * FF_COLS, D)
    gf2 = norm_final_g.reshape(1, D)
    r_act, act_t, n2_t, dh2, loss_acc, d_gfin = _mlp_fwd(h1, norm_mlp_g, w_up_g, w_down_f, gf2, tgt)

    def pair(names, grads, recv):
        return [_pair_sum("pair_sum_" + nm, g, r, core) for nm, g, r in zip(names, grads, recv)]

    g_down = _wgrad_t("wgrad_down", act_t, dh2, N_SLOT // 2, True, False, 2048)
    g_down = g_down.reshape(N_SLOT, FF_COLS, D)
    (df, dh1, d_gmlp), (r_down,) = _mlp_bwd(dh2, r_act, w_down_f, w_up_g, h1, norm_mlp_g,
                                            comm=_sibling_plan([g_down]))
    (p_down,) = pair(["down"], [g_down], [r_down])
    g_up = _wgrad_t("wgrad_up", n2_t, df, N_SLOT, False, True, 4096)
    g_out = _wgrad("wgrad_out", merged, dh1, 1, False, False).reshape(N_SLOT, D // N_SLOT, D)
    (dz, dpa, dpb, dya, dyb), (got_down, r_up, r_out) = _merge_bwd(
        dh1, z, pa, pb, w_out_f, w_oa_f, w_ob_f, comm=_join(_chips_plan([p_down]), _sibling_plan([g_up, g_out])))
    p_up, p_out = pair(["up", "out"], [g_up, g_out], [r_up, r_out])
    g_oa = _wgrad("wgrad_out_a", ya, dpa, 1, False, False).reshape(N_SLOT, D // N_SLOT, D)
    g_ob = _wgrad("wgrad_out_b", yb, dpb, 1, False, False).reshape(N_SLOT, D // N_SLOT, D)
    (dz, d_ws, d_bs, d_ln), (got_up, r_oa, r_ob) = _branch_b_bwd(
        dz, dyb, z, sgu_ln_g, sgu_ln_b, sgu_w_s[0], b_s_t,
        comm=_join(_chips_plan([p_up]), _sibling_plan([g_oa, g_ob])))
    p_oa, p_ob = pair(["out_a", "out_b"], [g_oa, g_ob], [r_oa, r_ob])
    (dz, d_vec, d_wr, d_wi), (got_out, got_oa, got_ob) = _branch_a_bwd(
        dz, dya, z, hs, xc, r_gate, i_gate, conv_w_f, w_r_f, w_i_f, lru_lambda,
        comm=_chips_plan([p_out, p_oa, p_ob]))
    g_in = _wgrad_t("wgrad_in", n1_t, dz, N_SLOT, False, True, 4096)
    g_gate = jnp.stack([d_wr, d_wi]).reshape(2, HEADS, N_SLOT, 32, HEAD_DIM).transpose(2, 0, 1, 3, 4)
    g_gate = g_gate.reshape(N_SLOT, 2 * HEADS * 32, HEAD_DIM).astype(BF16)

    d_bs_row = jnp.pad(d_bs[:, :, 0].reshape(1, GROUPS * CHUNK), ((0, 0), (0, D - GROUPS * CHUNK)))
    vecs = jnp.concatenate([d_vec, jnp.concatenate([d_ln[0:2], d_gmlp, d_gfin, d_bs_row, jnp.zeros((3, D), F32)])])
    d_ws2 = d_ws.reshape(GROUPS * CHUNK, CHUNK)
    r_in, r_gate, r_vecs, r_ws = _run_plan("rs_sibling_in", _sibling_plan([g_in, g_gate], [vecs, d_ws2]))
    p_in, p_gate = pair(["in", "gate"], [g_in, g_gate], [r_in, r_gate])
    vecs_chip = _add2("pair_sum_vecs", vecs, r_vecs)
    ws_chip = _add2("pair_sum_ws", d_ws2, r_ws)
    (dx, d_gmix), (got_in, got_gate, got_vecs, got_ws) = _in_bwd(
        dz, w_in_g, xs, dh1, norm_mix_g, comm=_chips_plan([p_in, p_gate], [vecs_chip, ws_chip]))
    vecs_sum = _sum_terms("sum_vecs", got_vecs)
    last = jnp.concatenate([d_gmix, jnp.pad(loss_acc[0:1], ((0, 0), (0, D - 128))), jnp.zeros((6, D), F32)])
    (last_all,) = _run_plan("exchange_last", _exchange_plan(last))
    last_sum = _sum_terms("sum_last", last_all)
    loss = last_sum[1, 0]
    got = [got_in, got_oa, got_ob, got_out, got_up, got_down, got_gate]

    def step(nm, terms, w, m, v, rows, cols):
        g, d, mn, vn = _adamw("adamw_" + nm, terms.reshape(4, rows, cols), w.reshape(rows, cols),
                              m.reshape(rows, cols), v.reshape(rows, cols))
        return [a.reshape(w.shape) for a in (g, d, mn, vn)]

    o_in = step("in", got[0], w_in, m_w_in, v_w_in, D, W_IN_COLS)
    o_oa = step("out_a", got[1], w_out_a, m_w_out_a, v_w_out_a, D // N_SLOT, D)
    o_ob = step("out_b", got[2], w_out_b, m_w_out_b, v_w_out_b, D // N_SLOT, D)
    o_out = step("out", got[3], w_out, m_w_out, v_w_out, D // N_SLOT, D)
    o_up = step("up", got[4], w_up, m_w_up, v_w_up, D, FF_COLS)
    o_down = step("down", got[5], w_down, m_w_down, v_w_down, FF_COLS, D)
    gate_w = jnp.stack([w_rgate[0], w_igate[0]]).reshape(2 * HEADS * 32, HEAD_DIM)
    gate_m = jnp.stack([m_w_rgate[0], m_w_igate[0]]).reshape(2 * HEADS * 32, HEAD_DIM)
    gate_v = jnp.stack([v_w_rgate[0], v_w_igate[0]]).reshape(2 * HEADS * 32, HEAD_DIM)
    o_gate = _adamw("adamw_gate", got[6], gate_w, gate_m, gate_v)
    o_gate = [a.reshape(2, 1, HEADS, 32, HEAD_DIM) for a in o_gate]
    o_wr = [a[0] for a in o_gate]
    o_wi = [a[1] for a in o_gate]

    def own(full, width):
        return lax.dynamic_slice_in_dim(full, me * width, width, axis=1)

    small_g = {
        "norm_mix_g": last_sum[0:1], "conv_w": own(vecs_sum[0:4], 128), "conv_b": vecs_sum[4:5],
        "b_rgate": own(vecs_sum[5:6].reshape(HEADS, HEAD_DIM), 32),
        "b_igate": own(vecs_sum[6:7].reshape(HEADS, HEAD_DIM), 32),
        "lru_lambda": vecs_sum[7:8], "sgu_ln_g": vecs_sum[8:9], "sgu_ln_b": vecs_sum[9:10],
        "norm_mlp_g": vecs_sum[10:11], "norm_final_g": vecs_sum[11:12],
        "sgu_b_s": vecs_sum[12, 0:GROUPS * CHUNK].reshape(GROUPS, CHUNK),
    }
    small_w = {"norm_mix_g": (norm_mix_g, m_norm_mix_g, v_norm_mix_g), "conv_w": (conv_w, m_conv_w, v_conv_w),
               "conv_b": (conv_b, m_conv_b, v_conv_b), "b_rgate": (b_rgate, m_b_rgate, v_b_rgate),
               "b_igate": (b_igate, m_b_igate, v_b_igate), "lru_lambda": (lru_lambda, m_lru_lambda, v_lru_lambda),
               "sgu_ln_g": (sgu_ln_g, m_sgu_ln_g, v_sgu_ln_g), "sgu_ln_b": (sgu_ln_b, m_sgu_ln_b, v_sgu_ln_b),
               "norm_mlp_g": (norm_mlp_g, m_norm_mlp_g, v_norm_mlp_g),
               "norm_final_g": (norm_final_g, m_norm_final_g, v_norm_final_g),
               "sgu_b_s": (sgu_b_s, m_sgu_b_s, v_sgu_b_s), "sgu_w_s": (sgu_w_s, m_sgu_w_s, v_sgu_w_s)}
    order = list(small_g)
    as2d = lambda k, a: a.reshape(small_g[k].shape)
    upd = _adamw_many("adamw_small", [small_g[k] for k in order], *[[as2d(k, small_w[k][q]) for k in order]
                                                                     for q in range(3)])
    o_small = {k: [a.reshape(small_w[k][0].shape) for a in (small_g[k],) + u] for k, u in zip(order, upd)}
    ws3 = [a[0].reshape(GROUPS * CHUNK, CHUNK) for a in small_w.pop("sgu_w_s")]
    o_small["sgu_w_s"] = [a.reshape(sgu_w_s.shape) for a in _adamw("adamw_ws", got_ws, *ws3)]

    per_weight = {"norm_mix_g": o_small["norm_mix_g"], "w_in": o_in, "conv_w": o_small["conv_w"],
                  "conv_b": o_small["conv_b"], "w_rgate": o_wr, "b_rgate": o_small["b_rgate"], "w_igate": o_wi,
                  "b_igate": o_small["b_igate"], "lru_lambda": o_small["lru_lambda"], "w_out_a": o_oa,
                  "sgu_ln_g": o_small["sgu_ln_g"], "sgu_ln_b": o_small["sgu_ln_b"], "sgu_w_s": o_small["sgu_w_s"],
                  "sgu_b_s": o_small["sgu_b_s"], "w_out_b": o_ob, "w_out": o_out, "norm_mlp_g": o_small["norm_mlp_g"],
                  "w_up": o_up, "w_down": o_down, "norm_final_g": o_small["norm_final_g"]}
    names_w = list(per_weight)
    return (loss, dx[None], *[per_weight[k][0] for k in names_w], *[per_weight[k][1] for k in names_w],
            *[per_weight[k][2] for k in names_w], *[per_weight[k][3] for k in names_w])
```

```python
import jax
import jax.numpy as jnp
from jax import lax
from jax.experimental import pallas as pl
from jax.experimental.pallas import tpu as pltpu

F32 = jnp.float32
BF16 = jnp.bfloat16
SDS = jax.ShapeDtypeStruct
MESH = pl.DeviceIdType.MESH
ANY = pl.BlockSpec(memory_space=pl.ANY)

D = 1024
N_SLOT = 8
W_IN_COLS = 768
FF_COLS = 512
HEADS, HEAD_DIM = 4, 256
GROUPS, GROUP_DIM = 4, 256
CHUNK = 128
CONV_K = 4
NORM_EPS = 1e-6
LN_EPS = 1e-5
LRU_C = 8.0
ADAM_LR, ADAM_B1, ADAM_B2, ADAM_EPS, ADAM_WD, ADAM_STEP = 0.001, 0.9, 0.999, 1e-08, 0.01, 10

TM_ROWS = 1024
TM_MERGE = 512
T_BRANCH_A = 512
T_BRANCH_B = 256
MiB = 1024 * 1024
SMALL_OPERAND = 16 * 1024

_GELU_C = 0.7978845608028654
_GELU_A = 0.044715


def _small_in_hbm(a):
    return pltpu.with_memory_space_constraint(a, pltpu.HBM) if a.size <= SMALL_OPERAND else a


def _cparams(sem, vmem_mib):
    return pltpu.CompilerParams(dimension_semantics=sem, vmem_limit_bytes=vmem_mib * MiB)


def _gelu(x):
    t = jnp.tanh(_GELU_C * (x + _GELU_A * x * x * x))
    return 0.5 * x * (1.0 + t)


def _gelu_and_grad(x):
    x2 = x * x
    t = jnp.tanh(_GELU_C * x * (1.0 + _GELU_A * x2))
    g = 0.5 * x * (1.0 + t)
    dg = 0.5 * (1.0 + t) + 0.5 * x * (1.0 - t * t) * _GELU_C * (1.0 + 3.0 * _GELU_A * x2)
    return g, dg


def _softplus(x):
    return jnp.maximum(x, 0.0) + jnp.log1p(jnp.exp(-jnp.abs(x)))


def _dot(a, b):
    return jnp.dot(a, b, preferred_element_type=F32)


def _dot_nt(a, b):
    return lax.dot_general(a, b, (((1,), (1,)), ((), ())), preferred_element_type=F32)


def _dot_tn(a, b):
    return lax.dot_general(a, b, (((0,), (0,)), ((), ())), preferred_element_type=F32)


def _rows_shifted(prev8, cur, k):
    ext = jnp.concatenate([prev8, cur], axis=0)
    return pltpu.roll(ext, k, 0)[8:]


def _rows_advanced(cur, next8, k):
    t = cur.shape[0]
    ext = jnp.concatenate([cur, next8], axis=0)
    return pltpu.roll(ext, t + 8 - k, 0)[:t]


def _first_second(x, y, c):
    ny, nx, far = _other_chips(x, y)
    pick = lambda a, b: a * (1 - c) + b * c
    first = tuple(pick(a, b) for a, b in zip(ny, nx))
    second = tuple(pick(b, a) for a, b in zip(ny, nx))
    return first, second, far


def _slot_order(x, y, c):
    chip = 2 * x + y
    first, second, far = _first_second(x, y, c)
    order = [2 * chip + c, 2 * chip + 1 - c, 2 * first[2] + c, 2 * second[2] + 1 - c, 2 * second[2] + c,
             2 * first[2] + 1 - c, 2 * far[2] + c, 2 * far[2] + 1 - c]
    return jnp.stack(order).astype(jnp.int32)


def _in_proj(x, g_mix, w_in_own, order, comm=None):
    s = x.shape[0]
    tm = min(TM_ROWS, s)
    ni = s // tm

    def body(order_ref, x_ref, g_ref, own_ref, z_ref, nt_ref, wg_ref, n_s, w_s, send_sems, recv_sems, local_sems):
        j, i = pl.program_id(0), pl.program_id(1)
        px, py, c = _place()
        chip = 2 * px + py
        me = 2 * chip + c
        sib = (px, py, 1 - c)
        chips = _other_chips(px, py)

        def rc(k, src, blk, to):
            return pltpu.make_async_remote_copy(src_ref=src, dst_ref=w_s.at[blk], send_sem=send_sems.at[k],
                                                recv_sem=recv_sems.at[k], device_id=to, device_id_type=MESH)

        del chips
        first, second, far = _first_second(px, py, c)
        blocks = [2 * first[2] + c, 2 * second[2] + c, 2 * far[2] + c]
        own_in = pltpu.make_async_copy(own_ref, w_s.at[me], local_sems.at[0])
        to_first = rc(1, own_ref, me, (first[0], first[1], c))
        to_second = rc(2, own_ref, me, (second[0], second[1], c))
        relay = rc(3, w_s.at[blocks[0]], blocks[0], (second[0], second[1], c))
        sends = [rc(0, own_ref, me, sib), to_first, to_second, relay]
        passed = [rc(4 + q, w_s.at[blk], blk, sib) for q, blk in enumerate(blocks)]
        keep = pltpu.make_async_copy(w_s, wg_ref, local_sems.at[1])

        @pl.when((i == 0) & (j == 0))
        def _():
            own_in.start()
            sends[0].start()
            to_first.start()
            own_in.wait()

        @pl.when((i == 0) & (j == 1))
        def _():
            rc(0, own_ref, 2 * chip + 1 - c, sib).wait_recv()

        for q, blk in enumerate(blocks):
            @pl.when((i == 0) & (j == 2 + 2 * q))
            def _():
                rc(1 + q, own_ref, blk, sib).wait_recv()
                passed[q].start()
                if q == 0:
                    to_second.start()
                    relay.start()

            @pl.when((i == 0) & (j == 3 + 2 * q))
            def _():
                rc(4 + q, own_ref, order_ref[j], sib).wait_recv()

        rows = pl.ds(pl.multiple_of(i * tm, tm), tm)

        @pl.when(j == 0)
        def _():
            xv = x_ref[...]
            rstd = lax.rsqrt(jnp.mean(xv * xv, axis=-1, keepdims=True) + NORM_EPS)
            nb = (xv * rstd * g_ref[...]).astype(BF16)
            n_s[rows, :] = nb
            nt_ref[...] = nb.T

        z_ref[...] = _dot(n_s[rows, :], w_s[order_ref[j]]).astype(BF16)

        @pl.when((i == 0) & (j == N_SLOT - 1))
        def _():
            keep.start()

        @pl.when((i == ni - 1) & (j == N_SLOT - 1))
        def _():
            for cp in sends + passed:
                cp.wait_send()
            keep.wait()

    first_pass = lambda j, i, o: (jnp.where(j == 0, i, ni - 1), 0)
    (z, n1, w_in_g), extra = _call(
        body, name="in_proj", grid=(N_SLOT, ni), prefetch=(order,),
        in_specs=[pl.BlockSpec((tm, D), first_pass),
                  pl.BlockSpec((1, D), lambda j, i, o: (0, 0)), ANY],
        out_specs=[pl.BlockSpec((tm, W_IN_COLS), lambda j, i, o: (i, o[j])),
                   pl.BlockSpec((D, tm), lambda j, i, o: (0, jnp.where(j == 0, i, ni - 1))), ANY],
        out_shape=[SDS((s, N_SLOT * W_IN_COLS), BF16), SDS((D, s), BF16), SDS((N_SLOT, D, W_IN_COLS), BF16)],
        scratch_shapes=[pltpu.VMEM((s, D), BF16), pltpu.VMEM((N_SLOT, D, W_IN_COLS), BF16),
                        pltpu.SemaphoreType.DMA((7,)), pltpu.SemaphoreType.DMA((7,)), pltpu.SemaphoreType.DMA((2,))],
        params=_cparams(("arbitrary", "arbitrary"), 56), args=(x, g_mix, w_in_own), comm=comm)
    return (z, n1, w_in_g), extra


def _decay(r, sp_lam):
    log_a = (-LRU_C) * r * sp_lam
    a = jnp.exp(log_a)
    return a, jnp.sqrt(-jnp.tanh(log_a) * (a * a + 1.0))


def _lru_gates(xc, xcb, wr_ref, br, wi_ref, bi, sp_lam, a_s, b_s, r_ref, i_ref):
    for h in range(HEADS):
        sl = slice(h * HEAD_DIM, (h + 1) * HEAD_DIM)
        r = jax.nn.sigmoid(_dot(xcb[:, sl], wr_ref[h]) + br[:, sl])
        ig = jax.nn.sigmoid(_dot(xcb[:, sl], wi_ref[h]) + bi[:, sl])
        a, mult = _decay(r, sp_lam[:, sl])
        a_s[:, sl] = a
        b_s[:, sl] = xc[:, sl] * ig * mult
        r_ref[:, sl] = r.astype(BF16)
        i_ref[:, sl] = ig.astype(BF16)


def _conv_fwd(xa, prev8, cw, cb):
    xc = cb + cw[0:1, :] * xa
    for k in range(1, CONV_K):
        xc = xc + cw[k:k + 1, :] * _rows_shifted(prev8, xa, k)
    return xc


def _branch_a_fwd(z, conv_w, conv_b, w_r, b_r, w_i, b_i, lam, comm=None):
    s = z.shape[0]
    ta = min(T_BRANCH_A, s)
    per16 = ta // 16

    def body(xa_ref, xp_ref, ga_ref, cw_ref, cb_ref, wr_ref, br_ref, wi_ref, bi_ref, lam_ref,
             ya_ref, hs_ref, xc_ref, r_ref, i_ref, a_s, b_s, h_s, carry_s):
        i = pl.program_id(0)

        @pl.when(i == 0)
        def _():
            carry_s[...] = jnp.zeros_like(carry_s)

        xa = xa_ref[...].astype(F32)
        prev8 = jnp.where(i > 0, xp_ref[...].astype(F32)[8:16], 0.0)
        xc = _conv_fwd(xa, prev8, cw_ref[...], cb_ref[...])
        xcb = xc.astype(BF16)
        xc_ref[...] = xcb
        sp_lam = _softplus(-lam_ref[...])
        _lru_gates(xc, xcb, wr_ref, br_ref[...], wi_ref, bi_ref[...], sp_lam, a_s, b_s, r_ref, i_ref)

        row = lax.broadcasted_iota(jnp.int32, (8, D), 0)

        def group(g, carry):
            off = pl.multiple_of(g * 8, 8)
            a8 = a_s[pl.ds(off, 8), :]
            b8 = b_s[pl.ds(off, 8), :]
            for d in (1, 2, 4):
                a_sh = jnp.where(row >= d, pltpu.roll(a8, d, 0), 1.0)
                b_sh = jnp.where(row >= d, pltpu.roll(b8, d, 0), 0.0)
                b8 = a8 * b_sh + b8
                a8 = a8 * a_sh
            h8 = b8 + a8 * carry
            h_s[pl.ds(off, 8), :] = h8
            return jnp.broadcast_to(h8[7:8, :], (8, D))

        carry_s[...] = lax.fori_loop(0, ta // 8, group, carry_s[...])
        hs = h_s[...]
        hs_ref[...] = hs.astype(BF16)
        ya_ref[...] = (hs * _gelu(ga_ref[...].astype(F32))).astype(BF16)

    vec = pl.BlockSpec((1, D), lambda i: (0, 0))
    gate = pl.BlockSpec((HEADS, HEAD_DIM, HEAD_DIM), lambda i: (0, 0, 0))
    return _call(
        body, name="branch_a_fwd", grid=(s // ta,),
        in_specs=[pl.BlockSpec((ta, D), lambda i: (i, 0)),
                  pl.BlockSpec((16, D), lambda i: (jnp.maximum(i * per16 - 1, 0), 0)),
                  pl.BlockSpec((ta, D), lambda i: (i, 1)),
                  pl.BlockSpec((CONV_K, D), lambda i: (0, 0)), vec, gate, vec, gate, vec, vec],
        out_specs=[pl.BlockSpec((ta, D), lambda i: (i, 0))] * 5,
        out_shape=[SDS((s, D), BF16)] * 5,
        scratch_shapes=[pltpu.VMEM((ta, D), F32), pltpu.VMEM((ta, D), F32), pltpu.VMEM((ta, D), F32),
                        pltpu.VMEM((8, D), F32)],
        params=_cparams(("arbitrary",), 40), args=(z, z, z, conv_w, conv_b, w_r, b_r, w_i, b_i, lam), comm=comm)


def _sgu_common(ub, vb, lg, lb, with_grad):
    if with_grad:
        u, du = _gelu_and_grad(ub)
        v, dv = _gelu_and_grad(vb)
    else:
        u, v, du, dv = _gelu(ub), _gelu(vb), None, None
    mu = jnp.mean(v, axis=-1, keepdims=True)
    vc = v - mu
    rstd = lax.rsqrt(jnp.mean(vc * vc, axis=-1, keepdims=True) + LN_EPS)
    vhat = vc * rstd
    vln = vhat * lg + lb
    return u, du, dv, rstd, vhat, vln


def _masked_ws(ws_ref):
    t = lax.broadcasted_iota(jnp.int32, (CHUNK, CHUNK), 0)
    c = lax.broadcasted_iota(jnp.int32, (CHUNK, CHUNK), 1)
    keep = c <= t
    return [jnp.where(keep, ws_ref[g], 0.0).astype(BF16) for g in range(GROUPS)]


def _branch_b_fwd(z, ln_g, ln_b, w_s, b_s_t, comm=None):
    s = z.shape[0]
    tb = min(T_BRANCH_B, s)

    def body(ub_ref, vb_ref, lg_ref, lb_ref, ws_ref, bs_ref, yb_ref):
        u, _, _, _, _, vln = _sgu_common(ub_ref[...].astype(F32), vb_ref[...].astype(F32),
                                         lg_ref[...], lb_ref[...], False)
        vlnb = vln.astype(BF16)
        wm = _masked_ws(ws_ref)
        bs = bs_ref[...]
        for c in range(tb // CHUNK):
            rs = slice(c * CHUNK, (c + 1) * CHUNK)
            for g in range(GROUPS):
                cs = slice(g * GROUP_DIM, (g + 1) * GROUP_DIM)
                sp = _dot(wm[g], vlnb[rs, cs]) + bs[:, g:g + 1]
                yb_ref[rs, cs] = (u[rs, cs] * sp).astype(BF16)

    vec = pl.BlockSpec((1, D), lambda i: (0, 0))
    return _call(
        body, name="branch_b_fwd", grid=(s // tb,),
        in_specs=[pl.BlockSpec((tb, D), lambda i: (i, 2)), pl.BlockSpec((tb, D), lambda i: (i, 3)), vec, vec,
                  pl.BlockSpec((GROUPS, CHUNK, CHUNK), lambda i: (0, 0, 0)),
                  pl.BlockSpec((CHUNK, GROUPS), lambda i: (0, 0))],
        out_specs=[pl.BlockSpec((tb, D), lambda i: (i, 0))],
        out_shape=[SDS((s, D), BF16)], scratch_shapes=[],
        params=_cparams(("arbitrary",), 40), args=(z, z, ln_g, ln_b, w_s, b_s_t), comm=comm)


def _merge_out(ya, yb, z, x, w_oa, w_ob, w_out, comm=None):
    s = x.shape[0]
    tm = min(TM_MERGE, s)

    def body(ya_ref, yb_ref, ma_ref, mb_ref, x_ref, woa_ref, wob_ref, wo_ref, pa_ref, pb_ref, mg_ref, h1_ref):
        pa = _dot(ya_ref[...], woa_ref[...])
        pb = _dot(yb_ref[...], wob_ref[...])
        merged = (jax.nn.sigmoid(ma_ref[...].astype(F32)) * pa
                  + jax.nn.sigmoid(mb_ref[...].astype(F32)) * pb).astype(BF16)
        pa_ref[...] = pa.astype(BF16)
        pb_ref[...] = pb.astype(BF16)
        mg_ref[...] = merged
        h1_ref[...] = x_ref[...] + _dot(merged, wo_ref[...])

    row = pl.BlockSpec((tm, D), lambda i: (i, 0))
    wsp = pl.BlockSpec((D, D), lambda i: (0, 0))
    return _call(
        body, name="merge_out", grid=(s // tm,),
        in_specs=[row, row, pl.BlockSpec((tm, D), lambda i: (i, 4)), pl.BlockSpec((tm, D), lambda i: (i, 5)),
                  row, wsp, wsp, wsp],
        out_specs=[row, row, row, row],
        out_shape=[SDS((s, D), BF16), SDS((s, D), BF16), SDS((s, D), BF16), SDS((s, D), F32)], scratch_shapes=[],
        params=_cparams(("arbitrary",), 48), args=(ya, yb, z, z, x, w_oa, w_ob, w_out), comm=comm)


def _mlp_fwd(h1, g_mlp, w_up_g, w_down, g_fin, tgt):
    s = h1.shape[0]
    tm = min(TM_ROWS, s)
    nj = N_SLOT

    def body(h1_ref, gm_ref, wu_ref, wd_ref, gf_ref, t_ref, r_ref, at_ref, n2t_ref, dh2_ref, loss_ref, dgf_ref,
             n2_s, acc_s):
        i, j = pl.program_id(0), pl.program_id(1)

        @pl.when(j == 0)
        def _():
            hv = h1_ref[...]
            rstd = lax.rsqrt(jnp.mean(hv * hv, axis=-1, keepdims=True) + NORM_EPS)
            nb = (hv * rstd * gm_ref[...]).astype(BF16)
            n2_s[...] = nb
            n2t_ref[...] = nb.T
            acc_s[...] = jnp.zeros_like(acc_s)

        @pl.when((i == 0) & (j == 0))
        def _():
            loss_ref[...] = jnp.zeros_like(loss_ref)
            dgf_ref[...] = jnp.zeros_like(dgf_ref)

        r = jnp.maximum(_dot(n2_s[...], wu_ref[...]), 0.0)
        r_ref[...] = r.astype(BF16)
        act = (r * r).astype(BF16)
        at_ref[...] = act.T
        acc_s[...] += _dot(act, wd_ref[...])

        @pl.when(j == nj - 1)
        def _():
            h2 = h1_ref[...] + acc_s[...]
            rstd = lax.rsqrt(jnp.mean(h2 * h2, axis=-1, keepdims=True) + NORM_EPS)
            hh = h2 * rstd
            gf = gf_ref[...]
            e = hh * gf - t_ref[...]
            loss_ref[...] += jnp.sum(e * e) * (0.5 / D)
            dy = e * (1.0 / D)
            dgf_ref[...] += jnp.sum(dy * hh, axis=0, keepdims=True)
            dhh = dy * gf
            dh2_ref[...] = rstd * (dhh - hh * jnp.mean(dhh * hh, axis=-1, keepdims=True))

    row = pl.BlockSpec((tm, D), lambda i, j: (i, 0))
    vec = pl.BlockSpec((1, D), lambda i, j: (0, 0))
    return pl.pallas_call(
        body, name="mlp_fwd", grid=(s // tm, nj),
        in_specs=[row, vec, pl.BlockSpec((None, D, FF_COLS), lambda i, j: (j, 0, 0)),
                  pl.BlockSpec((FF_COLS, D), lambda i, j: (j, 0)), vec, row],
        out_specs=[pl.BlockSpec((tm, FF_COLS), lambda i, j: (i, j)), pl.BlockSpec((FF_COLS, tm), lambda i, j: (j, i)),
                   pl.BlockSpec((D, tm), lambda i, j: (0, i)), row, pl.BlockSpec((8, 128), lambda i, j: (0, 0)), vec],
        out_shape=[SDS((s, nj * FF_COLS), BF16), SDS((nj * FF_COLS, s), BF16), SDS((D, s), BF16), SDS((s, D), F32),
                   SDS((8, 128), F32), SDS((1, D), F32)],
        scratch_shapes=[pltpu.VMEM((tm, D), BF16), pltpu.VMEM((tm, D), F32)],
        compiler_params=_cparams(("arbitrary", "arbitrary"), 52),
    )(h1, _small_in_hbm(g_mlp), w_up_g, w_down, _small_in_hbm(g_fin), tgt)


def _mlp_bwd(dh2, r, w_down, w_up_g, h1, g_mlp, comm=None):
    s = h1.shape[0]
    tm = min(TM_ROWS, s)
    nj = N_SLOT

    def body(dh2_ref, r_ref, wd_ref, wu_ref, h1_ref, gm_ref, df_ref, dh1_ref, dgm_ref, dh2b_s, acc_s):
        i, j = pl.program_id(0), pl.program_id(1)

        @pl.when(j == 0)
        def _():
            dh2b_s[...] = dh2_ref[...].astype(BF16)
            acc_s[...] = jnp.zeros_like(acc_s)

        @pl.when((i == 0) & (j == 0))
        def _():
            dgm_ref[...] = jnp.zeros_like(dgm_ref)

        d_act = _dot_nt(dh2b_s[...], wd_ref[...])
        df = (d_act * (2.0 * r_ref[...].astype(F32))).astype(BF16)
        df_ref[...] = df
        acc_s[...] += _dot_nt(df, wu_ref[...])

        @pl.when(j == nj - 1)
        def _():
            hv = h1_ref[...]
            rstd = lax.rsqrt(jnp.mean(hv * hv, axis=-1, keepdims=True) + NORM_EPS)
            hh = hv * rstd
            dn2 = acc_s[...]
            dgm_ref[...] += jnp.sum(dn2 * hh, axis=0, keepdims=True)
            dhat = dn2 * gm_ref[...]
            dh1_ref[...] = dh2_ref[...] + rstd * (dhat - hh * jnp.mean(dhat * hh, axis=-1, keepdims=True))

    row = pl.BlockSpec((tm, D), lambda i, j: (i, 0))
    vec = pl.BlockSpec((1, D), lambda i, j: (0, 0))
    ffb = pl.BlockSpec((tm, FF_COLS), lambda i, j: (i, j))
    return _call(
        body, name="mlp_bwd", grid=(s // tm, nj),
        in_specs=[row, ffb, pl.BlockSpec((FF_COLS, D), lambda i, j: (j, 0)),
                  pl.BlockSpec((None, D, FF_COLS), lambda i, j: (j, 0, 0)), row, vec],
        out_specs=[ffb, row, vec],
        out_shape=[SDS((s, nj * FF_COLS), BF16), SDS((s, D), F32), SDS((1, D), F32)],
        scratch_shapes=[pltpu.VMEM((tm, D), BF16), pltpu.VMEM((tm, D), F32)],
        params=_cparams(("arbitrary", "arbitrary"), 52), args=(dh2, r, w_down, w_up_g, h1, g_mlp), comm=comm)


def _merge_bwd(dh1, z, pa, pb, w_out, w_oa, w_ob, comm=None):
    s = dh1.shape[0]
    tm = min(TM_MERGE, s)

    def body(dh1_ref, ma_ref, mb_ref, pa_ref, pb_ref, wo_ref, woa_ref, wob_ref,
             dz_ref, dpa_ref, dpb_ref, dya_ref, dyb_ref):
        dm = _dot_nt(dh1_ref[...].astype(BF16), wo_ref[...])
        sa = jax.nn.sigmoid(ma_ref[...].astype(F32))
        sb = jax.nn.sigmoid(mb_ref[...].astype(F32))
        dpa = (dm * sa).astype(BF16)
        dpb = (dm * sb).astype(BF16)
        dz_ref[:, 0:D] = (dm * pa_ref[...].astype(F32) * sa * (1.0 - sa)).astype(BF16)
        dz_ref[:, D:2 * D] = (dm * pb_ref[...].astype(F32) * sb * (1.0 - sb)).astype(BF16)
        dpa_ref[...] = dpa
        dpb_ref[...] = dpb
        dya_ref[...] = _dot_nt(dpa, woa_ref[...]).astype(BF16)
        dyb_ref[...] = _dot_nt(dpb, wob_ref[...]).astype(BF16)

    row = pl.BlockSpec((tm, D), lambda i: (i, 0))
    wsp = pl.BlockSpec((D, D), lambda i: (0, 0))
    return _call(
        body, name="merge_bwd", grid=(s // tm,),
        in_specs=[row, pl.BlockSpec((tm, D), lambda i: (i, 4)), pl.BlockSpec((tm, D), lambda i: (i, 5)),
                  row, row, wsp, wsp, wsp],
        out_specs=[pl.BlockSpec((tm, 2 * D), lambda i: (i, 2)), row, row, row, row],
        out_shape=[SDS((s, 6 * D), BF16)] + [SDS((s, D), BF16)] * 4, scratch_shapes=[],
        params=_cparams(("arbitrary",), 48), args=(dh1, z, z, pa, pb, w_out, w_oa, w_ob), comm=comm)


def _branch_b_bwd(dz, dyb, z, ln_g, ln_b, w_s, b_s_t, comm=None):
    s = z.shape[0]
    tb = min(T_BRANCH_B, s)

    def body(dz_in, dyb_ref, ub_ref, vb_ref, lg_ref, lb_ref, ws_ref, bs_ref,
             dz_ref, dws_ref, dbs_ref, dln_ref, du_s, dvln_s):
        del dz_in

        @pl.when(pl.program_id(0) == 0)
        def _():
            dws_ref[...] = jnp.zeros_like(dws_ref)
            dbs_ref[...] = jnp.zeros_like(dbs_ref)
            dln_ref[...] = jnp.zeros_like(dln_ref)

        lg = lg_ref[...]
        u, du, dv, rstd, vhat, vln = _sgu_common(ub_ref[...].astype(F32), vb_ref[...].astype(F32),
                                                 lg, lb_ref[...], True)
        vlnb = vln.astype(BF16)
        dyb_v = dyb_ref[...].astype(F32)
        wm = _masked_ws(ws_ref)
        keep = (lax.broadcasted_iota(jnp.int32, (CHUNK, CHUNK), 1)
                <= lax.broadcasted_iota(jnp.int32, (CHUNK, CHUNK), 0))
        bs = bs_ref[...]
        for c in range(tb // CHUNK):
            rs = slice(c * CHUNK, (c + 1) * CHUNK)
            for g in range(GROUPS):
                cs = slice(g * GROUP_DIM, (g + 1) * GROUP_DIM)
                v_blk = vlnb[rs, cs]
                sp = _dot(wm[g], v_blk) + bs[:, g:g + 1]
                d_sp = dyb_v[rs, cs] * u[rs, cs]
                d_spb = d_sp.astype(BF16)
                du_s[rs, cs] = dyb_v[rs, cs] * sp
                dvln_s[rs, cs] = _dot_tn(wm[g], d_spb)
                dws_ref[g] += jnp.where(keep, _dot_nt(d_spb, v_blk), 0.0)
                dbs_ref[g] += jnp.broadcast_to(jnp.sum(d_sp, axis=-1, keepdims=True), (CHUNK, CHUNK))
        dvln = dvln_s[...]
        dln_ref[0:1, :] += jnp.sum(dvln * vhat, axis=0, keepdims=True)
        dln_ref[1:2, :] += jnp.sum(dvln, axis=0, keepdims=True)
        dvh = dvln * lg
        d_v = rstd * (dvh - jnp.mean(dvh, axis=-1, keepdims=True)
                      - vhat * jnp.mean(dvh * vhat, axis=-1, keepdims=True))
        dz_ref[:, 0:D] = (du_s[...] * du).astype(BF16)
        dz_ref[:, D:2 * D] = (d_v * dv).astype(BF16)

    vec = pl.BlockSpec((1, D), lambda i: (0, 0))
    sq = pl.BlockSpec((GROUPS, CHUNK, CHUNK), lambda i: (0, 0, 0))
    return _call(
        body, name="branch_b_bwd", grid=(s // tb,),
        in_specs=[ANY, pl.BlockSpec((tb, D), lambda i: (i, 0)),
                  pl.BlockSpec((tb, D), lambda i: (i, 2)), pl.BlockSpec((tb, D), lambda i: (i, 3)), vec, vec, sq,
                  pl.BlockSpec((CHUNK, GROUPS), lambda i: (0, 0))],
        out_specs=[pl.BlockSpec((tb, 2 * D), lambda i: (i, 1)), sq, sq, pl.BlockSpec((8, D), lambda i: (0, 0))],
        out_shape=[SDS(dz.shape, BF16), SDS((GROUPS, CHUNK, CHUNK), F32), SDS((GROUPS, CHUNK, CHUNK), F32),
                   SDS((8, D), F32)],
        scratch_shapes=[pltpu.VMEM((tb, D), F32), pltpu.VMEM((tb, D), F32)], aliases={0: 0},
        params=_cparams(("arbitrary",), 40), args=(dz, dyb, z, z, ln_g, ln_b, w_s, b_s_t), comm=comm)


def _branch_a_bwd(dz, dya, z, hs, xc, r, ig, conv_w, w_r, w_i, lam, comm=None):
    s = z.shape[0]
    ta = min(T_BRANCH_A, s)
    nb = s // ta
    per16 = ta // 16

    def body(dz_in, dya_ref, xa_ref, ga_ref, hs_ref, hp_ref, xc_ref, r_ref, i_ref, cw_ref, wr_ref, wi_ref,
             lam_ref, dz_ref, vec_ref, dwr_ref, dwi_ref, a_s, b_s, h_s, dcar_s, acar_s, dxc_s):
        del dz_in
        i = pl.program_id(0)
        blk = nb - 1 - i

        @pl.when(i == 0)
        def _():
            dcar_s[...] = jnp.zeros_like(dcar_s)
            acar_s[...] = jnp.zeros_like(acar_s)
            dxc_s[...] = jnp.zeros_like(dxc_s)
            vec_ref[...] = jnp.zeros_like(vec_ref)
            dwr_ref[...] = jnp.zeros_like(dwr_ref)
            dwi_ref[...] = jnp.zeros_like(dwi_ref)

        cw = cw_ref[...]
        lam_v = lam_ref[...]
        xa = xa_ref[...].astype(F32)
        xcb = xc_ref[...]
        xc = xcb.astype(F32)
        sp_lam = _softplus(-lam_v)
        r_v, i_v = r_ref[...].astype(F32), i_ref[...].astype(F32)
        a_v, m_v = _decay(r_v, sp_lam)

        hs_v = hs_ref[...].astype(F32)
        hprev8 = jnp.where(blk > 0, hp_ref[...].astype(F32)[8:16], 0.0)
        h_m1 = _rows_shifted(hprev8, hs_v, 1)
        gg, dgg = _gelu_and_grad(ga_ref[...].astype(F32))
        dya_v = dya_ref[...].astype(F32)
        dz_ref[:, D:2 * D] = (dya_v * hs_v * dgg).astype(BF16)

        a_s[...] = _rows_advanced(a_v, acar_s[...], 1)
        b_s[...] = dya_v * gg

        row = lax.broadcasted_iota(jnp.int32, (8, D), 0)
        ng = ta // 8

        def group(gi, carry):
            off = pl.multiple_of((ng - 1 - gi) * 8, 8)
            c8 = a_s[pl.ds(off, 8), :]
            d8 = b_s[pl.ds(off, 8), :]
            for d in (1, 2, 4):
                c_sh = jnp.where(row < 8 - d, pltpu.roll(c8, 8 - d, 0), 1.0)
                d_sh = jnp.where(row < 8 - d, pltpu.roll(d8, 8 - d, 0), 0.0)
                d8 = c8 * d_sh + d8
                c8 = c8 * c_sh
            dh8 = d8 + c8 * carry
            h_s[pl.ds(off, 8), :] = dh8
            return jnp.broadcast_to(dh8[0:1, :], (8, D))

        dcar_s[...] = lax.fori_loop(0, ng, group, dcar_s[...])
        acar_s[...] = jnp.broadcast_to(a_v[0:1, :], (8, D))

        dbx = h_s[...]
        d_mult = dbx * xc * i_v
        d_loga = dbx * h_m1 * a_v - d_mult * (a_v * a_v) / m_v
        d_pr = d_loga * ((-LRU_C) * sp_lam) * r_v * (1.0 - r_v)
        d_pi = dbx * xc * m_v * i_v * (1.0 - i_v)
        vec_ref[7:8, :] += jnp.sum(d_loga * r_v, axis=0, keepdims=True) * (LRU_C * jax.nn.sigmoid(-lam_v))
        vec_ref[5:6, :] += jnp.sum(d_pr, axis=0, keepdims=True)
        vec_ref[6:7, :] += jnp.sum(d_pi, axis=0, keepdims=True)
        d_prb = d_pr.astype(BF16)
        d_pib = d_pi.astype(BF16)
        h_s[...] = dbx * i_v * m_v
        for h in range(HEADS):
            sl = slice(h * HEAD_DIM, (h + 1) * HEAD_DIM)
            h_s[:, sl] += _dot_nt(d_prb[:, sl], wr_ref[h]) + _dot_nt(d_pib[:, sl], wi_ref[h])
            dwr_ref[h] += _dot_tn(xcb[:, sl], d_prb[:, sl])
            dwi_ref[h] += _dot_tn(xcb[:, sl], d_pib[:, sl])
        d_xc = h_s[...]
        vec_ref[4:5, :] += jnp.sum(d_xc, axis=0, keepdims=True)
        vec_ref[0:1, :] += jnp.sum(d_xc * xa, axis=0, keepdims=True)
        d_xa = cw[0:1, :] * d_xc
        nxt = dxc_s[...]
        for k in range(1, CONV_K):
            ahead = _rows_advanced(d_xc, nxt, k)
            vec_ref[k:k + 1, :] += jnp.sum(ahead * xa, axis=0, keepdims=True)
            d_xa = d_xa + cw[k:k + 1, :] * ahead
        dz_ref[:, 0:D] = d_xa.astype(BF16)
        dxc_s[...] = d_xc[0:8, :]

    vec = pl.BlockSpec((1, D), lambda i: (0, 0))
    gate = pl.BlockSpec((HEADS, HEAD_DIM, HEAD_DIM), lambda i: (0, 0, 0))
    cur = lambda c: pl.BlockSpec((ta, D), lambda i: (nb - 1 - i, c))
    before = lambda c: pl.BlockSpec((16, D), lambda i: (jnp.maximum((nb - 1 - i) * per16 - 1, 0), c))
    return _call(
        body, name="branch_a_bwd", grid=(nb,),
        in_specs=[ANY, cur(0), cur(0), cur(1), cur(0), before(0), cur(0), cur(0), cur(0),
                  pl.BlockSpec((CONV_K, D), lambda i: (0, 0)), gate, gate, vec],
        out_specs=[pl.BlockSpec((ta, 2 * D), lambda i: (nb - 1 - i, 0)), pl.BlockSpec((8, D), lambda i: (0, 0)),
                   gate, gate],
        out_shape=[SDS(dz.shape, BF16), SDS((8, D), F32), SDS((HEADS, HEAD_DIM, HEAD_DIM), F32),
                   SDS((HEADS, HEAD_DIM, HEAD_DIM), F32)],
        scratch_shapes=[pltpu.VMEM((ta, D), F32)] * 3 + [pltpu.VMEM((8, D), F32)] * 3, aliases={0: 0},
        params=_cparams(("arbitrary",), 48),
        args=(dz, dya, z, z, hs, hs, xc, r, ig, conv_w, w_r, w_i, lam), comm=comm)


def _in_bwd(dz, w_in_g, x, dh1, g_mix, comm=None):
    s = x.shape[0]
    tm = min(TM_ROWS, s)
    nj = N_SLOT

    def body(dz_ref, w_ref, x_ref, dh1_ref, g_ref, dx_ref, dg_ref, acc_s):
        i, j = pl.program_id(0), pl.program_id(1)

        @pl.when(j == 0)
        def _():
            acc_s[...] = jnp.zeros_like(acc_s)

        @pl.when((i == 0) & (j == 0))
        def _():
            dg_ref[...] = jnp.zeros_like(dg_ref)

        acc_s[...] += _dot_nt(dz_ref[...], w_ref[...])

        @pl.when(j == nj - 1)
        def _():
            xv = x_ref[...]
            rstd = lax.rsqrt(jnp.mean(xv * xv, axis=-1, keepdims=True) + NORM_EPS)
            xh = xv * rstd
            dn = acc_s[...]
            dg_ref[...] += jnp.sum(dn * xh, axis=0, keepdims=True)
            dhat = dn * g_ref[...]
            dx_ref[...] = dh1_ref[...] + rstd * (dhat - xh * jnp.mean(dhat * xh, axis=-1, keepdims=True))

    row = pl.BlockSpec((tm, D), lambda i, j: (i, 0))
    vec = pl.BlockSpec((1, D), lambda i, j: (0, 0))
    return _call(
        body, name="in_bwd", grid=(s // tm, nj),
        in_specs=[pl.BlockSpec((tm, W_IN_COLS), lambda i, j: (i, j)),
                  pl.BlockSpec((None, D, W_IN_COLS), lambda i, j: (j, 0, 0)), row, row, vec],
        out_specs=[row, vec],
        out_shape=[SDS((s, D), F32), SDS((1, D), F32)],
        scratch_shapes=[pltpu.VMEM((tm, D), F32)],
        params=_cparams(("arbitrary", "arbitrary"), 48), args=(dz, w_in_g, x, dh1, g_mix), comm=comm)


def _wgrad(name, a, b, nblk, a_split, b_split):
    s = a.shape[0]
    ts = min(TM_ROWS, s)
    a_w = a.shape[1] // nblk if a_split else a.shape[1]
    b_w = b.shape[1] // nblk if b_split else b.shape[1]

    def body(a_ref, b_ref, o_ref, acc_s):
        t = pl.program_id(1)

        @pl.when(t == 0)
        def _():
            acc_s[...] = jnp.zeros_like(acc_s)

        acc_s[...] += _dot_tn(a_ref[...].astype(BF16), b_ref[...].astype(BF16))

        @pl.when(t == pl.num_programs(1) - 1)
        def _():
            o_ref[...] = acc_s[...].astype(BF16)

    return pl.pallas_call(
        body, name=name, grid=(nblk, s // ts),
        in_specs=[pl.BlockSpec((ts, a_w), (lambda k, t: (t, k)) if a_split else (lambda k, t: (t, 0))),
                  pl.BlockSpec((ts, b_w), (lambda k, t: (t, k)) if b_split else (lambda k, t: (t, 0)))],
        out_specs=pl.BlockSpec((None, a_w, b_w), lambda k, t: (k, 0, 0)),
        out_shape=SDS((nblk, a_w, b_w), BF16),
        scratch_shapes=[pltpu.VMEM((a_w, b_w), F32)],
        compiler_params=_cparams(("arbitrary", "arbitrary"), 48),
    )(a, b)


def _wgrad_t(name, a_t, b, nblk, a_split, b_split, tokens):
    s = b.shape[0]
    ts = min(tokens, s)
    a_w = a_t.shape[0] // nblk if a_split else a_t.shape[0]
    b_w = b.shape[1] // nblk if b_split else b.shape[1]

    def body(a_ref, b_ref, o_ref, acc_s):
        t = pl.program_id(1)

        @pl.when(t == 0)
        def _():
            acc_s[...] = jnp.zeros_like(acc_s)

        acc_s[...] += _dot(a_ref[...], b_ref[...].astype(BF16))

        @pl.when(t == pl.num_programs(1) - 1)
        def _():
            o_ref[...] = acc_s[...].astype(BF16)

    return pl.pallas_call(
        body, name=name, grid=(nblk, s // ts),
        in_specs=[pl.BlockSpec((a_w, ts), (lambda k, t: (k, t)) if a_split else (lambda k, t: (0, t))),
                  pl.BlockSpec((ts, b_w), (lambda k, t: (t, k)) if b_split else (lambda k, t: (t, 0)))],
        out_specs=pl.BlockSpec((None, a_w, b_w), lambda k, t: (k, 0, 0)),
        out_shape=SDS((nblk, a_w, b_w), BF16),
        scratch_shapes=[pltpu.VMEM((a_w, b_w), F32)],
        compiler_params=_cparams(("arbitrary", "arbitrary"), 48),
    )(a_t, b)


def _place():
    x, y, c = lax.axis_index("x"), lax.axis_index("y"), lax.axis_index("c")
    return x, y, c


def _other_chips(x, y):
    return [(x, 1 - y, 2 * x + 1 - y), (1 - x, y, 2 * (1 - x) + y), (1 - x, 1 - y, 2 * (1 - x) + 1 - y)]


class _Plan:
    def __init__(self, arrays, out_shape, sems, start, finish, middle=None, middle_at=6):
        self.arrays, self.out_shape, self.sems, self.start, self.finish = arrays, out_shape, sems, start, finish
        self.middle, self.middle_at = middle, middle_at


def _gather_plan(shards, middle_at=6):
    n = len(shards)

    def copies(ins, outs, sems):
        send_sems, recv_sems, local_sems = sems
        x, y, c = _place()
        chip = 2 * x + y
        me = 2 * chip + c
        sib = (x, y, 1 - c)
        chips = _other_chips(x, y)

        def rc(k, t, src, blk, to):
            return pltpu.make_async_remote_copy(
                src_ref=src, dst_ref=outs[t].at[blk], send_sem=send_sems.at[k * n + t],
                recv_sem=recv_sems.at[k * n + t], device_id=to, device_id_type=MESH)

        (yx, yy, y_chip), (xx, xy, x_chip), _ = chips
        local = [pltpu.make_async_copy(ins[t], outs[t].at[me], local_sems.at[t]) for t in range(n)]
        sends = ([rc(0, t, ins[t], me, sib) for t in range(n)] + [rc(1, t, ins[t], me, (yx, yy, c)) for t in range(n)]
                 + [rc(2, t, ins[t], me, (xx, xy, c)) for t in range(n)])
        passed = [[rc(4 + j, t, outs[t].at[2 * pc + c], 2 * pc + c, sib) for t in range(n)]
                  for j, (_, _, pc) in enumerate(chips)]
        relays = [[rc(3, t, outs[t].at[2 * y_chip + c], 2 * y_chip + c, (xx, xy, c)) for t in range(n)],
                  [rc(3, t, outs[t].at[2 * x_chip + c], 2 * x_chip + c, (yx, yy, c)) for t in range(n)]]
        return rc, local, sends, passed, relays, chips, chip, c, sib

    def start(ins, outs, sems):
        _, local, sends, _, _, _, _, _, _ = copies(ins, outs, sems)
        for cp in local + sends:
            cp.start()

    def middle(ins, outs, sems):
        rc, _, _, passed, relays, chips, _, c, sib = copies(ins, outs, sems)
        for j in range(2):
            for t in range(n):
                rc(1 + j, t, ins[t], 2 * chips[j][2] + c, sib).wait_recv()
        for j in range(2):
            for cp in passed[j]:
                cp.start()

            @pl.when(c == j)
            def _():
                for cp in relays[j]:
                    cp.start()

    def finish(ins, outs, sems):
        rc, local, sends, passed, relays, chips, chip, c, sib = copies(ins, outs, sems)
        far = 2 * chips[2][2] + c
        for t in range(n):
            rc(3, t, ins[t], far, sib).wait_recv()
        for cp in passed[2]:
            cp.start()
        for t in range(n):
            rc(0, t, ins[t], 2 * chip + 1 - c, sib).wait_recv()
        for j, (px, py, pc) in enumerate(chips):
            for t in range(n):
                rc(4 + j, t, ins[t], 2 * pc + 1 - c, sib).wait_recv()
        for cp in sends + passed[0] + passed[1] + passed[2]:
            cp.wait_send()
        for j in range(2):
            @pl.when(c == j)
            def _():
                for cp in relays[j]:
                    cp.wait_send()
        for cp in local:
            cp.wait()

    return _Plan(list(shards), [SDS((N_SLOT,) + tuple(a.shape), a.dtype) for a in shards],
                 [pltpu.SemaphoreType.DMA((7 * n,)), pltpu.SemaphoreType.DMA((7 * n,)),
                  pltpu.SemaphoreType.DMA((n,))], start, finish, middle, middle_at)


def _sibling_plan(grads, whole=()):
    n, m = len(grads), len(whole)

    def copies(ins, outs, sems):
        send_sems, recv_sems = sems
        x, y, c = _place()
        sib = (x, y, 1 - c)

        def rc(t, src, dst):
            return pltpu.make_async_remote_copy(src_ref=src, dst_ref=dst, send_sem=send_sems.at[t],
                                                recv_sem=recv_sems.at[t], device_id=sib, device_id_type=MESH)
        return rc, c

    def start(ins, outs, sems):
        rc, c = copies(ins, outs, sems)
        for t in range(n):
            for j in range(4):
                rc(t, ins[t].at[2 * j + 1 - c], outs[t].at[j]).start()
        for t in range(n, n + m):
            rc(t, ins[t], outs[t]).start()

    def finish(ins, outs, sems):
        rc, _ = copies(ins, outs, sems)
        for t in range(n):
            rc(t, ins[t].at[pl.ds(0, 4)], outs[t]).wait()
        for t in range(n, n + m):
            rc(t, ins[t], outs[t]).wait()

    return _Plan(list(grads) + list(whole),
                 [SDS((4,) + tuple(g.shape[1:]), g.dtype) for g in grads] + [SDS(a.shape, a.dtype) for a in whole],
                 [pltpu.SemaphoreType.DMA((n + m,)), pltpu.SemaphoreType.DMA((n + m,))], start, finish)


def _chips_plan(parts, whole=()):
    n, m = len(parts), len(whole)

    def src_of(ins, t, pc):
        return ins[t].at[pc] if t < n else ins[t]

    def local_copies(ins, outs, sems, chip):
        return [pltpu.make_async_copy(src_of(ins, t, chip), outs[t].at[chip], sems[2].at[t]) for t in range(n + m)]

    def start(ins, outs, sems):
        send_sems, recv_sems, _ = sems
        x, y, c = _place()
        chip = 2 * x + y
        for cp in local_copies(ins, outs, sems, chip):
            cp.start()
        for px, py, pc in _other_chips(x, y):
            for t in range(n + m):
                pltpu.make_async_remote_copy(src_ref=src_of(ins, t, pc), dst_ref=outs[t].at[chip],
                                             send_sem=send_sems.at[t], recv_sem=recv_sems.at[t],
                                             device_id=(px, py, c), device_id_type=MESH).start()

    def finish(ins, outs, sems):
        send_sems, recv_sems, _ = sems
        x, y, c = _place()
        for t in range(n + m):
            three = outs[t].at[pl.ds(0, 3)]
            pltpu.make_async_remote_copy(src_ref=three, dst_ref=three, send_sem=send_sems.at[t],
                                         recv_sem=recv_sems.at[t], device_id=(x, y, c), device_id_type=MESH).wait()
        for cp in local_copies(ins, outs, sems, 2 * x + y):
            cp.wait()

    return _Plan(list(parts) + list(whole),
                 [SDS(p.shape, p.dtype) for p in parts] + [SDS((4,) + tuple(a.shape), a.dtype) for a in whole],
                 [pltpu.SemaphoreType.DMA((n + m,)), pltpu.SemaphoreType.DMA((n + m,)),
                  pltpu.SemaphoreType.DMA((n + m,))], start, finish)


def _exchange_plan(arr):
    def peers(x, y, c):
        flip = lambda v, f: 1 - v if f else v
        return [(flip(x, fx), flip(y, fy), flip(c, fc))
                for fx in (0, 1) for fy in (0, 1) for fc in (0, 1) if fx or fy or fc]

    def start(ins, outs, sems):
        x, y, c = _place()
        me = 4 * x + 2 * y + c
        pltpu.make_async_copy(ins[0], outs[0].at[me], sems[2].at[0]).start()
        for to in peers(x, y, c):
            pltpu.make_async_remote_copy(src_ref=ins[0], dst_ref=outs[0].at[me], send_sem=sems[0].at[0],
                                         recv_sem=sems[1].at[0], device_id=to, device_id_type=MESH).start()

    def finish(ins, outs, sems):
        x, y, c = _place()
        seven = outs[0].at[pl.ds(0, 7)]
        pltpu.make_async_remote_copy(src_ref=seven, dst_ref=seven, send_sem=sems[0].at[0], recv_sem=sems[1].at[0],
                                     device_id=(x, y, c), device_id_type=MESH).wait()
        pltpu.make_async_copy(ins[0], outs[0].at[4 * x + 2 * y + c], sems[2].at[0]).wait()

    return _Plan([arr], [SDS((N_SLOT,) + tuple(arr.shape), arr.dtype)],
                 [pltpu.SemaphoreType.DMA((1,)), pltpu.SemaphoreType.DMA((1,)), pltpu.SemaphoreType.DMA((1,))],
                 start, finish)


def _join(*plans):
    def cut(seq, sizes):
        out, at = [], 0
        for k in sizes:
            out.append(seq[at:at + k])
            at += k
        return out

    n_arr = [len(p.arrays) for p in plans]
    n_sem = [len(p.sems) for p in plans]

    def start(ins, outs, sems):
        for p, i, o, s in zip(plans, cut(ins, n_arr), cut(outs, n_arr), cut(sems, n_sem)):
            p.start(i, o, s)

    def finish(ins, outs, sems):
        for p, i, o, s in zip(plans, cut(ins, n_arr), cut(outs, n_arr), cut(sems, n_sem)):
            p.finish(i, o, s)

    def middle(ins, outs, sems):
        for p, i, o, s in zip(plans, cut(ins, n_arr), cut(outs, n_arr), cut(sems, n_sem)):
            if p.middle is not None:
                p.middle(i, o, s)

    return _Plan([a for p in plans for a in p.arrays], [o for p in plans for o in p.out_shape],
                 [s for p in plans for s in p.sems], start, finish,
                 middle if any(p.middle is not None for p in plans) else None,
                 max(p.middle_at for p in plans))


def _run_plan(name, plan):
    k = len(plan.arrays)

    def body(*refs):
        ins, outs, sems = refs[:k], refs[k:2 * k], refs[2 * k:]
        plan.start(ins, outs, sems)
        if plan.middle is not None:
            plan.middle(ins, outs, sems)
        plan.finish(ins, outs, sems)

    return pl.pallas_call(
        body, name=name, in_specs=[ANY] * k, out_specs=[ANY] * k, out_shape=plan.out_shape,
        scratch_shapes=plan.sems, compiler_params=pltpu.CompilerParams(has_side_effects=True),
    )(*plan.arrays)


def _call(body, *, name, grid, in_specs, out_specs, out_shape, scratch_shapes, params, args, comm=None,
          aliases=None, prefetch=()):
    aliases = aliases or {}
    n_pre = len(prefetch)

    def launch(fn, ins_specs, outs_specs, outs_shape, scratch, operands):
        spec = pltpu.PrefetchScalarGridSpec(num_scalar_prefetch=n_pre, grid=grid, in_specs=ins_specs,
                                            out_specs=outs_specs, scratch_shapes=scratch)
        return pl.pallas_call(fn, name=name, grid_spec=spec, out_shape=outs_shape, compiler_params=params,
                              input_output_aliases=aliases)(*prefetch, *[_small_in_hbm(a) for a in operands])

    if comm is None:
        return list(launch(body, in_specs, out_specs, out_shape, scratch_shapes, args)), []
    n_in, n_out, n_scr, k = len(in_specs), len(out_specs), len(scratch_shapes), len(comm.arrays)

    def wrapped(*refs):
        pre, refs = refs[:n_pre], refs[n_pre:]
        ins = refs[:n_in]
        c_in = refs[n_in:n_in + k]
        outs = refs[n_in + k:n_in + k + n_out]
        c_out = refs[n_in + k + n_out:n_in + 2 * k + n_out]
        scr = refs[n_in + 2 * k + n_out:n_in + 2 * k + n_out + n_scr]
        sems = refs[n_in + 2 * k + n_out + n_scr:]
        step, steps = pl.program_id(0), grid[0]
        for d in range(1, len(grid)):
            step, steps = step * grid[d] + pl.program_id(d), steps * grid[d]

        @pl.when(step == 0)
        def _():
            comm.start(c_in, c_out, sems)

        if comm.middle is not None:
            @pl.when(step == (comm.middle_at * steps) // 8)
            def _():
                comm.middle(c_in, c_out, sems)

        body(*pre, *ins, *outs, *scr)

        @pl.when(step == steps - 1)
        def _():
            comm.finish(c_in, c_out, sems)

    res = launch(wrapped, list(in_specs) + [ANY] * k, list(out_specs) + [ANY] * k,
                 list(out_shape) + list(comm.out_shape), list(scratch_shapes) + list(comm.sems),
                 tuple(args) + tuple(comm.arrays))
    return list(res[:n_out]), list(res[n_out:])


def _row_tile(rows):
    for t in (512, 256, 128, 64, 32, 16, 8):
        if rows % t == 0:
            return t
    return rows


def _pair_sum(name, g8, recv4, core):
    _, rows, cols = recv4.shape
    tr = _row_tile(rows)
    g42 = g8.reshape(4, 2, rows, cols)

    def body(c_ref, g_ref, r_ref, o_ref):
        del c_ref
        o_ref[...] = (g_ref[...].astype(F32) + r_ref[...].astype(F32)).astype(o_ref.dtype)

    return pl.pallas_call(
        body, name=name,
        grid_spec=pltpu.PrefetchScalarGridSpec(
            num_scalar_prefetch=1, grid=(4, rows // tr),
            in_specs=[pl.BlockSpec((None, None, tr, cols), lambda j, i, c_ref: (j, c_ref[0], i, 0)),
                      pl.BlockSpec((None, tr, cols), lambda j, i, c_ref: (j, i, 0))],
            out_specs=pl.BlockSpec((None, tr, cols), lambda j, i, c_ref: (j, i, 0))),
        out_shape=SDS(recv4.shape, g8.dtype),
        compiler_params=_cparams(("arbitrary", "arbitrary"), 32),
    )(core, g42, recv4)


def _add2(name, a, b):
    rows, cols = a.shape
    tr = _row_tile(rows)

    def body(a_ref, b_ref, o_ref):
        o_ref[...] = a_ref[...] + b_ref[...]

    blk = pl.BlockSpec((tr, cols), lambda i: (i, 0))
    return pl.pallas_call(body, name=name, grid=(rows // tr,), in_specs=[blk, blk], out_specs=blk,
                          out_shape=SDS(a.shape, a.dtype),
                          compiler_params=_cparams(("arbitrary",), 32))(a, b)


def _sum_terms(name, terms):
    k, rows, cols = terms.shape
    tr = _row_tile(rows)

    def body(r_ref, o_ref):
        acc = r_ref[0]
        for q in range(1, k):
            acc = acc + r_ref[q]
        o_ref[...] = acc

    return pl.pallas_call(body, name=name, grid=(rows // tr,),
                          in_specs=[pl.BlockSpec((k, tr, cols), lambda i: (0, i, 0))],
                          out_specs=pl.BlockSpec((tr, cols), lambda i: (i, 0)),
                          out_shape=SDS((rows, cols), terms.dtype),
                          compiler_params=_cparams(("arbitrary",), 32))(terms)


def _adam_update(g, w, m, v):
    c1 = 1.0 / (1.0 - ADAM_B1 ** ADAM_STEP)
    c2 = 1.0 / (1.0 - ADAM_B2 ** ADAM_STEP)
    mn = ADAM_B1 * m + (1.0 - ADAM_B1) * g
    vn = ADAM_B2 * v + (1.0 - ADAM_B2) * (g * g)
    delta = (-ADAM_LR) * ((mn * c1) / (jnp.sqrt(vn * c2) + ADAM_EPS) + ADAM_WD * w)
    return delta, mn, vn


def _adamw_many(name, gs, ws, ms, vs):
    n = len(gs)

    def body(*refs):
        for p in range(n):
            g, w, m, v = (refs[q * n + p][...] for q in range(4))
            d, mn, vn = _adam_update(g, w, m, v)
            refs[4 * n + p][...] = d
            refs[5 * n + p][...] = mn
            refs[6 * n + p][...] = vn

    full = [pl.BlockSpec(w.shape, lambda i: (0, 0)) for w in ws]
    shapes = [SDS(w.shape, F32) for w in ws]
    res = pl.pallas_call(body, name=name, grid=(1,), in_specs=full * 4, out_specs=full * 3, out_shape=shapes * 3,
                         compiler_params=_cparams(("arbitrary",), 32),
                         )(*[_small_in_hbm(a) for a in (*gs, *ws, *ms, *vs)])
    return [(res[p], res[n + p], res[2 * n + p]) for p in range(n)]


def _adamw(name, terms, w, m, v):
    k, rows, cols = terms.shape
    tr = _row_tile(rows)

    def body(t_ref, w_ref, m_ref, v_ref, g_ref, d_ref, mo_ref, vo_ref):
        g = t_ref[0].astype(F32)
        for q in range(1, k):
            g = g + t_ref[q].astype(F32)
        g_ref[...] = g
        d_ref[...], mo_ref[...], vo_ref[...] = _adam_update(g, w_ref[...], m_ref[...], v_ref[...])

    blk = pl.BlockSpec((tr, cols), lambda i: (i, 0))
    return pl.pallas_call(body, name=name, grid=(rows // tr,),
                          in_specs=[pl.BlockSpec((k, tr, cols), lambda i: (0, i, 0)), blk, blk, blk],
                          out_specs=[blk] * 4, out_shape=[SDS((rows, cols), F32)] * 4,
                          compiler_params=_cparams(("arbitrary",), 40),
                          )(*[pltpu.with_memory_space_constraint(a, pltpu.HBM) for a in (terms, w, m, v)])


def kernel(x, norm_mix_g, w_in, conv_w, conv_b, w_rgate, b_rgate, w_igate, b_igate, lru_lambda, w_out_a, sgu_ln_g, sgu_ln_b, sgu_w_s, sgu_b_s, w_out_b, w_out, norm_mlp_g, w_up, w_down, norm_final_g, loss_target, m_norm_mix_g, m_w_in, m_conv_w, m_conv_b, m_w_rgate, m_b_rgate, m_w_igate, m_b_igate, m_lru_lambda, m_w_out_a, m_sgu_ln_g, m_sgu_ln_b, m_sgu_w_s, m_sgu_b_s, m_w_out_b, m_w_out, m_norm_mlp_g, m_w_up, m_w_down, m_norm_final_g, v_norm_mix_g, v_w_in, v_conv_w, v_conv_b, v_w_rgate, v_b_rgate, v_w_igate, v_b_igate, v_lru_lambda, v_w_out_a, v_sgu_ln_g, v_sgu_ln_b, v_sgu_w_s, v_sgu_b_s, v_w_out_b, v_w_out, v_norm_mlp_g, v_w_up, v_w_down, v_norm_final_g):
    cx, cy, cc = _place()
    me = 4 * cx + 2 * cy + cc
    core = jnp.reshape(cc, (1,)).astype(jnp.int32)
    xs = x[0]
    tgt = loss_target[0]

    gate_shard = jnp.stack([w_rgate[0], w_igate[0]]).astype(BF16).reshape(2 * HEADS * 32, HEAD_DIM)
    vec_shard = jnp.concatenate([conv_w[0], b_rgate[0], b_igate[0]], axis=1)
    vec_shard = jnp.pad(vec_shard, ((0, 4), (0, 256 - vec_shard.shape[1])))
    shards = [w_in[0].astype(BF16), w_out_a[0].astype(BF16), w_out_b[0].astype(BF16), w_out[0].astype(BF16),
              w_up[0].astype(BF16), w_down[0].astype(BF16), gate_shard, vec_shard]
    (z, n1_t, w_in_g), (gate_g, vec_g, w_oa_g, w_ob_g, w_out_g) = _in_proj(
        xs, norm_mix_g, shards[0], _slot_order(cx, cy, cc), comm=_gather_plan(shards[6:8] + shards[1:4]))
    gates = gate_g.reshape(N_SLOT, 2, HEADS, 32, HEAD_DIM).transpose(1, 2, 0, 3, 4).reshape(2, HEADS, HEAD_DIM, HEAD_DIM)
    w_r_f, w_i_f = gates[0], gates[1]
    conv_w_f = vec_g[:, 0:4, 0:128].transpose(1, 0, 2).reshape(CONV_K, D)
    b_r_f = vec_g[:, 0:4, 128:160].transpose(1, 0, 2).reshape(1, D)
    b_i_f = vec_g[:, 0:4, 160:192].transpose(1, 0, 2).reshape(1, D)
    b_s_t = jnp.transpose(sgu_b_s[0])

    (ya, hs, xc, r_gate, i_gate), (w_up_g,) = _branch_a_fwd(
        z, conv_w_f, conv_b, w_r_f, b_r_f, w_i_f, b_i_f, lru_lambda, comm=_gather_plan(shards[4:5]))
    w_oa_f = w_oa_g.reshape(D, D)
    w_ob_f = w_ob_g.reshape(D, D)
    w_out_f = w_out_g.reshape(D, D)
    (yb,), _ = _branch_b_fwd(z, sgu_ln_g, sgu_ln_b, sgu_w_s[0], b_s_t)
    (pa, pb, merged, h1), (w_down_g,) = _merge_out(ya, yb, z, xs, w_oa_f, w_ob_f, w_out_f,
                                                   comm=_gather_plan(shards[5:6], middle_at=4))
    w_down_f = w_down_g.reshape(N_SLOT * FF_COLS, D)
    r_act, act_t, n2_t, dh2, loss_acc, d_gfin = _mlp_fwd(h1, norm_mlp_g, w_up_g, w_down_f,
                                                         norm_final_g.reshape(1, D), tgt)

    def pair(names, grads, recv):
        return [_pair_sum("pair_sum_" + nm, g, r, core) for nm, g, r in zip(names, grads, recv)]

    g_down = _wgrad_t("wgrad_down", act_t, dh2, N_SLOT // 2, True, False, 2048)
    g_down = g_down.reshape(N_SLOT, FF_COLS, D)
    (df, dh1, d_gmlp), (r_down,) = _mlp_bwd(dh2, r_act, w_down_f, w_up_g, h1, norm_mlp_g,
                                            comm=_sibling_plan([g_down]))
    (p_down,) = pair(["down"], [g_down], [r_down])
    g_up = _wgrad_t("wgrad_up", n2_t, df, N_SLOT, False, True, 4096)
    g_out = _wgrad("wgrad_out", merged, dh1, 1, False, False).reshape(N_SLOT, D // N_SLOT, D)
    (dz, dpa, dpb, dya, dyb), (got_down, r_up, r_out) = _merge_bwd(
        dh1, z, pa, pb, w_out_f, w_oa_f, w_ob_f, comm=_join(_chips_plan([p_down]), _sibling_plan([g_up, g_out])))
    p_up, p_out = pair(["up", "out"], [g_up, g_out], [r_up, r_out])
    g_oa = _wgrad("wgrad_out_a", ya, dpa, 1, False, False).reshape(N_SLOT, D // N_SLOT, D)
    g_ob = _wgrad("wgrad_out_b", yb, dpb, 1, False, False).reshape(N_SLOT, D // N_SLOT, D)
    (dz, d_ws, d_bs, d_ln), (got_up, r_oa, r_ob) = _branch_b_bwd(
        dz, dyb, z, sgu_ln_g, sgu_ln_b, sgu_w_s[0], b_s_t,
        comm=_join(_chips_plan([p_up]), _sibling_plan([g_oa, g_ob])))
    p_oa, p_ob = pair(["out_a", "out_b"], [g_oa, g_ob], [r_oa, r_ob])
    (dz, d_vec, d_wr, d_wi), (got_out, got_oa, got_ob) = _branch_a_bwd(
        dz, dya, z, hs, xc, r_gate, i_gate, conv_w_f, w_r_f, w_i_f, lru_lambda,
        comm=_chips_plan([p_out, p_oa, p_ob]))
    g_in = _wgrad_t("wgrad_in", n1_t, dz, N_SLOT, False, True, 4096)
    g_gate = jnp.stack([d_wr, d_wi]).reshape(2, HEADS, N_SLOT, 32, HEAD_DIM).transpose(2, 0, 1, 3, 4)
    g_gate = g_gate.reshape(N_SLOT, 2 * HEADS * 32, HEAD_DIM).astype(BF16)

    d_bs_row = jnp.pad(d_bs[:, :, 0].reshape(1, GROUPS * CHUNK), ((0, 0), (0, D - GROUPS * CHUNK)))
    vecs = jnp.concatenate([d_vec, jnp.concatenate([d_ln[0:2], d_gmlp, d_gfin, d_bs_row, jnp.zeros((3, D), F32)])])
    d_ws2 = d_ws.reshape(GROUPS * CHUNK, CHUNK)
    r_in, r_gate, r_vecs, r_ws = _run_plan("rs_sibling_in", _sibling_plan([g_in, g_gate], [vecs, d_ws2]))
    p_in, p_gate = pair(["in", "gate"], [g_in, g_gate], [r_in, r_gate])
    vecs_chip = _add2("pair_sum_vecs", vecs, r_vecs)
    ws_chip = _add2("pair_sum_ws", d_ws2, r_ws)
    (dx, d_gmix), (got_in, got_gate, got_vecs, got_ws) = _in_bwd(
        dz, w_in_g, xs, dh1, norm_mix_g, comm=_chips_plan([p_in, p_gate], [vecs_chip, ws_chip]))
    vecs_sum = _sum_terms("sum_vecs", got_vecs)
    last = jnp.concatenate([d_gmix, jnp.pad(loss_acc[0:1], ((0, 0), (0, D - 128))), jnp.zeros((6, D), F32)])
    (last_all,) = _run_plan("exchange_last", _exchange_plan(last))
    last_sum = _sum_terms("sum_last", last_all)
    loss = last_sum[1, 0]
    got = [got_in, got_oa, got_ob, got_out, got_up, got_down, got_gate]

    def step(nm, terms, w, m, v, rows, cols):
        g, d, mn, vn = _adamw("adamw_" + nm, terms.reshape(4, rows, cols), w.reshape(rows, cols),
                              m.reshape(rows, cols), v.reshape(rows, cols))
        return [a.reshape(w.shape) for a in (g, d, mn, vn)]

    o_in = step("in", got[0], w_in, m_w_in, v_w_in, D, W_IN_COLS)
    o_oa = step("out_a", got[1], w_out_a, m_w_out_a, v_w_out_a, D // N_SLOT, D)
    o_ob = step("out_b", got[2], w_out_b, m_w_out_b, v_w_out_b, D // N_SLOT, D)
    o_out = step("out", got[3], w_out, m_w_out, v_w_out, D // N_SLOT, D)
    o_up = step("up", got[4], w_up, m_w_up, v_w_up, D, FF_COLS)
    o_down = step("down", got[5], w_down, m_w_down, v_w_down, FF_COLS, D)
    gate_w = jnp.stack([w_rgate[0], w_igate[0]]).reshape(2 * HEADS * 32, HEAD_DIM)
    gate_m = jnp.stack([m_w_rgate[0], m_w_igate[0]]).reshape(2 * HEADS * 32, HEAD_DIM)
    gate_v = jnp.stack([v_w_rgate[0], v_w_igate[0]]).reshape(2 * HEADS * 32, HEAD_DIM)
    o_gate = _adamw("adamw_gate", got[6], gate_w, gate_m, gate_v)
    o_gate = [a.reshape(2, 1, HEADS, 32, HEAD_DIM) for a in o_gate]
    o_wr = [a[0] for a in o_gate]
    o_wi = [a[1] for a in o_gate]

    def own(full, width):
        return lax.dynamic_slice_in_dim(full, me * width, width, axis=1)

    small_g = {
        "norm_mix_g": last_sum[0:1], "conv_w": own(vecs_sum[0:4], 128), "conv_b": vecs_sum[4:5],
        "b_rgate": own(vecs_sum[5:6].reshape(HEADS, HEAD_DIM), 32),
        "b_igate": own(vecs_sum[6:7].reshape(HEADS, HEAD_DIM), 32),
        "lru_lambda": vecs_sum[7:8], "sgu_ln_g": vecs_sum[8:9], "sgu_ln_b": vecs_sum[9:10],
        "norm_mlp_g": vecs_sum[10:11], "norm_final_g": vecs_sum[11:12],
        "sgu_b_s": vecs_sum[12, 0:GROUPS * CHUNK].reshape(GROUPS, CHUNK),
    }
    small_w = {"norm_mix_g": (norm_mix_g, m_norm_mix_g, v_norm_mix_g), "conv_w": (conv_w, m_conv_w, v_conv_w),
               "conv_b": (conv_b, m_conv_b, v_conv_b), "b_rgate": (b_rgate, m_b_rgate, v_b_rgate),
               "b_igate": (b_igate, m_b_igate, v_b_igate), "lru_lambda": (lru_lambda, m_lru_lambda, v_lru_lambda),
               "sgu_ln_g": (sgu_ln_g, m_sgu_ln_g, v_sgu_ln_g), "sgu_ln_b": (sgu_ln_b, m_sgu_ln_b, v_sgu_ln_b),
               "norm_mlp_g": (norm_mlp_g, m_norm_mlp_g, v_norm_mlp_g),
               "norm_final_g": (norm_final_g, m_norm_final_g, v_norm_final_g),
               "sgu_b_s": (sgu_b_s, m_sgu_b_s, v_sgu_b_s), "sgu_w_s": (sgu_w_s, m_sgu_w_s, v_sgu_w_s)}
    order = list(small_g)
    as2d = lambda k, a: a.reshape(small_g[k].shape)
    upd = _adamw_many("adamw_small", [small_g[k] for k in order], *[[as2d(k, small_w[k][q]) for k in order]
                                                                     for q in range(3)])
    o_small = {k: [a.reshape(small_w[k][0].shape) for a in (small_g[k],) + u] for k, u in zip(order, upd)}
    ws3 = [a[0].reshape(GROUPS * CHUNK, CHUNK) for a in small_w.pop("sgu_w_s")]
    o_small["sgu_w_s"] = [a.reshape(sgu_w_s.shape) for a in _adamw("adamw_ws", got_ws, *ws3)]

    per_weight = {"norm_mix_g": o_small["norm_mix_g"], "w_in": o_in, "conv_w": o_small["conv_w"],
                  "conv_b": o_small["conv_b"], "w_rgate": o_wr, "b_rgate": o_small["b_rgate"], "w_igate": o_wi,
                  "b_igate": o_small["b_igate"], "lru_lambda": o_small["lru_lambda"], "w_out_a": o_oa,
                  "sgu_ln_g": o_small["sgu_ln_g"], "sgu_ln_b": o_small["sgu_ln_b"], "sgu_w_s": o_small["sgu_w_s"],
                  "sgu_b_s": o_small["sgu_b_s"], "w_out_b": o_ob, "w_out": o_out, "norm_mlp_g": o_small["norm_mlp_g"],
                  "w_up": o_up, "w_down": o_down, "norm_final_g": o_small["norm_final_g"]}
    names_w = list(per_weight)
    return (loss, dx[None], *[per_weight[k][0] for k in names_w], *[per_weight[k][1] for k in names_w],
            *[per_weight[k][2] for k in names_w], *[per_weight[k][3] for k in names_w])
```

```python
import jax
import jax.numpy as jnp
from jax import lax
from jax.experimental import pallas as pl
from jax.experimental.pallas import tpu as pltpu

F32 = jnp.float32
BF16 = jnp.bfloat16
SDS = jax.ShapeDtypeStruct
MESH = pl.DeviceIdType.MESH
ANY = pl.BlockSpec(memory_space=pl.ANY)

D = 1024
N_SLOT = 8
W_IN_COLS = 768
FF_COLS = 512
HEADS, HEAD_DIM = 4, 256
GROUPS, GROUP_DIM = 4, 256
CHUNK = 128
CONV_K = 4
NORM_EPS = 1e-6
LN_EPS = 1e-5
LRU_C = 8.0
ADAM_LR, ADAM_B1, ADAM_B2, ADAM_EPS, ADAM_WD, ADAM_STEP = 0.001, 0.9, 0.999, 1e-08, 0.01, 10

TM_ROWS = 1024
TM_MERGE = 512
T_BRANCH_A = 512
T_BRANCH_B = 256
MiB = 1024 * 1024
SMALL_OPERAND = 16 * 1024

_GELU_C = 0.7978845608028654
_GELU_A = 0.044715


def _small_in_hbm(a):
    return pltpu.with_memory_space_constraint(a, pltpu.HBM) if a.size <= SMALL_OPERAND else a


def _cparams(sem, vmem_mib):
    return pltpu.CompilerParams(dimension_semantics=sem, vmem_limit_bytes=vmem_mib * MiB)


def _gelu(x):
    t = jnp.tanh(_GELU_C * (x + _GELU_A * x * x * x))
    return 0.5 * x * (1.0 + t)


def _gelu_and_grad(x):
    x2 = x * x
    t = jnp.tanh(_GELU_C * x * (1.0 + _GELU_A * x2))
    g = 0.5 * x * (1.0 + t)
    dg = 0.5 * (1.0 + t) + 0.5 * x * (1.0 - t * t) * _GELU_C * (1.0 + 3.0 * _GELU_A * x2)
    return g, dg


def _softplus(x):
    return jnp.maximum(x, 0.0) + jnp.log1p(jnp.exp(-jnp.abs(x)))


def _dot(a, b):
    return jnp.dot(a, b, preferred_element_type=F32)


def _dot_nt(a, b):
    return lax.dot_general(a, b, (((1,), (1,)), ((), ())), preferred_element_type=F32)


def _dot_tn(a, b):
    return lax.dot_general(a, b, (((0,), (0,)), ((), ())), preferred_element_type=F32)


def _rows_shifted(prev8, cur, k):
    ext = jnp.concatenate([prev8, cur], axis=0)
    return pltpu.roll(ext, k, 0)[8:]


def _rows_advanced(cur, next8, k):
    t = cur.shape[0]
    ext = jnp.concatenate([cur, next8], axis=0)
    return pltpu.roll(ext, t + 8 - k, 0)[:t]


def _first_second(x, y, c):
    ny, nx, far = _other_chips(x, y)
    pick = lambda a, b: a * (1 - c) + b * c
    first = tuple(pick(a, b) for a, b in zip(ny, nx))
    second = tuple(pick(b, a) for a, b in zip(ny, nx))
    return first, second, far


def _slot_order(x, y, c):
    chip = 2 * x + y
    first, second, far = _first_second(x, y, c)
    order = [2 * chip + c, 2 * chip + 1 - c, 2 * first[2] + c, 2 * second[2] + 1 - c, 2 * second[2] + c,
             2 * first[2] + 1 - c, 2 * far[2] + c, 2 * far[2] + 1 - c]
    return jnp.stack(order).astype(jnp.int32)


def _in_proj(x, g_mix, w_in_own, order, comm=None):
    s = x.shape[0]
    tm = min(TM_ROWS, s)
    ni = s // tm

    def body(order_ref, x_ref, g_ref, own_ref, z_ref, nt_ref, wg_ref, n_s, w_s, send_sems, recv_sems, local_sems):
        j, i = pl.program_id(0), pl.program_id(1)
        px, py, c = _place()
        chip = 2 * px + py
        me = 2 * chip + c
        sib = (px, py, 1 - c)
        chips = _other_chips(px, py)

        def rc(k, src, blk, to):
            return pltpu.make_async_remote_copy(src_ref=src, dst_ref=w_s.at[blk], send_sem=send_sems.at[k],
                                                recv_sem=recv_sems.at[k], device_id=to, device_id_type=MESH)

        del chips
        first, second, far = _first_second(px, py, c)
        blocks = [2 * first[2] + c, 2 * second[2] + c, 2 * far[2] + c]
        own_in = pltpu.make_async_copy(own_ref, w_s.at[me], local_sems.at[0])
        to_first = rc(1, own_ref, me, (first[0], first[1], c))
        to_second = rc(2, own_ref, me, (second[0], second[1], c))
        relay = rc(3, w_s.at[blocks[0]], blocks[0], (second[0], second[1], c))
        sends = [rc(0, own_ref, me, sib), to_first, to_second, relay]
        passed = [rc(4 + q, w_s.at[blk], blk, sib) for q, blk in enumerate(blocks)]
        keep = pltpu.make_async_copy(w_s, wg_ref, local_sems.at[1])

        @pl.when((i == 0) & (j == 0))
        def _():
            own_in.start()
            sends[0].start()
            to_first.start()
            own_in.wait()

        @pl.when((i == 0) & (j == 1))
        def _():
            rc(0, own_ref, 2 * chip + 1 - c, sib).wait_recv()

        for q, blk in enumerate(blocks):
            @pl.when((i == 0) & (j == 2 + 2 * q))
            def _():
                rc(1 + q, own_ref, blk, sib).wait_recv()
                passed[q].start()
                if q == 0:
                    to_second.start()
                    relay.start()

            @pl.when((i == 0) & (j == 3 + 2 * q))
            def _():
                rc(4 + q, own_ref, order_ref[j], sib).wait_recv()

        rows = pl.ds(pl.multiple_of(i * tm, tm), tm)

        @pl.when(j == 0)
        def _():
            xv = x_ref[...]
            rstd = lax.rsqrt(jnp.mean(xv * xv, axis=-1, keepdims=True) + NORM_EPS)
            nb = (xv * rstd * g_ref[...]).astype(BF16)
            n_s[rows, :] = nb
            nt_ref[...] = nb.T

        z_ref[...] = _dot(n_s[rows, :], w_s[order_ref[j]]).astype(BF16)

        @pl.when((i == 0) & (j == N_SLOT - 1))
        def _():
            keep.start()

        @pl.when((i == ni - 1) & (j == N_SLOT - 1))
        def _():
            for cp in sends + passed:
                cp.wait_send()
            keep.wait()

    first_pass = lambda j, i, o: (jnp.where(j == 0, i, ni - 1), 0)
    (z, n1, w_in_g), extra = _call(
        body, name="in_proj", grid=(N_SLOT, ni), prefetch=(order,),
        in_specs=[pl.BlockSpec((tm, D), first_pass),
                  pl.BlockSpec((1, D), lambda j, i, o: (0, 0)), ANY],
        out_specs=[pl.BlockSpec((tm, W_IN_COLS), lambda j, i, o: (i, o[j])),
                   pl.BlockSpec((D, tm), lambda j, i, o: (0, jnp.where(j == 0, i, ni - 1))), ANY],
        out_shape=[SDS((s, N_SLOT * W_IN_COLS), BF16), SDS((D, s), BF16), SDS((N_SLOT, D, W_IN_COLS), BF16)],
        scratch_shapes=[pltpu.VMEM((s, D), BF16), pltpu.VMEM((N_SLOT, D, W_IN_COLS), BF16),
                        pltpu.SemaphoreType.DMA((7,)), pltpu.SemaphoreType.DMA((7,)), pltpu.SemaphoreType.DMA((2,))],
        params=_cparams(("arbitrary", "arbitrary"), 56), args=(x, g_mix, w_in_own), comm=comm)
    return (z, n1, w_in_g), extra


def _decay(r, sp_lam):
    log_a = (-LRU_C) * r * sp_lam
    a = jnp.exp(log_a)
    return a, jnp.sqrt(-jnp.tanh(log_a) * (a * a + 1.0))


def _lru_gates(xc, xcb, wr_ref, br, wi_ref, bi, sp_lam, a_s, b_s, r_ref, i_ref):
    for h in range(HEADS):
        sl = slice(h * HEAD_DIM, (h + 1) * HEAD_DIM)
        r = jax.nn.sigmoid(_dot(xcb[:, sl], wr_ref[h]) + br[:, sl])
        ig = jax.nn.sigmoid(_dot(xcb[:, sl], wi_ref[h]) + bi[:, sl])
        a, mult = _decay(r, sp_lam[:, sl])
        a_s[:, sl] = a
        b_s[:, sl] = xc[:, sl] * ig * mult
        r_ref[:, sl] = r.astype(BF16)
        i_ref[:, sl] = ig.astype(BF16)


def _conv_fwd(xa, prev8, cw, cb):
    xc = cb + cw[0:1, :] * xa
    for k in range(1, CONV_K):
        xc = xc + cw[k:k + 1, :] * _rows_shifted(prev8, xa, k)
    return xc


def _branch_a_fwd(z, conv_w, conv_b, w_r, b_r, w_i, b_i, lam, comm=None):
    s = z.shape[0]
    ta = min(T_BRANCH_A, s)
    per16 = ta // 16

    def body(xa_ref, xp_ref, ga_ref, cw_ref, cb_ref, wr_ref, br_ref, wi_ref, bi_ref, lam_ref,
             ya_ref, hs_ref, xc_ref, r_ref, i_ref, a_s, b_s, h_s, carry_s):
        i = pl.program_id(0)

        @pl.when(i == 0)
        def _():
            carry_s[...] = jnp.zeros_like(carry_s)

        xa = xa_ref[...].astype(F32)
        prev8 = jnp.where(i > 0, xp_ref[...].astype(F32)[8:16], 0.0)
        xc = _conv_fwd(xa, prev8, cw_ref[...], cb_ref[...])
        xcb = xc.astype(BF16)
        xc_ref[...] = xcb
        sp_lam = _softplus(-lam_ref[...])
        _lru_gates(xc, xcb, wr_ref, br_ref[...], wi_ref, bi_ref[...], sp_lam, a_s, b_s, r_ref, i_ref)

        row = lax.broadcasted_iota(jnp.int32, (8, D), 0)

        def group(g, carry):
            off = pl.multiple_of(g * 8, 8)
            a8 = a_s[pl.ds(off, 8), :]
            b8 = b_s[pl.ds(off, 8), :]
            for d in (1, 2, 4):
                a_sh = jnp.where(row >= d, pltpu.roll(a8, d, 0), 1.0)
                b_sh = jnp.where(row >= d, pltpu.roll(b8, d, 0), 0.0)
                b8 = a8 * b_sh + b8
                a8 = a8 * a_sh
            h8 = b8 + a8 * carry
            h_s[pl.ds(off, 8), :] = h8
            return jnp.broadcast_to(h8[7:8, :], (8, D))

        carry_s[...] = lax.fori_loop(0, ta // 8, group, carry_s[...])
        hs = h_s[...]
        hs_ref[...] = hs.astype(BF16)
        ya_ref[...] = (hs * _gelu(ga_ref[...].astype(F32))).astype(BF16)

    vec = pl.BlockSpec((1, D), lambda i: (0, 0))
    gate = pl.BlockSpec((HEADS, HEAD_DIM, HEAD_DIM), lambda i: (0, 0, 0))
    return _call(
        body, name="branch_a_fwd", grid=(s // ta,),
        in_specs=[pl.BlockSpec((ta, D), lambda i: (i, 0)),
                  pl.BlockSpec((16, D), lambda i: (jnp.maximum(i * per16 - 1, 0), 0)),
                  pl.BlockSpec((ta, D), lambda i: (i, 1)),
                  pl.BlockSpec((CONV_K, D), lambda i: (0, 0)), vec, gate, vec, gate, vec, vec],
        out_specs=[pl.BlockSpec((ta, D), lambda i: (i, 0))] * 5,
        out_shape=[SDS((s, D), BF16)] * 5,
        scratch_shapes=[pltpu.VMEM((ta, D), F32), pltpu.VMEM((ta, D), F32), pltpu.VMEM((ta, D), F32),
                        pltpu.VMEM((8, D), F32)],
        params=_cparams(("arbitrary",), 40), args=(z, z, z, conv_w, conv_b, w_r, b_r, w_i, b_i, lam), comm=comm)


def _sgu_common(ub, vb, lg, lb, with_grad):
    if with_grad:
        u, du = _gelu_and_grad(ub)
        v, dv = _gelu_and_grad(vb)
    else:
        u, v, du, dv = _gelu(ub), _gelu(vb), None, None
    mu = jnp.mean(v, axis=-1, keepdims=True)
    vc = v - mu
    rstd = lax.rsqrt(jnp.mean(vc * vc, axis=-1, keepdims=True) + LN_EPS)
    vhat = vc * rstd
    vln = vhat * lg + lb
    return u, du, dv, rstd, vhat, vln


def _masked_ws(ws_ref):
    t = lax.broadcasted_iota(jnp.int32, (CHUNK, CHUNK), 0)
    c = lax.broadcasted_iota(jnp.int32, (CHUNK, CHUNK), 1)
    keep = c <= t
    return [jnp.where(keep, ws_ref[g], 0.0).astype(BF16) for g in range(GROUPS)]


def _branch_b_fwd(z, ln_g, ln_b, w_s, b_s_t, comm=None):
    s = z.shape[0]
    tb = min(T_BRANCH_B, s)

    def body(ub_ref, vb_ref, lg_ref, lb_ref, ws_ref, bs_ref, yb_ref):
        u, _, _, _, _, vln = _sgu_common(ub_ref[...].astype(F32), vb_ref[...].astype(F32),
                                         lg_ref[...], lb_ref[...], False)
        vlnb = vln.astype(BF16)
        wm = _masked_ws(ws_ref)
        bs = bs_ref[...]
        for c in range(tb // CHUNK):
            rs = slice(c * CHUNK, (c + 1) * CHUNK)
            for g in range(GROUPS):
                cs = slice(g * GROUP_DIM, (g + 1) * GROUP_DIM)
                sp = _dot(wm[g], vlnb[rs, cs]) + bs[:, g:g + 1]
                yb_ref[rs, cs] = (u[rs, cs] * sp).astype(BF16)

    vec = pl.BlockSpec((1, D), lambda i: (0, 0))
    return _call(
        body, name="branch_b_fwd", grid=(s // tb,),
        in_specs=[pl.BlockSpec((tb, D), lambda i: (i, 2)), pl.BlockSpec((tb, D), lambda i: (i, 3)), vec, vec,
                  pl.BlockSpec((GROUPS, CHUNK, CHUNK), lambda i: (0, 0, 0)),
                  pl.BlockSpec((CHUNK, GROUPS), lambda i: (0, 0))],
        out_specs=[pl.BlockSpec((tb, D), lambda i: (i, 0))],
        out_shape=[SDS((s, D), BF16)], scratch_shapes=[],
        params=_cparams(("arbitrary",), 40), args=(z, z, ln_g, ln_b, w_s, b_s_t), comm=comm)


def _merge_out(ya, yb, z, x, w_oa, w_ob, w_out, comm=None):
    s = x.shape[0]
    tm = min(TM_MERGE, s)

    def body(ya_ref, yb_ref, ma_ref, mb_ref, x_ref, woa_ref, wob_ref, wo_ref, pa_ref, pb_ref, mg_ref, h1_ref):
        pa = _dot(ya_ref[...], woa_ref[...])
        pb = _dot(yb_ref[...], wob_ref[...])
        merged = (jax.nn.sigmoid(ma_ref[...].astype(F32)) * pa
                  + jax.nn.sigmoid(mb_ref[...].astype(F32)) * pb).astype(BF16)
        pa_ref[...] = pa.astype(BF16)
        pb_ref[...] = pb.astype(BF16)
        mg_ref[...] = merged
        h1_ref[...] = x_ref[...] + _dot(merged, wo_ref[...])

    row = pl.BlockSpec((tm, D), lambda i: (i, 0))
    wsp = pl.BlockSpec((D, D), lambda i: (0, 0))
    return _call(
        body, name="merge_out", grid=(s // tm,),
        in_specs=[row, row, pl.BlockSpec((tm, D), lambda i: (i, 4)), pl.BlockSpec((tm, D), lambda i: (i, 5)),
                  row, wsp, wsp, wsp],
        out_specs=[row, row, row, row],
        out_shape=[SDS((s, D), BF16), SDS((s, D), BF16), SDS((s, D), BF16), SDS((s, D), F32)], scratch_shapes=[],
        params=_cparams(("arbitrary",), 48), args=(ya, yb, z, z, x, w_oa, w_ob, w_out), comm=comm)


def _mlp_fwd(h1, g_mlp, w_up_g, w_down, g_fin, tgt):
    s = h1.shape[0]
    tm = min(TM_ROWS, s)
    nj = N_SLOT

    def body(h1_ref, gm_ref, wu_ref, wd_ref, gf_ref, t_ref, r_ref, at_ref, n2t_ref, dh2_ref, loss_ref, dgf_ref,
             n2_s, acc_s):
        i, j = pl.program_id(0), pl.program_id(1)

        @pl.when(j == 0)
        def _():
            hv = h1_ref[...]
            rstd = lax.rsqrt(jnp.mean(hv * hv, axis=-1, keepdims=True) + NORM_EPS)
            nb = (hv * rstd * gm_ref[...]).astype(BF16)
            n2_s[...] = nb
            n2t_ref[...] = nb.T
            acc_s[...] = jnp.zeros_like(acc_s)

        @pl.when((i == 0) & (j == 0))
        def _():
            loss_ref[...] = jnp.zeros_like(loss_ref)
            dgf_ref[...] = jnp.zeros_like(dgf_ref)

        r = jnp.maximum(_dot(n2_s[...], wu_ref[...]), 0.0)
        r_ref[...] = r.astype(BF16)
        act = (r * r).astype(BF16)
        at_ref[...] = act.T
        acc_s[...] += _dot(act, wd_ref[...])

        @pl.when(j == nj - 1)
        def _():
            h2 = h1_ref[...] + acc_s[...]
            rstd = lax.rsqrt(jnp.mean(h2 * h2, axis=-1, keepdims=True) + NORM_EPS)
            hh = h2 * rstd
            gf = gf_ref[...]
            e = hh * gf - t_ref[...]
            loss_ref[...] += jnp.sum(e * e) * (0.5 / D)
            dy = e * (1.0 / D)
            dgf_ref[...] += jnp.sum(dy * hh, axis=0, keepdims=True)
            dhh = dy * gf
            dh2_ref[...] = rstd * (dhh - hh * jnp.mean(dhh * hh, axis=-1, keepdims=True))

    row = pl.BlockSpec((tm, D), lambda i, j: (i, 0))
    vec = pl.BlockSpec((1, D), lambda i, j: (0, 0))
    return pl.pallas_call(
        body, name="mlp_fwd", grid=(s // tm, nj),
        in_specs=[row, vec, pl.BlockSpec((None, D, FF_COLS), lambda i, j: (j, 0, 0)),
                  pl.BlockSpec((FF_COLS, D), lambda i, j: (j, 0)), vec, row],
        out_specs=[pl.BlockSpec((tm, FF_COLS), lambda i, j: (i, j)), pl.BlockSpec((FF_COLS, tm), lambda i, j: (j, i)),
                   pl.BlockSpec((D, tm), lambda i, j: (0, i)), row, pl.BlockSpec((8, 128), lambda i, j: (0, 0)), vec],
        out_shape=[SDS((s, nj * FF_COLS), BF16), SDS((nj * FF_COLS, s), BF16), SDS((D, s), BF16), SDS((s, D), F32),
                   SDS((8, 128), F32), SDS((1, D), F32)],
        scratch_shapes=[pltpu.VMEM((tm, D), BF16), pltpu.VMEM((tm, D), F32)],
        compiler_params=_cparams(("arbitrary", "arbitrary"), 52),
    )(h1, _small_in_hbm(g_mlp), w_up_g, w_down, _small_in_hbm(g_fin), tgt)


def _mlp_bwd(dh2, r, w_down, w_up_g, h1, g_mlp, comm=None):
    s = h1.shape[0]
    tm = min(TM_ROWS, s)
    nj = N_SLOT

    def body(dh2_ref, r_ref, wd_ref, wu_ref, h1_ref, gm_ref, df_ref, dh1_ref, dgm_ref, dh2b_s, acc_s):
        i, j = pl.program_id(0), pl.program_id(1)

        @pl.when(j == 0)
        def _():
            dh2b_s[...] = dh2_ref[...].astype(BF16)
            acc_s[...] = jnp.zeros_like(acc_s)

        @pl.when((i == 0) & (j == 0))
        def _():
            dgm_ref[...] = jnp.zeros_like(dgm_ref)

        d_act = _dot_nt(dh2b_s[...], wd_ref[...])
        df = (d_act * (2.0 * r_ref[...].astype(F32))).astype(BF16)
        df_ref[...] = df
        acc_s[...] += _dot_nt(df, wu_ref[...])

        @pl.when(j == nj - 1)
        def _():
            hv = h1_ref[...]
            rstd = lax.rsqrt(jnp.mean(hv * hv, axis=-1, keepdims=True) + NORM_EPS)
            hh = hv * rstd
            dn2 = acc_s[...]
            dgm_ref[...] += jnp.sum(dn2 * hh, axis=0, keepdims=True)
            dhat = dn2 * gm_ref[...]
            dh1_ref[...] = dh2_ref[...] + rstd * (dhat - hh * jnp.mean(dhat * hh, axis=-1, keepdims=True))

    row = pl.BlockSpec((tm, D), lambda i, j: (i, 0))
    vec = pl.BlockSpec((1, D), lambda i, j: (0, 0))
    ffb = pl.BlockSpec((tm, FF_COLS), lambda i, j: (i, j))
    return _call(
        body, name="mlp_bwd", grid=(s // tm, nj),
        in_specs=[row, ffb, pl.BlockSpec((FF_COLS, D), lambda i, j: (j, 0)),
                  pl.BlockSpec((None, D, FF_COLS), lambda i, j: (j, 0, 0)), row, vec],
        out_specs=[ffb, row, vec],
        out_shape=[SDS((s, nj * FF_COLS), BF16), SDS((s, D), F32), SDS((1, D), F32)],
        scratch_shapes=[pltpu.VMEM((tm, D), BF16), pltpu.VMEM((tm, D), F32)],
        params=_cparams(("arbitrary", "arbitrary"), 52), args=(dh2, r, w_down, w_up_g, h1, g_mlp), comm=comm)


def _merge_bwd(dh1, z, pa, pb, w_out, w_oa, w_ob, comm=None):
    s = dh1.shape[0]
    tm = min(TM_MERGE, s)

    def body(dh1_ref, ma_ref, mb_ref, pa_ref, pb_ref, wo_ref, woa_ref, wob_ref,
             dz_ref, dpa_ref, dpb_ref, dya_ref, dyb_ref):
        dm = _dot_nt(dh1_ref[...].astype(BF16), wo_ref[...])
        sa = jax.nn.sigmoid(ma_ref[...].astype(F32))
        sb = jax.nn.sigmoid(mb_ref[...].astype(F32))
        dpa = (dm * sa).astype(BF16)
        dpb = (dm * sb).astype(BF16)
        dz_ref[:, 0:D] = (dm * pa_ref[...].astype(F32) * sa * (1.0 - sa)).astype(BF16)
        dz_ref[:, D:2 * D] = (dm * pb_ref[...].astype(F32) * sb * (1.0 - sb)).astype(BF16)
        dpa_ref[...] = dpa
        dpb_ref[...] = dpb
        dya_ref[...] = _dot_nt(dpa, woa_ref[...]).astype(BF16)
        dyb_ref[...] = _dot_nt(dpb, wob_ref[...]).astype(BF16)

    row = pl.BlockSpec((tm, D), lambda i: (i, 0))
    wsp = pl.BlockSpec((D, D), lambda i: (0, 0))
    return _call(
        body, name="merge_bwd", grid=(s // tm,),
        in_specs=[row, pl.BlockSpec((tm, D), lambda i: (i, 4)), pl.BlockSpec((tm, D), lambda i: (i, 5)),
                  row, row, wsp, wsp, wsp],
        out_specs=[pl.BlockSpec((tm, 2 * D), lambda i: (i, 2)), row, row, row, row],
        out_shape=[SDS((s, 6 * D), BF16)] + [SDS((s, D), BF16)] * 4, scratch_shapes=[],
        params=_cparams(("arbitrary",), 48), args=(dh1, z, z, pa, pb, w_out, w_oa, w_ob), comm=comm)


def _branch_b_bwd(dz, dyb, z, ln_g, ln_b, w_s, b_s_t, comm=None):
    s = z.shape[0]
    tb = min(T_BRANCH_B, s)

    def body(dz_in, dyb_ref, ub_ref, vb_ref, lg_ref, lb_ref, ws_ref, bs_ref,
             dz_ref, dws_ref, dbs_ref, dln_ref, du_s, dvln_s):
        del dz_in

        @pl.when(pl.program_id(0) == 0)
        def _():
            dws_ref[...] = jnp.zeros_like(dws_ref)
            dbs_ref[...] = jnp.zeros_like(dbs_ref)
            dln_ref[...] = jnp.zeros_like(dln_ref)

        lg = lg_ref[...]
        u, du, dv, rstd, vhat, vln = _sgu_common(ub_ref[...].astype(F32), vb_ref[...].astype(F32),
                                                 lg, lb_ref[...], True)
        vlnb = vln.astype(BF16)
        dyb_v = dyb_ref[...].astype(F32)
        wm = _masked_ws(ws_ref)
        keep = (lax.broadcasted_iota(jnp.int32, (CHUNK, CHUNK), 1)
                <= lax.broadcasted_iota(jnp.int32, (CHUNK, CHUNK), 0))
        bs = bs_ref[...]
        for c in range(tb // CHUNK):
            rs = slice(c * CHUNK, (c + 1) * CHUNK)
            for g in range(GROUPS):
                cs = slice(g * GROUP_DIM, (g + 1) * GROUP_DIM)
                v_blk = vlnb[rs, cs]
                sp = _dot(wm[g], v_blk) + bs[:, g:g + 1]
                d_sp = dyb_v[rs, cs] * u[rs, cs]
                d_spb = d_sp.astype(BF16)
                du_s[rs, cs] = dyb_v[rs, cs] * sp
                dvln_s[rs, cs] = _dot_tn(wm[g], d_spb)
                dws_ref[g] += jnp.where(keep, _dot_nt(d_spb, v_blk), 0.0)
                dbs_ref[g] += jnp.broadcast_to(jnp.sum(d_sp, axis=-1, keepdims=True), (CHUNK, CHUNK))
        dvln = dvln_s[...]
        dln_ref[0:1, :] += jnp.sum(dvln * vhat, axis=0, keepdims=True)
        dln_ref[1:2, :] += jnp.sum(dvln, axis=0, keepdims=True)
        dvh = dvln * lg
        d_v = rstd * (dvh - jnp.mean(dvh, axis=-1, keepdims=True)
                      - vhat * jnp.mean(dvh * vhat, axis=-1, keepdims=True))
        dz_ref[:, 0:D] = (du_s[...] * du).astype(BF16)
        dz_ref[:, D:2 * D] = (d_v * dv).astype(BF16)

    vec = pl.BlockSpec((1, D), lambda i: (0, 0))
    sq = pl.BlockSpec((GROUPS, CHUNK, CHUNK), lambda i: (0, 0, 0))
    return _call(
        body, name="branch_b_bwd", grid=(s // tb,),
        in_specs=[ANY, pl.BlockSpec((tb, D), lambda i: (i, 0)),
                  pl.BlockSpec((tb, D), lambda i: (i, 2)), pl.BlockSpec((tb, D), lambda i: (i, 3)), vec, vec, sq,
                  pl.BlockSpec((CHUNK, GROUPS), lambda i: (0, 0))],
        out_specs=[pl.BlockSpec((tb, 2 * D), lambda i: (i, 1)), sq, sq, pl.BlockSpec((8, D), lambda i: (0, 0))],
        out_shape=[SDS(dz.shape, BF16), SDS((GROUPS, CHUNK, CHUNK), F32), SDS((GROUPS, CHUNK, CHUNK), F32),
                   SDS((8, D), F32)],
        scratch_shapes=[pltpu.VMEM((tb, D), F32), pltpu.VMEM((tb, D), F32)], aliases={0: 0},
        params=_cparams(("arbitrary",), 40), args=(dz, dyb, z, z, ln_g, ln_b, w_s, b_s_t), comm=comm)


def _branch_a_bwd(dz, dya, z, hs, xc, r, ig, conv_w, w_r, w_i, lam, comm=None):
    s = z.shape[0]
    ta = min(T_BRANCH_A, s)
    nb = s // ta
    per16 = ta // 16

    def body(dz_in, dya_ref, xa_ref, ga_ref, hs_ref, hp_ref, xc_ref, r_ref, i_ref, cw_ref, wr_ref, wi_ref,
             lam_ref, dz_ref, vec_ref, dwr_ref, dwi_ref, a_s, b_s, h_s, dcar_s, acar_s, dxc_s):
        del dz_in
        i = pl.program_id(0)
        blk = nb - 1 - i

        @pl.when(i == 0)
        def _():
            dcar_s[...] = jnp.zeros_like(dcar_s)
            acar_s[...] = jnp.zeros_like(acar_s)
            dxc_s[...] = jnp.zeros_like(dxc_s)
            vec_ref[...] = jnp.zeros_like(vec_ref)
            dwr_ref[...] = jnp.zeros_like(dwr_ref)
            dwi_ref[...] = jnp.zeros_like(dwi_ref)

        cw = cw_ref[...]
        lam_v = lam_ref[...]
        xa = xa_ref[...].astype(F32)
        xcb = xc_ref[...]
        xc = xcb.astype(F32)
        sp_lam = _softplus(-lam_v)
        r_v, i_v = r_ref[...].astype(F32), i_ref[...].astype(F32)
        a_v, m_v = _decay(r_v, sp_lam)

        hs_v = hs_ref[...].astype(F32)
        hprev8 = jnp.where(blk > 0, hp_ref[...].astype(F32)[8:16], 0.0)
        h_m1 = _rows_shifted(hprev8, hs_v, 1)
        gg, dgg = _gelu_and_grad(ga_ref[...].astype(F32))
        dya_v = dya_ref[...].astype(F32)
        dz_ref[:, D:2 * D] = (dya_v * hs_v * dgg).astype(BF16)

        a_s[...] = _rows_advanced(a_v, acar_s[...], 1)
        b_s[...] = dya_v * gg

        row = lax.broadcasted_iota(jnp.int32, (8, D), 0)
        ng = ta // 8

        def group(gi, carry):
            off = pl.multiple_of((ng - 1 - gi) * 8, 8)
            c8 = a_s[pl.ds(off, 8), :]
            d8 = b_s[pl.ds(off, 8), :]
            for d in (1, 2, 4):
                c_sh = jnp.where(row < 8 - d, pltpu.roll(c8, 8 - d, 0), 1.0)
                d_sh = jnp.where(row < 8 - d, pltpu.roll(d8, 8 - d, 0), 0.0)
                d8 = c8 * d_sh + d8
                c8 = c8 * c_sh
            dh8 = d8 + c8 * carry
            h_s[pl.ds(off, 8), :] = dh8
            return jnp.broadcast_to(dh8[0:1, :], (8, D))

        dcar_s[...] = lax.fori_loop(0, ng, group, dcar_s[...])
        acar_s[...] = jnp.broadcast_to(a_v[0:1, :], (8, D))

        dbx = h_s[...]
        d_mult = dbx * xc * i_v
        d_loga = dbx * h_m1 * a_v - d_mult * (a_v * a_v) / m_v
        d_pr = d_loga * ((-LRU_C) * sp_lam) * r_v * (1.0 - r_v)
        d_pi = dbx * xc * m_v * i_v * (1.0 - i_v)
        vec_ref[7:8, :] += jnp.sum(d_loga * r_v, axis=0, keepdims=True) * (LRU_C * jax.nn.sigmoid(-lam_v))
        vec_ref[5:6, :] += jnp.sum(d_pr, axis=0, keepdims=True)
        vec_ref[6:7, :] += jnp.sum(d_pi, axis=0, keepdims=True)
        d_prb = d_pr.astype(BF16)
        d_pib = d_pi.astype(BF16)
        h_s[...] = dbx * i_v * m_v
        for h in range(HEADS):
            sl = slice(h * HEAD_DIM, (h + 1) * HEAD_DIM)
            h_s[:, sl] += _dot_nt(d_prb[:, sl], wr_ref[h]) + _dot_nt(d_pib[:, sl], wi_ref[h])
            dwr_ref[h] += _dot_tn(xcb[:, sl], d_prb[:, sl])
            dwi_ref[h] += _dot_tn(xcb[:, sl], d_pib[:, sl])
        d_xc = h_s[...]
        vec_ref[4:5, :] += jnp.sum(d_xc, axis=0, keepdims=True)
        vec_ref[0:1, :] += jnp.sum(d_xc * xa, axis=0, keepdims=True)
        d_xa = cw[0:1, :] * d_xc
        nxt = dxc_s[...]
        for k in range(1, CONV_K):
            ahead = _rows_advanced(d_xc, nxt, k)
            vec_ref[k:k + 1, :] += jnp.sum(ahead * xa, axis=0, keepdims=True)
            d_xa = d_xa + cw[k:k + 1, :] * ahead
        dz_ref[:, 0:D] = d_xa.astype(BF16)
        dxc_s[...] = d_xc[0:8, :]

    vec = pl.BlockSpec((1, D), lambda i: (0, 0))
    gate = pl.BlockSpec((HEADS, HEAD_DIM, HEAD_DIM), lambda i: (0, 0, 0))
    cur = lambda c: pl.BlockSpec((ta, D), lambda i: (nb - 1 - i, c))
    before = lambda c: pl.BlockSpec((16, D), lambda i: (jnp.maximum((nb - 1 - i) * per16 - 1, 0), c))
    return _call(
        body, name="branch_a_bwd", grid=(nb,),
        in_specs=[ANY, cur(0), cur(0), cur(1), cur(0), before(0), cur(0), cur(0), cur(0),
                  pl.BlockSpec((CONV_K, D), lambda i: (0, 0)), gate, gate, vec],
        out_specs=[pl.BlockSpec((ta, 2 * D), lambda i: (nb - 1 - i, 0)), pl.BlockSpec((8, D), lambda i: (0, 0)),
                   gate, gate],
        out_shape=[SDS(dz.shape, BF16), SDS((8, D), F32), SDS((HEADS, HEAD_DIM, HEAD_DIM), F32),
                   SDS((HEADS, HEAD_DIM, HEAD_DIM), F32)],
        scratch_shapes=[pltpu.VMEM((ta, D), F32)] * 3 + [pltpu.VMEM((8, D), F32)] * 3, aliases={0: 0},
        params=_cparams(("arbitrary",), 48),
        args=(dz, dya, z, z, hs, hs, xc, r, ig, conv_w, w_r, w_i, lam), comm=comm)


def _in_bwd(dz, w_in_g, x, dh1, g_mix, comm=None):
    s = x.shape[0]
    tm = min(TM_ROWS, s)
    nj = N_SLOT

    def body(dz_ref, w_ref, x_ref, dh1_ref, g_ref, dx_ref, dg_ref, acc_s):
        i, j = pl.program_id(0), pl.program_id(1)

        @pl.when(j == 0)
        def _():
            acc_s[...] = jnp.zeros_like(acc_s)

        @pl.when((i == 0) & (j == 0))
        def _():
            dg_ref[...] = jnp.zeros_like(dg_ref)

        acc_s[...] += _dot_nt(dz_ref[...], w_ref[...])

        @pl.when(j == nj - 1)
        def _():
            xv = x_ref[...]
            rstd = lax.rsqrt(jnp.mean(xv * xv, axis=-1, keepdims=True) + NORM_EPS)
            xh = xv * rstd
            dn = acc_s[...]
            dg_ref[...] += jnp.sum(dn * xh, axis=0, keepdims=True)
            dhat = dn * g_ref[...]
            dx_ref[...] = dh1_ref[...] + rstd * (dhat - xh * jnp.mean(dhat * xh, axis=-1, keepdims=True))

    row = pl.BlockSpec((tm, D), lambda i, j: (i, 0))
    vec = pl.BlockSpec((1, D), lambda i, j: (0, 0))
    return _call(
        body, name="in_bwd", grid=(s // tm, nj),
        in_specs=[pl.BlockSpec((tm, W_IN_COLS), lambda i, j: (i, j)),
                  pl.BlockSpec((None, D, W_IN_COLS), lambda i, j: (j, 0, 0)), row, row, vec],
        out_specs=[row, vec],
        out_shape=[SDS((s, D), F32), SDS((1, D), F32)],
        scratch_shapes=[pltpu.VMEM((tm, D), F32)],
        params=_cparams(("arbitrary", "arbitrary"), 48), args=(dz, w_in_g, x, dh1, g_mix), comm=comm)


def _wgrad(name, a, b, nblk, a_split, b_split):
    s = a.shape[0]
    ts = min(TM_ROWS, s)
    a_w = a.shape[1] // nblk if a_split else a.shape[1]
    b_w = b.shape[1] // nblk if b_split else b.shape[1]

    def body(a_ref, b_ref, o_ref, acc_s):
        t = pl.program_id(1)

        @pl.when(t == 0)
        def _():
            acc_s[...] = jnp.zeros_like(acc_s)

        acc_s[...] += _dot_tn(a_ref[...].astype(BF16), b_ref[...].astype(BF16))

        @pl.when(t == pl.num_programs(1) - 1)
        def _():
            o_ref[...] = acc_s[...].astype(BF16)

    return pl.pallas_call(
        body, name=name, grid=(nblk, s // ts),
        in_specs=[pl.BlockSpec((ts, a_w), (lambda k, t: (t, k)) if a_split else (lambda k, t: (t, 0))),
                  pl.BlockSpec((ts, b_w), (lambda k, t: (t, k)) if b_split else (lambda k, t: (t, 0)))],
        out_specs=pl.BlockSpec((None, a_w, b_w), lambda k, t: (k, 0, 0)),
        out_shape=SDS((nblk, a_w, b_w), BF16),
        scratch_shapes=[pltpu.VMEM((a_w, b_w), F32)],
        compiler_params=_cparams(("arbitrary", "arbitrary"), 48),
    )(a, b)


def _wgrad_t(name, a_t, b, nblk, a_split, b_split, tokens):
    s = b.shape[0]
    ts = min(tokens, s)
    a_w = a_t.shape[0] // nblk if a_split else a_t.shape[0]
    b_w = b.shape[1] // nblk if b_split else b.shape[1]

    def body(a_ref, b_ref, o_ref, acc_s):
        t = pl.program_id(1)

        @pl.when(t == 0)
        def _():
            acc_s[...] = jnp.zeros_like(acc_s)

        acc_s[...] += _dot(a_ref[...], b_ref[...].astype(BF16))

        @pl.when(t == pl.num_programs(1) - 1)
        def _():
            o_ref[...] = acc_s[...].astype(BF16)

    return pl.pallas_call(
        body, name=name, grid=(nblk, s // ts),
        in_specs=[pl.BlockSpec((a_w, ts), (lambda k, t: (k, t)) if a_split else (lambda k, t: (0, t))),
                  pl.BlockSpec((ts, b_w), (lambda k, t: (t, k)) if b_split else (lambda k, t: (t, 0)))],
        out_specs=pl.BlockSpec((None, a_w, b_w), lambda k, t: (k, 0, 0)),
        out_shape=SDS((nblk, a_w, b_w), BF16),
        scratch_shapes=[pltpu.VMEM((a_w, b_w), F32)],
        compiler_params=_cparams(("arbitrary", "arbitrary"), 48),
    )(a_t, b)


def _place():
    x, y, c = lax.axis_index("x"), lax.axis_index("y"), lax.axis_index("c")
    return x, y, c


def _other_chips(x, y):
    return [(x, 1 - y, 2 * x + 1 - y), (1 - x, y, 2 * (1 - x) + y), (1 - x, 1 - y, 2 * (1 - x) + 1 - y)]


class _Plan:
    def __init__(self, arrays, out_shape, sems, start, finish, middle=None, middle_at=6):
        self.arrays, self.out_shape, self.sems, self.start, self.finish = arrays, out_shape, sems, start, finish
        self.middle, self.middle_at = middle, middle_at


def _gather_plan(shards, middle_at=6):
    n = len(shards)

    def copies(ins, outs, sems):
        send_sems, recv_sems, local_sems = sems
        x, y, c = _place()
        chip = 2 * x + y
        me = 2 * chip + c
        sib = (x, y, 1 - c)
        chips = _other_chips(x, y)

        def rc(k, t, src, blk, to):
            return pltpu.make_async_remote_copy(
                src_ref=src, dst_ref=outs[t].at[blk], send_sem=send_sems.at[k * n + t],
                recv_sem=recv_sems.at[k * n + t], device_id=to, device_id_type=MESH)

        (yx, yy, y_chip), (xx, xy, x_chip), _ = chips
        local = [pltpu.make_async_copy(ins[t], outs[t].at[me], local_sems.at[t]) for t in range(n)]
        sends = ([rc(0, t, ins[t], me, sib) for t in range(n)] + [rc(1, t, ins[t], me, (yx, yy, c)) for t in range(n)]
                 + [rc(2, t, ins[t], me, (xx, xy, c)) for t in range(n)])
        passed = [[rc(4 + j, t, outs[t].at[2 * pc + c], 2 * pc + c, sib) for t in range(n)]
                  for j, (_, _, pc) in enumerate(chips)]
        relays = [[rc(3, t, outs[t].at[2 * y_chip + c], 2 * y_chip + c, (xx, xy, c)) for t in range(n)],
                  [rc(3, t, outs[t].at[2 * x_chip + c], 2 * x_chip + c, (yx, yy, c)) for t in range(n)]]
        return rc, local, sends, passed, relays, chips, chip, c, sib

    def start(ins, outs, sems):
        _, local, sends, _, _, _, _, _, _ = copies(ins, outs, sems)
        for cp in local + sends:
            cp.start()

    def middle(ins, outs, sems):
        rc, _, _, passed, relays, chips, _, c, sib = copies(ins, outs, sems)
        for j in range(2):
            for t in range(n):
                rc(1 + j, t, ins[t], 2 * chips[j][2] + c, sib).wait_recv()
        for j in range(2):
            for cp in passed[j]:
                cp.start()

            @pl.when(c == j)
            def _():
                for cp in relays[j]:
                    cp.start()

    def finish(ins, outs, sems):
        rc, local, sends, passed, relays, chips, chip, c, sib = copies(ins, outs, sems)
        far = 2 * chips[2][2] + c
        for t in range(n):
            rc(3, t, ins[t], far, sib).wait_recv()
        for cp in passed[2]:
            cp.start()
        for t in range(n):
            rc(0, t, ins[t], 2 * chip + 1 - c, sib).wait_recv()
        for j, (px, py, pc) in enumerate(chips):
            for t in range(n):
                rc(4 + j, t, ins[t], 2 * pc + 1 - c, sib).wait_recv()
        for cp in sends + passed[0] + passed[1] + passed[2]:
            cp.wait_send()
        for j in range(2):
            @pl.when(c == j)
            def _():
                for cp in relays[j]:
                    cp.wait_send()
        for cp in local:
            cp.wait()

    return _Plan(list(shards), [SDS((N_SLOT,) + tuple(a.shape), a.dtype) for a in shards],
                 [pltpu.SemaphoreType.DMA((7 * n,)), pltpu.SemaphoreType.DMA((7 * n,)),
                  pltpu.SemaphoreType.DMA((n,))], start, finish, middle, middle_at)


def _sibling_plan(grads, whole=()):
    n, m = len(grads), len(whole)

    def copies(ins, outs, sems):
        send_sems, recv_sems = sems
        x, y, c = _place()
        sib = (x, y, 1 - c)

        def rc(t, src, dst):
            return pltpu.make_async_remote_copy(src_ref=src, dst_ref=dst, send_sem=send_sems.at[t],
                                                recv_sem=recv_sems.at[t], device_id=sib, device_id_type=MESH)
        return rc, c

    def start(ins, outs, sems):
        rc, c = copies(ins, outs, sems)
        for t in range(n):
            for j in range(4):
                rc(t, ins[t].at[2 * j + 1 - c], outs[t].at[j]).start()
        for t in range(n, n + m):
            rc(t, ins[t], outs[t]).start()

    def finish(ins, outs, sems):
        rc, _ = copies(ins, outs, sems)
        for t in range(n):
            rc(t, ins[t].at[pl.ds(0, 4)], outs[t]).wait()
        for t in range(n, n + m):
            rc(t, ins[t], outs[t]).wait()

    return _Plan(list(grads) + list(whole),
                 [SDS((4,) + tuple(g.shape[1:]), g.dtype) for g in grads] + [SDS(a.shape, a.dtype) for a in whole],
                 [pltpu.SemaphoreType.DMA((n + m,)), pltpu.SemaphoreType.DMA((n + m,))], start, finish)


def _chips_plan(parts, whole=()):
    n, m = len(parts), len(whole)

    def src_of(ins, t, pc):
        return ins[t].at[pc] if t < n else ins[t]

    def local_copies(ins, outs, sems, chip):
        return [pltpu.make_async_copy(src_of(ins, t, chip), outs[t].at[chip], sems[2].at[t]) for t in range(n + m)]

    def start(ins, outs, sems):
        send_sems, recv_sems, _ = sems
        x, y, c = _place()
        chip = 2 * x + y
        for cp in local_copies(ins, outs, sems, chip):
            cp.start()
        for px, py, pc in _other_chips(x, y):
            for t in range(n + m):
                pltpu.make_async_remote_copy(src_ref=src_of(ins, t, pc), dst_ref=outs[t].at[chip],
                                             send_sem=send_sems.at[t], recv_sem=recv_sems.at[t],
                                             device_id=(px, py, c), device_id_type=MESH).start()

    def finish(ins, outs, sems):
        send_sems, recv_sems, _ = sems
        x, y, c = _place()
        for t in range(n + m):
            three = outs[t].at[pl.ds(0, 3)]
            pltpu.make_async_remote_copy(src_ref=three, dst_ref=three, send_sem=send_sems.at[t],
                                         recv_sem=recv_sems.at[t], device_id=(x, y, c), device_id_type=MESH).wait()
        for cp in local_copies(ins, outs, sems, 2 * x + y):
            cp.wait()

    return _Plan(list(parts) + list(whole),
                 [SDS(p.shape, p.dtype) for p in parts] + [SDS((4,) + tuple(a.shape), a.dtype) for a in whole],
                 [pltpu.SemaphoreType.DMA((n + m,)), pltpu.SemaphoreType.DMA((n + m,)),
                  pltpu.SemaphoreType.DMA((n + m,))], start, finish)


def _exchange_plan(arr):
    def peers(x, y, c):
        flip = lambda v, f: 1 - v if f else v
        return [(flip(x, fx), flip(y, fy), flip(c, fc))
                for fx in (0, 1) for fy in (0, 1) for fc in (0, 1) if fx or fy or fc]

    def start(ins, outs, sems):
        x, y, c = _place()
        me = 4 * x + 2 * y + c
        pltpu.make_async_copy(ins[0], outs[0].at[me], sems[2].at[0]).start()
        for to in peers(x, y, c):
            pltpu.make_async_remote_copy(src_ref=ins[0], dst_ref=outs[0].at[me], send_sem=sems[0].at[0],
                                         recv_sem=sems[1].at[0], device_id=to, device_id_type=MESH).start()

    def finish(ins, outs, sems):
        x, y, c = _place()
        seven = outs[0].at[pl.ds(0, 7)]
        pltpu.make_async_remote_copy(src_ref=seven, dst_ref=seven, send_sem=sems[0].at[0], recv_sem=sems[1].at[0],
                                     device_id=(x, y, c), device_id_type=MESH).wait()
        pltpu.make_async_copy(ins[0], outs[0].at[4 * x + 2 * y + c], sems[2].at[0]).wait()

    return _Plan([arr], [SDS((N_SLOT,) + tuple(arr.shape), arr.dtype)],
                 [pltpu.SemaphoreType.DMA((1,)), pltpu.SemaphoreType.DMA((1,)), pltpu.SemaphoreType.DMA((1,))],
                 start, finish)


def _join(*plans):
    def cut(seq, sizes):
        out, at = [], 0
        for k in sizes:
            out.append(seq[at:at + k])
            at += k
        return out

    n_arr = [len(p.arrays) for p in plans]
    n_sem = [len(p.sems) for p in plans]

    def start(ins, outs, sems):
        for p, i, o, s in zip(plans, cut(ins, n_arr), cut(outs, n_arr), cut(sems, n_sem)):
            p.start(i, o, s)

    def finish(ins, outs, sems):
        for p, i, o, s in zip(plans, cut(ins, n_arr), cut(outs, n_arr), cut(sems, n_sem)):
            p.finish(i, o, s)

    def middle(ins, outs, sems):
        for p, i, o, s in zip(plans, cut(ins, n_arr), cut(outs, n_arr), cut(sems, n_sem)):
            if p.middle is not None:
                p.middle(i, o, s)

    return _Plan([a for p in plans for a in p.arrays], [o for p in plans for o in p.out_shape],
                 [s for p in plans for s in p.sems], start, finish,
                 middle if any(p.middle is not None for p in plans) else None,
                 max(p.middle_at for p in plans))


def _run_plan(name, plan):
    k = len(plan.arrays)

    def body(*refs):
        ins, outs, sems = refs[:k], refs[k:2 * k], refs[2 * k:]
        plan.start(ins, outs, sems)
        if plan.middle is not None:
            plan.middle(ins, outs, sems)
        plan.finish(ins, outs, sems)

    return pl.pallas_call(
        body, name=name, in_specs=[ANY] * k, out_specs=[ANY] * k, out_shape=plan.out_shape,
        scratch_shapes=plan.sems, compiler_params=pltpu.CompilerParams(has_side_effects=True),
    )(*plan.arrays)


def _call(body, *, name, grid, in_specs, out_specs, out_shape, scratch_shapes, params, args, comm=None,
          aliases=None, prefetch=()):
    aliases = aliases or {}
    n_pre = len(prefetch)

    def launch(fn, ins_specs, outs_specs, outs_shape, scratch, operands):
        spec = pltpu.PrefetchScalarGridSpec(num_scalar_prefetch=n_pre, grid=grid, in_specs=ins_specs,
                                            out_specs=outs_specs, scratch_shapes=scratch)
        return pl.pallas_call(fn, name=name, grid_spec=spec, out_shape=outs_shape, compiler_params=params,
                              input_output_aliases=aliases)(*prefetch, *[_small_in_hbm(a) for a in operands])

    if comm is None:
        return list(launch(body, in_specs, out_specs, out_shape, scratch_shapes, args)), []
    n_in, n_out, n_scr, k = len(in_specs), len(out_specs), len(scratch_shapes), len(comm.arrays)

    def wrapped(*refs):
        pre, refs = refs[:n_pre], refs[n_pre:]
        ins = refs[:n_in]
        c_in = refs[n_in:n_in + k]
        outs = refs[n_in + k:n_in + k + n_out]
        c_out = refs[n_in + k + n_out:n_in + 2 * k + n_out]
        scr = refs[n_in + 2 * k + n_out:n_in + 2 * k + n_out + n_scr]
        sems = refs[n_in + 2 * k + n_out + n_scr:]
        step, steps = pl.program_id(0), grid[0]
        for d in range(1, len(grid)):
            step, steps = step * grid[d] + pl.program_id(d), steps * grid[d]

        @pl.when(step == 0)
        def _():
            comm.start(c_in, c_out, sems)

        if comm.middle is not None:
            @pl.when(step == (comm.middle_at * steps) // 8)
            def _():
                comm.middle(c_in, c_out, sems)

        body(*pre, *ins, *outs, *scr)

        @pl.when(step == steps - 1)
        def _():
            comm.finish(c_in, c_out, sems)

    res = launch(wrapped, list(in_specs) + [ANY] * k, list(out_specs) + [ANY] * k,
                 list(out_shape) + list(comm.out_shape), list(scratch_shapes) + list(comm.sems),
                 tuple(args) + tuple(comm.arrays))
    return list(res[:n_out]), list(res[n_out:])


def _row_tile(rows):
    for t in (512, 256, 128, 64, 32, 16, 8):
        if rows % t == 0:
            return t
    return rows


def _pair_sum(name, g8, recv4, core):
    _, rows, cols = recv4.shape
    tr = _row_tile(rows)
    g42 = g8.reshape(4, 2, rows, cols)

    def body(c_ref, g_ref, r_ref, o_ref):
        del c_ref
        o_ref[...] = (g_ref[...].astype(F32) + r_ref[...].astype(F32)).astype(o_ref.dtype)

    return pl.pallas_call(
        body, name=name,
        grid_spec=pltpu.PrefetchScalarGridSpec(
            num_scalar_prefetch=1, grid=(4, rows // tr),
            in_specs=[pl.BlockSpec((None, None, tr, cols), lambda j, i, c_ref: (j, c_ref[0], i, 0)),
                      pl.BlockSpec((None, tr, cols), lambda j, i, c_ref: (j, i, 0))],
            out_specs=pl.BlockSpec((None, tr, cols), lambda j, i, c_ref: (j, i, 0))),
        out_shape=SDS(recv4.shape, g8.dtype),
        compiler_params=_cparams(("arbitrary", "arbitrary"), 32),
    )(core, g42, recv4)


def _add2(name, a, b):
    rows, cols = a.shape
    tr = _row_tile(rows)

    def body(a_ref, b_ref, o_ref):
        o_ref[...] = a_ref[...] + b_ref[...]

    blk = pl.BlockSpec((tr, cols), lambda i: (i, 0))
    return pl.pallas_call(body, name=name, grid=(rows // tr,), in_specs=[blk, blk], out_specs=blk,
                          out_shape=SDS(a.shape, a.dtype),
                          compiler_params=_cparams(("arbitrary",), 32))(a, b)


def _sum_terms(name, terms):
    k, rows, cols = terms.shape
    tr = _row_tile(rows)

    def body(r_ref, o_ref):
        acc = r_ref[0]
        for q in range(1, k):
            acc = acc + r_ref[q]
        o_ref[...] = acc

    return pl.pallas_call(body, name=name, grid=(rows // tr,),
                          in_specs=[pl.BlockSpec((k, tr, cols), lambda i: (0, i, 0))],
                          out_specs=pl.BlockSpec((tr, cols), lambda i: (i, 0)),
                          out_shape=SDS((rows, cols), terms.dtype),
                          compiler_params=_cparams(("arbitrary",), 32))(terms)


def _adam_update(g, w, m, v):
    c1 = 1.0 / (1.0 - ADAM_B1 ** ADAM_STEP)
    c2 = 1.0 / (1.0 - ADAM_B2 ** ADAM_STEP)
    mn = ADAM_B1 * m + (1.0 - ADAM_B1) * g
    vn = ADAM_B2 * v + (1.0 - ADAM_B2) * (g * g)
    delta = (-ADAM_LR) * ((mn * c1) / (jnp.sqrt(vn * c2) + ADAM_EPS) + ADAM_WD * w)
    return delta, mn, vn


def _adamw_many(name, gs, ws, ms, vs):
    n = len(gs)

    def body(*refs):
        for p in range(n):
            g, w, m, v = (refs[q * n + p][...] for q in range(4))
            d, mn, vn = _adam_update(g, w, m, v)
            refs[4 * n + p][...] = d
            refs[5 * n + p][...] = mn
            refs[6 * n + p][...] = vn

    full = [pl.BlockSpec(w.shape, lambda i: (0, 0)) for w in ws]
    shapes = [SDS(w.shape, F32) for w in ws]
    res = pl.pallas_call(body, name=name, grid=(1,), in_specs=full * 4, out_specs=full * 3, out_shape=shapes * 3,
                         compiler_params=_cparams(("arbitrary",), 32),
                         )(*[_small_in_hbm(a) for a in (*gs, *ws, *ms, *vs)])
    return [(res[p], res[n + p], res[2 * n + p]) for p in range(n)]


def _adamw(name, terms, w, m, v):
    k, rows, cols = terms.shape
    tr = _row_tile(rows)

    def body(t_ref, w_ref, m_ref, v_ref, g_ref, d_ref, mo_ref, vo_ref):
        g = t_ref[0].astype(F32)
        for q in range(1, k):
            g = g + t_ref[q].astype(F32)
        g_ref[...] = g
        d_ref[...], mo_ref[...], vo_ref[...] = _adam_update(g, w_ref[...], m_ref[...], v_ref[...])

    blk = pl.BlockSpec((tr, cols), lambda i: (i, 0))
    return pl.pallas_call(body, name=name, grid=(rows // tr,),
                          in_specs=[pl.BlockSpec((k, tr, cols), lambda i: (0, i, 0)), blk, blk, blk],
                          out_specs=[blk] * 4, out_shape=[SDS((rows, cols), F32)] * 4,
                          compiler_params=_cparams(("arbitrary",), 40),
                          )(*[pltpu.with_memory_space_constraint(a, pltpu.HBM) for a in (terms, w, m, v)])


def kernel(x, norm_mix_g, w_in, conv_w, conv_b, w_rgate, b_rgate, w_igate, b_igate, lru_lambda, w_out_a, sgu_ln_g, sgu_ln_b, sgu_w_s, sgu_b_s, w_out_b, w_out, norm_mlp_g, w_up, w_down, norm_final_g, loss_target, m_norm_mix_g, m_w_in, m_conv_w, m_conv_b, m_w_rgate, m_b_rgate, m_w_igate, m_b_igate, m_lru_lambda, m_w_out_a, m_sgu_ln_g, m_sgu_ln_b, m_sgu_w_s, m_sgu_b_s, m_w_out_b, m_w_out, m_norm_mlp_g, m_w_up, m_w_down, m_norm_final_g, v_norm_mix_g, v_w_in, v_conv_w, v_conv_b, v_w_rgate, v_b_rgate, v_w_igate, v_b_igate, v_lru_lambda, v_w_out_a, v_sgu_ln_g, v_sgu_ln_b, v_sgu_w_s, v_sgu_b_s, v_w_out_b, v_w_out, v_norm_mlp_g, v_w_up, v_w_down, v_norm_final_g):
    cx, cy, cc = _place()
    me = 4 * cx + 2 * cy + cc
    core = jnp.reshape(cc, (1,)).astype(jnp.int32)
    xs = x[0]
    tgt = loss_target[0]

    gate_shard = jnp.stack([w_rgate[0], w_igate[0]]).astype(BF16).reshape(2 * HEADS * 32, HEAD_DIM)
    vec_shard = jnp.concatenate([conv_w[0], b_rgate[0], b_igate[0]], axis=1)
    vec_shard = jnp.pad(vec_shard, ((0, 4), (0, 256 - vec_shard.shape[1])))
    shards = [w_in[0].astype(BF16), w_out_a[0].astype(BF16), w_out_b[0].astype(BF16), w_out[0].astype(BF16),
              w_up[0].astype(BF16), w_down[0].astype(BF16), gate_shard, vec_shard]
    (z, n1_t, w_in_g), (gate_g, vec_g, w_oa_g, w_ob_g, w_out_g) = _in_proj(
        xs, norm_mix_g, shards[0], _slot_order(cx, cy, cc), comm=_gather_plan(shards[6:8] + shards[1:4], middle_at=4))
    gates = gate_g.reshape(N_SLOT, 2, HEADS, 32, HEAD_DIM).transpose(1, 2, 0, 3, 4).reshape(2, HEADS, HEAD_DIM, HEAD_DIM)
    w_r_f, w_i_f = gates[0], gates[1]
    conv_w_f = vec_g[:, 0:4, 0:128].transpose(1, 0, 2).reshape(CONV_K, D)
    b_r_f = vec_g[:, 0:4, 128:160].transpose(1, 0, 2).reshape(1, D)
    b_i_f = vec_g[:, 0:4, 160:192].transpose(1, 0, 2).reshape(1, D)
    b_s_t = jnp.transpose(sgu_b_s[0])

    (ya, hs, xc, r_gate, i_gate), (w_up_g,) = _branch_a_fwd(
        z, conv_w_f, conv_b, w_r_f, b_r_f, w_i_f, b_i_f, lru_lambda, comm=_gather_plan(shards[4:5], middle_at=3))
    w_oa_f = w_oa_g.reshape(D, D)
    w_ob_f = w_ob_g.reshape(D, D)
    w_out_f = w_out_g.reshape(D, D)
    (yb,), _ = _branch_b_fwd(z, sgu_ln_g, sgu_ln_b, sgu_w_s[0], b_s_t)
    (pa, pb, merged, h1), (w_down_g,) = _merge_out(ya, yb, z, xs, w_oa_f, w_ob_f, w_out_f,
                                                   comm=_gather_plan(shards[5:6], middle_at=4))
    w_down_f = w_down_g.reshape(N_SLOT * FF_COLS, D)
    r_act, act_t, n2_t, dh2, loss_acc, d_gfin = _mlp_fwd(h1, norm_mlp_g, w_up_g, w_down_f,
                                                         norm_final_g.reshape(1, D), tgt)

    def pair(names, grads, recv):
        return [_pair_sum("pair_sum_" + nm, g, r, core) for nm, g, r in zip(names, grads, recv)]

    g_down = _wgrad_t("wgrad_down", act_t, dh2, N_SLOT // 2, True, False, 2048)
    g_down = g_down.reshape(N_SLOT, FF_COLS, D)
    (df, dh1, d_gmlp), (r_down,) = _mlp_bwd(dh2, r_act, w_down_f, w_up_g, h1, norm_mlp_g,
                                            comm=_sibling_plan([g_down]))
    (p_down,) = pair(["down"], [g_down], [r_down])
    g_up = _wgrad_t("wgrad_up", n2_t, df, N_SLOT, False, True, 4096)
    g_out = _wgrad("wgrad_out", merged, dh1, 1, False, False).reshape(N_SLOT, D // N_SLOT, D)
    (dz, dpa, dpb, dya, dyb), (got_down, r_up, r_out) = _merge_bwd(
        dh1, z, pa, pb, w_out_f, w_oa_f, w_ob_f, comm=_join(_chips_plan([p_down]), _sibling_plan([g_up, g_out])))
    p_up, p_out = pair(["up", "out"], [g_up, g_out], [r_up, r_out])
    g_oa = _wgrad("wgrad_out_a", ya, dpa, 1, False, False).reshape(N_SLOT, D // N_SLOT, D)
    g_ob = _wgrad("wgrad_out_b", yb, dpb, 1, False, False).reshape(N_SLOT, D // N_SLOT, D)
    (dz, d_ws, d_bs, d_ln), (got_up, r_oa, r_ob) = _branch_b_bwd(
        dz, dyb, z, sgu_ln_g, sgu_ln_b, sgu_w_s[0], b_s_t,
        comm=_join(_chips_plan([p_up]), _sibling_plan([g_oa, g_ob])))
    p_oa, p_ob = pair(["out_a", "out_b"], [g_oa, g_ob], [r_oa, r_ob])
    (dz, d_vec, d_wr, d_wi), (got_out, got_oa, got_ob) = _branch_a_bwd(
        dz, dya, z, hs, xc, r_gate, i_gate, conv_w_f, w_r_f, w_i_f, lru_lambda,
        comm=_chips_plan([p_out, p_oa, p_ob]))
    g_in = _wgrad_t("wgrad_in", n1_t, dz, N_SLOT, False, True, 4096)
    g_gate = jnp.stack([d_wr, d_wi]).reshape(2, HEADS, N_SLOT, 32, HEAD_DIM).transpose(2, 0, 1, 3, 4)
    g_gate = g_gate.reshape(N_SLOT, 2 * HEADS * 32, HEAD_DIM).astype(BF16)

    d_bs_row = jnp.pad(d_bs[:, :, 0].reshape(1, GROUPS * CHUNK), ((0, 0), (0, D - GROUPS * CHUNK)))
    vecs = jnp.concatenate([d_vec, jnp.concatenate([d_ln[0:2], d_gmlp, d_gfin, d_bs_row, jnp.zeros((3, D), F32)])])
    d_ws2 = d_ws.reshape(GROUPS * CHUNK, CHUNK)
    r_in, r_gate, r_vecs, r_ws = _run_plan("rs_sibling_in", _sibling_plan([g_in, g_gate], [vecs, d_ws2]))
    p_in, p_gate = pair(["in", "gate"], [g_in, g_gate], [r_in, r_gate])
    vecs_chip = _add2("pair_sum_vecs", vecs, r_vecs)
    ws_chip = _add2("pair_sum_ws", d_ws2, r_ws)
    (dx, d_gmix), (got_in, got_gate, got_vecs, got_ws) = _in_bwd(
        dz, w_in_g, xs, dh1, norm_mix_g, comm=_chips_plan([p_in, p_gate], [vecs_chip, ws_chip]))
    vecs_sum = _sum_terms("sum_vecs", got_vecs)
    last = jnp.concatenate([d_gmix, jnp.pad(loss_acc[0:1], ((0, 0), (0, D - 128))), jnp.zeros((6, D), F32)])
    (last_all,) = _run_plan("exchange_last", _exchange_plan(last))
    last_sum = _sum_terms("sum_last", last_all)
    loss = last_sum[1, 0]
    got = [got_in, got_oa, got_ob, got_out, got_up, got_down, got_gate]

    def step(nm, terms, w, m, v, rows, cols):
        g, d, mn, vn = _adamw("adamw_" + nm, terms.reshape(4, rows, cols), w.reshape(rows, cols),
                              m.reshape(rows, cols), v.reshape(rows, cols))
        return [a.reshape(w.shape) for a in (g, d, mn, vn)]

    o_in = step("in", got[0], w_in, m_w_in, v_w_in, D, W_IN_COLS)
    o_oa = step("out_a", got[1], w_out_a, m_w_out_a, v_w_out_a, D // N_SLOT, D)
    o_ob = step("out_b", got[2], w_out_b, m_w_out_b, v_w_out_b, D // N_SLOT, D)
    o_out = step("out", got[3], w_out, m_w_out, v_w_out, D // N_SLOT, D)
    o_up = step("up", got[4], w_up, m_w_up, v_w_up, D, FF_COLS)
    o_down = step("down", got[5], w_down, m_w_down, v_w_down, FF_COLS, D)
    gate_w = jnp.stack([w_rgate[0], w_igate[0]]).reshape(2 * HEADS * 32, HEAD_DIM)
    gate_m = jnp.stack([m_w_rgate[0], m_w_igate[0]]).reshape(2 * HEADS * 32, HEAD_DIM)
    gate_v = jnp.stack([v_w_rgate[0], v_w_igate[0]]).reshape(2 * HEADS * 32, HEAD_DIM)
    o_gate = _adamw("adamw_gate", got[6], gate_w, gate_m, gate_v)
    o_gate = [a.reshape(2, 1, HEADS, 32, HEAD_DIM) for a in o_gate]
    o_wr = [a[0] for a in o_gate]
    o_wi = [a[1] for a in o_gate]

    def own(full, width):
        return lax.dynamic_slice_in_dim(full, me * width, width, axis=1)

    small_g = {
        "norm_mix_g": last_sum[0:1], "conv_w": own(vecs_sum[0:4], 128), "conv_b": vecs_sum[4:5],
        "b_rgate": own(vecs_sum[5:6].reshape(HEADS, HEAD_DIM), 32),
        "b_igate": own(vecs_sum[6:7].reshape(HEADS, HEAD_DIM), 32),
        "lru_lambda": vecs_sum[7:8], "sgu_ln_g": vecs_sum[8:9], "sgu_ln_b": vecs_sum[9:10],
        "norm_mlp_g": vecs_sum[10:11], "norm_final_g": vecs_sum[11:12],
        "sgu_b_s": vecs_sum[12, 0:GROUPS * CHUNK].reshape(GROUPS, CHUNK),
    }
    small_w = {"norm_mix_g": (norm_mix_g, m_norm_mix_g, v_norm_mix_g), "conv_w": (conv_w, m_conv_w, v_conv_w),
               "conv_b": (conv_b, m_conv_b, v_conv_b), "b_rgate": (b_rgate, m_b_rgate, v_b_rgate),
               "b_igate": (b_igate, m_b_igate, v_b_igate), "lru_lambda": (lru_lambda, m_lru_lambda, v_lru_lambda),
               "sgu_ln_g": (sgu_ln_g, m_sgu_ln_g, v_sgu_ln_g), "sgu_ln_b": (sgu_ln_b, m_sgu_ln_b, v_sgu_ln_b),
               "norm_mlp_g": (norm_mlp_g, m_norm_mlp_g, v_norm_mlp_g),
               "norm_final_g": (norm_final_g, m_norm_final_g, v_norm_final_g),
               "sgu_b_s": (sgu_b_s, m_sgu_b_s, v_sgu_b_s), "sgu_w_s": (sgu_w_s, m_sgu_w_s, v_sgu_w_s)}
    order = list(small_g)
    as2d = lambda k, a: a.reshape(small_g[k].shape)
    upd = _adamw_many("adamw_small", [small_g[k] for k in order], *[[as2d(k, small_w[k][q]) for k in order]
                                                                     for q in range(3)])
    o_small = {k: [a.reshape(small_w[k][0].shape) for a in (small_g[k],) + u] for k, u in zip(order, upd)}
    ws3 = [a[0].reshape(GROUPS * CHUNK, CHUNK) for a in small_w.pop("sgu_w_s")]
    o_small["sgu_w_s"] = [a.reshape(sgu_w_s.shape) for a in _adamw("adamw_ws", got_ws, *ws3)]

    per_weight = {"norm_mix_g": o_small["norm_mix_g"], "w_in": o_in, "conv_w": o_small["conv_w"],
                  "conv_b": o_small["conv_b"], "w_rgate": o_wr, "b_rgate": o_small["b_rgate"], "w_igate": o_wi,
                  "b_igate": o_small["b_igate"], "lru_lambda": o_small["lru_lambda"], "w_out_a": o_oa,
                  "sgu_ln_g": o_small["sgu_ln_g"], "sgu_ln_b": o_small["sgu_ln_b"], "sgu_w_s": o_small["sgu_w_s"],
                  "sgu_b_s": o_small["sgu_b_s"], "w_out_b": o_ob, "w_out": o_out, "norm_mlp_g": o_small["norm_mlp_g"],
                  "w_up": o_up, "w_down": o_down, "norm_final_g": o_small["norm_final_g"]}
    names_w = list(per_weight)
    return (loss, dx[None], *[per_weight[k][0] for k in names_w], *[per_weight[k][1] for k in names_w],
            *[per_weight[k][2] for k in names_w], *[per_weight[k][3] for k in names_w])
```

```python
import jax
import jax.numpy as jnp
from jax import lax
from jax.experimental import pallas as pl
from jax.experimental.pallas import tpu as pltpu

F32 = jnp.float32
BF16 = jnp.bfloat16
SDS = jax.ShapeDtypeStruct
MESH = pl.DeviceIdType.MESH
ANY = pl.BlockSpec(memory_space=pl.ANY)

D = 1024
N_SLOT = 8
W_IN_COLS = 768
FF_COLS = 512
HEADS, HEAD_DIM = 4, 256
GROUPS, GROUP_DIM = 4, 256
CHUNK = 128
CONV_K = 4
NORM_EPS = 1e-6
LN_EPS = 1e-5
LRU_C = 8.0
ADAM_LR, ADAM_B1, ADAM_B2, ADAM_EPS, ADAM_WD, ADAM_STEP = 0.001, 0.9, 0.999, 1e-08, 0.01, 10

TM_ROWS = 1024
TM_MERGE = 512
T_BRANCH_A = 512
T_BRANCH_B = 256
MiB = 1024 * 1024
SMALL_OPERAND = 16 * 1024

_GELU_C = 0.7978845608028654
_GELU_A = 0.044715


def _small_in_hbm(a):
    return pltpu.with_memory_space_constraint(a, pltpu.HBM) if a.size <= SMALL_OPERAND else a


def _cparams(sem, vmem_mib):
    return pltpu.CompilerParams(dimension_semantics=sem, vmem_limit_bytes=vmem_mib * MiB)


def _gelu(x):
    t = jnp.tanh(_GELU_C * (x + _GELU_A * x * x * x))
    return 0.5 * x * (1.0 + t)


def _gelu_and_grad(x):
    x2 = x * x
    t = jnp.tanh(_GELU_C * x * (1.0 + _GELU_A * x2))
    g = 0.5 * x * (1.0 + t)
    dg = 0.5 * (1.0 + t) + 0.5 * x * (1.0 - t * t) * _GELU_C * (1.0 + 3.0 * _GELU_A * x2)
    return g, dg


def _softplus(x):
    return jnp.maximum(x, 0.0) + jnp.log1p(jnp.exp(-jnp.abs(x)))


def _dot(a, b):
    return jnp.dot(a, b, preferred_element_type=F32)


def _dot_nt(a, b):
    return lax.dot_general(a, b, (((1,), (1,)), ((), ())), preferred_element_type=F32)


def _dot_tn(a, b):
    return lax.dot_general(a, b, (((0,), (0,)), ((), ())), preferred_element_type=F32)


def _rows_shifted(prev8, cur, k):
    ext = jnp.concatenate([prev8, cur], axis=0)
    return pltpu.roll(ext, k, 0)[8:]


def _rows_advanced(cur, next8, k):
    t = cur.shape[0]
    ext = jnp.concatenate([cur, next8], axis=0)
    return pltpu.roll(ext, t + 8 - k, 0)[:t]


def _first_second(x, y, c):
    ny, nx, far = _other_chips(x, y)
    pick = lambda a, b: a * (1 - c) + b * c
    first = tuple(pick(a, b) for a, b in zip(ny, nx))
    second = tuple(pick(b, a) for a, b in zip(ny, nx))
    return first, second, far


def _slot_order(x, y, c):
    chip = 2 * x + y
    first, second, far = _first_second(x, y, c)
    order = [2 * chip + c, 2 * chip + 1 - c, 2 * first[2] + c, 2 * second[2] + 1 - c, 2 * second[2] + c,
             2 * first[2] + 1 - c, 2 * far[2] + c, 2 * far[2] + 1 - c]
    return jnp.stack(order).astype(jnp.int32)


def _in_proj(x, g_mix, w_in_own, order, comm=None):
    s = x.shape[0]
    tm = min(TM_ROWS, s)
    ni = s // tm

    def body(order_ref, x_ref, g_ref, own_ref, z_ref, nt_ref, wg_ref, n_s, w_s, send_sems, recv_sems, local_sems):
        j, i = pl.program_id(0), pl.program_id(1)
        px, py, c = _place()
        chip = 2 * px + py
        me = 2 * chip + c
        sib = (px, py, 1 - c)
        chips = _other_chips(px, py)

        def rc(k, src, blk, to):
            return pltpu.make_async_remote_copy(src_ref=src, dst_ref=w_s.at[blk], send_sem=send_sems.at[k],
                                                recv_sem=recv_sems.at[k], device_id=to, device_id_type=MESH)

        del chips
        first, second, far = _first_second(px, py, c)
        blocks = [2 * first[2] + c, 2 * second[2] + c, 2 * far[2] + c]
        own_in = pltpu.make_async_copy(own_ref, w_s.at[me], local_sems.at[0])
        to_first = rc(1, own_ref, me, (first[0], first[1], c))
        to_second = rc(2, own_ref, me, (second[0], second[1], c))
        relay = rc(3, w_s.at[blocks[0]], blocks[0], (second[0], second[1], c))
        sends = [rc(0, own_ref, me, sib), to_first, to_second, relay]
        passed = [rc(4 + q, w_s.at[blk], blk, sib) for q, blk in enumerate(blocks)]
        keep = pltpu.make_async_copy(w_s, wg_ref, local_sems.at[1])

        @pl.when((i == 0) & (j == 0))
        def _():
            own_in.start()
            sends[0].start()
            to_first.start()
            own_in.wait()

        @pl.when((i == 0) & (j == 1))
        def _():
            rc(0, own_ref, 2 * chip + 1 - c, sib).wait_recv()

        for q, blk in enumerate(blocks):
            @pl.when((i == 0) & (j == 2 + 2 * q))
            def _():
                rc(1 + q, own_ref, blk, sib).wait_recv()
                passed[q].start()
                if q == 0:
                    to_second.start()
                    relay.start()

            @pl.when((i == 0) & (j == 3 + 2 * q))
            def _():
                rc(4 + q, own_ref, order_ref[j], sib).wait_recv()

        rows = pl.ds(pl.multiple_of(i * tm, tm), tm)

        @pl.when(j == 0)
        def _():
            xv = x_ref[...]
            rstd = lax.rsqrt(jnp.mean(xv * xv, axis=-1, keepdims=True) + NORM_EPS)
            nb = (xv * rstd * g_ref[...]).astype(BF16)
            n_s[rows, :] = nb
            nt_ref[...] = nb.T

        z_ref[...] = _dot(n_s[rows, :], w_s[order_ref[j]]).astype(BF16)

        @pl.when((i == 0) & (j == N_SLOT - 1))
        def _():
            keep.start()

        @pl.when((i == ni - 1) & (j == N_SLOT - 1))
        def _():
            for cp in sends + passed:
                cp.wait_send()
            keep.wait()

    first_pass = lambda j, i, o: (jnp.where(j == 0, i, ni - 1), 0)
    (z, n1, w_in_g), extra = _call(
        body, name="in_proj", grid=(N_SLOT, ni), prefetch=(order,),
        in_specs=[pl.BlockSpec((tm, D), first_pass),
                  pl.BlockSpec((1, D), lambda j, i, o: (0, 0)), ANY],
        out_specs=[pl.BlockSpec((tm, W_IN_COLS), lambda j, i, o: (i, o[j])),
                   pl.BlockSpec((D, tm), lambda j, i, o: (0, jnp.where(j == 0, i, ni - 1))), ANY],
        out_shape=[SDS((s, N_SLOT * W_IN_COLS), BF16), SDS((D, s), BF16), SDS((N_SLOT, D, W_IN_COLS), BF16)],
        scratch_shapes=[pltpu.VMEM((s, D), BF16), pltpu.VMEM((N_SLOT, D, W_IN_COLS), BF16),
                        pltpu.SemaphoreType.DMA((7,)), pltpu.SemaphoreType.DMA((7,)), pltpu.SemaphoreType.DMA((2,))],
        params=_cparams(("arbitrary", "arbitrary"), 56), args=(x, g_mix, w_in_own), comm=comm)
    return (z, n1, w_in_g), extra


def _decay(r, sp_lam):
    log_a = (-LRU_C) * r * sp_lam
    a = jnp.exp(log_a)
    return a, jnp.sqrt(-jnp.tanh(log_a) * (a * a + 1.0))


def _lru_gates(xc, xcb, wr_ref, br, wi_ref, bi, sp_lam, a_s, b_s, r_ref, i_ref):
    for h in range(HEADS):
        sl = slice(h * HEAD_DIM, (h + 1) * HEAD_DIM)
        r = jax.nn.sigmoid(_dot(xcb[:, sl], wr_ref[h]) + br[:, sl])
        ig = jax.nn.sigmoid(_dot(xcb[:, sl], wi_ref[h]) + bi[:, sl])
        a, mult = _decay(r, sp_lam[:, sl])
        a_s[:, sl] = a
        b_s[:, sl] = xc[:, sl] * ig * mult
        r_ref[:, sl] = r.astype(BF16)
        i_ref[:, sl] = ig.astype(BF16)


def _conv_fwd(xa, prev8, cw, cb):
    xc = cb + cw[0:1, :] * xa
    for k in range(1, CONV_K):
        xc = xc + cw[k:k + 1, :] * _rows_shifted(prev8, xa, k)
    return xc


def _branch_a_fwd(z, conv_w, conv_b, w_r, b_r, w_i, b_i, lam, comm=None):
    s = z.shape[0]
    ta = min(T_BRANCH_A, s)
    per16 = ta // 16

    def body(xa_ref, xp_ref, ga_ref, cw_ref, cb_ref, wr_ref, br_ref, wi_ref, bi_ref, lam_ref,
             ya_ref, hs_ref, xc_ref, r_ref, i_ref, a_s, b_s, h_s, carry_s):
        i = pl.program_id(0)

        @pl.when(i == 0)
        def _():
            carry_s[...] = jnp.zeros_like(carry_s)

        xa = xa_ref[...].astype(F32)
        prev8 = jnp.where(i > 0, xp_ref[...].astype(F32)[8:16], 0.0)
        xc = _conv_fwd(xa, prev8, cw_ref[...], cb_ref[...])
        xcb = xc.astype(BF16)
        xc_ref[...] = xcb
        sp_lam = _softplus(-lam_ref[...])
        _lru_gates(xc, xcb, wr_ref, br_ref[...], wi_ref, bi_ref[...], sp_lam, a_s, b_s, r_ref, i_ref)

        row = lax.broadcasted_iota(jnp.int32, (8, D), 0)

        def group(g, carry):
            off = pl.multiple_of(g * 8, 8)
            a8 = a_s[pl.ds(off, 8), :]
            b8 = b_s[pl.ds(off, 8), :]
            for d in (1, 2, 4):
                a_sh = jnp.where(row >= d, pltpu.roll(a8, d, 0), 1.0)
                b_sh = jnp.where(row >= d, pltpu.roll(b8, d, 0), 0.0)
                b8 = a8 * b_sh + b8
                a8 = a8 * a_sh
            h8 = b8 + a8 * carry
            h_s[pl.ds(off, 8), :] = h8
            return jnp.broadcast_to(h8[7:8, :], (8, D))

        carry_s[...] = lax.fori_loop(0, ta // 8, group, carry_s[...])
        hs = h_s[...]
        hs_ref[...] = hs.astype(BF16)
        ya_ref[...] = (hs * _gelu(ga_ref[...].astype(F32))).astype(BF16)

    vec = pl.BlockSpec((1, D), lambda i: (0, 0))
    gate = pl.BlockSpec((HEADS, HEAD_DIM, HEAD_DIM), lambda i: (0, 0, 0))
    return _call(
        body, name="branch_a_fwd", grid=(s // ta,),
        in_specs=[pl.BlockSpec((ta, D), lambda i: (i, 0)),
                  pl.BlockSpec((16, D), lambda i: (jnp.maximum(i * per16 - 1, 0), 0)),
                  pl.BlockSpec((ta, D), lambda i: (i, 1)),
                  pl.BlockSpec((CONV_K, D), lambda i: (0, 0)), vec, gate, vec, gate, vec, vec],
        out_specs=[pl.BlockSpec((ta, D), lambda i: (i, 0))] * 5,
        out_shape=[SDS((s, D), BF16)] * 5,
        scratch_shapes=[pltpu.VMEM((ta, D), F32), pltpu.VMEM((ta, D), F32), pltpu.VMEM((ta, D), F32),
                        pltpu.VMEM((8, D), F32)],
        params=_cparams(("arbitrary",), 40), args=(z, z, z, conv_w, conv_b, w_r, b_r, w_i, b_i, lam), comm=comm)


def _sgu_common(ub, vb, lg, lb, with_grad):
    if with_grad:
        u, du = _gelu_and_grad(ub)
        v, dv = _gelu_and_grad(vb)
    else:
        u, v, du, dv = _gelu(ub), _gelu(vb), None, None
    mu = jnp.mean(v, axis=-1, keepdims=True)
    vc = v - mu
    rstd = lax.rsqrt(jnp.mean(vc * vc, axis=-1, keepdims=True) + LN_EPS)
    vhat = vc * rstd
    vln = vhat * lg + lb
    return u, du, dv, rstd, vhat, vln


def _masked_ws(ws_ref):
    t = lax.broadcasted_iota(jnp.int32, (CHUNK, CHUNK), 0)
    c = lax.broadcasted_iota(jnp.int32, (CHUNK, CHUNK), 1)
    keep = c <= t
    return [jnp.where(keep, ws_ref[g], 0.0).astype(BF16) for g in range(GROUPS)]


def _branch_b_fwd(z, ln_g, ln_b, w_s, b_s_t, comm=None):
    s = z.shape[0]
    tb = min(T_BRANCH_B, s)

    def body(ub_ref, vb_ref, lg_ref, lb_ref, ws_ref, bs_ref, yb_ref):
        u, _, _, _, _, vln = _sgu_common(ub_ref[...].astype(F32), vb_ref[...].astype(F32),
                                         lg_ref[...], lb_ref[...], False)
        vlnb = vln.astype(BF16)
        wm = _masked_ws(ws_ref)
        bs = bs_ref[...]
        for c in range(tb // CHUNK):
            rs = slice(c * CHUNK, (c + 1) * CHUNK)
            for g in range(GROUPS):
                cs = slice(g * GROUP_DIM, (g + 1) * GROUP_DIM)
                sp = _dot(wm[g], vlnb[rs, cs]) + bs[:, g:g + 1]
                yb_ref[rs, cs] = (u[rs, cs] * sp).astype(BF16)

    vec = pl.BlockSpec((1, D), lambda i: (0, 0))
    return _call(
        body, name="branch_b_fwd", grid=(s // tb,),
        in_specs=[pl.BlockSpec((tb, D), lambda i: (i, 2)), pl.BlockSpec((tb, D), lambda i: (i, 3)), vec, vec,
                  pl.BlockSpec((GROUPS, CHUNK, CHUNK), lambda i: (0, 0, 0)),
                  pl.BlockSpec((CHUNK, GROUPS), lambda i: (0, 0))],
        out_specs=[pl.BlockSpec((tb, D), lambda i: (i, 0))],
        out_shape=[SDS((s, D), BF16)], scratch_shapes=[],
        params=_cparams(("arbitrary",), 40), args=(z, z, ln_g, ln_b, w_s, b_s_t), comm=comm)


def _merge_out(ya, yb, z, x, w_oa, w_ob, w_out, comm=None):
    s = x.shape[0]
    tm = min(TM_MERGE, s)

    def body(ya_ref, yb_ref, ma_ref, mb_ref, x_ref, woa_ref, wob_ref, wo_ref, pa_ref, pb_ref, mg_ref, h1_ref):
        pa = _dot(ya_ref[...], woa_ref[...])
        pb = _dot(yb_ref[...], wob_ref[...])
        merged = (jax.nn.sigmoid(ma_ref[...].astype(F32)) * pa
                  + jax.nn.sigmoid(mb_ref[...].astype(F32)) * pb).astype(BF16)
        pa_ref[...] = pa.astype(BF16)
        pb_ref[...] = pb.astype(BF16)
        mg_ref[...] = merged
        h1_ref[...] = x_ref[...] + _dot(merged, wo_ref[...])

    row = pl.BlockSpec((tm, D), lambda i: (i, 0))
    wsp = pl.BlockSpec((D, D), lambda i: (0, 0))
    return _call(
        body, name="merge_out", grid=(s // tm,),
        in_specs=[row, row, pl.BlockSpec((tm, D), lambda i: (i, 4)), pl.BlockSpec((tm, D), lambda i: (i, 5)),
                  row, wsp, wsp, wsp],
        out_specs=[row, row, row, row],
        out_shape=[SDS((s, D), BF16), SDS((s, D), BF16), SDS((s, D), BF16), SDS((s, D), F32)], scratch_shapes=[],
        params=_cparams(("arbitrary",), 48), args=(ya, yb, z, z, x, w_oa, w_ob, w_out), comm=comm)


def _mlp_fwd(h1, g_mlp, w_up_g, w_down, g_fin, tgt):
    s = h1.shape[0]
    tm = min(TM_ROWS, s)
    nj = N_SLOT

    def body(h1_ref, gm_ref, wu_ref, wd_ref, gf_ref, t_ref, r_ref, at_ref, n2t_ref, dh2_ref, loss_ref, dgf_ref,
             n2_s, acc_s):
        i, j = pl.program_id(0), pl.program_id(1)

        @pl.when(j == 0)
        def _():
            hv = h1_ref[...]
            rstd = lax.rsqrt(jnp.mean(hv * hv, axis=-1, keepdims=True) + NORM_EPS)
            nb = (hv * rstd * gm_ref[...]).astype(BF16)
            n2_s[...] = nb
            n2t_ref[...] = nb.T
            acc_s[...] = jnp.zeros_like(acc_s)

        @pl.when((i == 0) & (j == 0))
        def _():
            loss_ref[...] = jnp.zeros_like(loss_ref)
            dgf_ref[...] = jnp.zeros_like(dgf_ref)

        r = jnp.maximum(_dot(n2_s[...], wu_ref[...]), 0.0)
        r_ref[...] = r.astype(BF16)
        act = (r * r).astype(BF16)
        at_ref[...] = act.T
        acc_s[...] += _dot(act, wd_ref[...])

        @pl.when(j == nj - 1)
        def _():
            h2 = h1_ref[...] + acc_s[...]
            rstd = lax.rsqrt(jnp.mean(h2 * h2, axis=-1, keepdims=True) + NORM_EPS)
            hh = h2 * rstd
            gf = gf_ref[...]
            e = hh * gf - t_ref[...]
            loss_ref[...] += jnp.sum(e * e) * (0.5 / D)
            dy = e * (1.0 / D)
            dgf_ref[...] += jnp.sum(dy * hh, axis=0, keepdims=True)
            dhh = dy * gf
            dh2_ref[...] = rstd * (dhh - hh * jnp.mean(dhh * hh, axis=-1, keepdims=True))

    row = pl.BlockSpec((tm, D), lambda i, j: (i, 0))
    vec = pl.BlockSpec((1, D), lambda i, j: (0, 0))
    return pl.pallas_call(
        body, name="mlp_fwd", grid=(s // tm, nj),
        in_specs=[row, vec, pl.BlockSpec((None, D, FF_COLS), lambda i, j: (j, 0, 0)),
                  pl.BlockSpec((FF_COLS, D), lambda i, j: (j, 0)), vec, row],
        out_specs=[pl.BlockSpec((tm, FF_COLS), lambda i, j: (i, j)), pl.BlockSpec((FF_COLS, tm), lambda i, j: (j, i)),
                   pl.BlockSpec((D, tm), lambda i, j: (0, i)), row, pl.BlockSpec((8, 128), lambda i, j: (0, 0)), vec],
        out_shape=[SDS((s, nj * FF_COLS), BF16), SDS((nj * FF_COLS, s), BF16), SDS((D, s), BF16), SDS((s, D), F32),
                   SDS((8, 128), F32), SDS((1, D), F32)],
        scratch_shapes=[pltpu.VMEM((tm, D), BF16), pltpu.VMEM((tm, D), F32)],
        compiler_params=_cparams(("arbitrary", "arbitrary"), 52),
    )(h1, _small_in_hbm(g_mlp), w_up_g, w_down, _small_in_hbm(g_fin), tgt)


def _mlp_bwd(dh2, r, w_down, w_up_g, h1, g_mlp, comm=None):
    s = h1.shape[0]
    tm = min(TM_ROWS, s)
    nj = N_SLOT

    def body(dh2_ref, r_ref, wd_ref, wu_ref, h1_ref, gm_ref, df_ref, dh1_ref, dgm_ref, dh2b_s, acc_s):
        i, j = pl.program_id(0), pl.program_id(1)

        @pl.when(j == 0)
        def _():
            dh2b_s[...] = dh2_ref[...].astype(BF16)
            acc_s[...] = jnp.zeros_like(acc_s)

        @pl.when((i == 0) & (j == 0))
        def _():
            dgm_ref[...] = jnp.zeros_like(dgm_ref)

        d_act = _dot_nt(dh2b_s[...], wd_ref[...])
        df = (d_act * (2.0 * r_ref[...].astype(F32))).astype(BF16)
        df_ref[...] = df
        acc_s[...] += _dot_nt(df, wu_ref[...])

        @pl.when(j == nj - 1)
        def _():
            hv = h1_ref[...]
            rstd = lax.rsqrt(jnp.mean(hv * hv, axis=-1, keepdims=True) + NORM_EPS)
            hh = hv * rstd
            dn2 = acc_s[...]
            dgm_ref[...] += jnp.sum(dn2 * hh, axis=0, keepdims=True)
            dhat = dn2 * gm_ref[...]
            dh1_ref[...] = dh2_ref[...] + rstd * (dhat - hh * jnp.mean(dhat * hh, axis=-1, keepdims=True))

    row = pl.BlockSpec((tm, D), lambda i, j: (i, 0))
    vec = pl.BlockSpec((1, D), lambda i, j: (0, 0))
    ffb = pl.BlockSpec((tm, FF_COLS), lambda i, j: (i, j))
    return _call(
        body, name="mlp_bwd", grid=(s // tm, nj),
        in_specs=[row, ffb, pl.BlockSpec((FF_COLS, D), lambda i, j: (j, 0)),
                  pl.BlockSpec((None, D, FF_COLS), lambda i, j: (j, 0, 0)), row, vec],
        out_specs=[ffb, row, vec],
        out_shape=[SDS((s, nj * FF_COLS), BF16), SDS((s, D), F32), SDS((1, D), F32)],
        scratch_shapes=[pltpu.VMEM((tm, D), BF16), pltpu.VMEM((tm, D), F32)],
        params=_cparams(("arbitrary", "arbitrary"), 52), args=(dh2, r, w_down, w_up_g, h1, g_mlp), comm=comm)


def _merge_bwd(dh1, z, pa, pb, w_out, w_oa, w_ob, comm=None):
    s = dh1.shape[0]
    tm = min(TM_MERGE, s)

    def body(dh1_ref, ma_ref, mb_ref, pa_ref, pb_ref, wo_ref, woa_ref, wob_ref,
             dz_ref, dpa_ref, dpb_ref, dya_ref, dyb_ref):
        dm = _dot_nt(dh1_ref[...].astype(BF16), wo_ref[...])
        sa = jax.nn.sigmoid(ma_ref[...].astype(F32))
        sb = jax.nn.sigmoid(mb_ref[...].astype(F32))
        dpa = (dm * sa).astype(BF16)
        dpb = (dm * sb).astype(BF16)
        dz_ref[:, 0:D] = (dm * pa_ref[...].astype(F32) * sa * (1.0 - sa)).astype(BF16)
        dz_ref[:, D:2 * D] = (dm * pb_ref[...].astype(F32) * sb * (1.0 - sb)).astype(BF16)
        dpa_ref[...] = dpa
        dpb_ref[...] = dpb
        dya_ref[...] = _dot_nt(dpa, woa_ref[...]).astype(BF16)
        dyb_ref[...] = _dot_nt(dpb, wob_ref[...]).astype(BF16)

    row = pl.BlockSpec((tm, D), lambda i: (i, 0))
    wsp = pl.BlockSpec((D, D), lambda i: (0, 0))
    return _call(
        body, name="merge_bwd", grid=(s // tm,),
        in_specs=[row, pl.BlockSpec((tm, D), lambda i: (i, 4)), pl.BlockSpec((tm, D), lambda i: (i, 5)),
                  row, row, wsp, wsp, wsp],
        out_specs=[pl.BlockSpec((tm, 2 * D), lambda i: (i, 2)), row, row, row, row],
        out_shape=[SDS((s, 6 * D), BF16)] + [SDS((s, D), BF16)] * 4, scratch_shapes=[],
        params=_cparams(("arbitrary",), 48), args=(dh1, z, z, pa, pb, w_out, w_oa, w_ob), comm=comm)


def _branch_b_bwd(dz, dyb, z, ln_g, ln_b, w_s, b_s_t, comm=None):
    s = z.shape[0]
    tb = min(T_BRANCH_B, s)

    def body(dz_in, dyb_ref, ub_ref, vb_ref, lg_ref, lb_ref, ws_ref, bs_ref,
             dz_ref, dws_ref, dbs_ref, dln_ref, du_s, dvln_s):
        del dz_in

        @pl.when(pl.program_id(0) == 0)
        def _():
            dws_ref[...] = jnp.zeros_like(dws_ref)
            dbs_ref[...] = jnp.zeros_like(dbs_ref)
            dln_ref[...] = jnp.zeros_like(dln_ref)

        lg = lg_ref[...]
        u, du, dv, rstd, vhat, vln = _sgu_common(ub_ref[...].astype(F32), vb_ref[...].astype(F32),
                                                 lg, lb_ref[...], True)
        vlnb = vln.astype(BF16)
        dyb_v = dyb_ref[...].astype(F32)
        wm = _masked_ws(ws_ref)
        keep = (lax.broadcasted_iota(jnp.int32, (CHUNK, CHUNK), 1)
                <= lax.broadcasted_iota(jnp.int32, (CHUNK, CHUNK), 0))
        bs = bs_ref[...]
        for c in range(tb // CHUNK):
            rs = slice(c * CHUNK, (c + 1) * CHUNK)
            for g in range(GROUPS):
                cs = slice(g * GROUP_DIM, (g + 1) * GROUP_DIM)
                v_blk = vlnb[rs, cs]
                sp = _dot(wm[g], v_blk) + bs[:, g:g + 1]
                d_sp = dyb_v[rs, cs] * u[rs, cs]
                d_spb = d_sp.astype(BF16)
                du_s[rs, cs] = dyb_v[rs, cs] * sp
                dvln_s[rs, cs] = _dot_tn(wm[g], d_spb)
                dws_ref[g] += jnp.where(keep, _dot_nt(d_spb, v_blk), 0.0)
                dbs_ref[g] += jnp.broadcast_to(jnp.sum(d_sp, axis=-1, keepdims=True), (CHUNK, CHUNK))
        dvln = dvln_s[...]
        dln_ref[0:1, :] += jnp.sum(dvln * vhat, axis=0, keepdims=True)
        dln_ref[1:2, :] += jnp.sum(dvln, axis=0, keepdims=True)
        dvh = dvln * lg
        d_v = rstd * (dvh - jnp.mean(dvh, axis=-1, keepdims=True)
                      - vhat * jnp.mean(dvh * vhat, axis=-1, keepdims=True))
        dz_ref[:, 0:D] = (du_s[...] * du).astype(BF16)
        dz_ref[:, D:2 * D] = (d_v * dv).astype(BF16)

    vec = pl.BlockSpec((1, D), lambda i: (0, 0))
    sq = pl.BlockSpec((GROUPS, CHUNK, CHUNK), lambda i: (0, 0, 0))
    return _call(
        body, name="branch_b_bwd", grid=(s // tb,),
        in_specs=[ANY, pl.BlockSpec((tb, D), lambda i: (i, 0)),
                  pl.BlockSpec((tb, D), lambda i: (i, 2)), pl.BlockSpec((tb, D), lambda i: (i, 3)), vec, vec, sq,
                  pl.BlockSpec((CHUNK, GROUPS), lambda i: (0, 0))],
        out_specs=[pl.BlockSpec((tb, 2 * D), lambda i: (i, 1)), sq, sq, pl.BlockSpec((8, D), lambda i: (0, 0))],
        out_shape=[SDS(dz.shape, BF16), SDS((GROUPS, CHUNK, CHUNK), F32), SDS((GROUPS, CHUNK, CHUNK), F32),
                   SDS((8, D), F32)],
        scratch_shapes=[pltpu.VMEM((tb, D), F32), pltpu.VMEM((tb, D), F32)], aliases={0: 0},
        params=_cparams(("arbitrary",), 40), args=(dz, dyb, z, z, ln_g, ln_b, w_s, b_s_t), comm=comm)


def _branch_a_bwd(dz, dya, z, hs, xc, r, ig, conv_w, w_r, w_i, lam, comm=None):
    s = z.shape[0]
    ta = min(T_BRANCH_A, s)
    nb = s // ta
    per16 = ta // 16

    def body(dz_in, dya_ref, xa_ref, ga_ref, hs_ref, hp_ref, xc_ref, r_ref, i_ref, cw_ref, wr_ref, wi_ref,
             lam_ref, dz_ref, vec_ref, dwr_ref, dwi_ref, a_s, b_s, h_s, dcar_s, acar_s, dxc_s):
        del dz_in
        i = pl.program_id(0)
        blk = nb - 1 - i

        @pl.when(i == 0)
        def _():
            dcar_s[...] = jnp.zeros_like(dcar_s)
            acar_s[...] = jnp.zeros_like(acar_s)
            dxc_s[...] = jnp.zeros_like(dxc_s)
            vec_ref[...] = jnp.zeros_like(vec_ref)
            dwr_ref[...] = jnp.zeros_like(dwr_ref)
            dwi_ref[...] = jnp.zeros_like(dwi_ref)

        cw = cw_ref[...]
        lam_v = lam_ref[...]
        xa = xa_ref[...].astype(F32)
        xcb = xc_ref[...]
        xc = xcb.astype(F32)
        sp_lam = _softplus(-lam_v)
        r_v, i_v = r_ref[...].astype(F32), i_ref[...].astype(F32)
        a_v, m_v = _decay(r_v, sp_lam)

        hs_v = hs_ref[...].astype(F32)
        hprev8 = jnp.where(blk > 0, hp_ref[...].astype(F32)[8:16], 0.0)
        h_m1 = _rows_shifted(hprev8, hs_v, 1)
        gg, dgg = _gelu_and_grad(ga_ref[...].astype(F32))
        dya_v = dya_ref[...].astype(F32)
        dz_ref[:, D:2 * D] = (dya_v * hs_v * dgg).astype(BF16)

        a_s[...] = _rows_advanced(a_v, acar_s[...], 1)
        b_s[...] = dya_v * gg

        row = lax.broadcasted_iota(jnp.int32, (8, D), 0)
        ng = ta // 8

        def group(gi, carry):
            off = pl.multiple_of((ng - 1 - gi) * 8, 8)
            c8 = a_s[pl.ds(off, 8), :]
            d8 = b_s[pl.ds(off, 8), :]
            for d in (1, 2, 4):
                c_sh = jnp.where(row < 8 - d, pltpu.roll(c8, 8 - d, 0), 1.0)
                d_sh = jnp.where(row < 8 - d, pltpu.roll(d8, 8 - d, 0), 0.0)
                d8 = c8 * d_sh + d8
                c8 = c8 * c_sh
            dh8 = d8 + c8 * carry
            h_s[pl.ds(off, 8), :] = dh8
            return jnp.broadcast_to(dh8[0:1, :], (8, D))

        dcar_s[...] = lax.fori_loop(0, ng, group, dcar_s[...])
        acar_s[...] = jnp.broadcast_to(a_v[0:1, :], (8, D))

        dbx = h_s[...]
        d_mult = dbx * xc * i_v
        d_loga = dbx * h_m1 * a_v - d_mult * (a_v * a_v) / m_v
        d_pr = d_loga * ((-LRU_C) * sp_lam) * r_v * (1.0 - r_v)
        d_pi = dbx * xc * m_v * i_v * (1.0 - i_v)
        vec_ref[7:8, :] += jnp.sum(d_loga * r_v, axis=0, keepdims=True) * (LRU_C * jax.nn.sigmoid(-lam_v))
        vec_ref[5:6, :] += jnp.sum(d_pr, axis=0, keepdims=True)
        vec_ref[6:7, :] += jnp.sum(d_pi, axis=0, keepdims=True)
        d_prb = d_pr.astype(BF16)
        d_pib = d_pi.astype(BF16)
        h_s[...] = dbx * i_v * m_v
        for h in range(HEADS):
            sl = slice(h * HEAD_DIM, (h + 1) * HEAD_DIM)
            h_s[:, sl] += _dot_nt(d_prb[:, sl], wr_ref[h]) + _dot_nt(d_pib[:, sl], wi_ref[h])
            dwr_ref[h] += _dot_tn(xcb[:, sl], d_prb[:, sl])
            dwi_ref[h] += _dot_tn(xcb[:, sl], d_pib[:, sl])
        d_xc = h_s[...]
        vec_ref[4:5, :] += jnp.sum(d_xc, axis=0, keepdims=True)
        vec_ref[0:1, :] += jnp.sum(d_xc * xa, axis=0, keepdims=True)
        d_xa = cw[0:1, :] * d_xc
        nxt = dxc_s[...]
        for k in range(1, CONV_K):
            ahead = _rows_advanced(d_xc, nxt, k)
            vec_ref[k:k + 1, :] += jnp.sum(ahead * xa, axis=0, keepdims=True)
            d_xa = d_xa + cw[k:k + 1, :] * ahead
        dz_ref[:, 0:D] = d_xa.astype(BF16)
        dxc_s[...] = d_xc[0:8, :]

    vec = pl.BlockSpec((1, D), lambda i: (0, 0))
    gate = pl.BlockSpec((HEADS, HEAD_DIM, HEAD_DIM), lambda i: (0, 0, 0))
    cur = lambda c: pl.BlockSpec((ta, D), lambda i: (nb - 1 - i, c))
    before = lambda c: pl.BlockSpec((16, D), lambda i: (jnp.maximum((nb - 1 - i) * per16 - 1, 0), c))
    return _call(
        body, name="branch_a_bwd", grid=(nb,),
        in_specs=[ANY, cur(0), cur(0), cur(1), cur(0), before(0), cur(0), cur(0), cur(0),
                  pl.BlockSpec((CONV_K, D), lambda i: (0, 0)), gate, gate, vec],
        out_specs=[pl.BlockSpec((ta, 2 * D), lambda i: (nb - 1 - i, 0)), pl.BlockSpec((8, D), lambda i: (0, 0)),
                   gate, gate],
        out_shape=[SDS(dz.shape, BF16), SDS((8, D), F32), SDS((HEADS, HEAD_DIM, HEAD_DIM), F32),
                   SDS((HEADS, HEAD_DIM, HEAD_DIM), F32)],
        scratch_shapes=[pltpu.VMEM((ta, D), F32)] * 3 + [pltpu.VMEM((8, D), F32)] * 3, aliases={0: 0},
        params=_cparams(("arbitrary",), 48),
        args=(dz, dya, z, z, hs, hs, xc, r, ig, conv_w, w_r, w_i, lam), comm=comm)


def _in_bwd(dz, w_in_g, x, dh1, g_mix, comm=None):
    s = x.shape[0]
    tm = min(TM_ROWS, s)
    nj = N_SLOT

    def body(dz_ref, w_ref, x_ref, dh1_ref, g_ref, dx_ref, dg_ref, acc_s):
        i, j = pl.program_id(0), pl.program_id(1)

        @pl.when(j == 0)
        def _():
            acc_s[...] = jnp.zeros_like(acc_s)

        @pl.when((i == 0) & (j == 0))
        def _():
            dg_ref[...] = jnp.zeros_like(dg_ref)

        acc_s[...] += _dot_nt(dz_ref[...], w_ref[...])

        @pl.when(j == nj - 1)
        def _():
            xv = x_ref[...]
            rstd = lax.rsqrt(jnp.mean(xv * xv, axis=-1, keepdims=True) + NORM_EPS)
            xh = xv * rstd
            dn = acc_s[...]
            dg_ref[...] += jnp.sum(dn * xh, axis=0, keepdims=True)
            dhat = dn * g_ref[...]
            dx_ref[...] = dh1_ref[...] + rstd * (dhat - xh * jnp.mean(dhat * xh, axis=-1, keepdims=True))

    row = pl.BlockSpec((tm, D), lambda i, j: (i, 0))
    vec = pl.BlockSpec((1, D), lambda i, j: (0, 0))
    return _call(
        body, name="in_bwd", grid=(s // tm, nj),
        in_specs=[pl.BlockSpec((tm, W_IN_COLS), lambda i, j: (i, j)),
                  pl.BlockSpec((None, D, W_IN_COLS), lambda i, j: (j, 0, 0)), row, row, vec],
        out_specs=[row, vec],
        out_shape=[SDS((s, D), F32), SDS((1, D), F32)],
        scratch_shapes=[pltpu.VMEM((tm, D), F32)],
        params=_cparams(("arbitrary", "arbitrary"), 48), args=(dz, w_in_g, x, dh1, g_mix), comm=comm)


def _wgrad(name, a, b, nblk, a_split, b_split):
    s = a.shape[0]
    ts = min(TM_ROWS, s)
    a_w = a.shape[1] // nblk if a_split else a.shape[1]
    b_w = b.shape[1] // nblk if b_split else b.shape[1]

    def body(a_ref, b_ref, o_ref, acc_s):
        t = pl.program_id(1)

        @pl.when(t == 0)
        def _():
            acc_s[...] = jnp.zeros_like(acc_s)

        acc_s[...] += _dot_tn(a_ref[...].astype(BF16), b_ref[...].astype(BF16))

        @pl.when(t == pl.num_programs(1) - 1)
        def _():
            o_ref[...] = acc_s[...].astype(BF16)

    return pl.pallas_call(
        body, name=name, grid=(nblk, s // ts),
        in_specs=[pl.BlockSpec((ts, a_w), (lambda k, t: (t, k)) if a_split else (lambda k, t: (t, 0))),
                  pl.BlockSpec((ts, b_w), (lambda k, t: (t, k)) if b_split else (lambda k, t: (t, 0)))],
        out_specs=pl.BlockSpec((None, a_w, b_w), lambda k, t: (k, 0, 0)),
        out_shape=SDS((nblk, a_w, b_w), BF16),
        scratch_shapes=[pltpu.VMEM((a_w, b_w), F32)],
        compiler_params=_cparams(("arbitrary", "arbitrary"), 48),
    )(a, b)


def _wgrad_t(name, a_t, b, nblk, a_split, b_split, tokens):
    s = b.shape[0]
    ts = min(tokens, s)
    a_w = a_t.shape[0] // nblk if a_split else a_t.shape[0]
    b_w = b.shape[1] // nblk if b_split else b.shape[1]

    def body(a_ref, b_ref, o_ref, acc_s):
        t = pl.program_id(1)

        @pl.when(t == 0)
        def _():
            acc_s[...] = jnp.zeros_like(acc_s)

        acc_s[...] += _dot(a_ref[...], b_ref[...].astype(BF16))

        @pl.when(t == pl.num_programs(1) - 1)
        def _():
            o_ref[...] = acc_s[...].astype(BF16)

    return pl.pallas_call(
        body, name=name, grid=(nblk, s // ts),
        in_specs=[pl.BlockSpec((a_w, ts), (lambda k, t: (k, t)) if a_split else (lambda k, t: (0, t))),
                  pl.BlockSpec((ts, b_w), (lambda k, t: (t, k)) if b_split else (lambda k, t: (t, 0)))],
        out_specs=pl.BlockSpec((None, a_w, b_w), lambda k, t: (k, 0, 0)),
        out_shape=SDS((nblk, a_w, b_w), BF16),
        scratch_shapes=[pltpu.VMEM((a_w, b_w), F32)],
        compiler_params=_cparams(("arbitrary", "arbitrary"), 48),
    )(a_t, b)


def _place():
    x, y, c = lax.axis_index("x"), lax.axis_index("y"), lax.axis_index("c")
    return x, y, c


def _other_chips(x, y):
    return [(x, 1 - y, 2 * x + 1 - y), (1 - x, y, 2 * (1 - x) + y), (1 - x, 1 - y, 2 * (1 - x) + 1 - y)]


class _Plan:
    def __init__(self, arrays, out_shape, sems, start, finish, middle=None, middle_at=6):
        self.arrays, self.out_shape, self.sems, self.start, self.finish = arrays, out_shape, sems, start, finish
        self.middle, self.middle_at = middle, middle_at


def _gather_plan(shards, middle_at=6):
    n = len(shards)

    def copies(ins, outs, sems):
        send_sems, recv_sems, local_sems = sems
        x, y, c = _place()
        chip = 2 * x + y
        me = 2 * chip + c
        sib = (x, y, 1 - c)
        chips = _other_chips(x, y)

        def rc(k, t, src, blk, to):
            return pltpu.make_async_remote_copy(
                src_ref=src, dst_ref=outs[t].at[blk], send_sem=send_sems.at[k * n + t],
                recv_sem=recv_sems.at[k * n + t], device_id=to, device_id_type=MESH)

        (yx, yy, y_chip), (xx, xy, x_chip), _ = chips
        local = [pltpu.make_async_copy(ins[t], outs[t].at[me], local_sems.at[t]) for t in range(n)]
        sends = ([rc(0, t, ins[t], me, sib) for t in range(n)] + [rc(1, t, ins[t], me, (yx, yy, c)) for t in range(n)]
                 + [rc(2, t, ins[t], me, (xx, xy, c)) for t in range(n)])
        passed = [[rc(4 + j, t, outs[t].at[2 * pc + c], 2 * pc + c, sib) for t in range(n)]
                  for j, (_, _, pc) in enumerate(chips)]
        relays = [[rc(3, t, outs[t].at[2 * y_chip + c], 2 * y_chip + c, (xx, xy, c)) for t in range(n)],
                  [rc(3, t, outs[t].at[2 * x_chip + c], 2 * x_chip + c, (yx, yy, c)) for t in range(n)]]
        return rc, local, sends, passed, relays, chips, chip, c, sib

    def start(ins, outs, sems):
        _, local, sends, _, _, _, _, _, _ = copies(ins, outs, sems)
        for cp in local + sends:
            cp.start()

    def middle(ins, outs, sems):
        rc, _, _, passed, relays, chips, _, c, sib = copies(ins, outs, sems)
        for j in range(2):
            for t in range(n):
                rc(1 + j, t, ins[t], 2 * chips[j][2] + c, sib).wait_recv()
        for j in range(2):
            for cp in passed[j]:
                cp.start()

            @pl.when(c == j)
            def _():
                for cp in relays[j]:
                    cp.start()

    def finish(ins, outs, sems):
        rc, local, sends, passed, relays, chips, chip, c, sib = copies(ins, outs, sems)
        far = 2 * chips[2][2] + c
        for t in range(n):
            rc(3, t, ins[t], far, sib).wait_recv()
        for cp in passed[2]:
            cp.start()
        for t in range(n):
            rc(0, t, ins[t], 2 * chip + 1 - c, sib).wait_recv()
        for j, (px, py, pc) in enumerate(chips):
            for t in range(n):
                rc(4 + j, t, ins[t], 2 * pc + 1 - c, sib).wait_recv()
        for cp in sends + passed[0] + passed[1] + passed[2]:
            cp.wait_send()
        for j in range(2):
            @pl.when(c == j)
            def _():
                for cp in relays[j]:
                    cp.wait_send()
        for cp in local:
            cp.wait()

    return _Plan(list(shards), [SDS((N_SLOT,) + tuple(a.shape), a.dtype) for a in shards],
                 [pltpu.SemaphoreType.DMA((7 * n,)), pltpu.SemaphoreType.DMA((7 * n,)),
                  pltpu.SemaphoreType.DMA((n,))], start, finish, middle, middle_at)


def _sibling_plan(grads, whole=()):
    n, m = len(grads), len(whole)

    def copies(ins, outs, sems):
        send_sems, recv_sems = sems
        x, y, c = _place()
        sib = (x, y, 1 - c)

        def rc(t, src, dst):
            return pltpu.make_async_remote_copy(src_ref=src, dst_ref=dst, send_sem=send_sems.at[t],
                                                recv_sem=recv_sems.at[t], device_id=sib, device_id_type=MESH)
        return rc, c

    def start(ins, outs, sems):
        rc, c = copies(ins, outs, sems)
        for t in range(n):
            for j in range(4):
                rc(t, ins[t].at[2 * j + 1 - c], outs[t].at[j]).start()
        for t in range(n, n + m):
            rc(t, ins[t], outs[t]).start()

    def finish(ins, outs, sems):
        rc, _ = copies(ins, outs, sems)
        for t in range(n):
            rc(t, ins[t].at[pl.ds(0, 4)], outs[t]).wait()
        for t in range(n, n + m):
            rc(t, ins[t], outs[t]).wait()

    return _Plan(list(grads) + list(whole),
                 [SDS((4,) + tuple(g.shape[1:]), g.dtype) for g in grads] + [SDS(a.shape, a.dtype) for a in whole],
                 [pltpu.SemaphoreType.DMA((n + m,)), pltpu.SemaphoreType.DMA((n + m,))], start, finish)


def _chips_plan(parts, whole=()):
    n, m = len(parts), len(whole)

    def src_of(ins, t, pc):
        return ins[t].at[pc] if t < n else ins[t]

    def local_copies(ins, outs, sems, chip):
        return [pltpu.make_async_copy(src_of(ins, t, chip), outs[t].at[chip], sems[2].at[t]) for t in range(n + m)]

    def start(ins, outs, sems):
        send_sems, recv_sems, _ = sems
        x, y, c = _place()
        chip = 2 * x + y
        for cp in local_copies(ins, outs, sems, chip):
            cp.start()
        for px, py, pc in _other_chips(x, y):
            for t in range(n + m):
                pltpu.make_async_remote_copy(src_ref=src_of(ins, t, pc), dst_ref=outs[t].at[chip],
                                             send_sem=send_sems.at[t], recv_sem=recv_sems.at[t],
                                             device_id=(px, py, c), device_id_type=MESH).start()

    def finish(ins, outs, sems):
        send_sems, recv_sems, _ = sems
        x, y, c = _place()
        for t in range(n + m):
            three = outs[t].at[pl.ds(0, 3)]
            pltpu.make_async_remote_copy(src_ref=three, dst_ref=three, send_sem=send_sems.at[t],
                                         recv_sem=recv_sems.at[t], device_id=(x, y, c), device_id_type=MESH).wait()
        for cp in local_copies(ins, outs, sems, 2 * x + y):
            cp.wait()

    return _Plan(list(parts) + list(whole),
                 [SDS(p.shape, p.dtype) for p in parts] + [SDS((4,) + tuple(a.shape), a.dtype) for a in whole],
                 [pltpu.SemaphoreType.DMA((n + m,)), pltpu.SemaphoreType.DMA((n + m,)),
                  pltpu.SemaphoreType.DMA((n + m,))], start, finish)


def _exchange_plan(arr):
    def peers(x, y, c):
        flip = lambda v, f: 1 - v if f else v
        return [(flip(x, fx), flip(y, fy), flip(c, fc))
                for fx in (0, 1) for fy in (0, 1) for fc in (0, 1) if fx or fy or fc]

    def start(ins, outs, sems):
        x, y, c = _place()
        me = 4 * x + 2 * y + c
        pltpu.make_async_copy(ins[0], outs[0].at[me], sems[2].at[0]).start()
        for to in peers(x, y, c):
            pltpu.make_async_remote_copy(src_ref=ins[0], dst_ref=outs[0].at[me], send_sem=sems[0].at[0],
                                         recv_sem=sems[1].at[0], device_id=to, device_id_type=MESH).start()

    def finish(ins, outs, sems):
        x, y, c = _place()
        seven = outs[0].at[pl.ds(0, 7)]
        pltpu.make_async_remote_copy(src_ref=seven, dst_ref=seven, send_sem=sems[0].at[0], recv_sem=sems[1].at[0],
                                     device_id=(x, y, c), device_id_type=MESH).wait()
        pltpu.make_async_copy(ins[0], outs[0].at[4 * x + 2 * y + c], sems[2].at[0]).wait()

    return _Plan([arr], [SDS((N_SLOT,) + tuple(arr.shape), arr.dtype)],
                 [pltpu.SemaphoreType.DMA((1,)), pltpu.SemaphoreType.DMA((1,)), pltpu.SemaphoreType.DMA((1,))],
                 start, finish)


def _join(*plans):
    def cut(seq, sizes):
        out, at = [], 0
        for k in sizes:
            out.append(seq[at:at + k])
            at += k
        return out

    n_arr = [len(p.arrays) for p in plans]
    n_sem = [len(p.sems) for p in plans]

    def start(ins, outs, sems):
        for p, i, o, s in zip(plans, cut(ins, n_arr), cut(outs, n_arr), cut(sems, n_sem)):
            p.start(i, o, s)

    def finish(ins, outs, sems):
        for p, i, o, s in zip(plans, cut(ins, n_arr), cut(outs, n_arr), cut(sems, n_sem)):
            p.finish(i, o, s)

    def middle(ins, outs, sems):
        for p, i, o, s in zip(plans, cut(ins, n_arr), cut(outs, n_arr), cut(sems, n_sem)):
            if p.middle is not None:
                p.middle(i, o, s)

    return _Plan([a for p in plans for a in p.arrays], [o for p in plans for o in p.out_shape],
                 [s for p in plans for s in p.sems], start, finish,
                 middle if any(p.middle is not None for p in plans) else None,
                 max(p.middle_at for p in plans))


def _run_plan(name, plan):
    k = len(plan.arrays)

    def body(*refs):
        ins, outs, sems = refs[:k], refs[k:2 * k], refs[2 * k:]
        plan.start(ins, outs, sems)
        if plan.middle is not None:
            plan.middle(ins, outs, sems)
        plan.finish(ins, outs, sems)

    return pl.pallas_call(
        body, name=name, in_specs=[ANY] * k, out_specs=[ANY] * k, out_shape=plan.out_shape,
        scratch_shapes=plan.sems, compiler_params=pltpu.CompilerParams(has_side_effects=True),
    )(*plan.arrays)


def _call(body, *, name, grid, in_specs, out_specs, out_shape, scratch_shapes, params, args, comm=None,
          aliases=None, prefetch=()):
    aliases = aliases or {}
    n_pre = len(prefetch)

    def launch(fn, ins_specs, outs_specs, outs_shape, scratch, operands):
        spec = pltpu.PrefetchScalarGridSpec(num_scalar_prefetch=n_pre, grid=grid, in_specs=ins_specs,
                                            out_specs=outs_specs, scratch_shapes=scratch)
        return pl.pallas_call(fn, name=name, grid_spec=spec, out_shape=outs_shape, compiler_params=params,
                              input_output_aliases=aliases)(*prefetch, *[_small_in_hbm(a) for a in operands])

    if comm is None:
        return list(launch(body, in_specs, out_specs, out_shape, scratch_shapes, args)), []
    n_in, n_out, n_scr, k = len(in_specs), len(out_specs), len(scratch_shapes), len(comm.arrays)

    def wrapped(*refs):
        pre, refs = refs[:n_pre], refs[n_pre:]
        ins = refs[:n_in]
        c_in = refs[n_in:n_in + k]
        outs = refs[n_in + k:n_in + k + n_out]
        c_out = refs[n_in + k + n_out:n_in + 2 * k + n_out]
        scr = refs[n_in + 2 * k + n_out:n_in + 2 * k + n_out + n_scr]
        sems = refs[n_in + 2 * k + n_out + n_scr:]
        step, steps = pl.program_id(0), grid[0]
        for d in range(1, len(grid)):
            step, steps = step * grid[d] + pl.program_id(d), steps * grid[d]

        @pl.when(step == 0)
        def _():
            comm.start(c_in, c_out, sems)

        if comm.middle is not None:
            @pl.when(step == (comm.middle_at * steps) // 8)
            def _():
                comm.middle(c_in, c_out, sems)

        body(*pre, *ins, *outs, *scr)

        @pl.when(step == steps - 1)
        def _():
            comm.finish(c_in, c_out, sems)

    res = launch(wrapped, list(in_specs) + [ANY] * k, list(out_specs) + [ANY] * k,
                 list(out_shape) + list(comm.out_shape), list(scratch_shapes) + list(comm.sems),
                 tuple(args) + tuple(comm.arrays))
    return list(res[:n_out]), list(res[n_out:])


def _wgrad_in(n1_t, dz, core, comm=None):
    s = dz.shape[0]
    half = N_SLOT // 2

    def owner(k, c):
        return 2 * (k % half) + jnp.where(k < half, 1 - c, c)

    def body(c_ref, a_ref, b_ref, keep_ref, recv_ref, out_s, send_sems, recv_sems):
        del c_ref
        k = pl.program_id(0)
        x, y, c = _place()
        slot = k % 2

        def to_sibling(j):
            return pltpu.make_async_remote_copy(src_ref=out_s.at[j % 2], dst_ref=recv_ref.at[j],
                                                send_sem=send_sems.at[j], recv_sem=recv_sems.at[j],
                                                device_id=(x, y, 1 - c), device_id_type=MESH)

        @pl.when((k >= 2) & (k < half + 2))
        def _():
            to_sibling(k - 2).wait_send()

        @pl.when(k < half)
        def _():
            out_s[slot] = _dot(a_ref[...], b_ref[...]).astype(BF16)
            to_sibling(k).start()

        @pl.when(k >= half)
        def _():
            keep_ref[...] = _dot(a_ref[...], b_ref[...]).astype(BF16)

        @pl.when(k == N_SLOT - 1)
        def _():
            for j in range(half):
                to_sibling(j).wait_recv()

    blk = SDS((half, D, W_IN_COLS), BF16)
    return _call(body, name="wgrad_in", grid=(N_SLOT,),
                 in_specs=[pl.BlockSpec((D, s), lambda k, c_ref: (0, 0)),
                           pl.BlockSpec((s, W_IN_COLS), lambda k, c_ref: (0, owner(k, c_ref[0])))],
                 out_specs=[pl.BlockSpec((None, D, W_IN_COLS), lambda k, c_ref: (jnp.maximum(k - half, 0), 0, 0)), ANY],
                 out_shape=[blk, blk],
                 scratch_shapes=[pltpu.VMEM((2, D, W_IN_COLS), BF16), pltpu.SemaphoreType.DMA((half,)),
                                 pltpu.SemaphoreType.DMA((half,))],
                 params=_cparams(("arbitrary",), 48), args=(n1_t, dz), comm=comm, prefetch=(core,))


def _row_tile(rows):
    for t in (512, 256, 128, 64, 32, 16, 8):
        if rows % t == 0:
            return t
    return rows


def _pair_sum(name, g8, recv4, core):
    _, rows, cols = recv4.shape
    tr = _row_tile(rows)
    per = g8.shape[0] // 4
    g42 = g8.reshape(4, per, rows, cols)

    def body(c_ref, g_ref, r_ref, o_ref):
        del c_ref
        o_ref[...] = (g_ref[...].astype(F32) + r_ref[...].astype(F32)).astype(o_ref.dtype)

    return pl.pallas_call(
        body, name=name,
        grid_spec=pltpu.PrefetchScalarGridSpec(
            num_scalar_prefetch=1, grid=(4, rows // tr),
            in_specs=[pl.BlockSpec((None, None, tr, cols), lambda j, i, c_ref: (j, c_ref[0] * (per - 1), i, 0)),
                      pl.BlockSpec((None, tr, cols), lambda j, i, c_ref: (j, i, 0))],
            out_specs=pl.BlockSpec((None, tr, cols), lambda j, i, c_ref: (j, i, 0))),
        out_shape=SDS(recv4.shape, g8.dtype),
        compiler_params=_cparams(("arbitrary", "arbitrary"), 32),
    )(core, g42, recv4)


def _add2(name, a, b):
    rows, cols = a.shape
    tr = _row_tile(rows)

    def body(a_ref, b_ref, o_ref):
        o_ref[...] = a_ref[...] + b_ref[...]

    blk = pl.BlockSpec((tr, cols), lambda i: (i, 0))
    return pl.pallas_call(body, name=name, grid=(rows // tr,), in_specs=[blk, blk], out_specs=blk,
                          out_shape=SDS(a.shape, a.dtype),
                          compiler_params=_cparams(("arbitrary",), 32))(a, b)


def _sum_terms(name, terms):
    k, rows, cols = terms.shape
    tr = _row_tile(rows)

    def body(r_ref, o_ref):
        acc = r_ref[0]
        for q in range(1, k):
            acc = acc + r_ref[q]
        o_ref[...] = acc

    return pl.pallas_call(body, name=name, grid=(rows // tr,),
                          in_specs=[pl.BlockSpec((k, tr, cols), lambda i: (0, i, 0))],
                          out_specs=pl.BlockSpec((tr, cols), lambda i: (i, 0)),
                          out_shape=SDS((rows, cols), terms.dtype),
                          compiler_params=_cparams(("arbitrary",), 32))(terms)


def _adam_update(g, w, m, v):
    c1 = 1.0 / (1.0 - ADAM_B1 ** ADAM_STEP)
    c2 = 1.0 / (1.0 - ADAM_B2 ** ADAM_STEP)
    mn = ADAM_B1 * m + (1.0 - ADAM_B1) * g
    vn = ADAM_B2 * v + (1.0 - ADAM_B2) * (g * g)
    delta = (-ADAM_LR) * ((mn * c1) / (jnp.sqrt(vn * c2) + ADAM_EPS) + ADAM_WD * w)
    return delta, mn, vn


def _adamw_many(name, gs, ws, ms, vs):
    n = len(gs)

    def body(*refs):
        for p in range(n):
            g, w, m, v = (refs[q * n + p][...] for q in range(4))
            d, mn, vn = _adam_update(g, w, m, v)
            refs[4 * n + p][...] = d
            refs[5 * n + p][...] = mn
            refs[6 * n + p][...] = vn

    full = [pl.BlockSpec(w.shape, lambda i: (0, 0)) for w in ws]
    shapes = [SDS(w.shape, F32) for w in ws]
    res = pl.pallas_call(body, name=name, grid=(1,), in_specs=full * 4, out_specs=full * 3, out_shape=shapes * 3,
                         compiler_params=_cparams(("arbitrary",), 32),
                         )(*[_small_in_hbm(a) for a in (*gs, *ws, *ms, *vs)])
    return [(res[p], res[n + p], res[2 * n + p]) for p in range(n)]


def _adamw(name, terms, w, m, v):
    k, rows, cols = terms.shape
    tr = _row_tile(rows)

    def body(t_ref, w_ref, m_ref, v_ref, g_ref, d_ref, mo_ref, vo_ref):
        g = t_ref[0].astype(F32)
        for q in range(1, k):
            g = g + t_ref[q].astype(F32)
        g_ref[...] = g
        d_ref[...], mo_ref[...], vo_ref[...] = _adam_update(g, w_ref[...], m_ref[...], v_ref[...])

    blk = pl.BlockSpec((tr, cols), lambda i: (i, 0))
    return pl.pallas_call(body, name=name, grid=(rows // tr,),
                          in_specs=[pl.BlockSpec((k, tr, cols), lambda i: (0, i, 0)), blk, blk, blk],
                          out_specs=[blk] * 4, out_shape=[SDS((rows, cols), F32)] * 4,
                          compiler_params=_cparams(("arbitrary",), 40),
                          )(*[pltpu.with_memory_space_constraint(a, pltpu.HBM) for a in (terms, w, m, v)])


def kernel(x, norm_mix_g, w_in, conv_w, conv_b, w_rgate, b_rgate, w_igate, b_igate, lru_lambda, w_out_a, sgu_ln_g, sgu_ln_b, sgu_w_s, sgu_b_s, w_out_b, w_out, norm_mlp_g, w_up, w_down, norm_final_g, loss_target, m_norm_mix_g, m_w_in, m_conv_w, m_conv_b, m_w_rgate, m_b_rgate, m_w_igate, m_b_igate, m_lru_lambda, m_w_out_a, m_sgu_ln_g, m_sgu_ln_b, m_sgu_w_s, m_sgu_b_s, m_w_out_b, m_w_out, m_norm_mlp_g, m_w_up, m_w_down, m_norm_final_g, v_norm_mix_g, v_w_in, v_conv_w, v_conv_b, v_w_rgate, v_b_rgate, v_w_igate, v_b_igate, v_lru_lambda, v_w_out_a, v_sgu_ln_g, v_sgu_ln_b, v_sgu_w_s, v_sgu_b_s, v_w_out_b, v_w_out, v_norm_mlp_g, v_w_up, v_w_down, v_norm_final_g):
    cx, cy, cc = _place()
    me = 4 * cx + 2 * cy + cc
    core = jnp.reshape(cc, (1,)).astype(jnp.int32)
    xs = x[0]
    tgt = loss_target[0]

    gate_shard = jnp.stack([w_rgate[0], w_igate[0]]).astype(BF16).reshape(2 * HEADS * 32, HEAD_DIM)
    vec_shard = jnp.concatenate([conv_w[0], b_rgate[0], b_igate[0]], axis=1)
    vec_shard = jnp.pad(vec_shard, ((0, 4), (0, 256 - vec_shard.shape[1])))
    shards = [w_in[0].astype(BF16), w_out_a[0].astype(BF16), w_out_b[0].astype(BF16), w_out[0].astype(BF16),
              w_up[0].astype(BF16), w_down[0].astype(BF16), gate_shard, vec_shard]
    (z, n1_t, w_in_g), (gate_g, vec_g) = _in_proj(xs, norm_mix_g, shards[0], _slot_order(cx, cy, cc),
                                                comm=_gather_plan(shards[6:8]))
    gates = gate_g.reshape(N_SLOT, 2, HEADS, 32, HEAD_DIM).transpose(1, 2, 0, 3, 4).reshape(2, HEADS, HEAD_DIM, HEAD_DIM)
    w_r_f, w_i_f = gates[0], gates[1]
    conv_w_f = vec_g[:, 0:4, 0:128].transpose(1, 0, 2).reshape(CONV_K, D)
    b_r_f = vec_g[:, 0:4, 128:160].transpose(1, 0, 2).reshape(1, D)
    b_i_f = vec_g[:, 0:4, 160:192].transpose(1, 0, 2).reshape(1, D)
    b_s_t = jnp.transpose(sgu_b_s[0])

    (ya, hs, xc, r_gate, i_gate), (w_oa_g, w_ob_g, w_out_g, w_up_g) = _branch_a_fwd(
        z, conv_w_f, conv_b, w_r_f, b_r_f, w_i_f, b_i_f, lru_lambda, comm=_gather_plan(shards[1:5]))
    w_oa_f = w_oa_g.reshape(D, D)
    w_ob_f = w_ob_g.reshape(D, D)
    w_out_f = w_out_g.reshape(D, D)
    (yb,), _ = _branch_b_fwd(z, sgu_ln_g, sgu_ln_b, sgu_w_s[0], b_s_t)
    (pa, pb, merged, h1), (w_down_g,) = _merge_out(ya, yb, z, xs, w_oa_f, w_ob_f, w_out_f,
                                                   comm=_gather_plan(shards[5:6], middle_at=4))
    w_down_f = w_down_g.reshape(N_SLOT * FF_COLS, D)
    r_act, act_t, n2_t, dh2, loss_acc, d_gfin = _mlp_fwd(h1, norm_mlp_g, w_up_g, w_down_f,
                                                         norm_final_g.reshape(1, D), tgt)

    def pair(names, grads, recv):
        return [_pair_sum("pair_sum_" + nm, g, r, core) for nm, g, r in zip(names, grads, recv)]

    g_down = _wgrad_t("wgrad_down", act_t, dh2, N_SLOT // 2, True, False, 2048)
    g_down = g_down.reshape(N_SLOT, FF_COLS, D)
    (df, dh1, d_gmlp), (r_down,) = _mlp_bwd(dh2, r_act, w_down_f, w_up_g, h1, norm_mlp_g,
                                            comm=_sibling_plan([g_down]))
    (p_down,) = pair(["down"], [g_down], [r_down])
    g_up = _wgrad_t("wgrad_up", n2_t, df, N_SLOT, False, True, 4096)
    g_out = _wgrad("wgrad_out", merged, dh1, 1, False, False).reshape(N_SLOT, D // N_SLOT, D)
    (dz, dpa, dpb, dya, dyb), (got_down, r_up, r_out) = _merge_bwd(
        dh1, z, pa, pb, w_out_f, w_oa_f, w_ob_f, comm=_join(_chips_plan([p_down]), _sibling_plan([g_up, g_out])))
    p_up, p_out = pair(["up", "out"], [g_up, g_out], [r_up, r_out])
    g_oa = _wgrad("wgrad_out_a", ya, dpa, 1, False, False).reshape(N_SLOT, D // N_SLOT, D)
    g_ob = _wgrad("wgrad_out_b", yb, dpb, 1, False, False).reshape(N_SLOT, D // N_SLOT, D)
    (dz, d_ws, d_bs, d_ln), (got_up, r_oa, r_ob) = _branch_b_bwd(
        dz, dyb, z, sgu_ln_g, sgu_ln_b, sgu_w_s[0], b_s_t,
        comm=_join(_chips_plan([p_up]), _sibling_plan([g_oa, g_ob])))
    p_oa, p_ob = pair(["out_a", "out_b"], [g_oa, g_ob], [r_oa, r_ob])
    (dz, d_vec, d_wr, d_wi), (got_out, got_oa, got_ob) = _branch_a_bwd(
        dz, dya, z, hs, xc, r_gate, i_gate, conv_w_f, w_r_f, w_i_f, lru_lambda,
        comm=_chips_plan([p_out, p_oa, p_ob]))
    g_gate = jnp.stack([d_wr, d_wi]).reshape(2, HEADS, N_SLOT, 32, HEAD_DIM).transpose(2, 0, 1, 3, 4)
    g_gate = g_gate.reshape(N_SLOT, 2 * HEADS * 32, HEAD_DIM).astype(BF16)

    d_bs_row = jnp.pad(d_bs[:, :, 0].reshape(1, GROUPS * CHUNK), ((0, 0), (0, D - GROUPS * CHUNK)))
    vecs = jnp.concatenate([d_vec, jnp.concatenate([d_ln[0:2], d_gmlp, d_gfin, d_bs_row, jnp.zeros((3, D), F32)])])
    d_ws2 = d_ws.reshape(GROUPS * CHUNK, CHUNK)
    (g_in, r_in), (r_gate, r_vecs, r_ws) = _wgrad_in(n1_t, dz, core, comm=_sibling_plan([g_gate], [vecs, d_ws2]))
    p_in, p_gate = pair(["in", "gate"], [g_in, g_gate], [r_in, r_gate])
    vecs_chip = _add2("pair_sum_vecs", vecs, r_vecs)
    ws_chip = _add2("pair_sum_ws", d_ws2, r_ws)
    (dx, d_gmix), (got_in, got_gate, got_vecs, got_ws) = _in_bwd(
        dz, w_in_g, xs, dh1, norm_mix_g, comm=_chips_plan([p_in, p_gate], [vecs_chip, ws_chip]))
    vecs_sum = _sum_terms("sum_vecs", got_vecs)
    last = jnp.concatenate([d_gmix, jnp.pad(loss_acc[0:1], ((0, 0), (0, D - 128))), jnp.zeros((6, D), F32)])
    (last_all,) = _run_plan("exchange_last", _exchange_plan(last))
    last_sum = _sum_terms("sum_last", last_all)
    loss = last_sum[1, 0]
    got = [got_in, got_oa, got_ob, got_out, got_up, got_down, got_gate]

    def step(nm, terms, w, m, v, rows, cols):
        g, d, mn, vn = _adamw("adamw_" + nm, terms.reshape(4, rows, cols), w.reshape(rows, cols),
                              m.reshape(rows, cols), v.reshape(rows, cols))
        return [a.reshape(w.shape) for a in (g, d, mn, vn)]

    o_in = step("in", got[0], w_in, m_w_in, v_w_in, D, W_IN_COLS)
    o_oa = step("out_a", got[1], w_out_a, m_w_out_a, v_w_out_a, D // N_SLOT, D)
    o_ob = step("out_b", got[2], w_out_b, m_w_out_b, v_w_out_b, D // N_SLOT, D)
    o_out = step("out", got[3], w_out, m_w_out, v_w_out, D // N_SLOT, D)
    o_up = step("up", got[4], w_up, m_w_up, v_w_up, D, FF_COLS)
    o_down = step("down", got[5], w_down, m_w_down, v_w_down, FF_COLS, D)
    gate_w = jnp.stack([w_rgate[0], w_igate[0]]).reshape(2 * HEADS * 32, HEAD_DIM)
    gate_m = jnp.stack([m_w_rgate[0], m_w_igate[0]]).reshape(2 * HEADS * 32, HEAD_DIM)
    gate_v = jnp.stack([v_w_rgate[0], v_w_igate[0]]).reshape(2 * HEADS * 32, HEAD_DIM)
    o_gate = _adamw("adamw_gate", got[6], gate_w, gate_m, gate_v)
    o_gate = [a.reshape(2, 1, HEADS, 32, HEAD_DIM) for a in o_gate]
    o_wr = [a[0] for a in o_gate]
    o_wi = [a[1] for a in o_gate]

    def own(full, width):
        return lax.dynamic_slice_in_dim(full, me * width, width, axis=1)

    small_g = {
        "norm_mix_g": last_sum[0:1], "conv_w": own(vecs_sum[0:4], 128), "conv_b": vecs_sum[4:5],
        "b_rgate": own(vecs_sum[5:6].reshape(HEADS, HEAD_DIM), 32),
        "b_igate": own(vecs_sum[6:7].reshape(HEADS, HEAD_DIM), 32),
        "lru_lambda": vecs_sum[7:8], "sgu_ln_g": vecs_sum[8:9], "sgu_ln_b": vecs_sum[9:10],
        "norm_mlp_g": vecs_sum[10:11], "norm_final_g": vecs_sum[11:12],
        "sgu_b_s": vecs_sum[12, 0:GROUPS * CHUNK].reshape(GROUPS, CHUNK),
    }
    small_w = {"norm_mix_g": (norm_mix_g, m_norm_mix_g, v_norm_mix_g), "conv_w": (conv_w, m_conv_w, v_conv_w),
               "conv_b": (conv_b, m_conv_b, v_conv_b), "b_rgate": (b_rgate, m_b_rgate, v_b_rgate),
               "b_igate": (b_igate, m_b_igate, v_b_igate), "lru_lambda": (lru_lambda, m_lru_lambda, v_lru_lambda),
               "sgu_ln_g": (sgu_ln_g, m_sgu_ln_g, v_sgu_ln_g), "sgu_ln_b": (sgu_ln_b, m_sgu_ln_b, v_sgu_ln_b),
               "norm_mlp_g": (norm_mlp_g, m_norm_mlp_g, v_norm_mlp_g),
               "norm_final_g": (norm_final_g, m_norm_final_g, v_norm_final_g),
               "sgu_b_s": (sgu_b_s, m_sgu_b_s, v_sgu_b_s), "sgu_w_s": (sgu_w_s, m_sgu_w_s, v_sgu_w_s)}
    order = list(small_g)
    as2d = lambda k, a: a.reshape(small_g[k].shape)
    upd = _adamw_many("adamw_small", [small_g[k] for k in order], *[[as2d(k, small_w[k][q]) for k in order]
                                                                     for q in range(3)])
    o_small = {k: [a.reshape(small_w[k][0].shape) for a in (small_g[k],) + u] for k, u in zip(order, upd)}
    ws3 = [a[0].reshape(GROUPS * CHUNK, CHUNK) for a in small_w.pop("sgu_w_s")]
    o_small["sgu_w_s"] = [a.reshape(sgu_w_s.shape) for a in _adamw("adamw_ws", got_ws, *ws3)]

    per_weight = {"norm_mix_g": o_small["norm_mix_g"], "w_in": o_in, "conv_w": o_small["conv_w"],
                  "conv_b": o_small["conv_b"], "w_rgate": o_wr, "b_rgate": o_small["b_rgate"], "w_igate": o_wi,
                  "b_igate": o_small["b_igate"], "lru_lambda": o_small["lru_lambda"], "w_out_a": o_oa,
                  "sgu_ln_g": o_small["sgu_ln_g"], "sgu_ln_b": o_small["sgu_ln_b"], "sgu_w_s": o_small["sgu_w_s"],
                  "sgu_b_s": o_small["sgu_b_s"], "w_out_b": o_ob, "w_out": o_out, "norm_mlp_g": o_small["norm_mlp_g"],
                  "w_up": o_up, "w_down": o_down, "norm_final_g": o_small["norm_final_g"]}
    names_w = list(per_weight)
    return (loss, dx[None], *[per_weight[k][0] for k in names_w], *[per_weight[k][1] for k in names_w],
            *[per_weight[k][2] for k in names_w], *[per_weight[k][3] for k in names_w])
```

```python
import jax
import jax.numpy as jnp
from jax import lax
from jax.experimental import pallas as pl
from jax.experimental.pallas import tpu as pltpu

F32 = jnp.float32
BF16 = jnp.bfloat16
SDS = jax.ShapeDtypeStruct
MESH = pl.DeviceIdType.MESH
ANY = pl.BlockSpec(memory_space=pl.ANY)

D = 1024
N_SLOT = 8
W_IN_COLS = 768
FF_COLS = 512
HEADS, HEAD_DIM = 4, 256
GROUPS, GROUP_DIM = 4, 256
CHUNK = 128
CONV_K = 4
NORM_EPS = 1e-6
LN_EPS = 1e-5
LRU_C = 8.0
ADAM_LR, ADAM_B1, ADAM_B2, ADAM_EPS, ADAM_WD, ADAM_STEP = 0.001, 0.9, 0.999, 1e-08, 0.01, 10

TM_ROWS = 1024
TM_MERGE = 512
T_BRANCH_A = 512
T_BRANCH_B = 256
MiB = 1024 * 1024
SMALL_OPERAND = 16 * 1024

_GELU_C = 0.7978845608028654
_GELU_A = 0.044715


def _small_in_hbm(a):
    return pltpu.with_memory_space_constraint(a, pltpu.HBM) if a.size <= SMALL_OPERAND else a


def _cparams(sem, vmem_mib):
    return pltpu.CompilerParams(dimension_semantics=sem, vmem_limit_bytes=vmem_mib * MiB)


def _gelu(x):
    t = jnp.tanh(_GELU_C * (x + _GELU_A * x * x * x))
    return 0.5 * x * (1.0 + t)


def _gelu_and_grad(x):
    x2 = x * x
    t = jnp.tanh(_GELU_C * x * (1.0 + _GELU_A * x2))
    g = 0.5 * x * (1.0 + t)
    dg = 0.5 * (1.0 + t) + 0.5 * x * (1.0 - t * t) * _GELU_C * (1.0 + 3.0 * _GELU_A * x2)
    return g, dg


def _softplus(x):
    return jnp.maximum(x, 0.0) + jnp.log1p(jnp.exp(-jnp.abs(x)))


def _dot(a, b):
    return jnp.dot(a, b, preferred_element_type=F32)


def _dot_nt(a, b):
    return lax.dot_general(a, b, (((1,), (1,)), ((), ())), preferred_element_type=F32)


def _dot_tn(a, b):
    return lax.dot_general(a, b, (((0,), (0,)), ((), ())), preferred_element_type=F32)


def _rows_shifted(prev8, cur, k):
    ext = jnp.concatenate([prev8, cur], axis=0)
    return pltpu.roll(ext, k, 0)[8:]


def _rows_advanced(cur, next8, k):
    t = cur.shape[0]
    ext = jnp.concatenate([cur, next8], axis=0)
    return pltpu.roll(ext, t + 8 - k, 0)[:t]


def _first_second(x, y, c):
    ny, nx, far = _other_chips(x, y)
    pick = lambda a, b: a * (1 - c) + b * c
    first = tuple(pick(a, b) for a, b in zip(ny, nx))
    second = tuple(pick(b, a) for a, b in zip(ny, nx))
    return first, second, far


def _slot_order(x, y, c):
    chip = 2 * x + y
    first, second, far = _first_second(x, y, c)
    order = [2 * chip + c, 2 * chip + 1 - c, 2 * first[2] + c, 2 * second[2] + 1 - c, 2 * second[2] + c,
             2 * first[2] + 1 - c, 2 * far[2] + c, 2 * far[2] + 1 - c]
    return jnp.stack(order).astype(jnp.int32)


def _in_proj(x, g_mix, w_in_own, order, comm=None):
    s = x.shape[0]
    tm = min(TM_ROWS, s)
    ni = s // tm

    def body(order_ref, x_ref, g_ref, own_ref, z_ref, nt_ref, wg_ref, n_s, w_s, send_sems, recv_sems, local_sems):
        j, i = pl.program_id(0), pl.program_id(1)
        px, py, c = _place()
        chip = 2 * px + py
        me = 2 * chip + c
        sib = (px, py, 1 - c)
        chips = _other_chips(px, py)

        def rc(k, src, blk, to):
            return pltpu.make_async_remote_copy(src_ref=src, dst_ref=w_s.at[blk], send_sem=send_sems.at[k],
                                                recv_sem=recv_sems.at[k], device_id=to, device_id_type=MESH)

        del chips
        first, second, far = _first_second(px, py, c)
        blocks = [2 * first[2] + c, 2 * second[2] + c, 2 * far[2] + c]
        own_in = pltpu.make_async_copy(own_ref, w_s.at[me], local_sems.at[0])
        to_first = rc(1, own_ref, me, (first[0], first[1], c))
        to_second = rc(2, own_ref, me, (second[0], second[1], c))
        relay = rc(3, w_s.at[blocks[0]], blocks[0], (second[0], second[1], c))
        sends = [rc(0, own_ref, me, sib), to_first, to_second, relay]
        passed = [rc(4 + q, w_s.at[blk], blk, sib) for q, blk in enumerate(blocks)]
        keep = pltpu.make_async_copy(w_s, wg_ref, local_sems.at[1])

        @pl.when((i == 0) & (j == 0))
        def _():
            own_in.start()
            sends[0].start()
            to_first.start()
            own_in.wait()

        @pl.when((i == 0) & (j == 1))
        def _():
            rc(0, own_ref, 2 * chip + 1 - c, sib).wait_recv()

        for q, blk in enumerate(blocks):
            @pl.when((i == 0) & (j == 2 + 2 * q))
            def _():
                rc(1 + q, own_ref, blk, sib).wait_recv()
                passed[q].start()
                if q == 0:
                    to_second.start()
                    relay.start()

            @pl.when((i == 0) & (j == 3 + 2 * q))
            def _():
                rc(4 + q, own_ref, order_ref[j], sib).wait_recv()

        rows = pl.ds(pl.multiple_of(i * tm, tm), tm)

        @pl.when(j == 0)
        def _():
            xv = x_ref[...]
            rstd = lax.rsqrt(jnp.mean(xv * xv, axis=-1, keepdims=True) + NORM_EPS)
            nb = (xv * rstd * g_ref[...]).astype(BF16)
            n_s[rows, :] = nb
            nt_ref[...] = nb.T

        z_ref[...] = _dot(n_s[rows, :], w_s[order_ref[j]]).astype(BF16)

        @pl.when((i == 0) & (j == N_SLOT - 1))
        def _():
            keep.start()

        @pl.when((i == ni - 1) & (j == N_SLOT - 1))
        def _():
            for cp in sends + passed:
                cp.wait_send()
            keep.wait()

    first_pass = lambda j, i, o: (jnp.where(j == 0, i, ni - 1), 0)
    (z, n1, w_in_g), extra = _call(
        body, name="in_proj", grid=(N_SLOT, ni), prefetch=(order,),
        in_specs=[pl.BlockSpec((tm, D), first_pass),
                  pl.BlockSpec((1, D), lambda j, i, o: (0, 0)), ANY],
        out_specs=[pl.BlockSpec((tm, W_IN_COLS), lambda j, i, o: (i, o[j])),
                   pl.BlockSpec((D, tm), lambda j, i, o: (0, jnp.where(j == 0, i, ni - 1))), ANY],
        out_shape=[SDS((s, N_SLOT * W_IN_COLS), BF16), SDS((D, s), BF16), SDS((N_SLOT, D, W_IN_COLS), BF16)],
        scratch_shapes=[pltpu.VMEM((s, D), BF16), pltpu.VMEM((N_SLOT, D, W_IN_COLS), BF16),
                        pltpu.SemaphoreType.DMA((7,)), pltpu.SemaphoreType.DMA((7,)), pltpu.SemaphoreType.DMA((2,))],
        params=_cparams(("arbitrary", "arbitrary"), 56), args=(x, g_mix, w_in_own), comm=comm)
    return (z, n1, w_in_g), extra


def _decay(r, sp_lam):
    log_a = (-LRU_C) * r * sp_lam
    a = jnp.exp(log_a)
    return a, jnp.sqrt(-jnp.tanh(log_a) * (a * a + 1.0))


def _lru_gates(xc, xcb, wr_ref, br, wi_ref, bi, sp_lam, a_s, b_s, r_ref, i_ref):
    for h in range(HEADS):
        sl = slice(h * HEAD_DIM, (h + 1) * HEAD_DIM)
        r = jax.nn.sigmoid(_dot(xcb[:, sl], wr_ref[h]) + br[:, sl])
        ig = jax.nn.sigmoid(_dot(xcb[:, sl], wi_ref[h]) + bi[:, sl])
        a, mult = _decay(r, sp_lam[:, sl])
        a_s[:, sl] = a
        b_s[:, sl] = xc[:, sl] * ig * mult
        r_ref[:, sl] = r.astype(BF16)
        i_ref[:, sl] = ig.astype(BF16)


def _conv_fwd(xa, prev8, cw, cb):
    xc = cb + cw[0:1, :] * xa
    for k in range(1, CONV_K):
        xc = xc + cw[k:k + 1, :] * _rows_shifted(prev8, xa, k)
    return xc


def _branch_a_fwd(z, conv_w, conv_b, w_r, b_r, w_i, b_i, lam, comm=None):
    s = z.shape[0]
    ta = min(T_BRANCH_A, s)
    per16 = ta // 16

    def body(xa_ref, xp_ref, ga_ref, cw_ref, cb_ref, wr_ref, br_ref, wi_ref, bi_ref, lam_ref,
             ya_ref, hs_ref, xc_ref, r_ref, i_ref, a_s, b_s, h_s, carry_s):
        i = pl.program_id(0)

        @pl.when(i == 0)
        def _():
            carry_s[...] = jnp.zeros_like(carry_s)

        xa = xa_ref[...].astype(F32)
        prev8 = jnp.where(i > 0, xp_ref[...].astype(F32)[8:16], 0.0)
        xc = _conv_fwd(xa, prev8, cw_ref[...], cb_ref[...])
        xcb = xc.astype(BF16)
        xc_ref[...] = xcb
        sp_lam = _softplus(-lam_ref[...])
        _lru_gates(xc, xcb, wr_ref, br_ref[...], wi_ref, bi_ref[...], sp_lam, a_s, b_s, r_ref, i_ref)

        row = lax.broadcasted_iota(jnp.int32, (8, D), 0)

        def group(g, carry):
            off = pl.multiple_of(g * 8, 8)
            a8 = a_s[pl.ds(off, 8), :]
            b8 = b_s[pl.ds(off, 8), :]
            for d in (1, 2, 4):
                a_sh = jnp.where(row >= d, pltpu.roll(a8, d, 0), 1.0)
                b_sh = jnp.where(row >= d, pltpu.roll(b8, d, 0), 0.0)
                b8 = a8 * b_sh + b8
                a8 = a8 * a_sh
            h8 = b8 + a8 * carry
            h_s[pl.ds(off, 8), :] = h8
            return jnp.broadcast_to(h8[7:8, :], (8, D))

        carry_s[...] = lax.fori_loop(0, ta // 8, group, carry_s[...])
        hs = h_s[...]
        hs_ref[...] = hs.astype(BF16)
        ya_ref[...] = (hs * _gelu(ga_ref[...].astype(F32))).astype(BF16)

    vec = pl.BlockSpec((1, D), lambda i: (0, 0))
    gate = pl.BlockSpec((HEADS, HEAD_DIM, HEAD_DIM), lambda i: (0, 0, 0))
    return _call(
        body, name="branch_a_fwd", grid=(s // ta,),
        in_specs=[pl.BlockSpec((ta, D), lambda i: (i, 0)),
                  pl.BlockSpec((16, D), lambda i: (jnp.maximum(i * per16 - 1, 0), 0)),
                  pl.BlockSpec((ta, D), lambda i: (i, 1)),
                  pl.BlockSpec((CONV_K, D), lambda i: (0, 0)), vec, gate, vec, gate, vec, vec],
        out_specs=[pl.BlockSpec((ta, D), lambda i: (i, 0))] * 5,
        out_shape=[SDS((s, D), BF16)] * 5,
        scratch_shapes=[pltpu.VMEM((ta, D), F32), pltpu.VMEM((ta, D), F32), pltpu.VMEM((ta, D), F32),
                        pltpu.VMEM((8, D), F32)],
        params=_cparams(("arbitrary",), 40), args=(z, z, z, conv_w, conv_b, w_r, b_r, w_i, b_i, lam), comm=comm)


def _sgu_common(ub, vb, lg, lb, with_grad):
    if with_grad:
        u, du = _gelu_and_grad(ub)
        v, dv = _gelu_and_grad(vb)
    else:
        u, v, du, dv = _gelu(ub), _gelu(vb), None, None
    mu = jnp.mean(v, axis=-1, keepdims=True)
    vc = v - mu
    rstd = lax.rsqrt(jnp.mean(vc * vc, axis=-1, keepdims=True) + LN_EPS)
    vhat = vc * rstd
    vln = vhat * lg + lb
    return u, du, dv, rstd, vhat, vln


def _masked_ws(ws_ref):
    t = lax.broadcasted_iota(jnp.int32, (CHUNK, CHUNK), 0)
    c = lax.broadcasted_iota(jnp.int32, (CHUNK, CHUNK), 1)
    keep = c <= t
    return [jnp.where(keep, ws_ref[g], 0.0).astype(BF16) for g in range(GROUPS)]


def _branch_b_fwd(z, ln_g, ln_b, w_s, b_s_t, comm=None):
    s = z.shape[0]
    tb = min(T_BRANCH_B, s)

    def body(ub_ref, vb_ref, lg_ref, lb_ref, ws_ref, bs_ref, yb_ref):
        u, _, _, _, _, vln = _sgu_common(ub_ref[...].astype(F32), vb_ref[...].astype(F32),
                                         lg_ref[...], lb_ref[...], False)
        vlnb = vln.astype(BF16)
        wm = _masked_ws(ws_ref)
        bs = bs_ref[...]
        for c in range(tb // CHUNK):
            rs = slice(c * CHUNK, (c + 1) * CHUNK)
            for g in range(GROUPS):
                cs = slice(g * GROUP_DIM, (g + 1) * GROUP_DIM)
                sp = _dot(wm[g], vlnb[rs, cs]) + bs[:, g:g + 1]
                yb_ref[rs, cs] = (u[rs, cs] * sp).astype(BF16)

    vec = pl.BlockSpec((1, D), lambda i: (0, 0))
    return _call(
        body, name="branch_b_fwd", grid=(s // tb,),
        in_specs=[pl.BlockSpec((tb, D), lambda i: (i, 2)), pl.BlockSpec((tb, D), lambda i: (i, 3)), vec, vec,
                  pl.BlockSpec((GROUPS, CHUNK, CHUNK), lambda i: (0, 0, 0)),
                  pl.BlockSpec((CHUNK, GROUPS), lambda i: (0, 0))],
        out_specs=[pl.BlockSpec((tb, D), lambda i: (i, 0))],
        out_shape=[SDS((s, D), BF16)], scratch_shapes=[],
        params=_cparams(("arbitrary",), 40), args=(z, z, ln_g, ln_b, w_s, b_s_t), comm=comm)


def _merge_out(ya, yb, z, x, w_oa, w_ob, w_out, comm=None):
    s = x.shape[0]
    tm = min(TM_MERGE, s)

    def body(ya_ref, yb_ref, ma_ref, mb_ref, x_ref, woa_ref, wob_ref, wo_ref, pa_ref, pb_ref, mg_ref, h1_ref):
        pa = _dot(ya_ref[...], woa_ref[...])
        pb = _dot(yb_ref[...], wob_ref[...])
        merged = (jax.nn.sigmoid(ma_ref[...].astype(F32)) * pa
                  + jax.nn.sigmoid(mb_ref[...].astype(F32)) * pb).astype(BF16)
        pa_ref[...] = pa.astype(BF16)
        pb_ref[...] = pb.astype(BF16)
        mg_ref[...] = merged
        h1_ref[...] = x_ref[...] + _dot(merged, wo_ref[...])

    row = pl.BlockSpec((tm, D), lambda i: (i, 0))
    wsp = pl.BlockSpec((D, D), lambda i: (0, 0))
    return _call(
        body, name="merge_out", grid=(s // tm,),
        in_specs=[row, row, pl.BlockSpec((tm, D), lambda i: (i, 4)), pl.BlockSpec((tm, D), lambda i: (i, 5)),
                  row, wsp, wsp, wsp],
        out_specs=[row, row, row, row],
        out_shape=[SDS((s, D), BF16), SDS((s, D), BF16), SDS((s, D), BF16), SDS((s, D), F32)], scratch_shapes=[],
        params=_cparams(("arbitrary",), 48), args=(ya, yb, z, z, x, w_oa, w_ob, w_out), comm=comm)


def _mlp_fwd(h1, g_mlp, w_up_g, w_down, g_fin, tgt):
    s = h1.shape[0]
    tm = min(TM_ROWS, s)
    nj = N_SLOT

    def body(h1_ref, gm_ref, wu_ref, wd_ref, gf_ref, t_ref, r_ref, at_ref, n2t_ref, dh2_ref, loss_ref, dgf_ref,
             n2_s, acc_s):
        i, j = pl.program_id(0), pl.program_id(1)

        @pl.when(j == 0)
        def _():
            hv = h1_ref[...]
            rstd = lax.rsqrt(jnp.mean(hv * hv, axis=-1, keepdims=True) + NORM_EPS)
            nb = (hv * rstd * gm_ref[...]).astype(BF16)
            n2_s[...] = nb
            n2t_ref[...] = nb.T
            acc_s[...] = jnp.zeros_like(acc_s)

        @pl.when((i == 0) & (j == 0))
        def _():
            loss_ref[...] = jnp.zeros_like(loss_ref)
            dgf_ref[...] = jnp.zeros_like(dgf_ref)

        r = jnp.maximum(_dot(n2_s[...], wu_ref[...]), 0.0)
        r_ref[...] = r.astype(BF16)
        act = (r * r).astype(BF16)
        at_ref[...] = act.T
        acc_s[...] += _dot(act, wd_ref[...])

        @pl.when(j == nj - 1)
        def _():
            h2 = h1_ref[...] + acc_s[...]
            rstd = lax.rsqrt(jnp.mean(h2 * h2, axis=-1, keepdims=True) + NORM_EPS)
            hh = h2 * rstd
            gf = gf_ref[...]
            e = hh * gf - t_ref[...]
            loss_ref[...] += jnp.sum(e * e) * (0.5 / D)
            dy = e * (1.0 / D)
            dgf_ref[...] += jnp.sum(dy * hh, axis=0, keepdims=True)
            dhh = dy * gf
            dh2_ref[...] = rstd * (dhh - hh * jnp.mean(dhh * hh, axis=-1, keepdims=True))

    row = pl.BlockSpec((tm, D), lambda i, j: (i, 0))
    vec = pl.BlockSpec((1, D), lambda i, j: (0, 0))
    return pl.pallas_call(
        body, name="mlp_fwd", grid=(s // tm, nj),
        in_specs=[row, vec, pl.BlockSpec((None, D, FF_COLS), lambda i, j: (j, 0, 0)),
                  pl.BlockSpec((FF_COLS, D), lambda i, j: (j, 0)), vec, row],
        out_specs=[pl.BlockSpec((tm, FF_COLS), lambda i, j: (i, j)), pl.BlockSpec((FF_COLS, tm), lambda i, j: (j, i)),
                   pl.BlockSpec((D, tm), lambda i, j: (0, i)), row, pl.BlockSpec((8, 128), lambda i, j: (0, 0)), vec],
        out_shape=[SDS((s, nj * FF_COLS), BF16), SDS((nj * FF_COLS, s), BF16), SDS((D, s), BF16), SDS((s, D), F32),
                   SDS((8, 128), F32), SDS((1, D), F32)],
        scratch_shapes=[pltpu.VMEM((tm, D), BF16), pltpu.VMEM((tm, D), F32)],
        compiler_params=_cparams(("arbitrary", "arbitrary"), 52),
    )(h1, _small_in_hbm(g_mlp), w_up_g, w_down, _small_in_hbm(g_fin), tgt)


def _mlp_bwd(dh2, r, w_down, w_up_g, h1, g_mlp, comm=None):
    s = h1.shape[0]
    tm = min(TM_ROWS, s)
    nj = N_SLOT

    def body(dh2_ref, r_ref, wd_ref, wu_ref, h1_ref, gm_ref, df_ref, dh1_ref, dgm_ref, dh2b_s, acc_s):
        i, j = pl.program_id(0), pl.program_id(1)

        @pl.when(j == 0)
        def _():
            dh2b_s[...] = dh2_ref[...].astype(BF16)
            acc_s[...] = jnp.zeros_like(acc_s)

        @pl.when((i == 0) & (j == 0))
        def _():
            dgm_ref[...] = jnp.zeros_like(dgm_ref)

        d_act = _dot_nt(dh2b_s[...], wd_ref[...])
        df = (d_act * (2.0 * r_ref[...].astype(F32))).astype(BF16)
        df_ref[...] = df
        acc_s[...] += _dot_nt(df, wu_ref[...])

        @pl.when(j == nj - 1)
        def _():
            hv = h1_ref[...]
            rstd = lax.rsqrt(jnp.mean(hv * hv, axis=-1, keepdims=True) + NORM_EPS)
            hh = hv * rstd
            dn2 = acc_s[...]
            dgm_ref[...] += jnp.sum(dn2 * hh, axis=0, keepdims=True)
            dhat = dn2 * gm_ref[...]
            dh1_ref[...] = dh2_ref[...] + rstd * (dhat - hh * jnp.mean(dhat * hh, axis=-1, keepdims=True))

    row = pl.BlockSpec((tm, D), lambda i, j: (i, 0))
    vec = pl.BlockSpec((1, D), lambda i, j: (0, 0))
    ffb = pl.BlockSpec((tm, FF_COLS), lambda i, j: (i, j))
    return _call(
        body, name="mlp_bwd", grid=(s // tm, nj),
        in_specs=[row, ffb, pl.BlockSpec((FF_COLS, D), lambda i, j: (j, 0)),
                  pl.BlockSpec((None, D, FF_COLS), lambda i, j: (j, 0, 0)), row, vec],
        out_specs=[ffb, row, vec],
        out_shape=[SDS((s, nj * FF_COLS), BF16), SDS((s, D), F32), SDS((1, D), F32)],
        scratch_shapes=[pltpu.VMEM((tm, D), BF16), pltpu.VMEM((tm, D), F32)],
        params=_cparams(("arbitrary", "arbitrary"), 52), args=(dh2, r, w_down, w_up_g, h1, g_mlp), comm=comm)


def _merge_bwd(dh1, z, pa, pb, w_out, w_oa, w_ob, comm=None):
    s = dh1.shape[0]
    tm = min(TM_MERGE, s)

    def body(dh1_ref, ma_ref, mb_ref, pa_ref, pb_ref, wo_ref, woa_ref, wob_ref,
             dz_ref, dpa_ref, dpb_ref, dya_ref, dyb_ref):
        dm = _dot_nt(dh1_ref[...].astype(BF16), wo_ref[...])
        sa = jax.nn.sigmoid(ma_ref[...].astype(F32))
        sb = jax.nn.sigmoid(mb_ref[...].astype(F32))
        dpa = (dm * sa).astype(BF16)
        dpb = (dm * sb).astype(BF16)
        dz_ref[:, 0:D] = (dm * pa_ref[...].astype(F32) * sa * (1.0 - sa)).astype(BF16)
        dz_ref[:, D:2 * D] = (dm * pb_ref[...].astype(F32) * sb * (1.0 - sb)).astype(BF16)
        dpa_ref[...] = dpa
        dpb_ref[...] = dpb
        dya_ref[...] = _dot_nt(dpa, woa_ref[...]).astype(BF16)
        dyb_ref[...] = _dot_nt(dpb, wob_ref[...]).astype(BF16)

    row = pl.BlockSpec((tm, D), lambda i: (i, 0))
    wsp = pl.BlockSpec((D, D), lambda i: (0, 0))
    return _call(
        body, name="merge_bwd", grid=(s // tm,),
        in_specs=[row, pl.BlockSpec((tm, D), lambda i: (i, 4)), pl.BlockSpec((tm, D), lambda i: (i, 5)),
                  row, row, wsp, wsp, wsp],
        out_specs=[pl.BlockSpec((tm, 2 * D), lambda i: (i, 2)), row, row, row, row],
        out_shape=[SDS((s, 6 * D), BF16)] + [SDS((s, D), BF16)] * 4, scratch_shapes=[],
        params=_cparams(("arbitrary",), 48), args=(dh1, z, z, pa, pb, w_out, w_oa, w_ob), comm=comm)


def _branch_b_bwd(dz, dyb, z, ln_g, ln_b, w_s, b_s_t, comm=None):
    s = z.shape[0]
    tb = min(T_BRANCH_B, s)

    def body(dz_in, dyb_ref, ub_ref, vb_ref, lg_ref, lb_ref, ws_ref, bs_ref,
             dz_ref, dws_ref, dbs_ref, dln_ref, du_s, dvln_s):
        del dz_in

        @pl.when(pl.program_id(0) == 0)
        def _():
            dws_ref[...] = jnp.zeros_like(dws_ref)
            dbs_ref[...] = jnp.zeros_like(dbs_ref)
            dln_ref[...] = jnp.zeros_like(dln_ref)

        lg = lg_ref[...]
        u, du, dv, rstd, vhat, vln = _sgu_common(ub_ref[...].astype(F32), vb_ref[...].astype(F32),
                                                 lg, lb_ref[...], True)
        vlnb = vln.astype(BF16)
        dyb_v = dyb_ref[...].astype(F32)
        wm = _masked_ws(ws_ref)
        keep = (lax.broadcasted_iota(jnp.int32, (CHUNK, CHUNK), 1)
                <= lax.broadcasted_iota(jnp.int32, (CHUNK, CHUNK), 0))
        bs = bs_ref[...]
        for c in range(tb // CHUNK):
            rs = slice(c * CHUNK, (c + 1) * CHUNK)
            for g in range(GROUPS):
                cs = slice(g * GROUP_DIM, (g + 1) * GROUP_DIM)
                v_blk = vlnb[rs, cs]
                sp = _dot(wm[g], v_blk) + bs[:, g:g + 1]
                d_sp = dyb_v[rs, cs] * u[rs, cs]
                d_spb = d_sp.astype(BF16)
                du_s[rs, cs] = dyb_v[rs, cs] * sp
                dvln_s[rs, cs] = _dot_tn(wm[g], d_spb)
                dws_ref[g] += jnp.where(keep, _dot_nt(d_spb, v_blk), 0.0)
                dbs_ref[g] += jnp.broadcast_to(jnp.sum(d_sp, axis=-1, keepdims=True), (CHUNK, CHUNK))
        dvln = dvln_s[...]
        dln_ref[0:1, :] += jnp.sum(dvln * vhat, axis=0, keepdims=True)
        dln_ref[1:2, :] += jnp.sum(dvln, axis=0, keepdims=True)
        dvh = dvln * lg
        d_v = rstd * (dvh - jnp.mean(dvh, axis=-1, keepdims=True)
                      - vhat * jnp.mean(dvh * vhat, axis=-1, keepdims=True))
        dz_ref[:, 0:D] = (du_s[...] * du).astype(BF16)
        dz_ref[:, D:2 * D] = (d_v * dv).astype(BF16)

    vec = pl.BlockSpec((1, D), lambda i: (0, 0))
    sq = pl.BlockSpec((GROUPS, CHUNK, CHUNK), lambda i: (0, 0, 0))
    return _call(
        body, name="branch_b_bwd", grid=(s // tb,),
        in_specs=[ANY, pl.BlockSpec((tb, D), lambda i: (i, 0)),
                  pl.BlockSpec((tb, D), lambda i: (i, 2)), pl.BlockSpec((tb, D), lambda i: (i, 3)), vec, vec, sq,
                  pl.BlockSpec((CHUNK, GROUPS), lambda i: (0, 0))],
        out_specs=[pl.BlockSpec((tb, 2 * D), lambda i: (i, 1)), sq, sq, pl.BlockSpec((8, D), lambda i: (0, 0))],
        out_shape=[SDS(dz.shape, BF16), SDS((GROUPS, CHUNK, CHUNK), F32), SDS((GROUPS, CHUNK, CHUNK), F32),
                   SDS((8, D), F32)],
        scratch_shapes=[pltpu.VMEM((tb, D), F32), pltpu.VMEM((tb, D), F32)], aliases={0: 0},
        params=_cparams(("arbitrary",), 40), args=(dz, dyb, z, z, ln_g, ln_b, w_s, b_s_t), comm=comm)


def _branch_a_bwd(dz, dya, z, hs, xc, r, ig, conv_w, w_r, w_i, lam, comm=None):
    s = z.shape[0]
    ta = min(T_BRANCH_A, s)
    nb = s // ta
    per16 = ta // 16

    def body(dz_in, dya_ref, xa_ref, ga_ref, hs_ref, hp_ref, xc_ref, r_ref, i_ref, cw_ref, wr_ref, wi_ref,
             lam_ref, dz_ref, vec_ref, dwr_ref, dwi_ref, a_s, b_s, h_s, dcar_s, acar_s, dxc_s):
        del dz_in
        i = pl.program_id(0)
        blk = nb - 1 - i

        @pl.when(i == 0)
        def _():
            dcar_s[...] = jnp.zeros_like(dcar_s)
            acar_s[...] = jnp.zeros_like(acar_s)
            dxc_s[...] = jnp.zeros_like(dxc_s)
            vec_ref[...] = jnp.zeros_like(vec_ref)
            dwr_ref[...] = jnp.zeros_like(dwr_ref)
            dwi_ref[...] = jnp.zeros_like(dwi_ref)

        cw = cw_ref[...]
        lam_v = lam_ref[...]
        xa = xa_ref[...].astype(F32)
        xcb = xc_ref[...]
        xc = xcb.astype(F32)
        sp_lam = _softplus(-lam_v)
        r_v, i_v = r_ref[...].astype(F32), i_ref[...].astype(F32)
        a_v, m_v = _decay(r_v, sp_lam)

        hs_v = hs_ref[...].astype(F32)
        hprev8 = jnp.where(blk > 0, hp_ref[...].astype(F32)[8:16], 0.0)
        h_m1 = _rows_shifted(hprev8, hs_v, 1)
        gg, dgg = _gelu_and_grad(ga_ref[...].astype(F32))
        dya_v = dya_ref[...].astype(F32)
        dz_ref[:, D:2 * D] = (dya_v * hs_v * dgg).astype(BF16)

        a_s[...] = _rows_advanced(a_v, acar_s[...], 1)
        b_s[...] = dya_v * gg

        row = lax.broadcasted_iota(jnp.int32, (8, D), 0)
        ng = ta // 8

        def group(gi, carry):
            off = pl.multiple_of((ng - 1 - gi) * 8, 8)
            c8 = a_s[pl.ds(off, 8), :]
            d8 = b_s[pl.ds(off, 8), :]
            for d in (1, 2, 4):
                c_sh = jnp.where(row < 8 - d, pltpu.roll(c8, 8 - d, 0), 1.0)
                d_sh = jnp.where(row < 8 - d, pltpu.roll(d8, 8 - d, 0), 0.0)
                d8 = c8 * d_sh + d8
                c8 = c8 * c_sh
            dh8 = d8 + c8 * carry
            h_s[pl.ds(off, 8), :] = dh8
            return jnp.broadcast_to(dh8[0:1, :], (8, D))

        dcar_s[...] = lax.fori_loop(0, ng, group, dcar_s[...])
        acar_s[...] = jnp.broadcast_to(a_v[0:1, :], (8, D))

        dbx = h_s[...]
        d_mult = dbx * xc * i_v
        d_loga = dbx * h_m1 * a_v - d_mult * (a_v * a_v) / m_v
        d_pr = d_loga * ((-LRU_C) * sp_lam) * r_v * (1.0 - r_v)
        d_pi = dbx * xc * m_v * i_v * (1.0 - i_v)
        vec_ref[7:8, :] += jnp.sum(d_loga * r_v, axis=0, keepdims=True) * (LRU_C * jax.nn.sigmoid(-lam_v))
        vec_ref[5:6, :] += jnp.sum(d_pr, axis=0, keepdims=True)
        vec_ref[6:7, :] += jnp.sum(d_pi, axis=0, keepdims=True)
        d_prb = d_pr.astype(BF16)
        d_pib = d_pi.astype(BF16)
        h_s[...] = dbx * i_v * m_v
        for h in range(HEADS):
            sl = slice(h * HEAD_DIM, (h + 1) * HEAD_DIM)
            h_s[:, sl] += _dot_nt(d_prb[:, sl], wr_ref[h]) + _dot_nt(d_pib[:, sl], wi_ref[h])
            dwr_ref[h] += _dot_tn(xcb[:, sl], d_prb[:, sl])
            dwi_ref[h] += _dot_tn(xcb[:, sl], d_pib[:, sl])
        d_xc = h_s[...]
        vec_ref[4:5, :] += jnp.sum(d_xc, axis=0, keepdims=True)
        vec_ref[0:1, :] += jnp.sum(d_xc * xa, axis=0, keepdims=True)
        d_xa = cw[0:1, :] * d_xc
        nxt = dxc_s[...]
        for k in range(1, CONV_K):
            ahead = _rows_advanced(d_xc, nxt, k)
            vec_ref[k:k + 1, :] += jnp.sum(ahead * xa, axis=0, keepdims=True)
            d_xa = d_xa + cw[k:k + 1, :] * ahead
        dz_ref[:, 0:D] = d_xa.astype(BF16)
        dxc_s[...] = d_xc[0:8, :]

    vec = pl.BlockSpec((1, D), lambda i: (0, 0))
    gate = pl.BlockSpec((HEADS, HEAD_DIM, HEAD_DIM), lambda i: (0, 0, 0))
    cur = lambda c: pl.BlockSpec((ta, D), lambda i: (nb - 1 - i, c))
    before = lambda c: pl.BlockSpec((16, D), lambda i: (jnp.maximum((nb - 1 - i) * per16 - 1, 0), c))
    return _call(
        body, name="branch_a_bwd", grid=(nb,),
        in_specs=[ANY, cur(0), cur(0), cur(1), cur(0), before(0), cur(0), cur(0), cur(0),
                  pl.BlockSpec((CONV_K, D), lambda i: (0, 0)), gate, gate, vec],
        out_specs=[pl.BlockSpec((ta, 2 * D), lambda i: (nb - 1 - i, 0)), pl.BlockSpec((8, D), lambda i: (0, 0)),
                   gate, gate],
        out_shape=[SDS(dz.shape, BF16), SDS((8, D), F32), SDS((HEADS, HEAD_DIM, HEAD_DIM), F32),
                   SDS((HEADS, HEAD_DIM, HEAD_DIM), F32)],
        scratch_shapes=[pltpu.VMEM((ta, D), F32)] * 3 + [pltpu.VMEM((8, D), F32)] * 3, aliases={0: 0},
        params=_cparams(("arbitrary",), 48),
        args=(dz, dya, z, z, hs, hs, xc, r, ig, conv_w, w_r, w_i, lam), comm=comm)


def _in_bwd(dz, w_in_g, x, dh1, g_mix, comm=None):
    s = x.shape[0]
    tm = min(TM_ROWS, s)
    nj = N_SLOT

    def body(dz_ref, w_ref, x_ref, dh1_ref, g_ref, dx_ref, dg_ref, acc_s):
        i, j = pl.program_id(0), pl.program_id(1)

        @pl.when(j == 0)
        def _():
            acc_s[...] = jnp.zeros_like(acc_s)

        @pl.when((i == 0) & (j == 0))
        def _():
            dg_ref[...] = jnp.zeros_like(dg_ref)

        acc_s[...] += _dot_nt(dz_ref[...], w_ref[...])

        @pl.when(j == nj - 1)
        def _():
            xv = x_ref[...]
            rstd = lax.rsqrt(jnp.mean(xv * xv, axis=-1, keepdims=True) + NORM_EPS)
            xh = xv * rstd
            dn = acc_s[...]
            dg_ref[...] += jnp.sum(dn * xh, axis=0, keepdims=True)
            dhat = dn * g_ref[...]
            dx_ref[...] = dh1_ref[...] + rstd * (dhat - xh * jnp.mean(dhat * xh, axis=-1, keepdims=True))

    row = pl.BlockSpec((tm, D), lambda i, j: (i, 0))
    vec = pl.BlockSpec((1, D), lambda i, j: (0, 0))
    return _call(
        body, name="in_bwd", grid=(s // tm, nj),
        in_specs=[pl.BlockSpec((tm, W_IN_COLS), lambda i, j: (i, j)),
                  pl.BlockSpec((None, D, W_IN_COLS), lambda i, j: (j, 0, 0)), row, row, vec],
        out_specs=[row, vec],
        out_shape=[SDS((s, D), F32), SDS((1, D), F32)],
        scratch_shapes=[pltpu.VMEM((tm, D), F32)],
        params=_cparams(("arbitrary", "arbitrary"), 48), args=(dz, w_in_g, x, dh1, g_mix), comm=comm)


def _wgrad(name, a, b, nblk, a_split, b_split):
    s = a.shape[0]
    ts = min(TM_ROWS, s)
    a_w = a.shape[1] // nblk if a_split else a.shape[1]
    b_w = b.shape[1] // nblk if b_split else b.shape[1]

    def body(a_ref, b_ref, o_ref, acc_s):
        t = pl.program_id(1)

        @pl.when(t == 0)
        def _():
            acc_s[...] = jnp.zeros_like(acc_s)

        acc_s[...] += _dot_tn(a_ref[...].astype(BF16), b_ref[...].astype(BF16))

        @pl.when(t == pl.num_programs(1) - 1)
        def _():
            o_ref[...] = acc_s[...].astype(BF16)

    return pl.pallas_call(
        body, name=name, grid=(nblk, s // ts),
        in_specs=[pl.BlockSpec((ts, a_w), (lambda k, t: (t, k)) if a_split else (lambda k, t: (t, 0))),
                  pl.BlockSpec((ts, b_w), (lambda k, t: (t, k)) if b_split else (lambda k, t: (t, 0)))],
        out_specs=pl.BlockSpec((None, a_w, b_w), lambda k, t: (k, 0, 0)),
        out_shape=SDS((nblk, a_w, b_w), BF16),
        scratch_shapes=[pltpu.VMEM((a_w, b_w), F32)],
        compiler_params=_cparams(("arbitrary", "arbitrary"), 48),
    )(a, b)


def _wgrad_t(name, a_t, b, nblk, a_split, b_split, tokens):
    s = b.shape[0]
    ts = min(tokens, s)
    a_w = a_t.shape[0] // nblk if a_split else a_t.shape[0]
    b_w = b.shape[1] // nblk if b_split else b.shape[1]

    def body(a_ref, b_ref, o_ref, acc_s):
        t = pl.program_id(1)

        @pl.when(t == 0)
        def _():
            acc_s[...] = jnp.zeros_like(acc_s)

        acc_s[...] += _dot(a_ref[...], b_ref[...].astype(BF16))

        @pl.when(t == pl.num_programs(1) - 1)
        def _():
            o_ref[...] = acc_s[...].astype(BF16)

    return pl.pallas_call(
        body, name=name, grid=(nblk, s // ts),
        in_specs=[pl.BlockSpec((a_w, ts), (lambda k, t: (k, t)) if a_split else (lambda k, t: (0, t))),
                  pl.BlockSpec((ts, b_w), (lambda k, t: (t, k)) if b_split else (lambda k, t: (t, 0)))],
        out_specs=pl.BlockSpec((None, a_w, b_w), lambda k, t: (k, 0, 0)),
        out_shape=SDS((nblk, a_w, b_w), BF16),
        scratch_shapes=[pltpu.VMEM((a_w, b_w), F32)],
        compiler_params=_cparams(("arbitrary", "arbitrary"), 48),
    )(a_t, b)


def _place():
    x, y, c = lax.axis_index("x"), lax.axis_index("y"), lax.axis_index("c")
    return x, y, c


def _other_chips(x, y):
    return [(x, 1 - y, 2 * x + 1 - y), (1 - x, y, 2 * (1 - x) + y), (1 - x, 1 - y, 2 * (1 - x) + 1 - y)]


class _Plan:
    def __init__(self, arrays, out_shape, sems, start, finish, middle=None, middle_at=6):
        self.arrays, self.out_shape, self.sems, self.start, self.finish = arrays, out_shape, sems, start, finish
        self.middle, self.middle_at = middle, middle_at


def _gather_plan(shards, middle_at=6):
    n = len(shards)

    def copies(ins, outs, sems):
        send_sems, recv_sems, local_sems = sems
        x, y, c = _place()
        chip = 2 * x + y
        me = 2 * chip + c
        sib = (x, y, 1 - c)
        chips = _other_chips(x, y)

        def rc(k, t, src, blk, to):
            return pltpu.make_async_remote_copy(
                src_ref=src, dst_ref=outs[t].at[blk], send_sem=send_sems.at[k * n + t],
                recv_sem=recv_sems.at[k * n + t], device_id=to, device_id_type=MESH)

        (yx, yy, y_chip), (xx, xy, x_chip), _ = chips
        local = [pltpu.make_async_copy(ins[t], outs[t].at[me], local_sems.at[t]) for t in range(n)]
        sends = ([rc(0, t, ins[t], me, sib) for t in range(n)] + [rc(1, t, ins[t], me, (yx, yy, c)) for t in range(n)]
                 + [rc(2, t, ins[t], me, (xx, xy, c)) for t in range(n)])
        passed = [[rc(4 + j, t, outs[t].at[2 * pc + c], 2 * pc + c, sib) for t in range(n)]
                  for j, (_, _, pc) in enumerate(chips)]
        relays = [[rc(3, t, outs[t].at[2 * y_chip + c], 2 * y_chip + c, (xx, xy, c)) for t in range(n)],
                  [rc(3, t, outs[t].at[2 * x_chip + c], 2 * x_chip + c, (yx, yy, c)) for t in range(n)]]
        return rc, local, sends, passed, relays, chips, chip, c, sib

    def start(ins, outs, sems):
        _, local, sends, _, _, _, _, _, _ = copies(ins, outs, sems)
        for cp in local + sends:
            cp.start()

    def middle(ins, outs, sems):
        rc, _, _, passed, relays, chips, _, c, sib = copies(ins, outs, sems)
        for j in range(2):
            for t in range(n):
                rc(1 + j, t, ins[t], 2 * chips[j][2] + c, sib).wait_recv()
        for j in range(2):
            for cp in passed[j]:
                cp.start()

            @pl.when(c == j)
            def _():
                for cp in relays[j]:
                    cp.start()

    def finish(ins, outs, sems):
        rc, local, sends, passed, relays, chips, chip, c, sib = copies(ins, outs, sems)
        far = 2 * chips[2][2] + c
        for t in range(n):
            rc(3, t, ins[t], far, sib).wait_recv()
        for cp in passed[2]:
            cp.start()
        for t in range(n):
            rc(0, t, ins[t], 2 * chip + 1 - c, sib).wait_recv()
        for j, (px, py, pc) in enumerate(chips):
            for t in range(n):
                rc(4 + j, t, ins[t], 2 * pc + 1 - c, sib).wait_recv()
        for cp in sends + passed[0] + passed[1] + passed[2]:
            cp.wait_send()
        for j in range(2):
            @pl.when(c == j)
            def _():
                for cp in relays[j]:
                    cp.wait_send()
        for cp in local:
            cp.wait()

    return _Plan(list(shards), [SDS((N_SLOT,) + tuple(a.shape), a.dtype) for a in shards],
                 [pltpu.SemaphoreType.DMA((7 * n,)), pltpu.SemaphoreType.DMA((7 * n,)),
                  pltpu.SemaphoreType.DMA((n,))], start, finish, middle, middle_at)


def _sibling_plan(grads, whole=()):
    n, m = len(grads), len(whole)

    def copies(ins, outs, sems):
        send_sems, recv_sems = sems
        x, y, c = _place()
        sib = (x, y, 1 - c)

        def rc(t, src, dst):
            return pltpu.make_async_remote_copy(src_ref=src, dst_ref=dst, send_sem=send_sems.at[t],
                                                recv_sem=recv_sems.at[t], device_id=sib, device_id_type=MESH)
        return rc, c

    def start(ins, outs, sems):
        rc, c = copies(ins, outs, sems)
        for t in range(n):
            for j in range(4):
                rc(t, ins[t].at[2 * j + 1 - c], outs[t].at[j]).start()
        for t in range(n, n + m):
            rc(t, ins[t], outs[t]).start()

    def finish(ins, outs, sems):
        rc, _ = copies(ins, outs, sems)
        for t in range(n):
            rc(t, ins[t].at[pl.ds(0, 4)], outs[t]).wait()
        for t in range(n, n + m):
            rc(t, ins[t], outs[t]).wait()

    return _Plan(list(grads) + list(whole),
                 [SDS((4,) + tuple(g.shape[1:]), g.dtype) for g in grads] + [SDS(a.shape, a.dtype) for a in whole],
                 [pltpu.SemaphoreType.DMA((n + m,)), pltpu.SemaphoreType.DMA((n + m,))], start, finish)


def _chips_plan(parts, whole=()):
    n, m = len(parts), len(whole)

    def src_of(ins, t, pc):
        return ins[t].at[pc] if t < n else ins[t]

    def local_copies(ins, outs, sems, chip):
        return [pltpu.make_async_copy(src_of(ins, t, chip), outs[t].at[chip], sems[2].at[t]) for t in range(n + m)]

    def start(ins, outs, sems):
        send_sems, recv_sems, _ = sems
        x, y, c = _place()
        chip = 2 * x + y
        for cp in local_copies(ins, outs, sems, chip):
            cp.start()
        for px, py, pc in _other_chips(x, y):
            for t in range(n + m):
                pltpu.make_async_remote_copy(src_ref=src_of(ins, t, pc), dst_ref=outs[t].at[chip],
                                             send_sem=send_sems.at[t], recv_sem=recv_sems.at[t],
                                             device_id=(px, py, c), device_id_type=MESH).start()

    def finish(ins, outs, sems):
        send_sems, recv_sems, _ = sems
        x, y, c = _place()
        for t in range(n + m):
            three = outs[t].at[pl.ds(0, 3)]
            pltpu.make_async_remote_copy(src_ref=three, dst_ref=three, send_sem=send_sems.at[t],
                                         recv_sem=recv_sems.at[t], device_id=(x, y, c), device_id_type=MESH).wait()
        for cp in local_copies(ins, outs, sems, 2 * x + y):
            cp.wait()

    return _Plan(list(parts) + list(whole),
                 [SDS(p.shape, p.dtype) for p in parts] + [SDS((4,) + tuple(a.shape), a.dtype) for a in whole],
                 [pltpu.SemaphoreType.DMA((n + m,)), pltpu.SemaphoreType.DMA((n + m,)),
                  pltpu.SemaphoreType.DMA((n + m,))], start, finish)


def _exchange_plan(arr):
    def peers(x, y, c):
        flip = lambda v, f: 1 - v if f else v
        return [(flip(x, fx), flip(y, fy), flip(c, fc))
                for fx in (0, 1) for fy in (0, 1) for fc in (0, 1) if fx or fy or fc]

    def start(ins, outs, sems):
        x, y, c = _place()
        me = 4 * x + 2 * y + c
        pltpu.make_async_copy(ins[0], outs[0].at[me], sems[2].at[0]).start()
        for to in peers(x, y, c):
            pltpu.make_async_remote_copy(src_ref=ins[0], dst_ref=outs[0].at[me], send_sem=sems[0].at[0],
                                         recv_sem=sems[1].at[0], device_id=to, device_id_type=MESH).start()

    def finish(ins, outs, sems):
        x, y, c = _place()
        seven = outs[0].at[pl.ds(0, 7)]
        pltpu.make_async_remote_copy(src_ref=seven, dst_ref=seven, send_sem=sems[0].at[0], recv_sem=sems[1].at[0],
                                     device_id=(x, y, c), device_id_type=MESH).wait()
        pltpu.make_async_copy(ins[0], outs[0].at[4 * x + 2 * y + c], sems[2].at[0]).wait()

    return _Plan([arr], [SDS((N_SLOT,) + tuple(arr.shape), arr.dtype)],
                 [pltpu.SemaphoreType.DMA((1,)), pltpu.SemaphoreType.DMA((1,)), pltpu.SemaphoreType.DMA((1,))],
                 start, finish)


def _join(*plans):
    def cut(seq, sizes):
        out, at = [], 0
        for k in sizes:
            out.append(seq[at:at + k])
            at += k
        return out

    n_arr = [len(p.arrays) for p in plans]
    n_sem = [len(p.sems) for p in plans]

    def start(ins, outs, sems):
        for p, i, o, s in zip(plans, cut(ins, n_arr), cut(outs, n_arr), cut(sems, n_sem)):
            p.start(i, o, s)

    def finish(ins, outs, sems):
        for p, i, o, s in zip(plans, cut(ins, n_arr), cut(outs, n_arr), cut(sems, n_sem)):
            p.finish(i, o, s)

    def middle(ins, outs, sems):
        for p, i, o, s in zip(plans, cut(ins, n_arr), cut(outs, n_arr), cut(sems, n_sem)):
            if p.middle is not None:
                p.middle(i, o, s)

    return _Plan([a for p in plans for a in p.arrays], [o for p in plans for o in p.out_shape],
                 [s for p in plans for s in p.sems], start, finish,
                 middle if any(p.middle is not None for p in plans) else None,
                 max(p.middle_at for p in plans))


def _run_plan(name, plan):
    k = len(plan.arrays)

    def body(*refs):
        ins, outs, sems = refs[:k], refs[k:2 * k], refs[2 * k:]
        plan.start(ins, outs, sems)
        if plan.middle is not None:
            plan.middle(ins, outs, sems)
        plan.finish(ins, outs, sems)

    return pl.pallas_call(
        body, name=name, in_specs=[ANY] * k, out_specs=[ANY] * k, out_shape=plan.out_shape,
        scratch_shapes=plan.sems, compiler_params=pltpu.CompilerParams(has_side_effects=True),
    )(*plan.arrays)


def _call(body, *, name, grid, in_specs, out_specs, out_shape, scratch_shapes, params, args, comm=None,
          aliases=None, prefetch=()):
    aliases = aliases or {}
    n_pre = len(prefetch)

    def launch(fn, ins_specs, outs_specs, outs_shape, scratch, operands):
        spec = pltpu.PrefetchScalarGridSpec(num_scalar_prefetch=n_pre, grid=grid, in_specs=ins_specs,
                                            out_specs=outs_specs, scratch_shapes=scratch)
        return pl.pallas_call(fn, name=name, grid_spec=spec, out_shape=outs_shape, compiler_params=params,
                              input_output_aliases=aliases)(*prefetch, *[_small_in_hbm(a) for a in operands])

    if comm is None:
        return list(launch(body, in_specs, out_specs, out_shape, scratch_shapes, args)), []
    n_in, n_out, n_scr, k = len(in_specs), len(out_specs), len(scratch_shapes), len(comm.arrays)

    def wrapped(*refs):
        pre, refs = refs[:n_pre], refs[n_pre:]
        ins = refs[:n_in]
        c_in = refs[n_in:n_in + k]
        outs = refs[n_in + k:n_in + k + n_out]
        c_out = refs[n_in + k + n_out:n_in + 2 * k + n_out]
        scr = refs[n_in + 2 * k + n_out:n_in + 2 * k + n_out + n_scr]
        sems = refs[n_in + 2 * k + n_out + n_scr:]
        step, steps = pl.program_id(0), grid[0]
        for d in range(1, len(grid)):
            step, steps = step * grid[d] + pl.program_id(d), steps * grid[d]

        @pl.when(step == 0)
        def _():
            comm.start(c_in, c_out, sems)

        if comm.middle is not None:
            @pl.when(step == (comm.middle_at * steps) // 8)
            def _():
                comm.middle(c_in, c_out, sems)

        body(*pre, *ins, *outs, *scr)

        @pl.when(step == steps - 1)
        def _():
            comm.finish(c_in, c_out, sems)

    res = launch(wrapped, list(in_specs) + [ANY] * k, list(out_specs) + [ANY] * k,
                 list(out_shape) + list(comm.out_shape), list(scratch_shapes) + list(comm.sems),
                 tuple(args) + tuple(comm.arrays))
    return list(res[:n_out]), list(res[n_out:])


def _wgrad_in(n1_t, dz, core, comm=None):
    s = dz.shape[0]
    half = N_SLOT // 2

    def owner(k, c):
        return 2 * (k % half) + jnp.where(k < half, 1 - c, c)

    def body(c_ref, a_ref, b_ref, keep_ref, recv_ref, out_s, send_sems, recv_sems):
        del c_ref
        k = pl.program_id(0)
        x, y, c = _place()
        slot = k % 2

        def to_sibling(j):
            return pltpu.make_async_remote_copy(src_ref=out_s.at[j % 2], dst_ref=recv_ref.at[j],
                                                send_sem=send_sems.at[j], recv_sem=recv_sems.at[j],
                                                device_id=(x, y, 1 - c), device_id_type=MESH)

        @pl.when((k >= 2) & (k < half + 2))
        def _():
            to_sibling(k - 2).wait_send()

        @pl.when(k < half)
        def _():
            out_s[slot] = _dot(a_ref[...], b_ref[...]).astype(BF16)
            to_sibling(k).start()

        @pl.when(k >= half)
        def _():
            keep_ref[...] = _dot(a_ref[...], b_ref[...]).astype(BF16)

        @pl.when(k == N_SLOT - 1)
        def _():
            for j in range(half):
                to_sibling(j).wait_recv()

    blk = SDS((half, D, W_IN_COLS), BF16)
    return _call(body, name="wgrad_in", grid=(N_SLOT,),
                 in_specs=[pl.BlockSpec((D, s), lambda k, c_ref: (0, 0)),
                           pl.BlockSpec((s, W_IN_COLS), lambda k, c_ref: (0, owner(k, c_ref[0])))],
                 out_specs=[pl.BlockSpec((None, D, W_IN_COLS), lambda k, c_ref: (jnp.maximum(k - half, 0), 0, 0)), ANY],
                 out_shape=[blk, blk],
                 scratch_shapes=[pltpu.VMEM((2, D, W_IN_COLS), BF16), pltpu.SemaphoreType.DMA((half,)),
                                 pltpu.SemaphoreType.DMA((half,))],
                 params=_cparams(("arbitrary",), 48), args=(n1_t, dz), comm=comm, prefetch=(core,))


def _wgrad_down(act_t, dh2, core, tokens):
    s = dh2.shape[0]
    ts = min(tokens, s)
    half = N_SLOT // 2

    def body(c_ref, a_ref, b_ref, keep_ref, recv_ref, acc_s, out_s, send_sems, recv_sems):
        k, t = pl.program_id(0), pl.program_id(1)
        x, y, c = _place()
        last = t == pl.num_programs(1) - 1

        def to_sibling(j):
            return pltpu.make_async_remote_copy(src_ref=out_s.at[j % 2], dst_ref=recv_ref.at[j],
                                                send_sem=send_sems.at[j], recv_sem=recv_sems.at[j],
                                                device_id=(x, y, 1 - c), device_id_type=MESH)

        @pl.when(t == 0)
        def _():
            acc_s[...] = jnp.zeros_like(acc_s)

        acc_s[...] += _dot(a_ref[...], b_ref[...].astype(BF16)).reshape(acc_s.shape)

        @pl.when(last & (k >= 2))
        def _():
            to_sibling(k - 2).wait_send()

        @pl.when(last)
        def _():
            keep_ref[...] = acc_s[c_ref[0]].astype(BF16)
            out_s[k % 2] = acc_s[1 - c_ref[0]].astype(BF16)
            to_sibling(k).start()

        @pl.when(last & (k == half - 1))
        def _():
            for j in (half - 2, half - 1):
                to_sibling(j).wait_send()
            for j in range(half):
                to_sibling(j).wait_recv()

    blk = SDS((half, FF_COLS, D), BF16)
    (kept, recv), _ = _call(
        body, name="wgrad_down", grid=(half, s // ts),
        in_specs=[pl.BlockSpec((2 * FF_COLS, ts), lambda k, t, c_ref: (k, t)),
                  pl.BlockSpec((ts, D), lambda k, t, c_ref: (t, 0))],
        out_specs=[pl.BlockSpec((None, FF_COLS, D), lambda k, t, c_ref: (k, 0, 0)), ANY], out_shape=[blk, blk],
        scratch_shapes=[pltpu.VMEM((2, FF_COLS, D), F32), pltpu.VMEM((2, FF_COLS, D), BF16),
                        pltpu.SemaphoreType.DMA((half,)), pltpu.SemaphoreType.DMA((half,))],
        params=_cparams(("arbitrary", "arbitrary"), 48), args=(act_t, dh2), prefetch=(core,))
    return kept, recv


def _row_tile(rows):
    for t in (512, 256, 128, 64, 32, 16, 8):
        if rows % t == 0:
            return t
    return rows


def _pair_sum(name, g8, recv4, core):
    _, rows, cols = recv4.shape
    tr = _row_tile(rows)
    per = g8.shape[0] // 4
    g42 = g8.reshape(4, per, rows, cols)

    def body(c_ref, g_ref, r_ref, o_ref):
        del c_ref
        o_ref[...] = (g_ref[...].astype(F32) + r_ref[...].astype(F32)).astype(o_ref.dtype)

    return pl.pallas_call(
        body, name=name,
        grid_spec=pltpu.PrefetchScalarGridSpec(
            num_scalar_prefetch=1, grid=(4, rows // tr),
            in_specs=[pl.BlockSpec((None, None, tr, cols), lambda j, i, c_ref: (j, c_ref[0] * (per - 1), i, 0)),
                      pl.BlockSpec((None, tr, cols), lambda j, i, c_ref: (j, i, 0))],
            out_specs=pl.BlockSpec((None, tr, cols), lambda j, i, c_ref: (j, i, 0))),
        out_shape=SDS(recv4.shape, g8.dtype),
        compiler_params=_cparams(("arbitrary", "arbitrary"), 32),
    )(core, g42, recv4)


def _add2(name, a, b):
    rows, cols = a.shape
    tr = _row_tile(rows)

    def body(a_ref, b_ref, o_ref):
        o_ref[...] = a_ref[...] + b_ref[...]

    blk = pl.BlockSpec((tr, cols), lambda i: (i, 0))
    return pl.pallas_call(body, name=name, grid=(rows // tr,), in_specs=[blk, blk], out_specs=blk,
                          out_shape=SDS(a.shape, a.dtype),
                          compiler_params=_cparams(("arbitrary",), 32))(a, b)


def _sum_terms(name, terms):
    k, rows, cols = terms.shape
    tr = _row_tile(rows)

    def body(r_ref, o_ref):
        acc = r_ref[0]
        for q in range(1, k):
            acc = acc + r_ref[q]
        o_ref[...] = acc

    return pl.pallas_call(body, name=name, grid=(rows // tr,),
                          in_specs=[pl.BlockSpec((k, tr, cols), lambda i: (0, i, 0))],
                          out_specs=pl.BlockSpec((tr, cols), lambda i: (i, 0)),
                          out_shape=SDS((rows, cols), terms.dtype),
                          compiler_params=_cparams(("arbitrary",), 32))(terms)


def _adam_update(g, w, m, v):
    c1 = 1.0 / (1.0 - ADAM_B1 ** ADAM_STEP)
    c2 = 1.0 / (1.0 - ADAM_B2 ** ADAM_STEP)
    mn = ADAM_B1 * m + (1.0 - ADAM_B1) * g
    vn = ADAM_B2 * v + (1.0 - ADAM_B2) * (g * g)
    delta = (-ADAM_LR) * ((mn * c1) / (jnp.sqrt(vn * c2) + ADAM_EPS) + ADAM_WD * w)
    return delta, mn, vn


def _adamw_many(name, gs, ws, ms, vs):
    n = len(gs)

    def body(*refs):
        for p in range(n):
            g, w, m, v = (refs[q * n + p][...] for q in range(4))
            d, mn, vn = _adam_update(g, w, m, v)
            refs[4 * n + p][...] = d
            refs[5 * n + p][...] = mn
            refs[6 * n + p][...] = vn

    full = [pl.BlockSpec(w.shape, lambda i: (0, 0)) for w in ws]
    shapes = [SDS(w.shape, F32) for w in ws]
    res = pl.pallas_call(body, name=name, grid=(1,), in_specs=full * 4, out_specs=full * 3, out_shape=shapes * 3,
                         compiler_params=_cparams(("arbitrary",), 32),
                         )(*[_small_in_hbm(a) for a in (*gs, *ws, *ms, *vs)])
    return [(res[p], res[n + p], res[2 * n + p]) for p in range(n)]


def _adamw(name, terms, w, m, v):
    k, rows, cols = terms.shape
    tr = _row_tile(rows)

    def body(t_ref, w_ref, m_ref, v_ref, g_ref, d_ref, mo_ref, vo_ref):
        g = t_ref[0].astype(F32)
        for q in range(1, k):
            g = g + t_ref[q].astype(F32)
        g_ref[...] = g
        d_ref[...], mo_ref[...], vo_ref[...] = _adam_update(g, w_ref[...], m_ref[...], v_ref[...])

    blk = pl.BlockSpec((tr, cols), lambda i: (i, 0))
    return pl.pallas_call(body, name=name, grid=(rows // tr,),
                          in_specs=[pl.BlockSpec((k, tr, cols), lambda i: (0, i, 0)), blk, blk, blk],
                          out_specs=[blk] * 4, out_shape=[SDS((rows, cols), F32)] * 4,
                          compiler_params=_cparams(("arbitrary",), 40),
                          )(*[pltpu.with_memory_space_constraint(a, pltpu.HBM) for a in (terms, w, m, v)])


def kernel(x, norm_mix_g, w_in, conv_w, conv_b, w_rgate, b_rgate, w_igate, b_igate, lru_lambda, w_out_a, sgu_ln_g, sgu_ln_b, sgu_w_s, sgu_b_s, w_out_b, w_out, norm_mlp_g, w_up, w_down, norm_final_g, loss_target, m_norm_mix_g, m_w_in, m_conv_w, m_conv_b, m_w_rgate, m_b_rgate, m_w_igate, m_b_igate, m_lru_lambda, m_w_out_a, m_sgu_ln_g, m_sgu_ln_b, m_sgu_w_s, m_sgu_b_s, m_w_out_b, m_w_out, m_norm_mlp_g, m_w_up, m_w_down, m_norm_final_g, v_norm_mix_g, v_w_in, v_conv_w, v_conv_b, v_w_rgate, v_b_rgate, v_w_igate, v_b_igate, v_lru_lambda, v_w_out_a, v_sgu_ln_g, v_sgu_ln_b, v_sgu_w_s, v_sgu_b_s, v_w_out_b, v_w_out, v_norm_mlp_g, v_w_up, v_w_down, v_norm_final_g):
    cx, cy, cc = _place()
    me = 4 * cx + 2 * cy + cc
    core = jnp.reshape(cc, (1,)).astype(jnp.int32)
    xs = x[0]
    tgt = loss_target[0]

    gate_shard = jnp.stack([w_rgate[0], w_igate[0]]).astype(BF16).reshape(2 * HEADS * 32, HEAD_DIM)
    vec_shard = jnp.concatenate([conv_w[0], b_rgate[0], b_igate[0]], axis=1)
    vec_shard = jnp.pad(vec_shard, ((0, 4), (0, 256 - vec_shard.shape[1])))
    shards = [w_in[0].astype(BF16), w_out_a[0].astype(BF16), w_out_b[0].astype(BF16), w_out[0].astype(BF16),
              w_up[0].astype(BF16), w_down[0].astype(BF16), gate_shard, vec_shard]
    (z, n1_t, w_in_g), (gate_g, vec_g) = _in_proj(xs, norm_mix_g, shards[0], _slot_order(cx, cy, cc),
                                                comm=_gather_plan(shards[6:8]))
    gates = gate_g.reshape(N_SLOT, 2, HEADS, 32, HEAD_DIM).transpose(1, 2, 0, 3, 4).reshape(2, HEADS, HEAD_DIM, HEAD_DIM)
    w_r_f, w_i_f = gates[0], gates[1]
    conv_w_f = vec_g[:, 0:4, 0:128].transpose(1, 0, 2).reshape(CONV_K, D)
    b_r_f = vec_g[:, 0:4, 128:160].transpose(1, 0, 2).reshape(1, D)
    b_i_f = vec_g[:, 0:4, 160:192].transpose(1, 0, 2).reshape(1, D)
    b_s_t = jnp.transpose(sgu_b_s[0])

    (ya, hs, xc, r_gate, i_gate), (w_oa_g, w_ob_g, w_out_g, w_up_g) = _branch_a_fwd(
        z, conv_w_f, conv_b, w_r_f, b_r_f, w_i_f, b_i_f, lru_lambda, comm=_gather_plan(shards[1:5]))
    w_oa_f = w_oa_g.reshape(D, D)
    w_ob_f = w_ob_g.reshape(D, D)
    w_out_f = w_out_g.reshape(D, D)
    (yb,), _ = _branch_b_fwd(z, sgu_ln_g, sgu_ln_b, sgu_w_s[0], b_s_t)
    (pa, pb, merged, h1), (w_down_g,) = _merge_out(ya, yb, z, xs, w_oa_f, w_ob_f, w_out_f,
                                                   comm=_gather_plan(shards[5:6], middle_at=4))
    w_down_f = w_down_g.reshape(N_SLOT * FF_COLS, D)
    r_act, act_t, n2_t, dh2, loss_acc, d_gfin = _mlp_fwd(h1, norm_mlp_g, w_up_g, w_down_f,
                                                         norm_final_g.reshape(1, D), tgt)

    def pair(names, grads, recv):
        return [_pair_sum("pair_sum_" + nm, g, r, core) for nm, g, r in zip(names, grads, recv)]

    g_down, r_down = _wgrad_down(act_t, dh2, core, 2048)
    (p_down,) = pair(["down"], [g_down], [r_down])
    (df, dh1, d_gmlp), (got_down,) = _mlp_bwd(dh2, r_act, w_down_f, w_up_g, h1, norm_mlp_g,
                                              comm=_chips_plan([p_down]))
    g_up = _wgrad_t("wgrad_up", n2_t, df, N_SLOT, False, True, 4096)
    g_out = _wgrad("wgrad_out", merged, dh1, 1, False, False).reshape(N_SLOT, D // N_SLOT, D)
    (dz, dpa, dpb, dya, dyb), (r_up, r_out) = _merge_bwd(
        dh1, z, pa, pb, w_out_f, w_oa_f, w_ob_f, comm=_sibling_plan([g_up, g_out]))
    p_up, p_out = pair(["up", "out"], [g_up, g_out], [r_up, r_out])
    g_oa = _wgrad("wgrad_out_a", ya, dpa, 1, False, False).reshape(N_SLOT, D // N_SLOT, D)
    g_ob = _wgrad("wgrad_out_b", yb, dpb, 1, False, False).reshape(N_SLOT, D // N_SLOT, D)
    (dz, d_ws, d_bs, d_ln), (got_out, r_oa, r_ob) = _branch_b_bwd(
        dz, dyb, z, sgu_ln_g, sgu_ln_b, sgu_w_s[0], b_s_t,
        comm=_join(_chips_plan([p_out]), _sibling_plan([g_oa, g_ob])))
    p_oa, p_ob = pair(["out_a", "out_b"], [g_oa, g_ob], [r_oa, r_ob])
    (dz, d_vec, d_wr, d_wi), (got_up, got_oa, got_ob) = _branch_a_bwd(
        dz, dya, z, hs, xc, r_gate, i_gate, conv_w_f, w_r_f, w_i_f, lru_lambda,
        comm=_chips_plan([p_up, p_oa, p_ob]))
    g_gate = jnp.stack([d_wr, d_wi]).reshape(2, HEADS, N_SLOT, 32, HEAD_DIM).transpose(2, 0, 1, 3, 4)
    g_gate = g_gate.reshape(N_SLOT, 2 * HEADS * 32, HEAD_DIM).astype(BF16)

    d_bs_row = jnp.pad(d_bs[:, :, 0].reshape(1, GROUPS * CHUNK), ((0, 0), (0, D - GROUPS * CHUNK)))
    vecs = jnp.concatenate([d_vec, jnp.concatenate([d_ln[0:2], d_gmlp, d_gfin, d_bs_row, jnp.zeros((3, D), F32)])])
    d_ws2 = d_ws.reshape(GROUPS * CHUNK, CHUNK)
    (g_in, r_in), (r_gate, r_vecs, r_ws) = _wgrad_in(n1_t, dz, core, comm=_sibling_plan([g_gate], [vecs, d_ws2]))
    p_in, p_gate = pair(["in", "gate"], [g_in, g_gate], [r_in, r_gate])
    vecs_chip = _add2("pair_sum_vecs", vecs, r_vecs)
    ws_chip = _add2("pair_sum_ws", d_ws2, r_ws)
    (dx, d_gmix), (got_in, got_gate, got_vecs, got_ws) = _in_bwd(
        dz, w_in_g, xs, dh1, norm_mix_g, comm=_chips_plan([p_in, p_gate], [vecs_chip, ws_chip]))
    vecs_sum = _sum_terms("sum_vecs", got_vecs)
    last = jnp.concatenate([d_gmix, jnp.pad(loss_acc[0:1], ((0, 0), (0, D - 128))), jnp.zeros((6, D), F32)])
    (last_all,) = _run_plan("exchange_last", _exchange_plan(last))
    last_sum = _sum_terms("sum_last", last_all)
    loss = last_sum[1, 0]
    got = [got_in, got_oa, got_ob, got_out, got_up, got_down, got_gate]

    def step(nm, terms, w, m, v, rows, cols):
        g, d, mn, vn = _adamw("adamw_" + nm, terms.reshape(4, rows, cols), w.reshape(rows, cols),
                              m.reshape(rows, cols), v.reshape(rows, cols))
        return [a.reshape(w.shape) for a in (g, d, mn, vn)]

    o_in = step("in", got[0], w_in, m_w_in, v_w_in, D, W_IN_COLS)
    o_oa = step("out_a", got[1], w_out_a, m_w_out_a, v_w_out_a, D // N_SLOT, D)
    o_ob = step("out_b", got[2], w_out_b, m_w_out_b, v_w_out_b, D // N_SLOT, D)
    o_out = step("out", got[3], w_out, m_w_out, v_w_out, D // N_SLOT, D)
    o_up = step("up", got[4], w_up, m_w_up, v_w_up, D, FF_COLS)
    o_down = step("down", got[5], w_down, m_w_down, v_w_down, FF_COLS, D)
    gate_w = jnp.stack([w_rgate[0], w_igate[0]]).reshape(2 * HEADS * 32, HEAD_DIM)
    gate_m = jnp.stack([m_w_rgate[0], m_w_igate[0]]).reshape(2 * HEADS * 32, HEAD_DIM)
    gate_v = jnp.stack([v_w_rgate[0], v_w_igate[0]]).reshape(2 * HEADS * 32, HEAD_DIM)
    o_gate = _adamw("adamw_gate", got[6], gate_w, gate_m, gate_v)
    o_gate = [a.reshape(2, 1, HEADS, 32, HEAD_DIM) for a in o_gate]
    o_wr = [a[0] for a in o_gate]
    o_wi = [a[1] for a in o_gate]

    def own(full, width):
        return lax.dynamic_slice_in_dim(full, me * width, width, axis=1)

    small_g = {
        "norm_mix_g": last_sum[0:1], "conv_w": own(vecs_sum[0:4], 128), "conv_b": vecs_sum[4:5],
        "b_rgate": own(vecs_sum[5:6].reshape(HEADS, HEAD_DIM), 32),
        "b_igate": own(vecs_sum[6:7].reshape(HEADS, HEAD_DIM), 32),
        "lru_lambda": vecs_sum[7:8], "sgu_ln_g": vecs_sum[8:9], "sgu_ln_b": vecs_sum[9:10],
        "norm_mlp_g": vecs_sum[10:11], "norm_final_g": vecs_sum[11:12],
        "sgu_b_s": vecs_sum[12, 0:GROUPS * CHUNK].reshape(GROUPS, CHUNK),
    }
    small_w = {"norm_mix_g": (norm_mix_g, m_norm_mix_g, v_norm_mix_g), "conv_w": (conv_w, m_conv_w, v_conv_w),
               "conv_b": (conv_b, m_conv_b, v_conv_b), "b_rgate": (b_rgate, m_b_rgate, v_b_rgate),
               "b_igate": (b_igate, m_b_igate, v_b_igate), "lru_lambda": (lru_lambda, m_lru_lambda, v_lru_lambda),
               "sgu_ln_g": (sgu_ln_g, m_sgu_ln_g, v_sgu_ln_g), "sgu_ln_b": (sgu_ln_b, m_sgu_ln_b, v_sgu_ln_b),
               "norm_mlp_g": (norm_mlp_g, m_norm_mlp_g, v_norm_mlp_g),
               "norm_final_g": (norm_final_g, m_norm_final_g, v_norm_final_g),
               "sgu_b_s": (sgu_b_s, m_sgu_b_s, v_sgu_b_s), "sgu_w_s": (sgu_w_s, m_sgu_w_s, v_sgu_w_s)}
    order = list(small_g)
    as2d = lambda k, a: a.reshape(small_g[k].shape)
    upd = _adamw_many("adamw_small", [small_g[k] for k in order], *[[as2d(k, small_w[k][q]) for k in order]
                                                                     for q in range(3)])
    o_small = {k: [a.reshape(small_w[k][0].shape) for a in (small_g[k],) + u] for k, u in zip(order, upd)}
    ws3 = [a[0].reshape(GROUPS * CHUNK, CHUNK) for a in small_w.pop("sgu_w_s")]
    o_small["sgu_w_s"] = [a.reshape(sgu_w_s.shape) for a in _adamw("adamw_ws", got_ws, *ws3)]

    per_weight = {"norm_mix_g": o_small["norm_mix_g"], "w_in": o_in, "conv_w": o_small["conv_w"],
                  "conv_b": o_small["conv_b"], "w_rgate": o_wr, "b_rgate": o_small["b_rgate"], "w_igate": o_wi,
                  "b_igate": o_small["b_igate"], "lru_lambda": o_small["lru_lambda"], "w_out_a": o_oa,
                  "sgu_ln_g": o_small["sgu_ln_g"], "sgu_ln_b": o_small["sgu_ln_b"], "sgu_w_s": o_small["sgu_w_s"],
                  "sgu_b_s": o_small["sgu_b_s"], "w_out_b": o_ob, "w_out": o_out, "norm_mlp_g": o_small["norm_mlp_g"],
                  "w_up": o_up, "w_down": o_down, "norm_final_g": o_small["norm_final_g"]}
    names_w = list(per_weight)
    return (loss, dx[None], *[per_weight[k][0] for k in names_w], *[per_weight[k][1] for k in names_w],
            *[per_weight[k][2] for k in names_w], *[per_weight[k][3] for k in names_w])
```

```python
import jax
import jax.numpy as jnp
from jax import lax
from jax.experimental import pallas as pl
from jax.experimental.pallas import tpu as pltpu

F32 = jnp.float32
BF16 = jnp.bfloat16
SDS = jax.ShapeDtypeStruct
MESH = pl.DeviceIdType.MESH
ANY = pl.BlockSpec(memory_space=pl.ANY)

D = 1024
N_SLOT = 8
W_IN_COLS = 768
FF_COLS = 512
HEADS, HEAD_DIM = 4, 256
GROUPS, GROUP_DIM = 4, 256
CHUNK = 128
CONV_K = 4
NORM_EPS = 1e-6
LN_EPS = 1e-5
LRU_C = 8.0
ADAM_LR, ADAM_B1, ADAM_B2, ADAM_EPS, ADAM_WD, ADAM_STEP = 0.001, 0.9, 0.999, 1e-08, 0.01, 10

TM_ROWS = 1024
TM_MERGE = 512
T_BRANCH_A = 512
T_BRANCH_B = 256
MiB = 1024 * 1024
SMALL_OPERAND = 16 * 1024

_GELU_C = 0.7978845608028654
_GELU_A = 0.044715


def _small_in_hbm(a):
    return pltpu.with_memory_space_constraint(a, pltpu.HBM) if a.size <= SMALL_OPERAND else a


def _cparams(sem, vmem_mib):
    return pltpu.CompilerParams(dimension_semantics=sem, vmem_limit_bytes=vmem_mib * MiB)


def _gelu(x):
    t = jnp.tanh(_GELU_C * (x + _GELU_A * x * x * x))
    return 0.5 * x * (1.0 + t)


def _gelu_and_grad(x):
    x2 = x * x
    t = jnp.tanh(_GELU_C * x * (1.0 + _GELU_A * x2))
    g = 0.5 * x * (1.0 + t)
    dg = 0.5 * (1.0 + t) + 0.5 * x * (1.0 - t * t) * _GELU_C * (1.0 + 3.0 * _GELU_A * x2)
    return g, dg


def _softplus(x):
    return jnp.maximum(x, 0.0) + jnp.log1p(jnp.exp(-jnp.abs(x)))


def _dot(a, b):
    return jnp.dot(a, b, preferred_element_type=F32)


def _dot_nt(a, b):
    return lax.dot_general(a, b, (((1,), (1,)), ((), ())), preferred_element_type=F32)


def _dot_tn(a, b):
    return lax.dot_general(a, b, (((0,), (0,)), ((), ())), preferred_element_type=F32)


def _rows_shifted(prev8, cur, k):
    ext = jnp.concatenate([prev8, cur], axis=0)
    return pltpu.roll(ext, k, 0)[8:]


def _rows_advanced(cur, next8, k):
    t = cur.shape[0]
    ext = jnp.concatenate([cur, next8], axis=0)
    return pltpu.roll(ext, t + 8 - k, 0)[:t]


def _first_second(x, y, c):
    ny, nx, far = _other_chips(x, y)
    pick = lambda a, b: a * (1 - c) + b * c
    first = tuple(pick(a, b) for a, b in zip(ny, nx))
    second = tuple(pick(b, a) for a, b in zip(ny, nx))
    return first, second, far


def _slot_order(x, y, c):
    chip = 2 * x + y
    first, second, far = _first_second(x, y, c)
    order = [2 * chip + c, 2 * chip + 1 - c, 2 * first[2] + c, 2 * second[2] + 1 - c, 2 * second[2] + c,
             2 * first[2] + 1 - c, 2 * far[2] + c, 2 * far[2] + 1 - c]
    return jnp.stack(order).astype(jnp.int32)


def _in_proj(x, g_mix, w_in_own, order, comm=None):
    s = x.shape[0]
    tm = min(TM_ROWS, s)
    ni = s // tm

    def body(order_ref, x_ref, g_ref, own_ref, z_ref, nt_ref, wg_ref, n_s, w_s, send_sems, recv_sems, local_sems):
        j, i = pl.program_id(0), pl.program_id(1)
        px, py, c = _place()
        chip = 2 * px + py
        me = 2 * chip + c
        sib = (px, py, 1 - c)
        chips = _other_chips(px, py)

        def rc(k, src, blk, to):
            return pltpu.make_async_remote_copy(src_ref=src, dst_ref=w_s.at[blk], send_sem=send_sems.at[k],
                                                recv_sem=recv_sems.at[k], device_id=to, device_id_type=MESH)

        del chips
        first, second, far = _first_second(px, py, c)
        blocks = [2 * first[2] + c, 2 * second[2] + c, 2 * far[2] + c]
        own_in = pltpu.make_async_copy(own_ref, w_s.at[me], local_sems.at[0])
        to_first = rc(1, own_ref, me, (first[0], first[1], c))
        to_second = rc(2, own_ref, me, (second[0], second[1], c))
        relay = rc(3, w_s.at[blocks[0]], blocks[0], (second[0], second[1], c))
        sends = [rc(0, own_ref, me, sib), to_first, to_second, relay]
        passed = [rc(4 + q, w_s.at[blk], blk, sib) for q, blk in enumerate(blocks)]
        keep = pltpu.make_async_copy(w_s, wg_ref, local_sems.at[1])

        @pl.when((i == 0) & (j == 0))
        def _():
            own_in.start()
            sends[0].start()
            to_first.start()
            own_in.wait()

        @pl.when((i == 0) & (j == 1))
        def _():
            rc(0, own_ref, 2 * chip + 1 - c, sib).wait_recv()

        for q, blk in enumerate(blocks):
            @pl.when((i == 0) & (j == 2 + 2 * q))
            def _():
                rc(1 + q, own_ref, blk, sib).wait_recv()
                passed[q].start()
                if q == 0:
                    to_second.start()
                    relay.start()

            @pl.when((i == 0) & (j == 3 + 2 * q))
            def _():
                rc(4 + q, own_ref, order_ref[j], sib).wait_recv()

        rows = pl.ds(pl.multiple_of(i * tm, tm), tm)

        @pl.when(j == 0)
        def _():
            xv = x_ref[...]
            rstd = lax.rsqrt(jnp.mean(xv * xv, axis=-1, keepdims=True) + NORM_EPS)
            nb = (xv * rstd * g_ref[...]).astype(BF16)
            n_s[rows, :] = nb
            nt_ref[...] = nb.T

        z_ref[...] = _dot(n_s[rows, :], w_s[order_ref[j]]).astype(BF16)

        @pl.when((i == 0) & (j == N_SLOT - 1))
        def _():
            keep.start()

        @pl.when((i == ni - 1) & (j == N_SLOT - 1))
        def _():
            for cp in sends + passed:
                cp.wait_send()
            keep.wait()

    first_pass = lambda j, i, o: (jnp.where(j == 0, i, ni - 1), 0)
    (z, n1, w_in_g), extra = _call(
        body, name="in_proj", grid=(N_SLOT, ni), prefetch=(order,),
        in_specs=[pl.BlockSpec((tm, D), first_pass),
                  pl.BlockSpec((1, D), lambda j, i, o: (0, 0)), ANY],
        out_specs=[pl.BlockSpec((tm, W_IN_COLS), lambda j, i, o: (i, o[j])),
                   pl.BlockSpec((D, tm), lambda j, i, o: (0, jnp.where(j == 0, i, ni - 1))), ANY],
        out_shape=[SDS((s, N_SLOT * W_IN_COLS), BF16), SDS((D, s), BF16), SDS((N_SLOT, D, W_IN_COLS), BF16)],
        scratch_shapes=[pltpu.VMEM((s, D), BF16), pltpu.VMEM((N_SLOT, D, W_IN_COLS), BF16),
                        pltpu.SemaphoreType.DMA((7,)), pltpu.SemaphoreType.DMA((7,)), pltpu.SemaphoreType.DMA((2,))],
        params=_cparams(("arbitrary", "arbitrary"), 56), args=(x, g_mix, w_in_own), comm=comm)
    return (z, n1, w_in_g), extra


def _decay(r, sp_lam):
    log_a = (-LRU_C) * r * sp_lam
    a = jnp.exp(log_a)
    return a, jnp.sqrt(-jnp.tanh(log_a) * (a * a + 1.0))


def _lru_gates(xc, xcb, wr_ref, br, wi_ref, bi, sp_lam, a_s, b_s, r_ref, i_ref):
    for h in range(HEADS):
        sl = slice(h * HEAD_DIM, (h + 1) * HEAD_DIM)
        r = jax.nn.sigmoid(_dot(xcb[:, sl], wr_ref[h]) + br[:, sl])
        ig = jax.nn.sigmoid(_dot(xcb[:, sl], wi_ref[h]) + bi[:, sl])
        a, mult = _decay(r, sp_lam[:, sl])
        a_s[:, sl] = a
        b_s[:, sl] = xc[:, sl] * ig * mult
        r_ref[:, sl] = r.astype(BF16)
        i_ref[:, sl] = ig.astype(BF16)


def _conv_fwd(xa, prev8, cw, cb):
    xc = cb + cw[0:1, :] * xa
    for k in range(1, CONV_K):
        xc = xc + cw[k:k + 1, :] * _rows_shifted(prev8, xa, k)
    return xc


def _branch_a_fwd(z, conv_w, conv_b, w_r, b_r, w_i, b_i, lam, comm=None):
    s = z.shape[0]
    ta = min(T_BRANCH_A, s)
    per16 = ta // 16

    def body(xa_ref, xp_ref, ga_ref, cw_ref, cb_ref, wr_ref, br_ref, wi_ref, bi_ref, lam_ref,
             ya_ref, hs_ref, xc_ref, r_ref, i_ref, a_s, b_s, h_s, carry_s):
        i = pl.program_id(0)

        @pl.when(i == 0)
        def _():
            carry_s[...] = jnp.zeros_like(carry_s)

        xa = xa_ref[...].astype(F32)
        prev8 = jnp.where(i > 0, xp_ref[...].astype(F32)[8:16], 0.0)
        xc = _conv_fwd(xa, prev8, cw_ref[...], cb_ref[...])
        xcb = xc.astype(BF16)
        xc_ref[...] = xcb
        sp_lam = _softplus(-lam_ref[...])
        _lru_gates(xc, xcb, wr_ref, br_ref[...], wi_ref, bi_ref[...], sp_lam, a_s, b_s, r_ref, i_ref)

        row = lax.broadcasted_iota(jnp.int32, (8, D), 0)

        def group(g, carry):
            off = pl.multiple_of(g * 8, 8)
            a8 = a_s[pl.ds(off, 8), :]
            b8 = b_s[pl.ds(off, 8), :]
            for d in (1, 2, 4):
                a_sh = jnp.where(row >= d, pltpu.roll(a8, d, 0), 1.0)
                b_sh = jnp.where(row >= d, pltpu.roll(b8, d, 0), 0.0)
                b8 = a8 * b_sh + b8
                a8 = a8 * a_sh
            h8 = b8 + a8 * carry
            h_s[pl.ds(off, 8), :] = h8
            return jnp.broadcast_to(h8[7:8, :], (8, D))

        carry_s[...] = lax.fori_loop(0, ta // 8, group, carry_s[...])
        hs = h_s[...]
        hs_ref[...] = hs.astype(BF16)
        ya_ref[...] = (hs * _gelu(ga_ref[...].astype(F32))).astype(BF16)

    vec = pl.BlockSpec((1, D), lambda i: (0, 0))
    gate = pl.BlockSpec((HEADS, HEAD_DIM, HEAD_DIM), lambda i: (0, 0, 0))
    return _call(
        body, name="branch_a_fwd", grid=(s // ta,),
        in_specs=[pl.BlockSpec((ta, D), lambda i: (i, 0)),
                  pl.BlockSpec((16, D), lambda i: (jnp.maximum(i * per16 - 1, 0), 0)),
                  pl.BlockSpec((ta, D), lambda i: (i, 1)),
                  pl.BlockSpec((CONV_K, D), lambda i: (0, 0)), vec, gate, vec, gate, vec, vec],
        out_specs=[pl.BlockSpec((ta, D), lambda i: (i, 0))] * 5,
        out_shape=[SDS((s, D), BF16)] * 5,
        scratch_shapes=[pltpu.VMEM((ta, D), F32), pltpu.VMEM((ta, D), F32), pltpu.VMEM((ta, D), F32),
                        pltpu.VMEM((8, D), F32)],
        params=_cparams(("arbitrary",), 40), args=(z, z, z, conv_w, conv_b, w_r, b_r, w_i, b_i, lam), comm=comm)


def _sgu_common(ub, vb, lg, lb, with_grad):
    if with_grad:
        u, du = _gelu_and_grad(ub)
        v, dv = _gelu_and_grad(vb)
    else:
        u, v, du, dv = _gelu(ub), _gelu(vb), None, None
    mu = jnp.mean(v, axis=-1, keepdims=True)
    vc = v - mu
    rstd = lax.rsqrt(jnp.mean(vc * vc, axis=-1, keepdims=True) + LN_EPS)
    vhat = vc * rstd
    vln = vhat * lg + lb
    return u, du, dv, rstd, vhat, vln


def _masked_ws(ws_ref):
    t = lax.broadcasted_iota(jnp.int32, (CHUNK, CHUNK), 0)
    c = lax.broadcasted_iota(jnp.int32, (CHUNK, CHUNK), 1)
    keep = c <= t
    return [jnp.where(keep, ws_ref[g], 0.0).astype(BF16) for g in range(GROUPS)]


def _branch_b_fwd(z, ln_g, ln_b, w_s, b_s_t, comm=None):
    s = z.shape[0]
    tb = min(T_BRANCH_B, s)

    def body(ub_ref, vb_ref, lg_ref, lb_ref, ws_ref, bs_ref, yb_ref):
        u, _, _, _, _, vln = _sgu_common(ub_ref[...].astype(F32), vb_ref[...].astype(F32),
                                         lg_ref[...], lb_ref[...], False)
        vlnb = vln.astype(BF16)
        wm = _masked_ws(ws_ref)
        bs = bs_ref[...]
        for c in range(tb // CHUNK):
            rs = slice(c * CHUNK, (c + 1) * CHUNK)
            for g in range(GROUPS):
                cs = slice(g * GROUP_DIM, (g + 1) * GROUP_DIM)
                sp = _dot(wm[g], vlnb[rs, cs]) + bs[:, g:g + 1]
                yb_ref[rs, cs] = (u[rs, cs] * sp).astype(BF16)

    vec = pl.BlockSpec((1, D), lambda i: (0, 0))
    return _call(
        body, name="branch_b_fwd", grid=(s // tb,),
        in_specs=[pl.BlockSpec((tb, D), lambda i: (i, 2)), pl.BlockSpec((tb, D), lambda i: (i, 3)), vec, vec,
                  pl.BlockSpec((GROUPS, CHUNK, CHUNK), lambda i: (0, 0, 0)),
                  pl.BlockSpec((CHUNK, GROUPS), lambda i: (0, 0))],
        out_specs=[pl.BlockSpec((tb, D), lambda i: (i, 0))],
        out_shape=[SDS((s, D), BF16)], scratch_shapes=[],
        params=_cparams(("arbitrary",), 40), args=(z, z, ln_g, ln_b, w_s, b_s_t), comm=comm)


def _merge_out(ya, yb, z, x, w_oa, w_ob, w_out, comm=None):
    s = x.shape[0]
    tm = min(TM_MERGE, s)

    def body(ya_ref, yb_ref, ma_ref, mb_ref, x_ref, woa_ref, wob_ref, wo_ref, pa_ref, pb_ref, mg_ref, h1_ref):
        pa = _dot(ya_ref[...], woa_ref[...])
        pb = _dot(yb_ref[...], wob_ref[...])
        merged = (jax.nn.sigmoid(ma_ref[...].astype(F32)) * pa
                  + jax.nn.sigmoid(mb_ref[...].astype(F32)) * pb).astype(BF16)
        pa_ref[...] = pa.astype(BF16)
        pb_ref[...] = pb.astype(BF16)
        mg_ref[...] = merged
        h1_ref[...] = x_ref[...] + _dot(merged, wo_ref[...])

    row = pl.BlockSpec((tm, D), lambda i: (i, 0))
    wsp = pl.BlockSpec((D, D), lambda i: (0, 0))
    return _call(
        body, name="merge_out", grid=(s // tm,),
        in_specs=[row, row, pl.BlockSpec((tm, D), lambda i: (i, 4)), pl.BlockSpec((tm, D), lambda i: (i, 5)),
                  row, wsp, wsp, wsp],
        out_specs=[row, row, row, row],
        out_shape=[SDS((s, D), BF16), SDS((s, D), BF16), SDS((s, D), BF16), SDS((s, D), F32)], scratch_shapes=[],
        params=_cparams(("arbitrary",), 48), args=(ya, yb, z, z, x, w_oa, w_ob, w_out), comm=comm)


def _mlp_fwd(h1, g_mlp, w_up_g, w_down, g_fin, tgt):
    s = h1.shape[0]
    tm = min(TM_ROWS, s)
    nj = N_SLOT

    def body(h1_ref, gm_ref, wu_ref, wd_ref, gf_ref, t_ref, r_ref, at_ref, n2t_ref, dh2_ref, loss_ref, dgf_ref,
             n2_s, acc_s):
        i, j = pl.program_id(0), pl.program_id(1)

        @pl.when(j == 0)
        def _():
            hv = h1_ref[...]
            rstd = lax.rsqrt(jnp.mean(hv * hv, axis=-1, keepdims=True) + NORM_EPS)
            nb = (hv * rstd * gm_ref[...]).astype(BF16)
            n2_s[...] = nb
            n2t_ref[...] = nb.T
            acc_s[...] = jnp.zeros_like(acc_s)

        @pl.when((i == 0) & (j == 0))
        def _():
            loss_ref[...] = jnp.zeros_like(loss_ref)
            dgf_ref[...] = jnp.zeros_like(dgf_ref)

        r = jnp.maximum(_dot(n2_s[...], wu_ref[...]), 0.0)
        r_ref[...] = r.astype(BF16)
        act = (r * r).astype(BF16)
        at_ref[...] = act.T
        acc_s[...] += _dot(act, wd_ref[...])

        @pl.when(j == nj - 1)
        def _():
            h2 = h1_ref[...] + acc_s[...]
            rstd = lax.rsqrt(jnp.mean(h2 * h2, axis=-1, keepdims=True) + NORM_EPS)
            hh = h2 * rstd
            gf = gf_ref[...]
            e = hh * gf - t_ref[...]
            loss_ref[...] += jnp.sum(e * e) * (0.5 / D)
            dy = e * (1.0 / D)
            dgf_ref[...] += jnp.sum(dy * hh, axis=0, keepdims=True)
            dhh = dy * gf
            dh2_ref[...] = rstd * (dhh - hh * jnp.mean(dhh * hh, axis=-1, keepdims=True))

    row = pl.BlockSpec((tm, D), lambda i, j: (i, 0))
    vec = pl.BlockSpec((1, D), lambda i, j: (0, 0))
    return pl.pallas_call(
        body, name="mlp_fwd", grid=(s // tm, nj),
        in_specs=[row, vec, pl.BlockSpec((None, D, FF_COLS), lambda i, j: (j, 0, 0)),
                  pl.BlockSpec((FF_COLS, D), lambda i, j: (j, 0)), vec, row],
        out_specs=[pl.BlockSpec((tm, FF_COLS), lambda i, j: (i, j)), pl.BlockSpec((FF_COLS, tm), lambda i, j: (j, i)),
                   pl.BlockSpec((D, tm), lambda i, j: (0, i)), row, pl.BlockSpec((8, 128), lambda i, j: (0, 0)), vec],
        out_shape=[SDS((s, nj * FF_COLS), BF16), SDS((nj * FF_COLS, s), BF16), SDS((D, s), BF16), SDS((s, D), F32),
                   SDS((8, 128), F32), SDS((1, D), F32)],
        scratch_shapes=[pltpu.VMEM((tm, D), BF16), pltpu.VMEM((tm, D), F32)],
        compiler_params=_cparams(("arbitrary", "arbitrary"), 52),
    )(h1, _small_in_hbm(g_mlp), w_up_g, w_down, _small_in_hbm(g_fin), tgt)


def _mlp_bwd(dh2, r, w_down, w_up_g, h1, g_mlp, comm=None):
    s = h1.shape[0]
    tm = min(TM_ROWS, s)
    nj = N_SLOT

    def body(dh2_ref, r_ref, wd_ref, wu_ref, h1_ref, gm_ref, df_ref, dh1_ref, dgm_ref, dh2b_s, acc_s):
        i, j = pl.program_id(0), pl.program_id(1)

        @pl.when(j == 0)
        def _():
            dh2b_s[...] = dh2_ref[...].astype(BF16)
            acc_s[...] = jnp.zeros_like(acc_s)

        @pl.when((i == 0) & (j == 0))
        def _():
            dgm_ref[...] = jnp.zeros_like(dgm_ref)

        d_act = _dot_nt(dh2b_s[...], wd_ref[...])
        df = (d_act * (2.0 * r_ref[...].astype(F32))).astype(BF16)
        df_ref[...] = df
        acc_s[...] += _dot_nt(df, wu_ref[...])

        @pl.when(j == nj - 1)
        def _():
            hv = h1_ref[...]
            rstd = lax.rsqrt(jnp.mean(hv * hv, axis=-1, keepdims=True) + NORM_EPS)
            hh = hv * rstd
            dn2 = acc_s[...]
            dgm_ref[...] += jnp.sum(dn2 * hh, axis=0, keepdims=True)
            dhat = dn2 * gm_ref[...]
            dh1_ref[...] = dh2_ref[...] + rstd * (dhat - hh * jnp.mean(dhat * hh, axis=-1, keepdims=True))

    row = pl.BlockSpec((tm, D), lambda i, j: (i, 0))
    vec = pl.BlockSpec((1, D), lambda i, j: (0, 0))
    ffb = pl.BlockSpec((tm, FF_COLS), lambda i, j: (i, j))
    return _call(
        body, name="mlp_bwd", grid=(s // tm, nj),
        in_specs=[row, ffb, pl.BlockSpec((FF_COLS, D), lambda i, j: (j, 0)),
                  pl.BlockSpec((None, D, FF_COLS), lambda i, j: (j, 0, 0)), row, vec],
        out_specs=[ffb, row, vec],
        out_shape=[SDS((s, nj * FF_COLS), BF16), SDS((s, D), F32), SDS((1, D), F32)],
        scratch_shapes=[pltpu.VMEM((tm, D), BF16), pltpu.VMEM((tm, D), F32)],
        params=_cparams(("arbitrary", "arbitrary"), 52), args=(dh2, r, w_down, w_up_g, h1, g_mlp), comm=comm)


def _merge_bwd(dh1, z, pa, pb, w_out, w_oa, w_ob, comm=None):
    s = dh1.shape[0]
    tm = min(TM_MERGE, s)

    def body(dh1_ref, ma_ref, mb_ref, pa_ref, pb_ref, wo_ref, woa_ref, wob_ref,
             dz_ref, dpa_ref, dpb_ref, dya_ref, dyb_ref):
        dm = _dot_nt(dh1_ref[...].astype(BF16), wo_ref[...])
        sa = jax.nn.sigmoid(ma_ref[...].astype(F32))
        sb = jax.nn.sigmoid(mb_ref[...].astype(F32))
        dpa = (dm * sa).astype(BF16)
        dpb = (dm * sb).astype(BF16)
        dz_ref[:, 0:D] = (dm * pa_ref[...].astype(F32) * sa * (1.0 - sa)).astype(BF16)
        dz_ref[:, D:2 * D] = (dm * pb_ref[...].astype(F32) * sb * (1.0 - sb)).astype(BF16)
        dpa_ref[...] = dpa
        dpb_ref[...] = dpb
        dya_ref[...] = _dot_nt(dpa, woa_ref[...]).astype(BF16)
        dyb_ref[...] = _dot_nt(dpb, wob_ref[...]).astype(BF16)

    row = pl.BlockSpec((tm, D), lambda i: (i, 0))
    wsp = pl.BlockSpec((D, D), lambda i: (0, 0))
    return _call(
        body, name="merge_bwd", grid=(s // tm,),
        in_specs=[row, pl.BlockSpec((tm, D), lambda i: (i, 4)), pl.BlockSpec((tm, D), lambda i: (i, 5)),
                  row, row, wsp, wsp, wsp],
        out_specs=[pl.BlockSpec((tm, 2 * D), lambda i: (i, 2)), row, row, row, row],
        out_shape=[SDS((s, 6 * D), BF16)] + [SDS((s, D), BF16)] * 4, scratch_shapes=[],
        params=_cparams(("arbitrary",), 48), args=(dh1, z, z, pa, pb, w_out, w_oa, w_ob), comm=comm)


def _branch_b_bwd(dz, dyb, z, ln_g, ln_b, w_s, b_s_t, comm=None):
    s = z.shape[0]
    tb = min(T_BRANCH_B, s)

    def body(dz_in, dyb_ref, ub_ref, vb_ref, lg_ref, lb_ref, ws_ref, bs_ref,
             dz_ref, dws_ref, dbs_ref, dln_ref, du_s, dvln_s):
        del dz_in

        @pl.when(pl.program_id(0) == 0)
        def _():
            dws_ref[...] = jnp.zeros_like(dws_ref)
            dbs_ref[...] = jnp.zeros_like(dbs_ref)
            dln_ref[...] = jnp.zeros_like(dln_ref)

        lg = lg_ref[...]
        u, du, dv, rstd, vhat, vln = _sgu_common(ub_ref[...].astype(F32), vb_ref[...].astype(F32),
                                                 lg, lb_ref[...], True)
        vlnb = vln.astype(BF16)
        dyb_v = dyb_ref[...].astype(F32)
        wm = _masked_ws(ws_ref)
        keep = (lax.broadcasted_iota(jnp.int32, (CHUNK, CHUNK), 1)
                <= lax.broadcasted_iota(jnp.int32, (CHUNK, CHUNK), 0))
        bs = bs_ref[...]
        for c in range(tb // CHUNK):
            rs = slice(c * CHUNK, (c + 1) * CHUNK)
            for g in range(GROUPS):
                cs = slice(g * GROUP_DIM, (g + 1) * GROUP_DIM)
                v_blk = vlnb[rs, cs]
                sp = _dot(wm[g], v_blk) + bs[:, g:g + 1]
                d_sp = dyb_v[rs, cs] * u[rs, cs]
                d_spb = d_sp.astype(BF16)
                du_s[rs, cs] = dyb_v[rs, cs] * sp
                dvln_s[rs, cs] = _dot_tn(wm[g], d_spb)
                dws_ref[g] += jnp.where(keep, _dot_nt(d_spb, v_blk), 0.0)
                dbs_ref[g] += jnp.broadcast_to(jnp.sum(d_sp, axis=-1, keepdims=True), (CHUNK, CHUNK))
        dvln = dvln_s[...]
        dln_ref[0:1, :] += jnp.sum(dvln * vhat, axis=0, keepdims=True)
        dln_ref[1:2, :] += jnp.sum(dvln, axis=0, keepdims=True)
        dvh = dvln * lg
        d_v = rstd * (dvh - jnp.mean(dvh, axis=-1, keepdims=True)
                      - vhat * jnp.mean(dvh * vhat, axis=-1, keepdims=True))
        dz_ref[:, 0:D] = (du_s[...] * du).astype(BF16)
        dz_ref[:, D:2 * D] = (d_v * dv).astype(BF16)

    vec = pl.BlockSpec((1, D), lambda i: (0, 0))
    sq = pl.BlockSpec((GROUPS, CHUNK, CHUNK), lambda i: (0, 0, 0))
    return _call(
        body, name="branch_b_bwd", grid=(s // tb,),
        in_specs=[ANY, pl.BlockSpec((tb, D), lambda i: (i, 0)),
                  pl.BlockSpec((tb, D), lambda i: (i, 2)), pl.BlockSpec((tb, D), lambda i: (i, 3)), vec, vec, sq,
                  pl.BlockSpec((CHUNK, GROUPS), lambda i: (0, 0))],
        out_specs=[pl.BlockSpec((tb, 2 * D), lambda i: (i, 1)), sq, sq, pl.BlockSpec((8, D), lambda i: (0, 0))],
        out_shape=[SDS(dz.shape, BF16), SDS((GROUPS, CHUNK, CHUNK), F32), SDS((GROUPS, CHUNK, CHUNK), F32),
                   SDS((8, D), F32)],
        scratch_shapes=[pltpu.VMEM((tb, D), F32), pltpu.VMEM((tb, D), F32)], aliases={0: 0},
        params=_cparams(("arbitrary",), 40), args=(dz, dyb, z, z, ln_g, ln_b, w_s, b_s_t), comm=comm)


def _branch_a_bwd(dz, dya, z, hs, xc, r, ig, conv_w, w_r, w_i, lam, comm=None):
    s = z.shape[0]
    ta = min(T_BRANCH_A, s)
    nb = s // ta
    per16 = ta // 16

    def body(dz_in, dya_ref, xa_ref, ga_ref, hs_ref, hp_ref, xc_ref, r_ref, i_ref, cw_ref, wr_ref, wi_ref,
             lam_ref, dz_ref, vec_ref, dwr_ref, dwi_ref, a_s, b_s, h_s, dcar_s, acar_s, dxc_s):
        del dz_in
        i = pl.program_id(0)
        blk = nb - 1 - i

        @pl.when(i == 0)
        def _():
            dcar_s[...] = jnp.zeros_like(dcar_s)
            acar_s[...] = jnp.zeros_like(acar_s)
            dxc_s[...] = jnp.zeros_like(dxc_s)
            vec_ref[...] = jnp.zeros_like(vec_ref)
            dwr_ref[...] = jnp.zeros_like(dwr_ref)
            dwi_ref[...] = jnp.zeros_like(dwi_ref)

        cw = cw_ref[...]
        lam_v = lam_ref[...]
        xa = xa_ref[...].astype(F32)
        xcb = xc_ref[...]
        xc = xcb.astype(F32)
        sp_lam = _softplus(-lam_v)
        r_v, i_v = r_ref[...].astype(F32), i_ref[...].astype(F32)
        a_v, m_v = _decay(r_v, sp_lam)

        hs_v = hs_ref[...].astype(F32)
        hprev8 = jnp.where(blk > 0, hp_ref[...].astype(F32)[8:16], 0.0)
        h_m1 = _rows_shifted(hprev8, hs_v, 1)
        gg, dgg = _gelu_and_grad(ga_ref[...].astype(F32))
        dya_v = dya_ref[...].astype(F32)
        dz_ref[:, D:2 * D] = (dya_v * hs_v * dgg).astype(BF16)

        a_s[...] = _rows_advanced(a_v, acar_s[...], 1)
        b_s[...] = dya_v * gg

        row = lax.broadcasted_iota(jnp.int32, (8, D), 0)
        ng = ta // 8

        def group(gi, carry):
            off = pl.multiple_of((ng - 1 - gi) * 8, 8)
            c8 = a_s[pl.ds(off, 8), :]
            d8 = b_s[pl.ds(off, 8), :]
            for d in (1, 2, 4):
                c_sh = jnp.where(row < 8 - d, pltpu.roll(c8, 8 - d, 0), 1.0)
                d_sh = jnp.where(row < 8 - d, pltpu.roll(d8, 8 - d, 0), 0.0)
                d8 = c8 * d_sh + d8
                c8 = c8 * c_sh
            dh8 = d8 + c8 * carry
            h_s[pl.ds(off, 8), :] = dh8
            return jnp.broadcast_to(dh8[0:1, :], (8, D))

        dcar_s[...] = lax.fori_loop(0, ng, group, dcar_s[...])
        acar_s[...] = jnp.broadcast_to(a_v[0:1, :], (8, D))

        dbx = h_s[...]
        d_mult = dbx * xc * i_v
        d_loga = dbx * h_m1 * a_v - d_mult * (a_v * a_v) / m_v
        d_pr = d_loga * ((-LRU_C) * sp_lam) * r_v * (1.0 - r_v)
        d_pi = dbx * xc * m_v * i_v * (1.0 - i_v)
        vec_ref[7:8, :] += jnp.sum(d_loga * r_v, axis=0, keepdims=True) * (LRU_C * jax.nn.sigmoid(-lam_v))
        vec_ref[5:6, :] += jnp.sum(d_pr, axis=0, keepdims=True)
        vec_ref[6:7, :] += jnp.sum(d_pi, axis=0, keepdims=True)
        d_prb = d_pr.astype(BF16)
        d_pib = d_pi.astype(BF16)
        h_s[...] = dbx * i_v * m_v
        for h in range(HEADS):
            sl = slice(h * HEAD_DIM, (h + 1) * HEAD_DIM)
            h_s[:, sl] += _dot_nt(d_prb[:, sl], wr_ref[h]) + _dot_nt(d_pib[:, sl], wi_ref[h])
            dwr_ref[h] += _dot_tn(xcb[:, sl], d_prb[:, sl])
            dwi_ref[h] += _dot_tn(xcb[:, sl], d_pib[:, sl])
        d_xc = h_s[...]
        vec_ref[4:5, :] += jnp.sum(d_xc, axis=0, keepdims=True)
        vec_ref[0:1, :] += jnp.sum(d_xc * xa, axis=0, keepdims=True)
        d_xa = cw[0:1, :] * d_xc
        nxt = dxc_s[...]
        for k in range(1, CONV_K):
            ahead = _rows_advanced(d_xc, nxt, k)
            vec_ref[k:k + 1, :] += jnp.sum(ahead * xa, axis=0, keepdims=True)
            d_xa = d_xa + cw[k:k + 1, :] * ahead
        dz_ref[:, 0:D] = d_xa.astype(BF16)
        dxc_s[...] = d_xc[0:8, :]

    vec = pl.BlockSpec((1, D), lambda i: (0, 0))
    gate = pl.BlockSpec((HEADS, HEAD_DIM, HEAD_DIM), lambda i: (0, 0, 0))
    cur = lambda c: pl.BlockSpec((ta, D), lambda i: (nb - 1 - i, c))
    before = lambda c: pl.BlockSpec((16, D), lambda i: (jnp.maximum((nb - 1 - i) * per16 - 1, 0), c))
    return _call(
        body, name="branch_a_bwd", grid=(nb,),
        in_specs=[ANY, cur(0), cur(0), cur(1), cur(0), before(0), cur(0), cur(0), cur(0),
                  pl.BlockSpec((CONV_K, D), lambda i: (0, 0)), gate, gate, vec],
        out_specs=[pl.BlockSpec((ta, 2 * D), lambda i: (nb - 1 - i, 0)), pl.BlockSpec((8, D), lambda i: (0, 0)),
                   gate, gate],
        out_shape=[SDS(dz.shape, BF16), SDS((8, D), F32), SDS((HEADS, HEAD_DIM, HEAD_DIM), F32),
                   SDS((HEADS, HEAD_DIM, HEAD_DIM), F32)],
        scratch_shapes=[pltpu.VMEM((ta, D), F32)] * 3 + [pltpu.VMEM((8, D), F32)] * 3, aliases={0: 0},
        params=_cparams(("arbitrary",), 48),
        args=(dz, dya, z, z, hs, hs, xc, r, ig, conv_w, w_r, w_i, lam), comm=comm)


def _in_bwd(dz, w_in_g, x, dh1, g_mix, comm=None):
    s = x.shape[0]
    tm = min(TM_ROWS, s)
    nj = N_SLOT

    def body(dz_ref, w_ref, x_ref, dh1_ref, g_ref, dx_ref, dg_ref, acc_s):
        i, j = pl.program_id(0), pl.program_id(1)

        @pl.when(j == 0)
        def _():
            acc_s[...] = jnp.zeros_like(acc_s)

        @pl.when((i == 0) & (j == 0))
        def _():
            dg_ref[...] = jnp.zeros_like(dg_ref)

        acc_s[...] += _dot_nt(dz_ref[...], w_ref[...])

        @pl.when(j == nj - 1)
        def _():
            xv = x_ref[...]
            rstd = lax.rsqrt(jnp.mean(xv * xv, axis=-1, keepdims=True) + NORM_EPS)
            xh = xv * rstd
            dn = acc_s[...]
            dg_ref[...] += jnp.sum(dn * xh, axis=0, keepdims=True)
            dhat = dn * g_ref[...]
            dx_ref[...] = dh1_ref[...] + rstd * (dhat - xh * jnp.mean(dhat * xh, axis=-1, keepdims=True))

    row = pl.BlockSpec((tm, D), lambda i, j: (i, 0))
    vec = pl.BlockSpec((1, D), lambda i, j: (0, 0))
    return _call(
        body, name="in_bwd", grid=(s // tm, nj),
        in_specs=[pl.BlockSpec((tm, W_IN_COLS), lambda i, j: (i, j)),
                  pl.BlockSpec((None, D, W_IN_COLS), lambda i, j: (j, 0, 0)), row, row, vec],
        out_specs=[row, vec],
        out_shape=[SDS((s, D), F32), SDS((1, D), F32)],
        scratch_shapes=[pltpu.VMEM((tm, D), F32)],
        params=_cparams(("arbitrary", "arbitrary"), 48), args=(dz, w_in_g, x, dh1, g_mix), comm=comm)


def _wgrad(name, a, b, nblk, a_split, b_split):
    s = a.shape[0]
    ts = min(TM_ROWS, s)
    a_w = a.shape[1] // nblk if a_split else a.shape[1]
    b_w = b.shape[1] // nblk if b_split else b.shape[1]

    def body(a_ref, b_ref, o_ref, acc_s):
        t = pl.program_id(1)

        @pl.when(t == 0)
        def _():
            acc_s[...] = jnp.zeros_like(acc_s)

        acc_s[...] += _dot_tn(a_ref[...].astype(BF16), b_ref[...].astype(BF16))

        @pl.when(t == pl.num_programs(1) - 1)
        def _():
            o_ref[...] = acc_s[...].astype(BF16)

    return pl.pallas_call(
        body, name=name, grid=(nblk, s // ts),
        in_specs=[pl.BlockSpec((ts, a_w), (lambda k, t: (t, k)) if a_split else (lambda k, t: (t, 0))),
                  pl.BlockSpec((ts, b_w), (lambda k, t: (t, k)) if b_split else (lambda k, t: (t, 0)))],
        out_specs=pl.BlockSpec((None, a_w, b_w), lambda k, t: (k, 0, 0)),
        out_shape=SDS((nblk, a_w, b_w), BF16),
        scratch_shapes=[pltpu.VMEM((a_w, b_w), F32)],
        compiler_params=_cparams(("arbitrary", "arbitrary"), 48),
    )(a, b)


def _wgrad_t(name, a_t, b, nblk, a_split, b_split, tokens):
    s = b.shape[0]
    ts = min(tokens, s)
    a_w = a_t.shape[0] // nblk if a_split else a_t.shape[0]
    b_w = b.shape[1] // nblk if b_split else b.shape[1]

    def body(a_ref, b_ref, o_ref, acc_s):
        t = pl.program_id(1)

        @pl.when(t == 0)
        def _():
            acc_s[...] = jnp.zeros_like(acc_s)

        acc_s[...] += _dot(a_ref[...], b_ref[...].astype(BF16))

        @pl.when(t == pl.num_programs(1) - 1)
        def _():
            o_ref[...] = acc_s[...].astype(BF16)

    return pl.pallas_call(
        body, name=name, grid=(nblk, s // ts),
        in_specs=[pl.BlockSpec((a_w, ts), (lambda k, t: (k, t)) if a_split else (lambda k, t: (0, t))),
                  pl.BlockSpec((ts, b_w), (lambda k, t: (t, k)) if b_split else (lambda k, t: (t, 0)))],
        out_specs=pl.BlockSpec((None, a_w, b_w), lambda k, t: (k, 0, 0)),
        out_shape=SDS((nblk, a_w, b_w), BF16),
        scratch_shapes=[pltpu.VMEM((a_w, b_w), F32)],
        compiler_params=_cparams(("arbitrary", "arbitrary"), 48),
    )(a_t, b)


def _place():
    x, y, c = lax.axis_index("x"), lax.axis_index("y"), lax.axis_index("c")
    return x, y, c


def _other_chips(x, y):
    return [(x, 1 - y, 2 * x + 1 - y), (1 - x, y, 2 * (1 - x) + y), (1 - x, 1 - y, 2 * (1 - x) + 1 - y)]


class _Plan:
    def __init__(self, arrays, out_shape, sems, start, finish, middle=None, middle_at=6):
        self.arrays, self.out_shape, self.sems, self.start, self.finish = arrays, out_shape, sems, start, finish
        self.middle, self.middle_at = middle, middle_at


def _gather_plan(shards, middle_at=6):
    n = len(shards)

    def copies(ins, outs, sems):
        send_sems, recv_sems, local_sems = sems
        x, y, c = _place()
        chip = 2 * x + y
        me = 2 * chip + c
        sib = (x, y, 1 - c)
        chips = _other_chips(x, y)

        def rc(k, t, src, blk, to):
            return pltpu.make_async_remote_copy(
                src_ref=src, dst_ref=outs[t].at[blk], send_sem=send_sems.at[k * n + t],
                recv_sem=recv_sems.at[k * n + t], device_id=to, device_id_type=MESH)

        (yx, yy, y_chip), (xx, xy, x_chip), _ = chips
        local = [pltpu.make_async_copy(ins[t], outs[t].at[me], local_sems.at[t]) for t in range(n)]
        sends = ([rc(0, t, ins[t], me, sib) for t in range(n)] + [rc(1, t, ins[t], me, (yx, yy, c)) for t in range(n)]
                 + [rc(2, t, ins[t], me, (xx, xy, c)) for t in range(n)])
        passed = [[rc(4 + j, t, outs[t].at[2 * pc + c], 2 * pc + c, sib) for t in range(n)]
                  for j, (_, _, pc) in enumerate(chips)]
        relays = [[rc(3, t, outs[t].at[2 * y_chip + c], 2 * y_chip + c, (xx, xy, c)) for t in range(n)],
                  [rc(3, t, outs[t].at[2 * x_chip + c], 2 * x_chip + c, (yx, yy, c)) for t in range(n)]]
        return rc, local, sends, passed, relays, chips, chip, c, sib

    def start(ins, outs, sems):
        _, local, sends, _, _, _, _, _, _ = copies(ins, outs, sems)
        for cp in local + sends:
            cp.start()

    def middle(ins, outs, sems):
        rc, _, _, passed, relays, chips, _, c, sib = copies(ins, outs, sems)
        for j in range(2):
            for t in range(n):
                rc(1 + j, t, ins[t], 2 * chips[j][2] + c, sib).wait_recv()
        for j in range(2):
            for cp in passed[j]:
                cp.start()

            @pl.when(c == j)
            def _():
                for cp in relays[j]:
                    cp.start()

    def finish(ins, outs, sems):
        rc, local, sends, passed, relays, chips, chip, c, sib = copies(ins, outs, sems)
        far = 2 * chips[2][2] + c
        for t in range(n):
            rc(3, t, ins[t], far, sib).wait_recv()
        for cp in passed[2]:
            cp.start()
        for t in range(n):
            rc(0, t, ins[t], 2 * chip + 1 - c, sib).wait_recv()
        for j, (px, py, pc) in enumerate(chips):
            for t in range(n):
                rc(4 + j, t, ins[t], 2 * pc + 1 - c, sib).wait_recv()
        for cp in sends + passed[0] + passed[1] + passed[2]:
            cp.wait_send()
        for j in range(2):
            @pl.when(c == j)
            def _():
                for cp in relays[j]:
                    cp.wait_send()
        for cp in local:
            cp.wait()

    return _Plan(list(shards), [SDS((N_SLOT,) + tuple(a.shape), a.dtype) for a in shards],
                 [pltpu.SemaphoreType.DMA((7 * n,)), pltpu.SemaphoreType.DMA((7 * n,)),
                  pltpu.SemaphoreType.DMA((n,))], start, finish, middle, middle_at)


def _sibling_plan(grads, whole=()):
    n, m = len(grads), len(whole)

    def copies(ins, outs, sems):
        send_sems, recv_sems = sems
        x, y, c = _place()
        sib = (x, y, 1 - c)

        def rc(t, src, dst):
            return pltpu.make_async_remote_copy(src_ref=src, dst_ref=dst, send_sem=send_sems.at[t],
                                                recv_sem=recv_sems.at[t], device_id=sib, device_id_type=MESH)
        return rc, c

    def start(ins, outs, sems):
        rc, c = copies(ins, outs, sems)
        for t in range(n):
            for j in range(4):
                rc(t, ins[t].at[2 * j + 1 - c], outs[t].at[j]).start()
        for t in range(n, n + m):
            rc(t, ins[t], outs[t]).start()

    def finish(ins, outs, sems):
        rc, _ = copies(ins, outs, sems)
        for t in range(n):
            rc(t, ins[t].at[pl.ds(0, 4)], outs[t]).wait()
        for t in range(n, n + m):
            rc(t, ins[t], outs[t]).wait()

    return _Plan(list(grads) + list(whole),
                 [SDS((4,) + tuple(g.shape[1:]), g.dtype) for g in grads] + [SDS(a.shape, a.dtype) for a in whole],
                 [pltpu.SemaphoreType.DMA((n + m,)), pltpu.SemaphoreType.DMA((n + m,))], start, finish)


def _chips_plan(parts, whole=()):
    n, m = len(parts), len(whole)

    def src_of(ins, t, pc):
        return ins[t].at[pc] if t < n else ins[t]

    def local_copies(ins, outs, sems, chip):
        return [pltpu.make_async_copy(src_of(ins, t, chip), outs[t].at[chip], sems[2].at[t]) for t in range(n + m)]

    def start(ins, outs, sems):
        send_sems, recv_sems, _ = sems
        x, y, c = _place()
        chip = 2 * x + y
        for cp in local_copies(ins, outs, sems, chip):
            cp.start()
        for px, py, pc in _other_chips(x, y):
            for t in range(n + m):
                pltpu.make_async_remote_copy(src_ref=src_of(ins, t, pc), dst_ref=outs[t].at[chip],
                                             send_sem=send_sems.at[t], recv_sem=recv_sems.at[t],
                                             device_id=(px, py, c), device_id_type=MESH).start()

    def finish(ins, outs, sems):
        send_sems, recv_sems, _ = sems
        x, y, c = _place()
        for t in range(n + m):
            three = outs[t].at[pl.ds(0, 3)]
            pltpu.make_async_remote_copy(src_ref=three, dst_ref=three, send_sem=send_sems.at[t],
                                         recv_sem=recv_sems.at[t], device_id=(x, y, c), device_id_type=MESH).wait()
        for cp in local_copies(ins, outs, sems, 2 * x + y):
            cp.wait()

    return _Plan(list(parts) + list(whole),
                 [SDS(p.shape, p.dtype) for p in parts] + [SDS((4,) + tuple(a.shape), a.dtype) for a in whole],
                 [pltpu.SemaphoreType.DMA((n + m,)), pltpu.SemaphoreType.DMA((n + m,)),
                  pltpu.SemaphoreType.DMA((n + m,))], start, finish)


def _exchange_plan(arr):
    def peers(x, y, c):
        flip = lambda v, f: 1 - v if f else v
        return [(flip(x, fx), flip(y, fy), flip(c, fc))
                for fx in (0, 1) for fy in (0, 1) for fc in (0, 1) if fx or fy or fc]

    def start(ins, outs, sems):
        x, y, c = _place()
        me = 4 * x + 2 * y + c
        pltpu.make_async_copy(ins[0], outs[0].at[me], sems[2].at[0]).start()
        for to in peers(x, y, c):
            pltpu.make_async_remote_copy(src_ref=ins[0], dst_ref=outs[0].at[me], send_sem=sems[0].at[0],
                                         recv_sem=sems[1].at[0], device_id=to, device_id_type=MESH).start()

    def finish(ins, outs, sems):
        x, y, c = _place()
        seven = outs[0].at[pl.ds(0, 7)]
        pltpu.make_async_remote_copy(src_ref=seven, dst_ref=seven, send_sem=sems[0].at[0], recv_sem=sems[1].at[0],
                                     device_id=(x, y, c), device_id_type=MESH).wait()
        pltpu.make_async_copy(ins[0], outs[0].at[4 * x + 2 * y + c], sems[2].at[0]).wait()

    return _Plan([arr], [SDS((N_SLOT,) + tuple(arr.shape), arr.dtype)],
                 [pltpu.SemaphoreType.DMA((1,)), pltpu.SemaphoreType.DMA((1,)), pltpu.SemaphoreType.DMA((1,))],
                 start, finish)


def _join(*plans):
    def cut(seq, sizes):
        out, at = [], 0
        for k in sizes:
            out.append(seq[at:at + k])
            at += k
        return out

    n_arr = [len(p.arrays) for p in plans]
    n_sem = [len(p.sems) for p in plans]

    def start(ins, outs, sems):
        for p, i, o, s in zip(plans, cut(ins, n_arr), cut(outs, n_arr), cut(sems, n_sem)):
            p.start(i, o, s)

    def finish(ins, outs, sems):
        for p, i, o, s in zip(plans, cut(ins, n_arr), cut(outs, n_arr), cut(sems, n_sem)):
            p.finish(i, o, s)

    def middle(ins, outs, sems):
        for p, i, o, s in zip(plans, cut(ins, n_arr), cut(outs, n_arr), cut(sems, n_sem)):
            if p.middle is not None:
                p.middle(i, o, s)

    return _Plan([a for p in plans for a in p.arrays], [o for p in plans for o in p.out_shape],
                 [s for p in plans for s in p.sems], start, finish,
                 middle if any(p.middle is not None for p in plans) else None,
                 max(p.middle_at for p in plans))


def _run_plan(name, plan):
    k = len(plan.arrays)

    def body(*refs):
        ins, outs, sems = refs[:k], refs[k:2 * k], refs[2 * k:]
        plan.start(ins, outs, sems)
        if plan.middle is not None:
            plan.middle(ins, outs, sems)
        plan.finish(ins, outs, sems)

    return pl.pallas_call(
        body, name=name, in_specs=[ANY] * k, out_specs=[ANY] * k, out_shape=plan.out_shape,
        scratch_shapes=plan.sems, compiler_params=pltpu.CompilerParams(has_side_effects=True),
    )(*plan.arrays)


def _call(body, *, name, grid, in_specs, out_specs, out_shape, scratch_shapes, params, args, comm=None,
          aliases=None, prefetch=()):
    aliases = aliases or {}
    n_pre = len(prefetch)

    def launch(fn, ins_specs, outs_specs, outs_shape, scratch, operands):
        spec = pltpu.PrefetchScalarGridSpec(num_scalar_prefetch=n_pre, grid=grid, in_specs=ins_specs,
                                            out_specs=outs_specs, scratch_shapes=scratch)
        return pl.pallas_call(fn, name=name, grid_spec=spec, out_shape=outs_shape, compiler_params=params,
                              input_output_aliases=aliases)(*prefetch, *[_small_in_hbm(a) for a in operands])

    if comm is None:
        return list(launch(body, in_specs, out_specs, out_shape, scratch_shapes, args)), []
    n_in, n_out, n_scr, k = len(in_specs), len(out_specs), len(scratch_shapes), len(comm.arrays)

    def wrapped(*refs):
        pre, refs = refs[:n_pre], refs[n_pre:]
        ins = refs[:n_in]
        c_in = refs[n_in:n_in + k]
        outs = refs[n_in + k:n_in + k + n_out]
        c_out = refs[n_in + k + n_out:n_in + 2 * k + n_out]
        scr = refs[n_in + 2 * k + n_out:n_in + 2 * k + n_out + n_scr]
        sems = refs[n_in + 2 * k + n_out + n_scr:]
        step, steps = pl.program_id(0), grid[0]
        for d in range(1, len(grid)):
            step, steps = step * grid[d] + pl.program_id(d), steps * grid[d]

        @pl.when(step == 0)
        def _():
            comm.start(c_in, c_out, sems)

        if comm.middle is not None:
            @pl.when(step == (comm.middle_at * steps) // 8)
            def _():
                comm.middle(c_in, c_out, sems)

        body(*pre, *ins, *outs, *scr)

        @pl.when(step == steps - 1)
        def _():
            comm.finish(c_in, c_out, sems)

    res = launch(wrapped, list(in_specs) + [ANY] * k, list(out_specs) + [ANY] * k,
                 list(out_shape) + list(comm.out_shape), list(scratch_shapes) + list(comm.sems),
                 tuple(args) + tuple(comm.arrays))
    return list(res[:n_out]), list(res[n_out:])


def _wgrad_paired(name, a_t, b, core, comm=None):
    m, s = a_t.shape
    cols = b.shape[1] // N_SLOT
    half = N_SLOT // 2

    def owner(k, c):
        return 2 * (k % half) + jnp.where(k < half, 1 - c, c)

    def body(c_ref, a_ref, b_ref, sum_ref, out_s, recv_s, send_sems, recv_sems):
        del c_ref
        k = pl.program_id(0)
        x, y, c = _place()

        def to_sibling(j):
            return pltpu.make_async_remote_copy(src_ref=out_s.at[j % 2], dst_ref=recv_s.at[j],
                                                send_sem=send_sems.at[j], recv_sem=recv_sems.at[j],
                                                device_id=(x, y, 1 - c), device_id_type=MESH)

        @pl.when((k >= 2) & (k < half + 2))
        def _():
            to_sibling(k - 2).wait_send()

        @pl.when(k < half)
        def _():
            out_s[k % 2] = _dot(a_ref[...], b_ref[...]).astype(BF16)
            to_sibling(k).start()

        @pl.when(k >= half)
        def _():
            to_sibling(k - half).wait_recv()
            sum_ref[...] = (_dot(a_ref[...], b_ref[...]) + recv_s[k - half].astype(F32)).astype(BF16)

    return _call(body, name=name, grid=(N_SLOT,),
                 in_specs=[pl.BlockSpec((m, s), lambda k, c_ref: (0, 0)),
                           pl.BlockSpec((s, cols), lambda k, c_ref: (0, owner(k, c_ref[0])))],
                 out_specs=[pl.BlockSpec((None, m, cols), lambda k, c_ref: (jnp.maximum(k - half, 0), 0, 0))],
                 out_shape=[SDS((half, m, cols), BF16)],
                 scratch_shapes=[pltpu.VMEM((2, m, cols), BF16), pltpu.VMEM((half, m, cols), BF16),
                                 pltpu.SemaphoreType.DMA((half,)), pltpu.SemaphoreType.DMA((half,))],
                 params=_cparams(("arbitrary",), 56), args=(a_t, b), comm=comm, prefetch=(core,))


def _wgrad_down(act_t, dh2, core, tokens):
    s = dh2.shape[0]
    ts = min(tokens, s)
    half = N_SLOT // 2

    def body(c_ref, a_ref, b_ref, keep_ref, recv_ref, acc_s, out_s, send_sems, recv_sems):
        k, t = pl.program_id(0), pl.program_id(1)
        x, y, c = _place()
        last = t == pl.num_programs(1) - 1

        def to_sibling(j):
            return pltpu.make_async_remote_copy(src_ref=out_s.at[j % 2], dst_ref=recv_ref.at[j],
                                                send_sem=send_sems.at[j], recv_sem=recv_sems.at[j],
                                                device_id=(x, y, 1 - c), device_id_type=MESH)

        @pl.when(t == 0)
        def _():
            acc_s[...] = jnp.zeros_like(acc_s)

        acc_s[...] += _dot(a_ref[...], b_ref[...].astype(BF16)).reshape(acc_s.shape)

        @pl.when(last & (k >= 2))
        def _():
            to_sibling(k - 2).wait_send()

        @pl.when(last)
        def _():
            keep_ref[...] = acc_s[c_ref[0]].astype(BF16)
            out_s[k % 2] = acc_s[1 - c_ref[0]].astype(BF16)
            to_sibling(k).start()

        @pl.when(last & (k == half - 1))
        def _():
            for j in (half - 2, half - 1):
                to_sibling(j).wait_send()
            for j in range(half):
                to_sibling(j).wait_recv()

    blk = SDS((half, FF_COLS, D), BF16)
    (kept, recv), _ = _call(
        body, name="wgrad_down", grid=(half, s // ts),
        in_specs=[pl.BlockSpec((2 * FF_COLS, ts), lambda k, t, c_ref: (k, t)),
                  pl.BlockSpec((ts, D), lambda k, t, c_ref: (t, 0))],
        out_specs=[pl.BlockSpec((None, FF_COLS, D), lambda k, t, c_ref: (k, 0, 0)), ANY], out_shape=[blk, blk],
        scratch_shapes=[pltpu.VMEM((2, FF_COLS, D), F32), pltpu.VMEM((2, FF_COLS, D), BF16),
                        pltpu.SemaphoreType.DMA((half,)), pltpu.SemaphoreType.DMA((half,))],
        params=_cparams(("arbitrary", "arbitrary"), 48), args=(act_t, dh2), prefetch=(core,))
    return kept, recv


def _row_tile(rows):
    for t in (512, 256, 128, 64, 32, 16, 8):
        if rows % t == 0:
            return t
    return rows


def _pair_sum(name, g8, recv4, core):
    _, rows, cols = recv4.shape
    tr = _row_tile(rows)
    per = g8.shape[0] // 4
    g42 = g8.reshape(4, per, rows, cols)

    def body(c_ref, g_ref, r_ref, o_ref):
        del c_ref
        o_ref[...] = (g_ref[...].astype(F32) + r_ref[...].astype(F32)).astype(o_ref.dtype)

    return pl.pallas_call(
        body, name=name,
        grid_spec=pltpu.PrefetchScalarGridSpec(
            num_scalar_prefetch=1, grid=(4, rows // tr),
            in_specs=[pl.BlockSpec((None, None, tr, cols), lambda j, i, c_ref: (j, c_ref[0] * (per - 1), i, 0)),
                      pl.BlockSpec((None, tr, cols), lambda j, i, c_ref: (j, i, 0))],
            out_specs=pl.BlockSpec((None, tr, cols), lambda j, i, c_ref: (j, i, 0))),
        out_shape=SDS(recv4.shape, g8.dtype),
        compiler_params=_cparams(("arbitrary", "arbitrary"), 32),
    )(core, g42, recv4)


def _add2(name, a, b):
    rows, cols = a.shape
    tr = _row_tile(rows)

    def body(a_ref, b_ref, o_ref):
        o_ref[...] = a_ref[...] + b_ref[...]

    blk = pl.BlockSpec((tr, cols), lambda i: (i, 0))
    return pl.pallas_call(body, name=name, grid=(rows // tr,), in_specs=[blk, blk], out_specs=blk,
                          out_shape=SDS(a.shape, a.dtype),
                          compiler_params=_cparams(("arbitrary",), 32))(a, b)


def _sum_terms(name, terms):
    k, rows, cols = terms.shape
    tr = _row_tile(rows)

    def body(r_ref, o_ref):
        acc = r_ref[0]
        for q in range(1, k):
            acc = acc + r_ref[q]
        o_ref[...] = acc

    return pl.pallas_call(body, name=name, grid=(rows // tr,),
                          in_specs=[pl.BlockSpec((k, tr, cols), lambda i: (0, i, 0))],
                          out_specs=pl.BlockSpec((tr, cols), lambda i: (i, 0)),
                          out_shape=SDS((rows, cols), terms.dtype),
                          compiler_params=_cparams(("arbitrary",), 32))(terms)


def _adam_update(g, w, m, v):
    c1 = 1.0 / (1.0 - ADAM_B1 ** ADAM_STEP)
    c2 = 1.0 / (1.0 - ADAM_B2 ** ADAM_STEP)
    mn = ADAM_B1 * m + (1.0 - ADAM_B1) * g
    vn = ADAM_B2 * v + (1.0 - ADAM_B2) * (g * g)
    delta = (-ADAM_LR) * ((mn * c1) / (jnp.sqrt(vn * c2) + ADAM_EPS) + ADAM_WD * w)
    return delta, mn, vn


def _adamw_many(name, gs, ws, ms, vs):
    n = len(gs)

    def body(*refs):
        for p in range(n):
            g, w, m, v = (refs[q * n + p][...] for q in range(4))
            d, mn, vn = _adam_update(g, w, m, v)
            refs[4 * n + p][...] = d
            refs[5 * n + p][...] = mn
            refs[6 * n + p][...] = vn

    full = [pl.BlockSpec(w.shape, lambda i: (0, 0)) for w in ws]
    shapes = [SDS(w.shape, F32) for w in ws]
    res = pl.pallas_call(body, name=name, grid=(1,), in_specs=full * 4, out_specs=full * 3, out_shape=shapes * 3,
                         compiler_params=_cparams(("arbitrary",), 32),
                         )(*[_small_in_hbm(a) for a in (*gs, *ws, *ms, *vs)])
    return [(res[p], res[n + p], res[2 * n + p]) for p in range(n)]


def _adamw(name, terms, w, m, v):
    k, rows, cols = terms.shape
    tr = _row_tile(rows)

    def body(t_ref, w_ref, m_ref, v_ref, g_ref, d_ref, mo_ref, vo_ref):
        g = t_ref[0].astype(F32)
        for q in range(1, k):
            g = g + t_ref[q].astype(F32)
        g_ref[...] = g
        d_ref[...], mo_ref[...], vo_ref[...] = _adam_update(g, w_ref[...], m_ref[...], v_ref[...])

    blk = pl.BlockSpec((tr, cols), lambda i: (i, 0))
    return pl.pallas_call(body, name=name, grid=(rows // tr,),
                          in_specs=[pl.BlockSpec((k, tr, cols), lambda i: (0, i, 0)), blk, blk, blk],
                          out_specs=[blk] * 4, out_shape=[SDS((rows, cols), F32)] * 4,
                          compiler_params=_cparams(("arbitrary",), 40),
                          )(*[pltpu.with_memory_space_constraint(a, pltpu.HBM) for a in (terms, w, m, v)])


def kernel(x, norm_mix_g, w_in, conv_w, conv_b, w_rgate, b_rgate, w_igate, b_igate, lru_lambda, w_out_a, sgu_ln_g, sgu_ln_b, sgu_w_s, sgu_b_s, w_out_b, w_out, norm_mlp_g, w_up, w_down, norm_final_g, loss_target, m_norm_mix_g, m_w_in, m_conv_w, m_conv_b, m_w_rgate, m_b_rgate, m_w_igate, m_b_igate, m_lru_lambda, m_w_out_a, m_sgu_ln_g, m_sgu_ln_b, m_sgu_w_s, m_sgu_b_s, m_w_out_b, m_w_out, m_norm_mlp_g, m_w_up, m_w_down, m_norm_final_g, v_norm_mix_g, v_w_in, v_conv_w, v_conv_b, v_w_rgate, v_b_rgate, v_w_igate, v_b_igate, v_lru_lambda, v_w_out_a, v_sgu_ln_g, v_sgu_ln_b, v_sgu_w_s, v_sgu_b_s, v_w_out_b, v_w_out, v_norm_mlp_g, v_w_up, v_w_down, v_norm_final_g):
    cx, cy, cc = _place()
    me = 4 * cx + 2 * cy + cc
    core = jnp.reshape(cc, (1,)).astype(jnp.int32)
    xs = x[0]
    tgt = loss_target[0]

    gate_shard = jnp.stack([w_rgate[0], w_igate[0]]).astype(BF16).reshape(2 * HEADS * 32, HEAD_DIM)
    vec_shard = jnp.concatenate([conv_w[0], b_rgate[0], b_igate[0]], axis=1)
    vec_shard = jnp.pad(vec_shard, ((0, 4), (0, 256 - vec_shard.shape[1])))
    shards = [w_in[0].astype(BF16), w_out_a[0].astype(BF16), w_out_b[0].astype(BF16), w_out[0].astype(BF16),
              w_up[0].astype(BF16), w_down[0].astype(BF16), gate_shard, vec_shard]
    (z, n1_t, w_in_g), (gate_g, vec_g) = _in_proj(xs, norm_mix_g, shards[0], _slot_order(cx, cy, cc),
                                                comm=_gather_plan(shards[6:8]))
    gates = gate_g.reshape(N_SLOT, 2, HEADS, 32, HEAD_DIM).transpose(1, 2, 0, 3, 4).reshape(2, HEADS, HEAD_DIM, HEAD_DIM)
    w_r_f, w_i_f = gates[0], gates[1]
    conv_w_f = vec_g[:, 0:4, 0:128].transpose(1, 0, 2).reshape(CONV_K, D)
    b_r_f = vec_g[:, 0:4, 128:160].transpose(1, 0, 2).reshape(1, D)
    b_i_f = vec_g[:, 0:4, 160:192].transpose(1, 0, 2).reshape(1, D)
    b_s_t = jnp.transpose(sgu_b_s[0])

    (ya, hs, xc, r_gate, i_gate), (w_oa_g, w_ob_g, w_out_g, w_up_g) = _branch_a_fwd(
        z, conv_w_f, conv_b, w_r_f, b_r_f, w_i_f, b_i_f, lru_lambda, comm=_gather_plan(shards[1:5]))
    w_oa_f = w_oa_g.reshape(D, D)
    w_ob_f = w_ob_g.reshape(D, D)
    w_out_f = w_out_g.reshape(D, D)
    (yb,), _ = _branch_b_fwd(z, sgu_ln_g, sgu_ln_b, sgu_w_s[0], b_s_t)
    (pa, pb, merged, h1), (w_down_g,) = _merge_out(ya, yb, z, xs, w_oa_f, w_ob_f, w_out_f,
                                                   comm=_gather_plan(shards[5:6], middle_at=4))
    w_down_f = w_down_g.reshape(N_SLOT * FF_COLS, D)
    r_act, act_t, n2_t, dh2, loss_acc, d_gfin = _mlp_fwd(h1, norm_mlp_g, w_up_g, w_down_f,
                                                         norm_final_g.reshape(1, D), tgt)

    def pair(names, grads, recv):
        return [_pair_sum("pair_sum_" + nm, g, r, core) for nm, g, r in zip(names, grads, recv)]

    g_down, r_down = _wgrad_down(act_t, dh2, core, 2048)
    (p_down,) = pair(["down"], [g_down], [r_down])
    (df, dh1, d_gmlp), (got_down,) = _mlp_bwd(dh2, r_act, w_down_f, w_up_g, h1, norm_mlp_g,
                                              comm=_chips_plan([p_down]))
    g_up = _wgrad_t("wgrad_up", n2_t, df, N_SLOT, False, True, 4096)
    g_out = _wgrad("wgrad_out", merged, dh1, 1, False, False).reshape(N_SLOT, D // N_SLOT, D)
    (dz, dpa, dpb, dya, dyb), (r_up, r_out) = _merge_bwd(
        dh1, z, pa, pb, w_out_f, w_oa_f, w_ob_f, comm=_sibling_plan([g_up, g_out]))
    p_up, p_out = pair(["up", "out"], [g_up, g_out], [r_up, r_out])
    g_oa = _wgrad("wgrad_out_a", ya, dpa, 1, False, False).reshape(N_SLOT, D // N_SLOT, D)
    g_ob = _wgrad("wgrad_out_b", yb, dpb, 1, False, False).reshape(N_SLOT, D // N_SLOT, D)
    (dz, d_ws, d_bs, d_ln), (got_out, r_oa, r_ob) = _branch_b_bwd(
        dz, dyb, z, sgu_ln_g, sgu_ln_b, sgu_w_s[0], b_s_t,
        comm=_join(_chips_plan([p_out]), _sibling_plan([g_oa, g_ob])))
    p_oa, p_ob = pair(["out_a", "out_b"], [g_oa, g_ob], [r_oa, r_ob])
    (dz, d_vec, d_wr, d_wi), (got_up, got_oa, got_ob) = _branch_a_bwd(
        dz, dya, z, hs, xc, r_gate, i_gate, conv_w_f, w_r_f, w_i_f, lru_lambda,
        comm=_chips_plan([p_up, p_oa, p_ob]))
    g_gate = jnp.stack([d_wr, d_wi]).reshape(2, HEADS, N_SLOT, 32, HEAD_DIM).transpose(2, 0, 1, 3, 4)
    g_gate = g_gate.reshape(N_SLOT, 2 * HEADS * 32, HEAD_DIM).astype(BF16)

    d_bs_row = jnp.pad(d_bs[:, :, 0].reshape(1, GROUPS * CHUNK), ((0, 0), (0, D - GROUPS * CHUNK)))
    vecs = jnp.concatenate([d_vec, jnp.concatenate([d_ln[0:2], d_gmlp, d_gfin, d_bs_row, jnp.zeros((3, D), F32)])])
    d_ws2 = d_ws.reshape(GROUPS * CHUNK, CHUNK)
    (p_in,), (r_gate, r_vecs, r_ws) = _wgrad_paired("wgrad_in", n1_t, dz, core,
                                                    comm=_sibling_plan([g_gate], [vecs, d_ws2]))
    (p_gate,) = pair(["gate"], [g_gate], [r_gate])
    vecs_chip = _add2("pair_sum_vecs", vecs, r_vecs)
    ws_chip = _add2("pair_sum_ws", d_ws2, r_ws)
    (dx, d_gmix), (got_in, got_gate, got_vecs, got_ws) = _in_bwd(
        dz, w_in_g, xs, dh1, norm_mix_g, comm=_chips_plan([p_in, p_gate], [vecs_chip, ws_chip]))
    vecs_sum = _sum_terms("sum_vecs", got_vecs)
    last = jnp.concatenate([d_gmix, jnp.pad(loss_acc[0:1], ((0, 0), (0, D - 128))), jnp.zeros((6, D), F32)])
    (last_all,) = _run_plan("exchange_last", _exchange_plan(last))
    last_sum = _sum_terms("sum_last", last_all)
    loss = last_sum[1, 0]
    got = [got_in, got_oa, got_ob, got_out, got_up, got_down, got_gate]

    def step(nm, terms, w, m, v, rows, cols):
        g, d, mn, vn = _adamw("adamw_" + nm, terms.reshape(4, rows, cols), w.reshape(rows, cols),
                              m.reshape(rows, cols), v.reshape(rows, cols))
        return [a.reshape(w.shape) for a in (g, d, mn, vn)]

    o_in = step("in", got[0], w_in, m_w_in, v_w_in, D, W_IN_COLS)
    o_oa = step("out_a", got[1], w_out_a, m_w_out_a, v_w_out_a, D // N_SLOT, D)
    o_ob = step("out_b", got[2], w_out_b, m_w_out_b, v_w_out_b, D // N_SLOT, D)
    o_out = step("out", got[3], w_out, m_w_out, v_w_out, D // N_SLOT, D)
    o_up = step("up", got[4], w_up, m_w_up, v_w_up, D, FF_COLS)
    o_down = step("down", got[5], w_down, m_w_down, v_w_down, FF_COLS, D)
    gate_w = jnp.stack([w_rgate[0], w_igate[0]]).reshape(2 * HEADS * 32, HEAD_DIM)
    gate_m = jnp.stack([m_w_rgate[0], m_w_igate[0]]).reshape(2 * HEADS * 32, HEAD_DIM)
    gate_v = jnp.stack([v_w_rgate[0], v_w_igate[0]]).reshape(2 * HEADS * 32, HEAD_DIM)
    o_gate = _adamw("adamw_gate", got[6], gate_w, gate_m, gate_v)
    o_gate = [a.reshape(2, 1, HEADS, 32, HEAD_DIM) for a in o_gate]
    o_wr = [a[0] for a in o_gate]
    o_wi = [a[1] for a in o_gate]

    def own(full, width):
        return lax.dynamic_slice_in_dim(full, me * width, width, axis=1)

    small_g = {
        "norm_mix_g": last_sum[0:1], "conv_w": own(vecs_sum[0:4], 128), "conv_b": vecs_sum[4:5],
        "b_rgate": own(vecs_sum[5:6].reshape(HEADS, HEAD_DIM), 32),
        "b_igate": own(vecs_sum[6:7].reshape(HEADS, HEAD_DIM), 32),
        "lru_lambda": vecs_sum[7:8], "sgu_ln_g": vecs_sum[8:9], "sgu_ln_b": vecs_sum[9:10],
        "norm_mlp_g": vecs_sum[10:11], "norm_final_g": vecs_sum[11:12],
        "sgu_b_s": vecs_sum[12, 0:GROUPS * CHUNK].reshape(GROUPS, CHUNK),
    }
    small_w = {"norm_mix_g": (norm_mix_g, m_norm_mix_g, v_norm_mix_g), "conv_w": (conv_w, m_conv_w, v_conv_w),
               "conv_b": (conv_b, m_conv_b, v_conv_b), "b_rgate": (b_rgate, m_b_rgate, v_b_rgate),
               "b_igate": (b_igate, m_b_igate, v_b_igate), "lru_lambda": (lru_lambda, m_lru_lambda, v_lru_lambda),
               "sgu_ln_g": (sgu_ln_g, m_sgu_ln_g, v_sgu_ln_g), "sgu_ln_b": (sgu_ln_b, m_sgu_ln_b, v_sgu_ln_b),
               "norm_mlp_g": (norm_mlp_g, m_norm_mlp_g, v_norm_mlp_g),
               "norm_final_g": (norm_final_g, m_norm_final_g, v_norm_final_g),
               "sgu_b_s": (sgu_b_s, m_sgu_b_s, v_sgu_b_s), "sgu_w_s": (sgu_w_s, m_sgu_w_s, v_sgu_w_s)}
    order = list(small_g)
    as2d = lambda k, a: a.reshape(small_g[k].shape)
    upd = _adamw_many("adamw_small", [small_g[k] for k in order], *[[as2d(k, small_w[k][q]) for k in order]
                                                                     for q in range(3)])
    o_small = {k: [a.reshape(small_w[k][0].shape) for a in (small_g[k],) + u] for k, u in zip(order, upd)}
    ws3 = [a[0].reshape(GROUPS * CHUNK, CHUNK) for a in small_w.pop("sgu_w_s")]
    o_small["sgu_w_s"] = [a.reshape(sgu_w_s.shape) for a in _adamw("adamw_ws", got_ws, *ws3)]

    per_weight = {"norm_mix_g": o_small["norm_mix_g"], "w_in": o_in, "conv_w": o_small["conv_w"],
                  "conv_b": o_small["conv_b"], "w_rgate": o_wr, "b_rgate": o_small["b_rgate"], "w_igate": o_wi,
                  "b_igate": o_small["b_igate"], "lru_lambda": o_small["lru_lambda"], "w_out_a": o_oa,
                  "sgu_ln_g": o_small["sgu_ln_g"], "sgu_ln_b": o_small["sgu_ln_b"], "sgu_w_s": o_small["sgu_w_s"],
                  "sgu_b_s": o_small["sgu_b_s"], "w_out_b": o_ob, "w_out": o_out, "norm_mlp_g": o_small["norm_mlp_g"],
                  "w_up": o_up, "w_down": o_down, "norm_final_g": o_small["norm_final_g"]}
    names_w = list(per_weight)
    return (loss, dx[None], *[per_weight[k][0] for k in names_w], *[per_weight[k][1] for k in names_w],
            *[per_weight[k][2] for k in names_w], *[per_weight[k][3] for k in names_w])
```

```python
import jax
import jax.numpy as jnp
from jax import lax
from jax.experimental import pallas as pl
from jax.experimental.pallas import tpu as pltpu

F32 = jnp.float32
BF16 = jnp.bfloat16
SDS = jax.ShapeDtypeStruct
MESH = pl.DeviceIdType.MESH
ANY = pl.BlockSpec(memory_space=pl.ANY)

D = 1024
N_SLOT = 8
W_IN_COLS = 768
FF_COLS = 512
HEADS, HEAD_DIM = 4, 256
GROUPS, GROUP_DIM = 4, 256
CHUNK = 128
CONV_K = 4
NORM_EPS = 1e-6
LN_EPS = 1e-5
LRU_C = 8.0
ADAM_LR, ADAM_B1, ADAM_B2, ADAM_EPS, ADAM_WD, ADAM_STEP = 0.001, 0.9, 0.999, 1e-08, 0.01, 10

TM_ROWS = 1024
TM_MERGE = 512
T_BRANCH_A = 512
T_BRANCH_B = 256
MiB = 1024 * 1024
SMALL_OPERAND = 16 * 1024

_GELU_C = 0.7978845608028654
_GELU_A = 0.044715


def _small_in_hbm(a):
    return pltpu.with_memory_space_constraint(a, pltpu.HBM) if a.size <= SMALL_OPERAND else a


def _cparams(sem, vmem_mib):
    return pltpu.CompilerParams(dimension_semantics=sem, vmem_limit_bytes=vmem_mib * MiB)


def _gelu(x):
    t = jnp.tanh(_GELU_C * (x + _GELU_A * x * x * x))
    return 0.5 * x * (1.0 + t)


def _gelu_and_grad(x):
    x2 = x * x
    t = jnp.tanh(_GELU_C * x * (1.0 + _GELU_A * x2))
    g = 0.5 * x * (1.0 + t)
    dg = 0.5 * (1.0 + t) + 0.5 * x * (1.0 - t * t) * _GELU_C * (1.0 + 3.0 * _GELU_A * x2)
    return g, dg


def _softplus(x):
    return jnp.maximum(x, 0.0) + jnp.log1p(jnp.exp(-jnp.abs(x)))


def _dot(a, b):
    return jnp.dot(a, b, preferred_element_type=F32)


def _dot_nt(a, b):
    return lax.dot_general(a, b, (((1,), (1,)), ((), ())), preferred_element_type=F32)


def _dot_tn(a, b):
    return lax.dot_general(a, b, (((0,), (0,)), ((), ())), preferred_element_type=F32)


def _rows_shifted(prev8, cur, k):
    ext = jnp.concatenate([prev8, cur], axis=0)
    return pltpu.roll(ext, k, 0)[8:]


def _rows_advanced(cur, next8, k):
    t = cur.shape[0]
    ext = jnp.concatenate([cur, next8], axis=0)
    return pltpu.roll(ext, t + 8 - k, 0)[:t]


def _first_second(x, y, c):
    ny, nx, far = _other_chips(x, y)
    pick = lambda a, b: a * (1 - c) + b * c
    first = tuple(pick(a, b) for a, b in zip(ny, nx))
    second = tuple(pick(b, a) for a, b in zip(ny, nx))
    return first, second, far


def _slot_order(x, y, c):
    chip = 2 * x + y
    first, second, far = _first_second(x, y, c)
    order = [2 * chip + c, 2 * chip + 1 - c, 2 * first[2] + c, 2 * second[2] + 1 - c, 2 * second[2] + c,
             2 * first[2] + 1 - c, 2 * far[2] + c, 2 * far[2] + 1 - c]
    return jnp.stack(order).astype(jnp.int32)


def _in_proj(x, g_mix, w_in_own, order, comm=None):
    s = x.shape[0]
    tm = min(TM_ROWS, s)
    ni = s // tm

    def body(order_ref, x_ref, g_ref, own_ref, z_ref, nt_ref, wg_ref, n_s, w_s, send_sems, recv_sems, local_sems):
        j, i = pl.program_id(0), pl.program_id(1)
        px, py, c = _place()
        chip = 2 * px + py
        me = 2 * chip + c
        sib = (px, py, 1 - c)
        chips = _other_chips(px, py)

        def rc(k, src, blk, to):
            return pltpu.make_async_remote_copy(src_ref=src, dst_ref=w_s.at[blk], send_sem=send_sems.at[k],
                                                recv_sem=recv_sems.at[k], device_id=to, device_id_type=MESH)

        del chips
        first, second, far = _first_second(px, py, c)
        blocks = [2 * first[2] + c, 2 * second[2] + c, 2 * far[2] + c]
        own_in = pltpu.make_async_copy(own_ref, w_s.at[me], local_sems.at[0])
        to_first = rc(1, own_ref, me, (first[0], first[1], c))
        to_second = rc(2, own_ref, me, (second[0], second[1], c))
        relay = rc(3, w_s.at[blocks[0]], blocks[0], (second[0], second[1], c))
        sends = [rc(0, own_ref, me, sib), to_first, to_second, relay]
        passed = [rc(4 + q, w_s.at[blk], blk, sib) for q, blk in enumerate(blocks)]
        keep = pltpu.make_async_copy(w_s, wg_ref, local_sems.at[1])

        @pl.when((i == 0) & (j == 0))
        def _():
            own_in.start()
            sends[0].start()
            to_first.start()
            own_in.wait()

        @pl.when((i == 0) & (j == 1))
        def _():
            rc(0, own_ref, 2 * chip + 1 - c, sib).wait_recv()

        for q, blk in enumerate(blocks):
            @pl.when((i == 0) & (j == 2 + 2 * q))
            def _():
                rc(1 + q, own_ref, blk, sib).wait_recv()
                passed[q].start()
                if q == 0:
                    to_second.start()
                    relay.start()

            @pl.when((i == 0) & (j == 3 + 2 * q))
            def _():
                rc(4 + q, own_ref, order_ref[j], sib).wait_recv()

        rows = pl.ds(pl.multiple_of(i * tm, tm), tm)

        @pl.when(j == 0)
        def _():
            xv = x_ref[...]
            rstd = lax.rsqrt(jnp.mean(xv * xv, axis=-1, keepdims=True) + NORM_EPS)
            nb = (xv * rstd * g_ref[...]).astype(BF16)
            n_s[rows, :] = nb
            nt_ref[...] = nb.T

        z_ref[...] = _dot(n_s[rows, :], w_s[order_ref[j]]).astype(BF16)

        @pl.when((i == 0) & (j == N_SLOT - 1))
        def _():
            keep.start()

        @pl.when((i == ni - 1) & (j == N_SLOT - 1))
        def _():
            for cp in sends + passed:
                cp.wait_send()
            keep.wait()

    first_pass = lambda j, i, o: (jnp.where(j == 0, i, ni - 1), 0)
    (z, n1, w_in_g), extra = _call(
        body, name="in_proj", grid=(N_SLOT, ni), prefetch=(order,),
        in_specs=[pl.BlockSpec((tm, D), first_pass),
                  pl.BlockSpec((1, D), lambda j, i, o: (0, 0)), ANY],
        out_specs=[pl.BlockSpec((tm, W_IN_COLS), lambda j, i, o: (i, o[j])),
                   pl.BlockSpec((D, tm), lambda j, i, o: (0, jnp.where(j == 0, i, ni - 1))), ANY],
        out_shape=[SDS((s, N_SLOT * W_IN_COLS), BF16), SDS((D, s), BF16), SDS((N_SLOT, D, W_IN_COLS), BF16)],
        scratch_shapes=[pltpu.VMEM((s, D), BF16), pltpu.VMEM((N_SLOT, D, W_IN_COLS), BF16),
                        pltpu.SemaphoreType.DMA((7,)), pltpu.SemaphoreType.DMA((7,)), pltpu.SemaphoreType.DMA((2,))],
        params=_cparams(("arbitrary", "arbitrary"), 56), args=(x, g_mix, w_in_own), comm=comm)
    return (z, n1, w_in_g), extra


def _decay(r, sp_lam):
    log_a = (-LRU_C) * r * sp_lam
    a = jnp.exp(log_a)
    return a, jnp.sqrt(-jnp.tanh(log_a) * (a * a + 1.0))


def _lru_gates(xc, xcb, wr_ref, br, wi_ref, bi, sp_lam, a_s, b_s, r_ref, i_ref):
    for h in range(HEADS):
        sl = slice(h * HEAD_DIM, (h + 1) * HEAD_DIM)
        r = jax.nn.sigmoid(_dot(xcb[:, sl], wr_ref[h]) + br[:, sl])
        ig = jax.nn.sigmoid(_dot(xcb[:, sl], wi_ref[h]) + bi[:, sl])
        a, mult = _decay(r, sp_lam[:, sl])
        a_s[:, sl] = a
        b_s[:, sl] = xc[:, sl] * ig * mult
        r_ref[:, sl] = r.astype(BF16)
        i_ref[:, sl] = ig.astype(BF16)


def _conv_fwd(xa, prev8, cw, cb):
    xc = cb + cw[0:1, :] * xa
    for k in range(1, CONV_K):
        xc = xc + cw[k:k + 1, :] * _rows_shifted(prev8, xa, k)
    return xc


def _branch_a_fwd(z, conv_w, conv_b, w_r, b_r, w_i, b_i, lam, comm=None):
    s = z.shape[0]
    ta = min(T_BRANCH_A, s)
    per16 = ta // 16

    def body(xa_ref, xp_ref, ga_ref, cw_ref, cb_ref, wr_ref, br_ref, wi_ref, bi_ref, lam_ref,
             ya_ref, hs_ref, xc_ref, r_ref, i_ref, a_s, b_s, h_s, carry_s):
        i = pl.program_id(0)

        @pl.when(i == 0)
        def _():
            carry_s[...] = jnp.zeros_like(carry_s)

        xa = xa_ref[...].astype(F32)
        prev8 = jnp.where(i > 0, xp_ref[...].astype(F32)[8:16], 0.0)
        xc = _conv_fwd(xa, prev8, cw_ref[...], cb_ref[...])
        xcb = xc.astype(BF16)
        xc_ref[...] = xcb
        sp_lam = _softplus(-lam_ref[...])
        _lru_gates(xc, xcb, wr_ref, br_ref[...], wi_ref, bi_ref[...], sp_lam, a_s, b_s, r_ref, i_ref)

        row = lax.broadcasted_iota(jnp.int32, (8, D), 0)

        def group(g, carry):
            off = pl.multiple_of(g * 8, 8)
            a8 = a_s[pl.ds(off, 8), :]
            b8 = b_s[pl.ds(off, 8), :]
            for d in (1, 2, 4):
                a_sh = jnp.where(row >= d, pltpu.roll(a8, d, 0), 1.0)
                b_sh = jnp.where(row >= d, pltpu.roll(b8, d, 0), 0.0)
                b8 = a8 * b_sh + b8
                a8 = a8 * a_sh
            h8 = b8 + a8 * carry
            h_s[pl.ds(off, 8), :] = h8
            return jnp.broadcast_to(h8[7:8, :], (8, D))

        carry_s[...] = lax.fori_loop(0, ta // 8, group, carry_s[...])
        hs = h_s[...]
        hs_ref[...] = hs.astype(BF16)
        ya_ref[...] = (hs * _gelu(ga_ref[...].astype(F32))).astype(BF16)

    vec = pl.BlockSpec((1, D), lambda i: (0, 0))
    gate = pl.BlockSpec((HEADS, HEAD_DIM, HEAD_DIM), lambda i: (0, 0, 0))
    return _call(
        body, name="branch_a_fwd", grid=(s // ta,),
        in_specs=[pl.BlockSpec((ta, D), lambda i: (i, 0)),
                  pl.BlockSpec((16, D), lambda i: (jnp.maximum(i * per16 - 1, 0), 0)),
                  pl.BlockSpec((ta, D), lambda i: (i, 1)),
                  pl.BlockSpec((CONV_K, D), lambda i: (0, 0)), vec, gate, vec, gate, vec, vec],
        out_specs=[pl.BlockSpec((ta, D), lambda i: (i, 0))] * 5,
        out_shape=[SDS((s, D), BF16)] * 5,
        scratch_shapes=[pltpu.VMEM((ta, D), F32), pltpu.VMEM((ta, D), F32), pltpu.VMEM((ta, D), F32),
                        pltpu.VMEM((8, D), F32)],
        params=_cparams(("arbitrary",), 40), args=(z, z, z, conv_w, conv_b, w_r, b_r, w_i, b_i, lam), comm=comm)


def _sgu_common(ub, vb, lg, lb, with_grad):
    if with_grad:
        u, du = _gelu_and_grad(ub)
        v, dv = _gelu_and_grad(vb)
    else:
        u, v, du, dv = _gelu(ub), _gelu(vb), None, None
    mu = jnp.mean(v, axis=-1, keepdims=True)
    vc = v - mu
    rstd = lax.rsqrt(jnp.mean(vc * vc, axis=-1, keepdims=True) + LN_EPS)
    vhat = vc * rstd
    vln = vhat * lg + lb
    return u, du, dv, rstd, vhat, vln


def _masked_ws(ws_ref):
    t = lax.broadcasted_iota(jnp.int32, (CHUNK, CHUNK), 0)
    c = lax.broadcasted_iota(jnp.int32, (CHUNK, CHUNK), 1)
    keep = c <= t
    return [jnp.where(keep, ws_ref[g], 0.0).astype(BF16) for g in range(GROUPS)]


def _branch_b_fwd(z, ln_g, ln_b, w_s, b_s_t, comm=None):
    s = z.shape[0]
    tb = min(T_BRANCH_B, s)

    def body(ub_ref, vb_ref, lg_ref, lb_ref, ws_ref, bs_ref, yb_ref):
        u, _, _, _, _, vln = _sgu_common(ub_ref[...].astype(F32), vb_ref[...].astype(F32),
                                         lg_ref[...], lb_ref[...], False)
        vlnb = vln.astype(BF16)
        wm = _masked_ws(ws_ref)
        bs = bs_ref[...]
        for c in range(tb // CHUNK):
            rs = slice(c * CHUNK, (c + 1) * CHUNK)
            for g in range(GROUPS):
                cs = slice(g * GROUP_DIM, (g + 1) * GROUP_DIM)
                sp = _dot(wm[g], vlnb[rs, cs]) + bs[:, g:g + 1]
                yb_ref[rs, cs] = (u[rs, cs] * sp).astype(BF16)

    vec = pl.BlockSpec((1, D), lambda i: (0, 0))
    return _call(
        body, name="branch_b_fwd", grid=(s // tb,),
        in_specs=[pl.BlockSpec((tb, D), lambda i: (i, 2)), pl.BlockSpec((tb, D), lambda i: (i, 3)), vec, vec,
                  pl.BlockSpec((GROUPS, CHUNK, CHUNK), lambda i: (0, 0, 0)),
                  pl.BlockSpec((CHUNK, GROUPS), lambda i: (0, 0))],
        out_specs=[pl.BlockSpec((tb, D), lambda i: (i, 0))],
        out_shape=[SDS((s, D), BF16)], scratch_shapes=[],
        params=_cparams(("arbitrary",), 40), args=(z, z, ln_g, ln_b, w_s, b_s_t), comm=comm)


def _merge_out(ya, yb, z, x, w_oa, w_ob, w_out, comm=None):
    s = x.shape[0]
    tm = min(TM_MERGE, s)

    def body(ya_ref, yb_ref, ma_ref, mb_ref, x_ref, woa_ref, wob_ref, wo_ref, pa_ref, pb_ref, mg_ref, h1_ref):
        pa = _dot(ya_ref[...], woa_ref[...])
        pb = _dot(yb_ref[...], wob_ref[...])
        merged = (jax.nn.sigmoid(ma_ref[...].astype(F32)) * pa
                  + jax.nn.sigmoid(mb_ref[...].astype(F32)) * pb).astype(BF16)
        pa_ref[...] = pa.astype(BF16)
        pb_ref[...] = pb.astype(BF16)
        mg_ref[...] = merged
        h1_ref[...] = x_ref[...] + _dot(merged, wo_ref[...])

    row = pl.BlockSpec((tm, D), lambda i: (i, 0))
    wsp = pl.BlockSpec((D, D), lambda i: (0, 0))
    return _call(
        body, name="merge_out", grid=(s // tm,),
        in_specs=[row, row, pl.BlockSpec((tm, D), lambda i: (i, 4)), pl.BlockSpec((tm, D), lambda i: (i, 5)),
                  row, wsp, wsp, wsp],
        out_specs=[row, row, row, row],
        out_shape=[SDS((s, D), BF16), SDS((s, D), BF16), SDS((s, D), BF16), SDS((s, D), F32)], scratch_shapes=[],
        params=_cparams(("arbitrary",), 48), args=(ya, yb, z, z, x, w_oa, w_ob, w_out), comm=comm)


def _mlp_fwd(h1, g_mlp, w_up_g, w_down, g_fin, tgt):
    s = h1.shape[0]
    tm = min(TM_ROWS, s)
    nj = N_SLOT

    def body(h1_ref, gm_ref, wu_ref, wd_ref, gf_ref, t_ref, r_ref, at_ref, n2t_ref, dh2_ref, loss_ref, dgf_ref,
             n2_s, acc_s):
        i, j = pl.program_id(0), pl.program_id(1)

        @pl.when(j == 0)
        def _():
            hv = h1_ref[...]
            rstd = lax.rsqrt(jnp.mean(hv * hv, axis=-1, keepdims=True) + NORM_EPS)
            nb = (hv * rstd * gm_ref[...]).astype(BF16)
            n2_s[...] = nb
            n2t_ref[...] = nb.T
            acc_s[...] = jnp.zeros_like(acc_s)

        @pl.when((i == 0) & (j == 0))
        def _():
            loss_ref[...] = jnp.zeros_like(loss_ref)
            dgf_ref[...] = jnp.zeros_like(dgf_ref)

        r = jnp.maximum(_dot(n2_s[...], wu_ref[...]), 0.0)
        r_ref[...] = r.astype(BF16)
        act = (r * r).astype(BF16)
        at_ref[...] = act.T
        acc_s[...] += _dot(act, wd_ref[...])

        @pl.when(j == nj - 1)
        def _():
            h2 = h1_ref[...] + acc_s[...]
            rstd = lax.rsqrt(jnp.mean(h2 * h2, axis=-1, keepdims=True) + NORM_EPS)
            hh = h2 * rstd
            gf = gf_ref[...]
            e = hh * gf - t_ref[...]
            loss_ref[...] += jnp.sum(e * e) * (0.5 / D)
            dy = e * (1.0 / D)
            dgf_ref[...] += jnp.sum(dy * hh, axis=0, keepdims=True)
            dhh = dy * gf
            dh2_ref[...] = rstd * (dhh - hh * jnp.mean(dhh * hh, axis=-1, keepdims=True))

    row = pl.BlockSpec((tm, D), lambda i, j: (i, 0))
    vec = pl.BlockSpec((1, D), lambda i, j: (0, 0))
    return pl.pallas_call(
        body, name="mlp_fwd", grid=(s // tm, nj),
        in_specs=[row, vec, pl.BlockSpec((None, D, FF_COLS), lambda i, j: (j, 0, 0)),
                  pl.BlockSpec((FF_COLS, D), lambda i, j: (j, 0)), vec, row],
        out_specs=[pl.BlockSpec((tm, FF_COLS), lambda i, j: (i, j)), pl.BlockSpec((FF_COLS, tm), lambda i, j: (j, i)),
                   pl.BlockSpec((D, tm), lambda i, j: (0, i)), row, pl.BlockSpec((8, 128), lambda i, j: (0, 0)), vec],
        out_shape=[SDS((s, nj * FF_COLS), BF16), SDS((nj * FF_COLS, s), BF16), SDS((D, s), BF16), SDS((s, D), F32),
                   SDS((8, 128), F32), SDS((1, D), F32)],
        scratch_shapes=[pltpu.VMEM((tm, D), BF16), pltpu.VMEM((tm, D), F32)],
        compiler_params=_cparams(("arbitrary", "arbitrary"), 52),
    )(h1, _small_in_hbm(g_mlp), w_up_g, w_down, _small_in_hbm(g_fin), tgt)


def _mlp_bwd(dh2, r, w_down, w_up_g, h1, g_mlp, comm=None):
    s = h1.shape[0]
    tm = min(TM_ROWS, s)
    nj = N_SLOT

    def body(dh2_ref, r_ref, wd_ref, wu_ref, h1_ref, gm_ref, df_ref, dh1_ref, dgm_ref, dh2b_s, acc_s):
        i, j = pl.program_id(0), pl.program_id(1)

        @pl.when(j == 0)
        def _():
            dh2b_s[...] = dh2_ref[...].astype(BF16)
            acc_s[...] = jnp.zeros_like(acc_s)

        @pl.when((i == 0) & (j == 0))
        def _():
            dgm_ref[...] = jnp.zeros_like(dgm_ref)

        d_act = _dot_nt(dh2b_s[...], wd_ref[...])
        df = (d_act * (2.0 * r_ref[...].astype(F32))).astype(BF16)
        df_ref[...] = df
        acc_s[...] += _dot_nt(df, wu_ref[...])

        @pl.when(j == nj - 1)
        def _():
            hv = h1_ref[...]
            rstd = lax.rsqrt(jnp.mean(hv * hv, axis=-1, keepdims=True) + NORM_EPS)
            hh = hv * rstd
            dn2 = acc_s[...]
            dgm_ref[...] += jnp.sum(dn2 * hh, axis=0, keepdims=True)
            dhat = dn2 * gm_ref[...]
            dh1_ref[...] = dh2_ref[...] + rstd * (dhat - hh * jnp.mean(dhat * hh, axis=-1, keepdims=True))

    row = pl.BlockSpec((tm, D), lambda i, j: (i, 0))
    vec = pl.BlockSpec((1, D), lambda i, j: (0, 0))
    ffb = pl.BlockSpec((tm, FF_COLS), lambda i, j: (i, j))
    return _call(
        body, name="mlp_bwd", grid=(s // tm, nj),
        in_specs=[row, ffb, pl.BlockSpec((FF_COLS, D), lambda i, j: (j, 0)),
                  pl.BlockSpec((None, D, FF_COLS), lambda i, j: (j, 0, 0)), row, vec],
        out_specs=[ffb, row, vec],
        out_shape=[SDS((s, nj * FF_COLS), BF16), SDS((s, D), F32), SDS((1, D), F32)],
        scratch_shapes=[pltpu.VMEM((tm, D), BF16), pltpu.VMEM((tm, D), F32)],
        params=_cparams(("arbitrary", "arbitrary"), 52), args=(dh2, r, w_down, w_up_g, h1, g_mlp), comm=comm)


def _merge_bwd(dh1, z, pa, pb, w_out, w_oa, w_ob, comm=None):
    s = dh1.shape[0]
    tm = min(TM_MERGE, s)

    def body(dh1_ref, ma_ref, mb_ref, pa_ref, pb_ref, wo_ref, woa_ref, wob_ref,
             dz_ref, dpa_ref, dpb_ref, dya_ref, dyb_ref):
        dm = _dot_nt(dh1_ref[...].astype(BF16), wo_ref[...])
        sa = jax.nn.sigmoid(ma_ref[...].astype(F32))
        sb = jax.nn.sigmoid(mb_ref[...].astype(F32))
        dpa = (dm * sa).astype(BF16)
        dpb = (dm * sb).astype(BF16)
        dz_ref[:, 0:D] = (dm * pa_ref[...].astype(F32) * sa * (1.0 - sa)).astype(BF16)
        dz_ref[:, D:2 * D] = (dm * pb_ref[...].astype(F32) * sb * (1.0 - sb)).astype(BF16)
        dpa_ref[...] = dpa
        dpb_ref[...] = dpb
        dya_ref[...] = _dot_nt(dpa, woa_ref[...]).astype(BF16)
        dyb_ref[...] = _dot_nt(dpb, wob_ref[...]).astype(BF16)

    row = pl.BlockSpec((tm, D), lambda i: (i, 0))
    wsp = pl.BlockSpec((D, D), lambda i: (0, 0))
    return _call(
        body, name="merge_bwd", grid=(s // tm,),
        in_specs=[row, pl.BlockSpec((tm, D), lambda i: (i, 4)), pl.BlockSpec((tm, D), lambda i: (i, 5)),
                  row, row, wsp, wsp, wsp],
        out_specs=[pl.BlockSpec((tm, 2 * D), lambda i: (i, 2)), row, row, row, row],
        out_shape=[SDS((s, 6 * D), BF16)] + [SDS((s, D), BF16)] * 4, scratch_shapes=[],
        params=_cparams(("arbitrary",), 48), args=(dh1, z, z, pa, pb, w_out, w_oa, w_ob), comm=comm)


def _branch_b_bwd(dz, dyb, z, ln_g, ln_b, w_s, b_s_t, comm=None):
    s = z.shape[0]
    tb = min(T_BRANCH_B, s)

    def body(dz_in, dyb_ref, ub_ref, vb_ref, lg_ref, lb_ref, ws_ref, bs_ref,
             dz_ref, dws_ref, dbs_ref, dln_ref, du_s, dvln_s):
        del dz_in

        @pl.when(pl.program_id(0) == 0)
        def _():
            dws_ref[...] = jnp.zeros_like(dws_ref)
            dbs_ref[...] = jnp.zeros_like(dbs_ref)
            dln_ref[...] = jnp.zeros_like(dln_ref)

        lg = lg_ref[...]
        u, du, dv, rstd, vhat, vln = _sgu_common(ub_ref[...].astype(F32), vb_ref[...].astype(F32),
                                                 lg, lb_ref[...], True)
        vlnb = vln.astype(BF16)
        dyb_v = dyb_ref[...].astype(F32)
        wm = _masked_ws(ws_ref)
        keep = (lax.broadcasted_iota(jnp.int32, (CHUNK, CHUNK), 1)
                <= lax.broadcasted_iota(jnp.int32, (CHUNK, CHUNK), 0))
        bs = bs_ref[...]
        for c in range(tb // CHUNK):
            rs = slice(c * CHUNK, (c + 1) * CHUNK)
            for g in range(GROUPS):
                cs = slice(g * GROUP_DIM, (g + 1) * GROUP_DIM)
                v_blk = vlnb[rs, cs]
                sp = _dot(wm[g], v_blk) + bs[:, g:g + 1]
                d_sp = dyb_v[rs, cs] * u[rs, cs]
                d_spb = d_sp.astype(BF16)
                du_s[rs, cs] = dyb_v[rs, cs] * sp
                dvln_s[rs, cs] = _dot_tn(wm[g], d_spb)
                dws_ref[g] += jnp.where(keep, _dot_nt(d_spb, v_blk), 0.0)
                dbs_ref[g] += jnp.broadcast_to(jnp.sum(d_sp, axis=-1, keepdims=True), (CHUNK, CHUNK))
        dvln = dvln_s[...]
        dln_ref[0:1, :] += jnp.sum(dvln * vhat, axis=0, keepdims=True)
        dln_ref[1:2, :] += jnp.sum(dvln, axis=0, keepdims=True)
        dvh = dvln * lg
        d_v = rstd * (dvh - jnp.mean(dvh, axis=-1, keepdims=True)
                      - vhat * jnp.mean(dvh * vhat, axis=-1, keepdims=True))
        dz_ref[:, 0:D] = (du_s[...] * du).astype(BF16)
        dz_ref[:, D:2 * D] = (d_v * dv).astype(BF16)

    vec = pl.BlockSpec((1, D), lambda i: (0, 0))
    sq = pl.BlockSpec((GROUPS, CHUNK, CHUNK), lambda i: (0, 0, 0))
    return _call(
        body, name="branch_b_bwd", grid=(s // tb,),
        in_specs=[ANY, pl.BlockSpec((tb, D), lambda i: (i, 0)),
                  pl.BlockSpec((tb, D), lambda i: (i, 2)), pl.BlockSpec((tb, D), lambda i: (i, 3)), vec, vec, sq,
                  pl.BlockSpec((CHUNK, GROUPS), lambda i: (0, 0))],
        out_specs=[pl.BlockSpec((tb, 2 * D), lambda i: (i, 1)), sq, sq, pl.BlockSpec((8, D), lambda i: (0, 0))],
        out_shape=[SDS(dz.shape, BF16), SDS((GROUPS, CHUNK, CHUNK), F32), SDS((GROUPS, CHUNK, CHUNK), F32),
                   SDS((8, D), F32)],
        scratch_shapes=[pltpu.VMEM((tb, D), F32), pltpu.VMEM((tb, D), F32)], aliases={0: 0},
        params=_cparams(("arbitrary",), 40), args=(dz, dyb, z, z, ln_g, ln_b, w_s, b_s_t), comm=comm)


def _branch_a_bwd(dz, dya, z, hs, xc, r, ig, conv_w, w_r, w_i, lam, comm=None):
    s = z.shape[0]
    ta = min(T_BRANCH_A, s)
    nb = s // ta
    per16 = ta // 16

    def body(dz_in, dya_ref, xa_ref, ga_ref, hs_ref, hp_ref, xc_ref, r_ref, i_ref, cw_ref, wr_ref, wi_ref,
             lam_ref, dz_ref, vec_ref, dwr_ref, dwi_ref, a_s, b_s, h_s, dcar_s, acar_s, dxc_s):
        del dz_in
        i = pl.program_id(0)
        blk = nb - 1 - i

        @pl.when(i == 0)
        def _():
            dcar_s[...] = jnp.zeros_like(dcar_s)
            acar_s[...] = jnp.zeros_like(acar_s)
            dxc_s[...] = jnp.zeros_like(dxc_s)
            vec_ref[...] = jnp.zeros_like(vec_ref)
            dwr_ref[...] = jnp.zeros_like(dwr_ref)
            dwi_ref[...] = jnp.zeros_like(dwi_ref)

        cw = cw_ref[...]
        lam_v = lam_ref[...]
        xa = xa_ref[...].astype(F32)
        xcb = xc_ref[...]
        xc = xcb.astype(F32)
        sp_lam = _softplus(-lam_v)
        r_v, i_v = r_ref[...].astype(F32), i_ref[...].astype(F32)
        a_v, m_v = _decay(r_v, sp_lam)

        hs_v = hs_ref[...].astype(F32)
        hprev8 = jnp.where(blk > 0, hp_ref[...].astype(F32)[8:16], 0.0)
        h_m1 = _rows_shifted(hprev8, hs_v, 1)
        gg, dgg = _gelu_and_grad(ga_ref[...].astype(F32))
        dya_v = dya_ref[...].astype(F32)
        dz_ref[:, D:2 * D] = (dya_v * hs_v * dgg).astype(BF16)

        a_s[...] = _rows_advanced(a_v, acar_s[...], 1)
        b_s[...] = dya_v * gg

        row = lax.broadcasted_iota(jnp.int32, (8, D), 0)
        ng = ta // 8

        def group(gi, carry):
            off = pl.multiple_of((ng - 1 - gi) * 8, 8)
            c8 = a_s[pl.ds(off, 8), :]
            d8 = b_s[pl.ds(off, 8), :]
            for d in (1, 2, 4):
                c_sh = jnp.where(row < 8 - d, pltpu.roll(c8, 8 - d, 0), 1.0)
                d_sh = jnp.where(row < 8 - d, pltpu.roll(d8, 8 - d, 0), 0.0)
                d8 = c8 * d_sh + d8
                c8 = c8 * c_sh
            dh8 = d8 + c8 * carry
            h_s[pl.ds(off, 8), :] = dh8
            return jnp.broadcast_to(dh8[0:1, :], (8, D))

        dcar_s[...] = lax.fori_loop(0, ng, group, dcar_s[...])
        acar_s[...] = jnp.broadcast_to(a_v[0:1, :], (8, D))

        dbx = h_s[...]
        d_mult = dbx * xc * i_v
        d_loga = dbx * h_m1 * a_v - d_mult * (a_v * a_v) / m_v
        d_pr = d_loga * ((-LRU_C) * sp_lam) * r_v * (1.0 - r_v)
        d_pi = dbx * xc * m_v * i_v * (1.0 - i_v)
        vec_ref[7:8, :] += jnp.sum(d_loga * r_v, axis=0, keepdims=True) * (LRU_C * jax.nn.sigmoid(-lam_v))
        vec_ref[5:6, :] += jnp.sum(d_pr, axis=0, keepdims=True)
        vec_ref[6:7, :] += jnp.sum(d_pi, axis=0, keepdims=True)
        d_prb = d_pr.astype(BF16)
        d_pib = d_pi.astype(BF16)
        h_s[...] = dbx * i_v * m_v
        for h in range(HEADS):
            sl = slice(h * HEAD_DIM, (h + 1) * HEAD_DIM)
            h_s[:, sl] += _dot_nt(d_prb[:, sl], wr_ref[h]) + _dot_nt(d_pib[:, sl], wi_ref[h])
            dwr_ref[h] += _dot_tn(xcb[:, sl], d_prb[:, sl])
            dwi_ref[h] += _dot_tn(xcb[:, sl], d_pib[:, sl])
        d_xc = h_s[...]
        vec_ref[4:5, :] += jnp.sum(d_xc, axis=0, keepdims=True)
        vec_ref[0:1, :] += jnp.sum(d_xc * xa, axis=0, keepdims=True)
        d_xa = cw[0:1, :] * d_xc
        nxt = dxc_s[...]
        for k in range(1, CONV_K):
            ahead = _rows_advanced(d_xc, nxt, k)
            vec_ref[k:k + 1, :] += jnp.sum(ahead * xa, axis=0, keepdims=True)
            d_xa = d_xa + cw[k:k + 1, :] * ahead
        dz_ref[:, 0:D] = d_xa.astype(BF16)
        dxc_s[...] = d_xc[0:8, :]

    vec = pl.BlockSpec((1, D), lambda i: (0, 0))
    gate = pl.BlockSpec((HEADS, HEAD_DIM, HEAD_DIM), lambda i: (0, 0, 0))
    cur = lambda c: pl.BlockSpec((ta, D), lambda i: (nb - 1 - i, c))
    before = lambda c: pl.BlockSpec((16, D), lambda i: (jnp.maximum((nb - 1 - i) * per16 - 1, 0), c))
    return _call(
        body, name="branch_a_bwd", grid=(nb,),
        in_specs=[ANY, cur(0), cur(0), cur(1), cur(0), before(0), cur(0), cur(0), cur(0),
                  pl.BlockSpec((CONV_K, D), lambda i: (0, 0)), gate, gate, vec],
        out_specs=[pl.BlockSpec((ta, 2 * D), lambda i: (nb - 1 - i, 0)), pl.BlockSpec((8, D), lambda i: (0, 0)),
                   gate, gate],
        out_shape=[SDS(dz.shape, BF16), SDS((8, D), F32), SDS((HEADS, HEAD_DIM, HEAD_DIM), F32),
                   SDS((HEADS, HEAD_DIM, HEAD_DIM), F32)],
        scratch_shapes=[pltpu.VMEM((ta, D), F32)] * 3 + [pltpu.VMEM((8, D), F32)] * 3, aliases={0: 0},
        params=_cparams(("arbitrary",), 48),
        args=(dz, dya, z, z, hs, hs, xc, r, ig, conv_w, w_r, w_i, lam), comm=comm)


def _in_bwd(dz, w_in_g, x, dh1, g_mix, comm=None):
    s = x.shape[0]
    tm = min(TM_ROWS, s)
    nj = N_SLOT

    def body(dz_ref, w_ref, x_ref, dh1_ref, g_ref, dx_ref, dg_ref, acc_s):
        i, j = pl.program_id(0), pl.program_id(1)

        @pl.when(j == 0)
        def _():
            acc_s[...] = jnp.zeros_like(acc_s)

        @pl.when((i == 0) & (j == 0))
        def _():
            dg_ref[...] = jnp.zeros_like(dg_ref)

        acc_s[...] += _dot_nt(dz_ref[...], w_ref[...])

        @pl.when(j == nj - 1)
        def _():
            xv = x_ref[...]
            rstd = lax.rsqrt(jnp.mean(xv * xv, axis=-1, keepdims=True) + NORM_EPS)
            xh = xv * rstd
            dn = acc_s[...]
            dg_ref[...] += jnp.sum(dn * xh, axis=0, keepdims=True)
            dhat = dn * g_ref[...]
            dx_ref[...] = dh1_ref[...] + rstd * (dhat - xh * jnp.mean(dhat * xh, axis=-1, keepdims=True))

    row = pl.BlockSpec((tm, D), lambda i, j: (i, 0))
    vec = pl.BlockSpec((1, D), lambda i, j: (0, 0))
    return _call(
        body, name="in_bwd", grid=(s // tm, nj),
        in_specs=[pl.BlockSpec((tm, W_IN_COLS), lambda i, j: (i, j)),
                  pl.BlockSpec((None, D, W_IN_COLS), lambda i, j: (j, 0, 0)), row, row, vec],
        out_specs=[row, vec],
        out_shape=[SDS((s, D), F32), SDS((1, D), F32)],
        scratch_shapes=[pltpu.VMEM((tm, D), F32)],
        params=_cparams(("arbitrary", "arbitrary"), 48), args=(dz, w_in_g, x, dh1, g_mix), comm=comm)


def _wgrad(name, a, b):
    s = a.shape[0]
    ts = min(TM_ROWS, s)
    a_w, b_w = a.shape[1], b.shape[1]

    def body(a_ref, b_ref, o_ref, acc_s):
        t = pl.program_id(0)

        @pl.when(t == 0)
        def _():
            acc_s[...] = jnp.zeros_like(acc_s)

        acc_s[...] += _dot_tn(a_ref[...].astype(BF16), b_ref[...].astype(BF16))

        @pl.when(t == pl.num_programs(0) - 1)
        def _():
            o_ref[...] = acc_s[...].astype(BF16)

    return pl.pallas_call(
        body, name=name, grid=(s // ts,),
        in_specs=[pl.BlockSpec((ts, a_w), lambda t: (t, 0)), pl.BlockSpec((ts, b_w), lambda t: (t, 0))],
        out_specs=pl.BlockSpec((a_w, b_w), lambda t: (0, 0)),
        out_shape=SDS((a_w, b_w), BF16),
        scratch_shapes=[pltpu.VMEM((a_w, b_w), F32)],
        compiler_params=_cparams(("arbitrary",), 48),
    )(a, b)


def _place():
    x, y, c = lax.axis_index("x"), lax.axis_index("y"), lax.axis_index("c")
    return x, y, c


def _other_chips(x, y):
    return [(x, 1 - y, 2 * x + 1 - y), (1 - x, y, 2 * (1 - x) + y), (1 - x, 1 - y, 2 * (1 - x) + 1 - y)]


class _Plan:
    def __init__(self, arrays, out_shape, sems, start, finish, middle=None, middle_at=6):
        self.arrays, self.out_shape, self.sems, self.start, self.finish = arrays, out_shape, sems, start, finish
        self.middle, self.middle_at = middle, middle_at


def _gather_plan(shards, middle_at=6):
    n = len(shards)

    def copies(ins, outs, sems):
        send_sems, recv_sems, local_sems = sems
        x, y, c = _place()
        chip = 2 * x + y
        me = 2 * chip + c
        sib = (x, y, 1 - c)
        chips = _other_chips(x, y)

        def rc(k, t, src, blk, to):
            return pltpu.make_async_remote_copy(
                src_ref=src, dst_ref=outs[t].at[blk], send_sem=send_sems.at[k * n + t],
                recv_sem=recv_sems.at[k * n + t], device_id=to, device_id_type=MESH)

        (yx, yy, y_chip), (xx, xy, x_chip), _ = chips
        local = [pltpu.make_async_copy(ins[t], outs[t].at[me], local_sems.at[t]) for t in range(n)]
        sends = ([rc(0, t, ins[t], me, sib) for t in range(n)] + [rc(1, t, ins[t], me, (yx, yy, c)) for t in range(n)]
                 + [rc(2, t, ins[t], me, (xx, xy, c)) for t in range(n)])
        passed = [[rc(4 + j, t, outs[t].at[2 * pc + c], 2 * pc + c, sib) for t in range(n)]
                  for j, (_, _, pc) in enumerate(chips)]
        relays = [[rc(3, t, outs[t].at[2 * y_chip + c], 2 * y_chip + c, (xx, xy, c)) for t in range(n)],
                  [rc(3, t, outs[t].at[2 * x_chip + c], 2 * x_chip + c, (yx, yy, c)) for t in range(n)]]
        return rc, local, sends, passed, relays, chips, chip, c, sib

    def start(ins, outs, sems):
        _, local, sends, _, _, _, _, _, _ = copies(ins, outs, sems)
        for cp in local + sends:
            cp.start()

    def middle(ins, outs, sems):
        rc, _, _, passed, relays, chips, _, c, sib = copies(ins, outs, sems)
        for j in range(2):
            for t in range(n):
                rc(1 + j, t, ins[t], 2 * chips[j][2] + c, sib).wait_recv()
        for j in range(2):
            for cp in passed[j]:
                cp.start()

            @pl.when(c == j)
            def _():
                for cp in relays[j]:
                    cp.start()

    def finish(ins, outs, sems):
        rc, local, sends, passed, relays, chips, chip, c, sib = copies(ins, outs, sems)
        far = 2 * chips[2][2] + c
        for t in range(n):
            rc(3, t, ins[t], far, sib).wait_recv()
        for cp in passed[2]:
            cp.start()
        for t in range(n):
            rc(0, t, ins[t], 2 * chip + 1 - c, sib).wait_recv()
        for j, (px, py, pc) in enumerate(chips):
            for t in range(n):
                rc(4 + j, t, ins[t], 2 * pc + 1 - c, sib).wait_recv()
        for cp in sends + passed[0] + passed[1] + passed[2]:
            cp.wait_send()
        for j in range(2):
            @pl.when(c == j)
            def _():
                for cp in relays[j]:
                    cp.wait_send()
        for cp in local:
            cp.wait()

    return _Plan(list(shards), [SDS((N_SLOT,) + tuple(a.shape), a.dtype) for a in shards],
                 [pltpu.SemaphoreType.DMA((7 * n,)), pltpu.SemaphoreType.DMA((7 * n,)),
                  pltpu.SemaphoreType.DMA((n,))], start, finish, middle, middle_at)


def _sibling_plan(grads, whole=()):
    n, m = len(grads), len(whole)

    def copies(ins, outs, sems):
        send_sems, recv_sems = sems
        x, y, c = _place()
        sib = (x, y, 1 - c)

        def rc(t, src, dst):
            return pltpu.make_async_remote_copy(src_ref=src, dst_ref=dst, send_sem=send_sems.at[t],
                                                recv_sem=recv_sems.at[t], device_id=sib, device_id_type=MESH)
        return rc, c

    def start(ins, outs, sems):
        rc, c = copies(ins, outs, sems)
        for t in range(n):
            for j in range(4):
                rc(t, ins[t].at[2 * j + 1 - c], outs[t].at[j]).start()
        for t in range(n, n + m):
            rc(t, ins[t], outs[t]).start()

    def finish(ins, outs, sems):
        rc, _ = copies(ins, outs, sems)
        for t in range(n):
            rc(t, ins[t].at[pl.ds(0, 4)], outs[t]).wait()
        for t in range(n, n + m):
            rc(t, ins[t], outs[t]).wait()

    return _Plan(list(grads) + list(whole),
                 [SDS((4,) + tuple(g.shape[1:]), g.dtype) for g in grads] + [SDS(a.shape, a.dtype) for a in whole],
                 [pltpu.SemaphoreType.DMA((n + m,)), pltpu.SemaphoreType.DMA((n + m,))], start, finish)


def _chips_plan(parts, whole=()):
    n, m = len(parts), len(whole)

    def src_of(ins, t, pc):
        return ins[t].at[pc] if t < n else ins[t]

    def local_copies(ins, outs, sems, chip):
        return [pltpu.make_async_copy(src_of(ins, t, chip), outs[t].at[chip], sems[2].at[t]) for t in range(n + m)]

    def start(ins, outs, sems):
        send_sems, recv_sems, _ = sems
        x, y, c = _place()
        chip = 2 * x + y
        for cp in local_copies(ins, outs, sems, chip):
            cp.start()
        for px, py, pc in _other_chips(x, y):
            for t in range(n + m):
                pltpu.make_async_remote_copy(src_ref=src_of(ins, t, pc), dst_ref=outs[t].at[chip],
                                             send_sem=send_sems.at[t], recv_sem=recv_sems.at[t],
                                             device_id=(px, py, c), device_id_type=MESH).start()

    def finish(ins, outs, sems):
        send_sems, recv_sems, _ = sems
        x, y, c = _place()
        for t in range(n + m):
            three = outs[t].at[pl.ds(0, 3)]
            pltpu.make_async_remote_copy(src_ref=three, dst_ref=three, send_sem=send_sems.at[t],
                                         recv_sem=recv_sems.at[t], device_id=(x, y, c), device_id_type=MESH).wait()
        for cp in local_copies(ins, outs, sems, 2 * x + y):
            cp.wait()

    return _Plan(list(parts) + list(whole),
                 [SDS(p.shape, p.dtype) for p in parts] + [SDS((4,) + tuple(a.shape), a.dtype) for a in whole],
                 [pltpu.SemaphoreType.DMA((n + m,)), pltpu.SemaphoreType.DMA((n + m,)),
                  pltpu.SemaphoreType.DMA((n + m,))], start, finish)


def _exchange_plan(arr):
    def peers(x, y, c):
        flip = lambda v, f: 1 - v if f else v
        return [(flip(x, fx), flip(y, fy), flip(c, fc))
                for fx in (0, 1) for fy in (0, 1) for fc in (0, 1) if fx or fy or fc]

    def start(ins, outs, sems):
        x, y, c = _place()
        me = 4 * x + 2 * y + c
        pltpu.make_async_copy(ins[0], outs[0].at[me], sems[2].at[0]).start()
        for to in peers(x, y, c):
            pltpu.make_async_remote_copy(src_ref=ins[0], dst_ref=outs[0].at[me], send_sem=sems[0].at[0],
                                         recv_sem=sems[1].at[0], device_id=to, device_id_type=MESH).start()

    def finish(ins, outs, sems):
        x, y, c = _place()
        seven = outs[0].at[pl.ds(0, 7)]
        pltpu.make_async_remote_copy(src_ref=seven, dst_ref=seven, send_sem=sems[0].at[0], recv_sem=sems[1].at[0],
                                     device_id=(x, y, c), device_id_type=MESH).wait()
        pltpu.make_async_copy(ins[0], outs[0].at[4 * x + 2 * y + c], sems[2].at[0]).wait()

    return _Plan([arr], [SDS((N_SLOT,) + tuple(arr.shape), arr.dtype)],
                 [pltpu.SemaphoreType.DMA((1,)), pltpu.SemaphoreType.DMA((1,)), pltpu.SemaphoreType.DMA((1,))],
                 start, finish)


def _join(*plans):
    def cut(seq, sizes):
        out, at = [], 0
        for k in sizes:
            out.append(seq[at:at + k])
            at += k
        return out

    n_arr = [len(p.arrays) for p in plans]
    n_sem = [len(p.sems) for p in plans]

    def start(ins, outs, sems):
        for p, i, o, s in zip(plans, cut(ins, n_arr), cut(outs, n_arr), cut(sems, n_sem)):
            p.start(i, o, s)

    def finish(ins, outs, sems):
        for p, i, o, s in zip(plans, cut(ins, n_arr), cut(outs, n_arr), cut(sems, n_sem)):
            p.finish(i, o, s)

    def middle(ins, outs, sems):
        for p, i, o, s in zip(plans, cut(ins, n_arr), cut(outs, n_arr), cut(sems, n_sem)):
            if p.middle is not None:
                p.middle(i, o, s)

    return _Plan([a for p in plans for a in p.arrays], [o for p in plans for o in p.out_shape],
                 [s for p in plans for s in p.sems], start, finish,
                 middle if any(p.middle is not None for p in plans) else None,
                 max(p.middle_at for p in plans))


def _run_plan(name, plan):
    k = len(plan.arrays)

    def body(*refs):
        ins, outs, sems = refs[:k], refs[k:2 * k], refs[2 * k:]
        plan.start(ins, outs, sems)
        if plan.middle is not None:
            plan.middle(ins, outs, sems)
        plan.finish(ins, outs, sems)

    return pl.pallas_call(
        body, name=name, in_specs=[ANY] * k, out_specs=[ANY] * k, out_shape=plan.out_shape,
        scratch_shapes=plan.sems, compiler_params=pltpu.CompilerParams(has_side_effects=True),
    )(*plan.arrays)


def _call(body, *, name, grid, in_specs, out_specs, out_shape, scratch_shapes, params, args, comm=None,
          aliases=None, prefetch=()):
    aliases = aliases or {}
    n_pre = len(prefetch)

    def launch(fn, ins_specs, outs_specs, outs_shape, scratch, operands):
        spec = pltpu.PrefetchScalarGridSpec(num_scalar_prefetch=n_pre, grid=grid, in_specs=ins_specs,
                                            out_specs=outs_specs, scratch_shapes=scratch)
        return pl.pallas_call(fn, name=name, grid_spec=spec, out_shape=outs_shape, compiler_params=params,
                              input_output_aliases=aliases)(*prefetch, *[_small_in_hbm(a) for a in operands])

    if comm is None:
        return list(launch(body, in_specs, out_specs, out_shape, scratch_shapes, args)), []
    n_in, n_out, n_scr, k = len(in_specs), len(out_specs), len(scratch_shapes), len(comm.arrays)

    def wrapped(*refs):
        pre, refs = refs[:n_pre], refs[n_pre:]
        ins = refs[:n_in]
        c_in = refs[n_in:n_in + k]
        outs = refs[n_in + k:n_in + k + n_out]
        c_out = refs[n_in + k + n_out:n_in + 2 * k + n_out]
        scr = refs[n_in + 2 * k + n_out:n_in + 2 * k + n_out + n_scr]
        sems = refs[n_in + 2 * k + n_out + n_scr:]
        step, steps = pl.program_id(0), grid[0]
        for d in range(1, len(grid)):
            step, steps = step * grid[d] + pl.program_id(d), steps * grid[d]

        @pl.when(step == 0)
        def _():
            comm.start(c_in, c_out, sems)

        if comm.middle is not None:
            @pl.when(step == (comm.middle_at * steps) // 8)
            def _():
                comm.middle(c_in, c_out, sems)

        body(*pre, *ins, *outs, *scr)

        @pl.when(step == steps - 1)
        def _():
            comm.finish(c_in, c_out, sems)

    res = launch(wrapped, list(in_specs) + [ANY] * k, list(out_specs) + [ANY] * k,
                 list(out_shape) + list(comm.out_shape), list(scratch_shapes) + list(comm.sems),
                 tuple(args) + tuple(comm.arrays))
    return list(res[:n_out]), list(res[n_out:])


def _wgrad_paired(name, a_t, b, core, comm=None):
    m, s = a_t.shape
    cols = b.shape[1] // N_SLOT
    half = N_SLOT // 2

    def owner(k, c):
        return 2 * (k % half) + jnp.where(k < half, 1 - c, c)

    def body(c_ref, a_ref, b_ref, sum_ref, out_s, recv_s, send_sems, recv_sems):
        del c_ref
        k = pl.program_id(0)
        x, y, c = _place()

        def to_sibling(j):
            return pltpu.make_async_remote_copy(src_ref=out_s.at[j % 2], dst_ref=recv_s.at[j],
                                                send_sem=send_sems.at[j], recv_sem=recv_sems.at[j],
                                                device_id=(x, y, 1 - c), device_id_type=MESH)

        @pl.when((k >= 2) & (k < half + 2))
        def _():
            to_sibling(k - 2).wait_send()

        @pl.when(k < half)
        def _():
            out_s[k % 2] = _dot(a_ref[...], b_ref[...]).astype(BF16)
            to_sibling(k).start()

        @pl.when(k >= half)
        def _():
            to_sibling(k - half).wait_recv()
            sum_ref[...] = (_dot(a_ref[...], b_ref[...]) + recv_s[k - half].astype(F32)).astype(BF16)

    return _call(body, name=name, grid=(N_SLOT,),
                 in_specs=[pl.BlockSpec((m, s), lambda k, c_ref: (0, 0)),
                           pl.BlockSpec((s, cols), lambda k, c_ref: (0, owner(k, c_ref[0])))],
                 out_specs=[pl.BlockSpec((None, m, cols), lambda k, c_ref: (jnp.maximum(k - half, 0), 0, 0))],
                 out_shape=[SDS((half, m, cols), BF16)],
                 scratch_shapes=[pltpu.VMEM((2, m, cols), BF16), pltpu.VMEM((half, m, cols), BF16),
                                 pltpu.SemaphoreType.DMA((half,)), pltpu.SemaphoreType.DMA((half,))],
                 params=_cparams(("arbitrary",), 56), args=(a_t, b), comm=comm, prefetch=(core,))


def _wgrad_down(act_t, dh2, core, tokens):
    s = dh2.shape[0]
    ts = min(tokens, s)
    half = N_SLOT // 2

    def body(c_ref, a_ref, b_ref, keep_ref, recv_ref, acc_s, out_s, send_sems, recv_sems):
        k, t = pl.program_id(0), pl.program_id(1)
        x, y, c = _place()
        last = t == pl.num_programs(1) - 1

        def to_sibling(j):
            return pltpu.make_async_remote_copy(src_ref=out_s.at[j % 2], dst_ref=recv_ref.at[j],
                                                send_sem=send_sems.at[j], recv_sem=recv_sems.at[j],
                                                device_id=(x, y, 1 - c), device_id_type=MESH)

        @pl.when(t == 0)
        def _():
            acc_s[...] = jnp.zeros_like(acc_s)

        acc_s[...] += _dot(a_ref[...], b_ref[...].astype(BF16)).reshape(acc_s.shape)

        @pl.when(last & (k >= 2))
        def _():
            to_sibling(k - 2).wait_send()

        @pl.when(last)
        def _():
            keep_ref[...] = acc_s[c_ref[0]].astype(BF16)
            out_s[k % 2] = acc_s[1 - c_ref[0]].astype(BF16)
            to_sibling(k).start()

        @pl.when(last & (k == half - 1))
        def _():
            for j in (half - 2, half - 1):
                to_sibling(j).wait_send()
            for j in range(half):
                to_sibling(j).wait_recv()

    blk = SDS((half, FF_COLS, D), BF16)
    (kept, recv), _ = _call(
        body, name="wgrad_down", grid=(half, s // ts),
        in_specs=[pl.BlockSpec((2 * FF_COLS, ts), lambda k, t, c_ref: (k, t)),
                  pl.BlockSpec((ts, D), lambda k, t, c_ref: (t, 0))],
        out_specs=[pl.BlockSpec((None, FF_COLS, D), lambda k, t, c_ref: (k, 0, 0)), ANY], out_shape=[blk, blk],
        scratch_shapes=[pltpu.VMEM((2, FF_COLS, D), F32), pltpu.VMEM((2, FF_COLS, D), BF16),
                        pltpu.SemaphoreType.DMA((half,)), pltpu.SemaphoreType.DMA((half,))],
        params=_cparams(("arbitrary", "arbitrary"), 48), args=(act_t, dh2), prefetch=(core,))
    return kept, recv


def _row_tile(rows):
    for t in (512, 256, 128, 64, 32, 16, 8):
        if rows % t == 0:
            return t
    return rows


def _pair_sum(name, g8, recv4, core):
    _, rows, cols = recv4.shape
    tr = _row_tile(rows)
    per = g8.shape[0] // 4
    g42 = g8.reshape(4, per, rows, cols)

    def body(c_ref, g_ref, r_ref, o_ref):
        del c_ref
        o_ref[...] = (g_ref[...].astype(F32) + r_ref[...].astype(F32)).astype(o_ref.dtype)

    return pl.pallas_call(
        body, name=name,
        grid_spec=pltpu.PrefetchScalarGridSpec(
            num_scalar_prefetch=1, grid=(4, rows // tr),
            in_specs=[pl.BlockSpec((None, None, tr, cols), lambda j, i, c_ref: (j, c_ref[0] * (per - 1), i, 0)),
                      pl.BlockSpec((None, tr, cols), lambda j, i, c_ref: (j, i, 0))],
            out_specs=pl.BlockSpec((None, tr, cols), lambda j, i, c_ref: (j, i, 0))),
        out_shape=SDS(recv4.shape, g8.dtype),
        compiler_params=_cparams(("arbitrary", "arbitrary"), 32),
    )(core, g42, recv4)


def _add2(name, a, b):
    rows, cols = a.shape
    tr = _row_tile(rows)

    def body(a_ref, b_ref, o_ref):
        o_ref[...] = a_ref[...] + b_ref[...]

    blk = pl.BlockSpec((tr, cols), lambda i: (i, 0))
    return pl.pallas_call(body, name=name, grid=(rows // tr,), in_specs=[blk, blk], out_specs=blk,
                          out_shape=SDS(a.shape, a.dtype),
                          compiler_params=_cparams(("arbitrary",), 32))(a, b)


def _sum_terms(name, terms):
    k, rows, cols = terms.shape
    tr = _row_tile(rows)

    def body(r_ref, o_ref):
        acc = r_ref[0]
        for q in range(1, k):
            acc = acc + r_ref[q]
        o_ref[...] = acc

    return pl.pallas_call(body, name=name, grid=(rows // tr,),
                          in_specs=[pl.BlockSpec((k, tr, cols), lambda i: (0, i, 0))],
                          out_specs=pl.BlockSpec((tr, cols), lambda i: (i, 0)),
                          out_shape=SDS((rows, cols), terms.dtype),
                          compiler_params=_cparams(("arbitrary",), 32))(terms)


def _adam_update(g, w, m, v):
    c1 = 1.0 / (1.0 - ADAM_B1 ** ADAM_STEP)
    c2 = 1.0 / (1.0 - ADAM_B2 ** ADAM_STEP)
    mn = ADAM_B1 * m + (1.0 - ADAM_B1) * g
    vn = ADAM_B2 * v + (1.0 - ADAM_B2) * (g * g)
    delta = (-ADAM_LR) * ((mn * c1) / (jnp.sqrt(vn * c2) + ADAM_EPS) + ADAM_WD * w)
    return delta, mn, vn


def _adamw_many(name, gs, ws, ms, vs):
    n = len(gs)

    def body(*refs):
        for p in range(n):
            g, w, m, v = (refs[q * n + p][...] for q in range(4))
            d, mn, vn = _adam_update(g, w, m, v)
            refs[4 * n + p][...] = d
            refs[5 * n + p][...] = mn
            refs[6 * n + p][...] = vn

    full = [pl.BlockSpec(w.shape, lambda i: (0, 0)) for w in ws]
    shapes = [SDS(w.shape, F32) for w in ws]
    res = pl.pallas_call(body, name=name, grid=(1,), in_specs=full * 4, out_specs=full * 3, out_shape=shapes * 3,
                         compiler_params=_cparams(("arbitrary",), 32),
                         )(*[_small_in_hbm(a) for a in (*gs, *ws, *ms, *vs)])
    return [(res[p], res[n + p], res[2 * n + p]) for p in range(n)]


def _adamw(name, terms, w, m, v):
    k, rows, cols = terms.shape
    tr = _row_tile(rows)

    def body(t_ref, w_ref, m_ref, v_ref, g_ref, d_ref, mo_ref, vo_ref):
        g = t_ref[0].astype(F32)
        for q in range(1, k):
            g = g + t_ref[q].astype(F32)
        g_ref[...] = g
        d_ref[...], mo_ref[...], vo_ref[...] = _adam_update(g, w_ref[...], m_ref[...], v_ref[...])

    blk = pl.BlockSpec((tr, cols), lambda i: (i, 0))
    return pl.pallas_call(body, name=name, grid=(rows // tr,),
                          in_specs=[pl.BlockSpec((k, tr, cols), lambda i: (0, i, 0)), blk, blk, blk],
                          out_specs=[blk] * 4, out_shape=[SDS((rows, cols), F32)] * 4,
                          compiler_params=_cparams(("arbitrary",), 40),
                          )(*[pltpu.with_memory_space_constraint(a, pltpu.HBM) for a in (terms, w, m, v)])


def kernel(x, norm_mix_g, w_in, conv_w, conv_b, w_rgate, b_rgate, w_igate, b_igate, lru_lambda, w_out_a, sgu_ln_g, sgu_ln_b, sgu_w_s, sgu_b_s, w_out_b, w_out, norm_mlp_g, w_up, w_down, norm_final_g, loss_target, m_norm_mix_g, m_w_in, m_conv_w, m_conv_b, m_w_rgate, m_b_rgate, m_w_igate, m_b_igate, m_lru_lambda, m_w_out_a, m_sgu_ln_g, m_sgu_ln_b, m_sgu_w_s, m_sgu_b_s, m_w_out_b, m_w_out, m_norm_mlp_g, m_w_up, m_w_down, m_norm_final_g, v_norm_mix_g, v_w_in, v_conv_w, v_conv_b, v_w_rgate, v_b_rgate, v_w_igate, v_b_igate, v_lru_lambda, v_w_out_a, v_sgu_ln_g, v_sgu_ln_b, v_sgu_w_s, v_sgu_b_s, v_w_out_b, v_w_out, v_norm_mlp_g, v_w_up, v_w_down, v_norm_final_g):
    cx, cy, cc = _place()
    me = 4 * cx + 2 * cy + cc
    core = jnp.reshape(cc, (1,)).astype(jnp.int32)
    xs = x[0]
    tgt = loss_target[0]

    gate_shard = jnp.stack([w_rgate[0], w_igate[0]]).astype(BF16).reshape(2 * HEADS * 32, HEAD_DIM)
    vec_shard = jnp.concatenate([conv_w[0], b_rgate[0], b_igate[0]], axis=1)
    vec_shard = jnp.pad(vec_shard, ((0, 4), (0, 256 - vec_shard.shape[1])))
    shards = [w_in[0].astype(BF16), w_out_a[0].astype(BF16), w_out_b[0].astype(BF16), w_out[0].astype(BF16),
              w_up[0].astype(BF16), w_down[0].astype(BF16), gate_shard, vec_shard]
    (z, n1_t, w_in_g), (gate_g, vec_g) = _in_proj(xs, norm_mix_g, shards[0], _slot_order(cx, cy, cc),
                                                comm=_gather_plan(shards[6:8]))
    gates = gate_g.reshape(N_SLOT, 2, HEADS, 32, HEAD_DIM).transpose(1, 2, 0, 3, 4).reshape(2, HEADS, HEAD_DIM, HEAD_DIM)
    w_r_f, w_i_f = gates[0], gates[1]
    conv_w_f = vec_g[:, 0:4, 0:128].transpose(1, 0, 2).reshape(CONV_K, D)
    b_r_f = vec_g[:, 0:4, 128:160].transpose(1, 0, 2).reshape(1, D)
    b_i_f = vec_g[:, 0:4, 160:192].transpose(1, 0, 2).reshape(1, D)
    b_s_t = jnp.transpose(sgu_b_s[0])

    (ya, hs, xc, r_gate, i_gate), (w_oa_g, w_ob_g, w_out_g, w_up_g) = _branch_a_fwd(
        z, conv_w_f, conv_b, w_r_f, b_r_f, w_i_f, b_i_f, lru_lambda, comm=_gather_plan(shards[1:5]))
    w_oa_f = w_oa_g.reshape(D, D)
    w_ob_f = w_ob_g.reshape(D, D)
    w_out_f = w_out_g.reshape(D, D)
    (yb,), _ = _branch_b_fwd(z, sgu_ln_g, sgu_ln_b, sgu_w_s[0], b_s_t)
    (pa, pb, merged, h1), (w_down_g,) = _merge_out(ya, yb, z, xs, w_oa_f, w_ob_f, w_out_f,
                                                   comm=_gather_plan(shards[5:6], middle_at=4))
    w_down_f = w_down_g.reshape(N_SLOT * FF_COLS, D)
    r_act, act_t, n2_t, dh2, loss_acc, d_gfin = _mlp_fwd(h1, norm_mlp_g, w_up_g, w_down_f,
                                                         norm_final_g.reshape(1, D), tgt)

    def pair(names, grads, recv):
        return [_pair_sum("pair_sum_" + nm, g, r, core) for nm, g, r in zip(names, grads, recv)]

    g_down, r_down = _wgrad_down(act_t, dh2, core, 2048)
    (p_down,) = pair(["down"], [g_down], [r_down])
    (df, dh1, d_gmlp), (got_down,) = _mlp_bwd(dh2, r_act, w_down_f, w_up_g, h1, norm_mlp_g,
                                              comm=_chips_plan([p_down]))
    (p_up,), _ = _wgrad_paired("wgrad_up", n2_t, df, core)
    g_out = _wgrad("wgrad_out", merged, dh1).reshape(N_SLOT, D // N_SLOT, D)
    (dz, dpa, dpb, dya, dyb), (r_out,) = _merge_bwd(
        dh1, z, pa, pb, w_out_f, w_oa_f, w_ob_f, comm=_sibling_plan([g_out]))
    (p_out,) = pair(["out"], [g_out], [r_out])
    g_oa = _wgrad("wgrad_out_a", ya, dpa).reshape(N_SLOT, D // N_SLOT, D)
    g_ob = _wgrad("wgrad_out_b", yb, dpb).reshape(N_SLOT, D // N_SLOT, D)
    (dz, d_ws, d_bs, d_ln), (got_out, r_oa, r_ob) = _branch_b_bwd(
        dz, dyb, z, sgu_ln_g, sgu_ln_b, sgu_w_s[0], b_s_t,
        comm=_join(_chips_plan([p_out]), _sibling_plan([g_oa, g_ob])))
    p_oa, p_ob = pair(["out_a", "out_b"], [g_oa, g_ob], [r_oa, r_ob])
    (dz, d_vec, d_wr, d_wi), (got_up, got_oa, got_ob) = _branch_a_bwd(
        dz, dya, z, hs, xc, r_gate, i_gate, conv_w_f, w_r_f, w_i_f, lru_lambda,
        comm=_chips_plan([p_up, p_oa, p_ob]))
    g_gate = jnp.stack([d_wr, d_wi]).reshape(2, HEADS, N_SLOT, 32, HEAD_DIM).transpose(2, 0, 1, 3, 4)
    g_gate = g_gate.reshape(N_SLOT, 2 * HEADS * 32, HEAD_DIM).astype(BF16)

    d_bs_row = jnp.pad(d_bs[:, :, 0].reshape(1, GROUPS * CHUNK), ((0, 0), (0, D - GROUPS * CHUNK)))
    vecs = jnp.concatenate([d_vec, jnp.concatenate([d_ln[0:2], d_gmlp, d_gfin, d_bs_row, jnp.zeros((3, D), F32)])])
    d_ws2 = d_ws.reshape(GROUPS * CHUNK, CHUNK)
    (p_in,), (r_gate, r_vecs, r_ws) = _wgrad_paired("wgrad_in", n1_t, dz, core,
                                                    comm=_sibling_plan([g_gate], [vecs, d_ws2]))
    (p_gate,) = pair(["gate"], [g_gate], [r_gate])
    vecs_chip = _add2("pair_sum_vecs", vecs, r_vecs)
    ws_chip = _add2("pair_sum_ws", d_ws2, r_ws)
    (dx, d_gmix), (got_in, got_gate, got_vecs, got_ws) = _in_bwd(
        dz, w_in_g, xs, dh1, norm_mix_g, comm=_chips_plan([p_in, p_gate], [vecs_chip, ws_chip]))
    vecs_sum = _sum_terms("sum_vecs", got_vecs)
    last = jnp.concatenate([d_gmix, jnp.pad(loss_acc[0:1], ((0, 0), (0, D - 128))), jnp.zeros((6, D), F32)])
    (last_all,) = _run_plan("exchange_last", _exchange_plan(last))
    last_sum = _sum_terms("sum_last", last_all)
    loss = last_sum[1, 0]
    got = [got_in, got_oa, got_ob, got_out, got_up, got_down, got_gate]

    def step(nm, terms, w, m, v, rows, cols):
        g, d, mn, vn = _adamw("adamw_" + nm, terms.reshape(4, rows, cols), w.reshape(rows, cols),
                              m.reshape(rows, cols), v.reshape(rows, cols))
        return [a.reshape(w.shape) for a in (g, d, mn, vn)]

    o_in = step("in", got[0], w_in, m_w_in, v_w_in, D, W_IN_COLS)
    o_oa = step("out_a", got[1], w_out_a, m_w_out_a, v_w_out_a, D // N_SLOT, D)
    o_ob = step("out_b", got[2], w_out_b, m_w_out_b, v_w_out_b, D // N_SLOT, D)
    o_out = step("out", got[3], w_out, m_w_out, v_w_out, D // N_SLOT, D)
    o_up = step("up", got[4], w_up, m_w_up, v_w_up, D, FF_COLS)
    o_down = step("down", got[5], w_down, m_w_down, v_w_down, FF_COLS, D)
    gate_w = jnp.stack([w_rgate[0], w_igate[0]]).reshape(2 * HEADS * 32, HEAD_DIM)
    gate_m = jnp.stack([m_w_rgate[0], m_w_igate[0]]).reshape(2 * HEADS * 32, HEAD_DIM)
    gate_v = jnp.stack([v_w_rgate[0], v_w_igate[0]]).reshape(2 * HEADS * 32, HEAD_DIM)
    o_gate = _adamw("adamw_gate", got[6], gate_w, gate_m, gate_v)
    o_gate = [a.reshape(2, 1, HEADS, 32, HEAD_DIM) for a in o_gate]
    o_wr = [a[0] for a in o_gate]
    o_wi = [a[1] for a in o_gate]

    def own(full, width):
        return lax.dynamic_slice_in_dim(full, me * width, width, axis=1)

    small_g = {
        "norm_mix_g": last_sum[0:1], "conv_w": own(vecs_sum[0:4], 128), "conv_b": vecs_sum[4:5],
        "b_rgate": own(vecs_sum[5:6].reshape(HEADS, HEAD_DIM), 32),
        "b_igate": own(vecs_sum[6:7].reshape(HEADS, HEAD_DIM), 32),
        "lru_lambda": vecs_sum[7:8], "sgu_ln_g": vecs_sum[8:9], "sgu_ln_b": vecs_sum[9:10],
        "norm_mlp_g": vecs_sum[10:11], "norm_final_g": vecs_sum[11:12],
        "sgu_b_s": vecs_sum[12, 0:GROUPS * CHUNK].reshape(GROUPS, CHUNK),
    }
    small_w = {"norm_mix_g": (norm_mix_g, m_norm_mix_g, v_norm_mix_g), "conv_w": (conv_w, m_conv_w, v_conv_w),
               "conv_b": (conv_b, m_conv_b, v_conv_b), "b_rgate": (b_rgate, m_b_rgate, v_b_rgate),
               "b_igate": (b_igate, m_b_igate, v_b_igate), "lru_lambda": (lru_lambda, m_lru_lambda, v_lru_lambda),
               "sgu_ln_g": (sgu_ln_g, m_sgu_ln_g, v_sgu_ln_g), "sgu_ln_b": (sgu_ln_b, m_sgu_ln_b, v_sgu_ln_b),
               "norm_mlp_g": (norm_mlp_g, m_norm_mlp_g, v_norm_mlp_g),
               "norm_final_g": (norm_final_g, m_norm_final_g, v_norm_final_g),
               "sgu_b_s": (sgu_b_s, m_sgu_b_s, v_sgu_b_s), "sgu_w_s": (sgu_w_s, m_sgu_w_s, v_sgu_w_s)}
    order = list(small_g)
    as2d = lambda k, a: a.reshape(small_g[k].shape)
    upd = _adamw_many("adamw_small", [small_g[k] for k in order], *[[as2d(k, small_w[k][q]) for k in order]
                                                                     for q in range(3)])
    o_small = {k: [a.reshape(small_w[k][0].shape) for a in (small_g[k],) + u] for k, u in zip(order, upd)}
    ws3 = [a[0].reshape(GROUPS * CHUNK, CHUNK) for a in small_w.pop("sgu_w_s")]
    o_small["sgu_w_s"] = [a.reshape(sgu_w_s.shape) for a in _adamw("adamw_ws", got_ws, *ws3)]

    per_weight = {"norm_mix_g": o_small["norm_mix_g"], "w_in": o_in, "conv_w": o_small["conv_w"],
                  "conv_b": o_small["conv_b"], "w_rgate": o_wr, "b_rgate": o_small["b_rgate"], "w_igate": o_wi,
                  "b_igate": o_small["b_igate"], "lru_lambda": o_small["lru_lambda"], "w_out_a": o_oa,
                  "sgu_ln_g": o_small["sgu_ln_g"], "sgu_ln_b": o_small["sgu_ln_b"], "sgu_w_s": o_small["sgu_w_s"],
                  "sgu_b_s": o_small["sgu_b_s"], "w_out_b": o_ob, "w_out": o_out, "norm_mlp_g": o_small["norm_mlp_g"],
                  "w_up": o_up, "w_down": o_down, "norm_final_g": o_small["norm_final_g"]}
    names_w = list(per_weight)
    return (loss, dx[None], *[per_weight[k][0] for k in names_w], *[per_weight[k][1] for k in names_w],
            *[per_weight[k][2] for k in names_w], *[per_weight[k][3] for k in names_w])
```

```python
import jax
import jax.numpy as jnp
from jax import lax
from jax.experimental import pallas as pl
from jax.experimental.pallas import tpu as pltpu

F32 = jnp.float32
BF16 = jnp.bfloat16
SDS = jax.ShapeDtypeStruct
MESH = pl.DeviceIdType.MESH
ANY = pl.BlockSpec(memory_space=pl.ANY)

D = 1024
N_SLOT = 8
W_IN_COLS = 768
FF_COLS = 512
HEADS, HEAD_DIM = 4, 256
GROUPS, GROUP_DIM = 4, 256
CHUNK = 128
CONV_K = 4
NORM_EPS = 1e-6
LN_EPS = 1e-5
LRU_C = 8.0
ADAM_LR, ADAM_B1, ADAM_B2, ADAM_EPS, ADAM_WD, ADAM_STEP = 0.001, 0.9, 0.999, 1e-08, 0.01, 10

TM_ROWS = 1024
TM_MERGE = 512
T_BRANCH_A = 512
T_BRANCH_B = 256
MiB = 1024 * 1024
SMALL_OPERAND = 16 * 1024

_GELU_C = 0.7978845608028654
_GELU_A = 0.044715


def _small_in_hbm(a):
    return pltpu.with_memory_space_constraint(a, pltpu.HBM) if a.size <= SMALL_OPERAND else a


def _cparams(sem, vmem_mib):
    return pltpu.CompilerParams(dimension_semantics=sem, vmem_limit_bytes=vmem_mib * MiB)


def _gelu(x):
    t = jnp.tanh(_GELU_C * (x + _GELU_A * x * x * x))
    return 0.5 * x * (1.0 + t)


def _gelu_and_grad(x):
    x2 = x * x
    t = jnp.tanh(_GELU_C * x * (1.0 + _GELU_A * x2))
    g = 0.5 * x * (1.0 + t)
    dg = 0.5 * (1.0 + t) + 0.5 * x * (1.0 - t * t) * _GELU_C * (1.0 + 3.0 * _GELU_A * x2)
    return g, dg


def _softplus(x):
    return jnp.maximum(x, 0.0) + jnp.log1p(jnp.exp(-jnp.abs(x)))


def _dot(a, b):
    return jnp.dot(a, b, preferred_element_type=F32)


def _dot_nt(a, b):
    return lax.dot_general(a, b, (((1,), (1,)), ((), ())), preferred_element_type=F32)


def _dot_tn(a, b):
    return lax.dot_general(a, b, (((0,), (0,)), ((), ())), preferred_element_type=F32)


def _rows_shifted(prev8, cur, k):
    ext = jnp.concatenate([prev8, cur], axis=0)
    return pltpu.roll(ext, k, 0)[8:]


def _rows_advanced(cur, next8, k):
    t = cur.shape[0]
    ext = jnp.concatenate([cur, next8], axis=0)
    return pltpu.roll(ext, t + 8 - k, 0)[:t]


def _first_second(x, y, c):
    ny, nx, far = _other_chips(x, y)
    pick = lambda a, b: a * (1 - c) + b * c
    first = tuple(pick(a, b) for a, b in zip(ny, nx))
    second = tuple(pick(b, a) for a, b in zip(ny, nx))
    return first, second, far


def _slot_order(x, y, c):
    chip = 2 * x + y
    first, second, far = _first_second(x, y, c)
    order = [2 * chip + c, 2 * chip + 1 - c, 2 * first[2] + c, 2 * second[2] + 1 - c, 2 * second[2] + c,
             2 * first[2] + 1 - c, 2 * far[2] + c, 2 * far[2] + 1 - c]
    return jnp.stack(order).astype(jnp.int32)


def _in_proj(x, g_mix, w_in_own, order, comm=None):
    s = x.shape[0]
    tm = min(TM_ROWS, s)
    ni = s // tm

    def body(order_ref, x_ref, g_ref, own_ref, z_ref, nt_ref, wg_ref, n_s, w_s, send_sems, recv_sems, local_sems):
        j, i = pl.program_id(0), pl.program_id(1)
        px, py, c = _place()
        chip = 2 * px + py
        me = 2 * chip + c
        sib = (px, py, 1 - c)
        chips = _other_chips(px, py)

        def rc(k, src, blk, to):
            return pltpu.make_async_remote_copy(src_ref=src, dst_ref=w_s.at[blk], send_sem=send_sems.at[k],
                                                recv_sem=recv_sems.at[k], device_id=to, device_id_type=MESH)

        del chips
        first, second, far = _first_second(px, py, c)
        blocks = [2 * first[2] + c, 2 * second[2] + c, 2 * far[2] + c]
        own_in = pltpu.make_async_copy(own_ref, w_s.at[me], local_sems.at[0])
        to_first = rc(1, own_ref, me, (first[0], first[1], c))
        to_second = rc(2, own_ref, me, (second[0], second[1], c))
        relay = rc(3, w_s.at[blocks[0]], blocks[0], (second[0], second[1], c))
        sends = [rc(0, own_ref, me, sib), to_first, to_second, relay]
        passed = [rc(4 + q, w_s.at[blk], blk, sib) for q, blk in enumerate(blocks)]
        keep = pltpu.make_async_copy(w_s, wg_ref, local_sems.at[1])

        @pl.when((i == 0) & (j == 0))
        def _():
            own_in.start()
            sends[0].start()
            to_first.start()
            own_in.wait()

        @pl.when((i == 0) & (j == 1))
        def _():
            rc(0, own_ref, 2 * chip + 1 - c, sib).wait_recv()

        for q, blk in enumerate(blocks):
            @pl.when((i == 0) & (j == 2 + 2 * q))
            def _():
                rc(1 + q, own_ref, blk, sib).wait_recv()
                passed[q].start()
                if q == 0:
                    to_second.start()
                    relay.start()

            @pl.when((i == 0) & (j == 3 + 2 * q))
            def _():
                rc(4 + q, own_ref, order_ref[j], sib).wait_recv()

        rows = pl.ds(pl.multiple_of(i * tm, tm), tm)

        @pl.when(j == 0)
        def _():
            xv = x_ref[...]
            rstd = lax.rsqrt(jnp.mean(xv * xv, axis=-1, keepdims=True) + NORM_EPS)
            nb = (xv * rstd * g_ref[...]).astype(BF16)
            n_s[rows, :] = nb
            nt_ref[...] = nb.T

        z_ref[...] = _dot(n_s[rows, :], w_s[order_ref[j]]).astype(BF16)

        @pl.when((i == 0) & (j == N_SLOT - 1))
        def _():
            keep.start()

        @pl.when((i == ni - 1) & (j == N_SLOT - 1))
        def _():
            for cp in sends + passed:
                cp.wait_send()
            keep.wait()

    first_pass = lambda j, i, o: (jnp.where(j == 0, i, ni - 1), 0)
    (z, n1, w_in_g), extra = _call(
        body, name="in_proj", grid=(N_SLOT, ni), prefetch=(order,),
        in_specs=[pl.BlockSpec((tm, D), first_pass),
                  pl.BlockSpec((1, D), lambda j, i, o: (0, 0)), ANY],
        out_specs=[pl.BlockSpec((tm, W_IN_COLS), lambda j, i, o: (i, o[j])),
                   pl.BlockSpec((D, tm), lambda j, i, o: (0, jnp.where(j == 0, i, ni - 1))), ANY],
        out_shape=[SDS((s, N_SLOT * W_IN_COLS), BF16), SDS((D, s), BF16), SDS((N_SLOT, D, W_IN_COLS), BF16)],
        scratch_shapes=[pltpu.VMEM((s, D), BF16), pltpu.VMEM((N_SLOT, D, W_IN_COLS), BF16),
                        pltpu.SemaphoreType.DMA((7,)), pltpu.SemaphoreType.DMA((7,)), pltpu.SemaphoreType.DMA((2,))],
        params=_cparams(("arbitrary", "arbitrary"), 56), args=(x, g_mix, w_in_own), comm=comm)
    return (z, n1, w_in_g), extra


def _decay(r, sp_lam):
    log_a = (-LRU_C) * r * sp_lam
    a = jnp.exp(log_a)
    return a, jnp.sqrt(-jnp.tanh(log_a) * (a * a + 1.0))


def _lru_gates(xc, xcb, wr_ref, br, wi_ref, bi, sp_lam, a_s, b_s, r_ref, i_ref):
    for h in range(HEADS):
        sl = slice(h * HEAD_DIM, (h + 1) * HEAD_DIM)
        r = jax.nn.sigmoid(_dot(xcb[:, sl], wr_ref[h]) + br[:, sl])
        ig = jax.nn.sigmoid(_dot(xcb[:, sl], wi_ref[h]) + bi[:, sl])
        a, mult = _decay(r, sp_lam[:, sl])
        a_s[:, sl] = a
        b_s[:, sl] = xc[:, sl] * ig * mult
        r_ref[:, sl] = r.astype(BF16)
        i_ref[:, sl] = ig.astype(BF16)


def _conv_fwd(xa, prev8, cw, cb):
    xc = cb + cw[0:1, :] * xa
    for k in range(1, CONV_K):
        xc = xc + cw[k:k + 1, :] * _rows_shifted(prev8, xa, k)
    return xc


def _branch_a_fwd(z, conv_w, conv_b, w_r, b_r, w_i, b_i, lam, comm=None):
    s = z.shape[0]
    ta = min(T_BRANCH_A, s)
    per16 = ta // 16

    def body(xa_ref, xp_ref, ga_ref, cw_ref, cb_ref, wr_ref, br_ref, wi_ref, bi_ref, lam_ref,
             ya_ref, hs_ref, xc_ref, r_ref, i_ref, a_s, b_s, h_s, carry_s):
        i = pl.program_id(0)

        @pl.when(i == 0)
        def _():
            carry_s[...] = jnp.zeros_like(carry_s)

        xa = xa_ref[...].astype(F32)
        prev8 = jnp.where(i > 0, xp_ref[...].astype(F32)[8:16], 0.0)
        xc = _conv_fwd(xa, prev8, cw_ref[...], cb_ref[...])
        xcb = xc.astype(BF16)
        xc_ref[...] = xcb
        sp_lam = _softplus(-lam_ref[...])
        _lru_gates(xc, xcb, wr_ref, br_ref[...], wi_ref, bi_ref[...], sp_lam, a_s, b_s, r_ref, i_ref)

        row = lax.broadcasted_iota(jnp.int32, (8, D), 0)

        def group(g, carry):
            off = pl.multiple_of(g * 8, 8)
            a8 = a_s[pl.ds(off, 8), :]
            b8 = b_s[pl.ds(off, 8), :]
            for d in (1, 2, 4):
                a_sh = jnp.where(row >= d, pltpu.roll(a8, d, 0), 1.0)
                b_sh = jnp.where(row >= d, pltpu.roll(b8, d, 0), 0.0)
                b8 = a8 * b_sh + b8
                a8 = a8 * a_sh
            h8 = b8 + a8 * carry
            h_s[pl.ds(off, 8), :] = h8
            return jnp.broadcast_to(h8[7:8, :], (8, D))

        carry_s[...] = lax.fori_loop(0, ta // 8, group, carry_s[...])
        hs = h_s[...]
        hs_ref[...] = hs.astype(BF16)
        ya_ref[...] = (hs * _gelu(ga_ref[...].astype(F32))).astype(BF16)

    vec = pl.BlockSpec((1, D), lambda i: (0, 0))
    gate = pl.BlockSpec((HEADS, HEAD_DIM, HEAD_DIM), lambda i: (0, 0, 0))
    return _call(
        body, name="branch_a_fwd", grid=(s // ta,),
        in_specs=[pl.BlockSpec((ta, D), lambda i: (i, 0)),
                  pl.BlockSpec((16, D), lambda i: (jnp.maximum(i * per16 - 1, 0), 0)),
                  pl.BlockSpec((ta, D), lambda i: (i, 1)),
                  pl.BlockSpec((CONV_K, D), lambda i: (0, 0)), vec, gate, vec, gate, vec, vec],
        out_specs=[pl.BlockSpec((ta, D), lambda i: (i, 0))] * 5,
        out_shape=[SDS((s, D), BF16)] * 5,
        scratch_shapes=[pltpu.VMEM((ta, D), F32), pltpu.VMEM((ta, D), F32), pltpu.VMEM((ta, D), F32),
                        pltpu.VMEM((8, D), F32)],
        params=_cparams(("arbitrary",), 40), args=(z, z, z, conv_w, conv_b, w_r, b_r, w_i, b_i, lam), comm=comm)


def _sgu_common(ub, vb, lg, lb, with_grad):
    if with_grad:
        u, du = _gelu_and_grad(ub)
        v, dv = _gelu_and_grad(vb)
    else:
        u, v, du, dv = _gelu(ub), _gelu(vb), None, None
    mu = jnp.mean(v, axis=-1, keepdims=True)
    vc = v - mu
    rstd = lax.rsqrt(jnp.mean(vc * vc, axis=-1, keepdims=True) + LN_EPS)
    vhat = vc * rstd
    vln = vhat * lg + lb
    return u, du, dv, rstd, vhat, vln


def _masked_ws(ws_ref):
    t = lax.broadcasted_iota(jnp.int32, (CHUNK, CHUNK), 0)
    c = lax.broadcasted_iota(jnp.int32, (CHUNK, CHUNK), 1)
    keep = c <= t
    return [jnp.where(keep, ws_ref[g], 0.0).astype(BF16) for g in range(GROUPS)]


def _branch_b_fwd(z, ln_g, ln_b, w_s, b_s_t, comm=None):
    s = z.shape[0]
    tb = min(T_BRANCH_B, s)

    def body(ub_ref, vb_ref, lg_ref, lb_ref, ws_ref, bs_ref, yb_ref):
        u, _, _, _, _, vln = _sgu_common(ub_ref[...].astype(F32), vb_ref[...].astype(F32),
                                         lg_ref[...], lb_ref[...], False)
        vlnb = vln.astype(BF16)
        wm = _masked_ws(ws_ref)
        bs = bs_ref[...]
        for c in range(tb // CHUNK):
            rs = slice(c * CHUNK, (c + 1) * CHUNK)
            for g in range(GROUPS):
                cs = slice(g * GROUP_DIM, (g + 1) * GROUP_DIM)
                sp = _dot(wm[g], vlnb[rs, cs]) + bs[:, g:g + 1]
                yb_ref[rs, cs] = (u[rs, cs] * sp).astype(BF16)

    vec = pl.BlockSpec((1, D), lambda i: (0, 0))
    return _call(
        body, name="branch_b_fwd", grid=(s // tb,),
        in_specs=[pl.BlockSpec((tb, D), lambda i: (i, 2)), pl.BlockSpec((tb, D), lambda i: (i, 3)), vec, vec,
                  pl.BlockSpec((GROUPS, CHUNK, CHUNK), lambda i: (0, 0, 0)),
                  pl.BlockSpec((CHUNK, GROUPS), lambda i: (0, 0))],
        out_specs=[pl.BlockSpec((tb, D), lambda i: (i, 0))],
        out_shape=[SDS((s, D), BF16)], scratch_shapes=[],
        params=_cparams(("arbitrary",), 40), args=(z, z, ln_g, ln_b, w_s, b_s_t), comm=comm)


def _merge_out(ya, yb, z, x, w_oa, w_ob, w_out, comm=None):
    s = x.shape[0]
    tm = min(TM_MERGE, s)

    def body(ya_ref, yb_ref, ma_ref, mb_ref, x_ref, woa_ref, wob_ref, wo_ref, pa_ref, pb_ref, mg_ref, h1_ref):
        pa = _dot(ya_ref[...], woa_ref[...])
        pb = _dot(yb_ref[...], wob_ref[...])
        merged = (jax.nn.sigmoid(ma_ref[...].astype(F32)) * pa
                  + jax.nn.sigmoid(mb_ref[...].astype(F32)) * pb).astype(BF16)
        pa_ref[...] = pa.astype(BF16)
        pb_ref[...] = pb.astype(BF16)
        mg_ref[...] = merged
        h1_ref[...] = x_ref[...] + _dot(merged, wo_ref[...])

    row = pl.BlockSpec((tm, D), lambda i: (i, 0))
    wsp = pl.BlockSpec((D, D), lambda i: (0, 0))
    return _call(
        body, name="merge_out", grid=(s // tm,),
        in_specs=[row, row, pl.BlockSpec((tm, D), lambda i: (i, 4)), pl.BlockSpec((tm, D), lambda i: (i, 5)),
                  row, wsp, wsp, wsp],
        out_specs=[row, row, row, row],
        out_shape=[SDS((s, D), BF16), SDS((s, D), BF16), SDS((s, D), BF16), SDS((s, D), F32)], scratch_shapes=[],
        params=_cparams(("arbitrary",), 48), args=(ya, yb, z, z, x, w_oa, w_ob, w_out), comm=comm)


def _mlp_fwd(h1, g_mlp, w_up_g, w_down, g_fin, tgt):
    s = h1.shape[0]
    tm = min(TM_ROWS, s)
    nj = N_SLOT

    def body(h1_ref, gm_ref, wu_ref, wd_ref, gf_ref, t_ref, r_ref, at_ref, n2t_ref, dh2_ref, dh2b_ref, loss_ref,
             dgf_ref, n2_s, acc_s):
        i, j = pl.program_id(0), pl.program_id(1)

        @pl.when(j == 0)
        def _():
            hv = h1_ref[...]
            rstd = lax.rsqrt(jnp.mean(hv * hv, axis=-1, keepdims=True) + NORM_EPS)
            nb = (hv * rstd * gm_ref[...]).astype(BF16)
            n2_s[...] = nb
            n2t_ref[...] = nb.T
            acc_s[...] = jnp.zeros_like(acc_s)

        @pl.when((i == 0) & (j == 0))
        def _():
            loss_ref[...] = jnp.zeros_like(loss_ref)
            dgf_ref[...] = jnp.zeros_like(dgf_ref)

        r = jnp.maximum(_dot(n2_s[...], wu_ref[...]), 0.0)
        r_ref[...] = r.astype(BF16)
        act = (r * r).astype(BF16)
        at_ref[...] = act.T
        acc_s[...] += _dot(act, wd_ref[...])

        @pl.when(j == nj - 1)
        def _():
            h2 = h1_ref[...] + acc_s[...]
            rstd = lax.rsqrt(jnp.mean(h2 * h2, axis=-1, keepdims=True) + NORM_EPS)
            hh = h2 * rstd
            gf = gf_ref[...]
            e = hh * gf - t_ref[...]
            loss_ref[...] += jnp.sum(e * e) * (0.5 / D)
            dy = e * (1.0 / D)
            dgf_ref[...] += jnp.sum(dy * hh, axis=0, keepdims=True)
            dhh = dy * gf
            dh2 = rstd * (dhh - hh * jnp.mean(dhh * hh, axis=-1, keepdims=True))
            dh2_ref[...] = dh2
            dh2b_ref[...] = dh2.astype(BF16)

    row = pl.BlockSpec((tm, D), lambda i, j: (i, 0))
    vec = pl.BlockSpec((1, D), lambda i, j: (0, 0))
    return pl.pallas_call(
        body, name="mlp_fwd", grid=(s // tm, nj),
        in_specs=[row, vec, pl.BlockSpec((None, D, FF_COLS), lambda i, j: (j, 0, 0)),
                  pl.BlockSpec((FF_COLS, D), lambda i, j: (j, 0)), vec, row],
        out_specs=[pl.BlockSpec((tm, FF_COLS), lambda i, j: (i, j)), pl.BlockSpec((FF_COLS, tm), lambda i, j: (j, i)),
                   pl.BlockSpec((D, tm), lambda i, j: (0, i)), row, row, pl.BlockSpec((8, 128), lambda i, j: (0, 0)),
                   vec],
        out_shape=[SDS((s, nj * FF_COLS), BF16), SDS((nj * FF_COLS, s), BF16), SDS((D, s), BF16), SDS((s, D), F32),
                   SDS((s, D), BF16), SDS((8, 128), F32), SDS((1, D), F32)],
        scratch_shapes=[pltpu.VMEM((tm, D), BF16), pltpu.VMEM((tm, D), F32)],
        compiler_params=_cparams(("arbitrary", "arbitrary"), 56),
    )(h1, _small_in_hbm(g_mlp), w_up_g, w_down, _small_in_hbm(g_fin), tgt)


def _mlp_bwd(dh2, dh2b, r, w_down, w_up_g, h1, g_mlp, comm=None):
    s = h1.shape[0]
    tm = min(TM_ROWS, s)
    nj = N_SLOT

    def body(dh2_ref, dh2b_ref, r_ref, wd_ref, wu_ref, h1_ref, gm_ref, df_ref, dh1_ref, dgm_ref, acc_s):
        i, j = pl.program_id(0), pl.program_id(1)

        @pl.when(j == 0)
        def _():
            acc_s[...] = jnp.zeros_like(acc_s)

        @pl.when((i == 0) & (j == 0))
        def _():
            dgm_ref[...] = jnp.zeros_like(dgm_ref)

        d_act = _dot_nt(dh2b_ref[...], wd_ref[...])
        df = (d_act * (2.0 * r_ref[...].astype(F32))).astype(BF16)
        df_ref[...] = df
        acc_s[...] += _dot_nt(df, wu_ref[...])

        @pl.when(j == nj - 1)
        def _():
            hv = h1_ref[...]
            rstd = lax.rsqrt(jnp.mean(hv * hv, axis=-1, keepdims=True) + NORM_EPS)
            hh = hv * rstd
            dn2 = acc_s[...]
            dgm_ref[...] += jnp.sum(dn2 * hh, axis=0, keepdims=True)
            dhat = dn2 * gm_ref[...]
            dh1_ref[...] = dh2_ref[...] + rstd * (dhat - hh * jnp.mean(dhat * hh, axis=-1, keepdims=True))

    row = pl.BlockSpec((tm, D), lambda i, j: (i, 0))
    vec = pl.BlockSpec((1, D), lambda i, j: (0, 0))
    ffb = pl.BlockSpec((tm, FF_COLS), lambda i, j: (i, j))
    return _call(
        body, name="mlp_bwd", grid=(s // tm, nj),
        in_specs=[row, row, ffb, pl.BlockSpec((FF_COLS, D), lambda i, j: (j, 0)),
                  pl.BlockSpec((None, D, FF_COLS), lambda i, j: (j, 0, 0)), row, vec],
        out_specs=[ffb, row, vec],
        out_shape=[SDS((s, nj * FF_COLS), BF16), SDS((s, D), F32), SDS((1, D), F32)],
        scratch_shapes=[pltpu.VMEM((tm, D), F32)],
        params=_cparams(("arbitrary", "arbitrary"), 56), args=(dh2, dh2b, r, w_down, w_up_g, h1, g_mlp), comm=comm)


def _merge_bwd(dh1, z, pa, pb, w_out, w_oa, w_ob, comm=None):
    s = dh1.shape[0]
    tm = min(TM_MERGE, s)

    def body(dh1_ref, ma_ref, mb_ref, pa_ref, pb_ref, wo_ref, woa_ref, wob_ref,
             dz_ref, dpa_ref, dpb_ref, dya_ref, dyb_ref):
        dm = _dot_nt(dh1_ref[...].astype(BF16), wo_ref[...])
        sa = jax.nn.sigmoid(ma_ref[...].astype(F32))
        sb = jax.nn.sigmoid(mb_ref[...].astype(F32))
        dpa = (dm * sa).astype(BF16)
        dpb = (dm * sb).astype(BF16)
        dz_ref[:, 0:D] = (dm * pa_ref[...].astype(F32) * sa * (1.0 - sa)).astype(BF16)
        dz_ref[:, D:2 * D] = (dm * pb_ref[...].astype(F32) * sb * (1.0 - sb)).astype(BF16)
        dpa_ref[...] = dpa
        dpb_ref[...] = dpb
        dya_ref[...] = _dot_nt(dpa, woa_ref[...]).astype(BF16)
        dyb_ref[...] = _dot_nt(dpb, wob_ref[...]).astype(BF16)

    row = pl.BlockSpec((tm, D), lambda i: (i, 0))
    wsp = pl.BlockSpec((D, D), lambda i: (0, 0))
    return _call(
        body, name="merge_bwd", grid=(s // tm,),
        in_specs=[row, pl.BlockSpec((tm, D), lambda i: (i, 4)), pl.BlockSpec((tm, D), lambda i: (i, 5)),
                  row, row, wsp, wsp, wsp],
        out_specs=[pl.BlockSpec((tm, 2 * D), lambda i: (i, 2)), row, row, row, row],
        out_shape=[SDS((s, 6 * D), BF16)] + [SDS((s, D), BF16)] * 4, scratch_shapes=[],
        params=_cparams(("arbitrary",), 48), args=(dh1, z, z, pa, pb, w_out, w_oa, w_ob), comm=comm)


def _branch_b_bwd(dz, dyb, z, ln_g, ln_b, w_s, b_s_t, comm=None):
    s = z.shape[0]
    tb = min(T_BRANCH_B, s)

    def body(dz_in, dyb_ref, ub_ref, vb_ref, lg_ref, lb_ref, ws_ref, bs_ref,
             dz_ref, dws_ref, dbs_ref, dln_ref, du_s, dvln_s):
        del dz_in

        @pl.when(pl.program_id(0) == 0)
        def _():
            dws_ref[...] = jnp.zeros_like(dws_ref)
            dbs_ref[...] = jnp.zeros_like(dbs_ref)
            dln_ref[...] = jnp.zeros_like(dln_ref)

        lg = lg_ref[...]
        u, du, dv, rstd, vhat, vln = _sgu_common(ub_ref[...].astype(F32), vb_ref[...].astype(F32),
                                                 lg, lb_ref[...], True)
        vlnb = vln.astype(BF16)
        dyb_v = dyb_ref[...].astype(F32)
        wm = _masked_ws(ws_ref)
        keep = (lax.broadcasted_iota(jnp.int32, (CHUNK, CHUNK), 1)
                <= lax.broadcasted_iota(jnp.int32, (CHUNK, CHUNK), 0))
        bs = bs_ref[...]
        for c in range(tb // CHUNK):
            rs = slice(c * CHUNK, (c + 1) * CHUNK)
            for g in range(GROUPS):
                cs = slice(g * GROUP_DIM, (g + 1) * GROUP_DIM)
                v_blk = vlnb[rs, cs]
                sp = _dot(wm[g], v_blk) + bs[:, g:g + 1]
                d_sp = dyb_v[rs, cs] * u[rs, cs]
                d_spb = d_sp.astype(BF16)
                du_s[rs, cs] = dyb_v[rs, cs] * sp
                dvln_s[rs, cs] = _dot_tn(wm[g], d_spb)
                dws_ref[g] += jnp.where(keep, _dot_nt(d_spb, v_blk), 0.0)
                dbs_ref[g] += jnp.broadcast_to(jnp.sum(d_sp, axis=-1, keepdims=True), (CHUNK, CHUNK))
        dvln = dvln_s[...]
        dln_ref[0:1, :] += jnp.sum(dvln * vhat, axis=0, keepdims=True)
        dln_ref[1:2, :] += jnp.sum(dvln, axis=0, keepdims=True)
        dvh = dvln * lg
        d_v = rstd * (dvh - jnp.mean(dvh, axis=-1, keepdims=True)
                      - vhat * jnp.mean(dvh * vhat, axis=-1, keepdims=True))
        dz_ref[:, 0:D] = (du_s[...] * du).astype(BF16)
        dz_ref[:, D:2 * D] = (d_v * dv).astype(BF16)

    vec = pl.BlockSpec((1, D), lambda i: (0, 0))
    sq = pl.BlockSpec((GROUPS, CHUNK, CHUNK), lambda i: (0, 0, 0))
    return _call(
        body, name="branch_b_bwd", grid=(s // tb,),
        in_specs=[ANY, pl.BlockSpec((tb, D), lambda i: (i, 0)),
                  pl.BlockSpec((tb, D), lambda i: (i, 2)), pl.BlockSpec((tb, D), lambda i: (i, 3)), vec, vec, sq,
                  pl.BlockSpec((CHUNK, GROUPS), lambda i: (0, 0))],
        out_specs=[pl.BlockSpec((tb, 2 * D), lambda i: (i, 1)), sq, sq, pl.BlockSpec((8, D), lambda i: (0, 0))],
        out_shape=[SDS(dz.shape, BF16), SDS((GROUPS, CHUNK, CHUNK), F32), SDS((GROUPS, CHUNK, CHUNK), F32),
                   SDS((8, D), F32)],
        scratch_shapes=[pltpu.VMEM((tb, D), F32), pltpu.VMEM((tb, D), F32)], aliases={0: 0},
        params=_cparams(("arbitrary",), 40), args=(dz, dyb, z, z, ln_g, ln_b, w_s, b_s_t), comm=comm)


def _branch_a_bwd(dz, dya, z, hs, xc, r, ig, conv_w, w_r, w_i, lam, comm=None):
    s = z.shape[0]
    ta = min(T_BRANCH_A, s)
    nb = s // ta
    per16 = ta // 16

    def body(dz_in, dya_ref, xa_ref, ga_ref, hs_ref, hp_ref, xc_ref, r_ref, i_ref, cw_ref, wr_ref, wi_ref,
             lam_ref, dz_ref, vec_ref, dwr_ref, dwi_ref, a_s, b_s, h_s, dcar_s, acar_s, dxc_s):
        del dz_in
        i = pl.program_id(0)
        blk = nb - 1 - i

        @pl.when(i == 0)
        def _():
            dcar_s[...] = jnp.zeros_like(dcar_s)
            acar_s[...] = jnp.zeros_like(acar_s)
            dxc_s[...] = jnp.zeros_like(dxc_s)
            vec_ref[...] = jnp.zeros_like(vec_ref)
            dwr_ref[...] = jnp.zeros_like(dwr_ref)
            dwi_ref[...] = jnp.zeros_like(dwi_ref)

        cw = cw_ref[...]
        lam_v = lam_ref[...]
        xa = xa_ref[...].astype(F32)
        xcb = xc_ref[...]
        xc = xcb.astype(F32)
        sp_lam = _softplus(-lam_v)
        r_v, i_v = r_ref[...].astype(F32), i_ref[...].astype(F32)
        a_v, m_v = _decay(r_v, sp_lam)

        hs_v = hs_ref[...].astype(F32)
        hprev8 = jnp.where(blk > 0, hp_ref[...].astype(F32)[8:16], 0.0)
        h_m1 = _rows_shifted(hprev8, hs_v, 1)
        gg, dgg = _gelu_and_grad(ga_ref[...].astype(F32))
        dya_v = dya_ref[...].astype(F32)
        dz_ref[:, D:2 * D] = (dya_v * hs_v * dgg).astype(BF16)

        a_s[...] = _rows_advanced(a_v, acar_s[...], 1)
        b_s[...] = dya_v * gg

        row = lax.broadcasted_iota(jnp.int32, (8, D), 0)
        ng = ta // 8

        def group(gi, carry):
            off = pl.multiple_of((ng - 1 - gi) * 8, 8)
            c8 = a_s[pl.ds(off, 8), :]
            d8 = b_s[pl.ds(off, 8), :]
            for d in (1, 2, 4):
                c_sh = jnp.where(row < 8 - d, pltpu.roll(c8, 8 - d, 0), 1.0)
                d_sh = jnp.where(row < 8 - d, pltpu.roll(d8, 8 - d, 0), 0.0)
                d8 = c8 * d_sh + d8
                c8 = c8 * c_sh
            dh8 = d8 + c8 * carry
            h_s[pl.ds(off, 8), :] = dh8
            return jnp.broadcast_to(dh8[0:1, :], (8, D))

        dcar_s[...] = lax.fori_loop(0, ng, group, dcar_s[...])
        acar_s[...] = jnp.broadcast_to(a_v[0:1, :], (8, D))

        dbx = h_s[...]
        d_mult = dbx * xc * i_v
        d_loga = dbx * h_m1 * a_v - d_mult * (a_v * a_v) / m_v
        d_pr = d_loga * ((-LRU_C) * sp_lam) * r_v * (1.0 - r_v)
        d_pi = dbx * xc * m_v * i_v * (1.0 - i_v)
        vec_ref[7:8, :] += jnp.sum(d_loga * r_v, axis=0, keepdims=True) * (LRU_C * jax.nn.sigmoid(-lam_v))
        vec_ref[5:6, :] += jnp.sum(d_pr, axis=0, keepdims=True)
        vec_ref[6:7, :] += jnp.sum(d_pi, axis=0, keepdims=True)
        d_prb = d_pr.astype(BF16)
        d_pib = d_pi.astype(BF16)
        h_s[...] = dbx * i_v * m_v
        for h in range(HEADS):
            sl = slice(h * HEAD_DIM, (h + 1) * HEAD_DIM)
            h_s[:, sl] += _dot_nt(d_prb[:, sl], wr_ref[h]) + _dot_nt(d_pib[:, sl], wi_ref[h])
            dwr_ref[h] += _dot_tn(xcb[:, sl], d_prb[:, sl])
            dwi_ref[h] += _dot_tn(xcb[:, sl], d_pib[:, sl])
        d_xc = h_s[...]
        vec_ref[4:5, :] += jnp.sum(d_xc, axis=0, keepdims=True)
        vec_ref[0:1, :] += jnp.sum(d_xc * xa, axis=0, keepdims=True)
        d_xa = cw[0:1, :] * d_xc
        nxt = dxc_s[...]
        for k in range(1, CONV_K):
            ahead = _rows_advanced(d_xc, nxt, k)
            vec_ref[k:k + 1, :] += jnp.sum(ahead * xa, axis=0, keepdims=True)
            d_xa = d_xa + cw[k:k + 1, :] * ahead
        dz_ref[:, 0:D] = d_xa.astype(BF16)
        dxc_s[...] = d_xc[0:8, :]

    vec = pl.BlockSpec((1, D), lambda i: (0, 0))
    gate = pl.BlockSpec((HEADS, HEAD_DIM, HEAD_DIM), lambda i: (0, 0, 0))
    cur = lambda c: pl.BlockSpec((ta, D), lambda i: (nb - 1 - i, c))
    before = lambda c: pl.BlockSpec((16, D), lambda i: (jnp.maximum((nb - 1 - i) * per16 - 1, 0), c))
    return _call(
        body, name="branch_a_bwd", grid=(nb,),
        in_specs=[ANY, cur(0), cur(0), cur(1), cur(0), before(0), cur(0), cur(0), cur(0),
                  pl.BlockSpec((CONV_K, D), lambda i: (0, 0)), gate, gate, vec],
        out_specs=[pl.BlockSpec((ta, 2 * D), lambda i: (nb - 1 - i, 0)), pl.BlockSpec((8, D), lambda i: (0, 0)),
                   gate, gate],
        out_shape=[SDS(dz.shape, BF16), SDS((8, D), F32), SDS((HEADS, HEAD_DIM, HEAD_DIM), F32),
                   SDS((HEADS, HEAD_DIM, HEAD_DIM), F32)],
        scratch_shapes=[pltpu.VMEM((ta, D), F32)] * 3 + [pltpu.VMEM((8, D), F32)] * 3, aliases={0: 0},
        params=_cparams(("arbitrary",), 48),
        args=(dz, dya, z, z, hs, hs, xc, r, ig, conv_w, w_r, w_i, lam), comm=comm)


def _in_bwd(dz, w_in_g, x, dh1, g_mix, comm=None):
    s = x.shape[0]
    tm = min(TM_ROWS, s)
    nj = N_SLOT

    def body(dz_ref, w_ref, x_ref, dh1_ref, g_ref, dx_ref, dg_ref, acc_s):
        i, j = pl.program_id(0), pl.program_id(1)

        @pl.when(j == 0)
        def _():
            acc_s[...] = jnp.zeros_like(acc_s)

        @pl.when((i == 0) & (j == 0))
        def _():
            dg_ref[...] = jnp.zeros_like(dg_ref)

        acc_s[...] += _dot_nt(dz_ref[...], w_ref[...])

        @pl.when(j == nj - 1)
        def _():
            xv = x_ref[...]
            rstd = lax.rsqrt(jnp.mean(xv * xv, axis=-1, keepdims=True) + NORM_EPS)
            xh = xv * rstd
            dn = acc_s[...]
            dg_ref[...] += jnp.sum(dn * xh, axis=0, keepdims=True)
            dhat = dn * g_ref[...]
            dx_ref[...] = dh1_ref[...] + rstd * (dhat - xh * jnp.mean(dhat * xh, axis=-1, keepdims=True))

    row = pl.BlockSpec((tm, D), lambda i, j: (i, 0))
    vec = pl.BlockSpec((1, D), lambda i, j: (0, 0))
    return _call(
        body, name="in_bwd", grid=(s // tm, nj),
        in_specs=[pl.BlockSpec((tm, W_IN_COLS), lambda i, j: (i, j)),
                  pl.BlockSpec((None, D, W_IN_COLS), lambda i, j: (j, 0, 0)), row, row, vec],
        out_specs=[row, vec],
        out_shape=[SDS((s, D), F32), SDS((1, D), F32)],
        scratch_shapes=[pltpu.VMEM((tm, D), F32)],
        params=_cparams(("arbitrary", "arbitrary"), 48), args=(dz, w_in_g, x, dh1, g_mix), comm=comm)


def _wgrad(name, a, b):
    s = a.shape[0]
    ts = min(TM_ROWS, s)
    a_w, b_w = a.shape[1], b.shape[1]

    def body(a_ref, b_ref, o_ref, acc_s):
        t = pl.program_id(0)

        @pl.when(t == 0)
        def _():
            acc_s[...] = jnp.zeros_like(acc_s)

        acc_s[...] += _dot_tn(a_ref[...].astype(BF16), b_ref[...].astype(BF16))

        @pl.when(t == pl.num_programs(0) - 1)
        def _():
            o_ref[...] = acc_s[...].astype(BF16)

    return pl.pallas_call(
        body, name=name, grid=(s // ts,),
        in_specs=[pl.BlockSpec((ts, a_w), lambda t: (t, 0)), pl.BlockSpec((ts, b_w), lambda t: (t, 0))],
        out_specs=pl.BlockSpec((a_w, b_w), lambda t: (0, 0)),
        out_shape=SDS((a_w, b_w), BF16),
        scratch_shapes=[pltpu.VMEM((a_w, b_w), F32)],
        compiler_params=_cparams(("arbitrary",), 48),
    )(a, b)


def _place():
    x, y, c = lax.axis_index("x"), lax.axis_index("y"), lax.axis_index("c")
    return x, y, c


def _other_chips(x, y):
    return [(x, 1 - y, 2 * x + 1 - y), (1 - x, y, 2 * (1 - x) + y), (1 - x, 1 - y, 2 * (1 - x) + 1 - y)]


class _Plan:
    def __init__(self, arrays, out_shape, sems, start, finish, middle=None, middle_at=6):
        self.arrays, self.out_shape, self.sems, self.start, self.finish = arrays, out_shape, sems, start, finish
        self.middle, self.middle_at = middle, middle_at


def _gather_plan(shards, middle_at=6):
    n = len(shards)

    def copies(ins, outs, sems):
        send_sems, recv_sems, local_sems = sems
        x, y, c = _place()
        chip = 2 * x + y
        me = 2 * chip + c
        sib = (x, y, 1 - c)
        chips = _other_chips(x, y)

        def rc(k, t, src, blk, to):
            return pltpu.make_async_remote_copy(
                src_ref=src, dst_ref=outs[t].at[blk], send_sem=send_sems.at[k * n + t],
                recv_sem=recv_sems.at[k * n + t], device_id=to, device_id_type=MESH)

        (yx, yy, y_chip), (xx, xy, x_chip), _ = chips
        local = [pltpu.make_async_copy(ins[t], outs[t].at[me], local_sems.at[t]) for t in range(n)]
        sends = ([rc(0, t, ins[t], me, sib) for t in range(n)] + [rc(1, t, ins[t], me, (yx, yy, c)) for t in range(n)]
                 + [rc(2, t, ins[t], me, (xx, xy, c)) for t in range(n)])
        passed = [[rc(4 + j, t, outs[t].at[2 * pc + c], 2 * pc + c, sib) for t in range(n)]
                  for j, (_, _, pc) in enumerate(chips)]
        relays = [[rc(3, t, outs[t].at[2 * y_chip + c], 2 * y_chip + c, (xx, xy, c)) for t in range(n)],
                  [rc(3, t, outs[t].at[2 * x_chip + c], 2 * x_chip + c, (yx, yy, c)) for t in range(n)]]
        return rc, local, sends, passed, relays, chips, chip, c, sib

    def start(ins, outs, sems):
        _, local, sends, _, _, _, _, _, _ = copies(ins, outs, sems)
        for cp in local + sends:
            cp.start()

    def middle(ins, outs, sems):
        rc, _, _, passed, relays, chips, _, c, sib = copies(ins, outs, sems)
        for j in range(2):
            for t in range(n):
                rc(1 + j, t, ins[t], 2 * chips[j][2] + c, sib).wait_recv()
        for j in range(2):
            for cp in passed[j]:
                cp.start()

            @pl.when(c == j)
            def _():
                for cp in relays[j]:
                    cp.start()

    def finish(ins, outs, sems):
        rc, local, sends, passed, relays, chips, chip, c, sib = copies(ins, outs, sems)
        far = 2 * chips[2][2] + c
        for t in range(n):
            rc(3, t, ins[t], far, sib).wait_recv()
        for cp in passed[2]:
            cp.start()
        for t in range(n):
            rc(0, t, ins[t], 2 * chip + 1 - c, sib).wait_recv()
        for j, (px, py, pc) in enumerate(chips):
            for t in range(n):
                rc(4 + j, t, ins[t], 2 * pc + 1 - c, sib).wait_recv()
        for cp in sends + passed[0] + passed[1] + passed[2]:
            cp.wait_send()
        for j in range(2):
            @pl.when(c == j)
            def _():
                for cp in relays[j]:
                    cp.wait_send()
        for cp in local:
            cp.wait()

    return _Plan(list(shards), [SDS((N_SLOT,) + tuple(a.shape), a.dtype) for a in shards],
                 [pltpu.SemaphoreType.DMA((7 * n,)), pltpu.SemaphoreType.DMA((7 * n,)),
                  pltpu.SemaphoreType.DMA((n,))], start, finish, middle, middle_at)


def _sibling_plan(grads, whole=()):
    n, m = len(grads), len(whole)

    def copies(ins, outs, sems):
        send_sems, recv_sems = sems
        x, y, c = _place()
        sib = (x, y, 1 - c)

        def rc(t, src, dst):
            return pltpu.make_async_remote_copy(src_ref=src, dst_ref=dst, send_sem=send_sems.at[t],
                                                recv_sem=recv_sems.at[t], device_id=sib, device_id_type=MESH)
        return rc, c

    def start(ins, outs, sems):
        rc, c = copies(ins, outs, sems)
        for t in range(n):
            for j in range(4):
                rc(t, ins[t].at[2 * j + 1 - c], outs[t].at[j]).start()
        for t in range(n, n + m):
            rc(t, ins[t], outs[t]).start()

    def finish(ins, outs, sems):
        rc, _ = copies(ins, outs, sems)
        for t in range(n):
            rc(t, ins[t].at[pl.ds(0, 4)], outs[t]).wait()
        for t in range(n, n + m):
            rc(t, ins[t], outs[t]).wait()

    return _Plan(list(grads) + list(whole),
                 [SDS((4,) + tuple(g.shape[1:]), g.dtype) for g in grads] + [SDS(a.shape, a.dtype) for a in whole],
                 [pltpu.SemaphoreType.DMA((n + m,)), pltpu.SemaphoreType.DMA((n + m,))], start, finish)


def _chips_plan(parts, whole=()):
    n, m = len(parts), len(whole)

    def src_of(ins, t, pc):
        return ins[t].at[pc] if t < n else ins[t]

    def local_copies(ins, outs, sems, chip):
        return [pltpu.make_async_copy(src_of(ins, t, chip), outs[t].at[chip], sems[2].at[t]) for t in range(n + m)]

    def start(ins, outs, sems):
        send_sems, recv_sems, _ = sems
        x, y, c = _place()
        chip = 2 * x + y
        for cp in local_copies(ins, outs, sems, chip):
            cp.start()
        for px, py, pc in _other_chips(x, y):
            for t in range(n + m):
                pltpu.make_async_remote_copy(src_ref=src_of(ins, t, pc), dst_ref=outs[t].at[chip],
                                             send_sem=send_sems.at[t], recv_sem=recv_sems.at[t],
                                             device_id=(px, py, c), device_id_type=MESH).start()

    def finish(ins, outs, sems):
        send_sems, recv_sems, _ = sems
        x, y, c = _place()
        for t in range(n + m):
            three = outs[t].at[pl.ds(0, 3)]
            pltpu.make_async_remote_copy(src_ref=three, dst_ref=three, send_sem=send_sems.at[t],
                                         recv_sem=recv_sems.at[t], device_id=(x, y, c), device_id_type=MESH).wait()
        for cp in local_copies(ins, outs, sems, 2 * x + y):
            cp.wait()

    return _Plan(list(parts) + list(whole),
                 [SDS(p.shape, p.dtype) for p in parts] + [SDS((4,) + tuple(a.shape), a.dtype) for a in whole],
                 [pltpu.SemaphoreType.DMA((n + m,)), pltpu.SemaphoreType.DMA((n + m,)),
                  pltpu.SemaphoreType.DMA((n + m,))], start, finish)


def _exchange_plan(arr):
    def peers(x, y, c):
        flip = lambda v, f: 1 - v if f else v
        return [(flip(x, fx), flip(y, fy), flip(c, fc))
                for fx in (0, 1) for fy in (0, 1) for fc in (0, 1) if fx or fy or fc]

    def start(ins, outs, sems):
        x, y, c = _place()
        me = 4 * x + 2 * y + c
        pltpu.make_async_copy(ins[0], outs[0].at[me], sems[2].at[0]).start()
        for to in peers(x, y, c):
            pltpu.make_async_remote_copy(src_ref=ins[0], dst_ref=outs[0].at[me], send_sem=sems[0].at[0],
                                         recv_sem=sems[1].at[0], device_id=to, device_id_type=MESH).start()

    def finish(ins, outs, sems):
        x, y, c = _place()
        seven = outs[0].at[pl.ds(0, 7)]
        pltpu.make_async_remote_copy(src_ref=seven, dst_ref=seven, send_sem=sems[0].at[0], recv_sem=sems[1].at[0],
                                     device_id=(x, y, c), device_id_type=MESH).wait()
        pltpu.make_async_copy(ins[0], outs[0].at[4 * x + 2 * y + c], sems[2].at[0]).wait()

    return _Plan([arr], [SDS((N_SLOT,) + tuple(arr.shape), arr.dtype)],
                 [pltpu.SemaphoreType.DMA((1,)), pltpu.SemaphoreType.DMA((1,)), pltpu.SemaphoreType.DMA((1,))],
                 start, finish)


def _join(*plans):
    def cut(seq, sizes):
        out, at = [], 0
        for k in sizes:
            out.append(seq[at:at + k])
            at += k
        return out

    n_arr = [len(p.arrays) for p in plans]
    n_sem = [len(p.sems) for p in plans]

    def start(ins, outs, sems):
        for p, i, o, s in zip(plans, cut(ins, n_arr), cut(outs, n_arr), cut(sems, n_sem)):
            p.start(i, o, s)

    def finish(ins, outs, sems):
        for p, i, o, s in zip(plans, cut(ins, n_arr), cut(outs, n_arr), cut(sems, n_sem)):
            p.finish(i, o, s)

    def middle(ins, outs, sems):
        for p, i, o, s in zip(plans, cut(ins, n_arr), cut(outs, n_arr), cut(sems, n_sem)):
            if p.middle is not None:
                p.middle(i, o, s)

    return _Plan([a for p in plans for a in p.arrays], [o for p in plans for o in p.out_shape],
                 [s for p in plans for s in p.sems], start, finish,
                 middle if any(p.middle is not None for p in plans) else None,
                 max(p.middle_at for p in plans))


def _run_plan(name, plan):
    k = len(plan.arrays)

    def body(*refs):
        ins, outs, sems = refs[:k], refs[k:2 * k], refs[2 * k:]
        plan.start(ins, outs, sems)
        if plan.middle is not None:
            plan.middle(ins, outs, sems)
        plan.finish(ins, outs, sems)

    return pl.pallas_call(
        body, name=name, in_specs=[ANY] * k, out_specs=[ANY] * k, out_shape=plan.out_shape,
        scratch_shapes=plan.sems, compiler_params=pltpu.CompilerParams(has_side_effects=True),
    )(*plan.arrays)


def _call(body, *, name, grid, in_specs, out_specs, out_shape, scratch_shapes, params, args, comm=None,
          aliases=None, prefetch=()):
    aliases = aliases or {}
    n_pre = len(prefetch)

    def launch(fn, ins_specs, outs_specs, outs_shape, scratch, operands):
        spec = pltpu.PrefetchScalarGridSpec(num_scalar_prefetch=n_pre, grid=grid, in_specs=ins_specs,
                                            out_specs=outs_specs, scratch_shapes=scratch)
        return pl.pallas_call(fn, name=name, grid_spec=spec, out_shape=outs_shape, compiler_params=params,
                              input_output_aliases=aliases)(*prefetch, *[_small_in_hbm(a) for a in operands])

    if comm is None:
        return list(launch(body, in_specs, out_specs, out_shape, scratch_shapes, args)), []
    n_in, n_out, n_scr, k = len(in_specs), len(out_specs), len(scratch_shapes), len(comm.arrays)

    def wrapped(*refs):
        pre, refs = refs[:n_pre], refs[n_pre:]
        ins = refs[:n_in]
        c_in = refs[n_in:n_in + k]
        outs = refs[n_in + k:n_in + k + n_out]
        c_out = refs[n_in + k + n_out:n_in + 2 * k + n_out]
        scr = refs[n_in + 2 * k + n_out:n_in + 2 * k + n_out + n_scr]
        sems = refs[n_in + 2 * k + n_out + n_scr:]
        step, steps = pl.program_id(0), grid[0]
        for d in range(1, len(grid)):
            step, steps = step * grid[d] + pl.program_id(d), steps * grid[d]

        @pl.when(step == 0)
        def _():
            comm.start(c_in, c_out, sems)

        if comm.middle is not None:
            @pl.when(step == (comm.middle_at * steps) // 8)
            def _():
                comm.middle(c_in, c_out, sems)

        body(*pre, *ins, *outs, *scr)

        @pl.when(step == steps - 1)
        def _():
            comm.finish(c_in, c_out, sems)

    res = launch(wrapped, list(in_specs) + [ANY] * k, list(out_specs) + [ANY] * k,
                 list(out_shape) + list(comm.out_shape), list(scratch_shapes) + list(comm.sems),
                 tuple(args) + tuple(comm.arrays))
    return list(res[:n_out]), list(res[n_out:])


def _wgrad_paired(name, a_t, b, core, by_rows=False, comm=None):
    s = b.shape[0]
    m = a_t.shape[0] // N_SLOT if by_rows else a_t.shape[0]
    cols = b.shape[1] if by_rows else b.shape[1] // N_SLOT
    half = N_SLOT // 2

    def owner(k, c):
        return 2 * (k % half) + jnp.where(k < half, 1 - c, c)

    def body(c_ref, a_ref, b_ref, sum_ref, out_s, recv_s, send_sems, recv_sems):
        del c_ref
        k = pl.program_id(0)
        x, y, c = _place()

        def to_sibling(j):
            return pltpu.make_async_remote_copy(src_ref=out_s.at[j % 2], dst_ref=recv_s.at[j],
                                                send_sem=send_sems.at[j], recv_sem=recv_sems.at[j],
                                                device_id=(x, y, 1 - c), device_id_type=MESH)

        @pl.when((k >= 2) & (k < half + 2))
        def _():
            to_sibling(k - 2).wait_send()

        @pl.when(k < half)
        def _():
            out_s[k % 2] = _dot(a_ref[...], b_ref[...]).astype(BF16)
            to_sibling(k).start()

        @pl.when(k >= half)
        def _():
            to_sibling(k - half).wait_recv()
            sum_ref[...] = (_dot(a_ref[...], b_ref[...]) + recv_s[k - half].astype(F32)).astype(BF16)

    if by_rows:
        in_specs = [pl.BlockSpec((m, s), lambda k, c_ref: (owner(k, c_ref[0]), 0)),
                    pl.BlockSpec((s, cols), lambda k, c_ref: (0, 0))]
    else:
        in_specs = [pl.BlockSpec((m, s), lambda k, c_ref: (0, 0)),
                    pl.BlockSpec((s, cols), lambda k, c_ref: (0, owner(k, c_ref[0])))]
    return _call(body, name=name, grid=(N_SLOT,), in_specs=in_specs,
                 out_specs=[pl.BlockSpec((None, m, cols), lambda k, c_ref: (jnp.maximum(k - half, 0), 0, 0))],
                 out_shape=[SDS((half, m, cols), BF16)],
                 scratch_shapes=[pltpu.VMEM((2, m, cols), BF16), pltpu.VMEM((half, m, cols), BF16),
                                 pltpu.SemaphoreType.DMA((half,)), pltpu.SemaphoreType.DMA((half,))],
                 params=_cparams(("arbitrary",), 56), args=(a_t, b), comm=comm, prefetch=(core,))


def _row_tile(rows):
    for t in (512, 256, 128, 64, 32, 16, 8):
        if rows % t == 0:
            return t
    return rows


def _pair_sum(name, g8, recv4, core):
    _, rows, cols = recv4.shape
    tr = _row_tile(rows)
    g42 = g8.reshape(4, 2, rows, cols)

    def body(c_ref, g_ref, r_ref, o_ref):
        del c_ref
        o_ref[...] = (g_ref[...].astype(F32) + r_ref[...].astype(F32)).astype(o_ref.dtype)

    return pl.pallas_call(
        body, name=name,
        grid_spec=pltpu.PrefetchScalarGridSpec(
            num_scalar_prefetch=1, grid=(4, rows // tr),
            in_specs=[pl.BlockSpec((None, None, tr, cols), lambda j, i, c_ref: (j, c_ref[0], i, 0)),
                      pl.BlockSpec((None, tr, cols), lambda j, i, c_ref: (j, i, 0))],
            out_specs=pl.BlockSpec((None, tr, cols), lambda j, i, c_ref: (j, i, 0))),
        out_shape=SDS(recv4.shape, g8.dtype),
        compiler_params=_cparams(("arbitrary", "arbitrary"), 32),
    )(core, g42, recv4)


def _add2(name, a, b):
    rows, cols = a.shape
    tr = _row_tile(rows)

    def body(a_ref, b_ref, o_ref):
        o_ref[...] = a_ref[...] + b_ref[...]

    blk = pl.BlockSpec((tr, cols), lambda i: (i, 0))
    return pl.pallas_call(body, name=name, grid=(rows // tr,), in_specs=[blk, blk], out_specs=blk,
                          out_shape=SDS(a.shape, a.dtype),
                          compiler_params=_cparams(("arbitrary",), 32))(a, b)


def _sum_terms(name, terms):
    k, rows, cols = terms.shape
    tr = _row_tile(rows)

    def body(r_ref, o_ref):
        acc = r_ref[0]
        for q in range(1, k):
            acc = acc + r_ref[q]
        o_ref[...] = acc

    return pl.pallas_call(body, name=name, grid=(rows // tr,),
                          in_specs=[pl.BlockSpec((k, tr, cols), lambda i: (0, i, 0))],
                          out_specs=pl.BlockSpec((tr, cols), lambda i: (i, 0)),
                          out_shape=SDS((rows, cols), terms.dtype),
                          compiler_params=_cparams(("arbitrary",), 32))(terms)


def _adam_update(g, w, m, v):
    c1 = 1.0 / (1.0 - ADAM_B1 ** ADAM_STEP)
    c2 = 1.0 / (1.0 - ADAM_B2 ** ADAM_STEP)
    mn = ADAM_B1 * m + (1.0 - ADAM_B1) * g
    vn = ADAM_B2 * v + (1.0 - ADAM_B2) * (g * g)
    delta = (-ADAM_LR) * ((mn * c1) / (jnp.sqrt(vn * c2) + ADAM_EPS) + ADAM_WD * w)
    return delta, mn, vn


def _adamw_many(name, gs, ws, ms, vs):
    n = len(gs)

    def body(*refs):
        for p in range(n):
            g, w, m, v = (refs[q * n + p][...] for q in range(4))
            d, mn, vn = _adam_update(g, w, m, v)
            refs[4 * n + p][...] = d
            refs[5 * n + p][...] = mn
            refs[6 * n + p][...] = vn

    full = [pl.BlockSpec(w.shape, lambda i: (0, 0)) for w in ws]
    shapes = [SDS(w.shape, F32) for w in ws]
    res = pl.pallas_call(body, name=name, grid=(1,), in_specs=full * 4, out_specs=full * 3, out_shape=shapes * 3,
                         compiler_params=_cparams(("arbitrary",), 32),
                         )(*[_small_in_hbm(a) for a in (*gs, *ws, *ms, *vs)])
    return [(res[p], res[n + p], res[2 * n + p]) for p in range(n)]


def _adamw(name, terms, w, m, v):
    k, rows, cols = terms.shape
    tr = _row_tile(rows)

    def body(t_ref, w_ref, m_ref, v_ref, g_ref, d_ref, mo_ref, vo_ref):
        g = t_ref[0].astype(F32)
        for q in range(1, k):
            g = g + t_ref[q].astype(F32)
        g_ref[...] = g
        d_ref[...], mo_ref[...], vo_ref[...] = _adam_update(g, w_ref[...], m_ref[...], v_ref[...])

    blk = pl.BlockSpec((tr, cols), lambda i: (i, 0))
    return pl.pallas_call(body, name=name, grid=(rows // tr,),
                          in_specs=[pl.BlockSpec((k, tr, cols), lambda i: (0, i, 0)), blk, blk, blk],
                          out_specs=[blk] * 4, out_shape=[SDS((rows, cols), F32)] * 4,
                          compiler_params=_cparams(("arbitrary",), 40),
                          )(*[pltpu.with_memory_space_constraint(a, pltpu.HBM) for a in (terms, w, m, v)])


def kernel(x, norm_mix_g, w_in, conv_w, conv_b, w_rgate, b_rgate, w_igate, b_igate, lru_lambda, w_out_a, sgu_ln_g, sgu_ln_b, sgu_w_s, sgu_b_s, w_out_b, w_out, norm_mlp_g, w_up, w_down, norm_final_g, loss_target, m_norm_mix_g, m_w_in, m_conv_w, m_conv_b, m_w_rgate, m_b_rgate, m_w_igate, m_b_igate, m_lru_lambda, m_w_out_a, m_sgu_ln_g, m_sgu_ln_b, m_sgu_w_s, m_sgu_b_s, m_w_out_b, m_w_out, m_norm_mlp_g, m_w_up, m_w_down, m_norm_final_g, v_norm_mix_g, v_w_in, v_conv_w, v_conv_b, v_w_rgate, v_b_rgate, v_w_igate, v_b_igate, v_lru_lambda, v_w_out_a, v_sgu_ln_g, v_sgu_ln_b, v_sgu_w_s, v_sgu_b_s, v_w_out_b, v_w_out, v_norm_mlp_g, v_w_up, v_w_down, v_norm_final_g):
    cx, cy, cc = _place()
    me = 4 * cx + 2 * cy + cc
    core = jnp.reshape(cc, (1,)).astype(jnp.int32)
    xs = x[0]
    tgt = loss_target[0]

    gate_shard = jnp.stack([w_rgate[0], w_igate[0]]).astype(BF16).reshape(2 * HEADS * 32, HEAD_DIM)
    vec_shard = jnp.concatenate([conv_w[0], b_rgate[0], b_igate[0]], axis=1)
    vec_shard = jnp.pad(vec_shard, ((0, 4), (0, 256 - vec_shard.shape[1])))
    shards = [w_in[0].astype(BF16), w_out_a[0].astype(BF16), w_out_b[0].astype(BF16), w_out[0].astype(BF16),
              w_up[0].astype(BF16), w_down[0].astype(BF16), gate_shard, vec_shard]
    (z, n1_t, w_in_g), (gate_g, vec_g) = _in_proj(xs, norm_mix_g, shards[0], _slot_order(cx, cy, cc),
                                                comm=_gather_plan(shards[6:8]))
    gates = gate_g.reshape(N_SLOT, 2, HEADS, 32, HEAD_DIM).transpose(1, 2, 0, 3, 4).reshape(2, HEADS, HEAD_DIM, HEAD_DIM)
    w_r_f, w_i_f = gates[0], gates[1]
    conv_w_f = vec_g[:, 0:4, 0:128].transpose(1, 0, 2).reshape(CONV_K, D)
    b_r_f = vec_g[:, 0:4, 128:160].transpose(1, 0, 2).reshape(1, D)
    b_i_f = vec_g[:, 0:4, 160:192].transpose(1, 0, 2).reshape(1, D)
    b_s_t = jnp.transpose(sgu_b_s[0])

    (ya, hs, xc, r_gate, i_gate), (w_oa_g, w_ob_g, w_out_g, w_up_g) = _branch_a_fwd(
        z, conv_w_f, conv_b, w_r_f, b_r_f, w_i_f, b_i_f, lru_lambda, comm=_gather_plan(shards[1:5]))
    w_oa_f = w_oa_g.reshape(D, D)
    w_ob_f = w_ob_g.reshape(D, D)
    w_out_f = w_out_g.reshape(D, D)
    (yb,), _ = _branch_b_fwd(z, sgu_ln_g, sgu_ln_b, sgu_w_s[0], b_s_t)
    (pa, pb, merged, h1), (w_down_g,) = _merge_out(ya, yb, z, xs, w_oa_f, w_ob_f, w_out_f,
                                                   comm=_gather_plan(shards[5:6], middle_at=4))
    w_down_f = w_down_g.reshape(N_SLOT * FF_COLS, D)
    r_act, act_t, n2_t, dh2, dh2b, loss_acc, d_gfin = _mlp_fwd(h1, norm_mlp_g, w_up_g, w_down_f,
                                                               norm_final_g.reshape(1, D), tgt)

    def pair(names, grads, recv):
        return [_pair_sum("pair_sum_" + nm, g, r, core) for nm, g, r in zip(names, grads, recv)]

    (p_down,), _ = _wgrad_paired("wgrad_down", act_t, dh2b, core, by_rows=True)
    (df, dh1, d_gmlp), (got_down,) = _mlp_bwd(dh2, dh2b, r_act, w_down_f, w_up_g, h1, norm_mlp_g,
                                              comm=_chips_plan([p_down]))
    (p_up,), _ = _wgrad_paired("wgrad_up", n2_t, df, core)
    g_out = _wgrad("wgrad_out", merged, dh1).reshape(N_SLOT, D // N_SLOT, D)
    (dz, dpa, dpb, dya, dyb), (r_out,) = _merge_bwd(
        dh1, z, pa, pb, w_out_f, w_oa_f, w_ob_f, comm=_sibling_plan([g_out]))
    (p_out,) = pair(["out"], [g_out], [r_out])
    g_oa = _wgrad("wgrad_out_a", ya, dpa).reshape(N_SLOT, D // N_SLOT, D)
    g_ob = _wgrad("wgrad_out_b", yb, dpb).reshape(N_SLOT, D // N_SLOT, D)
    (dz, d_ws, d_bs, d_ln), (got_out, r_oa, r_ob) = _branch_b_bwd(
        dz, dyb, z, sgu_ln_g, sgu_ln_b, sgu_w_s[0], b_s_t,
        comm=_join(_chips_plan([p_out]), _sibling_plan([g_oa, g_ob])))
    p_oa, p_ob = pair(["out_a", "out_b"], [g_oa, g_ob], [r_oa, r_ob])
    (dz, d_vec, d_wr, d_wi), (got_up, got_oa, got_ob) = _branch_a_bwd(
        dz, dya, z, hs, xc, r_gate, i_gate, conv_w_f, w_r_f, w_i_f, lru_lambda,
        comm=_chips_plan([p_up, p_oa, p_ob]))
    g_gate = jnp.stack([d_wr, d_wi]).reshape(2, HEADS, N_SLOT, 32, HEAD_DIM).transpose(2, 0, 1, 3, 4)
    g_gate = g_gate.reshape(N_SLOT, 2 * HEADS * 32, HEAD_DIM).astype(BF16)

    d_bs_row = jnp.pad(d_bs[:, :, 0].reshape(1, GROUPS * CHUNK), ((0, 0), (0, D - GROUPS * CHUNK)))
    vecs = jnp.concatenate([d_vec, jnp.concatenate([d_ln[0:2], d_gmlp, d_gfin, d_bs_row, jnp.zeros((3, D), F32)])])
    d_ws2 = d_ws.reshape(GROUPS * CHUNK, CHUNK)
    (p_in,), (r_gate, r_vecs, r_ws) = _wgrad_paired("wgrad_in", n1_t, dz, core,
                                                    comm=_sibling_plan([g_gate], [vecs, d_ws2]))
    (p_gate,) = pair(["gate"], [g_gate], [r_gate])
    vecs_chip = _add2("pair_sum_vecs", vecs, r_vecs)
    ws_chip = _add2("pair_sum_ws", d_ws2, r_ws)
    (dx, d_gmix), (got_in, got_gate, got_vecs, got_ws) = _in_bwd(
        dz, w_in_g, xs, dh1, norm_mix_g, comm=_chips_plan([p_in, p_gate], [vecs_chip, ws_chip]))
    vecs_sum = _sum_terms("sum_vecs", got_vecs)
    last = jnp.concatenate([d_gmix, jnp.pad(loss_acc[0:1], ((0, 0), (0, D - 128))), jnp.zeros((6, D), F32)])
    (last_all,) = _run_plan("exchange_last", _exchange_plan(last))
    last_sum = _sum_terms("sum_last", last_all)
    loss = last_sum[1, 0]
    got = [got_in, got_oa, got_ob, got_out, got_up, got_down, got_gate]

    def step(nm, terms, w, m, v, rows, cols):
        g, d, mn, vn = _adamw("adamw_" + nm, terms.reshape(4, rows, cols), w.reshape(rows, cols),
                              m.reshape(rows, cols), v.reshape(rows, cols))
        return [a.reshape(w.shape) for a in (g, d, mn, vn)]

    o_in = step("in", got[0], w_in, m_w_in, v_w_in, D, W_IN_COLS)
    o_oa = step("out_a", got[1], w_out_a, m_w_out_a, v_w_out_a, D // N_SLOT, D)
    o_ob = step("out_b", got[2], w_out_b, m_w_out_b, v_w_out_b, D // N_SLOT, D)
    o_out = step("out", got[3], w_out, m_w_out, v_w_out, D // N_SLOT, D)
    o_up = step("up", got[4], w_up, m_w_up, v_w_up, D, FF_COLS)
    o_down = step("down", got[5], w_down, m_w_down, v_w_down, FF_COLS, D)
    gate_w = jnp.stack([w_rgate[0], w_igate[0]]).reshape(2 * HEADS * 32, HEAD_DIM)
    gate_m = jnp.stack([m_w_rgate[0], m_w_igate[0]]).reshape(2 * HEADS * 32, HEAD_DIM)
    gate_v = jnp.stack([v_w_rgate[0], v_w_igate[0]]).reshape(2 * HEADS * 32, HEAD_DIM)
    o_gate = _adamw("adamw_gate", got[6], gate_w, gate_m, gate_v)
    o_gate = [a.reshape(2, 1, HEADS, 32, HEAD_DIM) for a in o_gate]
    o_wr = [a[0] for a in o_gate]
    o_wi = [a[1] for a in o_gate]

    def own(full, width):
        return lax.dynamic_slice_in_dim(full, me * width, width, axis=1)

    small_g = {
        "norm_mix_g": last_sum[0:1], "conv_w": own(vecs_sum[0:4], 128), "conv_b": vecs_sum[4:5],
        "b_rgate": own(vecs_sum[5:6].reshape(HEADS, HEAD_DIM), 32),
        "b_igate": own(vecs_sum[6:7].reshape(HEADS, HEAD_DIM), 32),
        "lru_lambda": vecs_sum[7:8], "sgu_ln_g": vecs_sum[8:9], "sgu_ln_b": vecs_sum[9:10],
        "norm_mlp_g": vecs_sum[10:11], "norm_final_g": vecs_sum[11:12],
        "sgu_b_s": vecs_sum[12, 0:GROUPS * CHUNK].reshape(GROUPS, CHUNK),
    }
    small_w = {"norm_mix_g": (norm_mix_g, m_norm_mix_g, v_norm_mix_g), "conv_w": (conv_w, m_conv_w, v_conv_w),
               "conv_b": (conv_b, m_conv_b, v_conv_b), "b_rgate": (b_rgate, m_b_rgate, v_b_rgate),
               "b_igate": (b_igate, m_b_igate, v_b_igate), "lru_lambda": (lru_lambda, m_lru_lambda, v_lru_lambda),
               "sgu_ln_g": (sgu_ln_g, m_sgu_ln_g, v_sgu_ln_g), "sgu_ln_b": (sgu_ln_b, m_sgu_ln_b, v_sgu_ln_b),
               "norm_mlp_g": (norm_mlp_g, m_norm_mlp_g, v_norm_mlp_g),
               "norm_final_g": (norm_final_g, m_norm_final_g, v_norm_final_g),
               "sgu_b_s": (sgu_b_s, m_sgu_b_s, v_sgu_b_s), "sgu_w_s": (sgu_w_s, m_sgu_w_s, v_sgu_w_s)}
    order = list(small_g)
    as2d = lambda k, a: a.reshape(small_g[k].shape)
    upd = _adamw_many("adamw_small", [small_g[k] for k in order], *[[as2d(k, small_w[k][q]) for k in order]
                                                                     for q in range(3)])
    o_small = {k: [a.reshape(small_w[k][0].shape) for a in (small_g[k],) + u] for k, u in zip(order, upd)}
    ws3 = [a[0].reshape(GROUPS * CHUNK, CHUNK) for a in small_w.pop("sgu_w_s")]
    o_small["sgu_w_s"] = [a.reshape(sgu_w_s.shape) for a in _adamw("adamw_ws", got_ws, *ws3)]

    per_weight = {"norm_mix_g": o_small["norm_mix_g"], "w_in": o_in, "conv_w": o_small["conv_w"],
                  "conv_b": o_small["conv_b"], "w_rgate": o_wr, "b_rgate": o_small["b_rgate"], "w_igate": o_wi,
                  "b_igate": o_small["b_igate"], "lru_lambda": o_small["lru_lambda"], "w_out_a": o_oa,
                  "sgu_ln_g": o_small["sgu_ln_g"], "sgu_ln_b": o_small["sgu_ln_b"], "sgu_w_s": o_small["sgu_w_s"],
                  "sgu_b_s": o_small["sgu_b_s"], "w_out_b": o_ob, "w_out": o_out, "norm_mlp_g": o_small["norm_mlp_g"],
                  "w_up": o_up, "w_down": o_down, "norm_final_g": o_small["norm_final_g"]}
    names_w = list(per_weight)
    return (loss, dx[None], *[per_weight[k][0] for k in names_w], *[per_weight[k][1] for k in names_w],
            *[per_weight[k][2] for k in names_w], *[per_weight[k][3] for k in names_w])
```

```python
import jax
import jax.numpy as jnp
from jax import lax
from jax.experimental import pallas as pl
from jax.experimental.pallas import tpu as pltpu

F32 = jnp.float32
BF16 = jnp.bfloat16
SDS = jax.ShapeDtypeStruct
MESH = pl.DeviceIdType.MESH
ANY = pl.BlockSpec(memory_space=pl.ANY)

D = 1024
N_SLOT = 8
W_IN_COLS = 768
FF_COLS = 512
HEADS, HEAD_DIM = 4, 256
GROUPS, GROUP_DIM = 4, 256
CHUNK = 128
CONV_K = 4
NORM_EPS = 1e-6
LN_EPS = 1e-5
LRU_C = 8.0
ADAM_LR, ADAM_B1, ADAM_B2, ADAM_EPS, ADAM_WD, ADAM_STEP = 0.001, 0.9, 0.999, 1e-08, 0.01, 10

TM_ROWS = 1024
TM_MERGE = 512
T_BRANCH_A = 512
T_BRANCH_B = 256
MiB = 1024 * 1024
SMALL_OPERAND = 16 * 1024

_GELU_C = 0.7978845608028654
_GELU_A = 0.044715


def _small_in_hbm(a):
    return pltpu.with_memory_space_constraint(a, pltpu.HBM) if a.size <= SMALL_OPERAND else a


def _cparams(sem, vmem_mib):
    return pltpu.CompilerParams(dimension_semantics=sem, vmem_limit_bytes=vmem_mib * MiB)


def _gelu(x):
    t = jnp.tanh(_GELU_C * (x + _GELU_A * x * x * x))
    return 0.5 * x * (1.0 + t)


def _gelu_and_grad(x):
    x2 = x * x
    t = jnp.tanh(_GELU_C * x * (1.0 + _GELU_A * x2))
    g = 0.5 * x * (1.0 + t)
    dg = 0.5 * (1.0 + t) + 0.5 * x * (1.0 - t * t) * _GELU_C * (1.0 + 3.0 * _GELU_A * x2)
    return g, dg


def _softplus(x):
    return jnp.maximum(x, 0.0) + jnp.log1p(jnp.exp(-jnp.abs(x)))


def _dot(a, b):
    return jnp.dot(a, b, preferred_element_type=F32)


def _dot_nt(a, b):
    return lax.dot_general(a, b, (((1,), (1,)), ((), ())), preferred_element_type=F32)


def _dot_tn(a, b):
    return lax.dot_general(a, b, (((0,), (0,)), ((), ())), preferred_element_type=F32)


def _rows_shifted(prev8, cur, k):
    ext = jnp.concatenate([prev8, cur], axis=0)
    return pltpu.roll(ext, k, 0)[8:]


def _rows_advanced(cur, next8, k):
    t = cur.shape[0]
    ext = jnp.concatenate([cur, next8], axis=0)
    return pltpu.roll(ext, t + 8 - k, 0)[:t]


def _first_second(x, y, c):
    ny, nx, far = _other_chips(x, y)
    pick = lambda a, b: a * (1 - c) + b * c
    first = tuple(pick(a, b) for a, b in zip(ny, nx))
    second = tuple(pick(b, a) for a, b in zip(ny, nx))
    return first, second, far


def _slot_order(x, y, c):
    chip = 2 * x + y
    first, second, far = _first_second(x, y, c)
    order = [2 * chip + c, 2 * chip + 1 - c, 2 * first[2] + c, 2 * second[2] + 1 - c, 2 * second[2] + c,
             2 * first[2] + 1 - c, 2 * far[2] + c, 2 * far[2] + 1 - c]
    return jnp.stack(order).astype(jnp.int32)


def _in_proj(x, g_mix, w_in_own, order, comm=None):
    s = x.shape[0]
    tm = min(TM_ROWS, s)
    ni = s // tm

    def body(order_ref, x_ref, g_ref, own_ref, z_ref, nt_ref, wg_ref, n_s, w_s, send_sems, recv_sems, local_sems):
        j, i = pl.program_id(0), pl.program_id(1)
        px, py, c = _place()
        chip = 2 * px + py
        me = 2 * chip + c
        sib = (px, py, 1 - c)
        chips = _other_chips(px, py)

        def rc(k, src, blk, to):
            return pltpu.make_async_remote_copy(src_ref=src, dst_ref=w_s.at[blk], send_sem=send_sems.at[k],
                                                recv_sem=recv_sems.at[k], device_id=to, device_id_type=MESH)

        del chips
        first, second, far = _first_second(px, py, c)
        blocks = [2 * first[2] + c, 2 * second[2] + c, 2 * far[2] + c]
        own_in = pltpu.make_async_copy(own_ref, w_s.at[me], local_sems.at[0])
        to_first = rc(1, own_ref, me, (first[0], first[1], c))
        to_second = rc(2, own_ref, me, (second[0], second[1], c))
        relay = rc(3, w_s.at[blocks[0]], blocks[0], (second[0], second[1], c))
        sends = [rc(0, own_ref, me, sib), to_first, to_second, relay]
        passed = [rc(4 + q, w_s.at[blk], blk, sib) for q, blk in enumerate(blocks)]
        keep = pltpu.make_async_copy(w_s, wg_ref, local_sems.at[1])

        @pl.when((i == 0) & (j == 0))
        def _():
            own_in.start()
            sends[0].start()
            to_first.start()
            own_in.wait()

        @pl.when((i == 0) & (j == 1))
        def _():
            rc(0, own_ref, 2 * chip + 1 - c, sib).wait_recv()

        for q, blk in enumerate(blocks):
            @pl.when((i == 0) & (j == 2 + 2 * q))
            def _():
                rc(1 + q, own_ref, blk, sib).wait_recv()
                passed[q].start()
                if q == 0:
                    to_second.start()
                    relay.start()

            @pl.when((i == 0) & (j == 3 + 2 * q))
            def _():
                rc(4 + q, own_ref, order_ref[j], sib).wait_recv()

        rows = pl.ds(pl.multiple_of(i * tm, tm), tm)

        @pl.when(j == 0)
        def _():
            xv = x_ref[...]
            rstd = lax.rsqrt(jnp.mean(xv * xv, axis=-1, keepdims=True) + NORM_EPS)
            nb = (xv * rstd * g_ref[...]).astype(BF16)
            n_s[rows, :] = nb
            nt_ref[...] = nb.T

        z_ref[...] = _dot(n_s[rows, :], w_s[order_ref[j]]).astype(BF16)

        @pl.when((i == 0) & (j == N_SLOT - 1))
        def _():
            keep.start()

        @pl.when((i == ni - 1) & (j == N_SLOT - 1))
        def _():
            for cp in sends + passed:
                cp.wait_send()
            keep.wait()

    first_pass = lambda j, i, o: (jnp.where(j == 0, i, ni - 1), 0)
    (z, n1, w_in_g), extra = _call(
        body, name="in_proj", grid=(N_SLOT, ni), prefetch=(order,),
        in_specs=[pl.BlockSpec((tm, D), first_pass),
                  pl.BlockSpec((1, D), lambda j, i, o: (0, 0)), ANY],
        out_specs=[pl.BlockSpec((tm, W_IN_COLS), lambda j, i, o: (i, o[j])),
                   pl.BlockSpec((D, tm), lambda j, i, o: (0, jnp.where(j == 0, i, ni - 1))), ANY],
        out_shape=[SDS((s, N_SLOT * W_IN_COLS), BF16), SDS((D, s), BF16), SDS((N_SLOT, D, W_IN_COLS), BF16)],
        scratch_shapes=[pltpu.VMEM((s, D), BF16), pltpu.VMEM((N_SLOT, D, W_IN_COLS), BF16),
                        pltpu.SemaphoreType.DMA((7,)), pltpu.SemaphoreType.DMA((7,)), pltpu.SemaphoreType.DMA((2,))],
        params=_cparams(("arbitrary", "arbitrary"), 56), args=(x, g_mix, w_in_own), comm=comm)
    return (z, n1, w_in_g), extra


def _decay(r, sp_lam):
    log_a = (-LRU_C) * r * sp_lam
    a = jnp.exp(log_a)
    return a, jnp.sqrt(-jnp.tanh(log_a) * (a * a + 1.0))


def _lru_gates(xc, xcb, wr_ref, br, wi_ref, bi, sp_lam, a_s, b_s, r_ref, i_ref):
    for h in range(HEADS):
        sl = slice(h * HEAD_DIM, (h + 1) * HEAD_DIM)
        r = jax.nn.sigmoid(_dot(xcb[:, sl], wr_ref[h]) + br[:, sl])
        ig = jax.nn.sigmoid(_dot(xcb[:, sl], wi_ref[h]) + bi[:, sl])
        a, mult = _decay(r, sp_lam[:, sl])
        a_s[:, sl] = a
        b_s[:, sl] = xc[:, sl] * ig * mult
        r_ref[:, sl] = r.astype(BF16)
        i_ref[:, sl] = ig.astype(BF16)


def _conv_fwd(xa, prev8, cw, cb):
    xc = cb + cw[0:1, :] * xa
    for k in range(1, CONV_K):
        xc = xc + cw[k:k + 1, :] * _rows_shifted(prev8, xa, k)
    return xc


def _branch_a_fwd(z, conv_w, conv_b, w_r, b_r, w_i, b_i, lam, ta):
    s = z.shape[0]
    per16 = ta // 16

    def body(xa_ref, xp_ref, ga_ref, cw_ref, cb_ref, wr_ref, br_ref, wi_ref, bi_ref, lam_ref,
             ya_ref, hs_ref, xc_ref, r_ref, i_ref, a_s, b_s, h_s, carry_s):
        i = pl.program_id(0)

        @pl.when(i == 0)
        def _():
            carry_s[...] = jnp.zeros_like(carry_s)

        xa = xa_ref[...].astype(F32)
        prev8 = jnp.where(i > 0, xp_ref[...].astype(F32)[8:16], 0.0)
        xc = _conv_fwd(xa, prev8, cw_ref[...], cb_ref[...])
        xcb = xc.astype(BF16)
        xc_ref[...] = xcb
        sp_lam = _softplus(-lam_ref[...])
        _lru_gates(xc, xcb, wr_ref, br_ref[...], wi_ref, bi_ref[...], sp_lam, a_s, b_s, r_ref, i_ref)

        row = lax.broadcasted_iota(jnp.int32, (8, D), 0)

        def group(g, carry):
            off = pl.multiple_of(g * 8, 8)
            a8 = a_s[pl.ds(off, 8), :]
            b8 = b_s[pl.ds(off, 8), :]
            for d in (1, 2, 4):
                a_sh = jnp.where(row >= d, pltpu.roll(a8, d, 0), 1.0)
                b_sh = jnp.where(row >= d, pltpu.roll(b8, d, 0), 0.0)
                b8 = a8 * b_sh + b8
                a8 = a8 * a_sh
            h8 = b8 + a8 * carry
            h_s[pl.ds(off, 8), :] = h8
            return jnp.broadcast_to(h8[7:8, :], (8, D))

        carry_s[...] = lax.fori_loop(0, ta // 8, group, carry_s[...])
        hs = h_s[...]
        hs_ref[...] = hs.astype(BF16)
        ya_ref[...] = (hs * _gelu(ga_ref[...].astype(F32))).astype(BF16)

    vec = pl.BlockSpec((1, D), lambda i: (0, 0))
    gate = pl.BlockSpec((HEADS, HEAD_DIM, HEAD_DIM), lambda i: (0, 0, 0))
    return dict(
        body=body,
        in_specs=[pl.BlockSpec((ta, D), lambda i: (i, 0)),
                  pl.BlockSpec((16, D), lambda i: (jnp.maximum(i * per16 - 1, 0), 0)),
                  pl.BlockSpec((ta, D), lambda i: (i, 1)),
                  pl.BlockSpec((CONV_K, D), lambda i: (0, 0)), vec, gate, vec, gate, vec, vec],
        out_specs=[pl.BlockSpec((ta, D), lambda i: (i, 0))] * 5,
        out_shape=[SDS((s, D), BF16)] * 5,
        scratch_shapes=[pltpu.VMEM((ta, D), F32), pltpu.VMEM((ta, D), F32), pltpu.VMEM((ta, D), F32),
                        pltpu.VMEM((8, D), F32)],
        args=(z, z, z, conv_w, conv_b, w_r, b_r, w_i, b_i, lam))


def _sgu_common(ub, vb, lg, lb, with_grad):
    if with_grad:
        u, du = _gelu_and_grad(ub)
        v, dv = _gelu_and_grad(vb)
    else:
        u, v, du, dv = _gelu(ub), _gelu(vb), None, None
    mu = jnp.mean(v, axis=-1, keepdims=True)
    vc = v - mu
    rstd = lax.rsqrt(jnp.mean(vc * vc, axis=-1, keepdims=True) + LN_EPS)
    vhat = vc * rstd
    vln = vhat * lg + lb
    return u, du, dv, rstd, vhat, vln


def _masked_ws(ws_ref):
    t = lax.broadcasted_iota(jnp.int32, (CHUNK, CHUNK), 0)
    c = lax.broadcasted_iota(jnp.int32, (CHUNK, CHUNK), 1)
    keep = c <= t
    return [jnp.where(keep, ws_ref[g], 0.0).astype(BF16) for g in range(GROUPS)]


def _branch_b_fwd(z, ln_g, ln_b, w_s, b_s_t, tb):
    s = z.shape[0]

    def body(ub_ref, vb_ref, lg_ref, lb_ref, ws_ref, bs_ref, yb_ref):
        u, _, _, _, _, vln = _sgu_common(ub_ref[...].astype(F32), vb_ref[...].astype(F32),
                                         lg_ref[...], lb_ref[...], False)
        vlnb = vln.astype(BF16)
        wm = _masked_ws(ws_ref)
        bs = bs_ref[...]
        for c in range(tb // CHUNK):
            rs = slice(c * CHUNK, (c + 1) * CHUNK)
            for g in range(GROUPS):
                cs = slice(g * GROUP_DIM, (g + 1) * GROUP_DIM)
                sp = _dot(wm[g], vlnb[rs, cs]) + bs[:, g:g + 1]
                yb_ref[rs, cs] = (u[rs, cs] * sp).astype(BF16)

    vec = pl.BlockSpec((1, D), lambda i: (0, 0))
    return dict(
        body=body,
        in_specs=[pl.BlockSpec((tb, D), lambda i: (i, 2)), pl.BlockSpec((tb, D), lambda i: (i, 3)), vec, vec,
                  pl.BlockSpec((GROUPS, CHUNK, CHUNK), lambda i: (0, 0, 0)),
                  pl.BlockSpec((CHUNK, GROUPS), lambda i: (0, 0))],
        out_specs=[pl.BlockSpec((tb, D), lambda i: (i, 0))],
        out_shape=[SDS((s, D), BF16)], scratch_shapes=[], args=(z, z, ln_g, ln_b, w_s, b_s_t))


def _branches_fwd(z, branch_a, branch_b, comm=None):
    s = z.shape[0]
    ta = min(T_BRANCH_A, s)
    parts = [_branch_a_fwd(z, *branch_a, ta), _branch_b_fwd(z, *branch_b, ta)]
    n_in, n_out = [len(p["in_specs"]) for p in parts], [len(p["out_specs"]) for p in parts]
    n_scr = [len(p["scratch_shapes"]) for p in parts]

    def body(*refs):
        at = 0
        groups = []
        for sizes in (n_in, n_out, n_scr):
            groups.append([])
            for k in sizes:
                groups[-1].append(refs[at:at + k])
                at += k
        for p, ins, outs, scr in zip(parts, *groups):
            p["body"](*ins, *outs, *scr)

    every = lambda key: [v for p in parts for v in p[key]]
    return _call(body, name="branches_fwd", grid=(s // ta,), in_specs=every("in_specs"),
                 out_specs=every("out_specs"), out_shape=every("out_shape"), scratch_shapes=every("scratch_shapes"),
                 params=_cparams(("arbitrary",), 48), args=tuple(every("args")), comm=comm)


def _merge_out(ya, yb, z, x, w_oa, w_ob, w_out, comm=None):
    s = x.shape[0]
    tm = min(TM_MERGE, s)

    def body(ya_ref, yb_ref, ma_ref, mb_ref, x_ref, woa_ref, wob_ref, wo_ref, pa_ref, pb_ref, mg_ref, h1_ref):
        pa = _dot(ya_ref[...], woa_ref[...])
        pb = _dot(yb_ref[...], wob_ref[...])
        merged = (jax.nn.sigmoid(ma_ref[...].astype(F32)) * pa
                  + jax.nn.sigmoid(mb_ref[...].astype(F32)) * pb).astype(BF16)
        pa_ref[...] = pa.astype(BF16)
        pb_ref[...] = pb.astype(BF16)
        mg_ref[...] = merged
        h1_ref[...] = x_ref[...] + _dot(merged, wo_ref[...])

    row = pl.BlockSpec((tm, D), lambda i: (i, 0))
    wsp = pl.BlockSpec((D, D), lambda i: (0, 0))
    return _call(
        body, name="merge_out", grid=(s // tm,),
        in_specs=[row, row, pl.BlockSpec((tm, D), lambda i: (i, 4)), pl.BlockSpec((tm, D), lambda i: (i, 5)),
                  row, wsp, wsp, wsp],
        out_specs=[row, row, row, row],
        out_shape=[SDS((s, D), BF16), SDS((s, D), BF16), SDS((s, D), BF16), SDS((s, D), F32)], scratch_shapes=[],
        params=_cparams(("arbitrary",), 48), args=(ya, yb, z, z, x, w_oa, w_ob, w_out), comm=comm)


def _mlp_fwd(h1, g_mlp, w_up_g, w_down, g_fin, tgt):
    s = h1.shape[0]
    tm = min(TM_ROWS, s)
    nj = N_SLOT

    def body(h1_ref, gm_ref, wu_ref, wd_ref, gf_ref, t_ref, r_ref, at_ref, n2t_ref, dh2_ref, dh2b_ref, loss_ref,
             dgf_ref, n2_s, acc_s):
        i, j = pl.program_id(0), pl.program_id(1)

        @pl.when(j == 0)
        def _():
            hv = h1_ref[...]
            rstd = lax.rsqrt(jnp.mean(hv * hv, axis=-1, keepdims=True) + NORM_EPS)
            nb = (hv * rstd * gm_ref[...]).astype(BF16)
            n2_s[...] = nb
            n2t_ref[...] = nb.T
            acc_s[...] = jnp.zeros_like(acc_s)

        @pl.when((i == 0) & (j == 0))
        def _():
            loss_ref[...] = jnp.zeros_like(loss_ref)
            dgf_ref[...] = jnp.zeros_like(dgf_ref)

        r = jnp.maximum(_dot(n2_s[...], wu_ref[...]), 0.0)
        r_ref[...] = r.astype(BF16)
        act = (r * r).astype(BF16)
        at_ref[...] = act.T
        acc_s[...] += _dot(act, wd_ref[...])

        @pl.when(j == nj - 1)
        def _():
            h2 = h1_ref[...] + acc_s[...]
            rstd = lax.rsqrt(jnp.mean(h2 * h2, axis=-1, keepdims=True) + NORM_EPS)
            hh = h2 * rstd
            gf = gf_ref[...]
            e = hh * gf - t_ref[...]
            loss_ref[...] += jnp.sum(e * e) * (0.5 / D)
            dy = e * (1.0 / D)
            dgf_ref[...] += jnp.sum(dy * hh, axis=0, keepdims=True)
            dhh = dy * gf
            dh2 = rstd * (dhh - hh * jnp.mean(dhh * hh, axis=-1, keepdims=True))
            dh2_ref[...] = dh2
            dh2b_ref[...] = dh2.astype(BF16)

    row = pl.BlockSpec((tm, D), lambda i, j: (i, 0))
    vec = pl.BlockSpec((1, D), lambda i, j: (0, 0))
    return pl.pallas_call(
        body, name="mlp_fwd", grid=(s // tm, nj),
        in_specs=[row, vec, pl.BlockSpec((None, D, FF_COLS), lambda i, j: (j, 0, 0)),
                  pl.BlockSpec((FF_COLS, D), lambda i, j: (j, 0)), vec, row],
        out_specs=[pl.BlockSpec((tm, FF_COLS), lambda i, j: (i, j)), pl.BlockSpec((FF_COLS, tm), lambda i, j: (j, i)),
                   pl.BlockSpec((D, tm), lambda i, j: (0, i)), row, row, pl.BlockSpec((8, 128), lambda i, j: (0, 0)),
                   vec],
        out_shape=[SDS((s, nj * FF_COLS), BF16), SDS((nj * FF_COLS, s), BF16), SDS((D, s), BF16), SDS((s, D), F32),
                   SDS((s, D), BF16), SDS((8, 128), F32), SDS((1, D), F32)],
        scratch_shapes=[pltpu.VMEM((tm, D), BF16), pltpu.VMEM((tm, D), F32)],
        compiler_params=_cparams(("arbitrary", "arbitrary"), 56),
    )(h1, _small_in_hbm(g_mlp), w_up_g, w_down, _small_in_hbm(g_fin), tgt)


def _mlp_bwd(dh2, dh2b, r, w_down, w_up_g, h1, g_mlp, comm=None):
    s = h1.shape[0]
    tm = min(TM_ROWS, s)
    nj = N_SLOT

    def body(dh2_ref, dh2b_ref, r_ref, wd_ref, wu_ref, h1_ref, gm_ref, df_ref, dh1_ref, dgm_ref, acc_s):
        i, j = pl.program_id(0), pl.program_id(1)

        @pl.when(j == 0)
        def _():
            acc_s[...] = jnp.zeros_like(acc_s)

        @pl.when((i == 0) & (j == 0))
        def _():
            dgm_ref[...] = jnp.zeros_like(dgm_ref)

        d_act = _dot_nt(dh2b_ref[...], wd_ref[...])
        df = (d_act * (2.0 * r_ref[...].astype(F32))).astype(BF16)
        df_ref[...] = df
        acc_s[...] += _dot_nt(df, wu_ref[...])

        @pl.when(j == nj - 1)
        def _():
            hv = h1_ref[...]
            rstd = lax.rsqrt(jnp.mean(hv * hv, axis=-1, keepdims=True) + NORM_EPS)
            hh = hv * rstd
            dn2 = acc_s[...]
            dgm_ref[...] += jnp.sum(dn2 * hh, axis=0, keepdims=True)
            dhat = dn2 * gm_ref[...]
            dh1_ref[...] = dh2_ref[...] + rstd * (dhat - hh * jnp.mean(dhat * hh, axis=-1, keepdims=True))

    row = pl.BlockSpec((tm, D), lambda i, j: (i, 0))
    vec = pl.BlockSpec((1, D), lambda i, j: (0, 0))
    ffb = pl.BlockSpec((tm, FF_COLS), lambda i, j: (i, j))
    return _call(
        body, name="mlp_bwd", grid=(s // tm, nj),
        in_specs=[row, row, ffb, pl.BlockSpec((FF_COLS, D), lambda i, j: (j, 0)),
                  pl.BlockSpec((None, D, FF_COLS), lambda i, j: (j, 0, 0)), row, vec],
        out_specs=[ffb, row, vec],
        out_shape=[SDS((s, nj * FF_COLS), BF16), SDS((s, D), F32), SDS((1, D), F32)],
        scratch_shapes=[pltpu.VMEM((tm, D), F32)],
        params=_cparams(("arbitrary", "arbitrary"), 56), args=(dh2, dh2b, r, w_down, w_up_g, h1, g_mlp), comm=comm)


def _merge_bwd(dh1, z, pa, pb, w_out, w_oa, w_ob, comm=None):
    s = dh1.shape[0]
    tm = min(TM_MERGE, s)

    def body(dh1_ref, ma_ref, mb_ref, pa_ref, pb_ref, wo_ref, woa_ref, wob_ref,
             dz_ref, dpa_ref, dpb_ref, dya_ref, dyb_ref):
        dm = _dot_nt(dh1_ref[...].astype(BF16), wo_ref[...])
        sa = jax.nn.sigmoid(ma_ref[...].astype(F32))
        sb = jax.nn.sigmoid(mb_ref[...].astype(F32))
        dpa = (dm * sa).astype(BF16)
        dpb = (dm * sb).astype(BF16)
        dz_ref[:, 0:D] = (dm * pa_ref[...].astype(F32) * sa * (1.0 - sa)).astype(BF16)
        dz_ref[:, D:2 * D] = (dm * pb_ref[...].astype(F32) * sb * (1.0 - sb)).astype(BF16)
        dpa_ref[...] = dpa
        dpb_ref[...] = dpb
        dya_ref[...] = _dot_nt(dpa, woa_ref[...]).astype(BF16)
        dyb_ref[...] = _dot_nt(dpb, wob_ref[...]).astype(BF16)

    row = pl.BlockSpec((tm, D), lambda i: (i, 0))
    wsp = pl.BlockSpec((D, D), lambda i: (0, 0))
    return _call(
        body, name="merge_bwd", grid=(s // tm,),
        in_specs=[row, pl.BlockSpec((tm, D), lambda i: (i, 4)), pl.BlockSpec((tm, D), lambda i: (i, 5)),
                  row, row, wsp, wsp, wsp],
        out_specs=[pl.BlockSpec((tm, 2 * D), lambda i: (i, 2)), row, row, row, row],
        out_shape=[SDS((s, 6 * D), BF16)] + [SDS((s, D), BF16)] * 4, scratch_shapes=[],
        params=_cparams(("arbitrary",), 48), args=(dh1, z, z, pa, pb, w_out, w_oa, w_ob), comm=comm)


def _branch_b_bwd(dz, dyb, z, ln_g, ln_b, w_s, b_s_t, comm=None):
    s = z.shape[0]
    tb = min(T_BRANCH_B, s)

    def body(dz_in, dyb_ref, ub_ref, vb_ref, lg_ref, lb_ref, ws_ref, bs_ref,
             dz_ref, dws_ref, dbs_ref, dln_ref, du_s, dvln_s):
        del dz_in

        @pl.when(pl.program_id(0) == 0)
        def _():
            dws_ref[...] = jnp.zeros_like(dws_ref)
            dbs_ref[...] = jnp.zeros_like(dbs_ref)
            dln_ref[...] = jnp.zeros_like(dln_ref)

        lg = lg_ref[...]
        u, du, dv, rstd, vhat, vln = _sgu_common(ub_ref[...].astype(F32), vb_ref[...].astype(F32),
                                                 lg, lb_ref[...], True)
        vlnb = vln.astype(BF16)
        dyb_v = dyb_ref[...].astype(F32)
        wm = _masked_ws(ws_ref)
        keep = (lax.broadcasted_iota(jnp.int32, (CHUNK, CHUNK), 1)
                <= lax.broadcasted_iota(jnp.int32, (CHUNK, CHUNK), 0))
        bs = bs_ref[...]
        for c in range(tb // CHUNK):
            rs = slice(c * CHUNK, (c + 1) * CHUNK)
            for g in range(GROUPS):
                cs = slice(g * GROUP_DIM, (g + 1) * GROUP_DIM)
                v_blk = vlnb[rs, cs]
                sp = _dot(wm[g], v_blk) + bs[:, g:g + 1]
                d_sp = dyb_v[rs, cs] * u[rs, cs]
                d_spb = d_sp.astype(BF16)
                du_s[rs, cs] = dyb_v[rs, cs] * sp
                dvln_s[rs, cs] = _dot_tn(wm[g], d_spb)
                dws_ref[g] += jnp.where(keep, _dot_nt(d_spb, v_blk), 0.0)
                dbs_ref[g] += jnp.broadcast_to(jnp.sum(d_sp, axis=-1, keepdims=True), (CHUNK, CHUNK))
        dvln = dvln_s[...]
        dln_ref[0:1, :] += jnp.sum(dvln * vhat, axis=0, keepdims=True)
        dln_ref[1:2, :] += jnp.sum(dvln, axis=0, keepdims=True)
        dvh = dvln * lg
        d_v = rstd * (dvh - jnp.mean(dvh, axis=-1, keepdims=True)
                      - vhat * jnp.mean(dvh * vhat, axis=-1, keepdims=True))
        dz_ref[:, 0:D] = (du_s[...] * du).astype(BF16)
        dz_ref[:, D:2 * D] = (d_v * dv).astype(BF16)

    vec = pl.BlockSpec((1, D), lambda i: (0, 0))
    sq = pl.BlockSpec((GROUPS, CHUNK, CHUNK), lambda i: (0, 0, 0))
    return _call(
        body, name="branch_b_bwd", grid=(s // tb,),
        in_specs=[ANY, pl.BlockSpec((tb, D), lambda i: (i, 0)),
                  pl.BlockSpec((tb, D), lambda i: (i, 2)), pl.BlockSpec((tb, D), lambda i: (i, 3)), vec, vec, sq,
                  pl.BlockSpec((CHUNK, GROUPS), lambda i: (0, 0))],
        out_specs=[pl.BlockSpec((tb, 2 * D), lambda i: (i, 1)), sq, sq, pl.BlockSpec((8, D), lambda i: (0, 0))],
        out_shape=[SDS(dz.shape, BF16), SDS((GROUPS, CHUNK, CHUNK), F32), SDS((GROUPS, CHUNK, CHUNK), F32),
                   SDS((8, D), F32)],
        scratch_shapes=[pltpu.VMEM((tb, D), F32), pltpu.VMEM((tb, D), F32)], aliases={0: 0},
        params=_cparams(("arbitrary",), 40), args=(dz, dyb, z, z, ln_g, ln_b, w_s, b_s_t), comm=comm)


def _branch_a_bwd(dz, dya, z, hs, xc, r, ig, conv_w, w_r, w_i, lam, comm=None):
    s = z.shape[0]
    ta = min(T_BRANCH_A, s)
    nb = s // ta
    per16 = ta // 16

    def body(dz_in, dya_ref, xa_ref, ga_ref, hs_ref, hp_ref, xc_ref, r_ref, i_ref, cw_ref, wr_ref, wi_ref,
             lam_ref, dz_ref, vec_ref, dwr_ref, dwi_ref, a_s, b_s, h_s, dcar_s, acar_s, dxc_s):
        del dz_in
        i = pl.program_id(0)
        blk = nb - 1 - i

        @pl.when(i == 0)
        def _():
            dcar_s[...] = jnp.zeros_like(dcar_s)
            acar_s[...] = jnp.zeros_like(acar_s)
            dxc_s[...] = jnp.zeros_like(dxc_s)
            vec_ref[...] = jnp.zeros_like(vec_ref)
            dwr_ref[...] = jnp.zeros_like(dwr_ref)
            dwi_ref[...] = jnp.zeros_like(dwi_ref)

        cw = cw_ref[...]
        lam_v = lam_ref[...]
        xa = xa_ref[...].astype(F32)
        xcb = xc_ref[...]
        xc = xcb.astype(F32)
        sp_lam = _softplus(-lam_v)
        r_v, i_v = r_ref[...].astype(F32), i_ref[...].astype(F32)
        a_v, m_v = _decay(r_v, sp_lam)

        hs_v = hs_ref[...].astype(F32)
        hprev8 = jnp.where(blk > 0, hp_ref[...].astype(F32)[8:16], 0.0)
        h_m1 = _rows_shifted(hprev8, hs_v, 1)
        gg, dgg = _gelu_and_grad(ga_ref[...].astype(F32))
        dya_v = dya_ref[...].astype(F32)
        dz_ref[:, D:2 * D] = (dya_v * hs_v * dgg).astype(BF16)

        a_s[...] = _rows_advanced(a_v, acar_s[...], 1)
        b_s[...] = dya_v * gg

        row = lax.broadcasted_iota(jnp.int32, (8, D), 0)
        ng = ta // 8

        def group(gi, carry):
            off = pl.multiple_of((ng - 1 - gi) * 8, 8)
            c8 = a_s[pl.ds(off, 8), :]
            d8 = b_s[pl.ds(off, 8), :]
            for d in (1, 2, 4):
                c_sh = jnp.where(row < 8 - d, pltpu.roll(c8, 8 - d, 0), 1.0)
                d_sh = jnp.where(row < 8 - d, pltpu.roll(d8, 8 - d, 0), 0.0)
                d8 = c8 * d_sh + d8
                c8 = c8 * c_sh
            dh8 = d8 + c8 * carry
            h_s[pl.ds(off, 8), :] = dh8
            return jnp.broadcast_to(dh8[0:1, :], (8, D))

        dcar_s[...] = lax.fori_loop(0, ng, group, dcar_s[...])
        acar_s[...] = jnp.broadcast_to(a_v[0:1, :], (8, D))

        dbx = h_s[...]
        d_mult = dbx * xc * i_v
        d_loga = dbx * h_m1 * a_v - d_mult * (a_v * a_v) / m_v
        d_pr = d_loga * ((-LRU_C) * sp_lam) * r_v * (1.0 - r_v)
        d_pi = dbx * xc * m_v * i_v * (1.0 - i_v)
        vec_ref[7:8, :] += jnp.sum(d_loga * r_v, axis=0, keepdims=True) * (LRU_C * jax.nn.sigmoid(-lam_v))
        vec_ref[5:6, :] += jnp.sum(d_pr, axis=0, keepdims=True)
        vec_ref[6:7, :] += jnp.sum(d_pi, axis=0, keepdims=True)
        d_prb = d_pr.astype(BF16)
        d_pib = d_pi.astype(BF16)
        h_s[...] = dbx * i_v * m_v
        for h in range(HEADS):
            sl = slice(h * HEAD_DIM, (h + 1) * HEAD_DIM)
            h_s[:, sl] += _dot_nt(d_prb[:, sl], wr_ref[h]) + _dot_nt(d_pib[:, sl], wi_ref[h])
            dwr_ref[h] += _dot_tn(xcb[:, sl], d_prb[:, sl])
            dwi_ref[h] += _dot_tn(xcb[:, sl], d_pib[:, sl])
        d_xc = h_s[...]
        vec_ref[4:5, :] += jnp.sum(d_xc, axis=0, keepdims=True)
        vec_ref[0:1, :] += jnp.sum(d_xc * xa, axis=0, keepdims=True)
        d_xa = cw[0:1, :] * d_xc
        nxt = dxc_s[...]
        for k in range(1, CONV_K):
            ahead = _rows_advanced(d_xc, nxt, k)
            vec_ref[k:k + 1, :] += jnp.sum(ahead * xa, axis=0, keepdims=True)
            d_xa = d_xa + cw[k:k + 1, :] * ahead
        dz_ref[:, 0:D] = d_xa.astype(BF16)
        dxc_s[...] = d_xc[0:8, :]

    vec = pl.BlockSpec((1, D), lambda i: (0, 0))
    gate = pl.BlockSpec((HEADS, HEAD_DIM, HEAD_DIM), lambda i: (0, 0, 0))
    cur = lambda c: pl.BlockSpec((ta, D), lambda i: (nb - 1 - i, c))
    before = lambda c: pl.BlockSpec((16, D), lambda i: (jnp.maximum((nb - 1 - i) * per16 - 1, 0), c))
    return _call(
        body, name="branch_a_bwd", grid=(nb,),
        in_specs=[ANY, cur(0), cur(0), cur(1), cur(0), before(0), cur(0), cur(0), cur(0),
                  pl.BlockSpec((CONV_K, D), lambda i: (0, 0)), gate, gate, vec],
        out_specs=[pl.BlockSpec((ta, 2 * D), lambda i: (nb - 1 - i, 0)), pl.BlockSpec((8, D), lambda i: (0, 0)),
                   gate, gate],
        out_shape=[SDS(dz.shape, BF16), SDS((8, D), F32), SDS((HEADS, HEAD_DIM, HEAD_DIM), F32),
                   SDS((HEADS, HEAD_DIM, HEAD_DIM), F32)],
        scratch_shapes=[pltpu.VMEM((ta, D), F32)] * 3 + [pltpu.VMEM((8, D), F32)] * 3, aliases={0: 0},
        params=_cparams(("arbitrary",), 48),
        args=(dz, dya, z, z, hs, hs, xc, r, ig, conv_w, w_r, w_i, lam), comm=comm)


def _in_bwd(dz, w_in_g, x, dh1, g_mix, comm=None):
    s = x.shape[0]
    tm = min(TM_ROWS, s)
    nj = N_SLOT

    def body(dz_ref, w_ref, x_ref, dh1_ref, g_ref, dx_ref, dg_ref, acc_s):
        i, j = pl.program_id(0), pl.program_id(1)

        @pl.when(j == 0)
        def _():
            acc_s[...] = jnp.zeros_like(acc_s)

        @pl.when((i == 0) & (j == 0))
        def _():
            dg_ref[...] = jnp.zeros_like(dg_ref)

        acc_s[...] += _dot_nt(dz_ref[...], w_ref[...])

        @pl.when(j == nj - 1)
        def _():
            xv = x_ref[...]
            rstd = lax.rsqrt(jnp.mean(xv * xv, axis=-1, keepdims=True) + NORM_EPS)
            xh = xv * rstd
            dn = acc_s[...]
            dg_ref[...] += jnp.sum(dn * xh, axis=0, keepdims=True)
            dhat = dn * g_ref[...]
            dx_ref[...] = dh1_ref[...] + rstd * (dhat - xh * jnp.mean(dhat * xh, axis=-1, keepdims=True))

    row = pl.BlockSpec((tm, D), lambda i, j: (i, 0))
    vec = pl.BlockSpec((1, D), lambda i, j: (0, 0))
    return _call(
        body, name="in_bwd", grid=(s // tm, nj),
        in_specs=[pl.BlockSpec((tm, W_IN_COLS), lambda i, j: (i, j)),
                  pl.BlockSpec((None, D, W_IN_COLS), lambda i, j: (j, 0, 0)), row, row, vec],
        out_specs=[row, vec],
        out_shape=[SDS((s, D), F32), SDS((1, D), F32)],
        scratch_shapes=[pltpu.VMEM((tm, D), F32)],
        params=_cparams(("arbitrary", "arbitrary"), 48), args=(dz, w_in_g, x, dh1, g_mix), comm=comm)


def _wgrad(name, a, b):
    s = a.shape[0]
    ts = min(TM_ROWS, s)
    a_w, b_w = a.shape[1], b.shape[1]

    def body(a_ref, b_ref, o_ref, acc_s):
        t = pl.program_id(0)

        @pl.when(t == 0)
        def _():
            acc_s[...] = jnp.zeros_like(acc_s)

        acc_s[...] += _dot_tn(a_ref[...].astype(BF16), b_ref[...].astype(BF16))

        @pl.when(t == pl.num_programs(0) - 1)
        def _():
            o_ref[...] = acc_s[...].astype(BF16)

    return pl.pallas_call(
        body, name=name, grid=(s // ts,),
        in_specs=[pl.BlockSpec((ts, a_w), lambda t: (t, 0)), pl.BlockSpec((ts, b_w), lambda t: (t, 0))],
        out_specs=pl.BlockSpec((a_w, b_w), lambda t: (0, 0)),
        out_shape=SDS((a_w, b_w), BF16),
        scratch_shapes=[pltpu.VMEM((a_w, b_w), F32)],
        compiler_params=_cparams(("arbitrary",), 48),
    )(a, b)


def _place():
    x, y, c = lax.axis_index("x"), lax.axis_index("y"), lax.axis_index("c")
    return x, y, c


def _other_chips(x, y):
    return [(x, 1 - y, 2 * x + 1 - y), (1 - x, y, 2 * (1 - x) + y), (1 - x, 1 - y, 2 * (1 - x) + 1 - y)]


class _Plan:
    def __init__(self, arrays, out_shape, sems, start, finish, middle=None, middle_at=6):
        self.arrays, self.out_shape, self.sems, self.start, self.finish = arrays, out_shape, sems, start, finish
        self.middle, self.middle_at = middle, middle_at


def _gather_plan(shards, middle_at=6):
    n = len(shards)

    def copies(ins, outs, sems):
        send_sems, recv_sems, local_sems = sems
        x, y, c = _place()
        chip = 2 * x + y
        me = 2 * chip + c
        sib = (x, y, 1 - c)
        chips = _other_chips(x, y)

        def rc(k, t, src, blk, to):
            return pltpu.make_async_remote_copy(
                src_ref=src, dst_ref=outs[t].at[blk], send_sem=send_sems.at[k * n + t],
                recv_sem=recv_sems.at[k * n + t], device_id=to, device_id_type=MESH)

        (yx, yy, y_chip), (xx, xy, x_chip), _ = chips
        local = [pltpu.make_async_copy(ins[t], outs[t].at[me], local_sems.at[t]) for t in range(n)]
        sends = ([rc(0, t, ins[t], me, sib) for t in range(n)] + [rc(1, t, ins[t], me, (yx, yy, c)) for t in range(n)]
                 + [rc(2, t, ins[t], me, (xx, xy, c)) for t in range(n)])
        passed = [[rc(4 + j, t, outs[t].at[2 * pc + c], 2 * pc + c, sib) for t in range(n)]
                  for j, (_, _, pc) in enumerate(chips)]
        relays = [[rc(3, t, outs[t].at[2 * y_chip + c], 2 * y_chip + c, (xx, xy, c)) for t in range(n)],
                  [rc(3, t, outs[t].at[2 * x_chip + c], 2 * x_chip + c, (yx, yy, c)) for t in range(n)]]
        return rc, local, sends, passed, relays, chips, chip, c, sib

    def start(ins, outs, sems):
        _, local, sends, _, _, _, _, _, _ = copies(ins, outs, sems)
        for cp in local + sends:
            cp.start()

    def middle(ins, outs, sems):
        rc, _, _, passed, relays, chips, _, c, sib = copies(ins, outs, sems)
        for j in range(2):
            for t in range(n):
                rc(1 + j, t, ins[t], 2 * chips[j][2] + c, sib).wait_recv()
        for j in range(2):
            for cp in passed[j]:
                cp.start()

            @pl.when(c == j)
            def _():
                for cp in relays[j]:
                    cp.start()

    def finish(ins, outs, sems):
        rc, local, sends, passed, relays, chips, chip, c, sib = copies(ins, outs, sems)
        far = 2 * chips[2][2] + c
        for t in range(n):
            rc(3, t, ins[t], far, sib).wait_recv()
        for cp in passed[2]:
            cp.start()
        for t in range(n):
            rc(0, t, ins[t], 2 * chip + 1 - c, sib).wait_recv()
        for j, (px, py, pc) in enumerate(chips):
            for t in range(n):
                rc(4 + j, t, ins[t], 2 * pc + 1 - c, sib).wait_recv()
        for cp in sends + passed[0] + passed[1] + passed[2]:
            cp.wait_send()
        for j in range(2):
            @pl.when(c == j)
            def _():
                for cp in relays[j]:
                    cp.wait_send()
        for cp in local:
            cp.wait()

    return _Plan(list(shards), [SDS((N_SLOT,) + tuple(a.shape), a.dtype) for a in shards],
                 [pltpu.SemaphoreType.DMA((7 * n,)), pltpu.SemaphoreType.DMA((7 * n,)),
                  pltpu.SemaphoreType.DMA((n,))], start, finish, middle, middle_at)


def _sibling_plan(grads, whole=()):
    n, m = len(grads), len(whole)

    def copies(ins, outs, sems):
        send_sems, recv_sems = sems
        x, y, c = _place()
        sib = (x, y, 1 - c)

        def rc(t, src, dst):
            return pltpu.make_async_remote_copy(src_ref=src, dst_ref=dst, send_sem=send_sems.at[t],
                                                recv_sem=recv_sems.at[t], device_id=sib, device_id_type=MESH)
        return rc, c

    def start(ins, outs, sems):
        rc, c = copies(ins, outs, sems)
        for t in range(n):
            for j in range(4):
                rc(t, ins[t].at[2 * j + 1 - c], outs[t].at[j]).start()
        for t in range(n, n + m):
            rc(t, ins[t], outs[t]).start()

    def finish(ins, outs, sems):
        rc, _ = copies(ins, outs, sems)
        for t in range(n):
            rc(t, ins[t].at[pl.ds(0, 4)], outs[t]).wait()
        for t in range(n, n + m):
            rc(t, ins[t], outs[t]).wait()

    return _Plan(list(grads) + list(whole),
                 [SDS((4,) + tuple(g.shape[1:]), g.dtype) for g in grads] + [SDS(a.shape, a.dtype) for a in whole],
                 [pltpu.SemaphoreType.DMA((n + m,)), pltpu.SemaphoreType.DMA((n + m,))], start, finish)


def _chips_plan(parts, whole=()):
    n, m = len(parts), len(whole)

    def src_of(ins, t, pc):
        return ins[t].at[pc] if t < n else ins[t]

    def local_copies(ins, outs, sems, chip):
        return [pltpu.make_async_copy(src_of(ins, t, chip), outs[t].at[chip], sems[2].at[t]) for t in range(n + m)]

    def start(ins, outs, sems):
        send_sems, recv_sems, _ = sems
        x, y, c = _place()
        chip = 2 * x + y
        for cp in local_copies(ins, outs, sems, chip):
            cp.start()
        for px, py, pc in _other_chips(x, y):
            for t in range(n + m):
                pltpu.make_async_remote_copy(src_ref=src_of(ins, t, pc), dst_ref=outs[t].at[chip],
                                             send_sem=send_sems.at[t], recv_sem=recv_sems.at[t],
                                             device_id=(px, py, c), device_id_type=MESH).start()

    def finish(ins, outs, sems):
        send_sems, recv_sems, _ = sems
        x, y, c = _place()
        for t in range(n + m):
            three = outs[t].at[pl.ds(0, 3)]
            pltpu.make_async_remote_copy(src_ref=three, dst_ref=three, send_sem=send_sems.at[t],
                                         recv_sem=recv_sems.at[t], device_id=(x, y, c), device_id_type=MESH).wait()
        for cp in local_copies(ins, outs, sems, 2 * x + y):
            cp.wait()

    return _Plan(list(parts) + list(whole),
                 [SDS(p.shape, p.dtype) for p in parts] + [SDS((4,) + tuple(a.shape), a.dtype) for a in whole],
                 [pltpu.SemaphoreType.DMA((n + m,)), pltpu.SemaphoreType.DMA((n + m,)),
                  pltpu.SemaphoreType.DMA((n + m,))], start, finish)


def _exchange_plan(arr):
    def peers(x, y, c):
        flip = lambda v, f: 1 - v if f else v
        return [(flip(x, fx), flip(y, fy), flip(c, fc))
                for fx in (0, 1) for fy in (0, 1) for fc in (0, 1) if fx or fy or fc]

    def start(ins, outs, sems):
        x, y, c = _place()
        me = 4 * x + 2 * y + c
        pltpu.make_async_copy(ins[0], outs[0].at[me], sems[2].at[0]).start()
        for to in peers(x, y, c):
            pltpu.make_async_remote_copy(src_ref=ins[0], dst_ref=outs[0].at[me], send_sem=sems[0].at[0],
                                         recv_sem=sems[1].at[0], device_id=to, device_id_type=MESH).start()

    def finish(ins, outs, sems):
        x, y, c = _place()
        seven = outs[0].at[pl.ds(0, 7)]
        pltpu.make_async_remote_copy(src_ref=seven, dst_ref=seven, send_sem=sems[0].at[0], recv_sem=sems[1].at[0],
                                     device_id=(x, y, c), device_id_type=MESH).wait()
        pltpu.make_async_copy(ins[0], outs[0].at[4 * x + 2 * y + c], sems[2].at[0]).wait()

    return _Plan([arr], [SDS((N_SLOT,) + tuple(arr.shape), arr.dtype)],
                 [pltpu.SemaphoreType.DMA((1,)), pltpu.SemaphoreType.DMA((1,)), pltpu.SemaphoreType.DMA((1,))],
                 start, finish)


def _join(*plans):
    def cut(seq, sizes):
        out, at = [], 0
        for k in sizes:
            out.append(seq[at:at + k])
            at += k
        return out

    n_arr = [len(p.arrays) for p in plans]
    n_sem = [len(p.sems) for p in plans]

    def start(ins, outs, sems):
        for p, i, o, s in zip(plans, cut(ins, n_arr), cut(outs, n_arr), cut(sems, n_sem)):
            p.start(i, o, s)

    def finish(ins, outs, sems):
        for p, i, o, s in zip(plans, cut(ins, n_arr), cut(outs, n_arr), cut(sems, n_sem)):
            p.finish(i, o, s)

    def middle(ins, outs, sems):
        for p, i, o, s in zip(plans, cut(ins, n_arr), cut(outs, n_arr), cut(sems, n_sem)):
            if p.middle is not None:
                p.middle(i, o, s)

    return _Plan([a for p in plans for a in p.arrays], [o for p in plans for o in p.out_shape],
                 [s for p in plans for s in p.sems], start, finish,
                 middle if any(p.middle is not None for p in plans) else None,
                 max(p.middle_at for p in plans))


def _run_plan(name, plan):
    k = len(plan.arrays)

    def body(*refs):
        ins, outs, sems = refs[:k], refs[k:2 * k], refs[2 * k:]
        plan.start(ins, outs, sems)
        if plan.middle is not None:
            plan.middle(ins, outs, sems)
        plan.finish(ins, outs, sems)

    return pl.pallas_call(
        body, name=name, in_specs=[ANY] * k, out_specs=[ANY] * k, out_shape=plan.out_shape,
        scratch_shapes=plan.sems, compiler_params=pltpu.CompilerParams(has_side_effects=True),
    )(*plan.arrays)


def _call(body, *, name, grid, in_specs, out_specs, out_shape, scratch_shapes, params, args, comm=None,
          aliases=None, prefetch=()):
    aliases = aliases or {}
    n_pre = len(prefetch)

    def launch(fn, ins_specs, outs_specs, outs_shape, scratch, operands):
        spec = pltpu.PrefetchScalarGridSpec(num_scalar_prefetch=n_pre, grid=grid, in_specs=ins_specs,
                                            out_specs=outs_specs, scratch_shapes=scratch)
        return pl.pallas_call(fn, name=name, grid_spec=spec, out_shape=outs_shape, compiler_params=params,
                              input_output_aliases=aliases)(*prefetch, *[_small_in_hbm(a) for a in operands])

    if comm is None:
        return list(launch(body, in_specs, out_specs, out_shape, scratch_shapes, args)), []
    n_in, n_out, n_scr, k = len(in_specs), len(out_specs), len(scratch_shapes), len(comm.arrays)

    def wrapped(*refs):
        pre, refs = refs[:n_pre], refs[n_pre:]
        ins = refs[:n_in]
        c_in = refs[n_in:n_in + k]
        outs = refs[n_in + k:n_in + k + n_out]
        c_out = refs[n_in + k + n_out:n_in + 2 * k + n_out]
        scr = refs[n_in + 2 * k + n_out:n_in + 2 * k + n_out + n_scr]
        sems = refs[n_in + 2 * k + n_out + n_scr:]
        step, steps = pl.program_id(0), grid[0]
        for d in range(1, len(grid)):
            step, steps = step * grid[d] + pl.program_id(d), steps * grid[d]

        @pl.when(step == 0)
        def _():
            comm.start(c_in, c_out, sems)

        if comm.middle is not None:
            @pl.when(step == (comm.middle_at * steps) // 8)
            def _():
                comm.middle(c_in, c_out, sems)

        body(*pre, *ins, *outs, *scr)

        @pl.when(step == steps - 1)
        def _():
            comm.finish(c_in, c_out, sems)

    res = launch(wrapped, list(in_specs) + [ANY] * k, list(out_specs) + [ANY] * k,
                 list(out_shape) + list(comm.out_shape), list(scratch_shapes) + list(comm.sems),
                 tuple(args) + tuple(comm.arrays))
    return list(res[:n_out]), list(res[n_out:])


def _wgrad_paired(name, a_t, b, core, by_rows=False, comm=None):
    s = b.shape[0]
    m = a_t.shape[0] // N_SLOT if by_rows else a_t.shape[0]
    cols = b.shape[1] if by_rows else b.shape[1] // N_SLOT
    half = N_SLOT // 2

    def owner(k, c):
        return 2 * (k % half) + jnp.where(k < half, 1 - c, c)

    def body(c_ref, a_ref, b_ref, sum_ref, out_s, recv_s, send_sems, recv_sems):
        del c_ref
        k = pl.program_id(0)
        x, y, c = _place()

        def to_sibling(j):
            return pltpu.make_async_remote_copy(src_ref=out_s.at[j % 2], dst_ref=recv_s.at[j],
                                                send_sem=send_sems.at[j], recv_sem=recv_sems.at[j],
                                                device_id=(x, y, 1 - c), device_id_type=MESH)

        @pl.when((k >= 2) & (k < half + 2))
        def _():
            to_sibling(k - 2).wait_send()

        @pl.when(k < half)
        def _():
            out_s[k % 2] = _dot(a_ref[...], b_ref[...]).astype(BF16)
            to_sibling(k).start()

        @pl.when(k >= half)
        def _():
            to_sibling(k - half).wait_recv()
            sum_ref[...] = (_dot(a_ref[...], b_ref[...]) + recv_s[k - half].astype(F32)).astype(BF16)

    if by_rows:
        in_specs = [pl.BlockSpec((m, s), lambda k, c_ref: (owner(k, c_ref[0]), 0)),
                    pl.BlockSpec((s, cols), lambda k, c_ref: (0, 0))]
    else:
        in_specs = [pl.BlockSpec((m, s), lambda k, c_ref: (0, 0)),
                    pl.BlockSpec((s, cols), lambda k, c_ref: (0, owner(k, c_ref[0])))]
    return _call(body, name=name, grid=(N_SLOT,), in_specs=in_specs,
                 out_specs=[pl.BlockSpec((None, m, cols), lambda k, c_ref: (jnp.maximum(k - half, 0), 0, 0))],
                 out_shape=[SDS((half, m, cols), BF16)],
                 scratch_shapes=[pltpu.VMEM((2, m, cols), BF16), pltpu.VMEM((half, m, cols), BF16),
                                 pltpu.SemaphoreType.DMA((half,)), pltpu.SemaphoreType.DMA((half,))],
                 params=_cparams(("arbitrary",), 56), args=(a_t, b), comm=comm, prefetch=(core,))


def _row_tile(rows):
    for t in (512, 256, 128, 64, 32, 16, 8):
        if rows % t == 0:
            return t
    return rows


def _pair_sum(name, g8, recv4, core):
    _, rows, cols = recv4.shape
    tr = _row_tile(rows)
    g42 = g8.reshape(4, 2, rows, cols)

    def body(c_ref, g_ref, r_ref, o_ref):
        del c_ref
        o_ref[...] = (g_ref[...].astype(F32) + r_ref[...].astype(F32)).astype(o_ref.dtype)

    return pl.pallas_call(
        body, name=name,
        grid_spec=pltpu.PrefetchScalarGridSpec(
            num_scalar_prefetch=1, grid=(4, rows // tr),
            in_specs=[pl.BlockSpec((None, None, tr, cols), lambda j, i, c_ref: (j, c_ref[0], i, 0)),
                      pl.BlockSpec((None, tr, cols), lambda j, i, c_ref: (j, i, 0))],
            out_specs=pl.BlockSpec((None, tr, cols), lambda j, i, c_ref: (j, i, 0))),
        out_shape=SDS(recv4.shape, g8.dtype),
        compiler_params=_cparams(("arbitrary", "arbitrary"), 32),
    )(core, g42, recv4)


def _add2(name, a, b):
    rows, cols = a.shape
    tr = _row_tile(rows)

    def body(a_ref, b_ref, o_ref):
        o_ref[...] = a_ref[...] + b_ref[...]

    blk = pl.BlockSpec((tr, cols), lambda i: (i, 0))
    return pl.pallas_call(body, name=name, grid=(rows // tr,), in_specs=[blk, blk], out_specs=blk,
                          out_shape=SDS(a.shape, a.dtype),
                          compiler_params=_cparams(("arbitrary",), 32))(a, b)


def _sum_terms(name, terms):
    k, rows, cols = terms.shape
    tr = _row_tile(rows)

    def body(r_ref, o_ref):
        acc = r_ref[0]
        for q in range(1, k):
            acc = acc + r_ref[q]
        o_ref[...] = acc

    return pl.pallas_call(body, name=name, grid=(rows // tr,),
                          in_specs=[pl.BlockSpec((k, tr, cols), lambda i: (0, i, 0))],
                          out_specs=pl.BlockSpec((tr, cols), lambda i: (i, 0)),
                          out_shape=SDS((rows, cols), terms.dtype),
                          compiler_params=_cparams(("arbitrary",), 32))(terms)


def _adam_update(g, w, m, v):
    c1 = 1.0 / (1.0 - ADAM_B1 ** ADAM_STEP)
    c2 = 1.0 / (1.0 - ADAM_B2 ** ADAM_STEP)
    mn = ADAM_B1 * m + (1.0 - ADAM_B1) * g
    vn = ADAM_B2 * v + (1.0 - ADAM_B2) * (g * g)
    delta = (-ADAM_LR) * ((mn * c1) / (jnp.sqrt(vn * c2) + ADAM_EPS) + ADAM_WD * w)
    return delta, mn, vn


def _adamw_many(name, gs, ws, ms, vs):
    n = len(gs)

    def body(*refs):
        for p in range(n):
            g, w, m, v = (refs[q * n + p][...] for q in range(4))
            d, mn, vn = _adam_update(g, w, m, v)
            refs[4 * n + p][...] = d
            refs[5 * n + p][...] = mn
            refs[6 * n + p][...] = vn

    full = [pl.BlockSpec(w.shape, lambda i: (0, 0)) for w in ws]
    shapes = [SDS(w.shape, F32) for w in ws]
    res = pl.pallas_call(body, name=name, grid=(1,), in_specs=full * 4, out_specs=full * 3, out_shape=shapes * 3,
                         compiler_params=_cparams(("arbitrary",), 32),
                         )(*[_small_in_hbm(a) for a in (*gs, *ws, *ms, *vs)])
    return [(res[p], res[n + p], res[2 * n + p]) for p in range(n)]


def _adamw(name, terms, w, m, v):
    k, rows, cols = terms.shape
    tr = _row_tile(rows)

    def body(t_ref, w_ref, m_ref, v_ref, g_ref, d_ref, mo_ref, vo_ref):
        g = t_ref[0].astype(F32)
        for q in range(1, k):
            g = g + t_ref[q].astype(F32)
        g_ref[...] = g
        d_ref[...], mo_ref[...], vo_ref[...] = _adam_update(g, w_ref[...], m_ref[...], v_ref[...])

    blk = pl.BlockSpec((tr, cols), lambda i: (i, 0))
    return pl.pallas_call(body, name=name, grid=(rows // tr,),
                          in_specs=[pl.BlockSpec((k, tr, cols), lambda i: (0, i, 0)), blk, blk, blk],
                          out_specs=[blk] * 4, out_shape=[SDS((rows, cols), F32)] * 4,
                          compiler_params=_cparams(("arbitrary",), 40),
                          )(*[pltpu.with_memory_space_constraint(a, pltpu.HBM) for a in (terms, w, m, v)])


def kernel(x, norm_mix_g, w_in, conv_w, conv_b, w_rgate, b_rgate, w_igate, b_igate, lru_lambda, w_out_a, sgu_ln_g, sgu_ln_b, sgu_w_s, sgu_b_s, w_out_b, w_out, norm_mlp_g, w_up, w_down, norm_final_g, loss_target, m_norm_mix_g, m_w_in, m_conv_w, m_conv_b, m_w_rgate, m_b_rgate, m_w_igate, m_b_igate, m_lru_lambda, m_w_out_a, m_sgu_ln_g, m_sgu_ln_b, m_sgu_w_s, m_sgu_b_s, m_w_out_b, m_w_out, m_norm_mlp_g, m_w_up, m_w_down, m_norm_final_g, v_norm_mix_g, v_w_in, v_conv_w, v_conv_b, v_w_rgate, v_b_rgate, v_w_igate, v_b_igate, v_lru_lambda, v_w_out_a, v_sgu_ln_g, v_sgu_ln_b, v_sgu_w_s, v_sgu_b_s, v_w_out_b, v_w_out, v_norm_mlp_g, v_w_up, v_w_down, v_norm_final_g):
    cx, cy, cc = _place()
    me = 4 * cx + 2 * cy + cc
    core = jnp.reshape(cc, (1,)).astype(jnp.int32)
    xs = x[0]
    tgt = loss_target[0]

    gate_shard = jnp.stack([w_rgate[0], w_igate[0]]).astype(BF16).reshape(2 * HEADS * 32, HEAD_DIM)
    vec_shard = jnp.concatenate([conv_w[0], b_rgate[0], b_igate[0]], axis=1)
    vec_shard = jnp.pad(vec_shard, ((0, 4), (0, 256 - vec_shard.shape[1])))
    shards = [w_in[0].astype(BF16), w_out_a[0].astype(BF16), w_out_b[0].astype(BF16), w_out[0].astype(BF16),
              w_up[0].astype(BF16), w_down[0].astype(BF16), gate_shard, vec_shard]
    (z, n1_t, w_in_g), (gate_g, vec_g) = _in_proj(xs, norm_mix_g, shards[0], _slot_order(cx, cy, cc),
                                                comm=_gather_plan(shards[6:8]))
    gates = gate_g.reshape(N_SLOT, 2, HEADS, 32, HEAD_DIM).transpose(1, 2, 0, 3, 4).reshape(2, HEADS, HEAD_DIM, HEAD_DIM)
    w_r_f, w_i_f = gates[0], gates[1]
    conv_w_f = vec_g[:, 0:4, 0:128].transpose(1, 0, 2).reshape(CONV_K, D)
    b_r_f = vec_g[:, 0:4, 128:160].transpose(1, 0, 2).reshape(1, D)
    b_i_f = vec_g[:, 0:4, 160:192].transpose(1, 0, 2).reshape(1, D)
    b_s_t = jnp.transpose(sgu_b_s[0])

    (ya, hs, xc, r_gate, i_gate, yb), (w_oa_g, w_ob_g, w_out_g, w_up_g) = _branches_fwd(
        z, (conv_w_f, conv_b, w_r_f, b_r_f, w_i_f, b_i_f, lru_lambda), (sgu_ln_g, sgu_ln_b, sgu_w_s[0], b_s_t),
        comm=_gather_plan(shards[1:5]))
    w_oa_f = w_oa_g.reshape(D, D)
    w_ob_f = w_ob_g.reshape(D, D)
    w_out_f = w_out_g.reshape(D, D)
    (pa, pb, merged, h1), (w_down_g,) = _merge_out(ya, yb, z, xs, w_oa_f, w_ob_f, w_out_f,
                                                   comm=_gather_plan(shards[5:6], middle_at=4))
    w_down_f = w_down_g.reshape(N_SLOT * FF_COLS, D)
    r_act, act_t, n2_t, dh2, dh2b, loss_acc, d_gfin = _mlp_fwd(h1, norm_mlp_g, w_up_g, w_down_f,
                                                               norm_final_g.reshape(1, D), tgt)

    def pair(names, grads, recv):
        return [_pair_sum("pair_sum_" + nm, g, r, core) for nm, g, r in zip(names, grads, recv)]

    (p_down,), _ = _wgrad_paired("wgrad_down", act_t, dh2b, core, by_rows=True)
    (df, dh1, d_gmlp), (got_down,) = _mlp_bwd(dh2, dh2b, r_act, w_down_f, w_up_g, h1, norm_mlp_g,
                                              comm=_chips_plan([p_down]))
    (p_up,), _ = _wgrad_paired("wgrad_up", n2_t, df, core)
    g_out = _wgrad("wgrad_out", merged, dh1).reshape(N_SLOT, D // N_SLOT, D)
    (dz, dpa, dpb, dya, dyb), (r_out,) = _merge_bwd(
        dh1, z, pa, pb, w_out_f, w_oa_f, w_ob_f, comm=_sibling_plan([g_out]))
    (p_out,) = pair(["out"], [g_out], [r_out])
    g_oa = _wgrad("wgrad_out_a", ya, dpa).reshape(N_SLOT, D // N_SLOT, D)
    g_ob = _wgrad("wgrad_out_b", yb, dpb).reshape(N_SLOT, D // N_SLOT, D)
    (dz, d_ws, d_bs, d_ln), (got_out, r_oa, r_ob) = _branch_b_bwd(
        dz, dyb, z, sgu_ln_g, sgu_ln_b, sgu_w_s[0], b_s_t,
        comm=_join(_chips_plan([p_out]), _sibling_plan([g_oa, g_ob])))
    p_oa, p_ob = pair(["out_a", "out_b"], [g_oa, g_ob], [r_oa, r_ob])
    (dz, d_vec, d_wr, d_wi), (got_up, got_oa, got_ob) = _branch_a_bwd(
        dz, dya, z, hs, xc, r_gate, i_gate, conv_w_f, w_r_f, w_i_f, lru_lambda,
        comm=_chips_plan([p_up, p_oa, p_ob]))
    g_gate = jnp.stack([d_wr, d_wi]).reshape(2, HEADS, N_SLOT, 32, HEAD_DIM).transpose(2, 0, 1, 3, 4)
    g_gate = g_gate.reshape(N_SLOT, 2 * HEADS * 32, HEAD_DIM).astype(BF16)

    d_bs_row = jnp.pad(d_bs[:, :, 0].reshape(1, GROUPS * CHUNK), ((0, 0), (0, D - GROUPS * CHUNK)))
    vecs = jnp.concatenate([d_vec, jnp.concatenate([d_ln[0:2], d_gmlp, d_gfin, d_bs_row, jnp.zeros((3, D), F32)])])
    d_ws2 = d_ws.reshape(GROUPS * CHUNK, CHUNK)
    (p_in,), (r_gate, r_vecs, r_ws) = _wgrad_paired("wgrad_in", n1_t, dz, core,
                                                    comm=_sibling_plan([g_gate], [vecs, d_ws2]))
    (p_gate,) = pair(["gate"], [g_gate], [r_gate])
    vecs_chip = _add2("pair_sum_vecs", vecs, r_vecs)
    ws_chip = _add2("pair_sum_ws", d_ws2, r_ws)
    (dx, d_gmix), (got_in, got_gate, got_vecs, got_ws) = _in_bwd(
        dz, w_in_g, xs, dh1, norm_mix_g, comm=_chips_plan([p_in, p_gate], [vecs_chip, ws_chip]))
    vecs_sum = _sum_terms("sum_vecs", got_vecs)
    last = jnp.concatenate([d_gmix, jnp.pad(loss_acc[0:1], ((0, 0), (0, D - 128))), jnp.zeros((6, D), F32)])
    (last_all,) = _run_plan("exchange_last", _exchange_plan(last))
    last_sum = _sum_terms("sum_last", last_all)
    loss = last_sum[1, 0]
    got = [got_in, got_oa, got_ob, got_out, got_up, got_down, got_gate]

    def step(nm, terms, w, m, v, rows, cols):
        g, d, mn, vn = _adamw("adamw_" + nm, terms.reshape(4, rows, cols), w.reshape(rows, cols),
                              m.reshape(rows, cols), v.reshape(rows, cols))
        return [a.reshape(w.shape) for a in (g, d, mn, vn)]

    o_in = step("in", got[0], w_in, m_w_in, v_w_in, D, W_IN_COLS)
    o_oa = step("out_a", got[1], w_out_a, m_w_out_a, v_w_out_a, D // N_SLOT, D)
    o_ob = step("out_b", got[2], w_out_b, m_w_out_b, v_w_out_b, D // N_SLOT, D)
    o_out = step("out", got[3], w_out, m_w_out, v_w_out, D // N_SLOT, D)
    o_up = step("up", got[4], w_up, m_w_up, v_w_up, D, FF_COLS)
    o_down = step("down", got[5], w_down, m_w_down, v_w_down, FF_COLS, D)
    gate_w = jnp.stack([w_rgate[0], w_igate[0]]).reshape(2 * HEADS * 32, HEAD_DIM)
    gate_m = jnp.stack([m_w_rgate[0], m_w_igate[0]]).reshape(2 * HEADS * 32, HEAD_DIM)
    gate_v = jnp.stack([v_w_rgate[0], v_w_igate[0]]).reshape(2 * HEADS * 32, HEAD_DIM)
    o_gate = _adamw("adamw_gate", got[6], gate_w, gate_m, gate_v)
    o_gate = [a.reshape(2, 1, HEADS, 32, HEAD_DIM) for a in o_gate]
    o_wr = [a[0] for a in o_gate]
    o_wi = [a[1] for a in o_gate]

    def own(full, width):
        return lax.dynamic_slice_in_dim(full, me * width, width, axis=1)

    small_g = {
        "norm_mix_g": last_sum[0:1], "conv_w": own(vecs_sum[0:4], 128), "conv_b": vecs_sum[4:5],
        "b_rgate": own(vecs_sum[5:6].reshape(HEADS, HEAD_DIM), 32),
        "b_igate": own(vecs_sum[6:7].reshape(HEADS, HEAD_DIM), 32),
        "lru_lambda": vecs_sum[7:8], "sgu_ln_g": vecs_sum[8:9], "sgu_ln_b": vecs_sum[9:10],
        "norm_mlp_g": vecs_sum[10:11], "norm_final_g": vecs_sum[11:12],
        "sgu_b_s": vecs_sum[12, 0:GROUPS * CHUNK].reshape(GROUPS, CHUNK),
    }
    small_w = {"norm_mix_g": (norm_mix_g, m_norm_mix_g, v_norm_mix_g), "conv_w": (conv_w, m_conv_w, v_conv_w),
               "conv_b": (conv_b, m_conv_b, v_conv_b), "b_rgate": (b_rgate, m_b_rgate, v_b_rgate),
               "b_igate": (b_igate, m_b_igate, v_b_igate), "lru_lambda": (lru_lambda, m_lru_lambda, v_lru_lambda),
               "sgu_ln_g": (sgu_ln_g, m_sgu_ln_g, v_sgu_ln_g), "sgu_ln_b": (sgu_ln_b, m_sgu_ln_b, v_sgu_ln_b),
               "norm_mlp_g": (norm_mlp_g, m_norm_mlp_g, v_norm_mlp_g),
               "norm_final_g": (norm_final_g, m_norm_final_g, v_norm_final_g),
               "sgu_b_s": (sgu_b_s, m_sgu_b_s, v_sgu_b_s), "sgu_w_s": (sgu_w_s, m_sgu_w_s, v_sgu_w_s)}
    order = list(small_g)
    as2d = lambda k, a: a.reshape(small_g[k].shape)
    upd = _adamw_many("adamw_small", [small_g[k] for k in order], *[[as2d(k, small_w[k][q]) for k in order]
                                                                     for q in range(3)])
    o_small = {k: [a.reshape(small_w[k][0].shape) for a in (small_g[k],) + u] for k, u in zip(order, upd)}
    ws3 = [a[0].reshape(GROUPS * CHUNK, CHUNK) for a in small_w.pop("sgu_w_s")]
    o_small["sgu_w_s"] = [a.reshape(sgu_w_s.shape) for a in _adamw("adamw_ws", got_ws, *ws3)]

    per_weight = {"norm_mix_g": o_small["norm_mix_g"], "w_in": o_in, "conv_w": o_small["conv_w"],
                  "conv_b": o_small["conv_b"], "w_rgate": o_wr, "b_rgate": o_small["b_rgate"], "w_igate": o_wi,
                  "b_igate": o_small["b_igate"], "lru_lambda": o_small["lru_lambda"], "w_out_a": o_oa,
                  "sgu_ln_g": o_small["sgu_ln_g"], "sgu_ln_b": o_small["sgu_ln_b"], "sgu_w_s": o_small["sgu_w_s"],
                  "sgu_b_s": o_small["sgu_b_s"], "w_out_b": o_ob, "w_out": o_out, "norm_mlp_g": o_small["norm_mlp_g"],
                  "w_up": o_up, "w_down": o_down, "norm_final_g": o_small["norm_final_g"]}
    names_w = list(per_weight)
    return (loss, dx[None], *[per_weight[k][0] for k in names_w], *[per_weight[k][1] for k in names_w],
            *[per_weight[k][2] for k in names_w], *[per_weight[k][3] for k in names_w])
```

```python
import jax
import jax.numpy as jnp
from jax import lax
from jax.experimental import pallas as pl
from jax.experimental.pallas import tpu as pltpu

F32 = jnp.float32
BF16 = jnp.bfloat16
SDS = jax.ShapeDtypeStruct
MESH = pl.DeviceIdType.MESH
ANY = pl.BlockSpec(memory_space=pl.ANY)

D = 1024
N_SLOT = 8
W_IN_COLS = 768
FF_COLS = 512
HEADS, HEAD_DIM = 4, 256
GROUPS, GROUP_DIM = 4, 256
CHUNK = 128
CONV_K = 4
NORM_EPS = 1e-6
LN_EPS = 1e-5
LRU_C = 8.0
ADAM_LR, ADAM_B1, ADAM_B2, ADAM_EPS, ADAM_WD, ADAM_STEP = 0.001, 0.9, 0.999, 1e-08, 0.01, 10

TM_ROWS = 1024
TM_MERGE = 512
T_BRANCH_A = 512
T_BRANCH_B = 256
MiB = 1024 * 1024
SMALL_OPERAND = 16 * 1024

_GELU_C = 0.7978845608028654
_GELU_A = 0.044715


def _small_in_hbm(a):
    return pltpu.with_memory_space_constraint(a, pltpu.HBM) if a.size <= SMALL_OPERAND else a


def _cparams(sem, vmem_mib):
    return pltpu.CompilerParams(dimension_semantics=sem, vmem_limit_bytes=vmem_mib * MiB)


def _gelu(x):
    t = jnp.tanh(_GELU_C * (x + _GELU_A * x * x * x))
    return 0.5 * x * (1.0 + t)


def _gelu_and_grad(x):
    x2 = x * x
    t = jnp.tanh(_GELU_C * x * (1.0 + _GELU_A * x2))
    g = 0.5 * x * (1.0 + t)
    dg = 0.5 * (1.0 + t) + 0.5 * x * (1.0 - t * t) * _GELU_C * (1.0 + 3.0 * _GELU_A * x2)
    return g, dg


def _softplus(x):
    return jnp.maximum(x, 0.0) + jnp.log1p(jnp.exp(-jnp.abs(x)))


def _dot(a, b):
    return jnp.dot(a, b, preferred_element_type=F32)


def _dot_nt(a, b):
    return lax.dot_general(a, b, (((1,), (1,)), ((), ())), preferred_element_type=F32)


def _dot_tn(a, b):
    return lax.dot_general(a, b, (((0,), (0,)), ((), ())), preferred_element_type=F32)


def _rows_shifted(prev8, cur, k):
    ext = jnp.concatenate([prev8, cur], axis=0)
    return pltpu.roll(ext, k, 0)[8:]


def _rows_advanced(cur, next8, k):
    t = cur.shape[0]
    ext = jnp.concatenate([cur, next8], axis=0)
    return pltpu.roll(ext, t + 8 - k, 0)[:t]


def _first_second(x, y, c):
    ny, nx, far = _other_chips(x, y)
    pick = lambda a, b: a * (1 - c) + b * c
    first = tuple(pick(a, b) for a, b in zip(ny, nx))
    second = tuple(pick(b, a) for a, b in zip(ny, nx))
    return first, second, far


def _slot_order(x, y, c):
    chip = 2 * x + y
    first, second, far = _first_second(x, y, c)
    order = [2 * chip + c, 2 * chip + 1 - c, 2 * first[2] + c, 2 * second[2] + 1 - c, 2 * second[2] + c,
             2 * first[2] + 1 - c, 2 * far[2] + c, 2 * far[2] + 1 - c]
    return jnp.stack(order).astype(jnp.int32)


def _in_proj(x, g_mix, w_in_own, order, comm=None):
    s = x.shape[0]
    tm = min(TM_ROWS, s)
    ni = s // tm

    def body(order_ref, x_ref, g_ref, own_ref, z_ref, nt_ref, wg_ref, n_s, w_s, send_sems, recv_sems, local_sems):
        j, i = pl.program_id(0), pl.program_id(1)
        px, py, c = _place()
        chip = 2 * px + py
        me = 2 * chip + c
        sib = (px, py, 1 - c)
        chips = _other_chips(px, py)

        def rc(k, src, blk, to):
            return pltpu.make_async_remote_copy(src_ref=src, dst_ref=w_s.at[blk], send_sem=send_sems.at[k],
                                                recv_sem=recv_sems.at[k], device_id=to, device_id_type=MESH)

        del chips
        first, second, far = _first_second(px, py, c)
        blocks = [2 * first[2] + c, 2 * second[2] + c, 2 * far[2] + c]
        own_in = pltpu.make_async_copy(own_ref, w_s.at[me], local_sems.at[0])
        to_first = rc(1, own_ref, me, (first[0], first[1], c))
        to_second = rc(2, own_ref, me, (second[0], second[1], c))
        relay = rc(3, w_s.at[blocks[0]], blocks[0], (second[0], second[1], c))
        sends = [rc(0, own_ref, me, sib), to_first, to_second, relay]
        passed = [rc(4 + q, w_s.at[blk], blk, sib) for q, blk in enumerate(blocks)]
        keep = pltpu.make_async_copy(w_s, wg_ref, local_sems.at[1])

        @pl.when((i == 0) & (j == 0))
        def _():
            own_in.start()
            sends[0].start()
            to_first.start()
            own_in.wait()

        @pl.when((i == 0) & (j == 1))
        def _():
            rc(0, own_ref, 2 * chip + 1 - c, sib).wait_recv()

        for q, blk in enumerate(blocks):
            @pl.when((i == 0) & (j == 2 + 2 * q))
            def _():
                rc(1 + q, own_ref, blk, sib).wait_recv()
                passed[q].start()
                if q == 0:
                    to_second.start()
                    relay.start()

            @pl.when((i == 0) & (j == 3 + 2 * q))
            def _():
                rc(4 + q, own_ref, order_ref[j], sib).wait_recv()

        rows = pl.ds(pl.multiple_of(i * tm, tm), tm)

        @pl.when(j == 0)
        def _():
            xv = x_ref[...]
            rstd = lax.rsqrt(jnp.mean(xv * xv, axis=-1, keepdims=True) + NORM_EPS)
            nb = (xv * rstd * g_ref[...]).astype(BF16)
            n_s[rows, :] = nb
            nt_ref[...] = nb.T

        z_ref[...] = _dot(n_s[rows, :], w_s[order_ref[j]]).astype(BF16)

        @pl.when((i == 0) & (j == N_SLOT - 1))
        def _():
            keep.start()

        @pl.when((i == ni - 1) & (j == N_SLOT - 1))
        def _():
            for cp in sends + passed:
                cp.wait_send()
            keep.wait()

    first_pass = lambda j, i, o: (jnp.where(j == 0, i, ni - 1), 0)
    (z, n1, w_in_g), extra = _call(
        body, name="in_proj", grid=(N_SLOT, ni), prefetch=(order,),
        in_specs=[pl.BlockSpec((tm, D), first_pass),
                  pl.BlockSpec((1, D), lambda j, i, o: (0, 0)), ANY],
        out_specs=[pl.BlockSpec((tm, W_IN_COLS), lambda j, i, o: (i, o[j])),
                   pl.BlockSpec((D, tm), lambda j, i, o: (0, jnp.where(j == 0, i, ni - 1))), ANY],
        out_shape=[SDS((s, N_SLOT * W_IN_COLS), BF16), SDS((D, s), BF16), SDS((N_SLOT, D, W_IN_COLS), BF16)],
        scratch_shapes=[pltpu.VMEM((s, D), BF16), pltpu.VMEM((N_SLOT, D, W_IN_COLS), BF16),
                        pltpu.SemaphoreType.DMA((7,)), pltpu.SemaphoreType.DMA((7,)), pltpu.SemaphoreType.DMA((2,))],
        params=_cparams(("arbitrary", "arbitrary"), 56), args=(x, g_mix, w_in_own), comm=comm)
    return (z, n1, w_in_g), extra


def _decay(r, sp_lam):
    log_a = (-LRU_C) * r * sp_lam
    a = jnp.exp(log_a)
    return a, jnp.sqrt(-jnp.tanh(log_a) * (a * a + 1.0))


def _lru_gates(xc, xcb, wr_ref, br, wi_ref, bi, sp_lam, a_s, b_s, r_ref, i_ref):
    for h in range(HEADS):
        sl = slice(h * HEAD_DIM, (h + 1) * HEAD_DIM)
        r = jax.nn.sigmoid(_dot(xcb[:, sl], wr_ref[h]) + br[:, sl])
        ig = jax.nn.sigmoid(_dot(xcb[:, sl], wi_ref[h]) + bi[:, sl])
        a, mult = _decay(r, sp_lam[:, sl])
        a_s[:, sl] = a
        b_s[:, sl] = xc[:, sl] * ig * mult
        r_ref[:, sl] = r.astype(BF16)
        i_ref[:, sl] = ig.astype(BF16)


def _conv_fwd(xa, prev8, cw, cb):
    xc = cb + cw[0:1, :] * xa
    for k in range(1, CONV_K):
        xc = xc + cw[k:k + 1, :] * _rows_shifted(prev8, xa, k)
    return xc


def _branch_a_fwd(z, conv_w, conv_b, w_r, b_r, w_i, b_i, lam, comm=None):
    s = z.shape[0]
    ta = min(T_BRANCH_A, s)
    per16 = ta // 16

    def body(xa_ref, xp_ref, ga_ref, cw_ref, cb_ref, wr_ref, br_ref, wi_ref, bi_ref, lam_ref,
             ya_ref, hs_ref, xc_ref, r_ref, i_ref, a_s, b_s, h_s, carry_s):
        i = pl.program_id(0)

        @pl.when(i == 0)
        def _():
            carry_s[...] = jnp.zeros_like(carry_s)

        xa = xa_ref[...].astype(F32)
        prev8 = jnp.where(i > 0, xp_ref[...].astype(F32)[8:16], 0.0)
        xc = _conv_fwd(xa, prev8, cw_ref[...], cb_ref[...])
        xcb = xc.astype(BF16)
        xc_ref[...] = xcb
        sp_lam = _softplus(-lam_ref[...])
        _lru_gates(xc, xcb, wr_ref, br_ref[...], wi_ref, bi_ref[...], sp_lam, a_s, b_s, r_ref, i_ref)

        row = lax.broadcasted_iota(jnp.int32, (8, D), 0)

        def group(g, carry):
            off = pl.multiple_of(g * 8, 8)
            a8 = a_s[pl.ds(off, 8), :]
            b8 = b_s[pl.ds(off, 8), :]
            for d in (1, 2, 4):
                a_sh = jnp.where(row >= d, pltpu.roll(a8, d, 0), 1.0)
                b_sh = jnp.where(row >= d, pltpu.roll(b8, d, 0), 0.0)
                b8 = a8 * b_sh + b8
                a8 = a8 * a_sh
            h8 = b8 + a8 * carry
            h_s[pl.ds(off, 8), :] = h8
            return jnp.broadcast_to(h8[7:8, :], (8, D))

        carry_s[...] = lax.fori_loop(0, ta // 8, group, carry_s[...])
        hs = h_s[...]
        hs_ref[...] = hs.astype(BF16)
        ya_ref[...] = (hs * _gelu(ga_ref[...].astype(F32))).astype(BF16)

    vec = pl.BlockSpec((1, D), lambda i: (0, 0))
    gate = pl.BlockSpec((HEADS, HEAD_DIM, HEAD_DIM), lambda i: (0, 0, 0))
    return _call(
        body, name="branch_a_fwd", grid=(s // ta,),
        in_specs=[pl.BlockSpec((ta, D), lambda i: (i, 0)),
                  pl.BlockSpec((16, D), lambda i: (jnp.maximum(i * per16 - 1, 0), 0)),
                  pl.BlockSpec((ta, D), lambda i: (i, 1)),
                  pl.BlockSpec((CONV_K, D), lambda i: (0, 0)), vec, gate, vec, gate, vec, vec],
        out_specs=[pl.BlockSpec((ta, D), lambda i: (i, 0))] * 5,
        out_shape=[SDS((s, D), BF16)] * 5,
        scratch_shapes=[pltpu.VMEM((ta, D), F32), pltpu.VMEM((ta, D), F32), pltpu.VMEM((ta, D), F32),
                        pltpu.VMEM((8, D), F32)],
        params=_cparams(("arbitrary",), 40), args=(z, z, z, conv_w, conv_b, w_r, b_r, w_i, b_i, lam), comm=comm)


def _sgu_common(ub, vb, lg, lb, with_grad):
    if with_grad:
        u, du = _gelu_and_grad(ub)
        v, dv = _gelu_and_grad(vb)
    else:
        u, v, du, dv = _gelu(ub), _gelu(vb), None, None
    mu = jnp.mean(v, axis=-1, keepdims=True)
    vc = v - mu
    rstd = lax.rsqrt(jnp.mean(vc * vc, axis=-1, keepdims=True) + LN_EPS)
    vhat = vc * rstd
    vln = vhat * lg + lb
    return u, du, dv, rstd, vhat, vln


def _masked_ws(ws_ref):
    t = lax.broadcasted_iota(jnp.int32, (CHUNK, CHUNK), 0)
    c = lax.broadcasted_iota(jnp.int32, (CHUNK, CHUNK), 1)
    keep = c <= t
    return [jnp.where(keep, ws_ref[g], 0.0).astype(BF16) for g in range(GROUPS)]


def _branch_b_merge_out(ya, z, x, ln_g, ln_b, w_s, b_s_t, w_oa, w_ob, w_out, comm=None):
    s = x.shape[0]
    tm = min(TM_MERGE, s)

    def body(ub_ref, vb_ref, lg_ref, lb_ref, ws_ref, bs_ref, ya_ref, ma_ref, mb_ref, x_ref, woa_ref, wob_ref, wo_ref,
             yb_ref, pa_ref, pb_ref, mg_ref, h1_ref):
        u, _, _, _, _, vln = _sgu_common(ub_ref[...].astype(F32), vb_ref[...].astype(F32),
                                         lg_ref[...], lb_ref[...], False)
        vlnb = vln.astype(BF16)
        wm = _masked_ws(ws_ref)
        bs = bs_ref[...]
        for c in range(tm // CHUNK):
            rs = slice(c * CHUNK, (c + 1) * CHUNK)
            for g in range(GROUPS):
                cs = slice(g * GROUP_DIM, (g + 1) * GROUP_DIM)
                sp = _dot(wm[g], vlnb[rs, cs]) + bs[:, g:g + 1]
                yb_ref[rs, cs] = (u[rs, cs] * sp).astype(BF16)

        pa = _dot(ya_ref[...], woa_ref[...])
        pb = _dot(yb_ref[...], wob_ref[...])
        merged = (jax.nn.sigmoid(ma_ref[...].astype(F32)) * pa
                  + jax.nn.sigmoid(mb_ref[...].astype(F32)) * pb).astype(BF16)
        pa_ref[...] = pa.astype(BF16)
        pb_ref[...] = pb.astype(BF16)
        mg_ref[...] = merged
        h1_ref[...] = x_ref[...] + _dot(merged, wo_ref[...])

    row = pl.BlockSpec((tm, D), lambda i: (i, 0))
    col = lambda q: pl.BlockSpec((tm, D), lambda i: (i, q))
    vec = pl.BlockSpec((1, D), lambda i: (0, 0))
    wsp = pl.BlockSpec((D, D), lambda i: (0, 0))
    return _call(
        body, name="branch_b_merge_out", grid=(s // tm,),
        in_specs=[col(2), col(3), vec, vec, pl.BlockSpec((GROUPS, CHUNK, CHUNK), lambda i: (0, 0, 0)),
                  pl.BlockSpec((CHUNK, GROUPS), lambda i: (0, 0)), row, col(4), col(5), row, wsp, wsp, wsp],
        out_specs=[row, row, row, row, row],
        out_shape=[SDS((s, D), BF16)] * 4 + [SDS((s, D), F32)], scratch_shapes=[],
        params=_cparams(("arbitrary",), 56),
        args=(z, z, ln_g, ln_b, w_s, b_s_t, ya, z, z, x, w_oa, w_ob, w_out), comm=comm)


def _mlp_fwd(h1, g_mlp, w_up_g, w_down, g_fin, tgt):
    s = h1.shape[0]
    tm = min(TM_ROWS, s)
    nj = N_SLOT

    def body(h1_ref, gm_ref, wu_ref, wd_ref, gf_ref, t_ref, r_ref, at_ref, n2t_ref, dh2_ref, dh2b_ref, loss_ref,
             dgf_ref, n2_s, acc_s):
        i, j = pl.program_id(0), pl.program_id(1)

        @pl.when(j == 0)
        def _():
            hv = h1_ref[...]
            rstd = lax.rsqrt(jnp.mean(hv * hv, axis=-1, keepdims=True) + NORM_EPS)
            nb = (hv * rstd * gm_ref[...]).astype(BF16)
            n2_s[...] = nb
            n2t_ref[...] = nb.T
            acc_s[...] = jnp.zeros_like(acc_s)

        @pl.when((i == 0) & (j == 0))
        def _():
            loss_ref[...] = jnp.zeros_like(loss_ref)
            dgf_ref[...] = jnp.zeros_like(dgf_ref)

        r = jnp.maximum(_dot(n2_s[...], wu_ref[...]), 0.0)
        r_ref[...] = r.astype(BF16)
        act = (r * r).astype(BF16)
        at_ref[...] = act.T
        acc_s[...] += _dot(act, wd_ref[...])

        @pl.when(j == nj - 1)
        def _():
            h2 = h1_ref[...] + acc_s[...]
            rstd = lax.rsqrt(jnp.mean(h2 * h2, axis=-1, keepdims=True) + NORM_EPS)
            hh = h2 * rstd
            gf = gf_ref[...]
            e = hh * gf - t_ref[...]
            loss_ref[...] += jnp.sum(e * e) * (0.5 / D)
            dy = e * (1.0 / D)
            dgf_ref[...] += jnp.sum(dy * hh, axis=0, keepdims=True)
            dhh = dy * gf
            dh2 = rstd * (dhh - hh * jnp.mean(dhh * hh, axis=-1, keepdims=True))
            dh2_ref[...] = dh2
            dh2b_ref[...] = dh2.astype(BF16)

    row = pl.BlockSpec((tm, D), lambda i, j: (i, 0))
    vec = pl.BlockSpec((1, D), lambda i, j: (0, 0))
    return pl.pallas_call(
        body, name="mlp_fwd", grid=(s // tm, nj),
        in_specs=[row, vec, pl.BlockSpec((None, D, FF_COLS), lambda i, j: (j, 0, 0)),
                  pl.BlockSpec((FF_COLS, D), lambda i, j: (j, 0)), vec, row],
        out_specs=[pl.BlockSpec((tm, FF_COLS), lambda i, j: (i, j)), pl.BlockSpec((FF_COLS, tm), lambda i, j: (j, i)),
                   pl.BlockSpec((D, tm), lambda i, j: (0, i)), row, row, pl.BlockSpec((8, 128), lambda i, j: (0, 0)),
                   vec],
        out_shape=[SDS((s, nj * FF_COLS), BF16), SDS((nj * FF_COLS, s), BF16), SDS((D, s), BF16), SDS((s, D), F32),
                   SDS((s, D), BF16), SDS((8, 128), F32), SDS((1, D), F32)],
        scratch_shapes=[pltpu.VMEM((tm, D), BF16), pltpu.VMEM((tm, D), F32)],
        compiler_params=_cparams(("arbitrary", "arbitrary"), 56),
    )(h1, _small_in_hbm(g_mlp), w_up_g, w_down, _small_in_hbm(g_fin), tgt)


def _mlp_bwd(dh2, dh2b, r, w_down, w_up_g, h1, g_mlp, comm=None):
    s = h1.shape[0]
    tm = min(TM_ROWS, s)
    nj = N_SLOT

    def body(dh2_ref, dh2b_ref, r_ref, wd_ref, wu_ref, h1_ref, gm_ref, df_ref, dh1_ref, dgm_ref, acc_s):
        i, j = pl.program_id(0), pl.program_id(1)

        @pl.when(j == 0)
        def _():
            acc_s[...] = jnp.zeros_like(acc_s)

        @pl.when((i == 0) & (j == 0))
        def _():
            dgm_ref[...] = jnp.zeros_like(dgm_ref)

        d_act = _dot_nt(dh2b_ref[...], wd_ref[...])
        df = (d_act * (2.0 * r_ref[...].astype(F32))).astype(BF16)
        df_ref[...] = df
        acc_s[...] += _dot_nt(df, wu_ref[...])

        @pl.when(j == nj - 1)
        def _():
            hv = h1_ref[...]
            rstd = lax.rsqrt(jnp.mean(hv * hv, axis=-1, keepdims=True) + NORM_EPS)
            hh = hv * rstd
            dn2 = acc_s[...]
            dgm_ref[...] += jnp.sum(dn2 * hh, axis=0, keepdims=True)
            dhat = dn2 * gm_ref[...]
            dh1_ref[...] = dh2_ref[...] + rstd * (dhat - hh * jnp.mean(dhat * hh, axis=-1, keepdims=True))

    row = pl.BlockSpec((tm, D), lambda i, j: (i, 0))
    vec = pl.BlockSpec((1, D), lambda i, j: (0, 0))
    ffb = pl.BlockSpec((tm, FF_COLS), lambda i, j: (i, j))
    return _call(
        body, name="mlp_bwd", grid=(s // tm, nj),
        in_specs=[row, row, ffb, pl.BlockSpec((FF_COLS, D), lambda i, j: (j, 0)),
                  pl.BlockSpec((None, D, FF_COLS), lambda i, j: (j, 0, 0)), row, vec],
        out_specs=[ffb, row, vec],
        out_shape=[SDS((s, nj * FF_COLS), BF16), SDS((s, D), F32), SDS((1, D), F32)],
        scratch_shapes=[pltpu.VMEM((tm, D), F32)],
        params=_cparams(("arbitrary", "arbitrary"), 56), args=(dh2, dh2b, r, w_down, w_up_g, h1, g_mlp), comm=comm)


def _merge_bwd(dh1, z, pa, pb, w_out, w_oa, w_ob, comm=None):
    s = dh1.shape[0]
    tm = min(TM_MERGE, s)

    def body(dh1_ref, ma_ref, mb_ref, pa_ref, pb_ref, wo_ref, woa_ref, wob_ref,
             dz_ref, dpa_ref, dpb_ref, dya_ref, dyb_ref):
        dm = _dot_nt(dh1_ref[...].astype(BF16), wo_ref[...])
        sa = jax.nn.sigmoid(ma_ref[...].astype(F32))
        sb = jax.nn.sigmoid(mb_ref[...].astype(F32))
        dpa = (dm * sa).astype(BF16)
        dpb = (dm * sb).astype(BF16)
        dz_ref[:, 0:D] = (dm * pa_ref[...].astype(F32) * sa * (1.0 - sa)).astype(BF16)
        dz_ref[:, D:2 * D] = (dm * pb_ref[...].astype(F32) * sb * (1.0 - sb)).astype(BF16)
        dpa_ref[...] = dpa
        dpb_ref[...] = dpb
        dya_ref[...] = _dot_nt(dpa, woa_ref[...]).astype(BF16)
        dyb_ref[...] = _dot_nt(dpb, wob_ref[...]).astype(BF16)

    row = pl.BlockSpec((tm, D), lambda i: (i, 0))
    wsp = pl.BlockSpec((D, D), lambda i: (0, 0))
    return _call(
        body, name="merge_bwd", grid=(s // tm,),
        in_specs=[row, pl.BlockSpec((tm, D), lambda i: (i, 4)), pl.BlockSpec((tm, D), lambda i: (i, 5)),
                  row, row, wsp, wsp, wsp],
        out_specs=[pl.BlockSpec((tm, 2 * D), lambda i: (i, 2)), row, row, row, row],
        out_shape=[SDS((s, 6 * D), BF16)] + [SDS((s, D), BF16)] * 4, scratch_shapes=[],
        params=_cparams(("arbitrary",), 48), args=(dh1, z, z, pa, pb, w_out, w_oa, w_ob), comm=comm)


def _branch_b_bwd(dz, dyb, z, ln_g, ln_b, w_s, b_s_t, comm=None):
    s = z.shape[0]
    tb = min(T_BRANCH_B, s)

    def body(dz_in, dyb_ref, ub_ref, vb_ref, lg_ref, lb_ref, ws_ref, bs_ref,
             dz_ref, dws_ref, dbs_ref, dln_ref, du_s, dvln_s):
        del dz_in

        @pl.when(pl.program_id(0) == 0)
        def _():
            dws_ref[...] = jnp.zeros_like(dws_ref)
            dbs_ref[...] = jnp.zeros_like(dbs_ref)
            dln_ref[...] = jnp.zeros_like(dln_ref)

        lg = lg_ref[...]
        u, du, dv, rstd, vhat, vln = _sgu_common(ub_ref[...].astype(F32), vb_ref[...].astype(F32),
                                                 lg, lb_ref[...], True)
        vlnb = vln.astype(BF16)
        dyb_v = dyb_ref[...].astype(F32)
        wm = _masked_ws(ws_ref)
        keep = (lax.broadcasted_iota(jnp.int32, (CHUNK, CHUNK), 1)
                <= lax.broadcasted_iota(jnp.int32, (CHUNK, CHUNK), 0))
        bs = bs_ref[...]
        for c in range(tb // CHUNK):
            rs = slice(c * CHUNK, (c + 1) * CHUNK)
            for g in range(GROUPS):
                cs = slice(g * GROUP_DIM, (g + 1) * GROUP_DIM)
                v_blk = vlnb[rs, cs]
                sp = _dot(wm[g], v_blk) + bs[:, g:g + 1]
                d_sp = dyb_v[rs, cs] * u[rs, cs]
                d_spb = d_sp.astype(BF16)
                du_s[rs, cs] = dyb_v[rs, cs] * sp
                dvln_s[rs, cs] = _dot_tn(wm[g], d_spb)
                dws_ref[g] += jnp.where(keep, _dot_nt(d_spb, v_blk), 0.0)
                dbs_ref[g] += jnp.broadcast_to(jnp.sum(d_sp, axis=-1, keepdims=True), (CHUNK, CHUNK))
        dvln = dvln_s[...]
        dln_ref[0:1, :] += jnp.sum(dvln * vhat, axis=0, keepdims=True)
        dln_ref[1:2, :] += jnp.sum(dvln, axis=0, keepdims=True)
        dvh = dvln * lg
        d_v = rstd * (dvh - jnp.mean(dvh, axis=-1, keepdims=True)
                      - vhat * jnp.mean(dvh * vhat, axis=-1, keepdims=True))
        dz_ref[:, 0:D] = (du_s[...] * du).astype(BF16)
        dz_ref[:, D:2 * D] = (d_v * dv).astype(BF16)

    vec = pl.BlockSpec((1, D), lambda i: (0, 0))
    sq = pl.BlockSpec((GROUPS, CHUNK, CHUNK), lambda i: (0, 0, 0))
    return _call(
        body, name="branch_b_bwd", grid=(s // tb,),
        in_specs=[ANY, pl.BlockSpec((tb, D), lambda i: (i, 0)),
                  pl.BlockSpec((tb, D), lambda i: (i, 2)), pl.BlockSpec((tb, D), lambda i: (i, 3)), vec, vec, sq,
                  pl.BlockSpec((CHUNK, GROUPS), lambda i: (0, 0))],
        out_specs=[pl.BlockSpec((tb, 2 * D), lambda i: (i, 1)), sq, sq, pl.BlockSpec((8, D), lambda i: (0, 0))],
        out_shape=[SDS(dz.shape, BF16), SDS((GROUPS, CHUNK, CHUNK), F32), SDS((GROUPS, CHUNK, CHUNK), F32),
                   SDS((8, D), F32)],
        scratch_shapes=[pltpu.VMEM((tb, D), F32), pltpu.VMEM((tb, D), F32)], aliases={0: 0},
        params=_cparams(("arbitrary",), 40), args=(dz, dyb, z, z, ln_g, ln_b, w_s, b_s_t), comm=comm)


def _branch_a_bwd(dz, dya, z, hs, xc, r, ig, conv_w, w_r, w_i, lam, comm=None):
    s = z.shape[0]
    ta = min(T_BRANCH_A, s)
    nb = s // ta
    per16 = ta // 16

    def body(dz_in, dya_ref, xa_ref, ga_ref, hs_ref, hp_ref, xc_ref, r_ref, i_ref, cw_ref, wr_ref, wi_ref,
             lam_ref, dz_ref, vec_ref, dwr_ref, dwi_ref, a_s, b_s, h_s, dcar_s, acar_s, dxc_s):
        del dz_in
        i = pl.program_id(0)
        blk = nb - 1 - i

        @pl.when(i == 0)
        def _():
            dcar_s[...] = jnp.zeros_like(dcar_s)
            acar_s[...] = jnp.zeros_like(acar_s)
            dxc_s[...] = jnp.zeros_like(dxc_s)
            vec_ref[...] = jnp.zeros_like(vec_ref)
            dwr_ref[...] = jnp.zeros_like(dwr_ref)
            dwi_ref[...] = jnp.zeros_like(dwi_ref)

        cw = cw_ref[...]
        lam_v = lam_ref[...]
        xa = xa_ref[...].astype(F32)
        xcb = xc_ref[...]
        xc = xcb.astype(F32)
        sp_lam = _softplus(-lam_v)
        r_v, i_v = r_ref[...].astype(F32), i_ref[...].astype(F32)
        a_v, m_v = _decay(r_v, sp_lam)

        hs_v = hs_ref[...].astype(F32)
        hprev8 = jnp.where(blk > 0, hp_ref[...].astype(F32)[8:16], 0.0)
        h_m1 = _rows_shifted(hprev8, hs_v, 1)
        gg, dgg = _gelu_and_grad(ga_ref[...].astype(F32))
        dya_v = dya_ref[...].astype(F32)
        dz_ref[:, D:2 * D] = (dya_v * hs_v * dgg).astype(BF16)

        a_s[...] = _rows_advanced(a_v, acar_s[...], 1)
        b_s[...] = dya_v * gg

        row = lax.broadcasted_iota(jnp.int32, (8, D), 0)
        ng = ta // 8

        def group(gi, carry):
            off = pl.multiple_of((ng - 1 - gi) * 8, 8)
            c8 = a_s[pl.ds(off, 8), :]
            d8 = b_s[pl.ds(off, 8), :]
            for d in (1, 2, 4):
                c_sh = jnp.where(row < 8 - d, pltpu.roll(c8, 8 - d, 0), 1.0)
                d_sh = jnp.where(row < 8 - d, pltpu.roll(d8, 8 - d, 0), 0.0)
                d8 = c8 * d_sh + d8
                c8 = c8 * c_sh
            dh8 = d8 + c8 * carry
            h_s[pl.ds(off, 8), :] = dh8
            return jnp.broadcast_to(dh8[0:1, :], (8, D))

        dcar_s[...] = lax.fori_loop(0, ng, group, dcar_s[...])
        acar_s[...] = jnp.broadcast_to(a_v[0:1, :], (8, D))

        dbx = h_s[...]
        d_mult = dbx * xc * i_v
        d_loga = dbx * h_m1 * a_v - d_mult * (a_v * a_v) / m_v
        d_pr = d_loga * ((-LRU_C) * sp_lam) * r_v * (1.0 - r_v)
        d_pi = dbx * xc * m_v * i_v * (1.0 - i_v)
        vec_ref[7:8, :] += jnp.sum(d_loga * r_v, axis=0, keepdims=True) * (LRU_C * jax.nn.sigmoid(-lam_v))
        vec_ref[5:6, :] += jnp.sum(d_pr, axis=0, keepdims=True)
        vec_ref[6:7, :] += jnp.sum(d_pi, axis=0, keepdims=True)
        d_prb = d_pr.astype(BF16)
        d_pib = d_pi.astype(BF16)
        h_s[...] = dbx * i_v * m_v
        for h in range(HEADS):
            sl = slice(h * HEAD_DIM, (h + 1) * HEAD_DIM)
            h_s[:, sl] += _dot_nt(d_prb[:, sl], wr_ref[h]) + _dot_nt(d_pib[:, sl], wi_ref[h])
            dwr_ref[h] += _dot_tn(xcb[:, sl], d_prb[:, sl])
            dwi_ref[h] += _dot_tn(xcb[:, sl], d_pib[:, sl])
        d_xc = h_s[...]
        vec_ref[4:5, :] += jnp.sum(d_xc, axis=0, keepdims=True)
        vec_ref[0:1, :] += jnp.sum(d_xc * xa, axis=0, keepdims=True)
        d_xa = cw[0:1, :] * d_xc
        nxt = dxc_s[...]
        for k in range(1, CONV_K):
            ahead = _rows_advanced(d_xc, nxt, k)
            vec_ref[k:k + 1, :] += jnp.sum(ahead * xa, axis=0, keepdims=True)
            d_xa = d_xa + cw[k:k + 1, :] * ahead
        dz_ref[:, 0:D] = d_xa.astype(BF16)
        dxc_s[...] = d_xc[0:8, :]

    vec = pl.BlockSpec((1, D), lambda i: (0, 0))
    gate = pl.BlockSpec((HEADS, HEAD_DIM, HEAD_DIM), lambda i: (0, 0, 0))
    cur = lambda c: pl.BlockSpec((ta, D), lambda i: (nb - 1 - i, c))
    before = lambda c: pl.BlockSpec((16, D), lambda i: (jnp.maximum((nb - 1 - i) * per16 - 1, 0), c))
    return _call(
        body, name="branch_a_bwd", grid=(nb,),
        in_specs=[ANY, cur(0), cur(0), cur(1), cur(0), before(0), cur(0), cur(0), cur(0),
                  pl.BlockSpec((CONV_K, D), lambda i: (0, 0)), gate, gate, vec],
        out_specs=[pl.BlockSpec((ta, 2 * D), lambda i: (nb - 1 - i, 0)), pl.BlockSpec((8, D), lambda i: (0, 0)),
                   gate, gate],
        out_shape=[SDS(dz.shape, BF16), SDS((8, D), F32), SDS((HEADS, HEAD_DIM, HEAD_DIM), F32),
                   SDS((HEADS, HEAD_DIM, HEAD_DIM), F32)],
        scratch_shapes=[pltpu.VMEM((ta, D), F32)] * 3 + [pltpu.VMEM((8, D), F32)] * 3, aliases={0: 0},
        params=_cparams(("arbitrary",), 48),
        args=(dz, dya, z, z, hs, hs, xc, r, ig, conv_w, w_r, w_i, lam), comm=comm)


def _in_bwd(dz, w_in_g, x, dh1, g_mix, comm=None):
    s = x.shape[0]
    tm = min(TM_ROWS, s)
    nj = N_SLOT

    def body(dz_ref, w_ref, x_ref, dh1_ref, g_ref, dx_ref, dg_ref, acc_s):
        i, j = pl.program_id(0), pl.program_id(1)

        @pl.when(j == 0)
        def _():
            acc_s[...] = jnp.zeros_like(acc_s)

        @pl.when((i == 0) & (j == 0))
        def _():
            dg_ref[...] = jnp.zeros_like(dg_ref)

        acc_s[...] += _dot_nt(dz_ref[...], w_ref[...])

        @pl.when(j == nj - 1)
        def _():
            xv = x_ref[...]
            rstd = lax.rsqrt(jnp.mean(xv * xv, axis=-1, keepdims=True) + NORM_EPS)
            xh = xv * rstd
            dn = acc_s[...]
            dg_ref[...] += jnp.sum(dn * xh, axis=0, keepdims=True)
            dhat = dn * g_ref[...]
            dx_ref[...] = dh1_ref[...] + rstd * (dhat - xh * jnp.mean(dhat * xh, axis=-1, keepdims=True))

    row = pl.BlockSpec((tm, D), lambda i, j: (i, 0))
    vec = pl.BlockSpec((1, D), lambda i, j: (0, 0))
    return _call(
        body, name="in_bwd", grid=(s // tm, nj),
        in_specs=[pl.BlockSpec((tm, W_IN_COLS), lambda i, j: (i, j)),
                  pl.BlockSpec((None, D, W_IN_COLS), lambda i, j: (j, 0, 0)), row, row, vec],
        out_specs=[row, vec],
        out_shape=[SDS((s, D), F32), SDS((1, D), F32)],
        scratch_shapes=[pltpu.VMEM((tm, D), F32)],
        params=_cparams(("arbitrary", "arbitrary"), 48), args=(dz, w_in_g, x, dh1, g_mix), comm=comm)


def _wgrad(name, a, b):
    s = a.shape[0]
    ts = min(TM_ROWS, s)
    a_w, b_w = a.shape[1], b.shape[1]

    def body(a_ref, b_ref, o_ref, acc_s):
        t = pl.program_id(0)

        @pl.when(t == 0)
        def _():
            acc_s[...] = jnp.zeros_like(acc_s)

        acc_s[...] += _dot_tn(a_ref[...].astype(BF16), b_ref[...].astype(BF16))

        @pl.when(t == pl.num_programs(0) - 1)
        def _():
            o_ref[...] = acc_s[...].astype(BF16)

    return pl.pallas_call(
        body, name=name, grid=(s // ts,),
        in_specs=[pl.BlockSpec((ts, a_w), lambda t: (t, 0)), pl.BlockSpec((ts, b_w), lambda t: (t, 0))],
        out_specs=pl.BlockSpec((a_w, b_w), lambda t: (0, 0)),
        out_shape=SDS((a_w, b_w), BF16),
        scratch_shapes=[pltpu.VMEM((a_w, b_w), F32)],
        compiler_params=_cparams(("arbitrary",), 48),
    )(a, b)


def _place():
    x, y, c = lax.axis_index("x"), lax.axis_index("y"), lax.axis_index("c")
    return x, y, c


def _other_chips(x, y):
    return [(x, 1 - y, 2 * x + 1 - y), (1 - x, y, 2 * (1 - x) + y), (1 - x, 1 - y, 2 * (1 - x) + 1 - y)]


class _Plan:
    def __init__(self, arrays, out_shape, sems, start, finish, middle=None, middle_at=6):
        self.arrays, self.out_shape, self.sems, self.start, self.finish = arrays, out_shape, sems, start, finish
        self.middle, self.middle_at = middle, middle_at


def _gather_plan(shards, middle_at=6):
    n = len(shards)

    def copies(ins, outs, sems):
        send_sems, recv_sems, local_sems = sems
        x, y, c = _place()
        chip = 2 * x + y
        me = 2 * chip + c
        sib = (x, y, 1 - c)
        chips = _other_chips(x, y)

        def rc(k, t, src, blk, to):
            return pltpu.make_async_remote_copy(
                src_ref=src, dst_ref=outs[t].at[blk], send_sem=send_sems.at[k * n + t],
                recv_sem=recv_sems.at[k * n + t], device_id=to, device_id_type=MESH)

        (yx, yy, y_chip), (xx, xy, x_chip), _ = chips
        local = [pltpu.make_async_copy(ins[t], outs[t].at[me], local_sems.at[t]) for t in range(n)]
        sends = ([rc(0, t, ins[t], me, sib) for t in range(n)] + [rc(1, t, ins[t], me, (yx, yy, c)) for t in range(n)]
                 + [rc(2, t, ins[t], me, (xx, xy, c)) for t in range(n)])
        passed = [[rc(4 + j, t, outs[t].at[2 * pc + c], 2 * pc + c, sib) for t in range(n)]
                  for j, (_, _, pc) in enumerate(chips)]
        relays = [[rc(3, t, outs[t].at[2 * y_chip + c], 2 * y_chip + c, (xx, xy, c)) for t in range(n)],
                  [rc(3, t, outs[t].at[2 * x_chip + c], 2 * x_chip + c, (yx, yy, c)) for t in range(n)]]
        return rc, local, sends, passed, relays, chips, chip, c, sib

    def start(ins, outs, sems):
        _, local, sends, _, _, _, _, _, _ = copies(ins, outs, sems)
        for cp in local + sends:
            cp.start()

    def middle(ins, outs, sems):
        rc, _, _, passed, relays, chips, _, c, sib = copies(ins, outs, sems)
        for j in range(2):
            for t in range(n):
                rc(1 + j, t, ins[t], 2 * chips[j][2] + c, sib).wait_recv()
        for j in range(2):
            for cp in passed[j]:
                cp.start()

            @pl.when(c == j)
            def _():
                for cp in relays[j]:
                    cp.start()

    def finish(ins, outs, sems):
        rc, local, sends, passed, relays, chips, chip, c, sib = copies(ins, outs, sems)
        far = 2 * chips[2][2] + c
        for t in range(n):
            rc(3, t, ins[t], far, sib).wait_recv()
        for cp in passed[2]:
            cp.start()
        for t in range(n):
            rc(0, t, ins[t], 2 * chip + 1 - c, sib).wait_recv()
        for j, (px, py, pc) in enumerate(chips):
            for t in range(n):
                rc(4 + j, t, ins[t], 2 * pc + 1 - c, sib).wait_recv()
        for cp in sends + passed[0] + passed[1] + passed[2]:
            cp.wait_send()
        for j in range(2):
            @pl.when(c == j)
            def _():
                for cp in relays[j]:
                    cp.wait_send()
        for cp in local:
            cp.wait()

    return _Plan(list(shards), [SDS((N_SLOT,) + tuple(a.shape), a.dtype) for a in shards],
                 [pltpu.SemaphoreType.DMA((7 * n,)), pltpu.SemaphoreType.DMA((7 * n,)),
                  pltpu.SemaphoreType.DMA((n,))], start, finish, middle, middle_at)


def _sibling_plan(grads, whole=()):
    n, m = len(grads), len(whole)

    def copies(ins, outs, sems):
        send_sems, recv_sems = sems
        x, y, c = _place()
        sib = (x, y, 1 - c)

        def rc(t, src, dst):
            return pltpu.make_async_remote_copy(src_ref=src, dst_ref=dst, send_sem=send_sems.at[t],
                                                recv_sem=recv_sems.at[t], device_id=sib, device_id_type=MESH)
        return rc, c

    def start(ins, outs, sems):
        rc, c = copies(ins, outs, sems)
        for t in range(n):
            for j in range(4):
                rc(t, ins[t].at[2 * j + 1 - c], outs[t].at[j]).start()
        for t in range(n, n + m):
            rc(t, ins[t], outs[t]).start()

    def finish(ins, outs, sems):
        rc, _ = copies(ins, outs, sems)
        for t in range(n):
            rc(t, ins[t].at[pl.ds(0, 4)], outs[t]).wait()
        for t in range(n, n + m):
            rc(t, ins[t], outs[t]).wait()

    return _Plan(list(grads) + list(whole),
                 [SDS((4,) + tuple(g.shape[1:]), g.dtype) for g in grads] + [SDS(a.shape, a.dtype) for a in whole],
                 [pltpu.SemaphoreType.DMA((n + m,)), pltpu.SemaphoreType.DMA((n + m,))], start, finish)


def _chips_plan(parts, whole=()):
    n, m = len(parts), len(whole)

    def src_of(ins, t, pc):
        return ins[t].at[pc] if t < n else ins[t]

    def local_copies(ins, outs, sems, chip):
        return [pltpu.make_async_copy(src_of(ins, t, chip), outs[t].at[chip], sems[2].at[t]) for t in range(n + m)]

    def start(ins, outs, sems):
        send_sems, recv_sems, _ = sems
        x, y, c = _place()
        chip = 2 * x + y
        for cp in local_copies(ins, outs, sems, chip):
            cp.start()
        for px, py, pc in _other_chips(x, y):
            for t in range(n + m):
                pltpu.make_async_remote_copy(src_ref=src_of(ins, t, pc), dst_ref=outs[t].at[chip],
                                             send_sem=send_sems.at[t], recv_sem=recv_sems.at[t],
                                             device_id=(px, py, c), device_id_type=MESH).start()

    def finish(ins, outs, sems):
        send_sems, recv_sems, _ = sems
        x, y, c = _place()
        for t in range(n + m):
            three = outs[t].at[pl.ds(0, 3)]
            pltpu.make_async_remote_copy(src_ref=three, dst_ref=three, send_sem=send_sems.at[t],
                                         recv_sem=recv_sems.at[t], device_id=(x, y, c), device_id_type=MESH).wait()
        for cp in local_copies(ins, outs, sems, 2 * x + y):
            cp.wait()

    return _Plan(list(parts) + list(whole),
                 [SDS(p.shape, p.dtype) for p in parts] + [SDS((4,) + tuple(a.shape), a.dtype) for a in whole],
                 [pltpu.SemaphoreType.DMA((n + m,)), pltpu.SemaphoreType.DMA((n + m,)),
                  pltpu.SemaphoreType.DMA((n + m,))], start, finish)


def _exchange_plan(arr):
    def peers(x, y, c):
        flip = lambda v, f: 1 - v if f else v
        return [(flip(x, fx), flip(y, fy), flip(c, fc))
                for fx in (0, 1) for fy in (0, 1) for fc in (0, 1) if fx or fy or fc]

    def start(ins, outs, sems):
        x, y, c = _place()
        me = 4 * x + 2 * y + c
        pltpu.make_async_copy(ins[0], outs[0].at[me], sems[2].at[0]).start()
        for to in peers(x, y, c):
            pltpu.make_async_remote_copy(src_ref=ins[0], dst_ref=outs[0].at[me], send_sem=sems[0].at[0],
                                         recv_sem=sems[1].at[0], device_id=to, device_id_type=MESH).start()

    def finish(ins, outs, sems):
        x, y, c = _place()
        seven = outs[0].at[pl.ds(0, 7)]
        pltpu.make_async_remote_copy(src_ref=seven, dst_ref=seven, send_sem=sems[0].at[0], recv_sem=sems[1].at[0],
                                     device_id=(x, y, c), device_id_type=MESH).wait()
        pltpu.make_async_copy(ins[0], outs[0].at[4 * x + 2 * y + c], sems[2].at[0]).wait()

    return _Plan([arr], [SDS((N_SLOT,) + tuple(arr.shape), arr.dtype)],
                 [pltpu.SemaphoreType.DMA((1,)), pltpu.SemaphoreType.DMA((1,)), pltpu.SemaphoreType.DMA((1,))],
                 start, finish)


def _join(*plans):
    def cut(seq, sizes):
        out, at = [], 0
        for k in sizes:
            out.append(seq[at:at + k])
            at += k
        return out

    n_arr = [len(p.arrays) for p in plans]
    n_sem = [len(p.sems) for p in plans]

    def start(ins, outs, sems):
        for p, i, o, s in zip(plans, cut(ins, n_arr), cut(outs, n_arr), cut(sems, n_sem)):
            p.start(i, o, s)

    def finish(ins, outs, sems):
        for p, i, o, s in zip(plans, cut(ins, n_arr), cut(outs, n_arr), cut(sems, n_sem)):
            p.finish(i, o, s)

    def middle(ins, outs, sems):
        for p, i, o, s in zip(plans, cut(ins, n_arr), cut(outs, n_arr), cut(sems, n_sem)):
            if p.middle is not None:
                p.middle(i, o, s)

    return _Plan([a for p in plans for a in p.arrays], [o for p in plans for o in p.out_shape],
                 [s for p in plans for s in p.sems], start, finish,
                 middle if any(p.middle is not None for p in plans) else None,
                 max(p.middle_at for p in plans))


def _run_plan(name, plan):
    k = len(plan.arrays)

    def body(*refs):
        ins, outs, sems = refs[:k], refs[k:2 * k], refs[2 * k:]
        plan.start(ins, outs, sems)
        if plan.middle is not None:
            plan.middle(ins, outs, sems)
        plan.finish(ins, outs, sems)

    return pl.pallas_call(
        body, name=name, in_specs=[ANY] * k, out_specs=[ANY] * k, out_shape=plan.out_shape,
        scratch_shapes=plan.sems, compiler_params=pltpu.CompilerParams(has_side_effects=True),
    )(*plan.arrays)


def _call(body, *, name, grid, in_specs, out_specs, out_shape, scratch_shapes, params, args, comm=None,
          aliases=None, prefetch=()):
    aliases = aliases or {}
    n_pre = len(prefetch)

    def launch(fn, ins_specs, outs_specs, outs_shape, scratch, operands):
        spec = pltpu.PrefetchScalarGridSpec(num_scalar_prefetch=n_pre, grid=grid, in_specs=ins_specs,
                                            out_specs=outs_specs, scratch_shapes=scratch)
        return pl.pallas_call(fn, name=name, grid_spec=spec, out_shape=outs_shape, compiler_params=params,
                              input_output_aliases=aliases)(*prefetch, *[_small_in_hbm(a) for a in operands])

    if comm is None:
        return list(launch(body, in_specs, out_specs, out_shape, scratch_shapes, args)), []
    n_in, n_out, n_scr, k = len(in_specs), len(out_specs), len(scratch_shapes), len(comm.arrays)

    def wrapped(*refs):
        pre, refs = refs[:n_pre], refs[n_pre:]
        ins = refs[:n_in]
        c_in = refs[n_in:n_in + k]
        outs = refs[n_in + k:n_in + k + n_out]
        c_out = refs[n_in + k + n_out:n_in + 2 * k + n_out]
        scr = refs[n_in + 2 * k + n_out:n_in + 2 * k + n_out + n_scr]
        sems = refs[n_in + 2 * k + n_out + n_scr:]
        step, steps = pl.program_id(0), grid[0]
        for d in range(1, len(grid)):
            step, steps = step * grid[d] + pl.program_id(d), steps * grid[d]

        @pl.when(step == 0)
        def _():
            comm.start(c_in, c_out, sems)

        if comm.middle is not None:
            @pl.when(step == (comm.middle_at * steps) // 8)
            def _():
                comm.middle(c_in, c_out, sems)

        body(*pre, *ins, *outs, *scr)

        @pl.when(step == steps - 1)
        def _():
            comm.finish(c_in, c_out, sems)

    res = launch(wrapped, list(in_specs) + [ANY] * k, list(out_specs) + [ANY] * k,
                 list(out_shape) + list(comm.out_shape), list(scratch_shapes) + list(comm.sems),
                 tuple(args) + tuple(comm.arrays))
    return list(res[:n_out]), list(res[n_out:])


def _wgrad_paired(name, a_t, b, core, by_rows=False, comm=None):
    s = b.shape[0]
    m = a_t.shape[0] // N_SLOT if by_rows else a_t.shape[0]
    cols = b.shape[1] if by_rows else b.shape[1] // N_SLOT
    half = N_SLOT // 2

    def owner(k, c):
        return 2 * (k % half) + jnp.where(k < half, 1 - c, c)

    def body(c_ref, a_ref, b_ref, sum_ref, out_s, recv_s, send_sems, recv_sems):
        del c_ref
        k = pl.program_id(0)
        x, y, c = _place()

        def to_sibling(j):
            return pltpu.make_async_remote_copy(src_ref=out_s.at[j % 2], dst_ref=recv_s.at[j],
                                                send_sem=send_sems.at[j], recv_sem=recv_sems.at[j],
                                                device_id=(x, y, 1 - c), device_id_type=MESH)

        @pl.when((k >= 2) & (k < half + 2))
        def _():
            to_sibling(k - 2).wait_send()

        @pl.when(k < half)
        def _():
            out_s[k % 2] = _dot(a_ref[...], b_ref[...]).astype(BF16)
            to_sibling(k).start()

        @pl.when(k >= half)
        def _():
            to_sibling(k - half).wait_recv()
            sum_ref[...] = (_dot(a_ref[...], b_ref[...]) + recv_s[k - half].astype(F32)).astype(BF16)

    if by_rows:
        in_specs = [pl.BlockSpec((m, s), lambda k, c_ref: (owner(k, c_ref[0]), 0)),
                    pl.BlockSpec((s, cols), lambda k, c_ref: (0, 0))]
    else:
        in_specs = [pl.BlockSpec((m, s), lambda k, c_ref: (0, 0)),
                    pl.BlockSpec((s, cols), lambda k, c_ref: (0, owner(k, c_ref[0])))]
    return _call(body, name=name, grid=(N_SLOT,), in_specs=in_specs,
                 out_specs=[pl.BlockSpec((None, m, cols), lambda k, c_ref: (jnp.maximum(k - half, 0), 0, 0))],
                 out_shape=[SDS((half, m, cols), BF16)],
                 scratch_shapes=[pltpu.VMEM((2, m, cols), BF16), pltpu.VMEM((half, m, cols), BF16),
                                 pltpu.SemaphoreType.DMA((half,)), pltpu.SemaphoreType.DMA((half,))],
                 params=_cparams(("arbitrary",), 56), args=(a_t, b), comm=comm, prefetch=(core,))


def _row_tile(rows):
    for t in (512, 256, 128, 64, 32, 16, 8):
        if rows % t == 0:
            return t
    return rows


def _pair_sum(name, g8, recv4, core):
    _, rows, cols = recv4.shape
    tr = _row_tile(rows)
    g42 = g8.reshape(4, 2, rows, cols)

    def body(c_ref, g_ref, r_ref, o_ref):
        del c_ref
        o_ref[...] = (g_ref[...].astype(F32) + r_ref[...].astype(F32)).astype(o_ref.dtype)

    return pl.pallas_call(
        body, name=name,
        grid_spec=pltpu.PrefetchScalarGridSpec(
            num_scalar_prefetch=1, grid=(4, rows // tr),
            in_specs=[pl.BlockSpec((None, None, tr, cols), lambda j, i, c_ref: (j, c_ref[0], i, 0)),
                      pl.BlockSpec((None, tr, cols), lambda j, i, c_ref: (j, i, 0))],
            out_specs=pl.BlockSpec((None, tr, cols), lambda j, i, c_ref: (j, i, 0))),
        out_shape=SDS(recv4.shape, g8.dtype),
        compiler_params=_cparams(("arbitrary", "arbitrary"), 32),
    )(core, g42, recv4)


def _add2(name, a, b):
    rows, cols = a.shape
    tr = _row_tile(rows)

    def body(a_ref, b_ref, o_ref):
        o_ref[...] = a_ref[...] + b_ref[...]

    blk = pl.BlockSpec((tr, cols), lambda i: (i, 0))
    return pl.pallas_call(body, name=name, grid=(rows // tr,), in_specs=[blk, blk], out_specs=blk,
                          out_shape=SDS(a.shape, a.dtype),
                          compiler_params=_cparams(("arbitrary",), 32))(a, b)


def _sum_terms(name, terms):
    k, rows, cols = terms.shape
    tr = _row_tile(rows)

    def body(r_ref, o_ref):
        acc = r_ref[0]
        for q in range(1, k):
            acc = acc + r_ref[q]
        o_ref[...] = acc

    return pl.pallas_call(body, name=name, grid=(rows // tr,),
                          in_specs=[pl.BlockSpec((k, tr, cols), lambda i: (0, i, 0))],
                          out_specs=pl.BlockSpec((tr, cols), lambda i: (i, 0)),
                          out_shape=SDS((rows, cols), terms.dtype),
                          compiler_params=_cparams(("arbitrary",), 32))(terms)


def _adam_update(g, w, m, v):
    c1 = 1.0 / (1.0 - ADAM_B1 ** ADAM_STEP)
    c2 = 1.0 / (1.0 - ADAM_B2 ** ADAM_STEP)
    mn = ADAM_B1 * m + (1.0 - ADAM_B1) * g
    vn = ADAM_B2 * v + (1.0 - ADAM_B2) * (g * g)
    delta = (-ADAM_LR) * ((mn * c1) / (jnp.sqrt(vn * c2) + ADAM_EPS) + ADAM_WD * w)
    return delta, mn, vn


def _adamw_many(name, gs, ws, ms, vs):
    n = len(gs)

    def body(*refs):
        for p in range(n):
            g, w, m, v = (refs[q * n + p][...] for q in range(4))
            d, mn, vn = _adam_update(g, w, m, v)
            refs[4 * n + p][...] = d
            refs[5 * n + p][...] = mn
            refs[6 * n + p][...] = vn

    full = [pl.BlockSpec(w.shape, lambda i: (0, 0)) for w in ws]
    shapes = [SDS(w.shape, F32) for w in ws]
    res = pl.pallas_call(body, name=name, grid=(1,), in_specs=full * 4, out_specs=full * 3, out_shape=shapes * 3,
                         compiler_params=_cparams(("arbitrary",), 32),
                         )(*[_small_in_hbm(a) for a in (*gs, *ws, *ms, *vs)])
    return [(res[p], res[n + p], res[2 * n + p]) for p in range(n)]


def _adamw(name, terms, w, m, v):
    k, rows, cols = terms.shape
    tr = _row_tile(rows)

    def body(t_ref, w_ref, m_ref, v_ref, g_ref, d_ref, mo_ref, vo_ref):
        g = t_ref[0].astype(F32)
        for q in range(1, k):
            g = g + t_ref[q].astype(F32)
        g_ref[...] = g
        d_ref[...], mo_ref[...], vo_ref[...] = _adam_update(g, w_ref[...], m_ref[...], v_ref[...])

    blk = pl.BlockSpec((tr, cols), lambda i: (i, 0))
    return pl.pallas_call(body, name=name, grid=(rows // tr,),
                          in_specs=[pl.BlockSpec((k, tr, cols), lambda i: (0, i, 0)), blk, blk, blk],
                          out_specs=[blk] * 4, out_shape=[SDS((rows, cols), F32)] * 4,
                          compiler_params=_cparams(("arbitrary",), 40),
                          )(*[pltpu.with_memory_space_constraint(a, pltpu.HBM) for a in (terms, w, m, v)])


def kernel(x, norm_mix_g, w_in, conv_w, conv_b, w_rgate, b_rgate, w_igate, b_igate, lru_lambda, w_out_a, sgu_ln_g, sgu_ln_b, sgu_w_s, sgu_b_s, w_out_b, w_out, norm_mlp_g, w_up, w_down, norm_final_g, loss_target, m_norm_mix_g, m_w_in, m_conv_w, m_conv_b, m_w_rgate, m_b_rgate, m_w_igate, m_b_igate, m_lru_lambda, m_w_out_a, m_sgu_ln_g, m_sgu_ln_b, m_sgu_w_s, m_sgu_b_s, m_w_out_b, m_w_out, m_norm_mlp_g, m_w_up, m_w_down, m_norm_final_g, v_norm_mix_g, v_w_in, v_conv_w, v_conv_b, v_w_rgate, v_b_rgate, v_w_igate, v_b_igate, v_lru_lambda, v_w_out_a, v_sgu_ln_g, v_sgu_ln_b, v_sgu_w_s, v_sgu_b_s, v_w_out_b, v_w_out, v_norm_mlp_g, v_w_up, v_w_down, v_norm_final_g):
    cx, cy, cc = _place()
    me = 4 * cx + 2 * cy + cc
    core = jnp.reshape(cc, (1,)).astype(jnp.int32)
    xs = x[0]
    tgt = loss_target[0]

    gate_shard = jnp.stack([w_rgate[0], w_igate[0]]).astype(BF16).reshape(2 * HEADS * 32, HEAD_DIM)
    vec_shard = jnp.concatenate([conv_w[0], b_rgate[0], b_igate[0]], axis=1)
    vec_shard = jnp.pad(vec_shard, ((0, 4), (0, 256 - vec_shard.shape[1])))
    shards = [w_in[0].astype(BF16), w_out_a[0].astype(BF16), w_out_b[0].astype(BF16), w_out[0].astype(BF16),
              w_up[0].astype(BF16), w_down[0].astype(BF16), gate_shard, vec_shard]
    (z, n1_t, w_in_g), (gate_g, vec_g) = _in_proj(xs, norm_mix_g, shards[0], _slot_order(cx, cy, cc),
                                                comm=_gather_plan(shards[6:8]))
    gates = gate_g.reshape(N_SLOT, 2, HEADS, 32, HEAD_DIM).transpose(1, 2, 0, 3, 4).reshape(2, HEADS, HEAD_DIM, HEAD_DIM)
    w_r_f, w_i_f = gates[0], gates[1]
    conv_w_f = vec_g[:, 0:4, 0:128].transpose(1, 0, 2).reshape(CONV_K, D)
    b_r_f = vec_g[:, 0:4, 128:160].transpose(1, 0, 2).reshape(1, D)
    b_i_f = vec_g[:, 0:4, 160:192].transpose(1, 0, 2).reshape(1, D)
    b_s_t = jnp.transpose(sgu_b_s[0])

    (ya, hs, xc, r_gate, i_gate), (w_oa_g, w_ob_g, w_out_g, w_up_g) = _branch_a_fwd(
        z, conv_w_f, conv_b, w_r_f, b_r_f, w_i_f, b_i_f, lru_lambda, comm=_gather_plan(shards[1:5]))
    w_oa_f = w_oa_g.reshape(D, D)
    w_ob_f = w_ob_g.reshape(D, D)
    w_out_f = w_out_g.reshape(D, D)
    (yb, pa, pb, merged, h1), (w_down_g,) = _branch_b_merge_out(
        ya, z, xs, sgu_ln_g, sgu_ln_b, sgu_w_s[0], b_s_t, w_oa_f, w_ob_f, w_out_f,
        comm=_gather_plan(shards[5:6], middle_at=4))
    w_down_f = w_down_g.reshape(N_SLOT * FF_COLS, D)
    r_act, act_t, n2_t, dh2, dh2b, loss_acc, d_gfin = _mlp_fwd(h1, norm_mlp_g, w_up_g, w_down_f,
                                                               norm_final_g.reshape(1, D), tgt)

    def pair(names, grads, recv):
        return [_pair_sum("pair_sum_" + nm, g, r, core) for nm, g, r in zip(names, grads, recv)]

    (p_down,), _ = _wgrad_paired("wgrad_down", act_t, dh2b, core, by_rows=True)
    (df, dh1, d_gmlp), (got_down,) = _mlp_bwd(dh2, dh2b, r_act, w_down_f, w_up_g, h1, norm_mlp_g,
                                              comm=_chips_plan([p_down]))
    (p_up,), _ = _wgrad_paired("wgrad_up", n2_t, df, core)
    g_out = _wgrad("wgrad_out", merged, dh1).reshape(N_SLOT, D // N_SLOT, D)
    (dz, dpa, dpb, dya, dyb), (r_out,) = _merge_bwd(
        dh1, z, pa, pb, w_out_f, w_oa_f, w_ob_f, comm=_sibling_plan([g_out]))
    (p_out,) = pair(["out"], [g_out], [r_out])
    g_oa = _wgrad("wgrad_out_a", ya, dpa).reshape(N_SLOT, D // N_SLOT, D)
    g_ob = _wgrad("wgrad_out_b", yb, dpb).reshape(N_SLOT, D // N_SLOT, D)
    (dz, d_ws, d_bs, d_ln), (got_out, r_oa, r_ob) = _branch_b_bwd(
        dz, dyb, z, sgu_ln_g, sgu_ln_b, sgu_w_s[0], b_s_t,
        comm=_join(_chips_plan([p_out]), _sibling_plan([g_oa, g_ob])))
    p_oa, p_ob = pair(["out_a", "out_b"], [g_oa, g_ob], [r_oa, r_ob])
    (dz, d_vec, d_wr, d_wi), (got_up, got_oa, got_ob) = _branch_a_bwd(
        dz, dya, z, hs, xc, r_gate, i_gate, conv_w_f, w_r_f, w_i_f, lru_lambda,
        comm=_chips_plan([p_up, p_oa, p_ob]))
    g_gate = jnp.stack([d_wr, d_wi]).reshape(2, HEADS, N_SLOT, 32, HEAD_DIM).transpose(2, 0, 1, 3, 4)
    g_gate = g_gate.reshape(N_SLOT, 2 * HEADS * 32, HEAD_DIM).astype(BF16)

    d_bs_row = jnp.pad(d_bs[:, :, 0].reshape(1, GROUPS * CHUNK), ((0, 0), (0, D - GROUPS * CHUNK)))
    vecs = jnp.concatenate([d_vec, jnp.concatenate([d_ln[0:2], d_gmlp, d_gfin, d_bs_row, jnp.zeros((3, D), F32)])])
    d_ws2 = d_ws.reshape(GROUPS * CHUNK, CHUNK)
    (p_in,), (r_gate, r_vecs, r_ws) = _wgrad_paired("wgrad_in", n1_t, dz, core,
                                                    comm=_sibling_plan([g_gate], [vecs, d_ws2]))
    (p_gate,) = pair(["gate"], [g_gate], [r_gate])
    vecs_chip = _add2("pair_sum_vecs", vecs, r_vecs)
    ws_chip = _add2("pair_sum_ws", d_ws2, r_ws)
    (dx, d_gmix), (got_in, got_gate, got_vecs, got_ws) = _in_bwd(
        dz, w_in_g, xs, dh1, norm_mix_g, comm=_chips_plan([p_in, p_gate], [vecs_chip, ws_chip]))
    vecs_sum = _sum_terms("sum_vecs", got_vecs)
    last = jnp.concatenate([d_gmix, jnp.pad(loss_acc[0:1], ((0, 0), (0, D - 128))), jnp.zeros((6, D), F32)])
    (last_all,) = _run_plan("exchange_last", _exchange_plan(last))
    last_sum = _sum_terms("sum_last", last_all)
    loss = last_sum[1, 0]
    got = [got_in, got_oa, got_ob, got_out, got_up, got_down, got_gate]

    def step(nm, terms, w, m, v, rows, cols):
        g, d, mn, vn = _adamw("adamw_" + nm, terms.reshape(4, rows, cols), w.reshape(rows, cols),
                              m.reshape(rows, cols), v.reshape(rows, cols))
        return [a.reshape(w.shape) for a in (g, d, mn, vn)]

    o_in = step("in", got[0], w_in, m_w_in, v_w_in, D, W_IN_COLS)
    o_oa = step("out_a", got[1], w_out_a, m_w_out_a, v_w_out_a, D // N_SLOT, D)
    o_ob = step("out_b", got[2], w_out_b, m_w_out_b, v_w_out_b, D // N_SLOT, D)
    o_out = step("out", got[3], w_out, m_w_out, v_w_out, D // N_SLOT, D)
    o_up = step("up", got[4], w_up, m_w_up, v_w_up, D, FF_COLS)
    o_down = step("down", got[5], w_down, m_w_down, v_w_down, FF_COLS, D)
    gate_w = jnp.stack([w_rgate[0], w_igate[0]]).reshape(2 * HEADS * 32, HEAD_DIM)
    gate_m = jnp.stack([m_w_rgate[0], m_w_igate[0]]).reshape(2 * HEADS * 32, HEAD_DIM)
    gate_v = jnp.stack([v_w_rgate[0], v_w_igate[0]]).reshape(2 * HEADS * 32, HEAD_DIM)
    o_gate = _adamw("adamw_gate", got[6], gate_w, gate_m, gate_v)
    o_gate = [a.reshape(2, 1, HEADS, 32, HEAD_DIM) for a in o_gate]
    o_wr = [a[0] for a in o_gate]
    o_wi = [a[1] for a in o_gate]

    def own(full, width):
        return lax.dynamic_slice_in_dim(full, me * width, width, axis=1)

    small_g = {
        "norm_mix_g": last_sum[0:1], "conv_w": own(vecs_sum[0:4], 128), "conv_b": vecs_sum[4:5],
        "b_rgate": own(vecs_sum[5:6].reshape(HEADS, HEAD_DIM), 32),
        "b_igate": own(vecs_sum[6:7].reshape(HEADS, HEAD_DIM), 32),
        "lru_lambda": vecs_sum[7:8], "sgu_ln_g": vecs_sum[8:9], "sgu_ln_b": vecs_sum[9:10],
        "norm_mlp_g": vecs_sum[10:11], "norm_final_g": vecs_sum[11:12],
        "sgu_b_s": vecs_sum[12, 0:GROUPS * CHUNK].reshape(GROUPS, CHUNK),
    }
    small_w = {"norm_mix_g": (norm_mix_g, m_norm_mix_g, v_norm_mix_g), "conv_w": (conv_w, m_conv_w, v_conv_w),
               "conv_b": (conv_b, m_conv_b, v_conv_b), "b_rgate": (b_rgate, m_b_rgate, v_b_rgate),
               "b_igate": (b_igate, m_b_igate, v_b_igate), "lru_lambda": (lru_lambda, m_lru_lambda, v_lru_lambda),
               "sgu_ln_g": (sgu_ln_g, m_sgu_ln_g, v_sgu_ln_g), "sgu_ln_b": (sgu_ln_b, m_sgu_ln_b, v_sgu_ln_b),
               "norm_mlp_g": (norm_mlp_g, m_norm_mlp_g, v_norm_mlp_g),
               "norm_final_g": (norm_final_g, m_norm_final_g, v_norm_final_g),
               "sgu_b_s": (sgu_b_s, m_sgu_b_s, v_sgu_b_s), "sgu_w_s": (sgu_w_s, m_sgu_w_s, v_sgu_w_s)}
    order = list(small_g)
    as2d = lambda k, a: a.reshape(small_g[k].shape)
    upd = _adamw_many("adamw_small", [small_g[k] for k in order], *[[as2d(k, small_w[k][q]) for k in order]
                                                                     for q in range(3)])
    o_small = {k: [a.reshape(small_w[k][0].shape) for a in (small_g[k],) + u] for k, u in zip(order, upd)}
    ws3 = [a[0].reshape(GROUPS * CHUNK, CHUNK) for a in small_w.pop("sgu_w_s")]
    o_small["sgu_w_s"] = [a.reshape(sgu_w_s.shape) for a in _adamw("adamw_ws", got_ws, *ws3)]

    per_weight = {"norm_mix_g": o_small["norm_mix_g"], "w_in": o_in, "conv_w": o_small["conv_w"],
                  "conv_b": o_small["conv_b"], "w_rgate": o_wr, "b_rgate": o_small["b_rgate"], "w_igate": o_wi,
                  "b_igate": o_small["b_igate"], "lru_lambda": o_small["lru_lambda"], "w_out_a": o_oa,
                  "sgu_ln_g": o_small["sgu_ln_g"], "sgu_ln_b": o_small["sgu_ln_b"], "sgu_w_s": o_small["sgu_w_s"],
                  "sgu_b_s": o_small["sgu_b_s"], "w_out_b": o_ob, "w_out": o_out, "norm_mlp_g": o_small["norm_mlp_g"],
                  "w_up": o_up, "w_down": o_down, "norm_final_g": o_small["norm_final_g"]}
    names_w = list(per_weight)
    return (loss, dx[None], *[per_weight[k][0] for k in names_w], *[per_weight[k][1] for k in names_w],
            *[per_weight[k][2] for k in names_w], *[per_weight[k][3] for k in names_w])
```

```python
import jax
import jax.numpy as jnp
from jax import lax
from jax.experimental import pallas as pl
from jax.experimental.pallas import tpu as pltpu

F32 = jnp.float32
BF16 = jnp.bfloat16
SDS = jax.ShapeDtypeStruct
MESH = pl.DeviceIdType.MESH
ANY = pl.BlockSpec(memory_space=pl.ANY)

D = 1024
N_SLOT = 8
W_IN_COLS = 768
FF_COLS = 512
HEADS, HEAD_DIM = 4, 256
GROUPS, GROUP_DIM = 4, 256
CHUNK = 128
CONV_K = 4
NORM_EPS = 1e-6
LN_EPS = 1e-5
LRU_C = 8.0
ADAM_LR, ADAM_B1, ADAM_B2, ADAM_EPS, ADAM_WD, ADAM_STEP = 0.001, 0.9, 0.999, 1e-08, 0.01, 10

TM_ROWS = 1024
TM_MERGE = 512
T_BRANCH_A = 512
T_BRANCH_B = 256
MiB = 1024 * 1024
SMALL_OPERAND = 16 * 1024

_GELU_C = 0.7978845608028654
_GELU_A = 0.044715


def _small_in_hbm(a):
    return pltpu.with_memory_space_constraint(a, pltpu.HBM) if a.size <= SMALL_OPERAND else a


def _cparams(sem, vmem_mib):
    return pltpu.CompilerParams(dimension_semantics=sem, vmem_limit_bytes=vmem_mib * MiB)


def _gelu(x):
    t = jnp.tanh(_GELU_C * (x + _GELU_A * x * x * x))
    return 0.5 * x * (1.0 + t)


def _gelu_and_grad(x):
    x2 = x * x
    t = jnp.tanh(_GELU_C * x * (1.0 + _GELU_A * x2))
    g = 0.5 * x * (1.0 + t)
    dg = 0.5 * (1.0 + t) + 0.5 * x * (1.0 - t * t) * _GELU_C * (1.0 + 3.0 * _GELU_A * x2)
    return g, dg


def _softplus(x):
    return jnp.maximum(x, 0.0) + jnp.log1p(jnp.exp(-jnp.abs(x)))


def _dot(a, b):
    return jnp.dot(a, b, preferred_element_type=F32)


def _dot_nt(a, b):
    return lax.dot_general(a, b, (((1,), (1,)), ((), ())), preferred_element_type=F32)


def _dot_tn(a, b):
    return lax.dot_general(a, b, (((0,), (0,)), ((), ())), preferred_element_type=F32)


def _rows_shifted(prev8, cur, k):
    ext = jnp.concatenate([prev8, cur], axis=0)
    return pltpu.roll(ext, k, 0)[8:]


def _rows_advanced(cur, next8, k):
    t = cur.shape[0]
    ext = jnp.concatenate([cur, next8], axis=0)
    return pltpu.roll(ext, t + 8 - k, 0)[:t]


def _first_second(x, y, c):
    ny, nx, far = _other_chips(x, y)
    pick = lambda a, b: a * (1 - c) + b * c
    first = tuple(pick(a, b) for a, b in zip(ny, nx))
    second = tuple(pick(b, a) for a, b in zip(ny, nx))
    return first, second, far


def _slot_order(x, y, c):
    chip = 2 * x + y
    first, second, far = _first_second(x, y, c)
    order = [2 * chip + c, 2 * chip + 1 - c, 2 * first[2] + c, 2 * second[2] + 1 - c, 2 * second[2] + c,
             2 * first[2] + 1 - c, 2 * far[2] + c, 2 * far[2] + 1 - c]
    return jnp.stack(order).astype(jnp.int32)


def _in_proj(x, g_mix, w_in_own, order, comm=None):
    s = x.shape[0]
    tm = min(TM_ROWS, s)
    ni = s // tm

    def body(order_ref, x_ref, g_ref, own_ref, z_ref, nt_ref, wg_ref, n_s, w_s, send_sems, recv_sems, local_sems):
        j, i = pl.program_id(0), pl.program_id(1)
        px, py, c = _place()
        chip = 2 * px + py
        me = 2 * chip + c
        sib = (px, py, 1 - c)
        chips = _other_chips(px, py)

        def rc(k, src, blk, to):
            return pltpu.make_async_remote_copy(src_ref=src, dst_ref=w_s.at[blk], send_sem=send_sems.at[k],
                                                recv_sem=recv_sems.at[k], device_id=to, device_id_type=MESH)

        del chips
        first, second, far = _first_second(px, py, c)
        blocks = [2 * first[2] + c, 2 * second[2] + c, 2 * far[2] + c]
        own_in = pltpu.make_async_copy(own_ref, w_s.at[me], local_sems.at[0])
        to_first = rc(1, own_ref, me, (first[0], first[1], c))
        to_second = rc(2, own_ref, me, (second[0], second[1], c))
        relay = rc(3, w_s.at[blocks[0]], blocks[0], (second[0], second[1], c))
        sends = [rc(0, own_ref, me, sib), to_first, to_second, relay]
        passed = [rc(4 + q, w_s.at[blk], blk, sib) for q, blk in enumerate(blocks)]
        keep = pltpu.make_async_copy(w_s, wg_ref, local_sems.at[1])

        @pl.when((i == 0) & (j == 0))
        def _():
            own_in.start()
            sends[0].start()
            to_first.start()
            own_in.wait()

        @pl.when((i == 0) & (j == 1))
        def _():
            rc(0, own_ref, 2 * chip + 1 - c, sib).wait_recv()

        for q, blk in enumerate(blocks):
            @pl.when((i == 0) & (j == 2 + 2 * q))
            def _():
                rc(1 + q, own_ref, blk, sib).wait_recv()
                passed[q].start()
                if q == 0:
                    to_second.start()
                    relay.start()

            @pl.when((i == 0) & (j == 3 + 2 * q))
            def _():
                rc(4 + q, own_ref, order_ref[j], sib).wait_recv()

        rows = pl.ds(pl.multiple_of(i * tm, tm), tm)

        @pl.when(j == 0)
        def _():
            xv = x_ref[...]
            rstd = lax.rsqrt(jnp.mean(xv * xv, axis=-1, keepdims=True) + NORM_EPS)
            nb = (xv * rstd * g_ref[...]).astype(BF16)
            n_s[rows, :] = nb
            nt_ref[...] = nb.T

        z_ref[...] = _dot(n_s[rows, :], w_s[order_ref[j]]).astype(BF16)

        @pl.when((i == 0) & (j == N_SLOT - 1))
        def _():
            keep.start()

        @pl.when((i == ni - 1) & (j == N_SLOT - 1))
        def _():
            for cp in sends + passed:
                cp.wait_send()
            keep.wait()

    first_pass = lambda j, i, o: (jnp.where(j == 0, i, ni - 1), 0)
    (z, n1, w_in_g), extra = _call(
        body, name="in_proj", grid=(N_SLOT, ni), prefetch=(order,),
        in_specs=[pl.BlockSpec((tm, D), first_pass),
                  pl.BlockSpec((1, D), lambda j, i, o: (0, 0)), ANY],
        out_specs=[pl.BlockSpec((tm, W_IN_COLS), lambda j, i, o: (i, o[j])),
                   pl.BlockSpec((D, tm), lambda j, i, o: (0, jnp.where(j == 0, i, ni - 1))), ANY],
        out_shape=[SDS((s, N_SLOT * W_IN_COLS), BF16), SDS((D, s), BF16), SDS((N_SLOT, D, W_IN_COLS), BF16)],
        scratch_shapes=[pltpu.VMEM((s, D), BF16), pltpu.VMEM((N_SLOT, D, W_IN_COLS), BF16),
                        pltpu.SemaphoreType.DMA((7,)), pltpu.SemaphoreType.DMA((7,)), pltpu.SemaphoreType.DMA((2,))],
        params=_cparams(("arbitrary", "arbitrary"), 56), args=(x, g_mix, w_in_own), comm=comm)
    return (z, n1, w_in_g), extra


def _decay(r, sp_lam):
    log_a = (-LRU_C) * r * sp_lam
    a = jnp.exp(log_a)
    return a, jnp.sqrt(-jnp.tanh(log_a) * (a * a + 1.0))


def _lru_gates(xc, xcb, wr_ref, br, wi_ref, bi, sp_lam, a_s, b_s, r_ref, i_ref):
    for h in range(HEADS):
        sl = slice(h * HEAD_DIM, (h + 1) * HEAD_DIM)
        r = jax.nn.sigmoid(_dot(xcb[:, sl], wr_ref[h]) + br[:, sl])
        ig = jax.nn.sigmoid(_dot(xcb[:, sl], wi_ref[h]) + bi[:, sl])
        a, mult = _decay(r, sp_lam[:, sl])
        a_s[:, sl] = a
        b_s[:, sl] = xc[:, sl] * ig * mult
        r_ref[:, sl] = r.astype(BF16)
        i_ref[:, sl] = ig.astype(BF16)


def _conv_fwd(xa, prev8, cw, cb):
    xc = cb + cw[0:1, :] * xa
    for k in range(1, CONV_K):
        xc = xc + cw[k:k + 1, :] * _rows_shifted(prev8, xa, k)
    return xc


def _branch_a_fwd(z, conv_w, conv_b, w_r, b_r, w_i, b_i, lam, comm=None):
    s = z.shape[0]
    ta = min(T_BRANCH_A, s)
    per16 = ta // 16

    def body(xa_ref, xp_ref, ga_ref, cw_ref, cb_ref, wr_ref, br_ref, wi_ref, bi_ref, lam_ref,
             ya_ref, hs_ref, xc_ref, r_ref, i_ref, a_s, b_s, h_s, carry_s):
        i = pl.program_id(0)

        @pl.when(i == 0)
        def _():
            carry_s[...] = jnp.zeros_like(carry_s)

        xa = xa_ref[...].astype(F32)
        prev8 = jnp.where(i > 0, xp_ref[...].astype(F32)[8:16], 0.0)
        xc = _conv_fwd(xa, prev8, cw_ref[...], cb_ref[...])
        xcb = xc.astype(BF16)
        xc_ref[...] = xcb
        sp_lam = _softplus(-lam_ref[...])
        _lru_gates(xc, xcb, wr_ref, br_ref[...], wi_ref, bi_ref[...], sp_lam, a_s, b_s, r_ref, i_ref)

        row = lax.broadcasted_iota(jnp.int32, (8, D), 0)

        def group(g, carry):
            off = pl.multiple_of(g * 8, 8)
            a8 = a_s[pl.ds(off, 8), :]
            b8 = b_s[pl.ds(off, 8), :]
            for d in (1, 2, 4):
                a_sh = jnp.where(row >= d, pltpu.roll(a8, d, 0), 1.0)
                b_sh = jnp.where(row >= d, pltpu.roll(b8, d, 0), 0.0)
                b8 = a8 * b_sh + b8
                a8 = a8 * a_sh
            h8 = b8 + a8 * carry
            h_s[pl.ds(off, 8), :] = h8
            return jnp.broadcast_to(h8[7:8, :], (8, D))

        carry_s[...] = lax.fori_loop(0, ta // 8, group, carry_s[...])
        hs = h_s[...]
        hs_ref[...] = hs.astype(BF16)
        ya_ref[...] = (hs * _gelu(ga_ref[...].astype(F32))).astype(BF16)

    vec = pl.BlockSpec((1, D), lambda i: (0, 0))
    gate = pl.BlockSpec((HEADS, HEAD_DIM, HEAD_DIM), lambda i: (0, 0, 0))
    return _call(
        body, name="branch_a_fwd", grid=(s // ta,),
        in_specs=[pl.BlockSpec((ta, D), lambda i: (i, 0)),
                  pl.BlockSpec((16, D), lambda i: (jnp.maximum(i * per16 - 1, 0), 0)),
                  pl.BlockSpec((ta, D), lambda i: (i, 1)),
                  pl.BlockSpec((CONV_K, D), lambda i: (0, 0)), vec, gate, vec, gate, vec, vec],
        out_specs=[pl.BlockSpec((ta, D), lambda i: (i, 0))] * 5,
        out_shape=[SDS((s, D), BF16)] * 5,
        scratch_shapes=[pltpu.VMEM((ta, D), F32), pltpu.VMEM((ta, D), F32), pltpu.VMEM((ta, D), F32),
                        pltpu.VMEM((8, D), F32)],
        params=_cparams(("arbitrary",), 40), args=(z, z, z, conv_w, conv_b, w_r, b_r, w_i, b_i, lam), comm=comm)


def _sgu_common(ub, vb, lg, lb, with_grad):
    if with_grad:
        u, du = _gelu_and_grad(ub)
        v, dv = _gelu_and_grad(vb)
    else:
        u, v, du, dv = _gelu(ub), _gelu(vb), None, None
    mu = jnp.mean(v, axis=-1, keepdims=True)
    vc = v - mu
    rstd = lax.rsqrt(jnp.mean(vc * vc, axis=-1, keepdims=True) + LN_EPS)
    vhat = vc * rstd
    vln = vhat * lg + lb
    return u, du, dv, rstd, vhat, vln


def _masked_ws(ws_ref):
    t = lax.broadcasted_iota(jnp.int32, (CHUNK, CHUNK), 0)
    c = lax.broadcasted_iota(jnp.int32, (CHUNK, CHUNK), 1)
    keep = c <= t
    return [jnp.where(keep, ws_ref[g], 0.0).astype(BF16) for g in range(GROUPS)]


def _branch_b_merge_out(ya, z, x, ln_g, ln_b, w_s, b_s_t, w_oa, w_ob, w_out, comm=None):
    s = x.shape[0]
    tm = min(TM_MERGE, s)

    def body(ub_ref, vb_ref, lg_ref, lb_ref, ws_ref, bs_ref, ya_ref, ma_ref, mb_ref, x_ref, woa_ref, wob_ref, wo_ref,
             yb_ref, pa_ref, pb_ref, mg_ref, h1_ref):
        u, _, _, _, _, vln = _sgu_common(ub_ref[...].astype(F32), vb_ref[...].astype(F32),
                                         lg_ref[...], lb_ref[...], False)
        vlnb = vln.astype(BF16)
        wm = _masked_ws(ws_ref)
        bs = bs_ref[...]
        for c in range(tm // CHUNK):
            rs = slice(c * CHUNK, (c + 1) * CHUNK)
            for g in range(GROUPS):
                cs = slice(g * GROUP_DIM, (g + 1) * GROUP_DIM)
                sp = _dot(wm[g], vlnb[rs, cs]) + bs[:, g:g + 1]
                yb_ref[rs, cs] = (u[rs, cs] * sp).astype(BF16)

        pa = _dot(ya_ref[...], woa_ref[...])
        pb = _dot(yb_ref[...], wob_ref[...])
        merged = (jax.nn.sigmoid(ma_ref[...].astype(F32)) * pa
                  + jax.nn.sigmoid(mb_ref[...].astype(F32)) * pb).astype(BF16)
        pa_ref[...] = pa.astype(BF16)
        pb_ref[...] = pb.astype(BF16)
        mg_ref[...] = merged
        h1_ref[...] = x_ref[...] + _dot(merged, wo_ref[...])

    row = pl.BlockSpec((tm, D), lambda i: (i, 0))
    col = lambda q: pl.BlockSpec((tm, D), lambda i: (i, q))
    vec = pl.BlockSpec((1, D), lambda i: (0, 0))
    wsp = pl.BlockSpec((D, D), lambda i: (0, 0))
    return _call(
        body, name="branch_b_merge_out", grid=(s // tm,),
        in_specs=[col(2), col(3), vec, vec, pl.BlockSpec((GROUPS, CHUNK, CHUNK), lambda i: (0, 0, 0)),
                  pl.BlockSpec((CHUNK, GROUPS), lambda i: (0, 0)), row, col(4), col(5), row, wsp, wsp, wsp],
        out_specs=[row, row, row, row, row],
        out_shape=[SDS((s, D), BF16)] * 4 + [SDS((s, D), F32)], scratch_shapes=[],
        params=_cparams(("arbitrary",), 56),
        args=(z, z, ln_g, ln_b, w_s, b_s_t, ya, z, z, x, w_oa, w_ob, w_out), comm=comm)


def _mlp_fwd(h1, g_mlp, w_up_g, w_down, g_fin, tgt):
    s = h1.shape[0]
    tm = min(TM_ROWS, s)
    nj = N_SLOT

    def body(h1_ref, gm_ref, wu_ref, wd_ref, gf_ref, t_ref, r_ref, at_ref, n2t_ref, dh2_ref, dh2b_ref, loss_ref,
             dgf_ref, n2_s, acc_s):
        i, j = pl.program_id(0), pl.program_id(1)

        @pl.when(j == 0)
        def _():
            hv = h1_ref[...]
            rstd = lax.rsqrt(jnp.mean(hv * hv, axis=-1, keepdims=True) + NORM_EPS)
            nb = (hv * rstd * gm_ref[...]).astype(BF16)
            n2_s[...] = nb
            n2t_ref[...] = nb.T
            acc_s[...] = jnp.zeros_like(acc_s)

        @pl.when((i == 0) & (j == 0))
        def _():
            loss_ref[...] = jnp.zeros_like(loss_ref)
            dgf_ref[...] = jnp.zeros_like(dgf_ref)

        r = jnp.maximum(_dot(n2_s[...], wu_ref[...]), 0.0)
        r_ref[...] = r.astype(BF16)
        act = (r * r).astype(BF16)
        at_ref[...] = act.T
        acc_s[...] += _dot(act, wd_ref[...])

        @pl.when(j == nj - 1)
        def _():
            h2 = h1_ref[...] + acc_s[...]
            rstd = lax.rsqrt(jnp.mean(h2 * h2, axis=-1, keepdims=True) + NORM_EPS)
            hh = h2 * rstd
            gf = gf_ref[...]
            e = hh * gf - t_ref[...]
            loss_ref[...] += jnp.sum(e * e) * (0.5 / D)
            dy = e * (1.0 / D)
            dgf_ref[...] += jnp.sum(dy * hh, axis=0, keepdims=True)
            dhh = dy * gf
            dh2 = rstd * (dhh - hh * jnp.mean(dhh * hh, axis=-1, keepdims=True))
            dh2_ref[...] = dh2
            dh2b_ref[...] = dh2.astype(BF16)

    row = pl.BlockSpec((tm, D), lambda i, j: (i, 0))
    vec = pl.BlockSpec((1, D), lambda i, j: (0, 0))
    return pl.pallas_call(
        body, name="mlp_fwd", grid=(s // tm, nj),
        in_specs=[row, vec, pl.BlockSpec((None, D, FF_COLS), lambda i, j: (j, 0, 0)),
                  pl.BlockSpec((FF_COLS, D), lambda i, j: (j, 0)), vec, row],
        out_specs=[pl.BlockSpec((tm, FF_COLS), lambda i, j: (i, j)), pl.BlockSpec((FF_COLS, tm), lambda i, j: (j, i)),
                   pl.BlockSpec((D, tm), lambda i, j: (0, i)), row, row, pl.BlockSpec((8, 128), lambda i, j: (0, 0)),
                   vec],
        out_shape=[SDS((s, nj * FF_COLS), BF16), SDS((nj * FF_COLS, s), BF16), SDS((D, s), BF16), SDS((s, D), F32),
                   SDS((s, D), BF16), SDS((8, 128), F32), SDS((1, D), F32)],
        scratch_shapes=[pltpu.VMEM((tm, D), BF16), pltpu.VMEM((tm, D), F32)],
        compiler_params=_cparams(("arbitrary", "arbitrary"), 56),
    )(h1, _small_in_hbm(g_mlp), w_up_g, w_down, _small_in_hbm(g_fin), tgt)


def _mlp_bwd(dh2, dh2b, r, w_down, w_up_g, h1, g_mlp, comm=None):
    s = h1.shape[0]
    tm = min(TM_ROWS, s)
    nj = N_SLOT

    def body(dh2_ref, dh2b_ref, r_ref, wd_ref, wu_ref, h1_ref, gm_ref, df_ref, dh1_ref, dgm_ref, acc_s):
        i, j = pl.program_id(0), pl.program_id(1)

        @pl.when(j == 0)
        def _():
            acc_s[...] = jnp.zeros_like(acc_s)

        @pl.when((i == 0) & (j == 0))
        def _():
            dgm_ref[...] = jnp.zeros_like(dgm_ref)

        d_act = _dot_nt(dh2b_ref[...], wd_ref[...])
        df = (d_act * (2.0 * r_ref[...].astype(F32))).astype(BF16)
        df_ref[...] = df
        acc_s[...] += _dot_nt(df, wu_ref[...])

        @pl.when(j == nj - 1)
        def _():
            hv = h1_ref[...]
            rstd = lax.rsqrt(jnp.mean(hv * hv, axis=-1, keepdims=True) + NORM_EPS)
            hh = hv * rstd
            dn2 = acc_s[...]
            dgm_ref[...] += jnp.sum(dn2 * hh, axis=0, keepdims=True)
            dhat = dn2 * gm_ref[...]
            dh1_ref[...] = dh2_ref[...] + rstd * (dhat - hh * jnp.mean(dhat * hh, axis=-1, keepdims=True))

    row = pl.BlockSpec((tm, D), lambda i, j: (i, 0))
    vec = pl.BlockSpec((1, D), lambda i, j: (0, 0))
    ffb = pl.BlockSpec((tm, FF_COLS), lambda i, j: (i, j))
    return _call(
        body, name="mlp_bwd", grid=(s // tm, nj),
        in_specs=[row, row, ffb, pl.BlockSpec((FF_COLS, D), lambda i, j: (j, 0)),
                  pl.BlockSpec((None, D, FF_COLS), lambda i, j: (j, 0, 0)), row, vec],
        out_specs=[ffb, row, vec],
        out_shape=[SDS((s, nj * FF_COLS), BF16), SDS((s, D), F32), SDS((1, D), F32)],
        scratch_shapes=[pltpu.VMEM((tm, D), F32)],
        params=_cparams(("arbitrary", "arbitrary"), 56), args=(dh2, dh2b, r, w_down, w_up_g, h1, g_mlp), comm=comm)


def _merge_bwd(dh1, z, pa, pb, w_out, w_oa, w_ob, comm=None):
    s = dh1.shape[0]
    tm = min(TM_MERGE, s)

    def body(dh1_ref, ma_ref, mb_ref, pa_ref, pb_ref, wo_ref, woa_ref, wob_ref,
             dz_ref, dpa_ref, dpb_ref, dya_ref, dyb_ref):
        dm = _dot_nt(dh1_ref[...].astype(BF16), wo_ref[...])
        sa = jax.nn.sigmoid(ma_ref[...].astype(F32))
        sb = jax.nn.sigmoid(mb_ref[...].astype(F32))
        dpa = (dm * sa).astype(BF16)
        dpb = (dm * sb).astype(BF16)
        dz_ref[:, 0:D] = (dm * pa_ref[...].astype(F32) * sa * (1.0 - sa)).astype(BF16)
        dz_ref[:, D:2 * D] = (dm * pb_ref[...].astype(F32) * sb * (1.0 - sb)).astype(BF16)
        dpa_ref[...] = dpa
        dpb_ref[...] = dpb
        dya_ref[...] = _dot_nt(dpa, woa_ref[...]).astype(BF16)
        dyb_ref[...] = _dot_nt(dpb, wob_ref[...]).astype(BF16)

    row = pl.BlockSpec((tm, D), lambda i: (i, 0))
    wsp = pl.BlockSpec((D, D), lambda i: (0, 0))
    return _call(
        body, name="merge_bwd", grid=(s // tm,),
        in_specs=[row, pl.BlockSpec((tm, D), lambda i: (i, 4)), pl.BlockSpec((tm, D), lambda i: (i, 5)),
                  row, row, wsp, wsp, wsp],
        out_specs=[pl.BlockSpec((tm, 2 * D), lambda i: (i, 2)), row, row, row, row],
        out_shape=[SDS((s, 6 * D), BF16)] + [SDS((s, D), BF16)] * 4, scratch_shapes=[],
        params=_cparams(("arbitrary",), 48), args=(dh1, z, z, pa, pb, w_out, w_oa, w_ob), comm=comm)


def _branch_b_bwd(dz, dyb, z, ln_g, ln_b, w_s, b_s_t, comm=None):
    s = z.shape[0]
    tb = min(T_BRANCH_B, s)

    def body(dz_in, dyb_ref, ub_ref, vb_ref, lg_ref, lb_ref, ws_ref, bs_ref,
             dz_ref, dws_ref, dbs_ref, dln_ref, du_s, dvln_s):
        del dz_in

        @pl.when(pl.program_id(0) == 0)
        def _():
            dws_ref[...] = jnp.zeros_like(dws_ref)
            dbs_ref[...] = jnp.zeros_like(dbs_ref)
            dln_ref[...] = jnp.zeros_like(dln_ref)

        lg = lg_ref[...]
        u, du, dv, rstd, vhat, vln = _sgu_common(ub_ref[...].astype(F32), vb_ref[...].astype(F32),
                                                 lg, lb_ref[...], True)
        vlnb = vln.astype(BF16)
        dyb_v = dyb_ref[...].astype(F32)
        wm = _masked_ws(ws_ref)
        keep = (lax.broadcasted_iota(jnp.int32, (CHUNK, CHUNK), 1)
                <= lax.broadcasted_iota(jnp.int32, (CHUNK, CHUNK), 0))
        bs = bs_ref[...]
        for c in range(tb // CHUNK):
            rs = slice(c * CHUNK, (c + 1) * CHUNK)
            for g in range(GROUPS):
                cs = slice(g * GROUP_DIM, (g + 1) * GROUP_DIM)
                v_blk = vlnb[rs, cs]
                sp = _dot(wm[g], v_blk) + bs[:, g:g + 1]
                d_sp = dyb_v[rs, cs] * u[rs, cs]
                d_spb = d_sp.astype(BF16)
                du_s[rs, cs] = dyb_v[rs, cs] * sp
                dvln_s[rs, cs] = _dot_tn(wm[g], d_spb)
                dws_ref[g] += jnp.where(keep, _dot_nt(d_spb, v_blk), 0.0)
                dbs_ref[g] += jnp.broadcast_to(jnp.sum(d_sp, axis=-1, keepdims=True), (CHUNK, CHUNK))
        dvln = dvln_s[...]
        dln_ref[0:1, :] += jnp.sum(dvln * vhat, axis=0, keepdims=True)
        dln_ref[1:2, :] += jnp.sum(dvln, axis=0, keepdims=True)
        dvh = dvln * lg
        d_v = rstd * (dvh - jnp.mean(dvh, axis=-1, keepdims=True)
                      - vhat * jnp.mean(dvh * vhat, axis=-1, keepdims=True))
        dz_ref[:, 0:D] = (du_s[...] * du).astype(BF16)
        dz_ref[:, D:2 * D] = (d_v * dv).astype(BF16)

    vec = pl.BlockSpec((1, D), lambda i: (0, 0))
    sq = pl.BlockSpec((GROUPS, CHUNK, CHUNK), lambda i: (0, 0, 0))
    return _call(
        body, name="branch_b_bwd", grid=(s // tb,),
        in_specs=[ANY, pl.BlockSpec((tb, D), lambda i: (i, 0)),
                  pl.BlockSpec((tb, D), lambda i: (i, 2)), pl.BlockSpec((tb, D), lambda i: (i, 3)), vec, vec, sq,
                  pl.BlockSpec((CHUNK, GROUPS), lambda i: (0, 0))],
        out_specs=[pl.BlockSpec((tb, 2 * D), lambda i: (i, 1)), sq, sq, pl.BlockSpec((8, D), lambda i: (0, 0))],
        out_shape=[SDS(dz.shape, BF16), SDS((GROUPS, CHUNK, CHUNK), F32), SDS((GROUPS, CHUNK, CHUNK), F32),
                   SDS((8, D), F32)],
        scratch_shapes=[pltpu.VMEM((tb, D), F32), pltpu.VMEM((tb, D), F32)], aliases={0: 0},
        params=_cparams(("arbitrary",), 40), args=(dz, dyb, z, z, ln_g, ln_b, w_s, b_s_t), comm=comm)


def _branch_a_bwd(dz, dya, z, hs, xc, r, ig, conv_w, w_r, w_i, lam, comm=None):
    s = z.shape[0]
    ta = min(T_BRANCH_A, s)
    nb = s // ta
    per16 = ta // 16

    def body(dz_in, dya_ref, xa_ref, ga_ref, hs_ref, hp_ref, xc_ref, r_ref, i_ref, cw_ref, wr_ref, wi_ref,
             lam_ref, dz_ref, vec_ref, dwr_ref, dwi_ref, a_s, b_s, h_s, dcar_s, acar_s, dxc_s):
        del dz_in
        i = pl.program_id(0)
        blk = nb - 1 - i

        @pl.when(i == 0)
        def _():
            dcar_s[...] = jnp.zeros_like(dcar_s)
            acar_s[...] = jnp.zeros_like(acar_s)
            dxc_s[...] = jnp.zeros_like(dxc_s)
            vec_ref[...] = jnp.zeros_like(vec_ref)
            dwr_ref[...] = jnp.zeros_like(dwr_ref)
            dwi_ref[...] = jnp.zeros_like(dwi_ref)

        cw = cw_ref[...]
        lam_v = lam_ref[...]
        xa = xa_ref[...].astype(F32)
        xcb = xc_ref[...]
        xc = xcb.astype(F32)
        sp_lam = _softplus(-lam_v)
        r_v, i_v = r_ref[...].astype(F32), i_ref[...].astype(F32)
        a_v, m_v = _decay(r_v, sp_lam)

        hs_v = hs_ref[...].astype(F32)
        hprev8 = jnp.where(blk > 0, hp_ref[...].astype(F32)[8:16], 0.0)
        h_m1 = _rows_shifted(hprev8, hs_v, 1)
        gg, dgg = _gelu_and_grad(ga_ref[...].astype(F32))
        dya_v = dya_ref[...].astype(F32)
        dz_ref[:, D:2 * D] = (dya_v * hs_v * dgg).astype(BF16)

        a_s[...] = _rows_advanced(a_v, acar_s[...], 1)
        b_s[...] = dya_v * gg

        row = lax.broadcasted_iota(jnp.int32, (8, D), 0)
        ng = ta // 8

        def group(gi, carry):
            off = pl.multiple_of((ng - 1 - gi) * 8, 8)
            c8 = a_s[pl.ds(off, 8), :]
            d8 = b_s[pl.ds(off, 8), :]
            for d in (1, 2, 4):
                c_sh = jnp.where(row < 8 - d, pltpu.roll(c8, 8 - d, 0), 1.0)
                d_sh = jnp.where(row < 8 - d, pltpu.roll(d8, 8 - d, 0), 0.0)
                d8 = c8 * d_sh + d8
                c8 = c8 * c_sh
            dh8 = d8 + c8 * carry
            h_s[pl.ds(off, 8), :] = dh8
            return jnp.broadcast_to(dh8[0:1, :], (8, D))

        dcar_s[...] = lax.fori_loop(0, ng, group, dcar_s[...])
        acar_s[...] = jnp.broadcast_to(a_v[0:1, :], (8, D))

        dbx = h_s[...]
        d_mult = dbx * xc * i_v
        d_loga = dbx * h_m1 * a_v - d_mult * (a_v * a_v) / m_v
        d_pr = d_loga * ((-LRU_C) * sp_lam) * r_v * (1.0 - r_v)
        d_pi = dbx * xc * m_v * i_v * (1.0 - i_v)
        vec_ref[7:8, :] += jnp.sum(d_loga * r_v, axis=0, keepdims=True) * (LRU_C * jax.nn.sigmoid(-lam_v))
        vec_ref[5:6, :] += jnp.sum(d_pr, axis=0, keepdims=True)
        vec_ref[6:7, :] += jnp.sum(d_pi, axis=0, keepdims=True)
        d_prb = d_pr.astype(BF16)
        d_pib = d_pi.astype(BF16)
        h_s[...] = dbx * i_v * m_v
        for h in range(HEADS):
            sl = slice(h * HEAD_DIM, (h + 1) * HEAD_DIM)
            h_s[:, sl] += _dot_nt(d_prb[:, sl], wr_ref[h]) + _dot_nt(d_pib[:, sl], wi_ref[h])
            dwr_ref[h] += _dot_tn(xcb[:, sl], d_prb[:, sl])
            dwi_ref[h] += _dot_tn(xcb[:, sl], d_pib[:, sl])
        d_xc = h_s[...]
        vec_ref[4:5, :] += jnp.sum(d_xc, axis=0, keepdims=True)
        vec_ref[0:1, :] += jnp.sum(d_xc * xa, axis=0, keepdims=True)
        d_xa = cw[0:1, :] * d_xc
        nxt = dxc_s[...]
        for k in range(1, CONV_K):
            ahead = _rows_advanced(d_xc, nxt, k)
            vec_ref[k:k + 1, :] += jnp.sum(ahead * xa, axis=0, keepdims=True)
            d_xa = d_xa + cw[k:k + 1, :] * ahead
        dz_ref[:, 0:D] = d_xa.astype(BF16)
        dxc_s[...] = d_xc[0:8, :]

    vec = pl.BlockSpec((1, D), lambda i: (0, 0))
    gate = pl.BlockSpec((HEADS, HEAD_DIM, HEAD_DIM), lambda i: (0, 0, 0))
    cur = lambda c: pl.BlockSpec((ta, D), lambda i: (nb - 1 - i, c))
    before = lambda c: pl.BlockSpec((16, D), lambda i: (jnp.maximum((nb - 1 - i) * per16 - 1, 0), c))
    return _call(
        body, name="branch_a_bwd", grid=(nb,),
        in_specs=[ANY, cur(0), cur(0), cur(1), cur(0), before(0), cur(0), cur(0), cur(0),
                  pl.BlockSpec((CONV_K, D), lambda i: (0, 0)), gate, gate, vec],
        out_specs=[pl.BlockSpec((ta, 2 * D), lambda i: (nb - 1 - i, 0)), pl.BlockSpec((8, D), lambda i: (0, 0)),
                   gate, gate],
        out_shape=[SDS(dz.shape, BF16), SDS((8, D), F32), SDS((HEADS, HEAD_DIM, HEAD_DIM), F32),
                   SDS((HEADS, HEAD_DIM, HEAD_DIM), F32)],
        scratch_shapes=[pltpu.VMEM((ta, D), F32)] * 3 + [pltpu.VMEM((8, D), F32)] * 3, aliases={0: 0},
        params=_cparams(("arbitrary",), 48),
        args=(dz, dya, z, z, hs, hs, xc, r, ig, conv_w, w_r, w_i, lam), comm=comm)


def _in_bwd(dz, w_in_g, x, dh1, g_mix, updates=(), comm=None):
    s = x.shape[0]
    tm = min(TM_ROWS, s)
    nj = N_SLOT
    nu = len(updates)
    steps = (s // tm) * nj

    def body(dz_ref, w_ref, x_ref, dh1_ref, g_ref, *rest):
        upd_in, (dx_ref, dg_ref), upd_out, acc_s = rest[:4 * nu], rest[4 * nu:4 * nu + 2], rest[4 * nu + 2:-1], rest[-1]
        i, j = pl.program_id(0), pl.program_id(1)

        for u in range(nu):
            t_ref, wu_ref, m_ref, v_ref = upd_in[4 * u:4 * u + 4]
            go_ref, d_ref, mo_ref, vo_ref = upd_out[4 * u:4 * u + 4]
            g = t_ref[0].astype(F32)
            for q in range(1, 4):
                g = g + t_ref[q].astype(F32)
            go_ref[...] = g
            d_ref[...], mo_ref[...], vo_ref[...] = _adam_update(g, wu_ref[...], m_ref[...], v_ref[...])

        @pl.when(j == 0)
        def _():
            acc_s[...] = jnp.zeros_like(acc_s)

        @pl.when((i == 0) & (j == 0))
        def _():
            dg_ref[...] = jnp.zeros_like(dg_ref)

        acc_s[...] += _dot_nt(dz_ref[...], w_ref[...])

        @pl.when(j == nj - 1)
        def _():
            xv = x_ref[...]
            rstd = lax.rsqrt(jnp.mean(xv * xv, axis=-1, keepdims=True) + NORM_EPS)
            xh = xv * rstd
            dn = acc_s[...]
            dg_ref[...] += jnp.sum(dn * xh, axis=0, keepdims=True)
            dhat = dn * g_ref[...]
            dx_ref[...] = dh1_ref[...] + rstd * (dhat - xh * jnp.mean(dhat * xh, axis=-1, keepdims=True))

    row = pl.BlockSpec((tm, D), lambda i, j: (i, 0))
    vec = pl.BlockSpec((1, D), lambda i, j: (0, 0))
    upd_in, upd_out, upd_shape, upd_args = [], [], [], []
    for terms, w, m, v in updates:
        rows, cols = w.shape
        rs = rows // steps
        blk = pl.BlockSpec((rs, cols), lambda i, j: (i * nj + j, 0))
        upd_in += [pl.BlockSpec((4, rs, cols), lambda i, j: (0, i * nj + j, 0)), blk, blk, blk]
        upd_out += [blk] * 4
        upd_shape += [SDS((rows, cols), F32)] * 4
        upd_args += [pltpu.with_memory_space_constraint(a, pltpu.HBM) for a in (terms, w, m, v)]
    res, got = _call(
        body, name="in_bwd", grid=(s // tm, nj),
        in_specs=[pl.BlockSpec((tm, W_IN_COLS), lambda i, j: (i, j)),
                  pl.BlockSpec((None, D, W_IN_COLS), lambda i, j: (j, 0, 0)), row, row, vec] + upd_in,
        out_specs=[row, vec] + upd_out,
        out_shape=[SDS((s, D), F32), SDS((1, D), F32)] + upd_shape,
        scratch_shapes=[pltpu.VMEM((tm, D), F32)],
        params=_cparams(("arbitrary", "arbitrary"), 48), args=(dz, w_in_g, x, dh1, g_mix, *upd_args), comm=comm)
    return res[:2], [res[2 + 4 * u:6 + 4 * u] for u in range(nu)], got


def _wgrad(name, a, b):
    s = a.shape[0]
    ts = min(TM_ROWS, s)
    a_w, b_w = a.shape[1], b.shape[1]

    def body(a_ref, b_ref, o_ref, acc_s):
        t = pl.program_id(0)

        @pl.when(t == 0)
        def _():
            acc_s[...] = jnp.zeros_like(acc_s)

        acc_s[...] += _dot_tn(a_ref[...].astype(BF16), b_ref[...].astype(BF16))

        @pl.when(t == pl.num_programs(0) - 1)
        def _():
            o_ref[...] = acc_s[...].astype(BF16)

    return pl.pallas_call(
        body, name=name, grid=(s // ts,),
        in_specs=[pl.BlockSpec((ts, a_w), lambda t: (t, 0)), pl.BlockSpec((ts, b_w), lambda t: (t, 0))],
        out_specs=pl.BlockSpec((a_w, b_w), lambda t: (0, 0)),
        out_shape=SDS((a_w, b_w), BF16),
        scratch_shapes=[pltpu.VMEM((a_w, b_w), F32)],
        compiler_params=_cparams(("arbitrary",), 48),
    )(a, b)


def _place():
    x, y, c = lax.axis_index("x"), lax.axis_index("y"), lax.axis_index("c")
    return x, y, c


def _other_chips(x, y):
    return [(x, 1 - y, 2 * x + 1 - y), (1 - x, y, 2 * (1 - x) + y), (1 - x, 1 - y, 2 * (1 - x) + 1 - y)]


class _Plan:
    def __init__(self, arrays, out_shape, sems, start, finish, middle=None, middle_at=6):
        self.arrays, self.out_shape, self.sems, self.start, self.finish = arrays, out_shape, sems, start, finish
        self.middle, self.middle_at = middle, middle_at


def _gather_plan(shards, middle_at=6):
    n = len(shards)

    def copies(ins, outs, sems):
        send_sems, recv_sems, local_sems = sems
        x, y, c = _place()
        chip = 2 * x + y
        me = 2 * chip + c
        sib = (x, y, 1 - c)
        chips = _other_chips(x, y)

        def rc(k, t, src, blk, to):
            return pltpu.make_async_remote_copy(
                src_ref=src, dst_ref=outs[t].at[blk], send_sem=send_sems.at[k * n + t],
                recv_sem=recv_sems.at[k * n + t], device_id=to, device_id_type=MESH)

        (yx, yy, y_chip), (xx, xy, x_chip), _ = chips
        local = [pltpu.make_async_copy(ins[t], outs[t].at[me], local_sems.at[t]) for t in range(n)]
        sends = ([rc(0, t, ins[t], me, sib) for t in range(n)] + [rc(1, t, ins[t], me, (yx, yy, c)) for t in range(n)]
                 + [rc(2, t, ins[t], me, (xx, xy, c)) for t in range(n)])
        passed = [[rc(4 + j, t, outs[t].at[2 * pc + c], 2 * pc + c, sib) for t in range(n)]
                  for j, (_, _, pc) in enumerate(chips)]
        relays = [[rc(3, t, outs[t].at[2 * y_chip + c], 2 * y_chip + c, (xx, xy, c)) for t in range(n)],
                  [rc(3, t, outs[t].at[2 * x_chip + c], 2 * x_chip + c, (yx, yy, c)) for t in range(n)]]
        return rc, local, sends, passed, relays, chips, chip, c, sib

    def start(ins, outs, sems):
        _, local, sends, _, _, _, _, _, _ = copies(ins, outs, sems)
        for cp in local + sends:
            cp.start()

    def middle(ins, outs, sems):
        rc, _, _, passed, relays, chips, _, c, sib = copies(ins, outs, sems)
        for j in range(2):
            for t in range(n):
                rc(1 + j, t, ins[t], 2 * chips[j][2] + c, sib).wait_recv()
        for j in range(2):
            for cp in passed[j]:
                cp.start()

            @pl.when(c == j)
            def _():
                for cp in relays[j]:
                    cp.start()

    def finish(ins, outs, sems):
        rc, local, sends, passed, relays, chips, chip, c, sib = copies(ins, outs, sems)
        far = 2 * chips[2][2] + c
        for t in range(n):
            rc(3, t, ins[t], far, sib).wait_recv()
        for cp in passed[2]:
            cp.start()
        for t in range(n):
            rc(0, t, ins[t], 2 * chip + 1 - c, sib).wait_recv()
        for j, (px, py, pc) in enumerate(chips):
            for t in range(n):
                rc(4 + j, t, ins[t], 2 * pc + 1 - c, sib).wait_recv()
        for cp in sends + passed[0] + passed[1] + passed[2]:
            cp.wait_send()
        for j in range(2):
            @pl.when(c == j)
            def _():
                for cp in relays[j]:
                    cp.wait_send()
        for cp in local:
            cp.wait()

    return _Plan(list(shards), [SDS((N_SLOT,) + tuple(a.shape), a.dtype) for a in shards],
                 [pltpu.SemaphoreType.DMA((7 * n,)), pltpu.SemaphoreType.DMA((7 * n,)),
                  pltpu.SemaphoreType.DMA((n,))], start, finish, middle, middle_at)


def _sibling_plan(grads, whole=()):
    n, m = len(grads), len(whole)

    def copies(ins, outs, sems):
        send_sems, recv_sems = sems
        x, y, c = _place()
        sib = (x, y, 1 - c)

        def rc(t, src, dst):
            return pltpu.make_async_remote_copy(src_ref=src, dst_ref=dst, send_sem=send_sems.at[t],
                                                recv_sem=recv_sems.at[t], device_id=sib, device_id_type=MESH)
        return rc, c

    def start(ins, outs, sems):
        rc, c = copies(ins, outs, sems)
        for t in range(n):
            for j in range(4):
                rc(t, ins[t].at[2 * j + 1 - c], outs[t].at[j]).start()
        for t in range(n, n + m):
            rc(t, ins[t], outs[t]).start()

    def finish(ins, outs, sems):
        rc, _ = copies(ins, outs, sems)
        for t in range(n):
            rc(t, ins[t].at[pl.ds(0, 4)], outs[t]).wait()
        for t in range(n, n + m):
            rc(t, ins[t], outs[t]).wait()

    return _Plan(list(grads) + list(whole),
                 [SDS((4,) + tuple(g.shape[1:]), g.dtype) for g in grads] + [SDS(a.shape, a.dtype) for a in whole],
                 [pltpu.SemaphoreType.DMA((n + m,)), pltpu.SemaphoreType.DMA((n + m,))], start, finish)


def _chips_plan(parts, whole=()):
    n, m = len(parts), len(whole)

    def src_of(ins, t, pc):
        return ins[t].at[pc] if t < n else ins[t]

    def local_copies(ins, outs, sems, chip):
        return [pltpu.make_async_copy(src_of(ins, t, chip), outs[t].at[chip], sems[2].at[t]) for t in range(n + m)]

    def start(ins, outs, sems):
        send_sems, recv_sems, _ = sems
        x, y, c = _place()
        chip = 2 * x + y
        for cp in local_copies(ins, outs, sems, chip):
            cp.start()
        for px, py, pc in _other_chips(x, y):
            for t in range(n + m):
                pltpu.make_async_remote_copy(src_ref=src_of(ins, t, pc), dst_ref=outs[t].at[chip],
                                             send_sem=send_sems.at[t], recv_sem=recv_sems.at[t],
                                             device_id=(px, py, c), device_id_type=MESH).start()

    def finish(ins, outs, sems):
        send_sems, recv_sems, _ = sems
        x, y, c = _place()
        for t in range(n + m):
            three = outs[t].at[pl.ds(0, 3)]
            pltpu.make_async_remote_copy(src_ref=three, dst_ref=three, send_sem=send_sems.at[t],
                                         recv_sem=recv_sems.at[t], device_id=(x, y, c), device_id_type=MESH).wait()
        for cp in local_copies(ins, outs, sems, 2 * x + y):
            cp.wait()

    return _Plan(list(parts) + list(whole),
                 [SDS(p.shape, p.dtype) for p in parts] + [SDS((4,) + tuple(a.shape), a.dtype) for a in whole],
                 [pltpu.SemaphoreType.DMA((n + m,)), pltpu.SemaphoreType.DMA((n + m,)),
                  pltpu.SemaphoreType.DMA((n + m,))], start, finish)


def _exchange_plan(arr):
    def peers(x, y, c):
        flip = lambda v, f: 1 - v if f else v
        return [(flip(x, fx), flip(y, fy), flip(c, fc))
                for fx in (0, 1) for fy in (0, 1) for fc in (0, 1) if fx or fy or fc]

    def start(ins, outs, sems):
        x, y, c = _place()
        me = 4 * x + 2 * y + c
        pltpu.make_async_copy(ins[0], outs[0].at[me], sems[2].at[0]).start()
        for to in peers(x, y, c):
            pltpu.make_async_remote_copy(src_ref=ins[0], dst_ref=outs[0].at[me], send_sem=sems[0].at[0],
                                         recv_sem=sems[1].at[0], device_id=to, device_id_type=MESH).start()

    def finish(ins, outs, sems):
        x, y, c = _place()
        seven = outs[0].at[pl.ds(0, 7)]
        pltpu.make_async_remote_copy(src_ref=seven, dst_ref=seven, send_sem=sems[0].at[0], recv_sem=sems[1].at[0],
                                     device_id=(x, y, c), device_id_type=MESH).wait()
        pltpu.make_async_copy(ins[0], outs[0].at[4 * x + 2 * y + c], sems[2].at[0]).wait()

    return _Plan([arr], [SDS((N_SLOT,) + tuple(arr.shape), arr.dtype)],
                 [pltpu.SemaphoreType.DMA((1,)), pltpu.SemaphoreType.DMA((1,)), pltpu.SemaphoreType.DMA((1,))],
                 start, finish)


def _join(*plans):
    def cut(seq, sizes):
        out, at = [], 0
        for k in sizes:
            out.append(seq[at:at + k])
            at += k
        return out

    n_arr = [len(p.arrays) for p in plans]
    n_sem = [len(p.sems) for p in plans]

    def start(ins, outs, sems):
        for p, i, o, s in zip(plans, cut(ins, n_arr), cut(outs, n_arr), cut(sems, n_sem)):
            p.start(i, o, s)

    def finish(ins, outs, sems):
        for p, i, o, s in zip(plans, cut(ins, n_arr), cut(outs, n_arr), cut(sems, n_sem)):
            p.finish(i, o, s)

    def middle(ins, outs, sems):
        for p, i, o, s in zip(plans, cut(ins, n_arr), cut(outs, n_arr), cut(sems, n_sem)):
            if p.middle is not None:
                p.middle(i, o, s)

    return _Plan([a for p in plans for a in p.arrays], [o for p in plans for o in p.out_shape],
                 [s for p in plans for s in p.sems], start, finish,
                 middle if any(p.middle is not None for p in plans) else None,
                 max(p.middle_at for p in plans))


def _run_plan(name, plan):
    k = len(plan.arrays)

    def body(*refs):
        ins, outs, sems = refs[:k], refs[k:2 * k], refs[2 * k:]
        plan.start(ins, outs, sems)
        if plan.middle is not None:
            plan.middle(ins, outs, sems)
        plan.finish(ins, outs, sems)

    return pl.pallas_call(
        body, name=name, in_specs=[ANY] * k, out_specs=[ANY] * k, out_shape=plan.out_shape,
        scratch_shapes=plan.sems, compiler_params=pltpu.CompilerParams(has_side_effects=True),
    )(*plan.arrays)


def _call(body, *, name, grid, in_specs, out_specs, out_shape, scratch_shapes, params, args, comm=None,
          aliases=None, prefetch=()):
    aliases = aliases or {}
    n_pre = len(prefetch)

    def launch(fn, ins_specs, outs_specs, outs_shape, scratch, operands):
        spec = pltpu.PrefetchScalarGridSpec(num_scalar_prefetch=n_pre, grid=grid, in_specs=ins_specs,
                                            out_specs=outs_specs, scratch_shapes=scratch)
        return pl.pallas_call(fn, name=name, grid_spec=spec, out_shape=outs_shape, compiler_params=params,
                              input_output_aliases=aliases)(*prefetch, *[_small_in_hbm(a) for a in operands])

    if comm is None:
        return list(launch(body, in_specs, out_specs, out_shape, scratch_shapes, args)), []
    n_in, n_out, n_scr, k = len(in_specs), len(out_specs), len(scratch_shapes), len(comm.arrays)

    def wrapped(*refs):
        pre, refs = refs[:n_pre], refs[n_pre:]
        ins = refs[:n_in]
        c_in = refs[n_in:n_in + k]
        outs = refs[n_in + k:n_in + k + n_out]
        c_out = refs[n_in + k + n_out:n_in + 2 * k + n_out]
        scr = refs[n_in + 2 * k + n_out:n_in + 2 * k + n_out + n_scr]
        sems = refs[n_in + 2 * k + n_out + n_scr:]
        step, steps = pl.program_id(0), grid[0]
        for d in range(1, len(grid)):
            step, steps = step * grid[d] + pl.program_id(d), steps * grid[d]

        @pl.when(step == 0)
        def _():
            comm.start(c_in, c_out, sems)

        if comm.middle is not None:
            @pl.when(step == (comm.middle_at * steps) // 8)
            def _():
                comm.middle(c_in, c_out, sems)

        body(*pre, *ins, *outs, *scr)

        @pl.when(step == steps - 1)
        def _():
            comm.finish(c_in, c_out, sems)

    res = launch(wrapped, list(in_specs) + [ANY] * k, list(out_specs) + [ANY] * k,
                 list(out_shape) + list(comm.out_shape), list(scratch_shapes) + list(comm.sems),
                 tuple(args) + tuple(comm.arrays))
    return list(res[:n_out]), list(res[n_out:])


def _wgrad_paired(name, a_t, b, core, by_rows=False, comm=None):
    s = b.shape[0]
    m = a_t.shape[0] // N_SLOT if by_rows else a_t.shape[0]
    cols = b.shape[1] if by_rows else b.shape[1] // N_SLOT
    half = N_SLOT // 2

    def owner(k, c):
        return 2 * (k % half) + jnp.where(k < half, 1 - c, c)

    def body(c_ref, a_ref, b_ref, sum_ref, out_s, recv_s, send_sems, recv_sems):
        del c_ref
        k = pl.program_id(0)
        x, y, c = _place()

        def to_sibling(j):
            return pltpu.make_async_remote_copy(src_ref=out_s.at[j % 2], dst_ref=recv_s.at[j],
                                                send_sem=send_sems.at[j], recv_sem=recv_sems.at[j],
                                                device_id=(x, y, 1 - c), device_id_type=MESH)

        @pl.when((k >= 2) & (k < half + 2))
        def _():
            to_sibling(k - 2).wait_send()

        @pl.when(k < half)
        def _():
            out_s[k % 2] = _dot(a_ref[...], b_ref[...]).astype(BF16)
            to_sibling(k).start()

        @pl.when(k >= half)
        def _():
            to_sibling(k - half).wait_recv()
            sum_ref[...] = (_dot(a_ref[...], b_ref[...]) + recv_s[k - half].astype(F32)).astype(BF16)

    if by_rows:
        in_specs = [pl.BlockSpec((m, s), lambda k, c_ref: (owner(k, c_ref[0]), 0)),
                    pl.BlockSpec((s, cols), lambda k, c_ref: (0, 0))]
    else:
        in_specs = [pl.BlockSpec((m, s), lambda k, c_ref: (0, 0)),
                    pl.BlockSpec((s, cols), lambda k, c_ref: (0, owner(k, c_ref[0])))]
    return _call(body, name=name, grid=(N_SLOT,), in_specs=in_specs,
                 out_specs=[pl.BlockSpec((None, m, cols), lambda k, c_ref: (jnp.maximum(k - half, 0), 0, 0))],
                 out_shape=[SDS((half, m, cols), BF16)],
                 scratch_shapes=[pltpu.VMEM((2, m, cols), BF16), pltpu.VMEM((half, m, cols), BF16),
                                 pltpu.SemaphoreType.DMA((half,)), pltpu.SemaphoreType.DMA((half,))],
                 params=_cparams(("arbitrary",), 56), args=(a_t, b), comm=comm, prefetch=(core,))


def _row_tile(rows):
    for t in (512, 256, 128, 64, 32, 16, 8):
        if rows % t == 0:
            return t
    return rows


def _pair_sum(name, g8, recv4, core):
    _, rows, cols = recv4.shape
    tr = _row_tile(rows)
    g42 = g8.reshape(4, 2, rows, cols)

    def body(c_ref, g_ref, r_ref, o_ref):
        del c_ref
        o_ref[...] = (g_ref[...].astype(F32) + r_ref[...].astype(F32)).astype(o_ref.dtype)

    return pl.pallas_call(
        body, name=name,
        grid_spec=pltpu.PrefetchScalarGridSpec(
            num_scalar_prefetch=1, grid=(4, rows // tr),
            in_specs=[pl.BlockSpec((None, None, tr, cols), lambda j, i, c_ref: (j, c_ref[0], i, 0)),
                      pl.BlockSpec((None, tr, cols), lambda j, i, c_ref: (j, i, 0))],
            out_specs=pl.BlockSpec((None, tr, cols), lambda j, i, c_ref: (j, i, 0))),
        out_shape=SDS(recv4.shape, g8.dtype),
        compiler_params=_cparams(("arbitrary", "arbitrary"), 32),
    )(core, g42, recv4)


def _add2(name, a, b):
    rows, cols = a.shape
    tr = _row_tile(rows)

    def body(a_ref, b_ref, o_ref):
        o_ref[...] = a_ref[...] + b_ref[...]

    blk = pl.BlockSpec((tr, cols), lambda i: (i, 0))
    return pl.pallas_call(body, name=name, grid=(rows // tr,), in_specs=[blk, blk], out_specs=blk,
                          out_shape=SDS(a.shape, a.dtype),
                          compiler_params=_cparams(("arbitrary",), 32))(a, b)


def _sum_terms(name, terms):
    k, rows, cols = terms.shape
    tr = _row_tile(rows)

    def body(r_ref, o_ref):
        acc = r_ref[0]
        for q in range(1, k):
            acc = acc + r_ref[q]
        o_ref[...] = acc

    return pl.pallas_call(body, name=name, grid=(rows // tr,),
                          in_specs=[pl.BlockSpec((k, tr, cols), lambda i: (0, i, 0))],
                          out_specs=pl.BlockSpec((tr, cols), lambda i: (i, 0)),
                          out_shape=SDS((rows, cols), terms.dtype),
                          compiler_params=_cparams(("arbitrary",), 32))(terms)


def _adam_update(g, w, m, v):
    c1 = 1.0 / (1.0 - ADAM_B1 ** ADAM_STEP)
    c2 = 1.0 / (1.0 - ADAM_B2 ** ADAM_STEP)
    mn = ADAM_B1 * m + (1.0 - ADAM_B1) * g
    vn = ADAM_B2 * v + (1.0 - ADAM_B2) * (g * g)
    delta = (-ADAM_LR) * ((mn * c1) / (jnp.sqrt(vn * c2) + ADAM_EPS) + ADAM_WD * w)
    return delta, mn, vn


def _adamw_many(name, gs, ws, ms, vs):
    n = len(gs)

    def body(*refs):
        for p in range(n):
            g, w, m, v = (refs[q * n + p][...] for q in range(4))
            d, mn, vn = _adam_update(g, w, m, v)
            refs[4 * n + p][...] = d
            refs[5 * n + p][...] = mn
            refs[6 * n + p][...] = vn

    full = [pl.BlockSpec(w.shape, lambda i: (0, 0)) for w in ws]
    shapes = [SDS(w.shape, F32) for w in ws]
    res = pl.pallas_call(body, name=name, grid=(1,), in_specs=full * 4, out_specs=full * 3, out_shape=shapes * 3,
                         compiler_params=_cparams(("arbitrary",), 32),
                         )(*[_small_in_hbm(a) for a in (*gs, *ws, *ms, *vs)])
    return [(res[p], res[n + p], res[2 * n + p]) for p in range(n)]


def _adamw(name, terms, w, m, v):
    k, rows, cols = terms.shape
    tr = _row_tile(rows)

    def body(t_ref, w_ref, m_ref, v_ref, g_ref, d_ref, mo_ref, vo_ref):
        g = t_ref[0].astype(F32)
        for q in range(1, k):
            g = g + t_ref[q].astype(F32)
        g_ref[...] = g
        d_ref[...], mo_ref[...], vo_ref[...] = _adam_update(g, w_ref[...], m_ref[...], v_ref[...])

    blk = pl.BlockSpec((tr, cols), lambda i: (i, 0))
    return pl.pallas_call(body, name=name, grid=(rows // tr,),
                          in_specs=[pl.BlockSpec((k, tr, cols), lambda i: (0, i, 0)), blk, blk, blk],
                          out_specs=[blk] * 4, out_shape=[SDS((rows, cols), F32)] * 4,
                          compiler_params=_cparams(("arbitrary",), 40),
                          )(*[pltpu.with_memory_space_constraint(a, pltpu.HBM) for a in (terms, w, m, v)])


def kernel(x, norm_mix_g, w_in, conv_w, conv_b, w_rgate, b_rgate, w_igate, b_igate, lru_lambda, w_out_a, sgu_ln_g, sgu_ln_b, sgu_w_s, sgu_b_s, w_out_b, w_out, norm_mlp_g, w_up, w_down, norm_final_g, loss_target, m_norm_mix_g, m_w_in, m_conv_w, m_conv_b, m_w_rgate, m_b_rgate, m_w_igate, m_b_igate, m_lru_lambda, m_w_out_a, m_sgu_ln_g, m_sgu_ln_b, m_sgu_w_s, m_sgu_b_s, m_w_out_b, m_w_out, m_norm_mlp_g, m_w_up, m_w_down, m_norm_final_g, v_norm_mix_g, v_w_in, v_conv_w, v_conv_b, v_w_rgate, v_b_rgate, v_w_igate, v_b_igate, v_lru_lambda, v_w_out_a, v_sgu_ln_g, v_sgu_ln_b, v_sgu_w_s, v_sgu_b_s, v_w_out_b, v_w_out, v_norm_mlp_g, v_w_up, v_w_down, v_norm_final_g):
    cx, cy, cc = _place()
    me = 4 * cx + 2 * cy + cc
    core = jnp.reshape(cc, (1,)).astype(jnp.int32)
    xs = x[0]
    tgt = loss_target[0]

    gate_shard = jnp.stack([w_rgate[0], w_igate[0]]).astype(BF16).reshape(2 * HEADS * 32, HEAD_DIM)
    vec_shard = jnp.concatenate([conv_w[0], b_rgate[0], b_igate[0]], axis=1)
    vec_shard = jnp.pad(vec_shard, ((0, 4), (0, 256 - vec_shard.shape[1])))
    shards = [w_in[0].astype(BF16), w_out_a[0].astype(BF16), w_out_b[0].astype(BF16), w_out[0].astype(BF16),
              w_up[0].astype(BF16), w_down[0].astype(BF16), gate_shard, vec_shard]
    (z, n1_t, w_in_g), (gate_g, vec_g) = _in_proj(xs, norm_mix_g, shards[0], _slot_order(cx, cy, cc),
                                                comm=_gather_plan(shards[6:8]))
    gates = gate_g.reshape(N_SLOT, 2, HEADS, 32, HEAD_DIM).transpose(1, 2, 0, 3, 4).reshape(2, HEADS, HEAD_DIM, HEAD_DIM)
    w_r_f, w_i_f = gates[0], gates[1]
    conv_w_f = vec_g[:, 0:4, 0:128].transpose(1, 0, 2).reshape(CONV_K, D)
    b_r_f = vec_g[:, 0:4, 128:160].transpose(1, 0, 2).reshape(1, D)
    b_i_f = vec_g[:, 0:4, 160:192].transpose(1, 0, 2).reshape(1, D)
    b_s_t = jnp.transpose(sgu_b_s[0])

    (ya, hs, xc, r_gate, i_gate), (w_oa_g, w_ob_g, w_out_g, w_up_g) = _branch_a_fwd(
        z, conv_w_f, conv_b, w_r_f, b_r_f, w_i_f, b_i_f, lru_lambda, comm=_gather_plan(shards[1:5]))
    w_oa_f = w_oa_g.reshape(D, D)
    w_ob_f = w_ob_g.reshape(D, D)
    w_out_f = w_out_g.reshape(D, D)
    (yb, pa, pb, merged, h1), (w_down_g,) = _branch_b_merge_out(
        ya, z, xs, sgu_ln_g, sgu_ln_b, sgu_w_s[0], b_s_t, w_oa_f, w_ob_f, w_out_f,
        comm=_gather_plan(shards[5:6], middle_at=4))
    w_down_f = w_down_g.reshape(N_SLOT * FF_COLS, D)
    r_act, act_t, n2_t, dh2, dh2b, loss_acc, d_gfin = _mlp_fwd(h1, norm_mlp_g, w_up_g, w_down_f,
                                                               norm_final_g.reshape(1, D), tgt)

    def pair(names, grads, recv):
        return [_pair_sum("pair_sum_" + nm, g, r, core) for nm, g, r in zip(names, grads, recv)]

    (p_down,), _ = _wgrad_paired("wgrad_down", act_t, dh2b, core, by_rows=True)
    (df, dh1, d_gmlp), (got_down,) = _mlp_bwd(dh2, dh2b, r_act, w_down_f, w_up_g, h1, norm_mlp_g,
                                              comm=_chips_plan([p_down]))
    (p_up,), _ = _wgrad_paired("wgrad_up", n2_t, df, core)
    g_out = _wgrad("wgrad_out", merged, dh1).reshape(N_SLOT, D // N_SLOT, D)
    (dz, dpa, dpb, dya, dyb), (r_out,) = _merge_bwd(
        dh1, z, pa, pb, w_out_f, w_oa_f, w_ob_f, comm=_sibling_plan([g_out]))
    (p_out,) = pair(["out"], [g_out], [r_out])
    g_oa = _wgrad("wgrad_out_a", ya, dpa).reshape(N_SLOT, D // N_SLOT, D)
    g_ob = _wgrad("wgrad_out_b", yb, dpb).reshape(N_SLOT, D // N_SLOT, D)
    (dz, d_ws, d_bs, d_ln), (got_out, r_oa, r_ob) = _branch_b_bwd(
        dz, dyb, z, sgu_ln_g, sgu_ln_b, sgu_w_s[0], b_s_t,
        comm=_join(_chips_plan([p_out]), _sibling_plan([g_oa, g_ob])))
    p_oa, p_ob = pair(["out_a", "out_b"], [g_oa, g_ob], [r_oa, r_ob])
    (dz, d_vec, d_wr, d_wi), (got_up, got_oa, got_ob) = _branch_a_bwd(
        dz, dya, z, hs, xc, r_gate, i_gate, conv_w_f, w_r_f, w_i_f, lru_lambda,
        comm=_chips_plan([p_up, p_oa, p_ob]))
    g_gate = jnp.stack([d_wr, d_wi]).reshape(2, HEADS, N_SLOT, 32, HEAD_DIM).transpose(2, 0, 1, 3, 4)
    g_gate = g_gate.reshape(N_SLOT, 2 * HEADS * 32, HEAD_DIM).astype(BF16)

    d_bs_row = jnp.pad(d_bs[:, :, 0].reshape(1, GROUPS * CHUNK), ((0, 0), (0, D - GROUPS * CHUNK)))
    vecs = jnp.concatenate([d_vec, jnp.concatenate([d_ln[0:2], d_gmlp, d_gfin, d_bs_row, jnp.zeros((3, D), F32)])])
    d_ws2 = d_ws.reshape(GROUPS * CHUNK, CHUNK)
    (p_in,), (r_gate, r_vecs, r_ws) = _wgrad_paired("wgrad_in", n1_t, dz, core,
                                                    comm=_sibling_plan([g_gate], [vecs, d_ws2]))
    (p_gate,) = pair(["gate"], [g_gate], [r_gate])
    vecs_chip = _add2("pair_sum_vecs", vecs, r_vecs)
    ws_chip = _add2("pair_sum_ws", d_ws2, r_ws)
    riding = [(got_up, w_up[0], m_w_up[0], v_w_up[0]), (got_down, w_down[0], m_w_down[0], v_w_down[0])]
    (dx, d_gmix), (o_up, o_down), (got_in, got_gate, got_vecs, got_ws) = _in_bwd(
        dz, w_in_g, xs, dh1, norm_mix_g, updates=riding, comm=_chips_plan([p_in, p_gate], [vecs_chip, ws_chip]))
    o_up, o_down = [a[None] for a in o_up], [a[None] for a in o_down]
    vecs_sum = _sum_terms("sum_vecs", got_vecs)
    last = jnp.concatenate([d_gmix, jnp.pad(loss_acc[0:1], ((0, 0), (0, D - 128))), jnp.zeros((6, D), F32)])
    (last_all,) = _run_plan("exchange_last", _exchange_plan(last))
    last_sum = _sum_terms("sum_last", last_all)
    loss = last_sum[1, 0]

    def step(nm, terms, w, m, v, rows, cols):
        g, d, mn, vn = _adamw("adamw_" + nm, terms.reshape(4, rows, cols), w.reshape(rows, cols),
                              m.reshape(rows, cols), v.reshape(rows, cols))
        return [a.reshape(w.shape) for a in (g, d, mn, vn)]

    o_in = step("in", got_in, w_in, m_w_in, v_w_in, D, W_IN_COLS)
    o_oa = step("out_a", got_oa, w_out_a, m_w_out_a, v_w_out_a, D // N_SLOT, D)
    o_ob = step("out_b", got_ob, w_out_b, m_w_out_b, v_w_out_b, D // N_SLOT, D)
    o_out = step("out", got_out, w_out, m_w_out, v_w_out, D // N_SLOT, D)
    gate_w = jnp.stack([w_rgate[0], w_igate[0]]).reshape(2 * HEADS * 32, HEAD_DIM)
    gate_m = jnp.stack([m_w_rgate[0], m_w_igate[0]]).reshape(2 * HEADS * 32, HEAD_DIM)
    gate_v = jnp.stack([v_w_rgate[0], v_w_igate[0]]).reshape(2 * HEADS * 32, HEAD_DIM)
    o_gate = _adamw("adamw_gate", got_gate, gate_w, gate_m, gate_v)
    o_gate = [a.reshape(2, 1, HEADS, 32, HEAD_DIM) for a in o_gate]
    o_wr = [a[0] for a in o_gate]
    o_wi = [a[1] for a in o_gate]

    def own(full, width):
        return lax.dynamic_slice_in_dim(full, me * width, width, axis=1)

    small_g = {
        "norm_mix_g": last_sum[0:1], "conv_w": own(vecs_sum[0:4], 128), "conv_b": vecs_sum[4:5],
        "b_rgate": own(vecs_sum[5:6].reshape(HEADS, HEAD_DIM), 32),
        "b_igate": own(vecs_sum[6:7].reshape(HEADS, HEAD_DIM), 32),
        "lru_lambda": vecs_sum[7:8], "sgu_ln_g": vecs_sum[8:9], "sgu_ln_b": vecs_sum[9:10],
        "norm_mlp_g": vecs_sum[10:11], "norm_final_g": vecs_sum[11:12],
        "sgu_b_s": vecs_sum[12, 0:GROUPS * CHUNK].reshape(GROUPS, CHUNK),
    }
    small_w = {"norm_mix_g": (norm_mix_g, m_norm_mix_g, v_norm_mix_g), "conv_w": (conv_w, m_conv_w, v_conv_w),
               "conv_b": (conv_b, m_conv_b, v_conv_b), "b_rgate": (b_rgate, m_b_rgate, v_b_rgate),
               "b_igate": (b_igate, m_b_igate, v_b_igate), "lru_lambda": (lru_lambda, m_lru_lambda, v_lru_lambda),
               "sgu_ln_g": (sgu_ln_g, m_sgu_ln_g, v_sgu_ln_g), "sgu_ln_b": (sgu_ln_b, m_sgu_ln_b, v_sgu_ln_b),
               "norm_mlp_g": (norm_mlp_g, m_norm_mlp_g, v_norm_mlp_g),
               "norm_final_g": (norm_final_g, m_norm_final_g, v_norm_final_g),
               "sgu_b_s": (sgu_b_s, m_sgu_b_s, v_sgu_b_s), "sgu_w_s": (sgu_w_s, m_sgu_w_s, v_sgu_w_s)}
    order = list(small_g)
    as2d = lambda k, a: a.reshape(small_g[k].shape)
    upd = _adamw_many("adamw_small", [small_g[k] for k in order], *[[as2d(k, small_w[k][q]) for k in order]
                                                                     for q in range(3)])
    o_small = {k: [a.reshape(small_w[k][0].shape) for a in (small_g[k],) + u] for k, u in zip(order, upd)}
    ws3 = [a[0].reshape(GROUPS * CHUNK, CHUNK) for a in small_w.pop("sgu_w_s")]
    o_small["sgu_w_s"] = [a.reshape(sgu_w_s.shape) for a in _adamw("adamw_ws", got_ws, *ws3)]

    per_weight = {"norm_mix_g": o_small["norm_mix_g"], "w_in": o_in, "conv_w": o_small["conv_w"],
                  "conv_b": o_small["conv_b"], "w_rgate": o_wr, "b_rgate": o_small["b_rgate"], "w_igate": o_wi,
                  "b_igate": o_small["b_igate"], "lru_lambda": o_small["lru_lambda"], "w_out_a": o_oa,
                  "sgu_ln_g": o_small["sgu_ln_g"], "sgu_ln_b": o_small["sgu_ln_b"], "sgu_w_s": o_small["sgu_w_s"],
                  "sgu_b_s": o_small["sgu_b_s"], "w_out_b": o_ob, "w_out": o_out, "norm_mlp_g": o_small["norm_mlp_g"],
                  "w_up": o_up, "w_down": o_down, "norm_final_g": o_small["norm_final_g"]}
    names_w = list(per_weight)
    return (loss, dx[None], *[per_weight[k][0] for k in names_w], *[per_weight[k][1] for k in names_w],
            *[per_weight[k][2] for k in names_w], *[per_weight[k][3] for k in names_w])
```

```python
import jax
import jax.numpy as jnp
from jax import lax
from jax.experimental import pallas as pl
from jax.experimental.pallas import tpu as pltpu

F32 = jnp.float32
BF16 = jnp.bfloat16
SDS = jax.ShapeDtypeStruct
MESH = pl.DeviceIdType.MESH
ANY = pl.BlockSpec(memory_space=pl.ANY)

D = 1024
N_SLOT = 8
W_IN_COLS = 768
FF_COLS = 512
HEADS, HEAD_DIM = 4, 256
GROUPS, GROUP_DIM = 4, 256
CHUNK = 128
CONV_K = 4
NORM_EPS = 1e-6
LN_EPS = 1e-5
LRU_C = 8.0
ADAM_LR, ADAM_B1, ADAM_B2, ADAM_EPS, ADAM_WD, ADAM_STEP = 0.001, 0.9, 0.999, 1e-08, 0.01, 10

TM_ROWS = 1024
TM_MERGE = 512
T_BRANCH_A = 512
T_BRANCH_B = 256
MiB = 1024 * 1024
SMALL_OPERAND = 16 * 1024

_GELU_C = 0.7978845608028654
_GELU_A = 0.044715


def _small_in_hbm(a):
    return pltpu.with_memory_space_constraint(a, pltpu.HBM) if a.size <= SMALL_OPERAND else a


def _cparams(sem, vmem_mib):
    return pltpu.CompilerParams(dimension_semantics=sem, vmem_limit_bytes=vmem_mib * MiB)


def _gelu(x):
    t = jnp.tanh(_GELU_C * (x + _GELU_A * x * x * x))
    return 0.5 * x * (1.0 + t)


def _gelu_and_grad(x):
    x2 = x * x
    t = jnp.tanh(_GELU_C * x * (1.0 + _GELU_A * x2))
    g = 0.5 * x * (1.0 + t)
    dg = 0.5 * (1.0 + t) + 0.5 * x * (1.0 - t * t) * _GELU_C * (1.0 + 3.0 * _GELU_A * x2)
    return g, dg


def _softplus(x):
    return jnp.maximum(x, 0.0) + jnp.log1p(jnp.exp(-jnp.abs(x)))


def _dot(a, b):
    return jnp.dot(a, b, preferred_element_type=F32)


def _dot_nt(a, b):
    return lax.dot_general(a, b, (((1,), (1,)), ((), ())), preferred_element_type=F32)


def _dot_tn(a, b):
    return lax.dot_general(a, b, (((0,), (0,)), ((), ())), preferred_element_type=F32)


def _rows_shifted(prev8, cur, k):
    ext = jnp.concatenate([prev8, cur], axis=0)
    return pltpu.roll(ext, k, 0)[8:]


def _rows_advanced(cur, next8, k):
    t = cur.shape[0]
    ext = jnp.concatenate([cur, next8], axis=0)
    return pltpu.roll(ext, t + 8 - k, 0)[:t]


def _first_second(x, y, c):
    ny, nx, far = _other_chips(x, y)
    pick = lambda a, b: a * (1 - c) + b * c
    first = tuple(pick(a, b) for a, b in zip(ny, nx))
    second = tuple(pick(b, a) for a, b in zip(ny, nx))
    return first, second, far


def _slot_order(x, y, c):
    chip = 2 * x + y
    first, second, far = _first_second(x, y, c)
    order = [2 * chip + c, 2 * chip + 1 - c, 2 * first[2] + c, 2 * second[2] + 1 - c, 2 * second[2] + c,
             2 * first[2] + 1 - c, 2 * far[2] + c, 2 * far[2] + 1 - c]
    return jnp.stack(order).astype(jnp.int32)


def _in_proj(x, g_mix, w_in_own, order, comm=None):
    s = x.shape[0]
    tm = min(TM_ROWS, s)
    ni = s // tm

    def body(order_ref, x_ref, g_ref, own_ref, z_ref, nt_ref, wg_ref, n_s, w_s, send_sems, recv_sems, local_sems):
        j, i = pl.program_id(0), pl.program_id(1)
        px, py, c = _place()
        chip = 2 * px + py
        me = 2 * chip + c
        sib = (px, py, 1 - c)
        chips = _other_chips(px, py)

        def rc(k, src, blk, to):
            return pltpu.make_async_remote_copy(src_ref=src, dst_ref=w_s.at[blk], send_sem=send_sems.at[k],
                                                recv_sem=recv_sems.at[k], device_id=to, device_id_type=MESH)

        del chips
        first, second, far = _first_second(px, py, c)
        blocks = [2 * first[2] + c, 2 * second[2] + c, 2 * far[2] + c]
        own_in = pltpu.make_async_copy(own_ref, w_s.at[me], local_sems.at[0])
        to_first = rc(1, own_ref, me, (first[0], first[1], c))
        to_second = rc(2, own_ref, me, (second[0], second[1], c))
        relay = rc(3, w_s.at[blocks[0]], blocks[0], (second[0], second[1], c))
        sends = [rc(0, own_ref, me, sib), to_first, to_second, relay]
        passed = [rc(4 + q, w_s.at[blk], blk, sib) for q, blk in enumerate(blocks)]
        keep = pltpu.make_async_copy(w_s, wg_ref, local_sems.at[1])

        @pl.when((i == 0) & (j == 0))
        def _():
            own_in.start()
            sends[0].start()
            to_first.start()
            own_in.wait()

        @pl.when((i == 0) & (j == 1))
        def _():
            rc(0, own_ref, 2 * chip + 1 - c, sib).wait_recv()

        for q, blk in enumerate(blocks):
            @pl.when((i == 0) & (j == 2 + 2 * q))
            def _():
                rc(1 + q, own_ref, blk, sib).wait_recv()
                passed[q].start()
                if q == 0:
                    to_second.start()
                    relay.start()

            @pl.when((i == 0) & (j == 3 + 2 * q))
            def _():
                rc(4 + q, own_ref, order_ref[j], sib).wait_recv()

        rows = pl.ds(pl.multiple_of(i * tm, tm), tm)

        @pl.when(j == 0)
        def _():
            xv = x_ref[...]
            rstd = lax.rsqrt(jnp.mean(xv * xv, axis=-1, keepdims=True) + NORM_EPS)
            nb = (xv * rstd * g_ref[...]).astype(BF16)
            n_s[rows, :] = nb
            nt_ref[...] = nb.T

        z_ref[...] = _dot(n_s[rows, :], w_s[order_ref[j]]).astype(BF16)

        @pl.when((i == 0) & (j == N_SLOT - 1))
        def _():
            keep.start()

        @pl.when((i == ni - 1) & (j == N_SLOT - 1))
        def _():
            for cp in sends + passed:
                cp.wait_send()
            keep.wait()

    first_pass = lambda j, i, o: (jnp.where(j == 0, i, ni - 1), 0)
    (z, n1, w_in_g), extra = _call(
        body, name="in_proj", grid=(N_SLOT, ni), prefetch=(order,),
        in_specs=[pl.BlockSpec((tm, D), first_pass),
                  pl.BlockSpec((1, D), lambda j, i, o: (0, 0)), ANY],
        out_specs=[pl.BlockSpec((tm, W_IN_COLS), lambda j, i, o: (i, o[j])),
                   pl.BlockSpec((D, tm), lambda j, i, o: (0, jnp.where(j == 0, i, ni - 1))), ANY],
        out_shape=[SDS((s, N_SLOT * W_IN_COLS), BF16), SDS((D, s), BF16), SDS((N_SLOT, D, W_IN_COLS), BF16)],
        scratch_shapes=[pltpu.VMEM((s, D), BF16), pltpu.VMEM((N_SLOT, D, W_IN_COLS), BF16),
                        pltpu.SemaphoreType.DMA((7,)), pltpu.SemaphoreType.DMA((7,)), pltpu.SemaphoreType.DMA((2,))],
        params=_cparams(("arbitrary", "arbitrary"), 56), args=(x, g_mix, w_in_own), comm=comm)
    return (z, n1, w_in_g), extra


def _decay(r, sp_lam):
    log_a = (-LRU_C) * r * sp_lam
    a = jnp.exp(log_a)
    return a, jnp.sqrt(-jnp.tanh(log_a) * (a * a + 1.0))


def _lru_gates(xc, xcb, wr_ref, br, wi_ref, bi, sp_lam, a_s, b_s, r_ref, i_ref):
    for h in range(HEADS):
        sl = slice(h * HEAD_DIM, (h + 1) * HEAD_DIM)
        r = jax.nn.sigmoid(_dot(xcb[:, sl], wr_ref[h]) + br[:, sl])
        ig = jax.nn.sigmoid(_dot(xcb[:, sl], wi_ref[h]) + bi[:, sl])
        a, mult = _decay(r, sp_lam[:, sl])
        a_s[:, sl] = a
        b_s[:, sl] = xc[:, sl] * ig * mult
        r_ref[:, sl] = r.astype(BF16)
        i_ref[:, sl] = ig.astype(BF16)


def _conv_fwd(xa, prev8, cw, cb):
    xc = cb + cw[0:1, :] * xa
    for k in range(1, CONV_K):
        xc = xc + cw[k:k + 1, :] * _rows_shifted(prev8, xa, k)
    return xc


def _branch_a_fwd(z, conv_w, conv_b, w_r, b_r, w_i, b_i, lam, gating, comm=None):
    s = z.shape[0]
    ta = min(T_BRANCH_A, s)
    per16 = ta // 16
    nlo = max(s // ta // 2, 1)

    def body(xa_ref, xp_ref, ga_ref, cw_ref, cb_ref, wr_ref, br_ref, wi_ref, bi_ref, lam_ref,
             ub_ref, vb_ref, lg_ref, lb_ref, ws_ref, bs_ref,
             ya_ref, hs_ref, xc_ref, r_ref, i_ref, yb_ref, a_s, b_s, h_s, carry_s):
        i = pl.program_id(0)

        @pl.when(i == 0)
        def _():
            carry_s[...] = jnp.zeros_like(carry_s)

        @pl.when(i < nlo)
        def _():
            _gating_tile(ub_ref, vb_ref, lg_ref, lb_ref, ws_ref, bs_ref, yb_ref)

        xa = xa_ref[...].astype(F32)
        prev8 = jnp.where(i > 0, xp_ref[...].astype(F32)[8:16], 0.0)
        xc = _conv_fwd(xa, prev8, cw_ref[...], cb_ref[...])
        xcb = xc.astype(BF16)
        xc_ref[...] = xcb
        sp_lam = _softplus(-lam_ref[...])
        _lru_gates(xc, xcb, wr_ref, br_ref[...], wi_ref, bi_ref[...], sp_lam, a_s, b_s, r_ref, i_ref)

        row = lax.broadcasted_iota(jnp.int32, (8, D), 0)

        def group(g, carry):
            off = pl.multiple_of(g * 8, 8)
            a8 = a_s[pl.ds(off, 8), :]
            b8 = b_s[pl.ds(off, 8), :]
            for d in (1, 2, 4):
                a_sh = jnp.where(row >= d, pltpu.roll(a8, d, 0), 1.0)
                b_sh = jnp.where(row >= d, pltpu.roll(b8, d, 0), 0.0)
                b8 = a8 * b_sh + b8
                a8 = a8 * a_sh
            h8 = b8 + a8 * carry
            h_s[pl.ds(off, 8), :] = h8
            return jnp.broadcast_to(h8[7:8, :], (8, D))

        carry_s[...] = lax.fori_loop(0, ta // 8, group, carry_s[...])
        hs = h_s[...]
        hs_ref[...] = hs.astype(BF16)
        ya_ref[...] = (hs * _gelu(ga_ref[...].astype(F32))).astype(BF16)

    vec = pl.BlockSpec((1, D), lambda i: (0, 0))
    gate = pl.BlockSpec((HEADS, HEAD_DIM, HEAD_DIM), lambda i: (0, 0, 0))
    low = lambda q: pl.BlockSpec((ta, D), lambda i: (jnp.minimum(i, nlo - 1), q))
    return _call(
        body, name="branch_a_fwd", grid=(s // ta,),
        in_specs=[pl.BlockSpec((ta, D), lambda i: (i, 0)),
                  pl.BlockSpec((16, D), lambda i: (jnp.maximum(i * per16 - 1, 0), 0)),
                  pl.BlockSpec((ta, D), lambda i: (i, 1)),
                  pl.BlockSpec((CONV_K, D), lambda i: (0, 0)), vec, gate, vec, gate, vec, vec,
                  low(2), low(3), vec, vec, pl.BlockSpec((GROUPS, CHUNK, CHUNK), lambda i: (0, 0, 0)),
                  pl.BlockSpec((CHUNK, GROUPS), lambda i: (0, 0))],
        out_specs=[pl.BlockSpec((ta, D), lambda i: (i, 0))] * 5 + [low(0)],
        out_shape=[SDS((s, D), BF16)] * 5 + [SDS((nlo * ta, D), BF16)],
        scratch_shapes=[pltpu.VMEM((ta, D), F32), pltpu.VMEM((ta, D), F32), pltpu.VMEM((ta, D), F32),
                        pltpu.VMEM((8, D), F32)],
        params=_cparams(("arbitrary",), 48),
        args=(z, z, z, conv_w, conv_b, w_r, b_r, w_i, b_i, lam, z, z, *gating), comm=comm)


def _sgu_common(ub, vb, lg, lb, with_grad):
    if with_grad:
        u, du = _gelu_and_grad(ub)
        v, dv = _gelu_and_grad(vb)
    else:
        u, v, du, dv = _gelu(ub), _gelu(vb), None, None
    mu = jnp.mean(v, axis=-1, keepdims=True)
    vc = v - mu
    rstd = lax.rsqrt(jnp.mean(vc * vc, axis=-1, keepdims=True) + LN_EPS)
    vhat = vc * rstd
    vln = vhat * lg + lb
    return u, du, dv, rstd, vhat, vln


def _masked_ws(ws_ref):
    t = lax.broadcasted_iota(jnp.int32, (CHUNK, CHUNK), 0)
    c = lax.broadcasted_iota(jnp.int32, (CHUNK, CHUNK), 1)
    keep = c <= t
    return [jnp.where(keep, ws_ref[g], 0.0).astype(BF16) for g in range(GROUPS)]


def _gating_tile(ub_ref, vb_ref, lg_ref, lb_ref, ws_ref, bs_ref, yb_ref):
    u, _, _, _, _, vln = _sgu_common(ub_ref[...].astype(F32), vb_ref[...].astype(F32), lg_ref[...], lb_ref[...], False)
    vlnb = vln.astype(BF16)
    wm = _masked_ws(ws_ref)
    bs = bs_ref[...]
    for c in range(yb_ref.shape[0] // CHUNK):
        rs = slice(c * CHUNK, (c + 1) * CHUNK)
        for g in range(GROUPS):
            cs = slice(g * GROUP_DIM, (g + 1) * GROUP_DIM)
            sp = _dot(wm[g], vlnb[rs, cs]) + bs[:, g:g + 1]
            yb_ref[rs, cs] = (u[rs, cs] * sp).astype(BF16)


def _branch_b_merge_out(ya, yb_lo, z, x, ln_g, ln_b, w_s, b_s_t, w_oa, w_ob, w_out, comm=None):
    s = x.shape[0]
    tm = min(TM_MERGE, s)
    nlo = yb_lo.shape[0] // tm
    assert nlo * tm == yb_lo.shape[0] and nlo < s // tm

    def body(ub_ref, vb_ref, lg_ref, lb_ref, ws_ref, bs_ref, ylo_ref, ya_ref, ma_ref, mb_ref, x_ref, woa_ref, wob_ref,
             wo_ref, yb_ref, pa_ref, pb_ref, mg_ref, h1_ref):
        i = pl.program_id(0)

        @pl.when(i < nlo)
        def _():
            yb_ref[...] = ylo_ref[...]

        @pl.when(i >= nlo)
        def _():
            _gating_tile(ub_ref, vb_ref, lg_ref, lb_ref, ws_ref, bs_ref, yb_ref)

        pa = _dot(ya_ref[...], woa_ref[...])
        pb = _dot(yb_ref[...], wob_ref[...])
        merged = (jax.nn.sigmoid(ma_ref[...].astype(F32)) * pa
                  + jax.nn.sigmoid(mb_ref[...].astype(F32)) * pb).astype(BF16)
        pa_ref[...] = pa.astype(BF16)
        pb_ref[...] = pb.astype(BF16)
        mg_ref[...] = merged
        h1_ref[...] = x_ref[...] + _dot(merged, wo_ref[...])

    row = pl.BlockSpec((tm, D), lambda i: (i, 0))
    col = lambda q: pl.BlockSpec((tm, D), lambda i: (i, q))
    high = lambda q: pl.BlockSpec((tm, D), lambda i: (jnp.maximum(i, nlo), q))
    vec = pl.BlockSpec((1, D), lambda i: (0, 0))
    wsp = pl.BlockSpec((D, D), lambda i: (0, 0))
    return _call(
        body, name="branch_b_merge_out", grid=(s // tm,),
        in_specs=[high(2), high(3), vec, vec, pl.BlockSpec((GROUPS, CHUNK, CHUNK), lambda i: (0, 0, 0)),
                  pl.BlockSpec((CHUNK, GROUPS), lambda i: (0, 0)),
                  pl.BlockSpec((tm, D), lambda i: (jnp.minimum(i, nlo - 1), 0)), row, col(4), col(5), row, wsp, wsp, wsp],
        out_specs=[row, row, row, row, row],
        out_shape=[SDS((s, D), BF16)] * 4 + [SDS((s, D), F32)], scratch_shapes=[],
        params=_cparams(("arbitrary",), 56),
        args=(z, z, ln_g, ln_b, w_s, b_s_t, yb_lo, ya, z, z, x, w_oa, w_ob, w_out), comm=comm)


def _mlp_fwd(h1, g_mlp, w_up_g, w_down, g_fin, tgt):
    s = h1.shape[0]
    tm = min(TM_ROWS, s)
    nj = N_SLOT

    def body(h1_ref, gm_ref, wu_ref, wd_ref, gf_ref, t_ref, r_ref, at_ref, n2t_ref, dh2_ref, dh2b_ref, loss_ref,
             dgf_ref, n2_s, acc_s):
        i, j = pl.program_id(0), pl.program_id(1)

        @pl.when(j == 0)
        def _():
            hv = h1_ref[...]
            rstd = lax.rsqrt(jnp.mean(hv * hv, axis=-1, keepdims=True) + NORM_EPS)
            nb = (hv * rstd * gm_ref[...]).astype(BF16)
            n2_s[...] = nb
            n2t_ref[...] = nb.T
            acc_s[...] = jnp.zeros_like(acc_s)

        @pl.when((i == 0) & (j == 0))
        def _():
            loss_ref[...] = jnp.zeros_like(loss_ref)
            dgf_ref[...] = jnp.zeros_like(dgf_ref)

        r = jnp.maximum(_dot(n2_s[...], wu_ref[...]), 0.0)
        r_ref[...] = r.astype(BF16)
        act = (r * r).astype(BF16)
        at_ref[...] = act.T
        acc_s[...] += _dot(act, wd_ref[...])

        @pl.when(j == nj - 1)
        def _():
            h2 = h1_ref[...] + acc_s[...]
            rstd = lax.rsqrt(jnp.mean(h2 * h2, axis=-1, keepdims=True) + NORM_EPS)
            hh = h2 * rstd
            gf = gf_ref[...]
            e = hh * gf - t_ref[...]
            loss_ref[...] += jnp.sum(e * e) * (0.5 / D)
            dy = e * (1.0 / D)
            dgf_ref[...] += jnp.sum(dy * hh, axis=0, keepdims=True)
            dhh = dy * gf
            dh2 = rstd * (dhh - hh * jnp.mean(dhh * hh, axis=-1, keepdims=True))
            dh2_ref[...] = dh2
            dh2b_ref[...] = dh2.astype(BF16)

    row = pl.BlockSpec((tm, D), lambda i, j: (i, 0))
    vec = pl.BlockSpec((1, D), lambda i, j: (0, 0))
    return pl.pallas_call(
        body, name="mlp_fwd", grid=(s // tm, nj),
        in_specs=[row, vec, pl.BlockSpec((None, D, FF_COLS), lambda i, j: (j, 0, 0)),
                  pl.BlockSpec((FF_COLS, D), lambda i, j: (j, 0)), vec, row],
        out_specs=[pl.BlockSpec((tm, FF_COLS), lambda i, j: (i, j)), pl.BlockSpec((FF_COLS, tm), lambda i, j: (j, i)),
                   pl.BlockSpec((D, tm), lambda i, j: (0, i)), row, row, pl.BlockSpec((8, 128), lambda i, j: (0, 0)),
                   vec],
        out_shape=[SDS((s, nj * FF_COLS), BF16), SDS((nj * FF_COLS, s), BF16), SDS((D, s), BF16), SDS((s, D), F32),
                   SDS((s, D), BF16), SDS((8, 128), F32), SDS((1, D), F32)],
        scratch_shapes=[pltpu.VMEM((tm, D), BF16), pltpu.VMEM((tm, D), F32)],
        compiler_params=_cparams(("arbitrary", "arbitrary"), 56),
    )(h1, _small_in_hbm(g_mlp), w_up_g, w_down, _small_in_hbm(g_fin), tgt)


def _mlp_bwd(dh2, dh2b, r, w_down, w_up_g, h1, g_mlp, comm=None):
    s = h1.shape[0]
    tm = min(TM_ROWS, s)
    nj = N_SLOT

    def body(dh2_ref, dh2b_ref, r_ref, wd_ref, wu_ref, h1_ref, gm_ref, df_ref, dh1_ref, dgm_ref, acc_s):
        i, j = pl.program_id(0), pl.program_id(1)

        @pl.when(j == 0)
        def _():
            acc_s[...] = jnp.zeros_like(acc_s)

        @pl.when((i == 0) & (j == 0))
        def _():
            dgm_ref[...] = jnp.zeros_like(dgm_ref)

        d_act = _dot_nt(dh2b_ref[...], wd_ref[...])
        df = (d_act * (2.0 * r_ref[...].astype(F32))).astype(BF16)
        df_ref[...] = df
        acc_s[...] += _dot_nt(df, wu_ref[...])

        @pl.when(j == nj - 1)
        def _():
            hv = h1_ref[...]
            rstd = lax.rsqrt(jnp.mean(hv * hv, axis=-1, keepdims=True) + NORM_EPS)
            hh = hv * rstd
            dn2 = acc_s[...]
            dgm_ref[...] += jnp.sum(dn2 * hh, axis=0, keepdims=True)
            dhat = dn2 * gm_ref[...]
            dh1_ref[...] = dh2_ref[...] + rstd * (dhat - hh * jnp.mean(dhat * hh, axis=-1, keepdims=True))

    row = pl.BlockSpec((tm, D), lambda i, j: (i, 0))
    vec = pl.BlockSpec((1, D), lambda i, j: (0, 0))
    ffb = pl.BlockSpec((tm, FF_COLS), lambda i, j: (i, j))
    return _call(
        body, name="mlp_bwd", grid=(s // tm, nj),
        in_specs=[row, row, ffb, pl.BlockSpec((FF_COLS, D), lambda i, j: (j, 0)),
                  pl.BlockSpec((None, D, FF_COLS), lambda i, j: (j, 0, 0)), row, vec],
        out_specs=[ffb, row, vec],
        out_shape=[SDS((s, nj * FF_COLS), BF16), SDS((s, D), F32), SDS((1, D), F32)],
        scratch_shapes=[pltpu.VMEM((tm, D), F32)],
        params=_cparams(("arbitrary", "arbitrary"), 56), args=(dh2, dh2b, r, w_down, w_up_g, h1, g_mlp), comm=comm)


def _merge_bwd(dh1, z, pa, pb, w_out, w_oa, w_ob, comm=None):
    s = dh1.shape[0]
    tm = min(TM_MERGE, s)

    def body(dh1_ref, ma_ref, mb_ref, pa_ref, pb_ref, wo_ref, woa_ref, wob_ref,
             dz_ref, dpa_ref, dpb_ref, dya_ref, dyb_ref):
        dm = _dot_nt(dh1_ref[...].astype(BF16), wo_ref[...])
        sa = jax.nn.sigmoid(ma_ref[...].astype(F32))
        sb = jax.nn.sigmoid(mb_ref[...].astype(F32))
        dpa = (dm * sa).astype(BF16)
        dpb = (dm * sb).astype(BF16)
        dz_ref[:, 0:D] = (dm * pa_ref[...].astype(F32) * sa * (1.0 - sa)).astype(BF16)
        dz_ref[:, D:2 * D] = (dm * pb_ref[...].astype(F32) * sb * (1.0 - sb)).astype(BF16)
        dpa_ref[...] = dpa
        dpb_ref[...] = dpb
        dya_ref[...] = _dot_nt(dpa, woa_ref[...]).astype(BF16)
        dyb_ref[...] = _dot_nt(dpb, wob_ref[...]).astype(BF16)

    row = pl.BlockSpec((tm, D), lambda i: (i, 0))
    wsp = pl.BlockSpec((D, D), lambda i: (0, 0))
    return _call(
        body, name="merge_bwd", grid=(s // tm,),
        in_specs=[row, pl.BlockSpec((tm, D), lambda i: (i, 4)), pl.BlockSpec((tm, D), lambda i: (i, 5)),
                  row, row, wsp, wsp, wsp],
        out_specs=[pl.BlockSpec((tm, 2 * D), lambda i: (i, 2)), row, row, row, row],
        out_shape=[SDS((s, 6 * D), BF16)] + [SDS((s, D), BF16)] * 4, scratch_shapes=[],
        params=_cparams(("arbitrary",), 48), args=(dh1, z, z, pa, pb, w_out, w_oa, w_ob), comm=comm)


def _branch_b_bwd(dz, dyb, z, ln_g, ln_b, w_s, b_s_t, comm=None):
    s = z.shape[0]
    tb = min(T_BRANCH_B, s)

    def body(dz_in, dyb_ref, ub_ref, vb_ref, lg_ref, lb_ref, ws_ref, bs_ref,
             dz_ref, dws_ref, dbs_ref, dln_ref, du_s, dvln_s):
        del dz_in

        @pl.when(pl.program_id(0) == 0)
        def _():
            dws_ref[...] = jnp.zeros_like(dws_ref)
            dbs_ref[...] = jnp.zeros_like(dbs_ref)
            dln_ref[...] = jnp.zeros_like(dln_ref)

        lg = lg_ref[...]
        u, du, dv, rstd, vhat, vln = _sgu_common(ub_ref[...].astype(F32), vb_ref[...].astype(F32),
                                                 lg, lb_ref[...], True)
        vlnb = vln.astype(BF16)
        dyb_v = dyb_ref[...].astype(F32)
        wm = _masked_ws(ws_ref)
        keep = (lax.broadcasted_iota(jnp.int32, (CHUNK, CHUNK), 1)
                <= lax.broadcasted_iota(jnp.int32, (CHUNK, CHUNK), 0))
        bs = bs_ref[...]
        for c in range(tb // CHUNK):
            rs = slice(c * CHUNK, (c + 1) * CHUNK)
            for g in range(GROUPS):
                cs = slice(g * GROUP_DIM, (g + 1) * GROUP_DIM)
                v_blk = vlnb[rs, cs]
                sp = _dot(wm[g], v_blk) + bs[:, g:g + 1]
                d_sp = dyb_v[rs, cs] * u[rs, cs]
                d_spb = d_sp.astype(BF16)
                du_s[rs, cs] = dyb_v[rs, cs] * sp
                dvln_s[rs, cs] = _dot_tn(wm[g], d_spb)
                dws_ref[g] += jnp.where(keep, _dot_nt(d_spb, v_blk), 0.0)
                dbs_ref[g] += jnp.broadcast_to(jnp.sum(d_sp, axis=-1, keepdims=True), (CHUNK, CHUNK))
        dvln = dvln_s[...]
        dln_ref[0:1, :] += jnp.sum(dvln * vhat, axis=0, keepdims=True)
        dln_ref[1:2, :] += jnp.sum(dvln, axis=0, keepdims=True)
        dvh = dvln * lg
        d_v = rstd * (dvh - jnp.mean(dvh, axis=-1, keepdims=True)
                      - vhat * jnp.mean(dvh * vhat, axis=-1, keepdims=True))
        dz_ref[:, 0:D] = (du_s[...] * du).astype(BF16)
        dz_ref[:, D:2 * D] = (d_v * dv).astype(BF16)

    vec = pl.BlockSpec((1, D), lambda i: (0, 0))
    sq = pl.BlockSpec((GROUPS, CHUNK, CHUNK), lambda i: (0, 0, 0))
    return _call(
        body, name="branch_b_bwd", grid=(s // tb,),
        in_specs=[ANY, pl.BlockSpec((tb, D), lambda i: (i, 0)),
                  pl.BlockSpec((tb, D), lambda i: (i, 2)), pl.BlockSpec((tb, D), lambda i: (i, 3)), vec, vec, sq,
                  pl.BlockSpec((CHUNK, GROUPS), lambda i: (0, 0))],
        out_specs=[pl.BlockSpec((tb, 2 * D), lambda i: (i, 1)), sq, sq, pl.BlockSpec((8, D), lambda i: (0, 0))],
        out_shape=[SDS(dz.shape, BF16), SDS((GROUPS, CHUNK, CHUNK), F32), SDS((GROUPS, CHUNK, CHUNK), F32),
                   SDS((8, D), F32)],
        scratch_shapes=[pltpu.VMEM((tb, D), F32), pltpu.VMEM((tb, D), F32)], aliases={0: 0},
        params=_cparams(("arbitrary",), 40), args=(dz, dyb, z, z, ln_g, ln_b, w_s, b_s_t), comm=comm)


def _branch_a_bwd(dz, dya, z, hs, xc, r, ig, conv_w, w_r, w_i, lam, comm=None):
    s = z.shape[0]
    ta = min(T_BRANCH_A, s)
    nb = s // ta
    per16 = ta // 16

    def body(dz_in, dya_ref, xa_ref, ga_ref, hs_ref, hp_ref, xc_ref, r_ref, i_ref, cw_ref, wr_ref, wi_ref,
             lam_ref, dz_ref, vec_ref, dwr_ref, dwi_ref, a_s, b_s, h_s, dcar_s, acar_s, dxc_s):
        del dz_in
        i = pl.program_id(0)
        blk = nb - 1 - i

        @pl.when(i == 0)
        def _():
            dcar_s[...] = jnp.zeros_like(dcar_s)
            acar_s[...] = jnp.zeros_like(acar_s)
            dxc_s[...] = jnp.zeros_like(dxc_s)
            vec_ref[...] = jnp.zeros_like(vec_ref)
            dwr_ref[...] = jnp.zeros_like(dwr_ref)
            dwi_ref[...] = jnp.zeros_like(dwi_ref)

        cw = cw_ref[...]
        lam_v = lam_ref[...]
        xa = xa_ref[...].astype(F32)
        xcb = xc_ref[...]
        xc = xcb.astype(F32)
        sp_lam = _softplus(-lam_v)
        r_v, i_v = r_ref[...].astype(F32), i_ref[...].astype(F32)
        a_v, m_v = _decay(r_v, sp_lam)

        hs_v = hs_ref[...].astype(F32)
        hprev8 = jnp.where(blk > 0, hp_ref[...].astype(F32)[8:16], 0.0)
        h_m1 = _rows_shifted(hprev8, hs_v, 1)
        gg, dgg = _gelu_and_grad(ga_ref[...].astype(F32))
        dya_v = dya_ref[...].astype(F32)
        dz_ref[:, D:2 * D] = (dya_v * hs_v * dgg).astype(BF16)

        a_s[...] = _rows_advanced(a_v, acar_s[...], 1)
        b_s[...] = dya_v * gg

        row = lax.broadcasted_iota(jnp.int32, (8, D), 0)
        ng = ta // 8

        def group(gi, carry):
            off = pl.multiple_of((ng - 1 - gi) * 8, 8)
            c8 = a_s[pl.ds(off, 8), :]
            d8 = b_s[pl.ds(off, 8), :]
            for d in (1, 2, 4):
                c_sh = jnp.where(row < 8 - d, pltpu.roll(c8, 8 - d, 0), 1.0)
                d_sh = jnp.where(row < 8 - d, pltpu.roll(d8, 8 - d, 0), 0.0)
                d8 = c8 * d_sh + d8
                c8 = c8 * c_sh
            dh8 = d8 + c8 * carry
            h_s[pl.ds(off, 8), :] = dh8
            return jnp.broadcast_to(dh8[0:1, :], (8, D))

        dcar_s[...] = lax.fori_loop(0, ng, group, dcar_s[...])
        acar_s[...] = jnp.broadcast_to(a_v[0:1, :], (8, D))

        dbx = h_s[...]
        d_mult = dbx * xc * i_v
        d_loga = dbx * h_m1 * a_v - d_mult * (a_v * a_v) / m_v
        d_pr = d_loga * ((-LRU_C) * sp_lam) * r_v * (1.0 - r_v)
        d_pi = dbx * xc * m_v * i_v * (1.0 - i_v)
        vec_ref[7:8, :] += jnp.sum(d_loga * r_v, axis=0, keepdims=True) * (LRU_C * jax.nn.sigmoid(-lam_v))
        vec_ref[5:6, :] += jnp.sum(d_pr, axis=0, keepdims=True)
        vec_ref[6:7, :] += jnp.sum(d_pi, axis=0, keepdims=True)
        d_prb = d_pr.astype(BF16)
        d_pib = d_pi.astype(BF16)
        h_s[...] = dbx * i_v * m_v
        for h in range(HEADS):
            sl = slice(h * HEAD_DIM, (h + 1) * HEAD_DIM)
            h_s[:, sl] += _dot_nt(d_prb[:, sl], wr_ref[h]) + _dot_nt(d_pib[:, sl], wi_ref[h])
            dwr_ref[h] += _dot_tn(xcb[:, sl], d_prb[:, sl])
            dwi_ref[h] += _dot_tn(xcb[:, sl], d_pib[:, sl])
        d_xc = h_s[...]
        vec_ref[4:5, :] += jnp.sum(d_xc, axis=0, keepdims=True)
        vec_ref[0:1, :] += jnp.sum(d_xc * xa, axis=0, keepdims=True)
        d_xa = cw[0:1, :] * d_xc
        nxt = dxc_s[...]
        for k in range(1, CONV_K):
            ahead = _rows_advanced(d_xc, nxt, k)
            vec_ref[k:k + 1, :] += jnp.sum(ahead * xa, axis=0, keepdims=True)
            d_xa = d_xa + cw[k:k + 1, :] * ahead
        dz_ref[:, 0:D] = d_xa.astype(BF16)
        dxc_s[...] = d_xc[0:8, :]

    vec = pl.BlockSpec((1, D), lambda i: (0, 0))
    gate = pl.BlockSpec((HEADS, HEAD_DIM, HEAD_DIM), lambda i: (0, 0, 0))
    cur = lambda c: pl.BlockSpec((ta, D), lambda i: (nb - 1 - i, c))
    before = lambda c: pl.BlockSpec((16, D), lambda i: (jnp.maximum((nb - 1 - i) * per16 - 1, 0), c))
    return _call(
        body, name="branch_a_bwd", grid=(nb,),
        in_specs=[ANY, cur(0), cur(0), cur(1), cur(0), before(0), cur(0), cur(0), cur(0),
                  pl.BlockSpec((CONV_K, D), lambda i: (0, 0)), gate, gate, vec],
        out_specs=[pl.BlockSpec((ta, 2 * D), lambda i: (nb - 1 - i, 0)), pl.BlockSpec((8, D), lambda i: (0, 0)),
                   gate, gate],
        out_shape=[SDS(dz.shape, BF16), SDS((8, D), F32), SDS((HEADS, HEAD_DIM, HEAD_DIM), F32),
                   SDS((HEADS, HEAD_DIM, HEAD_DIM), F32)],
        scratch_shapes=[pltpu.VMEM((ta, D), F32)] * 3 + [pltpu.VMEM((8, D), F32)] * 3, aliases={0: 0},
        params=_cparams(("arbitrary",), 48),
        args=(dz, dya, z, z, hs, hs, xc, r, ig, conv_w, w_r, w_i, lam), comm=comm)


def _in_bwd(dz, w_in_g, x, dh1, g_mix, comm=None):
    s = x.shape[0]
    tm = min(TM_ROWS, s)
    nj = N_SLOT

    def body(dz_ref, w_ref, x_ref, dh1_ref, g_ref, dx_ref, dg_ref, acc_s):
        i, j = pl.program_id(0), pl.program_id(1)

        @pl.when(j == 0)
        def _():
            acc_s[...] = jnp.zeros_like(acc_s)

        @pl.when((i == 0) & (j == 0))
        def _():
            dg_ref[...] = jnp.zeros_like(dg_ref)

        acc_s[...] += _dot_nt(dz_ref[...], w_ref[...])

        @pl.when(j == nj - 1)
        def _():
            xv = x_ref[...]
            rstd = lax.rsqrt(jnp.mean(xv * xv, axis=-1, keepdims=True) + NORM_EPS)
            xh = xv * rstd
            dn = acc_s[...]
            dg_ref[...] += jnp.sum(dn * xh, axis=0, keepdims=True)
            dhat = dn * g_ref[...]
            dx_ref[...] = dh1_ref[...] + rstd * (dhat - xh * jnp.mean(dhat * xh, axis=-1, keepdims=True))

    row = pl.BlockSpec((tm, D), lambda i, j: (i, 0))
    vec = pl.BlockSpec((1, D), lambda i, j: (0, 0))
    return _call(
        body, name="in_bwd", grid=(s // tm, nj),
        in_specs=[pl.BlockSpec((tm, W_IN_COLS), lambda i, j: (i, j)),
                  pl.BlockSpec((None, D, W_IN_COLS), lambda i, j: (j, 0, 0)), row, row, vec],
        out_specs=[row, vec],
        out_shape=[SDS((s, D), F32), SDS((1, D), F32)],
        scratch_shapes=[pltpu.VMEM((tm, D), F32)],
        params=_cparams(("arbitrary", "arbitrary"), 48), args=(dz, w_in_g, x, dh1, g_mix), comm=comm)


def _wgrad(name, a, b):
    s = a.shape[0]
    ts = min(TM_ROWS, s)
    a_w, b_w = a.shape[1], b.shape[1]

    def body(a_ref, b_ref, o_ref, acc_s):
        t = pl.program_id(0)

        @pl.when(t == 0)
        def _():
            acc_s[...] = jnp.zeros_like(acc_s)

        acc_s[...] += _dot_tn(a_ref[...].astype(BF16), b_ref[...].astype(BF16))

        @pl.when(t == pl.num_programs(0) - 1)
        def _():
            o_ref[...] = acc_s[...].astype(BF16)

    return pl.pallas_call(
        body, name=name, grid=(s // ts,),
        in_specs=[pl.BlockSpec((ts, a_w), lambda t: (t, 0)), pl.BlockSpec((ts, b_w), lambda t: (t, 0))],
        out_specs=pl.BlockSpec((a_w, b_w), lambda t: (0, 0)),
        out_shape=SDS((a_w, b_w), BF16),
        scratch_shapes=[pltpu.VMEM((a_w, b_w), F32)],
        compiler_params=_cparams(("arbitrary",), 48),
    )(a, b)


def _place():
    x, y, c = lax.axis_index("x"), lax.axis_index("y"), lax.axis_index("c")
    return x, y, c


def _other_chips(x, y):
    return [(x, 1 - y, 2 * x + 1 - y), (1 - x, y, 2 * (1 - x) + y), (1 - x, 1 - y, 2 * (1 - x) + 1 - y)]


class _Plan:
    def __init__(self, arrays, out_shape, sems, start, finish, middle=None, middle_at=6):
        self.arrays, self.out_shape, self.sems, self.start, self.finish = arrays, out_shape, sems, start, finish
        self.middle, self.middle_at = middle, middle_at


def _gather_plan(shards, middle_at=6):
    n = len(shards)

    def copies(ins, outs, sems):
        send_sems, recv_sems, local_sems = sems
        x, y, c = _place()
        chip = 2 * x + y
        me = 2 * chip + c
        sib = (x, y, 1 - c)
        chips = _other_chips(x, y)

        def rc(k, t, src, blk, to):
            return pltpu.make_async_remote_copy(
                src_ref=src, dst_ref=outs[t].at[blk], send_sem=send_sems.at[k * n + t],
                recv_sem=recv_sems.at[k * n + t], device_id=to, device_id_type=MESH)

        (yx, yy, y_chip), (xx, xy, x_chip), _ = chips
        local = [pltpu.make_async_copy(ins[t], outs[t].at[me], local_sems.at[t]) for t in range(n)]
        sends = ([rc(0, t, ins[t], me, sib) for t in range(n)] + [rc(1, t, ins[t], me, (yx, yy, c)) for t in range(n)]
                 + [rc(2, t, ins[t], me, (xx, xy, c)) for t in range(n)])
        passed = [[rc(4 + j, t, outs[t].at[2 * pc + c], 2 * pc + c, sib) for t in range(n)]
                  for j, (_, _, pc) in enumerate(chips)]
        relays = [[rc(3, t, outs[t].at[2 * y_chip + c], 2 * y_chip + c, (xx, xy, c)) for t in range(n)],
                  [rc(3, t, outs[t].at[2 * x_chip + c], 2 * x_chip + c, (yx, yy, c)) for t in range(n)]]
        return rc, local, sends, passed, relays, chips, chip, c, sib

    def start(ins, outs, sems):
        _, local, sends, _, _, _, _, _, _ = copies(ins, outs, sems)
        for cp in local + sends:
            cp.start()

    def middle(ins, outs, sems):
        rc, _, _, passed, relays, chips, _, c, sib = copies(ins, outs, sems)
        for j in range(2):
            for t in range(n):
                rc(1 + j, t, ins[t], 2 * chips[j][2] + c, sib).wait_recv()
        for j in range(2):
            for cp in passed[j]:
                cp.start()

            @pl.when(c == j)
            def _():
                for cp in relays[j]:
                    cp.start()

    def finish(ins, outs, sems):
        rc, local, sends, passed, relays, chips, chip, c, sib = copies(ins, outs, sems)
        far = 2 * chips[2][2] + c
        for t in range(n):
            rc(3, t, ins[t], far, sib).wait_recv()
        for cp in passed[2]:
            cp.start()
        for t in range(n):
            rc(0, t, ins[t], 2 * chip + 1 - c, sib).wait_recv()
        for j, (px, py, pc) in enumerate(chips):
            for t in range(n):
                rc(4 + j, t, ins[t], 2 * pc + 1 - c, sib).wait_recv()
        for cp in sends + passed[0] + passed[1] + passed[2]:
            cp.wait_send()
        for j in range(2):
            @pl.when(c == j)
            def _():
                for cp in relays[j]:
                    cp.wait_send()
        for cp in local:
            cp.wait()

    return _Plan(list(shards), [SDS((N_SLOT,) + tuple(a.shape), a.dtype) for a in shards],
                 [pltpu.SemaphoreType.DMA((7 * n,)), pltpu.SemaphoreType.DMA((7 * n,)),
                  pltpu.SemaphoreType.DMA((n,))], start, finish, middle, middle_at)


def _sibling_plan(grads, whole=()):
    n, m = len(grads), len(whole)

    def copies(ins, outs, sems):
        send_sems, recv_sems = sems
        x, y, c = _place()
        sib = (x, y, 1 - c)

        def rc(t, src, dst):
            return pltpu.make_async_remote_copy(src_ref=src, dst_ref=dst, send_sem=send_sems.at[t],
                                                recv_sem=recv_sems.at[t], device_id=sib, device_id_type=MESH)
        return rc, c

    def start(ins, outs, sems):
        rc, c = copies(ins, outs, sems)
        for t in range(n):
            for j in range(4):
                rc(t, ins[t].at[2 * j + 1 - c], outs[t].at[j]).start()
        for t in range(n, n + m):
            rc(t, ins[t], outs[t]).start()

    def finish(ins, outs, sems):
        rc, _ = copies(ins, outs, sems)
        for t in range(n):
            rc(t, ins[t].at[pl.ds(0, 4)], outs[t]).wait()
        for t in range(n, n + m):
            rc(t, ins[t], outs[t]).wait()

    return _Plan(list(grads) + list(whole),
                 [SDS((4,) + tuple(g.shape[1:]), g.dtype) for g in grads] + [SDS(a.shape, a.dtype) for a in whole],
                 [pltpu.SemaphoreType.DMA((n + m,)), pltpu.SemaphoreType.DMA((n + m,))], start, finish)


def _chips_plan(parts, whole=()):
    n, m = len(parts), len(whole)

    def src_of(ins, t, pc):
        return ins[t].at[pc] if t < n else ins[t]

    def local_copies(ins, outs, sems, chip):
        return [pltpu.make_async_copy(src_of(ins, t, chip), outs[t].at[chip], sems[2].at[t]) for t in range(n + m)]

    def start(ins, outs, sems):
        send_sems, recv_sems, _ = sems
        x, y, c = _place()
        chip = 2 * x + y
        for cp in local_copies(ins, outs, sems, chip):
            cp.start()
        for px, py, pc in _other_chips(x, y):
            for t in range(n + m):
                pltpu.make_async_remote_copy(src_ref=src_of(ins, t, pc), dst_ref=outs[t].at[chip],
                                             send_sem=send_sems.at[t], recv_sem=recv_sems.at[t],
                                             device_id=(px, py, c), device_id_type=MESH).start()

    def finish(ins, outs, sems):
        send_sems, recv_sems, _ = sems
        x, y, c = _place()
        for t in range(n + m):
            three = outs[t].at[pl.ds(0, 3)]
            pltpu.make_async_remote_copy(src_ref=three, dst_ref=three, send_sem=send_sems.at[t],
                                         recv_sem=recv_sems.at[t], device_id=(x, y, c), device_id_type=MESH).wait()
        for cp in local_copies(ins, outs, sems, 2 * x + y):
            cp.wait()

    return _Plan(list(parts) + list(whole),
                 [SDS(p.shape, p.dtype) for p in parts] + [SDS((4,) + tuple(a.shape), a.dtype) for a in whole],
                 [pltpu.SemaphoreType.DMA((n + m,)), pltpu.SemaphoreType.DMA((n + m,)),
                  pltpu.SemaphoreType.DMA((n + m,))], start, finish)


def _exchange_plan(arr):
    def peers(x, y, c):
        flip = lambda v, f: 1 - v if f else v
        return [(flip(x, fx), flip(y, fy), flip(c, fc))
                for fx in (0, 1) for fy in (0, 1) for fc in (0, 1) if fx or fy or fc]

    def start(ins, outs, sems):
        x, y, c = _place()
        me = 4 * x + 2 * y + c
        pltpu.make_async_copy(ins[0], outs[0].at[me], sems[2].at[0]).start()
        for to in peers(x, y, c):
            pltpu.make_async_remote_copy(src_ref=ins[0], dst_ref=outs[0].at[me], send_sem=sems[0].at[0],
                                         recv_sem=sems[1].at[0], device_id=to, device_id_type=MESH).start()

    def finish(ins, outs, sems):
        x, y, c = _place()
        seven = outs[0].at[pl.ds(0, 7)]
        pltpu.make_async_remote_copy(src_ref=seven, dst_ref=seven, send_sem=sems[0].at[0], recv_sem=sems[1].at[0],
                                     device_id=(x, y, c), device_id_type=MESH).wait()
        pltpu.make_async_copy(ins[0], outs[0].at[4 * x + 2 * y + c], sems[2].at[0]).wait()

    return _Plan([arr], [SDS((N_SLOT,) + tuple(arr.shape), arr.dtype)],
                 [pltpu.SemaphoreType.DMA((1,)), pltpu.SemaphoreType.DMA((1,)), pltpu.SemaphoreType.DMA((1,))],
                 start, finish)


def _join(*plans):
    def cut(seq, sizes):
        out, at = [], 0
        for k in sizes:
            out.append(seq[at:at + k])
            at += k
        return out

    n_arr = [len(p.arrays) for p in plans]
    n_sem = [len(p.sems) for p in plans]

    def start(ins, outs, sems):
        for p, i, o, s in zip(plans, cut(ins, n_arr), cut(outs, n_arr), cut(sems, n_sem)):
            p.start(i, o, s)

    def finish(ins, outs, sems):
        for p, i, o, s in zip(plans, cut(ins, n_arr), cut(outs, n_arr), cut(sems, n_sem)):
            p.finish(i, o, s)

    def middle(ins, outs, sems):
        for p, i, o, s in zip(plans, cut(ins, n_arr), cut(outs, n_arr), cut(sems, n_sem)):
            if p.middle is not None:
                p.middle(i, o, s)

    return _Plan([a for p in plans for a in p.arrays], [o for p in plans for o in p.out_shape],
                 [s for p in plans for s in p.sems], start, finish,
                 middle if any(p.middle is not None for p in plans) else None,
                 max(p.middle_at for p in plans))


def _run_plan(name, plan):
    k = len(plan.arrays)

    def body(*refs):
        ins, outs, sems = refs[:k], refs[k:2 * k], refs[2 * k:]
        plan.start(ins, outs, sems)
        if plan.middle is not None:
            plan.middle(ins, outs, sems)
        plan.finish(ins, outs, sems)

    return pl.pallas_call(
        body, name=name, in_specs=[ANY] * k, out_specs=[ANY] * k, out_shape=plan.out_shape,
        scratch_shapes=plan.sems, compiler_params=pltpu.CompilerParams(has_side_effects=True),
    )(*plan.arrays)


def _call(body, *, name, grid, in_specs, out_specs, out_shape, scratch_shapes, params, args, comm=None,
          aliases=None, prefetch=()):
    aliases = aliases or {}
    n_pre = len(prefetch)

    def launch(fn, ins_specs, outs_specs, outs_shape, scratch, operands):
        spec = pltpu.PrefetchScalarGridSpec(num_scalar_prefetch=n_pre, grid=grid, in_specs=ins_specs,
                                            out_specs=outs_specs, scratch_shapes=scratch)
        return pl.pallas_call(fn, name=name, grid_spec=spec, out_shape=outs_shape, compiler_params=params,
                              input_output_aliases=aliases)(*prefetch, *[_small_in_hbm(a) for a in operands])

    if comm is None:
        return list(launch(body, in_specs, out_specs, out_shape, scratch_shapes, args)), []
    n_in, n_out, n_scr, k = len(in_specs), len(out_specs), len(scratch_shapes), len(comm.arrays)

    def wrapped(*refs):
        pre, refs = refs[:n_pre], refs[n_pre:]
        ins = refs[:n_in]
        c_in = refs[n_in:n_in + k]
        outs = refs[n_in + k:n_in + k + n_out]
        c_out = refs[n_in + k + n_out:n_in + 2 * k + n_out]
        scr = refs[n_in + 2 * k + n_out:n_in + 2 * k + n_out + n_scr]
        sems = refs[n_in + 2 * k + n_out + n_scr:]
        step, steps = pl.program_id(0), grid[0]
        for d in range(1, len(grid)):
            step, steps = step * grid[d] + pl.program_id(d), steps * grid[d]

        @pl.when(step == 0)
        def _():
            comm.start(c_in, c_out, sems)

        if comm.middle is not None:
            @pl.when(step == (comm.middle_at * steps) // 8)
            def _():
                comm.middle(c_in, c_out, sems)

        body(*pre, *ins, *outs, *scr)

        @pl.when(step == steps - 1)
        def _():
            comm.finish(c_in, c_out, sems)

    res = launch(wrapped, list(in_specs) + [ANY] * k, list(out_specs) + [ANY] * k,
                 list(out_shape) + list(comm.out_shape), list(scratch_shapes) + list(comm.sems),
                 tuple(args) + tuple(comm.arrays))
    return list(res[:n_out]), list(res[n_out:])


def _wgrad_paired(name, a_t, b, core, by_rows=False, comm=None):
    s = b.shape[0]
    m = a_t.shape[0] // N_SLOT if by_rows else a_t.shape[0]
    cols = b.shape[1] if by_rows else b.shape[1] // N_SLOT
    half = N_SLOT // 2

    def owner(k, c):
        return 2 * (k % half) + jnp.where(k < half, 1 - c, c)

    def body(c_ref, a_ref, b_ref, sum_ref, out_s, recv_s, send_sems, recv_sems):
        del c_ref
        k = pl.program_id(0)
        x, y, c = _place()

        def to_sibling(j):
            return pltpu.make_async_remote_copy(src_ref=out_s.at[j % 2], dst_ref=recv_s.at[j],
                                                send_sem=send_sems.at[j], recv_sem=recv_sems.at[j],
                                                device_id=(x, y, 1 - c), device_id_type=MESH)

        @pl.when((k >= 2) & (k < half + 2))
        def _():
            to_sibling(k - 2).wait_send()

        @pl.when(k < half)
        def _():
            out_s[k % 2] = _dot(a_ref[...], b_ref[...]).astype(BF16)
            to_sibling(k).start()

        @pl.when(k >= half)
        def _():
            to_sibling(k - half).wait_recv()
            sum_ref[...] = (_dot(a_ref[...], b_ref[...]) + recv_s[k - half].astype(F32)).astype(BF16)

    if by_rows:
        in_specs = [pl.BlockSpec((m, s), lambda k, c_ref: (owner(k, c_ref[0]), 0)),
                    pl.BlockSpec((s, cols), lambda k, c_ref: (0, 0))]
    else:
        in_specs = [pl.BlockSpec((m, s), lambda k, c_ref: (0, 0)),
                    pl.BlockSpec((s, cols), lambda k, c_ref: (0, owner(k, c_ref[0])))]
    return _call(body, name=name, grid=(N_SLOT,), in_specs=in_specs,
                 out_specs=[pl.BlockSpec((None, m, cols), lambda k, c_ref: (jnp.maximum(k - half, 0), 0, 0))],
                 out_shape=[SDS((half, m, cols), BF16)],
                 scratch_shapes=[pltpu.VMEM((2, m, cols), BF16), pltpu.VMEM((half, m, cols), BF16),
                                 pltpu.SemaphoreType.DMA((half,)), pltpu.SemaphoreType.DMA((half,))],
                 params=_cparams(("arbitrary",), 56), args=(a_t, b), comm=comm, prefetch=(core,))


def _row_tile(rows):
    for t in (512, 256, 128, 64, 32, 16, 8):
        if rows % t == 0:
            return t
    return rows


def _pair_sum(name, g8, recv4, core):
    _, rows, cols = recv4.shape
    tr = _row_tile(rows)
    g42 = g8.reshape(4, 2, rows, cols)

    def body(c_ref, g_ref, r_ref, o_ref):
        del c_ref
        o_ref[...] = (g_ref[...].astype(F32) + r_ref[...].astype(F32)).astype(o_ref.dtype)

    return pl.pallas_call(
        body, name=name,
        grid_spec=pltpu.PrefetchScalarGridSpec(
            num_scalar_prefetch=1, grid=(4, rows // tr),
            in_specs=[pl.BlockSpec((None, None, tr, cols), lambda j, i, c_ref: (j, c_ref[0], i, 0)),
                      pl.BlockSpec((None, tr, cols), lambda j, i, c_ref: (j, i, 0))],
            out_specs=pl.BlockSpec((None, tr, cols), lambda j, i, c_ref: (j, i, 0))),
        out_shape=SDS(recv4.shape, g8.dtype),
        compiler_params=_cparams(("arbitrary", "arbitrary"), 32),
    )(core, g42, recv4)


def _add2(name, a, b):
    rows, cols = a.shape
    tr = _row_tile(rows)

    def body(a_ref, b_ref, o_ref):
        o_ref[...] = a_ref[...] + b_ref[...]

    blk = pl.BlockSpec((tr, cols), lambda i: (i, 0))
    return pl.pallas_call(body, name=name, grid=(rows // tr,), in_specs=[blk, blk], out_specs=blk,
                          out_shape=SDS(a.shape, a.dtype),
                          compiler_params=_cparams(("arbitrary",), 32))(a, b)


def _sum_terms(name, terms):
    k, rows, cols = terms.shape
    tr = _row_tile(rows)

    def body(r_ref, o_ref):
        acc = r_ref[0]
        for q in range(1, k):
            acc = acc + r_ref[q]
        o_ref[...] = acc

    return pl.pallas_call(body, name=name, grid=(rows // tr,),
                          in_specs=[pl.BlockSpec((k, tr, cols), lambda i: (0, i, 0))],
                          out_specs=pl.BlockSpec((tr, cols), lambda i: (i, 0)),
                          out_shape=SDS((rows, cols), terms.dtype),
                          compiler_params=_cparams(("arbitrary",), 32))(terms)


def _adam_update(g, w, m, v):
    c1 = 1.0 / (1.0 - ADAM_B1 ** ADAM_STEP)
    c2 = 1.0 / (1.0 - ADAM_B2 ** ADAM_STEP)
    mn = ADAM_B1 * m + (1.0 - ADAM_B1) * g
    vn = ADAM_B2 * v + (1.0 - ADAM_B2) * (g * g)
    delta = (-ADAM_LR) * ((mn * c1) / (jnp.sqrt(vn * c2) + ADAM_EPS) + ADAM_WD * w)
    return delta, mn, vn


def _adamw_many(name, gs, ws, ms, vs):
    n = len(gs)

    def body(*refs):
        for p in range(n):
            g, w, m, v = (refs[q * n + p][...] for q in range(4))
            d, mn, vn = _adam_update(g, w, m, v)
            refs[4 * n + p][...] = d
            refs[5 * n + p][...] = mn
            refs[6 * n + p][...] = vn

    full = [pl.BlockSpec(w.shape, lambda i: (0, 0)) for w in ws]
    shapes = [SDS(w.shape, F32) for w in ws]
    res = pl.pallas_call(body, name=name, grid=(1,), in_specs=full * 4, out_specs=full * 3, out_shape=shapes * 3,
                         compiler_params=_cparams(("arbitrary",), 32),
                         )(*[_small_in_hbm(a) for a in (*gs, *ws, *ms, *vs)])
    return [(res[p], res[n + p], res[2 * n + p]) for p in range(n)]


def _adamw(name, terms, w, m, v):
    k, rows, cols = terms.shape
    tr = _row_tile(rows)

    def body(t_ref, w_ref, m_ref, v_ref, g_ref, d_ref, mo_ref, vo_ref):
        g = t_ref[0].astype(F32)
        for q in range(1, k):
            g = g + t_ref[q].astype(F32)
        g_ref[...] = g
        d_ref[...], mo_ref[...], vo_ref[...] = _adam_update(g, w_ref[...], m_ref[...], v_ref[...])

    blk = pl.BlockSpec((tr, cols), lambda i: (i, 0))
    return pl.pallas_call(body, name=name, grid=(rows // tr,),
                          in_specs=[pl.BlockSpec((k, tr, cols), lambda i: (0, i, 0)), blk, blk, blk],
                          out_specs=[blk] * 4, out_shape=[SDS((rows, cols), F32)] * 4,
                          compiler_params=_cparams(("arbitrary",), 40),
                          )(*[pltpu.with_memory_space_constraint(a, pltpu.HBM) for a in (terms, w, m, v)])


def kernel(x, norm_mix_g, w_in, conv_w, conv_b, w_rgate, b_rgate, w_igate, b_igate, lru_lambda, w_out_a, sgu_ln_g, sgu_ln_b, sgu_w_s, sgu_b_s, w_out_b, w_out, norm_mlp_g, w_up, w_down, norm_final_g, loss_target, m_norm_mix_g, m_w_in, m_conv_w, m_conv_b, m_w_rgate, m_b_rgate, m_w_igate, m_b_igate, m_lru_lambda, m_w_out_a, m_sgu_ln_g, m_sgu_ln_b, m_sgu_w_s, m_sgu_b_s, m_w_out_b, m_w_out, m_norm_mlp_g, m_w_up, m_w_down, m_norm_final_g, v_norm_mix_g, v_w_in, v_conv_w, v_conv_b, v_w_rgate, v_b_rgate, v_w_igate, v_b_igate, v_lru_lambda, v_w_out_a, v_sgu_ln_g, v_sgu_ln_b, v_sgu_w_s, v_sgu_b_s, v_w_out_b, v_w_out, v_norm_mlp_g, v_w_up, v_w_down, v_norm_final_g):
    cx, cy, cc = _place()
    me = 4 * cx + 2 * cy + cc
    core = jnp.reshape(cc, (1,)).astype(jnp.int32)
    xs = x[0]
    tgt = loss_target[0]

    gate_shard = jnp.stack([w_rgate[0], w_igate[0]]).astype(BF16).reshape(2 * HEADS * 32, HEAD_DIM)
    vec_shard = jnp.concatenate([conv_w[0], b_rgate[0], b_igate[0]], axis=1)
    vec_shard = jnp.pad(vec_shard, ((0, 4), (0, 256 - vec_shard.shape[1])))
    shards = [w_in[0].astype(BF16), w_out_a[0].astype(BF16), w_out_b[0].astype(BF16), w_out[0].astype(BF16),
              w_up[0].astype(BF16), w_down[0].astype(BF16), gate_shard, vec_shard]
    (z, n1_t, w_in_g), (gate_g, vec_g) = _in_proj(xs, norm_mix_g, shards[0], _slot_order(cx, cy, cc),
                                                comm=_gather_plan(shards[6:8]))
    gates = gate_g.reshape(N_SLOT, 2, HEADS, 32, HEAD_DIM).transpose(1, 2, 0, 3, 4).reshape(2, HEADS, HEAD_DIM, HEAD_DIM)
    w_r_f, w_i_f = gates[0], gates[1]
    conv_w_f = vec_g[:, 0:4, 0:128].transpose(1, 0, 2).reshape(CONV_K, D)
    b_r_f = vec_g[:, 0:4, 128:160].transpose(1, 0, 2).reshape(1, D)
    b_i_f = vec_g[:, 0:4, 160:192].transpose(1, 0, 2).reshape(1, D)
    b_s_t = jnp.transpose(sgu_b_s[0])

    gating = (sgu_ln_g, sgu_ln_b, sgu_w_s[0], b_s_t)
    (ya, hs, xc, r_gate, i_gate, yb_lo), (w_oa_g, w_ob_g, w_out_g, w_up_g) = _branch_a_fwd(
        z, conv_w_f, conv_b, w_r_f, b_r_f, w_i_f, b_i_f, lru_lambda, gating, comm=_gather_plan(shards[1:5]))
    w_oa_f = w_oa_g.reshape(D, D)
    w_ob_f = w_ob_g.reshape(D, D)
    w_out_f = w_out_g.reshape(D, D)
    (yb, pa, pb, merged, h1), (w_down_g,) = _branch_b_merge_out(
        ya, yb_lo, z, xs, *gating, w_oa_f, w_ob_f, w_out_f, comm=_gather_plan(shards[5:6], middle_at=4))
    w_down_f = w_down_g.reshape(N_SLOT * FF_COLS, D)
    r_act, act_t, n2_t, dh2, dh2b, loss_acc, d_gfin = _mlp_fwd(h1, norm_mlp_g, w_up_g, w_down_f,
                                                               norm_final_g.reshape(1, D), tgt)

    def pair(names, grads, recv):
        return [_pair_sum("pair_sum_" + nm, g, r, core) for nm, g, r in zip(names, grads, recv)]

    (p_down,), _ = _wgrad_paired("wgrad_down", act_t, dh2b, core, by_rows=True)
    (df, dh1, d_gmlp), (got_down,) = _mlp_bwd(dh2, dh2b, r_act, w_down_f, w_up_g, h1, norm_mlp_g,
                                              comm=_chips_plan([p_down]))
    (p_up,), _ = _wgrad_paired("wgrad_up", n2_t, df, core)
    g_out = _wgrad("wgrad_out", merged, dh1).reshape(N_SLOT, D // N_SLOT, D)
    (dz, dpa, dpb, dya, dyb), (r_out,) = _merge_bwd(
        dh1, z, pa, pb, w_out_f, w_oa_f, w_ob_f, comm=_sibling_plan([g_out]))
    (p_out,) = pair(["out"], [g_out], [r_out])
    g_oa = _wgrad("wgrad_out_a", ya, dpa).reshape(N_SLOT, D // N_SLOT, D)
    g_ob = _wgrad("wgrad_out_b", yb, dpb).reshape(N_SLOT, D // N_SLOT, D)
    (dz, d_ws, d_bs, d_ln), (got_out, r_oa, r_ob) = _branch_b_bwd(
        dz, dyb, z, sgu_ln_g, sgu_ln_b, sgu_w_s[0], b_s_t,
        comm=_join(_chips_plan([p_out]), _sibling_plan([g_oa, g_ob])))
    p_oa, p_ob = pair(["out_a", "out_b"], [g_oa, g_ob], [r_oa, r_ob])
    (dz, d_vec, d_wr, d_wi), (got_up, got_oa, got_ob) = _branch_a_bwd(
        dz, dya, z, hs, xc, r_gate, i_gate, conv_w_f, w_r_f, w_i_f, lru_lambda,
        comm=_chips_plan([p_up, p_oa, p_ob]))
    g_gate = jnp.stack([d_wr, d_wi]).reshape(2, HEADS, N_SLOT, 32, HEAD_DIM).transpose(2, 0, 1, 3, 4)
    g_gate = g_gate.reshape(N_SLOT, 2 * HEADS * 32, HEAD_DIM).astype(BF16)

    d_bs_row = jnp.pad(d_bs[:, :, 0].reshape(1, GROUPS * CHUNK), ((0, 0), (0, D - GROUPS * CHUNK)))
    vecs = jnp.concatenate([d_vec, jnp.concatenate([d_ln[0:2], d_gmlp, d_gfin, d_bs_row, jnp.zeros((3, D), F32)])])
    d_ws2 = d_ws.reshape(GROUPS * CHUNK, CHUNK)
    (p_in,), (r_gate, r_vecs, r_ws) = _wgrad_paired("wgrad_in", n1_t, dz, core,
                                                    comm=_sibling_plan([g_gate], [vecs, d_ws2]))
    (p_gate,) = pair(["gate"], [g_gate], [r_gate])
    vecs_chip = _add2("pair_sum_vecs", vecs, r_vecs)
    ws_chip = _add2("pair_sum_ws", d_ws2, r_ws)
    (dx, d_gmix), (got_in, got_gate, got_vecs, got_ws) = _in_bwd(
        dz, w_in_g, xs, dh1, norm_mix_g, comm=_chips_plan([p_in, p_gate], [vecs_chip, ws_chip]))
    vecs_sum = _sum_terms("sum_vecs", got_vecs)
    last = jnp.concatenate([d_gmix, jnp.pad(loss_acc[0:1], ((0, 0), (0, D - 128))), jnp.zeros((6, D), F32)])
    (last_all,) = _run_plan("exchange_last", _exchange_plan(last))
    last_sum = _sum_terms("sum_last", last_all)
    loss = last_sum[1, 0]
    got = [got_in, got_oa, got_ob, got_out, got_up, got_down, got_gate]

    def step(nm, terms, w, m, v, rows, cols):
        g, d, mn, vn = _adamw("adamw_" + nm, terms.reshape(4, rows, cols), w.reshape(rows, cols),
                              m.reshape(rows, cols), v.reshape(rows, cols))
        return [a.reshape(w.shape) for a in (g, d, mn, vn)]

    o_in = step("in", got[0], w_in, m_w_in, v_w_in, D, W_IN_COLS)
    o_oa = step("out_a", got[1], w_out_a, m_w_out_a, v_w_out_a, D // N_SLOT, D)
    o_ob = step("out_b", got[2], w_out_b, m_w_out_b, v_w_out_b, D // N_SLOT, D)
    o_out = step("out", got[3], w_out, m_w_out, v_w_out, D // N_SLOT, D)
    o_up = step("up", got[4], w_up, m_w_up, v_w_up, D, FF_COLS)
    o_down = step("down", got[5], w_down, m_w_down, v_w_down, FF_COLS, D)
    gate_w = jnp.stack([w_rgate[0], w_igate[0]]).reshape(2 * HEADS * 32, HEAD_DIM)
    gate_m = jnp.stack([m_w_rgate[0], m_w_igate[0]]).reshape(2 * HEADS * 32, HEAD_DIM)
    gate_v = jnp.stack([v_w_rgate[0], v_w_igate[0]]).reshape(2 * HEADS * 32, HEAD_DIM)
    o_gate = _adamw("adamw_gate", got[6], gate_w, gate_m, gate_v)
    o_gate = [a.reshape(2, 1, HEADS, 32, HEAD_DIM) for a in o_gate]
    o_wr = [a[0] for a in o_gate]
    o_wi = [a[1] for a in o_gate]

    def own(full, width):
        return lax.dynamic_slice_in_dim(full, me * width, width, axis=1)

    small_g = {
        "norm_mix_g": last_sum[0:1], "conv_w": own(vecs_sum[0:4], 128), "conv_b": vecs_sum[4:5],
        "b_rgate": own(vecs_sum[5:6].reshape(HEADS, HEAD_DIM), 32),
        "b_igate": own(vecs_sum[6:7].reshape(HEADS, HEAD_DIM), 32),
        "lru_lambda": vecs_sum[7:8], "sgu_ln_g": vecs_sum[8:9], "sgu_ln_b": vecs_sum[9:10],
        "norm_mlp_g": vecs_sum[10:11], "norm_final_g": vecs_sum[11:12],
        "sgu_b_s": vecs_sum[12, 0:GROUPS * CHUNK].reshape(GROUPS, CHUNK),
    }
    small_w = {"norm_mix_g": (norm_mix_g, m_norm_mix_g, v_norm_mix_g), "conv_w": (conv_w, m_conv_w, v_conv_w),
               "conv_b": (conv_b, m_conv_b, v_conv_b), "b_rgate": (b_rgate, m_b_rgate, v_b_rgate),
               "b_igate": (b_igate, m_b_igate, v_b_igate), "lru_lambda": (lru_lambda, m_lru_lambda, v_lru_lambda),
               "sgu_ln_g": (sgu_ln_g, m_sgu_ln_g, v_sgu_ln_g), "sgu_ln_b": (sgu_ln_b, m_sgu_ln_b, v_sgu_ln_b),
               "norm_mlp_g": (norm_mlp_g, m_norm_mlp_g, v_norm_mlp_g),
               "norm_final_g": (norm_final_g, m_norm_final_g, v_norm_final_g),
               "sgu_b_s": (sgu_b_s, m_sgu_b_s, v_sgu_b_s), "sgu_w_s": (sgu_w_s, m_sgu_w_s, v_sgu_w_s)}
    order = list(small_g)
    as2d = lambda k, a: a.reshape(small_g[k].shape)
    upd = _adamw_many("adamw_small", [small_g[k] for k in order], *[[as2d(k, small_w[k][q]) for k in order]
                                                                     for q in range(3)])
    o_small = {k: [a.reshape(small_w[k][0].shape) for a in (small_g[k],) + u] for k, u in zip(order, upd)}
    ws3 = [a[0].reshape(GROUPS * CHUNK, CHUNK) for a in small_w.pop("sgu_w_s")]
    o_small["sgu_w_s"] = [a.reshape(sgu_w_s.shape) for a in _adamw("adamw_ws", got_ws, *ws3)]

    per_weight = {"norm_mix_g": o_small["norm_mix_g"], "w_in": o_in, "conv_w": o_small["conv_w"],
                  "conv_b": o_small["conv_b"], "w_rgate": o_wr, "b_rgate": o_small["b_rgate"], "w_igate": o_wi,
                  "b_igate": o_small["b_igate"], "lru_lambda": o_small["lru_lambda"], "w_out_a": o_oa,
                  "sgu_ln_g": o_small["sgu_ln_g"], "sgu_ln_b": o_small["sgu_ln_b"], "sgu_w_s": o_small["sgu_w_s"],
                  "sgu_b_s": o_small["sgu_b_s"], "w_out_b": o_ob, "w_out": o_out, "norm_mlp_g": o_small["norm_mlp_g"],
                  "w_up": o_up, "w_down": o_down, "norm_final_g": o_small["norm_final_g"]}
    names_w = list(per_weight)
    return (loss, dx[None], *[per_weight[k][0] for k in names_w], *[per_weight[k][1] for k in names_w],
            *[per_weight[k][2] for k in names_w], *[per_weight[k][3] for k in names_w])
```

```python
import jax
import jax.numpy as jnp
from jax import lax
from jax.experimental import pallas as pl
from jax.experimental.pallas import tpu as pltpu

F32 = jnp.float32
BF16 = jnp.bfloat16
SDS = jax.ShapeDtypeStruct
MESH = pl.DeviceIdType.MESH
ANY = pl.BlockSpec(memory_space=pl.ANY)

D = 1024
N_SLOT = 8
W_IN_COLS = 768
FF_COLS = 512
HEADS, HEAD_DIM = 4, 256
GROUPS, GROUP_DIM = 4, 256
CHUNK = 128
CONV_K = 4
NORM_EPS = 1e-6
LN_EPS = 1e-5
LRU_C = 8.0
ADAM_LR, ADAM_B1, ADAM_B2, ADAM_EPS, ADAM_WD, ADAM_STEP = 0.001, 0.9, 0.999, 1e-08, 0.01, 10

TM_ROWS = 1024
TM_MERGE = 512
T_BRANCH_A = 512
T_BRANCH_B = 256
MiB = 1024 * 1024
SMALL_OPERAND = 16 * 1024

_GELU_C = 0.7978845608028654
_GELU_A = 0.044715


def _small_in_hbm(a):
    return pltpu.with_memory_space_constraint(a, pltpu.HBM) if a.size <= SMALL_OPERAND else a


def _cparams(sem, vmem_mib):
    return pltpu.CompilerParams(dimension_semantics=sem, vmem_limit_bytes=vmem_mib * MiB)


def _gelu(x):
    t = jnp.tanh(_GELU_C * (x + _GELU_A * x * x * x))
    return 0.5 * x * (1.0 + t)


def _gelu_and_grad(x):
    x2 = x * x
    t = jnp.tanh(_GELU_C * x * (1.0 + _GELU_A * x2))
    g = 0.5 * x * (1.0 + t)
    dg = 0.5 * (1.0 + t) + 0.5 * x * (1.0 - t * t) * _GELU_C * (1.0 + 3.0 * _GELU_A * x2)
    return g, dg


def _softplus(x):
    return jnp.maximum(x, 0.0) + jnp.log1p(jnp.exp(-jnp.abs(x)))


def _dot(a, b):
    return jnp.dot(a, b, preferred_element_type=F32)


def _dot_nt(a, b):
    return lax.dot_general(a, b, (((1,), (1,)), ((), ())), preferred_element_type=F32)


def _dot_tn(a, b):
    return lax.dot_general(a, b, (((0,), (0,)), ((), ())), preferred_element_type=F32)


def _rows_shifted(prev8, cur, k):
    ext = jnp.concatenate([prev8, cur], axis=0)
    return pltpu.roll(ext, k, 0)[8:]


def _rows_advanced(cur, next8, k):
    t = cur.shape[0]
    ext = jnp.concatenate([cur, next8], axis=0)
    return pltpu.roll(ext, t + 8 - k, 0)[:t]


def _first_second(x, y, c):
    ny, nx, far = _other_chips(x, y)
    pick = lambda a, b: a * (1 - c) + b * c
    first = tuple(pick(a, b) for a, b in zip(ny, nx))
    second = tuple(pick(b, a) for a, b in zip(ny, nx))
    return first, second, far


def _slot_order(x, y, c):
    chip = 2 * x + y
    first, second, far = _first_second(x, y, c)
    order = [2 * chip + c, 2 * chip + 1 - c, 2 * first[2] + c, 2 * second[2] + 1 - c, 2 * second[2] + c,
             2 * first[2] + 1 - c, 2 * far[2] + c, 2 * far[2] + 1 - c]
    return jnp.stack(order).astype(jnp.int32)


def _in_proj(x, g_mix, w_in_own, order, comm=None):
    s = x.shape[0]
    tm = min(TM_ROWS, s)
    ni = s // tm

    def body(order_ref, x_ref, g_ref, own_ref, z_ref, nt_ref, wg_ref, n_s, w_s, send_sems, recv_sems, local_sems):
        j, i = pl.program_id(0), pl.program_id(1)
        px, py, c = _place()
        chip = 2 * px + py
        me = 2 * chip + c
        sib = (px, py, 1 - c)
        chips = _other_chips(px, py)

        def rc(k, src, blk, to):
            return pltpu.make_async_remote_copy(src_ref=src, dst_ref=w_s.at[blk], send_sem=send_sems.at[k],
                                                recv_sem=recv_sems.at[k], device_id=to, device_id_type=MESH)

        del chips
        first, second, far = _first_second(px, py, c)
        blocks = [2 * first[2] + c, 2 * second[2] + c, 2 * far[2] + c]
        own_in = pltpu.make_async_copy(own_ref, w_s.at[me], local_sems.at[0])
        to_first = rc(1, own_ref, me, (first[0], first[1], c))
        to_second = rc(2, own_ref, me, (second[0], second[1], c))
        relay = rc(3, w_s.at[blocks[0]], blocks[0], (second[0], second[1], c))
        sends = [rc(0, own_ref, me, sib), to_first, to_second, relay]
        passed = [rc(4 + q, w_s.at[blk], blk, sib) for q, blk in enumerate(blocks)]
        keep = pltpu.make_async_copy(w_s, wg_ref, local_sems.at[1])

        @pl.when((i == 0) & (j == 0))
        def _():
            own_in.start()
            sends[0].start()
            to_first.start()
            own_in.wait()

        @pl.when((i == 0) & (j == 1))
        def _():
            rc(0, own_ref, 2 * chip + 1 - c, sib).wait_recv()

        for q, blk in enumerate(blocks):
            @pl.when((i == 0) & (j == 2 + 2 * q))
            def _():
                rc(1 + q, own_ref, blk, sib).wait_recv()
                passed[q].start()
                if q == 0:
                    to_second.start()
                    relay.start()

            @pl.when((i == 0) & (j == 3 + 2 * q))
            def _():
                rc(4 + q, own_ref, order_ref[j], sib).wait_recv()

        rows = pl.ds(pl.multiple_of(i * tm, tm), tm)

        @pl.when(j == 0)
        def _():
            xv = x_ref[...]
            rstd = lax.rsqrt(jnp.mean(xv * xv, axis=-1, keepdims=True) + NORM_EPS)
            nb = (xv * rstd * g_ref[...]).astype(BF16)
            n_s[rows, :] = nb
            nt_ref[...] = nb.T

        z_ref[...] = _dot(n_s[rows, :], w_s[order_ref[j]]).astype(BF16)

        @pl.when((i == 0) & (j == N_SLOT - 1))
        def _():
            keep.start()

        @pl.when((i == ni - 1) & (j == N_SLOT - 1))
        def _():
            for cp in sends + passed:
                cp.wait_send()
            keep.wait()

    first_pass = lambda j, i, o: (jnp.where(j == 0, i, ni - 1), 0)
    (z, n1, w_in_g), extra = _call(
        body, name="in_proj", grid=(N_SLOT, ni), prefetch=(order,),
        in_specs=[pl.BlockSpec((tm, D), first_pass),
                  pl.BlockSpec((1, D), lambda j, i, o: (0, 0)), ANY],
        out_specs=[pl.BlockSpec((tm, W_IN_COLS), lambda j, i, o: (i, o[j])),
                   pl.BlockSpec((D, tm), lambda j, i, o: (0, jnp.where(j == 0, i, ni - 1))), ANY],
        out_shape=[SDS((s, N_SLOT * W_IN_COLS), BF16), SDS((D, s), BF16), SDS((N_SLOT, D, W_IN_COLS), BF16)],
        scratch_shapes=[pltpu.VMEM((s, D), BF16), pltpu.VMEM((N_SLOT, D, W_IN_COLS), BF16),
                        pltpu.SemaphoreType.DMA((7,)), pltpu.SemaphoreType.DMA((7,)), pltpu.SemaphoreType.DMA((2,))],
        params=_cparams(("arbitrary", "arbitrary"), 56), args=(x, g_mix, w_in_own), comm=comm)
    return (z, n1, w_in_g), extra


def _decay(r, sp_lam):
    log_a = (-LRU_C) * r * sp_lam
    a = jnp.exp(log_a)
    return a, jnp.sqrt(-jnp.tanh(log_a) * (a * a + 1.0))


def _lru_gates(xc, xcb, wr_ref, br, wi_ref, bi, sp_lam, a_s, b_s, r_ref, i_ref):
    for h in range(HEADS):
        sl = slice(h * HEAD_DIM, (h + 1) * HEAD_DIM)
        r = jax.nn.sigmoid(_dot(xcb[:, sl], wr_ref[h]) + br[:, sl])
        ig = jax.nn.sigmoid(_dot(xcb[:, sl], wi_ref[h]) + bi[:, sl])
        a, mult = _decay(r, sp_lam[:, sl])
        a_s[:, sl] = a
        b_s[:, sl] = xc[:, sl] * ig * mult
        r_ref[:, sl] = r.astype(BF16)
        i_ref[:, sl] = ig.astype(BF16)


def _conv_fwd(xa, prev8, cw, cb):
    xc = cb + cw[0:1, :] * xa
    for k in range(1, CONV_K):
        xc = xc + cw[k:k + 1, :] * _rows_shifted(prev8, xa, k)
    return xc


def _branch_a_fwd(z, conv_w, conv_b, w_r, b_r, w_i, b_i, lam, comm=None):
    s = z.shape[0]
    ta = min(T_BRANCH_A, s)
    per16 = ta // 16

    def body(xa_ref, xp_ref, ga_ref, cw_ref, cb_ref, wr_ref, br_ref, wi_ref, bi_ref, lam_ref,
             ya_ref, hs_ref, xc_ref, r_ref, i_ref, a_s, b_s, h_s, carry_s):
        i = pl.program_id(0)

        @pl.when(i == 0)
        def _():
            carry_s[...] = jnp.zeros_like(carry_s)

        xa = xa_ref[...].astype(F32)
        prev8 = jnp.where(i > 0, xp_ref[...].astype(F32)[8:16], 0.0)
        xc = _conv_fwd(xa, prev8, cw_ref[...], cb_ref[...])
        xcb = xc.astype(BF16)
        xc_ref[...] = xcb
        sp_lam = _softplus(-lam_ref[...])
        _lru_gates(xc, xcb, wr_ref, br_ref[...], wi_ref, bi_ref[...], sp_lam, a_s, b_s, r_ref, i_ref)

        row = lax.broadcasted_iota(jnp.int32, (8, D), 0)

        def group(g, carry):
            off = pl.multiple_of(g * 8, 8)
            a8 = a_s[pl.ds(off, 8), :]
            b8 = b_s[pl.ds(off, 8), :]
            for d in (1, 2, 4):
                a_sh = jnp.where(row >= d, pltpu.roll(a8, d, 0), 1.0)
                b_sh = jnp.where(row >= d, pltpu.roll(b8, d, 0), 0.0)
                b8 = a8 * b_sh + b8
                a8 = a8 * a_sh
            h8 = b8 + a8 * carry
            h_s[pl.ds(off, 8), :] = h8
            return jnp.broadcast_to(h8[7:8, :], (8, D))

        carry_s[...] = lax.fori_loop(0, ta // 8, group, carry_s[...])
        hs = h_s[...]
        hs_ref[...] = hs.astype(BF16)
        ya_ref[...] = (hs * _gelu(ga_ref[...].astype(F32))).astype(BF16)

    vec = pl.BlockSpec((1, D), lambda i: (0, 0))
    gate = pl.BlockSpec((HEADS, HEAD_DIM, HEAD_DIM), lambda i: (0, 0, 0))
    return _call(
        body, name="branch_a_fwd", grid=(s // ta,),
        in_specs=[pl.BlockSpec((ta, D), lambda i: (i, 0)),
                  pl.BlockSpec((16, D), lambda i: (jnp.maximum(i * per16 - 1, 0), 0)),
                  pl.BlockSpec((ta, D), lambda i: (i, 1)),
                  pl.BlockSpec((CONV_K, D), lambda i: (0, 0)), vec, gate, vec, gate, vec, vec],
        out_specs=[pl.BlockSpec((ta, D), lambda i: (i, 0))] * 5,
        out_shape=[SDS((s, D), BF16)] * 5,
        scratch_shapes=[pltpu.VMEM((ta, D), F32), pltpu.VMEM((ta, D), F32), pltpu.VMEM((ta, D), F32),
                        pltpu.VMEM((8, D), F32)],
        params=_cparams(("arbitrary",), 40), args=(z, z, z, conv_w, conv_b, w_r, b_r, w_i, b_i, lam), comm=comm)


def _sgu_common(ub, vb, lg, lb, with_grad):
    if with_grad:
        u, du = _gelu_and_grad(ub)
        v, dv = _gelu_and_grad(vb)
    else:
        u, v, du, dv = _gelu(ub), _gelu(vb), None, None
    mu = jnp.mean(v, axis=-1, keepdims=True)
    vc = v - mu
    rstd = lax.rsqrt(jnp.mean(vc * vc, axis=-1, keepdims=True) + LN_EPS)
    vhat = vc * rstd
    vln = vhat * lg + lb
    return u, du, dv, rstd, vhat, vln


def _masked_ws(ws_ref):
    t = lax.broadcasted_iota(jnp.int32, (CHUNK, CHUNK), 0)
    c = lax.broadcasted_iota(jnp.int32, (CHUNK, CHUNK), 1)
    keep = c <= t
    return [jnp.where(keep, ws_ref[g], 0.0).astype(BF16) for g in range(GROUPS)]


def _branch_b_merge_out(ya, z, x, ln_g, ln_b, w_s, b_s_t, w_oa, w_ob, w_out, comm=None):
    s = x.shape[0]
    tm = min(TM_MERGE, s)

    def body(ub_ref, vb_ref, lg_ref, lb_ref, ws_ref, bs_ref, ya_ref, ma_ref, mb_ref, x_ref, woa_ref, wob_ref, wo_ref,
             yb_ref, pa_ref, pb_ref, mg_ref, h1_ref):
        u, _, _, _, _, vln = _sgu_common(ub_ref[...].astype(F32), vb_ref[...].astype(F32),
                                         lg_ref[...], lb_ref[...], False)
        vlnb = vln.astype(BF16)
        wm = _masked_ws(ws_ref)
        bs = bs_ref[...]
        for c in range(tm // CHUNK):
            rs = slice(c * CHUNK, (c + 1) * CHUNK)
            for g in range(GROUPS):
                cs = slice(g * GROUP_DIM, (g + 1) * GROUP_DIM)
                sp = _dot(wm[g], vlnb[rs, cs]) + bs[:, g:g + 1]
                yb_ref[rs, cs] = (u[rs, cs] * sp).astype(BF16)

        pa = _dot(ya_ref[...], woa_ref[...])
        pb = _dot(yb_ref[...], wob_ref[...])
        merged = (jax.nn.sigmoid(ma_ref[...].astype(F32)) * pa
                  + jax.nn.sigmoid(mb_ref[...].astype(F32)) * pb).astype(BF16)
        pa_ref[...] = pa.astype(BF16)
        pb_ref[...] = pb.astype(BF16)
        mg_ref[...] = merged
        h1_ref[...] = x_ref[...] + _dot(merged, wo_ref[...])

    row = pl.BlockSpec((tm, D), lambda i: (i, 0))
    col = lambda q: pl.BlockSpec((tm, D), lambda i: (i, q))
    vec = pl.BlockSpec((1, D), lambda i: (0, 0))
    wsp = pl.BlockSpec((D, D), lambda i: (0, 0))
    return _call(
        body, name="branch_b_merge_out", grid=(s // tm,),
        in_specs=[col(2), col(3), vec, vec, pl.BlockSpec((GROUPS, CHUNK, CHUNK), lambda i: (0, 0, 0)),
                  pl.BlockSpec((CHUNK, GROUPS), lambda i: (0, 0)), row, col(4), col(5), row, wsp, wsp, wsp],
        out_specs=[row, row, row, row, row],
        out_shape=[SDS((s, D), BF16)] * 4 + [SDS((s, D), F32)], scratch_shapes=[],
        params=_cparams(("arbitrary",), 56),
        args=(z, z, ln_g, ln_b, w_s, b_s_t, ya, z, z, x, w_oa, w_ob, w_out), comm=comm)


def _mlp_fwd(h1, g_mlp, w_up_g, w_down, g_fin, tgt):
    s = h1.shape[0]
    tm = min(TM_ROWS, s)
    nj = N_SLOT

    def body(h1_ref, gm_ref, wu_ref, wd_ref, gf_ref, t_ref, r_ref, at_ref, n2t_ref, dh2_ref, dh2b_ref, loss_ref,
             dgf_ref, n2_s, acc_s):
        i, j = pl.program_id(0), pl.program_id(1)

        @pl.when(j == 0)
        def _():
            hv = h1_ref[...]
            rstd = lax.rsqrt(jnp.mean(hv * hv, axis=-1, keepdims=True) + NORM_EPS)
            nb = (hv * rstd * gm_ref[...]).astype(BF16)
            n2_s[...] = nb
            n2t_ref[...] = nb.T
            acc_s[...] = jnp.zeros_like(acc_s)

        @pl.when((i == 0) & (j == 0))
        def _():
            loss_ref[...] = jnp.zeros_like(loss_ref)
            dgf_ref[...] = jnp.zeros_like(dgf_ref)

        r = jnp.maximum(_dot(n2_s[...], wu_ref[...]), 0.0)
        r_ref[...] = r.astype(BF16)
        act = (r * r).astype(BF16)
        at_ref[...] = act.T
        acc_s[...] += _dot(act, wd_ref[...])

        @pl.when(j == nj - 1)
        def _():
            h2 = h1_ref[...] + acc_s[...]
            rstd = lax.rsqrt(jnp.mean(h2 * h2, axis=-1, keepdims=True) + NORM_EPS)
            hh = h2 * rstd
            gf = gf_ref[...]
            e = hh * gf - t_ref[...]
            loss_ref[...] += jnp.sum(e * e) * (0.5 / D)
            dy = e * (1.0 / D)
            dgf_ref[...] += jnp.sum(dy * hh, axis=0, keepdims=True)
            dhh = dy * gf
            dh2 = rstd * (dhh - hh * jnp.mean(dhh * hh, axis=-1, keepdims=True))
            dh2_ref[...] = dh2
            dh2b_ref[...] = dh2.astype(BF16)

    row = pl.BlockSpec((tm, D), lambda i, j: (i, 0))
    vec = pl.BlockSpec((1, D), lambda i, j: (0, 0))
    return pl.pallas_call(
        body, name="mlp_fwd", grid=(s // tm, nj),
        in_specs=[row, vec, pl.BlockSpec((None, D, FF_COLS), lambda i, j: (j, 0, 0)),
                  pl.BlockSpec((FF_COLS, D), lambda i, j: (j, 0)), vec, row],
        out_specs=[pl.BlockSpec((tm, FF_COLS), lambda i, j: (i, j)), pl.BlockSpec((FF_COLS, tm), lambda i, j: (j, i)),
                   pl.BlockSpec((D, tm), lambda i, j: (0, i)), row, row, pl.BlockSpec((8, 128), lambda i, j: (0, 0)),
                   vec],
        out_shape=[SDS((s, nj * FF_COLS), BF16), SDS((nj * FF_COLS, s), BF16), SDS((D, s), BF16), SDS((s, D), F32),
                   SDS((s, D), BF16), SDS((8, 128), F32), SDS((1, D), F32)],
        scratch_shapes=[pltpu.VMEM((tm, D), BF16), pltpu.VMEM((tm, D), F32)],
        compiler_params=_cparams(("arbitrary", "arbitrary"), 56),
    )(h1, _small_in_hbm(g_mlp), w_up_g, w_down, _small_in_hbm(g_fin), tgt)


def _mlp_bwd(dh2, dh2b, r, w_down, w_up_g, h1, g_mlp, comm=None):
    s = h1.shape[0]
    tm = min(TM_ROWS, s)
    nj = N_SLOT

    def body(dh2_ref, dh2b_ref, r_ref, wd_ref, wu_ref, h1_ref, gm_ref, df_ref, dh1_ref, dgm_ref, acc_s):
        i, j = pl.program_id(0), pl.program_id(1)

        @pl.when(j == 0)
        def _():
            acc_s[...] = jnp.zeros_like(acc_s)

        @pl.when((i == 0) & (j == 0))
        def _():
            dgm_ref[...] = jnp.zeros_like(dgm_ref)

        d_act = _dot_nt(dh2b_ref[...], wd_ref[...])
        df = (d_act * (2.0 * r_ref[...].astype(F32))).astype(BF16)
        df_ref[...] = df
        acc_s[...] += _dot_nt(df, wu_ref[...])

        @pl.when(j == nj - 1)
        def _():
            hv = h1_ref[...]
            rstd = lax.rsqrt(jnp.mean(hv * hv, axis=-1, keepdims=True) + NORM_EPS)
            hh = hv * rstd
            dn2 = acc_s[...]
            dgm_ref[...] += jnp.sum(dn2 * hh, axis=0, keepdims=True)
            dhat = dn2 * gm_ref[...]
            dh1_ref[...] = dh2_ref[...] + rstd * (dhat - hh * jnp.mean(dhat * hh, axis=-1, keepdims=True))

    row = pl.BlockSpec((tm, D), lambda i, j: (i, 0))
    vec = pl.BlockSpec((1, D), lambda i, j: (0, 0))
    ffb = pl.BlockSpec((tm, FF_COLS), lambda i, j: (i, j))
    return _call(
        body, name="mlp_bwd", grid=(s // tm, nj),
        in_specs=[row, row, ffb, pl.BlockSpec((FF_COLS, D), lambda i, j: (j, 0)),
                  pl.BlockSpec((None, D, FF_COLS), lambda i, j: (j, 0, 0)), row, vec],
        out_specs=[ffb, row, vec],
        out_shape=[SDS((s, nj * FF_COLS), BF16), SDS((s, D), F32), SDS((1, D), F32)],
        scratch_shapes=[pltpu.VMEM((tm, D), F32)],
        params=_cparams(("arbitrary", "arbitrary"), 56), args=(dh2, dh2b, r, w_down, w_up_g, h1, g_mlp), comm=comm)


def _merge_bwd(dh1, z, pa, pb, w_out, w_oa, w_ob, comm=None):
    s = dh1.shape[0]
    tm = min(TM_MERGE, s)

    def body(dh1_ref, ma_ref, mb_ref, pa_ref, pb_ref, wo_ref, woa_ref, wob_ref,
             dz_ref, dpa_ref, dpb_ref, dya_ref, dyb_ref):
        dm = _dot_nt(dh1_ref[...].astype(BF16), wo_ref[...])
        sa = jax.nn.sigmoid(ma_ref[...].astype(F32))
        sb = jax.nn.sigmoid(mb_ref[...].astype(F32))
        dpa = (dm * sa).astype(BF16)
        dpb = (dm * sb).astype(BF16)
        dz_ref[:, 0:D] = (dm * pa_ref[...].astype(F32) * sa * (1.0 - sa)).astype(BF16)
        dz_ref[:, D:2 * D] = (dm * pb_ref[...].astype(F32) * sb * (1.0 - sb)).astype(BF16)
        dpa_ref[...] = dpa
        dpb_ref[...] = dpb
        dya_ref[...] = _dot_nt(dpa, woa_ref[...]).astype(BF16)
        dyb_ref[...] = _dot_nt(dpb, wob_ref[...]).astype(BF16)

    row = pl.BlockSpec((tm, D), lambda i: (i, 0))
    wsp = pl.BlockSpec((D, D), lambda i: (0, 0))
    return _call(
        body, name="merge_bwd", grid=(s // tm,),
        in_specs=[row, pl.BlockSpec((tm, D), lambda i: (i, 4)), pl.BlockSpec((tm, D), lambda i: (i, 5)),
                  row, row, wsp, wsp, wsp],
        out_specs=[pl.BlockSpec((tm, 2 * D), lambda i: (i, 2)), row, row, row, row],
        out_shape=[SDS((s, 6 * D), BF16)] + [SDS((s, D), BF16)] * 4, scratch_shapes=[],
        params=_cparams(("arbitrary",), 48), args=(dh1, z, z, pa, pb, w_out, w_oa, w_ob), comm=comm)


def _branch_b_bwd(dz, dyb, z, ln_g, ln_b, w_s, b_s_t, comm=None):
    s = z.shape[0]
    tb = min(T_BRANCH_B, s)

    def body(dz_in, dyb_ref, ub_ref, vb_ref, lg_ref, lb_ref, ws_ref, bs_ref,
             dz_ref, dws_ref, dbs_ref, dln_ref, du_s, dvln_s):
        del dz_in

        @pl.when(pl.program_id(0) == 0)
        def _():
            dws_ref[...] = jnp.zeros_like(dws_ref)
            dbs_ref[...] = jnp.zeros_like(dbs_ref)
            dln_ref[...] = jnp.zeros_like(dln_ref)

        lg = lg_ref[...]
        u, du, dv, rstd, vhat, vln = _sgu_common(ub_ref[...].astype(F32), vb_ref[...].astype(F32),
                                                 lg, lb_ref[...], True)
        vlnb = vln.astype(BF16)
        dyb_v = dyb_ref[...].astype(F32)
        wm = _masked_ws(ws_ref)
        keep = (lax.broadcasted_iota(jnp.int32, (CHUNK, CHUNK), 1)
                <= lax.broadcasted_iota(jnp.int32, (CHUNK, CHUNK), 0))
        bs = bs_ref[...]
        for c in range(tb // CHUNK):
            rs = slice(c * CHUNK, (c + 1) * CHUNK)
            for g in range(GROUPS):
                cs = slice(g * GROUP_DIM, (g + 1) * GROUP_DIM)
                v_blk = vlnb[rs, cs]
                sp = _dot(wm[g], v_blk) + bs[:, g:g + 1]
                d_sp = dyb_v[rs, cs] * u[rs, cs]
                d_spb = d_sp.astype(BF16)
                du_s[rs, cs] = dyb_v[rs, cs] * sp
                dvln_s[rs, cs] = _dot_tn(wm[g], d_spb)
                dws_ref[g] += jnp.where(keep, _dot_nt(d_spb, v_blk), 0.0)
                dbs_ref[g] += jnp.broadcast_to(jnp.sum(d_sp, axis=-1, keepdims=True), (CHUNK, CHUNK))
        dvln = dvln_s[...]
        dln_ref[0:1, :] += jnp.sum(dvln * vhat, axis=0, keepdims=True)
        dln_ref[1:2, :] += jnp.sum(dvln, axis=0, keepdims=True)
        dvh = dvln * lg
        d_v = rstd * (dvh - jnp.mean(dvh, axis=-1, keepdims=True)
                      - vhat * jnp.mean(dvh * vhat, axis=-1, keepdims=True))
        dz_ref[:, 0:D] = (du_s[...] * du).astype(BF16)
        dz_ref[:, D:2 * D] = (d_v * dv).astype(BF16)

    vec = pl.BlockSpec((1, D), lambda i: (0, 0))
    sq = pl.BlockSpec((GROUPS, CHUNK, CHUNK), lambda i: (0, 0, 0))
    return _call(
        body, name="branch_b_bwd", grid=(s // tb,),
        in_specs=[ANY, pl.BlockSpec((tb, D), lambda i: (i, 0)),
                  pl.BlockSpec((tb, D), lambda i: (i, 2)), pl.BlockSpec((tb, D), lambda i: (i, 3)), vec, vec, sq,
                  pl.BlockSpec((CHUNK, GROUPS), lambda i: (0, 0))],
        out_specs=[pl.BlockSpec((tb, 2 * D), lambda i: (i, 1)), sq, sq, pl.BlockSpec((8, D), lambda i: (0, 0))],
        out_shape=[SDS(dz.shape, BF16), SDS((GROUPS, CHUNK, CHUNK), F32), SDS((GROUPS, CHUNK, CHUNK), F32),
                   SDS((8, D), F32)],
        scratch_shapes=[pltpu.VMEM((tb, D), F32), pltpu.VMEM((tb, D), F32)], aliases={0: 0},
        params=_cparams(("arbitrary",), 40), args=(dz, dyb, z, z, ln_g, ln_b, w_s, b_s_t), comm=comm)


def _branch_a_bwd(dz, dya, z, hs, xc, r, ig, conv_w, w_r, w_i, lam, comm=None):
    s = z.shape[0]
    ta = min(T_BRANCH_A, s)
    nb = s // ta
    per16 = ta // 16

    def body(dz_in, dya_ref, xa_ref, ga_ref, hs_ref, hp_ref, xc_ref, r_ref, i_ref, cw_ref, wr_ref, wi_ref,
             lam_ref, dz_ref, vec_ref, dwr_ref, dwi_ref, a_s, b_s, h_s, dcar_s, acar_s, dxc_s):
        del dz_in
        i = pl.program_id(0)
        blk = nb - 1 - i

        @pl.when(i == 0)
        def _():
            dcar_s[...] = jnp.zeros_like(dcar_s)
            acar_s[...] = jnp.zeros_like(acar_s)
            dxc_s[...] = jnp.zeros_like(dxc_s)
            vec_ref[...] = jnp.zeros_like(vec_ref)
            dwr_ref[...] = jnp.zeros_like(dwr_ref)
            dwi_ref[...] = jnp.zeros_like(dwi_ref)

        cw = cw_ref[...]
        lam_v = lam_ref[...]
        xa = xa_ref[...].astype(F32)
        xcb = xc_ref[...]
        xc = xcb.astype(F32)
        sp_lam = _softplus(-lam_v)
        r_v, i_v = r_ref[...].astype(F32), i_ref[...].astype(F32)
        a_v, m_v = _decay(r_v, sp_lam)

        hs_v = hs_ref[...].astype(F32)
        hprev8 = jnp.where(blk > 0, hp_ref[...].astype(F32)[8:16], 0.0)
        h_m1 = _rows_shifted(hprev8, hs_v, 1)
        gg, dgg = _gelu_and_grad(ga_ref[...].astype(F32))
        dya_v = dya_ref[...].astype(F32)
        dz_ref[:, D:2 * D] = (dya_v * hs_v * dgg).astype(BF16)

        a_s[...] = _rows_advanced(a_v, acar_s[...], 1)
        b_s[...] = dya_v * gg

        row = lax.broadcasted_iota(jnp.int32, (8, D), 0)
        ng = ta // 8

        def group(gi, carry):
            off = pl.multiple_of((ng - 1 - gi) * 8, 8)
            c8 = a_s[pl.ds(off, 8), :]
            d8 = b_s[pl.ds(off, 8), :]
            for d in (1, 2, 4):
                c_sh = jnp.where(row < 8 - d, pltpu.roll(c8, 8 - d, 0), 1.0)
                d_sh = jnp.where(row < 8 - d, pltpu.roll(d8, 8 - d, 0), 0.0)
                d8 = c8 * d_sh + d8
                c8 = c8 * c_sh
            dh8 = d8 + c8 * carry
            h_s[pl.ds(off, 8), :] = dh8
            return jnp.broadcast_to(dh8[0:1, :], (8, D))

        dcar_s[...] = lax.fori_loop(0, ng, group, dcar_s[...])
        acar_s[...] = jnp.broadcast_to(a_v[0:1, :], (8, D))

        dbx = h_s[...]
        d_mult = dbx * xc * i_v
        d_loga = dbx * h_m1 * a_v - d_mult * (a_v * a_v) / m_v
        d_pr = d_loga * ((-LRU_C) * sp_lam) * r_v * (1.0 - r_v)
        d_pi = dbx * xc * m_v * i_v * (1.0 - i_v)
        vec_ref[7:8, :] += jnp.sum(d_loga * r_v, axis=0, keepdims=True) * (LRU_C * jax.nn.sigmoid(-lam_v))
        vec_ref[5:6, :] += jnp.sum(d_pr, axis=0, keepdims=True)
        vec_ref[6:7, :] += jnp.sum(d_pi, axis=0, keepdims=True)
        d_prb = d_pr.astype(BF16)
        d_pib = d_pi.astype(BF16)
        h_s[...] = dbx * i_v * m_v
        for h in range(HEADS):
            sl = slice(h * HEAD_DIM, (h + 1) * HEAD_DIM)
            h_s[:, sl] += _dot_nt(d_prb[:, sl], wr_ref[h]) + _dot_nt(d_pib[:, sl], wi_ref[h])
            dwr_ref[h] += _dot_tn(xcb[:, sl], d_prb[:, sl])
            dwi_ref[h] += _dot_tn(xcb[:, sl], d_pib[:, sl])
        d_xc = h_s[...]
        vec_ref[4:5, :] += jnp.sum(d_xc, axis=0, keepdims=True)
        vec_ref[0:1, :] += jnp.sum(d_xc * xa, axis=0, keepdims=True)
        d_xa = cw[0:1, :] * d_xc
        nxt = dxc_s[...]
        for k in range(1, CONV_K):
            ahead = _rows_advanced(d_xc, nxt, k)
            vec_ref[k:k + 1, :] += jnp.sum(ahead * xa, axis=0, keepdims=True)
            d_xa = d_xa + cw[k:k + 1, :] * ahead
        dz_ref[:, 0:D] = d_xa.astype(BF16)
        dxc_s[...] = d_xc[0:8, :]

    vec = pl.BlockSpec((1, D), lambda i: (0, 0))
    gate = pl.BlockSpec((HEADS, HEAD_DIM, HEAD_DIM), lambda i: (0, 0, 0))
    cur = lambda c: pl.BlockSpec((ta, D), lambda i: (nb - 1 - i, c))
    before = lambda c: pl.BlockSpec((16, D), lambda i: (jnp.maximum((nb - 1 - i) * per16 - 1, 0), c))
    return _call(
        body, name="branch_a_bwd", grid=(nb,),
        in_specs=[ANY, cur(0), cur(0), cur(1), cur(0), before(0), cur(0), cur(0), cur(0),
                  pl.BlockSpec((CONV_K, D), lambda i: (0, 0)), gate, gate, vec],
        out_specs=[pl.BlockSpec((ta, 2 * D), lambda i: (nb - 1 - i, 0)), pl.BlockSpec((8, D), lambda i: (0, 0)),
                   gate, gate],
        out_shape=[SDS(dz.shape, BF16), SDS((8, D), F32), SDS((HEADS, HEAD_DIM, HEAD_DIM), F32),
                   SDS((HEADS, HEAD_DIM, HEAD_DIM), F32)],
        scratch_shapes=[pltpu.VMEM((ta, D), F32)] * 3 + [pltpu.VMEM((8, D), F32)] * 3, aliases={0: 0},
        params=_cparams(("arbitrary",), 48),
        args=(dz, dya, z, z, hs, hs, xc, r, ig, conv_w, w_r, w_i, lam), comm=comm)


def _in_bwd(dz, w_in_g, x, dh1, g_mix, comm=None):
    s = x.shape[0]
    tm = min(TM_ROWS, s)
    nj = N_SLOT // 2

    def body(dz_ref, w_ref, x_ref, dh1_ref, g_ref, dx_ref, dg_ref, acc_s):
        i, j = pl.program_id(0), pl.program_id(1)

        @pl.when(j == 0)
        def _():
            acc_s[...] = jnp.zeros_like(acc_s)

        @pl.when((i == 0) & (j == 0))
        def _():
            dg_ref[...] = jnp.zeros_like(dg_ref)

        acc_s[...] += (_dot_nt(dz_ref[:, :W_IN_COLS], w_ref[0]) + _dot_nt(dz_ref[:, W_IN_COLS:], w_ref[1]))

        @pl.when(j == nj - 1)
        def _():
            xv = x_ref[...]
            rstd = lax.rsqrt(jnp.mean(xv * xv, axis=-1, keepdims=True) + NORM_EPS)
            xh = xv * rstd
            dn = acc_s[...]
            dg_ref[...] += jnp.sum(dn * xh, axis=0, keepdims=True)
            dhat = dn * g_ref[...]
            dx_ref[...] = dh1_ref[...] + rstd * (dhat - xh * jnp.mean(dhat * xh, axis=-1, keepdims=True))

    row = pl.BlockSpec((tm, D), lambda i, j: (i, 0))
    vec = pl.BlockSpec((1, D), lambda i, j: (0, 0))
    return _call(
        body, name="in_bwd", grid=(s // tm, nj),
        in_specs=[pl.BlockSpec((tm, 2 * W_IN_COLS), lambda i, j: (i, j)),
                  pl.BlockSpec((2, D, W_IN_COLS), lambda i, j: (j, 0, 0)), row, row, vec],
        out_specs=[row, vec],
        out_shape=[SDS((s, D), F32), SDS((1, D), F32)],
        scratch_shapes=[pltpu.VMEM((tm, D), F32)],
        params=_cparams(("arbitrary", "arbitrary"), 56), args=(dz, w_in_g, x, dh1, g_mix), comm=comm)


def _wgrad(name, a, b):
    s = a.shape[0]
    ts = min(TM_ROWS, s)
    a_w, b_w = a.shape[1], b.shape[1]

    def body(a_ref, b_ref, o_ref, acc_s):
        t = pl.program_id(0)

        @pl.when(t == 0)
        def _():
            acc_s[...] = jnp.zeros_like(acc_s)

        acc_s[...] += _dot_tn(a_ref[...].astype(BF16), b_ref[...].astype(BF16))

        @pl.when(t == pl.num_programs(0) - 1)
        def _():
            o_ref[...] = acc_s[...].astype(BF16)

    return pl.pallas_call(
        body, name=name, grid=(s // ts,),
        in_specs=[pl.BlockSpec((ts, a_w), lambda t: (t, 0)), pl.BlockSpec((ts, b_w), lambda t: (t, 0))],
        out_specs=pl.BlockSpec((a_w, b_w), lambda t: (0, 0)),
        out_shape=SDS((a_w, b_w), BF16),
        scratch_shapes=[pltpu.VMEM((a_w, b_w), F32)],
        compiler_params=_cparams(("arbitrary",), 48),
    )(a, b)


def _place():
    x, y, c = lax.axis_index("x"), lax.axis_index("y"), lax.axis_index("c")
    return x, y, c


def _other_chips(x, y):
    return [(x, 1 - y, 2 * x + 1 - y), (1 - x, y, 2 * (1 - x) + y), (1 - x, 1 - y, 2 * (1 - x) + 1 - y)]


class _Plan:
    def __init__(self, arrays, out_shape, sems, start, finish, middle=None, middle_at=6):
        self.arrays, self.out_shape, self.sems, self.start, self.finish = arrays, out_shape, sems, start, finish
        self.middle, self.middle_at = middle, middle_at


def _gather_plan(shards, middle_at=6):
    n = len(shards)

    def copies(ins, outs, sems):
        send_sems, recv_sems, local_sems = sems
        x, y, c = _place()
        chip = 2 * x + y
        me = 2 * chip + c
        sib = (x, y, 1 - c)
        chips = _other_chips(x, y)

        def rc(k, t, src, blk, to):
            return pltpu.make_async_remote_copy(
                src_ref=src, dst_ref=outs[t].at[blk], send_sem=send_sems.at[k * n + t],
                recv_sem=recv_sems.at[k * n + t], device_id=to, device_id_type=MESH)

        (yx, yy, y_chip), (xx, xy, x_chip), _ = chips
        local = [pltpu.make_async_copy(ins[t], outs[t].at[me], local_sems.at[t]) for t in range(n)]
        sends = ([rc(0, t, ins[t], me, sib) for t in range(n)] + [rc(1, t, ins[t], me, (yx, yy, c)) for t in range(n)]
                 + [rc(2, t, ins[t], me, (xx, xy, c)) for t in range(n)])
        passed = [[rc(4 + j, t, outs[t].at[2 * pc + c], 2 * pc + c, sib) for t in range(n)]
                  for j, (_, _, pc) in enumerate(chips)]
        relays = [[rc(3, t, outs[t].at[2 * y_chip + c], 2 * y_chip + c, (xx, xy, c)) for t in range(n)],
                  [rc(3, t, outs[t].at[2 * x_chip + c], 2 * x_chip + c, (yx, yy, c)) for t in range(n)]]
        return rc, local, sends, passed, relays, chips, chip, c, sib

    def start(ins, outs, sems):
        _, local, sends, _, _, _, _, _, _ = copies(ins, outs, sems)
        for cp in local + sends:
            cp.start()

    def middle(ins, outs, sems):
        rc, _, _, passed, relays, chips, _, c, sib = copies(ins, outs, sems)
        for j in range(2):
            for t in range(n):
                rc(1 + j, t, ins[t], 2 * chips[j][2] + c, sib).wait_recv()
        for j in range(2):
            for cp in passed[j]:
                cp.start()

            @pl.when(c == j)
            def _():
                for cp in relays[j]:
                    cp.start()

    def finish(ins, outs, sems):
        rc, local, sends, passed, relays, chips, chip, c, sib = copies(ins, outs, sems)
        far = 2 * chips[2][2] + c
        for t in range(n):
            rc(3, t, ins[t], far, sib).wait_recv()
        for cp in passed[2]:
            cp.start()
        for t in range(n):
            rc(0, t, ins[t], 2 * chip + 1 - c, sib).wait_recv()
        for j, (px, py, pc) in enumerate(chips):
            for t in range(n):
                rc(4 + j, t, ins[t], 2 * pc + 1 - c, sib).wait_recv()
        for cp in sends + passed[0] + passed[1] + passed[2]:
            cp.wait_send()
        for j in range(2):
            @pl.when(c == j)
            def _():
                for cp in relays[j]:
                    cp.wait_send()
        for cp in local:
            cp.wait()

    return _Plan(list(shards), [SDS((N_SLOT,) + tuple(a.shape), a.dtype) for a in shards],
                 [pltpu.SemaphoreType.DMA((7 * n,)), pltpu.SemaphoreType.DMA((7 * n,)),
                  pltpu.SemaphoreType.DMA((n,))], start, finish, middle, middle_at)


def _sibling_plan(grads, whole=()):
    n, m = len(grads), len(whole)

    def copies(ins, outs, sems):
        send_sems, recv_sems = sems
        x, y, c = _place()
        sib = (x, y, 1 - c)

        def rc(t, src, dst):
            return pltpu.make_async_remote_copy(src_ref=src, dst_ref=dst, send_sem=send_sems.at[t],
                                                recv_sem=recv_sems.at[t], device_id=sib, device_id_type=MESH)
        return rc, c

    def start(ins, outs, sems):
        rc, c = copies(ins, outs, sems)
        for t in range(n):
            for j in range(4):
                rc(t, ins[t].at[2 * j + 1 - c], outs[t].at[j]).start()
        for t in range(n, n + m):
            rc(t, ins[t], outs[t]).start()

    def finish(ins, outs, sems):
        rc, _ = copies(ins, outs, sems)
        for t in range(n):
            rc(t, ins[t].at[pl.ds(0, 4)], outs[t]).wait()
        for t in range(n, n + m):
            rc(t, ins[t], outs[t]).wait()

    return _Plan(list(grads) + list(whole),
                 [SDS((4,) + tuple(g.shape[1:]), g.dtype) for g in grads] + [SDS(a.shape, a.dtype) for a in whole],
                 [pltpu.SemaphoreType.DMA((n + m,)), pltpu.SemaphoreType.DMA((n + m,))], start, finish)


def _chips_plan(parts, whole=()):
    n, m = len(parts), len(whole)

    def src_of(ins, t, pc):
        return ins[t].at[pc] if t < n else ins[t]

    def local_copies(ins, outs, sems, chip):
        return [pltpu.make_async_copy(src_of(ins, t, chip), outs[t].at[chip], sems[2].at[t]) for t in range(n + m)]

    def start(ins, outs, sems):
        send_sems, recv_sems, _ = sems
        x, y, c = _place()
        chip = 2 * x + y
        for cp in local_copies(ins, outs, sems, chip):
            cp.start()
        for px, py, pc in _other_chips(x, y):
            for t in range(n + m):
                pltpu.make_async_remote_copy(src_ref=src_of(ins, t, pc), dst_ref=outs[t].at[chip],
                                             send_sem=send_sems.at[t], recv_sem=recv_sems.at[t],
                                             device_id=(px, py, c), device_id_type=MESH).start()

    def finish(ins, outs, sems):
        send_sems, recv_sems, _ = sems
        x, y, c = _place()
        for t in range(n + m):
            three = outs[t].at[pl.ds(0, 3)]
            pltpu.make_async_remote_copy(src_ref=three, dst_ref=three, send_sem=send_sems.at[t],
                                         recv_sem=recv_sems.at[t], device_id=(x, y, c), device_id_type=MESH).wait()
        for cp in local_copies(ins, outs, sems, 2 * x + y):
            cp.wait()

    return _Plan(list(parts) + list(whole),
                 [SDS(p.shape, p.dtype) for p in parts] + [SDS((4,) + tuple(a.shape), a.dtype) for a in whole],
                 [pltpu.SemaphoreType.DMA((n + m,)), pltpu.SemaphoreType.DMA((n + m,)),
                  pltpu.SemaphoreType.DMA((n + m,))], start, finish)


def _exchange_plan(arr):
    def peers(x, y, c):
        flip = lambda v, f: 1 - v if f else v
        return [(flip(x, fx), flip(y, fy), flip(c, fc))
                for fx in (0, 1) for fy in (0, 1) for fc in (0, 1) if fx or fy or fc]

    def start(ins, outs, sems):
        x, y, c = _place()
        me = 4 * x + 2 * y + c
        pltpu.make_async_copy(ins[0], outs[0].at[me], sems[2].at[0]).start()
        for to in peers(x, y, c):
            pltpu.make_async_remote_copy(src_ref=ins[0], dst_ref=outs[0].at[me], send_sem=sems[0].at[0],
                                         recv_sem=sems[1].at[0], device_id=to, device_id_type=MESH).start()

    def finish(ins, outs, sems):
        x, y, c = _place()
        seven = outs[0].at[pl.ds(0, 7)]
        pltpu.make_async_remote_copy(src_ref=seven, dst_ref=seven, send_sem=sems[0].at[0], recv_sem=sems[1].at[0],
                                     device_id=(x, y, c), device_id_type=MESH).wait()
        pltpu.make_async_copy(ins[0], outs[0].at[4 * x + 2 * y + c], sems[2].at[0]).wait()

    return _Plan([arr], [SDS((N_SLOT,) + tuple(arr.shape), arr.dtype)],
                 [pltpu.SemaphoreType.DMA((1,)), pltpu.SemaphoreType.DMA((1,)), pltpu.SemaphoreType.DMA((1,))],
                 start, finish)


def _join(*plans):
    def cut(seq, sizes):
        out, at = [], 0
        for k in sizes:
            out.append(seq[at:at + k])
            at += k
        return out

    n_arr = [len(p.arrays) for p in plans]
    n_sem = [len(p.sems) for p in plans]

    def start(ins, outs, sems):
        for p, i, o, s in zip(plans, cut(ins, n_arr), cut(outs, n_arr), cut(sems, n_sem)):
            p.start(i, o, s)

    def finish(ins, outs, sems):
        for p, i, o, s in zip(plans, cut(ins, n_arr), cut(outs, n_arr), cut(sems, n_sem)):
            p.finish(i, o, s)

    def middle(ins, outs, sems):
        for p, i, o, s in zip(plans, cut(ins, n_arr), cut(outs, n_arr), cut(sems, n_sem)):
            if p.middle is not None:
                p.middle(i, o, s)

    return _Plan([a for p in plans for a in p.arrays], [o for p in plans for o in p.out_shape],
                 [s for p in plans for s in p.sems], start, finish,
                 middle if any(p.middle is not None for p in plans) else None,
                 max(p.middle_at for p in plans))


def _run_plan(name, plan):
    k = len(plan.arrays)

    def body(*refs):
        ins, outs, sems = refs[:k], refs[k:2 * k], refs[2 * k:]
        plan.start(ins, outs, sems)
        if plan.middle is not None:
            plan.middle(ins, outs, sems)
        plan.finish(ins, outs, sems)

    return pl.pallas_call(
        body, name=name, in_specs=[ANY] * k, out_specs=[ANY] * k, out_shape=plan.out_shape,
        scratch_shapes=plan.sems, compiler_params=pltpu.CompilerParams(has_side_effects=True),
    )(*plan.arrays)


def _call(body, *, name, grid, in_specs, out_specs, out_shape, scratch_shapes, params, args, comm=None,
          aliases=None, prefetch=()):
    aliases = aliases or {}
    n_pre = len(prefetch)

    def launch(fn, ins_specs, outs_specs, outs_shape, scratch, operands):
        spec = pltpu.PrefetchScalarGridSpec(num_scalar_prefetch=n_pre, grid=grid, in_specs=ins_specs,
                                            out_specs=outs_specs, scratch_shapes=scratch)
        return pl.pallas_call(fn, name=name, grid_spec=spec, out_shape=outs_shape, compiler_params=params,
                              input_output_aliases=aliases)(*prefetch, *[_small_in_hbm(a) for a in operands])

    if comm is None:
        return list(launch(body, in_specs, out_specs, out_shape, scratch_shapes, args)), []
    n_in, n_out, n_scr, k = len(in_specs), len(out_specs), len(scratch_shapes), len(comm.arrays)

    def wrapped(*refs):
        pre, refs = refs[:n_pre], refs[n_pre:]
        ins = refs[:n_in]
        c_in = refs[n_in:n_in + k]
        outs = refs[n_in + k:n_in + k + n_out]
        c_out = refs[n_in + k + n_out:n_in + 2 * k + n_out]
        scr = refs[n_in + 2 * k + n_out:n_in + 2 * k + n_out + n_scr]
        sems = refs[n_in + 2 * k + n_out + n_scr:]
        step, steps = pl.program_id(0), grid[0]
        for d in range(1, len(grid)):
            step, steps = step * grid[d] + pl.program_id(d), steps * grid[d]

        @pl.when(step == 0)
        def _():
            comm.start(c_in, c_out, sems)

        if comm.middle is not None:
            @pl.when(step == (comm.middle_at * steps) // 8)
            def _():
                comm.middle(c_in, c_out, sems)

        body(*pre, *ins, *outs, *scr)

        @pl.when(step == steps - 1)
        def _():
            comm.finish(c_in, c_out, sems)

    res = launch(wrapped, list(in_specs) + [ANY] * k, list(out_specs) + [ANY] * k,
                 list(out_shape) + list(comm.out_shape), list(scratch_shapes) + list(comm.sems),
                 tuple(args) + tuple(comm.arrays))
    return list(res[:n_out]), list(res[n_out:])


def _wgrad_paired(name, a_t, b, core, by_rows=False, comm=None):
    s = b.shape[0]
    m = a_t.shape[0] // N_SLOT if by_rows else a_t.shape[0]
    cols = b.shape[1] if by_rows else b.shape[1] // N_SLOT
    half = N_SLOT // 2

    def owner(k, c):
        return 2 * (k % half) + jnp.where(k < half, 1 - c, c)

    def body(c_ref, a_ref, b_ref, sum_ref, out_s, recv_s, send_sems, recv_sems):
        del c_ref
        k = pl.program_id(0)
        x, y, c = _place()

        def to_sibling(j):
            return pltpu.make_async_remote_copy(src_ref=out_s.at[j % 2], dst_ref=recv_s.at[j],
                                                send_sem=send_sems.at[j], recv_sem=recv_sems.at[j],
                                                device_id=(x, y, 1 - c), device_id_type=MESH)

        @pl.when((k >= 2) & (k < half + 2))
        def _():
            to_sibling(k - 2).wait_send()

        @pl.when(k < half)
        def _():
            out_s[k % 2] = _dot(a_ref[...], b_ref[...]).astype(BF16)
            to_sibling(k).start()

        @pl.when(k >= half)
        def _():
            to_sibling(k - half).wait_recv()
            sum_ref[...] = (_dot(a_ref[...], b_ref[...]) + recv_s[k - half].astype(F32)).astype(BF16)

    if by_rows:
        in_specs = [pl.BlockSpec((m, s), lambda k, c_ref: (owner(k, c_ref[0]), 0)),
                    pl.BlockSpec((s, cols), lambda k, c_ref: (0, 0))]
    else:
        in_specs = [pl.BlockSpec((m, s), lambda k, c_ref: (0, 0)),
                    pl.BlockSpec((s, cols), lambda k, c_ref: (0, owner(k, c_ref[0])))]
    return _call(body, name=name, grid=(N_SLOT,), in_specs=in_specs,
                 out_specs=[pl.BlockSpec((None, m, cols), lambda k, c_ref: (jnp.maximum(k - half, 0), 0, 0))],
                 out_shape=[SDS((half, m, cols), BF16)],
                 scratch_shapes=[pltpu.VMEM((2, m, cols), BF16), pltpu.VMEM((half, m, cols), BF16),
                                 pltpu.SemaphoreType.DMA((half,)), pltpu.SemaphoreType.DMA((half,))],
                 params=_cparams(("arbitrary",), 56), args=(a_t, b), comm=comm, prefetch=(core,))


def _row_tile(rows):
    for t in (512, 256, 128, 64, 32, 16, 8):
        if rows % t == 0:
            return t
    return rows


def _pair_sum(name, g8, recv4, core):
    _, rows, cols = recv4.shape
    tr = _row_tile(rows)
    g42 = g8.reshape(4, 2, rows, cols)

    def body(c_ref, g_ref, r_ref, o_ref):
        del c_ref
        o_ref[...] = (g_ref[...].astype(F32) + r_ref[...].astype(F32)).astype(o_ref.dtype)

    return pl.pallas_call(
        body, name=name,
        grid_spec=pltpu.PrefetchScalarGridSpec(
            num_scalar_prefetch=1, grid=(4, rows // tr),
            in_specs=[pl.BlockSpec((None, None, tr, cols), lambda j, i, c_ref: (j, c_ref[0], i, 0)),
                      pl.BlockSpec((None, tr, cols), lambda j, i, c_ref: (j, i, 0))],
            out_specs=pl.BlockSpec((None, tr, cols), lambda j, i, c_ref: (j, i, 0))),
        out_shape=SDS(recv4.shape, g8.dtype),
        compiler_params=_cparams(("arbitrary", "arbitrary"), 32),
    )(core, g42, recv4)


def _add2(name, a, b):
    rows, cols = a.shape
    tr = _row_tile(rows)

    def body(a_ref, b_ref, o_ref):
        o_ref[...] = a_ref[...] + b_ref[...]

    blk = pl.BlockSpec((tr, cols), lambda i: (i, 0))
    return pl.pallas_call(body, name=name, grid=(rows // tr,), in_specs=[blk, blk], out_specs=blk,
                          out_shape=SDS(a.shape, a.dtype),
                          compiler_params=_cparams(("arbitrary",), 32))(a, b)


def _sum_terms(name, terms):
    k, rows, cols = terms.shape
    tr = _row_tile(rows)

    def body(r_ref, o_ref):
        acc = r_ref[0]
        for q in range(1, k):
            acc = acc + r_ref[q]
        o_ref[...] = acc

    return pl.pallas_call(body, name=name, grid=(rows // tr,),
                          in_specs=[pl.BlockSpec((k, tr, cols), lambda i: (0, i, 0))],
                          out_specs=pl.BlockSpec((tr, cols), lambda i: (i, 0)),
                          out_shape=SDS((rows, cols), terms.dtype),
                          compiler_params=_cparams(("arbitrary",), 32))(terms)


def _adam_update(g, w, m, v):
    c1 = 1.0 / (1.0 - ADAM_B1 ** ADAM_STEP)
    c2 = 1.0 / (1.0 - ADAM_B2 ** ADAM_STEP)
    mn = ADAM_B1 * m + (1.0 - ADAM_B1) * g
    vn = ADAM_B2 * v + (1.0 - ADAM_B2) * (g * g)
    delta = (-ADAM_LR) * ((mn * c1) / (jnp.sqrt(vn * c2) + ADAM_EPS) + ADAM_WD * w)
    return delta, mn, vn


def _adamw_many(name, gs, ws, ms, vs):
    n = len(gs)

    def body(*refs):
        for p in range(n):
            g, w, m, v = (refs[q * n + p][...] for q in range(4))
            d, mn, vn = _adam_update(g, w, m, v)
            refs[4 * n + p][...] = d
            refs[5 * n + p][...] = mn
            refs[6 * n + p][...] = vn

    full = [pl.BlockSpec(w.shape, lambda i: (0, 0)) for w in ws]
    shapes = [SDS(w.shape, F32) for w in ws]
    res = pl.pallas_call(body, name=name, grid=(1,), in_specs=full * 4, out_specs=full * 3, out_shape=shapes * 3,
                         compiler_params=_cparams(("arbitrary",), 32),
                         )(*[_small_in_hbm(a) for a in (*gs, *ws, *ms, *vs)])
    return [(res[p], res[n + p], res[2 * n + p]) for p in range(n)]


def _adamw(name, terms, w, m, v):
    k, rows, cols = terms.shape
    tr = _row_tile(rows)

    def body(t_ref, w_ref, m_ref, v_ref, g_ref, d_ref, mo_ref, vo_ref):
        g = t_ref[0].astype(F32)
        for q in range(1, k):
            g = g + t_ref[q].astype(F32)
        g_ref[...] = g
        d_ref[...], mo_ref[...], vo_ref[...] = _adam_update(g, w_ref[...], m_ref[...], v_ref[...])

    blk = pl.BlockSpec((tr, cols), lambda i: (i, 0))
    return pl.pallas_call(body, name=name, grid=(rows // tr,),
                          in_specs=[pl.BlockSpec((k, tr, cols), lambda i: (0, i, 0)), blk, blk, blk],
                          out_specs=[blk] * 4, out_shape=[SDS((rows, cols), F32)] * 4,
                          compiler_params=_cparams(("arbitrary",), 40),
                          )(*[pltpu.with_memory_space_constraint(a, pltpu.HBM) for a in (terms, w, m, v)])


def kernel(x, norm_mix_g, w_in, conv_w, conv_b, w_rgate, b_rgate, w_igate, b_igate, lru_lambda, w_out_a, sgu_ln_g, sgu_ln_b, sgu_w_s, sgu_b_s, w_out_b, w_out, norm_mlp_g, w_up, w_down, norm_final_g, loss_target, m_norm_mix_g, m_w_in, m_conv_w, m_conv_b, m_w_rgate, m_b_rgate, m_w_igate, m_b_igate, m_lru_lambda, m_w_out_a, m_sgu_ln_g, m_sgu_ln_b, m_sgu_w_s, m_sgu_b_s, m_w_out_b, m_w_out, m_norm_mlp_g, m_w_up, m_w_down, m_norm_final_g, v_norm_mix_g, v_w_in, v_conv_w, v_conv_b, v_w_rgate, v_b_rgate, v_w_igate, v_b_igate, v_lru_lambda, v_w_out_a, v_sgu_ln_g, v_sgu_ln_b, v_sgu_w_s, v_sgu_b_s, v_w_out_b, v_w_out, v_norm_mlp_g, v_w_up, v_w_down, v_norm_final_g):
    cx, cy, cc = _place()
    me = 4 * cx + 2 * cy + cc
    core = jnp.reshape(cc, (1,)).astype(jnp.int32)
    xs = x[0]
    tgt = loss_target[0]

    gate_shard = jnp.stack([w_rgate[0], w_igate[0]]).astype(BF16).reshape(2 * HEADS * 32, HEAD_DIM)
    vec_shard = jnp.concatenate([conv_w[0], b_rgate[0], b_igate[0]], axis=1)
    vec_shard = jnp.pad(vec_shard, ((0, 4), (0, 256 - vec_shard.shape[1])))
    shards = [w_in[0].astype(BF16), w_out_a[0].astype(BF16), w_out_b[0].astype(BF16), w_out[0].astype(BF16),
              w_up[0].astype(BF16), w_down[0].astype(BF16), gate_shard, vec_shard]
    (z, n1_t, w_in_g), (gate_g, vec_g) = _in_proj(xs, norm_mix_g, shards[0], _slot_order(cx, cy, cc),
                                                comm=_gather_plan(shards[6:8]))
    gates = gate_g.reshape(N_SLOT, 2, HEADS, 32, HEAD_DIM).transpose(1, 2, 0, 3, 4).reshape(2, HEADS, HEAD_DIM, HEAD_DIM)
    w_r_f, w_i_f = gates[0], gates[1]
    conv_w_f = vec_g[:, 0:4, 0:128].transpose(1, 0, 2).reshape(CONV_K, D)
    b_r_f = vec_g[:, 0:4, 128:160].transpose(1, 0, 2).reshape(1, D)
    b_i_f = vec_g[:, 0:4, 160:192].transpose(1, 0, 2).reshape(1, D)
    b_s_t = jnp.transpose(sgu_b_s[0])

    (ya, hs, xc, r_gate, i_gate), (w_oa_g, w_ob_g, w_out_g, w_up_g) = _branch_a_fwd(
        z, conv_w_f, conv_b, w_r_f, b_r_f, w_i_f, b_i_f, lru_lambda, comm=_gather_plan(shards[1:5]))
    w_oa_f = w_oa_g.reshape(D, D)
    w_ob_f = w_ob_g.reshape(D, D)
    w_out_f = w_out_g.reshape(D, D)
    (yb, pa, pb, merged, h1), (w_down_g,) = _branch_b_merge_out(
        ya, z, xs, sgu_ln_g, sgu_ln_b, sgu_w_s[0], b_s_t, w_oa_f, w_ob_f, w_out_f,
        comm=_gather_plan(shards[5:6], middle_at=4))
    w_down_f = w_down_g.reshape(N_SLOT * FF_COLS, D)
    r_act, act_t, n2_t, dh2, dh2b, loss_acc, d_gfin = _mlp_fwd(h1, norm_mlp_g, w_up_g, w_down_f,
                                                               norm_final_g.reshape(1, D), tgt)

    def pair(names, grads, recv):
        return [_pair_sum("pair_sum_" + nm, g, r, core) for nm, g, r in zip(names, grads, recv)]

    (p_down,), _ = _wgrad_paired("wgrad_down", act_t, dh2b, core, by_rows=True)
    (df, dh1, d_gmlp), (got_down,) = _mlp_bwd(dh2, dh2b, r_act, w_down_f, w_up_g, h1, norm_mlp_g,
                                              comm=_chips_plan([p_down]))
    (p_up,), _ = _wgrad_paired("wgrad_up", n2_t, df, core)
    g_out = _wgrad("wgrad_out", merged, dh1).reshape(N_SLOT, D // N_SLOT, D)
    (dz, dpa, dpb, dya, dyb), (r_out,) = _merge_bwd(
        dh1, z, pa, pb, w_out_f, w_oa_f, w_ob_f, comm=_sibling_plan([g_out]))
    (p_out,) = pair(["out"], [g_out], [r_out])
    g_oa = _wgrad("wgrad_out_a", ya, dpa).reshape(N_SLOT, D // N_SLOT, D)
    g_ob = _wgrad("wgrad_out_b", yb, dpb).reshape(N_SLOT, D // N_SLOT, D)
    (dz, d_ws, d_bs, d_ln), (got_out, r_oa, r_ob) = _branch_b_bwd(
        dz, dyb, z, sgu_ln_g, sgu_ln_b, sgu_w_s[0], b_s_t,
        comm=_join(_chips_plan([p_out]), _sibling_plan([g_oa, g_ob])))
    p_oa, p_ob = pair(["out_a", "out_b"], [g_oa, g_ob], [r_oa, r_ob])
    (dz, d_vec, d_wr, d_wi), (got_up, got_oa, got_ob) = _branch_a_bwd(
        dz, dya, z, hs, xc, r_gate, i_gate, conv_w_f, w_r_f, w_i_f, lru_lambda,
        comm=_chips_plan([p_up, p_oa, p_ob]))
    g_gate = jnp.stack([d_wr, d_wi]).reshape(2, HEADS, N_SLOT, 32, HEAD_DIM).transpose(2, 0, 1, 3, 4)
    g_gate = g_gate.reshape(N_SLOT, 2 * HEADS * 32, HEAD_DIM).astype(BF16)

    d_bs_row = jnp.pad(d_bs[:, :, 0].reshape(1, GROUPS * CHUNK), ((0, 0), (0, D - GROUPS * CHUNK)))
    vecs = jnp.concatenate([d_vec, jnp.concatenate([d_ln[0:2], d_gmlp, d_gfin, d_bs_row, jnp.zeros((3, D), F32)])])
    d_ws2 = d_ws.reshape(GROUPS * CHUNK, CHUNK)
    (p_in,), (r_gate, r_vecs, r_ws) = _wgrad_paired("wgrad_in", n1_t, dz, core,
                                                    comm=_sibling_plan([g_gate], [vecs, d_ws2]))
    (p_gate,) = pair(["gate"], [g_gate], [r_gate])
    vecs_chip = _add2("pair_sum_vecs", vecs, r_vecs)
    ws_chip = _add2("pair_sum_ws", d_ws2, r_ws)
    (dx, d_gmix), (got_in, got_gate, got_vecs, got_ws) = _in_bwd(
        dz, w_in_g, xs, dh1, norm_mix_g, comm=_chips_plan([p_in, p_gate], [vecs_chip, ws_chip]))
    vecs_sum = _sum_terms("sum_vecs", got_vecs)
    last = jnp.concatenate([d_gmix, jnp.pad(loss_acc[0:1], ((0, 0), (0, D - 128))), jnp.zeros((6, D), F32)])
    (last_all,) = _run_plan("exchange_last", _exchange_plan(last))
    last_sum = _sum_terms("sum_last", last_all)
    loss = last_sum[1, 0]
    got = [got_in, got_oa, got_ob, got_out, got_up, got_down, got_gate]

    def step(nm, terms, w, m, v, rows, cols):
        g, d, mn, vn = _adamw("adamw_" + nm, terms.reshape(4, rows, cols), w.reshape(rows, cols),
                              m.reshape(rows, cols), v.reshape(rows, cols))
        return [a.reshape(w.shape) for a in (g, d, mn, vn)]

    o_in = step("in", got[0], w_in, m_w_in, v_w_in, D, W_IN_COLS)
    o_oa = step("out_a", got[1], w_out_a, m_w_out_a, v_w_out_a, D // N_SLOT, D)
    o_ob = step("out_b", got[2], w_out_b, m_w_out_b, v_w_out_b, D // N_SLOT, D)
    o_out = step("out", got[3], w_out, m_w_out, v_w_out, D // N_SLOT, D)
    o_up = step("up", got[4], w_up, m_w_up, v_w_up, D, FF_COLS)
    o_down = step("down", got[5], w_down, m_w_down, v_w_down, FF_COLS, D)
    gate_w = jnp.stack([w_rgate[0], w_igate[0]]).reshape(2 * HEADS * 32, HEAD_DIM)
    gate_m = jnp.stack([m_w_rgate[0], m_w_igate[0]]).reshape(2 * HEADS * 32, HEAD_DIM)
    gate_v = jnp.stack([v_w_rgate[0], v_w_igate[0]]).reshape(2 * HEADS * 32, HEAD_DIM)
    o_gate = _adamw("adamw_gate", got[6], gate_w, gate_m, gate_v)
    o_gate = [a.reshape(2, 1, HEADS, 32, HEAD_DIM) for a in o_gate]
    o_wr = [a[0] for a in o_gate]
    o_wi = [a[1] for a in o_gate]

    def own(full, width):
        return lax.dynamic_slice_in_dim(full, me * width, width, axis=1)

    small_g = {
        "norm_mix_g": last_sum[0:1], "conv_w": own(vecs_sum[0:4], 128), "conv_b": vecs_sum[4:5],
        "b_rgate": own(vecs_sum[5:6].reshape(HEADS, HEAD_DIM), 32),
        "b_igate": own(vecs_sum[6:7].reshape(HEADS, HEAD_DIM), 32),
        "lru_lambda": vecs_sum[7:8], "sgu_ln_g": vecs_sum[8:9], "sgu_ln_b": vecs_sum[9:10],
        "norm_mlp_g": vecs_sum[10:11], "norm_final_g": vecs_sum[11:12],
        "sgu_b_s": vecs_sum[12, 0:GROUPS * CHUNK].reshape(GROUPS, CHUNK),
    }
    small_w = {"norm_mix_g": (norm_mix_g, m_norm_mix_g, v_norm_mix_g), "conv_w": (conv_w, m_conv_w, v_conv_w),
               "conv_b": (conv_b, m_conv_b, v_conv_b), "b_rgate": (b_rgate, m_b_rgate, v_b_rgate),
               "b_igate": (b_igate, m_b_igate, v_b_igate), "lru_lambda": (lru_lambda, m_lru_lambda, v_lru_lambda),
               "sgu_ln_g": (sgu_ln_g, m_sgu_ln_g, v_sgu_ln_g), "sgu_ln_b": (sgu_ln_b, m_sgu_ln_b, v_sgu_ln_b),
               "norm_mlp_g": (norm_mlp_g, m_norm_mlp_g, v_norm_mlp_g),
               "norm_final_g": (norm_final_g, m_norm_final_g, v_norm_final_g),
               "sgu_b_s": (sgu_b_s, m_sgu_b_s, v_sgu_b_s), "sgu_w_s": (sgu_w_s, m_sgu_w_s, v_sgu_w_s)}
    order = list(small_g)
    as2d = lambda k, a: a.reshape(small_g[k].shape)
    upd = _adamw_many("adamw_small", [small_g[k] for k in order], *[[as2d(k, small_w[k][q]) for k in order]
                                                                     for q in range(3)])
    o_small = {k: [a.reshape(small_w[k][0].shape) for a in (small_g[k],) + u] for k, u in zip(order, upd)}
    ws3 = [a[0].reshape(GROUPS * CHUNK, CHUNK) for a in small_w.pop("sgu_w_s")]
    o_small["sgu_w_s"] = [a.reshape(sgu_w_s.shape) for a in _adamw("adamw_ws", got_ws, *ws3)]

    per_weight = {"norm_mix_g": o_small["norm_mix_g"], "w_in": o_in, "conv_w": o_small["conv_w"],
                  "conv_b": o_small["conv_b"], "w_rgate": o_wr, "b_rgate": o_small["b_rgate"], "w_igate": o_wi,
                  "b_igate": o_small["b_igate"], "lru_lambda": o_small["lru_lambda"], "w_out_a": o_oa,
                  "sgu_ln_g": o_small["sgu_ln_g"], "sgu_ln_b": o_small["sgu_ln_b"], "sgu_w_s": o_small["sgu_w_s"],
                  "sgu_b_s": o_small["sgu_b_s"], "w_out_b": o_ob, "w_out": o_out, "norm_mlp_g": o_small["norm_mlp_g"],
                  "w_up": o_up, "w_down": o_down, "norm_final_g": o_small["norm_final_g"]}
    names_w = list(per_weight)
    return (loss, dx[None], *[per_weight[k][0] for k in names_w], *[per_weight[k][1] for k in names_w],
            *[per_weight[k][2] for k in names_w], *[per_weight[k][3] for k in names_w])
```

```python
import jax
import jax.numpy as jnp
from jax import lax
from jax.experimental import pallas as pl
from jax.experimental.pallas import tpu as pltpu

F32 = jnp.float32
BF16 = jnp.bfloat16
SDS = jax.ShapeDtypeStruct
MESH = pl.DeviceIdType.MESH
ANY = pl.BlockSpec(memory_space=pl.ANY)

D = 1024
N_SLOT = 8
W_IN_COLS = 768
FF_COLS = 512
HEADS, HEAD_DIM = 4, 256
GROUPS, GROUP_DIM = 4, 256
CHUNK = 128
CONV_K = 4
NORM_EPS = 1e-6
LN_EPS = 1e-5
LRU_C = 8.0
ADAM_LR, ADAM_B1, ADAM_B2, ADAM_EPS, ADAM_WD, ADAM_STEP = 0.001, 0.9, 0.999, 1e-08, 0.01, 10

TM_ROWS = 1024
TM_MERGE = 512
T_BRANCH_A = 512
T_BRANCH_B = 256
MiB = 1024 * 1024
SMALL_OPERAND = 16 * 1024

_GELU_C = 0.7978845608028654
_GELU_A = 0.044715


def _small_in_hbm(a):
    return pltpu.with_memory_space_constraint(a, pltpu.HBM) if a.size <= SMALL_OPERAND else a


def _cparams(sem, vmem_mib):
    return pltpu.CompilerParams(dimension_semantics=sem, vmem_limit_bytes=vmem_mib * MiB)


def _gelu(x):
    t = jnp.tanh(_GELU_C * (x + _GELU_A * x * x * x))
    return 0.5 * x * (1.0 + t)


def _gelu_and_grad(x):
    x2 = x * x
    t = jnp.tanh(_GELU_C * x * (1.0 + _GELU_A * x2))
    g = 0.5 * x * (1.0 + t)
    dg = 0.5 * (1.0 + t) + 0.5 * x * (1.0 - t * t) * _GELU_C * (1.0 + 3.0 * _GELU_A * x2)
    return g, dg


def _softplus(x):
    return jnp.maximum(x, 0.0) + jnp.log1p(jnp.exp(-jnp.abs(x)))


def _dot(a, b):
    return jnp.dot(a, b, preferred_element_type=F32)


def _dot_nt(a, b):
    return lax.dot_general(a, b, (((1,), (1,)), ((), ())), preferred_element_type=F32)


def _dot_tn(a, b):
    return lax.dot_general(a, b, (((0,), (0,)), ((), ())), preferred_element_type=F32)


def _rows_shifted(prev8, cur, k):
    ext = jnp.concatenate([prev8, cur], axis=0)
    return pltpu.roll(ext, k, 0)[8:]


def _rows_advanced(cur, next8, k):
    t = cur.shape[0]
    ext = jnp.concatenate([cur, next8], axis=0)
    return pltpu.roll(ext, t + 8 - k, 0)[:t]


def _first_second(x, y, c):
    ny, nx, far = _other_chips(x, y)
    pick = lambda a, b: a * (1 - c) + b * c
    first = tuple(pick(a, b) for a, b in zip(ny, nx))
    second = tuple(pick(b, a) for a, b in zip(ny, nx))
    return first, second, far


def _slot_order(x, y, c):
    chip = 2 * x + y
    first, second, far = _first_second(x, y, c)
    order = [2 * chip + c, 2 * chip + 1 - c, 2 * first[2] + c, 2 * second[2] + 1 - c, 2 * second[2] + c,
             2 * first[2] + 1 - c, 2 * far[2] + c, 2 * far[2] + 1 - c]
    return jnp.stack(order).astype(jnp.int32)


def _in_proj(x, g_mix, w_in_own, order, comm=None):
    s = x.shape[0]
    tm = min(TM_ROWS, s)
    ni = s // tm

    def body(order_ref, x_ref, g_ref, own_ref, z_ref, nt_ref, wg_ref, n_s, w_s, send_sems, recv_sems, local_sems):
        j, i = pl.program_id(0), pl.program_id(1)
        px, py, c = _place()
        chip = 2 * px + py
        me = 2 * chip + c
        sib = (px, py, 1 - c)
        chips = _other_chips(px, py)

        def rc(k, src, blk, to):
            return pltpu.make_async_remote_copy(src_ref=src, dst_ref=w_s.at[blk], send_sem=send_sems.at[k],
                                                recv_sem=recv_sems.at[k], device_id=to, device_id_type=MESH)

        del chips
        first, second, far = _first_second(px, py, c)
        blocks = [2 * first[2] + c, 2 * second[2] + c, 2 * far[2] + c]
        own_in = pltpu.make_async_copy(own_ref, w_s.at[me], local_sems.at[0])
        to_first = rc(1, own_ref, me, (first[0], first[1], c))
        to_second = rc(2, own_ref, me, (second[0], second[1], c))
        relay = rc(3, w_s.at[blocks[0]], blocks[0], (second[0], second[1], c))
        sends = [rc(0, own_ref, me, sib), to_first, to_second, relay]
        passed = [rc(4 + q, w_s.at[blk], blk, sib) for q, blk in enumerate(blocks)]
        keep = pltpu.make_async_copy(w_s, wg_ref, local_sems.at[1])

        @pl.when((i == 0) & (j == 0))
        def _():
            own_in.start()
            sends[0].start()
            to_first.start()
            own_in.wait()

        @pl.when((i == 0) & (j == 1))
        def _():
            rc(0, own_ref, 2 * chip + 1 - c, sib).wait_recv()

        for q, blk in enumerate(blocks):
            @pl.when((i == 0) & (j == 2 + 2 * q))
            def _():
                rc(1 + q, own_ref, blk, sib).wait_recv()
                passed[q].start()
                if q == 0:
                    to_second.start()
                    relay.start()

            @pl.when((i == 0) & (j == 3 + 2 * q))
            def _():
                rc(4 + q, own_ref, order_ref[j], sib).wait_recv()

        rows = pl.ds(pl.multiple_of(i * tm, tm), tm)

        @pl.when(j == 0)
        def _():
            xv = x_ref[...]
            rstd = lax.rsqrt(jnp.mean(xv * xv, axis=-1, keepdims=True) + NORM_EPS)
            nb = (xv * rstd * g_ref[...]).astype(BF16)
            n_s[rows, :] = nb
            nt_ref[...] = nb.T

        z_ref[...] = _dot(n_s[rows, :], w_s[order_ref[j]]).astype(BF16)

        @pl.when((i == 0) & (j == N_SLOT - 1))
        def _():
            keep.start()

        @pl.when((i == ni - 1) & (j == N_SLOT - 1))
        def _():
            for cp in sends + passed:
                cp.wait_send()
            keep.wait()

    first_pass = lambda j, i, o: (jnp.where(j == 0, i, ni - 1), 0)
    (z, n1, w_in_g), extra = _call(
        body, name="in_proj", grid=(N_SLOT, ni), prefetch=(order,),
        in_specs=[pl.BlockSpec((tm, D), first_pass),
                  pl.BlockSpec((1, D), lambda j, i, o: (0, 0)), ANY],
        out_specs=[pl.BlockSpec((tm, W_IN_COLS), lambda j, i, o: (i, o[j])),
                   pl.BlockSpec((D, tm), lambda j, i, o: (0, jnp.where(j == 0, i, ni - 1))), ANY],
        out_shape=[SDS((s, N_SLOT * W_IN_COLS), BF16), SDS((D, s), BF16), SDS((N_SLOT, D, W_IN_COLS), BF16)],
        scratch_shapes=[pltpu.VMEM((s, D), BF16), pltpu.VMEM((N_SLOT, D, W_IN_COLS), BF16),
                        pltpu.SemaphoreType.DMA((7,)), pltpu.SemaphoreType.DMA((7,)), pltpu.SemaphoreType.DMA((2,))],
        params=_cparams(("arbitrary", "arbitrary"), 56), args=(x, g_mix, w_in_own), comm=comm)
    return (z, n1, w_in_g), extra


def _decay(r, sp_lam):
    log_a = (-LRU_C) * r * sp_lam
    a = jnp.exp(log_a)
    return a, jnp.sqrt(-jnp.tanh(log_a) * (a * a + 1.0))


def _lru_gates(xc, xcb, wr_ref, br, wi_ref, bi, sp_lam, a_s, b_s, r_ref, i_ref):
    for h in range(HEADS):
        sl = slice(h * HEAD_DIM, (h + 1) * HEAD_DIM)
        r = jax.nn.sigmoid(_dot(xcb[:, sl], wr_ref[h]) + br[:, sl])
        ig = jax.nn.sigmoid(_dot(xcb[:, sl], wi_ref[h]) + bi[:, sl])
        a, mult = _decay(r, sp_lam[:, sl])
        a_s[:, sl] = a
        b_s[:, sl] = xc[:, sl] * ig * mult
        r_ref[:, sl] = r.astype(BF16)
        i_ref[:, sl] = ig.astype(BF16)


def _conv_fwd(xa, prev8, cw, cb):
    xc = cb + cw[0:1, :] * xa
    for k in range(1, CONV_K):
        xc = xc + cw[k:k + 1, :] * _rows_shifted(prev8, xa, k)
    return xc


def _branch_a_fwd(z, conv_w, conv_b, w_r, b_r, w_i, b_i, lam, comm=None):
    s = z.shape[0]
    ta = min(T_BRANCH_A, s)
    per16 = ta // 16

    def body(xa_ref, xp_ref, ga_ref, cw_ref, cb_ref, wr_ref, br_ref, wi_ref, bi_ref, lam_ref,
             ya_ref, hs_ref, xc_ref, r_ref, i_ref, a_s, b_s, h_s, carry_s):
        i = pl.program_id(0)

        @pl.when(i == 0)
        def _():
            carry_s[...] = jnp.zeros_like(carry_s)

        xa = xa_ref[...].astype(F32)
        prev8 = jnp.where(i > 0, xp_ref[...].astype(F32)[8:16], 0.0)
        xc = _conv_fwd(xa, prev8, cw_ref[...], cb_ref[...])
        xcb = xc.astype(BF16)
        xc_ref[...] = xcb
        sp_lam = _softplus(-lam_ref[...])
        _lru_gates(xc, xcb, wr_ref, br_ref[...], wi_ref, bi_ref[...], sp_lam, a_s, b_s, r_ref, i_ref)

        row = lax.broadcasted_iota(jnp.int32, (8, D), 0)

        def group(g, carry):
            off = pl.multiple_of(g * 8, 8)
            a8 = a_s[pl.ds(off, 8), :]
            b8 = b_s[pl.ds(off, 8), :]
            for d in (1, 2, 4):
                a_sh = jnp.where(row >= d, pltpu.roll(a8, d, 0), 1.0)
                b_sh = jnp.where(row >= d, pltpu.roll(b8, d, 0), 0.0)
                b8 = a8 * b_sh + b8
                a8 = a8 * a_sh
            h8 = b8 + a8 * carry
            h_s[pl.ds(off, 8), :] = h8
            return jnp.broadcast_to(h8[7:8, :], (8, D))

        carry_s[...] = lax.fori_loop(0, ta // 8, group, carry_s[...])
        hs = h_s[...]
        hs_ref[...] = hs.astype(BF16)
        ya_ref[...] = (hs * _gelu(ga_ref[...].astype(F32))).astype(BF16)

    vec = pl.BlockSpec((1, D), lambda i: (0, 0))
    gate = pl.BlockSpec((HEADS, HEAD_DIM, HEAD_DIM), lambda i: (0, 0, 0))
    return _call(
        body, name="branch_a_fwd", grid=(s // ta,),
        in_specs=[pl.BlockSpec((ta, D), lambda i: (i, 0)),
                  pl.BlockSpec((16, D), lambda i: (jnp.maximum(i * per16 - 1, 0), 0)),
                  pl.BlockSpec((ta, D), lambda i: (i, 1)),
                  pl.BlockSpec((CONV_K, D), lambda i: (0, 0)), vec, gate, vec, gate, vec, vec],
        out_specs=[pl.BlockSpec((ta, D), lambda i: (i, 0))] * 5,
        out_shape=[SDS((s, D), BF16)] * 5,
        scratch_shapes=[pltpu.VMEM((ta, D), F32), pltpu.VMEM((ta, D), F32), pltpu.VMEM((ta, D), F32),
                        pltpu.VMEM((8, D), F32)],
        params=_cparams(("arbitrary",), 40), args=(z, z, z, conv_w, conv_b, w_r, b_r, w_i, b_i, lam), comm=comm)


def _sgu_common(ub, vb, lg, lb, with_grad):
    if with_grad:
        u, du = _gelu_and_grad(ub)
        v, dv = _gelu_and_grad(vb)
    else:
        u, v, du, dv = _gelu(ub), _gelu(vb), None, None
    mu = jnp.mean(v, axis=-1, keepdims=True)
    vc = v - mu
    rstd = lax.rsqrt(jnp.mean(vc * vc, axis=-1, keepdims=True) + LN_EPS)
    vhat = vc * rstd
    vln = vhat * lg + lb
    return u, du, dv, rstd, vhat, vln


def _masked_ws(ws_ref):
    t = lax.broadcasted_iota(jnp.int32, (CHUNK, CHUNK), 0)
    c = lax.broadcasted_iota(jnp.int32, (CHUNK, CHUNK), 1)
    keep = c <= t
    return [jnp.where(keep, ws_ref[g], 0.0).astype(BF16) for g in range(GROUPS)]


def _branch_b_merge_out(ya, z, x, ln_g, ln_b, w_s, b_s_t, w_oa, w_ob, w_out, comm=None):
    s = x.shape[0]
    tm = min(TM_MERGE, s)
    assert tm % CHUNK == 0

    def body(ub_ref, vb_ref, lg_ref, lb_ref, ws_ref, bs_ref, ya_ref, ma_ref, mb_ref, x_ref, woa_ref, wob_ref, wo_ref,
             yb_ref, pa_ref, pb_ref, mg_ref, h1_ref):
        u, _, _, _, _, vln = _sgu_common(ub_ref[...].astype(F32), vb_ref[...].astype(F32),
                                         lg_ref[...], lb_ref[...], False)
        vlnb = vln.astype(BF16)
        wm = _masked_ws(ws_ref)
        bs = bs_ref[...]
        for c in range(tm // CHUNK):
            rs = slice(c * CHUNK, (c + 1) * CHUNK)
            for g in range(GROUPS):
                cs = slice(g * GROUP_DIM, (g + 1) * GROUP_DIM)
                sp = _dot(wm[g], vlnb[rs, cs]) + bs[:, g:g + 1]
                yb_ref[rs, cs] = (u[rs, cs] * sp).astype(BF16)

        pa = _dot(ya_ref[...], woa_ref[...])
        pb = _dot(yb_ref[...], wob_ref[...])
        merged = (jax.nn.sigmoid(ma_ref[...].astype(F32)) * pa
                  + jax.nn.sigmoid(mb_ref[...].astype(F32)) * pb).astype(BF16)
        pa_ref[...] = pa.astype(BF16)
        pb_ref[...] = pb.astype(BF16)
        mg_ref[...] = merged
        h1_ref[...] = x_ref[...] + _dot(merged, wo_ref[...])

    row = pl.BlockSpec((tm, D), lambda i: (i, 0))
    col = lambda q: pl.BlockSpec((tm, D), lambda i: (i, q))
    vec = pl.BlockSpec((1, D), lambda i: (0, 0))
    wsp = pl.BlockSpec((D, D), lambda i: (0, 0))
    return _call(
        body, name="branch_b_merge_out", grid=(s // tm,),
        in_specs=[col(2), col(3), vec, vec, pl.BlockSpec((GROUPS, CHUNK, CHUNK), lambda i: (0, 0, 0)),
                  pl.BlockSpec((CHUNK, GROUPS), lambda i: (0, 0)), row, col(4), col(5), row, wsp, wsp, wsp],
        out_specs=[row, row, row, row, row],
        out_shape=[SDS((s, D), BF16)] * 4 + [SDS((s, D), F32)], scratch_shapes=[],
        params=_cparams(("arbitrary",), 56),
        args=(z, z, ln_g, ln_b, w_s, b_s_t, ya, z, z, x, w_oa, w_ob, w_out), comm=comm)


def _mlp_fwd(h1, g_mlp, w_up_g, w_down, g_fin, tgt):
    s = h1.shape[0]
    tm = min(TM_ROWS, s)
    nj = N_SLOT

    def body(h1_ref, gm_ref, wu_ref, wd_ref, gf_ref, t_ref, r_ref, at_ref, n2t_ref, dh2_ref, dh2b_ref, loss_ref,
             dgf_ref, n2_s, acc_s):
        i, j = pl.program_id(0), pl.program_id(1)

        @pl.when(j == 0)
        def _():
            hv = h1_ref[...]
            rstd = lax.rsqrt(jnp.mean(hv * hv, axis=-1, keepdims=True) + NORM_EPS)
            nb = (hv * rstd * gm_ref[...]).astype(BF16)
            n2_s[...] = nb
            n2t_ref[...] = nb.T
            acc_s[...] = jnp.zeros_like(acc_s)

        @pl.when((i == 0) & (j == 0))
        def _():
            loss_ref[...] = jnp.zeros_like(loss_ref)
            dgf_ref[...] = jnp.zeros_like(dgf_ref)

        r = jnp.maximum(_dot(n2_s[...], wu_ref[...]), 0.0)
        r_ref[...] = r.astype(BF16)
        act = (r * r).astype(BF16)
        at_ref[...] = act.T
        acc_s[...] += _dot(act, wd_ref[...])

        @pl.when(j == nj - 1)
        def _():
            h2 = h1_ref[...] + acc_s[...]
            rstd = lax.rsqrt(jnp.mean(h2 * h2, axis=-1, keepdims=True) + NORM_EPS)
            hh = h2 * rstd
            gf = gf_ref[...]
            e = hh * gf - t_ref[...]
            loss_ref[...] += jnp.sum(e * e) * (0.5 / D)
            dy = e * (1.0 / D)
            dgf_ref[...] += jnp.sum(dy * hh, axis=0, keepdims=True)
            dhh = dy * gf
            dh2 = rstd * (dhh - hh * jnp.mean(dhh * hh, axis=-1, keepdims=True))
            dh2_ref[...] = dh2
            dh2b_ref[...] = dh2.astype(BF16)

    row = pl.BlockSpec((tm, D), lambda i, j: (i, 0))
    vec = pl.BlockSpec((1, D), lambda i, j: (0, 0))
    return pl.pallas_call(
        body, name="mlp_fwd", grid=(s // tm, nj),
        in_specs=[row, vec, pl.BlockSpec((None, D, FF_COLS), lambda i, j: (j, 0, 0)),
                  pl.BlockSpec((FF_COLS, D), lambda i, j: (j, 0)), vec, row],
        out_specs=[pl.BlockSpec((tm, FF_COLS), lambda i, j: (i, j)), pl.BlockSpec((FF_COLS, tm), lambda i, j: (j, i)),
                   pl.BlockSpec((D, tm), lambda i, j: (0, i)), row, row, pl.BlockSpec((8, 128), lambda i, j: (0, 0)),
                   vec],
        out_shape=[SDS((s, nj * FF_COLS), BF16), SDS((nj * FF_COLS, s), BF16), SDS((D, s), BF16), SDS((s, D), F32),
                   SDS((s, D), BF16), SDS((8, 128), F32), SDS((1, D), F32)],
        scratch_shapes=[pltpu.VMEM((tm, D), BF16), pltpu.VMEM((tm, D), F32)],
        compiler_params=_cparams(("arbitrary", "arbitrary"), 56),
    )(h1, _small_in_hbm(g_mlp), w_up_g, w_down, _small_in_hbm(g_fin), tgt)


def _mlp_bwd(dh2, dh2b, r, w_down, w_up_g, h1, g_mlp, comm=None):
    s = h1.shape[0]
    tm = min(TM_ROWS, s)
    nj = N_SLOT

    def body(dh2_ref, dh2b_ref, r_ref, wd_ref, wu_ref, h1_ref, gm_ref, df_ref, dh1_ref, dgm_ref, acc_s):
        i, j = pl.program_id(0), pl.program_id(1)

        @pl.when(j == 0)
        def _():
            acc_s[...] = jnp.zeros_like(acc_s)

        @pl.when((i == 0) & (j == 0))
        def _():
            dgm_ref[...] = jnp.zeros_like(dgm_ref)

        d_act = _dot_nt(dh2b_ref[...], wd_ref[...])
        df = (d_act * (2.0 * r_ref[...].astype(F32))).astype(BF16)
        df_ref[...] = df
        acc_s[...] += _dot_nt(df, wu_ref[...])

        @pl.when(j == nj - 1)
        def _():
            hv = h1_ref[...]
            rstd = lax.rsqrt(jnp.mean(hv * hv, axis=-1, keepdims=True) + NORM_EPS)
            hh = hv * rstd
            dn2 = acc_s[...]
            dgm_ref[...] += jnp.sum(dn2 * hh, axis=0, keepdims=True)
            dhat = dn2 * gm_ref[...]
            dh1_ref[...] = dh2_ref[...] + rstd * (dhat - hh * jnp.mean(dhat * hh, axis=-1, keepdims=True))

    row = pl.BlockSpec((tm, D), lambda i, j: (i, 0))
    vec = pl.BlockSpec((1, D), lambda i, j: (0, 0))
    ffb = pl.BlockSpec((tm, FF_COLS), lambda i, j: (i, j))
    return _call(
        body, name="mlp_bwd", grid=(s // tm, nj),
        in_specs=[row, row, ffb, pl.BlockSpec((FF_COLS, D), lambda i, j: (j, 0)),
                  pl.BlockSpec((None, D, FF_COLS), lambda i, j: (j, 0, 0)), row, vec],
        out_specs=[ffb, row, vec],
        out_shape=[SDS((s, nj * FF_COLS), BF16), SDS((s, D), F32), SDS((1, D), F32)],
        scratch_shapes=[pltpu.VMEM((tm, D), F32)],
        params=_cparams(("arbitrary", "arbitrary"), 56), args=(dh2, dh2b, r, w_down, w_up_g, h1, g_mlp), comm=comm)


def _merge_bwd(dh1, z, pa, pb, w_out, w_oa, w_ob, comm=None):
    s = dh1.shape[0]
    tm = min(TM_MERGE, s)

    def body(dh1_ref, ma_ref, mb_ref, pa_ref, pb_ref, wo_ref, woa_ref, wob_ref,
             dz_ref, dpa_ref, dpb_ref, dya_ref, dyb_ref):
        dm = _dot_nt(dh1_ref[...].astype(BF16), wo_ref[...])
        sa = jax.nn.sigmoid(ma_ref[...].astype(F32))
        sb = jax.nn.sigmoid(mb_ref[...].astype(F32))
        dpa = (dm * sa).astype(BF16)
        dpb = (dm * sb).astype(BF16)
        dz_ref[:, 0:D] = (dm * pa_ref[...].astype(F32) * sa * (1.0 - sa)).astype(BF16)
        dz_ref[:, D:2 * D] = (dm * pb_ref[...].astype(F32) * sb * (1.0 - sb)).astype(BF16)
        dpa_ref[...] = dpa
        dpb_ref[...] = dpb
        dya_ref[...] = _dot_nt(dpa, woa_ref[...]).astype(BF16)
        dyb_ref[...] = _dot_nt(dpb, wob_ref[...]).astype(BF16)

    row = pl.BlockSpec((tm, D), lambda i: (i, 0))
    wsp = pl.BlockSpec((D, D), lambda i: (0, 0))
    return _call(
        body, name="merge_bwd", grid=(s // tm,),
        in_specs=[row, pl.BlockSpec((tm, D), lambda i: (i, 4)), pl.BlockSpec((tm, D), lambda i: (i, 5)),
                  row, row, wsp, wsp, wsp],
        out_specs=[pl.BlockSpec((tm, 2 * D), lambda i: (i, 2)), row, row, row, row],
        out_shape=[SDS((s, 6 * D), BF16)] + [SDS((s, D), BF16)] * 4, scratch_shapes=[],
        params=_cparams(("arbitrary",), 48), args=(dh1, z, z, pa, pb, w_out, w_oa, w_ob), comm=comm)


def _branch_b_bwd(dz, dyb, z, ln_g, ln_b, w_s, b_s_t, comm=None):
    s = z.shape[0]
    tb = min(T_BRANCH_B, s)

    def body(dz_in, dyb_ref, ub_ref, vb_ref, lg_ref, lb_ref, ws_ref, bs_ref,
             dz_ref, dws_ref, dbs_ref, dln_ref, du_s, dvln_s):
        del dz_in

        @pl.when(pl.program_id(0) == 0)
        def _():
            dws_ref[...] = jnp.zeros_like(dws_ref)
            dbs_ref[...] = jnp.zeros_like(dbs_ref)
            dln_ref[...] = jnp.zeros_like(dln_ref)

        lg = lg_ref[...]
        u, du, dv, rstd, vhat, vln = _sgu_common(ub_ref[...].astype(F32), vb_ref[...].astype(F32),
                                                 lg, lb_ref[...], True)
        vlnb = vln.astype(BF16)
        dyb_v = dyb_ref[...].astype(F32)
        wm = _masked_ws(ws_ref)
        keep = (lax.broadcasted_iota(jnp.int32, (CHUNK, CHUNK), 1)
                <= lax.broadcasted_iota(jnp.int32, (CHUNK, CHUNK), 0))
        bs = bs_ref[...]
        for c in range(tb // CHUNK):
            rs = slice(c * CHUNK, (c + 1) * CHUNK)
            for g in range(GROUPS):
                cs = slice(g * GROUP_DIM, (g + 1) * GROUP_DIM)
                v_blk = vlnb[rs, cs]
                sp = _dot(wm[g], v_blk) + bs[:, g:g + 1]
                d_sp = dyb_v[rs, cs] * u[rs, cs]
                d_spb = d_sp.astype(BF16)
                du_s[rs, cs] = dyb_v[rs, cs] * sp
                dvln_s[rs, cs] = _dot_tn(wm[g], d_spb)
                dws_ref[g] += jnp.where(keep, _dot_nt(d_spb, v_blk), 0.0)
                dbs_ref[g] += jnp.broadcast_to(jnp.sum(d_sp, axis=-1, keepdims=True), (CHUNK, CHUNK))
        dvln = dvln_s[...]
        dln_ref[0:1, :] += jnp.sum(dvln * vhat, axis=0, keepdims=True)
        dln_ref[1:2, :] += jnp.sum(dvln, axis=0, keepdims=True)
        dvh = dvln * lg
        d_v = rstd * (dvh - jnp.mean(dvh, axis=-1, keepdims=True)
                      - vhat * jnp.mean(dvh * vhat, axis=-1, keepdims=True))
        dz_ref[:, 0:D] = (du_s[...] * du).astype(BF16)
        dz_ref[:, D:2 * D] = (d_v * dv).astype(BF16)

    vec = pl.BlockSpec((1, D), lambda i: (0, 0))
    sq = pl.BlockSpec((GROUPS, CHUNK, CHUNK), lambda i: (0, 0, 0))
    return _call(
        body, name="branch_b_bwd", grid=(s // tb,),
        in_specs=[ANY, pl.BlockSpec((tb, D), lambda i: (i, 0)),
                  pl.BlockSpec((tb, D), lambda i: (i, 2)), pl.BlockSpec((tb, D), lambda i: (i, 3)), vec, vec, sq,
                  pl.BlockSpec((CHUNK, GROUPS), lambda i: (0, 0))],
        out_specs=[pl.BlockSpec((tb, 2 * D), lambda i: (i, 1)), sq, sq, pl.BlockSpec((8, D), lambda i: (0, 0))],
        out_shape=[SDS(dz.shape, BF16), SDS((GROUPS, CHUNK, CHUNK), F32), SDS((GROUPS, CHUNK, CHUNK), F32),
                   SDS((8, D), F32)],
        scratch_shapes=[pltpu.VMEM((tb, D), F32), pltpu.VMEM((tb, D), F32)], aliases={0: 0},
        params=_cparams(("arbitrary",), 40), args=(dz, dyb, z, z, ln_g, ln_b, w_s, b_s_t), comm=comm)


def _branch_a_bwd(dz, dya, z, hs, xc, r, ig, conv_w, w_r, w_i, lam, comm=None):
    s = z.shape[0]
    ta = min(T_BRANCH_A, s)
    nb = s // ta
    per16 = ta // 16

    def body(dz_in, dya_ref, xa_ref, ga_ref, hs_ref, hp_ref, xc_ref, r_ref, i_ref, cw_ref, wr_ref, wi_ref,
             lam_ref, dz_ref, vec_ref, dwr_ref, dwi_ref, a_s, b_s, h_s, dcar_s, acar_s, dxc_s):
        del dz_in
        i = pl.program_id(0)
        blk = nb - 1 - i

        @pl.when(i == 0)
        def _():
            dcar_s[...] = jnp.zeros_like(dcar_s)
            acar_s[...] = jnp.zeros_like(acar_s)
            dxc_s[...] = jnp.zeros_like(dxc_s)
            vec_ref[...] = jnp.zeros_like(vec_ref)
            dwr_ref[...] = jnp.zeros_like(dwr_ref)
            dwi_ref[...] = jnp.zeros_like(dwi_ref)

        cw = cw_ref[...]
        lam_v = lam_ref[...]
        xa = xa_ref[...].astype(F32)
        xcb = xc_ref[...]
        xc = xcb.astype(F32)
        sp_lam = _softplus(-lam_v)
        r_v, i_v = r_ref[...].astype(F32), i_ref[...].astype(F32)
        a_v, m_v = _decay(r_v, sp_lam)

        hs_v = hs_ref[...].astype(F32)
        hprev8 = jnp.where(blk > 0, hp_ref[...].astype(F32)[8:16], 0.0)
        h_m1 = _rows_shifted(hprev8, hs_v, 1)
        gg, dgg = _gelu_and_grad(ga_ref[...].astype(F32))
        dya_v = dya_ref[...].astype(F32)
        dz_ref[:, D:2 * D] = (dya_v * hs_v * dgg).astype(BF16)

        a_s[...] = _rows_advanced(a_v, acar_s[...], 1)
        b_s[...] = dya_v * gg

        row = lax.broadcasted_iota(jnp.int32, (8, D), 0)
        ng = ta // 8

        def group(gi, carry):
            off = pl.multiple_of((ng - 1 - gi) * 8, 8)
            c8 = a_s[pl.ds(off, 8), :]
            d8 = b_s[pl.ds(off, 8), :]
            for d in (1, 2, 4):
                c_sh = jnp.where(row < 8 - d, pltpu.roll(c8, 8 - d, 0), 1.0)
                d_sh = jnp.where(row < 8 - d, pltpu.roll(d8, 8 - d, 0), 0.0)
                d8 = c8 * d_sh + d8
                c8 = c8 * c_sh
            dh8 = d8 + c8 * carry
            h_s[pl.ds(off, 8), :] = dh8
            return jnp.broadcast_to(dh8[0:1, :], (8, D))

        dcar_s[...] = lax.fori_loop(0, ng, group, dcar_s[...])
        acar_s[...] = jnp.broadcast_to(a_v[0:1, :], (8, D))

        dbx = h_s[...]
        d_mult = dbx * xc * i_v
        d_loga = dbx * h_m1 * a_v - d_mult * (a_v * a_v) / m_v
        d_pr = d_loga * ((-LRU_C) * sp_lam) * r_v * (1.0 - r_v)
        d_pi = dbx * xc * m_v * i_v * (1.0 - i_v)
        vec_ref[7:8, :] += jnp.sum(d_loga * r_v, axis=0, keepdims=True) * (LRU_C * jax.nn.sigmoid(-lam_v))
        vec_ref[5:6, :] += jnp.sum(d_pr, axis=0, keepdims=True)
        vec_ref[6:7, :] += jnp.sum(d_pi, axis=0, keepdims=True)
        d_prb = d_pr.astype(BF16)
        d_pib = d_pi.astype(BF16)
        h_s[...] = dbx * i_v * m_v
        for h in range(HEADS):
            sl = slice(h * HEAD_DIM, (h + 1) * HEAD_DIM)
            h_s[:, sl] += _dot_nt(d_prb[:, sl], wr_ref[h]) + _dot_nt(d_pib[:, sl], wi_ref[h])
            dwr_ref[h] += _dot_tn(xcb[:, sl], d_prb[:, sl])
            dwi_ref[h] += _dot_tn(xcb[:, sl], d_pib[:, sl])
        d_xc = h_s[...]
        vec_ref[4:5, :] += jnp.sum(d_xc, axis=0, keepdims=True)
        vec_ref[0:1, :] += jnp.sum(d_xc * xa, axis=0, keepdims=True)
        d_xa = cw[0:1, :] * d_xc
        nxt = dxc_s[...]
        for k in range(1, CONV_K):
            ahead = _rows_advanced(d_xc, nxt, k)
            vec_ref[k:k + 1, :] += jnp.sum(ahead * xa, axis=0, keepdims=True)
            d_xa = d_xa + cw[k:k + 1, :] * ahead
        dz_ref[:, 0:D] = d_xa.astype(BF16)
        dxc_s[...] = d_xc[0:8, :]

    vec = pl.BlockSpec((1, D), lambda i: (0, 0))
    gate = pl.BlockSpec((HEADS, HEAD_DIM, HEAD_DIM), lambda i: (0, 0, 0))
    cur = lambda c: pl.BlockSpec((ta, D), lambda i: (nb - 1 - i, c))
    before = lambda c: pl.BlockSpec((16, D), lambda i: (jnp.maximum((nb - 1 - i) * per16 - 1, 0), c))
    return _call(
        body, name="branch_a_bwd", grid=(nb,),
        in_specs=[ANY, cur(0), cur(0), cur(1), cur(0), before(0), cur(0), cur(0), cur(0),
                  pl.BlockSpec((CONV_K, D), lambda i: (0, 0)), gate, gate, vec],
        out_specs=[pl.BlockSpec((ta, 2 * D), lambda i: (nb - 1 - i, 0)), pl.BlockSpec((8, D), lambda i: (0, 0)),
                   gate, gate],
        out_shape=[SDS(dz.shape, BF16), SDS((8, D), F32), SDS((HEADS, HEAD_DIM, HEAD_DIM), F32),
                   SDS((HEADS, HEAD_DIM, HEAD_DIM), F32)],
        scratch_shapes=[pltpu.VMEM((ta, D), F32)] * 3 + [pltpu.VMEM((8, D), F32)] * 3, aliases={0: 0},
        params=_cparams(("arbitrary",), 48),
        args=(dz, dya, z, z, hs, hs, xc, r, ig, conv_w, w_r, w_i, lam), comm=comm)


def _in_bwd(dz, w_in_g, x, dh1, g_mix, comm=None):
    s = x.shape[0]
    tm = min(TM_ROWS, s)
    nj = N_SLOT

    def body(dz_ref, w_ref, x_ref, dh1_ref, g_ref, dx_ref, dg_ref, acc_s):
        i, j = pl.program_id(0), pl.program_id(1)

        @pl.when(j == 0)
        def _():
            acc_s[...] = jnp.zeros_like(acc_s)

        @pl.when((i == 0) & (j == 0))
        def _():
            dg_ref[...] = jnp.zeros_like(dg_ref)

        acc_s[...] += _dot_nt(dz_ref[...], w_ref[...])

        @pl.when(j == nj - 1)
        def _():
            xv = x_ref[...]
            rstd = lax.rsqrt(jnp.mean(xv * xv, axis=-1, keepdims=True) + NORM_EPS)
            xh = xv * rstd
            dn = acc_s[...]
            dg_ref[...] += jnp.sum(dn * xh, axis=0, keepdims=True)
            dhat = dn * g_ref[...]
            dx_ref[...] = dh1_ref[...] + rstd * (dhat - xh * jnp.mean(dhat * xh, axis=-1, keepdims=True))

    row = pl.BlockSpec((tm, D), lambda i, j: (i, 0))
    vec = pl.BlockSpec((1, D), lambda i, j: (0, 0))
    return _call(
        body, name="in_bwd", grid=(s // tm, nj),
        in_specs=[pl.BlockSpec((tm, W_IN_COLS), lambda i, j: (i, j)),
                  pl.BlockSpec((None, D, W_IN_COLS), lambda i, j: (j, 0, 0)), row, row, vec],
        out_specs=[row, vec],
        out_shape=[SDS((s, D), F32), SDS((1, D), F32)],
        scratch_shapes=[pltpu.VMEM((tm, D), F32)],
        params=_cparams(("arbitrary", "arbitrary"), 48), args=(dz, w_in_g, x, dh1, g_mix), comm=comm)


def _wgrad(name, a, b):
    s = a.shape[0]
    ts = min(TM_ROWS, s)
    a_w, b_w = a.shape[1], b.shape[1]

    def body(a_ref, b_ref, o_ref, acc_s):
        t = pl.program_id(0)

        @pl.when(t == 0)
        def _():
            acc_s[...] = jnp.zeros_like(acc_s)

        acc_s[...] += _dot_tn(a_ref[...].astype(BF16), b_ref[...].astype(BF16))

        @pl.when(t == pl.num_programs(0) - 1)
        def _():
            o_ref[...] = acc_s[...].astype(BF16)

    return pl.pallas_call(
        body, name=name, grid=(s // ts,),
        in_specs=[pl.BlockSpec((ts, a_w), lambda t: (t, 0)), pl.BlockSpec((ts, b_w), lambda t: (t, 0))],
        out_specs=pl.BlockSpec((a_w, b_w), lambda t: (0, 0)),
        out_shape=SDS((a_w, b_w), BF16),
        scratch_shapes=[pltpu.VMEM((a_w, b_w), F32)],
        compiler_params=_cparams(("arbitrary",), 48),
    )(a, b)


def _place():
    x, y, c = lax.axis_index("x"), lax.axis_index("y"), lax.axis_index("c")
    return x, y, c


def _other_chips(x, y):
    return [(x, 1 - y, 2 * x + 1 - y), (1 - x, y, 2 * (1 - x) + y), (1 - x, 1 - y, 2 * (1 - x) + 1 - y)]


class _Plan:
    def __init__(self, arrays, out_shape, sems, start, finish, middle=None, middle_at=6):
        self.arrays, self.out_shape, self.sems, self.start, self.finish = arrays, out_shape, sems, start, finish
        self.middle, self.middle_at = middle, middle_at


def _gather_plan(shards, middle_at=6):
    n = len(shards)

    def copies(ins, outs, sems):
        send_sems, recv_sems, local_sems = sems
        x, y, c = _place()
        chip = 2 * x + y
        me = 2 * chip + c
        sib = (x, y, 1 - c)
        chips = _other_chips(x, y)

        def rc(k, t, src, blk, to):
            return pltpu.make_async_remote_copy(
                src_ref=src, dst_ref=outs[t].at[blk], send_sem=send_sems.at[k * n + t],
                recv_sem=recv_sems.at[k * n + t], device_id=to, device_id_type=MESH)

        (yx, yy, y_chip), (xx, xy, x_chip), _ = chips
        local = [pltpu.make_async_copy(ins[t], outs[t].at[me], local_sems.at[t]) for t in range(n)]
        sends = ([rc(0, t, ins[t], me, sib) for t in range(n)] + [rc(1, t, ins[t], me, (yx, yy, c)) for t in range(n)]
                 + [rc(2, t, ins[t], me, (xx, xy, c)) for t in range(n)])
        passed = [[rc(4 + j, t, outs[t].at[2 * pc + c], 2 * pc + c, sib) for t in range(n)]
                  for j, (_, _, pc) in enumerate(chips)]
        relays = [[rc(3, t, outs[t].at[2 * y_chip + c], 2 * y_chip + c, (xx, xy, c)) for t in range(n)],
                  [rc(3, t, outs[t].at[2 * x_chip + c], 2 * x_chip + c, (yx, yy, c)) for t in range(n)]]
        return rc, local, sends, passed, relays, chips, chip, c, sib

    def start(ins, outs, sems):
        _, local, sends, _, _, _, _, _, _ = copies(ins, outs, sems)
        for cp in local + sends:
            cp.start()

    def middle(ins, outs, sems):
        rc, _, _, passed, relays, chips, _, c, sib = copies(ins, outs, sems)
        for j in range(2):
            for t in range(n):
                rc(1 + j, t, ins[t], 2 * chips[j][2] + c, sib).wait_recv()
        for j in range(2):
            for cp in passed[j]:
                cp.start()

            @pl.when(c == j)
            def _():
                for cp in relays[j]:
                    cp.start()

    def finish(ins, outs, sems):
        rc, local, sends, passed, relays, chips, chip, c, sib = copies(ins, outs, sems)
        far = 2 * chips[2][2] + c
        for t in range(n):
            rc(3, t, ins[t], far, sib).wait_recv()
        for cp in passed[2]:
            cp.start()
        for t in range(n):
            rc(0, t, ins[t], 2 * chip + 1 - c, sib).wait_recv()
        for j, (px, py, pc) in enumerate(chips):
            for t in range(n):
                rc(4 + j, t, ins[t], 2 * pc + 1 - c, sib).wait_recv()
        for cp in sends + passed[0] + passed[1] + passed[2]:
            cp.wait_send()
        for j in range(2):
            @pl.when(c == j)
            def _():
                for cp in relays[j]:
                    cp.wait_send()
        for cp in local:
            cp.wait()

    return _Plan(list(shards), [SDS((N_SLOT,) + tuple(a.shape), a.dtype) for a in shards],
                 [pltpu.SemaphoreType.DMA((7 * n,)), pltpu.SemaphoreType.DMA((7 * n,)),
                  pltpu.SemaphoreType.DMA((n,))], start, finish, middle, middle_at)


def _sibling_plan(grads, whole=()):
    n, m = len(grads), len(whole)

    def copies(ins, outs, sems):
        send_sems, recv_sems = sems
        x, y, c = _place()
        sib = (x, y, 1 - c)

        def rc(t, src, dst):
            return pltpu.make_async_remote_copy(src_ref=src, dst_ref=dst, send_sem=send_sems.at[t],
                                                recv_sem=recv_sems.at[t], device_id=sib, device_id_type=MESH)
        return rc, c

    def start(ins, outs, sems):
        rc, c = copies(ins, outs, sems)
        for t in range(n):
            for j in range(4):
                rc(t, ins[t].at[2 * j + 1 - c], outs[t].at[j]).start()
        for t in range(n, n + m):
            rc(t, ins[t], outs[t]).start()

    def finish(ins, outs, sems):
        rc, _ = copies(ins, outs, sems)
        for t in range(n):
            rc(t, ins[t].at[pl.ds(0, 4)], outs[t]).wait()
        for t in range(n, n + m):
            rc(t, ins[t], outs[t]).wait()

    return _Plan(list(grads) + list(whole),
                 [SDS((4,) + tuple(g.shape[1:]), g.dtype) for g in grads] + [SDS(a.shape, a.dtype) for a in whole],
                 [pltpu.SemaphoreType.DMA((n + m,)), pltpu.SemaphoreType.DMA((n + m,))], start, finish)


def _chips_plan(parts, whole=()):
    n, m = len(parts), len(whole)

    def src_of(ins, t, pc):
        return ins[t].at[pc] if t < n else ins[t]

    def local_copies(ins, outs, sems, chip):
        return [pltpu.make_async_copy(src_of(ins, t, chip), outs[t].at[chip], sems[2].at[t]) for t in range(n + m)]

    def start(ins, outs, sems):
        send_sems, recv_sems, _ = sems
        x, y, c = _place()
        chip = 2 * x + y
        for cp in local_copies(ins, outs, sems, chip):
            cp.start()
        for px, py, pc in _other_chips(x, y):
            for t in range(n + m):
                pltpu.make_async_remote_copy(src_ref=src_of(ins, t, pc), dst_ref=outs[t].at[chip],
                                             send_sem=send_sems.at[t], recv_sem=recv_sems.at[t],
                                             device_id=(px, py, c), device_id_type=MESH).start()

    def finish(ins, outs, sems):
        send_sems, recv_sems, _ = sems
        x, y, c = _place()
        for t in range(n + m):
            three = outs[t].at[pl.ds(0, 3)]
            pltpu.make_async_remote_copy(src_ref=three, dst_ref=three, send_sem=send_sems.at[t],
                                         recv_sem=recv_sems.at[t], device_id=(x, y, c), device_id_type=MESH).wait()
        for cp in local_copies(ins, outs, sems, 2 * x + y):
            cp.wait()

    return _Plan(list(parts) + list(whole),
                 [SDS(p.shape, p.dtype) for p in parts] + [SDS((4,) + tuple(a.shape), a.dtype) for a in whole],
                 [pltpu.SemaphoreType.DMA((n + m,)), pltpu.SemaphoreType.DMA((n + m,)),
                  pltpu.SemaphoreType.DMA((n + m,))], start, finish)


def _exchange_plan(arr):
    def peers(x, y, c):
        flip = lambda v, f: 1 - v if f else v
        return [(flip(x, fx), flip(y, fy), flip(c, fc))
                for fx in (0, 1) for fy in (0, 1) for fc in (0, 1) if fx or fy or fc]

    def start(ins, outs, sems):
        x, y, c = _place()
        me = 4 * x + 2 * y + c
        pltpu.make_async_copy(ins[0], outs[0].at[me], sems[2].at[0]).start()
        for to in peers(x, y, c):
            pltpu.make_async_remote_copy(src_ref=ins[0], dst_ref=outs[0].at[me], send_sem=sems[0].at[0],
                                         recv_sem=sems[1].at[0], device_id=to, device_id_type=MESH).start()

    def finish(ins, outs, sems):
        x, y, c = _place()
        seven = outs[0].at[pl.ds(0, 7)]
        pltpu.make_async_remote_copy(src_ref=seven, dst_ref=seven, send_sem=sems[0].at[0], recv_sem=sems[1].at[0],
                                     device_id=(x, y, c), device_id_type=MESH).wait()
        pltpu.make_async_copy(ins[0], outs[0].at[4 * x + 2 * y + c], sems[2].at[0]).wait()

    return _Plan([arr], [SDS((N_SLOT,) + tuple(arr.shape), arr.dtype)],
                 [pltpu.SemaphoreType.DMA((1,)), pltpu.SemaphoreType.DMA((1,)), pltpu.SemaphoreType.DMA((1,))],
                 start, finish)


def _join(*plans):
    def cut(seq, sizes):
        out, at = [], 0
        for k in sizes:
            out.append(seq[at:at + k])
            at += k
        return out

    n_arr = [len(p.arrays) for p in plans]
    n_sem = [len(p.sems) for p in plans]

    def start(ins, outs, sems):
        for p, i, o, s in zip(plans, cut(ins, n_arr), cut(outs, n_arr), cut(sems, n_sem)):
            p.start(i, o, s)

    def finish(ins, outs, sems):
        for p, i, o, s in zip(plans, cut(ins, n_arr), cut(outs, n_arr), cut(sems, n_sem)):
            p.finish(i, o, s)

    def middle(ins, outs, sems):
        for p, i, o, s in zip(plans, cut(ins, n_arr), cut(outs, n_arr), cut(sems, n_sem)):
            if p.middle is not None:
                p.middle(i, o, s)

    return _Plan([a for p in plans for a in p.arrays], [o for p in plans for o in p.out_shape],
                 [s for p in plans for s in p.sems], start, finish,
                 middle if any(p.middle is not None for p in plans) else None,
                 max(p.middle_at for p in plans))


def _run_plan(name, plan):
    k = len(plan.arrays)

    def body(*refs):
        ins, outs, sems = refs[:k], refs[k:2 * k], refs[2 * k:]
        plan.start(ins, outs, sems)
        if plan.middle is not None:
            plan.middle(ins, outs, sems)
        plan.finish(ins, outs, sems)

    return pl.pallas_call(
        body, name=name, in_specs=[ANY] * k, out_specs=[ANY] * k, out_shape=plan.out_shape,
        scratch_shapes=plan.sems, compiler_params=pltpu.CompilerParams(has_side_effects=True),
    )(*plan.arrays)


def _call(body, *, name, grid, in_specs, out_specs, out_shape, scratch_shapes, params, args, comm=None,
          aliases=None, prefetch=()):
    aliases = aliases or {}
    n_pre = len(prefetch)

    def launch(fn, ins_specs, outs_specs, outs_shape, scratch, operands):
        spec = pltpu.PrefetchScalarGridSpec(num_scalar_prefetch=n_pre, grid=grid, in_specs=ins_specs,
                                            out_specs=outs_specs, scratch_shapes=scratch)
        return pl.pallas_call(fn, name=name, grid_spec=spec, out_shape=outs_shape, compiler_params=params,
                              input_output_aliases=aliases)(*prefetch, *[_small_in_hbm(a) for a in operands])

    if comm is None:
        return list(launch(body, in_specs, out_specs, out_shape, scratch_shapes, args)), []
    n_in, n_out, n_scr, k = len(in_specs), len(out_specs), len(scratch_shapes), len(comm.arrays)

    def wrapped(*refs):
        pre, refs = refs[:n_pre], refs[n_pre:]
        ins = refs[:n_in]
        c_in = refs[n_in:n_in + k]
        outs = refs[n_in + k:n_in + k + n_out]
        c_out = refs[n_in + k + n_out:n_in + 2 * k + n_out]
        scr = refs[n_in + 2 * k + n_out:n_in + 2 * k + n_out + n_scr]
        sems = refs[n_in + 2 * k + n_out + n_scr:]
        step, steps = pl.program_id(0), grid[0]
        for d in range(1, len(grid)):
            step, steps = step * grid[d] + pl.program_id(d), steps * grid[d]

        @pl.when(step == 0)
        def _():
            comm.start(c_in, c_out, sems)

        if comm.middle is not None:
            @pl.when(step == (comm.middle_at * steps) // 8)
            def _():
                comm.middle(c_in, c_out, sems)

        body(*pre, *ins, *outs, *scr)

        @pl.when(step == steps - 1)
        def _():
            comm.finish(c_in, c_out, sems)

    res = launch(wrapped, list(in_specs) + [ANY] * k, list(out_specs) + [ANY] * k,
                 list(out_shape) + list(comm.out_shape), list(scratch_shapes) + list(comm.sems),
                 tuple(args) + tuple(comm.arrays))
    return list(res[:n_out]), list(res[n_out:])


def _wgrad_paired(name, a_t, b, core, by_rows=False, comm=None):
    s = b.shape[0]
    m = a_t.shape[0] // N_SLOT if by_rows else a_t.shape[0]
    cols = b.shape[1] if by_rows else b.shape[1] // N_SLOT
    half = N_SLOT // 2

    def owner(k, c):
        return 2 * (k % half) + jnp.where(k < half, 1 - c, c)

    def body(c_ref, a_ref, b_ref, sum_ref, out_s, recv_s, send_sems, recv_sems):
        del c_ref
        k = pl.program_id(0)
        x, y, c = _place()

        def to_sibling(j):
            return pltpu.make_async_remote_copy(src_ref=out_s.at[j % 2], dst_ref=recv_s.at[j],
                                                send_sem=send_sems.at[j], recv_sem=recv_sems.at[j],
                                                device_id=(x, y, 1 - c), device_id_type=MESH)

        @pl.when((k >= 2) & (k < half + 2))
        def _():
            to_sibling(k - 2).wait_send()

        @pl.when(k < half)
        def _():
            out_s[k % 2] = _dot(a_ref[...], b_ref[...]).astype(BF16)
            to_sibling(k).start()

        @pl.when(k >= half)
        def _():
            to_sibling(k - half).wait_recv()
            sum_ref[...] = (_dot(a_ref[...], b_ref[...]) + recv_s[k - half].astype(F32)).astype(BF16)

    if by_rows:
        in_specs = [pl.BlockSpec((m, s), lambda k, c_ref: (owner(k, c_ref[0]), 0)),
                    pl.BlockSpec((s, cols), lambda k, c_ref: (0, 0))]
    else:
        in_specs = [pl.BlockSpec((m, s), lambda k, c_ref: (0, 0)),
                    pl.BlockSpec((s, cols), lambda k, c_ref: (0, owner(k, c_ref[0])))]
    return _call(body, name=name, grid=(N_SLOT,), in_specs=in_specs,
                 out_specs=[pl.BlockSpec((None, m, cols), lambda k, c_ref: (jnp.maximum(k - half, 0), 0, 0))],
                 out_shape=[SDS((half, m, cols), BF16)],
                 scratch_shapes=[pltpu.VMEM((2, m, cols), BF16), pltpu.VMEM((half, m, cols), BF16),
                                 pltpu.SemaphoreType.DMA((half,)), pltpu.SemaphoreType.DMA((half,))],
                 params=_cparams(("arbitrary",), 56), args=(a_t, b), comm=comm, prefetch=(core,))


def _row_tile(rows):
    for t in (512, 256, 128, 64, 32, 16, 8):
        if rows % t == 0:
            return t
    return rows


def _pair_sum(name, g8, recv4, core):
    _, rows, cols = recv4.shape
    tr = _row_tile(rows)
    g42 = g8.reshape(4, 2, rows, cols)

    def body(c_ref, g_ref, r_ref, o_ref):
        del c_ref
        o_ref[...] = (g_ref[...].astype(F32) + r_ref[...].astype(F32)).astype(o_ref.dtype)

    return pl.pallas_call(
        body, name=name,
        grid_spec=pltpu.PrefetchScalarGridSpec(
            num_scalar_prefetch=1, grid=(4, rows // tr),
            in_specs=[pl.BlockSpec((None, None, tr, cols), lambda j, i, c_ref: (j, c_ref[0], i, 0)),
                      pl.BlockSpec((None, tr, cols), lambda j, i, c_ref: (j, i, 0))],
            out_specs=pl.BlockSpec((None, tr, cols), lambda j, i, c_ref: (j, i, 0))),
        out_shape=SDS(recv4.shape, g8.dtype),
        compiler_params=_cparams(("arbitrary", "arbitrary"), 32),
    )(core, g42, recv4)


def _add2(name, a, b):
    rows, cols = a.shape
    tr = _row_tile(rows)

    def body(a_ref, b_ref, o_ref):
        o_ref[...] = a_ref[...] + b_ref[...]

    blk = pl.BlockSpec((tr, cols), lambda i: (i, 0))
    return pl.pallas_call(body, name=name, grid=(rows // tr,), in_specs=[blk, blk], out_specs=blk,
                          out_shape=SDS(a.shape, a.dtype),
                          compiler_params=_cparams(("arbitrary",), 32))(a, b)


def _sum_terms(name, terms):
    k, rows, cols = terms.shape
    tr = _row_tile(rows)

    def body(r_ref, o_ref):
        acc = r_ref[0]
        for q in range(1, k):
            acc = acc + r_ref[q]
        o_ref[...] = acc

    return pl.pallas_call(body, name=name, grid=(rows // tr,),
                          in_specs=[pl.BlockSpec((k, tr, cols), lambda i: (0, i, 0))],
                          out_specs=pl.BlockSpec((tr, cols), lambda i: (i, 0)),
                          out_shape=SDS((rows, cols), terms.dtype),
                          compiler_params=_cparams(("arbitrary",), 32))(terms)


def _adam_update(g, w, m, v):
    c1 = 1.0 / (1.0 - ADAM_B1 ** ADAM_STEP)
    c2 = 1.0 / (1.0 - ADAM_B2 ** ADAM_STEP)
    mn = ADAM_B1 * m + (1.0 - ADAM_B1) * g
    vn = ADAM_B2 * v + (1.0 - ADAM_B2) * (g * g)
    delta = (-ADAM_LR) * ((mn * c1) / (jnp.sqrt(vn * c2) + ADAM_EPS) + ADAM_WD * w)
    return delta, mn, vn


def _adamw_many(name, gs, ws, ms, vs):
    n = len(gs)

    def body(*refs):
        for p in range(n):
            g, w, m, v = (refs[q * n + p][...] for q in range(4))
            d, mn, vn = _adam_update(g, w, m, v)
            refs[4 * n + p][...] = d
            refs[5 * n + p][...] = mn
            refs[6 * n + p][...] = vn

    full = [pl.BlockSpec(w.shape, lambda i: (0, 0)) for w in ws]
    shapes = [SDS(w.shape, F32) for w in ws]
    res = pl.pallas_call(body, name=name, grid=(1,), in_specs=full * 4, out_specs=full * 3, out_shape=shapes * 3,
                         compiler_params=_cparams(("arbitrary",), 32),
                         )(*[_small_in_hbm(a) for a in (*gs, *ws, *ms, *vs)])
    return [(res[p], res[n + p], res[2 * n + p]) for p in range(n)]


def _adamw(name, terms, w, m, v):
    k, rows, cols = terms.shape
    tr = rows // 4 if rows % 64 == 0 else rows

    def body(t_ref, w_ref, m_ref, v_ref, g_ref, d_ref, mo_ref, vo_ref):
        g = t_ref[0].astype(F32)
        for q in range(1, k):
            g = g + t_ref[q].astype(F32)
        g_ref[...] = g
        d_ref[...], mo_ref[...], vo_ref[...] = _adam_update(g, w_ref[...], m_ref[...], v_ref[...])

    blk = pl.BlockSpec((tr, cols), lambda i: (i, 0))
    return pl.pallas_call(body, name=name, grid=(rows // tr,),
                          in_specs=[pl.BlockSpec((k, tr, cols), lambda i: (0, i, 0)), blk, blk, blk],
                          out_specs=[blk] * 4, out_shape=[SDS((rows, cols), F32)] * 4,
                          compiler_params=_cparams(("arbitrary",), 40),
                          )(*[pltpu.with_memory_space_constraint(a, pltpu.HBM) for a in (terms, w, m, v)])


def kernel(x, norm_mix_g, w_in, conv_w, conv_b, w_rgate, b_rgate, w_igate, b_igate, lru_lambda, w_out_a, sgu_ln_g, sgu_ln_b, sgu_w_s, sgu_b_s, w_out_b, w_out, norm_mlp_g, w_up, w_down, norm_final_g, loss_target, m_norm_mix_g, m_w_in, m_conv_w, m_conv_b, m_w_rgate, m_b_rgate, m_w_igate, m_b_igate, m_lru_lambda, m_w_out_a, m_sgu_ln_g, m_sgu_ln_b, m_sgu_w_s, m_sgu_b_s, m_w_out_b, m_w_out, m_norm_mlp_g, m_w_up, m_w_down, m_norm_final_g, v_norm_mix_g, v_w_in, v_conv_w, v_conv_b, v_w_rgate, v_b_rgate, v_w_igate, v_b_igate, v_lru_lambda, v_w_out_a, v_sgu_ln_g, v_sgu_ln_b, v_sgu_w_s, v_sgu_b_s, v_w_out_b, v_w_out, v_norm_mlp_g, v_w_up, v_w_down, v_norm_final_g):
    cx, cy, cc = _place()
    me = 4 * cx + 2 * cy + cc
    core = jnp.reshape(cc, (1,)).astype(jnp.int32)
    xs = x[0]
    tgt = loss_target[0]

    gate_shard = jnp.stack([w_rgate[0], w_igate[0]]).astype(BF16).reshape(2 * HEADS * 32, HEAD_DIM)
    vec_shard = jnp.concatenate([conv_w[0], b_rgate[0], b_igate[0]], axis=1)
    vec_shard = jnp.pad(vec_shard, ((0, 4), (0, 256 - vec_shard.shape[1])))
    shards = [w_in[0].astype(BF16), w_out_a[0].astype(BF16), w_out_b[0].astype(BF16), w_out[0].astype(BF16),
              w_up[0].astype(BF16), w_down[0].astype(BF16), gate_shard, vec_shard]
    (z, n1_t, w_in_g), (gate_g, vec_g) = _in_proj(xs, norm_mix_g, shards[0], _slot_order(cx, cy, cc),
                                                comm=_gather_plan(shards[6:8]))
    gates = gate_g.reshape(N_SLOT, 2, HEADS, 32, HEAD_DIM).transpose(1, 2, 0, 3, 4).reshape(2, HEADS, HEAD_DIM, HEAD_DIM)
    w_r_f, w_i_f = gates[0], gates[1]
    conv_w_f = vec_g[:, 0:4, 0:128].transpose(1, 0, 2).reshape(CONV_K, D)
    b_r_f = vec_g[:, 0:4, 128:160].transpose(1, 0, 2).reshape(1, D)
    b_i_f = vec_g[:, 0:4, 160:192].transpose(1, 0, 2).reshape(1, D)
    b_s_t = jnp.transpose(sgu_b_s[0])

    (ya, hs, xc, r_gate, i_gate), (w_oa_g, w_ob_g, w_out_g, w_up_g) = _branch_a_fwd(
        z, conv_w_f, conv_b, w_r_f, b_r_f, w_i_f, b_i_f, lru_lambda, comm=_gather_plan(shards[1:5]))
    w_oa_f = w_oa_g.reshape(D, D)
    w_ob_f = w_ob_g.reshape(D, D)
    w_out_f = w_out_g.reshape(D, D)
    (yb, pa, pb, merged, h1), (w_down_g,) = _branch_b_merge_out(
        ya, z, xs, sgu_ln_g, sgu_ln_b, sgu_w_s[0], b_s_t, w_oa_f, w_ob_f, w_out_f,
        comm=_gather_plan(shards[5:6], middle_at=4))
    w_down_f = w_down_g.reshape(N_SLOT * FF_COLS, D)
    r_act, act_t, n2_t, dh2, dh2b, loss_acc, d_gfin = _mlp_fwd(h1, norm_mlp_g, w_up_g, w_down_f,
                                                               norm_final_g.reshape(1, D), tgt)

    def pair(names, grads, recv):
        return [_pair_sum("pair_sum_" + nm, g, r, core) for nm, g, r in zip(names, grads, recv)]

    (p_down,), _ = _wgrad_paired("wgrad_down", act_t, dh2b, core, by_rows=True)
    (df, dh1, d_gmlp), (got_down,) = _mlp_bwd(dh2, dh2b, r_act, w_down_f, w_up_g, h1, norm_mlp_g,
                                              comm=_chips_plan([p_down]))
    (p_up,), _ = _wgrad_paired("wgrad_up", n2_t, df, core)
    g_out = _wgrad("wgrad_out", merged, dh1).reshape(N_SLOT, D // N_SLOT, D)
    (dz, dpa, dpb, dya, dyb), (r_out,) = _merge_bwd(
        dh1, z, pa, pb, w_out_f, w_oa_f, w_ob_f, comm=_sibling_plan([g_out]))
    (p_out,) = pair(["out"], [g_out], [r_out])
    g_oa = _wgrad("wgrad_out_a", ya, dpa).reshape(N_SLOT, D // N_SLOT, D)
    g_ob = _wgrad("wgrad_out_b", yb, dpb).reshape(N_SLOT, D // N_SLOT, D)
    (dz, d_ws, d_bs, d_ln), (got_out, r_oa, r_ob) = _branch_b_bwd(
        dz, dyb, z, sgu_ln_g, sgu_ln_b, sgu_w_s[0], b_s_t,
        comm=_join(_chips_plan([p_out]), _sibling_plan([g_oa, g_ob])))
    p_oa, p_ob = pair(["out_a", "out_b"], [g_oa, g_ob], [r_oa, r_ob])
    (dz, d_vec, d_wr, d_wi), (got_up, got_oa, got_ob) = _branch_a_bwd(
        dz, dya, z, hs, xc, r_gate, i_gate, conv_w_f, w_r_f, w_i_f, lru_lambda,
        comm=_chips_plan([p_up, p_oa, p_ob]))
    g_gate = jnp.stack([d_wr, d_wi]).reshape(2, HEADS, N_SLOT, 32, HEAD_DIM).transpose(2, 0, 1, 3, 4)
    g_gate = g_gate.reshape(N_SLOT, 2 * HEADS * 32, HEAD_DIM).astype(BF16)

    d_bs_row = jnp.pad(d_bs[:, :, 0].reshape(1, GROUPS * CHUNK), ((0, 0), (0, D - GROUPS * CHUNK)))
    vecs = jnp.concatenate([d_vec, jnp.concatenate([d_ln[0:2], d_gmlp, d_gfin, d_bs_row, jnp.zeros((3, D), F32)])])
    d_ws2 = d_ws.reshape(GROUPS * CHUNK, CHUNK)
    (p_in,), (r_gate, r_vecs, r_ws) = _wgrad_paired("wgrad_in", n1_t, dz, core,
                                                    comm=_sibling_plan([g_gate], [vecs, d_ws2]))
    (p_gate,) = pair(["gate"], [g_gate], [r_gate])
    vecs_chip = _add2("pair_sum_vecs", vecs, r_vecs)
    ws_chip = _add2("pair_sum_ws", d_ws2, r_ws)
    (dx, d_gmix), (got_in, got_gate, got_vecs, got_ws) = _in_bwd(
        dz, w_in_g, xs, dh1, norm_mix_g, comm=_chips_plan([p_in, p_gate], [vecs_chip, ws_chip]))
    vecs_sum = _sum_terms("sum_vecs", got_vecs)
    last = jnp.concatenate([d_gmix, jnp.pad(loss_acc[0:1], ((0, 0), (0, D - 128))), jnp.zeros((6, D), F32)])
    (last_all,) = _run_plan("exchange_last", _exchange_plan(last))
    last_sum = _sum_terms("sum_last", last_all)
    loss = last_sum[1, 0]
    got = [got_in, got_oa, got_ob, got_out, got_up, got_down, got_gate]

    def step(nm, terms, w, m, v, rows, cols):
        g, d, mn, vn = _adamw("adamw_" + nm, terms.reshape(4, rows, cols), w.reshape(rows, cols),
                              m.reshape(rows, cols), v.reshape(rows, cols))
        return [a.reshape(w.shape) for a in (g, d, mn, vn)]

    o_in = step("in", got[0], w_in, m_w_in, v_w_in, D, W_IN_COLS)
    o_oa = step("out_a", got[1], w_out_a, m_w_out_a, v_w_out_a, D // N_SLOT, D)
    o_ob = step("out_b", got[2], w_out_b, m_w_out_b, v_w_out_b, D // N_SLOT, D)
    o_out = step("out", got[3], w_out, m_w_out, v_w_out, D // N_SLOT, D)
    o_up = step("up", got[4], w_up, m_w_up, v_w_up, D, FF_COLS)
    o_down = step("down", got[5], w_down, m_w_down, v_w_down, FF_COLS, D)
    gate_w = jnp.stack([w_rgate[0], w_igate[0]]).reshape(2 * HEADS * 32, HEAD_DIM)
    gate_m = jnp.stack([m_w_rgate[0], m_w_igate[0]]).reshape(2 * HEADS * 32, HEAD_DIM)
    gate_v = jnp.stack([v_w_rgate[0], v_w_igate[0]]).reshape(2 * HEADS * 32, HEAD_DIM)
    o_gate = _adamw("adamw_gate", got[6], gate_w, gate_m, gate_v)
    o_gate = [a.reshape(2, 1, HEADS, 32, HEAD_DIM) for a in o_gate]
    o_wr = [a[0] for a in o_gate]
    o_wi = [a[1] for a in o_gate]

    def own(full, width):
        return lax.dynamic_slice_in_dim(full, me * width, width, axis=1)

    small_g = {
        "norm_mix_g": last_sum[0:1], "conv_w": own(vecs_sum[0:4], 128), "conv_b": vecs_sum[4:5],
        "b_rgate": own(vecs_sum[5:6].reshape(HEADS, HEAD_DIM), 32),
        "b_igate": own(vecs_sum[6:7].reshape(HEADS, HEAD_DIM), 32),
        "lru_lambda": vecs_sum[7:8], "sgu_ln_g": vecs_sum[8:9], "sgu_ln_b": vecs_sum[9:10],
        "norm_mlp_g": vecs_sum[10:11], "norm_final_g": vecs_sum[11:12],
        "sgu_b_s": vecs_sum[12, 0:GROUPS * CHUNK].reshape(GROUPS, CHUNK),
    }
    small_w = {"norm_mix_g": (norm_mix_g, m_norm_mix_g, v_norm_mix_g), "conv_w": (conv_w, m_conv_w, v_conv_w),
               "conv_b": (conv_b, m_conv_b, v_conv_b), "b_rgate": (b_rgate, m_b_rgate, v_b_rgate),
               "b_igate": (b_igate, m_b_igate, v_b_igate), "lru_lambda": (lru_lambda, m_lru_lambda, v_lru_lambda),
               "sgu_ln_g": (sgu_ln_g, m_sgu_ln_g, v_sgu_ln_g), "sgu_ln_b": (sgu_ln_b, m_sgu_ln_b, v_sgu_ln_b),
               "norm_mlp_g": (norm_mlp_g, m_norm_mlp_g, v_norm_mlp_g),
               "norm_final_g": (norm_final_g, m_norm_final_g, v_norm_final_g),
               "sgu_b_s": (sgu_b_s, m_sgu_b_s, v_sgu_b_s), "sgu_w_s": (sgu_w_s, m_sgu_w_s, v_sgu_w_s)}
    order = list(small_g)
    as2d = lambda k, a: a.reshape(small_g[k].shape)
    upd = _adamw_many("adamw_small", [small_g[k] for k in order], *[[as2d(k, small_w[k][q]) for k in order]
                                                                     for q in range(3)])
    o_small = {k: [a.reshape(small_w[k][0].shape) for a in (small_g[k],) + u] for k, u in zip(order, upd)}
    ws3 = [a[0].reshape(GROUPS * CHUNK, CHUNK) for a in small_w.pop("sgu_w_s")]
    o_small["sgu_w_s"] = [a.reshape(sgu_w_s.shape) for a in _adamw("adamw_ws", got_ws, *ws3)]

    per_weight = {"norm_mix_g": o_small["norm_mix_g"], "w_in": o_in, "conv_w": o_small["conv_w"],
                  "conv_b": o_small["conv_b"], "w_rgate": o_wr, "b_rgate": o_small["b_rgate"], "w_igate": o_wi,
                  "b_igate": o_small["b_igate"], "lru_lambda": o_small["lru_lambda"], "w_out_a": o_oa,
                  "sgu_ln_g": o_small["sgu_ln_g"], "sgu_ln_b": o_small["sgu_ln_b"], "sgu_w_s": o_small["sgu_w_s"],
                  "sgu_b_s": o_small["sgu_b_s"], "w_out_b": o_ob, "w_out": o_out, "norm_mlp_g": o_small["norm_mlp_g"],
                  "w_up": o_up, "w_down": o_down, "norm_final_g": o_small["norm_final_g"]}
    names_w = list(per_weight)
    return (loss, dx[None], *[per_weight[k][0] for k in names_w], *[per_weight[k][1] for k in names_w],
            *[per_weight[k][2] for k in names_w], *[per_weight[k][3] for k in names_w])
```

```python
import jax
import jax.numpy as jnp
from jax import lax
from jax.experimental import pallas as pl
from jax.experimental.pallas import tpu as pltpu

F32 = jnp.float32
BF16 = jnp.bfloat16
SDS = jax.ShapeDtypeStruct
MESH = pl.DeviceIdType.MESH
ANY = pl.BlockSpec(memory_space=pl.ANY)

D = 1024
N_SLOT = 8
W_IN_COLS = 768
FF_COLS = 512
HEADS, HEAD_DIM = 4, 256
GROUPS, GROUP_DIM = 4, 256
CHUNK = 128
CONV_K = 4
NORM_EPS = 1e-6
LN_EPS = 1e-5
LRU_C = 8.0
ADAM_LR, ADAM_B1, ADAM_B2, ADAM_EPS, ADAM_WD, ADAM_STEP = 0.001, 0.9, 0.999, 1e-08, 0.01, 10

TM_ROWS = 1024
TM_MERGE = 512
T_BRANCH_A = 512
T_BRANCH_B = 256
MiB = 1024 * 1024
SMALL_OPERAND = 16 * 1024

_GELU_C = 0.7978845608028654
_GELU_A = 0.044715


def _small_in_hbm(a):
    return pltpu.with_memory_space_constraint(a, pltpu.HBM) if a.size <= SMALL_OPERAND else a


def _cparams(sem, vmem_mib):
    return pltpu.CompilerParams(dimension_semantics=sem, vmem_limit_bytes=vmem_mib * MiB)


def _gelu(x):
    t = jnp.tanh(_GELU_C * (x + _GELU_A * x * x * x))
    return 0.5 * x * (1.0 + t)


def _gelu_and_grad(x):
    x2 = x * x
    t = jnp.tanh(_GELU_C * x * (1.0 + _GELU_A * x2))
    g = 0.5 * x * (1.0 + t)
    dg = 0.5 * (1.0 + t) + 0.5 * x * (1.0 - t * t) * _GELU_C * (1.0 + 3.0 * _GELU_A * x2)
    return g, dg


def _softplus(x):
    return jnp.maximum(x, 0.0) + jnp.log1p(jnp.exp(-jnp.abs(x)))


def _dot(a, b):
    return jnp.dot(a, b, preferred_element_type=F32)


def _dot_nt(a, b):
    return lax.dot_general(a, b, (((1,), (1,)), ((), ())), preferred_element_type=F32)


def _dot_tn(a, b):
    return lax.dot_general(a, b, (((0,), (0,)), ((), ())), preferred_element_type=F32)


def _rows_shifted(prev8, cur, k):
    ext = jnp.concatenate([prev8, cur], axis=0)
    return pltpu.roll(ext, k, 0)[8:]


def _rows_advanced(cur, next8, k):
    t = cur.shape[0]
    ext = jnp.concatenate([cur, next8], axis=0)
    return pltpu.roll(ext, t + 8 - k, 0)[:t]


def _first_second(x, y, c):
    ny, nx, far = _other_chips(x, y)
    pick = lambda a, b: a * (1 - c) + b * c
    first = tuple(pick(a, b) for a, b in zip(ny, nx))
    second = tuple(pick(b, a) for a, b in zip(ny, nx))
    return first, second, far


def _slot_order(x, y, c):
    chip = 2 * x + y
    first, second, far = _first_second(x, y, c)
    order = [2 * chip + c, 2 * chip + 1 - c, 2 * first[2] + c, 2 * second[2] + 1 - c, 2 * second[2] + c,
             2 * first[2] + 1 - c, 2 * far[2] + c, 2 * far[2] + 1 - c]
    return jnp.stack(order).astype(jnp.int32)


def _in_proj(x, g_mix, w_in_own, order, comm=None):
    s = x.shape[0]
    tm = min(TM_ROWS, s)
    ni = s // tm

    def body(order_ref, x_ref, g_ref, own_ref, z_ref, nt_ref, wg_ref, n_s, w_s, send_sems, recv_sems, local_sems):
        j, i = pl.program_id(0), pl.program_id(1)
        px, py, c = _place()
        chip = 2 * px + py
        me = 2 * chip + c
        sib = (px, py, 1 - c)
        chips = _other_chips(px, py)

        def rc(k, src, blk, to):
            return pltpu.make_async_remote_copy(src_ref=src, dst_ref=w_s.at[blk], send_sem=send_sems.at[k],
                                                recv_sem=recv_sems.at[k], device_id=to, device_id_type=MESH)

        del chips
        first, second, far = _first_second(px, py, c)
        blocks = [2 * first[2] + c, 2 * second[2] + c, 2 * far[2] + c]
        own_in = pltpu.make_async_copy(own_ref, w_s.at[me], local_sems.at[0])
        to_first = rc(1, own_ref, me, (first[0], first[1], c))
        to_second = rc(2, own_ref, me, (second[0], second[1], c))
        relay = rc(3, w_s.at[blocks[0]], blocks[0], (second[0], second[1], c))
        sends = [rc(0, own_ref, me, sib), to_first, to_second, relay]
        passed = [rc(4 + q, w_s.at[blk], blk, sib) for q, blk in enumerate(blocks)]
        keep = pltpu.make_async_copy(w_s, wg_ref, local_sems.at[1])

        @pl.when((i == 0) & (j == 0))
        def _():
            own_in.start()
            sends[0].start()
            to_first.start()
            own_in.wait()

        @pl.when((i == 0) & (j == 1))
        def _():
            rc(0, own_ref, 2 * chip + 1 - c, sib).wait_recv()

        for q, blk in enumerate(blocks):
            @pl.when((i == 0) & (j == 2 + 2 * q))
            def _():
                rc(1 + q, own_ref, blk, sib).wait_recv()
                passed[q].start()
                if q == 0:
                    to_second.start()
                    relay.start()

            @pl.when((i == 0) & (j == 3 + 2 * q))
            def _():
                rc(4 + q, own_ref, order_ref[j], sib).wait_recv()

        rows = pl.ds(pl.multiple_of(i * tm, tm), tm)

        @pl.when(j == 0)
        def _():
            xv = x_ref[...]
            rstd = lax.rsqrt(jnp.mean(xv * xv, axis=-1, keepdims=True) + NORM_EPS)
            nb = (xv * rstd * g_ref[...]).astype(BF16)
            n_s[rows, :] = nb
            nt_ref[...] = nb.T

        z_ref[...] = _dot(n_s[rows, :], w_s[order_ref[j]]).astype(BF16)

        @pl.when((i == 0) & (j == N_SLOT - 1))
        def _():
            keep.start()

        @pl.when((i == ni - 1) & (j == N_SLOT - 1))
        def _():
            for cp in sends + passed:
                cp.wait_send()
            keep.wait()

    first_pass = lambda j, i, o: (jnp.where(j == 0, i, ni - 1), 0)
    (z, n1, w_in_g), extra = _call(
        body, name="in_proj", grid=(N_SLOT, ni), prefetch=(order,),
        in_specs=[pl.BlockSpec((tm, D), first_pass),
                  pl.BlockSpec((1, D), lambda j, i, o: (0, 0)), ANY],
        out_specs=[pl.BlockSpec((tm, W_IN_COLS), lambda j, i, o: (i, o[j])),
                   pl.BlockSpec((D, tm), lambda j, i, o: (0, jnp.where(j == 0, i, ni - 1))), ANY],
        out_shape=[SDS((s, N_SLOT * W_IN_COLS), BF16), SDS((D, s), BF16), SDS((N_SLOT, D, W_IN_COLS), BF16)],
        scratch_shapes=[pltpu.VMEM((s, D), BF16), pltpu.VMEM((N_SLOT, D, W_IN_COLS), BF16),
                        pltpu.SemaphoreType.DMA((7,)), pltpu.SemaphoreType.DMA((7,)), pltpu.SemaphoreType.DMA((2,))],
        params=_cparams(("arbitrary", "arbitrary"), 56), args=(x, g_mix, w_in_own), comm=comm)
    return (z, n1, w_in_g), extra


def _decay(r, sp_lam):
    log_a = (-LRU_C) * r * sp_lam
    a = jnp.exp(log_a)
    return a, jnp.sqrt(-jnp.tanh(log_a) * (a * a + 1.0))


def _lru_gates(xc, xcb, wr_ref, br, wi_ref, bi, sp_lam, a_s, b_s, r_ref, i_ref):
    for h in range(HEADS):
        sl = slice(h * HEAD_DIM, (h + 1) * HEAD_DIM)
        r = jax.nn.sigmoid(_dot(xcb[:, sl], wr_ref[h]) + br[:, sl])
        ig = jax.nn.sigmoid(_dot(xcb[:, sl], wi_ref[h]) + bi[:, sl])
        a, mult = _decay(r, sp_lam[:, sl])
        a_s[:, sl] = a
        b_s[:, sl] = xc[:, sl] * ig * mult
        r_ref[:, sl] = r.astype(BF16)
        i_ref[:, sl] = ig.astype(BF16)


def _conv_fwd(xa, prev8, cw, cb):
    xc = cb + cw[0:1, :] * xa
    for k in range(1, CONV_K):
        xc = xc + cw[k:k + 1, :] * _rows_shifted(prev8, xa, k)
    return xc


def _branch_a_fwd(z, conv_w, conv_b, w_r, b_r, w_i, b_i, lam, comm=None):
    s = z.shape[0]
    ta = min(T_BRANCH_A, s)
    per16 = ta // 16

    def body(xa_ref, xp_ref, ga_ref, cw_ref, cb_ref, wr_ref, br_ref, wi_ref, bi_ref, lam_ref,
             ya_ref, hs_ref, xc_ref, r_ref, i_ref, a_s, b_s, h_s, carry_s):
        i = pl.program_id(0)

        @pl.when(i == 0)
        def _():
            carry_s[...] = jnp.zeros_like(carry_s)

        xa = xa_ref[...].astype(F32)
        prev8 = jnp.where(i > 0, xp_ref[...].astype(F32)[8:16], 0.0)
        xc = _conv_fwd(xa, prev8, cw_ref[...], cb_ref[...])
        xcb = xc.astype(BF16)
        xc_ref[...] = xcb
        sp_lam = _softplus(-lam_ref[...])
        _lru_gates(xc, xcb, wr_ref, br_ref[...], wi_ref, bi_ref[...], sp_lam, a_s, b_s, r_ref, i_ref)

        row = lax.broadcasted_iota(jnp.int32, (8, D), 0)

        def group(g, carry):
            off = pl.multiple_of(g * 8, 8)
            a8 = a_s[pl.ds(off, 8), :]
            b8 = b_s[pl.ds(off, 8), :]
            for d in (1, 2, 4):
                a_sh = jnp.where(row >= d, pltpu.roll(a8, d, 0), 1.0)
                b_sh = jnp.where(row >= d, pltpu.roll(b8, d, 0), 0.0)
                b8 = a8 * b_sh + b8
                a8 = a8 * a_sh
            h8 = b8 + a8 * carry
            h_s[pl.ds(off, 8), :] = h8
            return jnp.broadcast_to(h8[7:8, :], (8, D))

        carry_s[...] = lax.fori_loop(0, ta // 8, group, carry_s[...])
        hs = h_s[...]
        hs_ref[...] = hs.astype(BF16)
        ya_ref[...] = (hs * _gelu(ga_ref[...].astype(F32))).astype(BF16)

    vec = pl.BlockSpec((1, D), lambda i: (0, 0))
    gate = pl.BlockSpec((HEADS, HEAD_DIM, HEAD_DIM), lambda i: (0, 0, 0))
    return _call(
        body, name="branch_a_fwd", grid=(s // ta,),
        in_specs=[pl.BlockSpec((ta, D), lambda i: (i, 0)),
                  pl.BlockSpec((16, D), lambda i: (jnp.maximum(i * per16 - 1, 0), 0)),
                  pl.BlockSpec((ta, D), lambda i: (i, 1)),
                  pl.BlockSpec((CONV_K, D), lambda i: (0, 0)), vec, gate, vec, gate, vec, vec],
        out_specs=[pl.BlockSpec((ta, D), lambda i: (i, 0))] * 5,
        out_shape=[SDS((s, D), BF16)] * 5,
        scratch_shapes=[pltpu.VMEM((ta, D), F32), pltpu.VMEM((ta, D), F32), pltpu.VMEM((ta, D), F32),
                        pltpu.VMEM((8, D), F32)],
        params=_cparams(("arbitrary",), 40), args=(z, z, z, conv_w, conv_b, w_r, b_r, w_i, b_i, lam), comm=comm)


def _sgu_common(ub, vb, lg, lb, with_grad):
    if with_grad:
        u, du = _gelu_and_grad(ub)
        v, dv = _gelu_and_grad(vb)
    else:
        u, v, du, dv = _gelu(ub), _gelu(vb), None, None
    mu = jnp.mean(v, axis=-1, keepdims=True)
    vc = v - mu
    rstd = lax.rsqrt(jnp.mean(vc * vc, axis=-1, keepdims=True) + LN_EPS)
    vhat = vc * rstd
    vln = vhat * lg + lb
    return u, du, dv, rstd, vhat, vln


def _masked_ws(ws_ref):
    t = lax.broadcasted_iota(jnp.int32, (CHUNK, CHUNK), 0)
    c = lax.broadcasted_iota(jnp.int32, (CHUNK, CHUNK), 1)
    keep = c <= t
    return [jnp.where(keep, ws_ref[g], 0.0).astype(BF16) for g in range(GROUPS)]


def _branch_b_merge_out(ya, z, x, ln_g, ln_b, w_s, b_s_t, w_oa, w_ob, w_out, comm=None):
    s = x.shape[0]
    tm = min(TM_MERGE, s)
    assert tm % CHUNK == 0

    def body(ub_ref, vb_ref, lg_ref, lb_ref, ws_ref, bs_ref, ya_ref, ma_ref, mb_ref, x_ref, woa_ref, wob_ref, wo_ref,
             yb_ref, pa_ref, pb_ref, mg_ref, h1_ref):
        u, _, _, _, _, vln = _sgu_common(ub_ref[...].astype(F32), vb_ref[...].astype(F32),
                                         lg_ref[...], lb_ref[...], False)
        vlnb = vln.astype(BF16)
        wm = _masked_ws(ws_ref)
        bs = bs_ref[...]
        for c in range(tm // CHUNK):
            rs = slice(c * CHUNK, (c + 1) * CHUNK)
            for g in range(GROUPS):
                cs = slice(g * GROUP_DIM, (g + 1) * GROUP_DIM)
                sp = _dot(wm[g], vlnb[rs, cs]) + bs[:, g:g + 1]
                yb_ref[rs, cs] = (u[rs, cs] * sp).astype(BF16)

        pa = _dot(ya_ref[...], woa_ref[...])
        pb = _dot(yb_ref[...], wob_ref[...])
        merged = (jax.nn.sigmoid(ma_ref[...].astype(F32)) * pa
                  + jax.nn.sigmoid(mb_ref[...].astype(F32)) * pb).astype(BF16)
        pa_ref[...] = pa.astype(BF16)
        pb_ref[...] = pb.astype(BF16)
        mg_ref[...] = merged
        h1_ref[...] = x_ref[...] + _dot(merged, wo_ref[...])

    row = pl.BlockSpec((tm, D), lambda i: (i, 0))
    col = lambda q: pl.BlockSpec((tm, D), lambda i: (i, q))
    vec = pl.BlockSpec((1, D), lambda i: (0, 0))
    wsp = pl.BlockSpec((D, D), lambda i: (0, 0))
    return _call(
        body, name="branch_b_merge_out", grid=(s // tm,),
        in_specs=[col(2), col(3), vec, vec, pl.BlockSpec((GROUPS, CHUNK, CHUNK), lambda i: (0, 0, 0)),
                  pl.BlockSpec((CHUNK, GROUPS), lambda i: (0, 0)), row, col(4), col(5), row, wsp, wsp, wsp],
        out_specs=[row, row, row, row, row],
        out_shape=[SDS((s, D), BF16)] * 4 + [SDS((s, D), F32)], scratch_shapes=[],
        params=_cparams(("arbitrary",), 56),
        args=(z, z, ln_g, ln_b, w_s, b_s_t, ya, z, z, x, w_oa, w_ob, w_out), comm=comm)


def _mlp_fwd(h1, g_mlp, w_up_g, w_down, g_fin, tgt):
    s = h1.shape[0]
    tm = min(TM_ROWS, s)
    nj = N_SLOT

    def body(h1_ref, gm_ref, wu_ref, wd_ref, gf_ref, t_ref, r_ref, at_ref, n2t_ref, dh2_ref, dh2b_ref, loss_ref,
             dgf_ref, n2_s, acc_s):
        i, j = pl.program_id(0), pl.program_id(1)

        @pl.when(j == 0)
        def _():
            hv = h1_ref[...]
            rstd = lax.rsqrt(jnp.mean(hv * hv, axis=-1, keepdims=True) + NORM_EPS)
            nb = (hv * rstd * gm_ref[...]).astype(BF16)
            n2_s[...] = nb
            n2t_ref[...] = nb.T
            acc_s[...] = jnp.zeros_like(acc_s)

        @pl.when((i == 0) & (j == 0))
        def _():
            loss_ref[...] = jnp.zeros_like(loss_ref)
            dgf_ref[...] = jnp.zeros_like(dgf_ref)

        r = jnp.maximum(_dot(n2_s[...], wu_ref[...]), 0.0)
        r_ref[...] = r.astype(BF16)
        act = (r * r).astype(BF16)
        at_ref[...] = act.T
        acc_s[...] += _dot(act, wd_ref[...])

        @pl.when(j == nj - 1)
        def _():
            h2 = h1_ref[...] + acc_s[...]
            rstd = lax.rsqrt(jnp.mean(h2 * h2, axis=-1, keepdims=True) + NORM_EPS)
            hh = h2 * rstd
            gf = gf_ref[...]
            e = hh * gf - t_ref[...]
            loss_ref[...] += jnp.sum(e * e) * (0.5 / D)
            dy = e * (1.0 / D)
            dgf_ref[...] += jnp.sum(dy * hh, axis=0, keepdims=True)
            dhh = dy * gf
            dh2 = rstd * (dhh - hh * jnp.mean(dhh * hh, axis=-1, keepdims=True))
            dh2_ref[...] = dh2
            dh2b_ref[...] = dh2.astype(BF16)

    row = pl.BlockSpec((tm, D), lambda i, j: (i, 0))
    vec = pl.BlockSpec((1, D), lambda i, j: (0, 0))
    return pl.pallas_call(
        body, name="mlp_fwd", grid=(s // tm, nj),
        in_specs=[row, vec, pl.BlockSpec((None, D, FF_COLS), lambda i, j: (j, 0, 0)),
                  pl.BlockSpec((FF_COLS, D), lambda i, j: (j, 0)), vec, row],
        out_specs=[pl.BlockSpec((tm, FF_COLS), lambda i, j: (i, j)), pl.BlockSpec((FF_COLS, tm), lambda i, j: (j, i)),
                   pl.BlockSpec((D, tm), lambda i, j: (0, i)), row, row, pl.BlockSpec((8, 128), lambda i, j: (0, 0)),
                   vec],
        out_shape=[SDS((s, nj * FF_COLS), BF16), SDS((nj * FF_COLS, s), BF16), SDS((D, s), BF16), SDS((s, D), F32),
                   SDS((s, D), BF16), SDS((8, 128), F32), SDS((1, D), F32)],
        scratch_shapes=[pltpu.VMEM((tm, D), BF16), pltpu.VMEM((tm, D), F32)],
        compiler_params=_cparams(("arbitrary", "arbitrary"), 56),
    )(h1, _small_in_hbm(g_mlp), w_up_g, w_down, _small_in_hbm(g_fin), tgt)


def _mlp_bwd(dh2, dh2b, r, w_down, w_up_g, h1, g_mlp, comm=None):
    s = h1.shape[0]
    tm = min(TM_ROWS, s)
    nj = N_SLOT

    def body(dh2_ref, dh2b_ref, r_ref, wd_ref, wu_ref, h1_ref, gm_ref, df_ref, dh1_ref, dgm_ref, acc_s):
        i, j = pl.program_id(0), pl.program_id(1)

        @pl.when(j == 0)
        def _():
            acc_s[...] = jnp.zeros_like(acc_s)

        @pl.when((i == 0) & (j == 0))
        def _():
            dgm_ref[...] = jnp.zeros_like(dgm_ref)

        d_act = _dot_nt(dh2b_ref[...], wd_ref[...])
        df = (d_act * (2.0 * r_ref[...].astype(F32))).astype(BF16)
        df_ref[...] = df
        acc_s[...] += _dot_nt(df, wu_ref[...])

        @pl.when(j == nj - 1)
        def _():
            hv = h1_ref[...]
            rstd = lax.rsqrt(jnp.mean(hv * hv, axis=-1, keepdims=True) + NORM_EPS)
            hh = hv * rstd
            dn2 = acc_s[...]
            dgm_ref[...] += jnp.sum(dn2 * hh, axis=0, keepdims=True)
            dhat = dn2 * gm_ref[...]
            dh1_ref[...] = dh2_ref[...] + rstd * (dhat - hh * jnp.mean(dhat * hh, axis=-1, keepdims=True))

    row = pl.BlockSpec((tm, D), lambda i, j: (i, 0))
    vec = pl.BlockSpec((1, D), lambda i, j: (0, 0))
    ffb = pl.BlockSpec((tm, FF_COLS), lambda i, j: (i, j))
    return _call(
        body, name="mlp_bwd", grid=(s // tm, nj),
        in_specs=[row, row, ffb, pl.BlockSpec((FF_COLS, D), lambda i, j: (j, 0)),
                  pl.BlockSpec((None, D, FF_COLS), lambda i, j: (j, 0, 0)), row, vec],
        out_specs=[ffb, row, vec],
        out_shape=[SDS((s, nj * FF_COLS), BF16), SDS((s, D), F32), SDS((1, D), F32)],
        scratch_shapes=[pltpu.VMEM((tm, D), F32)],
        params=_cparams(("arbitrary", "arbitrary"), 56), args=(dh2, dh2b, r, w_down, w_up_g, h1, g_mlp), comm=comm)


def _merge_bwd(dh1, z, pa, pb, w_out, w_oa, w_ob, comm=None):
    s = dh1.shape[0]
    tm = min(TM_MERGE, s)

    def body(dh1_ref, ma_ref, mb_ref, pa_ref, pb_ref, wo_ref, woa_ref, wob_ref,
             dz_ref, dpa_ref, dpb_ref, dya_ref, dyb_ref):
        dm = _dot_nt(dh1_ref[...].astype(BF16), wo_ref[...])
        sa = jax.nn.sigmoid(ma_ref[...].astype(F32))
        sb = jax.nn.sigmoid(mb_ref[...].astype(F32))
        dpa = (dm * sa).astype(BF16)
        dpb = (dm * sb).astype(BF16)
        dz_ref[:, 0:D] = (dm * pa_ref[...].astype(F32) * sa * (1.0 - sa)).astype(BF16)
        dz_ref[:, D:2 * D] = (dm * pb_ref[...].astype(F32) * sb * (1.0 - sb)).astype(BF16)
        dpa_ref[...] = dpa
        dpb_ref[...] = dpb
        dya_ref[...] = _dot_nt(dpa, woa_ref[...]).astype(BF16)
        dyb_ref[...] = _dot_nt(dpb, wob_ref[...]).astype(BF16)

    row = pl.BlockSpec((tm, D), lambda i: (i, 0))
    wsp = pl.BlockSpec((D, D), lambda i: (0, 0))
    return _call(
        body, name="merge_bwd", grid=(s // tm,),
        in_specs=[row, pl.BlockSpec((tm, D), lambda i: (i, 4)), pl.BlockSpec((tm, D), lambda i: (i, 5)),
                  row, row, wsp, wsp, wsp],
        out_specs=[pl.BlockSpec((tm, 2 * D), lambda i: (i, 2)), row, row, row, row],
        out_shape=[SDS((s, 6 * D), BF16)] + [SDS((s, D), BF16)] * 4, scratch_shapes=[],
        params=_cparams(("arbitrary",), 48), args=(dh1, z, z, pa, pb, w_out, w_oa, w_ob), comm=comm)


def _branch_b_bwd(dz, dyb, z, ln_g, ln_b, w_s, b_s_t, comm=None):
    s = z.shape[0]
    tb = min(T_BRANCH_B, s)

    def body(dz_in, dyb_ref, ub_ref, vb_ref, lg_ref, lb_ref, ws_ref, bs_ref,
             dz_ref, dws_ref, dbs_ref, dln_ref, du_s, dvln_s):
        del dz_in

        @pl.when(pl.program_id(0) == 0)
        def _():
            dws_ref[...] = jnp.zeros_like(dws_ref)
            dbs_ref[...] = jnp.zeros_like(dbs_ref)
            dln_ref[...] = jnp.zeros_like(dln_ref)

        lg = lg_ref[...]
        u, du, dv, rstd, vhat, vln = _sgu_common(ub_ref[...].astype(F32), vb_ref[...].astype(F32),
                                                 lg, lb_ref[...], True)
        vlnb = vln.astype(BF16)
        dyb_v = dyb_ref[...].astype(F32)
        wm = _masked_ws(ws_ref)
        keep = (lax.broadcasted_iota(jnp.int32, (CHUNK, CHUNK), 1)
                <= lax.broadcasted_iota(jnp.int32, (CHUNK, CHUNK), 0))
        bs = bs_ref[...]
        for c in range(tb // CHUNK):
            rs = slice(c * CHUNK, (c + 1) * CHUNK)
            for g in range(GROUPS):
                cs = slice(g * GROUP_DIM, (g + 1) * GROUP_DIM)
                v_blk = vlnb[rs, cs]
                sp = _dot(wm[g], v_blk) + bs[:, g:g + 1]
                d_sp = dyb_v[rs, cs] * u[rs, cs]
                d_spb = d_sp.astype(BF16)
                du_s[rs, cs] = dyb_v[rs, cs] * sp
                dvln_s[rs, cs] = _dot_tn(wm[g], d_spb)
                dws_ref[g] += jnp.where(keep, _dot_nt(d_spb, v_blk), 0.0)
                dbs_ref[g] += jnp.broadcast_to(jnp.sum(d_sp, axis=-1, keepdims=True), (CHUNK, CHUNK))
        dvln = dvln_s[...]
        dln_ref[0:1, :] += jnp.sum(dvln * vhat, axis=0, keepdims=True)
        dln_ref[1:2, :] += jnp.sum(dvln, axis=0, keepdims=True)
        dvh = dvln * lg
        d_v = rstd * (dvh - jnp.mean(dvh, axis=-1, keepdims=True)
                      - vhat * jnp.mean(dvh * vhat, axis=-1, keepdims=True))
        dz_ref[:, 0:D] = (du_s[...] * du).astype(BF16)
        dz_ref[:, D:2 * D] = (d_v * dv).astype(BF16)

    vec = pl.BlockSpec((1, D), lambda i: (0, 0))
    sq = pl.BlockSpec((GROUPS, CHUNK, CHUNK), lambda i: (0, 0, 0))
    return _call(
        body, name="branch_b_bwd", grid=(s // tb,),
        in_specs=[ANY, pl.BlockSpec((tb, D), lambda i: (i, 0)),
                  pl.BlockSpec((tb, D), lambda i: (i, 2)), pl.BlockSpec((tb, D), lambda i: (i, 3)), vec, vec, sq,
                  pl.BlockSpec((CHUNK, GROUPS), lambda i: (0, 0))],
        out_specs=[pl.BlockSpec((tb, 2 * D), lambda i: (i, 1)), sq, sq, pl.BlockSpec((8, D), lambda i: (0, 0))],
        out_shape=[SDS(dz.shape, BF16), SDS((GROUPS, CHUNK, CHUNK), F32), SDS((GROUPS, CHUNK, CHUNK), F32),
                   SDS((8, D), F32)],
        scratch_shapes=[pltpu.VMEM((tb, D), F32), pltpu.VMEM((tb, D), F32)], aliases={0: 0},
        params=_cparams(("arbitrary",), 40), args=(dz, dyb, z, z, ln_g, ln_b, w_s, b_s_t), comm=comm)


def _branch_a_bwd(dz, dya, z, hs, xc, r, ig, conv_w, w_r, w_i, lam, comm=None):
    s = z.shape[0]
    ta = min(T_BRANCH_A, s)
    nb = s // ta
    per16 = ta // 16

    def body(dz_in, dya_ref, xa_ref, ga_ref, hs_ref, hp_ref, xc_ref, r_ref, i_ref, cw_ref, wr_ref, wi_ref,
             lam_ref, dz_ref, vec_ref, dwr_ref, dwi_ref, a_s, b_s, h_s, dcar_s, acar_s, dxc_s):
        del dz_in
        i = pl.program_id(0)
        blk = nb - 1 - i

        @pl.when(i == 0)
        def _():
            dcar_s[...] = jnp.zeros_like(dcar_s)
            acar_s[...] = jnp.zeros_like(acar_s)
            dxc_s[...] = jnp.zeros_like(dxc_s)
            vec_ref[...] = jnp.zeros_like(vec_ref)
            dwr_ref[...] = jnp.zeros_like(dwr_ref)
            dwi_ref[...] = jnp.zeros_like(dwi_ref)

        cw = cw_ref[...]
        lam_v = lam_ref[...]
        xa = xa_ref[...].astype(F32)
        xcb = xc_ref[...]
        xc = xcb.astype(F32)
        sp_lam = _softplus(-lam_v)
        r_v, i_v = r_ref[...].astype(F32), i_ref[...].astype(F32)
        a_v, m_v = _decay(r_v, sp_lam)

        hs_v = hs_ref[...].astype(F32)
        hprev8 = jnp.where(blk > 0, hp_ref[...].astype(F32)[8:16], 0.0)
        h_m1 = _rows_shifted(hprev8, hs_v, 1)
        gg, dgg = _gelu_and_grad(ga_ref[...].astype(F32))
        dya_v = dya_ref[...].astype(F32)
        dz_ref[:, D:2 * D] = (dya_v * hs_v * dgg).astype(BF16)

        a_s[...] = _rows_advanced(a_v, acar_s[...], 1)
        b_s[...] = dya_v * gg

        row = lax.broadcasted_iota(jnp.int32, (8, D), 0)
        ng = ta // 8

        def group(gi, carry):
            off = pl.multiple_of((ng - 1 - gi) * 8, 8)
            c8 = a_s[pl.ds(off, 8), :]
            d8 = b_s[pl.ds(off, 8), :]
            for d in (1, 2, 4):
                c_sh = jnp.where(row < 8 - d, pltpu.roll(c8, 8 - d, 0), 1.0)
                d_sh = jnp.where(row < 8 - d, pltpu.roll(d8, 8 - d, 0), 0.0)
                d8 = c8 * d_sh + d8
                c8 = c8 * c_sh
            dh8 = d8 + c8 * carry
            h_s[pl.ds(off, 8), :] = dh8
            return jnp.broadcast_to(dh8[0:1, :], (8, D))

        dcar_s[...] = lax.fori_loop(0, ng, group, dcar_s[...])
        acar_s[...] = jnp.broadcast_to(a_v[0:1, :], (8, D))

        dbx = h_s[...]
        d_mult = dbx * xc * i_v
        d_loga = dbx * h_m1 * a_v - d_mult * (a_v * a_v) / m_v
        d_pr = d_loga * ((-LRU_C) * sp_lam) * r_v * (1.0 - r_v)
        d_pi = dbx * xc * m_v * i_v * (1.0 - i_v)
        vec_ref[7:8, :] += jnp.sum(d_loga * r_v, axis=0, keepdims=True) * (LRU_C * jax.nn.sigmoid(-lam_v))
        vec_ref[5:6, :] += jnp.sum(d_pr, axis=0, keepdims=True)
        vec_ref[6:7, :] += jnp.sum(d_pi, axis=0, keepdims=True)
        d_prb = d_pr.astype(BF16)
        d_pib = d_pi.astype(BF16)
        h_s[...] = dbx * i_v * m_v
        for h in range(HEADS):
            sl = slice(h * HEAD_DIM, (h + 1) * HEAD_DIM)
            h_s[:, sl] += _dot_nt(d_prb[:, sl], wr_ref[h]) + _dot_nt(d_pib[:, sl], wi_ref[h])
            dwr_ref[h] += _dot_tn(xcb[:, sl], d_prb[:, sl])
            dwi_ref[h] += _dot_tn(xcb[:, sl], d_pib[:, sl])
        d_xc = h_s[...]
        vec_ref[4:5, :] += jnp.sum(d_xc, axis=0, keepdims=True)
        vec_ref[0:1, :] += jnp.sum(d_xc * xa, axis=0, keepdims=True)
        d_xa = cw[0:1, :] * d_xc
        nxt = dxc_s[...]
        for k in range(1, CONV_K):
            ahead = _rows_advanced(d_xc, nxt, k)
            vec_ref[k:k + 1, :] += jnp.sum(ahead * xa, axis=0, keepdims=True)
            d_xa = d_xa + cw[k:k + 1, :] * ahead
        dz_ref[:, 0:D] = d_xa.astype(BF16)
        dxc_s[...] = d_xc[0:8, :]

    vec = pl.BlockSpec((1, D), lambda i: (0, 0))
    gate = pl.BlockSpec((HEADS, HEAD_DIM, HEAD_DIM), lambda i: (0, 0, 0))
    cur = lambda c: pl.BlockSpec((ta, D), lambda i: (nb - 1 - i, c))
    before = lambda c: pl.BlockSpec((16, D), lambda i: (jnp.maximum((nb - 1 - i) * per16 - 1, 0), c))
    return _call(
        body, name="branch_a_bwd", grid=(nb,),
        in_specs=[ANY, cur(0), cur(0), cur(1), cur(0), before(0), cur(0), cur(0), cur(0),
                  pl.BlockSpec((CONV_K, D), lambda i: (0, 0)), gate, gate, vec],
        out_specs=[pl.BlockSpec((ta, 2 * D), lambda i: (nb - 1 - i, 0)), pl.BlockSpec((8, D), lambda i: (0, 0)),
                   gate, gate],
        out_shape=[SDS(dz.shape, BF16), SDS((8, D), F32), SDS((HEADS, HEAD_DIM, HEAD_DIM), F32),
                   SDS((HEADS, HEAD_DIM, HEAD_DIM), F32)],
        scratch_shapes=[pltpu.VMEM((ta, D), F32)] * 3 + [pltpu.VMEM((8, D), F32)] * 3, aliases={0: 0},
        params=_cparams(("arbitrary",), 48),
        args=(dz, dya, z, z, hs, hs, xc, r, ig, conv_w, w_r, w_i, lam), comm=comm)


def _in_bwd(dz, w_in_g, x, dh1, g_mix, comm=None):
    s = x.shape[0]
    tm = min(TM_ROWS, s)
    nj = N_SLOT

    def body(dz_ref, w_ref, x_ref, dh1_ref, g_ref, dx_ref, dg_ref, acc_s):
        i, j = pl.program_id(0), pl.program_id(1)

        @pl.when(j == 0)
        def _():
            acc_s[...] = jnp.zeros_like(acc_s)

        @pl.when((i == 0) & (j == 0))
        def _():
            dg_ref[...] = jnp.zeros_like(dg_ref)

        acc_s[...] += _dot_nt(dz_ref[...], w_ref[...])

        @pl.when(j == nj - 1)
        def _():
            xv = x_ref[...]
            rstd = lax.rsqrt(jnp.mean(xv * xv, axis=-1, keepdims=True) + NORM_EPS)
            xh = xv * rstd
            dn = acc_s[...]
            dg_ref[...] += jnp.sum(dn * xh, axis=0, keepdims=True)
            dhat = dn * g_ref[...]
            dx_ref[...] = dh1_ref[...] + rstd * (dhat - xh * jnp.mean(dhat * xh, axis=-1, keepdims=True))

    row = pl.BlockSpec((tm, D), lambda i, j: (i, 0))
    vec = pl.BlockSpec((1, D), lambda i, j: (0, 0))
    return _call(
        body, name="in_bwd", grid=(s // tm, nj),
        in_specs=[pl.BlockSpec((tm, W_IN_COLS), lambda i, j: (i, j)),
                  pl.BlockSpec((None, D, W_IN_COLS), lambda i, j: (j, 0, 0)), row, row, vec],
        out_specs=[row, vec],
        out_shape=[SDS((s, D), F32), SDS((1, D), F32)],
        scratch_shapes=[pltpu.VMEM((tm, D), F32)],
        params=_cparams(("arbitrary", "arbitrary"), 48), args=(dz, w_in_g, x, dh1, g_mix), comm=comm)


def _wgrad(name, a, b):
    s = a.shape[0]
    ts = min(TM_ROWS, s)
    a_w, b_w = a.shape[1], b.shape[1]

    def body(a_ref, b_ref, o_ref, acc_s):
        t = pl.program_id(0)

        @pl.when(t == 0)
        def _():
            acc_s[...] = jnp.zeros_like(acc_s)

        acc_s[...] += _dot_tn(a_ref[...].astype(BF16), b_ref[...].astype(BF16))

        @pl.when(t == pl.num_programs(0) - 1)
        def _():
            o_ref[...] = acc_s[...].astype(BF16)

    return pl.pallas_call(
        body, name=name, grid=(s // ts,),
        in_specs=[pl.BlockSpec((ts, a_w), lambda t: (t, 0)), pl.BlockSpec((ts, b_w), lambda t: (t, 0))],
        out_specs=pl.BlockSpec((a_w, b_w), lambda t: (0, 0)),
        out_shape=SDS((a_w, b_w), BF16),
        scratch_shapes=[pltpu.VMEM((a_w, b_w), F32)],
        compiler_params=_cparams(("arbitrary",), 48),
    )(a, b)


def _place():
    x, y, c = lax.axis_index("x"), lax.axis_index("y"), lax.axis_index("c")
    return x, y, c


def _other_chips(x, y):
    return [(x, 1 - y, 2 * x + 1 - y), (1 - x, y, 2 * (1 - x) + y), (1 - x, 1 - y, 2 * (1 - x) + 1 - y)]


class _Plan:
    def __init__(self, arrays, out_shape, sems, start, finish, middle=None, middle_at=6):
        self.arrays, self.out_shape, self.sems, self.start, self.finish = arrays, out_shape, sems, start, finish
        self.middle, self.middle_at = middle, middle_at


def _gather_plan(shards, middle_at=6):
    n = len(shards)

    def copies(ins, outs, sems):
        send_sems, recv_sems, local_sems = sems
        x, y, c = _place()
        chip = 2 * x + y
        me = 2 * chip + c
        sib = (x, y, 1 - c)
        chips = _other_chips(x, y)

        def rc(k, t, src, blk, to):
            return pltpu.make_async_remote_copy(
                src_ref=src, dst_ref=outs[t].at[blk], send_sem=send_sems.at[k * n + t],
                recv_sem=recv_sems.at[k * n + t], device_id=to, device_id_type=MESH)

        (yx, yy, y_chip), (xx, xy, x_chip), _ = chips
        local = [pltpu.make_async_copy(ins[t], outs[t].at[me], local_sems.at[t]) for t in range(n)]
        sends = ([rc(0, t, ins[t], me, sib) for t in range(n)] + [rc(1, t, ins[t], me, (yx, yy, c)) for t in range(n)]
                 + [rc(2, t, ins[t], me, (xx, xy, c)) for t in range(n)])
        passed = [[rc(4 + j, t, outs[t].at[2 * pc + c], 2 * pc + c, sib) for t in range(n)]
                  for j, (_, _, pc) in enumerate(chips)]
        relays = [[rc(3, t, outs[t].at[2 * y_chip + c], 2 * y_chip + c, (xx, xy, c)) for t in range(n)],
                  [rc(3, t, outs[t].at[2 * x_chip + c], 2 * x_chip + c, (yx, yy, c)) for t in range(n)]]
        return rc, local, sends, passed, relays, chips, chip, c, sib

    def start(ins, outs, sems):
        _, local, sends, _, _, _, _, _, _ = copies(ins, outs, sems)
        for cp in local + sends:
            cp.start()

    def middle(ins, outs, sems):
        rc, _, _, passed, relays, chips, _, c, sib = copies(ins, outs, sems)
        for j in range(2):
            for t in range(n):
                rc(1 + j, t, ins[t], 2 * chips[j][2] + c, sib).wait_recv()
        for j in range(2):
            for cp in passed[j]:
                cp.start()

            @pl.when(c == j)
            def _():
                for cp in relays[j]:
                    cp.start()

    def finish(ins, outs, sems):
        rc, local, sends, passed, relays, chips, chip, c, sib = copies(ins, outs, sems)
        far = 2 * chips[2][2] + c
        for t in range(n):
            rc(3, t, ins[t], far, sib).wait_recv()
        for cp in passed[2]:
            cp.start()
        for t in range(n):
            rc(0, t, ins[t], 2 * chip + 1 - c, sib).wait_recv()
        for j, (px, py, pc) in enumerate(chips):
            for t in range(n):
                rc(4 + j, t, ins[t], 2 * pc + 1 - c, sib).wait_recv()
        for cp in sends + passed[0] + passed[1] + passed[2]:
            cp.wait_send()
        for j in range(2):
            @pl.when(c == j)
            def _():
                for cp in relays[j]:
                    cp.wait_send()
        for cp in local:
            cp.wait()

    return _Plan(list(shards), [SDS((N_SLOT,) + tuple(a.shape), a.dtype) for a in shards],
                 [pltpu.SemaphoreType.DMA((7 * n,)), pltpu.SemaphoreType.DMA((7 * n,)),
                  pltpu.SemaphoreType.DMA((n,))], start, finish, middle, middle_at)


def _sibling_plan(grads, whole=()):
    n, m = len(grads), len(whole)

    def copies(ins, outs, sems):
        send_sems, recv_sems = sems
        x, y, c = _place()
        sib = (x, y, 1 - c)

        def rc(t, src, dst):
            return pltpu.make_async_remote_copy(src_ref=src, dst_ref=dst, send_sem=send_sems.at[t],
                                                recv_sem=recv_sems.at[t], device_id=sib, device_id_type=MESH)
        return rc, c

    def start(ins, outs, sems):
        rc, c = copies(ins, outs, sems)
        for t in range(n):
            for j in range(4):
                rc(t, ins[t].at[2 * j + 1 - c], outs[t].at[j]).start()
        for t in range(n, n + m):
            rc(t, ins[t], outs[t]).start()

    def finish(ins, outs, sems):
        rc, _ = copies(ins, outs, sems)
        for t in range(n):
            rc(t, ins[t].at[pl.ds(0, 4)], outs[t]).wait()
        for t in range(n, n + m):
            rc(t, ins[t], outs[t]).wait()

    return _Plan(list(grads) + list(whole),
                 [SDS((4,) + tuple(g.shape[1:]), g.dtype) for g in grads] + [SDS(a.shape, a.dtype) for a in whole],
                 [pltpu.SemaphoreType.DMA((n + m,)), pltpu.SemaphoreType.DMA((n + m,))], start, finish)


def _chips_plan(parts, whole=()):
    n, m = len(parts), len(whole)

    def src_of(ins, t, pc):
        return ins[t].at[pc] if t < n else ins[t]

    def local_copies(ins, outs, sems, chip):
        return [pltpu.make_async_copy(src_of(ins, t, chip), outs[t].at[chip], sems[2].at[t]) for t in range(n + m)]

    def start(ins, outs, sems):
        send_sems, recv_sems, _ = sems
        x, y, c = _place()
        chip = 2 * x + y
        for cp in local_copies(ins, outs, sems, chip):
            cp.start()
        for px, py, pc in _other_chips(x, y):
            for t in range(n + m):
                pltpu.make_async_remote_copy(src_ref=src_of(ins, t, pc), dst_ref=outs[t].at[chip],
                                             send_sem=send_sems.at[t], recv_sem=recv_sems.at[t],
                                             device_id=(px, py, c), device_id_type=MESH).start()

    def finish(ins, outs, sems):
        send_sems, recv_sems, _ = sems
        x, y, c = _place()
        for t in range(n + m):
            three = outs[t].at[pl.ds(0, 3)]
            pltpu.make_async_remote_copy(src_ref=three, dst_ref=three, send_sem=send_sems.at[t],
                                         recv_sem=recv_sems.at[t], device_id=(x, y, c), device_id_type=MESH).wait()
        for cp in local_copies(ins, outs, sems, 2 * x + y):
            cp.wait()

    return _Plan(list(parts) + list(whole),
                 [SDS(p.shape, p.dtype) for p in parts] + [SDS((4,) + tuple(a.shape), a.dtype) for a in whole],
                 [pltpu.SemaphoreType.DMA((n + m,)), pltpu.SemaphoreType.DMA((n + m,)),
                  pltpu.SemaphoreType.DMA((n + m,))], start, finish)


def _exchange_plan(arr):
    def peers(x, y, c):
        flip = lambda v, f: 1 - v if f else v
        return [(flip(x, fx), flip(y, fy), flip(c, fc))
                for fx in (0, 1) for fy in (0, 1) for fc in (0, 1) if fx or fy or fc]

    def start(ins, outs, sems):
        x, y, c = _place()
        me = 4 * x + 2 * y + c
        pltpu.make_async_copy(ins[0], outs[0].at[me], sems[2].at[0]).start()
        for to in peers(x, y, c):
            pltpu.make_async_remote_copy(src_ref=ins[0], dst_ref=outs[0].at[me], send_sem=sems[0].at[0],
                                         recv_sem=sems[1].at[0], device_id=to, device_id_type=MESH).start()

    def finish(ins, outs, sems):
        x, y, c = _place()
        seven = outs[0].at[pl.ds(0, 7)]
        pltpu.make_async_remote_copy(src_ref=seven, dst_ref=seven, send_sem=sems[0].at[0], recv_sem=sems[1].at[0],
                                     device_id=(x, y, c), device_id_type=MESH).wait()
        pltpu.make_async_copy(ins[0], outs[0].at[4 * x + 2 * y + c], sems[2].at[0]).wait()

    return _Plan([arr], [SDS((N_SLOT,) + tuple(arr.shape), arr.dtype)],
                 [pltpu.SemaphoreType.DMA((1,)), pltpu.SemaphoreType.DMA((1,)), pltpu.SemaphoreType.DMA((1,))],
                 start, finish)


def _join(*plans):
    def cut(seq, sizes):
        out, at = [], 0
        for k in sizes:
            out.append(seq[at:at + k])
            at += k
        return out

    n_arr = [len(p.arrays) for p in plans]
    n_sem = [len(p.sems) for p in plans]

    def start(ins, outs, sems):
        for p, i, o, s in zip(plans, cut(ins, n_arr), cut(outs, n_arr), cut(sems, n_sem)):
            p.start(i, o, s)

    def finish(ins, outs, sems):
        for p, i, o, s in zip(plans, cut(ins, n_arr), cut(outs, n_arr), cut(sems, n_sem)):
            p.finish(i, o, s)

    def middle(ins, outs, sems):
        for p, i, o, s in zip(plans, cut(ins, n_arr), cut(outs, n_arr), cut(sems, n_sem)):
            if p.middle is not None:
                p.middle(i, o, s)

    return _Plan([a for p in plans for a in p.arrays], [o for p in plans for o in p.out_shape],
                 [s for p in plans for s in p.sems], start, finish,
                 middle if any(p.middle is not None for p in plans) else None,
                 max(p.middle_at for p in plans))


def _run_plan(name, plan):
    k = len(plan.arrays)

    def body(*refs):
        ins, outs, sems = refs[:k], refs[k:2 * k], refs[2 * k:]
        plan.start(ins, outs, sems)
        if plan.middle is not None:
            plan.middle(ins, outs, sems)
        plan.finish(ins, outs, sems)

    return pl.pallas_call(
        body, name=name, in_specs=[ANY] * k, out_specs=[ANY] * k, out_shape=plan.out_shape,
        scratch_shapes=plan.sems, compiler_params=pltpu.CompilerParams(has_side_effects=True),
    )(*plan.arrays)


def _call(body, *, name, grid, in_specs, out_specs, out_shape, scratch_shapes, params, args, comm=None,
          aliases=None, prefetch=()):
    aliases = aliases or {}
    n_pre = len(prefetch)

    def launch(fn, ins_specs, outs_specs, outs_shape, scratch, operands):
        spec = pltpu.PrefetchScalarGridSpec(num_scalar_prefetch=n_pre, grid=grid, in_specs=ins_specs,
                                            out_specs=outs_specs, scratch_shapes=scratch)
        return pl.pallas_call(fn, name=name, grid_spec=spec, out_shape=outs_shape, compiler_params=params,
                              input_output_aliases=aliases)(*prefetch, *[_small_in_hbm(a) for a in operands])

    if comm is None:
        return list(launch(body, in_specs, out_specs, out_shape, scratch_shapes, args)), []
    n_in, n_out, n_scr, k = len(in_specs), len(out_specs), len(scratch_shapes), len(comm.arrays)

    def wrapped(*refs):
        pre, refs = refs[:n_pre], refs[n_pre:]
        ins = refs[:n_in]
        c_in = refs[n_in:n_in + k]
        outs = refs[n_in + k:n_in + k + n_out]
        c_out = refs[n_in + k + n_out:n_in + 2 * k + n_out]
        scr = refs[n_in + 2 * k + n_out:n_in + 2 * k + n_out + n_scr]
        sems = refs[n_in + 2 * k + n_out + n_scr:]
        step, steps = pl.program_id(0), grid[0]
        for d in range(1, len(grid)):
            step, steps = step * grid[d] + pl.program_id(d), steps * grid[d]

        @pl.when(step == 0)
        def _():
            comm.start(c_in, c_out, sems)

        if comm.middle is not None:
            @pl.when(step == (comm.middle_at * steps) // 8)
            def _():
                comm.middle(c_in, c_out, sems)

        body(*pre, *ins, *outs, *scr)

        @pl.when(step == steps - 1)
        def _():
            comm.finish(c_in, c_out, sems)

    res = launch(wrapped, list(in_specs) + [ANY] * k, list(out_specs) + [ANY] * k,
                 list(out_shape) + list(comm.out_shape), list(scratch_shapes) + list(comm.sems),
                 tuple(args) + tuple(comm.arrays))
    return list(res[:n_out]), list(res[n_out:])


def _wgrad_paired(name, a_t, b, core, by_rows=False, comm=None):
    s = b.shape[0]
    m = a_t.shape[0] // N_SLOT if by_rows else a_t.shape[0]
    cols = b.shape[1] if by_rows else b.shape[1] // N_SLOT
    half = N_SLOT // 2

    def owner(k, c):
        return 2 * (k % half) + jnp.where(k < half, 1 - c, c)

    def body(c_ref, a_ref, b_ref, sum_ref, out_s, recv_s, send_sems, recv_sems):
        del c_ref
        k = pl.program_id(0)
        x, y, c = _place()

        def to_sibling(j):
            return pltpu.make_async_remote_copy(src_ref=out_s.at[j % 2], dst_ref=recv_s.at[j],
                                                send_sem=send_sems.at[j], recv_sem=recv_sems.at[j],
                                                device_id=(x, y, 1 - c), device_id_type=MESH)

        @pl.when((k >= 2) & (k < half + 2))
        def _():
            to_sibling(k - 2).wait_send()

        @pl.when(k < half)
        def _():
            out_s[k % 2] = _dot(a_ref[...], b_ref[...]).astype(BF16)
            to_sibling(k).start()

        @pl.when(k >= half)
        def _():
            to_sibling(k - half).wait_recv()
            sum_ref[...] = (_dot(a_ref[...], b_ref[...]) + recv_s[k - half].astype(F32)).astype(BF16)

    if by_rows:
        in_specs = [pl.BlockSpec((m, s), lambda k, c_ref: (owner(k, c_ref[0]), 0)),
                    pl.BlockSpec((s, cols), lambda k, c_ref: (0, 0))]
    else:
        in_specs = [pl.BlockSpec((m, s), lambda k, c_ref: (0, 0)),
                    pl.BlockSpec((s, cols), lambda k, c_ref: (0, owner(k, c_ref[0])))]
    return _call(body, name=name, grid=(N_SLOT,), in_specs=in_specs,
                 out_specs=[pl.BlockSpec((None, m, cols), lambda k, c_ref: (jnp.maximum(k - half, 0), 0, 0))],
                 out_shape=[SDS((half, m, cols), BF16)],
                 scratch_shapes=[pltpu.VMEM((2, m, cols), BF16), pltpu.VMEM((half, m, cols), BF16),
                                 pltpu.SemaphoreType.DMA((half,)), pltpu.SemaphoreType.DMA((half,))],
                 params=_cparams(("arbitrary",), 56), args=(a_t, b), comm=comm, prefetch=(core,))


def _row_tile(rows):
    for t in (512, 256, 128, 64, 32, 16, 8):
        if rows % t == 0:
            return t
    return rows


def _pair_sum(name, g8, recv4, core):
    _, rows, cols = recv4.shape
    tr = _row_tile(rows)
    g42 = g8.reshape(4, 2, rows, cols)

    def body(c_ref, g_ref, r_ref, o_ref):
        del c_ref
        o_ref[...] = (g_ref[...].astype(F32) + r_ref[...].astype(F32)).astype(o_ref.dtype)

    return pl.pallas_call(
        body, name=name,
        grid_spec=pltpu.PrefetchScalarGridSpec(
            num_scalar_prefetch=1, grid=(4, rows // tr),
            in_specs=[pl.BlockSpec((None, None, tr, cols), lambda j, i, c_ref: (j, c_ref[0], i, 0)),
                      pl.BlockSpec((None, tr, cols), lambda j, i, c_ref: (j, i, 0))],
            out_specs=pl.BlockSpec((None, tr, cols), lambda j, i, c_ref: (j, i, 0))),
        out_shape=SDS(recv4.shape, g8.dtype),
        compiler_params=_cparams(("arbitrary", "arbitrary"), 32),
    )(core, g42, recv4)


def _add2(name, a, b):
    rows, cols = a.shape
    tr = _row_tile(rows)

    def body(a_ref, b_ref, o_ref):
        o_ref[...] = a_ref[...] + b_ref[...]

    blk = pl.BlockSpec((tr, cols), lambda i: (i, 0))
    return pl.pallas_call(body, name=name, grid=(rows // tr,), in_specs=[blk, blk], out_specs=blk,
                          out_shape=SDS(a.shape, a.dtype),
                          compiler_params=_cparams(("arbitrary",), 32))(a, b)


def _sum_terms(name, terms):
    k, rows, cols = terms.shape
    tr = _row_tile(rows)

    def body(r_ref, o_ref):
        acc = r_ref[0]
        for q in range(1, k):
            acc = acc + r_ref[q]
        o_ref[...] = acc

    return pl.pallas_call(body, name=name, grid=(rows // tr,),
                          in_specs=[pl.BlockSpec((k, tr, cols), lambda i: (0, i, 0))],
                          out_specs=pl.BlockSpec((tr, cols), lambda i: (i, 0)),
                          out_shape=SDS((rows, cols), terms.dtype),
                          compiler_params=_cparams(("arbitrary",), 32))(terms)


def _adam_update(g, w, m, v):
    c1 = 1.0 / (1.0 - ADAM_B1 ** ADAM_STEP)
    c2 = 1.0 / (1.0 - ADAM_B2 ** ADAM_STEP)
    mn = ADAM_B1 * m + (1.0 - ADAM_B1) * g
    vn = ADAM_B2 * v + (1.0 - ADAM_B2) * (g * g)
    delta = (-ADAM_LR) * ((mn * c1) / (jnp.sqrt(vn * c2) + ADAM_EPS) + ADAM_WD * w)
    return delta, mn, vn


def _adamw_many(name, gs, ws, ms, vs):
    n = len(gs)

    def body(*refs):
        for p in range(n):
            g, w, m, v = (refs[q * n + p][...] for q in range(4))
            d, mn, vn = _adam_update(g, w, m, v)
            refs[4 * n + p][...] = d
            refs[5 * n + p][...] = mn
            refs[6 * n + p][...] = vn

    full = [pl.BlockSpec(w.shape, lambda i: (0, 0)) for w in ws]
    shapes = [SDS(w.shape, F32) for w in ws]
    res = pl.pallas_call(body, name=name, grid=(1,), in_specs=full * 4, out_specs=full * 3, out_shape=shapes * 3,
                         compiler_params=_cparams(("arbitrary",), 32),
                         )(*[_small_in_hbm(a) for a in (*gs, *ws, *ms, *vs)])
    return [(res[p], res[n + p], res[2 * n + p]) for p in range(n)]


def _adamw(name, terms, w, m, v):
    k, rows, cols = terms.shape
    tr = _row_tile(rows)

    def body(t_ref, w_ref, m_ref, v_ref, g_ref, d_ref, mo_ref, vo_ref):
        g = t_ref[0].astype(F32)
        for q in range(1, k):
            g = g + t_ref[q].astype(F32)
        g_ref[...] = g
        d_ref[...], mo_ref[...], vo_ref[...] = _adam_update(g, w_ref[...], m_ref[...], v_ref[...])

    blk = pl.BlockSpec((tr, cols), lambda i: (i, 0))
    return pl.pallas_call(body, name=name, grid=(rows // tr,),
                          in_specs=[pl.BlockSpec((k, tr, cols), lambda i: (0, i, 0)), blk, blk, blk],
                          out_specs=[blk] * 4, out_shape=[SDS((rows, cols), F32)] * 4,
                          compiler_params=_cparams(("arbitrary",), 40),
                          )(*[pltpu.with_memory_space_constraint(a, pltpu.HBM) for a in (terms, w, m, v)])


def kernel(x, norm_mix_g, w_in, conv_w, conv_b, w_rgate, b_rgate, w_igate, b_igate, lru_lambda, w_out_a, sgu_ln_g, sgu_ln_b, sgu_w_s, sgu_b_s, w_out_b, w_out, norm_mlp_g, w_up, w_down, norm_final_g, loss_target, m_norm_mix_g, m_w_in, m_conv_w, m_conv_b, m_w_rgate, m_b_rgate, m_w_igate, m_b_igate, m_lru_lambda, m_w_out_a, m_sgu_ln_g, m_sgu_ln_b, m_sgu_w_s, m_sgu_b_s, m_w_out_b, m_w_out, m_norm_mlp_g, m_w_up, m_w_down, m_norm_final_g, v_norm_mix_g, v_w_in, v_conv_w, v_conv_b, v_w_rgate, v_b_rgate, v_w_igate, v_b_igate, v_lru_lambda, v_w_out_a, v_sgu_ln_g, v_sgu_ln_b, v_sgu_w_s, v_sgu_b_s, v_w_out_b, v_w_out, v_norm_mlp_g, v_w_up, v_w_down, v_norm_final_g):
    cx, cy, cc = _place()
    me = 4 * cx + 2 * cy + cc
    core = jnp.reshape(cc, (1,)).astype(jnp.int32)
    xs = x[0]
    tgt = loss_target[0]

    gate_shard = jnp.stack([w_rgate[0], w_igate[0]]).astype(BF16).reshape(2 * HEADS * 32, HEAD_DIM)
    vec_shard = jnp.concatenate([conv_w[0], b_rgate[0], b_igate[0]], axis=1)
    vec_shard = jnp.pad(vec_shard, ((0, 4), (0, 256 - vec_shard.shape[1])))
    shards = [w_in[0].astype(BF16), w_out_a[0].astype(BF16), w_out_b[0].astype(BF16), w_out[0].astype(BF16),
              w_up[0].astype(BF16), w_down[0].astype(BF16), gate_shard, vec_shard]
    (z, n1_t, w_in_g), (gate_g, vec_g) = _in_proj(xs, norm_mix_g, shards[0], _slot_order(cx, cy, cc),
                                                comm=_gather_plan(shards[6:8]))
    gates = gate_g.reshape(N_SLOT, 2, HEADS, 32, HEAD_DIM).transpose(1, 2, 0, 3, 4).reshape(2, HEADS, HEAD_DIM, HEAD_DIM)
    w_r_f, w_i_f = gates[0], gates[1]
    conv_w_f = vec_g[:, 0:4, 0:128].transpose(1, 0, 2).reshape(CONV_K, D)
    b_r_f = vec_g[:, 0:4, 128:160].transpose(1, 0, 2).reshape(1, D)
    b_i_f = vec_g[:, 0:4, 160:192].transpose(1, 0, 2).reshape(1, D)
    b_s_t = jnp.transpose(sgu_b_s[0])

    (ya, hs, xc, r_gate, i_gate), (w_oa_g, w_ob_g, w_out_g, w_up_g) = _branch_a_fwd(
        z, conv_w_f, conv_b, w_r_f, b_r_f, w_i_f, b_i_f, lru_lambda, comm=_gather_plan(shards[1:5], middle_at=5))
    w_oa_f = w_oa_g.reshape(D, D)
    w_ob_f = w_ob_g.reshape(D, D)
    w_out_f = w_out_g.reshape(D, D)
    (yb, pa, pb, merged, h1), (w_down_g,) = _branch_b_merge_out(
        ya, z, xs, sgu_ln_g, sgu_ln_b, sgu_w_s[0], b_s_t, w_oa_f, w_ob_f, w_out_f,
        comm=_gather_plan(shards[5:6], middle_at=4))
    w_down_f = w_down_g.reshape(N_SLOT * FF_COLS, D)
    r_act, act_t, n2_t, dh2, dh2b, loss_acc, d_gfin = _mlp_fwd(h1, norm_mlp_g, w_up_g, w_down_f,
                                                               norm_final_g.reshape(1, D), tgt)

    def pair(names, grads, recv):
        return [_pair_sum("pair_sum_" + nm, g, r, core) for nm, g, r in zip(names, grads, recv)]

    (p_down,), _ = _wgrad_paired("wgrad_down", act_t, dh2b, core, by_rows=True)
    (df, dh1, d_gmlp), (got_down,) = _mlp_bwd(dh2, dh2b, r_act, w_down_f, w_up_g, h1, norm_mlp_g,
                                              comm=_chips_plan([p_down]))
    (p_up,), _ = _wgrad_paired("wgrad_up", n2_t, df, core)
    g_out = _wgrad("wgrad_out", merged, dh1).reshape(N_SLOT, D // N_SLOT, D)
    (dz, dpa, dpb, dya, dyb), (r_out,) = _merge_bwd(
        dh1, z, pa, pb, w_out_f, w_oa_f, w_ob_f, comm=_sibling_plan([g_out]))
    (p_out,) = pair(["out"], [g_out], [r_out])
    g_oa = _wgrad("wgrad_out_a", ya, dpa).reshape(N_SLOT, D // N_SLOT, D)
    g_ob = _wgrad("wgrad_out_b", yb, dpb).reshape(N_SLOT, D // N_SLOT, D)
    (dz, d_ws, d_bs, d_ln), (got_out, r_oa, r_ob) = _branch_b_bwd(
        dz, dyb, z, sgu_ln_g, sgu_ln_b, sgu_w_s[0], b_s_t,
        comm=_join(_chips_plan([p_out]), _sibling_plan([g_oa, g_ob])))
    p_oa, p_ob = pair(["out_a", "out_b"], [g_oa, g_ob], [r_oa, r_ob])
    (dz, d_vec, d_wr, d_wi), (got_up, got_oa, got_ob) = _branch_a_bwd(
        dz, dya, z, hs, xc, r_gate, i_gate, conv_w_f, w_r_f, w_i_f, lru_lambda,
        comm=_chips_plan([p_up, p_oa, p_ob]))
    g_gate = jnp.stack([d_wr, d_wi]).reshape(2, HEADS, N_SLOT, 32, HEAD_DIM).transpose(2, 0, 1, 3, 4)
    g_gate = g_gate.reshape(N_SLOT, 2 * HEADS * 32, HEAD_DIM).astype(BF16)

    d_bs_row = jnp.pad(d_bs[:, :, 0].reshape(1, GROUPS * CHUNK), ((0, 0), (0, D - GROUPS * CHUNK)))
    vecs = jnp.concatenate([d_vec, jnp.concatenate([d_ln[0:2], d_gmlp, d_gfin, d_bs_row, jnp.zeros((3, D), F32)])])
    d_ws2 = d_ws.reshape(GROUPS * CHUNK, CHUNK)
    (p_in,), (r_gate, r_vecs, r_ws) = _wgrad_paired("wgrad_in", n1_t, dz, core,
                                                    comm=_sibling_plan([g_gate], [vecs, d_ws2]))
    (p_gate,) = pair(["gate"], [g_gate], [r_gate])
    vecs_chip = _add2("pair_sum_vecs", vecs, r_vecs)
    ws_chip = _add2("pair_sum_ws", d_ws2, r_ws)
    (dx, d_gmix), (got_in, got_gate, got_vecs, got_ws) = _in_bwd(
        dz, w_in_g, xs, dh1, norm_mix_g, comm=_chips_plan([p_in, p_gate], [vecs_chip, ws_chip]))
    vecs_sum = _sum_terms("sum_vecs", got_vecs)
    last = jnp.concatenate([d_gmix, jnp.pad(loss_acc[0:1], ((0, 0), (0, D - 128))), jnp.zeros((6, D), F32)])
    (last_all,) = _run_plan("exchange_last", _exchange_plan(last))
    last_sum = _sum_terms("sum_last", last_all)
    loss = last_sum[1, 0]
    got = [got_in, got_oa, got_ob, got_out, got_up, got_down, got_gate]

    def step(nm, terms, w, m, v, rows, cols):
        g, d, mn, vn = _adamw("adamw_" + nm, terms.reshape(4, rows, cols), w.reshape(rows, cols),
                              m.reshape(rows, cols), v.reshape(rows, cols))
        return [a.reshape(w.shape) for a in (g, d, mn, vn)]

    o_in = step("in", got[0], w_in, m_w_in, v_w_in, D, W_IN_COLS)
    o_oa = step("out_a", got[1], w_out_a, m_w_out_a, v_w_out_a, D // N_SLOT, D)
    o_ob = step("out_b", got[2], w_out_b, m_w_out_b, v_w_out_b, D // N_SLOT, D)
    o_out = step("out", got[3], w_out, m_w_out, v_w_out, D // N_SLOT, D)
    o_up = step("up", got[4], w_up, m_w_up, v_w_up, D, FF_COLS)
    o_down = step("down", got[5], w_down, m_w_down, v_w_down, FF_COLS, D)
    gate_w = jnp.stack([w_rgate[0], w_igate[0]]).reshape(2 * HEADS * 32, HEAD_DIM)
    gate_m = jnp.stack([m_w_rgate[0], m_w_igate[0]]).reshape(2 * HEADS * 32, HEAD_DIM)
    gate_v = jnp.stack([v_w_rgate[0], v_w_igate[0]]).reshape(2 * HEADS * 32, HEAD_DIM)
    o_gate = _adamw("adamw_gate", got[6], gate_w, gate_m, gate_v)
    o_gate = [a.reshape(2, 1, HEADS, 32, HEAD_DIM) for a in o_gate]
    o_wr = [a[0] for a in o_gate]
    o_wi = [a[1] for a in o_gate]

    def own(full, width):
        return lax.dynamic_slice_in_dim(full, me * width, width, axis=1)

    small_g = {
        "norm_mix_g": last_sum[0:1], "conv_w": own(vecs_sum[0:4], 128), "conv_b": vecs_sum[4:5],
        "b_rgate": own(vecs_sum[5:6].reshape(HEADS, HEAD_DIM), 32),
        "b_igate": own(vecs_sum[6:7].reshape(HEADS, HEAD_DIM), 32),
        "lru_lambda": vecs_sum[7:8], "sgu_ln_g": vecs_sum[8:9], "sgu_ln_b": vecs_sum[9:10],
        "norm_mlp_g": vecs_sum[10:11], "norm_final_g": vecs_sum[11:12],
        "sgu_b_s": vecs_sum[12, 0:GROUPS * CHUNK].reshape(GROUPS, CHUNK),
    }
    small_w = {"norm_mix_g": (norm_mix_g, m_norm_mix_g, v_norm_mix_g), "conv_w": (conv_w, m_conv_w, v_conv_w),
               "conv_b": (conv_b, m_conv_b, v_conv_b), "b_rgate": (b_rgate, m_b_rgate, v_b_rgate),
               "b_igate": (b_igate, m_b_igate, v_b_igate), "lru_lambda": (lru_lambda, m_lru_lambda, v_lru_lambda),
               "sgu_ln_g": (sgu_ln_g, m_sgu_ln_g, v_sgu_ln_g), "sgu_ln_b": (sgu_ln_b, m_sgu_ln_b, v_sgu_ln_b),
               "norm_mlp_g": (norm_mlp_g, m_norm_mlp_g, v_norm_mlp_g),
               "norm_final_g": (norm_final_g, m_norm_final_g, v_norm_final_g),
               "sgu_b_s": (sgu_b_s, m_sgu_b_s, v_sgu_b_s), "sgu_w_s": (sgu_w_s, m_sgu_w_s, v_sgu_w_s)}
    order = list(small_g)
    as2d = lambda k, a: a.reshape(small_g[k].shape)
    upd = _adamw_many("adamw_small", [small_g[k] for k in order], *[[as2d(k, small_w[k][q]) for k in order]
                                                                     for q in range(3)])
    o_small = {k: [a.reshape(small_w[k][0].shape) for a in (small_g[k],) + u] for k, u in zip(order, upd)}
    ws3 = [a[0].reshape(GROUPS * CHUNK, CHUNK) for a in small_w.pop("sgu_w_s")]
    o_small["sgu_w_s"] = [a.reshape(sgu_w_s.shape) for a in _adamw("adamw_ws", got_ws, *ws3)]

    per_weight = {"norm_mix_g": o_small["norm_mix_g"], "w_in": o_in, "conv_w": o_small["conv_w"],
                  "conv_b": o_small["conv_b"], "w_rgate": o_wr, "b_rgate": o_small["b_rgate"], "w_igate": o_wi,
                  "b_igate": o_small["b_igate"], "lru_lambda": o_small["lru_lambda"], "w_out_a": o_oa,
                  "sgu_ln_g": o_small["sgu_ln_g"], "sgu_ln_b": o_small["sgu_ln_b"], "sgu_w_s": o_small["sgu_w_s"],
                  "sgu_b_s": o_small["sgu_b_s"], "w_out_b": o_ob, "w_out": o_out, "norm_mlp_g": o_small["norm_mlp_g"],
                  "w_up": o_up, "w_down": o_down, "norm_final_g": o_small["norm_final_g"]}
    names_w = list(per_weight)
    return (loss, dx[None], *[per_weight[k][0] for k in names_w], *[per_weight[k][1] for k in names_w],
            *[per_weight[k][2] for k in names_w], *[per_weight[k][3] for k in names_w])
```

```python
import jax
import jax.numpy as jnp
from jax import lax
from jax.experimental import pallas as pl
from jax.experimental.pallas import tpu as pltpu

F32 = jnp.float32
BF16 = jnp.bfloat16
SDS = jax.ShapeDtypeStruct
MESH = pl.DeviceIdType.MESH
ANY = pl.BlockSpec(memory_space=pl.ANY)

D = 1024
N_SLOT = 8
W_IN_COLS = 768
FF_COLS = 512
HEADS, HEAD_DIM = 4, 256
GROUPS, GROUP_DIM = 4, 256
CHUNK = 128
CONV_K = 4
NORM_EPS = 1e-6
LN_EPS = 1e-5
LRU_C = 8.0
ADAM_LR, ADAM_B1, ADAM_B2, ADAM_EPS, ADAM_WD, ADAM_STEP = 0.001, 0.9, 0.999, 1e-08, 0.01, 10

TM_ROWS = 1024
TM_MERGE = 512
T_BRANCH_A = 512
T_BRANCH_B = 512
MiB = 1024 * 1024
SMALL_OPERAND = 16 * 1024

_GELU_C = 0.7978845608028654
_GELU_A = 0.044715


def _small_in_hbm(a):
    return pltpu.with_memory_space_constraint(a, pltpu.HBM) if a.size <= SMALL_OPERAND else a


def _cparams(sem, vmem_mib):
    return pltpu.CompilerParams(dimension_semantics=sem, vmem_limit_bytes=vmem_mib * MiB)


def _gelu(x):
    t = jnp.tanh(_GELU_C * (x + _GELU_A * x * x * x))
    return 0.5 * x * (1.0 + t)


def _gelu_and_grad(x):
    x2 = x * x
    t = jnp.tanh(_GELU_C * x * (1.0 + _GELU_A * x2))
    g = 0.5 * x * (1.0 + t)
    dg = 0.5 * (1.0 + t) + 0.5 * x * (1.0 - t * t) * _GELU_C * (1.0 + 3.0 * _GELU_A * x2)
    return g, dg


def _softplus(x):
    return jnp.maximum(x, 0.0) + jnp.log1p(jnp.exp(-jnp.abs(x)))


def _dot(a, b):
    return jnp.dot(a, b, preferred_element_type=F32)


def _dot_nt(a, b):
    return lax.dot_general(a, b, (((1,), (1,)), ((), ())), preferred_element_type=F32)


def _dot_tn(a, b):
    return lax.dot_general(a, b, (((0,), (0,)), ((), ())), preferred_element_type=F32)


def _rows_shifted(prev8, cur, k):
    ext = jnp.concatenate([prev8, cur], axis=0)
    return pltpu.roll(ext, k, 0)[8:]


def _rows_advanced(cur, next8, k):
    t = cur.shape[0]
    ext = jnp.concatenate([cur, next8], axis=0)
    return pltpu.roll(ext, t + 8 - k, 0)[:t]


def _first_second(x, y, c):
    ny, nx, far = _other_chips(x, y)
    pick = lambda a, b: a * (1 - c) + b * c
    first = tuple(pick(a, b) for a, b in zip(ny, nx))
    second = tuple(pick(b, a) for a, b in zip(ny, nx))
    return first, second, far


def _slot_order(x, y, c):
    chip = 2 * x + y
    first, second, far = _first_second(x, y, c)
    order = [2 * chip + c, 2 * chip + 1 - c, 2 * first[2] + c, 2 * second[2] + 1 - c, 2 * second[2] + c,
             2 * first[2] + 1 - c, 2 * far[2] + c, 2 * far[2] + 1 - c]
    return jnp.stack(order).astype(jnp.int32)


def _in_proj(x, g_mix, w_in_own, order, comm=None):
    s = x.shape[0]
    tm = min(TM_ROWS, s)
    ni = s // tm

    def body(order_ref, x_ref, g_ref, own_ref, z_ref, nt_ref, wg_ref, n_s, w_s, send_sems, recv_sems, local_sems):
        j, i = pl.program_id(0), pl.program_id(1)
        px, py, c = _place()
        chip = 2 * px + py
        me = 2 * chip + c
        sib = (px, py, 1 - c)
        chips = _other_chips(px, py)

        def rc(k, src, blk, to):
            return pltpu.make_async_remote_copy(src_ref=src, dst_ref=w_s.at[blk], send_sem=send_sems.at[k],
                                                recv_sem=recv_sems.at[k], device_id=to, device_id_type=MESH)

        del chips
        first, second, far = _first_second(px, py, c)
        blocks = [2 * first[2] + c, 2 * second[2] + c, 2 * far[2] + c]
        own_in = pltpu.make_async_copy(own_ref, w_s.at[me], local_sems.at[0])
        to_first = rc(1, own_ref, me, (first[0], first[1], c))
        to_second = rc(2, own_ref, me, (second[0], second[1], c))
        relay = rc(3, w_s.at[blocks[0]], blocks[0], (second[0], second[1], c))
        sends = [rc(0, own_ref, me, sib), to_first, to_second, relay]
        passed = [rc(4 + q, w_s.at[blk], blk, sib) for q, blk in enumerate(blocks)]
        keep = pltpu.make_async_copy(w_s, wg_ref, local_sems.at[1])

        @pl.when((i == 0) & (j == 0))
        def _():
            own_in.start()
            sends[0].start()
            to_first.start()
            own_in.wait()

        @pl.when((i == 0) & (j == 1))
        def _():
            rc(0, own_ref, 2 * chip + 1 - c, sib).wait_recv()

        for q, blk in enumerate(blocks):
            @pl.when((i == 0) & (j == 2 + 2 * q))
            def _():
                rc(1 + q, own_ref, blk, sib).wait_recv()
                passed[q].start()
                if q == 0:
                    to_second.start()
                    relay.start()

            @pl.when((i == 0) & (j == 3 + 2 * q))
            def _():
                rc(4 + q, own_ref, order_ref[j], sib).wait_recv()

        rows = pl.ds(pl.multiple_of(i * tm, tm), tm)

        @pl.when(j == 0)
        def _():
            xv = x_ref[...]
            rstd = lax.rsqrt(jnp.mean(xv * xv, axis=-1, keepdims=True) + NORM_EPS)
            nb = (xv * rstd * g_ref[...]).astype(BF16)
            n_s[rows, :] = nb
            nt_ref[...] = nb.T

        z_ref[...] = _dot(n_s[rows, :], w_s[order_ref[j]]).astype(BF16)

        @pl.when((i == 0) & (j == N_SLOT - 1))
        def _():
            keep.start()

        @pl.when((i == ni - 1) & (j == N_SLOT - 1))
        def _():
            for cp in sends + passed:
                cp.wait_send()
            keep.wait()

    first_pass = lambda j, i, o: (jnp.where(j == 0, i, ni - 1), 0)
    (z, n1, w_in_g), extra = _call(
        body, name="in_proj", grid=(N_SLOT, ni), prefetch=(order,),
        in_specs=[pl.BlockSpec((tm, D), first_pass),
                  pl.BlockSpec((1, D), lambda j, i, o: (0, 0)), ANY],
        out_specs=[pl.BlockSpec((tm, W_IN_COLS), lambda j, i, o: (i, o[j])),
                   pl.BlockSpec((D, tm), lambda j, i, o: (0, jnp.where(j == 0, i, ni - 1))), ANY],
        out_shape=[SDS((s, N_SLOT * W_IN_COLS), BF16), SDS((D, s), BF16), SDS((N_SLOT, D, W_IN_COLS), BF16)],
        scratch_shapes=[pltpu.VMEM((s, D), BF16), pltpu.VMEM((N_SLOT, D, W_IN_COLS), BF16),
                        pltpu.SemaphoreType.DMA((7,)), pltpu.SemaphoreType.DMA((7,)), pltpu.SemaphoreType.DMA((2,))],
        params=_cparams(("arbitrary", "arbitrary"), 56), args=(x, g_mix, w_in_own), comm=comm)
    return (z, n1, w_in_g), extra


def _decay(r, sp_lam):
    log_a = (-LRU_C) * r * sp_lam
    a = jnp.exp(log_a)
    return a, jnp.sqrt(-jnp.tanh(log_a) * (a * a + 1.0))


def _lru_gates(xc, xcb, wr_ref, br, wi_ref, bi, sp_lam, a_s, b_s, r_ref, i_ref):
    for h in range(HEADS):
        sl = slice(h * HEAD_DIM, (h + 1) * HEAD_DIM)
        r = jax.nn.sigmoid(_dot(xcb[:, sl], wr_ref[h]) + br[:, sl])
        ig = jax.nn.sigmoid(_dot(xcb[:, sl], wi_ref[h]) + bi[:, sl])
        a, mult = _decay(r, sp_lam[:, sl])
        a_s[:, sl] = a
        b_s[:, sl] = xc[:, sl] * ig * mult
        r_ref[:, sl] = r.astype(BF16)
        i_ref[:, sl] = ig.astype(BF16)


def _conv_fwd(xa, prev8, cw, cb):
    xc = cb + cw[0:1, :] * xa
    for k in range(1, CONV_K):
        xc = xc + cw[k:k + 1, :] * _rows_shifted(prev8, xa, k)
    return xc


def _branch_a_fwd(z, conv_w, conv_b, w_r, b_r, w_i, b_i, lam, comm=None):
    s = z.shape[0]
    ta = min(T_BRANCH_A, s)
    per16 = ta // 16

    def body(xa_ref, xp_ref, ga_ref, cw_ref, cb_ref, wr_ref, br_ref, wi_ref, bi_ref, lam_ref,
             ya_ref, hs_ref, xc_ref, r_ref, i_ref, a_s, b_s, h_s, carry_s):
        i = pl.program_id(0)

        @pl.when(i == 0)
        def _():
            carry_s[...] = jnp.zeros_like(carry_s)

        xa = xa_ref[...].astype(F32)
        prev8 = jnp.where(i > 0, xp_ref[...].astype(F32)[8:16], 0.0)
        xc = _conv_fwd(xa, prev8, cw_ref[...], cb_ref[...])
        xcb = xc.astype(BF16)
        xc_ref[...] = xcb
        sp_lam = _softplus(-lam_ref[...])
        _lru_gates(xc, xcb, wr_ref, br_ref[...], wi_ref, bi_ref[...], sp_lam, a_s, b_s, r_ref, i_ref)

        row = lax.broadcasted_iota(jnp.int32, (8, D), 0)

        def group(g, carry):
            off = pl.multiple_of(g * 8, 8)
            a8 = a_s[pl.ds(off, 8), :]
            b8 = b_s[pl.ds(off, 8), :]
            for d in (1, 2, 4):
                a_sh = jnp.where(row >= d, pltpu.roll(a8, d, 0), 1.0)
                b_sh = jnp.where(row >= d, pltpu.roll(b8, d, 0), 0.0)
                b8 = a8 * b_sh + b8
                a8 = a8 * a_sh
            h8 = b8 + a8 * carry
            h_s[pl.ds(off, 8), :] = h8
            return jnp.broadcast_to(h8[7:8, :], (8, D))

        carry_s[...] = lax.fori_loop(0, ta // 8, group, carry_s[...])
        hs = h_s[...]
        hs_ref[...] = hs.astype(BF16)
        ya_ref[...] = (hs * _gelu(ga_ref[...].astype(F32))).astype(BF16)

    vec = pl.BlockSpec((1, D), lambda i: (0, 0))
    gate = pl.BlockSpec((HEADS, HEAD_DIM, HEAD_DIM), lambda i: (0, 0, 0))
    return _call(
        body, name="branch_a_fwd", grid=(s // ta,),
        in_specs=[pl.BlockSpec((ta, D), lambda i: (i, 0)),
                  pl.BlockSpec((16, D), lambda i: (jnp.maximum(i * per16 - 1, 0), 0)),
                  pl.BlockSpec((ta, D), lambda i: (i, 1)),
                  pl.BlockSpec((CONV_K, D), lambda i: (0, 0)), vec, gate, vec, gate, vec, vec],
        out_specs=[pl.BlockSpec((ta, D), lambda i: (i, 0))] * 5,
        out_shape=[SDS((s, D), BF16)] * 5,
        scratch_shapes=[pltpu.VMEM((ta, D), F32), pltpu.VMEM((ta, D), F32), pltpu.VMEM((ta, D), F32),
                        pltpu.VMEM((8, D), F32)],
        params=_cparams(("arbitrary",), 40), args=(z, z, z, conv_w, conv_b, w_r, b_r, w_i, b_i, lam), comm=comm)


def _sgu_common(ub, vb, lg, lb, with_grad):
    if with_grad:
        u, du = _gelu_and_grad(ub)
        v, dv = _gelu_and_grad(vb)
    else:
        u, v, du, dv = _gelu(ub), _gelu(vb), None, None
    mu = jnp.mean(v, axis=-1, keepdims=True)
    vc = v - mu
    rstd = lax.rsqrt(jnp.mean(vc * vc, axis=-1, keepdims=True) + LN_EPS)
    vhat = vc * rstd
    vln = vhat * lg + lb
    return u, du, dv, rstd, vhat, vln


def _masked_ws(ws_ref):
    t = lax.broadcasted_iota(jnp.int32, (CHUNK, CHUNK), 0)
    c = lax.broadcasted_iota(jnp.int32, (CHUNK, CHUNK), 1)
    keep = c <= t
    return [jnp.where(keep, ws_ref[g], 0.0).astype(BF16) for g in range(GROUPS)]


def _branch_b_merge_out(ya, z, x, ln_g, ln_b, w_s, b_s_t, w_oa, w_ob, w_out, comm=None):
    s = x.shape[0]
    tm = min(TM_MERGE, s)
    assert tm % CHUNK == 0

    def body(ub_ref, vb_ref, lg_ref, lb_ref, ws_ref, bs_ref, ya_ref, ma_ref, mb_ref, x_ref, woa_ref, wob_ref, wo_ref,
             yb_ref, pa_ref, pb_ref, mg_ref, h1_ref):
        u, _, _, _, _, vln = _sgu_common(ub_ref[...].astype(F32), vb_ref[...].astype(F32),
                                         lg_ref[...], lb_ref[...], False)
        vlnb = vln.astype(BF16)
        wm = _masked_ws(ws_ref)
        bs = bs_ref[...]
        for c in range(tm // CHUNK):
            rs = slice(c * CHUNK, (c + 1) * CHUNK)
            for g in range(GROUPS):
                cs = slice(g * GROUP_DIM, (g + 1) * GROUP_DIM)
                sp = _dot(wm[g], vlnb[rs, cs]) + bs[:, g:g + 1]
                yb_ref[rs, cs] = (u[rs, cs] * sp).astype(BF16)

        pa = _dot(ya_ref[...], woa_ref[...])
        pb = _dot(yb_ref[...], wob_ref[...])
        merged = (jax.nn.sigmoid(ma_ref[...].astype(F32)) * pa
                  + jax.nn.sigmoid(mb_ref[...].astype(F32)) * pb).astype(BF16)
        pa_ref[...] = pa.astype(BF16)
        pb_ref[...] = pb.astype(BF16)
        mg_ref[...] = merged
        h1_ref[...] = x_ref[...] + _dot(merged, wo_ref[...])

    row = pl.BlockSpec((tm, D), lambda i: (i, 0))
    col = lambda q: pl.BlockSpec((tm, D), lambda i: (i, q))
    vec = pl.BlockSpec((1, D), lambda i: (0, 0))
    wsp = pl.BlockSpec((D, D), lambda i: (0, 0))
    return _call(
        body, name="branch_b_merge_out", grid=(s // tm,),
        in_specs=[col(2), col(3), vec, vec, pl.BlockSpec((GROUPS, CHUNK, CHUNK), lambda i: (0, 0, 0)),
                  pl.BlockSpec((CHUNK, GROUPS), lambda i: (0, 0)), row, col(4), col(5), row, wsp, wsp, wsp],
        out_specs=[row, row, row, row, row],
        out_shape=[SDS((s, D), BF16)] * 4 + [SDS((s, D), F32)], scratch_shapes=[],
        params=_cparams(("arbitrary",), 56),
        args=(z, z, ln_g, ln_b, w_s, b_s_t, ya, z, z, x, w_oa, w_ob, w_out), comm=comm)


def _mlp_fwd(h1, g_mlp, w_up_g, w_down, g_fin, tgt):
    s = h1.shape[0]
    tm = min(TM_ROWS, s)
    nj = N_SLOT

    def body(h1_ref, gm_ref, wu_ref, wd_ref, gf_ref, t_ref, r_ref, at_ref, n2t_ref, dh2_ref, dh2b_ref, loss_ref,
             dgf_ref, n2_s, acc_s):
        i, j = pl.program_id(0), pl.program_id(1)

        @pl.when(j == 0)
        def _():
            hv = h1_ref[...]
            rstd = lax.rsqrt(jnp.mean(hv * hv, axis=-1, keepdims=True) + NORM_EPS)
            nb = (hv * rstd * gm_ref[...]).astype(BF16)
            n2_s[...] = nb
            n2t_ref[...] = nb.T
            acc_s[...] = jnp.zeros_like(acc_s)

        @pl.when((i == 0) & (j == 0))
        def _():
            loss_ref[...] = jnp.zeros_like(loss_ref)
            dgf_ref[...] = jnp.zeros_like(dgf_ref)

        r = jnp.maximum(_dot(n2_s[...], wu_ref[...]), 0.0)
        r_ref[...] = r.astype(BF16)
        act = (r * r).astype(BF16)
        at_ref[...] = act.T
        acc_s[...] += _dot(act, wd_ref[...])

        @pl.when(j == nj - 1)
        def _():
            h2 = h1_ref[...] + acc_s[...]
            rstd = lax.rsqrt(jnp.mean(h2 * h2, axis=-1, keepdims=True) + NORM_EPS)
            hh = h2 * rstd
            gf = gf_ref[...]
            e = hh * gf - t_ref[...]
            loss_ref[...] += jnp.sum(e * e) * (0.5 / D)
            dy = e * (1.0 / D)
            dgf_ref[...] += jnp.sum(dy * hh, axis=0, keepdims=True)
            dhh = dy * gf
            dh2 = rstd * (dhh - hh * jnp.mean(dhh * hh, axis=-1, keepdims=True))
            dh2_ref[...] = dh2
            dh2b_ref[...] = dh2.astype(BF16)

    row = pl.BlockSpec((tm, D), lambda i, j: (i, 0))
    vec = pl.BlockSpec((1, D), lambda i, j: (0, 0))
    return pl.pallas_call(
        body, name="mlp_fwd", grid=(s // tm, nj),
        in_specs=[row, vec, pl.BlockSpec((None, D, FF_COLS), lambda i, j: (j, 0, 0)),
                  pl.BlockSpec((FF_COLS, D), lambda i, j: (j, 0)), vec, row],
        out_specs=[pl.BlockSpec((tm, FF_COLS), lambda i, j: (i, j)), pl.BlockSpec((FF_COLS, tm), lambda i, j: (j, i)),
                   pl.BlockSpec((D, tm), lambda i, j: (0, i)), row, row, pl.BlockSpec((8, 128), lambda i, j: (0, 0)),
                   vec],
        out_shape=[SDS((s, nj * FF_COLS), BF16), SDS((nj * FF_COLS, s), BF16), SDS((D, s), BF16), SDS((s, D), F32),
                   SDS((s, D), BF16), SDS((8, 128), F32), SDS((1, D), F32)],
        scratch_shapes=[pltpu.VMEM((tm, D), BF16), pltpu.VMEM((tm, D), F32)],
        compiler_params=_cparams(("arbitrary", "arbitrary"), 56),
    )(h1, _small_in_hbm(g_mlp), w_up_g, w_down, _small_in_hbm(g_fin), tgt)


def _mlp_bwd(dh2, dh2b, r, w_down, w_up_g, h1, g_mlp, comm=None):
    s = h1.shape[0]
    tm = min(TM_ROWS, s)
    nj = N_SLOT

    def body(dh2_ref, dh2b_ref, r_ref, wd_ref, wu_ref, h1_ref, gm_ref, df_ref, dh1_ref, dgm_ref, acc_s):
        i, j = pl.program_id(0), pl.program_id(1)

        @pl.when(j == 0)
        def _():
            acc_s[...] = jnp.zeros_like(acc_s)

        @pl.when((i == 0) & (j == 0))
        def _():
            dgm_ref[...] = jnp.zeros_like(dgm_ref)

        d_act = _dot_nt(dh2b_ref[...], wd_ref[...])
        df = (d_act * (2.0 * r_ref[...].astype(F32))).astype(BF16)
        df_ref[...] = df
        acc_s[...] += _dot_nt(df, wu_ref[...])

        @pl.when(j == nj - 1)
        def _():
            hv = h1_ref[...]
            rstd = lax.rsqrt(jnp.mean(hv * hv, axis=-1, keepdims=True) + NORM_EPS)
            hh = hv * rstd
            dn2 = acc_s[...]
            dgm_ref[...] += jnp.sum(dn2 * hh, axis=0, keepdims=True)
            dhat = dn2 * gm_ref[...]
            dh1_ref[...] = dh2_ref[...] + rstd * (dhat - hh * jnp.mean(dhat * hh, axis=-1, keepdims=True))

    row = pl.BlockSpec((tm, D), lambda i, j: (i, 0))
    vec = pl.BlockSpec((1, D), lambda i, j: (0, 0))
    ffb = pl.BlockSpec((tm, FF_COLS), lambda i, j: (i, j))
    return _call(
        body, name="mlp_bwd", grid=(s // tm, nj),
        in_specs=[row, row, ffb, pl.BlockSpec((FF_COLS, D), lambda i, j: (j, 0)),
                  pl.BlockSpec((None, D, FF_COLS), lambda i, j: (j, 0, 0)), row, vec],
        out_specs=[ffb, row, vec],
        out_shape=[SDS((s, nj * FF_COLS), BF16), SDS((s, D), F32), SDS((1, D), F32)],
        scratch_shapes=[pltpu.VMEM((tm, D), F32)],
        params=_cparams(("arbitrary", "arbitrary"), 56), args=(dh2, dh2b, r, w_down, w_up_g, h1, g_mlp), comm=comm)


def _merge_bwd(dh1, z, pa, pb, w_out, w_oa, w_ob, comm=None):
    s = dh1.shape[0]
    tm = min(TM_MERGE, s)

    def body(dh1_ref, ma_ref, mb_ref, pa_ref, pb_ref, wo_ref, woa_ref, wob_ref,
             dz_ref, dpa_ref, dpb_ref, dya_ref, dyb_ref):
        dm = _dot_nt(dh1_ref[...].astype(BF16), wo_ref[...])
        sa = jax.nn.sigmoid(ma_ref[...].astype(F32))
        sb = jax.nn.sigmoid(mb_ref[...].astype(F32))
        dpa = (dm * sa).astype(BF16)
        dpb = (dm * sb).astype(BF16)
        dz_ref[:, 0:D] = (dm * pa_ref[...].astype(F32) * sa * (1.0 - sa)).astype(BF16)
        dz_ref[:, D:2 * D] = (dm * pb_ref[...].astype(F32) * sb * (1.0 - sb)).astype(BF16)
        dpa_ref[...] = dpa
        dpb_ref[...] = dpb
        dya_ref[...] = _dot_nt(dpa, woa_ref[...]).astype(BF16)
        dyb_ref[...] = _dot_nt(dpb, wob_ref[...]).astype(BF16)

    row = pl.BlockSpec((tm, D), lambda i: (i, 0))
    wsp = pl.BlockSpec((D, D), lambda i: (0, 0))
    return _call(
        body, name="merge_bwd", grid=(s // tm,),
        in_specs=[row, pl.BlockSpec((tm, D), lambda i: (i, 4)), pl.BlockSpec((tm, D), lambda i: (i, 5)),
                  row, row, wsp, wsp, wsp],
        out_specs=[pl.BlockSpec((tm, 2 * D), lambda i: (i, 2)), row, row, row, row],
        out_shape=[SDS((s, 6 * D), BF16)] + [SDS((s, D), BF16)] * 4, scratch_shapes=[],
        params=_cparams(("arbitrary",), 48), args=(dh1, z, z, pa, pb, w_out, w_oa, w_ob), comm=comm)


def _branch_b_bwd(dz, dyb, z, ln_g, ln_b, w_s, b_s_t, comm=None):
    s = z.shape[0]
    tb = min(T_BRANCH_B, s)

    def body(dz_in, dyb_ref, ub_ref, vb_ref, lg_ref, lb_ref, ws_ref, bs_ref,
             dz_ref, dws_ref, dbs_ref, dln_ref, du_s, dvln_s):
        del dz_in

        @pl.when(pl.program_id(0) == 0)
        def _():
            dws_ref[...] = jnp.zeros_like(dws_ref)
            dbs_ref[...] = jnp.zeros_like(dbs_ref)
            dln_ref[...] = jnp.zeros_like(dln_ref)

        lg = lg_ref[...]
        u, du, dv, rstd, vhat, vln = _sgu_common(ub_ref[...].astype(F32), vb_ref[...].astype(F32),
                                                 lg, lb_ref[...], True)
        vlnb = vln.astype(BF16)
        dyb_v = dyb_ref[...].astype(F32)
        wm = _masked_ws(ws_ref)
        keep = (lax.broadcasted_iota(jnp.int32, (CHUNK, CHUNK), 1)
                <= lax.broadcasted_iota(jnp.int32, (CHUNK, CHUNK), 0))
        bs = bs_ref[...]
        for c in range(tb // CHUNK):
            rs = slice(c * CHUNK, (c + 1) * CHUNK)
            for g in range(GROUPS):
                cs = slice(g * GROUP_DIM, (g + 1) * GROUP_DIM)
                v_blk = vlnb[rs, cs]
                sp = _dot(wm[g], v_blk) + bs[:, g:g + 1]
                d_sp = dyb_v[rs, cs] * u[rs, cs]
                d_spb = d_sp.astype(BF16)
                du_s[rs, cs] = dyb_v[rs, cs] * sp
                dvln_s[rs, cs] = _dot_tn(wm[g], d_spb)
                dws_ref[g] += jnp.where(keep, _dot_nt(d_spb, v_blk), 0.0)
                dbs_ref[g] += jnp.broadcast_to(jnp.sum(d_sp, axis=-1, keepdims=True), (CHUNK, CHUNK))
        dvln = dvln_s[...]
        dln_ref[0:1, :] += jnp.sum(dvln * vhat, axis=0, keepdims=True)
        dln_ref[1:2, :] += jnp.sum(dvln, axis=0, keepdims=True)
        dvh = dvln * lg
        d_v = rstd * (dvh - jnp.mean(dvh, axis=-1, keepdims=True)
                      - vhat * jnp.mean(dvh * vhat, axis=-1, keepdims=True))
        dz_ref[:, 0:D] = (du_s[...] * du).astype(BF16)
        dz_ref[:, D:2 * D] = (d_v * dv).astype(BF16)

    vec = pl.BlockSpec((1, D), lambda i: (0, 0))
    sq = pl.BlockSpec((GROUPS, CHUNK, CHUNK), lambda i: (0, 0, 0))
    return _call(
        body, name="branch_b_bwd", grid=(s // tb,),
        in_specs=[ANY, pl.BlockSpec((tb, D), lambda i: (i, 0)),
                  pl.BlockSpec((tb, D), lambda i: (i, 2)), pl.BlockSpec((tb, D), lambda i: (i, 3)), vec, vec, sq,
                  pl.BlockSpec((CHUNK, GROUPS), lambda i: (0, 0))],
        out_specs=[pl.BlockSpec((tb, 2 * D), lambda i: (i, 1)), sq, sq, pl.BlockSpec((8, D), lambda i: (0, 0))],
        out_shape=[SDS(dz.shape, BF16), SDS((GROUPS, CHUNK, CHUNK), F32), SDS((GROUPS, CHUNK, CHUNK), F32),
                   SDS((8, D), F32)],
        scratch_shapes=[pltpu.VMEM((tb, D), F32), pltpu.VMEM((tb, D), F32)], aliases={0: 0},
        params=_cparams(("arbitrary",), 40), args=(dz, dyb, z, z, ln_g, ln_b, w_s, b_s_t), comm=comm)


def _branch_a_bwd(dz, dya, z, hs, xc, r, ig, conv_w, w_r, w_i, lam, comm=None):
    s = z.shape[0]
    ta = min(T_BRANCH_A, s)
    nb = s // ta
    per16 = ta // 16

    def body(dz_in, dya_ref, xa_ref, ga_ref, hs_ref, hp_ref, xc_ref, r_ref, i_ref, cw_ref, wr_ref, wi_ref,
             lam_ref, dz_ref, vec_ref, dwr_ref, dwi_ref, a_s, b_s, h_s, dcar_s, acar_s, dxc_s):
        del dz_in
        i = pl.program_id(0)
        blk = nb - 1 - i

        @pl.when(i == 0)
        def _():
            dcar_s[...] = jnp.zeros_like(dcar_s)
            acar_s[...] = jnp.zeros_like(acar_s)
            dxc_s[...] = jnp.zeros_like(dxc_s)
            vec_ref[...] = jnp.zeros_like(vec_ref)
            dwr_ref[...] = jnp.zeros_like(dwr_ref)
            dwi_ref[...] = jnp.zeros_like(dwi_ref)

        cw = cw_ref[...]
        lam_v = lam_ref[...]
        xa = xa_ref[...].astype(F32)
        xcb = xc_ref[...]
        xc = xcb.astype(F32)
        sp_lam = _softplus(-lam_v)
        r_v, i_v = r_ref[...].astype(F32), i_ref[...].astype(F32)
        a_v, m_v = _decay(r_v, sp_lam)

        hs_v = hs_ref[...].astype(F32)
        hprev8 = jnp.where(blk > 0, hp_ref[...].astype(F32)[8:16], 0.0)
        h_m1 = _rows_shifted(hprev8, hs_v, 1)
        gg, dgg = _gelu_and_grad(ga_ref[...].astype(F32))
        dya_v = dya_ref[...].astype(F32)
        dz_ref[:, D:2 * D] = (dya_v * hs_v * dgg).astype(BF16)

        a_s[...] = _rows_advanced(a_v, acar_s[...], 1)
        b_s[...] = dya_v * gg

        row = lax.broadcasted_iota(jnp.int32, (8, D), 0)
        ng = ta // 8

        def group(gi, carry):
            off = pl.multiple_of((ng - 1 - gi) * 8, 8)
            c8 = a_s[pl.ds(off, 8), :]
            d8 = b_s[pl.ds(off, 8), :]
            for d in (1, 2, 4):
                c_sh = jnp.where(row < 8 - d, pltpu.roll(c8, 8 - d, 0), 1.0)
                d_sh = jnp.where(row < 8 - d, pltpu.roll(d8, 8 - d, 0), 0.0)
                d8 = c8 * d_sh + d8
                c8 = c8 * c_sh
            dh8 = d8 + c8 * carry
            h_s[pl.ds(off, 8), :] = dh8
            return jnp.broadcast_to(dh8[0:1, :], (8, D))

        dcar_s[...] = lax.fori_loop(0, ng, group, dcar_s[...])
        acar_s[...] = jnp.broadcast_to(a_v[0:1, :], (8, D))

        dbx = h_s[...]
        d_mult = dbx * xc * i_v
        d_loga = dbx * h_m1 * a_v - d_mult * (a_v * a_v) / m_v
        d_pr = d_loga * ((-LRU_C) * sp_lam) * r_v * (1.0 - r_v)
        d_pi = dbx * xc * m_v * i_v * (1.0 - i_v)
        vec_ref[7:8, :] += jnp.sum(d_loga * r_v, axis=0, keepdims=True) * (LRU_C * jax.nn.sigmoid(-lam_v))
        vec_ref[5:6, :] += jnp.sum(d_pr, axis=0, keepdims=True)
        vec_ref[6:7, :] += jnp.sum(d_pi, axis=0, keepdims=True)
        d_prb = d_pr.astype(BF16)
        d_pib = d_pi.astype(BF16)
        h_s[...] = dbx * i_v * m_v
        for h in range(HEADS):
            sl = slice(h * HEAD_DIM, (h + 1) * HEAD_DIM)
            h_s[:, sl] += _dot_nt(d_prb[:, sl], wr_ref[h]) + _dot_nt(d_pib[:, sl], wi_ref[h])
            dwr_ref[h] += _dot_tn(xcb[:, sl], d_prb[:, sl])
            dwi_ref[h] += _dot_tn(xcb[:, sl], d_pib[:, sl])
        d_xc = h_s[...]
        vec_ref[4:5, :] += jnp.sum(d_xc, axis=0, keepdims=True)
        vec_ref[0:1, :] += jnp.sum(d_xc * xa, axis=0, keepdims=True)
        d_xa = cw[0:1, :] * d_xc
        nxt = dxc_s[...]
        for k in range(1, CONV_K):
            ahead = _rows_advanced(d_xc, nxt, k)
            vec_ref[k:k + 1, :] += jnp.sum(ahead * xa, axis=0, keepdims=True)
            d_xa = d_xa + cw[k:k + 1, :] * ahead
        dz_ref[:, 0:D] = d_xa.astype(BF16)
        dxc_s[...] = d_xc[0:8, :]

    vec = pl.BlockSpec((1, D), lambda i: (0, 0))
    gate = pl.BlockSpec((HEADS, HEAD_DIM, HEAD_DIM), lambda i: (0, 0, 0))
    cur = lambda c: pl.BlockSpec((ta, D), lambda i: (nb - 1 - i, c))
    before = lambda c: pl.BlockSpec((16, D), lambda i: (jnp.maximum((nb - 1 - i) * per16 - 1, 0), c))
    return _call(
        body, name="branch_a_bwd", grid=(nb,),
        in_specs=[ANY, cur(0), cur(0), cur(1), cur(0), before(0), cur(0), cur(0), cur(0),
                  pl.BlockSpec((CONV_K, D), lambda i: (0, 0)), gate, gate, vec],
        out_specs=[pl.BlockSpec((ta, 2 * D), lambda i: (nb - 1 - i, 0)), pl.BlockSpec((8, D), lambda i: (0, 0)),
                   gate, gate],
        out_shape=[SDS(dz.shape, BF16), SDS((8, D), F32), SDS((HEADS, HEAD_DIM, HEAD_DIM), F32),
                   SDS((HEADS, HEAD_DIM, HEAD_DIM), F32)],
        scratch_shapes=[pltpu.VMEM((ta, D), F32)] * 3 + [pltpu.VMEM((8, D), F32)] * 3, aliases={0: 0},
        params=_cparams(("arbitrary",), 48),
        args=(dz, dya, z, z, hs, hs, xc, r, ig, conv_w, w_r, w_i, lam), comm=comm)


def _in_bwd(dz, w_in_g, x, dh1, g_mix, comm=None):
    s = x.shape[0]
    tm = min(TM_ROWS, s)
    nj = N_SLOT

    def body(dz_ref, w_ref, x_ref, dh1_ref, g_ref, dx_ref, dg_ref, acc_s):
        i, j = pl.program_id(0), pl.program_id(1)

        @pl.when(j == 0)
        def _():
            acc_s[...] = jnp.zeros_like(acc_s)

        @pl.when((i == 0) & (j == 0))
        def _():
            dg_ref[...] = jnp.zeros_like(dg_ref)

        acc_s[...] += _dot_nt(dz_ref[...], w_ref[...])

        @pl.when(j == nj - 1)
        def _():
            xv = x_ref[...]
            rstd = lax.rsqrt(jnp.mean(xv * xv, axis=-1, keepdims=True) + NORM_EPS)
            xh = xv * rstd
            dn = acc_s[...]
            dg_ref[...] += jnp.sum(dn * xh, axis=0, keepdims=True)
            dhat = dn * g_ref[...]
            dx_ref[...] = dh1_ref[...] + rstd * (dhat - xh * jnp.mean(dhat * xh, axis=-1, keepdims=True))

    row = pl.BlockSpec((tm, D), lambda i, j: (i, 0))
    vec = pl.BlockSpec((1, D), lambda i, j: (0, 0))
    return _call(
        body, name="in_bwd", grid=(s // tm, nj),
        in_specs=[pl.BlockSpec((tm, W_IN_COLS), lambda i, j: (i, j)),
                  pl.BlockSpec((None, D, W_IN_COLS), lambda i, j: (j, 0, 0)), row, row, vec],
        out_specs=[row, vec],
        out_shape=[SDS((s, D), F32), SDS((1, D), F32)],
        scratch_shapes=[pltpu.VMEM((tm, D), F32)],
        params=_cparams(("arbitrary", "arbitrary"), 48), args=(dz, w_in_g, x, dh1, g_mix), comm=comm)


def _wgrad(name, a, b):
    s = a.shape[0]
    ts = min(TM_ROWS, s)
    a_w, b_w = a.shape[1], b.shape[1]

    def body(a_ref, b_ref, o_ref, acc_s):
        t = pl.program_id(0)

        @pl.when(t == 0)
        def _():
            acc_s[...] = jnp.zeros_like(acc_s)

        acc_s[...] += _dot_tn(a_ref[...].astype(BF16), b_ref[...].astype(BF16))

        @pl.when(t == pl.num_programs(0) - 1)
        def _():
            o_ref[...] = acc_s[...].astype(BF16)

    return pl.pallas_call(
        body, name=name, grid=(s // ts,),
        in_specs=[pl.BlockSpec((ts, a_w), lambda t: (t, 0)), pl.BlockSpec((ts, b_w), lambda t: (t, 0))],
        out_specs=pl.BlockSpec((a_w, b_w), lambda t: (0, 0)),
        out_shape=SDS((a_w, b_w), BF16),
        scratch_shapes=[pltpu.VMEM((a_w, b_w), F32)],
        compiler_params=_cparams(("arbitrary",), 48),
    )(a, b)


def _place():
    x, y, c = lax.axis_index("x"), lax.axis_index("y"), lax.axis_index("c")
    return x, y, c


def _other_chips(x, y):
    return [(x, 1 - y, 2 * x + 1 - y), (1 - x, y, 2 * (1 - x) + y), (1 - x, 1 - y, 2 * (1 - x) + 1 - y)]


class _Plan:
    def __init__(self, arrays, out_shape, sems, start, finish, middle=None, middle_at=6):
        self.arrays, self.out_shape, self.sems, self.start, self.finish = arrays, out_shape, sems, start, finish
        self.middle, self.middle_at = middle, middle_at


def _gather_plan(shards, middle_at=6):
    n = len(shards)

    def copies(ins, outs, sems):
        send_sems, recv_sems, local_sems = sems
        x, y, c = _place()
        chip = 2 * x + y
        me = 2 * chip + c
        sib = (x, y, 1 - c)
        chips = _other_chips(x, y)

        def rc(k, t, src, blk, to):
            return pltpu.make_async_remote_copy(
                src_ref=src, dst_ref=outs[t].at[blk], send_sem=send_sems.at[k * n + t],
                recv_sem=recv_sems.at[k * n + t], device_id=to, device_id_type=MESH)

        (yx, yy, y_chip), (xx, xy, x_chip), _ = chips
        local = [pltpu.make_async_copy(ins[t], outs[t].at[me], local_sems.at[t]) for t in range(n)]
        sends = ([rc(0, t, ins[t], me, sib) for t in range(n)] + [rc(1, t, ins[t], me, (yx, yy, c)) for t in range(n)]
                 + [rc(2, t, ins[t], me, (xx, xy, c)) for t in range(n)])
        passed = [[rc(4 + j, t, outs[t].at[2 * pc + c], 2 * pc + c, sib) for t in range(n)]
                  for j, (_, _, pc) in enumerate(chips)]
        relays = [[rc(3, t, outs[t].at[2 * y_chip + c], 2 * y_chip + c, (xx, xy, c)) for t in range(n)],
                  [rc(3, t, outs[t].at[2 * x_chip + c], 2 * x_chip + c, (yx, yy, c)) for t in range(n)]]
        return rc, local, sends, passed, relays, chips, chip, c, sib

    def start(ins, outs, sems):
        _, local, sends, _, _, _, _, _, _ = copies(ins, outs, sems)
        for cp in local + sends:
            cp.start()

    def middle(ins, outs, sems):
        rc, _, _, passed, relays, chips, _, c, sib = copies(ins, outs, sems)
        for j in range(2):
            for t in range(n):
                rc(1 + j, t, ins[t], 2 * chips[j][2] + c, sib).wait_recv()
        for j in range(2):
            for cp in passed[j]:
                cp.start()

            @pl.when(c == j)
            def _():
                for cp in relays[j]:
                    cp.start()

    def finish(ins, outs, sems):
        rc, local, sends, passed, relays, chips, chip, c, sib = copies(ins, outs, sems)
        far = 2 * chips[2][2] + c
        for t in range(n):
            rc(3, t, ins[t], far, sib).wait_recv()
        for cp in passed[2]:
            cp.start()
        for t in range(n):
            rc(0, t, ins[t], 2 * chip + 1 - c, sib).wait_recv()
        for j, (px, py, pc) in enumerate(chips):
            for t in range(n):
                rc(4 + j, t, ins[t], 2 * pc + 1 - c, sib).wait_recv()
        for cp in sends + passed[0] + passed[1] + passed[2]:
            cp.wait_send()
        for j in range(2):
            @pl.when(c == j)
            def _():
                for cp in relays[j]:
                    cp.wait_send()
        for cp in local:
            cp.wait()

    return _Plan(list(shards), [SDS((N_SLOT,) + tuple(a.shape), a.dtype) for a in shards],
                 [pltpu.SemaphoreType.DMA((7 * n,)), pltpu.SemaphoreType.DMA((7 * n,)),
                  pltpu.SemaphoreType.DMA((n,))], start, finish, middle, middle_at)


def _sibling_plan(grads, whole=()):
    n, m = len(grads), len(whole)

    def copies(ins, outs, sems):
        send_sems, recv_sems = sems
        x, y, c = _place()
        sib = (x, y, 1 - c)

        def rc(t, src, dst):
            return pltpu.make_async_remote_copy(src_ref=src, dst_ref=dst, send_sem=send_sems.at[t],
                                                recv_sem=recv_sems.at[t], device_id=sib, device_id_type=MESH)
        return rc, c

    def start(ins, outs, sems):
        rc, c = copies(ins, outs, sems)
        for t in range(n):
            for j in range(4):
                rc(t, ins[t].at[2 * j + 1 - c], outs[t].at[j]).start()
        for t in range(n, n + m):
            rc(t, ins[t], outs[t]).start()

    def finish(ins, outs, sems):
        rc, _ = copies(ins, outs, sems)
        for t in range(n):
            rc(t, ins[t].at[pl.ds(0, 4)], outs[t]).wait()
        for t in range(n, n + m):
            rc(t, ins[t], outs[t]).wait()

    return _Plan(list(grads) + list(whole),
                 [SDS((4,) + tuple(g.shape[1:]), g.dtype) for g in grads] + [SDS(a.shape, a.dtype) for a in whole],
                 [pltpu.SemaphoreType.DMA((n + m,)), pltpu.SemaphoreType.DMA((n + m,))], start, finish)


def _chips_plan(parts, whole=()):
    n, m = len(parts), len(whole)

    def src_of(ins, t, pc):
        return ins[t].at[pc] if t < n else ins[t]

    def local_copies(ins, outs, sems, chip):
        return [pltpu.make_async_copy(src_of(ins, t, chip), outs[t].at[chip], sems[2].at[t]) for t in range(n + m)]

    def start(ins, outs, sems):
        send_sems, recv_sems, _ = sems
        x, y, c = _place()
        chip = 2 * x + y
        for cp in local_copies(ins, outs, sems, chip):
            cp.start()
        for px, py, pc in _other_chips(x, y):
            for t in range(n + m):
                pltpu.make_async_remote_copy(src_ref=src_of(ins, t, pc), dst_ref=outs[t].at[chip],
                                             send_sem=send_sems.at[t], recv_sem=recv_sems.at[t],
                                             device_id=(px, py, c), device_id_type=MESH).start()

    def finish(ins, outs, sems):
        send_sems, recv_sems, _ = sems
        x, y, c = _place()
        for t in range(n + m):
            three = outs[t].at[pl.ds(0, 3)]
            pltpu.make_async_remote_copy(src_ref=three, dst_ref=three, send_sem=send_sems.at[t],
                                         recv_sem=recv_sems.at[t], device_id=(x, y, c), device_id_type=MESH).wait()
        for cp in local_copies(ins, outs, sems, 2 * x + y):
            cp.wait()

    return _Plan(list(parts) + list(whole),
                 [SDS(p.shape, p.dtype) for p in parts] + [SDS((4,) + tuple(a.shape), a.dtype) for a in whole],
                 [pltpu.SemaphoreType.DMA((n + m,)), pltpu.SemaphoreType.DMA((n + m,)),
                  pltpu.SemaphoreType.DMA((n + m,))], start, finish)


def _exchange_plan(arr):
    def peers(x, y, c):
        flip = lambda v, f: 1 - v if f else v
        return [(flip(x, fx), flip(y, fy), flip(c, fc))
                for fx in (0, 1) for fy in (0, 1) for fc in (0, 1) if fx or fy or fc]

    def start(ins, outs, sems):
        x, y, c = _place()
        me = 4 * x + 2 * y + c
        pltpu.make_async_copy(ins[0], outs[0].at[me], sems[2].at[0]).start()
        for to in peers(x, y, c):
            pltpu.make_async_remote_copy(src_ref=ins[0], dst_ref=outs[0].at[me], send_sem=sems[0].at[0],
                                         recv_sem=sems[1].at[0], device_id=to, device_id_type=MESH).start()

    def finish(ins, outs, sems):
        x, y, c = _place()
        seven = outs[0].at[pl.ds(0, 7)]
        pltpu.make_async_remote_copy(src_ref=seven, dst_ref=seven, send_sem=sems[0].at[0], recv_sem=sems[1].at[0],
                                     device_id=(x, y, c), device_id_type=MESH).wait()
        pltpu.make_async_copy(ins[0], outs[0].at[4 * x + 2 * y + c], sems[2].at[0]).wait()

    return _Plan([arr], [SDS((N_SLOT,) + tuple(arr.shape), arr.dtype)],
                 [pltpu.SemaphoreType.DMA((1,)), pltpu.SemaphoreType.DMA((1,)), pltpu.SemaphoreType.DMA((1,))],
                 start, finish)


def _join(*plans):
    def cut(seq, sizes):
        out, at = [], 0
        for k in sizes:
            out.append(seq[at:at + k])
            at += k
        return out

    n_arr = [len(p.arrays) for p in plans]
    n_sem = [len(p.sems) for p in plans]

    def start(ins, outs, sems):
        for p, i, o, s in zip(plans, cut(ins, n_arr), cut(outs, n_arr), cut(sems, n_sem)):
            p.start(i, o, s)

    def finish(ins, outs, sems):
        for p, i, o, s in zip(plans, cut(ins, n_arr), cut(outs, n_arr), cut(sems, n_sem)):
            p.finish(i, o, s)

    def middle(ins, outs, sems):
        for p, i, o, s in zip(plans, cut(ins, n_arr), cut(outs, n_arr), cut(sems, n_sem)):
            if p.middle is not None:
                p.middle(i, o, s)

    return _Plan([a for p in plans for a in p.arrays], [o for p in plans for o in p.out_shape],
                 [s for p in plans for s in p.sems], start, finish,
                 middle if any(p.middle is not None for p in plans) else None,
                 max(p.middle_at for p in plans))


def _run_plan(name, plan):
    k = len(plan.arrays)

    def body(*refs):
        ins, outs, sems = refs[:k], refs[k:2 * k], refs[2 * k:]
        plan.start(ins, outs, sems)
        if plan.middle is not None:
            plan.middle(ins, outs, sems)
        plan.finish(ins, outs, sems)

    return pl.pallas_call(
        body, name=name, in_specs=[ANY] * k, out_specs=[ANY] * k, out_shape=plan.out_shape,
        scratch_shapes=plan.sems, compiler_params=pltpu.CompilerParams(has_side_effects=True),
    )(*plan.arrays)


def _call(body, *, name, grid, in_specs, out_specs, out_shape, scratch_shapes, params, args, comm=None,
          aliases=None, prefetch=()):
    aliases = aliases or {}
    n_pre = len(prefetch)

    def launch(fn, ins_specs, outs_specs, outs_shape, scratch, operands):
        spec = pltpu.PrefetchScalarGridSpec(num_scalar_prefetch=n_pre, grid=grid, in_specs=ins_specs,
                                            out_specs=outs_specs, scratch_shapes=scratch)
        return pl.pallas_call(fn, name=name, grid_spec=spec, out_shape=outs_shape, compiler_params=params,
                              input_output_aliases=aliases)(*prefetch, *[_small_in_hbm(a) for a in operands])

    if comm is None:
        return list(launch(body, in_specs, out_specs, out_shape, scratch_shapes, args)), []
    n_in, n_out, n_scr, k = len(in_specs), len(out_specs), len(scratch_shapes), len(comm.arrays)

    def wrapped(*refs):
        pre, refs = refs[:n_pre], refs[n_pre:]
        ins = refs[:n_in]
        c_in = refs[n_in:n_in + k]
        outs = refs[n_in + k:n_in + k + n_out]
        c_out = refs[n_in + k + n_out:n_in + 2 * k + n_out]
        scr = refs[n_in + 2 * k + n_out:n_in + 2 * k + n_out + n_scr]
        sems = refs[n_in + 2 * k + n_out + n_scr:]
        step, steps = pl.program_id(0), grid[0]
        for d in range(1, len(grid)):
            step, steps = step * grid[d] + pl.program_id(d), steps * grid[d]

        @pl.when(step == 0)
        def _():
            comm.start(c_in, c_out, sems)

        if comm.middle is not None:
            @pl.when(step == (comm.middle_at * steps) // 8)
            def _():
                comm.middle(c_in, c_out, sems)

        body(*pre, *ins, *outs, *scr)

        @pl.when(step == steps - 1)
        def _():
            comm.finish(c_in, c_out, sems)

    res = launch(wrapped, list(in_specs) + [ANY] * k, list(out_specs) + [ANY] * k,
                 list(out_shape) + list(comm.out_shape), list(scratch_shapes) + list(comm.sems),
                 tuple(args) + tuple(comm.arrays))
    return list(res[:n_out]), list(res[n_out:])


def _wgrad_paired(name, a_t, b, core, by_rows=False, comm=None):
    s = b.shape[0]
    m = a_t.shape[0] // N_SLOT if by_rows else a_t.shape[0]
    cols = b.shape[1] if by_rows else b.shape[1] // N_SLOT
    half = N_SLOT // 2

    def owner(k, c):
        return 2 * (k % half) + jnp.where(k < half, 1 - c, c)

    def body(c_ref, a_ref, b_ref, sum_ref, out_s, recv_s, send_sems, recv_sems):
        del c_ref
        k = pl.program_id(0)
        x, y, c = _place()

        def to_sibling(j):
            return pltpu.make_async_remote_copy(src_ref=out_s.at[j % 2], dst_ref=recv_s.at[j],
                                                send_sem=send_sems.at[j], recv_sem=recv_sems.at[j],
                                                device_id=(x, y, 1 - c), device_id_type=MESH)

        @pl.when((k >= 2) & (k < half + 2))
        def _():
            to_sibling(k - 2).wait_send()

        @pl.when(k < half)
        def _():
            out_s[k % 2] = _dot(a_ref[...], b_ref[...]).astype(BF16)
            to_sibling(k).start()

        @pl.when(k >= half)
        def _():
            to_sibling(k - half).wait_recv()
            sum_ref[...] = (_dot(a_ref[...], b_ref[...]) + recv_s[k - half].astype(F32)).astype(BF16)

    if by_rows:
        in_specs = [pl.BlockSpec((m, s), lambda k, c_ref: (owner(k, c_ref[0]), 0)),
                    pl.BlockSpec((s, cols), lambda k, c_ref: (0, 0))]
    else:
        in_specs = [pl.BlockSpec((m, s), lambda k, c_ref: (0, 0)),
                    pl.BlockSpec((s, cols), lambda k, c_ref: (0, owner(k, c_ref[0])))]
    return _call(body, name=name, grid=(N_SLOT,), in_specs=in_specs,
                 out_specs=[pl.BlockSpec((None, m, cols), lambda k, c_ref: (jnp.maximum(k - half, 0), 0, 0))],
                 out_shape=[SDS((half, m, cols), BF16)],
                 scratch_shapes=[pltpu.VMEM((2, m, cols), BF16), pltpu.VMEM((half, m, cols), BF16),
                                 pltpu.SemaphoreType.DMA((half,)), pltpu.SemaphoreType.DMA((half,))],
                 params=_cparams(("arbitrary",), 56), args=(a_t, b), comm=comm, prefetch=(core,))


def _row_tile(rows):
    for t in (512, 256, 128, 64, 32, 16, 8):
        if rows % t == 0:
            return t
    return rows


def _pair_sum(name, g8, recv4, core):
    _, rows, cols = recv4.shape
    tr = _row_tile(rows)
    g42 = g8.reshape(4, 2, rows, cols)

    def body(c_ref, g_ref, r_ref, o_ref):
        del c_ref
        o_ref[...] = (g_ref[...].astype(F32) + r_ref[...].astype(F32)).astype(o_ref.dtype)

    return pl.pallas_call(
        body, name=name,
        grid_spec=pltpu.PrefetchScalarGridSpec(
            num_scalar_prefetch=1, grid=(4, rows // tr),
            in_specs=[pl.BlockSpec((None, None, tr, cols), lambda j, i, c_ref: (j, c_ref[0], i, 0)),
                      pl.BlockSpec((None, tr, cols), lambda j, i, c_ref: (j, i, 0))],
            out_specs=pl.BlockSpec((None, tr, cols), lambda j, i, c_ref: (j, i, 0))),
        out_shape=SDS(recv4.shape, g8.dtype),
        compiler_params=_cparams(("arbitrary", "arbitrary"), 32),
    )(core, g42, recv4)


def _add2(name, a, b):
    rows, cols = a.shape
    tr = _row_tile(rows)

    def body(a_ref, b_ref, o_ref):
        o_ref[...] = a_ref[...] + b_ref[...]

    blk = pl.BlockSpec((tr, cols), lambda i: (i, 0))
    return pl.pallas_call(body, name=name, grid=(rows // tr,), in_specs=[blk, blk], out_specs=blk,
                          out_shape=SDS(a.shape, a.dtype),
                          compiler_params=_cparams(("arbitrary",), 32))(a, b)


def _sum_terms(name, terms):
    k, rows, cols = terms.shape
    tr = _row_tile(rows)

    def body(r_ref, o_ref):
        acc = r_ref[0]
        for q in range(1, k):
            acc = acc + r_ref[q]
        o_ref[...] = acc

    return pl.pallas_call(body, name=name, grid=(rows // tr,),
                          in_specs=[pl.BlockSpec((k, tr, cols), lambda i: (0, i, 0))],
                          out_specs=pl.BlockSpec((tr, cols), lambda i: (i, 0)),
                          out_shape=SDS((rows, cols), terms.dtype),
                          compiler_params=_cparams(("arbitrary",), 32))(terms)


def _adam_update(g, w, m, v):
    c1 = 1.0 / (1.0 - ADAM_B1 ** ADAM_STEP)
    c2 = 1.0 / (1.0 - ADAM_B2 ** ADAM_STEP)
    mn = ADAM_B1 * m + (1.0 - ADAM_B1) * g
    vn = ADAM_B2 * v + (1.0 - ADAM_B2) * (g * g)
    delta = (-ADAM_LR) * ((mn * c1) / (jnp.sqrt(vn * c2) + ADAM_EPS) + ADAM_WD * w)
    return delta, mn, vn


def _adamw_many(name, gs, ws, ms, vs):
    n = len(gs)

    def body(*refs):
        for p in range(n):
            g, w, m, v = (refs[q * n + p][...] for q in range(4))
            d, mn, vn = _adam_update(g, w, m, v)
            refs[4 * n + p][...] = d
            refs[5 * n + p][...] = mn
            refs[6 * n + p][...] = vn

    full = [pl.BlockSpec(w.shape, lambda i: (0, 0)) for w in ws]
    shapes = [SDS(w.shape, F32) for w in ws]
    res = pl.pallas_call(body, name=name, grid=(1,), in_specs=full * 4, out_specs=full * 3, out_shape=shapes * 3,
                         compiler_params=_cparams(("arbitrary",), 32),
                         )(*[_small_in_hbm(a) for a in (*gs, *ws, *ms, *vs)])
    return [(res[p], res[n + p], res[2 * n + p]) for p in range(n)]


def _adamw(name, terms, w, m, v):
    k, rows, cols = terms.shape
    tr = _row_tile(rows)

    def body(t_ref, w_ref, m_ref, v_ref, g_ref, d_ref, mo_ref, vo_ref):
        g = t_ref[0].astype(F32)
        for q in range(1, k):
            g = g + t_ref[q].astype(F32)
        g_ref[...] = g
        d_ref[...], mo_ref[...], vo_ref[...] = _adam_update(g, w_ref[...], m_ref[...], v_ref[...])

    blk = pl.BlockSpec((tr, cols), lambda i: (i, 0))
    return pl.pallas_call(body, name=name, grid=(rows // tr,),
                          in_specs=[pl.BlockSpec((k, tr, cols), lambda i: (0, i, 0)), blk, blk, blk],
                          out_specs=[blk] * 4, out_shape=[SDS((rows, cols), F32)] * 4,
                          compiler_params=_cparams(("arbitrary",), 40),
                          )(*[pltpu.with_memory_space_constraint(a, pltpu.HBM) for a in (terms, w, m, v)])


def kernel(x, norm_mix_g, w_in, conv_w, conv_b, w_rgate, b_rgate, w_igate, b_igate, lru_lambda, w_out_a, sgu_ln_g, sgu_ln_b, sgu_w_s, sgu_b_s, w_out_b, w_out, norm_mlp_g, w_up, w_down, norm_final_g, loss_target, m_norm_mix_g, m_w_in, m_conv_w, m_conv_b, m_w_rgate, m_b_rgate, m_w_igate, m_b_igate, m_lru_lambda, m_w_out_a, m_sgu_ln_g, m_sgu_ln_b, m_sgu_w_s, m_sgu_b_s, m_w_out_b, m_w_out, m_norm_mlp_g, m_w_up, m_w_down, m_norm_final_g, v_norm_mix_g, v_w_in, v_conv_w, v_conv_b, v_w_rgate, v_b_rgate, v_w_igate, v_b_igate, v_lru_lambda, v_w_out_a, v_sgu_ln_g, v_sgu_ln_b, v_sgu_w_s, v_sgu_b_s, v_w_out_b, v_w_out, v_norm_mlp_g, v_w_up, v_w_down, v_norm_final_g):
    cx, cy, cc = _place()
    me = 4 * cx + 2 * cy + cc
    core = jnp.reshape(cc, (1,)).astype(jnp.int32)
    xs = x[0]
    tgt = loss_target[0]

    gate_shard = jnp.stack([w_rgate[0], w_igate[0]]).astype(BF16).reshape(2 * HEADS * 32, HEAD_DIM)
    vec_shard = jnp.concatenate([conv_w[0], b_rgate[0], b_igate[0]], axis=1)
    vec_shard = jnp.pad(vec_shard, ((0, 4), (0, 256 - vec_shard.shape[1])))
    shards = [w_in[0].astype(BF16), w_out_a[0].astype(BF16), w_out_b[0].astype(BF16), w_out[0].astype(BF16),
              w_up[0].astype(BF16), w_down[0].astype(BF16), gate_shard, vec_shard]
    (z, n1_t, w_in_g), (gate_g, vec_g) = _in_proj(xs, norm_mix_g, shards[0], _slot_order(cx, cy, cc),
                                                comm=_gather_plan(shards[6:8]))
    gates = gate_g.reshape(N_SLOT, 2, HEADS, 32, HEAD_DIM).transpose(1, 2, 0, 3, 4).reshape(2, HEADS, HEAD_DIM, HEAD_DIM)
    w_r_f, w_i_f = gates[0], gates[1]
    conv_w_f = vec_g[:, 0:4, 0:128].transpose(1, 0, 2).reshape(CONV_K, D)
    b_r_f = vec_g[:, 0:4, 128:160].transpose(1, 0, 2).reshape(1, D)
    b_i_f = vec_g[:, 0:4, 160:192].transpose(1, 0, 2).reshape(1, D)
    b_s_t = jnp.transpose(sgu_b_s[0])

    (ya, hs, xc, r_gate, i_gate), (w_oa_g, w_ob_g, w_out_g, w_up_g) = _branch_a_fwd(
        z, conv_w_f, conv_b, w_r_f, b_r_f, w_i_f, b_i_f, lru_lambda, comm=_gather_plan(shards[1:5], middle_at=5))
    w_oa_f = w_oa_g.reshape(D, D)
    w_ob_f = w_ob_g.reshape(D, D)
    w_out_f = w_out_g.reshape(D, D)
    (yb, pa, pb, merged, h1), (w_down_g,) = _branch_b_merge_out(
        ya, z, xs, sgu_ln_g, sgu_ln_b, sgu_w_s[0], b_s_t, w_oa_f, w_ob_f, w_out_f,
        comm=_gather_plan(shards[5:6], middle_at=4))
    w_down_f = w_down_g.reshape(N_SLOT * FF_COLS, D)
    r_act, act_t, n2_t, dh2, dh2b, loss_acc, d_gfin = _mlp_fwd(h1, norm_mlp_g, w_up_g, w_down_f,
                                                               norm_final_g.reshape(1, D), tgt)

    def pair(names, grads, recv):
        return [_pair_sum("pair_sum_" + nm, g, r, core) for nm, g, r in zip(names, grads, recv)]

    (p_down,), _ = _wgrad_paired("wgrad_down", act_t, dh2b, core, by_rows=True)
    (df, dh1, d_gmlp), (got_down,) = _mlp_bwd(dh2, dh2b, r_act, w_down_f, w_up_g, h1, norm_mlp_g,
                                              comm=_chips_plan([p_down]))
    (p_up,), _ = _wgrad_paired("wgrad_up", n2_t, df, core)
    g_out = _wgrad("wgrad_out", merged, dh1).reshape(N_SLOT, D // N_SLOT, D)
    (dz, dpa, dpb, dya, dyb), (r_out,) = _merge_bwd(
        dh1, z, pa, pb, w_out_f, w_oa_f, w_ob_f, comm=_sibling_plan([g_out]))
    (p_out,) = pair(["out"], [g_out], [r_out])
    g_oa = _wgrad("wgrad_out_a", ya, dpa).reshape(N_SLOT, D // N_SLOT, D)
    g_ob = _wgrad("wgrad_out_b", yb, dpb).reshape(N_SLOT, D // N_SLOT, D)
    (dz, d_ws, d_bs, d_ln), (got_out, r_oa, r_ob) = _branch_b_bwd(
        dz, dyb, z, sgu_ln_g, sgu_ln_b, sgu_w_s[0], b_s_t,
        comm=_join(_chips_plan([p_out]), _sibling_plan([g_oa, g_ob])))
    p_oa, p_ob = pair(["out_a", "out_b"], [g_oa, g_ob], [r_oa, r_ob])
    (dz, d_vec, d_wr, d_wi), (got_up, got_oa, got_ob) = _branch_a_bwd(
        dz, dya, z, hs, xc, r_gate, i_gate, conv_w_f, w_r_f, w_i_f, lru_lambda,
        comm=_chips_plan([p_up, p_oa, p_ob]))
    g_gate = jnp.stack([d_wr, d_wi]).reshape(2, HEADS, N_SLOT, 32, HEAD_DIM).transpose(2, 0, 1, 3, 4)
    g_gate = g_gate.reshape(N_SLOT, 2 * HEADS * 32, HEAD_DIM).astype(BF16)

    d_bs_row = jnp.pad(d_bs[:, :, 0].reshape(1, GROUPS * CHUNK), ((0, 0), (0, D - GROUPS * CHUNK)))
    vecs = jnp.concatenate([d_vec, jnp.concatenate([d_ln[0:2], d_gmlp, d_gfin, d_bs_row, jnp.zeros((3, D), F32)])])
    d_ws2 = d_ws.reshape(GROUPS * CHUNK, CHUNK)
    (p_in,), (r_gate, r_vecs, r_ws) = _wgrad_paired("wgrad_in", n1_t, dz, core,
                                                    comm=_sibling_plan([g_gate], [vecs, d_ws2]))
    (p_gate,) = pair(["gate"], [g_gate], [r_gate])
    vecs_chip = _add2("pair_sum_vecs", vecs, r_vecs)
    ws_chip = _add2("pair_sum_ws", d_ws2, r_ws)
    (dx, d_gmix), (got_in, got_gate, got_vecs, got_ws) = _in_bwd(
        dz, w_in_g, xs, dh1, norm_mix_g, comm=_chips_plan([p_in, p_gate], [vecs_chip, ws_chip]))
    vecs_sum = _sum_terms("sum_vecs", got_vecs)
    last = jnp.concatenate([d_gmix, jnp.pad(loss_acc[0:1], ((0, 0), (0, D - 128))), jnp.zeros((6, D), F32)])
    (last_all,) = _run_plan("exchange_last", _exchange_plan(last))
    last_sum = _sum_terms("sum_last", last_all)
    loss = last_sum[1, 0]
    got = [got_in, got_oa, got_ob, got_out, got_up, got_down, got_gate]

    def step(nm, terms, w, m, v, rows, cols):
        g, d, mn, vn = _adamw("adamw_" + nm, terms.reshape(4, rows, cols), w.reshape(rows, cols),
                              m.reshape(rows, cols), v.reshape(rows, cols))
        return [a.reshape(w.shape) for a in (g, d, mn, vn)]

    o_in = step("in", got[0], w_in, m_w_in, v_w_in, D, W_IN_COLS)
    o_oa = step("out_a", got[1], w_out_a, m_w_out_a, v_w_out_a, D // N_SLOT, D)
    o_ob = step("out_b", got[2], w_out_b, m_w_out_b, v_w_out_b, D // N_SLOT, D)
    o_out = step("out", got[3], w_out, m_w_out, v_w_out, D // N_SLOT, D)
    o_up = step("up", got[4], w_up, m_w_up, v_w_up, D, FF_COLS)
    o_down = step("down", got[5], w_down, m_w_down, v_w_down, FF_COLS, D)
    gate_w = jnp.stack([w_rgate[0], w_igate[0]]).reshape(2 * HEADS * 32, HEAD_DIM)
    gate_m = jnp.stack([m_w_rgate[0], m_w_igate[0]]).reshape(2 * HEADS * 32, HEAD_DIM)
    gate_v = jnp.stack([v_w_rgate[0], v_w_igate[0]]).reshape(2 * HEADS * 32, HEAD_DIM)
    o_gate = _adamw("adamw_gate", got[6], gate_w, gate_m, gate_v)
    o_gate = [a.reshape(2, 1, HEADS, 32, HEAD_DIM) for a in o_gate]
    o_wr = [a[0] for a in o_gate]
    o_wi = [a[1] for a in o_gate]

    def own(full, width):
        return lax.dynamic_slice_in_dim(full, me * width, width, axis=1)

    small_g = {
        "norm_mix_g": last_sum[0:1], "conv_w": own(vecs_sum[0:4], 128), "conv_b": vecs_sum[4:5],
        "b_rgate": own(vecs_sum[5:6].reshape(HEADS, HEAD_DIM), 32),
        "b_igate": own(vecs_sum[6:7].reshape(HEADS, HEAD_DIM), 32),
        "lru_lambda": vecs_sum[7:8], "sgu_ln_g": vecs_sum[8:9], "sgu_ln_b": vecs_sum[9:10],
        "norm_mlp_g": vecs_sum[10:11], "norm_final_g": vecs_sum[11:12],
        "sgu_b_s": vecs_sum[12, 0:GROUPS * CHUNK].reshape(GROUPS, CHUNK),
    }
    small_w = {"norm_mix_g": (norm_mix_g, m_norm_mix_g, v_norm_mix_g), "conv_w": (conv_w, m_conv_w, v_conv_w),
               "conv_b": (conv_b, m_conv_b, v_conv_b), "b_rgate": (b_rgate, m_b_rgate, v_b_rgate),
               "b_igate": (b_igate, m_b_igate, v_b_igate), "lru_lambda": (lru_lambda, m_lru_lambda, v_lru_lambda),
               "sgu_ln_g": (sgu_ln_g, m_sgu_ln_g, v_sgu_ln_g), "sgu_ln_b": (sgu_ln_b, m_sgu_ln_b, v_sgu_ln_b),
               "norm_mlp_g": (norm_mlp_g, m_norm_mlp_g, v_norm_mlp_g),
               "norm_final_g": (norm_final_g, m_norm_final_g, v_norm_final_g),
               "sgu_b_s": (sgu_b_s, m_sgu_b_s, v_sgu_b_s), "sgu_w_s": (sgu_w_s, m_sgu_w_s, v_sgu_w_s)}
    order = list(small_g)
    as2d = lambda k, a: a.reshape(small_g[k].shape)
    upd = _adamw_many("adamw_small", [small_g[k] for k in order], *[[as2d(k, small_w[k][q]) for k in order]
                                                                     for q in range(3)])
    o_small = {k: [a.reshape(small_w[k][0].shape) for a in (small_g[k],) + u] for k, u in zip(order, upd)}
    ws3 = [a[0].reshape(GROUPS * CHUNK, CHUNK) for a in small_w.pop("sgu_w_s")]
    o_small["sgu_w_s"] = [a.reshape(sgu_w_s.shape) for a in _adamw("adamw_ws", got_ws, *ws3)]

    per_weight = {"norm_mix_g": o_small["norm_mix_g"], "w_in": o_in, "conv_w": o_small["conv_w"],
                  "conv_b": o_small["conv_b"], "w_rgate": o_wr, "b_rgate": o_small["b_rgate"], "w_igate": o_wi,
                  "b_igate": o_small["b_igate"], "lru_lambda": o_small["lru_lambda"], "w_out_a": o_oa,
                  "sgu_ln_g": o_small["sgu_ln_g"], "sgu_ln_b": o_small["sgu_ln_b"], "sgu_w_s": o_small["sgu_w_s"],
                  "sgu_b_s": o_small["sgu_b_s"], "w_out_b": o_ob, "w_out": o_out, "norm_mlp_g": o_small["norm_mlp_g"],
                  "w_up": o_up, "w_down": o_down, "norm_final_g": o_small["norm_final_g"]}
    names_w = list(per_weight)
    return (loss, dx[None], *[per_weight[k][0] for k in names_w], *[per_weight[k][1] for k in names_w],
            *[per_weight[k][2] for k in names_w], *[per_weight[k][3] for k in names_w])
```

```python
import jax
import jax.numpy as jnp
from jax import lax
from jax.experimental import pallas as pl
from jax.experimental.pallas import tpu as pltpu

F32 = jnp.float32
BF16 = jnp.bfloat16
SDS = jax.ShapeDtypeStruct
MESH = pl.DeviceIdType.MESH
ANY = pl.BlockSpec(memory_space=pl.ANY)

D = 1024
N_SLOT = 8
W_IN_COLS = 768
FF_COLS = 512
HEADS, HEAD_DIM = 4, 256
GROUPS, GROUP_DIM = 4, 256
CHUNK = 128
CONV_K = 4
NORM_EPS = 1e-6
LN_EPS = 1e-5
LRU_C = 8.0
ADAM_LR, ADAM_B1, ADAM_B2, ADAM_EPS, ADAM_WD, ADAM_STEP = 0.001, 0.9, 0.999, 1e-08, 0.01, 10

TM_ROWS = 1024
TM_MERGE = 512
T_BRANCH_A = 512
T_BRANCH_B = 256
MiB = 1024 * 1024
SMALL_OPERAND = 16 * 1024

_GELU_C = 0.7978845608028654
_GELU_A = 0.044715


def _small_in_hbm(a):
    return pltpu.with_memory_space_constraint(a, pltpu.HBM) if a.size <= SMALL_OPERAND else a


def _cparams(sem, vmem_mib):
    return pltpu.CompilerParams(dimension_semantics=sem, vmem_limit_bytes=vmem_mib * MiB)


def _gelu(x):
    t = jnp.tanh(_GELU_C * (x + _GELU_A * x * x * x))
    return 0.5 * x * (1.0 + t)


def _gelu_and_grad(x):
    x2 = x * x
    t = jnp.tanh(_GELU_C * x * (1.0 + _GELU_A * x2))
    g = 0.5 * x * (1.0 + t)
    dg = 0.5 * (1.0 + t) + 0.5 * x * (1.0 - t * t) * _GELU_C * (1.0 + 3.0 * _GELU_A * x2)
    return g, dg


def _softplus(x):
    return jnp.maximum(x, 0.0) + jnp.log1p(jnp.exp(-jnp.abs(x)))


def _dot(a, b):
    return jnp.dot(a, b, preferred_element_type=F32)


def _dot_nt(a, b):
    return lax.dot_general(a, b, (((1,), (1,)), ((), ())), preferred_element_type=F32)


def _dot_tn(a, b):
    return lax.dot_general(a, b, (((0,), (0,)), ((), ())), preferred_element_type=F32)


def _rows_shifted(prev8, cur, k):
    ext = jnp.concatenate([prev8, cur], axis=0)
    return pltpu.roll(ext, k, 0)[8:]


def _rows_advanced(cur, next8, k):
    t = cur.shape[0]
    ext = jnp.concatenate([cur, next8], axis=0)
    return pltpu.roll(ext, t + 8 - k, 0)[:t]


def _first_second(x, y, c):
    ny, nx, far = _other_chips(x, y)
    pick = lambda a, b: a * (1 - c) + b * c
    first = tuple(pick(a, b) for a, b in zip(ny, nx))
    second = tuple(pick(b, a) for a, b in zip(ny, nx))
    return first, second, far


def _slot_order(x, y, c):
    chip = 2 * x + y
    first, second, far = _first_second(x, y, c)
    order = [2 * chip + c, 2 * chip + 1 - c, 2 * first[2] + c, 2 * second[2] + 1 - c, 2 * second[2] + c,
             2 * first[2] + 1 - c, 2 * far[2] + c, 2 * far[2] + 1 - c]
    return jnp.stack(order).astype(jnp.int32)


def _in_proj(x, g_mix, w_in_own, order, comm=None):
    s = x.shape[0]
    tm = min(TM_ROWS, s)
    ni = s // tm

    def body(order_ref, x_ref, g_ref, own_ref, z_ref, nt_ref, wg_ref, n_s, w_s, send_sems, recv_sems, local_sems):
        j, i = pl.program_id(0), pl.program_id(1)
        px, py, c = _place()
        chip = 2 * px + py
        me = 2 * chip + c
        sib = (px, py, 1 - c)
        chips = _other_chips(px, py)

        def rc(k, src, blk, to):
            return pltpu.make_async_remote_copy(src_ref=src, dst_ref=w_s.at[blk], send_sem=send_sems.at[k],
                                                recv_sem=recv_sems.at[k], device_id=to, device_id_type=MESH)

        del chips
        first, second, far = _first_second(px, py, c)
        blocks = [2 * first[2] + c, 2 * second[2] + c, 2 * far[2] + c]
        own_in = pltpu.make_async_copy(own_ref, w_s.at[me], local_sems.at[0])
        to_first = rc(1, own_ref, me, (first[0], first[1], c))
        to_second = rc(2, own_ref, me, (second[0], second[1], c))
        relay = rc(3, w_s.at[blocks[0]], blocks[0], (second[0], second[1], c))
        sends = [rc(0, own_ref, me, sib), to_first, to_second, relay]
        passed = [rc(4 + q, w_s.at[blk], blk, sib) for q, blk in enumerate(blocks)]
        keep = pltpu.make_async_copy(w_s, wg_ref, local_sems.at[1])

        @pl.when((i == 0) & (j == 0))
        def _():
            own_in.start()
            sends[0].start()
            to_first.start()
            own_in.wait()

        @pl.when((i == 0) & (j == 1))
        def _():
            rc(0, own_ref, 2 * chip + 1 - c, sib).wait_recv()

        for q, blk in enumerate(blocks):
            @pl.when((i == 0) & (j == 2 + 2 * q))
            def _():
                rc(1 + q, own_ref, blk, sib).wait_recv()
                passed[q].start()
                if q == 0:
                    to_second.start()
                    relay.start()

            @pl.when((i == 0) & (j == 3 + 2 * q))
            def _():
                rc(4 + q, own_ref, order_ref[j], sib).wait_recv()

        rows = pl.ds(pl.multiple_of(i * tm, tm), tm)

        @pl.when(j == 0)
        def _():
            xv = x_ref[...]
            rstd = lax.rsqrt(jnp.mean(xv * xv, axis=-1, keepdims=True) + NORM_EPS)
            nb = (xv * rstd * g_ref[...]).astype(BF16)
            n_s[rows, :] = nb
            nt_ref[...] = nb.T

        z_ref[...] = _dot(n_s[rows, :], w_s[order_ref[j]]).astype(BF16)

        @pl.when((i == 0) & (j == N_SLOT - 1))
        def _():
            keep.start()

        @pl.when((i == ni - 1) & (j == N_SLOT - 1))
        def _():
            for cp in sends + passed:
                cp.wait_send()
            keep.wait()

    first_pass = lambda j, i, o: (jnp.where(j == 0, i, ni - 1), 0)
    (z, n1, w_in_g), extra = _call(
        body, name="in_proj", grid=(N_SLOT, ni), prefetch=(order,),
        in_specs=[pl.BlockSpec((tm, D), first_pass),
                  pl.BlockSpec((1, D), lambda j, i, o: (0, 0)), ANY],
        out_specs=[pl.BlockSpec((tm, W_IN_COLS), lambda j, i, o: (i, o[j])),
                   pl.BlockSpec((D, tm), lambda j, i, o: (0, jnp.where(j == 0, i, ni - 1))), ANY],
        out_shape=[SDS((s, N_SLOT * W_IN_COLS), BF16), SDS((D, s), BF16), SDS((N_SLOT, D, W_IN_COLS), BF16)],
        scratch_shapes=[pltpu.VMEM((s, D), BF16), pltpu.VMEM((N_SLOT, D, W_IN_COLS), BF16),
                        pltpu.SemaphoreType.DMA((7,)), pltpu.SemaphoreType.DMA((7,)), pltpu.SemaphoreType.DMA((2,))],
        params=_cparams(("arbitrary", "arbitrary"), 56), args=(x, g_mix, w_in_own), comm=comm)
    return (z, n1, w_in_g), extra


def _decay(r, sp_lam):
    log_a = (-LRU_C) * r * sp_lam
    a = jnp.exp(log_a)
    return a, jnp.sqrt(-jnp.tanh(log_a) * (a * a + 1.0))


def _lru_gates(xc, xcb, wr_ref, br, wi_ref, bi, sp_lam, a_s, b_s, r_ref, i_ref):
    for h in range(HEADS):
        sl = slice(h * HEAD_DIM, (h + 1) * HEAD_DIM)
        r = jax.nn.sigmoid(_dot(xcb[:, sl], wr_ref[h]) + br[:, sl])
        ig = jax.nn.sigmoid(_dot(xcb[:, sl], wi_ref[h]) + bi[:, sl])
        a, mult = _decay(r, sp_lam[:, sl])
        a_s[:, sl] = a
        b_s[:, sl] = xc[:, sl] * ig * mult
        r_ref[:, sl] = r.astype(BF16)
        i_ref[:, sl] = ig.astype(BF16)


def _conv_fwd(xa, prev8, cw, cb):
    xc = cb + cw[0:1, :] * xa
    for k in range(1, CONV_K):
        xc = xc + cw[k:k + 1, :] * _rows_shifted(prev8, xa, k)
    return xc


def _branch_a_fwd(z, conv_w, conv_b, w_r, b_r, w_i, b_i, lam, comm=None):
    s = z.shape[0]
    ta = min(T_BRANCH_A, s)
    per16 = ta // 16

    def body(xa_ref, xp_ref, ga_ref, cw_ref, cb_ref, wr_ref, br_ref, wi_ref, bi_ref, lam_ref,
             ya_ref, hs_ref, xc_ref, r_ref, i_ref, a_s, b_s, h_s, carry_s):
        i = pl.program_id(0)

        @pl.when(i == 0)
        def _():
            carry_s[...] = jnp.zeros_like(carry_s)

        xa = xa_ref[...].astype(F32)
        prev8 = jnp.where(i > 0, xp_ref[...].astype(F32)[8:16], 0.0)
        xc = _conv_fwd(xa, prev8, cw_ref[...], cb_ref[...])
        xcb = xc.astype(BF16)
        xc_ref[...] = xcb
        sp_lam = _softplus(-lam_ref[...])
        _lru_gates(xc, xcb, wr_ref, br_ref[...], wi_ref, bi_ref[...], sp_lam, a_s, b_s, r_ref, i_ref)

        row = lax.broadcasted_iota(jnp.int32, (8, D), 0)

        def group(g, carry):
            off = pl.multiple_of(g * 8, 8)
            a8 = a_s[pl.ds(off, 8), :]
            b8 = b_s[pl.ds(off, 8), :]
            for d in (1, 2, 4):
                a_sh = jnp.where(row >= d, pltpu.roll(a8, d, 0), 1.0)
                b_sh = jnp.where(row >= d, pltpu.roll(b8, d, 0), 0.0)
                b8 = a8 * b_sh + b8
                a8 = a8 * a_sh
            h8 = b8 + a8 * carry
            h_s[pl.ds(off, 8), :] = h8
            return jnp.broadcast_to(h8[7:8, :], (8, D))

        carry_s[...] = lax.fori_loop(0, ta // 8, group, carry_s[...])
        hs = h_s[...]
        hs_ref[...] = hs.astype(BF16)
        ya_ref[...] = (hs * _gelu(ga_ref[...].astype(F32))).astype(BF16)

    vec = pl.BlockSpec((1, D), lambda i: (0, 0))
    gate = pl.BlockSpec((HEADS, HEAD_DIM, HEAD_DIM), lambda i: (0, 0, 0))
    return _call(
        body, name="branch_a_fwd", grid=(s // ta,),
        in_specs=[pl.BlockSpec((ta, D), lambda i: (i, 0)),
                  pl.BlockSpec((16, D), lambda i: (jnp.maximum(i * per16 - 1, 0), 0)),
                  pl.BlockSpec((ta, D), lambda i: (i, 1)),
                  pl.BlockSpec((CONV_K, D), lambda i: (0, 0)), vec, gate, vec, gate, vec, vec],
        out_specs=[pl.BlockSpec((ta, D), lambda i: (i, 0))] * 5,
        out_shape=[SDS((s, D), BF16)] * 5,
        scratch_shapes=[pltpu.VMEM((ta, D), F32), pltpu.VMEM((ta, D), F32), pltpu.VMEM((ta, D), F32),
                        pltpu.VMEM((8, D), F32)],
        params=_cparams(("arbitrary",), 40), args=(z, z, z, conv_w, conv_b, w_r, b_r, w_i, b_i, lam), comm=comm)


def _sgu_common(ub, vb, lg, lb, with_grad):
    if with_grad:
        u, du = _gelu_and_grad(ub)
        v, dv = _gelu_and_grad(vb)
    else:
        u, v, du, dv = _gelu(ub), _gelu(vb), None, None
    mu = jnp.mean(v, axis=-1, keepdims=True)
    vc = v - mu
    rstd = lax.rsqrt(jnp.mean(vc * vc, axis=-1, keepdims=True) + LN_EPS)
    vhat = vc * rstd
    vln = vhat * lg + lb
    return u, du, dv, rstd, vhat, vln


def _masked_ws(ws_ref):
    t = lax.broadcasted_iota(jnp.int32, (CHUNK, CHUNK), 0)
    c = lax.broadcasted_iota(jnp.int32, (CHUNK, CHUNK), 1)
    keep = c <= t
    return [jnp.where(keep, ws_ref[g], 0.0).astype(BF16) for g in range(GROUPS)]


def _branch_b_merge_out(ya, z, x, ln_g, ln_b, w_s, b_s_t, w_oa, w_ob, w_out, comm=None):
    s = x.shape[0]
    tm = min(TM_MERGE, s)
    assert tm % CHUNK == 0

    def body(ub_ref, vb_ref, lg_ref, lb_ref, ws_ref, bs_ref, ya_ref, ma_ref, mb_ref, x_ref, woa_ref, wob_ref, wo_ref,
             yb_ref, pa_ref, pb_ref, mg_ref, h1_ref):
        u, _, _, _, _, vln = _sgu_common(ub_ref[...].astype(F32), vb_ref[...].astype(F32),
                                         lg_ref[...], lb_ref[...], False)
        vlnb = vln.astype(BF16)
        wm = _masked_ws(ws_ref)
        bs = bs_ref[...]
        for c in range(tm // CHUNK):
            rs = slice(c * CHUNK, (c + 1) * CHUNK)
            for g in range(GROUPS):
                cs = slice(g * GROUP_DIM, (g + 1) * GROUP_DIM)
                sp = _dot(wm[g], vlnb[rs, cs]) + bs[:, g:g + 1]
                yb_ref[rs, cs] = (u[rs, cs] * sp).astype(BF16)

        pa = _dot(ya_ref[...], woa_ref[...])
        pb = _dot(yb_ref[...], wob_ref[...])
        merged = (jax.nn.sigmoid(ma_ref[...].astype(F32)) * pa
                  + jax.nn.sigmoid(mb_ref[...].astype(F32)) * pb).astype(BF16)
        pa_ref[...] = pa.astype(BF16)
        pb_ref[...] = pb.astype(BF16)
        mg_ref[...] = merged
        h1_ref[...] = x_ref[...] + _dot(merged, wo_ref[...])

    row = pl.BlockSpec((tm, D), lambda i: (i, 0))
    col = lambda q: pl.BlockSpec((tm, D), lambda i: (i, q))
    vec = pl.BlockSpec((1, D), lambda i: (0, 0))
    wsp = pl.BlockSpec((D, D), lambda i: (0, 0))
    return _call(
        body, name="branch_b_merge_out", grid=(s // tm,),
        in_specs=[col(2), col(3), vec, vec, pl.BlockSpec((GROUPS, CHUNK, CHUNK), lambda i: (0, 0, 0)),
                  pl.BlockSpec((CHUNK, GROUPS), lambda i: (0, 0)), row, col(4), col(5), row, wsp, wsp, wsp],
        out_specs=[row, row, row, row, row],
        out_shape=[SDS((s, D), BF16)] * 4 + [SDS((s, D), F32)], scratch_shapes=[],
        params=_cparams(("arbitrary",), 56),
        args=(z, z, ln_g, ln_b, w_s, b_s_t, ya, z, z, x,
              *[pltpu.with_memory_space_constraint(w, pltpu.HBM) for w in (w_oa, w_ob, w_out)]), comm=comm)


def _mlp_fwd(h1, g_mlp, w_up_g, w_down, g_fin, tgt):
    s = h1.shape[0]
    tm = min(TM_ROWS, s)
    nj = N_SLOT

    def body(h1_ref, gm_ref, wu_ref, wd_ref, gf_ref, t_ref, r_ref, at_ref, n2t_ref, dh2_ref, dh2b_ref, loss_ref,
             dgf_ref, n2_s, acc_s):
        i, j = pl.program_id(0), pl.program_id(1)

        @pl.when(j == 0)
        def _():
            hv = h1_ref[...]
            rstd = lax.rsqrt(jnp.mean(hv * hv, axis=-1, keepdims=True) + NORM_EPS)
            nb = (hv * rstd * gm_ref[...]).astype(BF16)
            n2_s[...] = nb
            n2t_ref[...] = nb.T
            acc_s[...] = jnp.zeros_like(acc_s)

        @pl.when((i == 0) & (j == 0))
        def _():
            loss_ref[...] = jnp.zeros_like(loss_ref)
            dgf_ref[...] = jnp.zeros_like(dgf_ref)

        r = jnp.maximum(_dot(n2_s[...], wu_ref[...]), 0.0)
        r_ref[...] = r.astype(BF16)
        act = (r * r).astype(BF16)
        at_ref[...] = act.T
        acc_s[...] += _dot(act, wd_ref[...])

        @pl.when(j == nj - 1)
        def _():
            h2 = h1_ref[...] + acc_s[...]
            rstd = lax.rsqrt(jnp.mean(h2 * h2, axis=-1, keepdims=True) + NORM_EPS)
            hh = h2 * rstd
            gf = gf_ref[...]
            e = hh * gf - t_ref[...]
            loss_ref[...] += jnp.sum(e * e) * (0.5 / D)
            dy = e * (1.0 / D)
            dgf_ref[...] += jnp.sum(dy * hh, axis=0, keepdims=True)
            dhh = dy * gf
            dh2 = rstd * (dhh - hh * jnp.mean(dhh * hh, axis=-1, keepdims=True))
            dh2_ref[...] = dh2
            dh2b_ref[...] = dh2.astype(BF16)

    row = pl.BlockSpec((tm, D), lambda i, j: (i, 0))
    vec = pl.BlockSpec((1, D), lambda i, j: (0, 0))
    return pl.pallas_call(
        body, name="mlp_fwd", grid=(s // tm, nj),
        in_specs=[row, vec, pl.BlockSpec((None, D, FF_COLS), lambda i, j: (j, 0, 0)),
                  pl.BlockSpec((FF_COLS, D), lambda i, j: (j, 0)), vec, row],
        out_specs=[pl.BlockSpec((tm, FF_COLS), lambda i, j: (i, j)), pl.BlockSpec((FF_COLS, tm), lambda i, j: (j, i)),
                   pl.BlockSpec((D, tm), lambda i, j: (0, i)), row, row, pl.BlockSpec((8, 128), lambda i, j: (0, 0)),
                   vec],
        out_shape=[SDS((s, nj * FF_COLS), BF16), SDS((nj * FF_COLS, s), BF16), SDS((D, s), BF16), SDS((s, D), F32),
                   SDS((s, D), BF16), SDS((8, 128), F32), SDS((1, D), F32)],
        scratch_shapes=[pltpu.VMEM((tm, D), BF16), pltpu.VMEM((tm, D), F32)],
        compiler_params=_cparams(("arbitrary", "arbitrary"), 56),
    )(h1, _small_in_hbm(g_mlp), w_up_g, w_down, _small_in_hbm(g_fin), tgt)


def _mlp_bwd(dh2, dh2b, r, w_down, w_up_g, h1, g_mlp, comm=None):
    s = h1.shape[0]
    tm = min(TM_ROWS, s)
    nj = N_SLOT

    def body(dh2_ref, dh2b_ref, r_ref, wd_ref, wu_ref, h1_ref, gm_ref, df_ref, dh1_ref, dgm_ref, acc_s):
        i, j = pl.program_id(0), pl.program_id(1)

        @pl.when(j == 0)
        def _():
            acc_s[...] = jnp.zeros_like(acc_s)

        @pl.when((i == 0) & (j == 0))
        def _():
            dgm_ref[...] = jnp.zeros_like(dgm_ref)

        d_act = _dot_nt(dh2b_ref[...], wd_ref[...])
        df = (d_act * (2.0 * r_ref[...].astype(F32))).astype(BF16)
        df_ref[...] = df
        acc_s[...] += _dot_nt(df, wu_ref[...])

        @pl.when(j == nj - 1)
        def _():
            hv = h1_ref[...]
            rstd = lax.rsqrt(jnp.mean(hv * hv, axis=-1, keepdims=True) + NORM_EPS)
            hh = hv * rstd
            dn2 = acc_s[...]
            dgm_ref[...] += jnp.sum(dn2 * hh, axis=0, keepdims=True)
            dhat = dn2 * gm_ref[...]
            dh1_ref[...] = dh2_ref[...] + rstd * (dhat - hh * jnp.mean(dhat * hh, axis=-1, keepdims=True))

    row = pl.BlockSpec((tm, D), lambda i, j: (i, 0))
    vec = pl.BlockSpec((1, D), lambda i, j: (0, 0))
    ffb = pl.BlockSpec((tm, FF_COLS), lambda i, j: (i, j))
    return _call(
        body, name="mlp_bwd", grid=(s // tm, nj),
        in_specs=[row, row, ffb, pl.BlockSpec((FF_COLS, D), lambda i, j: (j, 0)),
                  pl.BlockSpec((None, D, FF_COLS), lambda i, j: (j, 0, 0)), row, vec],
        out_specs=[ffb, row, vec],
        out_shape=[SDS((s, nj * FF_COLS), BF16), SDS((s, D), F32), SDS((1, D), F32)],
        scratch_shapes=[pltpu.VMEM((tm, D), F32)],
        params=_cparams(("arbitrary", "arbitrary"), 56), args=(dh2, dh2b, r, w_down, w_up_g, h1, g_mlp), comm=comm)


def _merge_bwd(dh1, z, pa, pb, w_out, w_oa, w_ob, comm=None):
    s = dh1.shape[0]
    tm = min(TM_MERGE, s)

    def body(dh1_ref, ma_ref, mb_ref, pa_ref, pb_ref, wo_ref, woa_ref, wob_ref,
             dz_ref, dpa_ref, dpb_ref, dya_ref, dyb_ref):
        dm = _dot_nt(dh1_ref[...].astype(BF16), wo_ref[...])
        sa = jax.nn.sigmoid(ma_ref[...].astype(F32))
        sb = jax.nn.sigmoid(mb_ref[...].astype(F32))
        dpa = (dm * sa).astype(BF16)
        dpb = (dm * sb).astype(BF16)
        dz_ref[:, 0:D] = (dm * pa_ref[...].astype(F32) * sa * (1.0 - sa)).astype(BF16)
        dz_ref[:, D:2 * D] = (dm * pb_ref[...].astype(F32) * sb * (1.0 - sb)).astype(BF16)
        dpa_ref[...] = dpa
        dpb_ref[...] = dpb
        dya_ref[...] = _dot_nt(dpa, woa_ref[...]).astype(BF16)
        dyb_ref[...] = _dot_nt(dpb, wob_ref[...]).astype(BF16)

    row = pl.BlockSpec((tm, D), lambda i: (i, 0))
    wsp = pl.BlockSpec((D, D), lambda i: (0, 0))
    return _call(
        body, name="merge_bwd", grid=(s // tm,),
        in_specs=[row, pl.BlockSpec((tm, D), lambda i: (i, 4)), pl.BlockSpec((tm, D), lambda i: (i, 5)),
                  row, row, wsp, wsp, wsp],
        out_specs=[pl.BlockSpec((tm, 2 * D), lambda i: (i, 2)), row, row, row, row],
        out_shape=[SDS((s, 6 * D), BF16)] + [SDS((s, D), BF16)] * 4, scratch_shapes=[],
        params=_cparams(("arbitrary",), 48), args=(dh1, z, z, pa, pb, w_out, w_oa, w_ob), comm=comm)


def _branch_b_bwd(dz, dyb, z, ln_g, ln_b, w_s, b_s_t, comm=None):
    s = z.shape[0]
    tb = min(T_BRANCH_B, s)

    def body(dz_in, dyb_ref, ub_ref, vb_ref, lg_ref, lb_ref, ws_ref, bs_ref,
             dz_ref, dws_ref, dbs_ref, dln_ref, du_s, dvln_s):
        del dz_in

        @pl.when(pl.program_id(0) == 0)
        def _():
            dws_ref[...] = jnp.zeros_like(dws_ref)
            dbs_ref[...] = jnp.zeros_like(dbs_ref)
            dln_ref[...] = jnp.zeros_like(dln_ref)

        lg = lg_ref[...]
        u, du, dv, rstd, vhat, vln = _sgu_common(ub_ref[...].astype(F32), vb_ref[...].astype(F32),
                                                 lg, lb_ref[...], True)
        vlnb = vln.astype(BF16)
        dyb_v = dyb_ref[...].astype(F32)
        wm = _masked_ws(ws_ref)
        keep = (lax.broadcasted_iota(jnp.int32, (CHUNK, CHUNK), 1)
                <= lax.broadcasted_iota(jnp.int32, (CHUNK, CHUNK), 0))
        bs = bs_ref[...]
        for c in range(tb // CHUNK):
            rs = slice(c * CHUNK, (c + 1) * CHUNK)
            for g in range(GROUPS):
                cs = slice(g * GROUP_DIM, (g + 1) * GROUP_DIM)
                v_blk = vlnb[rs, cs]
                sp = _dot(wm[g], v_blk) + bs[:, g:g + 1]
                d_sp = dyb_v[rs, cs] * u[rs, cs]
                d_spb = d_sp.astype(BF16)
                du_s[rs, cs] = dyb_v[rs, cs] * sp
                dvln_s[rs, cs] = _dot_tn(wm[g], d_spb)
                dws_ref[g] += jnp.where(keep, _dot_nt(d_spb, v_blk), 0.0)
                dbs_ref[g] += jnp.broadcast_to(jnp.sum(d_sp, axis=-1, keepdims=True), (CHUNK, CHUNK))
        dvln = dvln_s[...]
        dln_ref[0:1, :] += jnp.sum(dvln * vhat, axis=0, keepdims=True)
        dln_ref[1:2, :] += jnp.sum(dvln, axis=0, keepdims=True)
        dvh = dvln * lg
        d_v = rstd * (dvh - jnp.mean(dvh, axis=-1, keepdims=True)
                      - vhat * jnp.mean(dvh * vhat, axis=-1, keepdims=True))
        dz_ref[:, 0:D] = (du_s[...] * du).astype(BF16)
        dz_ref[:, D:2 * D] = (d_v * dv).astype(BF16)

    vec = pl.BlockSpec((1, D), lambda i: (0, 0))
    sq = pl.BlockSpec((GROUPS, CHUNK, CHUNK), lambda i: (0, 0, 0))
    return _call(
        body, name="branch_b_bwd", grid=(s // tb,),
        in_specs=[ANY, pl.BlockSpec((tb, D), lambda i: (i, 0)),
                  pl.BlockSpec((tb, D), lambda i: (i, 2)), pl.BlockSpec((tb, D), lambda i: (i, 3)), vec, vec, sq,
                  pl.BlockSpec((CHUNK, GROUPS), lambda i: (0, 0))],
        out_specs=[pl.BlockSpec((tb, 2 * D), lambda i: (i, 1)), sq, sq, pl.BlockSpec((8, D), lambda i: (0, 0))],
        out_shape=[SDS(dz.shape, BF16), SDS((GROUPS, CHUNK, CHUNK), F32), SDS((GROUPS, CHUNK, CHUNK), F32),
                   SDS((8, D), F32)],
        scratch_shapes=[pltpu.VMEM((tb, D), F32), pltpu.VMEM((tb, D), F32)], aliases={0: 0},
        params=_cparams(("arbitrary",), 40), args=(dz, dyb, z, z, ln_g, ln_b, w_s, b_s_t), comm=comm)


def _branch_a_bwd(dz, dya, z, hs, xc, r, ig, conv_w, w_r, w_i, lam, comm=None):
    s = z.shape[0]
    ta = min(T_BRANCH_A, s)
    nb = s // ta
    per16 = ta // 16

    def body(dz_in, dya_ref, xa_ref, ga_ref, hs_ref, hp_ref, xc_ref, r_ref, i_ref, cw_ref, wr_ref, wi_ref,
             lam_ref, dz_ref, vec_ref, dwr_ref, dwi_ref, a_s, b_s, h_s, dcar_s, acar_s, dxc_s):
        del dz_in
        i = pl.program_id(0)
        blk = nb - 1 - i

        @pl.when(i == 0)
        def _():
            dcar_s[...] = jnp.zeros_like(dcar_s)
            acar_s[...] = jnp.zeros_like(acar_s)
            dxc_s[...] = jnp.zeros_like(dxc_s)
            vec_ref[...] = jnp.zeros_like(vec_ref)
            dwr_ref[...] = jnp.zeros_like(dwr_ref)
            dwi_ref[...] = jnp.zeros_like(dwi_ref)

        cw = cw_ref[...]
        lam_v = lam_ref[...]
        xa = xa_ref[...].astype(F32)
        xcb = xc_ref[...]
        xc = xcb.astype(F32)
        sp_lam = _softplus(-lam_v)
        r_v, i_v = r_ref[...].astype(F32), i_ref[...].astype(F32)
        a_v, m_v = _decay(r_v, sp_lam)

        hs_v = hs_ref[...].astype(F32)
        hprev8 = jnp.where(blk > 0, hp_ref[...].astype(F32)[8:16], 0.0)
        h_m1 = _rows_shifted(hprev8, hs_v, 1)
        gg, dgg = _gelu_and_grad(ga_ref[...].astype(F32))
        dya_v = dya_ref[...].astype(F32)
        dz_ref[:, D:2 * D] = (dya_v * hs_v * dgg).astype(BF16)

        a_s[...] = _rows_advanced(a_v, acar_s[...], 1)
        b_s[...] = dya_v * gg

        row = lax.broadcasted_iota(jnp.int32, (8, D), 0)
        ng = ta // 8

        def group(gi, carry):
            off = pl.multiple_of((ng - 1 - gi) * 8, 8)
            c8 = a_s[pl.ds(off, 8), :]
            d8 = b_s[pl.ds(off, 8), :]
            for d in (1, 2, 4):
                c_sh = jnp.where(row < 8 - d, pltpu.roll(c8, 8 - d, 0), 1.0)
                d_sh = jnp.where(row < 8 - d, pltpu.roll(d8, 8 - d, 0), 0.0)
                d8 = c8 * d_sh + d8
                c8 = c8 * c_sh
            dh8 = d8 + c8 * carry
            h_s[pl.ds(off, 8), :] = dh8
            return jnp.broadcast_to(dh8[0:1, :], (8, D))

        dcar_s[...] = lax.fori_loop(0, ng, group, dcar_s[...])
        acar_s[...] = jnp.broadcast_to(a_v[0:1, :], (8, D))

        dbx = h_s[...]
        d_mult = dbx * xc * i_v
        d_loga = dbx * h_m1 * a_v - d_mult * (a_v * a_v) / m_v
        d_pr = d_loga * ((-LRU_C) * sp_lam) * r_v * (1.0 - r_v)
        d_pi = dbx * xc * m_v * i_v * (1.0 - i_v)
        vec_ref[7:8, :] += jnp.sum(d_loga * r_v, axis=0, keepdims=True) * (LRU_C * jax.nn.sigmoid(-lam_v))
        vec_ref[5:6, :] += jnp.sum(d_pr, axis=0, keepdims=True)
        vec_ref[6:7, :] += jnp.sum(d_pi, axis=0, keepdims=True)
        d_prb = d_pr.astype(BF16)
        d_pib = d_pi.astype(BF16)
        h_s[...] = dbx * i_v * m_v
        for h in range(HEADS):
            sl = slice(h * HEAD_DIM, (h + 1) * HEAD_DIM)
            h_s[:, sl] += _dot_nt(d_prb[:, sl], wr_ref[h]) + _dot_nt(d_pib[:, sl], wi_ref[h])
            dwr_ref[h] += _dot_tn(xcb[:, sl], d_prb[:, sl])
            dwi_ref[h] += _dot_tn(xcb[:, sl], d_pib[:, sl])
        d_xc = h_s[...]
        vec_ref[4:5, :] += jnp.sum(d_xc, axis=0, keepdims=True)
        vec_ref[0:1, :] += jnp.sum(d_xc * xa, axis=0, keepdims=True)
        d_xa = cw[0:1, :] * d_xc
        nxt = dxc_s[...]
        for k in range(1, CONV_K):
            ahead = _rows_advanced(d_xc, nxt, k)
            vec_ref[k:k + 1, :] += jnp.sum(ahead * xa, axis=0, keepdims=True)
            d_xa = d_xa + cw[k:k + 1, :] * ahead
        dz_ref[:, 0:D] = d_xa.astype(BF16)
        dxc_s[...] = d_xc[0:8, :]

    vec = pl.BlockSpec((1, D), lambda i: (0, 0))
    gate = pl.BlockSpec((HEADS, HEAD_DIM, HEAD_DIM), lambda i: (0, 0, 0))
    cur = lambda c: pl.BlockSpec((ta, D), lambda i: (nb - 1 - i, c))
    before = lambda c: pl.BlockSpec((16, D), lambda i: (jnp.maximum((nb - 1 - i) * per16 - 1, 0), c))
    return _call(
        body, name="branch_a_bwd", grid=(nb,),
        in_specs=[ANY, cur(0), cur(0), cur(1), cur(0), before(0), cur(0), cur(0), cur(0),
                  pl.BlockSpec((CONV_K, D), lambda i: (0, 0)), gate, gate, vec],
        out_specs=[pl.BlockSpec((ta, 2 * D), lambda i: (nb - 1 - i, 0)), pl.BlockSpec((8, D), lambda i: (0, 0)),
                   gate, gate],
        out_shape=[SDS(dz.shape, BF16), SDS((8, D), F32), SDS((HEADS, HEAD_DIM, HEAD_DIM), F32),
                   SDS((HEADS, HEAD_DIM, HEAD_DIM), F32)],
        scratch_shapes=[pltpu.VMEM((ta, D), F32)] * 3 + [pltpu.VMEM((8, D), F32)] * 3, aliases={0: 0},
        params=_cparams(("arbitrary",), 48),
        args=(dz, dya, z, z, hs, hs, xc, r, ig, conv_w, w_r, w_i, lam), comm=comm)


def _in_bwd(dz, w_in_g, x, dh1, g_mix, comm=None):
    s = x.shape[0]
    tm = min(TM_ROWS, s)
    nj = N_SLOT

    def body(dz_ref, w_ref, x_ref, dh1_ref, g_ref, dx_ref, dg_ref, acc_s):
        i, j = pl.program_id(0), pl.program_id(1)

        @pl.when(j == 0)
        def _():
            acc_s[...] = jnp.zeros_like(acc_s)

        @pl.when((i == 0) & (j == 0))
        def _():
            dg_ref[...] = jnp.zeros_like(dg_ref)

        acc_s[...] += _dot_nt(dz_ref[...], w_ref[...])

        @pl.when(j == nj - 1)
        def _():
            xv = x_ref[...]
            rstd = lax.rsqrt(jnp.mean(xv * xv, axis=-1, keepdims=True) + NORM_EPS)
            xh = xv * rstd
            dn = acc_s[...]
            dg_ref[...] += jnp.sum(dn * xh, axis=0, keepdims=True)
            dhat = dn * g_ref[...]
            dx_ref[...] = dh1_ref[...] + rstd * (dhat - xh * jnp.mean(dhat * xh, axis=-1, keepdims=True))

    row = pl.BlockSpec((tm, D), lambda i, j: (i, 0))
    vec = pl.BlockSpec((1, D), lambda i, j: (0, 0))
    return _call(
        body, name="in_bwd", grid=(s // tm, nj),
        in_specs=[pl.BlockSpec((tm, W_IN_COLS), lambda i, j: (i, j)),
                  pl.BlockSpec((None, D, W_IN_COLS), lambda i, j: (j, 0, 0)), row, row, vec],
        out_specs=[row, vec],
        out_shape=[SDS((s, D), F32), SDS((1, D), F32)],
        scratch_shapes=[pltpu.VMEM((tm, D), F32)],
        params=_cparams(("arbitrary", "arbitrary"), 48), args=(dz, w_in_g, x, dh1, g_mix), comm=comm)


def _wgrad(name, a, b):
    s = a.shape[0]
    ts = min(TM_ROWS, s)
    a_w, b_w = a.shape[1], b.shape[1]

    def body(a_ref, b_ref, o_ref, acc_s):
        t = pl.program_id(0)

        @pl.when(t == 0)
        def _():
            acc_s[...] = jnp.zeros_like(acc_s)

        acc_s[...] += _dot_tn(a_ref[...].astype(BF16), b_ref[...].astype(BF16))

        @pl.when(t == pl.num_programs(0) - 1)
        def _():
            o_ref[...] = acc_s[...].astype(BF16)

    return pl.pallas_call(
        body, name=name, grid=(s // ts,),
        in_specs=[pl.BlockSpec((ts, a_w), lambda t: (t, 0)), pl.BlockSpec((ts, b_w), lambda t: (t, 0))],
        out_specs=pl.BlockSpec((a_w, b_w), lambda t: (0, 0)),
        out_shape=SDS((a_w, b_w), BF16),
        scratch_shapes=[pltpu.VMEM((a_w, b_w), F32)],
        compiler_params=_cparams(("arbitrary",), 48),
    )(a, b)


def _place():
    x, y, c = lax.axis_index("x"), lax.axis_index("y"), lax.axis_index("c")
    return x, y, c


def _other_chips(x, y):
    return [(x, 1 - y, 2 * x + 1 - y), (1 - x, y, 2 * (1 - x) + y), (1 - x, 1 - y, 2 * (1 - x) + 1 - y)]


class _Plan:
    def __init__(self, arrays, out_shape, sems, start, finish, middle=None, middle_at=6):
        self.arrays, self.out_shape, self.sems, self.start, self.finish = arrays, out_shape, sems, start, finish
        self.middle, self.middle_at = middle, middle_at


def _gather_plan(shards, middle_at=6):
    n = len(shards)

    def copies(ins, outs, sems):
        send_sems, recv_sems, local_sems = sems
        x, y, c = _place()
        chip = 2 * x + y
        me = 2 * chip + c
        sib = (x, y, 1 - c)
        chips = _other_chips(x, y)

        def rc(k, t, src, blk, to):
            return pltpu.make_async_remote_copy(
                src_ref=src, dst_ref=outs[t].at[blk], send_sem=send_sems.at[k * n + t],
                recv_sem=recv_sems.at[k * n + t], device_id=to, device_id_type=MESH)

        (yx, yy, y_chip), (xx, xy, x_chip), _ = chips
        local = [pltpu.make_async_copy(ins[t], outs[t].at[me], local_sems.at[t]) for t in range(n)]
        sends = ([rc(0, t, ins[t], me, sib) for t in range(n)] + [rc(1, t, ins[t], me, (yx, yy, c)) for t in range(n)]
                 + [rc(2, t, ins[t], me, (xx, xy, c)) for t in range(n)])
        passed = [[rc(4 + j, t, outs[t].at[2 * pc + c], 2 * pc + c, sib) for t in range(n)]
                  for j, (_, _, pc) in enumerate(chips)]
        relays = [[rc(3, t, outs[t].at[2 * y_chip + c], 2 * y_chip + c, (xx, xy, c)) for t in range(n)],
                  [rc(3, t, outs[t].at[2 * x_chip + c], 2 * x_chip + c, (yx, yy, c)) for t in range(n)]]
        return rc, local, sends, passed, relays, chips, chip, c, sib

    def start(ins, outs, sems):
        _, local, sends, _, _, _, _, _, _ = copies(ins, outs, sems)
        for cp in local + sends:
            cp.start()

    def middle(ins, outs, sems):
        rc, _, _, passed, relays, chips, _, c, sib = copies(ins, outs, sems)
        for j in range(2):
            for t in range(n):
                rc(1 + j, t, ins[t], 2 * chips[j][2] + c, sib).wait_recv()
        for j in range(2):
            for cp in passed[j]:
                cp.start()

            @pl.when(c == j)
            def _():
                for cp in relays[j]:
                    cp.start()

    def finish(ins, outs, sems):
        rc, local, sends, passed, relays, chips, chip, c, sib = copies(ins, outs, sems)
        far = 2 * chips[2][2] + c
        for t in range(n):
            rc(3, t, ins[t], far, sib).wait_recv()
        for cp in passed[2]:
            cp.start()
        for t in range(n):
            rc(0, t, ins[t], 2 * chip + 1 - c, sib).wait_recv()
        for j, (px, py, pc) in enumerate(chips):
            for t in range(n):
                rc(4 + j, t, ins[t], 2 * pc + 1 - c, sib).wait_recv()
        for cp in sends + passed[0] + passed[1] + passed[2]:
            cp.wait_send()
        for j in range(2):
            @pl.when(c == j)
            def _():
                for cp in relays[j]:
                    cp.wait_send()
        for cp in local:
            cp.wait()

    return _Plan(list(shards), [SDS((N_SLOT,) + tuple(a.shape), a.dtype) for a in shards],
                 [pltpu.SemaphoreType.DMA((7 * n,)), pltpu.SemaphoreType.DMA((7 * n,)),
                  pltpu.SemaphoreType.DMA((n,))], start, finish, middle, middle_at)


def _sibling_plan(grads, whole=()):
    n, m = len(grads), len(whole)

    def copies(ins, outs, sems):
        send_sems, recv_sems = sems
        x, y, c = _place()
        sib = (x, y, 1 - c)

        def rc(t, src, dst):
            return pltpu.make_async_remote_copy(src_ref=src, dst_ref=dst, send_sem=send_sems.at[t],
                                                recv_sem=recv_sems.at[t], device_id=sib, device_id_type=MESH)
        return rc, c

    def start(ins, outs, sems):
        rc, c = copies(ins, outs, sems)
        for t in range(n):
            for j in range(4):
                rc(t, ins[t].at[2 * j + 1 - c], outs[t].at[j]).start()
        for t in range(n, n + m):
            rc(t, ins[t], outs[t]).start()

    def finish(ins, outs, sems):
        rc, _ = copies(ins, outs, sems)
        for t in range(n):
            rc(t, ins[t].at[pl.ds(0, 4)], outs[t]).wait()
        for t in range(n, n + m):
            rc(t, ins[t], outs[t]).wait()

    return _Plan(list(grads) + list(whole),
                 [SDS((4,) + tuple(g.shape[1:]), g.dtype) for g in grads] + [SDS(a.shape, a.dtype) for a in whole],
                 [pltpu.SemaphoreType.DMA((n + m,)), pltpu.SemaphoreType.DMA((n + m,))], start, finish)


def _chips_plan(parts, whole=()):
    n, m = len(parts), len(whole)

    def src_of(ins, t, pc):
        return ins[t].at[pc] if t < n else ins[t]

    def local_copies(ins, outs, sems, chip):
        return [pltpu.make_async_copy(src_of(ins, t, chip), outs[t].at[chip], sems[2].at[t]) for t in range(n + m)]

    def start(ins, outs, sems):
        send_sems, recv_sems, _ = sems
        x, y, c = _place()
        chip = 2 * x + y
        for cp in local_copies(ins, outs, sems, chip):
            cp.start()
        for px, py, pc in _other_chips(x, y):
            for t in range(n + m):
                pltpu.make_async_remote_copy(src_ref=src_of(ins, t, pc), dst_ref=outs[t].at[chip],
                                             send_sem=send_sems.at[t], recv_sem=recv_sems.at[t],
                                             device_id=(px, py, c), device_id_type=MESH).start()

    def finish(ins, outs, sems):
        send_sems, recv_sems, _ = sems
        x, y, c = _place()
        for t in range(n + m):
            three = outs[t].at[pl.ds(0, 3)]
            pltpu.make_async_remote_copy(src_ref=three, dst_ref=three, send_sem=send_sems.at[t],
                                         recv_sem=recv_sems.at[t], device_id=(x, y, c), device_id_type=MESH).wait()
        for cp in local_copies(ins, outs, sems, 2 * x + y):
            cp.wait()

    return _Plan(list(parts) + list(whole),
                 [SDS(p.shape, p.dtype) for p in parts] + [SDS((4,) + tuple(a.shape), a.dtype) for a in whole],
                 [pltpu.SemaphoreType.DMA((n + m,)), pltpu.SemaphoreType.DMA((n + m,)),
                  pltpu.SemaphoreType.DMA((n + m,))], start, finish)


def _exchange_plan(arr):
    def peers(x, y, c):
        flip = lambda v, f: 1 - v if f else v
        return [(flip(x, fx), flip(y, fy), flip(c, fc))
                for fx in (0, 1) for fy in (0, 1) for fc in (0, 1) if fx or fy or fc]

    def start(ins, outs, sems):
        x, y, c = _place()
        me = 4 * x + 2 * y + c
        pltpu.make_async_copy(ins[0], outs[0].at[me], sems[2].at[0]).start()
        for to in peers(x, y, c):
            pltpu.make_async_remote_copy(src_ref=ins[0], dst_ref=outs[0].at[me], send_sem=sems[0].at[0],
                                         recv_sem=sems[1].at[0], device_id=to, device_id_type=MESH).start()

    def finish(ins, outs, sems):
        x, y, c = _place()
        seven = outs[0].at[pl.ds(0, 7)]
        pltpu.make_async_remote_copy(src_ref=seven, dst_ref=seven, send_sem=sems[0].at[0], recv_sem=sems[1].at[0],
                                     device_id=(x, y, c), device_id_type=MESH).wait()
        pltpu.make_async_copy(ins[0], outs[0].at[4 * x + 2 * y + c], sems[2].at[0]).wait()

    return _Plan([arr], [SDS((N_SLOT,) + tuple(arr.shape), arr.dtype)],
                 [pltpu.SemaphoreType.DMA((1,)), pltpu.SemaphoreType.DMA((1,)), pltpu.SemaphoreType.DMA((1,))],
                 start, finish)


def _join(*plans):
    def cut(seq, sizes):
        out, at = [], 0
        for k in sizes:
            out.append(seq[at:at + k])
            at += k
        return out

    n_arr = [len(p.arrays) for p in plans]
    n_sem = [len(p.sems) for p in plans]

    def start(ins, outs, sems):
        for p, i, o, s in zip(plans, cut(ins, n_arr), cut(outs, n_arr), cut(sems, n_sem)):
            p.start(i, o, s)

    def finish(ins, outs, sems):
        for p, i, o, s in zip(plans, cut(ins, n_arr), cut(outs, n_arr), cut(sems, n_sem)):
            p.finish(i, o, s)

    def middle(ins, outs, sems):
        for p, i, o, s in zip(plans, cut(ins, n_arr), cut(outs, n_arr), cut(sems, n_sem)):
            if p.middle is not None:
                p.middle(i, o, s)

    return _Plan([a for p in plans for a in p.arrays], [o for p in plans for o in p.out_shape],
                 [s for p in plans for s in p.sems], start, finish,
                 middle if any(p.middle is not None for p in plans) else None,
                 max(p.middle_at for p in plans))


def _run_plan(name, plan):
    k = len(plan.arrays)

    def body(*refs):
        ins, outs, sems = refs[:k], refs[k:2 * k], refs[2 * k:]
        plan.start(ins, outs, sems)
        if plan.middle is not None:
            plan.middle(ins, outs, sems)
        plan.finish(ins, outs, sems)

    return pl.pallas_call(
        body, name=name, in_specs=[ANY] * k, out_specs=[ANY] * k, out_shape=plan.out_shape,
        scratch_shapes=plan.sems, compiler_params=pltpu.CompilerParams(has_side_effects=True),
    )(*plan.arrays)


def _call(body, *, name, grid, in_specs, out_specs, out_shape, scratch_shapes, params, args, comm=None,
          aliases=None, prefetch=()):
    aliases = aliases or {}
    n_pre = len(prefetch)

    def launch(fn, ins_specs, outs_specs, outs_shape, scratch, operands):
        spec = pltpu.PrefetchScalarGridSpec(num_scalar_prefetch=n_pre, grid=grid, in_specs=ins_specs,
                                            out_specs=outs_specs, scratch_shapes=scratch)
        return pl.pallas_call(fn, name=name, grid_spec=spec, out_shape=outs_shape, compiler_params=params,
                              input_output_aliases=aliases)(*prefetch, *[_small_in_hbm(a) for a in operands])

    if comm is None:
        return list(launch(body, in_specs, out_specs, out_shape, scratch_shapes, args)), []
    n_in, n_out, n_scr, k = len(in_specs), len(out_specs), len(scratch_shapes), len(comm.arrays)

    def wrapped(*refs):
        pre, refs = refs[:n_pre], refs[n_pre:]
        ins = refs[:n_in]
        c_in = refs[n_in:n_in + k]
        outs = refs[n_in + k:n_in + k + n_out]
        c_out = refs[n_in + k + n_out:n_in + 2 * k + n_out]
        scr = refs[n_in + 2 * k + n_out:n_in + 2 * k + n_out + n_scr]
        sems = refs[n_in + 2 * k + n_out + n_scr:]
        step, steps = pl.program_id(0), grid[0]
        for d in range(1, len(grid)):
            step, steps = step * grid[d] + pl.program_id(d), steps * grid[d]

        @pl.when(step == 0)
        def _():
            comm.start(c_in, c_out, sems)

        if comm.middle is not None:
            @pl.when(step == (comm.middle_at * steps) // 8)
            def _():
                comm.middle(c_in, c_out, sems)

        body(*pre, *ins, *outs, *scr)

        @pl.when(step == steps - 1)
        def _():
            comm.finish(c_in, c_out, sems)

    res = launch(wrapped, list(in_specs) + [ANY] * k, list(out_specs) + [ANY] * k,
                 list(out_shape) + list(comm.out_shape), list(scratch_shapes) + list(comm.sems),
                 tuple(args) + tuple(comm.arrays))
    return list(res[:n_out]), list(res[n_out:])


def _wgrad_paired(name, a_t, b, core, by_rows=False, comm=None):
    s = b.shape[0]
    m = a_t.shape[0] // N_SLOT if by_rows else a_t.shape[0]
    cols = b.shape[1] if by_rows else b.shape[1] // N_SLOT
    half = N_SLOT // 2

    def owner(k, c):
        return 2 * (k % half) + jnp.where(k < half, 1 - c, c)

    def body(c_ref, a_ref, b_ref, sum_ref, out_s, recv_s, send_sems, recv_sems):
        del c_ref
        k = pl.program_id(0)
        x, y, c = _place()

        def to_sibling(j):
            return pltpu.make_async_remote_copy(src_ref=out_s.at[j % 2], dst_ref=recv_s.at[j],
                                                send_sem=send_sems.at[j], recv_sem=recv_sems.at[j],
                                                device_id=(x, y, 1 - c), device_id_type=MESH)

        @pl.when((k >= 2) & (k < half + 2))
        def _():
            to_sibling(k - 2).wait_send()

        @pl.when(k < half)
        def _():
            out_s[k % 2] = _dot(a_ref[...], b_ref[...]).astype(BF16)
            to_sibling(k).start()

        @pl.when(k >= half)
        def _():
            to_sibling(k - half).wait_recv()
            sum_ref[...] = (_dot(a_ref[...], b_ref[...]) + recv_s[k - half].astype(F32)).astype(BF16)

    if by_rows:
        in_specs = [pl.BlockSpec((m, s), lambda k, c_ref: (owner(k, c_ref[0]), 0)),
                    pl.BlockSpec((s, cols), lambda k, c_ref: (0, 0))]
    else:
        in_specs = [pl.BlockSpec((m, s), lambda k, c_ref: (0, 0)),
                    pl.BlockSpec((s, cols), lambda k, c_ref: (0, owner(k, c_ref[0])))]
    return _call(body, name=name, grid=(N_SLOT,), in_specs=in_specs,
                 out_specs=[pl.BlockSpec((None, m, cols), lambda k, c_ref: (jnp.maximum(k - half, 0), 0, 0))],
                 out_shape=[SDS((half, m, cols), BF16)],
                 scratch_shapes=[pltpu.VMEM((2, m, cols), BF16), pltpu.VMEM((half, m, cols), BF16),
                                 pltpu.SemaphoreType.DMA((half,)), pltpu.SemaphoreType.DMA((half,))],
                 params=_cparams(("arbitrary",), 56), args=(a_t, b), comm=comm, prefetch=(core,))


def _row_tile(rows):
    for t in (512, 256, 128, 64, 32, 16, 8):
        if rows % t == 0:
            return t
    return rows


def _pair_sum(name, g8, recv4, core):
    _, rows, cols = recv4.shape
    tr = _row_tile(rows)
    g42 = g8.reshape(4, 2, rows, cols)

    def body(c_ref, g_ref, r_ref, o_ref):
        del c_ref
        o_ref[...] = (g_ref[...].astype(F32) + r_ref[...].astype(F32)).astype(o_ref.dtype)

    return pl.pallas_call(
        body, name=name,
        grid_spec=pltpu.PrefetchScalarGridSpec(
            num_scalar_prefetch=1, grid=(4, rows // tr),
            in_specs=[pl.BlockSpec((None, None, tr, cols), lambda j, i, c_ref: (j, c_ref[0], i, 0)),
                      pl.BlockSpec((None, tr, cols), lambda j, i, c_ref: (j, i, 0))],
            out_specs=pl.BlockSpec((None, tr, cols), lambda j, i, c_ref: (j, i, 0))),
        out_shape=SDS(recv4.shape, g8.dtype),
        compiler_params=_cparams(("arbitrary", "arbitrary"), 32),
    )(core, g42, recv4)


def _add2(name, a, b):
    rows, cols = a.shape
    tr = _row_tile(rows)

    def body(a_ref, b_ref, o_ref):
        o_ref[...] = a_ref[...] + b_ref[...]

    blk = pl.BlockSpec((tr, cols), lambda i: (i, 0))
    return pl.pallas_call(body, name=name, grid=(rows // tr,), in_specs=[blk, blk], out_specs=blk,
                          out_shape=SDS(a.shape, a.dtype),
                          compiler_params=_cparams(("arbitrary",), 32))(a, b)


def _sum_terms(name, terms):
    k, rows, cols = terms.shape
    tr = _row_tile(rows)

    def body(r_ref, o_ref):
        acc = r_ref[0]
        for q in range(1, k):
            acc = acc + r_ref[q]
        o_ref[...] = acc

    return pl.pallas_call(body, name=name, grid=(rows // tr,),
                          in_specs=[pl.BlockSpec((k, tr, cols), lambda i: (0, i, 0))],
                          out_specs=pl.BlockSpec((tr, cols), lambda i: (i, 0)),
                          out_shape=SDS((rows, cols), terms.dtype),
                          compiler_params=_cparams(("arbitrary",), 32))(terms)


def _adam_update(g, w, m, v):
    c1 = 1.0 / (1.0 - ADAM_B1 ** ADAM_STEP)
    c2 = 1.0 / (1.0 - ADAM_B2 ** ADAM_STEP)
    mn = ADAM_B1 * m + (1.0 - ADAM_B1) * g
    vn = ADAM_B2 * v + (1.0 - ADAM_B2) * (g * g)
    delta = (-ADAM_LR) * ((mn * c1) / (jnp.sqrt(vn * c2) + ADAM_EPS) + ADAM_WD * w)
    return delta, mn, vn


def _adamw_many(name, gs, ws, ms, vs):
    n = len(gs)

    def body(*refs):
        for p in range(n):
            g, w, m, v = (refs[q * n + p][...] for q in range(4))
            d, mn, vn = _adam_update(g, w, m, v)
            refs[4 * n + p][...] = d
            refs[5 * n + p][...] = mn
            refs[6 * n + p][...] = vn

    full = [pl.BlockSpec(w.shape, lambda i: (0, 0)) for w in ws]
    shapes = [SDS(w.shape, F32) for w in ws]
    res = pl.pallas_call(body, name=name, grid=(1,), in_specs=full * 4, out_specs=full * 3, out_shape=shapes * 3,
                         compiler_params=_cparams(("arbitrary",), 32),
                         )(*[_small_in_hbm(a) for a in (*gs, *ws, *ms, *vs)])
    return [(res[p], res[n + p], res[2 * n + p]) for p in range(n)]


def _adamw(name, terms, w, m, v):
    k, rows, cols = terms.shape
    tr = _row_tile(rows)

    def body(t_ref, w_ref, m_ref, v_ref, g_ref, d_ref, mo_ref, vo_ref):
        g = t_ref[0].astype(F32)
        for q in range(1, k):
            g = g + t_ref[q].astype(F32)
        g_ref[...] = g
        d_ref[...], mo_ref[...], vo_ref[...] = _adam_update(g, w_ref[...], m_ref[...], v_ref[...])

    blk = pl.BlockSpec((tr, cols), lambda i: (i, 0))
    return pl.pallas_call(body, name=name, grid=(rows // tr,),
                          in_specs=[pl.BlockSpec((k, tr, cols), lambda i: (0, i, 0)), blk, blk, blk],
                          out_specs=[blk] * 4, out_shape=[SDS((rows, cols), F32)] * 4,
                          compiler_params=_cparams(("arbitrary",), 40),
                          )(*[pltpu.with_memory_space_constraint(a, pltpu.HBM) for a in (terms, w, m, v)])


def kernel(x, norm_mix_g, w_in, conv_w, conv_b, w_rgate, b_rgate, w_igate, b_igate, lru_lambda, w_out_a, sgu_ln_g, sgu_ln_b, sgu_w_s, sgu_b_s, w_out_b, w_out, norm_mlp_g, w_up, w_down, norm_final_g, loss_target, m_norm_mix_g, m_w_in, m_conv_w, m_conv_b, m_w_rgate, m_b_rgate, m_w_igate, m_b_igate, m_lru_lambda, m_w_out_a, m_sgu_ln_g, m_sgu_ln_b, m_sgu_w_s, m_sgu_b_s, m_w_out_b, m_w_out, m_norm_mlp_g, m_w_up, m_w_down, m_norm_final_g, v_norm_mix_g, v_w_in, v_conv_w, v_conv_b, v_w_rgate, v_b_rgate, v_w_igate, v_b_igate, v_lru_lambda, v_w_out_a, v_sgu_ln_g, v_sgu_ln_b, v_sgu_w_s, v_sgu_b_s, v_w_out_b, v_w_out, v_norm_mlp_g, v_w_up, v_w_down, v_norm_final_g):
    cx, cy, cc = _place()
    me = 4 * cx + 2 * cy + cc
    core = jnp.reshape(cc, (1,)).astype(jnp.int32)
    xs = x[0]
    tgt = loss_target[0]

    gate_shard = jnp.stack([w_rgate[0], w_igate[0]]).astype(BF16).reshape(2 * HEADS * 32, HEAD_DIM)
    vec_shard = jnp.concatenate([conv_w[0], b_rgate[0], b_igate[0]], axis=1)
    vec_shard = jnp.pad(vec_shard, ((0, 4), (0, 256 - vec_shard.shape[1])))
    shards = [w_in[0].astype(BF16), w_out_a[0].astype(BF16), w_out_b[0].astype(BF16), w_out[0].astype(BF16),
              w_up[0].astype(BF16), w_down[0].astype(BF16), gate_shard, vec_shard]
    (z, n1_t, w_in_g), (gate_g, vec_g) = _in_proj(xs, norm_mix_g, shards[0], _slot_order(cx, cy, cc),
                                                comm=_gather_plan(shards[6:8]))
    gates = gate_g.reshape(N_SLOT, 2, HEADS, 32, HEAD_DIM).transpose(1, 2, 0, 3, 4).reshape(2, HEADS, HEAD_DIM, HEAD_DIM)
    w_r_f, w_i_f = gates[0], gates[1]
    conv_w_f = vec_g[:, 0:4, 0:128].transpose(1, 0, 2).reshape(CONV_K, D)
    b_r_f = vec_g[:, 0:4, 128:160].transpose(1, 0, 2).reshape(1, D)
    b_i_f = vec_g[:, 0:4, 160:192].transpose(1, 0, 2).reshape(1, D)
    b_s_t = jnp.transpose(sgu_b_s[0])

    (ya, hs, xc, r_gate, i_gate), (w_oa_g, w_ob_g, w_out_g, w_up_g) = _branch_a_fwd(
        z, conv_w_f, conv_b, w_r_f, b_r_f, w_i_f, b_i_f, lru_lambda, comm=_gather_plan(shards[1:5], middle_at=5))
    w_oa_f = w_oa_g.reshape(D, D)
    w_ob_f = w_ob_g.reshape(D, D)
    w_out_f = w_out_g.reshape(D, D)
    (yb, pa, pb, merged, h1), (w_down_g,) = _branch_b_merge_out(
        ya, z, xs, sgu_ln_g, sgu_ln_b, sgu_w_s[0], b_s_t, w_oa_f, w_ob_f, w_out_f,
        comm=_gather_plan(shards[5:6], middle_at=4))
    w_down_f = w_down_g.reshape(N_SLOT * FF_COLS, D)
    r_act, act_t, n2_t, dh2, dh2b, loss_acc, d_gfin = _mlp_fwd(h1, norm_mlp_g, w_up_g, w_down_f,
                                                               norm_final_g.reshape(1, D), tgt)

    def pair(names, grads, recv):
        return [_pair_sum("pair_sum_" + nm, g, r, core) for nm, g, r in zip(names, grads, recv)]

    (p_down,), _ = _wgrad_paired("wgrad_down", act_t, dh2b, core, by_rows=True)
    (df, dh1, d_gmlp), (got_down,) = _mlp_bwd(dh2, dh2b, r_act, w_down_f, w_up_g, h1, norm_mlp_g,
                                              comm=_chips_plan([p_down]))
    (p_up,), _ = _wgrad_paired("wgrad_up", n2_t, df, core)
    g_out = _wgrad("wgrad_out", merged, dh1).reshape(N_SLOT, D // N_SLOT, D)
    (dz, dpa, dpb, dya, dyb), (r_out,) = _merge_bwd(
        dh1, z, pa, pb, w_out_f, w_oa_f, w_ob_f, comm=_sibling_plan([g_out]))
    (p_out,) = pair(["out"], [g_out], [r_out])
    g_oa = _wgrad("wgrad_out_a", ya, dpa).reshape(N_SLOT, D // N_SLOT, D)
    g_ob = _wgrad("wgrad_out_b", yb, dpb).reshape(N_SLOT, D // N_SLOT, D)
    (dz, d_ws, d_bs, d_ln), (got_out, r_oa, r_ob) = _branch_b_bwd(
        dz, dyb, z, sgu_ln_g, sgu_ln_b, sgu_w_s[0], b_s_t,
        comm=_join(_chips_plan([p_out]), _sibling_plan([g_oa, g_ob])))
    p_oa, p_ob = pair(["out_a", "out_b"], [g_oa, g_ob], [r_oa, r_ob])
    (dz, d_vec, d_wr, d_wi), (got_up, got_oa, got_ob) = _branch_a_bwd(
        dz, dya, z, hs, xc, r_gate, i_gate, conv_w_f, w_r_f, w_i_f, lru_lambda,
        comm=_chips_plan([p_up, p_oa, p_ob]))
    g_gate = jnp.stack([d_wr, d_wi]).reshape(2, HEADS, N_SLOT, 32, HEAD_DIM).transpose(2, 0, 1, 3, 4)
    g_gate = g_gate.reshape(N_SLOT, 2 * HEADS * 32, HEAD_DIM).astype(BF16)

    d_bs_row = jnp.pad(d_bs[:, :, 0].reshape(1, GROUPS * CHUNK), ((0, 0), (0, D - GROUPS * CHUNK)))
    vecs = jnp.concatenate([d_vec, jnp.concatenate([d_ln[0:2], d_gmlp, d_gfin, d_bs_row, jnp.zeros((3, D), F32)])])
    d_ws2 = d_ws.reshape(GROUPS * CHUNK, CHUNK)
    (p_in,), (r_gate, r_vecs, r_ws) = _wgrad_paired("wgrad_in", n1_t, dz, core,
                                                    comm=_sibling_plan([g_gate], [vecs, d_ws2]))
    (p_gate,) = pair(["gate"], [g_gate], [r_gate])
    vecs_chip = _add2("pair_sum_vecs", vecs, r_vecs)
    ws_chip = _add2("pair_sum_ws", d_ws2, r_ws)
    (dx, d_gmix), (got_in, got_gate, got_vecs, got_ws) = _in_bwd(
        dz, w_in_g, xs, dh1, norm_mix_g, comm=_chips_plan([p_in, p_gate], [vecs_chip, ws_chip]))
    vecs_sum = _sum_terms("sum_vecs", got_vecs)
    last = jnp.concatenate([d_gmix, jnp.pad(loss_acc[0:1], ((0, 0), (0, D - 128))), jnp.zeros((6, D), F32)])
    (last_all,) = _run_plan("exchange_last", _exchange_plan(last))
    last_sum = _sum_terms("sum_last", last_all)
    loss = last_sum[1, 0]
    got = [got_in, got_oa, got_ob, got_out, got_up, got_down, got_gate]

    def step(nm, terms, w, m, v, rows, cols):
        g, d, mn, vn = _adamw("adamw_" + nm, terms.reshape(4, rows, cols), w.reshape(rows, cols),
                              m.reshape(rows, cols), v.reshape(rows, cols))
        return [a.reshape(w.shape) for a in (g, d, mn, vn)]

    o_in = step("in", got[0], w_in, m_w_in, v_w_in, D, W_IN_COLS)
    o_oa = step("out_a", got[1], w_out_a, m_w_out_a, v_w_out_a, D // N_SLOT, D)
    o_ob = step("out_b", got[2], w_out_b, m_w_out_b, v_w_out_b, D // N_SLOT, D)
    o_out = step("out", got[3], w_out, m_w_out, v_w_out, D // N_SLOT, D)
    o_up = step("up", got[4], w_up, m_w_up, v_w_up, D, FF_COLS)
    o_down = step("down", got[5], w_down, m_w_down, v_w_down, FF_COLS, D)
    gate_w = jnp.stack([w_rgate[0], w_igate[0]]).reshape(2 * HEADS * 32, HEAD_DIM)
    gate_m = jnp.stack([m_w_rgate[0], m_w_igate[0]]).reshape(2 * HEADS * 32, HEAD_DIM)
    gate_v = jnp.stack([v_w_rgate[0], v_w_igate[0]]).reshape(2 * HEADS * 32, HEAD_DIM)
    o_gate = _adamw("adamw_gate", got[6], gate_w, gate_m, gate_v)
    o_gate = [a.reshape(2, 1, HEADS, 32, HEAD_DIM) for a in o_gate]
    o_wr = [a[0] for a in o_gate]
    o_wi = [a[1] for a in o_gate]

    def own(full, width):
        return lax.dynamic_slice_in_dim(full, me * width, width, axis=1)

    small_g = {
        "norm_mix_g": last_sum[0:1], "conv_w": own(vecs_sum[0:4], 128), "conv_b": vecs_sum[4:5],
        "b_rgate": own(vecs_sum[5:6].reshape(HEADS, HEAD_DIM), 32),
        "b_igate": own(vecs_sum[6:7].reshape(HEADS, HEAD_DIM), 32),
        "lru_lambda": vecs_sum[7:8], "sgu_ln_g": vecs_sum[8:9], "sgu_ln_b": vecs_sum[9:10],
        "norm_mlp_g": vecs_sum[10:11], "norm_final_g": vecs_sum[11:12],
        "sgu_b_s": vecs_sum[12, 0:GROUPS * CHUNK].reshape(GROUPS, CHUNK),
    }
    small_w = {"norm_mix_g": (norm_mix_g, m_norm_mix_g, v_norm_mix_g), "conv_w": (conv_w, m_conv_w, v_conv_w),
               "conv_b": (conv_b, m_conv_b, v_conv_b), "b_rgate": (b_rgate, m_b_rgate, v_b_rgate),
               "b_igate": (b_igate, m_b_igate, v_b_igate), "lru_lambda": (lru_lambda, m_lru_lambda, v_lru_lambda),
               "sgu_ln_g": (sgu_ln_g, m_sgu_ln_g, v_sgu_ln_g), "sgu_ln_b": (sgu_ln_b, m_sgu_ln_b, v_sgu_ln_b),
               "norm_mlp_g": (norm_mlp_g, m_norm_mlp_g, v_norm_mlp_g),
               "norm_final_g": (norm_final_g, m_norm_final_g, v_norm_final_g),
               "sgu_b_s": (sgu_b_s, m_sgu_b_s, v_sgu_b_s), "sgu_w_s": (sgu_w_s, m_sgu_w_s, v_sgu_w_s)}
    order = list(small_g)
    as2d = lambda k, a: a.reshape(small_g[k].shape)
    upd = _adamw_many("adamw_small", [small_g[k] for k in order], *[[as2d(k, small_w[k][q]) for k in order]
                                                                     for q in range(3)])
    o_small = {k: [a.reshape(small_w[k][0].shape) for a in (small_g[k],) + u] for k, u in zip(order, upd)}
    ws3 = [a[0].reshape(GROUPS * CHUNK, CHUNK) for a in small_w.pop("sgu_w_s")]
    o_small["sgu_w_s"] = [a.reshape(sgu_w_s.shape) for a in _adamw("adamw_ws", got_ws, *ws3)]

    per_weight = {"norm_mix_g": o_small["norm_mix_g"], "w_in": o_in, "conv_w": o_small["conv_w"],
                  "conv_b": o_small["conv_b"], "w_rgate": o_wr, "b_rgate": o_small["b_rgate"], "w_igate": o_wi,
                  "b_igate": o_small["b_igate"], "lru_lambda": o_small["lru_lambda"], "w_out_a": o_oa,
                  "sgu_ln_g": o_small["sgu_ln_g"], "sgu_ln_b": o_small["sgu_ln_b"], "sgu_w_s": o_small["sgu_w_s"],
                  "sgu_b_s": o_small["sgu_b_s"], "w_out_b": o_ob, "w_out": o_out, "norm_mlp_g": o_small["norm_mlp_g"],
                  "w_up": o_up, "w_down": o_down, "norm_final_g": o_small["norm_final_g"]}
    names_w = list(per_weight)
    return (loss, dx[None], *[per_weight[k][0] for k in names_w], *[per_weight[k][1] for k in names_w],
            *[per_weight[k][2] for k in names_w], *[per_weight[k][3] for k in names_w])
```

```python
import jax
import jax.numpy as jnp
from jax import lax
from jax.experimental import pallas as pl
from jax.experimental.pallas import tpu as pltpu

F32 = jnp.float32
BF16 = jnp.bfloat16
SDS = jax.ShapeDtypeStruct
MESH = pl.DeviceIdType.MESH
ANY = pl.BlockSpec(memory_space=pl.ANY)

D = 1024
N_SLOT = 8
W_IN_COLS = 768
FF_COLS = 512
HEADS, HEAD_DIM = 4, 256
GROUPS, GROUP_DIM = 4, 256
CHUNK = 128
CONV_K = 4
NORM_EPS = 1e-6
LN_EPS = 1e-5
LRU_C = 8.0
ADAM_LR, ADAM_B1, ADAM_B2, ADAM_EPS, ADAM_WD, ADAM_STEP = 0.001, 0.9, 0.999, 1e-08, 0.01, 10

TM_ROWS = 1024
TM_MERGE = 512
T_BRANCH_A = 512
T_BRANCH_B = 256
MiB = 1024 * 1024
SMALL_OPERAND = 16 * 1024

_GELU_C = 0.7978845608028654
_GELU_A = 0.044715


def _small_in_hbm(a):
    return pltpu.with_memory_space_constraint(a, pltpu.HBM) if a.size <= SMALL_OPERAND else a


def _cparams(sem, vmem_mib):
    return pltpu.CompilerParams(dimension_semantics=sem, vmem_limit_bytes=vmem_mib * MiB)


def _gelu(x):
    t = jnp.tanh(_GELU_C * (x + _GELU_A * x * x * x))
    return 0.5 * x * (1.0 + t)


def _gelu_and_grad(x):
    x2 = x * x
    t = jnp.tanh(_GELU_C * x * (1.0 + _GELU_A * x2))
    g = 0.5 * x * (1.0 + t)
    dg = 0.5 * (1.0 + t) + 0.5 * x * (1.0 - t * t) * _GELU_C * (1.0 + 3.0 * _GELU_A * x2)
    return g, dg


def _softplus(x):
    return jnp.maximum(x, 0.0) + jnp.log1p(jnp.exp(-jnp.abs(x)))


def _dot(a, b):
    return jnp.dot(a, b, preferred_element_type=F32)


def _dot_nt(a, b):
    return lax.dot_general(a, b, (((1,), (1,)), ((), ())), preferred_element_type=F32)


def _dot_tn(a, b):
    return lax.dot_general(a, b, (((0,), (0,)), ((), ())), preferred_element_type=F32)


def _rows_shifted(prev8, cur, k):
    ext = jnp.concatenate([prev8, cur], axis=0)
    return pltpu.roll(ext, k, 0)[8:]


def _rows_advanced(cur, next8, k):
    t = cur.shape[0]
    ext = jnp.concatenate([cur, next8], axis=0)
    return pltpu.roll(ext, t + 8 - k, 0)[:t]


def _first_second(x, y, c):
    ny, nx, far = _other_chips(x, y)
    pick = lambda a, b: a * (1 - c) + b * c
    first = tuple(pick(a, b) for a, b in zip(ny, nx))
    second = tuple(pick(b, a) for a, b in zip(ny, nx))
    return first, second, far


def _slot_order(x, y, c):
    chip = 2 * x + y
    first, second, far = _first_second(x, y, c)
    order = [2 * chip + c, 2 * chip + 1 - c, 2 * first[2] + c, 2 * second[2] + 1 - c, 2 * second[2] + c,
             2 * first[2] + 1 - c, 2 * far[2] + c, 2 * far[2] + 1 - c]
    return jnp.stack(order).astype(jnp.int32)


def _in_proj(x, g_mix, w_in_own, order, comm=None):
    s = x.shape[0]
    tm = min(TM_ROWS, s)
    ni = s // tm

    def body(order_ref, x_ref, g_ref, own_ref, z_ref, nt_ref, wg_ref, n_s, w_s, send_sems, recv_sems, local_sems):
        j, i = pl.program_id(0), pl.program_id(1)
        px, py, c = _place()
        chip = 2 * px + py
        me = 2 * chip + c
        sib = (px, py, 1 - c)
        chips = _other_chips(px, py)

        def rc(k, src, blk, to):
            return pltpu.make_async_remote_copy(src_ref=src, dst_ref=w_s.at[blk], send_sem=send_sems.at[k],
                                                recv_sem=recv_sems.at[k], device_id=to, device_id_type=MESH)

        del chips
        first, second, far = _first_second(px, py, c)
        blocks = [2 * first[2] + c, 2 * second[2] + c, 2 * far[2] + c]
        own_in = pltpu.make_async_copy(own_ref, w_s.at[me], local_sems.at[0])
        to_first = rc(1, own_ref, me, (first[0], first[1], c))
        to_second = rc(2, own_ref, me, (second[0], second[1], c))
        relay = rc(3, w_s.at[blocks[0]], blocks[0], (second[0], second[1], c))
        sends = [rc(0, own_ref, me, sib), to_first, to_second, relay]
        passed = [rc(4 + q, w_s.at[blk], blk, sib) for q, blk in enumerate(blocks)]
        keep = pltpu.make_async_copy(w_s, wg_ref, local_sems.at[1])

        @pl.when((i == 0) & (j == 0))
        def _():
            own_in.start()
            sends[0].start()
            to_first.start()
            own_in.wait()

        @pl.when((i == 0) & (j == 1))
        def _():
            rc(0, own_ref, 2 * chip + 1 - c, sib).wait_recv()

        for q, blk in enumerate(blocks):
            @pl.when((i == 0) & (j == 2 + 2 * q))
            def _():
                rc(1 + q, own_ref, blk, sib).wait_recv()
                passed[q].start()
                if q == 0:
                    to_second.start()
                    relay.start()

            @pl.when((i == 0) & (j == 3 + 2 * q))
            def _():
                rc(4 + q, own_ref, order_ref[j], sib).wait_recv()

        rows = pl.ds(pl.multiple_of(i * tm, tm), tm)

        @pl.when(j == 0)
        def _():
            xv = x_ref[...]
            rstd = lax.rsqrt(jnp.mean(xv * xv, axis=-1, keepdims=True) + NORM_EPS)
            nb = (xv * rstd * g_ref[...]).astype(BF16)
            n_s[rows, :] = nb
            nt_ref[...] = nb.T

        z_ref[...] = _dot(n_s[rows, :], w_s[order_ref[j]]).astype(BF16)

        @pl.when((i == 0) & (j == N_SLOT - 1))
        def _():
            keep.start()

        @pl.when((i == ni - 1) & (j == N_SLOT - 1))
        def _():
            for cp in sends + passed:
                cp.wait_send()
            keep.wait()

    first_pass = lambda j, i, o: (jnp.where(j == 0, i, ni - 1), 0)
    (z, n1, w_in_g), extra = _call(
        body, name="in_proj", grid=(N_SLOT, ni), prefetch=(order,),
        in_specs=[pl.BlockSpec((tm, D), first_pass),
                  pl.BlockSpec((1, D), lambda j, i, o: (0, 0)), ANY],
        out_specs=[pl.BlockSpec((tm, W_IN_COLS), lambda j, i, o: (i, o[j])),
                   pl.BlockSpec((D, tm), lambda j, i, o: (0, jnp.where(j == 0, i, ni - 1))), ANY],
        out_shape=[SDS((s, N_SLOT * W_IN_COLS), BF16), SDS((D, s), BF16), SDS((N_SLOT, D, W_IN_COLS), BF16)],
        scratch_shapes=[pltpu.VMEM((s, D), BF16), pltpu.VMEM((N_SLOT, D, W_IN_COLS), BF16),
                        pltpu.SemaphoreType.DMA((7,)), pltpu.SemaphoreType.DMA((7,)), pltpu.SemaphoreType.DMA((2,))],
        params=_cparams(("arbitrary", "arbitrary"), 56), args=(x, g_mix, w_in_own), comm=comm)
    return (z, n1, w_in_g), extra


def _decay(r, sp_lam):
    log_a = (-LRU_C) * r * sp_lam
    a = jnp.exp(log_a)
    return a, jnp.sqrt(-jnp.tanh(log_a) * (a * a + 1.0))


def _lru_gates(xc, xcb, wr_ref, br, wi_ref, bi, sp_lam, a_s, b_s, r_ref, i_ref):
    for h in range(HEADS):
        sl = slice(h * HEAD_DIM, (h + 1) * HEAD_DIM)
        r = jax.nn.sigmoid(_dot(xcb[:, sl], wr_ref[h]) + br[:, sl])
        ig = jax.nn.sigmoid(_dot(xcb[:, sl], wi_ref[h]) + bi[:, sl])
        a, mult = _decay(r, sp_lam[:, sl])
        a_s[:, sl] = a
        b_s[:, sl] = xc[:, sl] * ig * mult
        r_ref[:, sl] = r.astype(BF16)
        i_ref[:, sl] = ig.astype(BF16)


def _conv_fwd(xa, prev8, cw, cb):
    xc = cb + cw[0:1, :] * xa
    for k in range(1, CONV_K):
        xc = xc + cw[k:k + 1, :] * _rows_shifted(prev8, xa, k)
    return xc


def _branch_a_fwd(z, conv_w, conv_b, w_r, b_r, w_i, b_i, lam, comm=None):
    s = z.shape[0]
    ta = min(T_BRANCH_A, s)
    per16 = ta // 16

    def body(xa_ref, xp_ref, ga_ref, cw_ref, cb_ref, wr_ref, br_ref, wi_ref, bi_ref, lam_ref,
             ya_ref, hs_ref, xc_ref, r_ref, i_ref, a_s, b_s, h_s, carry_s):
        i = pl.program_id(0)

        @pl.when(i == 0)
        def _():
            carry_s[...] = jnp.zeros_like(carry_s)

        xa = xa_ref[...].astype(F32)
        prev8 = jnp.where(i > 0, xp_ref[...].astype(F32)[8:16], 0.0)
        xc = _conv_fwd(xa, prev8, cw_ref[...], cb_ref[...])
        xcb = xc.astype(BF16)
        xc_ref[...] = xcb
        sp_lam = _softplus(-lam_ref[...])
        _lru_gates(xc, xcb, wr_ref, br_ref[...], wi_ref, bi_ref[...], sp_lam, a_s, b_s, r_ref, i_ref)

        row = lax.broadcasted_iota(jnp.int32, (8, D), 0)

        def group(g, carry):
            off = pl.multiple_of(g * 8, 8)
            a8 = a_s[pl.ds(off, 8), :]
            b8 = b_s[pl.ds(off, 8), :]
            for d in (1, 2, 4):
                a_sh = jnp.where(row >= d, pltpu.roll(a8, d, 0), 1.0)
                b_sh = jnp.where(row >= d, pltpu.roll(b8, d, 0), 0.0)
                b8 = a8 * b_sh + b8
                a8 = a8 * a_sh
            h8 = b8 + a8 * carry
            h_s[pl.ds(off, 8), :] = h8
            return jnp.broadcast_to(h8[7:8, :], (8, D))

        carry_s[...] = lax.fori_loop(0, ta // 8, group, carry_s[...])
        hs = h_s[...]
        hs_ref[...] = hs.astype(BF16)
        ya_ref[...] = (hs * _gelu(ga_ref[...].astype(F32))).astype(BF16)

    vec = pl.BlockSpec((1, D), lambda i: (0, 0))
    gate = pl.BlockSpec((HEADS, HEAD_DIM, HEAD_DIM), lambda i: (0, 0, 0))
    return _call(
        body, name="branch_a_fwd", grid=(s // ta,),
        in_specs=[pl.BlockSpec((ta, D), lambda i: (i, 0)),
                  pl.BlockSpec((16, D), lambda i: (jnp.maximum(i * per16 - 1, 0), 0)),
                  pl.BlockSpec((ta, D), lambda i: (i, 1)),
                  pl.BlockSpec((CONV_K, D), lambda i: (0, 0)), vec, gate, vec, gate, vec, vec],
        out_specs=[pl.BlockSpec((ta, D), lambda i: (i, 0))] * 5,
        out_shape=[SDS((s, D), BF16)] * 5,
        scratch_shapes=[pltpu.VMEM((ta, D), F32), pltpu.VMEM((ta, D), F32), pltpu.VMEM((ta, D), F32),
                        pltpu.VMEM((8, D), F32)],
        params=_cparams(("arbitrary",), 40), args=(z, z, z, conv_w, conv_b, w_r, b_r, w_i, b_i, lam), comm=comm)


def _sgu_common(ub, vb, lg, lb, with_grad):
    if with_grad:
        u, du = _gelu_and_grad(ub)
        v, dv = _gelu_and_grad(vb)
    else:
        u, v, du, dv = _gelu(ub), _gelu(vb), None, None
    mu = jnp.mean(v, axis=-1, keepdims=True)
    vc = v - mu
    rstd = lax.rsqrt(jnp.mean(vc * vc, axis=-1, keepdims=True) + LN_EPS)
    vhat = vc * rstd
    vln = vhat * lg + lb
    return u, du, dv, rstd, vhat, vln


def _masked_ws(ws_ref):
    t = lax.broadcasted_iota(jnp.int32, (CHUNK, CHUNK), 0)
    c = lax.broadcasted_iota(jnp.int32, (CHUNK, CHUNK), 1)
    keep = c <= t
    return [jnp.where(keep, ws_ref[g], 0.0).astype(BF16) for g in range(GROUPS)]


def _branch_b_merge_out(ya, z, x, ln_g, ln_b, w_s, b_s_t, w_oa, w_ob, w_out, comm=None):
    s = x.shape[0]
    tm = min(TM_MERGE, s)
    assert tm % CHUNK == 0

    def body(ub_ref, vb_ref, lg_ref, lb_ref, ws_ref, bs_ref, ya_ref, ma_ref, mb_ref, x_ref, woa_ref, wob_ref, wo_ref,
             yb_ref, pa_ref, pb_ref, mg_ref, h1_ref):
        u, _, _, _, _, vln = _sgu_common(ub_ref[...].astype(F32), vb_ref[...].astype(F32),
                                         lg_ref[...], lb_ref[...], False)
        vlnb = vln.astype(BF16)
        wm = _masked_ws(ws_ref)
        bs = bs_ref[...]
        for c in range(tm // CHUNK):
            rs = slice(c * CHUNK, (c + 1) * CHUNK)
            for g in range(GROUPS):
                cs = slice(g * GROUP_DIM, (g + 1) * GROUP_DIM)
                sp = _dot(wm[g], vlnb[rs, cs]) + bs[:, g:g + 1]
                yb_ref[rs, cs] = (u[rs, cs] * sp).astype(BF16)

        pa = _dot(ya_ref[...], woa_ref[...])
        pb = _dot(yb_ref[...], wob_ref[...])
        merged = (jax.nn.sigmoid(ma_ref[...].astype(F32)) * pa
                  + jax.nn.sigmoid(mb_ref[...].astype(F32)) * pb).astype(BF16)
        pa_ref[...] = pa.astype(BF16)
        pb_ref[...] = pb.astype(BF16)
        mg_ref[...] = merged
        h1_ref[...] = x_ref[...] + _dot(merged, wo_ref[...])

    row = pl.BlockSpec((tm, D), lambda i: (i, 0))
    col = lambda q: pl.BlockSpec((tm, D), lambda i: (i, q))
    vec = pl.BlockSpec((1, D), lambda i: (0, 0))
    wsp = pl.BlockSpec((D, D), lambda i: (0, 0))
    return _call(
        body, name="branch_b_merge_out", grid=(s // tm,),
        in_specs=[col(2), col(3), vec, vec, pl.BlockSpec((GROUPS, CHUNK, CHUNK), lambda i: (0, 0, 0)),
                  pl.BlockSpec((CHUNK, GROUPS), lambda i: (0, 0)), row, col(4), col(5), row, wsp, wsp, wsp],
        out_specs=[row, row, row, row, row],
        out_shape=[SDS((s, D), BF16)] * 4 + [SDS((s, D), F32)], scratch_shapes=[],
        params=_cparams(("arbitrary",), 56),
        args=(z, z, ln_g, ln_b, w_s, b_s_t, ya, z, z, x, w_oa, w_ob, w_out), comm=comm)


def _mlp_fwd(h1, g_mlp, w_up_g, w_down, g_fin, tgt):
    s = h1.shape[0]
    tm = min(TM_ROWS, s)
    nj = N_SLOT

    steps = (s // tm) * nj

    def body(h1_ref, gm_ref, wu_hbm, wd_hbm, gf_ref, t_ref, r_ref, at_ref, n2t_ref, dh2_ref, dh2b_ref, loss_ref,
             dgf_ref, n2_s, acc_s, wu_s, wd_s, w_sems):
        i, j = pl.program_id(0), pl.program_id(1)
        q = i * nj + j

        def fetch(step):
            blk, slot = step % nj, step % 3
            return (pltpu.make_async_copy(wu_hbm.at[blk], wu_s.at[slot], w_sems.at[0, slot]),
                    pltpu.make_async_copy(wd_hbm.at[pl.ds(blk * FF_COLS, FF_COLS)], wd_s.at[slot], w_sems.at[1, slot]))

        @pl.when(q == 0)
        def _():
            for first in range(min(2, steps)):
                for cp in fetch(first):
                    cp.start()

        @pl.when(q + 2 < steps)
        def _():
            for cp in fetch(q + 2):
                cp.start()

        for cp in fetch(q):
            cp.wait()
        wu_ref, wd_ref = wu_s.at[q % 3], wd_s.at[q % 3]

        @pl.when(j == 0)
        def _():
            hv = h1_ref[...]
            rstd = lax.rsqrt(jnp.mean(hv * hv, axis=-1, keepdims=True) + NORM_EPS)
            nb = (hv * rstd * gm_ref[...]).astype(BF16)
            n2_s[...] = nb
            n2t_ref[...] = nb.T
            acc_s[...] = jnp.zeros_like(acc_s)

        @pl.when((i == 0) & (j == 0))
        def _():
            loss_ref[...] = jnp.zeros_like(loss_ref)
            dgf_ref[...] = jnp.zeros_like(dgf_ref)

        r = jnp.maximum(_dot(n2_s[...], wu_ref[...]), 0.0)
        r_ref[...] = r.astype(BF16)
        act = (r * r).astype(BF16)
        at_ref[...] = act.T
        acc_s[...] += _dot(act, wd_ref[...])

        @pl.when(j == nj - 1)
        def _():
            h2 = h1_ref[...] + acc_s[...]
            rstd = lax.rsqrt(jnp.mean(h2 * h2, axis=-1, keepdims=True) + NORM_EPS)
            hh = h2 * rstd
            gf = gf_ref[...]
            e = hh * gf - t_ref[...]
            loss_ref[...] += jnp.sum(e * e) * (0.5 / D)
            dy = e * (1.0 / D)
            dgf_ref[...] += jnp.sum(dy * hh, axis=0, keepdims=True)
            dhh = dy * gf
            dh2 = rstd * (dhh - hh * jnp.mean(dhh * hh, axis=-1, keepdims=True))
            dh2_ref[...] = dh2
            dh2b_ref[...] = dh2.astype(BF16)

    row = pl.BlockSpec((tm, D), lambda i, j: (i, 0))
    vec = pl.BlockSpec((1, D), lambda i, j: (0, 0))
    return pl.pallas_call(
        body, name="mlp_fwd", grid=(s // tm, nj),
        in_specs=[row, vec, ANY, ANY, vec, row],
        out_specs=[pl.BlockSpec((tm, FF_COLS), lambda i, j: (i, j)), pl.BlockSpec((FF_COLS, tm), lambda i, j: (j, i)),
                   pl.BlockSpec((D, tm), lambda i, j: (0, i)), row, row, pl.BlockSpec((8, 128), lambda i, j: (0, 0)),
                   vec],
        out_shape=[SDS((s, nj * FF_COLS), BF16), SDS((nj * FF_COLS, s), BF16), SDS((D, s), BF16), SDS((s, D), F32),
                   SDS((s, D), BF16), SDS((8, 128), F32), SDS((1, D), F32)],
        scratch_shapes=[pltpu.VMEM((tm, D), BF16), pltpu.VMEM((tm, D), F32), pltpu.VMEM((3, D, FF_COLS), BF16),
                        pltpu.VMEM((3, FF_COLS, D), BF16), pltpu.SemaphoreType.DMA((2, 3))],
        compiler_params=_cparams(("arbitrary", "arbitrary"), 56),
    )(h1, _small_in_hbm(g_mlp), w_up_g, w_down, _small_in_hbm(g_fin), tgt)


def _mlp_bwd(dh2, dh2b, r, w_down, w_up_g, h1, g_mlp, comm=None):
    s = h1.shape[0]
    tm = min(TM_ROWS, s)
    nj = N_SLOT

    def body(dh2_ref, dh2b_ref, r_ref, wd_ref, wu_ref, h1_ref, gm_ref, df_ref, dh1_ref, dgm_ref, acc_s):
        i, j = pl.program_id(0), pl.program_id(1)

        @pl.when(j == 0)
        def _():
            acc_s[...] = jnp.zeros_like(acc_s)

        @pl.when((i == 0) & (j == 0))
        def _():
            dgm_ref[...] = jnp.zeros_like(dgm_ref)

        d_act = _dot_nt(dh2b_ref[...], wd_ref[...])
        df = (d_act * (2.0 * r_ref[...].astype(F32))).astype(BF16)
        df_ref[...] = df
        acc_s[...] += _dot_nt(df, wu_ref[...])

        @pl.when(j == nj - 1)
        def _():
            hv = h1_ref[...]
            rstd = lax.rsqrt(jnp.mean(hv * hv, axis=-1, keepdims=True) + NORM_EPS)
            hh = hv * rstd
            dn2 = acc_s[...]
            dgm_ref[...] += jnp.sum(dn2 * hh, axis=0, keepdims=True)
            dhat = dn2 * gm_ref[...]
            dh1_ref[...] = dh2_ref[...] + rstd * (dhat - hh * jnp.mean(dhat * hh, axis=-1, keepdims=True))

    row = pl.BlockSpec((tm, D), lambda i, j: (i, 0))
    vec = pl.BlockSpec((1, D), lambda i, j: (0, 0))
    ffb = pl.BlockSpec((tm, FF_COLS), lambda i, j: (i, j))
    return _call(
        body, name="mlp_bwd", grid=(s // tm, nj),
        in_specs=[row, row, ffb, pl.BlockSpec((FF_COLS, D), lambda i, j: (j, 0)),
                  pl.BlockSpec((None, D, FF_COLS), lambda i, j: (j, 0, 0)), row, vec],
        out_specs=[ffb, row, vec],
        out_shape=[SDS((s, nj * FF_COLS), BF16), SDS((s, D), F32), SDS((1, D), F32)],
        scratch_shapes=[pltpu.VMEM((tm, D), F32)],
        params=_cparams(("arbitrary", "arbitrary"), 56), args=(dh2, dh2b, r, w_down, w_up_g, h1, g_mlp), comm=comm)


def _merge_bwd(dh1, z, pa, pb, w_out, w_oa, w_ob, comm=None):
    s = dh1.shape[0]
    tm = min(TM_MERGE, s)

    def body(dh1_ref, ma_ref, mb_ref, pa_ref, pb_ref, wo_ref, woa_ref, wob_ref,
             dz_ref, dpa_ref, dpb_ref, dya_ref, dyb_ref):
        dm = _dot_nt(dh1_ref[...].astype(BF16), wo_ref[...])
        sa = jax.nn.sigmoid(ma_ref[...].astype(F32))
        sb = jax.nn.sigmoid(mb_ref[...].astype(F32))
        dpa = (dm * sa).astype(BF16)
        dpb = (dm * sb).astype(BF16)
        dz_ref[:, 0:D] = (dm * pa_ref[...].astype(F32) * sa * (1.0 - sa)).astype(BF16)
        dz_ref[:, D:2 * D] = (dm * pb_ref[...].astype(F32) * sb * (1.0 - sb)).astype(BF16)
        dpa_ref[...] = dpa
        dpb_ref[...] = dpb
        dya_ref[...] = _dot_nt(dpa, woa_ref[...]).astype(BF16)
        dyb_ref[...] = _dot_nt(dpb, wob_ref[...]).astype(BF16)

    row = pl.BlockSpec((tm, D), lambda i: (i, 0))
    wsp = pl.BlockSpec((D, D), lambda i: (0, 0))
    return _call(
        body, name="merge_bwd", grid=(s // tm,),
        in_specs=[row, pl.BlockSpec((tm, D), lambda i: (i, 4)), pl.BlockSpec((tm, D), lambda i: (i, 5)),
                  row, row, wsp, wsp, wsp],
        out_specs=[pl.BlockSpec((tm, 2 * D), lambda i: (i, 2)), row, row, row, row],
        out_shape=[SDS((s, 6 * D), BF16)] + [SDS((s, D), BF16)] * 4, scratch_shapes=[],
        params=_cparams(("arbitrary",), 48), args=(dh1, z, z, pa, pb, w_out, w_oa, w_ob), comm=comm)


def _branch_b_bwd(dz, dyb, z, ln_g, ln_b, w_s, b_s_t, comm=None):
    s = z.shape[0]
    tb = min(T_BRANCH_B, s)

    def body(dz_in, dyb_ref, ub_ref, vb_ref, lg_ref, lb_ref, ws_ref, bs_ref,
             dz_ref, dws_ref, dbs_ref, dln_ref, du_s, dvln_s):
        del dz_in

        @pl.when(pl.program_id(0) == 0)
        def _():
            dws_ref[...] = jnp.zeros_like(dws_ref)
            dbs_ref[...] = jnp.zeros_like(dbs_ref)
            dln_ref[...] = jnp.zeros_like(dln_ref)

        lg = lg_ref[...]
        u, du, dv, rstd, vhat, vln = _sgu_common(ub_ref[...].astype(F32), vb_ref[...].astype(F32),
                                                 lg, lb_ref[...], True)
        vlnb = vln.astype(BF16)
        dyb_v = dyb_ref[...].astype(F32)
        wm = _masked_ws(ws_ref)
        keep = (lax.broadcasted_iota(jnp.int32, (CHUNK, CHUNK), 1)
                <= lax.broadcasted_iota(jnp.int32, (CHUNK, CHUNK), 0))
        bs = bs_ref[...]
        for c in range(tb // CHUNK):
            rs = slice(c * CHUNK, (c + 1) * CHUNK)
            for g in range(GROUPS):
                cs = slice(g * GROUP_DIM, (g + 1) * GROUP_DIM)
                v_blk = vlnb[rs, cs]
                sp = _dot(wm[g], v_blk) + bs[:, g:g + 1]
                d_sp = dyb_v[rs, cs] * u[rs, cs]
                d_spb = d_sp.astype(BF16)
                du_s[rs, cs] = dyb_v[rs, cs] * sp
                dvln_s[rs, cs] = _dot_tn(wm[g], d_spb)
                dws_ref[g] += jnp.where(keep, _dot_nt(d_spb, v_blk), 0.0)
                dbs_ref[g] += jnp.broadcast_to(jnp.sum(d_sp, axis=-1, keepdims=True), (CHUNK, CHUNK))
        dvln = dvln_s[...]
        dln_ref[0:1, :] += jnp.sum(dvln * vhat, axis=0, keepdims=True)
        dln_ref[1:2, :] += jnp.sum(dvln, axis=0, keepdims=True)
        dvh = dvln * lg
        d_v = rstd * (dvh - jnp.mean(dvh, axis=-1, keepdims=True)
                      - vhat * jnp.mean(dvh * vhat, axis=-1, keepdims=True))
        dz_ref[:, 0:D] = (du_s[...] * du).astype(BF16)
        dz_ref[:, D:2 * D] = (d_v * dv).astype(BF16)

    vec = pl.BlockSpec((1, D), lambda i: (0, 0))
    sq = pl.BlockSpec((GROUPS, CHUNK, CHUNK), lambda i: (0, 0, 0))
    return _call(
        body, name="branch_b_bwd", grid=(s // tb,),
        in_specs=[ANY, pl.BlockSpec((tb, D), lambda i: (i, 0)),
                  pl.BlockSpec((tb, D), lambda i: (i, 2)), pl.BlockSpec((tb, D), lambda i: (i, 3)), vec, vec, sq,
                  pl.BlockSpec((CHUNK, GROUPS), lambda i: (0, 0))],
        out_specs=[pl.BlockSpec((tb, 2 * D), lambda i: (i, 1)), sq, sq, pl.BlockSpec((8, D), lambda i: (0, 0))],
        out_shape=[SDS(dz.shape, BF16), SDS((GROUPS, CHUNK, CHUNK), F32), SDS((GROUPS, CHUNK, CHUNK), F32),
                   SDS((8, D), F32)],
        scratch_shapes=[pltpu.VMEM((tb, D), F32), pltpu.VMEM((tb, D), F32)], aliases={0: 0},
        params=_cparams(("arbitrary",), 40), args=(dz, dyb, z, z, ln_g, ln_b, w_s, b_s_t), comm=comm)


def _branch_a_bwd(dz, dya, z, hs, xc, r, ig, conv_w, w_r, w_i, lam, comm=None):
    s = z.shape[0]
    ta = min(T_BRANCH_A, s)
    nb = s // ta
    per16 = ta // 16

    def body(dz_in, dya_ref, xa_ref, ga_ref, hs_ref, hp_ref, xc_ref, r_ref, i_ref, cw_ref, wr_ref, wi_ref,
             lam_ref, dz_ref, vec_ref, dwr_ref, dwi_ref, a_s, b_s, h_s, dcar_s, acar_s, dxc_s):
        del dz_in
        i = pl.program_id(0)
        blk = nb - 1 - i

        @pl.when(i == 0)
        def _():
            dcar_s[...] = jnp.zeros_like(dcar_s)
            acar_s[...] = jnp.zeros_like(acar_s)
            dxc_s[...] = jnp.zeros_like(dxc_s)
            vec_ref[...] = jnp.zeros_like(vec_ref)
            dwr_ref[...] = jnp.zeros_like(dwr_ref)
            dwi_ref[...] = jnp.zeros_like(dwi_ref)

        cw = cw_ref[...]
        lam_v = lam_ref[...]
        xa = xa_ref[...].astype(F32)
        xcb = xc_ref[...]
        xc = xcb.astype(F32)
        sp_lam = _softplus(-lam_v)
        r_v, i_v = r_ref[...].astype(F32), i_ref[...].astype(F32)
        a_v, m_v = _decay(r_v, sp_lam)

        hs_v = hs_ref[...].astype(F32)
        hprev8 = jnp.where(blk > 0, hp_ref[...].astype(F32)[8:16], 0.0)
        h_m1 = _rows_shifted(hprev8, hs_v, 1)
        gg, dgg = _gelu_and_grad(ga_ref[...].astype(F32))
        dya_v = dya_ref[...].astype(F32)
        dz_ref[:, D:2 * D] = (dya_v * hs_v * dgg).astype(BF16)

        a_s[...] = _rows_advanced(a_v, acar_s[...], 1)
        b_s[...] = dya_v * gg

        row = lax.broadcasted_iota(jnp.int32, (8, D), 0)
        ng = ta // 8

        def group(gi, carry):
            off = pl.multiple_of((ng - 1 - gi) * 8, 8)
            c8 = a_s[pl.ds(off, 8), :]
            d8 = b_s[pl.ds(off, 8), :]
            for d in (1, 2, 4):
                c_sh = jnp.where(row < 8 - d, pltpu.roll(c8, 8 - d, 0), 1.0)
                d_sh = jnp.where(row < 8 - d, pltpu.roll(d8, 8 - d, 0), 0.0)
                d8 = c8 * d_sh + d8
                c8 = c8 * c_sh
            dh8 = d8 + c8 * carry
            h_s[pl.ds(off, 8), :] = dh8
            return jnp.broadcast_to(dh8[0:1, :], (8, D))

        dcar_s[...] = lax.fori_loop(0, ng, group, dcar_s[...])
        acar_s[...] = jnp.broadcast_to(a_v[0:1, :], (8, D))

        dbx = h_s[...]
        d_mult = dbx * xc * i_v
        d_loga = dbx * h_m1 * a_v - d_mult * (a_v * a_v) / m_v
        d_pr = d_loga * ((-LRU_C) * sp_lam) * r_v * (1.0 - r_v)
        d_pi = dbx * xc * m_v * i_v * (1.0 - i_v)
        vec_ref[7:8, :] += jnp.sum(d_loga * r_v, axis=0, keepdims=True) * (LRU_C * jax.nn.sigmoid(-lam_v))
        vec_ref[5:6, :] += jnp.sum(d_pr, axis=0, keepdims=True)
        vec_ref[6:7, :] += jnp.sum(d_pi, axis=0, keepdims=True)
        d_prb = d_pr.astype(BF16)
        d_pib = d_pi.astype(BF16)
        h_s[...] = dbx * i_v * m_v
        for h in range(HEADS):
            sl = slice(h * HEAD_DIM, (h + 1) * HEAD_DIM)
            h_s[:, sl] += _dot_nt(d_prb[:, sl], wr_ref[h]) + _dot_nt(d_pib[:, sl], wi_ref[h])
            dwr_ref[h] += _dot_tn(xcb[:, sl], d_prb[:, sl])
            dwi_ref[h] += _dot_tn(xcb[:, sl], d_pib[:, sl])
        d_xc = h_s[...]
        vec_ref[4:5, :] += jnp.sum(d_xc, axis=0, keepdims=True)
        vec_ref[0:1, :] += jnp.sum(d_xc * xa, axis=0, keepdims=True)
        d_xa = cw[0:1, :] * d_xc
        nxt = dxc_s[...]
        for k in range(1, CONV_K):
            ahead = _rows_advanced(d_xc, nxt, k)
            vec_ref[k:k + 1, :] += jnp.sum(ahead * xa, axis=0, keepdims=True)
            d_xa = d_xa + cw[k:k + 1, :] * ahead
        dz_ref[:, 0:D] = d_xa.astype(BF16)
        dxc_s[...] = d_xc[0:8, :]

    vec = pl.BlockSpec((1, D), lambda i: (0, 0))
    gate = pl.BlockSpec((HEADS, HEAD_DIM, HEAD_DIM), lambda i: (0, 0, 0))
    cur = lambda c: pl.BlockSpec((ta, D), lambda i: (nb - 1 - i, c))
    before = lambda c: pl.BlockSpec((16, D), lambda i: (jnp.maximum((nb - 1 - i) * per16 - 1, 0), c))
    return _call(
        body, name="branch_a_bwd", grid=(nb,),
        in_specs=[ANY, cur(0), cur(0), cur(1), cur(0), before(0), cur(0), cur(0), cur(0),
                  pl.BlockSpec((CONV_K, D), lambda i: (0, 0)), gate, gate, vec],
        out_specs=[pl.BlockSpec((ta, 2 * D), lambda i: (nb - 1 - i, 0)), pl.BlockSpec((8, D), lambda i: (0, 0)),
                   gate, gate],
        out_shape=[SDS(dz.shape, BF16), SDS((8, D), F32), SDS((HEADS, HEAD_DIM, HEAD_DIM), F32),
                   SDS((HEADS, HEAD_DIM, HEAD_DIM), F32)],
        scratch_shapes=[pltpu.VMEM((ta, D), F32)] * 3 + [pltpu.VMEM((8, D), F32)] * 3, aliases={0: 0},
        params=_cparams(("arbitrary",), 48),
        args=(dz, dya, z, z, hs, hs, xc, r, ig, conv_w, w_r, w_i, lam), comm=comm)


def _in_bwd(dz, w_in_g, x, dh1, g_mix, comm=None):
    s = x.shape[0]
    tm = min(TM_ROWS, s)
    nj = N_SLOT

    def body(dz_ref, w_ref, x_ref, dh1_ref, g_ref, dx_ref, dg_ref, acc_s):
        i, j = pl.program_id(0), pl.program_id(1)

        @pl.when(j == 0)
        def _():
            acc_s[...] = jnp.zeros_like(acc_s)

        @pl.when((i == 0) & (j == 0))
        def _():
            dg_ref[...] = jnp.zeros_like(dg_ref)

        acc_s[...] += _dot_nt(dz_ref[...], w_ref[...])

        @pl.when(j == nj - 1)
        def _():
            xv = x_ref[...]
            rstd = lax.rsqrt(jnp.mean(xv * xv, axis=-1, keepdims=True) + NORM_EPS)
            xh = xv * rstd
            dn = acc_s[...]
            dg_ref[...] += jnp.sum(dn * xh, axis=0, keepdims=True)
            dhat = dn * g_ref[...]
            dx_ref[...] = dh1_ref[...] + rstd * (dhat - xh * jnp.mean(dhat * xh, axis=-1, keepdims=True))

    row = pl.BlockSpec((tm, D), lambda i, j: (i, 0))
    vec = pl.BlockSpec((1, D), lambda i, j: (0, 0))
    return _call(
        body, name="in_bwd", grid=(s // tm, nj),
        in_specs=[pl.BlockSpec((tm, W_IN_COLS), lambda i, j: (i, j)),
                  pl.BlockSpec((None, D, W_IN_COLS), lambda i, j: (j, 0, 0)), row, row, vec],
        out_specs=[row, vec],
        out_shape=[SDS((s, D), F32), SDS((1, D), F32)],
        scratch_shapes=[pltpu.VMEM((tm, D), F32)],
        params=_cparams(("arbitrary", "arbitrary"), 48), args=(dz, w_in_g, x, dh1, g_mix), comm=comm)


def _wgrad(name, a, b):
    s = a.shape[0]
    ts = min(TM_ROWS, s)
    a_w, b_w = a.shape[1], b.shape[1]

    def body(a_ref, b_ref, o_ref, acc_s):
        t = pl.program_id(0)

        @pl.when(t == 0)
        def _():
            acc_s[...] = jnp.zeros_like(acc_s)

        acc_s[...] += _dot_tn(a_ref[...].astype(BF16), b_ref[...].astype(BF16))

        @pl.when(t == pl.num_programs(0) - 1)
        def _():
            o_ref[...] = acc_s[...].astype(BF16)

    return pl.pallas_call(
        body, name=name, grid=(s // ts,),
        in_specs=[pl.BlockSpec((ts, a_w), lambda t: (t, 0)), pl.BlockSpec((ts, b_w), lambda t: (t, 0))],
        out_specs=pl.BlockSpec((a_w, b_w), lambda t: (0, 0)),
        out_shape=SDS((a_w, b_w), BF16),
        scratch_shapes=[pltpu.VMEM((a_w, b_w), F32)],
        compiler_params=_cparams(("arbitrary",), 48),
    )(a, b)


def _place():
    x, y, c = lax.axis_index("x"), lax.axis_index("y"), lax.axis_index("c")
    return x, y, c


def _other_chips(x, y):
    return [(x, 1 - y, 2 * x + 1 - y), (1 - x, y, 2 * (1 - x) + y), (1 - x, 1 - y, 2 * (1 - x) + 1 - y)]


class _Plan:
    def __init__(self, arrays, out_shape, sems, start, finish, middle=None, middle_at=6):
        self.arrays, self.out_shape, self.sems, self.start, self.finish = arrays, out_shape, sems, start, finish
        self.middle, self.middle_at = middle, middle_at


def _gather_plan(shards, middle_at=6):
    n = len(shards)

    def copies(ins, outs, sems):
        send_sems, recv_sems, local_sems = sems
        x, y, c = _place()
        chip = 2 * x + y
        me = 2 * chip + c
        sib = (x, y, 1 - c)
        chips = _other_chips(x, y)

        def rc(k, t, src, blk, to):
            return pltpu.make_async_remote_copy(
                src_ref=src, dst_ref=outs[t].at[blk], send_sem=send_sems.at[k * n + t],
                recv_sem=recv_sems.at[k * n + t], device_id=to, device_id_type=MESH)

        (yx, yy, y_chip), (xx, xy, x_chip), _ = chips
        local = [pltpu.make_async_copy(ins[t], outs[t].at[me], local_sems.at[t]) for t in range(n)]
        sends = ([rc(0, t, ins[t], me, sib) for t in range(n)] + [rc(1, t, ins[t], me, (yx, yy, c)) for t in range(n)]
                 + [rc(2, t, ins[t], me, (xx, xy, c)) for t in range(n)])
        passed = [[rc(4 + j, t, outs[t].at[2 * pc + c], 2 * pc + c, sib) for t in range(n)]
                  for j, (_, _, pc) in enumerate(chips)]
        relays = [[rc(3, t, outs[t].at[2 * y_chip + c], 2 * y_chip + c, (xx, xy, c)) for t in range(n)],
                  [rc(3, t, outs[t].at[2 * x_chip + c], 2 * x_chip + c, (yx, yy, c)) for t in range(n)]]
        return rc, local, sends, passed, relays, chips, chip, c, sib

    def start(ins, outs, sems):
        _, local, sends, _, _, _, _, _, _ = copies(ins, outs, sems)
        for cp in local + sends:
            cp.start()

    def middle(ins, outs, sems):
        rc, _, _, passed, relays, chips, _, c, sib = copies(ins, outs, sems)
        for j in range(2):
            for t in range(n):
                rc(1 + j, t, ins[t], 2 * chips[j][2] + c, sib).wait_recv()
        for j in range(2):
            for cp in passed[j]:
                cp.start()

            @pl.when(c == j)
            def _():
                for cp in relays[j]:
                    cp.start()

    def finish(ins, outs, sems):
        rc, local, sends, passed, relays, chips, chip, c, sib = copies(ins, outs, sems)
        far = 2 * chips[2][2] + c
        for t in range(n):
            rc(3, t, ins[t], far, sib).wait_recv()
        for cp in passed[2]:
            cp.start()
        for t in range(n):
            rc(0, t, ins[t], 2 * chip + 1 - c, sib).wait_recv()
        for j, (px, py, pc) in enumerate(chips):
            for t in range(n):
                rc(4 + j, t, ins[t], 2 * pc + 1 - c, sib).wait_recv()
        for cp in sends + passed[0] + passed[1] + passed[2]:
            cp.wait_send()
        for j in range(2):
            @pl.when(c == j)
            def _():
                for cp in relays[j]:
                    cp.wait_send()
        for cp in local:
            cp.wait()

    return _Plan(list(shards), [SDS((N_SLOT,) + tuple(a.shape), a.dtype) for a in shards],
                 [pltpu.SemaphoreType.DMA((7 * n,)), pltpu.SemaphoreType.DMA((7 * n,)),
                  pltpu.SemaphoreType.DMA((n,))], start, finish, middle, middle_at)


def _sibling_plan(grads, whole=()):
    n, m = len(grads), len(whole)

    def copies(ins, outs, sems):
        send_sems, recv_sems = sems
        x, y, c = _place()
        sib = (x, y, 1 - c)

        def rc(t, src, dst):
            return pltpu.make_async_remote_copy(src_ref=src, dst_ref=dst, send_sem=send_sems.at[t],
                                                recv_sem=recv_sems.at[t], device_id=sib, device_id_type=MESH)
        return rc, c

    def start(ins, outs, sems):
        rc, c = copies(ins, outs, sems)
        for t in range(n):
            for j in range(4):
                rc(t, ins[t].at[2 * j + 1 - c], outs[t].at[j]).start()
        for t in range(n, n + m):
            rc(t, ins[t], outs[t]).start()

    def finish(ins, outs, sems):
        rc, _ = copies(ins, outs, sems)
        for t in range(n):
            rc(t, ins[t].at[pl.ds(0, 4)], outs[t]).wait()
        for t in range(n, n + m):
            rc(t, ins[t], outs[t]).wait()

    return _Plan(list(grads) + list(whole),
                 [SDS((4,) + tuple(g.shape[1:]), g.dtype) for g in grads] + [SDS(a.shape, a.dtype) for a in whole],
                 [pltpu.SemaphoreType.DMA((n + m,)), pltpu.SemaphoreType.DMA((n + m,))], start, finish)


def _chips_plan(parts, whole=()):
    n, m = len(parts), len(whole)

    def src_of(ins, t, pc):
        return ins[t].at[pc] if t < n else ins[t]

    def local_copies(ins, outs, sems, chip):
        return [pltpu.make_async_copy(src_of(ins, t, chip), outs[t].at[chip], sems[2].at[t]) for t in range(n + m)]

    def start(ins, outs, sems):
        send_sems, recv_sems, _ = sems
        x, y, c = _place()
        chip = 2 * x + y
        for cp in local_copies(ins, outs, sems, chip):
            cp.start()
        for px, py, pc in _other_chips(x, y):
            for t in range(n + m):
                pltpu.make_async_remote_copy(src_ref=src_of(ins, t, pc), dst_ref=outs[t].at[chip],
                                             send_sem=send_sems.at[t], recv_sem=recv_sems.at[t],
                                             device_id=(px, py, c), device_id_type=MESH).start()

    def finish(ins, outs, sems):
        send_sems, recv_sems, _ = sems
        x, y, c = _place()
        for t in range(n + m):
            three = outs[t].at[pl.ds(0, 3)]
            pltpu.make_async_remote_copy(src_ref=three, dst_ref=three, send_sem=send_sems.at[t],
                                         recv_sem=recv_sems.at[t], device_id=(x, y, c), device_id_type=MESH).wait()
        for cp in local_copies(ins, outs, sems, 2 * x + y):
            cp.wait()

    return _Plan(list(parts) + list(whole),
                 [SDS(p.shape, p.dtype) for p in parts] + [SDS((4,) + tuple(a.shape), a.dtype) for a in whole],
                 [pltpu.SemaphoreType.DMA((n + m,)), pltpu.SemaphoreType.DMA((n + m,)),
                  pltpu.SemaphoreType.DMA((n + m,))], start, finish)


def _exchange_plan(arr):
    def peers(x, y, c):
        flip = lambda v, f: 1 - v if f else v
        return [(flip(x, fx), flip(y, fy), flip(c, fc))
                for fx in (0, 1) for fy in (0, 1) for fc in (0, 1) if fx or fy or fc]

    def start(ins, outs, sems):
        x, y, c = _place()
        me = 4 * x + 2 * y + c
        pltpu.make_async_copy(ins[0], outs[0].at[me], sems[2].at[0]).start()
        for to in peers(x, y, c):
            pltpu.make_async_remote_copy(src_ref=ins[0], dst_ref=outs[0].at[me], send_sem=sems[0].at[0],
                                         recv_sem=sems[1].at[0], device_id=to, device_id_type=MESH).start()

    def finish(ins, outs, sems):
        x, y, c = _place()
        seven = outs[0].at[pl.ds(0, 7)]
        pltpu.make_async_remote_copy(src_ref=seven, dst_ref=seven, send_sem=sems[0].at[0], recv_sem=sems[1].at[0],
                                     device_id=(x, y, c), device_id_type=MESH).wait()
        pltpu.make_async_copy(ins[0], outs[0].at[4 * x + 2 * y + c], sems[2].at[0]).wait()

    return _Plan([arr], [SDS((N_SLOT,) + tuple(arr.shape), arr.dtype)],
                 [pltpu.SemaphoreType.DMA((1,)), pltpu.SemaphoreType.DMA((1,)), pltpu.SemaphoreType.DMA((1,))],
                 start, finish)


def _join(*plans):
    def cut(seq, sizes):
        out, at = [], 0
        for k in sizes:
            out.append(seq[at:at + k])
            at += k
        return out

    n_arr = [len(p.arrays) for p in plans]
    n_sem = [len(p.sems) for p in plans]

    def start(ins, outs, sems):
        for p, i, o, s in zip(plans, cut(ins, n_arr), cut(outs, n_arr), cut(sems, n_sem)):
            p.start(i, o, s)

    def finish(ins, outs, sems):
        for p, i, o, s in zip(plans, cut(ins, n_arr), cut(outs, n_arr), cut(sems, n_sem)):
            p.finish(i, o, s)

    def middle(ins, outs, sems):
        for p, i, o, s in zip(plans, cut(ins, n_arr), cut(outs, n_arr), cut(sems, n_sem)):
            if p.middle is not None:
                p.middle(i, o, s)

    return _Plan([a for p in plans for a in p.arrays], [o for p in plans for o in p.out_shape],
                 [s for p in plans for s in p.sems], start, finish,
                 middle if any(p.middle is not None for p in plans) else None,
                 max(p.middle_at for p in plans))


def _run_plan(name, plan):
    k = len(plan.arrays)

    def body(*refs):
        ins, outs, sems = refs[:k], refs[k:2 * k], refs[2 * k:]
        plan.start(ins, outs, sems)
        if plan.middle is not None:
            plan.middle(ins, outs, sems)
        plan.finish(ins, outs, sems)

    return pl.pallas_call(
        body, name=name, in_specs=[ANY] * k, out_specs=[ANY] * k, out_shape=plan.out_shape,
        scratch_shapes=plan.sems, compiler_params=pltpu.CompilerParams(has_side_effects=True),
    )(*plan.arrays)


def _call(body, *, name, grid, in_specs, out_specs, out_shape, scratch_shapes, params, args, comm=None,
          aliases=None, prefetch=()):
    aliases = aliases or {}
    n_pre = len(prefetch)

    def launch(fn, ins_specs, outs_specs, outs_shape, scratch, operands):
        spec = pltpu.PrefetchScalarGridSpec(num_scalar_prefetch=n_pre, grid=grid, in_specs=ins_specs,
                                            out_specs=outs_specs, scratch_shapes=scratch)
        return pl.pallas_call(fn, name=name, grid_spec=spec, out_shape=outs_shape, compiler_params=params,
                              input_output_aliases=aliases)(*prefetch, *[_small_in_hbm(a) for a in operands])

    if comm is None:
        return list(launch(body, in_specs, out_specs, out_shape, scratch_shapes, args)), []
    n_in, n_out, n_scr, k = len(in_specs), len(out_specs), len(scratch_shapes), len(comm.arrays)

    def wrapped(*refs):
        pre, refs = refs[:n_pre], refs[n_pre:]
        ins = refs[:n_in]
        c_in = refs[n_in:n_in + k]
        outs = refs[n_in + k:n_in + k + n_out]
        c_out = refs[n_in + k + n_out:n_in + 2 * k + n_out]
        scr = refs[n_in + 2 * k + n_out:n_in + 2 * k + n_out + n_scr]
        sems = refs[n_in + 2 * k + n_out + n_scr:]
        step, steps = pl.program_id(0), grid[0]
        for d in range(1, len(grid)):
            step, steps = step * grid[d] + pl.program_id(d), steps * grid[d]

        @pl.when(step == 0)
        def _():
            comm.start(c_in, c_out, sems)

        if comm.middle is not None:
            @pl.when(step == (comm.middle_at * steps) // 8)
            def _():
                comm.middle(c_in, c_out, sems)

        body(*pre, *ins, *outs, *scr)

        @pl.when(step == steps - 1)
        def _():
            comm.finish(c_in, c_out, sems)

    res = launch(wrapped, list(in_specs) + [ANY] * k, list(out_specs) + [ANY] * k,
                 list(out_shape) + list(comm.out_shape), list(scratch_shapes) + list(comm.sems),
                 tuple(args) + tuple(comm.arrays))
    return list(res[:n_out]), list(res[n_out:])


def _wgrad_paired(name, a_t, b, core, by_rows=False, comm=None):
    s = b.shape[0]
    m = a_t.shape[0] // N_SLOT if by_rows else a_t.shape[0]
    cols = b.shape[1] if by_rows else b.shape[1] // N_SLOT
    half = N_SLOT // 2

    def owner(k, c):
        return 2 * (k % half) + jnp.where(k < half, 1 - c, c)

    def body(c_ref, a_ref, b_ref, sum_ref, out_s, recv_s, send_sems, recv_sems):
        del c_ref
        k = pl.program_id(0)
        x, y, c = _place()

        def to_sibling(j):
            return pltpu.make_async_remote_copy(src_ref=out_s.at[j % 2], dst_ref=recv_s.at[j],
                                                send_sem=send_sems.at[j], recv_sem=recv_sems.at[j],
                                                device_id=(x, y, 1 - c), device_id_type=MESH)

        @pl.when((k >= 2) & (k < half + 2))
        def _():
            to_sibling(k - 2).wait_send()

        @pl.when(k < half)
        def _():
            out_s[k % 2] = _dot(a_ref[...], b_ref[...]).astype(BF16)
            to_sibling(k).start()

        @pl.when(k >= half)
        def _():
            to_sibling(k - half).wait_recv()
            sum_ref[...] = (_dot(a_ref[...], b_ref[...]) + recv_s[k - half].astype(F32)).astype(BF16)

    if by_rows:
        in_specs = [pl.BlockSpec((m, s), lambda k, c_ref: (owner(k, c_ref[0]), 0)),
                    pl.BlockSpec((s, cols), lambda k, c_ref: (0, 0))]
    else:
        in_specs = [pl.BlockSpec((m, s), lambda k, c_ref: (0, 0)),
                    pl.BlockSpec((s, cols), lambda k, c_ref: (0, owner(k, c_ref[0])))]
    return _call(body, name=name, grid=(N_SLOT,), in_specs=in_specs,
                 out_specs=[pl.BlockSpec((None, m, cols), lambda k, c_ref: (jnp.maximum(k - half, 0), 0, 0))],
                 out_shape=[SDS((half, m, cols), BF16)],
                 scratch_shapes=[pltpu.VMEM((2, m, cols), BF16), pltpu.VMEM((half, m, cols), BF16),
                                 pltpu.SemaphoreType.DMA((half,)), pltpu.SemaphoreType.DMA((half,))],
                 params=_cparams(("arbitrary",), 56), args=(a_t, b), comm=comm, prefetch=(core,))


def _row_tile(rows):
    for t in (512, 256, 128, 64, 32, 16, 8):
        if rows % t == 0:
            return t
    return rows


def _pair_sum(name, g8, recv4, core):
    _, rows, cols = recv4.shape
    tr = _row_tile(rows)
    g42 = g8.reshape(4, 2, rows, cols)

    def body(c_ref, g_ref, r_ref, o_ref):
        del c_ref
        o_ref[...] = (g_ref[...].astype(F32) + r_ref[...].astype(F32)).astype(o_ref.dtype)

    return pl.pallas_call(
        body, name=name,
        grid_spec=pltpu.PrefetchScalarGridSpec(
            num_scalar_prefetch=1, grid=(4, rows // tr),
            in_specs=[pl.BlockSpec((None, None, tr, cols), lambda j, i, c_ref: (j, c_ref[0], i, 0)),
                      pl.BlockSpec((None, tr, cols), lambda j, i, c_ref: (j, i, 0))],
            out_specs=pl.BlockSpec((None, tr, cols), lambda j, i, c_ref: (j, i, 0))),
        out_shape=SDS(recv4.shape, g8.dtype),
        compiler_params=_cparams(("arbitrary", "arbitrary"), 32),
    )(core, g42, recv4)


def _add2(name, a, b):
    rows, cols = a.shape
    tr = _row_tile(rows)

    def body(a_ref, b_ref, o_ref):
        o_ref[...] = a_ref[...] + b_ref[...]

    blk = pl.BlockSpec((tr, cols), lambda i: (i, 0))
    return pl.pallas_call(body, name=name, grid=(rows // tr,), in_specs=[blk, blk], out_specs=blk,
                          out_shape=SDS(a.shape, a.dtype),
                          compiler_params=_cparams(("arbitrary",), 32))(a, b)


def _sum_terms(name, terms):
    k, rows, cols = terms.shape
    tr = _row_tile(rows)

    def body(r_ref, o_ref):
        acc = r_ref[0]
        for q in range(1, k):
            acc = acc + r_ref[q]
        o_ref[...] = acc

    return pl.pallas_call(body, name=name, grid=(rows // tr,),
                          in_specs=[pl.BlockSpec((k, tr, cols), lambda i: (0, i, 0))],
                          out_specs=pl.BlockSpec((tr, cols), lambda i: (i, 0)),
                          out_shape=SDS((rows, cols), terms.dtype),
                          compiler_params=_cparams(("arbitrary",), 32))(terms)


def _adam_update(g, w, m, v):
    c1 = 1.0 / (1.0 - ADAM_B1 ** ADAM_STEP)
    c2 = 1.0 / (1.0 - ADAM_B2 ** ADAM_STEP)
    mn = ADAM_B1 * m + (1.0 - ADAM_B1) * g
    vn = ADAM_B2 * v + (1.0 - ADAM_B2) * (g * g)
    delta = (-ADAM_LR) * ((mn * c1) / (jnp.sqrt(vn * c2) + ADAM_EPS) + ADAM_WD * w)
    return delta, mn, vn


def _adamw_many(name, gs, ws, ms, vs):
    n = len(gs)

    def body(*refs):
        for p in range(n):
            g, w, m, v = (refs[q * n + p][...] for q in range(4))
            d, mn, vn = _adam_update(g, w, m, v)
            refs[4 * n + p][...] = d
            refs[5 * n + p][...] = mn
            refs[6 * n + p][...] = vn

    full = [pl.BlockSpec(w.shape, lambda i: (0, 0)) for w in ws]
    shapes = [SDS(w.shape, F32) for w in ws]
    res = pl.pallas_call(body, name=name, grid=(1,), in_specs=full * 4, out_specs=full * 3, out_shape=shapes * 3,
                         compiler_params=_cparams(("arbitrary",), 32),
                         )(*[_small_in_hbm(a) for a in (*gs, *ws, *ms, *vs)])
    return [(res[p], res[n + p], res[2 * n + p]) for p in range(n)]


def _adamw(name, terms, w, m, v):
    k, rows, cols = terms.shape
    tr = _row_tile(rows)

    def body(t_ref, w_ref, m_ref, v_ref, g_ref, d_ref, mo_ref, vo_ref):
        g = t_ref[0].astype(F32)
        for q in range(1, k):
            g = g + t_ref[q].astype(F32)
        g_ref[...] = g
        d_ref[...], mo_ref[...], vo_ref[...] = _adam_update(g, w_ref[...], m_ref[...], v_ref[...])

    blk = pl.BlockSpec((tr, cols), lambda i: (i, 0))
    return pl.pallas_call(body, name=name, grid=(rows // tr,),
                          in_specs=[pl.BlockSpec((k, tr, cols), lambda i: (0, i, 0)), blk, blk, blk],
                          out_specs=[blk] * 4, out_shape=[SDS((rows, cols), F32)] * 4,
                          compiler_params=_cparams(("arbitrary",), 40),
                          )(*[pltpu.with_memory_space_constraint(a, pltpu.HBM) for a in (terms, w, m, v)])


def kernel(x, norm_mix_g, w_in, conv_w, conv_b, w_rgate, b_rgate, w_igate, b_igate, lru_lambda, w_out_a, sgu_ln_g, sgu_ln_b, sgu_w_s, sgu_b_s, w_out_b, w_out, norm_mlp_g, w_up, w_down, norm_final_g, loss_target, m_norm_mix_g, m_w_in, m_conv_w, m_conv_b, m_w_rgate, m_b_rgate, m_w_igate, m_b_igate, m_lru_lambda, m_w_out_a, m_sgu_ln_g, m_sgu_ln_b, m_sgu_w_s, m_sgu_b_s, m_w_out_b, m_w_out, m_norm_mlp_g, m_w_up, m_w_down, m_norm_final_g, v_norm_mix_g, v_w_in, v_conv_w, v_conv_b, v_w_rgate, v_b_rgate, v_w_igate, v_b_igate, v_lru_lambda, v_w_out_a, v_sgu_ln_g, v_sgu_ln_b, v_sgu_w_s, v_sgu_b_s, v_w_out_b, v_w_out, v_norm_mlp_g, v_w_up, v_w_down, v_norm_final_g):
    cx, cy, cc = _place()
    me = 4 * cx + 2 * cy + cc
    core = jnp.reshape(cc, (1,)).astype(jnp.int32)
    xs = x[0]
    tgt = loss_target[0]

    gate_shard = jnp.stack([w_rgate[0], w_igate[0]]).astype(BF16).reshape(2 * HEADS * 32, HEAD_DIM)
    vec_shard = jnp.concatenate([conv_w[0], b_rgate[0], b_igate[0]], axis=1)
    vec_shard = jnp.pad(vec_shard, ((0, 4), (0, 256 - vec_shard.shape[1])))
    shards = [w_in[0].astype(BF16), w_out_a[0].astype(BF16), w_out_b[0].astype(BF16), w_out[0].astype(BF16),
              w_up[0].astype(BF16), w_down[0].astype(BF16), gate_shard, vec_shard]
    (z, n1_t, w_in_g), (gate_g, vec_g) = _in_proj(xs, norm_mix_g, shards[0], _slot_order(cx, cy, cc),
                                                comm=_gather_plan(shards[6:8]))
    gates = gate_g.reshape(N_SLOT, 2, HEADS, 32, HEAD_DIM).transpose(1, 2, 0, 3, 4).reshape(2, HEADS, HEAD_DIM, HEAD_DIM)
    w_r_f, w_i_f = gates[0], gates[1]
    conv_w_f = vec_g[:, 0:4, 0:128].transpose(1, 0, 2).reshape(CONV_K, D)
    b_r_f = vec_g[:, 0:4, 128:160].transpose(1, 0, 2).reshape(1, D)
    b_i_f = vec_g[:, 0:4, 160:192].transpose(1, 0, 2).reshape(1, D)
    b_s_t = jnp.transpose(sgu_b_s[0])

    (ya, hs, xc, r_gate, i_gate), (w_oa_g, w_ob_g, w_out_g, w_up_g) = _branch_a_fwd(
        z, conv_w_f, conv_b, w_r_f, b_r_f, w_i_f, b_i_f, lru_lambda, comm=_gather_plan(shards[1:5], middle_at=5))
    w_oa_f = w_oa_g.reshape(D, D)
    w_ob_f = w_ob_g.reshape(D, D)
    w_out_f = w_out_g.reshape(D, D)
    (yb, pa, pb, merged, h1), (w_down_g,) = _branch_b_merge_out(
        ya, z, xs, sgu_ln_g, sgu_ln_b, sgu_w_s[0], b_s_t, w_oa_f, w_ob_f, w_out_f,
        comm=_gather_plan(shards[5:6], middle_at=4))
    w_down_f = w_down_g.reshape(N_SLOT * FF_COLS, D)
    r_act, act_t, n2_t, dh2, dh2b, loss_acc, d_gfin = _mlp_fwd(h1, norm_mlp_g, w_up_g, w_down_f,
                                                               norm_final_g.reshape(1, D), tgt)

    def pair(names, grads, recv):
        return [_pair_sum("pair_sum_" + nm, g, r, core) for nm, g, r in zip(names, grads, recv)]

    (p_down,), _ = _wgrad_paired("wgrad_down", act_t, dh2b, core, by_rows=True)
    (df, dh1, d_gmlp), (got_down,) = _mlp_bwd(dh2, dh2b, r_act, w_down_f, w_up_g, h1, norm_mlp_g,
                                              comm=_chips_plan([p_down]))
    (p_up,), _ = _wgrad_paired("wgrad_up", n2_t, df, core)
    g_out = _wgrad("wgrad_out", merged, dh1).reshape(N_SLOT, D // N_SLOT, D)
    (dz, dpa, dpb, dya, dyb), (r_out,) = _merge_bwd(
        dh1, z, pa, pb, w_out_f, w_oa_f, w_ob_f, comm=_sibling_plan([g_out]))
    (p_out,) = pair(["out"], [g_out], [r_out])
    g_oa = _wgrad("wgrad_out_a", ya, dpa).reshape(N_SLOT, D // N_SLOT, D)
    g_ob = _wgrad("wgrad_out_b", yb, dpb).reshape(N_SLOT, D // N_SLOT, D)
    (dz, d_ws, d_bs, d_ln), (got_out, r_oa, r_ob) = _branch_b_bwd(
        dz, dyb, z, sgu_ln_g, sgu_ln_b, sgu_w_s[0], b_s_t,
        comm=_join(_chips_plan([p_out]), _sibling_plan([g_oa, g_ob])))
    p_oa, p_ob = pair(["out_a", "out_b"], [g_oa, g_ob], [r_oa, r_ob])
    (dz, d_vec, d_wr, d_wi), (got_up, got_oa, got_ob) = _branch_a_bwd(
        dz, dya, z, hs, xc, r_gate, i_gate, conv_w_f, w_r_f, w_i_f, lru_lambda,
        comm=_chips_plan([p_up, p_oa, p_ob]))
    g_gate = jnp.stack([d_wr, d_wi]).reshape(2, HEADS, N_SLOT, 32, HEAD_DIM).transpose(2, 0, 1, 3, 4)
    g_gate = g_gate.reshape(N_SLOT, 2 * HEADS * 32, HEAD_DIM).astype(BF16)

    d_bs_row = jnp.pad(d_bs[:, :, 0].reshape(1, GROUPS * CHUNK), ((0, 0), (0, D - GROUPS * CHUNK)))
    vecs = jnp.concatenate([d_vec, jnp.concatenate([d_ln[0:2], d_gmlp, d_gfin, d_bs_row, jnp.zeros((3, D), F32)])])
    d_ws2 = d_ws.reshape(GROUPS * CHUNK, CHUNK)
    (p_in,), (r_gate, r_vecs, r_ws) = _wgrad_paired("wgrad_in", n1_t, dz, core,
                                                    comm=_sibling_plan([g_gate], [vecs, d_ws2]))
    (p_gate,) = pair(["gate"], [g_gate], [r_gate])
    vecs_chip = _add2("pair_sum_vecs", vecs, r_vecs)
    ws_chip = _add2("pair_sum_ws", d_ws2, r_ws)
    (dx, d_gmix), (got_in, got_gate, got_vecs, got_ws) = _in_bwd(
        dz, w_in_g, xs, dh1, norm_mix_g, comm=_chips_plan([p_in, p_gate], [vecs_chip, ws_chip]))
    vecs_sum = _sum_terms("sum_vecs", got_vecs)
    last = jnp.concatenate([d_gmix, jnp.pad(loss_acc[0:1], ((0, 0), (0, D - 128))), jnp.zeros((6, D), F32)])
    (last_all,) = _run_plan("exchange_last", _exchange_plan(last))
    last_sum = _sum_terms("sum_last", last_all)
    loss = last_sum[1, 0]
    got = [got_in, got_oa, got_ob, got_out, got_up, got_down, got_gate]

    def step(nm, terms, w, m, v, rows, cols):
        g, d, mn, vn = _adamw("adamw_" + nm, terms.reshape(4, rows, cols), w.reshape(rows, cols),
                              m.reshape(rows, cols), v.reshape(rows, cols))
        return [a.reshape(w.shape) for a in (g, d, mn, vn)]

    o_in = step("in", got[0], w_in, m_w_in, v_w_in, D, W_IN_COLS)
    o_oa = step("out_a", got[1], w_out_a, m_w_out_a, v_w_out_a, D // N_SLOT, D)
    o_ob = step("out_b", got[2], w_out_b, m_w_out_b, v_w_out_b, D // N_SLOT, D)
    o_out = step("out", got[3], w_out, m_w_out, v_w_out, D // N_SLOT, D)
    o_up = step("up", got[4], w_up, m_w_up, v_w_up, D, FF_COLS)
    o_down = step("down", got[5], w_down, m_w_down, v_w_down, FF_COLS, D)
    gate_w = jnp.stack([w_rgate[0], w_igate[0]]).reshape(2 * HEADS * 32, HEAD_DIM)
    gate_m = jnp.stack([m_w_rgate[0], m_w_igate[0]]).reshape(2 * HEADS * 32, HEAD_DIM)
    gate_v = jnp.stack([v_w_rgate[0], v_w_igate[0]]).reshape(2 * HEADS * 32, HEAD_DIM)
    o_gate = _adamw("adamw_gate", got[6], gate_w, gate_m, gate_v)
    o_gate = [a.reshape(2, 1, HEADS, 32, HEAD_DIM) for a in o_gate]
    o_wr = [a[0] for a in o_gate]
    o_wi = [a[1] for a in o_gate]

    def own(full, width):
        return lax.dynamic_slice_in_dim(full, me * width, width, axis=1)

    small_g = {
        "norm_mix_g": last_sum[0:1], "conv_w": own(vecs_sum[0:4], 128), "conv_b": vecs_sum[4:5],
        "b_rgate": own(vecs_sum[5:6].reshape(HEADS, HEAD_DIM), 32),
        "b_igate": own(vecs_sum[6:7].reshape(HEADS, HEAD_DIM), 32),
        "lru_lambda": vecs_sum[7:8], "sgu_ln_g": vecs_sum[8:9], "sgu_ln_b": vecs_sum[9:10],
        "norm_mlp_g": vecs_sum[10:11], "norm_final_g": vecs_sum[11:12],
        "sgu_b_s": vecs_sum[12, 0:GROUPS * CHUNK].reshape(GROUPS, CHUNK),
    }
    small_w = {"norm_mix_g": (norm_mix_g, m_norm_mix_g, v_norm_mix_g), "conv_w": (conv_w, m_conv_w, v_conv_w),
               "conv_b": (conv_b, m_conv_b, v_conv_b), "b_rgate": (b_rgate, m_b_rgate, v_b_rgate),
               "b_igate": (b_igate, m_b_igate, v_b_igate), "lru_lambda": (lru_lambda, m_lru_lambda, v_lru_lambda),
               "sgu_ln_g": (sgu_ln_g, m_sgu_ln_g, v_sgu_ln_g), "sgu_ln_b": (sgu_ln_b, m_sgu_ln_b, v_sgu_ln_b),
               "norm_mlp_g": (norm_mlp_g, m_norm_mlp_g, v_norm_mlp_g),
               "norm_final_g": (norm_final_g, m_norm_final_g, v_norm_final_g),
               "sgu_b_s": (sgu_b_s, m_sgu_b_s, v_sgu_b_s), "sgu_w_s": (sgu_w_s, m_sgu_w_s, v_sgu_w_s)}
    order = list(small_g)
    as2d = lambda k, a: a.reshape(small_g[k].shape)
    upd = _adamw_many("adamw_small", [small_g[k] for k in order], *[[as2d(k, small_w[k][q]) for k in order]
                                                                     for q in range(3)])
    o_small = {k: [a.reshape(small_w[k][0].shape) for a in (small_g[k],) + u] for k, u in zip(order, upd)}
    ws3 = [a[0].reshape(GROUPS * CHUNK, CHUNK) for a in small_w.pop("sgu_w_s")]
    o_small["sgu_w_s"] = [a.reshape(sgu_w_s.shape) for a in _adamw("adamw_ws", got_ws, *ws3)]

    per_weight = {"norm_mix_g": o_small["norm_mix_g"], "w_in": o_in, "conv_w": o_small["conv_w"],
                  "conv_b": o_small["conv_b"], "w_rgate": o_wr, "b_rgate": o_small["b_rgate"], "w_igate": o_wi,
                  "b_igate": o_small["b_igate"], "lru_lambda": o_small["lru_lambda"], "w_out_a": o_oa,
                  "sgu_ln_g": o_small["sgu_ln_g"], "sgu_ln_b": o_small["sgu_ln_b"], "sgu_w_s": o_small["sgu_w_s"],
                  "sgu_b_s": o_small["sgu_b_s"], "w_out_b": o_ob, "w_out": o_out, "norm_mlp_g": o_small["norm_mlp_g"],
                  "w_up": o_up, "w_down": o_down, "norm_final_g": o_small["norm_final_g"]}
    names_w = list(per_weight)
    return (loss, dx[None], *[per_weight[k][0] for k in names_w], *[per_weight[k][1] for k in names_w],
            *[per_weight[k][2] for k in names_w], *[per_weight[k][3] for k in names_w])
```
